```python
import jax, jax.numpy as jnp
from jax import lax
import numpy as np

D_MODEL = 1024
BATCH = 8
SEQ = 4096
DEPTH = 1

D_RNN = D_MODEL
N_RNN_BLOCKS = 4
RNN_BLOCK = D_RNN // N_RNN_BLOCKS
CONV_WIDTH = 4
C_RG = 8.0
MIN_RAD, MAX_RAD = 0.9, 0.999
D_POOL = D_MODEL
POOL_WINDOWS = (2, 4, 8, 16)
N_POOL_GROUPS = len(POOL_WINDOWS)
POOL_GROUP = D_POOL // N_POOL_GROUPS
N_BRANCHES = 2
D_FF = 4 * D_MODEL
N_MOD = 6
D_IN = 2 * D_RNN + D_POOL + N_BRANCHES * D_MODEL
EPS = 1e-6

kernel_name = "hybrid_rglru_pool_gated_block"


def rmsnorm(x, g):
    xf = x.astype(jnp.float32)
    y = xf * lax.rsqrt(jnp.mean(xf * xf, axis=-1, keepdims=True) + EPS)
    return (y * g.astype(jnp.float32)).astype(x.dtype)


def block_diag(x, w):
    b, s, _ = x.shape
    g, dg, _ = w.shape
    y = jnp.einsum('bsgi,gij->bsgj', x.reshape(b, s, g, dg), w)
    return y.reshape(b, s, g * dg)


def causal_conv(x, w, bias):
    k = w.shape[0]
    s = x.shape[1]
    xp = jnp.pad(x, ((0, 0), (k - 1, 0), (0, 0)))
    y = bias
    for j in range(k):
        y = y + xp[:, j:j + s] * w[j]
    return y


def rg_lru(x, w_a, b_a, w_x, b_x, a_param):
    xf = x.astype(jnp.float32)
    r = jax.nn.sigmoid(block_diag(xf, w_a.astype(jnp.float32)) + b_a.astype(jnp.float32))
    i = jax.nn.sigmoid(block_diag(xf, w_x.astype(jnp.float32)) + b_x.astype(jnp.float32))
    log_a = -C_RG * r * jax.nn.softplus(a_param.astype(jnp.float32))
    a = jnp.exp(log_a)
    mult = jnp.sqrt(-jnp.expm1(2.0 * log_a))
    first = (jnp.arange(x.shape[1]) == 0)[None, :, None]
    mult = jnp.where(first, 1.0, mult)
    bval = xf * i * mult

    def combine(l, rr):
        a1, b1 = l
        a2, b2 = rr
        return a1 * a2, a2 * b1 + b2

    _, h = lax.associative_scan(combine, (a, bval), axis=1)
    return h.astype(x.dtype)


def multiscale_pool(u, w_pool, b_pool, pool_scale):
    uf = u.astype(jnp.float32)
    s = u.shape[1]
    pos = jnp.arange(s, dtype=jnp.float32)[None, :, None]
    outs = []
    for gi, win in enumerate(POOL_WINDOWS):
        seg = uf[..., gi * POOL_GROUP:(gi + 1) * POOL_GROUP]
        cs = jnp.cumsum(seg, axis=1)
        cs_prev = jnp.pad(cs, ((0, 0), (win, 0), (0, 0)))[:, :s]
        cnt = jnp.minimum(pos + 1.0, float(win))
        outs.append((cs - cs_prev) / cnt - seg)
    p = jnp.concatenate(outs, axis=-1)
    p = block_diag(p, w_pool.astype(jnp.float32)) + b_pool.astype(jnp.float32)
    return (p * pool_scale.astype(jnp.float32)).astype(u.dtype)


def _fwd_setup_inputs(seed: int = 0) -> dict:
    key = jax.random.key(seed)
    ks = jax.random.split(key, 24)
    f32 = jnp.float32
    L = DEPTH

    def nrm(k, shape, fan_in):
        return jax.random.normal(k, shape, f32) * (fan_in ** -0.5)

    u = jax.random.uniform(ks[10], (L, D_RNN), f32)
    a_real = 0.5 * jnp.log(u * (MAX_RAD ** 2 - MIN_RAD ** 2) + MIN_RAD ** 2)
    a_param = jnp.log(jnp.expm1(-a_real))
    return {
        "x": jax.random.normal(ks[0], (BATCH, SEQ, D_MODEL), f32),
        "c": jax.random.normal(ks[1], (BATCH, D_MODEL), f32),
        "norm_mix_g": 1.0 + 0.05 * jax.random.normal(ks[2], (L, D_MODEL), f32),
        "norm_mlp_g": 1.0 + 0.05 * jax.random.normal(ks[3], (L, D_MODEL), f32),
        "w_ada": nrm(ks[4], (L, D_MODEL, N_MOD * D_MODEL), D_MODEL),
        "b_ada": 0.02 * jax.random.normal(ks[5], (L, N_MOD * D_MODEL), f32),
        "w_in": nrm(ks[6], (L, D_MODEL, D_IN), D_MODEL),
        "conv_w": nrm(ks[7], (L, CONV_WIDTH, D_RNN), CONV_WIDTH),
        "conv_b": 0.02 * jax.random.normal(ks[8], (L, D_RNN), f32),
        "w_rg_a": nrm(ks[9], (L, N_RNN_BLOCKS, RNN_BLOCK, RNN_BLOCK), RNN_BLOCK),
        "b_rg_a": 0.02 * jax.random.normal(ks[11], (L, D_RNN), f32),
        "w_rg_x": nrm(ks[12], (L, N_RNN_BLOCKS, RNN_BLOCK, RNN_BLOCK), RNN_BLOCK),
        "b_rg_x": 0.02 * jax.random.normal(ks[13], (L, D_RNN), f32),
        "a_param": a_param,
        "w_branch_a": nrm(ks[14], (L, D_RNN, D_MODEL), D_RNN),
        "w_pool": nrm(ks[15], (L, N_POOL_GROUPS, POOL_GROUP, POOL_GROUP), POOL_GROUP),
        "b_pool": 0.02 * jax.random.normal(ks[16], (L, D_POOL), f32),
        "pool_scale": 1.0 + 0.1 * jax.random.normal(ks[17], (L, D_POOL), f32),
        "w_branch_b": nrm(ks[18], (L, D_POOL, D_MODEL), D_POOL),
        "w_out": nrm(ks[19], (L, D_MODEL, D_MODEL), D_MODEL),
        "w_up": nrm(ks[20], (L, D_MODEL, D_FF), D_MODEL),
        "w_down": nrm(ks[21], (L, D_FF, D_MODEL), D_FF),
        "final_g": 1.0 + 0.05 * jax.random.normal(ks[22], (D_MODEL,), f32),
    }


def _fwd_reference(x, c, norm_mix_g, norm_mlp_g, w_ada, b_ada, w_in, conv_w, conv_b,
              w_rg_a, b_rg_a, w_rg_x, b_rg_x, a_param, w_branch_a, w_pool, b_pool,
              pool_scale, w_branch_b, w_out, w_up, w_down, final_g):
    c_act = jax.nn.silu(c)
    for l in range(DEPTH):
        mod = c_act @ w_ada[l] + b_ada[l]
        sh1, sc1, gt1, sh2, sc2, gt2 = [m[:, None, :] for m in jnp.split(mod, N_MOD, axis=-1)]

        h = rmsnorm(x, norm_mix_g[l]) * (1.0 + sc1) + sh1
        proj = h @ w_in[l]
        x_rnn, y_rnn, u_pool, g_a, g_b = jnp.split(
            proj, np.cumsum([D_RNN, D_RNN, D_POOL, D_MODEL]).tolist(), axis=-1)

        xr = causal_conv(x_rnn, conv_w[l], conv_b[l])
        hr = rg_lru(xr, w_rg_a[l], b_rg_a[l], w_rg_x[l], b_rg_x[l], a_param[l])
        branch_a = (jax.nn.gelu(y_rnn) * hr) @ w_branch_a[l]

        pooled = multiscale_pool(u_pool, w_pool[l], b_pool[l], pool_scale[l])
        branch_b = pooled @ w_branch_b[l]

        merged = jax.nn.sigmoid(g_a) * branch_a + jax.nn.sigmoid(g_b) * branch_b
        x = x + gt1 * (merged @ w_out[l])

        h = rmsnorm(x, norm_mlp_g[l]) * (1.0 + sc2) + sh2
        ff = jnp.square(jax.nn.relu(h @ w_up[l]))
        x = x + gt2 * (ff @ w_down[l])
    return rmsnorm(x, final_g)


import jax as _jax
import jax.numpy as _jnp

TWIN_FORMAT = 'train_step'
FWD_PARAMS = ['x', 'c', 'norm_mix_g', 'norm_mlp_g', 'w_ada', 'b_ada', 'w_in', 'conv_w', 'conv_b', 'w_rg_a', 'b_rg_a', 'w_rg_x', 'b_rg_x', 'a_param', 'w_branch_a', 'w_pool', 'b_pool', 'pool_scale', 'w_branch_b', 'w_out', 'w_up', 'w_down', 'final_g']
TWIN_WEIGHTS = ['norm_mix_g', 'norm_mlp_g', 'w_ada', 'b_ada', 'w_in', 'conv_w', 'conv_b', 'w_rg_a', 'b_rg_a', 'w_rg_x', 'b_rg_x', 'a_param', 'w_branch_a', 'w_pool', 'b_pool', 'pool_scale', 'w_branch_b', 'w_out', 'w_up', 'w_down', 'final_g']
TWIN_DIFF_INPUT = 'x'
TWIN_INPUTS = ['x', 'c', 'norm_mix_g', 'norm_mlp_g', 'w_ada', 'b_ada', 'w_in', 'conv_w', 'conv_b', 'w_rg_a', 'b_rg_a', 'w_rg_x', 'b_rg_x', 'a_param', 'w_branch_a', 'w_pool', 'b_pool', 'pool_scale', 'w_branch_b', 'w_out', 'w_up', 'w_down', 'final_g', 'loss_target', 'm_norm_mix_g', 'm_norm_mlp_g', 'm_w_ada', 'm_b_ada', 'm_w_in', 'm_conv_w', 'm_conv_b', 'm_w_rg_a', 'm_b_rg_a', 'm_w_rg_x', 'm_b_rg_x', 'm_a_param', 'm_w_branch_a', 'm_w_pool', 'm_b_pool', 'm_pool_scale', 'm_w_branch_b', 'm_w_out', 'm_w_up', 'm_w_down', 'm_final_g', 'v_norm_mix_g', 'v_norm_mlp_g', 'v_w_ada', 'v_b_ada', 'v_w_in', 'v_conv_w', 'v_conv_b', 'v_w_rg_a', 'v_b_rg_a', 'v_w_rg_x', 'v_b_rg_x', 'v_a_param', 'v_w_branch_a', 'v_w_pool', 'v_b_pool', 'v_pool_scale', 'v_w_branch_b', 'v_w_out', 'v_w_up', 'v_w_down', 'v_final_g']
TWIN_OUTPUTS = ['loss', 'grad_x', 'grad_norm_mix_g', 'grad_norm_mlp_g', 'grad_w_ada', 'grad_b_ada', 'grad_w_in', 'grad_conv_w', 'grad_conv_b', 'grad_w_rg_a', 'grad_b_rg_a', 'grad_w_rg_x', 'grad_b_rg_x', 'grad_a_param', 'grad_w_branch_a', 'grad_w_pool', 'grad_b_pool', 'grad_pool_scale', 'grad_w_branch_b', 'grad_w_out', 'grad_w_up', 'grad_w_down', 'grad_final_g', 'delta_norm_mix_g', 'delta_norm_mlp_g', 'delta_w_ada', 'delta_b_ada', 'delta_w_in', 'delta_conv_w', 'delta_conv_b', 'delta_w_rg_a', 'delta_b_rg_a', 'delta_w_rg_x', 'delta_b_rg_x', 'delta_a_param', 'delta_w_branch_a', 'delta_w_pool', 'delta_b_pool', 'delta_pool_scale', 'delta_w_branch_b', 'delta_w_out', 'delta_w_up', 'delta_w_down', 'delta_final_g', 'new_m_norm_mix_g', 'new_m_norm_mlp_g', 'new_m_w_ada', 'new_m_b_ada', 'new_m_w_in', 'new_m_conv_w', 'new_m_conv_b', 'new_m_w_rg_a', 'new_m_b_rg_a', 'new_m_w_rg_x', 'new_m_b_rg_x', 'new_m_a_param', 'new_m_w_branch_a', 'new_m_w_pool', 'new_m_b_pool', 'new_m_pool_scale', 'new_m_w_branch_b', 'new_m_w_out', 'new_m_w_up', 'new_m_w_down', 'new_m_final_g', 'new_v_norm_mix_g', 'new_v_norm_mlp_g', 'new_v_w_ada', 'new_v_b_ada', 'new_v_w_in', 'new_v_conv_w', 'new_v_conv_b', 'new_v_w_rg_a', 'new_v_b_rg_a', 'new_v_w_rg_x', 'new_v_b_rg_x', 'new_v_a_param', 'new_v_w_branch_a', 'new_v_w_pool', 'new_v_b_pool', 'new_v_pool_scale', 'new_v_w_branch_b', 'new_v_w_out', 'new_v_w_up', 'new_v_w_down', 'new_v_final_g']
TWIN_LEAF_KINDS = {'loss': 'loss', 'grad_x': 'grad_x', 'grad_norm_mix_g': 'grad_w', 'grad_norm_mlp_g': 'grad_w', 'grad_w_ada': 'grad_w', 'grad_b_ada': 'grad_w', 'grad_w_in': 'grad_w', 'grad_conv_w': 'grad_w', 'grad_conv_b': 'grad_w', 'grad_w_rg_a': 'grad_w', 'grad_b_rg_a': 'grad_w', 'grad_w_rg_x': 'grad_w', 'grad_b_rg_x': 'grad_w', 'grad_a_param': 'grad_w', 'grad_w_branch_a': 'grad_w', 'grad_w_pool': 'grad_w', 'grad_b_pool': 'grad_w', 'grad_pool_scale': 'grad_w', 'grad_w_branch_b': 'grad_w', 'grad_w_out': 'grad_w', 'grad_w_up': 'grad_w', 'grad_w_down': 'grad_w', 'grad_final_g': 'grad_w', 'delta_norm_mix_g': 'delta_w', 'delta_norm_mlp_g': 'delta_w', 'delta_w_ada': 'delta_w', 'delta_b_ada': 'delta_w', 'delta_w_in': 'delta_w', 'delta_conv_w': 'delta_w', 'delta_conv_b': 'delta_w', 'delta_w_rg_a': 'delta_w', 'delta_b_rg_a': 'delta_w', 'delta_w_rg_x': 'delta_w', 'delta_b_rg_x': 'delta_w', 'delta_a_param': 'delta_w', 'delta_w_branch_a': 'delta_w', 'delta_w_pool': 'delta_w', 'delta_b_pool': 'delta_w', 'delta_pool_scale': 'delta_w', 'delta_w_branch_b': 'delta_w', 'delta_w_out': 'delta_w', 'delta_w_up': 'delta_w', 'delta_w_down': 'delta_w', 'delta_final_g': 'delta_w', 'new_m_norm_mix_g': 'new_m', 'new_m_norm_mlp_g': 'new_m', 'new_m_w_ada': 'new_m', 'new_m_b_ada': 'new_m', 'new_m_w_in': 'new_m', 'new_m_conv_w': 'new_m', 'new_m_conv_b': 'new_m', 'new_m_w_rg_a': 'new_m', 'new_m_b_rg_a': 'new_m', 'new_m_w_rg_x': 'new_m', 'new_m_b_rg_x': 'new_m', 'new_m_a_param': 'new_m', 'new_m_w_branch_a': 'new_m', 'new_m_w_pool': 'new_m', 'new_m_b_pool': 'new_m', 'new_m_pool_scale': 'new_m', 'new_m_w_branch_b': 'new_m', 'new_m_w_out': 'new_m', 'new_m_w_up': 'new_m', 'new_m_w_down': 'new_m', 'new_m_final_g': 'new_m', 'new_v_norm_mix_g': 'new_v', 'new_v_norm_mlp_g': 'new_v', 'new_v_w_ada': 'new_v', 'new_v_b_ada': 'new_v', 'new_v_w_in': 'new_v', 'new_v_conv_w': 'new_v', 'new_v_conv_b': 'new_v', 'new_v_w_rg_a': 'new_v', 'new_v_b_rg_a': 'new_v', 'new_v_w_rg_x': 'new_v', 'new_v_b_rg_x': 'new_v', 'new_v_a_param': 'new_v', 'new_v_w_branch_a': 'new_v', 'new_v_w_pool': 'new_v', 'new_v_b_pool': 'new_v', 'new_v_pool_scale': 'new_v', 'new_v_w_branch_b': 'new_v', 'new_v_w_out': 'new_v', 'new_v_w_up': 'new_v', 'new_v_w_down': 'new_v', 'new_v_final_g': 'new_v'}


def _forward(args):
    return _fwd_reference(*[args[k] for k in FWD_PARAMS])


def _output_shape():
    def fwd():
        inp = _fwd_setup_inputs(0)
        return _fwd_reference(*[inp[k] for k in FWD_PARAMS])
    out = _jax.eval_shape(fwd)
    return out.shape, out.dtype

N_MICROBATCH = 1
ADAM_LR = 0.001
ADAM_B1 = 0.9
ADAM_B2 = 0.999
ADAM_EPS = 1e-08
ADAM_WD = 0.01
ADAM_STEP = 10
PER_EXAMPLE_BATCH_AXIS = {'x': 0, 'c': 0, 'loss_target': 0}
SHARED_INPUTS = []
_WEIGHT_DTYPES = {'norm_mix_g': _jnp.float32, 'norm_mlp_g': _jnp.float32, 'w_ada': _jnp.float32, 'b_ada': _jnp.float32, 'w_in': _jnp.float32, 'conv_w': _jnp.float32, 'conv_b': _jnp.float32, 'w_rg_a': _jnp.float32, 'b_rg_a': _jnp.float32, 'w_rg_x': _jnp.float32, 'b_rg_x': _jnp.float32, 'a_param': _jnp.float32, 'w_branch_a': _jnp.float32, 'w_pool': _jnp.float32, 'b_pool': _jnp.float32, 'pool_scale': _jnp.float32, 'w_branch_b': _jnp.float32, 'w_out': _jnp.float32, 'w_up': _jnp.float32, 'w_down': _jnp.float32, 'final_g': _jnp.float32}
MOMENT_SCALE = {'norm_mix_g': 9.871518e-02, 'norm_mlp_g': 1.434149e-01, 'w_ada': 2.727570e-01, 'b_ada': 5.726230e-01, 'w_in': 7.342629e-02, 'conv_w': 1.102335e-01, 'conv_b': 1.945982e-01, 'w_rg_a': 1.582886e-02, 'b_rg_a': 2.455250e-02, 'w_rg_x': 2.902141e-02, 'b_rg_x': 4.899405e-02, 'a_param': 7.061439e-02, 'w_branch_a': 1.204284e-01, 'w_pool': 5.481628e-02, 'b_pool': 1.155379e-01, 'pool_scale': 5.466972e-02, 'w_branch_b': 5.471012e-02, 'w_out': 1.302706e-01, 'w_up': 1.179802e-01, 'w_down': 3.690240e-01, 'final_g': 3.415820e+01}


def _to_microbatches(a, axis):
    t = _jnp.moveaxis(a, axis, 0)
    t = t.reshape((N_MICROBATCH, t.shape[0] // N_MICROBATCH) + t.shape[1:])
    return _jnp.moveaxis(t, 1, axis + 1)


def setup_inputs(seed: int = 0) -> dict:
    inp = _fwd_setup_inputs(seed)
    key = _jax.random.fold_in(_jax.random.key(seed), 7919)
    shape, _ = _output_shape()
    out = dict(inp)
    out["loss_target"] = _jax.random.normal(_jax.random.fold_in(key, 0), shape, _jnp.float32)
    for i, name in enumerate(TWIN_WEIGHTS):
        w = inp[name].astype(_jnp.float32)
        if MOMENT_SCALE is None:
            s = _jnp.sqrt(_jnp.mean(_jnp.square(w)) + 1e-30)
        else:
            s = MOMENT_SCALE[name]
        km, kv = _jax.random.split(_jax.random.fold_in(key, i + 1))
        out[name] = w
        out["m_" + name] = s * _jax.random.normal(km, w.shape, _jnp.float32)
        out["v_" + name] = (s * s) * _jax.random.uniform(kv, w.shape, _jnp.float32, 0.5, 1.5)
    if N_MICROBATCH > 1:
        for name, axis in PER_EXAMPLE_BATCH_AXIS.items():
            out[name] = _to_microbatches(out[name], axis)
    return {'x': out['x'], 'c': out['c'], 'norm_mix_g': out['norm_mix_g'], 'norm_mlp_g': out['norm_mlp_g'], 'w_ada': out['w_ada'], 'b_ada': out['b_ada'], 'w_in': out['w_in'], 'conv_w': out['conv_w'], 'conv_b': out['conv_b'], 'w_rg_a': out['w_rg_a'], 'b_rg_a': out['b_rg_a'], 'w_rg_x': out['w_rg_x'], 'b_rg_x': out['b_rg_x'], 'a_param': out['a_param'], 'w_branch_a': out['w_branch_a'], 'w_pool': out['w_pool'], 'b_pool': out['b_pool'], 'pool_scale': out['pool_scale'], 'w_branch_b': out['w_branch_b'], 'w_out': out['w_out'], 'w_up': out['w_up'], 'w_down': out['w_down'], 'final_g': out['final_g'], 'loss_target': out['loss_target'], 'm_norm_mix_g': out['m_norm_mix_g'], 'm_norm_mlp_g': out['m_norm_mlp_g'], 'm_w_ada': out['m_w_ada'], 'm_b_ada': out['m_b_ada'], 'm_w_in': out['m_w_in'], 'm_conv_w': out['m_conv_w'], 'm_conv_b': out['m_conv_b'], 'm_w_rg_a': out['m_w_rg_a'], 'm_b_rg_a': out['m_b_rg_a'], 'm_w_rg_x': out['m_w_rg_x'], 'm_b_rg_x': out['m_b_rg_x'], 'm_a_param': out['m_a_param'], 'm_w_branch_a': out['m_w_branch_a'], 'm_w_pool': out['m_w_pool'], 'm_b_pool': out['m_b_pool'], 'm_pool_scale': out['m_pool_scale'], 'm_w_branch_b': out['m_w_branch_b'], 'm_w_out': out['m_w_out'], 'm_w_up': out['m_w_up'], 'm_w_down': out['m_w_down'], 'm_final_g': out['m_final_g'], 'v_norm_mix_g': out['v_norm_mix_g'], 'v_norm_mlp_g': out['v_norm_mlp_g'], 'v_w_ada': out['v_w_ada'], 'v_b_ada': out['v_b_ada'], 'v_w_in': out['v_w_in'], 'v_conv_w': out['v_conv_w'], 'v_conv_b': out['v_conv_b'], 'v_w_rg_a': out['v_w_rg_a'], 'v_b_rg_a': out['v_b_rg_a'], 'v_w_rg_x': out['v_w_rg_x'], 'v_b_rg_x': out['v_b_rg_x'], 'v_a_param': out['v_a_param'], 'v_w_branch_a': out['v_w_branch_a'], 'v_w_pool': out['v_w_pool'], 'v_b_pool': out['v_b_pool'], 'v_pool_scale': out['v_pool_scale'], 'v_w_branch_b': out['v_w_branch_b'], 'v_w_out': out['v_w_out'], 'v_w_up': out['v_w_up'], 'v_w_down': out['v_w_down'], 'v_final_g': out['v_final_g']}


def _loss(weights, diff, rest, loss_target):
    with _jax.named_scope("forward"):
        args = {**rest, TWIN_DIFF_INPUT: diff, **{k: w.astype(_WEIGHT_DTYPES[k]) for k, w in weights.items()}}
        y = _forward(args)
    with _jax.named_scope("loss_head"):
        err = _jnp.square(y.astype(_jnp.float32) - loss_target)
        return 0.5 * _jnp.sum(_jnp.mean(err, axis=-1)) if err.ndim else 0.5 * err


def _adamw(w, g, m, v):
    m = ADAM_B1 * m + (1.0 - ADAM_B1) * g
    v = ADAM_B2 * v + (1.0 - ADAM_B2) * _jnp.square(g)
    m_hat = m / (1.0 - ADAM_B1 ** ADAM_STEP)
    v_hat = v / (1.0 - ADAM_B2 ** ADAM_STEP)
    delta = -ADAM_LR * (m_hat / (_jnp.sqrt(v_hat) + ADAM_EPS) + ADAM_WD * w)
    return delta, m, v


def reference(x, c, norm_mix_g, norm_mlp_g, w_ada, b_ada, w_in, conv_w, conv_b, w_rg_a, b_rg_a, w_rg_x, b_rg_x, a_param, w_branch_a, w_pool, b_pool, pool_scale, w_branch_b, w_out, w_up, w_down, final_g, loss_target, m_norm_mix_g, m_norm_mlp_g, m_w_ada, m_b_ada, m_w_in, m_conv_w, m_conv_b, m_w_rg_a, m_b_rg_a, m_w_rg_x, m_b_rg_x, m_a_param, m_w_branch_a, m_w_pool, m_b_pool, m_pool_scale, m_w_branch_b, m_w_out, m_w_up, m_w_down, m_final_g, v_norm_mix_g, v_norm_mlp_g, v_w_ada, v_b_ada, v_w_in, v_conv_w, v_conv_b, v_w_rg_a, v_b_rg_a, v_w_rg_x, v_b_rg_x, v_a_param, v_w_branch_a, v_w_pool, v_b_pool, v_pool_scale, v_w_branch_b, v_w_out, v_w_up, v_w_down, v_final_g):
    given = dict(x=x, c=c, norm_mix_g=norm_mix_g, norm_mlp_g=norm_mlp_g, w_ada=w_ada, b_ada=b_ada, w_in=w_in, conv_w=conv_w, conv_b=conv_b, w_rg_a=w_rg_a, b_rg_a=b_rg_a, w_rg_x=w_rg_x, b_rg_x=b_rg_x, a_param=a_param, w_branch_a=w_branch_a, w_pool=w_pool, b_pool=b_pool, pool_scale=pool_scale, w_branch_b=w_branch_b, w_out=w_out, w_up=w_up, w_down=w_down, final_g=final_g, loss_target=loss_target, m_norm_mix_g=m_norm_mix_g, m_norm_mlp_g=m_norm_mlp_g, m_w_ada=m_w_ada, m_b_ada=m_b_ada, m_w_in=m_w_in, m_conv_w=m_conv_w, m_conv_b=m_conv_b, m_w_rg_a=m_w_rg_a, m_b_rg_a=m_b_rg_a, m_w_rg_x=m_w_rg_x, m_b_rg_x=m_b_rg_x, m_a_param=m_a_param, m_w_branch_a=m_w_branch_a, m_w_pool=m_w_pool, m_b_pool=m_b_pool, m_pool_scale=m_pool_scale, m_w_branch_b=m_w_branch_b, m_w_out=m_w_out, m_w_up=m_w_up, m_w_down=m_w_down, m_final_g=m_final_g, v_norm_mix_g=v_norm_mix_g, v_norm_mlp_g=v_norm_mlp_g, v_w_ada=v_w_ada, v_b_ada=v_b_ada, v_w_in=v_w_in, v_conv_w=v_conv_w, v_conv_b=v_conv_b, v_w_rg_a=v_w_rg_a, v_b_rg_a=v_b_rg_a, v_w_rg_x=v_w_rg_x, v_b_rg_x=v_b_rg_x, v_a_param=v_a_param, v_w_branch_a=v_w_branch_a, v_w_pool=v_w_pool, v_b_pool=v_b_pool, v_pool_scale=v_pool_scale, v_w_branch_b=v_w_branch_b, v_w_out=v_w_out, v_w_up=v_w_up, v_w_down=v_w_down, v_final_g=v_final_g)
    weights = {n: given[n] for n in TWIN_WEIGHTS}
    shared = {n: given[n] for n in SHARED_INPUTS}
    per_example = {n: given[n] for n in ['x', 'c']}
    grad_fn = _jax.value_and_grad(_loss, argnums=(0, 1))

    def one_microbatch(ex, loss_target):
        ex = dict(ex)
        diff = ex.pop(TWIN_DIFF_INPUT)
        return grad_fn(weights, diff, {**shared, **ex}, loss_target)

    if N_MICROBATCH == 1:
        loss, (grad_w, grad_x) = one_microbatch(per_example, given["loss_target"])
    else:
        def body(carry, xs):
            loss_sum, grad_sum = carry
            l_k, (gw_k, gx_k) = one_microbatch(xs[0], xs[1])
            with _jax.named_scope("update"):
                return (loss_sum + l_k, _jax.tree.map(_jnp.add, grad_sum, gw_k)), gx_k

        init = (_jnp.zeros((), _jnp.float32), _jax.tree.map(_jnp.zeros_like, weights))
        (loss, grad_w), grad_x = _jax.lax.scan(body, init, (per_example, given["loss_target"]))
    with _jax.named_scope("update"):
        delta_w, new_m, new_v = {}, {}, {}
        for n in TWIN_WEIGHTS:
            delta_w[n], new_m[n], new_v[n] = _adamw(weights[n], grad_w[n], given["m_" + n], given["v_" + n])
    return (loss, grad_x, *[grad_w[n] for n in TWIN_WEIGHTS], *[delta_w[n] for n in TWIN_WEIGHTS],
            *[new_m[n] for n in TWIN_WEIGHTS], *[new_v[n] for n in TWIN_WEIGHTS])
```

```python
import functools

import jax
import jax.numpy as jnp
from jax import lax
from jax.experimental import pallas as pl
from jax.experimental.pallas import tpu as pltpu

F32 = jnp.float32
BF16 = jnp.bfloat16
MESH = pl.DeviceIdType.MESH

N_DEV = 8
D = 1024
N_GROUPS = 4
GW = D // N_GROUPS
D_IN = 5 * D
D_FF = 4 * D
POOL_WINDOWS = (2, 4, 8, 16)
HALO_X = 8
HALO_U = 16
EPS = 1e-6
C_RG = 8.0
ADAM_LR, ADAM_B1, ADAM_B2, ADAM_EPS, ADAM_WD, ADAM_STEP = 0.001, 0.9, 0.999, 1e-08, 0.01, 10

V7X_VMEM_LIMIT = 56 * 1024 * 1024

V_CONV_W, V_CONV_B, V_B_RG_A, V_B_RG_X, V_A_PARAM, V_B_POOL, V_POOL_SCALE, V_G1, V_G2, V_GF = 0, 4, 5, 6, 7, 8, 9, 10, 11, 12
M_SH1, M_SC1, M_GT1, M_SH2, M_SC2, M_GT2 = 0, 1, 2, 3, 4, 5

TM_PROJ = 512
TM_MIX = 256
TM_BRANCH = 256
TM_MLP = 512
TS_WGRAD = 512


def _params(semantics):
    return pltpu.CompilerParams(dimension_semantics=semantics, vmem_limit_bytes=V7X_VMEM_LIMIT)


def _dot(a, b):
    return jnp.dot(a, b, preferred_element_type=F32)


def _dot_nt(a, b):
    return lax.dot_general(a, b, (((1,), (1,)), ((), ())), preferred_element_type=F32)


def _dot_tn(a, b):
    return lax.dot_general(a, b, (((0,), (0,)), ((), ())), preferred_element_type=F32)


def _sigmoid(x):
    return 1.0 / (1.0 + jnp.exp(-x))


def _gelu_and_grad(x):
    k = 0.7978845608028654
    x2 = x * x
    t = jnp.tanh(k * (x + 0.044715 * x * x2))
    g = 0.5 * x * (1.0 + t)
    dg = 0.5 * (1.0 + t) + 0.5 * x * (1.0 - t * t) * (k * (1.0 + 3.0 * 0.044715 * x2))
    return g, dg


def _softplus(a):
    e = jnp.exp(-jnp.abs(a))
    u = 1.0 + e
    log1p_e = jnp.where(u == 1.0, e, jnp.log(u) * e / jnp.where(u == 1.0, 1.0, u - 1.0))
    return jnp.maximum(a, 0.0) + log1p_e


def _neg_expm1(z):
    series = -(z * (1.0 + z * (0.5 + z * (1.0 / 6.0 + z * (1.0 / 24.0 + z * (1.0 / 120.0))))))
    return jnp.where(z > -0.1, series, 1.0 - jnp.exp(z))


def _shift_down(x, k):
    return pltpu.roll(x, k, 0)


def _shift_up(x, k):
    return pltpu.roll(x, x.shape[0] - k, 0)


def _rglru_gates(xr, w_a, w_x, b_a, b_x, a_param, is_t0):
    xb = xr.astype(BF16)
    ra = _sigmoid(_dot(xb, w_a) + b_a)
    ri = _sigmoid(_dot(xb, w_x) + b_x)
    sp = _softplus(a_param)
    log_a = (-C_RG) * ra * sp
    a = jnp.exp(log_a)
    mult = jnp.where(is_t0, 1.0, jnp.sqrt(_neg_expm1(2.0 * log_a)))
    return ra, ri, sp, a, mult


def _conv_taps(x_ext):
    return [_shift_down(x_ext, 3 - j)[HALO_X:] if j < 3 else x_ext[HALO_X:] for j in range(4)]


def _proj_fwd(x, modr, vecs, w_in):
    s = x.shape[0]
    tm, tn = min(TM_PROJ, s), 640

    def body(x_ref, mod_ref, vec_ref, w_ref, proj_ref, h1_ref, h1_scr):
        j = pl.program_id(1)

        @pl.when(j == 0)
        def _():
            xv = x_ref[...]
            r = lax.rsqrt(jnp.mean(xv * xv, axis=-1, keepdims=True) + EPS)
            gain = vec_ref[V_G1:V_G1 + 1, :] * (1.0 + mod_ref[M_SC1:M_SC1 + 1, :])
            h = (xv * r * gain + mod_ref[M_SH1:M_SH1 + 1, :]).astype(BF16)
            h1_scr[...] = h
            h1_ref[...] = h

        proj_ref[...] = _dot(h1_scr[...], w_ref[...])

    return pl.pallas_call(
        body, name="proj_fwd", grid=(s // tm, D_IN // tn),
        in_specs=[pl.BlockSpec((tm, D), lambda i, j: (i, 0)),
                  pl.BlockSpec((8, D), lambda i, j: (0, 0)),
                  pl.BlockSpec((16, D), lambda i, j: (0, 0)),
                  pl.BlockSpec((D, tn), lambda i, j: (0, j))],
        out_specs=[pl.BlockSpec((tm, tn), lambda i, j: (i, j)),
                   pl.BlockSpec((tm, D), lambda i, j: (i, 0))],
        out_shape=[jax.ShapeDtypeStruct((s, D_IN), F32), jax.ShapeDtypeStruct((s, D), BF16)],
        scratch_shapes=[pltpu.VMEM((tm, D), BF16)],
        compiler_params=_params(("parallel", "arbitrary")),
    )(x, modr, vecs, w_in)


def _mix_fwd(proj, vecs, w_rg_a, w_rg_x, w_pool):
    s = proj.shape[0]
    tm = min(TM_MIX, s)
    nb = s // tm

    def body(xh_ref, x_ref, y_ref, uh_ref, u_ref, vec_ref, wa_ref, wx_ref, wp_ref,
             xr_ref, hr_ref, za_ref, p_ref, pooled_ref, carry_ref):
        i = pl.program_id(0)
        first = i == 0

        @pl.when(first)
        def _():
            carry_ref[...] = jnp.zeros_like(carry_ref)

        row = lax.broadcasted_iota(jnp.int32, (tm, GW), 0)
        is_t0 = jnp.logical_and(first, row == 0)
        t_glob = (row + i * tm + 1).astype(F32)
        for g in range(N_GROUPS):
            cs = slice(g * GW, (g + 1) * GW)
            vec = vec_ref[:, cs]
            xh = jnp.where(first, 0.0, xh_ref[:, cs])
            taps = _conv_taps(jnp.concatenate([xh, x_ref[:, cs]], axis=0))
            xr = vec[V_CONV_B:V_CONV_B + 1]
            for j in range(4):
                xr = xr + vec[V_CONV_W + j:V_CONV_W + j + 1] * taps[j]
            xr_ref[:, cs] = xr
            _, ri, _, a, mult = _rglru_gates(
                xr, wa_ref[g], wx_ref[g], vec[V_B_RG_A:V_B_RG_A + 1], vec[V_B_RG_X:V_B_RG_X + 1],
                vec[V_A_PARAM:V_A_PARAM + 1], is_t0)
            b = xr * ri * mult
            b = b + jnp.where(row == 0, a * carry_ref[7:8, cs], 0.0)
            k = 1
            while k < tm:
                b = b + a * jnp.where(row >= k, _shift_down(b, k), 0.0)
                if 2 * k < tm:
                    a = a * _shift_down(a, k)
                k *= 2
            hr_ref[:, cs] = b
            carry_ref[:, cs] = b[tm - 8:, :]
            ga, _ = _gelu_and_grad(y_ref[:, cs])
            za_ref[:, cs] = (ga * b).astype(BF16)
            uh = jnp.where(first, 0.0, uh_ref[:, cs])
            sm = jnp.concatenate([uh, u_ref[:, cs]], axis=0)
            k = 1
            while k < POOL_WINDOWS[g]:
                sm = sm + _shift_down(sm, k)
                k *= 2
            cnt = jnp.minimum(t_glob, float(POOL_WINDOWS[g]))
            p = (sm[HALO_U:] / cnt - u_ref[:, cs]).astype(BF16)
            p_ref[:, cs] = p
            pb = _dot(p, wp_ref[g]) + vec[V_B_POOL:V_B_POOL + 1]
            pooled_ref[:, cs] = (pb * vec[V_POOL_SCALE:V_POOL_SCALE + 1]).astype(BF16)

    col = lambda k: (lambda i: (i, k))
    wspec = pl.BlockSpec((N_GROUPS, GW, GW), lambda i: (0, 0, 0))
    return pl.pallas_call(
        body, name="mix_fwd", grid=(nb,),
        in_specs=[pl.BlockSpec((HALO_X, D), lambda i: (jnp.maximum(i * (tm // HALO_X) - 1, 0), 0)),
                  pl.BlockSpec((tm, D), col(0)),
                  pl.BlockSpec((tm, D), col(1)),
                  pl.BlockSpec((HALO_U, D), lambda i: (jnp.maximum(i * (tm // HALO_U) - 1, 0), 2)),
                  pl.BlockSpec((tm, D), col(2)),
                  pl.BlockSpec((16, D), lambda i: (0, 0)),
                  wspec, wspec, wspec],
        out_specs=[pl.BlockSpec((tm, D), lambda i: (i, 0))] * 5,
        out_shape=[jax.ShapeDtypeStruct((s, D), F32), jax.ShapeDtypeStruct((s, D), F32),
                   jax.ShapeDtypeStruct((s, D), BF16), jax.ShapeDtypeStruct((s, D), BF16),
                   jax.ShapeDtypeStruct((s, D), BF16)],
        scratch_shapes=[pltpu.VMEM((8, D), F32)],
        compiler_params=_params(("arbitrary",)),
    )(proj, proj, proj, proj, proj, vecs, w_rg_a, w_rg_x, w_pool)


def _branch_fwd(za, pooled, proj, x, modr, vecs, w_a, w_b, w_out):
    s = x.shape[0]
    tm = min(TM_BRANCH, s)

    def body(za_ref, pooled_ref, ga_ref, gb_ref, x_ref, mod_ref, vec_ref, wa_ref, wb_ref, wo_ref,
             ba_ref, bb_ref, merged_ref, o_ref, x2_ref, h2_ref):
        ba = _dot(za_ref[...], wa_ref[...])
        bb = _dot(pooled_ref[...], wb_ref[...])
        ba_ref[...] = ba.astype(BF16)
        bb_ref[...] = bb.astype(BF16)
        merged = (_sigmoid(ga_ref[...]) * ba + _sigmoid(gb_ref[...]) * bb).astype(BF16)
        merged_ref[...] = merged
        o = _dot(merged, wo_ref[...])
        o_ref[...] = o.astype(BF16)
        x2 = x_ref[...] + mod_ref[M_GT1:M_GT1 + 1, :] * o
        x2_ref[...] = x2
        r = lax.rsqrt(jnp.mean(x2 * x2, axis=-1, keepdims=True) + EPS)
        gain = vec_ref[V_G2:V_G2 + 1, :] * (1.0 + mod_ref[M_SC2:M_SC2 + 1, :])
        h2_ref[...] = (x2 * r * gain + mod_ref[M_SH2:M_SH2 + 1, :]).astype(BF16)

    tok = pl.BlockSpec((tm, D), lambda i: (i, 0))
    wspec = pl.BlockSpec((D, D), lambda i: (0, 0))
    sd = lambda dt: jax.ShapeDtypeStruct((s, D), dt)
    return pl.pallas_call(
        body, name="branch_fwd", grid=(s // tm,),
        in_specs=[tok, tok,
                  pl.BlockSpec((tm, D), lambda i: (i, 3)), pl.BlockSpec((tm, D), lambda i: (i, 4)),
                  tok, pl.BlockSpec((8, D), lambda i: (0, 0)), pl.BlockSpec((16, D), lambda i: (0, 0)),
                  wspec, wspec, wspec],
        out_specs=[tok] * 6,
        out_shape=[sd(BF16), sd(BF16), sd(BF16), sd(BF16), sd(F32), sd(BF16)],
        compiler_params=_params(("parallel",)),
    )(za, pooled, proj, proj, x, modr, vecs, w_a, w_b, w_out)


def _mlp_fwd(h2, x2, target, modr, vecs, w_up, w_down):
    s = x2.shape[0]
    tm, tf = min(TM_MLP, s), 512
    nj = D_FF // tf

    def body(h2_ref, x2_ref, tgt_ref, mod_ref, vec_ref, wu_ref, wd_ref,
             ru_ref, dx3_ref, ddn_ref, small_ref, acc_ref):
        i = pl.program_id(0)
        j = pl.program_id(1)

        @pl.when(jnp.logical_and(i == 0, j == 0))
        def _():
            small_ref[...] = jnp.zeros_like(small_ref)

        ru = jnp.maximum(_dot(h2_ref[...], wu_ref[...]), 0.0)
        ru_ref[...] = ru.astype(BF16)
        part = _dot((ru * ru).astype(BF16), wd_ref[...])

        @pl.when(j == 0)
        def _():
            acc_ref[...] = part

        @pl.when(j > 0)
        def _():
            acc_ref[...] += part

        @pl.when(j == nj - 1)
        def _():
            dn = acc_ref[...]
            gt2 = mod_ref[M_GT2:M_GT2 + 1, :]
            gf = vec_ref[V_GF:V_GF + 1, :]
            x3 = x2_ref[...] + gt2 * dn
            r3 = lax.rsqrt(jnp.mean(x3 * x3, axis=-1, keepdims=True) + EPS)
            n3 = x3 * r3
            err = n3 * gf - tgt_ref[...]
            dy = err * (1.0 / D)
            dn3 = dy * gf
            dx3 = r3 * (dn3 - n3 * jnp.mean(dn3 * n3, axis=-1, keepdims=True))
            dx3_ref[...] = dx3
            ddn_ref[...] = (dx3 * gt2).astype(BF16)
            small_ref[0:1, :] += jnp.sum(dy * n3, axis=0, keepdims=True)
            small_ref[1:2, :] += jnp.sum(dx3 * dn, axis=0, keepdims=True)
            small_ref[2:3, :] += (0.5 / D) * jnp.sum(err * err, axis=0, keepdims=True)

    tok = pl.BlockSpec((tm, D), lambda i, j: (i, 0))
    return pl.pallas_call(
        body, name="mlp_fwd", grid=(s // tm, nj),
        in_specs=[tok, tok, tok,
                  pl.BlockSpec((8, D), lambda i, j: (0, 0)), pl.BlockSpec((16, D), lambda i, j: (0, 0)),
                  pl.BlockSpec((D, tf), lambda i, j: (0, j)), pl.BlockSpec((tf, D), lambda i, j: (j, 0))],
        out_specs=[pl.BlockSpec((tm, tf), lambda i, j: (i, j)), tok, tok,
                   pl.BlockSpec((8, D), lambda i, j: (0, 0))],
        out_shape=[jax.ShapeDtypeStruct((s, D_FF), BF16), jax.ShapeDtypeStruct((s, D), F32),
                   jax.ShapeDtypeStruct((s, D), BF16), jax.ShapeDtypeStruct((8, D), F32)],
        scratch_shapes=[pltpu.VMEM((tm, D), F32)],
        compiler_params=_params(("arbitrary", "arbitrary")),
    )(h2, x2, target, modr, vecs, w_up, w_down)


def _mlp_bwd(d_dn, ru, x2, dx3, o, modr, vecs, w_up, w_down):
    s = x2.shape[0]
    tm, tf = min(TM_MLP, s), 512
    nj = D_FF // tf

    def body(ddn_ref, ru_ref, x2_ref, dx3_ref, o_ref, mod_ref, vec_ref, wu_ref, wd_ref,
             dup_ref, dx2_ref, do_ref, small_ref, acc_ref):
        i = pl.program_id(0)
        j = pl.program_id(1)

        @pl.when(jnp.logical_and(i == 0, j == 0))
        def _():
            small_ref[...] = jnp.zeros_like(small_ref)

        dff = _dot_nt(ddn_ref[...], wd_ref[...])
        dup = (dff * (2.0 * ru_ref[...].astype(F32))).astype(BF16)
        dup_ref[...] = dup
        part = _dot_nt(dup, wu_ref[...])

        @pl.when(j == 0)
        def _():
            acc_ref[...] = part

        @pl.when(j > 0)
        def _():
            acc_ref[...] += part

        @pl.when(j == nj - 1)
        def _():
            dh2 = acc_ref[...]
            x2 = x2_ref[...]
            r2 = lax.rsqrt(jnp.mean(x2 * x2, axis=-1, keepdims=True) + EPS)
            xn2 = x2 * r2
            gain = vec_ref[V_G2:V_G2 + 1, :] * (1.0 + mod_ref[M_SC2:M_SC2 + 1, :])
            dxn2 = dh2 * gain
            dx2 = dx3_ref[...] + r2 * (dxn2 - xn2 * jnp.mean(dxn2 * xn2, axis=-1, keepdims=True))
            dx2_ref[...] = dx2
            do_ref[...] = (dx2 * mod_ref[M_GT1:M_GT1 + 1, :]).astype(BF16)
            small_ref[0:1, :] += jnp.sum(dh2, axis=0, keepdims=True)
            small_ref[1:2, :] += jnp.sum(dh2 * xn2, axis=0, keepdims=True)
            small_ref[2:3, :] += jnp.sum(dx2 * o_ref[...].astype(F32), axis=0, keepdims=True)

    tok = pl.BlockSpec((tm, D), lambda i, j: (i, 0))
    chunk = pl.BlockSpec((tm, tf), lambda i, j: (i, j))
    return pl.pallas_call(
        body, name="mlp_bwd", grid=(s // tm, nj),
        in_specs=[tok, chunk, tok, tok, tok,
                  pl.BlockSpec((8, D), lambda i, j: (0, 0)), pl.BlockSpec((16, D), lambda i, j: (0, 0)),
                  pl.BlockSpec((D, tf), lambda i, j: (0, j)), pl.BlockSpec((tf, D), lambda i, j: (j, 0))],
        out_specs=[chunk, tok, tok, pl.BlockSpec((8, D), lambda i, j: (0, 0))],
        out_shape=[jax.ShapeDtypeStruct((s, D_FF), BF16), jax.ShapeDtypeStruct((s, D), F32),
                   jax.ShapeDtypeStruct((s, D), BF16), jax.ShapeDtypeStruct((8, D), F32)],
        scratch_shapes=[pltpu.VMEM((tm, D), F32)],
        compiler_params=_params(("arbitrary", "arbitrary")),
    )(d_dn, ru, x2, dx3, o, modr, vecs, w_up, w_down)


def _branch_bwd(do, proj, ba, bb, w_a, w_b, w_out):
    s = do.shape[0]
    tm = min(TM_BRANCH, s)

    def body(do_ref, ga_ref, gb_ref, ba_ref, bb_ref, wa_ref, wb_ref, wo_ref,
             dba_ref, dbb_ref, dg_ref, dza_ref, dpooled_ref):
        dmerged = _dot_nt(do_ref[...], wo_ref[...])
        sa = _sigmoid(ga_ref[...])
        sb = _sigmoid(gb_ref[...])
        dba = (dmerged * sa).astype(BF16)
        dbb = (dmerged * sb).astype(BF16)
        dba_ref[...] = dba
        dbb_ref[...] = dbb
        dg_ref[:, :D] = (dmerged * ba_ref[...].astype(F32) * sa * (1.0 - sa)).astype(BF16)
        dg_ref[:, D:] = (dmerged * bb_ref[...].astype(F32) * sb * (1.0 - sb)).astype(BF16)
        dza_ref[...] = _dot_nt(dba, wa_ref[...])
        dpooled_ref[...] = _dot_nt(dbb, wb_ref[...])

    tok = pl.BlockSpec((tm, D), lambda i: (i, 0))
    wspec = pl.BlockSpec((D, D), lambda i: (0, 0))
    sd = lambda dt: jax.ShapeDtypeStruct((s, D), dt)
    return pl.pallas_call(
        body, name="branch_bwd", grid=(s // tm,),
        in_specs=[tok, pl.BlockSpec((tm, D), lambda i: (i, 3)), pl.BlockSpec((tm, D), lambda i: (i, 4)),
                  tok, tok, wspec, wspec, wspec],
        out_specs=[tok, tok, pl.BlockSpec((tm, 2 * D), lambda i: (i, 0)), tok, tok],
        out_shape=[sd(BF16), sd(BF16), jax.ShapeDtypeStruct((s, 2 * D), BF16), sd(F32), sd(F32)],
        compiler_params=_params(("parallel",)),
    )(do, proj, proj, ba, bb, w_a, w_b, w_out)


def _mix_bwd(dza, dpooled, proj, xr, hr, p, dgates, vecs, w_rg_a, w_rg_x, w_pool):
    s = xr.shape[0]
    tm = min(TM_MIX, s)
    nb = s // tm

    def body(dza_ref, dpooled_ref, xh_ref, x_ref, y_ref, xr_ref, hh_ref, hr_ref, p_ref, dg_ref,
             vec_ref, wa_ref, wx_ref, wp_ref,
             dproj_ref, dwa_ref, dwx_ref, dwp_ref, small_ref,
             scan_carry, dxr_carry, q_carry):
        i = pl.program_id(0)
        bi = nb - 1 - i
        first_t = bi == 0

        @pl.when(i == 0)
        def _():
            scan_carry[...] = jnp.zeros_like(scan_carry)
            dxr_carry[...] = jnp.zeros_like(dxr_carry)
            q_carry[...] = jnp.zeros_like(q_carry)
            dwa_ref[...] = jnp.zeros_like(dwa_ref)
            dwx_ref[...] = jnp.zeros_like(dwx_ref)
            dwp_ref[...] = jnp.zeros_like(dwp_ref)
            small_ref[...] = jnp.zeros_like(small_ref)

        row = lax.broadcasted_iota(jnp.int32, (tm, GW), 0)
        is_t0 = jnp.logical_and(first_t, row == 0)
        t_glob = (row + bi * tm + 1).astype(F32)
        colsum = lambda v: jnp.sum(v, axis=0, keepdims=True)
        for g in range(N_GROUPS):
            cs = slice(g * GW, (g + 1) * GW)
            vec = vec_ref[:, cs]
            xr = xr_ref[:, cs]
            hr = hr_ref[:, cs]
            dza = dza_ref[:, cs]
            ga, dga = _gelu_and_grad(y_ref[:, cs])
            dproj_ref[:, D + g * GW:D + (g + 1) * GW] = (dza * hr * dga).astype(BF16)
            dhr = dza * ga
            ra, ri, sp, a, mult = _rglru_gates(
                xr, wa_ref[g], wx_ref[g], vec[V_B_RG_A:V_B_RG_A + 1], vec[V_B_RG_X:V_B_RG_X + 1],
                vec[V_A_PARAM:V_A_PARAM + 1], is_t0)
            m = jnp.where(row == tm - 1, 0.0, _shift_up(a, 1))
            gsum = dhr + jnp.where(row == tm - 1, scan_carry[0:1, cs], 0.0)
            k = 1
            while k < tm:
                gsum = gsum + m * jnp.where(row < tm - k, _shift_up(gsum, k), 0.0)
                if 2 * k < tm:
                    m = m * _shift_up(m, k)
                k *= 2
            scan_carry[:, cs] = (a * gsum)[0:8, :]
            hh = jnp.where(first_t, 0.0, hh_ref[:, cs])
            hprev = _shift_down(jnp.concatenate([hh, hr], axis=0), 1)[8:]
            da = gsum * hprev
            dmult = jnp.where(is_t0, 0.0, gsum * xr * ri)
            dlog_a = da * a - dmult * a * a / mult
            dri = gsum * xr * mult
            dxr = gsum * ri * mult
            small_ref[7:8, cs] += colsum((-C_RG) * ra * dlog_a)
            dpa = (((-C_RG) * sp) * dlog_a * ra * (1.0 - ra))
            dpx = dri * ri * (1.0 - ri)
            small_ref[5:6, cs] += colsum(dpa)
            small_ref[6:7, cs] += colsum(dpx)
            dpa = dpa.astype(BF16)
            dpx = dpx.astype(BF16)
            xrb = xr.astype(BF16)
            dwa_ref[g] += _dot_tn(xrb, dpa)
            dwx_ref[g] += _dot_tn(xrb, dpx)
            dxr = dxr + _dot_nt(dpa, wa_ref[g]) + _dot_nt(dpx, wx_ref[g])
            small_ref[4:5, cs] += colsum(dxr)
            xh = jnp.where(first_t, 0.0, xh_ref[:, cs])
            taps = _conv_taps(jnp.concatenate([xh, x_ref[:, cs]], axis=0))
            dxr_ext = jnp.concatenate([dxr, dxr_carry[:, cs]], axis=0)
            dx = vec[V_CONV_W + 3:V_CONV_W + 4] * dxr
            for j in range(4):
                small_ref[j:j + 1, cs] += colsum(dxr * taps[j])
                if j < 3:
                    dx = dx + vec[V_CONV_W + j:V_CONV_W + j + 1] * _shift_up(dxr_ext, 3 - j)[:tm]
            dxr_carry[:, cs] = dxr[0:8, :]
            dproj_ref[:, cs] = dx.astype(BF16)
            pg = p_ref[:, cs]
            dpooled = dpooled_ref[:, cs]
            pb = _dot(pg, wp_ref[g]) + vec[V_B_POOL:V_B_POOL + 1]
            small_ref[9:10, cs] += colsum(dpooled * pb)
            dpb = dpooled * vec[V_POOL_SCALE:V_POOL_SCALE + 1]
            small_ref[8:9, cs] += colsum(dpb)
            dpbb = dpb.astype(BF16)
            dwp_ref[g] += _dot_tn(pg, dpbb)
            dp = _dot_nt(dpbb, wp_ref[g])
            q = dp / jnp.minimum(t_glob, float(POOL_WINDOWS[g]))
            sm = jnp.concatenate([q, q_carry[:, cs]], axis=0)
            k = 1
            while k < POOL_WINDOWS[g]:
                sm = sm + _shift_up(sm, k)
                k *= 2
            q_carry[:, cs] = q[0:HALO_U, :]
            dproj_ref[:, 2 * D + g * GW:2 * D + (g + 1) * GW] = (sm[:tm] - dp).astype(BF16)
        dproj_ref[:, 3 * D:] = dg_ref[...]

    rev = lambda i: nb - 1 - i
    tok = pl.BlockSpec((tm, D), lambda i: (rev(i), 0))
    col = lambda k: pl.BlockSpec((tm, D), lambda i: (rev(i), k))
    halo8 = lambda k: pl.BlockSpec((8, D), lambda i: (jnp.maximum(rev(i) * (tm // 8) - 1, 0), k))
    wspec = pl.BlockSpec((N_GROUPS, GW, GW), lambda i: (0, 0, 0))
    wshape = jax.ShapeDtypeStruct((N_GROUPS, GW, GW), F32)
    return pl.pallas_call(
        body, name="mix_bwd", grid=(nb,),
        in_specs=[tok, tok, halo8(0), col(0), col(1), tok, halo8(0), tok, tok,
                  pl.BlockSpec((tm, 2 * D), lambda i: (rev(i), 0)),
                  pl.BlockSpec((16, D), lambda i: (0, 0)), wspec, wspec, wspec],
        out_specs=[pl.BlockSpec((tm, D_IN), lambda i: (rev(i), 0)), wspec, wspec, wspec,
                   pl.BlockSpec((16, D), lambda i: (0, 0))],
        out_shape=[jax.ShapeDtypeStruct((s, D_IN), BF16), wshape, wshape, wshape,
                   jax.ShapeDtypeStruct((16, D), F32)],
        scratch_shapes=[pltpu.VMEM((8, D), F32), pltpu.VMEM((8, D), F32), pltpu.VMEM((HALO_U, D), F32)],
        compiler_params=_params(("arbitrary",)),
    )(dza, dpooled, proj, proj, proj, xr, hr, hr, p, dgates, vecs, w_rg_a, w_rg_x, w_pool)


def _proj_bwd(dproj, x, dx2, modr, vecs, w_in):
    s = x.shape[0]
    tm, tk = min(TM_PROJ, s), 1024
    nk = D_IN // tk

    def body(dp_ref, x_ref, dx2_ref, mod_ref, vec_ref, w_ref, gx_ref, small_ref, acc_ref):
        i = pl.program_id(0)
        k = pl.program_id(1)

        @pl.when(jnp.logical_and(i == 0, k == 0))
        def _():
            small_ref[...] = jnp.zeros_like(small_ref)

        part = _dot_nt(dp_ref[...], w_ref[...])

        @pl.when(k == 0)
        def _():
            acc_ref[...] = part

        @pl.when(k > 0)
        def _():
            acc_ref[...] += part

        @pl.when(k == nk - 1)
        def _():
            dh1 = acc_ref[...]
            xv = x_ref[...]
            r1 = lax.rsqrt(jnp.mean(xv * xv, axis=-1, keepdims=True) + EPS)
            xn1 = xv * r1
            gain = vec_ref[V_G1:V_G1 + 1, :] * (1.0 + mod_ref[M_SC1:M_SC1 + 1, :])
            dxn1 = dh1 * gain
            gx_ref[...] = dx2_ref[...] + r1 * (dxn1 - xn1 * jnp.mean(dxn1 * xn1, axis=-1, keepdims=True))
            small_ref[0:1, :] += jnp.sum(dh1, axis=0, keepdims=True)
            small_ref[1:2, :] += jnp.sum(dh1 * xn1, axis=0, keepdims=True)

    tok = pl.BlockSpec((tm, D), lambda i, k: (i, 0))
    return pl.pallas_call(
        body, name="proj_bwd", grid=(s // tm, nk),
        in_specs=[pl.BlockSpec((tm, tk), lambda i, k: (i, k)), tok, tok,
                  pl.BlockSpec((8, D), lambda i, k: (0, 0)), pl.BlockSpec((16, D), lambda i, k: (0, 0)),
                  pl.BlockSpec((D, tk), lambda i, k: (0, k))],
        out_specs=[tok, pl.BlockSpec((8, D), lambda i, k: (0, 0))],
        out_shape=[jax.ShapeDtypeStruct((s, D), F32), jax.ShapeDtypeStruct((8, D), F32)],
        scratch_shapes=[pltpu.VMEM((tm, D), F32)],
        compiler_params=_params(("arbitrary", "arbitrary")),
    )(dproj, x, dx2, modr, vecs, w_in)


def _wgrad(a, b, name, square_a=False):
    s, ka = a.shape
    n = b.shape[1]
    tka, tn, ts = 1024, 1024, min(TS_WGRAD, s)
    ns = s // ts

    def body(a_ref, b_ref, out_ref, acc_ref):
        t = pl.program_id(2)
        av = a_ref[...]
        if square_a:
            af = av.astype(F32)
            av = (af * af).astype(BF16)
        part = _dot_tn(av, b_ref[...])

        @pl.when(t == 0)
        def _():
            acc_ref[...] = part

        @pl.when(t > 0)
        def _():
            acc_ref[...] += part

        @pl.when(t == ns - 1)
        def _():
            out_ref[...] = acc_ref[...].astype(BF16)

    return pl.pallas_call(
        body, name=name, grid=(ka // tka, n // tn, ns),
        in_specs=[pl.BlockSpec((ts, tka), lambda i, j, t: (t, i)),
                  pl.BlockSpec((ts, tn), lambda i, j, t: (t, j))],
        out_specs=pl.BlockSpec((tka, tn), lambda i, j, t: (i, j)),
        out_shape=jax.ShapeDtypeStruct((ka, n), BF16),
        scratch_shapes=[pltpu.VMEM((tka, tn), F32)],
        compiler_params=_params(("parallel", "parallel", "arbitrary")),
    )(a, b)


def _window(ref, kind, idx, size):
    start = pl.multiple_of(idx * size, size)
    if kind == 0:
        return ref.at[pl.ds(start, size)]
    if kind == 1:
        return ref.at[:, pl.ds(start, size)]
    return ref.at[:, :, pl.ds(start, size)]


def _mesh_place():
    x, y, c = lax.axis_index("x"), lax.axis_index("y"), lax.axis_index("c")
    return x, y, c, 4 * x + 2 * y + c


def _peer(x, y, c, q):
    px = 1 - x if q & 4 else x
    py = 1 - y if q & 2 else y
    pc = 1 - c if q & 1 else c
    return (px, py, pc), 4 * px + 2 * py + pc


def _all_gather(shards, kinds, name):
    n = len(shards)
    full_shapes = []
    for sh, kind in zip(shards, kinds):
        dims = list(sh.shape)
        dims[kind] *= N_DEV
        full_shapes.append(jax.ShapeDtypeStruct(tuple(dims), sh.dtype))

    def body(*refs):
        ins, outs = refs[:n], refs[n:2 * n]
        send_sems, recv_sems, local_sems = refs[2 * n:]
        x, y, c, me = _mesh_place()
        sends, recvs, locals_ = [], [], []
        for k in range(n):
            size = shards[k].shape[kinds[k]]
            mine = _window(outs[k], kinds[k], me, size)
            lc = pltpu.make_async_copy(ins[k], mine, local_sems.at[k])
            lc.start()
            locals_.append(lc)
            for q in range(1, N_DEV):
                peer, peer_idx = _peer(x, y, c, q)
                cp = pltpu.make_async_remote_copy(
                    src_ref=ins[k], dst_ref=mine, send_sem=send_sems.at[k, q], recv_sem=recv_sems.at[k, q],
                    device_id=peer, device_id_type=MESH)
                cp.start()
                sends.append(cp)
                recvs.append(pltpu.make_async_remote_copy(
                    src_ref=ins[k], dst_ref=_window(outs[k], kinds[k], peer_idx, size),
                    send_sem=send_sems.at[k, q], recv_sem=recv_sems.at[k, q],
                    device_id=peer, device_id_type=MESH))
        for cp in recvs:
            cp.wait_recv()
        for cp in sends:
            cp.wait_send()
        for lc in locals_:
            lc.wait()

    any_spec = pl.BlockSpec(memory_space=pl.ANY)
    return pl.pallas_call(
        body, name=name,
        in_specs=[any_spec] * n, out_specs=[any_spec] * n, out_shape=full_shapes,
        scratch_shapes=[pltpu.SemaphoreType.DMA((n, N_DEV)), pltpu.SemaphoreType.DMA((n, N_DEV)),
                        pltpu.SemaphoreType.DMA((n,))],
    )(*shards)


def _scatter_partials(fulls, kinds, name):
    n = len(fulls)
    sizes = [f.shape[kind] // N_DEV for f, kind in zip(fulls, kinds)]
    slot_shapes = []
    for f, kind, size in zip(fulls, kinds, sizes):
        dims = list(f.shape)
        dims[kind] = size
        slot_shapes.append(jax.ShapeDtypeStruct((N_DEV, *dims), f.dtype))

    def body(*refs):
        ins, outs = refs[:n], refs[n:2 * n]
        send_sems, recv_sems, local_sems = refs[2 * n:]
        x, y, c, me = _mesh_place()
        sends, recvs, locals_ = [], [], []
        for k in range(n):
            lc = pltpu.make_async_copy(_window(ins[k], kinds[k], me, sizes[k]), outs[k].at[me], local_sems.at[k])
            lc.start()
            locals_.append(lc)
            for q in range(1, N_DEV):
                peer, peer_idx = _peer(x, y, c, q)
                src = _window(ins[k], kinds[k], peer_idx, sizes[k])
                cp = pltpu.make_async_remote_copy(
                    src_ref=src, dst_ref=outs[k].at[me], send_sem=send_sems.at[k, q], recv_sem=recv_sems.at[k, q],
                    device_id=peer, device_id_type=MESH)
                cp.start()
                sends.append(cp)
                recvs.append(pltpu.make_async_remote_copy(
                    src_ref=src, dst_ref=outs[k].at[peer_idx], send_sem=send_sems.at[k, q],
                    recv_sem=recv_sems.at[k, q], device_id=peer, device_id_type=MESH))
        for cp in recvs:
            cp.wait_recv()
        for cp in sends:
            cp.wait_send()
        for lc in locals_:
            lc.wait()

    any_spec = pl.BlockSpec(memory_space=pl.ANY)
    return pl.pallas_call(
        body, name=name,
        in_specs=[any_spec] * n, out_specs=[any_spec] * n, out_shape=slot_shapes,
        scratch_shapes=[pltpu.SemaphoreType.DMA((n, N_DEV)), pltpu.SemaphoreType.DMA((n, N_DEV)),
                        pltpu.SemaphoreType.DMA((n,))],
    )(*fulls)


def _silu(c):
    return c * _sigmoid(c)


def _ada_fwd(c_all, w_ada, b_ada_cols):
    def body(c_ref, w_ref, b_ref, out_ref):
        out_ref[...] = jnp.dot(_silu(c_ref[...]), w_ref[...], preferred_element_type=F32,
                               precision=lax.Precision.HIGHEST) + b_ref[...]

    return pl.pallas_call(
        body, name="ada_fwd", out_shape=jax.ShapeDtypeStruct((N_DEV, w_ada.shape[1]), F32),
    )(c_all, w_ada, b_ada_cols)


def _adam(w, g, m, v):
    m = ADAM_B1 * m + (1.0 - ADAM_B1) * g
    v = ADAM_B2 * v + (1.0 - ADAM_B2) * (g * g)
    m_hat = m / (1.0 - ADAM_B1 ** ADAM_STEP)
    v_hat = v / (1.0 - ADAM_B2 ** ADAM_STEP)
    delta = -ADAM_LR * (m_hat / (jnp.sqrt(v_hat) + ADAM_EPS) + ADAM_WD * w)
    return delta, m, v


def _ada_bwd_adam(c_all, dmod_cols, w, m, v):
    def body(c_ref, d_ref, w_ref, m_ref, v_ref, g_ref, delta_ref, nm_ref, nv_ref):
        g = lax.dot_general(_silu(c_ref[...]), d_ref[...], (((0,), (0,)), ((), ())),
                            preferred_element_type=F32, precision=lax.Precision.HIGHEST)
        g_ref[...] = g
        delta_ref[...], nm_ref[...], nv_ref[...] = _adam(w_ref[...], g, m_ref[...], v_ref[...])

    sd = jax.ShapeDtypeStruct(w.shape, F32)
    return pl.pallas_call(body, name="ada_bwd_adam", out_shape=[sd] * 4,
                          compiler_params=pltpu.CompilerParams(vmem_limit_bytes=V7X_VMEM_LIMIT),
                          )(c_all, dmod_cols, w, m, v)


def _sum_slots_adam(slots, w, m, v, name):
    r, cdim = w.shape
    tr = min(r, 128)

    def body(s_ref, w_ref, m_ref, v_ref, g_ref, delta_ref, nm_ref, nv_ref):
        g = s_ref[0].astype(F32)
        for p in range(1, N_DEV):
            g = g + s_ref[p].astype(F32)
        g_ref[...] = g
        delta_ref[...], nm_ref[...], nv_ref[...] = _adam(w_ref[...], g, m_ref[...], v_ref[...])

    blk = pl.BlockSpec((tr, cdim), lambda i: (i, 0))
    sd = jax.ShapeDtypeStruct((r, cdim), F32)
    return pl.pallas_call(
        body, name=name, grid=(r // tr,),
        in_specs=[pl.BlockSpec((N_DEV, tr, cdim), lambda i: (0, i, 0)), blk, blk, blk],
        out_specs=[blk] * 4, out_shape=[sd] * 4,
        compiler_params=_params(("parallel",)),
    )(slots, w, m, v)


N_SMALL = 40
N_PACK = 24


def _small_finish(gathered, mod_all, vecs, w_pack, m_pack, v_pack):
    def body(g_ref, mod_ref, vec_ref, w_ref, m_ref, v_ref,
             gp_ref, delta_ref, nm_ref, nv_ref, dmod_ref, loss_ref):
        g1 = vec_ref[V_G1:V_G1 + 1, :]
        g2 = vec_ref[V_G2:V_G2 + 1, :]
        zero = jnp.zeros((1, D), F32)
        dg1, dg2, dgf, loss_lanes = zero, zero, zero, zero
        mixer = jnp.zeros((16, D), F32)
        db_ada = jnp.zeros((8, D), F32)
        for b in range(N_DEV):
            gb = g_ref[b]
            mod = mod_ref[b]
            q1 = gb[33:34]
            q2 = gb[9:10]
            dmod = jnp.concatenate([gb[32:33], q1 * g1, gb[10:11], gb[8:9], q2 * g2, gb[1:2], zero, zero], axis=0)
            dmod_ref[b] = dmod
            db_ada = db_ada + dmod
            dg1 = dg1 + q1 * (1.0 + mod[M_SC1:M_SC1 + 1])
            dg2 = dg2 + q2 * (1.0 + mod[M_SC2:M_SC2 + 1])
            dgf = dgf + gb[0:1]
            loss_lanes = loss_lanes + gb[2:3]
            mixer = mixer + gb[16:32]
        d_a_param = mixer[7:8] * _sigmoid(vec_ref[V_A_PARAM:V_A_PARAM + 1, :])
        g = jnp.concatenate([dg1, dg2, mixer[4:5], mixer[5:6], mixer[6:7], d_a_param, mixer[8:9], mixer[9:10],
                             dgf, db_ada[0:6], mixer[0:4]] + [zero] * (N_PACK - 19), axis=0)
        gp_ref[...] = g
        delta_ref[...], nm_ref[...], nv_ref[...] = _adam(w_ref[...], g, m_ref[...], v_ref[...])
        loss_ref[...] = jnp.broadcast_to(jnp.sum(loss_lanes, axis=1, keepdims=True), (8, 128))

    sd = jax.ShapeDtypeStruct((N_PACK, D), F32)
    return pl.pallas_call(
        body, name="small_finish",
        out_shape=[sd, sd, sd, sd, jax.ShapeDtypeStruct((N_DEV, 8, D), F32), jax.ShapeDtypeStruct((8, 128), F32)],
    )(gathered, mod_all, vecs, w_pack, m_pack, v_pack)


def _pad_rows(a, rows):
    return jnp.concatenate([a, jnp.zeros((rows - a.shape[0], a.shape[1]), a.dtype)], axis=0)


def kernel(x, c, norm_mix_g, norm_mlp_g, w_ada, b_ada, w_in, conv_w, conv_b, w_rg_a, b_rg_a, w_rg_x, b_rg_x, a_param, w_branch_a, w_pool, b_pool, pool_scale, w_branch_b, w_out, w_up, w_down, final_g, loss_target, m_norm_mix_g, m_norm_mlp_g, m_w_ada, m_b_ada, m_w_in, m_conv_w, m_conv_b, m_w_rg_a, m_b_rg_a, m_w_rg_x, m_b_rg_x, m_a_param, m_w_branch_a, m_w_pool, m_b_pool, m_pool_scale, m_w_branch_b, m_w_out, m_w_up, m_w_down, m_final_g, v_norm_mix_g, v_norm_mlp_g, v_w_ada, v_b_ada, v_w_in, v_conv_w, v_conv_b, v_w_rg_a, v_b_rg_a, v_w_rg_x, v_b_rg_x, v_a_param, v_w_branch_a, v_w_pool, v_b_pool, v_pool_scale, v_w_branch_b, v_w_out, v_w_up, v_w_down, v_final_g):
    me = 4 * lax.axis_index("x") + 2 * lax.axis_index("y") + lax.axis_index("c")
    s = x.shape[1]
    x2d = x.reshape(s, D)
    target = loss_target.reshape(s, D)
    n_ada = w_ada.shape[2]

    sharded = dict(w_in=(w_in[0], 1), w_up=(w_up[0], 1), w_down=(w_down[0], 0), w_branch_a=(w_branch_a[0], 0),
                   w_branch_b=(w_branch_b[0], 0), w_out=(w_out[0], 0), w_rg_a=(w_rg_a[0], 1), w_rg_x=(w_rg_x[0], 1),
                   w_pool=(w_pool[0], 1))
    names = list(sharded)
    gathered = _all_gather(
        [sharded[k][0].astype(BF16) for k in names] + [_pad_rows(conv_w[0], 8), _pad_rows(c, 8)],
        [sharded[k][1] for k in names] + [1, 0], "gather_weights")
    wg = dict(zip(names, gathered[:len(names)]))
    conv_w_full = gathered[-2][:4]
    c_all = gathered[-1].reshape(N_DEV, 8, D)[:, 0, :]

    b_ada_cols = lax.dynamic_slice(b_ada, (0, me * n_ada), (1, n_ada))
    mod_part = _ada_fwd(c_all, w_ada[0], b_ada_cols)
    mod_parts, = _all_gather([mod_part], [0], "gather_mod")
    mod_all = jnp.transpose(mod_parts.reshape(N_DEV, N_DEV, n_ada), (1, 0, 2)).reshape(N_DEV, 6, D)
    mod_all = jnp.concatenate([mod_all, jnp.zeros((N_DEV, 2, D), F32)], axis=1)
    modr = lax.dynamic_index_in_dim(mod_all, me, 0, keepdims=False)

    vecs = jnp.concatenate([conv_w_full, conv_b, b_rg_a, b_rg_x, a_param, b_pool, pool_scale,
                            norm_mix_g, norm_mlp_g, final_g.reshape(1, D), jnp.zeros((3, D), F32)], axis=0)

    proj, h1 = _proj_fwd(x2d, modr, vecs, wg["w_in"])
    xr, hr, za, p, pooled = _mix_fwd(proj, vecs, wg["w_rg_a"], wg["w_rg_x"], wg["w_pool"])
    ba, bb, merged, o, x2, h2 = _branch_fwd(za, pooled, proj, x2d, modr, vecs,
                                            wg["w_branch_a"], wg["w_branch_b"], wg["w_out"])
    ru, dx3, d_dn, small_f = _mlp_fwd(h2, x2, target, modr, vecs, wg["w_up"], wg["w_down"])

    dup, dx2, do, small_m = _mlp_bwd(d_dn, ru, x2, dx3, o, modr, vecs, wg["w_up"], wg["w_down"])
    dba, dbb, dgates, dza, dpooled = _branch_bwd(do, proj, ba, bb, wg["w_branch_a"], wg["w_branch_b"], wg["w_out"])
    dproj, dw_rg_a, dw_rg_x, dw_pool, small_x = _mix_bwd(dza, dpooled, proj, xr, hr, p, dgates, vecs,
                                                         wg["w_rg_a"], wg["w_rg_x"], wg["w_pool"])
    grad_x, small_p = _proj_bwd(dproj, x2d, dx2, modr, vecs, wg["w_in"])

    partial = dict(
        w_in=_wgrad(h1, dproj, "wgrad_in"),
        w_up=_wgrad(h2, dup, "wgrad_up"),
        w_down=_wgrad(ru, d_dn, "wgrad_down", square_a=True),
        w_branch_a=_wgrad(za, dba, "wgrad_branch_a"),
        w_branch_b=_wgrad(pooled, dbb, "wgrad_branch_b"),
        w_out=_wgrad(merged, do, "wgrad_out"),
        w_rg_a=dw_rg_a.astype(BF16), w_rg_x=dw_rg_x.astype(BF16), w_pool=dw_pool.astype(BF16))

    slots = dict(zip(names, _scatter_partials([partial[k] for k in names], [sharded[k][1] for k in names],
                                              "scatter_grads")))
    small = jnp.concatenate([small_f, small_m, small_x, small_p], axis=0)
    small_all, = _all_gather([small], [0], "gather_small")
    small_all = small_all.reshape(N_DEV, N_SMALL, D)

    locals_ = dict(w_in=(w_in, m_w_in, v_w_in), w_up=(w_up, m_w_up, v_w_up), w_down=(w_down, m_w_down, v_w_down),
                   w_branch_a=(w_branch_a, m_w_branch_a, v_w_branch_a),
                   w_branch_b=(w_branch_b, m_w_branch_b, v_w_branch_b), w_out=(w_out, m_w_out, v_w_out),
                   w_rg_a=(w_rg_a, m_w_rg_a, v_w_rg_a), w_rg_x=(w_rg_x, m_w_rg_x, v_w_rg_x),
                   w_pool=(w_pool, m_w_pool, v_w_pool))
    res = {}
    for k in names:
        w, m, v = locals_[k]
        shape2d = (-1, w.shape[-1])
        sl = slots[k].reshape(N_DEV, *w.reshape(shape2d).shape)
        outs = _sum_slots_adam(sl, w.reshape(shape2d), m.reshape(shape2d), v.reshape(shape2d), "adam_" + k)
        res[k] = [t.reshape(w.shape) for t in outs]

    def embed(cw):
        return lax.dynamic_update_slice(jnp.zeros((4, D), F32), cw[0], (0, me * (D // N_DEV)))

    def pack(ng, nl, cb, bra, brx, ap, bp, ps, fg, ba_, cw):
        return jnp.concatenate([ng, nl, cb, bra, brx, ap, bp, ps, fg.reshape(1, D), ba_.reshape(6, D), embed(cw),
                                jnp.zeros((N_PACK - 19, D), F32)], axis=0)

    w_pack = pack(norm_mix_g, norm_mlp_g, conv_b, b_rg_a, b_rg_x, a_param, b_pool, pool_scale, final_g, b_ada, conv_w)
    m_pack = pack(m_norm_mix_g, m_norm_mlp_g, m_conv_b, m_b_rg_a, m_b_rg_x, m_a_param, m_b_pool, m_pool_scale,
                  m_final_g, m_b_ada, m_conv_w)
    v_pack = pack(v_norm_mix_g, v_norm_mlp_g, v_conv_b, v_b_rg_a, v_b_rg_x, v_a_param, v_b_pool, v_pool_scale,
                  v_final_g, v_b_ada, v_conv_w)
    g_pack, d_pack, nm_pack, nv_pack, dmod_all, loss_tile = _small_finish(small_all, mod_all, vecs,
                                                                          w_pack, m_pack, v_pack)
    dmod_cols = lax.dynamic_slice(dmod_all[:, :6, :].reshape(N_DEV, 6 * D), (0, me * n_ada), (N_DEV, n_ada))
    res["w_ada"] = [t.reshape(w_ada.shape) for t in _ada_bwd_adam(c_all, dmod_cols, w_ada[0], m_w_ada[0], v_w_ada[0])]

    def unpack(pk):
        row = lambda r: pk[r:r + 1]
        cw = lax.dynamic_slice(pk[15:19], (0, me * (D // N_DEV)), (4, D // N_DEV)).reshape(conv_w.shape)
        return dict(norm_mix_g=row(0), norm_mlp_g=row(1), conv_b=row(2), b_rg_a=row(3), b_rg_x=row(4), a_param=row(5),
                    b_pool=row(6), pool_scale=row(7), final_g=pk[8], b_ada=pk[9:15].reshape(1, 6 * D), conv_w=cw)

    small_res = [unpack(pk) for pk in (g_pack, d_pack, nm_pack, nv_pack)]
    order = ["norm_mix_g", "norm_mlp_g", "w_ada", "b_ada", "w_in", "conv_w", "conv_b", "w_rg_a", "b_rg_a", "w_rg_x",
             "b_rg_x", "a_param", "w_branch_a", "w_pool", "b_pool", "pool_scale", "w_branch_b", "w_out", "w_up",
             "w_down", "final_g"]
    outs = [loss_tile[0, 0], grad_x.reshape(x.shape)]
    for which in range(4):
        for k in order:
            outs.append(res[k][which] if k in res else small_res[which][k])
    return tuple(outs)
```

```python
import functools

import jax
import jax.numpy as jnp
from jax import lax
from jax.experimental import pallas as pl
from jax.experimental.pallas import tpu as pltpu

F32 = jnp.float32
BF16 = jnp.bfloat16
MESH = pl.DeviceIdType.MESH

N_DEV = 8
D = 1024
N_GROUPS = 4
GW = D // N_GROUPS
D_IN = 5 * D
D_FF = 4 * D
POOL_WINDOWS = (2, 4, 8, 16)
HALO_X = 8
HALO_U = 16
EPS = 1e-6
C_RG = 8.0
ADAM_LR, ADAM_B1, ADAM_B2, ADAM_EPS, ADAM_WD, ADAM_STEP = 0.001, 0.9, 0.999, 1e-08, 0.01, 10

V7X_VMEM_LIMIT = 56 * 1024 * 1024

V_CONV_W, V_CONV_B, V_B_RG_A, V_B_RG_X, V_A_PARAM, V_B_POOL, V_POOL_SCALE, V_G1, V_G2, V_GF = 0, 4, 5, 6, 7, 8, 9, 10, 11, 12
M_SH1, M_SC1, M_GT1, M_SH2, M_SC2, M_GT2 = 0, 1, 2, 3, 4, 5

TM_PROJ = 512
TM_MIX = 256
TM_BRANCH = 256
TM_MLP = 512
TS_WGRAD = 512


def _params(semantics):
    return pltpu.CompilerParams(dimension_semantics=semantics, vmem_limit_bytes=V7X_VMEM_LIMIT)


def _dot(a, b):
    return jnp.dot(a, b, preferred_element_type=F32)


def _dot_nt(a, b):
    return lax.dot_general(a, b, (((1,), (1,)), ((), ())), preferred_element_type=F32)


def _dot_tn(a, b):
    return lax.dot_general(a, b, (((0,), (0,)), ((), ())), preferred_element_type=F32)


def _sigmoid(x):
    return 1.0 / (1.0 + jnp.exp(-x))


def _gelu_and_grad(x):
    k = 0.7978845608028654
    x2 = x * x
    t = jnp.tanh(k * (x + 0.044715 * x * x2))
    g = 0.5 * x * (1.0 + t)
    dg = 0.5 * (1.0 + t) + 0.5 * x * (1.0 - t * t) * (k * (1.0 + 3.0 * 0.044715 * x2))
    return g, dg


def _softplus(a):
    e = jnp.exp(-jnp.abs(a))
    u = 1.0 + e
    log1p_e = jnp.where(u == 1.0, e, jnp.log(u) * e / jnp.where(u == 1.0, 1.0, u - 1.0))
    return jnp.maximum(a, 0.0) + log1p_e


def _neg_expm1(z):
    series = -(z * (1.0 + z * (0.5 + z * (1.0 / 6.0 + z * (1.0 / 24.0 + z * (1.0 / 120.0))))))
    return jnp.where(z > -0.1, series, 1.0 - jnp.exp(z))


def _shift_down(x, k):
    return pltpu.roll(x, k, 0)


def _shift_up(x, k):
    return pltpu.roll(x, x.shape[0] - k, 0)


def _rglru_gates(xr, w_a, w_x, b_a, b_x, a_param, is_t0):
    xb = xr.astype(BF16)
    ra = _sigmoid(_dot(xb, w_a) + b_a)
    ri = _sigmoid(_dot(xb, w_x) + b_x)
    sp = _softplus(a_param)
    log_a = (-C_RG) * ra * sp
    a = jnp.exp(log_a)
    mult = jnp.where(is_t0, 1.0, jnp.sqrt(_neg_expm1(2.0 * log_a)))
    return ra, ri, sp, a, mult


def _conv_taps(x_ext):
    return [_shift_down(x_ext, 3 - j)[HALO_X:] if j < 3 else x_ext[HALO_X:] for j in range(4)]


def _proj_fwd(x, modr, vecs, w_in):
    s = x.shape[0]
    tm, tn = min(TM_PROJ, s), 640

    def body(x_ref, mod_ref, vec_ref, w_ref, proj_ref, h1_ref, h1_scr):
        j = pl.program_id(1)

        @pl.when(j == 0)
        def _():
            xv = x_ref[...]
            r = lax.rsqrt(jnp.mean(xv * xv, axis=-1, keepdims=True) + EPS)
            gain = vec_ref[V_G1:V_G1 + 1, :] * (1.0 + mod_ref[M_SC1:M_SC1 + 1, :])
            h = (xv * r * gain + mod_ref[M_SH1:M_SH1 + 1, :]).astype(BF16)
            h1_scr[...] = h
            h1_ref[...] = h

        proj_ref[...] = _dot(h1_scr[...], w_ref[...])

    return pl.pallas_call(
        body, name="proj_fwd", grid=(s // tm, D_IN // tn),
        in_specs=[pl.BlockSpec((tm, D), lambda i, j: (i, 0)),
                  pl.BlockSpec((8, D), lambda i, j: (0, 0)),
                  pl.BlockSpec((16, D), lambda i, j: (0, 0)),
                  pl.BlockSpec((D, tn), lambda i, j: (0, j))],
        out_specs=[pl.BlockSpec((tm, tn), lambda i, j: (i, j)),
                   pl.BlockSpec((tm, D), lambda i, j: (i, 0))],
        out_shape=[jax.ShapeDtypeStruct((s, D_IN), F32), jax.ShapeDtypeStruct((s, D), BF16)],
        scratch_shapes=[pltpu.VMEM((tm, D), BF16)],
        compiler_params=_params(("parallel", "arbitrary")),
    )(x, modr, vecs, w_in)


def _mix_fwd(proj, vecs, w_rg_a, w_rg_x, w_pool):
    s = proj.shape[0]
    tm = min(TM_MIX, s)
    nb = s // tm

    def body(xh_ref, x_ref, y_ref, uh_ref, u_ref, vec_ref, wa_ref, wx_ref, wp_ref,
             xr_ref, hr_ref, za_ref, p_ref, pooled_ref, carry_ref):
        i = pl.program_id(0)
        first = i == 0

        @pl.when(first)
        def _():
            carry_ref[...] = jnp.zeros_like(carry_ref)

        row = lax.broadcasted_iota(jnp.int32, (tm, GW), 0)
        is_t0 = jnp.logical_and(first, row == 0)
        t_glob = (row + i * tm + 1).astype(F32)
        for g in range(N_GROUPS):
            cs = slice(g * GW, (g + 1) * GW)
            vec = vec_ref[:, cs]
            xh = jnp.where(first, 0.0, xh_ref[:, cs])
            taps = _conv_taps(jnp.concatenate([xh, x_ref[:, cs]], axis=0))
            xr = vec[V_CONV_B:V_CONV_B + 1]
            for j in range(4):
                xr = xr + vec[V_CONV_W + j:V_CONV_W + j + 1] * taps[j]
            xr_ref[:, cs] = xr
            _, ri, _, a, mult = _rglru_gates(
                xr, wa_ref[g], wx_ref[g], vec[V_B_RG_A:V_B_RG_A + 1], vec[V_B_RG_X:V_B_RG_X + 1],
                vec[V_A_PARAM:V_A_PARAM + 1], is_t0)
            b = xr * ri * mult
            b = b + jnp.where(row == 0, a * carry_ref[7:8, cs], 0.0)
            k = 1
            while k < tm:
                b = b + a * jnp.where(row >= k, _shift_down(b, k), 0.0)
                if 2 * k < tm:
                    a = a * _shift_down(a, k)
                k *= 2
            hr_ref[:, cs] = b
            carry_ref[:, cs] = b[tm - 8:, :]
            ga, _ = _gelu_and_grad(y_ref[:, cs])
            za_ref[:, cs] = (ga * b).astype(BF16)
            uh = jnp.where(first, 0.0, uh_ref[:, cs])
            sm = jnp.concatenate([uh, u_ref[:, cs]], axis=0)
            k = 1
            while k < POOL_WINDOWS[g]:
                sm = sm + _shift_down(sm, k)
                k *= 2
            cnt = jnp.minimum(t_glob, float(POOL_WINDOWS[g]))
            p = (sm[HALO_U:] / cnt - u_ref[:, cs]).astype(BF16)
            p_ref[:, cs] = p
            pb = _dot(p, wp_ref[g]) + vec[V_B_POOL:V_B_POOL + 1]
            pooled_ref[:, cs] = (pb * vec[V_POOL_SCALE:V_POOL_SCALE + 1]).astype(BF16)

    col = lambda k: (lambda i: (i, k))
    wspec = pl.BlockSpec((N_GROUPS, GW, GW), lambda i: (0, 0, 0))
    return pl.pallas_call(
        body, name="mix_fwd", grid=(nb,),
        in_specs=[pl.BlockSpec((HALO_X, D), lambda i: (jnp.maximum(i * (tm // HALO_X) - 1, 0), 0)),
                  pl.BlockSpec((tm, D), col(0)),
                  pl.BlockSpec((tm, D), col(1)),
                  pl.BlockSpec((HALO_U, D), lambda i: (jnp.maximum(i * (tm // HALO_U) - 1, 0), 2)),
                  pl.BlockSpec((tm, D), col(2)),
                  pl.BlockSpec((16, D), lambda i: (0, 0)),
                  wspec, wspec, wspec],
        out_specs=[pl.BlockSpec((tm, D), lambda i: (i, 0))] * 5,
        out_shape=[jax.ShapeDtypeStruct((s, D), F32), jax.ShapeDtypeStruct((s, D), F32),
                   jax.ShapeDtypeStruct((s, D), BF16), jax.ShapeDtypeStruct((s, D), BF16),
                   jax.ShapeDtypeStruct((s, D), BF16)],
        scratch_shapes=[pltpu.VMEM((8, D), F32)],
        compiler_params=_params(("arbitrary",)),
    )(proj, proj, proj, proj, proj, vecs, w_rg_a, w_rg_x, w_pool)


def _branch_fwd(za, pooled, proj, x, modr, vecs, w_a, w_b, w_out):
    s = x.shape[0]
    tm = min(TM_BRANCH, s)

    def body(za_ref, pooled_ref, ga_ref, gb_ref, x_ref, mod_ref, vec_ref, wa_ref, wb_ref, wo_ref,
             ba_ref, bb_ref, merged_ref, o_ref, x2_ref, h2_ref):
        ba = _dot(za_ref[...], wa_ref[...])
        bb = _dot(pooled_ref[...], wb_ref[...])
        ba_ref[...] = ba.astype(BF16)
        bb_ref[...] = bb.astype(BF16)
        merged = (_sigmoid(ga_ref[...]) * ba + _sigmoid(gb_ref[...]) * bb).astype(BF16)
        merged_ref[...] = merged
        o = _dot(merged, wo_ref[...])
        o_ref[...] = o.astype(BF16)
        x2 = x_ref[...] + mod_ref[M_GT1:M_GT1 + 1, :] * o
        x2_ref[...] = x2
        r = lax.rsqrt(jnp.mean(x2 * x2, axis=-1, keepdims=True) + EPS)
        gain = vec_ref[V_G2:V_G2 + 1, :] * (1.0 + mod_ref[M_SC2:M_SC2 + 1, :])
        h2_ref[...] = (x2 * r * gain + mod_ref[M_SH2:M_SH2 + 1, :]).astype(BF16)

    tok = pl.BlockSpec((tm, D), lambda i: (i, 0))
    wspec = pl.BlockSpec((D, D), lambda i: (0, 0))
    sd = lambda dt: jax.ShapeDtypeStruct((s, D), dt)
    return pl.pallas_call(
        body, name="branch_fwd", grid=(s // tm,),
        in_specs=[tok, tok,
                  pl.BlockSpec((tm, D), lambda i: (i, 3)), pl.BlockSpec((tm, D), lambda i: (i, 4)),
                  tok, pl.BlockSpec((8, D), lambda i: (0, 0)), pl.BlockSpec((16, D), lambda i: (0, 0)),
                  wspec, wspec, wspec],
        out_specs=[tok] * 6,
        out_shape=[sd(BF16), sd(BF16), sd(BF16), sd(BF16), sd(F32), sd(BF16)],
        compiler_params=_params(("parallel",)),
    )(za, pooled, proj, proj, x, modr, vecs, w_a, w_b, w_out)


def _mlp_fwd(h2, x2, target, modr, vecs, w_up, w_down):
    s = x2.shape[0]
    tm, tf = min(TM_MLP, s), 512
    nj = D_FF // tf

    def body(h2_ref, x2_ref, tgt_ref, mod_ref, vec_ref, wu_ref, wd_ref,
             ru_ref, dx3_ref, ddn_ref, small_ref, acc_ref):
        i = pl.program_id(0)
        j = pl.program_id(1)

        @pl.when(jnp.logical_and(i == 0, j == 0))
        def _():
            small_ref[...] = jnp.zeros_like(small_ref)

        ru = jnp.maximum(_dot(h2_ref[...], wu_ref[...]), 0.0)
        ru_ref[...] = ru.astype(BF16)
        part = _dot((ru * ru).astype(BF16), wd_ref[...])

        @pl.when(j == 0)
        def _():
            acc_ref[...] = part

        @pl.when(j > 0)
        def _():
            acc_ref[...] += part

        @pl.when(j == nj - 1)
        def _():
            dn = acc_ref[...]
            gt2 = mod_ref[M_GT2:M_GT2 + 1, :]
            gf = vec_ref[V_GF:V_GF + 1, :]
            x3 = x2_ref[...] + gt2 * dn
            r3 = lax.rsqrt(jnp.mean(x3 * x3, axis=-1, keepdims=True) + EPS)
            n3 = x3 * r3
            err = n3 * gf - tgt_ref[...]
            dy = err * (1.0 / D)
            dn3 = dy * gf
            dx3 = r3 * (dn3 - n3 * jnp.mean(dn3 * n3, axis=-1, keepdims=True))
            dx3_ref[...] = dx3
            ddn_ref[...] = (dx3 * gt2).astype(BF16)
            small_ref[0:1, :] += jnp.sum(dy * n3, axis=0, keepdims=True)
            small_ref[1:2, :] += jnp.sum(dx3 * dn, axis=0, keepdims=True)
            small_ref[2:3, :] += (0.5 / D) * jnp.sum(err * err, axis=0, keepdims=True)

    tok = pl.BlockSpec((tm, D), lambda i, j: (i, 0))
    return pl.pallas_call(
        body, name="mlp_fwd", grid=(s // tm, nj),
        in_specs=[tok, tok, tok,
                  pl.BlockSpec((8, D), lambda i, j: (0, 0)), pl.BlockSpec((16, D), lambda i, j: (0, 0)),
                  pl.BlockSpec((D, tf), lambda i, j: (0, j)), pl.BlockSpec((tf, D), lambda i, j: (j, 0))],
        out_specs=[pl.BlockSpec((tm, tf), lambda i, j: (i, j)), tok, tok,
                   pl.BlockSpec((8, D), lambda i, j: (0, 0))],
        out_shape=[jax.ShapeDtypeStruct((s, D_FF), BF16), jax.ShapeDtypeStruct((s, D), F32),
                   jax.ShapeDtypeStruct((s, D), BF16), jax.ShapeDtypeStruct((8, D), F32)],
        scratch_shapes=[pltpu.VMEM((tm, D), F32)],
        compiler_params=_params(("arbitrary", "arbitrary")),
    )(h2, x2, target, modr, vecs, w_up, w_down)


def _mlp_bwd(d_dn, ru, x2, dx3, o, modr, vecs, w_up, w_down):
    s = x2.shape[0]
    tm, tf = min(TM_MLP, s), 512
    nj = D_FF // tf

    def body(ddn_ref, ru_ref, x2_ref, dx3_ref, o_ref, mod_ref, vec_ref, wu_ref, wd_ref,
             dup_ref, dx2_ref, do_ref, small_ref, acc_ref):
        i = pl.program_id(0)
        j = pl.program_id(1)

        @pl.when(jnp.logical_and(i == 0, j == 0))
        def _():
            small_ref[...] = jnp.zeros_like(small_ref)

        dff = _dot_nt(ddn_ref[...], wd_ref[...])
        dup = (dff * (2.0 * ru_ref[...].astype(F32))).astype(BF16)
        dup_ref[...] = dup
        part = _dot_nt(dup, wu_ref[...])

        @pl.when(j == 0)
        def _():
            acc_ref[...] = part

        @pl.when(j > 0)
        def _():
            acc_ref[...] += part

        @pl.when(j == nj - 1)
        def _():
            dh2 = acc_ref[...]
            x2 = x2_ref[...]
            r2 = lax.rsqrt(jnp.mean(x2 * x2, axis=-1, keepdims=True) + EPS)
            xn2 = x2 * r2
            gain = vec_ref[V_G2:V_G2 + 1, :] * (1.0 + mod_ref[M_SC2:M_SC2 + 1, :])
            dxn2 = dh2 * gain
            dx2 = dx3_ref[...] + r2 * (dxn2 - xn2 * jnp.mean(dxn2 * xn2, axis=-1, keepdims=True))
            dx2_ref[...] = dx2
            do_ref[...] = (dx2 * mod_ref[M_GT1:M_GT1 + 1, :]).astype(BF16)
            small_ref[0:1, :] += jnp.sum(dh2, axis=0, keepdims=True)
            small_ref[1:2, :] += jnp.sum(dh2 * xn2, axis=0, keepdims=True)
            small_ref[2:3, :] += jnp.sum(dx2 * o_ref[...].astype(F32), axis=0, keepdims=True)

    tok = pl.BlockSpec((tm, D), lambda i, j: (i, 0))
    chunk = pl.BlockSpec((tm, tf), lambda i, j: (i, j))
    return pl.pallas_call(
        body, name="mlp_bwd", grid=(s // tm, nj),
        in_specs=[tok, chunk, tok, tok, tok,
                  pl.BlockSpec((8, D), lambda i, j: (0, 0)), pl.BlockSpec((16, D), lambda i, j: (0, 0)),
                  pl.BlockSpec((D, tf), lambda i, j: (0, j)), pl.BlockSpec((tf, D), lambda i, j: (j, 0))],
        out_specs=[chunk, tok, tok, pl.BlockSpec((8, D), lambda i, j: (0, 0))],
        out_shape=[jax.ShapeDtypeStruct((s, D_FF), BF16), jax.ShapeDtypeStruct((s, D), F32),
                   jax.ShapeDtypeStruct((s, D), BF16), jax.ShapeDtypeStruct((8, D), F32)],
        scratch_shapes=[pltpu.VMEM((tm, D), F32)],
        compiler_params=_params(("arbitrary", "arbitrary")),
    )(d_dn, ru, x2, dx3, o, modr, vecs, w_up, w_down)


def _branch_bwd(do, proj, ba, bb, w_a, w_b, w_out):
    s = do.shape[0]
    tm = min(TM_BRANCH, s)

    def body(do_ref, ga_ref, gb_ref, ba_ref, bb_ref, wa_ref, wb_ref, wo_ref,
             dba_ref, dbb_ref, dg_ref, dza_ref, dpooled_ref):
        dmerged = _dot_nt(do_ref[...], wo_ref[...])
        sa = _sigmoid(ga_ref[...])
        sb = _sigmoid(gb_ref[...])
        dba = (dmerged * sa).astype(BF16)
        dbb = (dmerged * sb).astype(BF16)
        dba_ref[...] = dba
        dbb_ref[...] = dbb
        dg_ref[:, :D] = (dmerged * ba_ref[...].astype(F32) * sa * (1.0 - sa)).astype(BF16)
        dg_ref[:, D:] = (dmerged * bb_ref[...].astype(F32) * sb * (1.0 - sb)).astype(BF16)
        dza_ref[...] = _dot_nt(dba, wa_ref[...])
        dpooled_ref[...] = _dot_nt(dbb, wb_ref[...])

    tok = pl.BlockSpec((tm, D), lambda i: (i, 0))
    wspec = pl.BlockSpec((D, D), lambda i: (0, 0))
    sd = lambda dt: jax.ShapeDtypeStruct((s, D), dt)
    return pl.pallas_call(
        body, name="branch_bwd", grid=(s // tm,),
        in_specs=[tok, pl.BlockSpec((tm, D), lambda i: (i, 3)), pl.BlockSpec((tm, D), lambda i: (i, 4)),
                  tok, tok, wspec, wspec, wspec],
        out_specs=[tok, tok, pl.BlockSpec((tm, 2 * D), lambda i: (i, 0)), tok, tok],
        out_shape=[sd(BF16), sd(BF16), jax.ShapeDtypeStruct((s, 2 * D), BF16), sd(F32), sd(F32)],
        compiler_params=_params(("parallel",)),
    )(do, proj, proj, ba, bb, w_a, w_b, w_out)


def _mix_bwd(dza, dpooled, proj, xr, hr, p, dgates, vecs, w_rg_a, w_rg_x, w_pool):
    s = xr.shape[0]
    tm = min(TM_MIX, s)
    nb = s // tm

    def body(dza_ref, dpooled_ref, xh_ref, x_ref, y_ref, xr_ref, hh_ref, hr_ref, p_ref, dg_ref,
             vec_ref, wa_ref, wx_ref, wp_ref,
             dproj_ref, dwa_ref, dwx_ref, dwp_ref, small_ref,
             scan_carry, dxr_carry, q_carry):
        i = pl.program_id(0)
        bi = nb - 1 - i
        first_t = bi == 0

        @pl.when(i == 0)
        def _():
            scan_carry[...] = jnp.zeros_like(scan_carry)
            dxr_carry[...] = jnp.zeros_like(dxr_carry)
            q_carry[...] = jnp.zeros_like(q_carry)
            dwa_ref[...] = jnp.zeros_like(dwa_ref)
            dwx_ref[...] = jnp.zeros_like(dwx_ref)
            dwp_ref[...] = jnp.zeros_like(dwp_ref)
            small_ref[...] = jnp.zeros_like(small_ref)

        row = lax.broadcasted_iota(jnp.int32, (tm, GW), 0)
        is_t0 = jnp.logical_and(first_t, row == 0)
        t_glob = (row + bi * tm + 1).astype(F32)
        colsum = lambda v: jnp.sum(v, axis=0, keepdims=True)
        for g in range(N_GROUPS):
            cs = slice(g * GW, (g + 1) * GW)
            vec = vec_ref[:, cs]
            xr = xr_ref[:, cs]
            hr = hr_ref[:, cs]
            dza = dza_ref[:, cs]
            ga, dga = _gelu_and_grad(y_ref[:, cs])
            dproj_ref[:, D + g * GW:D + (g + 1) * GW] = (dza * hr * dga).astype(BF16)
            dhr = dza * ga
            ra, ri, sp, a, mult = _rglru_gates(
                xr, wa_ref[g], wx_ref[g], vec[V_B_RG_A:V_B_RG_A + 1], vec[V_B_RG_X:V_B_RG_X + 1],
                vec[V_A_PARAM:V_A_PARAM + 1], is_t0)
            m = jnp.where(row == tm - 1, 0.0, _shift_up(a, 1))
            gsum = dhr + jnp.where(row == tm - 1, scan_carry[0:1, cs], 0.0)
            k = 1
            while k < tm:
                gsum = gsum + m * jnp.where(row < tm - k, _shift_up(gsum, k), 0.0)
                if 2 * k < tm:
                    m = m * _shift_up(m, k)
                k *= 2
            scan_carry[:, cs] = (a * gsum)[0:8, :]
            hh = jnp.where(first_t, 0.0, hh_ref[:, cs])
            hprev = _shift_down(jnp.concatenate([hh, hr], axis=0), 1)[8:]
            da = gsum * hprev
            dmult = jnp.where(is_t0, 0.0, gsum * xr * ri)
            dlog_a = da * a - dmult * a * a / mult
            dri = gsum * xr * mult
            dxr = gsum * ri * mult
            small_ref[7:8, cs] += colsum((-C_RG) * ra * dlog_a)
            dpa = (((-C_RG) * sp) * dlog_a * ra * (1.0 - ra))
            dpx = dri * ri * (1.0 - ri)
            small_ref[5:6, cs] += colsum(dpa)
            small_ref[6:7, cs] += colsum(dpx)
            dpa = dpa.astype(BF16)
            dpx = dpx.astype(BF16)
            xrb = xr.astype(BF16)
            dwa_ref[g] += _dot_tn(xrb, dpa)
            dwx_ref[g] += _dot_tn(xrb, dpx)
            dxr = dxr + _dot_nt(dpa, wa_ref[g]) + _dot_nt(dpx, wx_ref[g])
            small_ref[4:5, cs] += colsum(dxr)
            xh = jnp.where(first_t, 0.0, xh_ref[:, cs])
            taps = _conv_taps(jnp.concatenate([xh, x_ref[:, cs]], axis=0))
            dxr_ext = jnp.concatenate([dxr, dxr_carry[:, cs]], axis=0)
            dx = vec[V_CONV_W + 3:V_CONV_W + 4] * dxr
            for j in range(4):
                small_ref[j:j + 1, cs] += colsum(dxr * taps[j])
                if j < 3:
                    dx = dx + vec[V_CONV_W + j:V_CONV_W + j + 1] * _shift_up(dxr_ext, 3 - j)[:tm]
            dxr_carry[:, cs] = dxr[0:8, :]
            dproj_ref[:, cs] = dx.astype(BF16)
            pg = p_ref[:, cs]
            dpooled = dpooled_ref[:, cs]
            pb = _dot(pg, wp_ref[g]) + vec[V_B_POOL:V_B_POOL + 1]
            small_ref[9:10, cs] += colsum(dpooled * pb)
            dpb = dpooled * vec[V_POOL_SCALE:V_POOL_SCALE + 1]
            small_ref[8:9, cs] += colsum(dpb)
            dpbb = dpb.astype(BF16)
            dwp_ref[g] += _dot_tn(pg, dpbb)
            dp = _dot_nt(dpbb, wp_ref[g])
            q = dp / jnp.minimum(t_glob, float(POOL_WINDOWS[g]))
            sm = jnp.concatenate([q, q_carry[:, cs]], axis=0)
            k = 1
            while k < POOL_WINDOWS[g]:
                sm = sm + _shift_up(sm, k)
                k *= 2
            q_carry[:, cs] = q[0:HALO_U, :]
            dproj_ref[:, 2 * D + g * GW:2 * D + (g + 1) * GW] = (sm[:tm] - dp).astype(BF16)
        dproj_ref[:, 3 * D:] = dg_ref[...]

    rev = lambda i: nb - 1 - i
    tok = pl.BlockSpec((tm, D), lambda i: (rev(i), 0))
    col = lambda k: pl.BlockSpec((tm, D), lambda i: (rev(i), k))
    halo8 = lambda k: pl.BlockSpec((8, D), lambda i: (jnp.maximum(rev(i) * (tm // 8) - 1, 0), k))
    wspec = pl.BlockSpec((N_GROUPS, GW, GW), lambda i: (0, 0, 0))
    wshape = jax.ShapeDtypeStruct((N_GROUPS, GW, GW), F32)
    return pl.pallas_call(
        body, name="mix_bwd", grid=(nb,),
        in_specs=[tok, tok, halo8(0), col(0), col(1), tok, halo8(0), tok, tok,
                  pl.BlockSpec((tm, 2 * D), lambda i: (rev(i), 0)),
                  pl.BlockSpec((16, D), lambda i: (0, 0)), wspec, wspec, wspec],
        out_specs=[pl.BlockSpec((tm, D_IN), lambda i: (rev(i), 0)), wspec, wspec, wspec,
                   pl.BlockSpec((16, D), lambda i: (0, 0))],
        out_shape=[jax.ShapeDtypeStruct((s, D_IN), BF16), wshape, wshape, wshape,
                   jax.ShapeDtypeStruct((16, D), F32)],
        scratch_shapes=[pltpu.VMEM((8, D), F32), pltpu.VMEM((8, D), F32), pltpu.VMEM((HALO_U, D), F32)],
        compiler_params=_params(("arbitrary",)),
    )(dza, dpooled, proj, proj, proj, xr, hr, hr, p, dgates, vecs, w_rg_a, w_rg_x, w_pool)


def _proj_bwd(dproj, x, dx2, modr, vecs, w_in):
    s = x.shape[0]
    tm, tk = min(TM_PROJ, s), 1024
    nk = D_IN // tk

    def body(dp_ref, x_ref, dx2_ref, mod_ref, vec_ref, w_ref, gx_ref, small_ref, acc_ref):
        i = pl.program_id(0)
        k = pl.program_id(1)

        @pl.when(jnp.logical_and(i == 0, k == 0))
        def _():
            small_ref[...] = jnp.zeros_like(small_ref)

        part = _dot_nt(dp_ref[...], w_ref[...])

        @pl.when(k == 0)
        def _():
            acc_ref[...] = part

        @pl.when(k > 0)
        def _():
            acc_ref[...] += part

        @pl.when(k == nk - 1)
        def _():
            dh1 = acc_ref[...]
            xv = x_ref[...]
            r1 = lax.rsqrt(jnp.mean(xv * xv, axis=-1, keepdims=True) + EPS)
            xn1 = xv * r1
            gain = vec_ref[V_G1:V_G1 + 1, :] * (1.0 + mod_ref[M_SC1:M_SC1 + 1, :])
            dxn1 = dh1 * gain
            gx_ref[...] = dx2_ref[...] + r1 * (dxn1 - xn1 * jnp.mean(dxn1 * xn1, axis=-1, keepdims=True))
            small_ref[0:1, :] += jnp.sum(dh1, axis=0, keepdims=True)
            small_ref[1:2, :] += jnp.sum(dh1 * xn1, axis=0, keepdims=True)

    tok = pl.BlockSpec((tm, D), lambda i, k: (i, 0))
    return pl.pallas_call(
        body, name="proj_bwd", grid=(s // tm, nk),
        in_specs=[pl.BlockSpec((tm, tk), lambda i, k: (i, k)), tok, tok,
                  pl.BlockSpec((8, D), lambda i, k: (0, 0)), pl.BlockSpec((16, D), lambda i, k: (0, 0)),
                  pl.BlockSpec((D, tk), lambda i, k: (0, k))],
        out_specs=[tok, pl.BlockSpec((8, D), lambda i, k: (0, 0))],
        out_shape=[jax.ShapeDtypeStruct((s, D), F32), jax.ShapeDtypeStruct((8, D), F32)],
        scratch_shapes=[pltpu.VMEM((tm, D), F32)],
        compiler_params=_params(("arbitrary", "arbitrary")),
    )(dproj, x, dx2, modr, vecs, w_in)


def _wgrad(a, b, name, square_a=False):
    s, ka = a.shape
    n = b.shape[1]
    tka, tn, ts = 1024, 1024, min(TS_WGRAD, s)
    ns = s // ts

    def body(a_ref, b_ref, out_ref, acc_ref):
        t = pl.program_id(2)
        av = a_ref[...]
        if square_a:
            af = av.astype(F32)
            av = (af * af).astype(BF16)
        part = _dot_tn(av, b_ref[...])

        @pl.when(t == 0)
        def _():
            acc_ref[...] = part

        @pl.when(t > 0)
        def _():
            acc_ref[...] += part

        @pl.when(t == ns - 1)
        def _():
            out_ref[...] = acc_ref[...].astype(BF16)

    return pl.pallas_call(
        body, name=name, grid=(ka // tka, n // tn, ns),
        in_specs=[pl.BlockSpec((ts, tka), lambda i, j, t: (t, i)),
                  pl.BlockSpec((ts, tn), lambda i, j, t: (t, j))],
        out_specs=pl.BlockSpec((tka, tn), lambda i, j, t: (i, j)),
        out_shape=jax.ShapeDtypeStruct((ka, n), BF16),
        scratch_shapes=[pltpu.VMEM((tka, tn), F32)],
        compiler_params=_params(("parallel", "parallel", "arbitrary")),
    )(a, b)


def _window(ref, kind, idx, size):
    start = pl.multiple_of(idx * size, size)
    if kind == 0:
        return ref.at[pl.ds(start, size)]
    if kind == 1:
        return ref.at[:, pl.ds(start, size)]
    return ref.at[:, :, pl.ds(start, size)]


def _mesh_place():
    x, y, c = lax.axis_index("x"), lax.axis_index("y"), lax.axis_index("c")
    return x, y, c, 4 * x + 2 * y + c


def _peer(x, y, c, q):
    px = 1 - x if q & 4 else x
    py = 1 - y if q & 2 else y
    pc = 1 - c if q & 1 else c
    return (px, py, pc), 4 * px + 2 * py + pc


def _all_gather(shards, kinds, name):
    n = len(shards)
    full_shapes = []
    for sh, kind in zip(shards, kinds):
        dims = list(sh.shape)
        dims[kind] *= N_DEV
        full_shapes.append(jax.ShapeDtypeStruct(tuple(dims), sh.dtype))

    def body(*refs):
        ins, outs = refs[:n], refs[n:2 * n]
        send_sems, recv_sems, local_sems = refs[2 * n:]
        x, y, c, me = _mesh_place()
        sends, recvs, locals_ = [], [], []
        for k in range(n):
            size = shards[k].shape[kinds[k]]
            mine = _window(outs[k], kinds[k], me, size)
            lc = pltpu.make_async_copy(ins[k], mine, local_sems.at[k])
            lc.start()
            locals_.append(lc)
            for q in range(1, N_DEV):
                peer, peer_idx = _peer(x, y, c, q)
                cp = pltpu.make_async_remote_copy(
                    src_ref=ins[k], dst_ref=mine, send_sem=send_sems.at[k, q], recv_sem=recv_sems.at[k, q],
                    device_id=peer, device_id_type=MESH)
                cp.start()
                sends.append(cp)
                recvs.append(pltpu.make_async_remote_copy(
                    src_ref=ins[k], dst_ref=_window(outs[k], kinds[k], peer_idx, size),
                    send_sem=send_sems.at[k, q], recv_sem=recv_sems.at[k, q],
                    device_id=peer, device_id_type=MESH))
        for cp in recvs:
            cp.wait_recv()
        for cp in sends:
            cp.wait_send()
        for lc in locals_:
            lc.wait()

    any_spec = pl.BlockSpec(memory_space=pl.ANY)
    return pl.pallas_call(
        body, name=name,
        in_specs=[any_spec] * n, out_specs=[any_spec] * n, out_shape=full_shapes,
        scratch_shapes=[pltpu.SemaphoreType.DMA((n, N_DEV)), pltpu.SemaphoreType.DMA((n, N_DEV)),
                        pltpu.SemaphoreType.DMA((n,))],
    )(*shards)


_HBM = pl.BlockSpec(memory_space=pltpu.HBM)
_SEM = pl.BlockSpec(memory_space=pltpu.SEMAPHORE)
_EFFECT = pltpu.SideEffectType.DATAFLOW_SIDE_EFFECTING


def _exchange_copies(gather, kinds, src_refs, land_refs, send_sems, recv_sems):
    x, y, c, me = _mesh_place()
    sends, recvs = [], []
    for k in range(len(kinds)):
        for q in range(1, N_DEV):
            peer, peer_idx = _peer(x, y, c, q)
            if gather:
                size = src_refs[k].shape[kinds[k]]
                src = src_refs[k]
                dst = _window(land_refs[k], kinds[k], me, size)
                arriving = _window(land_refs[k], kinds[k], peer_idx, size)
            else:
                size = src_refs[k].shape[kinds[k]] // N_DEV
                src = _window(src_refs[k], kinds[k], peer_idx, size)
                dst = land_refs[k].at[me]
                arriving = land_refs[k].at[peer_idx]
            sems = dict(send_sem=send_sems.at[k * N_DEV + q], recv_sem=recv_sems.at[k * N_DEV + q],
                        device_id=peer, device_id_type=MESH)
            sends.append(pltpu.make_async_remote_copy(src_ref=src, dst_ref=dst, **sems))
            recvs.append(pltpu.make_async_remote_copy(src_ref=src, dst_ref=arriving, **sems))
    return sends, recvs


def _exchange_start(gather, srcs, lands, kinds, after, name):
    n = len(srcs)

    def body(*refs):
        src_refs, land_refs = refs[:n], refs[n:2 * n]
        send_sems, recv_sems = refs[2 * n + 1], refs[2 * n + 2]
        token = refs[-1]
        sends, _ = _exchange_copies(gather, kinds, src_refs, land_refs, send_sems, recv_sems)
        for cp in sends:
            cp.start()
        token[...] = jnp.zeros_like(token)

    hbm = lambda a: pltpu.HBM(a.shape, a.dtype)
    outs = pl.pallas_call(
        body, name=name,
        out_shape=(pltpu.SemaphoreType.DMA((n * N_DEV,)), pltpu.SemaphoreType.DMA((n * N_DEV,)),
                   *[hbm(a) for a in srcs], *[hbm(a) for a in lands], jax.ShapeDtypeStruct((8, 128), F32)),
        in_specs=[_HBM] * (2 * n) + [pl.BlockSpec(memory_space=pl.ANY)],
        out_specs=(_SEM, _SEM, *[_HBM] * (2 * n), pl.BlockSpec(memory_space=pltpu.VMEM)),
        input_output_aliases={i: 2 + i for i in range(2 * n)},
        compiler_params=pltpu.CompilerParams(has_side_effects=_EFFECT),
    )(*[pltpu.with_memory_space_constraint(a, pltpu.HBM) for a in (*srcs, *lands)], after)
    return outs[0], outs[1], outs[2:2 + n], outs[2 + n:2 + 2 * n], outs[-1]


def _exchange_wait(gather, started, kinds, after, name):
    send_sems, recv_sems, srcs, lands, _ = started
    n = len(srcs)

    def body(*refs):
        src_refs, land_refs = refs[:n], refs[n:2 * n]
        sends, recvs = _exchange_copies(gather, kinds, src_refs, land_refs, refs[2 * n], refs[2 * n + 1])
        for cp in sends:
            cp.wait_send()
        for cp in recvs:
            cp.wait_recv()

    hbm = lambda a: pltpu.HBM(a.shape, a.dtype)
    outs = pl.pallas_call(
        body, name=name,
        out_shape=(*[hbm(a) for a in srcs], *[hbm(a) for a in lands]),
        in_specs=[_HBM] * (2 * n) + [_SEM, _SEM, pl.BlockSpec(memory_space=pl.ANY)],
        out_specs=tuple([_HBM] * (2 * n)),
        input_output_aliases={i: i for i in range(2 * n)},
        compiler_params=pltpu.CompilerParams(has_side_effects=_EFFECT),
    )(*srcs, *lands, send_sems, recv_sems, after)
    return outs[n:]


def _after(value, token):
    value, _ = lax.optimization_barrier((value, token))
    return value


def _own_window(kind, shard, me):
    dims = list(shard.shape)
    dims[kind] *= N_DEV
    start = [0] * len(dims)
    start[kind] = me * shard.shape[kind]
    return lax.dynamic_update_slice(lax.empty(tuple(dims), shard.dtype), shard, tuple(start))


def _own_slot(kind, full, me):
    size = full.shape[kind] // N_DEV
    mine = lax.dynamic_slice_in_dim(full, me * size, size, axis=kind)
    return lax.dynamic_update_index_in_dim(lax.empty((N_DEV, *mine.shape), full.dtype), mine, me, 0)


def _silu(c):
    return c * _sigmoid(c)


def _ada_fwd(c_all, w_ada, b_ada_cols):
    def body(c_ref, w_ref, b_ref, out_ref):
        out_ref[...] = jnp.dot(_silu(c_ref[...]), w_ref[...], preferred_element_type=F32,
                               precision=lax.Precision.HIGHEST) + b_ref[...]

    return pl.pallas_call(
        body, name="ada_fwd", out_shape=jax.ShapeDtypeStruct((N_DEV, w_ada.shape[1]), F32),
    )(c_all, w_ada, b_ada_cols)


def _adam(w, g, m, v):
    m = ADAM_B1 * m + (1.0 - ADAM_B1) * g
    v = ADAM_B2 * v + (1.0 - ADAM_B2) * (g * g)
    m_hat = m / (1.0 - ADAM_B1 ** ADAM_STEP)
    v_hat = v / (1.0 - ADAM_B2 ** ADAM_STEP)
    delta = -ADAM_LR * (m_hat / (jnp.sqrt(v_hat) + ADAM_EPS) + ADAM_WD * w)
    return delta, m, v


def _ada_bwd_adam(c_all, dmod_cols, w, m, v):
    def body(c_ref, d_ref, w_ref, m_ref, v_ref, g_ref, delta_ref, nm_ref, nv_ref):
        g = lax.dot_general(_silu(c_ref[...]), d_ref[...], (((0,), (0,)), ((), ())),
                            preferred_element_type=F32, precision=lax.Precision.HIGHEST)
        g_ref[...] = g
        delta_ref[...], nm_ref[...], nv_ref[...] = _adam(w_ref[...], g, m_ref[...], v_ref[...])

    sd = jax.ShapeDtypeStruct(w.shape, F32)
    return pl.pallas_call(body, name="ada_bwd_adam", out_shape=[sd] * 4,
                          compiler_params=pltpu.CompilerParams(vmem_limit_bytes=V7X_VMEM_LIMIT),
                          )(c_all, dmod_cols, w, m, v)


def _sum_slots_adam(slots, w, m, v, name):
    r, cdim = w.shape
    tr = min(r, 128)

    def body(s_ref, w_ref, m_ref, v_ref, g_ref, delta_ref, nm_ref, nv_ref):
        g = s_ref[0].astype(F32)
        for p in range(1, N_DEV):
            g = g + s_ref[p].astype(F32)
        g_ref[...] = g
        delta_ref[...], nm_ref[...], nv_ref[...] = _adam(w_ref[...], g, m_ref[...], v_ref[...])

    blk = pl.BlockSpec((tr, cdim), lambda i: (i, 0))
    sd = jax.ShapeDtypeStruct((r, cdim), F32)
    return pl.pallas_call(
        body, name=name, grid=(r // tr,),
        in_specs=[pl.BlockSpec((N_DEV, tr, cdim), lambda i: (0, i, 0)), blk, blk, blk],
        out_specs=[blk] * 4, out_shape=[sd] * 4,
        compiler_params=_params(("parallel",)),
    )(slots, w, m, v)


N_SMALL = 40
N_SMALL_PARAMS = 11


def _pack_vecs(conv_w_full, rows):
    def body(cw_ref, *refs):
        out = refs[-1]
        out[...] = jnp.zeros_like(out)
        out[0:4, :] = cw_ref[0:4, :]
        for r, ref in enumerate(refs[:-1]):
            out[4 + r:5 + r, :] = ref[...]

    return pl.pallas_call(body, name="pack_vecs", out_shape=jax.ShapeDtypeStruct((16, D), F32))(conv_w_full, *rows)


def _small_finish(gathered, mod_all, vecs, ws, ms, vs):
    n = N_SMALL_PARAMS

    def body(g_ref, mod_ref, vec_ref, *refs):
        w_refs, m_refs, v_refs = refs[:n], refs[n:2 * n], refs[2 * n:3 * n]
        outs = refs[3 * n:]
        g1 = vec_ref[V_G1:V_G1 + 1, :]
        g2 = vec_ref[V_G2:V_G2 + 1, :]
        zero = jnp.zeros((1, D), F32)
        dg1, dg2, dgf, loss_lanes = zero, zero, zero, zero
        mixer = jnp.zeros((16, D), F32)
        db_ada = jnp.zeros((6, D), F32)
        for b in range(N_DEV):
            gb = g_ref[b]
            mod = mod_ref[b]
            q1 = gb[33:34]
            q2 = gb[9:10]
            dmod = jnp.concatenate([gb[32:33], q1 * g1, gb[10:11], gb[8:9], q2 * g2, gb[1:2]], axis=0)
            outs[4 * n][b] = dmod
            db_ada = db_ada + dmod
            dg1 = dg1 + q1 * (1.0 + mod[M_SC1:M_SC1 + 1])
            dg2 = dg2 + q2 * (1.0 + mod[M_SC2:M_SC2 + 1])
            dgf = dgf + gb[0:1]
            loss_lanes = loss_lanes + gb[2:3]
            mixer = mixer + gb[16:32]
        d_a_param = mixer[7:8] * _sigmoid(vec_ref[V_A_PARAM:V_A_PARAM + 1, :])
        grads = [dg1, dg2, mixer[4:5], mixer[5:6], mixer[6:7], d_a_param, mixer[8:9], mixer[9:10], dgf,
                 db_ada, mixer[0:4]]
        for k in range(n):
            outs[k][...] = grads[k]
            outs[n + k][...], outs[2 * n + k][...], outs[3 * n + k][...] = _adam(
                w_refs[k][...], grads[k], m_refs[k][...], v_refs[k][...])
        outs[4 * n + 1][...] = jnp.broadcast_to(jnp.sum(loss_lanes, axis=1, keepdims=True), (8, 128))

    shapes = [jax.ShapeDtypeStruct(w.shape, F32) for w in ws]
    return pl.pallas_call(
        body, name="small_finish",
        out_shape=shapes * 4 + [jax.ShapeDtypeStruct((N_DEV, 6, D), F32), jax.ShapeDtypeStruct((8, 128), F32)],
    )(gathered, mod_all, vecs, *ws, *ms, *vs)


def _pad_rows(a, rows):
    return jnp.pad(a, ((0, rows - a.shape[0]), (0, 0)))


def kernel(x, c, norm_mix_g, norm_mlp_g, w_ada, b_ada, w_in, conv_w, conv_b, w_rg_a, b_rg_a, w_rg_x, b_rg_x, a_param, w_branch_a, w_pool, b_pool, pool_scale, w_branch_b, w_out, w_up, w_down, final_g, loss_target, m_norm_mix_g, m_norm_mlp_g, m_w_ada, m_b_ada, m_w_in, m_conv_w, m_conv_b, m_w_rg_a, m_b_rg_a, m_w_rg_x, m_b_rg_x, m_a_param, m_w_branch_a, m_w_pool, m_b_pool, m_pool_scale, m_w_branch_b, m_w_out, m_w_up, m_w_down, m_final_g, v_norm_mix_g, v_norm_mlp_g, v_w_ada, v_b_ada, v_w_in, v_conv_w, v_conv_b, v_w_rg_a, v_b_rg_a, v_w_rg_x, v_b_rg_x, v_a_param, v_w_branch_a, v_w_pool, v_b_pool, v_pool_scale, v_w_branch_b, v_w_out, v_w_up, v_w_down, v_final_g):
    me = 4 * lax.axis_index("x") + 2 * lax.axis_index("y") + lax.axis_index("c")
    s = x.shape[1]
    x2d = x.reshape(s, D)
    target = loss_target.reshape(s, D)
    n_ada = w_ada.shape[2]

    sharded = dict(w_in=(w_in[0], 1), w_up=(w_up[0], 1), w_down=(w_down[0], 0), w_branch_a=(w_branch_a[0], 0),
                   w_branch_b=(w_branch_b[0], 0), w_out=(w_out[0], 0), w_rg_a=(w_rg_a[0], 1), w_rg_x=(w_rg_x[0], 1),
                   w_pool=(w_pool[0], 1))
    kind = {k: v[1] for k, v in sharded.items()}
    shard = {k: v[0].astype(BF16) for k, v in sharded.items()}

    w_in_full, conv_w_full, c_rows = _all_gather([shard["w_in"], _pad_rows(conv_w[0], 8), _pad_rows(c, 8)],
                                                 [1, 1, 0], "gather_first")
    c_all = c_rows.reshape(N_DEV, 8, D)[:, 0, :]

    b_ada_cols = lax.dynamic_slice(b_ada, (0, me * n_ada), (1, n_ada))
    mod_part = _ada_fwd(c_all, w_ada[0], b_ada_cols)
    mod_parts, = _all_gather([mod_part], [0], "gather_mod")
    mod_all = jnp.transpose(mod_parts.reshape(N_DEV, N_DEV, n_ada), (1, 0, 2)).reshape(N_DEV, 6, D)
    mod_all = jnp.pad(mod_all, ((0, 0), (0, 2), (0, 0)))
    modr = lax.dynamic_index_in_dim(mod_all, me, 0, keepdims=False)
    vecs = _pack_vecs(conv_w_full, [conv_b, b_rg_a, b_rg_x, a_param, b_pool, pool_scale,
                                    norm_mix_g, norm_mlp_g, final_g.reshape(1, D)])

    mixer_names = ["w_rg_a", "w_rg_x", "w_pool", "w_branch_a", "w_branch_b", "w_out"]
    mlp_names = ["w_up", "w_down"]

    def start_gather(group, after, name):
        return _exchange_start(True, [shard[k] for k in group], [_own_window(kind[k], shard[k], me) for k in group],
                               [kind[k] for k in group], after, name)

    g_mixer = start_gather(mixer_names, modr, "gather_mixer_start")
    g_mlp = start_gather(mlp_names, g_mixer[-1], "gather_mlp_start")

    proj, h1 = _proj_fwd(x2d, _after(modr, g_mlp[-1]), vecs, w_in_full)
    wg = dict(zip(mixer_names, _exchange_wait(True, g_mixer, [kind[k] for k in mixer_names], h1, "gather_mixer_wait")))
    xr, hr, za, p, pooled = _mix_fwd(proj, vecs, wg["w_rg_a"], wg["w_rg_x"], wg["w_pool"])
    ba, bb, merged, o, x2, h2 = _branch_fwd(za, pooled, proj, x2d, modr, vecs,
                                            wg["w_branch_a"], wg["w_branch_b"], wg["w_out"])
    wg.update(zip(mlp_names, _exchange_wait(True, g_mlp, [kind[k] for k in mlp_names], h2, "gather_mlp_wait")))
    ru, dx3, d_dn, small_f = _mlp_fwd(h2, x2, target, modr, vecs, wg["w_up"], wg["w_down"])

    def start_scatter(group, partial, after, name):
        return _exchange_start(False, [partial[k] for k in group], [_own_slot(kind[k], partial[k], me) for k in group],
                               [kind[k] for k in group], after, name)

    dup, dx2, do, small_m = _mlp_bwd(d_dn, ru, x2, dx3, o, modr, vecs, wg["w_up"], wg["w_down"])
    partial = dict(w_up=_wgrad(h2, dup, "wgrad_up"), w_down=_wgrad(ru, d_dn, "wgrad_down", square_a=True))
    s_mlp = start_scatter(mlp_names, partial, dx2, "scatter_mlp_start")

    dba, dbb, dgates, dza, dpooled = _branch_bwd(_after(do, s_mlp[-1]), proj, ba, bb,
                                                 wg["w_branch_a"], wg["w_branch_b"], wg["w_out"])
    dproj, dw_rg_a, dw_rg_x, dw_pool, small_x = _mix_bwd(dza, dpooled, proj, xr, hr, p, dgates, vecs,
                                                         wg["w_rg_a"], wg["w_rg_x"], wg["w_pool"])
    partial.update(w_branch_a=_wgrad(za, dba, "wgrad_branch_a"), w_branch_b=_wgrad(pooled, dbb, "wgrad_branch_b"),
                   w_out=_wgrad(merged, do, "wgrad_out"),
                   w_rg_a=dw_rg_a.astype(BF16), w_rg_x=dw_rg_x.astype(BF16), w_pool=dw_pool.astype(BF16))
    s_mixer = start_scatter(mixer_names, partial, s_mlp[-1], "scatter_mixer_start")

    partial["w_in"] = _wgrad(h1, _after(dproj, s_mixer[-1]), "wgrad_in")
    s_in = start_scatter(["w_in"], partial, s_mixer[-1], "scatter_in_start")
    grad_x, small_p = _proj_bwd(_after(dproj, s_in[-1]), x2d, dx2, modr, vecs, w_in_full)

    locals_ = dict(w_in=(w_in, m_w_in, v_w_in), w_up=(w_up, m_w_up, v_w_up), w_down=(w_down, m_w_down, v_w_down),
                   w_branch_a=(w_branch_a, m_w_branch_a, v_w_branch_a),
                   w_branch_b=(w_branch_b, m_w_branch_b, v_w_branch_b), w_out=(w_out, m_w_out, v_w_out),
                   w_rg_a=(w_rg_a, m_w_rg_a, v_w_rg_a), w_rg_x=(w_rg_x, m_w_rg_x, v_w_rg_x),
                   w_pool=(w_pool, m_w_pool, v_w_pool))
    res = {}

    def finish(group, started, after, name):
        slots = _exchange_wait(False, started, [kind[k] for k in group], after, name)
        for k, sl in zip(group, slots):
            w, m, v = locals_[k]
            shape2d = (-1, w.shape[-1])
            sl = sl.reshape(N_DEV, *w.reshape(shape2d).shape)
            outs = _sum_slots_adam(sl, w.reshape(shape2d), m.reshape(shape2d), v.reshape(shape2d), "adam_" + k)
            res[k] = [t.reshape(w.shape) for t in outs]
        return res[group[-1]][0]

    done = finish(mlp_names, s_mlp, grad_x, "scatter_mlp_wait")
    done = finish(mixer_names, s_mixer, done, "scatter_mixer_wait")
    done = finish(["w_in"], s_in, done, "scatter_in_wait")

    small = jnp.concatenate([small_f, small_m, small_x, _after(small_p, done)], axis=0)
    small_all, = _all_gather([small], [0], "gather_small")
    small_all = small_all.reshape(N_DEV, N_SMALL, D)

    def embed(cw):
        return lax.dynamic_update_slice(jnp.zeros((4, D), F32), cw[0], (0, me * (D // N_DEV)))

    def smalls(ng, nl, cb, bra, brx, ap, bp, ps, fg, ba_, cw):
        return [ng, nl, cb, bra, brx, ap, bp, ps, fg.reshape(1, D), ba_.reshape(6, D), embed(cw)]

    small_names = ["norm_mix_g", "norm_mlp_g", "conv_b", "b_rg_a", "b_rg_x", "a_param", "b_pool", "pool_scale",
                   "final_g", "b_ada", "conv_w"]
    fin = _small_finish(
        small_all, mod_all, vecs,
        smalls(norm_mix_g, norm_mlp_g, conv_b, b_rg_a, b_rg_x, a_param, b_pool, pool_scale, final_g, b_ada, conv_w),
        smalls(m_norm_mix_g, m_norm_mlp_g, m_conv_b, m_b_rg_a, m_b_rg_x, m_a_param, m_b_pool, m_pool_scale,
               m_final_g, m_b_ada, m_conv_w),
        smalls(v_norm_mix_g, v_norm_mlp_g, v_conv_b, v_b_rg_a, v_b_rg_x, v_a_param, v_b_pool, v_pool_scale,
               v_final_g, v_b_ada, v_conv_w))
    dmod_all, loss_tile = fin[4 * N_SMALL_PARAMS], fin[4 * N_SMALL_PARAMS + 1]
    dmod_cols = lax.dynamic_slice(dmod_all.reshape(N_DEV, 6 * D), (0, me * n_ada), (N_DEV, n_ada))
    res["w_ada"] = [t.reshape(w_ada.shape) for t in _ada_bwd_adam(c_all, dmod_cols, w_ada[0], m_w_ada[0], v_w_ada[0])]

    def final_shape(k, t):
        if k == "final_g":
            return t.reshape(D)
        if k == "b_ada":
            return t.reshape(1, 6 * D)
        if k == "conv_w":
            return lax.dynamic_slice(t, (0, me * (D // N_DEV)), (4, D // N_DEV)).reshape(conv_w.shape)
        return t

    for i, k in enumerate(small_names):
        res[k] = [final_shape(k, fin[which * N_SMALL_PARAMS + i]) for which in range(4)]
    order = ["norm_mix_g", "norm_mlp_g", "w_ada", "b_ada", "w_in", "conv_w", "conv_b", "w_rg_a", "b_rg_a", "w_rg_x",
             "b_rg_x", "a_param", "w_branch_a", "w_pool", "b_pool", "pool_scale", "w_branch_b", "w_out", "w_up",
             "w_down", "final_g"]
    outs = [loss_tile[0, 0], grad_x.reshape(x.shape)]
    for which in range(4):
        for k in order:
            outs.append(res[k][which])
    return tuple(outs)
```

```python
import functools

import jax
import jax.numpy as jnp
from jax import lax
from jax.experimental import pallas as pl
from jax.experimental.pallas import tpu as pltpu

F32 = jnp.float32
BF16 = jnp.bfloat16
MESH = pl.DeviceIdType.MESH

N_DEV = 8
D = 1024
N_GROUPS = 4
GW = D // N_GROUPS
D_IN = 5 * D
D_FF = 4 * D
POOL_WINDOWS = (2, 4, 8, 16)
HALO_X = 8
HALO_U = 16
EPS = 1e-6
C_RG = 8.0
ADAM_LR, ADAM_B1, ADAM_B2, ADAM_EPS, ADAM_WD, ADAM_STEP = 0.001, 0.9, 0.999, 1e-08, 0.01, 10

V7X_VMEM_LIMIT = 56 * 1024 * 1024

V_CONV_W, V_CONV_B, V_B_RG_A, V_B_RG_X, V_A_PARAM, V_B_POOL, V_POOL_SCALE, V_G1, V_G2, V_GF = 0, 4, 5, 6, 7, 8, 9, 10, 11, 12
M_SH1, M_SC1, M_GT1, M_SH2, M_SC2, M_GT2 = 0, 1, 2, 3, 4, 5

TM_PROJ = 512
TM_MIX = 256
TM_BRANCH = 256
TM_MLP = 512
TS_WGRAD = 512


def _params(semantics):
    return pltpu.CompilerParams(dimension_semantics=semantics, vmem_limit_bytes=V7X_VMEM_LIMIT)


def _dot(a, b):
    return jnp.dot(a, b, preferred_element_type=F32)


def _dot_nt(a, b):
    return lax.dot_general(a, b, (((1,), (1,)), ((), ())), preferred_element_type=F32)


def _dot_tn(a, b):
    return lax.dot_general(a, b, (((0,), (0,)), ((), ())), preferred_element_type=F32)


def _sigmoid(x):
    return 1.0 / (1.0 + jnp.exp(-x))


def _gelu_and_grad(x):
    k = 0.7978845608028654
    x2 = x * x
    t = jnp.tanh(k * (x + 0.044715 * x * x2))
    g = 0.5 * x * (1.0 + t)
    dg = 0.5 * (1.0 + t) + 0.5 * x * (1.0 - t * t) * (k * (1.0 + 3.0 * 0.044715 * x2))
    return g, dg


def _softplus(a):
    e = jnp.exp(-jnp.abs(a))
    u = 1.0 + e
    log1p_e = jnp.where(u == 1.0, e, jnp.log(u) * e / jnp.where(u == 1.0, 1.0, u - 1.0))
    return jnp.maximum(a, 0.0) + log1p_e


def _neg_expm1(z):
    series = -(z * (1.0 + z * (0.5 + z * (1.0 / 6.0 + z * (1.0 / 24.0 + z * (1.0 / 120.0))))))
    return jnp.where(z > -0.1, series, 1.0 - jnp.exp(z))


def _shift_down(x, k):
    return pltpu.roll(x, k, 0)


def _shift_up(x, k):
    return pltpu.roll(x, x.shape[0] - k, 0)


def _rglru_gates(xr, w_a, w_x, b_a, b_x, a_param, is_t0):
    xb = xr.astype(BF16)
    ra = _sigmoid(_dot(xb, w_a) + b_a)
    ri = _sigmoid(_dot(xb, w_x) + b_x)
    sp = _softplus(a_param)
    log_a = (-C_RG) * ra * sp
    a = jnp.exp(log_a)
    mult = jnp.where(is_t0, 1.0, jnp.sqrt(_neg_expm1(2.0 * log_a)))
    return ra, ri, sp, a, mult


def _conv_taps(x_ext):
    return [_shift_down(x_ext, 3 - j)[HALO_X:] if j < 3 else x_ext[HALO_X:] for j in range(4)]


def _proj_fwd(x, modr, vecs, w_in):
    s = x.shape[0]
    tm, tn = min(TM_PROJ, s), 640

    def body(x_ref, mod_ref, vec_ref, w_ref, proj_ref, h1_ref, h1_scr):
        j = pl.program_id(1)

        @pl.when(j == 0)
        def _():
            xv = x_ref[...]
            r = lax.rsqrt(jnp.mean(xv * xv, axis=-1, keepdims=True) + EPS)
            gain = vec_ref[V_G1:V_G1 + 1, :] * (1.0 + mod_ref[M_SC1:M_SC1 + 1, :])
            h = (xv * r * gain + mod_ref[M_SH1:M_SH1 + 1, :]).astype(BF16)
            h1_scr[...] = h
            h1_ref[...] = h

        proj_ref[...] = _dot(h1_scr[...], w_ref[...])

    return pl.pallas_call(
        body, name="proj_fwd", grid=(s // tm, D_IN // tn),
        in_specs=[pl.BlockSpec((tm, D), lambda i, j: (i, 0)),
                  pl.BlockSpec((8, D), lambda i, j: (0, 0)),
                  pl.BlockSpec((16, D), lambda i, j: (0, 0)),
                  pl.BlockSpec((D, tn), lambda i, j: (0, j))],
        out_specs=[pl.BlockSpec((tm, tn), lambda i, j: (i, j)),
                   pl.BlockSpec((tm, D), lambda i, j: (i, 0))],
        out_shape=[jax.ShapeDtypeStruct((s, D_IN), F32), jax.ShapeDtypeStruct((s, D), BF16)],
        scratch_shapes=[pltpu.VMEM((tm, D), BF16)],
        compiler_params=_params(("parallel", "arbitrary")),
    )(x, modr, vecs, w_in)


def _mix_fwd(proj, vecs, w_rg_a, w_rg_x, w_pool):
    s = proj.shape[0]
    tm = min(TM_MIX, s)
    nb = s // tm

    def body(xh_ref, x_ref, y_ref, uh_ref, u_ref, vec_ref, wa_ref, wx_ref, wp_ref,
             xr_ref, hr_ref, za_ref, p_ref, pooled_ref, carry_ref):
        i = pl.program_id(0)
        first = i == 0

        @pl.when(first)
        def _():
            carry_ref[...] = jnp.zeros_like(carry_ref)

        row = lax.broadcasted_iota(jnp.int32, (tm, GW), 0)
        is_t0 = jnp.logical_and(first, row == 0)
        t_glob = (row + i * tm + 1).astype(F32)
        for g in range(N_GROUPS):
            cs = slice(g * GW, (g + 1) * GW)
            vec = vec_ref[:, cs]
            xh = jnp.where(first, 0.0, xh_ref[:, cs])
            taps = _conv_taps(jnp.concatenate([xh, x_ref[:, cs]], axis=0))
            xr = vec[V_CONV_B:V_CONV_B + 1]
            for j in range(4):
                xr = xr + vec[V_CONV_W + j:V_CONV_W + j + 1] * taps[j]
            xr_ref[:, cs] = xr
            _, ri, _, a, mult = _rglru_gates(
                xr, wa_ref[g], wx_ref[g], vec[V_B_RG_A:V_B_RG_A + 1], vec[V_B_RG_X:V_B_RG_X + 1],
                vec[V_A_PARAM:V_A_PARAM + 1], is_t0)
            b = xr * ri * mult
            b = b + jnp.where(row == 0, a * carry_ref[7:8, cs], 0.0)
            k = 1
            while k < tm:
                b = b + a * jnp.where(row >= k, _shift_down(b, k), 0.0)
                if 2 * k < tm:
                    a = a * _shift_down(a, k)
                k *= 2
            hr_ref[:, cs] = b
            carry_ref[:, cs] = b[tm - 8:, :]
            ga, _ = _gelu_and_grad(y_ref[:, cs])
            za_ref[:, cs] = (ga * b).astype(BF16)
            uh = jnp.where(first, 0.0, uh_ref[:, cs])
            sm = jnp.concatenate([uh, u_ref[:, cs]], axis=0)
            k = 1
            while k < POOL_WINDOWS[g]:
                sm = sm + _shift_down(sm, k)
                k *= 2
            cnt = jnp.minimum(t_glob, float(POOL_WINDOWS[g]))
            p = (sm[HALO_U:] / cnt - u_ref[:, cs]).astype(BF16)
            p_ref[:, cs] = p
            pb = _dot(p, wp_ref[g]) + vec[V_B_POOL:V_B_POOL + 1]
            pooled_ref[:, cs] = (pb * vec[V_POOL_SCALE:V_POOL_SCALE + 1]).astype(BF16)

    col = lambda k: (lambda i: (i, k))
    wspec = pl.BlockSpec((N_GROUPS, GW, GW), lambda i: (0, 0, 0))
    return pl.pallas_call(
        body, name="mix_fwd", grid=(nb,),
        in_specs=[pl.BlockSpec((HALO_X, D), lambda i: (jnp.maximum(i * (tm // HALO_X) - 1, 0), 0)),
                  pl.BlockSpec((tm, D), col(0)),
                  pl.BlockSpec((tm, D), col(1)),
                  pl.BlockSpec((HALO_U, D), lambda i: (jnp.maximum(i * (tm // HALO_U) - 1, 0), 2)),
                  pl.BlockSpec((tm, D), col(2)),
                  pl.BlockSpec((16, D), lambda i: (0, 0)),
                  wspec, wspec, wspec],
        out_specs=[pl.BlockSpec((tm, D), lambda i: (i, 0))] * 5,
        out_shape=[jax.ShapeDtypeStruct((s, D), F32), jax.ShapeDtypeStruct((s, D), F32),
                   jax.ShapeDtypeStruct((s, D), BF16), jax.ShapeDtypeStruct((s, D), BF16),
                   jax.ShapeDtypeStruct((s, D), BF16)],
        scratch_shapes=[pltpu.VMEM((8, D), F32)],
        compiler_params=_params(("arbitrary",)),
    )(proj, proj, proj, proj, proj, vecs, w_rg_a, w_rg_x, w_pool)


def _branch_fwd(za, pooled, proj, x, modr, vecs, w_a, w_b, w_out):
    s = x.shape[0]
    tm = min(TM_BRANCH, s)

    def body(za_ref, pooled_ref, ga_ref, gb_ref, x_ref, mod_ref, vec_ref, wa_ref, wb_ref, wo_ref,
             ba_ref, bb_ref, merged_ref, o_ref, x2_ref, h2_ref):
        ba = _dot(za_ref[...], wa_ref[...])
        bb = _dot(pooled_ref[...], wb_ref[...])
        ba_ref[...] = ba.astype(BF16)
        bb_ref[...] = bb.astype(BF16)
        merged = (_sigmoid(ga_ref[...]) * ba + _sigmoid(gb_ref[...]) * bb).astype(BF16)
        merged_ref[...] = merged
        o = _dot(merged, wo_ref[...])
        o_ref[...] = o.astype(BF16)
        x2 = x_ref[...] + mod_ref[M_GT1:M_GT1 + 1, :] * o
        x2_ref[...] = x2
        r = lax.rsqrt(jnp.mean(x2 * x2, axis=-1, keepdims=True) + EPS)
        gain = vec_ref[V_G2:V_G2 + 1, :] * (1.0 + mod_ref[M_SC2:M_SC2 + 1, :])
        h2_ref[...] = (x2 * r * gain + mod_ref[M_SH2:M_SH2 + 1, :]).astype(BF16)

    tok = pl.BlockSpec((tm, D), lambda i: (i, 0))
    wspec = pl.BlockSpec((D, D), lambda i: (0, 0))
    sd = lambda dt: jax.ShapeDtypeStruct((s, D), dt)
    return pl.pallas_call(
        body, name="branch_fwd", grid=(s // tm,),
        in_specs=[tok, tok,
                  pl.BlockSpec((tm, D), lambda i: (i, 3)), pl.BlockSpec((tm, D), lambda i: (i, 4)),
                  tok, pl.BlockSpec((8, D), lambda i: (0, 0)), pl.BlockSpec((16, D), lambda i: (0, 0)),
                  wspec, wspec, wspec],
        out_specs=[tok] * 6,
        out_shape=[sd(BF16), sd(BF16), sd(BF16), sd(BF16), sd(F32), sd(BF16)],
        compiler_params=_params(("parallel",)),
    )(za, pooled, proj, proj, x, modr, vecs, w_a, w_b, w_out)


def _mlp_fwd(h2, x2, target, modr, vecs, w_up, w_down):
    s = x2.shape[0]
    tm, tf = min(TM_MLP, s), 512
    nj = D_FF // tf

    def body(h2_ref, x2_ref, tgt_ref, mod_ref, vec_ref, wu_ref, wd_ref,
             ru_ref, dx3_ref, ddn_ref, small_ref, acc_ref):
        i = pl.program_id(0)
        j = pl.program_id(1)

        @pl.when(jnp.logical_and(i == 0, j == 0))
        def _():
            small_ref[...] = jnp.zeros_like(small_ref)

        ru = jnp.maximum(_dot(h2_ref[...], wu_ref[...]), 0.0)
        ru_ref[...] = ru.astype(BF16)
        part = _dot((ru * ru).astype(BF16), wd_ref[...])

        @pl.when(j == 0)
        def _():
            acc_ref[...] = part

        @pl.when(j > 0)
        def _():
            acc_ref[...] += part

        @pl.when(j == nj - 1)
        def _():
            dn = acc_ref[...]
            gt2 = mod_ref[M_GT2:M_GT2 + 1, :]
            gf = vec_ref[V_GF:V_GF + 1, :]
            x3 = x2_ref[...] + gt2 * dn
            r3 = lax.rsqrt(jnp.mean(x3 * x3, axis=-1, keepdims=True) + EPS)
            n3 = x3 * r3
            err = n3 * gf - tgt_ref[...]
            dy = err * (1.0 / D)
            dn3 = dy * gf
            dx3 = r3 * (dn3 - n3 * jnp.mean(dn3 * n3, axis=-1, keepdims=True))
            dx3_ref[...] = dx3
            ddn_ref[...] = (dx3 * gt2).astype(BF16)
            small_ref[0:1, :] += jnp.sum(dy * n3, axis=0, keepdims=True)
            small_ref[1:2, :] += jnp.sum(dx3 * dn, axis=0, keepdims=True)
            small_ref[2:3, :] += (0.5 / D) * jnp.sum(err * err, axis=0, keepdims=True)

    tok = pl.BlockSpec((tm, D), lambda i, j: (i, 0))
    return pl.pallas_call(
        body, name="mlp_fwd", grid=(s // tm, nj),
        in_specs=[tok, tok, tok,
                  pl.BlockSpec((8, D), lambda i, j: (0, 0)), pl.BlockSpec((16, D), lambda i, j: (0, 0)),
                  pl.BlockSpec((D, tf), lambda i, j: (0, j)), pl.BlockSpec((tf, D), lambda i, j: (j, 0))],
        out_specs=[pl.BlockSpec((tm, tf), lambda i, j: (i, j)), tok, tok,
                   pl.BlockSpec((8, D), lambda i, j: (0, 0))],
        out_shape=[jax.ShapeDtypeStruct((s, D_FF), BF16), jax.ShapeDtypeStruct((s, D), F32),
                   jax.ShapeDtypeStruct((s, D), BF16), jax.ShapeDtypeStruct((8, D), F32)],
        scratch_shapes=[pltpu.VMEM((tm, D), F32)],
        compiler_params=_params(("arbitrary", "arbitrary")),
    )(h2, x2, target, modr, vecs, w_up, w_down)


def _mlp_bwd(d_dn, ru, x2, dx3, o, modr, vecs, w_up, w_down):
    s = x2.shape[0]
    tm, tf = min(TM_MLP, s), 512
    nj = D_FF // tf

    def body(ddn_ref, ru_ref, x2_ref, dx3_ref, o_ref, mod_ref, vec_ref, wu_ref, wd_ref,
             dup_ref, dx2_ref, do_ref, small_ref, acc_ref):
        i = pl.program_id(0)
        j = pl.program_id(1)

        @pl.when(jnp.logical_and(i == 0, j == 0))
        def _():
            small_ref[...] = jnp.zeros_like(small_ref)

        dff = _dot_nt(ddn_ref[...], wd_ref[...])
        dup = (dff * (2.0 * ru_ref[...].astype(F32))).astype(BF16)
        dup_ref[...] = dup
        part = _dot_nt(dup, wu_ref[...])

        @pl.when(j == 0)
        def _():
            acc_ref[...] = part

        @pl.when(j > 0)
        def _():
            acc_ref[...] += part

        @pl.when(j == nj - 1)
        def _():
            dh2 = acc_ref[...]
            x2 = x2_ref[...]
            r2 = lax.rsqrt(jnp.mean(x2 * x2, axis=-1, keepdims=True) + EPS)
            xn2 = x2 * r2
            gain = vec_ref[V_G2:V_G2 + 1, :] * (1.0 + mod_ref[M_SC2:M_SC2 + 1, :])
            dxn2 = dh2 * gain
            dx2 = dx3_ref[...] + r2 * (dxn2 - xn2 * jnp.mean(dxn2 * xn2, axis=-1, keepdims=True))
            dx2_ref[...] = dx2
            do_ref[...] = (dx2 * mod_ref[M_GT1:M_GT1 + 1, :]).astype(BF16)
            small_ref[0:1, :] += jnp.sum(dh2, axis=0, keepdims=True)
            small_ref[1:2, :] += jnp.sum(dh2 * xn2, axis=0, keepdims=True)
            small_ref[2:3, :] += jnp.sum(dx2 * o_ref[...].astype(F32), axis=0, keepdims=True)

    tok = pl.BlockSpec((tm, D), lambda i, j: (i, 0))
    chunk = pl.BlockSpec((tm, tf), lambda i, j: (i, j))
    return pl.pallas_call(
        body, name="mlp_bwd", grid=(s // tm, nj),
        in_specs=[tok, chunk, tok, tok, tok,
                  pl.BlockSpec((8, D), lambda i, j: (0, 0)), pl.BlockSpec((16, D), lambda i, j: (0, 0)),
                  pl.BlockSpec((D, tf), lambda i, j: (0, j)), pl.BlockSpec((tf, D), lambda i, j: (j, 0))],
        out_specs=[chunk, tok, tok, pl.BlockSpec((8, D), lambda i, j: (0, 0))],
        out_shape=[jax.ShapeDtypeStruct((s, D_FF), BF16), jax.ShapeDtypeStruct((s, D), F32),
                   jax.ShapeDtypeStruct((s, D), BF16), jax.ShapeDtypeStruct((8, D), F32)],
        scratch_shapes=[pltpu.VMEM((tm, D), F32)],
        compiler_params=_params(("arbitrary", "arbitrary")),
    )(d_dn, ru, x2, dx3, o, modr, vecs, w_up, w_down)


def _branch_bwd(do, proj, ba, bb, w_a, w_b, w_out, dep):
    s = do.shape[0]
    tm = min(TM_BRANCH, s)

    def body(do_ref, ga_ref, gb_ref, ba_ref, bb_ref, wa_ref, wb_ref, wo_ref, dep_ref,
             dba_ref, dbb_ref, dg_ref, dza_ref, dpooled_ref):
        dmerged = _dot_nt(do_ref[...], wo_ref[...])
        sa = _sigmoid(ga_ref[...])
        sb = _sigmoid(gb_ref[...])
        dba = (dmerged * sa).astype(BF16)
        dbb = (dmerged * sb).astype(BF16)
        dba_ref[...] = dba
        dbb_ref[...] = dbb
        dg_ref[:, :D] = (dmerged * ba_ref[...].astype(F32) * sa * (1.0 - sa)).astype(BF16)
        dg_ref[:, D:] = (dmerged * bb_ref[...].astype(F32) * sb * (1.0 - sb)).astype(BF16)
        dza_ref[...] = _dot_nt(dba, wa_ref[...])
        dpooled_ref[...] = _dot_nt(dbb, wb_ref[...])

    tok = pl.BlockSpec((tm, D), lambda i: (i, 0))
    wspec = pl.BlockSpec((D, D), lambda i: (0, 0))
    sd = lambda dt: jax.ShapeDtypeStruct((s, D), dt)
    return pl.pallas_call(
        body, name="branch_bwd", grid=(s // tm,),
        in_specs=[tok, pl.BlockSpec((tm, D), lambda i: (i, 3)), pl.BlockSpec((tm, D), lambda i: (i, 4)),
                  tok, tok, wspec, wspec, wspec, pl.BlockSpec(memory_space=pl.ANY)],
        out_specs=[tok, tok, pl.BlockSpec((tm, 2 * D), lambda i: (i, 0)), tok, tok],
        out_shape=[sd(BF16), sd(BF16), jax.ShapeDtypeStruct((s, 2 * D), BF16), sd(F32), sd(F32)],
        compiler_params=_params(("parallel",)),
    )(do, proj, proj, ba, bb, w_a, w_b, w_out, dep)


def _mix_bwd(dza, dpooled, proj, xr, hr, p, dgates, vecs, w_rg_a, w_rg_x, w_pool):
    s = xr.shape[0]
    tm = min(TM_MIX, s)
    nb = s // tm

    def body(dza_ref, dpooled_ref, xh_ref, x_ref, y_ref, xr_ref, hh_ref, hr_ref, p_ref, dg_ref,
             vec_ref, wa_ref, wx_ref, wp_ref,
             dproj_ref, dwa_ref, dwx_ref, dwp_ref, small_ref,
             scan_carry, dxr_carry, q_carry):
        i = pl.program_id(0)
        bi = nb - 1 - i
        first_t = bi == 0

        @pl.when(i == 0)
        def _():
            scan_carry[...] = jnp.zeros_like(scan_carry)
            dxr_carry[...] = jnp.zeros_like(dxr_carry)
            q_carry[...] = jnp.zeros_like(q_carry)
            dwa_ref[...] = jnp.zeros_like(dwa_ref)
            dwx_ref[...] = jnp.zeros_like(dwx_ref)
            dwp_ref[...] = jnp.zeros_like(dwp_ref)
            small_ref[...] = jnp.zeros_like(small_ref)

        row = lax.broadcasted_iota(jnp.int32, (tm, GW), 0)
        is_t0 = jnp.logical_and(first_t, row == 0)
        t_glob = (row + bi * tm + 1).astype(F32)
        colsum = lambda v: jnp.sum(v, axis=0, keepdims=True)
        for g in range(N_GROUPS):
            cs = slice(g * GW, (g + 1) * GW)
            vec = vec_ref[:, cs]
            xr = xr_ref[:, cs]
            hr = hr_ref[:, cs]
            dza = dza_ref[:, cs]
            ga, dga = _gelu_and_grad(y_ref[:, cs])
            dproj_ref[:, D + g * GW:D + (g + 1) * GW] = (dza * hr * dga).astype(BF16)
            dhr = dza * ga
            ra, ri, sp, a, mult = _rglru_gates(
                xr, wa_ref[g], wx_ref[g], vec[V_B_RG_A:V_B_RG_A + 1], vec[V_B_RG_X:V_B_RG_X + 1],
                vec[V_A_PARAM:V_A_PARAM + 1], is_t0)
            m = jnp.where(row == tm - 1, 0.0, _shift_up(a, 1))
            gsum = dhr + jnp.where(row == tm - 1, scan_carry[0:1, cs], 0.0)
            k = 1
            while k < tm:
                gsum = gsum + m * jnp.where(row < tm - k, _shift_up(gsum, k), 0.0)
                if 2 * k < tm:
                    m = m * _shift_up(m, k)
                k *= 2
            scan_carry[:, cs] = (a * gsum)[0:8, :]
            hh = jnp.where(first_t, 0.0, hh_ref[:, cs])
            hprev = _shift_down(jnp.concatenate([hh, hr], axis=0), 1)[8:]
            da = gsum * hprev
            dmult = jnp.where(is_t0, 0.0, gsum * xr * ri)
            dlog_a = da * a - dmult * a * a / mult
            dri = gsum * xr * mult
            dxr = gsum * ri * mult
            small_ref[7:8, cs] += colsum((-C_RG) * ra * dlog_a)
            dpa = (((-C_RG) * sp) * dlog_a * ra * (1.0 - ra))
            dpx = dri * ri * (1.0 - ri)
            small_ref[5:6, cs] += colsum(dpa)
            small_ref[6:7, cs] += colsum(dpx)
            dpa = dpa.astype(BF16)
            dpx = dpx.astype(BF16)
            xrb = xr.astype(BF16)
            dwa_ref[g] += _dot_tn(xrb, dpa)
            dwx_ref[g] += _dot_tn(xrb, dpx)
            dxr = dxr + _dot_nt(dpa, wa_ref[g]) + _dot_nt(dpx, wx_ref[g])
            small_ref[4:5, cs] += colsum(dxr)
            xh = jnp.where(first_t, 0.0, xh_ref[:, cs])
            taps = _conv_taps(jnp.concatenate([xh, x_ref[:, cs]], axis=0))
            dxr_ext = jnp.concatenate([dxr, dxr_carry[:, cs]], axis=0)
            dx = vec[V_CONV_W + 3:V_CONV_W + 4] * dxr
            for j in range(4):
                small_ref[j:j + 1, cs] += colsum(dxr * taps[j])
                if j < 3:
                    dx = dx + vec[V_CONV_W + j:V_CONV_W + j + 1] * _shift_up(dxr_ext, 3 - j)[:tm]
            dxr_carry[:, cs] = dxr[0:8, :]
            dproj_ref[:, cs] = dx.astype(BF16)
            pg = p_ref[:, cs]
            dpooled = dpooled_ref[:, cs]
            pb = _dot(pg, wp_ref[g]) + vec[V_B_POOL:V_B_POOL + 1]
            small_ref[9:10, cs] += colsum(dpooled * pb)
            dpb = dpooled * vec[V_POOL_SCALE:V_POOL_SCALE + 1]
            small_ref[8:9, cs] += colsum(dpb)
            dpbb = dpb.astype(BF16)
            dwp_ref[g] += _dot_tn(pg, dpbb)
            dp = _dot_nt(dpbb, wp_ref[g])
            q = dp / jnp.minimum(t_glob, float(POOL_WINDOWS[g]))
            sm = jnp.concatenate([q, q_carry[:, cs]], axis=0)
            k = 1
            while k < POOL_WINDOWS[g]:
                sm = sm + _shift_up(sm, k)
                k *= 2
            q_carry[:, cs] = q[0:HALO_U, :]
            dproj_ref[:, 2 * D + g * GW:2 * D + (g + 1) * GW] = (sm[:tm] - dp).astype(BF16)
        dproj_ref[:, 3 * D:] = dg_ref[...]

    rev = lambda i: nb - 1 - i
    tok = pl.BlockSpec((tm, D), lambda i: (rev(i), 0))
    col = lambda k: pl.BlockSpec((tm, D), lambda i: (rev(i), k))
    halo8 = lambda k: pl.BlockSpec((8, D), lambda i: (jnp.maximum(rev(i) * (tm // 8) - 1, 0), k))
    wspec = pl.BlockSpec((N_GROUPS, GW, GW), lambda i: (0, 0, 0))
    wshape = jax.ShapeDtypeStruct((N_GROUPS, GW, GW), F32)
    return pl.pallas_call(
        body, name="mix_bwd", grid=(nb,),
        in_specs=[tok, tok, halo8(0), col(0), col(1), tok, halo8(0), tok, tok,
                  pl.BlockSpec((tm, 2 * D), lambda i: (rev(i), 0)),
                  pl.BlockSpec((16, D), lambda i: (0, 0)), wspec, wspec, wspec],
        out_specs=[pl.BlockSpec((tm, D_IN), lambda i: (rev(i), 0)), wspec, wspec, wspec,
                   pl.BlockSpec((16, D), lambda i: (0, 0))],
        out_shape=[jax.ShapeDtypeStruct((s, D_IN), BF16), wshape, wshape, wshape,
                   jax.ShapeDtypeStruct((16, D), F32)],
        scratch_shapes=[pltpu.VMEM((8, D), F32), pltpu.VMEM((8, D), F32), pltpu.VMEM((HALO_U, D), F32)],
        compiler_params=_params(("arbitrary",)),
    )(dza, dpooled, proj, proj, proj, xr, hr, hr, p, dgates, vecs, w_rg_a, w_rg_x, w_pool)


def _proj_bwd(dproj, x, dx2, modr, vecs, w_in):
    s = x.shape[0]
    tm, tk = min(TM_PROJ, s), 1024
    nk = D_IN // tk

    def body(dp_ref, x_ref, dx2_ref, mod_ref, vec_ref, w_ref, gx_ref, small_ref, acc_ref):
        i = pl.program_id(0)
        k = pl.program_id(1)

        @pl.when(jnp.logical_and(i == 0, k == 0))
        def _():
            small_ref[...] = jnp.zeros_like(small_ref)

        part = _dot_nt(dp_ref[...], w_ref[...])

        @pl.when(k == 0)
        def _():
            acc_ref[...] = part

        @pl.when(k > 0)
        def _():
            acc_ref[...] += part

        @pl.when(k == nk - 1)
        def _():
            dh1 = acc_ref[...]
            xv = x_ref[...]
            r1 = lax.rsqrt(jnp.mean(xv * xv, axis=-1, keepdims=True) + EPS)
            xn1 = xv * r1
            gain = vec_ref[V_G1:V_G1 + 1, :] * (1.0 + mod_ref[M_SC1:M_SC1 + 1, :])
            dxn1 = dh1 * gain
            gx_ref[...] = dx2_ref[...] + r1 * (dxn1 - xn1 * jnp.mean(dxn1 * xn1, axis=-1, keepdims=True))
            small_ref[0:1, :] += jnp.sum(dh1, axis=0, keepdims=True)
            small_ref[1:2, :] += jnp.sum(dh1 * xn1, axis=0, keepdims=True)

    tok = pl.BlockSpec((tm, D), lambda i, k: (i, 0))
    return pl.pallas_call(
        body, name="proj_bwd", grid=(s // tm, nk),
        in_specs=[pl.BlockSpec((tm, tk), lambda i, k: (i, k)), tok, tok,
                  pl.BlockSpec((8, D), lambda i, k: (0, 0)), pl.BlockSpec((16, D), lambda i, k: (0, 0)),
                  pl.BlockSpec((D, tk), lambda i, k: (0, k))],
        out_specs=[tok, pl.BlockSpec((8, D), lambda i, k: (0, 0))],
        out_shape=[jax.ShapeDtypeStruct((s, D), F32), jax.ShapeDtypeStruct((8, D), F32)],
        scratch_shapes=[pltpu.VMEM((tm, D), F32)],
        compiler_params=_params(("arbitrary", "arbitrary")),
    )(dproj, x, dx2, modr, vecs, w_in)


def _wgrad(a, b, name, square_a=False, dep=None):
    s, ka = a.shape
    n = b.shape[1]
    tka, tn, ts = 1024, 1024, min(TS_WGRAD, s)
    ns = s // ts
    deps = [] if dep is None else [dep]

    def body(a_ref, b_ref, *refs):
        out_ref, acc_ref = refs[-2:]
        t = pl.program_id(2)
        av = a_ref[...]
        if square_a:
            af = av.astype(F32)
            av = (af * af).astype(BF16)
        part = _dot_tn(av, b_ref[...])

        @pl.when(t == 0)
        def _():
            acc_ref[...] = part

        @pl.when(t > 0)
        def _():
            acc_ref[...] += part

        @pl.when(t == ns - 1)
        def _():
            out_ref[...] = acc_ref[...].astype(BF16)

    return pl.pallas_call(
        body, name=name, grid=(ka // tka, n // tn, ns),
        in_specs=[pl.BlockSpec((ts, tka), lambda i, j, t: (t, i)),
                  pl.BlockSpec((ts, tn), lambda i, j, t: (t, j))] + [pl.BlockSpec(memory_space=pl.ANY)] * len(deps),
        out_specs=pl.BlockSpec((tka, tn), lambda i, j, t: (i, j)),
        out_shape=jax.ShapeDtypeStruct((ka, n), BF16),
        scratch_shapes=[pltpu.VMEM((tka, tn), F32)],
        compiler_params=_params(("parallel", "parallel", "arbitrary")),
    )(a, b, *deps)


def _window(ref, kind, idx, size):
    start = pl.multiple_of(idx * size, size)
    if kind == 0:
        return ref.at[pl.ds(start, size)]
    if kind == 1:
        return ref.at[:, pl.ds(start, size)]
    return ref.at[:, :, pl.ds(start, size)]


def _mesh_place():
    x, y, c = lax.axis_index("x"), lax.axis_index("y"), lax.axis_index("c")
    return x, y, c, 4 * x + 2 * y + c


def _peer(x, y, c, q):
    px = 1 - x if q & 4 else x
    py = 1 - y if q & 2 else y
    pc = 1 - c if q & 1 else c
    return (px, py, pc), 4 * px + 2 * py + pc


def _all_gather(shards, kinds, name, dep=None):
    n = len(shards)
    deps = [] if dep is None else [dep]
    full_shapes = []
    for sh, kind in zip(shards, kinds):
        dims = list(sh.shape)
        dims[kind] *= N_DEV
        full_shapes.append(jax.ShapeDtypeStruct(tuple(dims), sh.dtype))

    def body(*refs):
        ins, outs = refs[:n], refs[n + len(deps):2 * n + len(deps)]
        send_sems, recv_sems, local_sems = refs[2 * n + len(deps):]
        x, y, c, me = _mesh_place()
        sends, recvs, locals_ = [], [], []
        for k in range(n):
            size = shards[k].shape[kinds[k]]
            mine = _window(outs[k], kinds[k], me, size)
            lc = pltpu.make_async_copy(ins[k], mine, local_sems.at[k])
            lc.start()
            locals_.append(lc)
            for q in range(1, N_DEV):
                peer, peer_idx = _peer(x, y, c, q)
                cp = pltpu.make_async_remote_copy(
                    src_ref=ins[k], dst_ref=mine, send_sem=send_sems.at[k, q], recv_sem=recv_sems.at[k, q],
                    device_id=peer, device_id_type=MESH)
                cp.start()
                sends.append(cp)
                recvs.append(pltpu.make_async_remote_copy(
                    src_ref=ins[k], dst_ref=_window(outs[k], kinds[k], peer_idx, size),
                    send_sem=send_sems.at[k, q], recv_sem=recv_sems.at[k, q],
                    device_id=peer, device_id_type=MESH))
        for cp in recvs:
            cp.wait_recv()
        for cp in sends:
            cp.wait_send()
        for lc in locals_:
            lc.wait()

    any_spec = pl.BlockSpec(memory_space=pl.ANY)
    return pl.pallas_call(
        body, name=name,
        in_specs=[any_spec] * (n + len(deps)), out_specs=[any_spec] * n, out_shape=full_shapes,
        scratch_shapes=[pltpu.SemaphoreType.DMA((n, N_DEV)), pltpu.SemaphoreType.DMA((n, N_DEV)),
                        pltpu.SemaphoreType.DMA((n,))],
    )(*shards, *deps)


_HBM = pl.BlockSpec(memory_space=pltpu.HBM)
_SEM = pl.BlockSpec(memory_space=pltpu.SEMAPHORE)
_EFFECT = pltpu.SideEffectType.DATAFLOW_SIDE_EFFECTING


def _exchange_copies(gather, kinds, src_refs, land_refs, send_sems, recv_sems):
    x, y, c, me = _mesh_place()
    sends, recvs = [], []
    for k in range(len(kinds)):
        for q in range(1, N_DEV):
            peer, peer_idx = _peer(x, y, c, q)
            if gather:
                size = src_refs[k].shape[kinds[k]]
                src = src_refs[k]
                dst = _window(land_refs[k], kinds[k], me, size)
                arriving = _window(land_refs[k], kinds[k], peer_idx, size)
            else:
                size = src_refs[k].shape[kinds[k]] // N_DEV
                src = _window(src_refs[k], kinds[k], peer_idx, size)
                dst = land_refs[k].at[me]
                arriving = land_refs[k].at[peer_idx]
            sems = dict(send_sem=send_sems.at[k * N_DEV + q], recv_sem=recv_sems.at[k * N_DEV + q],
                        device_id=peer, device_id_type=MESH)
            sends.append(pltpu.make_async_remote_copy(src_ref=src, dst_ref=dst, **sems))
            recvs.append(pltpu.make_async_remote_copy(src_ref=src, dst_ref=arriving, **sems))
    return sends, recvs


def _exchange_start(gather, srcs, lands, kinds, after, name):
    n = len(srcs)

    def body(*refs):
        src_refs, land_refs = refs[:n], refs[n:2 * n]
        send_sems, recv_sems = refs[2 * n + 1], refs[2 * n + 2]
        token = refs[-1]
        sends, _ = _exchange_copies(gather, kinds, src_refs, land_refs, send_sems, recv_sems)
        for cp in sends:
            cp.start()
        token[...] = jnp.zeros_like(token)

    hbm = lambda a: pltpu.HBM(a.shape, a.dtype)
    outs = pl.pallas_call(
        body, name=name,
        out_shape=(pltpu.SemaphoreType.DMA((n * N_DEV,)), pltpu.SemaphoreType.DMA((n * N_DEV,)),
                   *[hbm(a) for a in srcs], *[hbm(a) for a in lands], jax.ShapeDtypeStruct((8, 128), F32)),
        in_specs=[_HBM] * (2 * n) + [pl.BlockSpec(memory_space=pl.ANY)],
        out_specs=(_SEM, _SEM, *[_HBM] * (2 * n), pl.BlockSpec(memory_space=pltpu.VMEM)),
        input_output_aliases={i: 2 + i for i in range(2 * n)},
        compiler_params=pltpu.CompilerParams(has_side_effects=_EFFECT),
    )(*[pltpu.with_memory_space_constraint(a, pltpu.HBM) for a in (*srcs, *lands)], after)
    return outs[0], outs[1], outs[2:2 + n], outs[2 + n:2 + 2 * n], outs[-1]


def _exchange_wait(gather, started, kinds, after, name):
    send_sems, recv_sems, srcs, lands, _ = started
    n = len(srcs)

    def body(*refs):
        src_refs, land_refs = refs[:n], refs[n:2 * n]
        sends, recvs = _exchange_copies(gather, kinds, src_refs, land_refs, refs[2 * n], refs[2 * n + 1])
        for cp in sends:
            cp.wait_send()
        for cp in recvs:
            cp.wait_recv()

    hbm = lambda a: pltpu.HBM(a.shape, a.dtype)
    outs = pl.pallas_call(
        body, name=name,
        out_shape=(*[hbm(a) for a in srcs], *[hbm(a) for a in lands]),
        in_specs=[_HBM] * (2 * n) + [_SEM, _SEM, pl.BlockSpec(memory_space=pl.ANY)],
        out_specs=tuple([_HBM] * (2 * n)),
        input_output_aliases={i: i for i in range(2 * n)},
        compiler_params=pltpu.CompilerParams(has_side_effects=_EFFECT),
    )(*srcs, *lands, send_sems, recv_sems, after)
    return outs[n:]


def _after(small, token):
    return small + token[0:1, 0:1].astype(small.dtype)


def _own_window(kind, shard, me):
    dims = list(shard.shape)
    dims[kind] *= N_DEV
    start = [0] * len(dims)
    start[kind] = me * shard.shape[kind]
    return lax.dynamic_update_slice(lax.empty(tuple(dims), shard.dtype), shard, tuple(start))


def _own_slot(kind, full, me):
    size = full.shape[kind] // N_DEV
    mine = lax.dynamic_slice_in_dim(full, me * size, size, axis=kind)
    return lax.dynamic_update_index_in_dim(lax.empty((N_DEV, *mine.shape), full.dtype), mine, me, 0)


def _silu(c):
    return c * _sigmoid(c)


def _ada_fwd(c_all, w_ada, b_ada_cols):
    def body(c_ref, w_ref, b_ref, out_ref):
        out_ref[...] = jnp.dot(_silu(c_ref[...]), w_ref[...], preferred_element_type=F32,
                               precision=lax.Precision.HIGHEST) + b_ref[...]

    return pl.pallas_call(
        body, name="ada_fwd", out_shape=jax.ShapeDtypeStruct((N_DEV, w_ada.shape[1]), F32),
    )(c_all, w_ada, b_ada_cols)


def _adam(w, g, m, v):
    m = ADAM_B1 * m + (1.0 - ADAM_B1) * g
    v = ADAM_B2 * v + (1.0 - ADAM_B2) * (g * g)
    m_hat = m / (1.0 - ADAM_B1 ** ADAM_STEP)
    v_hat = v / (1.0 - ADAM_B2 ** ADAM_STEP)
    delta = -ADAM_LR * (m_hat / (jnp.sqrt(v_hat) + ADAM_EPS) + ADAM_WD * w)
    return delta, m, v


def _ada_bwd_adam(c_all, dmod_cols, w, m, v):
    def body(c_ref, d_ref, w_ref, m_ref, v_ref, g_ref, delta_ref, nm_ref, nv_ref):
        g = lax.dot_general(_silu(c_ref[...]), d_ref[...], (((0,), (0,)), ((), ())),
                            preferred_element_type=F32, precision=lax.Precision.HIGHEST)
        g_ref[...] = g
        delta_ref[...], nm_ref[...], nv_ref[...] = _adam(w_ref[...], g, m_ref[...], v_ref[...])

    sd = jax.ShapeDtypeStruct(w.shape, F32)
    return pl.pallas_call(body, name="ada_bwd_adam", out_shape=[sd] * 4,
                          compiler_params=pltpu.CompilerParams(vmem_limit_bytes=V7X_VMEM_LIMIT),
                          )(c_all, dmod_cols, w, m, v)


def _sum_slots_adam(slots, w, m, v, name):
    r, cdim = w.shape
    tr = min(r, 128)

    def body(s_ref, w_ref, m_ref, v_ref, g_ref, delta_ref, nm_ref, nv_ref):
        g = s_ref[0].astype(F32)
        for p in range(1, N_DEV):
            g = g + s_ref[p].astype(F32)
        g_ref[...] = g
        delta_ref[...], nm_ref[...], nv_ref[...] = _adam(w_ref[...], g, m_ref[...], v_ref[...])

    blk = pl.BlockSpec((tr, cdim), lambda i: (i, 0))
    sd = jax.ShapeDtypeStruct((r, cdim), F32)
    return pl.pallas_call(
        body, name=name, grid=(r // tr,),
        in_specs=[pl.BlockSpec((N_DEV, tr, cdim), lambda i: (0, i, 0)), blk, blk, blk],
        out_specs=[blk] * 4, out_shape=[sd] * 4,
        compiler_params=_params(("parallel",)),
    )(slots, w, m, v)


N_SMALL = 40
N_SMALL_PARAMS = 11


def _pack_vecs(conv_w_full, rows):
    def body(cw_ref, *refs):
        out = refs[-1]
        out[...] = jnp.zeros_like(out)
        out[0:4, :] = cw_ref[0:4, :]
        for r, ref in enumerate(refs[:-1]):
            out[4 + r:5 + r, :] = ref[...]

    return pl.pallas_call(body, name="pack_vecs", out_shape=jax.ShapeDtypeStruct((16, D), F32))(conv_w_full, *rows)


def _small_finish(gathered, mod_all, vecs, ws, ms, vs):
    n = N_SMALL_PARAMS

    def body(g_ref, mod_ref, vec_ref, *refs):
        w_refs, m_refs, v_refs = refs[:n], refs[n:2 * n], refs[2 * n:3 * n]
        outs = refs[3 * n:]
        g1 = vec_ref[V_G1:V_G1 + 1, :]
        g2 = vec_ref[V_G2:V_G2 + 1, :]
        zero = jnp.zeros((1, D), F32)
        dg1, dg2, dgf, loss_lanes = zero, zero, zero, zero
        mixer = jnp.zeros((16, D), F32)
        db_ada = jnp.zeros((6, D), F32)
        for b in range(N_DEV):
            gb = g_ref[b]
            mod = mod_ref[b]
            q1 = gb[33:34]
            q2 = gb[9:10]
            dmod = jnp.concatenate([gb[32:33], q1 * g1, gb[10:11], gb[8:9], q2 * g2, gb[1:2]], axis=0)
            outs[4 * n][b] = dmod
            db_ada = db_ada + dmod
            dg1 = dg1 + q1 * (1.0 + mod[M_SC1:M_SC1 + 1])
            dg2 = dg2 + q2 * (1.0 + mod[M_SC2:M_SC2 + 1])
            dgf = dgf + gb[0:1]
            loss_lanes = loss_lanes + gb[2:3]
            mixer = mixer + gb[16:32]
        d_a_param = mixer[7:8] * _sigmoid(vec_ref[V_A_PARAM:V_A_PARAM + 1, :])
        grads = [dg1, dg2, mixer[4:5], mixer[5:6], mixer[6:7], d_a_param, mixer[8:9], mixer[9:10], dgf,
                 db_ada, mixer[0:4]]
        for k in range(n):
            outs[k][...] = grads[k]
            outs[n + k][...], outs[2 * n + k][...], outs[3 * n + k][...] = _adam(
                w_refs[k][...], grads[k], m_refs[k][...], v_refs[k][...])
        outs[4 * n + 1][...] = jnp.broadcast_to(jnp.sum(loss_lanes, axis=1, keepdims=True), (8, 128))

    shapes = [jax.ShapeDtypeStruct(w.shape, F32) for w in ws]
    return pl.pallas_call(
        body, name="small_finish",
        out_shape=shapes * 4 + [jax.ShapeDtypeStruct((N_DEV, 6, D), F32), jax.ShapeDtypeStruct((8, 128), F32)],
    )(gathered, mod_all, vecs, *ws, *ms, *vs)


def _pad_rows(a, rows):
    return jnp.pad(a, ((0, rows - a.shape[0]), (0, 0)))


def kernel(x, c, norm_mix_g, norm_mlp_g, w_ada, b_ada, w_in, conv_w, conv_b, w_rg_a, b_rg_a, w_rg_x, b_rg_x, a_param, w_branch_a, w_pool, b_pool, pool_scale, w_branch_b, w_out, w_up, w_down, final_g, loss_target, m_norm_mix_g, m_norm_mlp_g, m_w_ada, m_b_ada, m_w_in, m_conv_w, m_conv_b, m_w_rg_a, m_b_rg_a, m_w_rg_x, m_b_rg_x, m_a_param, m_w_branch_a, m_w_pool, m_b_pool, m_pool_scale, m_w_branch_b, m_w_out, m_w_up, m_w_down, m_final_g, v_norm_mix_g, v_norm_mlp_g, v_w_ada, v_b_ada, v_w_in, v_conv_w, v_conv_b, v_w_rg_a, v_b_rg_a, v_w_rg_x, v_b_rg_x, v_a_param, v_w_branch_a, v_w_pool, v_b_pool, v_pool_scale, v_w_branch_b, v_w_out, v_w_up, v_w_down, v_final_g):
    me = 4 * lax.axis_index("x") + 2 * lax.axis_index("y") + lax.axis_index("c")
    s = x.shape[1]
    x2d = x.reshape(s, D)
    target = loss_target.reshape(s, D)
    n_ada = w_ada.shape[2]

    sharded = dict(w_in=(w_in[0], 1), w_up=(w_up[0], 1), w_down=(w_down[0], 0), w_branch_a=(w_branch_a[0], 0),
                   w_branch_b=(w_branch_b[0], 0), w_out=(w_out[0], 0), w_rg_a=(w_rg_a[0], 1), w_rg_x=(w_rg_x[0], 1),
                   w_pool=(w_pool[0], 1))
    kind = {k: v[1] for k, v in sharded.items()}
    shard = {k: v[0].astype(BF16) for k, v in sharded.items()}

    w_in_full, conv_w_full, c_rows = _all_gather([shard["w_in"], _pad_rows(conv_w[0], 8), _pad_rows(c, 8)],
                                                 [1, 1, 0], "gather_first")
    c_all = c_rows.reshape(N_DEV, 8, D)[:, 0, :]

    b_ada_cols = lax.dynamic_slice(b_ada, (0, me * n_ada), (1, n_ada))
    mod_part = _ada_fwd(c_all, w_ada[0], b_ada_cols)
    mod_parts, = _all_gather([mod_part], [0], "gather_mod")
    mod_all = jnp.transpose(mod_parts.reshape(N_DEV, N_DEV, n_ada), (1, 0, 2)).reshape(N_DEV, 6, D)
    mod_all = jnp.pad(mod_all, ((0, 0), (0, 2), (0, 0)))
    modr = lax.dynamic_index_in_dim(mod_all, me, 0, keepdims=False)
    vecs = _pack_vecs(conv_w_full, [conv_b, b_rg_a, b_rg_x, a_param, b_pool, pool_scale,
                                    norm_mix_g, norm_mlp_g, final_g.reshape(1, D)])

    mixer_names = ["w_rg_a", "w_rg_x", "w_pool", "w_branch_a", "w_branch_b", "w_out"]
    mlp_names = ["w_up", "w_down"]

    def start_gather(group, after, name):
        return _exchange_start(True, [shard[k] for k in group], [_own_window(kind[k], shard[k], me) for k in group],
                               [kind[k] for k in group], after, name)

    g_mixer = start_gather(mixer_names, modr, "gather_mixer_start")
    g_mlp = start_gather(mlp_names, g_mixer[-1], "gather_mlp_start")

    proj, h1 = _proj_fwd(x2d, _after(modr, g_mlp[-1]), vecs, w_in_full)
    wg = dict(zip(mixer_names, _exchange_wait(True, g_mixer, [kind[k] for k in mixer_names], h1, "gather_mixer_wait")))
    xr, hr, za, p, pooled = _mix_fwd(proj, vecs, wg["w_rg_a"], wg["w_rg_x"], wg["w_pool"])
    ba, bb, merged, o, x2, h2 = _branch_fwd(za, pooled, proj, x2d, modr, vecs,
                                            wg["w_branch_a"], wg["w_branch_b"], wg["w_out"])
    wg.update(zip(mlp_names, _exchange_wait(True, g_mlp, [kind[k] for k in mlp_names], h2, "gather_mlp_wait")))
    ru, dx3, d_dn, small_f = _mlp_fwd(h2, x2, target, modr, vecs, wg["w_up"], wg["w_down"])

    def start_scatter(group, partial, after, name):
        return _exchange_start(False, [partial[k] for k in group], [_own_slot(kind[k], partial[k], me) for k in group],
                               [kind[k] for k in group], after, name)

    dup, dx2, do, small_m = _mlp_bwd(d_dn, ru, x2, dx3, o, modr, vecs, wg["w_up"], wg["w_down"])
    partial = dict(w_up=_wgrad(h2, dup, "wgrad_up"), w_down=_wgrad(ru, d_dn, "wgrad_down", square_a=True))
    s_mlp = start_scatter(mlp_names, partial, dx2, "scatter_mlp_start")

    dba, dbb, dgates, dza, dpooled = _branch_bwd(do, proj, ba, bb, wg["w_branch_a"], wg["w_branch_b"], wg["w_out"],
                                                 dep=s_mlp[-1])
    dproj, dw_rg_a, dw_rg_x, dw_pool, small_x = _mix_bwd(dza, dpooled, proj, xr, hr, p, dgates, vecs,
                                                         wg["w_rg_a"], wg["w_rg_x"], wg["w_pool"])
    partial.update(w_branch_a=_wgrad(za, dba, "wgrad_branch_a"), w_branch_b=_wgrad(pooled, dbb, "wgrad_branch_b"),
                   w_out=_wgrad(merged, do, "wgrad_out"),
                   w_rg_a=dw_rg_a.astype(BF16), w_rg_x=dw_rg_x.astype(BF16), w_pool=dw_pool.astype(BF16))
    s_mixer = start_scatter(mixer_names, partial, s_mlp[-1], "scatter_mixer_start")

    partial["w_in"] = _wgrad(h1, dproj, "wgrad_in", dep=s_mixer[-1])
    s_in = start_scatter(["w_in"], partial, s_mixer[-1], "scatter_in_start")
    grad_x, small_p = _proj_bwd(dproj, x2d, dx2, _after(modr, s_in[-1]), vecs, w_in_full)

    locals_ = dict(w_in=(w_in, m_w_in, v_w_in), w_up=(w_up, m_w_up, v_w_up), w_down=(w_down, m_w_down, v_w_down),
                   w_branch_a=(w_branch_a, m_w_branch_a, v_w_branch_a),
                   w_branch_b=(w_branch_b, m_w_branch_b, v_w_branch_b), w_out=(w_out, m_w_out, v_w_out),
                   w_rg_a=(w_rg_a, m_w_rg_a, v_w_rg_a), w_rg_x=(w_rg_x, m_w_rg_x, v_w_rg_x),
                   w_pool=(w_pool, m_w_pool, v_w_pool))
    res = {}

    def finish(group, started, after, name):
        slots = _exchange_wait(False, started, [kind[k] for k in group], after, name)
        for k, sl in zip(group, slots):
            w, m, v = locals_[k]
            shape2d = (-1, w.shape[-1])
            sl = sl.reshape(N_DEV, *w.reshape(shape2d).shape)
            outs = _sum_slots_adam(sl, w.reshape(shape2d), m.reshape(shape2d), v.reshape(shape2d), "adam_" + k)
            res[k] = [t.reshape(w.shape) for t in outs]
        return res[group[-1]][0]

    done = finish(mlp_names, s_mlp, grad_x, "scatter_mlp_wait")
    done = finish(mixer_names, s_mixer, done, "scatter_mixer_wait")
    done = finish(["w_in"], s_in, done, "scatter_in_wait")

    small = jnp.concatenate([small_f, small_m, small_x, small_p], axis=0)
    small_all, = _all_gather([small], [0], "gather_small", dep=done)
    small_all = small_all.reshape(N_DEV, N_SMALL, D)

    def embed(cw):
        return lax.dynamic_update_slice(jnp.zeros((4, D), F32), cw[0], (0, me * (D // N_DEV)))

    def smalls(ng, nl, cb, bra, brx, ap, bp, ps, fg, ba_, cw):
        return [ng, nl, cb, bra, brx, ap, bp, ps, fg.reshape(1, D), ba_.reshape(6, D), embed(cw)]

    small_names = ["norm_mix_g", "norm_mlp_g", "conv_b", "b_rg_a", "b_rg_x", "a_param", "b_pool", "pool_scale",
                   "final_g", "b_ada", "conv_w"]
    fin = _small_finish(
        small_all, mod_all, vecs,
        smalls(norm_mix_g, norm_mlp_g, conv_b, b_rg_a, b_rg_x, a_param, b_pool, pool_scale, final_g, b_ada, conv_w),
        smalls(m_norm_mix_g, m_norm_mlp_g, m_conv_b, m_b_rg_a, m_b_rg_x, m_a_param, m_b_pool, m_pool_scale,
               m_final_g, m_b_ada, m_conv_w),
        smalls(v_norm_mix_g, v_norm_mlp_g, v_conv_b, v_b_rg_a, v_b_rg_x, v_a_param, v_b_pool, v_pool_scale,
               v_final_g, v_b_ada, v_conv_w))
    dmod_all, loss_tile = fin[4 * N_SMALL_PARAMS], fin[4 * N_SMALL_PARAMS + 1]
    dmod_cols = lax.dynamic_slice(dmod_all.reshape(N_DEV, 6 * D), (0, me * n_ada), (N_DEV, n_ada))
    res["w_ada"] = [t.reshape(w_ada.shape) for t in _ada_bwd_adam(c_all, dmod_cols, w_ada[0], m_w_ada[0], v_w_ada[0])]

    def final_shape(k, t):
        if k == "final_g":
            return t.reshape(D)
        if k == "b_ada":
            return t.reshape(1, 6 * D)
        if k == "conv_w":
            return lax.dynamic_slice(t, (0, me * (D // N_DEV)), (4, D // N_DEV)).reshape(conv_w.shape)
        return t

    for i, k in enumerate(small_names):
        res[k] = [final_shape(k, fin[which * N_SMALL_PARAMS + i]) for which in range(4)]
    order = ["norm_mix_g", "norm_mlp_g", "w_ada", "b_ada", "w_in", "conv_w", "conv_b", "w_rg_a", "b_rg_a", "w_rg_x",
             "b_rg_x", "a_param", "w_branch_a", "w_pool", "b_pool", "pool_scale", "w_branch_b", "w_out", "w_up",
             "w_down", "final_g"]
    outs = [loss_tile[0, 0], grad_x.reshape(x.shape)]
    for which in range(4):
        for k in order:
            outs.append(res[k][which])
    return tuple(outs)
```

```python
import functools

import jax
import jax.numpy as jnp
from jax import lax
from jax.experimental import pallas as pl
from jax.experimental.pallas import tpu as pltpu

F32 = jnp.float32
BF16 = jnp.bfloat16
MESH = pl.DeviceIdType.MESH

N_DEV = 8
D = 1024
N_GROUPS = 4
GW = D // N_GROUPS
D_IN = 5 * D
D_FF = 4 * D
POOL_WINDOWS = (2, 4, 8, 16)
HALO_X = 8
HALO_U = 16
EPS = 1e-6
C_RG = 8.0
ADAM_LR, ADAM_B1, ADAM_B2, ADAM_EPS, ADAM_WD, ADAM_STEP = 0.001, 0.9, 0.999, 1e-08, 0.01, 10

V7X_VMEM_LIMIT = 56 * 1024 * 1024

V_CONV_W, V_CONV_B, V_B_RG_A, V_B_RG_X, V_A_PARAM, V_B_POOL, V_POOL_SCALE, V_G1, V_G2, V_GF = 0, 4, 5, 6, 7, 8, 9, 10, 11, 12
M_SH1, M_SC1, M_GT1, M_SH2, M_SC2, M_GT2 = 0, 1, 2, 3, 4, 5

TM_PROJ = 512
TM_MIX = 256
TM_BRANCH = 256
TM_MLP = 512
TM_MLP_BWD = 256
TS_WGRAD = 1024


def _params(semantics):
    return pltpu.CompilerParams(dimension_semantics=semantics, vmem_limit_bytes=V7X_VMEM_LIMIT)


def _resident(shape):
    return pl.BlockSpec(shape, lambda *_: (0,) * len(shape), pipeline_mode=pl.Buffered(1))


def _dot(a, b):
    return jnp.dot(a, b, preferred_element_type=F32)


def _dot_nt(a, b):
    return lax.dot_general(a, b, (((1,), (1,)), ((), ())), preferred_element_type=F32)


def _dot_tn(a, b):
    return lax.dot_general(a, b, (((0,), (0,)), ((), ())), preferred_element_type=F32)


def _sigmoid(x):
    return 1.0 / (1.0 + jnp.exp(-x))


def _gelu_and_grad(x):
    k = 0.7978845608028654
    x2 = x * x
    t = jnp.tanh(k * (x + 0.044715 * x * x2))
    g = 0.5 * x * (1.0 + t)
    dg = 0.5 * (1.0 + t) + 0.5 * x * (1.0 - t * t) * (k * (1.0 + 3.0 * 0.044715 * x2))
    return g, dg


def _softplus(a):
    e = jnp.exp(-jnp.abs(a))
    u = 1.0 + e
    log1p_e = jnp.where(u == 1.0, e, jnp.log(u) * e / jnp.where(u == 1.0, 1.0, u - 1.0))
    return jnp.maximum(a, 0.0) + log1p_e


def _neg_expm1(z):
    series = -(z * (1.0 + z * (0.5 + z * (1.0 / 6.0 + z * (1.0 / 24.0 + z * (1.0 / 120.0))))))
    return jnp.where(z > -0.1, series, 1.0 - jnp.exp(z))


def _shift_down(x, k):
    return pltpu.roll(x, k, 0)


def _shift_up(x, k):
    return pltpu.roll(x, x.shape[0] - k, 0)


def _rglru_gates(xr, w_a, w_x, b_a, b_x, a_param, is_t0):
    xb = xr.astype(BF16)
    ra = _sigmoid(_dot(xb, w_a) + b_a)
    ri = _sigmoid(_dot(xb, w_x) + b_x)
    sp = _softplus(a_param)
    log_a = (-C_RG) * ra * sp
    a = jnp.exp(log_a)
    mult = jnp.where(is_t0, 1.0, jnp.sqrt(_neg_expm1(2.0 * log_a)))
    return ra, ri, sp, a, mult


def _conv_taps(x_ext):
    return [_shift_down(x_ext, 3 - j)[HALO_X:] if j < 3 else x_ext[HALO_X:] for j in range(4)]


def _proj_fwd(x, modr, vecs, w_in):
    s = x.shape[0]
    tm = min(TM_PROJ, s)

    def body(x_ref, mod_ref, vec_ref, w_ref, proj_ref, h1_ref):
        xv = x_ref[...]
        r = lax.rsqrt(jnp.mean(xv * xv, axis=-1, keepdims=True) + EPS)
        gain = vec_ref[V_G1:V_G1 + 1, :] * (1.0 + mod_ref[M_SC1:M_SC1 + 1, :])
        h = (xv * r * gain + mod_ref[M_SH1:M_SH1 + 1, :]).astype(BF16)
        h1_ref[...] = h
        for c in range(D_IN // D):
            proj_ref[:, c * D:(c + 1) * D] = _dot(h, w_ref[:, c * D:(c + 1) * D])

    return pl.pallas_call(
        body, name="proj_fwd", grid=(s // tm,),
        in_specs=[pl.BlockSpec((tm, D), lambda i: (i, 0)),
                  pl.BlockSpec((8, D), lambda i: (0, 0)),
                  pl.BlockSpec((16, D), lambda i: (0, 0)),
                  _resident((D, D_IN))],
        out_specs=[pl.BlockSpec((tm, D_IN), lambda i: (i, 0)),
                   pl.BlockSpec((tm, D), lambda i: (i, 0))],
        out_shape=[jax.ShapeDtypeStruct((s, D_IN), F32), jax.ShapeDtypeStruct((s, D), BF16)],
        compiler_params=_params(("parallel",)),
    )(x, modr, vecs, w_in)


def _mix_fwd(proj, vecs, w_rg_a, w_rg_x, w_pool):
    s = proj.shape[0]
    tm = min(TM_MIX, s)
    nb = s // tm

    def body(xh_ref, x_ref, y_ref, uh_ref, u_ref, vec_ref, wa_ref, wx_ref, wp_ref,
             xr_ref, hr_ref, za_ref, p_ref, pooled_ref, carry_ref):
        i = pl.program_id(0)
        first = i == 0

        @pl.when(first)
        def _():
            carry_ref[...] = jnp.zeros_like(carry_ref)

        row = lax.broadcasted_iota(jnp.int32, (tm, GW), 0)
        is_t0 = jnp.logical_and(first, row == 0)
        t_glob = (row + i * tm + 1).astype(F32)
        for g in range(N_GROUPS):
            cs = slice(g * GW, (g + 1) * GW)
            vec = vec_ref[:, cs]
            xh = jnp.where(first, 0.0, xh_ref[:, cs])
            taps = _conv_taps(jnp.concatenate([xh, x_ref[:, cs]], axis=0))
            xr = vec[V_CONV_B:V_CONV_B + 1]
            for j in range(4):
                xr = xr + vec[V_CONV_W + j:V_CONV_W + j + 1] * taps[j]
            xr_ref[:, cs] = xr
            _, ri, _, a, mult = _rglru_gates(
                xr, wa_ref[g], wx_ref[g], vec[V_B_RG_A:V_B_RG_A + 1], vec[V_B_RG_X:V_B_RG_X + 1],
                vec[V_A_PARAM:V_A_PARAM + 1], is_t0)
            b = xr * ri * mult
            b = b + jnp.where(row == 0, a * carry_ref[7:8, cs], 0.0)
            k = 1
            while k < tm:
                b = b + a * jnp.where(row >= k, _shift_down(b, k), 0.0)
                if 2 * k < tm:
                    a = a * _shift_down(a, k)
                k *= 2
            hr_ref[:, cs] = b
            carry_ref[:, cs] = b[tm - 8:, :]
            ga, _ = _gelu_and_grad(y_ref[:, cs])
            za_ref[:, cs] = (ga * b).astype(BF16)
            uh = jnp.where(first, 0.0, uh_ref[:, cs])
            sm = jnp.concatenate([uh, u_ref[:, cs]], axis=0)
            k = 1
            while k < POOL_WINDOWS[g]:
                sm = sm + _shift_down(sm, k)
                k *= 2
            cnt = jnp.minimum(t_glob, float(POOL_WINDOWS[g]))
            p = (sm[HALO_U:] / cnt - u_ref[:, cs]).astype(BF16)
            p_ref[:, cs] = p
            pb = _dot(p, wp_ref[g]) + vec[V_B_POOL:V_B_POOL + 1]
            pooled_ref[:, cs] = (pb * vec[V_POOL_SCALE:V_POOL_SCALE + 1]).astype(BF16)

    col = lambda k: (lambda i: (i, k))
    wspec = pl.BlockSpec((N_GROUPS, GW, GW), lambda i: (0, 0, 0))
    return pl.pallas_call(
        body, name="mix_fwd", grid=(nb,),
        in_specs=[pl.BlockSpec((HALO_X, D), lambda i: (jnp.maximum(i * (tm // HALO_X) - 1, 0), 0)),
                  pl.BlockSpec((tm, D), col(0)),
                  pl.BlockSpec((tm, D), col(1)),
                  pl.BlockSpec((HALO_U, D), lambda i: (jnp.maximum(i * (tm // HALO_U) - 1, 0), 2)),
                  pl.BlockSpec((tm, D), col(2)),
                  pl.BlockSpec((16, D), lambda i: (0, 0)),
                  wspec, wspec, wspec],
        out_specs=[pl.BlockSpec((tm, D), lambda i: (i, 0))] * 5,
        out_shape=[jax.ShapeDtypeStruct((s, D), F32), jax.ShapeDtypeStruct((s, D), F32),
                   jax.ShapeDtypeStruct((s, D), BF16), jax.ShapeDtypeStruct((s, D), BF16),
                   jax.ShapeDtypeStruct((s, D), BF16)],
        scratch_shapes=[pltpu.VMEM((8, D), F32)],
        compiler_params=_params(("arbitrary",)),
    )(proj, proj, proj, proj, proj, vecs, w_rg_a, w_rg_x, w_pool)


def _branch_fwd(za, pooled, proj, x, modr, vecs, w_a, w_b, w_out):
    s = x.shape[0]
    tm = min(TM_BRANCH, s)

    def body(za_ref, pooled_ref, ga_ref, gb_ref, x_ref, mod_ref, vec_ref, wa_ref, wb_ref, wo_ref,
             ba_ref, bb_ref, merged_ref, o_ref, x2_ref, h2_ref):
        ba = _dot(za_ref[...], wa_ref[...])
        bb = _dot(pooled_ref[...], wb_ref[...])
        ba_ref[...] = ba.astype(BF16)
        bb_ref[...] = bb.astype(BF16)
        merged = (_sigmoid(ga_ref[...]) * ba + _sigmoid(gb_ref[...]) * bb).astype(BF16)
        merged_ref[...] = merged
        o = _dot(merged, wo_ref[...])
        o_ref[...] = o.astype(BF16)
        x2 = x_ref[...] + mod_ref[M_GT1:M_GT1 + 1, :] * o
        x2_ref[...] = x2
        r = lax.rsqrt(jnp.mean(x2 * x2, axis=-1, keepdims=True) + EPS)
        gain = vec_ref[V_G2:V_G2 + 1, :] * (1.0 + mod_ref[M_SC2:M_SC2 + 1, :])
        h2_ref[...] = (x2 * r * gain + mod_ref[M_SH2:M_SH2 + 1, :]).astype(BF16)

    tok = pl.BlockSpec((tm, D), lambda i: (i, 0))
    wspec = pl.BlockSpec((D, D), lambda i: (0, 0))
    sd = lambda dt: jax.ShapeDtypeStruct((s, D), dt)
    return pl.pallas_call(
        body, name="branch_fwd", grid=(s // tm,),
        in_specs=[tok, tok,
                  pl.BlockSpec((tm, D), lambda i: (i, 3)), pl.BlockSpec((tm, D), lambda i: (i, 4)),
                  tok, pl.BlockSpec((8, D), lambda i: (0, 0)), pl.BlockSpec((16, D), lambda i: (0, 0)),
                  wspec, wspec, wspec],
        out_specs=[tok] * 6,
        out_shape=[sd(BF16), sd(BF16), sd(BF16), sd(BF16), sd(F32), sd(BF16)],
        compiler_params=_params(("parallel",)),
    )(za, pooled, proj, proj, x, modr, vecs, w_a, w_b, w_out)


def _mlp_fwd(h2, x2, target, modr, vecs, w_up, w_down):
    s = x2.shape[0]
    tm = min(TM_MLP, s)

    def body(h2_ref, x2_ref, tgt_ref, mod_ref, vec_ref, wu_ref, wd_ref,
             ru_ref, dx3_ref, ddn_ref, small_ref):
        @pl.when(pl.program_id(0) == 0)
        def _():
            small_ref[...] = jnp.zeros_like(small_ref)

        h2 = h2_ref[...]
        dn = None
        for c in range(D_FF // D):
            cs = slice(c * D, (c + 1) * D)
            ru = jnp.maximum(_dot(h2, wu_ref[:, cs]), 0.0)
            ru_ref[:, cs] = ru.astype(BF16)
            part = _dot((ru * ru).astype(BF16), wd_ref[cs, :])
            dn = part if dn is None else dn + part
        gt2 = mod_ref[M_GT2:M_GT2 + 1, :]
        gf = vec_ref[V_GF:V_GF + 1, :]
        x3 = x2_ref[...] + gt2 * dn
        r3 = lax.rsqrt(jnp.mean(x3 * x3, axis=-1, keepdims=True) + EPS)
        n3 = x3 * r3
        err = n3 * gf - tgt_ref[...]
        dy = err * (1.0 / D)
        dn3 = dy * gf
        dx3 = r3 * (dn3 - n3 * jnp.mean(dn3 * n3, axis=-1, keepdims=True))
        dx3_ref[...] = dx3
        ddn_ref[...] = (dx3 * gt2).astype(BF16)
        small_ref[0:1, :] += jnp.sum(dy * n3, axis=0, keepdims=True)
        small_ref[1:2, :] += jnp.sum(dx3 * dn, axis=0, keepdims=True)
        small_ref[2:3, :] += (0.5 / D) * jnp.sum(err * err, axis=0, keepdims=True)

    tok = pl.BlockSpec((tm, D), lambda i: (i, 0))
    return pl.pallas_call(
        body, name="mlp_fwd", grid=(s // tm,),
        in_specs=[tok, tok, tok,
                  pl.BlockSpec((8, D), lambda i: (0, 0)), pl.BlockSpec((16, D), lambda i: (0, 0)),
                  _resident((D, D_FF)), _resident((D_FF, D))],
        out_specs=[pl.BlockSpec((tm, D_FF), lambda i: (i, 0)), tok, tok,
                   pl.BlockSpec((8, D), lambda i: (0, 0))],
        out_shape=[jax.ShapeDtypeStruct((s, D_FF), BF16), jax.ShapeDtypeStruct((s, D), F32),
                   jax.ShapeDtypeStruct((s, D), BF16), jax.ShapeDtypeStruct((8, D), F32)],
        compiler_params=_params(("arbitrary",)),
    )(h2, x2, target, modr, vecs, w_up, w_down)


def _mlp_bwd(d_dn, ru, x2, dx3, o, modr, vecs, w_up, w_down):
    s = x2.shape[0]
    tm = min(TM_MLP_BWD, s)

    def body(ddn_ref, ru_ref, x2_ref, dx3_ref, o_ref, mod_ref, vec_ref, wu_ref, wd_ref,
             dup_ref, dx2_ref, do_ref, small_ref):
        @pl.when(pl.program_id(0) == 0)
        def _():
            small_ref[...] = jnp.zeros_like(small_ref)

        ddn = ddn_ref[...]
        dh2 = None
        for c in range(D_FF // D):
            cs = slice(c * D, (c + 1) * D)
            dff = _dot_nt(ddn, wd_ref[cs, :])
            dup = (dff * (2.0 * ru_ref[:, cs].astype(F32))).astype(BF16)
            dup_ref[:, cs] = dup
            part = _dot_nt(dup, wu_ref[:, cs])
            dh2 = part if dh2 is None else dh2 + part
        x2 = x2_ref[...]
        r2 = lax.rsqrt(jnp.mean(x2 * x2, axis=-1, keepdims=True) + EPS)
        xn2 = x2 * r2
        gain = vec_ref[V_G2:V_G2 + 1, :] * (1.0 + mod_ref[M_SC2:M_SC2 + 1, :])
        dxn2 = dh2 * gain
        dx2 = dx3_ref[...] + r2 * (dxn2 - xn2 * jnp.mean(dxn2 * xn2, axis=-1, keepdims=True))
        dx2_ref[...] = dx2
        do_ref[...] = (dx2 * mod_ref[M_GT1:M_GT1 + 1, :]).astype(BF16)
        small_ref[0:1, :] += jnp.sum(dh2, axis=0, keepdims=True)
        small_ref[1:2, :] += jnp.sum(dh2 * xn2, axis=0, keepdims=True)
        small_ref[2:3, :] += jnp.sum(dx2 * o_ref[...].astype(F32), axis=0, keepdims=True)

    tok = pl.BlockSpec((tm, D), lambda i: (i, 0))
    wide = pl.BlockSpec((tm, D_FF), lambda i: (i, 0))
    return pl.pallas_call(
        body, name="mlp_bwd", grid=(s // tm,),
        in_specs=[tok, wide, tok, tok, tok,
                  pl.BlockSpec((8, D), lambda i: (0, 0)), pl.BlockSpec((16, D), lambda i: (0, 0)),
                  _resident((D, D_FF)), _resident((D_FF, D))],
        out_specs=[wide, tok, tok, pl.BlockSpec((8, D), lambda i: (0, 0))],
        out_shape=[jax.ShapeDtypeStruct((s, D_FF), BF16), jax.ShapeDtypeStruct((s, D), F32),
                   jax.ShapeDtypeStruct((s, D), BF16), jax.ShapeDtypeStruct((8, D), F32)],
        compiler_params=_params(("arbitrary",)),
    )(d_dn, ru, x2, dx3, o, modr, vecs, w_up, w_down)


def _branch_bwd(do, proj, ba, bb, w_a, w_b, w_out, dep):
    s = do.shape[0]
    tm = min(TM_BRANCH, s)

    def body(do_ref, ga_ref, gb_ref, ba_ref, bb_ref, wa_ref, wb_ref, wo_ref, dep_ref,
             dba_ref, dbb_ref, dg_ref, dza_ref, dpooled_ref):
        dmerged = _dot_nt(do_ref[...], wo_ref[...])
        sa = _sigmoid(ga_ref[...])
        sb = _sigmoid(gb_ref[...])
        dba = (dmerged * sa).astype(BF16)
        dbb = (dmerged * sb).astype(BF16)
        dba_ref[...] = dba
        dbb_ref[...] = dbb
        dg_ref[:, :D] = (dmerged * ba_ref[...].astype(F32) * sa * (1.0 - sa)).astype(BF16)
        dg_ref[:, D:] = (dmerged * bb_ref[...].astype(F32) * sb * (1.0 - sb)).astype(BF16)
        dza_ref[...] = _dot_nt(dba, wa_ref[...])
        dpooled_ref[...] = _dot_nt(dbb, wb_ref[...])

    tok = pl.BlockSpec((tm, D), lambda i: (i, 0))
    wspec = pl.BlockSpec((D, D), lambda i: (0, 0))
    sd = lambda dt: jax.ShapeDtypeStruct((s, D), dt)
    return pl.pallas_call(
        body, name="branch_bwd", grid=(s // tm,),
        in_specs=[tok, pl.BlockSpec((tm, D), lambda i: (i, 3)), pl.BlockSpec((tm, D), lambda i: (i, 4)),
                  tok, tok, wspec, wspec, wspec, pl.BlockSpec(memory_space=pl.ANY)],
        out_specs=[tok, tok, pl.BlockSpec((tm, 2 * D), lambda i: (i, 0)), tok, tok],
        out_shape=[sd(BF16), sd(BF16), jax.ShapeDtypeStruct((s, 2 * D), BF16), sd(F32), sd(F32)],
        compiler_params=_params(("parallel",)),
    )(do, proj, proj, ba, bb, w_a, w_b, w_out, dep)


def _mix_bwd(dza, dpooled, proj, xr, hr, p, dgates, vecs, w_rg_a, w_rg_x, w_pool):
    s = xr.shape[0]
    tm = min(TM_MIX, s)
    nb = s // tm

    def body(dza_ref, dpooled_ref, xh_ref, x_ref, y_ref, xr_ref, hh_ref, hr_ref, p_ref, dg_ref,
             vec_ref, wa_ref, wx_ref, wp_ref,
             dproj_ref, dwa_ref, dwx_ref, dwp_ref, small_ref,
             scan_carry, dxr_carry, q_carry):
        i = pl.program_id(0)
        bi = nb - 1 - i
        first_t = bi == 0

        @pl.when(i == 0)
        def _():
            scan_carry[...] = jnp.zeros_like(scan_carry)
            dxr_carry[...] = jnp.zeros_like(dxr_carry)
            q_carry[...] = jnp.zeros_like(q_carry)
            dwa_ref[...] = jnp.zeros_like(dwa_ref)
            dwx_ref[...] = jnp.zeros_like(dwx_ref)
            dwp_ref[...] = jnp.zeros_like(dwp_ref)
            small_ref[...] = jnp.zeros_like(small_ref)

        row = lax.broadcasted_iota(jnp.int32, (tm, GW), 0)
        is_t0 = jnp.logical_and(first_t, row == 0)
        t_glob = (row + bi * tm + 1).astype(F32)
        colsum = lambda v: jnp.sum(v, axis=0, keepdims=True)
        for g in range(N_GROUPS):
            cs = slice(g * GW, (g + 1) * GW)
            vec = vec_ref[:, cs]
            xr = xr_ref[:, cs]
            hr = hr_ref[:, cs]
            dza = dza_ref[:, cs]
            ga, dga = _gelu_and_grad(y_ref[:, cs])
            dproj_ref[:, D + g * GW:D + (g + 1) * GW] = (dza * hr * dga).astype(BF16)
            dhr = dza * ga
            ra, ri, sp, a, mult = _rglru_gates(
                xr, wa_ref[g], wx_ref[g], vec[V_B_RG_A:V_B_RG_A + 1], vec[V_B_RG_X:V_B_RG_X + 1],
                vec[V_A_PARAM:V_A_PARAM + 1], is_t0)
            m = jnp.where(row == tm - 1, 0.0, _shift_up(a, 1))
            gsum = dhr + jnp.where(row == tm - 1, scan_carry[0:1, cs], 0.0)
            k = 1
            while k < tm:
                gsum = gsum + m * jnp.where(row < tm - k, _shift_up(gsum, k), 0.0)
                if 2 * k < tm:
                    m = m * _shift_up(m, k)
                k *= 2
            scan_carry[:, cs] = (a * gsum)[0:8, :]
            hh = jnp.where(first_t, 0.0, hh_ref[:, cs])
            hprev = _shift_down(jnp.concatenate([hh, hr], axis=0), 1)[8:]
            da = gsum * hprev
            dmult = jnp.where(is_t0, 0.0, gsum * xr * ri)
            dlog_a = da * a - dmult * a * a / mult
            dri = gsum * xr * mult
            dxr = gsum * ri * mult
            small_ref[7:8, cs] += colsum((-C_RG) * ra * dlog_a)
            dpa = (((-C_RG) * sp) * dlog_a * ra * (1.0 - ra))
            dpx = dri * ri * (1.0 - ri)
            small_ref[5:6, cs] += colsum(dpa)
            small_ref[6:7, cs] += colsum(dpx)
            dpa = dpa.astype(BF16)
            dpx = dpx.astype(BF16)
            xrb = xr.astype(BF16)
            dwa_ref[g] += _dot_tn(xrb, dpa)
            dwx_ref[g] += _dot_tn(xrb, dpx)
            dxr = dxr + _dot_nt(dpa, wa_ref[g]) + _dot_nt(dpx, wx_ref[g])
            small_ref[4:5, cs] += colsum(dxr)
            xh = jnp.where(first_t, 0.0, xh_ref[:, cs])
            taps = _conv_taps(jnp.concatenate([xh, x_ref[:, cs]], axis=0))
            dxr_ext = jnp.concatenate([dxr, dxr_carry[:, cs]], axis=0)
            dx = vec[V_CONV_W + 3:V_CONV_W + 4] * dxr
            for j in range(4):
                small_ref[j:j + 1, cs] += colsum(dxr * taps[j])
                if j < 3:
                    dx = dx + vec[V_CONV_W + j:V_CONV_W + j + 1] * _shift_up(dxr_ext, 3 - j)[:tm]
            dxr_carry[:, cs] = dxr[0:8, :]
            dproj_ref[:, cs] = dx.astype(BF16)
            pg = p_ref[:, cs]
            dpooled = dpooled_ref[:, cs]
            pb = _dot(pg, wp_ref[g]) + vec[V_B_POOL:V_B_POOL + 1]
            small_ref[9:10, cs] += colsum(dpooled * pb)
            dpb = dpooled * vec[V_POOL_SCALE:V_POOL_SCALE + 1]
            small_ref[8:9, cs] += colsum(dpb)
            dpbb = dpb.astype(BF16)
            dwp_ref[g] += _dot_tn(pg, dpbb)
            dp = _dot_nt(dpbb, wp_ref[g])
            q = dp / jnp.minimum(t_glob, float(POOL_WINDOWS[g]))
            sm = jnp.concatenate([q, q_carry[:, cs]], axis=0)
            k = 1
            while k < POOL_WINDOWS[g]:
                sm = sm + _shift_up(sm, k)
                k *= 2
            q_carry[:, cs] = q[0:HALO_U, :]
            dproj_ref[:, 2 * D + g * GW:2 * D + (g + 1) * GW] = (sm[:tm] - dp).astype(BF16)
        dproj_ref[:, 3 * D:] = dg_ref[...]

    rev = lambda i: nb - 1 - i
    tok = pl.BlockSpec((tm, D), lambda i: (rev(i), 0))
    col = lambda k: pl.BlockSpec((tm, D), lambda i: (rev(i), k))
    halo8 = lambda k: pl.BlockSpec((8, D), lambda i: (jnp.maximum(rev(i) * (tm // 8) - 1, 0), k))
    wspec = pl.BlockSpec((N_GROUPS, GW, GW), lambda i: (0, 0, 0))
    wshape = jax.ShapeDtypeStruct((N_GROUPS, GW, GW), F32)
    return pl.pallas_call(
        body, name="mix_bwd", grid=(nb,),
        in_specs=[tok, tok, halo8(0), col(0), col(1), tok, halo8(0), tok, tok,
                  pl.BlockSpec((tm, 2 * D), lambda i: (rev(i), 0)),
                  pl.BlockSpec((16, D), lambda i: (0, 0)), wspec, wspec, wspec],
        out_specs=[pl.BlockSpec((tm, D_IN), lambda i: (rev(i), 0)), wspec, wspec, wspec,
                   pl.BlockSpec((16, D), lambda i: (0, 0))],
        out_shape=[jax.ShapeDtypeStruct((s, D_IN), BF16), wshape, wshape, wshape,
                   jax.ShapeDtypeStruct((16, D), F32)],
        scratch_shapes=[pltpu.VMEM((8, D), F32), pltpu.VMEM((8, D), F32), pltpu.VMEM((HALO_U, D), F32)],
        compiler_params=_params(("arbitrary",)),
    )(dza, dpooled, proj, proj, proj, xr, hr, hr, p, dgates, vecs, w_rg_a, w_rg_x, w_pool)


def _proj_bwd(dproj, x, dx2, modr, vecs, w_in):
    s = x.shape[0]
    tm = min(TM_PROJ, s)

    def body(dp_ref, x_ref, dx2_ref, mod_ref, vec_ref, w_ref, gx_ref, small_ref):
        @pl.when(pl.program_id(0) == 0)
        def _():
            small_ref[...] = jnp.zeros_like(small_ref)

        dh1 = None
        for c in range(D_IN // D):
            cs = slice(c * D, (c + 1) * D)
            part = _dot_nt(dp_ref[:, cs], w_ref[:, cs])
            dh1 = part if dh1 is None else dh1 + part
        xv = x_ref[...]
        r1 = lax.rsqrt(jnp.mean(xv * xv, axis=-1, keepdims=True) + EPS)
        xn1 = xv * r1
        gain = vec_ref[V_G1:V_G1 + 1, :] * (1.0 + mod_ref[M_SC1:M_SC1 + 1, :])
        dxn1 = dh1 * gain
        gx_ref[...] = dx2_ref[...] + r1 * (dxn1 - xn1 * jnp.mean(dxn1 * xn1, axis=-1, keepdims=True))
        small_ref[0:1, :] += jnp.sum(dh1, axis=0, keepdims=True)
        small_ref[1:2, :] += jnp.sum(dh1 * xn1, axis=0, keepdims=True)

    tok = pl.BlockSpec((tm, D), lambda i: (i, 0))
    return pl.pallas_call(
        body, name="proj_bwd", grid=(s // tm,),
        in_specs=[pl.BlockSpec((tm, D_IN), lambda i: (i, 0)), tok, tok,
                  pl.BlockSpec((8, D), lambda i: (0, 0)), pl.BlockSpec((16, D), lambda i: (0, 0)),
                  _resident((D, D_IN))],
        out_specs=[tok, pl.BlockSpec((8, D), lambda i: (0, 0))],
        out_shape=[jax.ShapeDtypeStruct((s, D), F32), jax.ShapeDtypeStruct((8, D), F32)],
        compiler_params=_params(("arbitrary",)),
    )(dproj, x, dx2, modr, vecs, w_in)


def _wgrad(a, b, name, square_a=False, dep=None):
    s, ka = a.shape
    n = b.shape[1]
    tka = ka if ka <= 1024 else ka // 2
    tn = n if n <= 1024 else n // 2
    ts = min(TS_WGRAD, s)
    ns = s // ts
    nc = 512
    deps = [] if dep is None else [dep]

    def body(a_ref, b_ref, *refs):
        out_ref, acc_ref = refs[-2:]
        t = pl.program_id(2)

        @pl.when(t == 0)
        def _():
            acc_ref[...] = jnp.zeros_like(acc_ref)

        av = a_ref[...]
        if square_a:
            af = av.astype(F32)
            av = (af * af).astype(BF16)
        for c in range(tn // nc):
            cs = slice(c * nc, (c + 1) * nc)
            acc_ref[:, cs] += _dot_tn(av, b_ref[:, cs])

        @pl.when(t == ns - 1)
        def _():
            out_ref[...] = acc_ref[...].astype(BF16)

    return pl.pallas_call(
        body, name=name, grid=(ka // tka, n // tn, ns),
        in_specs=[pl.BlockSpec((ts, tka), lambda i, j, t: (t, i)),
                  pl.BlockSpec((ts, tn), lambda i, j, t: (t, j))] + [pl.BlockSpec(memory_space=pl.ANY)] * len(deps),
        out_specs=pl.BlockSpec((tka, tn), lambda i, j, t: (i, j)),
        out_shape=jax.ShapeDtypeStruct((ka, n), BF16),
        scratch_shapes=[pltpu.VMEM((tka, tn), F32)],
        compiler_params=_params(("parallel", "parallel", "arbitrary")),
    )(a, b, *deps)


def _window(ref, kind, idx, size):
    start = pl.multiple_of(idx * size, size)
    if kind == 0:
        return ref.at[pl.ds(start, size)]
    if kind == 1:
        return ref.at[:, pl.ds(start, size)]
    return ref.at[:, :, pl.ds(start, size)]


def _mesh_place():
    x, y, c = lax.axis_index("x"), lax.axis_index("y"), lax.axis_index("c")
    return x, y, c, 4 * x + 2 * y + c


def _peer(x, y, c, q):
    px = 1 - x if q & 4 else x
    py = 1 - y if q & 2 else y
    pc = 1 - c if q & 1 else c
    return (px, py, pc), 4 * px + 2 * py + pc


def _all_gather(shards, kinds, name, dep=None):
    n = len(shards)
    deps = [] if dep is None else [dep]
    full_shapes = []
    for sh, kind in zip(shards, kinds):
        dims = list(sh.shape)
        dims[kind] *= N_DEV
        full_shapes.append(jax.ShapeDtypeStruct(tuple(dims), sh.dtype))

    def body(*refs):
        ins, outs = refs[:n], refs[n + len(deps):2 * n + len(deps)]
        send_sems, recv_sems, local_sems = refs[2 * n + len(deps):]
        x, y, c, me = _mesh_place()
        sends, recvs, locals_ = [], [], []
        for k in range(n):
            size = shards[k].shape[kinds[k]]
            mine = _window(outs[k], kinds[k], me, size)
            lc = pltpu.make_async_copy(ins[k], mine, local_sems.at[k])
            lc.start()
            locals_.append(lc)
            for q in range(1, N_DEV):
                peer, peer_idx = _peer(x, y, c, q)
                cp = pltpu.make_async_remote_copy(
                    src_ref=ins[k], dst_ref=mine, send_sem=send_sems.at[k, q], recv_sem=recv_sems.at[k, q],
                    device_id=peer, device_id_type=MESH)
                cp.start()
                sends.append(cp)
                recvs.append(pltpu.make_async_remote_copy(
                    src_ref=ins[k], dst_ref=_window(outs[k], kinds[k], peer_idx, size),
                    send_sem=send_sems.at[k, q], recv_sem=recv_sems.at[k, q],
                    device_id=peer, device_id_type=MESH))
        for cp in recvs:
            cp.wait_recv()
        for cp in sends:
            cp.wait_send()
        for lc in locals_:
            lc.wait()

    any_spec = pl.BlockSpec(memory_space=pl.ANY)
    return pl.pallas_call(
        body, name=name,
        in_specs=[any_spec] * (n + len(deps)), out_specs=[any_spec] * n, out_shape=full_shapes,
        scratch_shapes=[pltpu.SemaphoreType.DMA((n, N_DEV)), pltpu.SemaphoreType.DMA((n, N_DEV)),
                        pltpu.SemaphoreType.DMA((n,))],
    )(*shards, *deps)


_HBM = pl.BlockSpec(memory_space=pltpu.HBM)
_SEM = pl.BlockSpec(memory_space=pltpu.SEMAPHORE)
_EFFECT = pltpu.SideEffectType.DATAFLOW_SIDE_EFFECTING


def _exchange_copies(gather, kinds, src_refs, land_refs, send_sems, recv_sems):
    x, y, c, me = _mesh_place()
    sends, recvs = [], []
    for k in range(len(kinds)):
        for q in range(1, N_DEV):
            peer, peer_idx = _peer(x, y, c, q)
            if gather:
                size = src_refs[k].shape[kinds[k]]
                src = src_refs[k]
                dst = _window(land_refs[k], kinds[k], me, size)
                arriving = _window(land_refs[k], kinds[k], peer_idx, size)
            else:
                size = src_refs[k].shape[kinds[k]] // N_DEV
                src = _window(src_refs[k], kinds[k], peer_idx, size)
                dst = land_refs[k].at[me]
                arriving = land_refs[k].at[peer_idx]
            sems = dict(send_sem=send_sems.at[k * N_DEV + q], recv_sem=recv_sems.at[k * N_DEV + q],
                        device_id=peer, device_id_type=MESH)
            sends.append(pltpu.make_async_remote_copy(src_ref=src, dst_ref=dst, **sems))
            recvs.append(pltpu.make_async_remote_copy(src_ref=src, dst_ref=arriving, **sems))
    return sends, recvs


def _exchange_start(gather, srcs, lands, kinds, after, name):
    n = len(srcs)

    def body(*refs):
        src_refs, land_refs = refs[:n], refs[n:2 * n]
        send_sems, recv_sems = refs[2 * n + 1], refs[2 * n + 2]
        token = refs[-1]
        sends, _ = _exchange_copies(gather, kinds, src_refs, land_refs, send_sems, recv_sems)
        for cp in sends:
            cp.start()
        token[...] = jnp.zeros_like(token)

    hbm = lambda a: pltpu.HBM(a.shape, a.dtype)
    outs = pl.pallas_call(
        body, name=name,
        out_shape=(pltpu.SemaphoreType.DMA((n * N_DEV,)), pltpu.SemaphoreType.DMA((n * N_DEV,)),
                   *[hbm(a) for a in srcs], *[hbm(a) for a in lands], jax.ShapeDtypeStruct((8, 128), F32)),
        in_specs=[_HBM] * (2 * n) + [pl.BlockSpec(memory_space=pl.ANY)],
        out_specs=(_SEM, _SEM, *[_HBM] * (2 * n), pl.BlockSpec(memory_space=pltpu.VMEM)),
        input_output_aliases={i: 2 + i for i in range(2 * n)},
        compiler_params=pltpu.CompilerParams(has_side_effects=_EFFECT),
    )(*[pltpu.with_memory_space_constraint(a, pltpu.HBM) for a in (*srcs, *lands)], after)
    return outs[0], outs[1], outs[2:2 + n], outs[2 + n:2 + 2 * n], outs[-1]


def _exchange_wait(gather, started, kinds, after, name):
    send_sems, recv_sems, srcs, lands, _ = started
    n = len(srcs)

    def body(*refs):
        src_refs, land_refs = refs[:n], refs[n:2 * n]
        sends, recvs = _exchange_copies(gather, kinds, src_refs, land_refs, refs[2 * n], refs[2 * n + 1])
        for cp in sends:
            cp.wait_send()
        for cp in recvs:
            cp.wait_recv()

    hbm = lambda a: pltpu.HBM(a.shape, a.dtype)
    outs = pl.pallas_call(
        body, name=name,
        out_shape=(*[hbm(a) for a in srcs], *[hbm(a) for a in lands]),
        in_specs=[_HBM] * (2 * n) + [_SEM, _SEM, pl.BlockSpec(memory_space=pl.ANY)],
        out_specs=tuple([_HBM] * (2 * n)),
        input_output_aliases={i: i for i in range(2 * n)},
        compiler_params=pltpu.CompilerParams(has_side_effects=_EFFECT),
    )(*srcs, *lands, send_sems, recv_sems, after)
    return outs[n:]


def _after(small, token):
    return small + token[0:1, 0:1].astype(small.dtype)


def _own_window(kind, shard, me):
    dims = list(shard.shape)
    dims[kind] *= N_DEV
    start = [0] * len(dims)
    start[kind] = me * shard.shape[kind]
    return lax.dynamic_update_slice(lax.empty(tuple(dims), shard.dtype), shard, tuple(start))


def _own_slot(kind, full, me):
    size = full.shape[kind] // N_DEV
    mine = lax.dynamic_slice_in_dim(full, me * size, size, axis=kind)
    return lax.dynamic_update_index_in_dim(lax.empty((N_DEV, *mine.shape), full.dtype), mine, me, 0)


def _silu(c):
    return c * _sigmoid(c)


def _ada_fwd(c_all, w_ada, b_ada_cols):
    def body(c_ref, w_ref, b_ref, out_ref):
        out_ref[...] = jnp.dot(_silu(c_ref[...]), w_ref[...], preferred_element_type=F32,
                               precision=lax.Precision.HIGHEST) + b_ref[...]

    return pl.pallas_call(
        body, name="ada_fwd", out_shape=jax.ShapeDtypeStruct((N_DEV, w_ada.shape[1]), F32),
    )(c_all, w_ada, b_ada_cols)


def _adam(w, g, m, v):
    m = ADAM_B1 * m + (1.0 - ADAM_B1) * g
    v = ADAM_B2 * v + (1.0 - ADAM_B2) * (g * g)
    m_hat = m / (1.0 - ADAM_B1 ** ADAM_STEP)
    v_hat = v / (1.0 - ADAM_B2 ** ADAM_STEP)
    delta = -ADAM_LR * (m_hat / (jnp.sqrt(v_hat) + ADAM_EPS) + ADAM_WD * w)
    return delta, m, v


def _ada_bwd_adam(c_all, dmod_cols, w, m, v):
    def body(c_ref, d_ref, w_ref, m_ref, v_ref, g_ref, delta_ref, nm_ref, nv_ref):
        g = lax.dot_general(_silu(c_ref[...]), d_ref[...], (((0,), (0,)), ((), ())),
                            preferred_element_type=F32, precision=lax.Precision.HIGHEST)
        g_ref[...] = g
        delta_ref[...], nm_ref[...], nv_ref[...] = _adam(w_ref[...], g, m_ref[...], v_ref[...])

    sd = jax.ShapeDtypeStruct(w.shape, F32)
    return pl.pallas_call(body, name="ada_bwd_adam", out_shape=[sd] * 4,
                          compiler_params=pltpu.CompilerParams(vmem_limit_bytes=V7X_VMEM_LIMIT),
                          )(c_all, dmod_cols, w, m, v)


def _sum_slots_adam(slots, w, m, v, name):
    r, cdim = w.shape
    tr = min(r, 128)

    def body(s_ref, w_ref, m_ref, v_ref, g_ref, delta_ref, nm_ref, nv_ref):
        g = s_ref[0].astype(F32)
        for p in range(1, N_DEV):
            g = g + s_ref[p].astype(F32)
        g_ref[...] = g
        delta_ref[...], nm_ref[...], nv_ref[...] = _adam(w_ref[...], g, m_ref[...], v_ref[...])

    blk = pl.BlockSpec((tr, cdim), lambda i: (i, 0))
    sd = jax.ShapeDtypeStruct((r, cdim), F32)
    return pl.pallas_call(
        body, name=name, grid=(r // tr,),
        in_specs=[pl.BlockSpec((N_DEV, tr, cdim), lambda i: (0, i, 0)), blk, blk, blk],
        out_specs=[blk] * 4, out_shape=[sd] * 4,
        compiler_params=_params(("parallel",)),
    )(slots, w, m, v)


N_SMALL = 40
N_SMALL_PARAMS = 11


def _pack_vecs(conv_w_full, rows):
    def body(cw_ref, *refs):
        out = refs[-1]
        out[...] = jnp.zeros_like(out)
        out[0:4, :] = cw_ref[0:4, :]
        for r, ref in enumerate(refs[:-1]):
            out[4 + r:5 + r, :] = ref[...]

    return pl.pallas_call(body, name="pack_vecs", out_shape=jax.ShapeDtypeStruct((16, D), F32))(conv_w_full, *rows)


def _small_finish(gathered, mod_all, vecs, ws, ms, vs):
    n = N_SMALL_PARAMS

    def body(g_ref, mod_ref, vec_ref, *refs):
        w_refs, m_refs, v_refs = refs[:n], refs[n:2 * n], refs[2 * n:3 * n]
        outs = refs[3 * n:]
        g1 = vec_ref[V_G1:V_G1 + 1, :]
        g2 = vec_ref[V_G2:V_G2 + 1, :]
        zero = jnp.zeros((1, D), F32)
        dg1, dg2, dgf, loss_lanes = zero, zero, zero, zero
        mixer = jnp.zeros((16, D), F32)
        db_ada = jnp.zeros((6, D), F32)
        for b in range(N_DEV):
            gb = g_ref[b]
            mod = mod_ref[b]
            q1 = gb[33:34]
            q2 = gb[9:10]
            dmod = jnp.concatenate([gb[32:33], q1 * g1, gb[10:11], gb[8:9], q2 * g2, gb[1:2]], axis=0)
            outs[4 * n][b] = dmod
            db_ada = db_ada + dmod
            dg1 = dg1 + q1 * (1.0 + mod[M_SC1:M_SC1 + 1])
            dg2 = dg2 + q2 * (1.0 + mod[M_SC2:M_SC2 + 1])
            dgf = dgf + gb[0:1]
            loss_lanes = loss_lanes + gb[2:3]
            mixer = mixer + gb[16:32]
        d_a_param = mixer[7:8] * _sigmoid(vec_ref[V_A_PARAM:V_A_PARAM + 1, :])
        grads = [dg1, dg2, mixer[4:5], mixer[5:6], mixer[6:7], d_a_param, mixer[8:9], mixer[9:10], dgf,
                 db_ada, mixer[0:4]]
        for k in range(n):
            outs[k][...] = grads[k]
            outs[n + k][...], outs[2 * n + k][...], outs[3 * n + k][...] = _adam(
                w_refs[k][...], grads[k], m_refs[k][...], v_refs[k][...])
        outs[4 * n + 1][...] = jnp.broadcast_to(jnp.sum(loss_lanes, axis=1, keepdims=True), (8, 128))

    shapes = [jax.ShapeDtypeStruct(w.shape, F32) for w in ws]
    return pl.pallas_call(
        body, name="small_finish",
        out_shape=shapes * 4 + [jax.ShapeDtypeStruct((N_DEV, 6, D), F32), jax.ShapeDtypeStruct((8, 128), F32)],
    )(gathered, mod_all, vecs, *ws, *ms, *vs)


def _pad_rows(a, rows):
    return jnp.pad(a, ((0, rows - a.shape[0]), (0, 0)))


def kernel(x, c, norm_mix_g, norm_mlp_g, w_ada, b_ada, w_in, conv_w, conv_b, w_rg_a, b_rg_a, w_rg_x, b_rg_x, a_param, w_branch_a, w_pool, b_pool, pool_scale, w_branch_b, w_out, w_up, w_down, final_g, loss_target, m_norm_mix_g, m_norm_mlp_g, m_w_ada, m_b_ada, m_w_in, m_conv_w, m_conv_b, m_w_rg_a, m_b_rg_a, m_w_rg_x, m_b_rg_x, m_a_param, m_w_branch_a, m_w_pool, m_b_pool, m_pool_scale, m_w_branch_b, m_w_out, m_w_up, m_w_down, m_final_g, v_norm_mix_g, v_norm_mlp_g, v_w_ada, v_b_ada, v_w_in, v_conv_w, v_conv_b, v_w_rg_a, v_b_rg_a, v_w_rg_x, v_b_rg_x, v_a_param, v_w_branch_a, v_w_pool, v_b_pool, v_pool_scale, v_w_branch_b, v_w_out, v_w_up, v_w_down, v_final_g):
    me = 4 * lax.axis_index("x") + 2 * lax.axis_index("y") + lax.axis_index("c")
    s = x.shape[1]
    x2d = x.reshape(s, D)
    target = loss_target.reshape(s, D)
    n_ada = w_ada.shape[2]

    sharded = dict(w_in=(w_in[0], 1), w_up=(w_up[0], 1), w_down=(w_down[0], 0), w_branch_a=(w_branch_a[0], 0),
                   w_branch_b=(w_branch_b[0], 0), w_out=(w_out[0], 0), w_rg_a=(w_rg_a[0], 1), w_rg_x=(w_rg_x[0], 1),
                   w_pool=(w_pool[0], 1))
    kind = {k: v[1] for k, v in sharded.items()}
    shard = {k: v[0].astype(BF16) for k, v in sharded.items()}

    w_in_full, conv_w_full, c_rows = _all_gather([shard["w_in"], _pad_rows(conv_w[0], 8), _pad_rows(c, 8)],
                                                 [1, 1, 0], "gather_first")
    c_all = c_rows.reshape(N_DEV, 8, D)[:, 0, :]

    b_ada_cols = lax.dynamic_slice(b_ada, (0, me * n_ada), (1, n_ada))
    mod_part = _ada_fwd(c_all, w_ada[0], b_ada_cols)
    mod_parts, = _all_gather([mod_part], [0], "gather_mod")
    mod_all = jnp.transpose(mod_parts.reshape(N_DEV, N_DEV, n_ada), (1, 0, 2)).reshape(N_DEV, 6, D)
    mod_all = jnp.pad(mod_all, ((0, 0), (0, 2), (0, 0)))
    modr = lax.dynamic_index_in_dim(mod_all, me, 0, keepdims=False)
    vecs = _pack_vecs(conv_w_full, [conv_b, b_rg_a, b_rg_x, a_param, b_pool, pool_scale,
                                    norm_mix_g, norm_mlp_g, final_g.reshape(1, D)])

    mixer_names = ["w_rg_a", "w_rg_x", "w_pool", "w_branch_a", "w_branch_b", "w_out"]
    mlp_names = ["w_up", "w_down"]

    def start_gather(group, after, name):
        return _exchange_start(True, [shard[k] for k in group], [_own_window(kind[k], shard[k], me) for k in group],
                               [kind[k] for k in group], after, name)

    g_mixer = start_gather(mixer_names, modr, "gather_mixer_start")
    g_mlp = start_gather(mlp_names, g_mixer[-1], "gather_mlp_start")

    proj, h1 = _proj_fwd(x2d, _after(modr, g_mlp[-1]), vecs, w_in_full)
    wg = dict(zip(mixer_names, _exchange_wait(True, g_mixer, [kind[k] for k in mixer_names], h1, "gather_mixer_wait")))
    xr, hr, za, p, pooled = _mix_fwd(proj, vecs, wg["w_rg_a"], wg["w_rg_x"], wg["w_pool"])
    ba, bb, merged, o, x2, h2 = _branch_fwd(za, pooled, proj, x2d, modr, vecs,
                                            wg["w_branch_a"], wg["w_branch_b"], wg["w_out"])
    wg.update(zip(mlp_names, _exchange_wait(True, g_mlp, [kind[k] for k in mlp_names], h2, "gather_mlp_wait")))
    ru, dx3, d_dn, small_f = _mlp_fwd(h2, x2, target, modr, vecs, wg["w_up"], wg["w_down"])

    def start_scatter(group, partial, after, name):
        return _exchange_start(False, [partial[k] for k in group], [_own_slot(kind[k], partial[k], me) for k in group],
                               [kind[k] for k in group], after, name)

    dup, dx2, do, small_m = _mlp_bwd(d_dn, ru, x2, dx3, o, modr, vecs, wg["w_up"], wg["w_down"])
    partial = dict(w_up=_wgrad(h2, dup, "wgrad_up"), w_down=_wgrad(ru, d_dn, "wgrad_down", square_a=True))
    s_mlp = start_scatter(mlp_names, partial, dx2, "scatter_mlp_start")

    dba, dbb, dgates, dza, dpooled = _branch_bwd(do, proj, ba, bb, wg["w_branch_a"], wg["w_branch_b"], wg["w_out"],
                                                 dep=s_mlp[-1])
    dproj, dw_rg_a, dw_rg_x, dw_pool, small_x = _mix_bwd(dza, dpooled, proj, xr, hr, p, dgates, vecs,
                                                         wg["w_rg_a"], wg["w_rg_x"], wg["w_pool"])
    partial.update(w_branch_a=_wgrad(za, dba, "wgrad_branch_a"), w_branch_b=_wgrad(pooled, dbb, "wgrad_branch_b"),
                   w_out=_wgrad(merged, do, "wgrad_out"),
                   w_rg_a=dw_rg_a.astype(BF16), w_rg_x=dw_rg_x.astype(BF16), w_pool=dw_pool.astype(BF16))
    s_mixer = start_scatter(mixer_names, partial, s_mlp[-1], "scatter_mixer_start")

    partial["w_in"] = _wgrad(h1, dproj, "wgrad_in", dep=s_mixer[-1])
    s_in = start_scatter(["w_in"], partial, s_mixer[-1], "scatter_in_start")
    grad_x, small_p = _proj_bwd(dproj, x2d, dx2, _after(modr, s_in[-1]), vecs, w_in_full)

    locals_ = dict(w_in=(w_in, m_w_in, v_w_in), w_up=(w_up, m_w_up, v_w_up), w_down=(w_down, m_w_down, v_w_down),
                   w_branch_a=(w_branch_a, m_w_branch_a, v_w_branch_a),
                   w_branch_b=(w_branch_b, m_w_branch_b, v_w_branch_b), w_out=(w_out, m_w_out, v_w_out),
                   w_rg_a=(w_rg_a, m_w_rg_a, v_w_rg_a), w_rg_x=(w_rg_x, m_w_rg_x, v_w_rg_x),
                   w_pool=(w_pool, m_w_pool, v_w_pool))
    res = {}

    def finish(group, started, after, name):
        slots = _exchange_wait(False, started, [kind[k] for k in group], after, name)
        for k, sl in zip(group, slots):
            w, m, v = locals_[k]
            shape2d = (-1, w.shape[-1])
            sl = sl.reshape(N_DEV, *w.reshape(shape2d).shape)
            outs = _sum_slots_adam(sl, w.reshape(shape2d), m.reshape(shape2d), v.reshape(shape2d), "adam_" + k)
            res[k] = [t.reshape(w.shape) for t in outs]
        return res[group[-1]][0]

    done = finish(mlp_names, s_mlp, grad_x, "scatter_mlp_wait")
    done = finish(mixer_names, s_mixer, done, "scatter_mixer_wait")
    done = finish(["w_in"], s_in, done, "scatter_in_wait")

    small = jnp.concatenate([small_f, small_m, small_x, small_p], axis=0)
    small_all, = _all_gather([small], [0], "gather_small", dep=done)
    small_all = small_all.reshape(N_DEV, N_SMALL, D)

    def embed(cw):
        return lax.dynamic_update_slice(jnp.zeros((4, D), F32), cw[0], (0, me * (D // N_DEV)))

    def smalls(ng, nl, cb, bra, brx, ap, bp, ps, fg, ba_, cw):
        return [ng, nl, cb, bra, brx, ap, bp, ps, fg.reshape(1, D), ba_.reshape(6, D), embed(cw)]

    small_names = ["norm_mix_g", "norm_mlp_g", "conv_b", "b_rg_a", "b_rg_x", "a_param", "b_pool", "pool_scale",
                   "final_g", "b_ada", "conv_w"]
    fin = _small_finish(
        small_all, mod_all, vecs,
        smalls(norm_mix_g, norm_mlp_g, conv_b, b_rg_a, b_rg_x, a_param, b_pool, pool_scale, final_g, b_ada, conv_w),
        smalls(m_norm_mix_g, m_norm_mlp_g, m_conv_b, m_b_rg_a, m_b_rg_x, m_a_param, m_b_pool, m_pool_scale,
               m_final_g, m_b_ada, m_conv_w),
        smalls(v_norm_mix_g, v_norm_mlp_g, v_conv_b, v_b_rg_a, v_b_rg_x, v_a_param, v_b_pool, v_pool_scale,
               v_final_g, v_b_ada, v_conv_w))
    dmod_all, loss_tile = fin[4 * N_SMALL_PARAMS], fin[4 * N_SMALL_PARAMS + 1]
    dmod_cols = lax.dynamic_slice(dmod_all.reshape(N_DEV, 6 * D), (0, me * n_ada), (N_DEV, n_ada))
    res["w_ada"] = [t.reshape(w_ada.shape) for t in _ada_bwd_adam(c_all, dmod_cols, w_ada[0], m_w_ada[0], v_w_ada[0])]

    def final_shape(k, t):
        if k == "final_g":
            return t.reshape(D)
        if k == "b_ada":
            return t.reshape(1, 6 * D)
        if k == "conv_w":
            return lax.dynamic_slice(t, (0, me * (D // N_DEV)), (4, D // N_DEV)).reshape(conv_w.shape)
        return t

    for i, k in enumerate(small_names):
        res[k] = [final_shape(k, fin[which * N_SMALL_PARAMS + i]) for which in range(4)]
    order = ["norm_mix_g", "norm_mlp_g", "w_ada", "b_ada", "w_in", "conv_w", "conv_b", "w_rg_a", "b_rg_a", "w_rg_x",
             "b_rg_x", "a_param", "w_branch_a", "w_pool", "b_pool", "pool_scale", "w_branch_b", "w_out", "w_up",
             "w_down", "final_g"]
    outs = [loss_tile[0, 0], grad_x.reshape(x.shape)]
    for which in range(4):
        for k in order:
            outs.append(res[k][which])
    return tuple(outs)
```

```python
import functools

import jax
import jax.numpy as jnp
from jax import lax
from jax.experimental import pallas as pl
from jax.experimental.pallas import tpu as pltpu

F32 = jnp.float32
BF16 = jnp.bfloat16
MESH = pl.DeviceIdType.MESH

N_DEV = 8
D = 1024
N_GROUPS = 4
GW = D // N_GROUPS
D_IN = 5 * D
D_FF = 4 * D
POOL_WINDOWS = (2, 4, 8, 16)
HALO_X = 8
HALO_U = 16
EPS = 1e-6
C_RG = 8.0
ADAM_LR, ADAM_B1, ADAM_B2, ADAM_EPS, ADAM_WD, ADAM_STEP = 0.001, 0.9, 0.999, 1e-08, 0.01, 10

V7X_VMEM_LIMIT = 56 * 1024 * 1024

V_CONV_W, V_CONV_B, V_B_RG_A, V_B_RG_X, V_A_PARAM, V_B_POOL, V_POOL_SCALE, V_G1, V_G2, V_GF = 0, 4, 5, 6, 7, 8, 9, 10, 11, 12
M_SH1, M_SC1, M_GT1, M_SH2, M_SC2, M_GT2 = 0, 1, 2, 3, 4, 5

TM_PROJ = 512
TM_MIX = 256
TM_BRANCH = 256
TM_MLP = 512
TM_MLP_BWD = 256
TS_WGRAD = 1024


def _params(semantics):
    return pltpu.CompilerParams(dimension_semantics=semantics, vmem_limit_bytes=V7X_VMEM_LIMIT)


def _resident(shape):
    return pl.BlockSpec(shape, lambda *_: (0,) * len(shape), pipeline_mode=pl.Buffered(1))


def _dot(a, b):
    return jnp.dot(a, b, preferred_element_type=F32)


def _dot_nt(a, b):
    return lax.dot_general(a, b, (((1,), (1,)), ((), ())), preferred_element_type=F32)


def _dot_tn(a, b):
    return lax.dot_general(a, b, (((0,), (0,)), ((), ())), preferred_element_type=F32)


def _sigmoid(x):
    return 0.5 * jnp.tanh(0.5 * x) + 0.5


def _sigmoid_tail(x):
    return 1.0 / (1.0 + jnp.exp(-x))


def _gelu_and_grad(x):
    k = 0.7978845608028654
    x2 = x * x
    t = jnp.tanh(k * (x + 0.044715 * x * x2))
    g = 0.5 * x * (1.0 + t)
    dg = 0.5 * (1.0 + t) + 0.5 * x * (1.0 - t * t) * (k * (1.0 + 3.0 * 0.044715 * x2))
    return g, dg


def _softplus(a):
    e = jnp.exp(-jnp.abs(a))
    u = 1.0 + e
    log1p_e = jnp.where(u == 1.0, e, jnp.log(u) * e / jnp.where(u == 1.0, 1.0, u - 1.0))
    return jnp.maximum(a, 0.0) + log1p_e


def _neg_expm1(z):
    series = -(z * (1.0 + z * (0.5 + z * (1.0 / 6.0 + z * (1.0 / 24.0 + z * (1.0 / 120.0))))))
    return jnp.where(z > -0.1, series, 1.0 - jnp.exp(z))


def _shift_down(x, k):
    return pltpu.roll(x, k, 0)


def _shift_up(x, k):
    return pltpu.roll(x, x.shape[0] - k, 0)


def _rglru_gates(xr, w_a, w_x, b_a, b_x, a_param, is_t0):
    xb = xr.astype(BF16)
    ra = _sigmoid(_dot(xb, w_a) + b_a)
    ri = _sigmoid(_dot(xb, w_x) + b_x)
    sp = _softplus(a_param)
    log_a = (-C_RG) * ra * sp
    a = jnp.exp(log_a)
    mult = jnp.where(is_t0, 1.0, jnp.sqrt(_neg_expm1(2.0 * log_a)))
    return ra, ri, sp, a, mult


SUBLANES = 8


def _scan_down(a, b, carry):
    t, w = b.shape
    g = t // SUBLANES
    a3 = a.reshape(g, SUBLANES, w)
    b3 = b.reshape(g, SUBLANES, w)
    sub = lax.broadcasted_iota(jnp.int32, (g, SUBLANES, w), 1)
    for k in (1, 2, 4):
        keep = sub >= k
        b3 = b3 + a3 * jnp.where(keep, pltpu.roll(b3, k, 1), 0.0)
        a3 = a3 * jnp.where(keep, pltpu.roll(a3, k, 1), 1.0)
    ag = a3[:, SUBLANES - 1, :]
    bg = b3[:, SUBLANES - 1, :]
    rg = lax.broadcasted_iota(jnp.int32, (g, w), 0)
    bg = bg + jnp.where(rg == 0, ag * carry, 0.0)
    k = 1
    while k < g:
        bg = bg + ag * jnp.where(rg >= k, pltpu.roll(bg, k, 0), 0.0)
        if 2 * k < g:
            ag = ag * pltpu.roll(ag, k, 0)
        k *= 2
    entering = jnp.where(rg >= 1, pltpu.roll(bg, 1, 0), carry)
    h = (b3 + a3 * entering[:, None, :]).reshape(t, w)
    return h, bg[g - 1:g, :]


def _scan_up(m, b, carry):
    t, w = b.shape
    g = t // SUBLANES
    m3 = m.reshape(g, SUBLANES, w)
    b3 = b.reshape(g, SUBLANES, w)
    sub = lax.broadcasted_iota(jnp.int32, (g, SUBLANES, w), 1)
    for k in (1, 2, 4):
        keep = sub < SUBLANES - k
        b3 = b3 + m3 * jnp.where(keep, pltpu.roll(b3, SUBLANES - k, 1), 0.0)
        m3 = m3 * jnp.where(keep, pltpu.roll(m3, SUBLANES - k, 1), 1.0)
    mg = m3[:, 0, :]
    bg = b3[:, 0, :]
    rg = lax.broadcasted_iota(jnp.int32, (g, w), 0)
    bg = bg + jnp.where(rg == g - 1, mg * carry, 0.0)
    k = 1
    while k < g:
        bg = bg + mg * jnp.where(rg < g - k, pltpu.roll(bg, g - k, 0), 0.0)
        if 2 * k < g:
            mg = mg * pltpu.roll(mg, g - k, 0)
        k *= 2
    entering = jnp.where(rg < g - 1, pltpu.roll(bg, g - 1, 0), carry)
    return (b3 + m3 * entering[:, None, :]).reshape(t, w)


def _conv_taps(x_ext):
    return [_shift_down(x_ext, 3 - j)[HALO_X:] if j < 3 else x_ext[HALO_X:] for j in range(4)]


def _proj_fwd(x, modr, vecs, w_in):
    s = x.shape[0]
    tm = min(TM_PROJ, s)

    def body(x_ref, mod_ref, vec_ref, w_ref, proj_ref, h1_ref):
        xv = x_ref[...]
        r = lax.rsqrt(jnp.mean(xv * xv, axis=-1, keepdims=True) + EPS)
        gain = vec_ref[V_G1:V_G1 + 1, :] * (1.0 + mod_ref[M_SC1:M_SC1 + 1, :])
        h = (xv * r * gain + mod_ref[M_SH1:M_SH1 + 1, :]).astype(BF16)
        h1_ref[...] = h
        for c in range(D_IN // D):
            proj_ref[:, c * D:(c + 1) * D] = _dot(h, w_ref[:, c * D:(c + 1) * D])

    return pl.pallas_call(
        body, name="proj_fwd", grid=(s // tm,),
        in_specs=[pl.BlockSpec((tm, D), lambda i: (i, 0)),
                  pl.BlockSpec((8, D), lambda i: (0, 0)),
                  pl.BlockSpec((16, D), lambda i: (0, 0)),
                  _resident((D, D_IN))],
        out_specs=[pl.BlockSpec((tm, D_IN), lambda i: (i, 0)),
                   pl.BlockSpec((tm, D), lambda i: (i, 0))],
        out_shape=[jax.ShapeDtypeStruct((s, D_IN), F32), jax.ShapeDtypeStruct((s, D), BF16)],
        compiler_params=_params(("parallel",)),
    )(x, modr, vecs, w_in)


def _mix_fwd(proj, vecs, w_rg_a, w_rg_x, w_pool):
    s = proj.shape[0]
    tm = min(TM_MIX, s)
    nb = s // tm

    def body(xh_ref, x_ref, y_ref, uh_ref, u_ref, vec_ref, wa_ref, wx_ref, wp_ref,
             xr_ref, hr_ref, za_ref, p_ref, pooled_ref, a_ref, mult_ref, ra_ref, ri_ref, carry_ref):
        i = pl.program_id(0)
        first = i == 0

        @pl.when(first)
        def _():
            carry_ref[...] = jnp.zeros_like(carry_ref)

        row = lax.broadcasted_iota(jnp.int32, (tm, GW), 0)
        is_t0 = jnp.logical_and(first, row == 0)
        t_glob = (row + i * tm + 1).astype(F32)
        for g in range(N_GROUPS):
            cs = slice(g * GW, (g + 1) * GW)
            vec = vec_ref[:, cs]
            xh = jnp.where(first, 0.0, xh_ref[:, cs])
            taps = _conv_taps(jnp.concatenate([xh, x_ref[:, cs]], axis=0))
            xr = vec[V_CONV_B:V_CONV_B + 1]
            for j in range(4):
                xr = xr + vec[V_CONV_W + j:V_CONV_W + j + 1] * taps[j]
            xr_ref[:, cs] = xr
            ra, ri, _, a, mult = _rglru_gates(
                xr, wa_ref[g], wx_ref[g], vec[V_B_RG_A:V_B_RG_A + 1], vec[V_B_RG_X:V_B_RG_X + 1],
                vec[V_A_PARAM:V_A_PARAM + 1], is_t0)
            a_ref[:, cs] = a
            mult_ref[:, cs] = mult
            ra_ref[:, cs] = ra.astype(BF16)
            ri_ref[:, cs] = ri.astype(BF16)
            h, last = _scan_down(a, xr * ri * mult, carry_ref[0:1, cs])
            hr_ref[:, cs] = h
            carry_ref[0:1, cs] = last
            ga, _ = _gelu_and_grad(y_ref[:, cs])
            za_ref[:, cs] = (ga * h).astype(BF16)
            uh = jnp.where(first, 0.0, uh_ref[:, cs])
            sm = jnp.concatenate([uh, u_ref[:, cs]], axis=0)
            k = 1
            while k < POOL_WINDOWS[g]:
                sm = sm + _shift_down(sm, k)
                k *= 2
            cnt = jnp.minimum(t_glob, float(POOL_WINDOWS[g]))
            p = (sm[HALO_U:] / cnt - u_ref[:, cs]).astype(BF16)
            p_ref[:, cs] = p
            pb = _dot(p, wp_ref[g]) + vec[V_B_POOL:V_B_POOL + 1]
            pooled_ref[:, cs] = (pb * vec[V_POOL_SCALE:V_POOL_SCALE + 1]).astype(BF16)

    col = lambda k: (lambda i: (i, k))
    wspec = pl.BlockSpec((N_GROUPS, GW, GW), lambda i: (0, 0, 0))
    return pl.pallas_call(
        body, name="mix_fwd", grid=(nb,),
        in_specs=[pl.BlockSpec((HALO_X, D), lambda i: (jnp.maximum(i * (tm // HALO_X) - 1, 0), 0)),
                  pl.BlockSpec((tm, D), col(0)),
                  pl.BlockSpec((tm, D), col(1)),
                  pl.BlockSpec((HALO_U, D), lambda i: (jnp.maximum(i * (tm // HALO_U) - 1, 0), 2)),
                  pl.BlockSpec((tm, D), col(2)),
                  pl.BlockSpec((16, D), lambda i: (0, 0)),
                  wspec, wspec, wspec],
        out_specs=[pl.BlockSpec((tm, D), lambda i: (i, 0))] * 9,
        out_shape=[jax.ShapeDtypeStruct((s, D), F32), jax.ShapeDtypeStruct((s, D), F32),
                   jax.ShapeDtypeStruct((s, D), BF16), jax.ShapeDtypeStruct((s, D), BF16),
                   jax.ShapeDtypeStruct((s, D), BF16),
                   jax.ShapeDtypeStruct((s, D), F32), jax.ShapeDtypeStruct((s, D), F32),
                   jax.ShapeDtypeStruct((s, D), BF16), jax.ShapeDtypeStruct((s, D), BF16)],
        scratch_shapes=[pltpu.VMEM((8, D), F32)],
        compiler_params=_params(("arbitrary",)),
    )(proj, proj, proj, proj, proj, vecs, w_rg_a, w_rg_x, w_pool)


def _branch_fwd(za, pooled, proj, x, modr, vecs, w_a, w_b, w_out):
    s = x.shape[0]
    tm = min(TM_BRANCH, s)

    def body(za_ref, pooled_ref, ga_ref, gb_ref, x_ref, mod_ref, vec_ref, wa_ref, wb_ref, wo_ref,
             ba_ref, bb_ref, merged_ref, o_ref, x2_ref, h2_ref):
        ba = _dot(za_ref[...], wa_ref[...])
        bb = _dot(pooled_ref[...], wb_ref[...])
        ba_ref[...] = ba.astype(BF16)
        bb_ref[...] = bb.astype(BF16)
        merged = (_sigmoid(ga_ref[...]) * ba + _sigmoid(gb_ref[...]) * bb).astype(BF16)
        merged_ref[...] = merged
        o = _dot(merged, wo_ref[...])
        o_ref[...] = o.astype(BF16)
        x2 = x_ref[...] + mod_ref[M_GT1:M_GT1 + 1, :] * o
        x2_ref[...] = x2
        r = lax.rsqrt(jnp.mean(x2 * x2, axis=-1, keepdims=True) + EPS)
        gain = vec_ref[V_G2:V_G2 + 1, :] * (1.0 + mod_ref[M_SC2:M_SC2 + 1, :])
        h2_ref[...] = (x2 * r * gain + mod_ref[M_SH2:M_SH2 + 1, :]).astype(BF16)

    tok = pl.BlockSpec((tm, D), lambda i: (i, 0))
    wspec = pl.BlockSpec((D, D), lambda i: (0, 0))
    sd = lambda dt: jax.ShapeDtypeStruct((s, D), dt)
    return pl.pallas_call(
        body, name="branch_fwd", grid=(s // tm,),
        in_specs=[tok, tok,
                  pl.BlockSpec((tm, D), lambda i: (i, 3)), pl.BlockSpec((tm, D), lambda i: (i, 4)),
                  tok, pl.BlockSpec((8, D), lambda i: (0, 0)), pl.BlockSpec((16, D), lambda i: (0, 0)),
                  wspec, wspec, wspec],
        out_specs=[tok] * 6,
        out_shape=[sd(BF16), sd(BF16), sd(BF16), sd(BF16), sd(F32), sd(BF16)],
        compiler_params=_params(("parallel",)),
    )(za, pooled, proj, proj, x, modr, vecs, w_a, w_b, w_out)


def _mlp_fwd(h2, x2, target, modr, vecs, w_up, w_down):
    s = x2.shape[0]
    tm = min(TM_MLP, s)

    def body(h2_ref, x2_ref, tgt_ref, mod_ref, vec_ref, wu_ref, wd_ref,
             ru_ref, dx3_ref, ddn_ref, small_ref):
        @pl.when(pl.program_id(0) == 0)
        def _():
            small_ref[...] = jnp.zeros_like(small_ref)

        h2 = h2_ref[...]
        dn = None
        for c in range(D_FF // D):
            cs = slice(c * D, (c + 1) * D)
            ru = jnp.maximum(_dot(h2, wu_ref[:, cs]), 0.0)
            ru_ref[:, cs] = ru.astype(BF16)
            part = _dot((ru * ru).astype(BF16), wd_ref[cs, :])
            dn = part if dn is None else dn + part
        gt2 = mod_ref[M_GT2:M_GT2 + 1, :]
        gf = vec_ref[V_GF:V_GF + 1, :]
        x3 = x2_ref[...] + gt2 * dn
        r3 = lax.rsqrt(jnp.mean(x3 * x3, axis=-1, keepdims=True) + EPS)
        n3 = x3 * r3
        err = n3 * gf - tgt_ref[...]
        dy = err * (1.0 / D)
        dn3 = dy * gf
        dx3 = r3 * (dn3 - n3 * jnp.mean(dn3 * n3, axis=-1, keepdims=True))
        dx3_ref[...] = dx3
        ddn_ref[...] = (dx3 * gt2).astype(BF16)
        small_ref[0:1, :] += jnp.sum(dy * n3, axis=0, keepdims=True)
        small_ref[1:2, :] += jnp.sum(dx3 * dn, axis=0, keepdims=True)
        small_ref[2:3, :] += (0.5 / D) * jnp.sum(err * err, axis=0, keepdims=True)

    tok = pl.BlockSpec((tm, D), lambda i: (i, 0))
    return pl.pallas_call(
        body, name="mlp_fwd", grid=(s // tm,),
        in_specs=[tok, tok, tok,
                  pl.BlockSpec((8, D), lambda i: (0, 0)), pl.BlockSpec((16, D), lambda i: (0, 0)),
                  _resident((D, D_FF)), _resident((D_FF, D))],
        out_specs=[pl.BlockSpec((tm, D_FF), lambda i: (i, 0)), tok, tok,
                   pl.BlockSpec((8, D), lambda i: (0, 0))],
        out_shape=[jax.ShapeDtypeStruct((s, D_FF), BF16), jax.ShapeDtypeStruct((s, D), F32),
                   jax.ShapeDtypeStruct((s, D), BF16), jax.ShapeDtypeStruct((8, D), F32)],
        compiler_params=_params(("arbitrary",)),
    )(h2, x2, target, modr, vecs, w_up, w_down)


def _mlp_bwd(d_dn, ru, x2, dx3, o, modr, vecs, w_up, w_down):
    s = x2.shape[0]
    tm = min(TM_MLP_BWD, s)

    def body(ddn_ref, ru_ref, x2_ref, dx3_ref, o_ref, mod_ref, vec_ref, wu_ref, wd_ref,
             dup_ref, dx2_ref, do_ref, small_ref):
        @pl.when(pl.program_id(0) == 0)
        def _():
            small_ref[...] = jnp.zeros_like(small_ref)

        ddn = ddn_ref[...]
        dh2 = None
        for c in range(D_FF // D):
            cs = slice(c * D, (c + 1) * D)
            dff = _dot_nt(ddn, wd_ref[cs, :])
            dup = (dff * (2.0 * ru_ref[:, cs].astype(F32))).astype(BF16)
            dup_ref[:, cs] = dup
            part = _dot_nt(dup, wu_ref[:, cs])
            dh2 = part if dh2 is None else dh2 + part
        x2 = x2_ref[...]
        r2 = lax.rsqrt(jnp.mean(x2 * x2, axis=-1, keepdims=True) + EPS)
        xn2 = x2 * r2
        gain = vec_ref[V_G2:V_G2 + 1, :] * (1.0 + mod_ref[M_SC2:M_SC2 + 1, :])
        dxn2 = dh2 * gain
        dx2 = dx3_ref[...] + r2 * (dxn2 - xn2 * jnp.mean(dxn2 * xn2, axis=-1, keepdims=True))
        dx2_ref[...] = dx2
        do_ref[...] = (dx2 * mod_ref[M_GT1:M_GT1 + 1, :]).astype(BF16)
        small_ref[0:1, :] += jnp.sum(dh2, axis=0, keepdims=True)
        small_ref[1:2, :] += jnp.sum(dh2 * xn2, axis=0, keepdims=True)
        small_ref[2:3, :] += jnp.sum(dx2 * o_ref[...].astype(F32), axis=0, keepdims=True)

    tok = pl.BlockSpec((tm, D), lambda i: (i, 0))
    wide = pl.BlockSpec((tm, D_FF), lambda i: (i, 0))
    return pl.pallas_call(
        body, name="mlp_bwd", grid=(s // tm,),
        in_specs=[tok, wide, tok, tok, tok,
                  pl.BlockSpec((8, D), lambda i: (0, 0)), pl.BlockSpec((16, D), lambda i: (0, 0)),
                  _resident((D, D_FF)), _resident((D_FF, D))],
        out_specs=[wide, tok, tok, pl.BlockSpec((8, D), lambda i: (0, 0))],
        out_shape=[jax.ShapeDtypeStruct((s, D_FF), BF16), jax.ShapeDtypeStruct((s, D), F32),
                   jax.ShapeDtypeStruct((s, D), BF16), jax.ShapeDtypeStruct((8, D), F32)],
        compiler_params=_params(("arbitrary",)),
    )(d_dn, ru, x2, dx3, o, modr, vecs, w_up, w_down)


def _branch_bwd(do, proj, ba, bb, w_a, w_b, w_out, dep):
    s = do.shape[0]
    tm = min(TM_BRANCH, s)

    def body(do_ref, ga_ref, gb_ref, ba_ref, bb_ref, wa_ref, wb_ref, wo_ref, dep_ref,
             dba_ref, dbb_ref, dg_ref, dza_ref, dpooled_ref):
        dmerged = _dot_nt(do_ref[...], wo_ref[...])
        sa = _sigmoid(ga_ref[...])
        sb = _sigmoid(gb_ref[...])
        dba = (dmerged * sa).astype(BF16)
        dbb = (dmerged * sb).astype(BF16)
        dba_ref[...] = dba
        dbb_ref[...] = dbb
        dg_ref[:, :D] = (dmerged * ba_ref[...].astype(F32) * sa * (1.0 - sa)).astype(BF16)
        dg_ref[:, D:] = (dmerged * bb_ref[...].astype(F32) * sb * (1.0 - sb)).astype(BF16)
        dza_ref[...] = _dot_nt(dba, wa_ref[...])
        dpooled_ref[...] = _dot_nt(dbb, wb_ref[...])

    tok = pl.BlockSpec((tm, D), lambda i: (i, 0))
    wspec = pl.BlockSpec((D, D), lambda i: (0, 0))
    sd = lambda dt: jax.ShapeDtypeStruct((s, D), dt)
    return pl.pallas_call(
        body, name="branch_bwd", grid=(s // tm,),
        in_specs=[tok, pl.BlockSpec((tm, D), lambda i: (i, 3)), pl.BlockSpec((tm, D), lambda i: (i, 4)),
                  tok, tok, wspec, wspec, wspec, pl.BlockSpec(memory_space=pl.ANY)],
        out_specs=[tok, tok, pl.BlockSpec((tm, 2 * D), lambda i: (i, 0)), tok, tok],
        out_shape=[sd(BF16), sd(BF16), jax.ShapeDtypeStruct((s, 2 * D), BF16), sd(F32), sd(F32)],
        compiler_params=_params(("parallel",)),
    )(do, proj, proj, ba, bb, w_a, w_b, w_out, dep)


def _mix_bwd(dza, dpooled, proj, xr, hr, p, gates, dgates, vecs, w_rg_a, w_rg_x, w_pool):
    s = xr.shape[0]
    tm = min(TM_MIX, s)
    nb = s // tm

    def body(dza_ref, dpooled_ref, xh_ref, x_ref, y_ref, xr_ref, hh_ref, hr_ref, p_ref,
             a_ref, mult_ref, ra_ref, ri_ref, dg_ref, vec_ref, wa_ref, wx_ref, wp_ref,
             dproj_ref, dwa_ref, dwx_ref, dwp_ref, small_ref,
             scan_carry, dxr_carry, q_carry):
        i = pl.program_id(0)
        bi = nb - 1 - i
        first_t = bi == 0

        @pl.when(i == 0)
        def _():
            scan_carry[...] = jnp.zeros_like(scan_carry)
            dxr_carry[...] = jnp.zeros_like(dxr_carry)
            q_carry[...] = jnp.zeros_like(q_carry)
            dwa_ref[...] = jnp.zeros_like(dwa_ref)
            dwx_ref[...] = jnp.zeros_like(dwx_ref)
            dwp_ref[...] = jnp.zeros_like(dwp_ref)
            small_ref[...] = jnp.zeros_like(small_ref)

        row = lax.broadcasted_iota(jnp.int32, (tm, GW), 0)
        is_t0 = jnp.logical_and(first_t, row == 0)
        t_glob = (row + bi * tm + 1).astype(F32)
        colsum = lambda v: jnp.sum(v, axis=0, keepdims=True)
        for g in range(N_GROUPS):
            cs = slice(g * GW, (g + 1) * GW)
            vec = vec_ref[:, cs]
            xr = xr_ref[:, cs]
            hr = hr_ref[:, cs]
            dza = dza_ref[:, cs]
            ga, dga = _gelu_and_grad(y_ref[:, cs])
            dproj_ref[:, D + g * GW:D + (g + 1) * GW] = (dza * hr * dga).astype(BF16)
            dhr = dza * ga
            a = a_ref[:, cs]
            mult = mult_ref[:, cs]
            ra = ra_ref[:, cs].astype(F32)
            ri = ri_ref[:, cs].astype(F32)
            sp = _softplus(vec[V_A_PARAM:V_A_PARAM + 1])
            m = jnp.where(row == tm - 1, 1.0, _shift_up(a, 1))
            gsum = _scan_up(m, dhr, scan_carry[0:1, cs])
            scan_carry[0:1, cs] = a[0:1, :] * gsum[0:1, :]
            hh = jnp.where(first_t, 0.0, hh_ref[:, cs])
            hprev = _shift_down(jnp.concatenate([hh, hr], axis=0), 1)[8:]
            da = gsum * hprev
            dmult = jnp.where(is_t0, 0.0, gsum * xr * ri)
            dlog_a = da * a - dmult * a * a / mult
            dri = gsum * xr * mult
            dxr = gsum * ri * mult
            small_ref[7:8, cs] += colsum((-C_RG) * ra * dlog_a)
            dpa = (((-C_RG) * sp) * dlog_a * ra * (1.0 - ra))
            dpx = dri * ri * (1.0 - ri)
            small_ref[5:6, cs] += colsum(dpa)
            small_ref[6:7, cs] += colsum(dpx)
            dpa = dpa.astype(BF16)
            dpx = dpx.astype(BF16)
            xrb = xr.astype(BF16)
            dwa_ref[g] += _dot_tn(xrb, dpa)
            dwx_ref[g] += _dot_tn(xrb, dpx)
            dxr = dxr + _dot_nt(dpa, wa_ref[g]) + _dot_nt(dpx, wx_ref[g])
            small_ref[4:5, cs] += colsum(dxr)
            xh = jnp.where(first_t, 0.0, xh_ref[:, cs])
            taps = _conv_taps(jnp.concatenate([xh, x_ref[:, cs]], axis=0))
            dxr_ext = jnp.concatenate([dxr, dxr_carry[:, cs]], axis=0)
            dx = vec[V_CONV_W + 3:V_CONV_W + 4] * dxr
            for j in range(4):
                small_ref[j:j + 1, cs] += colsum(dxr * taps[j])
                if j < 3:
                    dx = dx + vec[V_CONV_W + j:V_CONV_W + j + 1] * _shift_up(dxr_ext, 3 - j)[:tm]
            dxr_carry[:, cs] = dxr[0:8, :]
            dproj_ref[:, cs] = dx.astype(BF16)
            pg = p_ref[:, cs]
            dpooled = dpooled_ref[:, cs]
            pb = _dot(pg, wp_ref[g]) + vec[V_B_POOL:V_B_POOL + 1]
            small_ref[9:10, cs] += colsum(dpooled * pb)
            dpb = dpooled * vec[V_POOL_SCALE:V_POOL_SCALE + 1]
            small_ref[8:9, cs] += colsum(dpb)
            dpbb = dpb.astype(BF16)
            dwp_ref[g] += _dot_tn(pg, dpbb)
            dp = _dot_nt(dpbb, wp_ref[g])
            q = dp / jnp.minimum(t_glob, float(POOL_WINDOWS[g]))
            sm = jnp.concatenate([q, q_carry[:, cs]], axis=0)
            k = 1
            while k < POOL_WINDOWS[g]:
                sm = sm + _shift_up(sm, k)
                k *= 2
            q_carry[:, cs] = q[0:HALO_U, :]
            dproj_ref[:, 2 * D + g * GW:2 * D + (g + 1) * GW] = (sm[:tm] - dp).astype(BF16)
        dproj_ref[:, 3 * D:] = dg_ref[...]

    rev = lambda i: nb - 1 - i
    tok = pl.BlockSpec((tm, D), lambda i: (rev(i), 0))
    col = lambda k: pl.BlockSpec((tm, D), lambda i: (rev(i), k))
    halo8 = lambda k: pl.BlockSpec((8, D), lambda i: (jnp.maximum(rev(i) * (tm // 8) - 1, 0), k))
    wspec = pl.BlockSpec((N_GROUPS, GW, GW), lambda i: (0, 0, 0))
    wshape = jax.ShapeDtypeStruct((N_GROUPS, GW, GW), F32)
    return pl.pallas_call(
        body, name="mix_bwd", grid=(nb,),
        in_specs=[tok, tok, halo8(0), col(0), col(1), tok, halo8(0), tok, tok, tok, tok, tok, tok,
                  pl.BlockSpec((tm, 2 * D), lambda i: (rev(i), 0)),
                  pl.BlockSpec((16, D), lambda i: (0, 0)), wspec, wspec, wspec],
        out_specs=[pl.BlockSpec((tm, D_IN), lambda i: (rev(i), 0)), wspec, wspec, wspec,
                   pl.BlockSpec((16, D), lambda i: (0, 0))],
        out_shape=[jax.ShapeDtypeStruct((s, D_IN), BF16), wshape, wshape, wshape,
                   jax.ShapeDtypeStruct((16, D), F32)],
        scratch_shapes=[pltpu.VMEM((8, D), F32), pltpu.VMEM((8, D), F32), pltpu.VMEM((HALO_U, D), F32)],
        compiler_params=_params(("arbitrary",)),
    )(dza, dpooled, proj, proj, proj, xr, hr, hr, p, *gates, dgates, vecs, w_rg_a, w_rg_x, w_pool)


def _proj_bwd(dproj, x, dx2, modr, vecs, w_in):
    s = x.shape[0]
    tm = min(TM_PROJ, s)

    def body(dp_ref, x_ref, dx2_ref, mod_ref, vec_ref, w_ref, gx_ref, small_ref):
        @pl.when(pl.program_id(0) == 0)
        def _():
            small_ref[...] = jnp.zeros_like(small_ref)

        dh1 = None
        for c in range(D_IN // D):
            cs = slice(c * D, (c + 1) * D)
            part = _dot_nt(dp_ref[:, cs], w_ref[:, cs])
            dh1 = part if dh1 is None else dh1 + part
        xv = x_ref[...]
        r1 = lax.rsqrt(jnp.mean(xv * xv, axis=-1, keepdims=True) + EPS)
        xn1 = xv * r1
        gain = vec_ref[V_G1:V_G1 + 1, :] * (1.0 + mod_ref[M_SC1:M_SC1 + 1, :])
        dxn1 = dh1 * gain
        gx_ref[...] = dx2_ref[...] + r1 * (dxn1 - xn1 * jnp.mean(dxn1 * xn1, axis=-1, keepdims=True))
        small_ref[0:1, :] += jnp.sum(dh1, axis=0, keepdims=True)
        small_ref[1:2, :] += jnp.sum(dh1 * xn1, axis=0, keepdims=True)

    tok = pl.BlockSpec((tm, D), lambda i: (i, 0))
    return pl.pallas_call(
        body, name="proj_bwd", grid=(s // tm,),
        in_specs=[pl.BlockSpec((tm, D_IN), lambda i: (i, 0)), tok, tok,
                  pl.BlockSpec((8, D), lambda i: (0, 0)), pl.BlockSpec((16, D), lambda i: (0, 0)),
                  _resident((D, D_IN))],
        out_specs=[tok, pl.BlockSpec((8, D), lambda i: (0, 0))],
        out_shape=[jax.ShapeDtypeStruct((s, D), F32), jax.ShapeDtypeStruct((8, D), F32)],
        compiler_params=_params(("arbitrary",)),
    )(dproj, x, dx2, modr, vecs, w_in)


def _wgrad(a, b, name, square_a=False, dep=None):
    s, ka = a.shape
    n = b.shape[1]
    tka = ka if ka <= 1024 else ka // 2
    tn = n if n <= 1024 else n // 2
    ts = min(TS_WGRAD, s)
    ns = s // ts
    nc = 512
    deps = [] if dep is None else [dep]

    def body(a_ref, b_ref, *refs):
        out_ref, acc_ref = refs[-2:]
        t = pl.program_id(2)

        @pl.when(t == 0)
        def _():
            acc_ref[...] = jnp.zeros_like(acc_ref)

        av = a_ref[...]
        if square_a:
            af = av.astype(F32)
            av = (af * af).astype(BF16)
        for c in range(tn // nc):
            cs = slice(c * nc, (c + 1) * nc)
            acc_ref[:, cs] += _dot_tn(av, b_ref[:, cs])

        @pl.when(t == ns - 1)
        def _():
            out_ref[...] = acc_ref[...].astype(BF16)

    return pl.pallas_call(
        body, name=name, grid=(ka // tka, n // tn, ns),
        in_specs=[pl.BlockSpec((ts, tka), lambda i, j, t: (t, i)),
                  pl.BlockSpec((ts, tn), lambda i, j, t: (t, j))] + [pl.BlockSpec(memory_space=pl.ANY)] * len(deps),
        out_specs=pl.BlockSpec((tka, tn), lambda i, j, t: (i, j)),
        out_shape=jax.ShapeDtypeStruct((ka, n), BF16),
        scratch_shapes=[pltpu.VMEM((tka, tn), F32)],
        compiler_params=_params(("parallel", "parallel", "arbitrary")),
    )(a, b, *deps)


def _window(ref, kind, idx, size):
    start = pl.multiple_of(idx * size, size)
    if kind == 0:
        return ref.at[pl.ds(start, size)]
    if kind == 1:
        return ref.at[:, pl.ds(start, size)]
    return ref.at[:, :, pl.ds(start, size)]


def _mesh_place():
    x, y, c = lax.axis_index("x"), lax.axis_index("y"), lax.axis_index("c")
    return x, y, c, 4 * x + 2 * y + c


def _peer(x, y, c, q):
    px = 1 - x if q & 4 else x
    py = 1 - y if q & 2 else y
    pc = 1 - c if q & 1 else c
    return (px, py, pc), 4 * px + 2 * py + pc


def _all_gather(shards, kinds, name, dep=None):
    n = len(shards)
    deps = [] if dep is None else [dep]
    full_shapes = []
    for sh, kind in zip(shards, kinds):
        dims = list(sh.shape)
        dims[kind] *= N_DEV
        full_shapes.append(jax.ShapeDtypeStruct(tuple(dims), sh.dtype))

    def body(*refs):
        ins, outs = refs[:n], refs[n + len(deps):2 * n + len(deps)]
        send_sems, recv_sems, local_sems = refs[2 * n + len(deps):]
        x, y, c, me = _mesh_place()
        sends, recvs, locals_ = [], [], []
        for k in range(n):
            size = shards[k].shape[kinds[k]]
            mine = _window(outs[k], kinds[k], me, size)
            lc = pltpu.make_async_copy(ins[k], mine, local_sems.at[k])
            lc.start()
            locals_.append(lc)
            for q in range(1, N_DEV):
                peer, peer_idx = _peer(x, y, c, q)
                cp = pltpu.make_async_remote_copy(
                    src_ref=ins[k], dst_ref=mine, send_sem=send_sems.at[k, q], recv_sem=recv_sems.at[k, q],
                    device_id=peer, device_id_type=MESH)
                cp.start()
                sends.append(cp)
                recvs.append(pltpu.make_async_remote_copy(
                    src_ref=ins[k], dst_ref=_window(outs[k], kinds[k], peer_idx, size),
                    send_sem=send_sems.at[k, q], recv_sem=recv_sems.at[k, q],
                    device_id=peer, device_id_type=MESH))
        for cp in recvs:
            cp.wait_recv()
        for cp in sends:
            cp.wait_send()
        for lc in locals_:
            lc.wait()

    any_spec = pl.BlockSpec(memory_space=pl.ANY)
    return pl.pallas_call(
        body, name=name,
        in_specs=[any_spec] * (n + len(deps)), out_specs=[any_spec] * n, out_shape=full_shapes,
        scratch_shapes=[pltpu.SemaphoreType.DMA((n, N_DEV)), pltpu.SemaphoreType.DMA((n, N_DEV)),
                        pltpu.SemaphoreType.DMA((n,))],
    )(*shards, *deps)


_HBM = pl.BlockSpec(memory_space=pltpu.HBM)
_SEM = pl.BlockSpec(memory_space=pltpu.SEMAPHORE)
_EFFECT = pltpu.SideEffectType.DATAFLOW_SIDE_EFFECTING


def _exchange_copies(gather, kinds, src_refs, land_refs, send_sems, recv_sems):
    x, y, c, me = _mesh_place()
    sends, recvs = [], []
    for k in range(len(kinds)):
        for q in range(1, N_DEV):
            peer, peer_idx = _peer(x, y, c, q)
            if gather:
                size = src_refs[k].shape[kinds[k]]
                src = src_refs[k]
                dst = _window(land_refs[k], kinds[k], me, size)
                arriving = _window(land_refs[k], kinds[k], peer_idx, size)
            else:
                size = src_refs[k].shape[kinds[k]] // N_DEV
                src = _window(src_refs[k], kinds[k], peer_idx, size)
                dst = land_refs[k].at[me]
                arriving = land_refs[k].at[peer_idx]
            sems = dict(send_sem=send_sems.at[k * N_DEV + q], recv_sem=recv_sems.at[k * N_DEV + q],
                        device_id=peer, device_id_type=MESH)
            sends.append(pltpu.make_async_remote_copy(src_ref=src, dst_ref=dst, **sems))
            recvs.append(pltpu.make_async_remote_copy(src_ref=src, dst_ref=arriving, **sems))
    return sends, recvs


def _exchange_start(gather, srcs, lands, kinds, after, name):
    n = len(srcs)

    def body(*refs):
        src_refs, land_refs = refs[:n], refs[n:2 * n]
        send_sems, recv_sems = refs[2 * n + 1], refs[2 * n + 2]
        token = refs[-1]
        sends, _ = _exchange_copies(gather, kinds, src_refs, land_refs, send_sems, recv_sems)
        for cp in sends:
            cp.start()
        token[...] = jnp.zeros_like(token)

    hbm = lambda a: pltpu.HBM(a.shape, a.dtype)
    outs = pl.pallas_call(
        body, name=name,
        out_shape=(pltpu.SemaphoreType.DMA((n * N_DEV,)), pltpu.SemaphoreType.DMA((n * N_DEV,)),
                   *[hbm(a) for a in srcs], *[hbm(a) for a in lands], jax.ShapeDtypeStruct((8, 128), F32)),
        in_specs=[_HBM] * (2 * n) + [pl.BlockSpec(memory_space=pl.ANY)],
        out_specs=(_SEM, _SEM, *[_HBM] * (2 * n), pl.BlockSpec(memory_space=pltpu.VMEM)),
        input_output_aliases={i: 2 + i for i in range(2 * n)},
        compiler_params=pltpu.CompilerParams(has_side_effects=_EFFECT),
    )(*[pltpu.with_memory_space_constraint(a, pltpu.HBM) for a in (*srcs, *lands)], after)
    return outs[0], outs[1], outs[2:2 + n], outs[2 + n:2 + 2 * n], outs[-1]


def _exchange_wait(gather, started, kinds, after, name):
    send_sems, recv_sems, srcs, lands, _ = started
    n = len(srcs)

    def body(*refs):
        src_refs, land_refs = refs[:n], refs[n:2 * n]
        sends, recvs = _exchange_copies(gather, kinds, src_refs, land_refs, refs[2 * n], refs[2 * n + 1])
        for cp in sends:
            cp.wait_send()
        for cp in recvs:
            cp.wait_recv()

    hbm = lambda a: pltpu.HBM(a.shape, a.dtype)
    outs = pl.pallas_call(
        body, name=name,
        out_shape=(*[hbm(a) for a in srcs], *[hbm(a) for a in lands]),
        in_specs=[_HBM] * (2 * n) + [_SEM, _SEM, pl.BlockSpec(memory_space=pl.ANY)],
        out_specs=tuple([_HBM] * (2 * n)),
        input_output_aliases={i: i for i in range(2 * n)},
        compiler_params=pltpu.CompilerParams(has_side_effects=_EFFECT),
    )(*srcs, *lands, send_sems, recv_sems, after)
    return outs[n:]


def _after(small, token):
    return small + token[0:1, 0:1].astype(small.dtype)


def _own_window(kind, shard, me):
    dims = list(shard.shape)
    dims[kind] *= N_DEV
    start = [0] * len(dims)
    start[kind] = me * shard.shape[kind]
    return lax.dynamic_update_slice(lax.empty(tuple(dims), shard.dtype), shard, tuple(start))


def _own_slot(kind, full, me):
    size = full.shape[kind] // N_DEV
    mine = lax.dynamic_slice_in_dim(full, me * size, size, axis=kind)
    return lax.dynamic_update_index_in_dim(lax.empty((N_DEV, *mine.shape), full.dtype), mine, me, 0)


def _silu(c):
    return c * _sigmoid_tail(c)


def _ada_fwd(c_all, w_ada, b_ada_cols):
    def body(c_ref, w_ref, b_ref, out_ref):
        out_ref[...] = jnp.dot(_silu(c_ref[...]), w_ref[...], preferred_element_type=F32,
                               precision=lax.Precision.HIGHEST) + b_ref[...]

    return pl.pallas_call(
        body, name="ada_fwd", out_shape=jax.ShapeDtypeStruct((N_DEV, w_ada.shape[1]), F32),
    )(c_all, w_ada, b_ada_cols)


def _adam(w, g, m, v):
    m = ADAM_B1 * m + (1.0 - ADAM_B1) * g
    v = ADAM_B2 * v + (1.0 - ADAM_B2) * (g * g)
    m_hat = m / (1.0 - ADAM_B1 ** ADAM_STEP)
    v_hat = v / (1.0 - ADAM_B2 ** ADAM_STEP)
    delta = -ADAM_LR * (m_hat / (jnp.sqrt(v_hat) + ADAM_EPS) + ADAM_WD * w)
    return delta, m, v


def _ada_bwd_adam(c_all, dmod_cols, w, m, v):
    def body(c_ref, d_ref, w_ref, m_ref, v_ref, g_ref, delta_ref, nm_ref, nv_ref):
        g = lax.dot_general(_silu(c_ref[...]), d_ref[...], (((0,), (0,)), ((), ())),
                            preferred_element_type=F32, precision=lax.Precision.HIGHEST)
        g_ref[...] = g
        delta_ref[...], nm_ref[...], nv_ref[...] = _adam(w_ref[...], g, m_ref[...], v_ref[...])

    sd = jax.ShapeDtypeStruct(w.shape, F32)
    return pl.pallas_call(body, name="ada_bwd_adam", out_shape=[sd] * 4,
                          compiler_params=pltpu.CompilerParams(vmem_limit_bytes=V7X_VMEM_LIMIT),
                          )(c_all, dmod_cols, w, m, v)


def _sum_slots_adam(slots, w, m, v, name):
    r, cdim = w.shape
    tr = min(r, 128)

    def body(s_ref, w_ref, m_ref, v_ref, g_ref, delta_ref, nm_ref, nv_ref):
        g = s_ref[0].astype(F32)
        for p in range(1, N_DEV):
            g = g + s_ref[p].astype(F32)
        g_ref[...] = g
        delta_ref[...], nm_ref[...], nv_ref[...] = _adam(w_ref[...], g, m_ref[...], v_ref[...])

    blk = pl.BlockSpec((tr, cdim), lambda i: (i, 0))
    sd = jax.ShapeDtypeStruct((r, cdim), F32)
    return pl.pallas_call(
        body, name=name, grid=(r // tr,),
        in_specs=[pl.BlockSpec((N_DEV, tr, cdim), lambda i: (0, i, 0)), blk, blk, blk],
        out_specs=[blk] * 4, out_shape=[sd] * 4,
        compiler_params=_params(("parallel",)),
    )(slots, w, m, v)


N_SMALL = 40
N_SMALL_PARAMS = 11


def _pack_vecs(conv_w_full, rows):
    def body(cw_ref, *refs):
        out = refs[-1]
        out[...] = jnp.zeros_like(out)
        out[0:4, :] = cw_ref[0:4, :]
        for r, ref in enumerate(refs[:-1]):
            out[4 + r:5 + r, :] = ref[...]

    return pl.pallas_call(body, name="pack_vecs", out_shape=jax.ShapeDtypeStruct((16, D), F32))(conv_w_full, *rows)


def _small_finish(gathered, mod_all, vecs, ws, ms, vs):
    n = N_SMALL_PARAMS

    def body(g_ref, mod_ref, vec_ref, *refs):
        w_refs, m_refs, v_refs = refs[:n], refs[n:2 * n], refs[2 * n:3 * n]
        outs = refs[3 * n:]
        g1 = vec_ref[V_G1:V_G1 + 1, :]
        g2 = vec_ref[V_G2:V_G2 + 1, :]
        zero = jnp.zeros((1, D), F32)
        dg1, dg2, dgf, loss_lanes = zero, zero, zero, zero
        mixer = jnp.zeros((16, D), F32)
        db_ada = jnp.zeros((6, D), F32)
        for b in range(N_DEV):
            gb = g_ref[b]
            mod = mod_ref[b]
            q1 = gb[33:34]
            q2 = gb[9:10]
            dmod = jnp.concatenate([gb[32:33], q1 * g1, gb[10:11], gb[8:9], q2 * g2, gb[1:2]], axis=0)
            outs[4 * n][b] = dmod
            db_ada = db_ada + dmod
            dg1 = dg1 + q1 * (1.0 + mod[M_SC1:M_SC1 + 1])
            dg2 = dg2 + q2 * (1.0 + mod[M_SC2:M_SC2 + 1])
            dgf = dgf + gb[0:1]
            loss_lanes = loss_lanes + gb[2:3]
            mixer = mixer + gb[16:32]
        d_a_param = mixer[7:8] * _sigmoid_tail(vec_ref[V_A_PARAM:V_A_PARAM + 1, :])
        grads = [dg1, dg2, mixer[4:5], mixer[5:6], mixer[6:7], d_a_param, mixer[8:9], mixer[9:10], dgf,
                 db_ada, mixer[0:4]]
        for k in range(n):
            outs[k][...] = grads[k]
            outs[n + k][...], outs[2 * n + k][...], outs[3 * n + k][...] = _adam(
                w_refs[k][...], grads[k], m_refs[k][...], v_refs[k][...])
        outs[4 * n + 1][...] = jnp.broadcast_to(jnp.sum(loss_lanes, axis=1, keepdims=True), (8, 128))

    shapes = [jax.ShapeDtypeStruct(w.shape, F32) for w in ws]
    return pl.pallas_call(
        body, name="small_finish",
        out_shape=shapes * 4 + [jax.ShapeDtypeStruct((N_DEV, 6, D), F32), jax.ShapeDtypeStruct((8, 128), F32)],
    )(gathered, mod_all, vecs, *ws, *ms, *vs)


def _pad_rows(a, rows):
    return jnp.pad(a, ((0, rows - a.shape[0]), (0, 0)))


def kernel(x, c, norm_mix_g, norm_mlp_g, w_ada, b_ada, w_in, conv_w, conv_b, w_rg_a, b_rg_a, w_rg_x, b_rg_x, a_param, w_branch_a, w_pool, b_pool, pool_scale, w_branch_b, w_out, w_up, w_down, final_g, loss_target, m_norm_mix_g, m_norm_mlp_g, m_w_ada, m_b_ada, m_w_in, m_conv_w, m_conv_b, m_w_rg_a, m_b_rg_a, m_w_rg_x, m_b_rg_x, m_a_param, m_w_branch_a, m_w_pool, m_b_pool, m_pool_scale, m_w_branch_b, m_w_out, m_w_up, m_w_down, m_final_g, v_norm_mix_g, v_norm_mlp_g, v_w_ada, v_b_ada, v_w_in, v_conv_w, v_conv_b, v_w_rg_a, v_b_rg_a, v_w_rg_x, v_b_rg_x, v_a_param, v_w_branch_a, v_w_pool, v_b_pool, v_pool_scale, v_w_branch_b, v_w_out, v_w_up, v_w_down, v_final_g):
    me = 4 * lax.axis_index("x") + 2 * lax.axis_index("y") + lax.axis_index("c")
    s = x.shape[1]
    x2d = x.reshape(s, D)
    target = loss_target.reshape(s, D)
    n_ada = w_ada.shape[2]

    sharded = dict(w_in=(w_in[0], 1), w_up=(w_up[0], 1), w_down=(w_down[0], 0), w_branch_a=(w_branch_a[0], 0),
                   w_branch_b=(w_branch_b[0], 0), w_out=(w_out[0], 0), w_rg_a=(w_rg_a[0], 1), w_rg_x=(w_rg_x[0], 1),
                   w_pool=(w_pool[0], 1))
    kind = {k: v[1] for k, v in sharded.items()}
    shard = {k: v[0].astype(BF16) for k, v in sharded.items()}

    w_in_full, conv_w_full, c_rows = _all_gather([shard["w_in"], _pad_rows(conv_w[0], 8), _pad_rows(c, 8)],
                                                 [1, 1, 0], "gather_first")
    c_all = c_rows.reshape(N_DEV, 8, D)[:, 0, :]

    b_ada_cols = lax.dynamic_slice(b_ada, (0, me * n_ada), (1, n_ada))
    mod_part = _ada_fwd(c_all, w_ada[0], b_ada_cols)
    mod_parts, = _all_gather([mod_part], [0], "gather_mod")
    mod_all = jnp.transpose(mod_parts.reshape(N_DEV, N_DEV, n_ada), (1, 0, 2)).reshape(N_DEV, 6, D)
    mod_all = jnp.pad(mod_all, ((0, 0), (0, 2), (0, 0)))
    modr = lax.dynamic_index_in_dim(mod_all, me, 0, keepdims=False)
    vecs = _pack_vecs(conv_w_full, [conv_b, b_rg_a, b_rg_x, a_param, b_pool, pool_scale,
                                    norm_mix_g, norm_mlp_g, final_g.reshape(1, D)])

    mixer_names = ["w_rg_a", "w_rg_x", "w_pool", "w_branch_a", "w_branch_b", "w_out"]
    mlp_names = ["w_up", "w_down"]

    def start_gather(group, after, name):
        return _exchange_start(True, [shard[k] for k in group], [_own_window(kind[k], shard[k], me) for k in group],
                               [kind[k] for k in group], after, name)

    g_mixer = start_gather(mixer_names, modr, "gather_mixer_start")
    g_mlp = start_gather(mlp_names, g_mixer[-1], "gather_mlp_start")

    proj, h1 = _proj_fwd(x2d, _after(modr, g_mlp[-1]), vecs, w_in_full)
    wg = dict(zip(mixer_names, _exchange_wait(True, g_mixer, [kind[k] for k in mixer_names], h1, "gather_mixer_wait")))
    xr, hr, za, p, pooled, *gates = _mix_fwd(proj, vecs, wg["w_rg_a"], wg["w_rg_x"], wg["w_pool"])
    ba, bb, merged, o, x2, h2 = _branch_fwd(za, pooled, proj, x2d, modr, vecs,
                                            wg["w_branch_a"], wg["w_branch_b"], wg["w_out"])
    wg.update(zip(mlp_names, _exchange_wait(True, g_mlp, [kind[k] for k in mlp_names], h2, "gather_mlp_wait")))
    ru, dx3, d_dn, small_f = _mlp_fwd(h2, x2, target, modr, vecs, wg["w_up"], wg["w_down"])

    def start_scatter(group, partial, after, name):
        return _exchange_start(False, [partial[k] for k in group], [_own_slot(kind[k], partial[k], me) for k in group],
                               [kind[k] for k in group], after, name)

    dup, dx2, do, small_m = _mlp_bwd(d_dn, ru, x2, dx3, o, modr, vecs, wg["w_up"], wg["w_down"])
    partial = dict(w_up=_wgrad(h2, dup, "wgrad_up"), w_down=_wgrad(ru, d_dn, "wgrad_down", square_a=True))
    s_mlp = start_scatter(mlp_names, partial, dx2, "scatter_mlp_start")

    dba, dbb, dgates, dza, dpooled = _branch_bwd(do, proj, ba, bb, wg["w_branch_a"], wg["w_branch_b"], wg["w_out"],
                                                 dep=s_mlp[-1])
    dproj, dw_rg_a, dw_rg_x, dw_pool, small_x = _mix_bwd(dza, dpooled, proj, xr, hr, p, gates, dgates, vecs,
                                                         wg["w_rg_a"], wg["w_rg_x"], wg["w_pool"])
    partial.update(w_branch_a=_wgrad(za, dba, "wgrad_branch_a"), w_branch_b=_wgrad(pooled, dbb, "wgrad_branch_b"),
                   w_out=_wgrad(merged, do, "wgrad_out"),
                   w_rg_a=dw_rg_a.astype(BF16), w_rg_x=dw_rg_x.astype(BF16), w_pool=dw_pool.astype(BF16))
    s_mixer = start_scatter(mixer_names, partial, s_mlp[-1], "scatter_mixer_start")

    partial["w_in"] = _wgrad(h1, dproj, "wgrad_in", dep=s_mixer[-1])
    s_in = start_scatter(["w_in"], partial, s_mixer[-1], "scatter_in_start")
    grad_x, small_p = _proj_bwd(dproj, x2d, dx2, _after(modr, s_in[-1]), vecs, w_in_full)

    locals_ = dict(w_in=(w_in, m_w_in, v_w_in), w_up=(w_up, m_w_up, v_w_up), w_down=(w_down, m_w_down, v_w_down),
                   w_branch_a=(w_branch_a, m_w_branch_a, v_w_branch_a),
                   w_branch_b=(w_branch_b, m_w_branch_b, v_w_branch_b), w_out=(w_out, m_w_out, v_w_out),
                   w_rg_a=(w_rg_a, m_w_rg_a, v_w_rg_a), w_rg_x=(w_rg_x, m_w_rg_x, v_w_rg_x),
                   w_pool=(w_pool, m_w_pool, v_w_pool))
    res = {}

    def finish(group, started, after, name):
        slots = _exchange_wait(False, started, [kind[k] for k in group], after, name)
        for k, sl in zip(group, slots):
            w, m, v = locals_[k]
            shape2d = (-1, w.shape[-1])
            sl = sl.reshape(N_DEV, *w.reshape(shape2d).shape)
            outs = _sum_slots_adam(sl, w.reshape(shape2d), m.reshape(shape2d), v.reshape(shape2d), "adam_" + k)
            res[k] = [t.reshape(w.shape) for t in outs]
        return res[group[-1]][0]

    done = finish(mlp_names, s_mlp, grad_x, "scatter_mlp_wait")
    done = finish(mixer_names, s_mixer, done, "scatter_mixer_wait")
    done = finish(["w_in"], s_in, done, "scatter_in_wait")

    small = jnp.concatenate([small_f, small_m, small_x, small_p], axis=0)
    small_all, = _all_gather([small], [0], "gather_small", dep=done)
    small_all = small_all.reshape(N_DEV, N_SMALL, D)

    def embed(cw):
        return lax.dynamic_update_slice(jnp.zeros((4, D), F32), cw[0], (0, me * (D // N_DEV)))

    def smalls(ng, nl, cb, bra, brx, ap, bp, ps, fg, ba_, cw):
        return [ng, nl, cb, bra, brx, ap, bp, ps, fg.reshape(1, D), ba_.reshape(6, D), embed(cw)]

    small_names = ["norm_mix_g", "norm_mlp_g", "conv_b", "b_rg_a", "b_rg_x", "a_param", "b_pool", "pool_scale",
                   "final_g", "b_ada", "conv_w"]
    fin = _small_finish(
        small_all, mod_all, vecs,
        smalls(norm_mix_g, norm_mlp_g, conv_b, b_rg_a, b_rg_x, a_param, b_pool, pool_scale, final_g, b_ada, conv_w),
        smalls(m_norm_mix_g, m_norm_mlp_g, m_conv_b, m_b_rg_a, m_b_rg_x, m_a_param, m_b_pool, m_pool_scale,
               m_final_g, m_b_ada, m_conv_w),
        smalls(v_norm_mix_g, v_norm_mlp_g, v_conv_b, v_b_rg_a, v_b_rg_x, v_a_param, v_b_pool, v_pool_scale,
               v_final_g, v_b_ada, v_conv_w))
    dmod_all, loss_tile = fin[4 * N_SMALL_PARAMS], fin[4 * N_SMALL_PARAMS + 1]
    dmod_cols = lax.dynamic_slice(dmod_all.reshape(N_DEV, 6 * D), (0, me * n_ada), (N_DEV, n_ada))
    res["w_ada"] = [t.reshape(w_ada.shape) for t in _ada_bwd_adam(c_all, dmod_cols, w_ada[0], m_w_ada[0], v_w_ada[0])]

    def final_shape(k, t):
        if k == "final_g":
            return t.reshape(D)
        if k == "b_ada":
            return t.reshape(1, 6 * D)
        if k == "conv_w":
            return lax.dynamic_slice(t, (0, me * (D // N_DEV)), (4, D // N_DEV)).reshape(conv_w.shape)
        return t

    for i, k in enumerate(small_names):
        res[k] = [final_shape(k, fin[which * N_SMALL_PARAMS + i]) for which in range(4)]
    order = ["norm_mix_g", "norm_mlp_g", "w_ada", "b_ada", "w_in", "conv_w", "conv_b", "w_rg_a", "b_rg_a", "w_rg_x",
             "b_rg_x", "a_param", "w_branch_a", "w_pool", "b_pool", "pool_scale", "w_branch_b", "w_out", "w_up",
             "w_down", "final_g"]
    outs = [loss_tile[0, 0], grad_x.reshape(x.shape)]
    for which in range(4):
        for k in order:
            outs.append(res[k][which])
    return tuple(outs)
```

```python
import functools

import jax
import jax.numpy as jnp
from jax import lax
from jax.experimental import pallas as pl
from jax.experimental.pallas import tpu as pltpu

F32 = jnp.float32
BF16 = jnp.bfloat16
MESH = pl.DeviceIdType.MESH

N_DEV = 8
D = 1024
N_GROUPS = 4
GW = D // N_GROUPS
D_IN = 5 * D
D_FF = 4 * D
POOL_WINDOWS = (2, 4, 8, 16)
HALO_X = 8
HALO_U = 16
EPS = 1e-6
C_RG = 8.0
ADAM_LR, ADAM_B1, ADAM_B2, ADAM_EPS, ADAM_WD, ADAM_STEP = 0.001, 0.9, 0.999, 1e-08, 0.01, 10

V7X_VMEM_LIMIT = 56 * 1024 * 1024

V_CONV_W, V_CONV_B, V_B_RG_A, V_B_RG_X, V_A_PARAM, V_B_POOL, V_POOL_SCALE, V_G1, V_G2, V_GF = 0, 4, 5, 6, 7, 8, 9, 10, 11, 12
M_SH1, M_SC1, M_GT1, M_SH2, M_SC2, M_GT2 = 0, 1, 2, 3, 4, 5

TM_PROJ = 512
TM_MIX = 256
TM_BRANCH = 256
TM_MLP = 512
TM_MLP_BWD = 256
TS_WGRAD = 1024


def _params(semantics):
    return pltpu.CompilerParams(dimension_semantics=semantics, vmem_limit_bytes=V7X_VMEM_LIMIT)


def _resident(shape):
    return pl.BlockSpec(shape, lambda *_: (0,) * len(shape), pipeline_mode=pl.Buffered(1))


def _dot(a, b):
    return jnp.dot(a, b, preferred_element_type=F32)


def _dot_nt(a, b):
    return lax.dot_general(a, b, (((1,), (1,)), ((), ())), preferred_element_type=F32)


def _dot_tn(a, b):
    return lax.dot_general(a, b, (((0,), (0,)), ((), ())), preferred_element_type=F32)


def _sigmoid(x):
    return 0.5 * jnp.tanh(0.5 * x) + 0.5


def _sigmoid_tail(x):
    return 1.0 / (1.0 + jnp.exp(-x))


def _gelu_and_grad(x):
    k = 0.7978845608028654
    x2 = x * x
    t = jnp.tanh(k * (x + 0.044715 * x * x2))
    g = 0.5 * x * (1.0 + t)
    dg = 0.5 * (1.0 + t) + 0.5 * x * (1.0 - t * t) * (k * (1.0 + 3.0 * 0.044715 * x2))
    return g, dg


def _softplus(a):
    e = jnp.exp(-jnp.abs(a))
    u = 1.0 + e
    log1p_e = jnp.where(u == 1.0, e, jnp.log(u) * e / jnp.where(u == 1.0, 1.0, u - 1.0))
    return jnp.maximum(a, 0.0) + log1p_e


def _neg_expm1(z):
    series = -(z * (1.0 + z * (0.5 + z * (1.0 / 6.0 + z * (1.0 / 24.0 + z * (1.0 / 120.0))))))
    return jnp.where(z > -0.1, series, 1.0 - jnp.exp(z))


def _shift_down(x, k):
    return pltpu.roll(x, k, 0)


def _shift_up(x, k):
    return pltpu.roll(x, x.shape[0] - k, 0)


def _rglru_gates(xr, w_a, w_x, b_a, b_x, a_param, is_t0):
    xb = xr.astype(BF16)
    ra = _sigmoid(_dot(xb, w_a) + b_a)
    ri = _sigmoid(_dot(xb, w_x) + b_x)
    sp = _softplus(a_param)
    log_a = (-C_RG) * ra * sp
    a = jnp.exp(log_a)
    mult = jnp.where(is_t0, 1.0, jnp.sqrt(_neg_expm1(2.0 * log_a)))
    return ra, ri, sp, a, mult


SUBLANES = 8


LANES = 128


def _scan_strip(a, b, carry, scr, down):
    t = b.shape[0]
    g = t // SUBLANES
    a3 = a.reshape(g, SUBLANES, LANES)
    b3 = b.reshape(g, SUBLANES, LANES)
    sub = lax.broadcasted_iota(jnp.int32, (g, SUBLANES, LANES), 1)
    for k in (1, 2, 4):
        keep = sub >= k if down else sub < SUBLANES - k
        shift = k if down else SUBLANES - k
        b3 = b3 + a3 * jnp.where(keep, pltpu.roll(b3, shift, 1), 0.0)
        a3 = a3 * jnp.where(keep, pltpu.roll(a3, shift, 1), 1.0)
    scr[0] = a3.reshape(t, LANES)
    scr[1] = b3.reshape(t, LANES)
    end_row = SUBLANES - 1 if down else 0
    ag = scr[0, pl.ds(end_row, g, stride=SUBLANES), :]
    bg = scr[1, pl.ds(end_row, g, stride=SUBLANES), :]
    rg = lax.broadcasted_iota(jnp.int32, (g, LANES), 0)
    edge = 0 if down else g - 1
    bg = bg + jnp.where(rg == edge, ag * carry, 0.0)
    k = 1
    while k < g:
        keep = rg >= k if down else rg < g - k
        shift = k if down else g - k
        bg = bg + ag * jnp.where(keep, pltpu.roll(bg, shift, 0), 0.0)
        if 2 * k < g:
            ag = ag * pltpu.roll(ag, shift, 0)
        k *= 2
    entering = jnp.where(rg != edge, pltpu.roll(bg, 1 if down else g - 1, 0), carry)
    for r in range(SUBLANES):
        scr[2, pl.ds(r, g, stride=SUBLANES), :] = entering
    return scr[1] + scr[0] * scr[2], bg[g - 1:g, :]


def _scan_strips(a, b, carry, scr, down):
    outs = [_scan_strip(a[:, c:c + LANES], b[:, c:c + LANES], carry[:, c:c + LANES], scr, down)
            for c in range(0, b.shape[1], LANES)]
    return jnp.concatenate([o[0] for o in outs], axis=1), jnp.concatenate([o[1] for o in outs], axis=1)


def _scan_down(a, b, carry, scr):
    return _scan_strips(a, b, carry, scr, True)


def _scan_up(m, b, carry, scr):
    return _scan_strips(m, b, carry, scr, False)[0]


def _conv_taps(x_ext):
    return [_shift_down(x_ext, 3 - j)[HALO_X:] if j < 3 else x_ext[HALO_X:] for j in range(4)]


def _proj_fwd(x, modr, vecs, w_in):
    s = x.shape[0]
    tm = min(TM_PROJ, s)

    def body(x_ref, mod_ref, vec_ref, w_ref, proj_ref, h1_ref):
        xv = x_ref[...]
        r = lax.rsqrt(jnp.mean(xv * xv, axis=-1, keepdims=True) + EPS)
        gain = vec_ref[V_G1:V_G1 + 1, :] * (1.0 + mod_ref[M_SC1:M_SC1 + 1, :])
        h = (xv * r * gain + mod_ref[M_SH1:M_SH1 + 1, :]).astype(BF16)
        h1_ref[...] = h
        for c in range(D_IN // D):
            proj_ref[:, c * D:(c + 1) * D] = _dot(h, w_ref[:, c * D:(c + 1) * D])

    return pl.pallas_call(
        body, name="proj_fwd", grid=(s // tm,),
        in_specs=[pl.BlockSpec((tm, D), lambda i: (i, 0)),
                  pl.BlockSpec((8, D), lambda i: (0, 0)),
                  pl.BlockSpec((16, D), lambda i: (0, 0)),
                  _resident((D, D_IN))],
        out_specs=[pl.BlockSpec((tm, D_IN), lambda i: (i, 0)),
                   pl.BlockSpec((tm, D), lambda i: (i, 0))],
        out_shape=[jax.ShapeDtypeStruct((s, D_IN), F32), jax.ShapeDtypeStruct((s, D), BF16)],
        compiler_params=_params(("parallel",)),
    )(x, modr, vecs, w_in)


def _mix_fwd(proj, vecs, w_rg_a, w_rg_x, w_pool):
    s = proj.shape[0]
    tm = min(TM_MIX, s)
    nb = s // tm

    def body(xh_ref, x_ref, y_ref, uh_ref, u_ref, vec_ref, wa_ref, wx_ref, wp_ref,
             xr_ref, hr_ref, za_ref, p_ref, pooled_ref, a_ref, mult_ref, ra_ref, ri_ref, carry_ref, scan_scr):
        i = pl.program_id(0)
        first = i == 0

        @pl.when(first)
        def _():
            carry_ref[...] = jnp.zeros_like(carry_ref)

        row = lax.broadcasted_iota(jnp.int32, (tm, GW), 0)
        is_t0 = jnp.logical_and(first, row == 0)
        t_glob = (row + i * tm + 1).astype(F32)
        for g in range(N_GROUPS):
            cs = slice(g * GW, (g + 1) * GW)
            vec = vec_ref[:, cs]
            xh = jnp.where(first, 0.0, xh_ref[:, cs])
            taps = _conv_taps(jnp.concatenate([xh, x_ref[:, cs]], axis=0))
            xr = vec[V_CONV_B:V_CONV_B + 1]
            for j in range(4):
                xr = xr + vec[V_CONV_W + j:V_CONV_W + j + 1] * taps[j]
            xr_ref[:, cs] = xr
            ra, ri, _, a, mult = _rglru_gates(
                xr, wa_ref[g], wx_ref[g], vec[V_B_RG_A:V_B_RG_A + 1], vec[V_B_RG_X:V_B_RG_X + 1],
                vec[V_A_PARAM:V_A_PARAM + 1], is_t0)
            a_ref[:, cs] = a
            mult_ref[:, cs] = mult
            ra_ref[:, cs] = ra.astype(BF16)
            ri_ref[:, cs] = ri.astype(BF16)
            h, last = _scan_down(a, xr * ri * mult, carry_ref[0:1, cs], scan_scr)
            hr_ref[:, cs] = h
            carry_ref[0:1, cs] = last
            ga, _ = _gelu_and_grad(y_ref[:, cs])
            za_ref[:, cs] = (ga * h).astype(BF16)
            uh = jnp.where(first, 0.0, uh_ref[:, cs])
            sm = jnp.concatenate([uh, u_ref[:, cs]], axis=0)
            k = 1
            while k < POOL_WINDOWS[g]:
                sm = sm + _shift_down(sm, k)
                k *= 2
            cnt = jnp.minimum(t_glob, float(POOL_WINDOWS[g]))
            p = (sm[HALO_U:] / cnt - u_ref[:, cs]).astype(BF16)
            p_ref[:, cs] = p
            pb = _dot(p, wp_ref[g]) + vec[V_B_POOL:V_B_POOL + 1]
            pooled_ref[:, cs] = (pb * vec[V_POOL_SCALE:V_POOL_SCALE + 1]).astype(BF16)

    col = lambda k: (lambda i: (i, k))
    wspec = pl.BlockSpec((N_GROUPS, GW, GW), lambda i: (0, 0, 0))
    return pl.pallas_call(
        body, name="mix_fwd", grid=(nb,),
        in_specs=[pl.BlockSpec((HALO_X, D), lambda i: (jnp.maximum(i * (tm // HALO_X) - 1, 0), 0)),
                  pl.BlockSpec((tm, D), col(0)),
                  pl.BlockSpec((tm, D), col(1)),
                  pl.BlockSpec((HALO_U, D), lambda i: (jnp.maximum(i * (tm // HALO_U) - 1, 0), 2)),
                  pl.BlockSpec((tm, D), col(2)),
                  pl.BlockSpec((16, D), lambda i: (0, 0)),
                  wspec, wspec, wspec],
        out_specs=[pl.BlockSpec((tm, D), lambda i: (i, 0))] * 9,
        out_shape=[jax.ShapeDtypeStruct((s, D), F32), jax.ShapeDtypeStruct((s, D), F32),
                   jax.ShapeDtypeStruct((s, D), BF16), jax.ShapeDtypeStruct((s, D), BF16),
                   jax.ShapeDtypeStruct((s, D), BF16),
                   jax.ShapeDtypeStruct((s, D), F32), jax.ShapeDtypeStruct((s, D), F32),
                   jax.ShapeDtypeStruct((s, D), BF16), jax.ShapeDtypeStruct((s, D), BF16)],
        scratch_shapes=[pltpu.VMEM((8, D), F32), pltpu.VMEM((3, tm, LANES), F32)],
        compiler_params=_params(("arbitrary",)),
    )(proj, proj, proj, proj, proj, vecs, w_rg_a, w_rg_x, w_pool)


def _branch_fwd(za, pooled, proj, x, modr, vecs, w_a, w_b, w_out):
    s = x.shape[0]
    tm = min(TM_BRANCH, s)

    def body(za_ref, pooled_ref, ga_ref, gb_ref, x_ref, mod_ref, vec_ref, wa_ref, wb_ref, wo_ref,
             ba_ref, bb_ref, merged_ref, o_ref, x2_ref, h2_ref):
        ba = _dot(za_ref[...], wa_ref[...])
        bb = _dot(pooled_ref[...], wb_ref[...])
        ba_ref[...] = ba.astype(BF16)
        bb_ref[...] = bb.astype(BF16)
        merged = (_sigmoid(ga_ref[...]) * ba + _sigmoid(gb_ref[...]) * bb).astype(BF16)
        merged_ref[...] = merged
        o = _dot(merged, wo_ref[...])
        o_ref[...] = o.astype(BF16)
        x2 = x_ref[...] + mod_ref[M_GT1:M_GT1 + 1, :] * o
        x2_ref[...] = x2
        r = lax.rsqrt(jnp.mean(x2 * x2, axis=-1, keepdims=True) + EPS)
        gain = vec_ref[V_G2:V_G2 + 1, :] * (1.0 + mod_ref[M_SC2:M_SC2 + 1, :])
        h2_ref[...] = (x2 * r * gain + mod_ref[M_SH2:M_SH2 + 1, :]).astype(BF16)

    tok = pl.BlockSpec((tm, D), lambda i: (i, 0))
    wspec = pl.BlockSpec((D, D), lambda i: (0, 0))
    sd = lambda dt: jax.ShapeDtypeStruct((s, D), dt)
    return pl.pallas_call(
        body, name="branch_fwd", grid=(s // tm,),
        in_specs=[tok, tok,
                  pl.BlockSpec((tm, D), lambda i: (i, 3)), pl.BlockSpec((tm, D), lambda i: (i, 4)),
                  tok, pl.BlockSpec((8, D), lambda i: (0, 0)), pl.BlockSpec((16, D), lambda i: (0, 0)),
                  wspec, wspec, wspec],
        out_specs=[tok] * 6,
        out_shape=[sd(BF16), sd(BF16), sd(BF16), sd(BF16), sd(F32), sd(BF16)],
        compiler_params=_params(("parallel",)),
    )(za, pooled, proj, proj, x, modr, vecs, w_a, w_b, w_out)


def _mlp_fwd(h2, x2, target, modr, vecs, w_up, w_down):
    s = x2.shape[0]
    tm = min(TM_MLP, s)

    def body(h2_ref, x2_ref, tgt_ref, mod_ref, vec_ref, wu_ref, wd_ref,
             ru_ref, dx3_ref, ddn_ref, small_ref):
        @pl.when(pl.program_id(0) == 0)
        def _():
            small_ref[...] = jnp.zeros_like(small_ref)

        h2 = h2_ref[...]
        dn = None
        for c in range(D_FF // D):
            cs = slice(c * D, (c + 1) * D)
            ru = jnp.maximum(_dot(h2, wu_ref[:, cs]), 0.0)
            ru_ref[:, cs] = ru.astype(BF16)
            part = _dot((ru * ru).astype(BF16), wd_ref[cs, :])
            dn = part if dn is None else dn + part
        gt2 = mod_ref[M_GT2:M_GT2 + 1, :]
        gf = vec_ref[V_GF:V_GF + 1, :]
        x3 = x2_ref[...] + gt2 * dn
        r3 = lax.rsqrt(jnp.mean(x3 * x3, axis=-1, keepdims=True) + EPS)
        n3 = x3 * r3
        err = n3 * gf - tgt_ref[...]
        dy = err * (1.0 / D)
        dn3 = dy * gf
        dx3 = r3 * (dn3 - n3 * jnp.mean(dn3 * n3, axis=-1, keepdims=True))
        dx3_ref[...] = dx3
        ddn_ref[...] = (dx3 * gt2).astype(BF16)
        small_ref[0:1, :] += jnp.sum(dy * n3, axis=0, keepdims=True)
        small_ref[1:2, :] += jnp.sum(dx3 * dn, axis=0, keepdims=True)
        small_ref[2:3, :] += (0.5 / D) * jnp.sum(err * err, axis=0, keepdims=True)

    tok = pl.BlockSpec((tm, D), lambda i: (i, 0))
    return pl.pallas_call(
        body, name="mlp_fwd", grid=(s // tm,),
        in_specs=[tok, tok, tok,
                  pl.BlockSpec((8, D), lambda i: (0, 0)), pl.BlockSpec((16, D), lambda i: (0, 0)),
                  _resident((D, D_FF)), _resident((D_FF, D))],
        out_specs=[pl.BlockSpec((tm, D_FF), lambda i: (i, 0)), tok, tok,
                   pl.BlockSpec((8, D), lambda i: (0, 0))],
        out_shape=[jax.ShapeDtypeStruct((s, D_FF), BF16), jax.ShapeDtypeStruct((s, D), F32),
                   jax.ShapeDtypeStruct((s, D), BF16), jax.ShapeDtypeStruct((8, D), F32)],
        compiler_params=_params(("arbitrary",)),
    )(h2, x2, target, modr, vecs, w_up, w_down)


def _mlp_bwd(d_dn, ru, x2, dx3, o, modr, vecs, w_up, w_down):
    s = x2.shape[0]
    tm = min(TM_MLP_BWD, s)

    def body(ddn_ref, ru_ref, x2_ref, dx3_ref, o_ref, mod_ref, vec_ref, wu_ref, wd_ref,
             dup_ref, dx2_ref, do_ref, small_ref):
        @pl.when(pl.program_id(0) == 0)
        def _():
            small_ref[...] = jnp.zeros_like(small_ref)

        ddn = ddn_ref[...]
        dh2 = None
        for c in range(D_FF // D):
            cs = slice(c * D, (c + 1) * D)
            dff = _dot_nt(ddn, wd_ref[cs, :])
            dup = (dff * (2.0 * ru_ref[:, cs].astype(F32))).astype(BF16)
            dup_ref[:, cs] = dup
            part = _dot_nt(dup, wu_ref[:, cs])
            dh2 = part if dh2 is None else dh2 + part
        x2 = x2_ref[...]
        r2 = lax.rsqrt(jnp.mean(x2 * x2, axis=-1, keepdims=True) + EPS)
        xn2 = x2 * r2
        gain = vec_ref[V_G2:V_G2 + 1, :] * (1.0 + mod_ref[M_SC2:M_SC2 + 1, :])
        dxn2 = dh2 * gain
        dx2 = dx3_ref[...] + r2 * (dxn2 - xn2 * jnp.mean(dxn2 * xn2, axis=-1, keepdims=True))
        dx2_ref[...] = dx2
        do_ref[...] = (dx2 * mod_ref[M_GT1:M_GT1 + 1, :]).astype(BF16)
        small_ref[0:1, :] += jnp.sum(dh2, axis=0, keepdims=True)
        small_ref[1:2, :] += jnp.sum(dh2 * xn2, axis=0, keepdims=True)
        small_ref[2:3, :] += jnp.sum(dx2 * o_ref[...].astype(F32), axis=0, keepdims=True)

    tok = pl.BlockSpec((tm, D), lambda i: (i, 0))
    wide = pl.BlockSpec((tm, D_FF), lambda i: (i, 0))
    return pl.pallas_call(
        body, name="mlp_bwd", grid=(s // tm,),
        in_specs=[tok, wide, tok, tok, tok,
                  pl.BlockSpec((8, D), lambda i: (0, 0)), pl.BlockSpec((16, D), lambda i: (0, 0)),
                  _resident((D, D_FF)), _resident((D_FF, D))],
        out_specs=[wide, tok, tok, pl.BlockSpec((8, D), lambda i: (0, 0))],
        out_shape=[jax.ShapeDtypeStruct((s, D_FF), BF16), jax.ShapeDtypeStruct((s, D), F32),
                   jax.ShapeDtypeStruct((s, D), BF16), jax.ShapeDtypeStruct((8, D), F32)],
        compiler_params=_params(("arbitrary",)),
    )(d_dn, ru, x2, dx3, o, modr, vecs, w_up, w_down)


def _branch_bwd(do, proj, ba, bb, w_a, w_b, w_out, dep):
    s = do.shape[0]
    tm = min(TM_BRANCH, s)

    def body(do_ref, ga_ref, gb_ref, ba_ref, bb_ref, wa_ref, wb_ref, wo_ref, dep_ref,
             dba_ref, dbb_ref, dg_ref, dza_ref, dpooled_ref):
        dmerged = _dot_nt(do_ref[...], wo_ref[...])
        sa = _sigmoid(ga_ref[...])
        sb = _sigmoid(gb_ref[...])
        dba = (dmerged * sa).astype(BF16)
        dbb = (dmerged * sb).astype(BF16)
        dba_ref[...] = dba
        dbb_ref[...] = dbb
        dg_ref[:, :D] = (dmerged * ba_ref[...].astype(F32) * sa * (1.0 - sa)).astype(BF16)
        dg_ref[:, D:] = (dmerged * bb_ref[...].astype(F32) * sb * (1.0 - sb)).astype(BF16)
        dza_ref[...] = _dot_nt(dba, wa_ref[...])
        dpooled_ref[...] = _dot_nt(dbb, wb_ref[...])

    tok = pl.BlockSpec((tm, D), lambda i: (i, 0))
    wspec = pl.BlockSpec((D, D), lambda i: (0, 0))
    sd = lambda dt: jax.ShapeDtypeStruct((s, D), dt)
    return pl.pallas_call(
        body, name="branch_bwd", grid=(s // tm,),
        in_specs=[tok, pl.BlockSpec((tm, D), lambda i: (i, 3)), pl.BlockSpec((tm, D), lambda i: (i, 4)),
                  tok, tok, wspec, wspec, wspec, pl.BlockSpec(memory_space=pl.ANY)],
        out_specs=[tok, tok, pl.BlockSpec((tm, 2 * D), lambda i: (i, 0)), tok, tok],
        out_shape=[sd(BF16), sd(BF16), jax.ShapeDtypeStruct((s, 2 * D), BF16), sd(F32), sd(F32)],
        compiler_params=_params(("parallel",)),
    )(do, proj, proj, ba, bb, w_a, w_b, w_out, dep)


def _mix_bwd(dza, dpooled, proj, xr, hr, p, gates, dgates, vecs, w_rg_a, w_rg_x, w_pool):
    s = xr.shape[0]
    tm = min(TM_MIX, s)
    nb = s // tm

    def body(dza_ref, dpooled_ref, xh_ref, x_ref, y_ref, xr_ref, hh_ref, hr_ref, p_ref,
             a_ref, mult_ref, ra_ref, ri_ref, dg_ref, vec_ref, wa_ref, wx_ref, wp_ref,
             dproj_ref, dwa_ref, dwx_ref, dwp_ref, small_ref,
             scan_carry, dxr_carry, q_carry, scan_scr):
        i = pl.program_id(0)
        bi = nb - 1 - i
        first_t = bi == 0

        @pl.when(i == 0)
        def _():
            scan_carry[...] = jnp.zeros_like(scan_carry)
            dxr_carry[...] = jnp.zeros_like(dxr_carry)
            q_carry[...] = jnp.zeros_like(q_carry)
            dwa_ref[...] = jnp.zeros_like(dwa_ref)
            dwx_ref[...] = jnp.zeros_like(dwx_ref)
            dwp_ref[...] = jnp.zeros_like(dwp_ref)
            small_ref[...] = jnp.zeros_like(small_ref)

        row = lax.broadcasted_iota(jnp.int32, (tm, GW), 0)
        is_t0 = jnp.logical_and(first_t, row == 0)
        t_glob = (row + bi * tm + 1).astype(F32)
        colsum = lambda v: jnp.sum(v, axis=0, keepdims=True)
        for g in range(N_GROUPS):
            cs = slice(g * GW, (g + 1) * GW)
            vec = vec_ref[:, cs]
            xr = xr_ref[:, cs]
            hr = hr_ref[:, cs]
            dza = dza_ref[:, cs]
            ga, dga = _gelu_and_grad(y_ref[:, cs])
            dproj_ref[:, D + g * GW:D + (g + 1) * GW] = (dza * hr * dga).astype(BF16)
            dhr = dza * ga
            a = a_ref[:, cs]
            mult = mult_ref[:, cs]
            ra = ra_ref[:, cs].astype(F32)
            ri = ri_ref[:, cs].astype(F32)
            sp = _softplus(vec[V_A_PARAM:V_A_PARAM + 1])
            m = jnp.where(row == tm - 1, 1.0, _shift_up(a, 1))
            gsum = _scan_up(m, dhr, scan_carry[0:1, cs], scan_scr)
            scan_carry[0:1, cs] = a[0:1, :] * gsum[0:1, :]
            hh = jnp.where(first_t, 0.0, hh_ref[:, cs])
            hprev = _shift_down(jnp.concatenate([hh, hr], axis=0), 1)[8:]
            da = gsum * hprev
            dmult = jnp.where(is_t0, 0.0, gsum * xr * ri)
            dlog_a = da * a - dmult * a * a / mult
            dri = gsum * xr * mult
            dxr = gsum * ri * mult
            small_ref[7:8, cs] += colsum((-C_RG) * ra * dlog_a)
            dpa = (((-C_RG) * sp) * dlog_a * ra * (1.0 - ra))
            dpx = dri * ri * (1.0 - ri)
            small_ref[5:6, cs] += colsum(dpa)
            small_ref[6:7, cs] += colsum(dpx)
            dpa = dpa.astype(BF16)
            dpx = dpx.astype(BF16)
            xrb = xr.astype(BF16)
            dwa_ref[g] += _dot_tn(xrb, dpa)
            dwx_ref[g] += _dot_tn(xrb, dpx)
            dxr = dxr + _dot_nt(dpa, wa_ref[g]) + _dot_nt(dpx, wx_ref[g])
            small_ref[4:5, cs] += colsum(dxr)
            xh = jnp.where(first_t, 0.0, xh_ref[:, cs])
            taps = _conv_taps(jnp.concatenate([xh, x_ref[:, cs]], axis=0))
            dxr_ext = jnp.concatenate([dxr, dxr_carry[:, cs]], axis=0)
            dx = vec[V_CONV_W + 3:V_CONV_W + 4] * dxr
            for j in range(4):
                small_ref[j:j + 1, cs] += colsum(dxr * taps[j])
                if j < 3:
                    dx = dx + vec[V_CONV_W + j:V_CONV_W + j + 1] * _shift_up(dxr_ext, 3 - j)[:tm]
            dxr_carry[:, cs] = dxr[0:8, :]
            dproj_ref[:, cs] = dx.astype(BF16)
            pg = p_ref[:, cs]
            dpooled = dpooled_ref[:, cs]
            pb = _dot(pg, wp_ref[g]) + vec[V_B_POOL:V_B_POOL + 1]
            small_ref[9:10, cs] += colsum(dpooled * pb)
            dpb = dpooled * vec[V_POOL_SCALE:V_POOL_SCALE + 1]
            small_ref[8:9, cs] += colsum(dpb)
            dpbb = dpb.astype(BF16)
            dwp_ref[g] += _dot_tn(pg, dpbb)
            dp = _dot_nt(dpbb, wp_ref[g])
            q = dp / jnp.minimum(t_glob, float(POOL_WINDOWS[g]))
            sm = jnp.concatenate([q, q_carry[:, cs]], axis=0)
            k = 1
            while k < POOL_WINDOWS[g]:
                sm = sm + _shift_up(sm, k)
                k *= 2
            q_carry[:, cs] = q[0:HALO_U, :]
            dproj_ref[:, 2 * D + g * GW:2 * D + (g + 1) * GW] = (sm[:tm] - dp).astype(BF16)
        dproj_ref[:, 3 * D:] = dg_ref[...]

    rev = lambda i: nb - 1 - i
    tok = pl.BlockSpec((tm, D), lambda i: (rev(i), 0))
    col = lambda k: pl.BlockSpec((tm, D), lambda i: (rev(i), k))
    halo8 = lambda k: pl.BlockSpec((8, D), lambda i: (jnp.maximum(rev(i) * (tm // 8) - 1, 0), k))
    wspec = pl.BlockSpec((N_GROUPS, GW, GW), lambda i: (0, 0, 0))
    wshape = jax.ShapeDtypeStruct((N_GROUPS, GW, GW), F32)
    return pl.pallas_call(
        body, name="mix_bwd", grid=(nb,),
        in_specs=[tok, tok, halo8(0), col(0), col(1), tok, halo8(0), tok, tok, tok, tok, tok, tok,
                  pl.BlockSpec((tm, 2 * D), lambda i: (rev(i), 0)),
                  pl.BlockSpec((16, D), lambda i: (0, 0)), wspec, wspec, wspec],
        out_specs=[pl.BlockSpec((tm, D_IN), lambda i: (rev(i), 0)), wspec, wspec, wspec,
                   pl.BlockSpec((16, D), lambda i: (0, 0))],
        out_shape=[jax.ShapeDtypeStruct((s, D_IN), BF16), wshape, wshape, wshape,
                   jax.ShapeDtypeStruct((16, D), F32)],
        scratch_shapes=[pltpu.VMEM((8, D), F32), pltpu.VMEM((8, D), F32), pltpu.VMEM((HALO_U, D), F32),
                        pltpu.VMEM((3, tm, LANES), F32)],
        compiler_params=_params(("arbitrary",)),
    )(dza, dpooled, proj, proj, proj, xr, hr, hr, p, *gates, dgates, vecs, w_rg_a, w_rg_x, w_pool)


def _proj_bwd(dproj, x, dx2, modr, vecs, w_in):
    s = x.shape[0]
    tm = min(TM_PROJ, s)

    def body(dp_ref, x_ref, dx2_ref, mod_ref, vec_ref, w_ref, gx_ref, small_ref):
        @pl.when(pl.program_id(0) == 0)
        def _():
            small_ref[...] = jnp.zeros_like(small_ref)

        dh1 = None
        for c in range(D_IN // D):
            cs = slice(c * D, (c + 1) * D)
            part = _dot_nt(dp_ref[:, cs], w_ref[:, cs])
            dh1 = part if dh1 is None else dh1 + part
        xv = x_ref[...]
        r1 = lax.rsqrt(jnp.mean(xv * xv, axis=-1, keepdims=True) + EPS)
        xn1 = xv * r1
        gain = vec_ref[V_G1:V_G1 + 1, :] * (1.0 + mod_ref[M_SC1:M_SC1 + 1, :])
        dxn1 = dh1 * gain
        gx_ref[...] = dx2_ref[...] + r1 * (dxn1 - xn1 * jnp.mean(dxn1 * xn1, axis=-1, keepdims=True))
        small_ref[0:1, :] += jnp.sum(dh1, axis=0, keepdims=True)
        small_ref[1:2, :] += jnp.sum(dh1 * xn1, axis=0, keepdims=True)

    tok = pl.BlockSpec((tm, D), lambda i: (i, 0))
    return pl.pallas_call(
        body, name="proj_bwd", grid=(s // tm,),
        in_specs=[pl.BlockSpec((tm, D_IN), lambda i: (i, 0)), tok, tok,
                  pl.BlockSpec((8, D), lambda i: (0, 0)), pl.BlockSpec((16, D), lambda i: (0, 0)),
                  _resident((D, D_IN))],
        out_specs=[tok, pl.BlockSpec((8, D), lambda i: (0, 0))],
        out_shape=[jax.ShapeDtypeStruct((s, D), F32), jax.ShapeDtypeStruct((8, D), F32)],
        compiler_params=_params(("arbitrary",)),
    )(dproj, x, dx2, modr, vecs, w_in)


def _wgrad(a, b, name, square_a=False, dep=None):
    s, ka = a.shape
    n = b.shape[1]
    tka = ka if ka <= 1024 else ka // 2
    tn = n if n <= 1024 else n // 2
    ts = min(TS_WGRAD, s)
    ns = s // ts
    nc = 512
    deps = [] if dep is None else [dep]

    def body(a_ref, b_ref, *refs):
        out_ref, acc_ref = refs[-2:]
        t = pl.program_id(2)

        @pl.when(t == 0)
        def _():
            acc_ref[...] = jnp.zeros_like(acc_ref)

        av = a_ref[...]
        if square_a:
            af = av.astype(F32)
            av = (af * af).astype(BF16)
        for c in range(tn // nc):
            cs = slice(c * nc, (c + 1) * nc)
            acc_ref[:, cs] += _dot_tn(av, b_ref[:, cs])

        @pl.when(t == ns - 1)
        def _():
            out_ref[...] = acc_ref[...].astype(BF16)

    return pl.pallas_call(
        body, name=name, grid=(ka // tka, n // tn, ns),
        in_specs=[pl.BlockSpec((ts, tka), lambda i, j, t: (t, i)),
                  pl.BlockSpec((ts, tn), lambda i, j, t: (t, j))] + [pl.BlockSpec(memory_space=pl.ANY)] * len(deps),
        out_specs=pl.BlockSpec((tka, tn), lambda i, j, t: (i, j)),
        out_shape=jax.ShapeDtypeStruct((ka, n), BF16),
        scratch_shapes=[pltpu.VMEM((tka, tn), F32)],
        compiler_params=_params(("parallel", "parallel", "arbitrary")),
    )(a, b, *deps)


def _window(ref, kind, idx, size):
    start = pl.multiple_of(idx * size, size)
    if kind == 0:
        return ref.at[pl.ds(start, size)]
    if kind == 1:
        return ref.at[:, pl.ds(start, size)]
    return ref.at[:, :, pl.ds(start, size)]


def _mesh_place():
    x, y, c = lax.axis_index("x"), lax.axis_index("y"), lax.axis_index("c")
    return x, y, c, 4 * x + 2 * y + c


def _peer(x, y, c, q):
    px = 1 - x if q & 4 else x
    py = 1 - y if q & 2 else y
    pc = 1 - c if q & 1 else c
    return (px, py, pc), 4 * px + 2 * py + pc


def _all_gather(shards, kinds, name, dep=None):
    n = len(shards)
    deps = [] if dep is None else [dep]
    full_shapes = []
    for sh, kind in zip(shards, kinds):
        dims = list(sh.shape)
        dims[kind] *= N_DEV
        full_shapes.append(jax.ShapeDtypeStruct(tuple(dims), sh.dtype))

    def body(*refs):
        ins, outs = refs[:n], refs[n + len(deps):2 * n + len(deps)]
        send_sems, recv_sems, local_sems = refs[2 * n + len(deps):]
        x, y, c, me = _mesh_place()
        sends, recvs, locals_ = [], [], []
        for k in range(n):
            size = shards[k].shape[kinds[k]]
            mine = _window(outs[k], kinds[k], me, size)
            lc = pltpu.make_async_copy(ins[k], mine, local_sems.at[k])
            lc.start()
            locals_.append(lc)
            for q in range(1, N_DEV):
                peer, peer_idx = _peer(x, y, c, q)
                cp = pltpu.make_async_remote_copy(
                    src_ref=ins[k], dst_ref=mine, send_sem=send_sems.at[k, q], recv_sem=recv_sems.at[k, q],
                    device_id=peer, device_id_type=MESH)
                cp.start()
                sends.append(cp)
                recvs.append(pltpu.make_async_remote_copy(
                    src_ref=ins[k], dst_ref=_window(outs[k], kinds[k], peer_idx, size),
                    send_sem=send_sems.at[k, q], recv_sem=recv_sems.at[k, q],
                    device_id=peer, device_id_type=MESH))
        for cp in recvs:
            cp.wait_recv()
        for cp in sends:
            cp.wait_send()
        for lc in locals_:
            lc.wait()

    any_spec = pl.BlockSpec(memory_space=pl.ANY)
    return pl.pallas_call(
        body, name=name,
        in_specs=[any_spec] * (n + len(deps)), out_specs=[any_spec] * n, out_shape=full_shapes,
        scratch_shapes=[pltpu.SemaphoreType.DMA((n, N_DEV)), pltpu.SemaphoreType.DMA((n, N_DEV)),
                        pltpu.SemaphoreType.DMA((n,))],
    )(*shards, *deps)


_HBM = pl.BlockSpec(memory_space=pltpu.HBM)
_SEM = pl.BlockSpec(memory_space=pltpu.SEMAPHORE)
_EFFECT = pltpu.SideEffectType.DATAFLOW_SIDE_EFFECTING


def _exchange_copies(gather, kinds, src_refs, land_refs, send_sems, recv_sems):
    x, y, c, me = _mesh_place()
    sends, recvs = [], []
    for k in range(len(kinds)):
        for q in range(1, N_DEV):
            peer, peer_idx = _peer(x, y, c, q)
            if gather:
                size = src_refs[k].shape[kinds[k]]
                src = src_refs[k]
                dst = _window(land_refs[k], kinds[k], me, size)
                arriving = _window(land_refs[k], kinds[k], peer_idx, size)
            else:
                size = src_refs[k].shape[kinds[k]] // N_DEV
                src = _window(src_refs[k], kinds[k], peer_idx, size)
                dst = land_refs[k].at[me]
                arriving = land_refs[k].at[peer_idx]
            sems = dict(send_sem=send_sems.at[k * N_DEV + q], recv_sem=recv_sems.at[k * N_DEV + q],
                        device_id=peer, device_id_type=MESH)
            sends.append(pltpu.make_async_remote_copy(src_ref=src, dst_ref=dst, **sems))
            recvs.append(pltpu.make_async_remote_copy(src_ref=src, dst_ref=arriving, **sems))
    return sends, recvs


def _exchange_start(gather, srcs, lands, kinds, after, name):
    n = len(srcs)

    def body(*refs):
        src_refs, land_refs = refs[:n], refs[n:2 * n]
        send_sems, recv_sems = refs[2 * n + 1], refs[2 * n + 2]
        token = refs[-1]
        sends, _ = _exchange_copies(gather, kinds, src_refs, land_refs, send_sems, recv_sems)
        for cp in sends:
            cp.start()
        token[...] = jnp.zeros_like(token)

    hbm = lambda a: pltpu.HBM(a.shape, a.dtype)
    outs = pl.pallas_call(
        body, name=name,
        out_shape=(pltpu.SemaphoreType.DMA((n * N_DEV,)), pltpu.SemaphoreType.DMA((n * N_DEV,)),
                   *[hbm(a) for a in srcs], *[hbm(a) for a in lands], jax.ShapeDtypeStruct((8, 128), F32)),
        in_specs=[_HBM] * (2 * n) + [pl.BlockSpec(memory_space=pl.ANY)],
        out_specs=(_SEM, _SEM, *[_HBM] * (2 * n), pl.BlockSpec(memory_space=pltpu.VMEM)),
        input_output_aliases={i: 2 + i for i in range(2 * n)},
        compiler_params=pltpu.CompilerParams(has_side_effects=_EFFECT),
    )(*[pltpu.with_memory_space_constraint(a, pltpu.HBM) for a in (*srcs, *lands)], after)
    return outs[0], outs[1], outs[2:2 + n], outs[2 + n:2 + 2 * n], outs[-1]


def _exchange_wait(gather, started, kinds, after, name):
    send_sems, recv_sems, srcs, lands, _ = started
    n = len(srcs)

    def body(*refs):
        src_refs, land_refs = refs[:n], refs[n:2 * n]
        sends, recvs = _exchange_copies(gather, kinds, src_refs, land_refs, refs[2 * n], refs[2 * n + 1])
        for cp in sends:
            cp.wait_send()
        for cp in recvs:
            cp.wait_recv()

    hbm = lambda a: pltpu.HBM(a.shape, a.dtype)
    outs = pl.pallas_call(
        body, name=name,
        out_shape=(*[hbm(a) for a in srcs], *[hbm(a) for a in lands]),
        in_specs=[_HBM] * (2 * n) + [_SEM, _SEM, pl.BlockSpec(memory_space=pl.ANY)],
        out_specs=tuple([_HBM] * (2 * n)),
        input_output_aliases={i: i for i in range(2 * n)},
        compiler_params=pltpu.CompilerParams(has_side_effects=_EFFECT),
    )(*srcs, *lands, send_sems, recv_sems, after)
    return outs[n:]


def _after(small, token):
    return small + token[0:1, 0:1].astype(small.dtype)


def _own_window(kind, shard, me):
    dims = list(shard.shape)
    dims[kind] *= N_DEV
    start = [0] * len(dims)
    start[kind] = me * shard.shape[kind]
    return lax.dynamic_update_slice(lax.empty(tuple(dims), shard.dtype), shard, tuple(start))


def _own_slot(kind, full, me):
    size = full.shape[kind] // N_DEV
    mine = lax.dynamic_slice_in_dim(full, me * size, size, axis=kind)
    return lax.dynamic_update_index_in_dim(lax.empty((N_DEV, *mine.shape), full.dtype), mine, me, 0)


def _silu(c):
    return c * _sigmoid_tail(c)


def _ada_fwd(c_all, w_ada, b_ada_cols):
    def body(c_ref, w_ref, b_ref, out_ref):
        out_ref[...] = jnp.dot(_silu(c_ref[...]), w_ref[...], preferred_element_type=F32,
                               precision=lax.Precision.HIGHEST) + b_ref[...]

    return pl.pallas_call(
        body, name="ada_fwd", out_shape=jax.ShapeDtypeStruct((N_DEV, w_ada.shape[1]), F32),
    )(c_all, w_ada, b_ada_cols)


def _adam(w, g, m, v):
    m = ADAM_B1 * m + (1.0 - ADAM_B1) * g
    v = ADAM_B2 * v + (1.0 - ADAM_B2) * (g * g)
    m_hat = m / (1.0 - ADAM_B1 ** ADAM_STEP)
    v_hat = v / (1.0 - ADAM_B2 ** ADAM_STEP)
    delta = -ADAM_LR * (m_hat / (jnp.sqrt(v_hat) + ADAM_EPS) + ADAM_WD * w)
    return delta, m, v


def _ada_bwd_adam(c_all, dmod_cols, w, m, v):
    def body(c_ref, d_ref, w_ref, m_ref, v_ref, g_ref, delta_ref, nm_ref, nv_ref):
        g = lax.dot_general(_silu(c_ref[...]), d_ref[...], (((0,), (0,)), ((), ())),
                            preferred_element_type=F32, precision=lax.Precision.HIGHEST)
        g_ref[...] = g
        delta_ref[...], nm_ref[...], nv_ref[...] = _adam(w_ref[...], g, m_ref[...], v_ref[...])

    sd = jax.ShapeDtypeStruct(w.shape, F32)
    return pl.pallas_call(body, name="ada_bwd_adam", out_shape=[sd] * 4,
                          compiler_params=pltpu.CompilerParams(vmem_limit_bytes=V7X_VMEM_LIMIT),
                          )(c_all, dmod_cols, w, m, v)


def _sum_slots_adam(slots, w, m, v, name):
    r, cdim = w.shape
    tr = min(r, 128)

    def body(s_ref, w_ref, m_ref, v_ref, g_ref, delta_ref, nm_ref, nv_ref):
        g = s_ref[0].astype(F32)
        for p in range(1, N_DEV):
            g = g + s_ref[p].astype(F32)
        g_ref[...] = g
        delta_ref[...], nm_ref[...], nv_ref[...] = _adam(w_ref[...], g, m_ref[...], v_ref[...])

    blk = pl.BlockSpec((tr, cdim), lambda i: (i, 0))
    sd = jax.ShapeDtypeStruct((r, cdim), F32)
    return pl.pallas_call(
        body, name=name, grid=(r // tr,),
        in_specs=[pl.BlockSpec((N_DEV, tr, cdim), lambda i: (0, i, 0)), blk, blk, blk],
        out_specs=[blk] * 4, out_shape=[sd] * 4,
        compiler_params=_params(("parallel",)),
    )(slots, w, m, v)


N_SMALL = 40
N_SMALL_PARAMS = 11


def _pack_vecs(conv_w_full, rows):
    def body(cw_ref, *refs):
        out = refs[-1]
        out[...] = jnp.zeros_like(out)
        out[0:4, :] = cw_ref[0:4, :]
        for r, ref in enumerate(refs[:-1]):
            out[4 + r:5 + r, :] = ref[...]

    return pl.pallas_call(body, name="pack_vecs", out_shape=jax.ShapeDtypeStruct((16, D), F32))(conv_w_full, *rows)


def _small_finish(gathered, mod_all, vecs, ws, ms, vs):
    n = N_SMALL_PARAMS

    def body(g_ref, mod_ref, vec_ref, *refs):
        w_refs, m_refs, v_refs = refs[:n], refs[n:2 * n], refs[2 * n:3 * n]
        outs = refs[3 * n:]
        g1 = vec_ref[V_G1:V_G1 + 1, :]
        g2 = vec_ref[V_G2:V_G2 + 1, :]
        zero = jnp.zeros((1, D), F32)
        dg1, dg2, dgf, loss_lanes = zero, zero, zero, zero
        mixer = jnp.zeros((16, D), F32)
        db_ada = jnp.zeros((6, D), F32)
        for b in range(N_DEV):
            gb = g_ref[b]
            mod = mod_ref[b]
            q1 = gb[33:34]
            q2 = gb[9:10]
            dmod = jnp.concatenate([gb[32:33], q1 * g1, gb[10:11], gb[8:9], q2 * g2, gb[1:2]], axis=0)
            outs[4 * n][b] = dmod
            db_ada = db_ada + dmod
            dg1 = dg1 + q1 * (1.0 + mod[M_SC1:M_SC1 + 1])
            dg2 = dg2 + q2 * (1.0 + mod[M_SC2:M_SC2 + 1])
            dgf = dgf + gb[0:1]
            loss_lanes = loss_lanes + gb[2:3]
            mixer = mixer + gb[16:32]
        d_a_param = mixer[7:8] * _sigmoid_tail(vec_ref[V_A_PARAM:V_A_PARAM + 1, :])
        grads = [dg1, dg2, mixer[4:5], mixer[5:6], mixer[6:7], d_a_param, mixer[8:9], mixer[9:10], dgf,
                 db_ada, mixer[0:4]]
        for k in range(n):
            outs[k][...] = grads[k]
            outs[n + k][...], outs[2 * n + k][...], outs[3 * n + k][...] = _adam(
                w_refs[k][...], grads[k], m_refs[k][...], v_refs[k][...])
        outs[4 * n + 1][...] = jnp.broadcast_to(jnp.sum(loss_lanes, axis=1, keepdims=True), (8, 128))

    shapes = [jax.ShapeDtypeStruct(w.shape, F32) for w in ws]
    return pl.pallas_call(
        body, name="small_finish",
        out_shape=shapes * 4 + [jax.ShapeDtypeStruct((N_DEV, 6, D), F32), jax.ShapeDtypeStruct((8, 128), F32)],
    )(gathered, mod_all, vecs, *ws, *ms, *vs)


def _pad_rows(a, rows):
    return jnp.pad(a, ((0, rows - a.shape[0]), (0, 0)))


def kernel(x, c, norm_mix_g, norm_mlp_g, w_ada, b_ada, w_in, conv_w, conv_b, w_rg_a, b_rg_a, w_rg_x, b_rg_x, a_param, w_branch_a, w_pool, b_pool, pool_scale, w_branch_b, w_out, w_up, w_down, final_g, loss_target, m_norm_mix_g, m_norm_mlp_g, m_w_ada, m_b_ada, m_w_in, m_conv_w, m_conv_b, m_w_rg_a, m_b_rg_a, m_w_rg_x, m_b_rg_x, m_a_param, m_w_branch_a, m_w_pool, m_b_pool, m_pool_scale, m_w_branch_b, m_w_out, m_w_up, m_w_down, m_final_g, v_norm_mix_g, v_norm_mlp_g, v_w_ada, v_b_ada, v_w_in, v_conv_w, v_conv_b, v_w_rg_a, v_b_rg_a, v_w_rg_x, v_b_rg_x, v_a_param, v_w_branch_a, v_w_pool, v_b_pool, v_pool_scale, v_w_branch_b, v_w_out, v_w_up, v_w_down, v_final_g):
    me = 4 * lax.axis_index("x") + 2 * lax.axis_index("y") + lax.axis_index("c")
    s = x.shape[1]
    x2d = x.reshape(s, D)
    target = loss_target.reshape(s, D)
    n_ada = w_ada.shape[2]

    sharded = dict(w_in=(w_in[0], 1), w_up=(w_up[0], 1), w_down=(w_down[0], 0), w_branch_a=(w_branch_a[0], 0),
                   w_branch_b=(w_branch_b[0], 0), w_out=(w_out[0], 0), w_rg_a=(w_rg_a[0], 1), w_rg_x=(w_rg_x[0], 1),
                   w_pool=(w_pool[0], 1))
    kind = {k: v[1] for k, v in sharded.items()}
    shard = {k: v[0].astype(BF16) for k, v in sharded.items()}

    w_in_full, conv_w_full, c_rows = _all_gather([shard["w_in"], _pad_rows(conv_w[0], 8), _pad_rows(c, 8)],
                                                 [1, 1, 0], "gather_first")
    c_all = c_rows.reshape(N_DEV, 8, D)[:, 0, :]

    b_ada_cols = lax.dynamic_slice(b_ada, (0, me * n_ada), (1, n_ada))
    mod_part = _ada_fwd(c_all, w_ada[0], b_ada_cols)
    mod_parts, = _all_gather([mod_part], [0], "gather_mod")
    mod_all = jnp.transpose(mod_parts.reshape(N_DEV, N_DEV, n_ada), (1, 0, 2)).reshape(N_DEV, 6, D)
    mod_all = jnp.pad(mod_all, ((0, 0), (0, 2), (0, 0)))
    modr = lax.dynamic_index_in_dim(mod_all, me, 0, keepdims=False)
    vecs = _pack_vecs(conv_w_full, [conv_b, b_rg_a, b_rg_x, a_param, b_pool, pool_scale,
                                    norm_mix_g, norm_mlp_g, final_g.reshape(1, D)])

    mixer_names = ["w_rg_a", "w_rg_x", "w_pool", "w_branch_a", "w_branch_b", "w_out"]
    mlp_names = ["w_up", "w_down"]

    def start_gather(group, after, name):
        return _exchange_start(True, [shard[k] for k in group], [_own_window(kind[k], shard[k], me) for k in group],
                               [kind[k] for k in group], after, name)

    g_mixer = start_gather(mixer_names, modr, "gather_mixer_start")
    g_mlp = start_gather(mlp_names, g_mixer[-1], "gather_mlp_start")

    proj, h1 = _proj_fwd(x2d, _after(modr, g_mlp[-1]), vecs, w_in_full)
    wg = dict(zip(mixer_names, _exchange_wait(True, g_mixer, [kind[k] for k in mixer_names], h1, "gather_mixer_wait")))
    xr, hr, za, p, pooled, *gates = _mix_fwd(proj, vecs, wg["w_rg_a"], wg["w_rg_x"], wg["w_pool"])
    ba, bb, merged, o, x2, h2 = _branch_fwd(za, pooled, proj, x2d, modr, vecs,
                                            wg["w_branch_a"], wg["w_branch_b"], wg["w_out"])
    wg.update(zip(mlp_names, _exchange_wait(True, g_mlp, [kind[k] for k in mlp_names], h2, "gather_mlp_wait")))
    ru, dx3, d_dn, small_f = _mlp_fwd(h2, x2, target, modr, vecs, wg["w_up"], wg["w_down"])

    def start_scatter(group, partial, after, name):
        return _exchange_start(False, [partial[k] for k in group], [_own_slot(kind[k], partial[k], me) for k in group],
                               [kind[k] for k in group], after, name)

    dup, dx2, do, small_m = _mlp_bwd(d_dn, ru, x2, dx3, o, modr, vecs, wg["w_up"], wg["w_down"])
    partial = dict(w_up=_wgrad(h2, dup, "wgrad_up"), w_down=_wgrad(ru, d_dn, "wgrad_down", square_a=True))
    s_mlp = start_scatter(mlp_names, partial, dx2, "scatter_mlp_start")

    dba, dbb, dgates, dza, dpooled = _branch_bwd(do, proj, ba, bb, wg["w_branch_a"], wg["w_branch_b"], wg["w_out"],
                                                 dep=s_mlp[-1])
    dproj, dw_rg_a, dw_rg_x, dw_pool, small_x = _mix_bwd(dza, dpooled, proj, xr, hr, p, gates, dgates, vecs,
                                                         wg["w_rg_a"], wg["w_rg_x"], wg["w_pool"])
    partial.update(w_branch_a=_wgrad(za, dba, "wgrad_branch_a"), w_branch_b=_wgrad(pooled, dbb, "wgrad_branch_b"),
                   w_out=_wgrad(merged, do, "wgrad_out"),
                   w_rg_a=dw_rg_a.astype(BF16), w_rg_x=dw_rg_x.astype(BF16), w_pool=dw_pool.astype(BF16))
    s_mixer = start_scatter(mixer_names, partial, s_mlp[-1], "scatter_mixer_start")

    partial["w_in"] = _wgrad(h1, dproj, "wgrad_in", dep=s_mixer[-1])
    s_in = start_scatter(["w_in"], partial, s_mixer[-1], "scatter_in_start")
    grad_x, small_p = _proj_bwd(dproj, x2d, dx2, _after(modr, s_in[-1]), vecs, w_in_full)

    locals_ = dict(w_in=(w_in, m_w_in, v_w_in), w_up=(w_up, m_w_up, v_w_up), w_down=(w_down, m_w_down, v_w_down),
                   w_branch_a=(w_branch_a, m_w_branch_a, v_w_branch_a),
                   w_branch_b=(w_branch_b, m_w_branch_b, v_w_branch_b), w_out=(w_out, m_w_out, v_w_out),
                   w_rg_a=(w_rg_a, m_w_rg_a, v_w_rg_a), w_rg_x=(w_rg_x, m_w_rg_x, v_w_rg_x),
                   w_pool=(w_pool, m_w_pool, v_w_pool))
    res = {}

    def finish(group, started, after, name):
        slots = _exchange_wait(False, started, [kind[k] for k in group], after, name)
        for k, sl in zip(group, slots):
            w, m, v = locals_[k]
            shape2d = (-1, w.shape[-1])
            sl = sl.reshape(N_DEV, *w.reshape(shape2d).shape)
            outs = _sum_slots_adam(sl, w.reshape(shape2d), m.reshape(shape2d), v.reshape(shape2d), "adam_" + k)
            res[k] = [t.reshape(w.shape) for t in outs]
        return res[group[-1]][0]

    done = finish(mlp_names, s_mlp, grad_x, "scatter_mlp_wait")
    done = finish(mixer_names, s_mixer, done, "scatter_mixer_wait")
    done = finish(["w_in"], s_in, done, "scatter_in_wait")

    small = jnp.concatenate([small_f, small_m, small_x, small_p], axis=0)
    small_all, = _all_gather([small], [0], "gather_small", dep=done)
    small_all = small_all.reshape(N_DEV, N_SMALL, D)

    def embed(cw):
        return lax.dynamic_update_slice(jnp.zeros((4, D), F32), cw[0], (0, me * (D // N_DEV)))

    def smalls(ng, nl, cb, bra, brx, ap, bp, ps, fg, ba_, cw):
        return [ng, nl, cb, bra, brx, ap, bp, ps, fg.reshape(1, D), ba_.reshape(6, D), embed(cw)]

    small_names = ["norm_mix_g", "norm_mlp_g", "conv_b", "b_rg_a", "b_rg_x", "a_param", "b_pool", "pool_scale",
                   "final_g", "b_ada", "conv_w"]
    fin = _small_finish(
        small_all, mod_all, vecs,
        smalls(norm_mix_g, norm_mlp_g, conv_b, b_rg_a, b_rg_x, a_param, b_pool, pool_scale, final_g, b_ada, conv_w),
        smalls(m_norm_mix_g, m_norm_mlp_g, m_conv_b, m_b_rg_a, m_b_rg_x, m_a_param, m_b_pool, m_pool_scale,
               m_final_g, m_b_ada, m_conv_w),
        smalls(v_norm_mix_g, v_norm_mlp_g, v_conv_b, v_b_rg_a, v_b_rg_x, v_a_param, v_b_pool, v_pool_scale,
               v_final_g, v_b_ada, v_conv_w))
    dmod_all, loss_tile = fin[4 * N_SMALL_PARAMS], fin[4 * N_SMALL_PARAMS + 1]
    dmod_cols = lax.dynamic_slice(dmod_all.reshape(N_DEV, 6 * D), (0, me * n_ada), (N_DEV, n_ada))
    res["w_ada"] = [t.reshape(w_ada.shape) for t in _ada_bwd_adam(c_all, dmod_cols, w_ada[0], m_w_ada[0], v_w_ada[0])]

    def final_shape(k, t):
        if k == "final_g":
            return t.reshape(D)
        if k == "b_ada":
            return t.reshape(1, 6 * D)
        if k == "conv_w":
            return lax.dynamic_slice(t, (0, me * (D // N_DEV)), (4, D // N_DEV)).reshape(conv_w.shape)
        return t

    for i, k in enumerate(small_names):
        res[k] = [final_shape(k, fin[which * N_SMALL_PARAMS + i]) for which in range(4)]
    order = ["norm_mix_g", "norm_mlp_g", "w_ada", "b_ada", "w_in", "conv_w", "conv_b", "w_rg_a", "b_rg_a", "w_rg_x",
             "b_rg_x", "a_param", "w_branch_a", "w_pool", "b_pool", "pool_scale", "w_branch_b", "w_out", "w_up",
             "w_down", "final_g"]
    outs = [loss_tile[0, 0], grad_x.reshape(x.shape)]
    for which in range(4):
        for k in order:
            outs.append(res[k][which])
    return tuple(outs)
```

```python
import functools

import jax
import jax.numpy as jnp
from jax import lax
from jax.experimental import pallas as pl
from jax.experimental.pallas import tpu as pltpu

F32 = jnp.float32
BF16 = jnp.bfloat16
MESH = pl.DeviceIdType.MESH

N_DEV = 8
D = 1024
N_GROUPS = 4
GW = D // N_GROUPS
D_IN = 5 * D
D_FF = 4 * D
POOL_WINDOWS = (2, 4, 8, 16)
HALO_X = 8
HALO_U = 16
EPS = 1e-6
C_RG = 8.0
ADAM_LR, ADAM_B1, ADAM_B2, ADAM_EPS, ADAM_WD, ADAM_STEP = 0.001, 0.9, 0.999, 1e-08, 0.01, 10

V7X_VMEM_LIMIT = 56 * 1024 * 1024

V_CONV_W, V_CONV_B, V_B_RG_A, V_B_RG_X, V_A_PARAM, V_B_POOL, V_POOL_SCALE, V_G1, V_G2, V_GF = 0, 4, 5, 6, 7, 8, 9, 10, 11, 12
M_SH1, M_SC1, M_GT1, M_SH2, M_SC2, M_GT2 = 0, 1, 2, 3, 4, 5

TM_PROJ = 512
TM_MIX = 256
TM_BRANCH = 256
TM_MLP = 512
TM_MLP_BWD = 256
TS_WGRAD = 1024


def _params(semantics):
    return pltpu.CompilerParams(dimension_semantics=semantics, vmem_limit_bytes=V7X_VMEM_LIMIT)


def _resident(shape):
    return pl.BlockSpec(shape, lambda *_: (0,) * len(shape), pipeline_mode=pl.Buffered(1))


def _dot(a, b):
    return jnp.dot(a, b, preferred_element_type=F32)


def _dot_nt(a, b):
    return lax.dot_general(a, b, (((1,), (1,)), ((), ())), preferred_element_type=F32)


def _dot_tn(a, b):
    return lax.dot_general(a, b, (((0,), (0,)), ((), ())), preferred_element_type=F32)


def _sigmoid(x):
    return 0.5 * jnp.tanh(0.5 * x) + 0.5


def _sigmoid_tail(x):
    return 1.0 / (1.0 + jnp.exp(-x))


def _gelu_and_grad(x):
    k = 0.7978845608028654
    x2 = x * x
    t = jnp.tanh(k * (x + 0.044715 * x * x2))
    g = 0.5 * x * (1.0 + t)
    dg = 0.5 * (1.0 + t) + 0.5 * x * (1.0 - t * t) * (k * (1.0 + 3.0 * 0.044715 * x2))
    return g, dg


def _softplus(a):
    e = jnp.exp(-jnp.abs(a))
    u = 1.0 + e
    log1p_e = jnp.where(u == 1.0, e, jnp.log(u) * e / jnp.where(u == 1.0, 1.0, u - 1.0))
    return jnp.maximum(a, 0.0) + log1p_e


def _neg_expm1(z):
    series = -(z * (1.0 + z * (0.5 + z * (1.0 / 6.0 + z * (1.0 / 24.0 + z * (1.0 / 120.0))))))
    return jnp.where(z > -0.1, series, 1.0 - jnp.exp(z))


def _shift_down(x, k):
    return pltpu.roll(x, k, 0)


def _shift_up(x, k):
    return pltpu.roll(x, x.shape[0] - k, 0)


def _rglru_gates(xr, w_a, w_x, b_a, b_x, a_param, is_t0):
    xb = xr.astype(BF16)
    ra = _sigmoid(_dot(xb, w_a) + b_a)
    ri = _sigmoid(_dot(xb, w_x) + b_x)
    sp = _softplus(a_param)
    log_a = (-C_RG) * ra * sp
    a = jnp.exp(log_a)
    mult = jnp.where(is_t0, 1.0, jnp.sqrt(_neg_expm1(2.0 * log_a)))
    return ra, ri, sp, a, mult


SUBLANES = 8


LANES = 128


def _scan_strip(a, b, carry, scr, down):
    t = b.shape[0]
    g = t // SUBLANES
    a3 = a.reshape(g, SUBLANES, LANES)
    b3 = b.reshape(g, SUBLANES, LANES)
    sub = lax.broadcasted_iota(jnp.int32, (g, SUBLANES, LANES), 1)
    for k in (1, 2, 4):
        keep = sub >= k if down else sub < SUBLANES - k
        shift = k if down else SUBLANES - k
        b3 = b3 + a3 * jnp.where(keep, pltpu.roll(b3, shift, 1), 0.0)
        a3 = a3 * jnp.where(keep, pltpu.roll(a3, shift, 1), 1.0)
    scr[0] = a3.reshape(t, LANES)
    scr[1] = b3.reshape(t, LANES)
    end_row = SUBLANES - 1 if down else 0
    ag = scr[0, pl.ds(end_row, g, stride=SUBLANES), :]
    bg = scr[1, pl.ds(end_row, g, stride=SUBLANES), :]
    rg = lax.broadcasted_iota(jnp.int32, (g, LANES), 0)
    edge = 0 if down else g - 1
    bg = bg + jnp.where(rg == edge, ag * carry, 0.0)
    k = 1
    while k < g:
        keep = rg >= k if down else rg < g - k
        shift = k if down else g - k
        bg = bg + ag * jnp.where(keep, pltpu.roll(bg, shift, 0), 0.0)
        if 2 * k < g:
            ag = ag * pltpu.roll(ag, shift, 0)
        k *= 2
    entering = jnp.where(rg != edge, pltpu.roll(bg, 1 if down else g - 1, 0), carry)
    for r in range(SUBLANES):
        scr[2, pl.ds(r, g, stride=SUBLANES), :] = entering
    return scr[1] + scr[0] * scr[2], bg[g - 1:g, :]


def _scan_strips(a, b, carry, scr, down):
    outs = [_scan_strip(a[:, c:c + LANES], b[:, c:c + LANES], carry[:, c:c + LANES], scr, down)
            for c in range(0, b.shape[1], LANES)]
    return jnp.concatenate([o[0] for o in outs], axis=1), jnp.concatenate([o[1] for o in outs], axis=1)


def _scan_down(a, b, carry, scr):
    return _scan_strips(a, b, carry, scr, True)


def _scan_up(m, b, carry, scr):
    return _scan_strips(m, b, carry, scr, False)[0]


def _conv_taps(x_ext):
    return [_shift_down(x_ext, 3 - j)[HALO_X:] if j < 3 else x_ext[HALO_X:] for j in range(4)]


def _proj_fwd(x, modr, vecs, w_in):
    s = x.shape[0]
    tm = min(TM_PROJ, s)

    def body(x_ref, mod_ref, vec_ref, w_ref, proj_ref, h1_ref):
        xv = x_ref[...]
        r = lax.rsqrt(jnp.mean(xv * xv, axis=-1, keepdims=True) + EPS)
        gain = vec_ref[V_G1:V_G1 + 1, :] * (1.0 + mod_ref[M_SC1:M_SC1 + 1, :])
        h = (xv * r * gain + mod_ref[M_SH1:M_SH1 + 1, :]).astype(BF16)
        h1_ref[...] = h
        for c in range(D_IN // D):
            proj_ref[:, c * D:(c + 1) * D] = _dot(h, w_ref[:, c * D:(c + 1) * D])

    return pl.pallas_call(
        body, name="proj_fwd", grid=(s // tm,),
        in_specs=[pl.BlockSpec((tm, D), lambda i: (i, 0)),
                  pl.BlockSpec((8, D), lambda i: (0, 0)),
                  pl.BlockSpec((16, D), lambda i: (0, 0)),
                  _resident((D, D_IN))],
        out_specs=[pl.BlockSpec((tm, D_IN), lambda i: (i, 0)),
                   pl.BlockSpec((tm, D), lambda i: (i, 0))],
        out_shape=[jax.ShapeDtypeStruct((s, D_IN), F32), jax.ShapeDtypeStruct((s, D), BF16)],
        compiler_params=_params(("parallel",)),
    )(x, modr, vecs, w_in)


def _mix_fwd(proj, vecs, w_rg_a, w_rg_x, w_pool):
    s = proj.shape[0]
    tm = min(TM_MIX, s)
    nb = s // tm

    def body(xh_ref, x_ref, y_ref, uh_ref, u_ref, vec_ref, wa_ref, wx_ref, wp_ref,
             xr_ref, hr_ref, za_ref, p_ref, pooled_ref, a_ref, mult_ref, ra_ref, ri_ref, carry_ref, scan_scr):
        i = pl.program_id(0)
        first = i == 0

        @pl.when(first)
        def _():
            carry_ref[...] = jnp.zeros_like(carry_ref)

        row = lax.broadcasted_iota(jnp.int32, (tm, GW), 0)
        is_t0 = jnp.logical_and(first, row == 0)
        t_glob = (row + i * tm + 1).astype(F32)
        for g in range(N_GROUPS):
            cs = slice(g * GW, (g + 1) * GW)
            vec = vec_ref[:, cs]
            xh = jnp.where(first, 0.0, xh_ref[:, cs])
            taps = _conv_taps(jnp.concatenate([xh, x_ref[:, cs]], axis=0))
            xr = vec[V_CONV_B:V_CONV_B + 1]
            for j in range(4):
                xr = xr + vec[V_CONV_W + j:V_CONV_W + j + 1] * taps[j]
            xr_ref[:, cs] = xr
            ra, ri, _, a, mult = _rglru_gates(
                xr, wa_ref[g], wx_ref[g], vec[V_B_RG_A:V_B_RG_A + 1], vec[V_B_RG_X:V_B_RG_X + 1],
                vec[V_A_PARAM:V_A_PARAM + 1], is_t0)
            a_ref[:, cs] = a
            mult_ref[:, cs] = mult
            ra_ref[:, cs] = ra.astype(BF16)
            ri_ref[:, cs] = ri.astype(BF16)
            h, last = _scan_down(a, xr * ri * mult, carry_ref[0:1, cs], scan_scr)
            hr_ref[:, cs] = h
            carry_ref[0:1, cs] = last
            ga, _ = _gelu_and_grad(y_ref[:, cs])
            za_ref[:, cs] = (ga * h).astype(BF16)
            uh = jnp.where(first, 0.0, uh_ref[:, cs])
            sm = jnp.concatenate([uh, u_ref[:, cs]], axis=0)
            k = 1
            while k < POOL_WINDOWS[g]:
                sm = sm + _shift_down(sm, k)
                k *= 2
            cnt = jnp.minimum(t_glob, float(POOL_WINDOWS[g]))
            p = (sm[HALO_U:] / cnt - u_ref[:, cs]).astype(BF16)
            p_ref[:, cs] = p
            pb = _dot(p, wp_ref[g]) + vec[V_B_POOL:V_B_POOL + 1]
            pooled_ref[:, cs] = (pb * vec[V_POOL_SCALE:V_POOL_SCALE + 1]).astype(BF16)

    col = lambda k: (lambda i: (i, k))
    wspec = pl.BlockSpec((N_GROUPS, GW, GW), lambda i: (0, 0, 0))
    return pl.pallas_call(
        body, name="mix_fwd", grid=(nb,),
        in_specs=[pl.BlockSpec((HALO_X, D), lambda i: (jnp.maximum(i * (tm // HALO_X) - 1, 0), 0)),
                  pl.BlockSpec((tm, D), col(0)),
                  pl.BlockSpec((tm, D), col(1)),
                  pl.BlockSpec((HALO_U, D), lambda i: (jnp.maximum(i * (tm // HALO_U) - 1, 0), 2)),
                  pl.BlockSpec((tm, D), col(2)),
                  pl.BlockSpec((16, D), lambda i: (0, 0)),
                  wspec, wspec, wspec],
        out_specs=[pl.BlockSpec((tm, D), lambda i: (i, 0))] * 9,
        out_shape=[jax.ShapeDtypeStruct((s, D), F32), jax.ShapeDtypeStruct((s, D), F32),
                   jax.ShapeDtypeStruct((s, D), BF16), jax.ShapeDtypeStruct((s, D), BF16),
                   jax.ShapeDtypeStruct((s, D), BF16),
                   jax.ShapeDtypeStruct((s, D), F32), jax.ShapeDtypeStruct((s, D), F32),
                   jax.ShapeDtypeStruct((s, D), BF16), jax.ShapeDtypeStruct((s, D), BF16)],
        scratch_shapes=[pltpu.VMEM((8, D), F32), pltpu.VMEM((3, tm, LANES), F32)],
        compiler_params=_params(("arbitrary",)),
    )(proj, proj, proj, proj, proj, vecs, w_rg_a, w_rg_x, w_pool)


def _branch_fwd(za, pooled, proj, x, modr, vecs, w_a, w_b, w_out):
    s = x.shape[0]
    tm = min(TM_BRANCH, s)

    def body(za_ref, pooled_ref, ga_ref, gb_ref, x_ref, mod_ref, vec_ref, wa_ref, wb_ref, wo_ref,
             ba_ref, bb_ref, merged_ref, o_ref, x2_ref, h2_ref):
        ba = _dot(za_ref[...], wa_ref[...])
        bb = _dot(pooled_ref[...], wb_ref[...])
        ba_ref[...] = ba.astype(BF16)
        bb_ref[...] = bb.astype(BF16)
        merged = (_sigmoid(ga_ref[...]) * ba + _sigmoid(gb_ref[...]) * bb).astype(BF16)
        merged_ref[...] = merged
        o = _dot(merged, wo_ref[...])
        o_ref[...] = o.astype(BF16)
        x2 = x_ref[...] + mod_ref[M_GT1:M_GT1 + 1, :] * o
        x2_ref[...] = x2
        r = lax.rsqrt(jnp.mean(x2 * x2, axis=-1, keepdims=True) + EPS)
        gain = vec_ref[V_G2:V_G2 + 1, :] * (1.0 + mod_ref[M_SC2:M_SC2 + 1, :])
        h2_ref[...] = (x2 * r * gain + mod_ref[M_SH2:M_SH2 + 1, :]).astype(BF16)

    tok = pl.BlockSpec((tm, D), lambda i: (i, 0))
    wspec = pl.BlockSpec((D, D), lambda i: (0, 0))
    sd = lambda dt: jax.ShapeDtypeStruct((s, D), dt)
    return pl.pallas_call(
        body, name="branch_fwd", grid=(s // tm,),
        in_specs=[tok, tok,
                  pl.BlockSpec((tm, D), lambda i: (i, 3)), pl.BlockSpec((tm, D), lambda i: (i, 4)),
                  tok, pl.BlockSpec((8, D), lambda i: (0, 0)), pl.BlockSpec((16, D), lambda i: (0, 0)),
                  wspec, wspec, wspec],
        out_specs=[tok] * 6,
        out_shape=[sd(BF16), sd(BF16), sd(BF16), sd(BF16), sd(F32), sd(BF16)],
        compiler_params=_params(("parallel",)),
    )(za, pooled, proj, proj, x, modr, vecs, w_a, w_b, w_out)


def _mlp_fwd(h2, x2, target, modr, vecs, w_up, w_down):
    s = x2.shape[0]
    tm = min(TM_MLP, s)

    def body(h2_ref, x2_ref, tgt_ref, mod_ref, vec_ref, wu_ref, wd_ref,
             ru_ref, dx3_ref, ddn_ref, small_ref):
        @pl.when(pl.program_id(0) == 0)
        def _():
            small_ref[...] = jnp.zeros_like(small_ref)

        h2 = h2_ref[...]
        dn = None
        for c in range(D_FF // D):
            cs = slice(c * D, (c + 1) * D)
            ru = jnp.maximum(_dot(h2, wu_ref[:, cs]), 0.0)
            ru_ref[:, cs] = ru.astype(BF16)
            part = _dot((ru * ru).astype(BF16), wd_ref[cs, :])
            dn = part if dn is None else dn + part
        gt2 = mod_ref[M_GT2:M_GT2 + 1, :]
        gf = vec_ref[V_GF:V_GF + 1, :]
        x3 = x2_ref[...] + gt2 * dn
        r3 = lax.rsqrt(jnp.mean(x3 * x3, axis=-1, keepdims=True) + EPS)
        n3 = x3 * r3
        err = n3 * gf - tgt_ref[...]
        dy = err * (1.0 / D)
        dn3 = dy * gf
        dx3 = r3 * (dn3 - n3 * jnp.mean(dn3 * n3, axis=-1, keepdims=True))
        dx3_ref[...] = dx3
        ddn_ref[...] = (dx3 * gt2).astype(BF16)
        small_ref[0:1, :] += jnp.sum(dy * n3, axis=0, keepdims=True)
        small_ref[1:2, :] += jnp.sum(dx3 * dn, axis=0, keepdims=True)
        small_ref[2:3, :] += (0.5 / D) * jnp.sum(err * err, axis=0, keepdims=True)

    tok = pl.BlockSpec((tm, D), lambda i: (i, 0))
    return pl.pallas_call(
        body, name="mlp_fwd", grid=(s // tm,),
        in_specs=[tok, tok, tok,
                  pl.BlockSpec((8, D), lambda i: (0, 0)), pl.BlockSpec((16, D), lambda i: (0, 0)),
                  _resident((D, D_FF)), _resident((D_FF, D))],
        out_specs=[pl.BlockSpec((tm, D_FF), lambda i: (i, 0)), tok, tok,
                   pl.BlockSpec((8, D), lambda i: (0, 0))],
        out_shape=[jax.ShapeDtypeStruct((s, D_FF), BF16), jax.ShapeDtypeStruct((s, D), F32),
                   jax.ShapeDtypeStruct((s, D), BF16), jax.ShapeDtypeStruct((8, D), F32)],
        compiler_params=_params(("arbitrary",)),
    )(h2, x2, target, modr, vecs, w_up, w_down)


def _mlp_bwd(d_dn, ru, x2, dx3, o, modr, vecs, w_up, w_down):
    s = x2.shape[0]
    tm = min(TM_MLP_BWD, s)

    def body(ddn_ref, ru_ref, x2_ref, dx3_ref, o_ref, mod_ref, vec_ref, wu_ref, wd_ref,
             dup_ref, dx2_ref, do_ref, small_ref):
        @pl.when(pl.program_id(0) == 0)
        def _():
            small_ref[...] = jnp.zeros_like(small_ref)

        ddn = ddn_ref[...]
        dh2 = None
        for c in range(D_FF // D):
            cs = slice(c * D, (c + 1) * D)
            dff = _dot_nt(ddn, wd_ref[cs, :])
            dup = (dff * (2.0 * ru_ref[:, cs].astype(F32))).astype(BF16)
            dup_ref[:, cs] = dup
            part = _dot_nt(dup, wu_ref[:, cs])
            dh2 = part if dh2 is None else dh2 + part
        x2 = x2_ref[...]
        r2 = lax.rsqrt(jnp.mean(x2 * x2, axis=-1, keepdims=True) + EPS)
        xn2 = x2 * r2
        gain = vec_ref[V_G2:V_G2 + 1, :] * (1.0 + mod_ref[M_SC2:M_SC2 + 1, :])
        dxn2 = dh2 * gain
        dx2 = dx3_ref[...] + r2 * (dxn2 - xn2 * jnp.mean(dxn2 * xn2, axis=-1, keepdims=True))
        dx2_ref[...] = dx2
        do_ref[...] = (dx2 * mod_ref[M_GT1:M_GT1 + 1, :]).astype(BF16)
        small_ref[0:1, :] += jnp.sum(dh2, axis=0, keepdims=True)
        small_ref[1:2, :] += jnp.sum(dh2 * xn2, axis=0, keepdims=True)
        small_ref[2:3, :] += jnp.sum(dx2 * o_ref[...].astype(F32), axis=0, keepdims=True)

    tok = pl.BlockSpec((tm, D), lambda i: (i, 0))
    wide = pl.BlockSpec((tm, D_FF), lambda i: (i, 0))
    return pl.pallas_call(
        body, name="mlp_bwd", grid=(s // tm,),
        in_specs=[tok, wide, tok, tok, tok,
                  pl.BlockSpec((8, D), lambda i: (0, 0)), pl.BlockSpec((16, D), lambda i: (0, 0)),
                  _resident((D, D_FF)), _resident((D_FF, D))],
        out_specs=[wide, tok, tok, pl.BlockSpec((8, D), lambda i: (0, 0))],
        out_shape=[jax.ShapeDtypeStruct((s, D_FF), BF16), jax.ShapeDtypeStruct((s, D), F32),
                   jax.ShapeDtypeStruct((s, D), BF16), jax.ShapeDtypeStruct((8, D), F32)],
        compiler_params=_params(("arbitrary",)),
    )(d_dn, ru, x2, dx3, o, modr, vecs, w_up, w_down)


def _branch_bwd(do, proj, ba, bb, w_a, w_b, w_out, dep):
    s = do.shape[0]
    tm = min(TM_BRANCH, s)

    def body(do_ref, ga_ref, gb_ref, ba_ref, bb_ref, wa_ref, wb_ref, wo_ref, dep_ref,
             dba_ref, dbb_ref, dg_ref, dza_ref, dpooled_ref):
        dmerged = _dot_nt(do_ref[...], wo_ref[...])
        sa = _sigmoid(ga_ref[...])
        sb = _sigmoid(gb_ref[...])
        dba = (dmerged * sa).astype(BF16)
        dbb = (dmerged * sb).astype(BF16)
        dba_ref[...] = dba
        dbb_ref[...] = dbb
        dg_ref[:, :D] = (dmerged * ba_ref[...].astype(F32) * sa * (1.0 - sa)).astype(BF16)
        dg_ref[:, D:] = (dmerged * bb_ref[...].astype(F32) * sb * (1.0 - sb)).astype(BF16)
        dza_ref[...] = _dot_nt(dba, wa_ref[...])
        dpooled_ref[...] = _dot_nt(dbb, wb_ref[...])

    tok = pl.BlockSpec((tm, D), lambda i: (i, 0))
    wspec = pl.BlockSpec((D, D), lambda i: (0, 0))
    sd = lambda dt: jax.ShapeDtypeStruct((s, D), dt)
    return pl.pallas_call(
        body, name="branch_bwd", grid=(s // tm,),
        in_specs=[tok, pl.BlockSpec((tm, D), lambda i: (i, 3)), pl.BlockSpec((tm, D), lambda i: (i, 4)),
                  tok, tok, wspec, wspec, wspec, pl.BlockSpec(memory_space=pl.ANY)],
        out_specs=[tok, tok, pl.BlockSpec((tm, 2 * D), lambda i: (i, 0)), tok, tok],
        out_shape=[sd(BF16), sd(BF16), jax.ShapeDtypeStruct((s, 2 * D), BF16), sd(F32), sd(F32)],
        compiler_params=_params(("parallel",)),
    )(do, proj, proj, ba, bb, w_a, w_b, w_out, dep)


def _mix_bwd(dza, dpooled, proj, xr, hr, p, gates, dgates, vecs, w_rg_a, w_rg_x, w_pool):
    s = xr.shape[0]
    tm = min(TM_MIX, s)
    nb = s // tm

    def body(dza_ref, dpooled_ref, xh_ref, x_ref, y_ref, xr_ref, hh_ref, hr_ref, p_ref,
             a_ref, mult_ref, ra_ref, ri_ref, dg_ref, vec_ref, wa_ref, wx_ref, wp_ref,
             dproj_ref, dwa_ref, dwx_ref, dwp_ref, small_ref,
             scan_carry, dxr_carry, q_carry, scan_scr):
        i = pl.program_id(0)
        bi = nb - 1 - i
        first_t = bi == 0

        @pl.when(i == 0)
        def _():
            scan_carry[...] = jnp.zeros_like(scan_carry)
            dxr_carry[...] = jnp.zeros_like(dxr_carry)
            q_carry[...] = jnp.zeros_like(q_carry)
            dwa_ref[...] = jnp.zeros_like(dwa_ref)
            dwx_ref[...] = jnp.zeros_like(dwx_ref)
            dwp_ref[...] = jnp.zeros_like(dwp_ref)
            small_ref[...] = jnp.zeros_like(small_ref)

        row = lax.broadcasted_iota(jnp.int32, (tm, GW), 0)
        is_t0 = jnp.logical_and(first_t, row == 0)
        t_glob = (row + bi * tm + 1).astype(F32)
        colsum = lambda v: jnp.sum(v, axis=0, keepdims=True)
        for g in range(N_GROUPS):
            cs = slice(g * GW, (g + 1) * GW)
            vec = vec_ref[:, cs]
            xr = xr_ref[:, cs]
            hr = hr_ref[:, cs]
            dza = dza_ref[:, cs]
            ga, dga = _gelu_and_grad(y_ref[:, cs])
            dproj_ref[:, D + g * GW:D + (g + 1) * GW] = (dza * hr * dga).astype(BF16)
            dhr = dza * ga
            a = a_ref[:, cs]
            mult = mult_ref[:, cs]
            ra = ra_ref[:, cs].astype(F32)
            ri = ri_ref[:, cs].astype(F32)
            sp = _softplus(vec[V_A_PARAM:V_A_PARAM + 1])
            m = jnp.where(row == tm - 1, 1.0, _shift_up(a, 1))
            gsum = _scan_up(m, dhr, scan_carry[0:1, cs], scan_scr)
            scan_carry[0:1, cs] = a[0:1, :] * gsum[0:1, :]
            hh = jnp.where(first_t, 0.0, hh_ref[:, cs])
            hprev = _shift_down(jnp.concatenate([hh, hr], axis=0), 1)[8:]
            da = gsum * hprev
            dmult = jnp.where(is_t0, 0.0, gsum * xr * ri)
            dlog_a = da * a - dmult * a * a / mult
            dri = gsum * xr * mult
            dxr = gsum * ri * mult
            small_ref[7:8, cs] += colsum((-C_RG) * ra * dlog_a)
            dpa = (((-C_RG) * sp) * dlog_a * ra * (1.0 - ra))
            dpx = dri * ri * (1.0 - ri)
            small_ref[5:6, cs] += colsum(dpa)
            small_ref[6:7, cs] += colsum(dpx)
            dpa = dpa.astype(BF16)
            dpx = dpx.astype(BF16)
            xrb = xr.astype(BF16)
            dwa_ref[g] += _dot_tn(xrb, dpa)
            dwx_ref[g] += _dot_tn(xrb, dpx)
            dxr = dxr + _dot_nt(dpa, wa_ref[g]) + _dot_nt(dpx, wx_ref[g])
            small_ref[4:5, cs] += colsum(dxr)
            xh = jnp.where(first_t, 0.0, xh_ref[:, cs])
            taps = _conv_taps(jnp.concatenate([xh, x_ref[:, cs]], axis=0))
            dxr_ext = jnp.concatenate([dxr, dxr_carry[:, cs]], axis=0)
            dx = vec[V_CONV_W + 3:V_CONV_W + 4] * dxr
            for j in range(4):
                small_ref[j:j + 1, cs] += colsum(dxr * taps[j])
                if j < 3:
                    dx = dx + vec[V_CONV_W + j:V_CONV_W + j + 1] * _shift_up(dxr_ext, 3 - j)[:tm]
            dxr_carry[:, cs] = dxr[0:8, :]
            dproj_ref[:, cs] = dx.astype(BF16)
            pg = p_ref[:, cs]
            dpooled = dpooled_ref[:, cs]
            pb = _dot(pg, wp_ref[g]) + vec[V_B_POOL:V_B_POOL + 1]
            small_ref[9:10, cs] += colsum(dpooled * pb)
            dpb = dpooled * vec[V_POOL_SCALE:V_POOL_SCALE + 1]
            small_ref[8:9, cs] += colsum(dpb)
            dpbb = dpb.astype(BF16)
            dwp_ref[g] += _dot_tn(pg, dpbb)
            dp = _dot_nt(dpbb, wp_ref[g])
            q = dp / jnp.minimum(t_glob, float(POOL_WINDOWS[g]))
            sm = jnp.concatenate([q, q_carry[:, cs]], axis=0)
            k = 1
            while k < POOL_WINDOWS[g]:
                sm = sm + _shift_up(sm, k)
                k *= 2
            q_carry[:, cs] = q[0:HALO_U, :]
            dproj_ref[:, 2 * D + g * GW:2 * D + (g + 1) * GW] = (sm[:tm] - dp).astype(BF16)
        dproj_ref[:, 3 * D:] = dg_ref[...]

    rev = lambda i: nb - 1 - i
    tok = pl.BlockSpec((tm, D), lambda i: (rev(i), 0))
    col = lambda k: pl.BlockSpec((tm, D), lambda i: (rev(i), k))
    halo8 = lambda k: pl.BlockSpec((8, D), lambda i: (jnp.maximum(rev(i) * (tm // 8) - 1, 0), k))
    wspec = pl.BlockSpec((N_GROUPS, GW, GW), lambda i: (0, 0, 0))
    wshape = jax.ShapeDtypeStruct((N_GROUPS, GW, GW), F32)
    return pl.pallas_call(
        body, name="mix_bwd", grid=(nb,),
        in_specs=[tok, tok, halo8(0), col(0), col(1), tok, halo8(0), tok, tok, tok, tok, tok, tok,
                  pl.BlockSpec((tm, 2 * D), lambda i: (rev(i), 0)),
                  pl.BlockSpec((16, D), lambda i: (0, 0)), wspec, wspec, wspec],
        out_specs=[pl.BlockSpec((tm, D_IN), lambda i: (rev(i), 0)), wspec, wspec, wspec,
                   pl.BlockSpec((16, D), lambda i: (0, 0))],
        out_shape=[jax.ShapeDtypeStruct((s, D_IN), BF16), wshape, wshape, wshape,
                   jax.ShapeDtypeStruct((16, D), F32)],
        scratch_shapes=[pltpu.VMEM((8, D), F32), pltpu.VMEM((8, D), F32), pltpu.VMEM((HALO_U, D), F32),
                        pltpu.VMEM((3, tm, LANES), F32)],
        compiler_params=_params(("arbitrary",)),
    )(dza, dpooled, proj, proj, proj, xr, hr, hr, p, *gates, dgates, vecs, w_rg_a, w_rg_x, w_pool)


def _proj_bwd(dproj, x, dx2, modr, vecs, w_in):
    s = x.shape[0]
    tm = min(TM_PROJ, s)

    def body(dp_ref, x_ref, dx2_ref, mod_ref, vec_ref, w_ref, gx_ref, small_ref):
        @pl.when(pl.program_id(0) == 0)
        def _():
            small_ref[...] = jnp.zeros_like(small_ref)

        dh1 = None
        for c in range(D_IN // D):
            cs = slice(c * D, (c + 1) * D)
            part = _dot_nt(dp_ref[:, cs], w_ref[:, cs])
            dh1 = part if dh1 is None else dh1 + part
        xv = x_ref[...]
        r1 = lax.rsqrt(jnp.mean(xv * xv, axis=-1, keepdims=True) + EPS)
        xn1 = xv * r1
        gain = vec_ref[V_G1:V_G1 + 1, :] * (1.0 + mod_ref[M_SC1:M_SC1 + 1, :])
        dxn1 = dh1 * gain
        gx_ref[...] = dx2_ref[...] + r1 * (dxn1 - xn1 * jnp.mean(dxn1 * xn1, axis=-1, keepdims=True))
        small_ref[0:1, :] += jnp.sum(dh1, axis=0, keepdims=True)
        small_ref[1:2, :] += jnp.sum(dh1 * xn1, axis=0, keepdims=True)

    tok = pl.BlockSpec((tm, D), lambda i: (i, 0))
    return pl.pallas_call(
        body, name="proj_bwd", grid=(s // tm,),
        in_specs=[pl.BlockSpec((tm, D_IN), lambda i: (i, 0)), tok, tok,
                  pl.BlockSpec((8, D), lambda i: (0, 0)), pl.BlockSpec((16, D), lambda i: (0, 0)),
                  _resident((D, D_IN))],
        out_specs=[tok, pl.BlockSpec((8, D), lambda i: (0, 0))],
        out_shape=[jax.ShapeDtypeStruct((s, D), F32), jax.ShapeDtypeStruct((8, D), F32)],
        compiler_params=_params(("arbitrary",)),
    )(dproj, x, dx2, modr, vecs, w_in)


def _wgrad(a, b, name, square_a=False, dep=None):
    s, ka = a.shape
    n = b.shape[1]
    tka = ka if ka <= 1024 else ka // 2
    tn = n if n <= 1024 else n // 2
    ts = min(TS_WGRAD, s)
    ns = s // ts
    nc = 512
    deps = [] if dep is None else [dep]

    def body(a_ref, b_ref, *refs):
        out_ref, acc_ref = refs[-2:]
        t = pl.program_id(2)

        @pl.when(t == 0)
        def _():
            acc_ref[...] = jnp.zeros_like(acc_ref)

        av = a_ref[...]
        if square_a:
            af = av.astype(F32)
            av = (af * af).astype(BF16)
        for c in range(tn // nc):
            cs = slice(c * nc, (c + 1) * nc)
            acc_ref[:, cs] += _dot_tn(av, b_ref[:, cs])

        @pl.when(t == ns - 1)
        def _():
            out_ref[...] = acc_ref[...].astype(BF16)

    return pl.pallas_call(
        body, name=name, grid=(ka // tka, n // tn, ns),
        in_specs=[pl.BlockSpec((ts, tka), lambda i, j, t: (t, i)),
                  pl.BlockSpec((ts, tn), lambda i, j, t: (t, j))] + [pl.BlockSpec(memory_space=pl.ANY)] * len(deps),
        out_specs=pl.BlockSpec((tka, tn), lambda i, j, t: (i, j)),
        out_shape=jax.ShapeDtypeStruct((ka, n), BF16),
        scratch_shapes=[pltpu.VMEM((tka, tn), F32)],
        compiler_params=_params(("parallel", "parallel", "arbitrary")),
    )(a, b, *deps)


def _window(ref, kind, idx, size):
    start = pl.multiple_of(idx * size, size)
    if kind == 0:
        return ref.at[pl.ds(start, size)]
    if kind == 1:
        return ref.at[:, pl.ds(start, size)]
    return ref.at[:, :, pl.ds(start, size)]


def _mesh_place():
    x, y, c = lax.axis_index("x"), lax.axis_index("y"), lax.axis_index("c")
    return x, y, c, 4 * x + 2 * y + c


def _peer(x, y, c, q):
    px = 1 - x if q & 4 else x
    py = 1 - y if q & 2 else y
    pc = 1 - c if q & 1 else c
    return (px, py, pc), 4 * px + 2 * py + pc


def _all_gather(shards, kinds, name, dep=None):
    n = len(shards)
    deps = [] if dep is None else [dep]
    full_shapes = []
    for sh, kind in zip(shards, kinds):
        dims = list(sh.shape)
        dims[kind] *= N_DEV
        full_shapes.append(jax.ShapeDtypeStruct(tuple(dims), sh.dtype))

    def body(*refs):
        ins, outs = refs[:n], refs[n + len(deps):2 * n + len(deps)]
        send_sems, recv_sems, local_sems = refs[2 * n + len(deps):]
        x, y, c, me = _mesh_place()
        sends, recvs, locals_ = [], [], []
        for k in range(n):
            size = shards[k].shape[kinds[k]]
            mine = _window(outs[k], kinds[k], me, size)
            lc = pltpu.make_async_copy(ins[k], mine, local_sems.at[k])
            lc.start()
            locals_.append(lc)
            for q in range(1, N_DEV):
                peer, peer_idx = _peer(x, y, c, q)
                cp = pltpu.make_async_remote_copy(
                    src_ref=ins[k], dst_ref=mine, send_sem=send_sems.at[k, q], recv_sem=recv_sems.at[k, q],
                    device_id=peer, device_id_type=MESH)
                cp.start()
                sends.append(cp)
                recvs.append(pltpu.make_async_remote_copy(
                    src_ref=ins[k], dst_ref=_window(outs[k], kinds[k], peer_idx, size),
                    send_sem=send_sems.at[k, q], recv_sem=recv_sems.at[k, q],
                    device_id=peer, device_id_type=MESH))
        for cp in recvs:
            cp.wait_recv()
        for cp in sends:
            cp.wait_send()
        for lc in locals_:
            lc.wait()

    any_spec = pl.BlockSpec(memory_space=pl.ANY)
    return pl.pallas_call(
        body, name=name,
        in_specs=[any_spec] * (n + len(deps)), out_specs=[any_spec] * n, out_shape=full_shapes,
        scratch_shapes=[pltpu.SemaphoreType.DMA((n, N_DEV)), pltpu.SemaphoreType.DMA((n, N_DEV)),
                        pltpu.SemaphoreType.DMA((n,))],
    )(*shards, *deps)


_HBM = pl.BlockSpec(memory_space=pltpu.HBM)
_SEM = pl.BlockSpec(memory_space=pltpu.SEMAPHORE)
_EFFECT = pltpu.SideEffectType.DATAFLOW_SIDE_EFFECTING


N_NEAR = 4


def _near(x, y, c):
    out = [((x, y, 1 - c), 4 * x + 2 * y + 1 - c)]
    for j in (1, 2, 3):
        px = 1 - x if j & 2 else x
        py = 1 - y if j & 1 else y
        out.append(((px, py, c), 4 * px + 2 * py + c))
    return out


def _remote(src, dst, send_sems, recv_sems, slot, device):
    return pltpu.make_async_remote_copy(src_ref=src, dst_ref=dst, send_sem=send_sems.at[slot], recv_sem=recv_sems.at[slot],
                                        device_id=device, device_id_type=MESH)


def _split_call(name, arrays, sems_in, n_new_sems, after, emit):
    na, ns, nn = len(arrays), len(sems_in), len(n_new_sems)

    def body(*refs):
        emit(refs[:na], refs[na:na + ns], refs[na + ns + 1:na + ns + 1 + nn])
        refs[-1][...] = jnp.zeros_like(refs[-1])

    outs = pl.pallas_call(
        body, name=name,
        out_shape=(*[pltpu.SemaphoreType.DMA((m,)) for m in n_new_sems],
                   *[pltpu.HBM(a.shape, a.dtype) for a in arrays], jax.ShapeDtypeStruct((8, 128), F32)),
        in_specs=[_HBM] * na + [_SEM] * ns + [pl.BlockSpec(memory_space=pl.ANY)],
        out_specs=(*[_SEM] * nn, *[_HBM] * na, pl.BlockSpec(memory_space=pltpu.VMEM)),
        input_output_aliases={i: nn + i for i in range(na)},
        compiler_params=pltpu.CompilerParams(has_side_effects=_EFFECT),
    )(*[pltpu.with_memory_space_constraint(a, pltpu.HBM) for a in arrays], *sems_in, after)
    return list(outs[:nn]), list(outs[nn:nn + na]), outs[-1]


class _Gather:
    def __init__(self, shards, kinds, lands, after, name):
        self.n, self.kinds, self.name = len(shards), kinds, name
        self.sizes = [s.shape[k] for s, k in zip(shards, kinds)]
        n = self.n

        def emit(arr, _, new):
            x, y, c, me = _mesh_place()
            for k in range(n):
                mine = _window(arr[n + k], kinds[k], me, self.sizes[k])
                for j, (dev, _) in enumerate(_near(x, y, c)):
                    _remote(arr[k], mine, new[0], new[1], k * N_NEAR + j, dev).start()

        self.sems, self.arrays, self.token = _split_call(name + "_start", [*shards, *lands], [], [n * N_NEAR] * 2,
                                                         after, emit)

    def forward(self, after):
        n, kinds, sizes = self.n, self.kinds, self.sizes

        def emit(arr, old, new):
            x, y, c, _ = _mesh_place()
            near = _near(x, y, c)
            for k in range(n):
                for j in (1, 2, 3):
                    dev, idx = near[j]
                    landed = _window(arr[n + k], kinds[k], idx, sizes[k])
                    _remote(arr[k], landed, old[0], old[1], k * N_NEAR + j, dev).wait_recv()
                    _remote(landed, landed, new[0], new[1], k * N_NEAR + j, near[0][0]).start()

        new, self.arrays, self.token = _split_call(self.name + "_forward", self.arrays, self.sems, [n * N_NEAR] * 2,
                                                   after, emit)
        self.sems = [*self.sems, *new]

    def finish(self, after):
        n, kinds, sizes = self.n, self.kinds, self.sizes

        def emit(arr, old, _):
            x, y, c, me = _mesh_place()
            near = _near(x, y, c)
            other_core = near[0][0]
            for k in range(n):
                win = lambda idx: _window(arr[n + k], kinds[k], idx, sizes[k])
                for j, (dev, idx) in enumerate(near):
                    _remote(arr[k], win(me), old[0], old[1], k * N_NEAR + j, dev).wait_send()
                _remote(arr[k], win(near[0][1]), old[0], old[1], k * N_NEAR, other_core).wait_recv()
                for j in (1, 2, 3):
                    idx = near[j][1]
                    _remote(win(idx), win(idx), old[2], old[3], k * N_NEAR + j, other_core).wait_send()
                    _remote(arr[k], win(idx + 1 - 2 * c), old[2], old[3], k * N_NEAR + j, other_core).wait_recv()

        _, arrays, _ = _split_call(self.name + "_finish", self.arrays, self.sems, [], after, emit)
        return arrays[n:]


class _Scatter:
    def __init__(self, partials, kinds, after, name):
        self.n, self.kinds, self.name, self.partials = len(partials), kinds, name, partials
        self.sizes = [p.shape[k] // N_DEV for p, k in zip(partials, kinds)]
        n, sizes = self.n, self.sizes
        self.slot_shapes = []
        for p, k, size in zip(partials, kinds, sizes):
            dims = list(p.shape)
            dims[k] = size
            self.slot_shapes.append((N_NEAR, *dims))
        slots = [lax.empty(sh, p.dtype) for sh, p in zip(self.slot_shapes, partials)]

        def emit(arr, _, new):
            x, y, c, _ = _mesh_place()
            near = _near(x, y, c)
            for k in range(n):
                for j in range(N_NEAR):
                    owner = near[j][1] if j == 0 else near[j][1] + 1 - 2 * c
                    _remote(_window(arr[k], kinds[k], owner, sizes[k]), arr[n + k].at[j], new[0], new[1],
                            k * N_NEAR + j, near[0][0]).start()

        self.sems, self.arrays, self.token = _split_call(name + "_start", [*partials, *slots], [], [n * N_NEAR] * 2,
                                                         after, emit)

    def combine_and_send(self, own4, after):
        n, kinds, sizes = self.n, self.kinds, self.sizes

        def emit_wait(arr, old, _):
            x, y, c, _ = _mesh_place()
            near = _near(x, y, c)
            for k in range(n):
                for j in range(N_NEAR):
                    owner = near[j][1] if j == 0 else near[j][1] + 1 - 2 * c
                    cp = _remote(_window(arr[k], kinds[k], owner, sizes[k]), arr[n + k].at[j], old[0], old[1],
                                 k * N_NEAR + j, near[0][0])
                    cp.wait_send()
                    cp.wait_recv()

        _, arrays, _ = _split_call(self.name + "_landed", self.arrays, self.sems, [], after, emit_wait)
        chip_sums = _chip_sums(arrays[:n], arrays[n:], kinds, sizes, own4, self.name + "_combine")
        arrivals = [lax.empty((N_NEAR - 1, *sh[1:]), p.dtype) for sh, p in zip(self.slot_shapes, self.partials)]

        def emit_send(arr, _, new):
            x, y, c, _ = _mesh_place()
            near = _near(x, y, c)
            for k in range(n):
                for j in (1, 2, 3):
                    _remote(arr[k].at[j], arr[n + k].at[j - 1], new[0], new[1], k * N_NEAR + j, near[j][0]).start()

        self.sems, self.arrays, self.token = _split_call(self.name + "_send", [*chip_sums, *arrivals], [],
                                                         [n * N_NEAR] * 2, chip_sums[0], emit_send)

    def finish(self, after):
        n = self.n

        def emit(arr, old, _):
            x, y, c, _ = _mesh_place()
            near = _near(x, y, c)
            for k in range(n):
                for j in (1, 2, 3):
                    cp = _remote(arr[k].at[j], arr[n + k].at[j - 1], old[0], old[1], k * N_NEAR + j, near[j][0])
                    cp.wait_send()
                    cp.wait_recv()

        _, arrays, _ = _split_call(self.name + "_finish", self.arrays, self.sems, [], after, emit)
        return arrays[:n], arrays[n:]


def _chip_sums(partials, slots, kinds, sizes, own4, name):
    n = len(partials)

    def body(own_ref, *refs):
        for k in range(n):
            refs[2 * n + k][...] = (refs[k][...].astype(F32) + refs[n + k][...].astype(F32)).astype(BF16)

    in_specs, slot_specs = [], []
    for p, s, kind, size in zip(partials, slots, kinds, sizes):
        block = list(p.shape)
        block[kind] = size
        nd = len(block)
        in_specs.append(pl.BlockSpec(tuple(block), functools.partial(
            lambda j, own, kind, nd: tuple(own[j] if d == kind else 0 for d in range(nd)), kind=kind, nd=nd)))
        slot_specs.append(pl.BlockSpec((None, *block), functools.partial(
            lambda j, own, nd: (j,) + (0,) * nd, nd=nd)))
    return pl.pallas_call(
        body, name=name,
        grid_spec=pltpu.PrefetchScalarGridSpec(num_scalar_prefetch=1, grid=(N_NEAR,),
                                               in_specs=in_specs + slot_specs, out_specs=slot_specs),
        out_shape=[jax.ShapeDtypeStruct(s.shape, s.dtype) for s in slots],
        compiler_params=_params(("arbitrary",)),
    )(own4, *partials, *slots)


def _after(small, token):
    return small + token[0:1, 0:1].astype(small.dtype)


def _own_window(kind, shard, me):
    dims = list(shard.shape)
    dims[kind] *= N_DEV
    start = [0] * len(dims)
    start[kind] = me * shard.shape[kind]
    return lax.dynamic_update_slice(lax.empty(tuple(dims), shard.dtype), shard, tuple(start))


def _silu(c):
    return c * _sigmoid_tail(c)


def _ada_fwd(c_all, w_ada, b_ada_cols):
    def body(c_ref, w_ref, b_ref, out_ref):
        out_ref[...] = jnp.dot(_silu(c_ref[...]), w_ref[...], preferred_element_type=F32,
                               precision=lax.Precision.HIGHEST) + b_ref[...]

    return pl.pallas_call(
        body, name="ada_fwd", out_shape=jax.ShapeDtypeStruct((N_DEV, w_ada.shape[1]), F32),
    )(c_all, w_ada, b_ada_cols)


def _adam(w, g, m, v):
    m = ADAM_B1 * m + (1.0 - ADAM_B1) * g
    v = ADAM_B2 * v + (1.0 - ADAM_B2) * (g * g)
    m_hat = m / (1.0 - ADAM_B1 ** ADAM_STEP)
    v_hat = v / (1.0 - ADAM_B2 ** ADAM_STEP)
    delta = -ADAM_LR * (m_hat / (jnp.sqrt(v_hat) + ADAM_EPS) + ADAM_WD * w)
    return delta, m, v


def _ada_bwd_adam(c_all, dmod_cols, w, m, v):
    def body(c_ref, d_ref, w_ref, m_ref, v_ref, g_ref, delta_ref, nm_ref, nv_ref):
        g = lax.dot_general(_silu(c_ref[...]), d_ref[...], (((0,), (0,)), ((), ())),
                            preferred_element_type=F32, precision=lax.Precision.HIGHEST)
        g_ref[...] = g
        delta_ref[...], nm_ref[...], nv_ref[...] = _adam(w_ref[...], g, m_ref[...], v_ref[...])

    sd = jax.ShapeDtypeStruct(w.shape, F32)
    return pl.pallas_call(body, name="ada_bwd_adam", out_shape=[sd] * 4,
                          compiler_params=pltpu.CompilerParams(vmem_limit_bytes=V7X_VMEM_LIMIT),
                          )(c_all, dmod_cols, w, m, v)


def _sum_slots_adam(chip_sums, arrivals, w, m, v, name):
    r, cdim = w.shape
    tr = min(r, 256)

    def body(c_ref, a_ref, w_ref, m_ref, v_ref, g_ref, delta_ref, nm_ref, nv_ref):
        g = c_ref[...].astype(F32)
        for j in (1, 2, 3):
            g = g + a_ref[j - 1].astype(F32)
        g_ref[...] = g
        delta_ref[...], nm_ref[...], nv_ref[...] = _adam(w_ref[...], g, m_ref[...], v_ref[...])

    blk = pl.BlockSpec((tr, cdim), lambda i: (i, 0))
    sd = jax.ShapeDtypeStruct((r, cdim), F32)
    return pl.pallas_call(
        body, name=name, grid=(r // tr,),
        in_specs=[pl.BlockSpec((None, tr, cdim), lambda i: (0, i, 0)),
                  pl.BlockSpec((N_NEAR - 1, tr, cdim), lambda i: (0, i, 0)), blk, blk, blk],
        out_specs=[blk] * 4, out_shape=[sd] * 4,
        compiler_params=_params(("parallel",)),
    )(chip_sums, arrivals, w, m, v)


N_SMALL = 40
N_SMALL_PARAMS = 11


def _pack_vecs(conv_w_full, rows):
    def body(cw_ref, *refs):
        out = refs[-1]
        out[...] = jnp.zeros_like(out)
        out[0:4, :] = cw_ref[0:4, :]
        for r, ref in enumerate(refs[:-1]):
            out[4 + r:5 + r, :] = ref[...]

    return pl.pallas_call(body, name="pack_vecs", out_shape=jax.ShapeDtypeStruct((16, D), F32))(conv_w_full, *rows)


def _small_finish(gathered, mod_all, vecs, ws, ms, vs):
    n = N_SMALL_PARAMS

    def body(g_ref, mod_ref, vec_ref, *refs):
        w_refs, m_refs, v_refs = refs[:n], refs[n:2 * n], refs[2 * n:3 * n]
        outs = refs[3 * n:]
        g1 = vec_ref[V_G1:V_G1 + 1, :]
        g2 = vec_ref[V_G2:V_G2 + 1, :]
        zero = jnp.zeros((1, D), F32)
        dg1, dg2, dgf, loss_lanes = zero, zero, zero, zero
        mixer = jnp.zeros((16, D), F32)
        db_ada = jnp.zeros((6, D), F32)
        for b in range(N_DEV):
            gb = g_ref[b]
            mod = mod_ref[b]
            q1 = gb[33:34]
            q2 = gb[9:10]
            dmod = jnp.concatenate([gb[32:33], q1 * g1, gb[10:11], gb[8:9], q2 * g2, gb[1:2]], axis=0)
            outs[4 * n][b] = dmod
            db_ada = db_ada + dmod
            dg1 = dg1 + q1 * (1.0 + mod[M_SC1:M_SC1 + 1])
            dg2 = dg2 + q2 * (1.0 + mod[M_SC2:M_SC2 + 1])
            dgf = dgf + gb[0:1]
            loss_lanes = loss_lanes + gb[2:3]
            mixer = mixer + gb[16:32]
        d_a_param = mixer[7:8] * _sigmoid_tail(vec_ref[V_A_PARAM:V_A_PARAM + 1, :])
        grads = [dg1, dg2, mixer[4:5], mixer[5:6], mixer[6:7], d_a_param, mixer[8:9], mixer[9:10], dgf,
                 db_ada, mixer[0:4]]
        for k in range(n):
            outs[k][...] = grads[k]
            outs[n + k][...], outs[2 * n + k][...], outs[3 * n + k][...] = _adam(
                w_refs[k][...], grads[k], m_refs[k][...], v_refs[k][...])
        outs[4 * n + 1][...] = jnp.broadcast_to(jnp.sum(loss_lanes, axis=1, keepdims=True), (8, 128))

    shapes = [jax.ShapeDtypeStruct(w.shape, F32) for w in ws]
    return pl.pallas_call(
        body, name="small_finish",
        out_shape=shapes * 4 + [jax.ShapeDtypeStruct((N_DEV, 6, D), F32), jax.ShapeDtypeStruct((8, 128), F32)],
    )(gathered, mod_all, vecs, *ws, *ms, *vs)


def _pad_rows(a, rows):
    return jnp.pad(a, ((0, rows - a.shape[0]), (0, 0)))


def kernel(x, c, norm_mix_g, norm_mlp_g, w_ada, b_ada, w_in, conv_w, conv_b, w_rg_a, b_rg_a, w_rg_x, b_rg_x, a_param, w_branch_a, w_pool, b_pool, pool_scale, w_branch_b, w_out, w_up, w_down, final_g, loss_target, m_norm_mix_g, m_norm_mlp_g, m_w_ada, m_b_ada, m_w_in, m_conv_w, m_conv_b, m_w_rg_a, m_b_rg_a, m_w_rg_x, m_b_rg_x, m_a_param, m_w_branch_a, m_w_pool, m_b_pool, m_pool_scale, m_w_branch_b, m_w_out, m_w_up, m_w_down, m_final_g, v_norm_mix_g, v_norm_mlp_g, v_w_ada, v_b_ada, v_w_in, v_conv_w, v_conv_b, v_w_rg_a, v_b_rg_a, v_w_rg_x, v_b_rg_x, v_a_param, v_w_branch_a, v_w_pool, v_b_pool, v_pool_scale, v_w_branch_b, v_w_out, v_w_up, v_w_down, v_final_g):
    me = 4 * lax.axis_index("x") + 2 * lax.axis_index("y") + lax.axis_index("c")
    s = x.shape[1]
    x2d = x.reshape(s, D)
    target = loss_target.reshape(s, D)
    n_ada = w_ada.shape[2]

    sharded = dict(w_in=(w_in[0], 1), w_up=(w_up[0], 1), w_down=(w_down[0], 0), w_branch_a=(w_branch_a[0], 0),
                   w_branch_b=(w_branch_b[0], 0), w_out=(w_out[0], 0), w_rg_a=(w_rg_a[0], 1), w_rg_x=(w_rg_x[0], 1),
                   w_pool=(w_pool[0], 1))
    kind = {k: v[1] for k, v in sharded.items()}
    shard = {k: v[0].astype(BF16) for k, v in sharded.items()}

    conv_w_full, c_rows = _all_gather([_pad_rows(conv_w[0], 8), _pad_rows(c, 8)], [1, 0], "gather_c")
    c_all = c_rows.reshape(N_DEV, 8, D)[:, 0, :]
    b_ada_cols = lax.dynamic_slice(b_ada, (0, me * n_ada), (1, n_ada))
    mod_part = _ada_fwd(c_all, w_ada[0], b_ada_cols)
    mod_parts, = _all_gather([mod_part], [0], "gather_mod")

    first_names = ["w_in", "w_rg_a", "w_rg_x", "w_pool"]
    branch_names = ["w_branch_a", "w_branch_b", "w_out"]
    mlp_names = ["w_up", "w_down"]

    def gather(group, after, name):
        return _Gather([shard[k] for k in group], [kind[k] for k in group],
                       [_own_window(kind[k], shard[k], me) for k in group], after, name)

    g_first = gather(first_names, mod_parts, "gather_first")
    g_branch = gather(branch_names, g_first.token, "gather_branch")
    g_mlp = gather(mlp_names, g_branch.token, "gather_mlp")

    mod_all = jnp.transpose(mod_parts.reshape(N_DEV, N_DEV, n_ada), (1, 0, 2)).reshape(N_DEV, 6, D)
    mod_all = jnp.pad(mod_all, ((0, 0), (0, 2), (0, 0)))
    modr = lax.dynamic_index_in_dim(mod_all, me, 0, keepdims=False)
    vecs = _pack_vecs(conv_w_full, [conv_b, b_rg_a, b_rg_x, a_param, b_pool, pool_scale,
                                    norm_mix_g, norm_mlp_g, final_g.reshape(1, D)])
    vecs = _after(vecs, g_mlp.token)
    g_first.forward(vecs)
    wg = dict(zip(first_names, g_first.finish(g_first.token)))

    proj, h1 = _proj_fwd(x2d, modr, vecs, wg["w_in"])
    g_branch.forward(h1)
    xr, hr, za, p, pooled, *gates = _mix_fwd(proj, _after(vecs, g_branch.token), wg["w_rg_a"], wg["w_rg_x"], wg["w_pool"])
    g_mlp.forward(za)
    wg.update(zip(branch_names, g_branch.finish(g_mlp.token)))
    ba, bb, merged, o, x2, h2 = _branch_fwd(za, pooled, proj, x2d, modr, vecs,
                                            wg["w_branch_a"], wg["w_branch_b"], wg["w_out"])
    wg.update(zip(mlp_names, g_mlp.finish(h2)))
    ru, dx3, d_dn, small_f = _mlp_fwd(h2, x2, target, modr, vecs, wg["w_up"], wg["w_down"])

    near = _near(lax.axis_index("x"), lax.axis_index("y"), lax.axis_index("c"))
    own4 = jnp.stack([me, near[1][1], near[2][1], near[3][1]]).astype(jnp.int32)

    def scatter(group, partial, after, name):
        return _Scatter([partial[k] for k in group], [kind[k] for k in group], after, name)

    dup, dx2, do, small_m = _mlp_bwd(d_dn, ru, x2, dx3, o, modr, vecs, wg["w_up"], wg["w_down"])
    partial = dict(w_up=_wgrad(h2, dup, "wgrad_up"), w_down=_wgrad(ru, d_dn, "wgrad_down", square_a=True))
    s_mlp = scatter(mlp_names, partial, dx2, "scatter_mlp")

    dba, dbb, dgates, dza, dpooled = _branch_bwd(do, proj, ba, bb, wg["w_branch_a"], wg["w_branch_b"], wg["w_out"],
                                                 dep=s_mlp.token)
    s_mlp.combine_and_send(own4, dza)
    dproj, dw_rg_a, dw_rg_x, dw_pool, small_x = _mix_bwd(dza, dpooled, proj, xr, hr, p, gates, dgates,
                                                         _after(vecs, s_mlp.token),
                                                         wg["w_rg_a"], wg["w_rg_x"], wg["w_pool"])
    partial.update(w_branch_a=_wgrad(za, dba, "wgrad_branch_a"), w_branch_b=_wgrad(pooled, dbb, "wgrad_branch_b"),
                   w_out=_wgrad(merged, do, "wgrad_out"),
                   w_rg_a=dw_rg_a.astype(BF16), w_rg_x=dw_rg_x.astype(BF16), w_pool=dw_pool.astype(BF16))
    mixer_names = ["w_rg_a", "w_rg_x", "w_pool", "w_branch_a", "w_branch_b", "w_out"]
    s_mixer = scatter(mixer_names, partial, s_mlp.token, "scatter_mixer")

    partial["w_in"] = _wgrad(h1, dproj, "wgrad_in", dep=s_mixer.token)
    s_in = scatter(["w_in"], partial, s_mixer.token, "scatter_in")
    s_mixer.combine_and_send(own4, s_in.token)
    grad_x, small_p = _proj_bwd(dproj, x2d, dx2, _after(modr, s_mixer.token), vecs, wg["w_in"])
    s_in.combine_and_send(own4, grad_x)

    locals_ = dict(w_in=(w_in, m_w_in, v_w_in), w_up=(w_up, m_w_up, v_w_up), w_down=(w_down, m_w_down, v_w_down),
                   w_branch_a=(w_branch_a, m_w_branch_a, v_w_branch_a),
                   w_branch_b=(w_branch_b, m_w_branch_b, v_w_branch_b), w_out=(w_out, m_w_out, v_w_out),
                   w_rg_a=(w_rg_a, m_w_rg_a, v_w_rg_a), w_rg_x=(w_rg_x, m_w_rg_x, v_w_rg_x),
                   w_pool=(w_pool, m_w_pool, v_w_pool))
    res = {}

    def finish(group, exchange, after):
        chip_sums, arrivals = exchange.finish(after)
        for k, cs, ar in zip(group, chip_sums, arrivals):
            w, m, v = locals_[k]
            shape2d = w.reshape(-1, w.shape[-1]).shape
            outs = _sum_slots_adam(cs.reshape(N_NEAR, *shape2d), ar.reshape(N_NEAR - 1, *shape2d),
                                   w.reshape(shape2d), m.reshape(shape2d), v.reshape(shape2d), "adam_" + k)
            res[k] = [t.reshape(w.shape) for t in outs]
        return res[group[-1]][0]

    done = finish(mlp_names, s_mlp, s_in.token)
    done = finish(mixer_names, s_mixer, done)
    done = finish(["w_in"], s_in, done)

    small = jnp.concatenate([small_f, small_m, small_x, small_p], axis=0)
    small_all, = _all_gather([small], [0], "gather_small", dep=done)
    small_all = small_all.reshape(N_DEV, N_SMALL, D)

    def embed(cw):
        return lax.dynamic_update_slice(jnp.zeros((4, D), F32), cw[0], (0, me * (D // N_DEV)))

    def smalls(ng, nl, cb, bra, brx, ap, bp, ps, fg, ba_, cw):
        return [ng, nl, cb, bra, brx, ap, bp, ps, fg.reshape(1, D), ba_.reshape(6, D), embed(cw)]

    small_names = ["norm_mix_g", "norm_mlp_g", "conv_b", "b_rg_a", "b_rg_x", "a_param", "b_pool", "pool_scale",
                   "final_g", "b_ada", "conv_w"]
    fin = _small_finish(
        small_all, mod_all, vecs,
        smalls(norm_mix_g, norm_mlp_g, conv_b, b_rg_a, b_rg_x, a_param, b_pool, pool_scale, final_g, b_ada, conv_w),
        smalls(m_norm_mix_g, m_norm_mlp_g, m_conv_b, m_b_rg_a, m_b_rg_x, m_a_param, m_b_pool, m_pool_scale,
               m_final_g, m_b_ada, m_conv_w),
        smalls(v_norm_mix_g, v_norm_mlp_g, v_conv_b, v_b_rg_a, v_b_rg_x, v_a_param, v_b_pool, v_pool_scale,
               v_final_g, v_b_ada, v_conv_w))
    dmod_all, loss_tile = fin[4 * N_SMALL_PARAMS], fin[4 * N_SMALL_PARAMS + 1]
    dmod_cols = lax.dynamic_slice(dmod_all.reshape(N_DEV, 6 * D), (0, me * n_ada), (N_DEV, n_ada))
    res["w_ada"] = [t.reshape(w_ada.shape) for t in _ada_bwd_adam(c_all, dmod_cols, w_ada[0], m_w_ada[0], v_w_ada[0])]

    def final_shape(k, t):
        if k == "final_g":
            return t.reshape(D)
        if k == "b_ada":
            return t.reshape(1, 6 * D)
        if k == "conv_w":
            return lax.dynamic_slice(t, (0, me * (D // N_DEV)), (4, D // N_DEV)).reshape(conv_w.shape)
        return t

    for i, k in enumerate(small_names):
        res[k] = [final_shape(k, fin[which * N_SMALL_PARAMS + i]) for which in range(4)]
    order = ["norm_mix_g", "norm_mlp_g", "w_ada", "b_ada", "w_in", "conv_w", "conv_b", "w_rg_a", "b_rg_a", "w_rg_x",
             "b_rg_x", "a_param", "w_branch_a", "w_pool", "b_pool", "pool_scale", "w_branch_b", "w_out", "w_up",
             "w_down", "final_g"]
    outs = [loss_tile[0, 0], grad_x.reshape(x.shape)]
    for which in range(4):
        for k in order:
            outs.append(res[k][which])
    return tuple(outs)
```

```python
import functools

import jax
import jax.numpy as jnp
from jax import lax
from jax.experimental import pallas as pl
from jax.experimental.pallas import tpu as pltpu

F32 = jnp.float32
BF16 = jnp.bfloat16
MESH = pl.DeviceIdType.MESH

N_DEV = 8
D = 1024
N_GROUPS = 4
GW = D // N_GROUPS
D_IN = 5 * D
D_FF = 4 * D
POOL_WINDOWS = (2, 4, 8, 16)
HALO_X = 8
HALO_U = 16
EPS = 1e-6
C_RG = 8.0
ADAM_LR, ADAM_B1, ADAM_B2, ADAM_EPS, ADAM_WD, ADAM_STEP = 0.001, 0.9, 0.999, 1e-08, 0.01, 10

V7X_VMEM_LIMIT = 56 * 1024 * 1024

V_CONV_W, V_CONV_B, V_B_RG_A, V_B_RG_X, V_A_PARAM, V_B_POOL, V_POOL_SCALE, V_G1, V_G2, V_GF = 0, 4, 5, 6, 7, 8, 9, 10, 11, 12
M_SH1, M_SC1, M_GT1, M_SH2, M_SC2, M_GT2 = 0, 1, 2, 3, 4, 5

TM_PROJ = 512
TM_MIX = 256
TM_BRANCH = 256
TM_MLP = 512
TM_MLP_BWD = 256
TS_WGRAD = 1024


def _params(semantics):
    return pltpu.CompilerParams(dimension_semantics=semantics, vmem_limit_bytes=V7X_VMEM_LIMIT)


def _resident(shape):
    return pl.BlockSpec(shape, lambda *_: (0,) * len(shape), pipeline_mode=pl.Buffered(1))


def _dot(a, b):
    return jnp.dot(a, b, preferred_element_type=F32)


def _dot_nt(a, b):
    return lax.dot_general(a, b, (((1,), (1,)), ((), ())), preferred_element_type=F32)


def _dot_tn(a, b):
    return lax.dot_general(a, b, (((0,), (0,)), ((), ())), preferred_element_type=F32)


def _sigmoid(x):
    return 0.5 * jnp.tanh(0.5 * x) + 0.5


def _sigmoid_tail(x):
    return 1.0 / (1.0 + jnp.exp(-x))


def _gelu_and_grad(x):
    k = 0.7978845608028654
    x2 = x * x
    t = jnp.tanh(k * (x + 0.044715 * x * x2))
    g = 0.5 * x * (1.0 + t)
    dg = 0.5 * (1.0 + t) + 0.5 * x * (1.0 - t * t) * (k * (1.0 + 3.0 * 0.044715 * x2))
    return g, dg


def _softplus(a):
    e = jnp.exp(-jnp.abs(a))
    u = 1.0 + e
    log1p_e = jnp.where(u == 1.0, e, jnp.log(u) * e / jnp.where(u == 1.0, 1.0, u - 1.0))
    return jnp.maximum(a, 0.0) + log1p_e


def _neg_expm1(z):
    series = -(z * (1.0 + z * (0.5 + z * (1.0 / 6.0 + z * (1.0 / 24.0 + z * (1.0 / 120.0))))))
    return jnp.where(z > -0.1, series, 1.0 - jnp.exp(z))


def _shift_down(x, k):
    return pltpu.roll(x, k, 0)


def _shift_up(x, k):
    return pltpu.roll(x, x.shape[0] - k, 0)


def _rglru_gates(xr, w_a, w_x, b_a, b_x, a_param, is_t0):
    xb = xr.astype(BF16)
    ra = _sigmoid(_dot(xb, w_a) + b_a)
    ri = _sigmoid(_dot(xb, w_x) + b_x)
    sp = _softplus(a_param)
    log_a = (-C_RG) * ra * sp
    a = jnp.exp(log_a)
    mult = jnp.where(is_t0, 1.0, jnp.sqrt(_neg_expm1(2.0 * log_a)))
    return ra, ri, sp, a, mult


SUBLANES = 8


LANES = 128


def _scan_strip(a, b, carry, scr, down):
    t = b.shape[0]
    g = t // SUBLANES
    a3 = a.reshape(g, SUBLANES, LANES)
    b3 = b.reshape(g, SUBLANES, LANES)
    sub = lax.broadcasted_iota(jnp.int32, (g, SUBLANES, LANES), 1)
    for k in (1, 2, 4):
        keep = sub >= k if down else sub < SUBLANES - k
        shift = k if down else SUBLANES - k
        b3 = b3 + a3 * jnp.where(keep, pltpu.roll(b3, shift, 1), 0.0)
        a3 = a3 * jnp.where(keep, pltpu.roll(a3, shift, 1), 1.0)
    scr[0] = a3.reshape(t, LANES)
    scr[1] = b3.reshape(t, LANES)
    end_row = SUBLANES - 1 if down else 0
    ag = scr[0, pl.ds(end_row, g, stride=SUBLANES), :]
    bg = scr[1, pl.ds(end_row, g, stride=SUBLANES), :]
    rg = lax.broadcasted_iota(jnp.int32, (g, LANES), 0)
    edge = 0 if down else g - 1
    bg = bg + jnp.where(rg == edge, ag * carry, 0.0)
    k = 1
    while k < g:
        keep = rg >= k if down else rg < g - k
        shift = k if down else g - k
        bg = bg + ag * jnp.where(keep, pltpu.roll(bg, shift, 0), 0.0)
        if 2 * k < g:
            ag = ag * pltpu.roll(ag, shift, 0)
        k *= 2
    entering = jnp.where(rg != edge, pltpu.roll(bg, 1 if down else g - 1, 0), carry)
    for r in range(SUBLANES):
        scr[2, pl.ds(r, g, stride=SUBLANES), :] = entering
    return scr[1] + scr[0] * scr[2], bg[g - 1:g, :]


def _scan_strips(a, b, carry, scr, down):
    outs = [_scan_strip(a[:, c:c + LANES], b[:, c:c + LANES], carry[:, c:c + LANES], scr, down)
            for c in range(0, b.shape[1], LANES)]
    return jnp.concatenate([o[0] for o in outs], axis=1), jnp.concatenate([o[1] for o in outs], axis=1)


def _scan_down(a, b, carry, scr):
    return _scan_strips(a, b, carry, scr, True)


def _scan_up(m, b, carry, scr):
    return _scan_strips(m, b, carry, scr, False)[0]


def _conv_taps(x_ext):
    return [_shift_down(x_ext, 3 - j)[HALO_X:] if j < 3 else x_ext[HALO_X:] for j in range(4)]


def _proj_fwd(x, modr, vecs, w_in):
    s = x.shape[0]
    tm = min(TM_PROJ, s)

    def body(x_ref, mod_ref, vec_ref, w_ref, proj_ref, h1_ref):
        xv = x_ref[...]
        r = lax.rsqrt(jnp.mean(xv * xv, axis=-1, keepdims=True) + EPS)
        gain = vec_ref[V_G1:V_G1 + 1, :] * (1.0 + mod_ref[M_SC1:M_SC1 + 1, :])
        h = (xv * r * gain + mod_ref[M_SH1:M_SH1 + 1, :]).astype(BF16)
        h1_ref[...] = h
        for c in range(D_IN // D):
            proj_ref[:, c * D:(c + 1) * D] = _dot(h, w_ref[:, c * D:(c + 1) * D])

    return pl.pallas_call(
        body, name="proj_fwd", grid=(s // tm,),
        in_specs=[pl.BlockSpec((tm, D), lambda i: (i, 0)),
                  pl.BlockSpec((8, D), lambda i: (0, 0)),
                  pl.BlockSpec((16, D), lambda i: (0, 0)),
                  _resident((D, D_IN))],
        out_specs=[pl.BlockSpec((tm, D_IN), lambda i: (i, 0)),
                   pl.BlockSpec((tm, D), lambda i: (i, 0))],
        out_shape=[jax.ShapeDtypeStruct((s, D_IN), F32), jax.ShapeDtypeStruct((s, D), BF16)],
        compiler_params=_params(("parallel",)),
    )(x, modr, vecs, w_in)


def _mix_fwd(proj, vecs, w_rg_a, w_rg_x, w_pool):
    s = proj.shape[0]
    tm = min(TM_MIX, s)
    nb = s // tm

    def body(xh_ref, x_ref, y_ref, uh_ref, u_ref, vec_ref, wa_ref, wx_ref, wp_ref,
             xr_ref, hr_ref, za_ref, p_ref, pooled_ref, a_ref, mult_ref, ra_ref, ri_ref, carry_ref, scan_scr):
        i = pl.program_id(0)
        first = i == 0

        @pl.when(first)
        def _():
            carry_ref[...] = jnp.zeros_like(carry_ref)

        row = lax.broadcasted_iota(jnp.int32, (tm, GW), 0)
        is_t0 = jnp.logical_and(first, row == 0)
        t_glob = (row + i * tm + 1).astype(F32)
        for g in range(N_GROUPS):
            cs = slice(g * GW, (g + 1) * GW)
            vec = vec_ref[:, cs]
            xh = jnp.where(first, 0.0, xh_ref[:, cs])
            taps = _conv_taps(jnp.concatenate([xh, x_ref[:, cs]], axis=0))
            xr = vec[V_CONV_B:V_CONV_B + 1]
            for j in range(4):
                xr = xr + vec[V_CONV_W + j:V_CONV_W + j + 1] * taps[j]
            xr_ref[:, cs] = xr
            ra, ri, _, a, mult = _rglru_gates(
                xr, wa_ref[g], wx_ref[g], vec[V_B_RG_A:V_B_RG_A + 1], vec[V_B_RG_X:V_B_RG_X + 1],
                vec[V_A_PARAM:V_A_PARAM + 1], is_t0)
            a_ref[:, cs] = a
            mult_ref[:, cs] = mult
            ra_ref[:, cs] = ra.astype(BF16)
            ri_ref[:, cs] = ri.astype(BF16)
            h, last = _scan_down(a, xr * ri * mult, carry_ref[0:1, cs], scan_scr)
            hr_ref[:, cs] = h
            carry_ref[0:1, cs] = last
            ga, _ = _gelu_and_grad(y_ref[:, cs])
            za_ref[:, cs] = (ga * h).astype(BF16)
            uh = jnp.where(first, 0.0, uh_ref[:, cs])
            sm = jnp.concatenate([uh, u_ref[:, cs]], axis=0)
            k = 1
            while k < POOL_WINDOWS[g]:
                sm = sm + _shift_down(sm, k)
                k *= 2
            cnt = jnp.minimum(t_glob, float(POOL_WINDOWS[g]))
            p = (sm[HALO_U:] / cnt - u_ref[:, cs]).astype(BF16)
            p_ref[:, cs] = p
            pb = _dot(p, wp_ref[g]) + vec[V_B_POOL:V_B_POOL + 1]
            pooled_ref[:, cs] = (pb * vec[V_POOL_SCALE:V_POOL_SCALE + 1]).astype(BF16)

    col = lambda k: (lambda i: (i, k))
    wspec = pl.BlockSpec((N_GROUPS, GW, GW), lambda i: (0, 0, 0))
    return pl.pallas_call(
        body, name="mix_fwd", grid=(nb,),
        in_specs=[pl.BlockSpec((HALO_X, D), lambda i: (jnp.maximum(i * (tm // HALO_X) - 1, 0), 0)),
                  pl.BlockSpec((tm, D), col(0)),
                  pl.BlockSpec((tm, D), col(1)),
                  pl.BlockSpec((HALO_U, D), lambda i: (jnp.maximum(i * (tm // HALO_U) - 1, 0), 2)),
                  pl.BlockSpec((tm, D), col(2)),
                  pl.BlockSpec((16, D), lambda i: (0, 0)),
                  wspec, wspec, wspec],
        out_specs=[pl.BlockSpec((tm, D), lambda i: (i, 0))] * 9,
        out_shape=[jax.ShapeDtypeStruct((s, D), F32), jax.ShapeDtypeStruct((s, D), F32),
                   jax.ShapeDtypeStruct((s, D), BF16), jax.ShapeDtypeStruct((s, D), BF16),
                   jax.ShapeDtypeStruct((s, D), BF16),
                   jax.ShapeDtypeStruct((s, D), F32), jax.ShapeDtypeStruct((s, D), F32),
                   jax.ShapeDtypeStruct((s, D), BF16), jax.ShapeDtypeStruct((s, D), BF16)],
        scratch_shapes=[pltpu.VMEM((8, D), F32), pltpu.VMEM((3, tm, LANES), F32)],
        compiler_params=_params(("arbitrary",)),
    )(proj, proj, proj, proj, proj, vecs, w_rg_a, w_rg_x, w_pool)


def _branch_fwd(za, pooled, proj, x, modr, vecs, w_a, w_b, w_out):
    s = x.shape[0]
    tm = min(TM_BRANCH, s)

    def body(za_ref, pooled_ref, ga_ref, gb_ref, x_ref, mod_ref, vec_ref, wa_ref, wb_ref, wo_ref,
             ba_ref, bb_ref, merged_ref, o_ref, x2_ref, h2_ref):
        ba = _dot(za_ref[...], wa_ref[...])
        bb = _dot(pooled_ref[...], wb_ref[...])
        ba_ref[...] = ba.astype(BF16)
        bb_ref[...] = bb.astype(BF16)
        merged = (_sigmoid(ga_ref[...]) * ba + _sigmoid(gb_ref[...]) * bb).astype(BF16)
        merged_ref[...] = merged
        o = _dot(merged, wo_ref[...])
        o_ref[...] = o.astype(BF16)
        x2 = x_ref[...] + mod_ref[M_GT1:M_GT1 + 1, :] * o
        x2_ref[...] = x2
        r = lax.rsqrt(jnp.mean(x2 * x2, axis=-1, keepdims=True) + EPS)
        gain = vec_ref[V_G2:V_G2 + 1, :] * (1.0 + mod_ref[M_SC2:M_SC2 + 1, :])
        h2_ref[...] = (x2 * r * gain + mod_ref[M_SH2:M_SH2 + 1, :]).astype(BF16)

    tok = pl.BlockSpec((tm, D), lambda i: (i, 0))
    wspec = pl.BlockSpec((D, D), lambda i: (0, 0))
    sd = lambda dt: jax.ShapeDtypeStruct((s, D), dt)
    return pl.pallas_call(
        body, name="branch_fwd", grid=(s // tm,),
        in_specs=[tok, tok,
                  pl.BlockSpec((tm, D), lambda i: (i, 3)), pl.BlockSpec((tm, D), lambda i: (i, 4)),
                  tok, pl.BlockSpec((8, D), lambda i: (0, 0)), pl.BlockSpec((16, D), lambda i: (0, 0)),
                  wspec, wspec, wspec],
        out_specs=[tok] * 6,
        out_shape=[sd(BF16), sd(BF16), sd(BF16), sd(BF16), sd(F32), sd(BF16)],
        compiler_params=_params(("parallel",)),
    )(za, pooled, proj, proj, x, modr, vecs, w_a, w_b, w_out)


def _mlp_fwd(h2, x2, target, modr, vecs, w_up, w_down):
    s = x2.shape[0]
    tm = min(TM_MLP, s)

    def body(h2_ref, x2_ref, tgt_ref, mod_ref, vec_ref, wu_ref, wd_ref,
             ru_ref, dx3_ref, ddn_ref, small_ref):
        @pl.when(pl.program_id(0) == 0)
        def _():
            small_ref[...] = jnp.zeros_like(small_ref)

        h2 = h2_ref[...]
        dn = None
        for c in range(D_FF // D):
            cs = slice(c * D, (c + 1) * D)
            ru = jnp.maximum(_dot(h2, wu_ref[:, cs]), 0.0)
            ru_ref[:, cs] = ru.astype(BF16)
            part = _dot((ru * ru).astype(BF16), wd_ref[cs, :])
            dn = part if dn is None else dn + part
        gt2 = mod_ref[M_GT2:M_GT2 + 1, :]
        gf = vec_ref[V_GF:V_GF + 1, :]
        x3 = x2_ref[...] + gt2 * dn
        r3 = lax.rsqrt(jnp.mean(x3 * x3, axis=-1, keepdims=True) + EPS)
        n3 = x3 * r3
        err = n3 * gf - tgt_ref[...]
        dy = err * (1.0 / D)
        dn3 = dy * gf
        dx3 = r3 * (dn3 - n3 * jnp.mean(dn3 * n3, axis=-1, keepdims=True))
        dx3_ref[...] = dx3
        ddn_ref[...] = (dx3 * gt2).astype(BF16)
        small_ref[0:1, :] += jnp.sum(dy * n3, axis=0, keepdims=True)
        small_ref[1:2, :] += jnp.sum(dx3 * dn, axis=0, keepdims=True)
        small_ref[2:3, :] += (0.5 / D) * jnp.sum(err * err, axis=0, keepdims=True)

    tok = pl.BlockSpec((tm, D), lambda i: (i, 0))
    return pl.pallas_call(
        body, name="mlp_fwd", grid=(s // tm,),
        in_specs=[tok, tok, tok,
                  pl.BlockSpec((8, D), lambda i: (0, 0)), pl.BlockSpec((16, D), lambda i: (0, 0)),
                  _resident((D, D_FF)), _resident((D_FF, D))],
        out_specs=[pl.BlockSpec((tm, D_FF), lambda i: (i, 0)), tok, tok,
                   pl.BlockSpec((8, D), lambda i: (0, 0))],
        out_shape=[jax.ShapeDtypeStruct((s, D_FF), BF16), jax.ShapeDtypeStruct((s, D), F32),
                   jax.ShapeDtypeStruct((s, D), BF16), jax.ShapeDtypeStruct((8, D), F32)],
        compiler_params=_params(("arbitrary",)),
    )(h2, x2, target, modr, vecs, w_up, w_down)


def _mlp_bwd(d_dn, ru, x2, dx3, o, modr, vecs, w_up, w_down):
    s = x2.shape[0]
    tm = min(TM_MLP_BWD, s)

    def body(ddn_ref, ru_ref, x2_ref, dx3_ref, o_ref, mod_ref, vec_ref, wu_ref, wd_ref,
             dup_ref, dx2_ref, do_ref, small_ref):
        @pl.when(pl.program_id(0) == 0)
        def _():
            small_ref[...] = jnp.zeros_like(small_ref)

        ddn = ddn_ref[...]
        dh2 = None
        for c in range(D_FF // D):
            cs = slice(c * D, (c + 1) * D)
            dff = _dot_nt(ddn, wd_ref[cs, :])
            dup = (dff * (2.0 * ru_ref[:, cs].astype(F32))).astype(BF16)
            dup_ref[:, cs] = dup
            part = _dot_nt(dup, wu_ref[:, cs])
            dh2 = part if dh2 is None else dh2 + part
        x2 = x2_ref[...]
        r2 = lax.rsqrt(jnp.mean(x2 * x2, axis=-1, keepdims=True) + EPS)
        xn2 = x2 * r2
        gain = vec_ref[V_G2:V_G2 + 1, :] * (1.0 + mod_ref[M_SC2:M_SC2 + 1, :])
        dxn2 = dh2 * gain
        dx2 = dx3_ref[...] + r2 * (dxn2 - xn2 * jnp.mean(dxn2 * xn2, axis=-1, keepdims=True))
        dx2_ref[...] = dx2
        do_ref[...] = (dx2 * mod_ref[M_GT1:M_GT1 + 1, :]).astype(BF16)
        small_ref[0:1, :] += jnp.sum(dh2, axis=0, keepdims=True)
        small_ref[1:2, :] += jnp.sum(dh2 * xn2, axis=0, keepdims=True)
        small_ref[2:3, :] += jnp.sum(dx2 * o_ref[...].astype(F32), axis=0, keepdims=True)

    tok = pl.BlockSpec((tm, D), lambda i: (i, 0))
    wide = pl.BlockSpec((tm, D_FF), lambda i: (i, 0))
    return pl.pallas_call(
        body, name="mlp_bwd", grid=(s // tm,),
        in_specs=[tok, wide, tok, tok, tok,
                  pl.BlockSpec((8, D), lambda i: (0, 0)), pl.BlockSpec((16, D), lambda i: (0, 0)),
                  _resident((D, D_FF)), _resident((D_FF, D))],
        out_specs=[wide, tok, tok, pl.BlockSpec((8, D), lambda i: (0, 0))],
        out_shape=[jax.ShapeDtypeStruct((s, D_FF), BF16), jax.ShapeDtypeStruct((s, D), F32),
                   jax.ShapeDtypeStruct((s, D), BF16), jax.ShapeDtypeStruct((8, D), F32)],
        compiler_params=_params(("arbitrary",)),
    )(d_dn, ru, x2, dx3, o, modr, vecs, w_up, w_down)


def _branch_bwd(do, proj, ba, bb, w_a, w_b, w_out, dep):
    s = do.shape[0]
    tm = min(TM_BRANCH, s)

    def body(do_ref, ga_ref, gb_ref, ba_ref, bb_ref, wa_ref, wb_ref, wo_ref, dep_ref,
             dba_ref, dbb_ref, dg_ref, dza_ref, dpooled_ref):
        dmerged = _dot_nt(do_ref[...], wo_ref[...])
        sa = _sigmoid(ga_ref[...])
        sb = _sigmoid(gb_ref[...])
        dba = (dmerged * sa).astype(BF16)
        dbb = (dmerged * sb).astype(BF16)
        dba_ref[...] = dba
        dbb_ref[...] = dbb
        dg_ref[:, :D] = (dmerged * ba_ref[...].astype(F32) * sa * (1.0 - sa)).astype(BF16)
        dg_ref[:, D:] = (dmerged * bb_ref[...].astype(F32) * sb * (1.0 - sb)).astype(BF16)
        dza_ref[...] = _dot_nt(dba, wa_ref[...])
        dpooled_ref[...] = _dot_nt(dbb, wb_ref[...])

    tok = pl.BlockSpec((tm, D), lambda i: (i, 0))
    wspec = pl.BlockSpec((D, D), lambda i: (0, 0))
    sd = lambda dt: jax.ShapeDtypeStruct((s, D), dt)
    return pl.pallas_call(
        body, name="branch_bwd", grid=(s // tm,),
        in_specs=[tok, pl.BlockSpec((tm, D), lambda i: (i, 3)), pl.BlockSpec((tm, D), lambda i: (i, 4)),
                  tok, tok, wspec, wspec, wspec, pl.BlockSpec(memory_space=pl.ANY)],
        out_specs=[tok, tok, pl.BlockSpec((tm, 2 * D), lambda i: (i, 0)), tok, tok],
        out_shape=[sd(BF16), sd(BF16), jax.ShapeDtypeStruct((s, 2 * D), BF16), sd(F32), sd(F32)],
        compiler_params=_params(("parallel",)),
    )(do, proj, proj, ba, bb, w_a, w_b, w_out, dep)


def _mix_bwd(dza, dpooled, proj, xr, hr, p, gates, dgates, vecs, w_rg_a, w_rg_x, w_pool):
    s = xr.shape[0]
    tm = min(TM_MIX, s)
    nb = s // tm

    def body(dza_ref, dpooled_ref, xh_ref, x_ref, y_ref, xr_ref, hh_ref, hr_ref, p_ref,
             a_ref, mult_ref, ra_ref, ri_ref, dg_ref, vec_ref, wa_ref, wx_ref, wp_ref,
             dproj_ref, dwa_ref, dwx_ref, dwp_ref, small_ref,
             scan_carry, dxr_carry, q_carry, scan_scr):
        i = pl.program_id(0)
        bi = nb - 1 - i
        first_t = bi == 0

        @pl.when(i == 0)
        def _():
            scan_carry[...] = jnp.zeros_like(scan_carry)
            dxr_carry[...] = jnp.zeros_like(dxr_carry)
            q_carry[...] = jnp.zeros_like(q_carry)
            dwa_ref[...] = jnp.zeros_like(dwa_ref)
            dwx_ref[...] = jnp.zeros_like(dwx_ref)
            dwp_ref[...] = jnp.zeros_like(dwp_ref)
            small_ref[...] = jnp.zeros_like(small_ref)

        row = lax.broadcasted_iota(jnp.int32, (tm, GW), 0)
        is_t0 = jnp.logical_and(first_t, row == 0)
        t_glob = (row + bi * tm + 1).astype(F32)
        colsum = lambda v: jnp.sum(v, axis=0, keepdims=True)
        for g in range(N_GROUPS):
            cs = slice(g * GW, (g + 1) * GW)
            vec = vec_ref[:, cs]
            xr = xr_ref[:, cs]
            hr = hr_ref[:, cs]
            dza = dza_ref[:, cs]
            ga, dga = _gelu_and_grad(y_ref[:, cs])
            dproj_ref[:, D + g * GW:D + (g + 1) * GW] = (dza * hr * dga).astype(BF16)
            dhr = dza * ga
            a = a_ref[:, cs]
            mult = mult_ref[:, cs]
            ra = ra_ref[:, cs].astype(F32)
            ri = ri_ref[:, cs].astype(F32)
            sp = _softplus(vec[V_A_PARAM:V_A_PARAM + 1])
            m = jnp.where(row == tm - 1, 1.0, _shift_up(a, 1))
            gsum = _scan_up(m, dhr, scan_carry[0:1, cs], scan_scr)
            scan_carry[0:1, cs] = a[0:1, :] * gsum[0:1, :]
            hh = jnp.where(first_t, 0.0, hh_ref[:, cs])
            hprev = _shift_down(jnp.concatenate([hh, hr], axis=0), 1)[8:]
            da = gsum * hprev
            dmult = jnp.where(is_t0, 0.0, gsum * xr * ri)
            dlog_a = da * a - dmult * a * a / mult
            dri = gsum * xr * mult
            dxr = gsum * ri * mult
            small_ref[7:8, cs] += colsum((-C_RG) * ra * dlog_a)
            dpa = (((-C_RG) * sp) * dlog_a * ra * (1.0 - ra))
            dpx = dri * ri * (1.0 - ri)
            small_ref[5:6, cs] += colsum(dpa)
            small_ref[6:7, cs] += colsum(dpx)
            dpa = dpa.astype(BF16)
            dpx = dpx.astype(BF16)
            xrb = xr.astype(BF16)
            dwa_ref[g] += _dot_tn(xrb, dpa)
            dwx_ref[g] += _dot_tn(xrb, dpx)
            dxr = dxr + _dot_nt(dpa, wa_ref[g]) + _dot_nt(dpx, wx_ref[g])
            small_ref[4:5, cs] += colsum(dxr)
            xh = jnp.where(first_t, 0.0, xh_ref[:, cs])
            taps = _conv_taps(jnp.concatenate([xh, x_ref[:, cs]], axis=0))
            dxr_ext = jnp.concatenate([dxr, dxr_carry[:, cs]], axis=0)
            dx = vec[V_CONV_W + 3:V_CONV_W + 4] * dxr
            for j in range(4):
                small_ref[j:j + 1, cs] += colsum(dxr * taps[j])
                if j < 3:
                    dx = dx + vec[V_CONV_W + j:V_CONV_W + j + 1] * _shift_up(dxr_ext, 3 - j)[:tm]
            dxr_carry[:, cs] = dxr[0:8, :]
            dproj_ref[:, cs] = dx.astype(BF16)
            pg = p_ref[:, cs]
            dpooled = dpooled_ref[:, cs]
            pb = _dot(pg, wp_ref[g]) + vec[V_B_POOL:V_B_POOL + 1]
            small_ref[9:10, cs] += colsum(dpooled * pb)
            dpb = dpooled * vec[V_POOL_SCALE:V_POOL_SCALE + 1]
            small_ref[8:9, cs] += colsum(dpb)
            dpbb = dpb.astype(BF16)
            dwp_ref[g] += _dot_tn(pg, dpbb)
            dp = _dot_nt(dpbb, wp_ref[g])
            q = dp / jnp.minimum(t_glob, float(POOL_WINDOWS[g]))
            sm = jnp.concatenate([q, q_carry[:, cs]], axis=0)
            k = 1
            while k < POOL_WINDOWS[g]:
                sm = sm + _shift_up(sm, k)
                k *= 2
            q_carry[:, cs] = q[0:HALO_U, :]
            dproj_ref[:, 2 * D + g * GW:2 * D + (g + 1) * GW] = (sm[:tm] - dp).astype(BF16)
        dproj_ref[:, 3 * D:] = dg_ref[...]

    rev = lambda i: nb - 1 - i
    tok = pl.BlockSpec((tm, D), lambda i: (rev(i), 0))
    col = lambda k: pl.BlockSpec((tm, D), lambda i: (rev(i), k))
    halo8 = lambda k: pl.BlockSpec((8, D), lambda i: (jnp.maximum(rev(i) * (tm // 8) - 1, 0), k))
    wspec = pl.BlockSpec((N_GROUPS, GW, GW), lambda i: (0, 0, 0))
    wshape = jax.ShapeDtypeStruct((N_GROUPS, GW, GW), F32)
    return pl.pallas_call(
        body, name="mix_bwd", grid=(nb,),
        in_specs=[tok, tok, halo8(0), col(0), col(1), tok, halo8(0), tok, tok, tok, tok, tok, tok,
                  pl.BlockSpec((tm, 2 * D), lambda i: (rev(i), 0)),
                  pl.BlockSpec((16, D), lambda i: (0, 0)), wspec, wspec, wspec],
        out_specs=[pl.BlockSpec((tm, D_IN), lambda i: (rev(i), 0)), wspec, wspec, wspec,
                   pl.BlockSpec((16, D), lambda i: (0, 0))],
        out_shape=[jax.ShapeDtypeStruct((s, D_IN), BF16), wshape, wshape, wshape,
                   jax.ShapeDtypeStruct((16, D), F32)],
        scratch_shapes=[pltpu.VMEM((8, D), F32), pltpu.VMEM((8, D), F32), pltpu.VMEM((HALO_U, D), F32),
                        pltpu.VMEM((3, tm, LANES), F32)],
        compiler_params=_params(("arbitrary",)),
    )(dza, dpooled, proj, proj, proj, xr, hr, hr, p, *gates, dgates, vecs, w_rg_a, w_rg_x, w_pool)


def _proj_bwd(dproj, x, dx2, modr, vecs, w_in):
    s = x.shape[0]
    tm = min(TM_PROJ, s)

    def body(dp_ref, x_ref, dx2_ref, mod_ref, vec_ref, w_ref, gx_ref, small_ref):
        @pl.when(pl.program_id(0) == 0)
        def _():
            small_ref[...] = jnp.zeros_like(small_ref)

        dh1 = None
        for c in range(D_IN // D):
            cs = slice(c * D, (c + 1) * D)
            part = _dot_nt(dp_ref[:, cs], w_ref[:, cs])
            dh1 = part if dh1 is None else dh1 + part
        xv = x_ref[...]
        r1 = lax.rsqrt(jnp.mean(xv * xv, axis=-1, keepdims=True) + EPS)
        xn1 = xv * r1
        gain = vec_ref[V_G1:V_G1 + 1, :] * (1.0 + mod_ref[M_SC1:M_SC1 + 1, :])
        dxn1 = dh1 * gain
        gx_ref[...] = dx2_ref[...] + r1 * (dxn1 - xn1 * jnp.mean(dxn1 * xn1, axis=-1, keepdims=True))
        small_ref[0:1, :] += jnp.sum(dh1, axis=0, keepdims=True)
        small_ref[1:2, :] += jnp.sum(dh1 * xn1, axis=0, keepdims=True)

    tok = pl.BlockSpec((tm, D), lambda i: (i, 0))
    return pl.pallas_call(
        body, name="proj_bwd", grid=(s // tm,),
        in_specs=[pl.BlockSpec((tm, D_IN), lambda i: (i, 0)), tok, tok,
                  pl.BlockSpec((8, D), lambda i: (0, 0)), pl.BlockSpec((16, D), lambda i: (0, 0)),
                  _resident((D, D_IN))],
        out_specs=[tok, pl.BlockSpec((8, D), lambda i: (0, 0))],
        out_shape=[jax.ShapeDtypeStruct((s, D), F32), jax.ShapeDtypeStruct((8, D), F32)],
        compiler_params=_params(("arbitrary",)),
    )(dproj, x, dx2, modr, vecs, w_in)


def _wgrad(a, b, name, square_a=False, dep=None):
    s, ka = a.shape
    n = b.shape[1]
    tka = ka if ka <= 1024 else ka // 2
    tn = n if n <= 1024 else n // 2
    ts = min(TS_WGRAD, s)
    ns = s // ts
    nc = 512
    deps = [] if dep is None else [dep]

    def body(a_ref, b_ref, *refs):
        out_ref, acc_ref = refs[-2:]
        t = pl.program_id(2)

        @pl.when(t == 0)
        def _():
            acc_ref[...] = jnp.zeros_like(acc_ref)

        av = a_ref[...]
        if square_a:
            af = av.astype(F32)
            av = (af * af).astype(BF16)
        for c in range(tn // nc):
            cs = slice(c * nc, (c + 1) * nc)
            acc_ref[:, cs] += _dot_tn(av, b_ref[:, cs])

        @pl.when(t == ns - 1)
        def _():
            out_ref[...] = acc_ref[...].astype(BF16)

    return pl.pallas_call(
        body, name=name, grid=(ka // tka, n // tn, ns),
        in_specs=[pl.BlockSpec((ts, tka), lambda i, j, t: (t, i)),
                  pl.BlockSpec((ts, tn), lambda i, j, t: (t, j))] + [pl.BlockSpec(memory_space=pl.ANY)] * len(deps),
        out_specs=pl.BlockSpec((tka, tn), lambda i, j, t: (i, j)),
        out_shape=jax.ShapeDtypeStruct((ka, n), BF16),
        scratch_shapes=[pltpu.VMEM((tka, tn), F32)],
        compiler_params=_params(("parallel", "parallel", "arbitrary")),
    )(a, b, *deps)


def _window(ref, kind, idx, size):
    start = pl.multiple_of(idx * size, size)
    if kind == 0:
        return ref.at[pl.ds(start, size)]
    if kind == 1:
        return ref.at[:, pl.ds(start, size)]
    return ref.at[:, :, pl.ds(start, size)]


def _mesh_place():
    x, y, c = lax.axis_index("x"), lax.axis_index("y"), lax.axis_index("c")
    return x, y, c, 4 * x + 2 * y + c


def _peer(x, y, c, q):
    px = 1 - x if q & 4 else x
    py = 1 - y if q & 2 else y
    pc = 1 - c if q & 1 else c
    return (px, py, pc), 4 * px + 2 * py + pc


def _all_gather(shards, kinds, name, dep=None):
    n = len(shards)
    deps = [] if dep is None else [dep]
    full_shapes = []
    for sh, kind in zip(shards, kinds):
        dims = list(sh.shape)
        dims[kind] *= N_DEV
        full_shapes.append(jax.ShapeDtypeStruct(tuple(dims), sh.dtype))

    def body(*refs):
        ins, outs = refs[:n], refs[n + len(deps):2 * n + len(deps)]
        send_sems, recv_sems, local_sems = refs[2 * n + len(deps):]
        x, y, c, me = _mesh_place()
        sends, recvs, locals_ = [], [], []
        for k in range(n):
            size = shards[k].shape[kinds[k]]
            mine = _window(outs[k], kinds[k], me, size)
            lc = pltpu.make_async_copy(ins[k], mine, local_sems.at[k])
            lc.start()
            locals_.append(lc)
            for q in range(1, N_DEV):
                peer, peer_idx = _peer(x, y, c, q)
                cp = pltpu.make_async_remote_copy(
                    src_ref=ins[k], dst_ref=mine, send_sem=send_sems.at[k, q], recv_sem=recv_sems.at[k, q],
                    device_id=peer, device_id_type=MESH)
                cp.start()
                sends.append(cp)
                recvs.append(pltpu.make_async_remote_copy(
                    src_ref=ins[k], dst_ref=_window(outs[k], kinds[k], peer_idx, size),
                    send_sem=send_sems.at[k, q], recv_sem=recv_sems.at[k, q],
                    device_id=peer, device_id_type=MESH))
        for cp in recvs:
            cp.wait_recv()
        for cp in sends:
            cp.wait_send()
        for lc in locals_:
            lc.wait()

    any_spec = pl.BlockSpec(memory_space=pl.ANY)
    return pl.pallas_call(
        body, name=name,
        in_specs=[any_spec] * (n + len(deps)), out_specs=[any_spec] * n, out_shape=full_shapes,
        scratch_shapes=[pltpu.SemaphoreType.DMA((n, N_DEV)), pltpu.SemaphoreType.DMA((n, N_DEV)),
                        pltpu.SemaphoreType.DMA((n,))],
    )(*shards, *deps)


_HBM = pl.BlockSpec(memory_space=pltpu.HBM)
_SEM = pl.BlockSpec(memory_space=pltpu.SEMAPHORE)
_EFFECT = pltpu.SideEffectType.DATAFLOW_SIDE_EFFECTING


N_NEAR = 4


def _near(x, y, c):
    out = [((x, y, 1 - c), 4 * x + 2 * y + 1 - c)]
    for j in (1, 2, 3):
        px = 1 - x if j & 2 else x
        py = 1 - y if j & 1 else y
        out.append(((px, py, c), 4 * px + 2 * py + c))
    return out


def _remote(src, dst, send_sems, recv_sems, slot, device):
    return pltpu.make_async_remote_copy(src_ref=src, dst_ref=dst, send_sem=send_sems.at[slot], recv_sem=recv_sems.at[slot],
                                        device_id=device, device_id_type=MESH)


def _split_call(name, arrays, sems_in, n_new_sems, after, emit, n_local=0):
    na, ns, nn = len(arrays), len(sems_in), len(n_new_sems)

    def body(*refs):
        token = refs[na + ns + 1 + nn + na]
        emit(refs[:na], refs[na:na + ns], refs[na + ns + 1:na + ns + 1 + nn], *refs[na + ns + 2 + nn + na:])
        token[...] = jnp.zeros_like(token)

    outs = pl.pallas_call(
        body, name=name,
        out_shape=(*[pltpu.SemaphoreType.DMA((m,)) for m in n_new_sems],
                   *[pltpu.HBM(a.shape, a.dtype) for a in arrays], jax.ShapeDtypeStruct((8, 128), F32)),
        in_specs=[_HBM] * na + [_SEM] * ns + [pl.BlockSpec(memory_space=pl.ANY)],
        out_specs=(*[_SEM] * nn, *[_HBM] * na, pl.BlockSpec(memory_space=pltpu.VMEM)),
        input_output_aliases={i: nn + i for i in range(na)},
        scratch_shapes=[pltpu.SemaphoreType.DMA((n_local,))] if n_local else [],
        compiler_params=pltpu.CompilerParams(has_side_effects=_EFFECT),
    )(*[pltpu.with_memory_space_constraint(a, pltpu.HBM) for a in arrays], *sems_in, after)
    return list(outs[:nn]), list(outs[nn:nn + na]), outs[-1]


class _Gather:
    def __init__(self, shards, kinds, after, name):
        self.n, self.kinds, self.name = len(shards), kinds, name
        self.sizes = [s.shape[k] for s, k in zip(shards, kinds)]
        n = self.n
        lands = []
        for s, k in zip(shards, kinds):
            dims = list(s.shape)
            dims[k] *= N_DEV
            lands.append(lax.empty(tuple(dims), s.dtype))

        def emit(arr, _, new, local):
            x, y, c, me = _mesh_place()
            own = []
            for k in range(n):
                mine = _window(arr[n + k], kinds[k], me, self.sizes[k])
                own.append(pltpu.make_async_copy(arr[k], mine, local.at[k]))
                own[-1].start()
                for j, (dev, _) in enumerate(_near(x, y, c)):
                    _remote(arr[k], mine, new[0], new[1], k * N_NEAR + j, dev).start()
            for cp in own:
                cp.wait()

        self.sems, self.arrays, self.token = _split_call(name + "_start", [*shards, *lands], [], [n * N_NEAR] * 2,
                                                         after, emit, n_local=n)

    def forward(self, after):
        n, kinds, sizes = self.n, self.kinds, self.sizes

        def emit(arr, old, new):
            x, y, c, _ = _mesh_place()
            near = _near(x, y, c)
            for k in range(n):
                for j in (1, 2, 3):
                    dev, idx = near[j]
                    landed = _window(arr[n + k], kinds[k], idx, sizes[k])
                    _remote(arr[k], landed, old[0], old[1], k * N_NEAR + j, dev).wait_recv()
                    _remote(landed, landed, new[0], new[1], k * N_NEAR + j, near[0][0]).start()

        new, self.arrays, self.token = _split_call(self.name + "_forward", self.arrays, self.sems, [n * N_NEAR] * 2,
                                                   after, emit)
        self.sems = [*self.sems, *new]

    def finish(self, after):
        n, kinds, sizes = self.n, self.kinds, self.sizes

        def emit(arr, old, _):
            x, y, c, me = _mesh_place()
            near = _near(x, y, c)
            other_core = near[0][0]
            for k in range(n):
                win = lambda idx: _window(arr[n + k], kinds[k], idx, sizes[k])
                for j, (dev, idx) in enumerate(near):
                    _remote(arr[k], win(me), old[0], old[1], k * N_NEAR + j, dev).wait_send()
                _remote(arr[k], win(near[0][1]), old[0], old[1], k * N_NEAR, other_core).wait_recv()
                for j in (1, 2, 3):
                    idx = near[j][1]
                    _remote(win(idx), win(idx), old[2], old[3], k * N_NEAR + j, other_core).wait_send()
                    _remote(arr[k], win(idx + 1 - 2 * c), old[2], old[3], k * N_NEAR + j, other_core).wait_recv()

        _, arrays, _ = _split_call(self.name + "_finish", self.arrays, self.sems, [], after, emit)
        return arrays[n:]


class _Scatter:
    def __init__(self, partials, kinds, after, name):
        self.n, self.kinds, self.name, self.partials = len(partials), kinds, name, partials
        self.sizes = [p.shape[k] // N_DEV for p, k in zip(partials, kinds)]
        n, sizes = self.n, self.sizes
        self.slot_shapes = []
        for p, k, size in zip(partials, kinds, sizes):
            dims = list(p.shape)
            dims[k] = size
            self.slot_shapes.append((N_NEAR, *dims))
        slots = [lax.empty(sh, p.dtype) for sh, p in zip(self.slot_shapes, partials)]

        def emit(arr, _, new):
            x, y, c, _ = _mesh_place()
            near = _near(x, y, c)
            for k in range(n):
                for j in range(N_NEAR):
                    owner = near[j][1] if j == 0 else near[j][1] + 1 - 2 * c
                    _remote(_window(arr[k], kinds[k], owner, sizes[k]), arr[n + k].at[j], new[0], new[1],
                            k * N_NEAR + j, near[0][0]).start()

        self.sems, self.arrays, self.token = _split_call(name + "_start", [*partials, *slots], [], [n * N_NEAR] * 2,
                                                         after, emit)

    def combine_and_send(self, own4, after):
        n, kinds, sizes = self.n, self.kinds, self.sizes

        def emit_wait(arr, old, _):
            x, y, c, _ = _mesh_place()
            near = _near(x, y, c)
            for k in range(n):
                for j in range(N_NEAR):
                    owner = near[j][1] if j == 0 else near[j][1] + 1 - 2 * c
                    cp = _remote(_window(arr[k], kinds[k], owner, sizes[k]), arr[n + k].at[j], old[0], old[1],
                                 k * N_NEAR + j, near[0][0])
                    cp.wait_send()
                    cp.wait_recv()

        _, arrays, _ = _split_call(self.name + "_landed", self.arrays, self.sems, [], after, emit_wait)
        chip_sums = _chip_sums(arrays[:n], arrays[n:], kinds, sizes, own4, self.name + "_combine")
        arrivals = [lax.empty((N_NEAR - 1, *sh[1:]), p.dtype) for sh, p in zip(self.slot_shapes, self.partials)]

        def emit_send(arr, _, new):
            x, y, c, _ = _mesh_place()
            near = _near(x, y, c)
            for k in range(n):
                for j in (1, 2, 3):
                    _remote(arr[k].at[j], arr[n + k].at[j - 1], new[0], new[1], k * N_NEAR + j, near[j][0]).start()

        self.sems, self.arrays, self.token = _split_call(self.name + "_send", [*chip_sums, *arrivals], [],
                                                         [n * N_NEAR] * 2, own4, emit_send)

    def finish(self, after):
        n = self.n

        def emit(arr, old, _):
            x, y, c, _ = _mesh_place()
            near = _near(x, y, c)
            for k in range(n):
                for j in (1, 2, 3):
                    cp = _remote(arr[k].at[j], arr[n + k].at[j - 1], old[0], old[1], k * N_NEAR + j, near[j][0])
                    cp.wait_send()
                    cp.wait_recv()

        _, arrays, _ = _split_call(self.name + "_finish", self.arrays, self.sems, [], after, emit)
        return arrays[:n], arrays[n:]


def _chip_sums(partials, slots, kinds, sizes, own4, name):
    n = len(partials)

    def body(own_ref, *refs):
        for k in range(n):
            refs[2 * n + k][...] = (refs[k][...].astype(F32) + refs[n + k][...].astype(F32)).astype(BF16)

    in_specs, slot_specs = [], []
    for p, s, kind, size in zip(partials, slots, kinds, sizes):
        block = list(p.shape)
        block[kind] = size
        nd = len(block)
        in_specs.append(pl.BlockSpec(tuple(block), functools.partial(
            lambda j, own, kind, nd: tuple(own[j] if d == kind else 0 for d in range(nd)), kind=kind, nd=nd)))
        slot_specs.append(pl.BlockSpec((None, *block), functools.partial(
            lambda j, own, nd: (j,) + (0,) * nd, nd=nd)))
    return pl.pallas_call(
        body, name=name,
        grid_spec=pltpu.PrefetchScalarGridSpec(num_scalar_prefetch=1, grid=(N_NEAR,),
                                               in_specs=in_specs + slot_specs, out_specs=slot_specs),
        out_shape=[jax.ShapeDtypeStruct(s.shape, s.dtype) for s in slots],
        compiler_params=_params(("arbitrary",)),
    )(own4, *partials, *slots)


def _after(small, token):
    return small + token[0:1, 0:1].astype(small.dtype)


def _silu(c):
    return c * _sigmoid_tail(c)


def _ada_fwd(c_all, w_ada, b_ada_cols):
    def body(c_ref, w_ref, b_ref, out_ref):
        out_ref[...] = jnp.dot(_silu(c_ref[...]), w_ref[...], preferred_element_type=F32,
                               precision=lax.Precision.HIGHEST) + b_ref[...]

    return pl.pallas_call(
        body, name="ada_fwd", out_shape=jax.ShapeDtypeStruct((N_DEV, w_ada.shape[1]), F32),
    )(c_all, w_ada, b_ada_cols)


def _adam(w, g, m, v):
    m = ADAM_B1 * m + (1.0 - ADAM_B1) * g
    v = ADAM_B2 * v + (1.0 - ADAM_B2) * (g * g)
    m_hat = m / (1.0 - ADAM_B1 ** ADAM_STEP)
    v_hat = v / (1.0 - ADAM_B2 ** ADAM_STEP)
    delta = -ADAM_LR * (m_hat / (jnp.sqrt(v_hat) + ADAM_EPS) + ADAM_WD * w)
    return delta, m, v


def _ada_bwd_adam(c_all, dmod_cols, w, m, v):
    def body(c_ref, d_ref, w_ref, m_ref, v_ref, g_ref, delta_ref, nm_ref, nv_ref):
        g = lax.dot_general(_silu(c_ref[...]), d_ref[...], (((0,), (0,)), ((), ())),
                            preferred_element_type=F32, precision=lax.Precision.HIGHEST)
        g_ref[...] = g
        delta_ref[...], nm_ref[...], nv_ref[...] = _adam(w_ref[...], g, m_ref[...], v_ref[...])

    sd = jax.ShapeDtypeStruct(w.shape, F32)
    return pl.pallas_call(body, name="ada_bwd_adam", out_shape=[sd] * 4,
                          compiler_params=pltpu.CompilerParams(vmem_limit_bytes=V7X_VMEM_LIMIT),
                          )(c_all, dmod_cols, w, m, v)


def _sum_slots_adam(chip_sums, arrivals, w, m, v, name):
    r, cdim = w.shape
    tr = min(r, 256)

    def body(c_ref, a_ref, w_ref, m_ref, v_ref, g_ref, delta_ref, nm_ref, nv_ref):
        g = c_ref[...].astype(F32)
        for j in (1, 2, 3):
            g = g + a_ref[j - 1].astype(F32)
        g_ref[...] = g
        delta_ref[...], nm_ref[...], nv_ref[...] = _adam(w_ref[...], g, m_ref[...], v_ref[...])

    blk = pl.BlockSpec((tr, cdim), lambda i: (i, 0))
    sd = jax.ShapeDtypeStruct((r, cdim), F32)
    return pl.pallas_call(
        body, name=name, grid=(r // tr,),
        in_specs=[pl.BlockSpec((None, tr, cdim), lambda i: (0, i, 0)),
                  pl.BlockSpec((N_NEAR - 1, tr, cdim), lambda i: (0, i, 0)), blk, blk, blk],
        out_specs=[blk] * 4, out_shape=[sd] * 4,
        compiler_params=_params(("parallel",)),
    )(chip_sums, arrivals, w, m, v)


N_SMALL = 40
N_SMALL_PARAMS = 11


def _pack_vecs(conv_w_full, rows):
    def body(cw_ref, *refs):
        out = refs[-1]
        out[...] = jnp.zeros_like(out)
        out[0:4, :] = cw_ref[0:4, :]
        for r, ref in enumerate(refs[:-1]):
            out[4 + r:5 + r, :] = ref[...]

    return pl.pallas_call(body, name="pack_vecs", out_shape=jax.ShapeDtypeStruct((16, D), F32))(conv_w_full, *rows)


def _small_finish(gathered, mod_all, vecs, ws, ms, vs):
    n = N_SMALL_PARAMS

    def body(g_ref, mod_ref, vec_ref, *refs):
        w_refs, m_refs, v_refs = refs[:n], refs[n:2 * n], refs[2 * n:3 * n]
        outs = refs[3 * n:]
        g1 = vec_ref[V_G1:V_G1 + 1, :]
        g2 = vec_ref[V_G2:V_G2 + 1, :]
        zero = jnp.zeros((1, D), F32)
        dg1, dg2, dgf, loss_lanes = zero, zero, zero, zero
        mixer = jnp.zeros((16, D), F32)
        db_ada = jnp.zeros((6, D), F32)
        for b in range(N_DEV):
            gb = g_ref[b]
            mod = mod_ref[b]
            q1 = gb[33:34]
            q2 = gb[9:10]
            dmod = jnp.concatenate([gb[32:33], q1 * g1, gb[10:11], gb[8:9], q2 * g2, gb[1:2]], axis=0)
            outs[4 * n][b] = dmod
            db_ada = db_ada + dmod
            dg1 = dg1 + q1 * (1.0 + mod[M_SC1:M_SC1 + 1])
            dg2 = dg2 + q2 * (1.0 + mod[M_SC2:M_SC2 + 1])
            dgf = dgf + gb[0:1]
            loss_lanes = loss_lanes + gb[2:3]
            mixer = mixer + gb[16:32]
        d_a_param = mixer[7:8] * _sigmoid_tail(vec_ref[V_A_PARAM:V_A_PARAM + 1, :])
        grads = [dg1, dg2, mixer[4:5], mixer[5:6], mixer[6:7], d_a_param, mixer[8:9], mixer[9:10], dgf,
                 db_ada, mixer[0:4]]
        for k in range(n):
            outs[k][...] = grads[k]
            outs[n + k][...], outs[2 * n + k][...], outs[3 * n + k][...] = _adam(
                w_refs[k][...], grads[k], m_refs[k][...], v_refs[k][...])
        outs[4 * n + 1][...] = jnp.broadcast_to(jnp.sum(loss_lanes, axis=1, keepdims=True), (8, 128))

    shapes = [jax.ShapeDtypeStruct(w.shape, F32) for w in ws]
    return pl.pallas_call(
        body, name="small_finish",
        out_shape=shapes * 4 + [jax.ShapeDtypeStruct((N_DEV, 6, D), F32), jax.ShapeDtypeStruct((8, 128), F32)],
    )(gathered, mod_all, vecs, *ws, *ms, *vs)


def _pad_rows(a, rows):
    return jnp.pad(a, ((0, rows - a.shape[0]), (0, 0)))


def kernel(x, c, norm_mix_g, norm_mlp_g, w_ada, b_ada, w_in, conv_w, conv_b, w_rg_a, b_rg_a, w_rg_x, b_rg_x, a_param, w_branch_a, w_pool, b_pool, pool_scale, w_branch_b, w_out, w_up, w_down, final_g, loss_target, m_norm_mix_g, m_norm_mlp_g, m_w_ada, m_b_ada, m_w_in, m_conv_w, m_conv_b, m_w_rg_a, m_b_rg_a, m_w_rg_x, m_b_rg_x, m_a_param, m_w_branch_a, m_w_pool, m_b_pool, m_pool_scale, m_w_branch_b, m_w_out, m_w_up, m_w_down, m_final_g, v_norm_mix_g, v_norm_mlp_g, v_w_ada, v_b_ada, v_w_in, v_conv_w, v_conv_b, v_w_rg_a, v_b_rg_a, v_w_rg_x, v_b_rg_x, v_a_param, v_w_branch_a, v_w_pool, v_b_pool, v_pool_scale, v_w_branch_b, v_w_out, v_w_up, v_w_down, v_final_g):
    me = 4 * lax.axis_index("x") + 2 * lax.axis_index("y") + lax.axis_index("c")
    s = x.shape[1]
    x2d = x.reshape(s, D)
    target = loss_target.reshape(s, D)
    n_ada = w_ada.shape[2]

    sharded = dict(w_in=(w_in[0], 1), w_up=(w_up[0], 1), w_down=(w_down[0], 0), w_branch_a=(w_branch_a[0], 0),
                   w_branch_b=(w_branch_b[0], 0), w_out=(w_out[0], 0), w_rg_a=(w_rg_a[0], 1), w_rg_x=(w_rg_x[0], 1),
                   w_pool=(w_pool[0], 1))
    kind = {k: v[1] for k, v in sharded.items()}
    shard = {k: v[0].astype(BF16) for k, v in sharded.items()}

    conv_w_full, c_rows = _all_gather([_pad_rows(conv_w[0], 8), _pad_rows(c, 8)], [1, 0], "gather_c")
    c_all = c_rows.reshape(N_DEV, 8, D)[:, 0, :]
    b_ada_cols = lax.dynamic_slice(b_ada, (0, me * n_ada), (1, n_ada))
    mod_part = _ada_fwd(c_all, w_ada[0], b_ada_cols)
    mod_parts, = _all_gather([mod_part], [0], "gather_mod")

    first_names = ["w_in", "w_rg_a", "w_rg_x", "w_pool"]
    branch_names = ["w_branch_a", "w_branch_b", "w_out"]
    mlp_names = ["w_up", "w_down"]

    def gather(group, after, name):
        return _Gather([shard[k] for k in group], [kind[k] for k in group], after, name)

    g_first = gather(first_names, mod_parts, "gather_first")
    g_branch = gather(branch_names, g_first.token, "gather_branch")
    g_mlp = gather(mlp_names, g_branch.token, "gather_mlp")

    mod_all = jnp.transpose(mod_parts.reshape(N_DEV, N_DEV, n_ada), (1, 0, 2)).reshape(N_DEV, 6, D)
    mod_all = jnp.pad(mod_all, ((0, 0), (0, 2), (0, 0)))
    modr = lax.dynamic_index_in_dim(mod_all, me, 0, keepdims=False)
    vecs = _pack_vecs(conv_w_full, [conv_b, b_rg_a, b_rg_x, a_param, b_pool, pool_scale,
                                    norm_mix_g, norm_mlp_g, final_g.reshape(1, D)])
    vecs = _after(vecs, g_mlp.token)
    g_first.forward(vecs)
    wg = dict(zip(first_names, g_first.finish(g_first.token)))

    proj, h1 = _proj_fwd(x2d, modr, vecs, wg["w_in"])
    g_branch.forward(h1)
    xr, hr, za, p, pooled, *gates = _mix_fwd(proj, _after(vecs, g_branch.token), wg["w_rg_a"], wg["w_rg_x"], wg["w_pool"])
    g_mlp.forward(za)
    wg.update(zip(branch_names, g_branch.finish(g_mlp.token)))
    ba, bb, merged, o, x2, h2 = _branch_fwd(za, pooled, proj, x2d, modr, vecs,
                                            wg["w_branch_a"], wg["w_branch_b"], wg["w_out"])
    wg.update(zip(mlp_names, g_mlp.finish(h2)))
    ru, dx3, d_dn, small_f = _mlp_fwd(h2, x2, target, modr, vecs, wg["w_up"], wg["w_down"])

    near = _near(lax.axis_index("x"), lax.axis_index("y"), lax.axis_index("c"))
    own4 = jnp.stack([me, near[1][1], near[2][1], near[3][1]]).astype(jnp.int32)

    def scatter(group, partial, after, name):
        return _Scatter([partial[k] for k in group], [kind[k] for k in group], after, name)

    dup, dx2, do, small_m = _mlp_bwd(d_dn, ru, x2, dx3, o, modr, vecs, wg["w_up"], wg["w_down"])
    partial = dict(w_up=_wgrad(h2, dup, "wgrad_up"), w_down=_wgrad(ru, d_dn, "wgrad_down", square_a=True))
    s_mlp = scatter(mlp_names, partial, dx2, "scatter_mlp")

    dba, dbb, dgates, dza, dpooled = _branch_bwd(do, proj, ba, bb, wg["w_branch_a"], wg["w_branch_b"], wg["w_out"],
                                                 dep=s_mlp.token)
    s_mlp.combine_and_send(own4, dza)
    dproj, dw_rg_a, dw_rg_x, dw_pool, small_x = _mix_bwd(dza, dpooled, proj, xr, hr, p, gates, dgates,
                                                         _after(vecs, s_mlp.token),
                                                         wg["w_rg_a"], wg["w_rg_x"], wg["w_pool"])
    partial.update(w_branch_a=_wgrad(za, dba, "wgrad_branch_a"), w_branch_b=_wgrad(pooled, dbb, "wgrad_branch_b"),
                   w_out=_wgrad(merged, do, "wgrad_out"),
                   w_rg_a=dw_rg_a.astype(BF16), w_rg_x=dw_rg_x.astype(BF16), w_pool=dw_pool.astype(BF16))
    mixer_names = ["w_rg_a", "w_rg_x", "w_pool", "w_branch_a", "w_branch_b", "w_out"]
    s_mixer = scatter(mixer_names, partial, s_mlp.token, "scatter_mixer")

    partial["w_in"] = _wgrad(h1, dproj, "wgrad_in", dep=s_mixer.token)
    s_in = scatter(["w_in"], partial, s_mixer.token, "scatter_in")
    s_mixer.combine_and_send(own4, s_in.token)
    s_in.combine_and_send(own4, s_mixer.token)
    grad_x, small_p = _proj_bwd(dproj, x2d, dx2, _after(modr, s_in.token), vecs, wg["w_in"])

    locals_ = dict(w_in=(w_in, m_w_in, v_w_in), w_up=(w_up, m_w_up, v_w_up), w_down=(w_down, m_w_down, v_w_down),
                   w_branch_a=(w_branch_a, m_w_branch_a, v_w_branch_a),
                   w_branch_b=(w_branch_b, m_w_branch_b, v_w_branch_b), w_out=(w_out, m_w_out, v_w_out),
                   w_rg_a=(w_rg_a, m_w_rg_a, v_w_rg_a), w_rg_x=(w_rg_x, m_w_rg_x, v_w_rg_x),
                   w_pool=(w_pool, m_w_pool, v_w_pool))
    res = {}

    def finish(group, exchange, after):
        chip_sums, arrivals = exchange.finish(after)
        for k, cs, ar in zip(group, chip_sums, arrivals):
            w, m, v = locals_[k]
            shape2d = w.reshape(-1, w.shape[-1]).shape
            outs = _sum_slots_adam(cs.reshape(N_NEAR, *shape2d), ar.reshape(N_NEAR - 1, *shape2d),
                                   w.reshape(shape2d), m.reshape(shape2d), v.reshape(shape2d), "adam_" + k)
            res[k] = [t.reshape(w.shape) for t in outs]
        return res[group[-1]][0]

    done = finish(mlp_names, s_mlp, grad_x)
    done = finish(mixer_names, s_mixer, done)
    done = finish(["w_in"], s_in, done)

    small = jnp.concatenate([small_f, small_m, small_x, small_p], axis=0)
    small_all, = _all_gather([small], [0], "gather_small", dep=done)
    small_all = small_all.reshape(N_DEV, N_SMALL, D)

    def embed(cw):
        return lax.dynamic_update_slice(jnp.zeros((4, D), F32), cw[0], (0, me * (D // N_DEV)))

    def smalls(ng, nl, cb, bra, brx, ap, bp, ps, fg, ba_, cw):
        return [ng, nl, cb, bra, brx, ap, bp, ps, fg.reshape(1, D), ba_.reshape(6, D), embed(cw)]

    small_names = ["norm_mix_g", "norm_mlp_g", "conv_b", "b_rg_a", "b_rg_x", "a_param", "b_pool", "pool_scale",
                   "final_g", "b_ada", "conv_w"]
    fin = _small_finish(
        small_all, mod_all, vecs,
        smalls(norm_mix_g, norm_mlp_g, conv_b, b_rg_a, b_rg_x, a_param, b_pool, pool_scale, final_g, b_ada, conv_w),
        smalls(m_norm_mix_g, m_norm_mlp_g, m_conv_b, m_b_rg_a, m_b_rg_x, m_a_param, m_b_pool, m_pool_scale,
               m_final_g, m_b_ada, m_conv_w),
        smalls(v_norm_mix_g, v_norm_mlp_g, v_conv_b, v_b_rg_a, v_b_rg_x, v_a_param, v_b_pool, v_pool_scale,
               v_final_g, v_b_ada, v_conv_w))
    dmod_all, loss_tile = fin[4 * N_SMALL_PARAMS], fin[4 * N_SMALL_PARAMS + 1]
    dmod_cols = lax.dynamic_slice(dmod_all.reshape(N_DEV, 6 * D), (0, me * n_ada), (N_DEV, n_ada))
    res["w_ada"] = [t.reshape(w_ada.shape) for t in _ada_bwd_adam(c_all, dmod_cols, w_ada[0], m_w_ada[0], v_w_ada[0])]

    def final_shape(k, t):
        if k == "final_g":
            return t.reshape(D)
        if k == "b_ada":
            return t.reshape(1, 6 * D)
        if k == "conv_w":
            return lax.dynamic_slice(t, (0, me * (D // N_DEV)), (4, D // N_DEV)).reshape(conv_w.shape)
        return t

    for i, k in enumerate(small_names):
        res[k] = [final_shape(k, fin[which * N_SMALL_PARAMS + i]) for which in range(4)]
    order = ["norm_mix_g", "norm_mlp_g", "w_ada", "b_ada", "w_in", "conv_w", "conv_b", "w_rg_a", "b_rg_a", "w_rg_x",
             "b_rg_x", "a_param", "w_branch_a", "w_pool", "b_pool", "pool_scale", "w_branch_b", "w_out", "w_up",
             "w_down", "final_g"]
    outs = [loss_tile[0, 0], grad_x.reshape(x.shape)]
    for which in range(4):
        for k in order:
            outs.append(res[k][which])
    return tuple(outs)
```

```python
import functools

import jax
import jax.numpy as jnp
from jax import lax
from jax.experimental import pallas as pl
from jax.experimental.pallas import tpu as pltpu

F32 = jnp.float32
BF16 = jnp.bfloat16
MESH = pl.DeviceIdType.MESH

N_DEV = 8
D = 1024
N_GROUPS = 4
GW = D // N_GROUPS
D_IN = 5 * D
D_FF = 4 * D
POOL_WINDOWS = (2, 4, 8, 16)
HALO_X = 8
HALO_U = 16
EPS = 1e-6
C_RG = 8.0
ADAM_LR, ADAM_B1, ADAM_B2, ADAM_EPS, ADAM_WD, ADAM_STEP = 0.001, 0.9, 0.999, 1e-08, 0.01, 10

V7X_VMEM_LIMIT = 56 * 1024 * 1024

V_CONV_W, V_CONV_B, V_B_RG_A, V_B_RG_X, V_A_PARAM, V_B_POOL, V_POOL_SCALE, V_G1, V_G2, V_GF = 0, 4, 5, 6, 7, 8, 9, 10, 11, 12
M_SH1, M_SC1, M_GT1, M_SH2, M_SC2, M_GT2 = 0, 1, 2, 3, 4, 5

TM_PROJ = 512
TM_MIX = 256
TM_BRANCH = 256
TM_MLP = 512
TM_MLP_BWD = 256
TS_WGRAD = 1024


def _params(semantics):
    return pltpu.CompilerParams(dimension_semantics=semantics, vmem_limit_bytes=V7X_VMEM_LIMIT)


def _resident(shape):
    return pl.BlockSpec(shape, lambda *_: (0,) * len(shape), pipeline_mode=pl.Buffered(1))


def _dot(a, b):
    return jnp.dot(a, b, preferred_element_type=F32)


def _dot_nt(a, b):
    return lax.dot_general(a, b, (((1,), (1,)), ((), ())), preferred_element_type=F32)


def _dot_tn(a, b):
    return lax.dot_general(a, b, (((0,), (0,)), ((), ())), preferred_element_type=F32)


def _sigmoid(x):
    return 0.5 * jnp.tanh(0.5 * x) + 0.5


def _sigmoid_tail(x):
    return 1.0 / (1.0 + jnp.exp(-x))


def _gelu_and_grad(x):
    k = 0.7978845608028654
    x2 = x * x
    t = jnp.tanh(k * (x + 0.044715 * x * x2))
    g = 0.5 * x * (1.0 + t)
    dg = 0.5 * (1.0 + t) + 0.5 * x * (1.0 - t * t) * (k * (1.0 + 3.0 * 0.044715 * x2))
    return g, dg


def _softplus(a):
    e = jnp.exp(-jnp.abs(a))
    u = 1.0 + e
    log1p_e = jnp.where(u == 1.0, e, jnp.log(u) * e / jnp.where(u == 1.0, 1.0, u - 1.0))
    return jnp.maximum(a, 0.0) + log1p_e


def _neg_expm1(z):
    series = -(z * (1.0 + z * (0.5 + z * (1.0 / 6.0 + z * (1.0 / 24.0 + z * (1.0 / 120.0))))))
    return jnp.where(z > -0.1, series, 1.0 - jnp.exp(z))


def _shift_down(x, k):
    return pltpu.roll(x, k, 0)


def _shift_up(x, k):
    return pltpu.roll(x, x.shape[0] - k, 0)


def _rglru_gates(xr, w_a, w_x, b_a, b_x, a_param, is_t0):
    xb = xr.astype(BF16)
    ra = _sigmoid(_dot(xb, w_a) + b_a)
    ri = _sigmoid(_dot(xb, w_x) + b_x)
    sp = _softplus(a_param)
    log_a = (-C_RG) * ra * sp
    a = jnp.exp(log_a)
    mult = jnp.where(is_t0, 1.0, jnp.sqrt(_neg_expm1(2.0 * log_a)))
    return ra, ri, sp, a, mult


SUBLANES = 8


LANES = 128


def _scan_strip(a, b, carry, scr, down):
    t = b.shape[0]
    g = t // SUBLANES
    a3 = a.reshape(g, SUBLANES, LANES)
    b3 = b.reshape(g, SUBLANES, LANES)
    sub = lax.broadcasted_iota(jnp.int32, (g, SUBLANES, LANES), 1)
    for k in (1, 2, 4):
        keep = sub >= k if down else sub < SUBLANES - k
        shift = k if down else SUBLANES - k
        b3 = b3 + a3 * jnp.where(keep, pltpu.roll(b3, shift, 1), 0.0)
        a3 = a3 * jnp.where(keep, pltpu.roll(a3, shift, 1), 1.0)
    scr[0] = a3.reshape(t, LANES)
    scr[1] = b3.reshape(t, LANES)
    end_row = SUBLANES - 1 if down else 0
    ag = scr[0, pl.ds(end_row, g, stride=SUBLANES), :]
    bg = scr[1, pl.ds(end_row, g, stride=SUBLANES), :]
    rg = lax.broadcasted_iota(jnp.int32, (g, LANES), 0)
    edge = 0 if down else g - 1
    bg = bg + jnp.where(rg == edge, ag * carry, 0.0)
    k = 1
    while k < g:
        keep = rg >= k if down else rg < g - k
        shift = k if down else g - k
        bg = bg + ag * jnp.where(keep, pltpu.roll(bg, shift, 0), 0.0)
        if 2 * k < g:
            ag = ag * pltpu.roll(ag, shift, 0)
        k *= 2
    entering = jnp.where(rg != edge, pltpu.roll(bg, 1 if down else g - 1, 0), carry)
    for r in range(SUBLANES):
        scr[2, pl.ds(r, g, stride=SUBLANES), :] = entering
    return scr[1] + scr[0] * scr[2], bg[g - 1:g, :]


def _scan_strips(a, b, carry, scr, down):
    outs = [_scan_strip(a[:, c:c + LANES], b[:, c:c + LANES], carry[:, c:c + LANES], scr, down)
            for c in range(0, b.shape[1], LANES)]
    return jnp.concatenate([o[0] for o in outs], axis=1), jnp.concatenate([o[1] for o in outs], axis=1)


def _scan_down(a, b, carry, scr):
    return _scan_strips(a, b, carry, scr, True)


def _scan_up(m, b, carry, scr):
    return _scan_strips(m, b, carry, scr, False)[0]


def _conv_taps(x_ext):
    return [_shift_down(x_ext, 3 - j)[HALO_X:] if j < 3 else x_ext[HALO_X:] for j in range(4)]


def _proj_fwd(x, modr, vecs, w_in):
    s = x.shape[0]
    tm = min(TM_PROJ, s)

    def body(x_ref, mod_ref, vec_ref, w_ref, proj_ref, h1_ref):
        xv = x_ref[...]
        r = lax.rsqrt(jnp.mean(xv * xv, axis=-1, keepdims=True) + EPS)
        gain = vec_ref[V_G1:V_G1 + 1, :] * (1.0 + mod_ref[M_SC1:M_SC1 + 1, :])
        h = (xv * r * gain + mod_ref[M_SH1:M_SH1 + 1, :]).astype(BF16)
        h1_ref[...] = h
        for c in range(D_IN // D):
            proj_ref[:, c * D:(c + 1) * D] = _dot(h, w_ref[:, c * D:(c + 1) * D])

    return pl.pallas_call(
        body, name="proj_fwd", grid=(s // tm,),
        in_specs=[pl.BlockSpec((tm, D), lambda i: (i, 0)),
                  pl.BlockSpec((8, D), lambda i: (0, 0)),
                  pl.BlockSpec((16, D), lambda i: (0, 0)),
                  _resident((D, D_IN))],
        out_specs=[pl.BlockSpec((tm, D_IN), lambda i: (i, 0)),
                   pl.BlockSpec((tm, D), lambda i: (i, 0))],
        out_shape=[jax.ShapeDtypeStruct((s, D_IN), F32), jax.ShapeDtypeStruct((s, D), BF16)],
        compiler_params=_params(("parallel",)),
    )(x, modr, vecs, w_in)


def _mix_fwd(proj, vecs, w_rg_a, w_rg_x, w_pool):
    s = proj.shape[0]
    tm = min(TM_MIX, s)
    nb = s // tm

    def body(xh_ref, x_ref, y_ref, uh_ref, u_ref, vec_ref, wa_ref, wx_ref, wp_ref,
             xr_ref, hr_ref, za_ref, p_ref, pooled_ref, a_ref, mult_ref, ra_ref, ri_ref, carry_ref, scan_scr):
        i = pl.program_id(0)
        first = i == 0

        @pl.when(first)
        def _():
            carry_ref[...] = jnp.zeros_like(carry_ref)

        row = lax.broadcasted_iota(jnp.int32, (tm, GW), 0)
        is_t0 = jnp.logical_and(first, row == 0)
        t_glob = (row + i * tm + 1).astype(F32)
        for g in range(N_GROUPS):
            cs = slice(g * GW, (g + 1) * GW)
            vec = vec_ref[:, cs]
            xh = jnp.where(first, 0.0, xh_ref[:, cs])
            taps = _conv_taps(jnp.concatenate([xh, x_ref[:, cs]], axis=0))
            xr = vec[V_CONV_B:V_CONV_B + 1]
            for j in range(4):
                xr = xr + vec[V_CONV_W + j:V_CONV_W + j + 1] * taps[j]
            xr_ref[:, cs] = xr
            ra, ri, _, a, mult = _rglru_gates(
                xr, wa_ref[g], wx_ref[g], vec[V_B_RG_A:V_B_RG_A + 1], vec[V_B_RG_X:V_B_RG_X + 1],
                vec[V_A_PARAM:V_A_PARAM + 1], is_t0)
            a_ref[:, cs] = a
            mult_ref[:, cs] = mult
            ra_ref[:, cs] = ra.astype(BF16)
            ri_ref[:, cs] = ri.astype(BF16)
            h, last = _scan_down(a, xr * ri * mult, carry_ref[0:1, cs], scan_scr)
            hr_ref[:, cs] = h
            carry_ref[0:1, cs] = last
            ga, _ = _gelu_and_grad(y_ref[:, cs])
            za_ref[:, cs] = (ga * h).astype(BF16)
            uh = jnp.where(first, 0.0, uh_ref[:, cs])
            sm = jnp.concatenate([uh, u_ref[:, cs]], axis=0)
            k = 1
            while k < POOL_WINDOWS[g]:
                sm = sm + _shift_down(sm, k)
                k *= 2
            cnt = jnp.minimum(t_glob, float(POOL_WINDOWS[g]))
            p = (sm[HALO_U:] / cnt - u_ref[:, cs]).astype(BF16)
            p_ref[:, cs] = p
            pb = _dot(p, wp_ref[g]) + vec[V_B_POOL:V_B_POOL + 1]
            pooled_ref[:, cs] = (pb * vec[V_POOL_SCALE:V_POOL_SCALE + 1]).astype(BF16)

    col = lambda k: (lambda i: (i, k))
    wspec = pl.BlockSpec((N_GROUPS, GW, GW), lambda i: (0, 0, 0))
    return pl.pallas_call(
        body, name="mix_fwd", grid=(nb,),
        in_specs=[pl.BlockSpec((HALO_X, D), lambda i: (jnp.maximum(i * (tm // HALO_X) - 1, 0), 0)),
                  pl.BlockSpec((tm, D), col(0)),
                  pl.BlockSpec((tm, D), col(1)),
                  pl.BlockSpec((HALO_U, D), lambda i: (jnp.maximum(i * (tm // HALO_U) - 1, 0), 2)),
                  pl.BlockSpec((tm, D), col(2)),
                  pl.BlockSpec((16, D), lambda i: (0, 0)),
                  wspec, wspec, wspec],
        out_specs=[pl.BlockSpec((tm, D), lambda i: (i, 0))] * 9,
        out_shape=[jax.ShapeDtypeStruct((s, D), F32), jax.ShapeDtypeStruct((s, D), F32),
                   jax.ShapeDtypeStruct((s, D), BF16), jax.ShapeDtypeStruct((s, D), BF16),
                   jax.ShapeDtypeStruct((s, D), BF16),
                   jax.ShapeDtypeStruct((s, D), F32), jax.ShapeDtypeStruct((s, D), F32),
                   jax.ShapeDtypeStruct((s, D), BF16), jax.ShapeDtypeStruct((s, D), BF16)],
        scratch_shapes=[pltpu.VMEM((8, D), F32), pltpu.VMEM((3, tm, LANES), F32)],
        compiler_params=_params(("arbitrary",)),
    )(proj, proj, proj, proj, proj, vecs, w_rg_a, w_rg_x, w_pool)


def _branch_fwd(za, pooled, proj, x, modr, vecs, w_a, w_b, w_out):
    s = x.shape[0]
    tm = min(TM_BRANCH, s)

    def body(za_ref, pooled_ref, ga_ref, gb_ref, x_ref, mod_ref, vec_ref, wa_ref, wb_ref, wo_ref,
             ba_ref, bb_ref, merged_ref, o_ref, x2_ref, h2_ref):
        ba = _dot(za_ref[...], wa_ref[...])
        bb = _dot(pooled_ref[...], wb_ref[...])
        ba_ref[...] = ba.astype(BF16)
        bb_ref[...] = bb.astype(BF16)
        merged = (_sigmoid(ga_ref[...]) * ba + _sigmoid(gb_ref[...]) * bb).astype(BF16)
        merged_ref[...] = merged
        o = _dot(merged, wo_ref[...])
        o_ref[...] = o.astype(BF16)
        x2 = x_ref[...] + mod_ref[M_GT1:M_GT1 + 1, :] * o
        x2_ref[...] = x2
        r = lax.rsqrt(jnp.mean(x2 * x2, axis=-1, keepdims=True) + EPS)
        gain = vec_ref[V_G2:V_G2 + 1, :] * (1.0 + mod_ref[M_SC2:M_SC2 + 1, :])
        h2_ref[...] = (x2 * r * gain + mod_ref[M_SH2:M_SH2 + 1, :]).astype(BF16)

    tok = pl.BlockSpec((tm, D), lambda i: (i, 0))
    wspec = pl.BlockSpec((D, D), lambda i: (0, 0))
    sd = lambda dt: jax.ShapeDtypeStruct((s, D), dt)
    return pl.pallas_call(
        body, name="branch_fwd", grid=(s // tm,),
        in_specs=[tok, tok,
                  pl.BlockSpec((tm, D), lambda i: (i, 3)), pl.BlockSpec((tm, D), lambda i: (i, 4)),
                  tok, pl.BlockSpec((8, D), lambda i: (0, 0)), pl.BlockSpec((16, D), lambda i: (0, 0)),
                  wspec, wspec, wspec],
        out_specs=[tok] * 6,
        out_shape=[sd(BF16), sd(BF16), sd(BF16), sd(BF16), sd(F32), sd(BF16)],
        compiler_params=_params(("parallel",)),
    )(za, pooled, proj, proj, x, modr, vecs, w_a, w_b, w_out)


def _mlp_fwd(h2, x2, target, modr, vecs, w_up, w_down):
    s = x2.shape[0]
    tm = min(TM_MLP, s)

    def body(h2_ref, x2_ref, tgt_ref, mod_ref, vec_ref, wu_ref, wd_ref,
             ru_ref, dx3_ref, ddn_ref, small_ref):
        @pl.when(pl.program_id(0) == 0)
        def _():
            small_ref[...] = jnp.zeros_like(small_ref)

        h2 = h2_ref[...]
        dn = None
        for c in range(D_FF // D):
            cs = slice(c * D, (c + 1) * D)
            ru = jnp.maximum(_dot(h2, wu_ref[:, cs]), 0.0)
            ru_ref[:, cs] = ru.astype(BF16)
            part = _dot((ru * ru).astype(BF16), wd_ref[cs, :])
            dn = part if dn is None else dn + part
        gt2 = mod_ref[M_GT2:M_GT2 + 1, :]
        gf = vec_ref[V_GF:V_GF + 1, :]
        x3 = x2_ref[...] + gt2 * dn
        r3 = lax.rsqrt(jnp.mean(x3 * x3, axis=-1, keepdims=True) + EPS)
        n3 = x3 * r3
        err = n3 * gf - tgt_ref[...]
        dy = err * (1.0 / D)
        dn3 = dy * gf
        dx3 = r3 * (dn3 - n3 * jnp.mean(dn3 * n3, axis=-1, keepdims=True))
        dx3_ref[...] = dx3
        ddn_ref[...] = (dx3 * gt2).astype(BF16)
        small_ref[0:1, :] += jnp.sum(dy * n3, axis=0, keepdims=True)
        small_ref[1:2, :] += jnp.sum(dx3 * dn, axis=0, keepdims=True)
        small_ref[2:3, :] += (0.5 / D) * jnp.sum(err * err, axis=0, keepdims=True)

    tok = pl.BlockSpec((tm, D), lambda i: (i, 0))
    return pl.pallas_call(
        body, name="mlp_fwd", grid=(s // tm,),
        in_specs=[tok, tok, tok,
                  pl.BlockSpec((8, D), lambda i: (0, 0)), pl.BlockSpec((16, D), lambda i: (0, 0)),
                  _resident((D, D_FF)), _resident((D_FF, D))],
        out_specs=[pl.BlockSpec((tm, D_FF), lambda i: (i, 0)), tok, tok,
                   pl.BlockSpec((8, D), lambda i: (0, 0))],
        out_shape=[jax.ShapeDtypeStruct((s, D_FF), BF16), jax.ShapeDtypeStruct((s, D), F32),
                   jax.ShapeDtypeStruct((s, D), BF16), jax.ShapeDtypeStruct((8, D), F32)],
        compiler_params=_params(("arbitrary",)),
    )(h2, x2, target, modr, vecs, w_up, w_down)


def _mlp_bwd(d_dn, ru, x2, dx3, o, modr, vecs, w_up, w_down):
    s = x2.shape[0]
    tm = min(TM_MLP_BWD, s)

    def body(ddn_ref, ru_ref, x2_ref, dx3_ref, o_ref, mod_ref, vec_ref, wu_ref, wd_ref,
             dup_ref, dx2_ref, do_ref, small_ref):
        @pl.when(pl.program_id(0) == 0)
        def _():
            small_ref[...] = jnp.zeros_like(small_ref)

        ddn = ddn_ref[...]
        dh2 = None
        for c in range(D_FF // D):
            cs = slice(c * D, (c + 1) * D)
            dff = _dot_nt(ddn, wd_ref[cs, :])
            dup = (dff * (2.0 * ru_ref[:, cs].astype(F32))).astype(BF16)
            dup_ref[:, cs] = dup
            part = _dot_nt(dup, wu_ref[:, cs])
            dh2 = part if dh2 is None else dh2 + part
        x2 = x2_ref[...]
        r2 = lax.rsqrt(jnp.mean(x2 * x2, axis=-1, keepdims=True) + EPS)
        xn2 = x2 * r2
        gain = vec_ref[V_G2:V_G2 + 1, :] * (1.0 + mod_ref[M_SC2:M_SC2 + 1, :])
        dxn2 = dh2 * gain
        dx2 = dx3_ref[...] + r2 * (dxn2 - xn2 * jnp.mean(dxn2 * xn2, axis=-1, keepdims=True))
        dx2_ref[...] = dx2
        do_ref[...] = (dx2 * mod_ref[M_GT1:M_GT1 + 1, :]).astype(BF16)
        small_ref[0:1, :] += jnp.sum(dh2, axis=0, keepdims=True)
        small_ref[1:2, :] += jnp.sum(dh2 * xn2, axis=0, keepdims=True)
        small_ref[2:3, :] += jnp.sum(dx2 * o_ref[...].astype(F32), axis=0, keepdims=True)

    tok = pl.BlockSpec((tm, D), lambda i: (i, 0))
    wide = pl.BlockSpec((tm, D_FF), lambda i: (i, 0))
    return pl.pallas_call(
        body, name="mlp_bwd", grid=(s // tm,),
        in_specs=[tok, wide, tok, tok, tok,
                  pl.BlockSpec((8, D), lambda i: (0, 0)), pl.BlockSpec((16, D), lambda i: (0, 0)),
                  _resident((D, D_FF)), _resident((D_FF, D))],
        out_specs=[wide, tok, tok, pl.BlockSpec((8, D), lambda i: (0, 0))],
        out_shape=[jax.ShapeDtypeStruct((s, D_FF), BF16), jax.ShapeDtypeStruct((s, D), F32),
                   jax.ShapeDtypeStruct((s, D), BF16), jax.ShapeDtypeStruct((8, D), F32)],
        compiler_params=_params(("arbitrary",)),
    )(d_dn, ru, x2, dx3, o, modr, vecs, w_up, w_down)


def _branch_bwd(do, proj, ba, bb, w_a, w_b, w_out, dep):
    s = do.shape[0]
    tm = min(TM_BRANCH, s)

    def body(do_ref, ga_ref, gb_ref, ba_ref, bb_ref, wa_ref, wb_ref, wo_ref, dep_ref,
             dba_ref, dbb_ref, dg_ref, dza_ref, dpooled_ref):
        dmerged = _dot_nt(do_ref[...], wo_ref[...])
        sa = _sigmoid(ga_ref[...])
        sb = _sigmoid(gb_ref[...])
        dba = (dmerged * sa).astype(BF16)
        dbb = (dmerged * sb).astype(BF16)
        dba_ref[...] = dba
        dbb_ref[...] = dbb
        dg_ref[:, :D] = (dmerged * ba_ref[...].astype(F32) * sa * (1.0 - sa)).astype(BF16)
        dg_ref[:, D:] = (dmerged * bb_ref[...].astype(F32) * sb * (1.0 - sb)).astype(BF16)
        dza_ref[...] = _dot_nt(dba, wa_ref[...])
        dpooled_ref[...] = _dot_nt(dbb, wb_ref[...])

    tok = pl.BlockSpec((tm, D), lambda i: (i, 0))
    wspec = pl.BlockSpec((D, D), lambda i: (0, 0))
    sd = lambda dt: jax.ShapeDtypeStruct((s, D), dt)
    return pl.pallas_call(
        body, name="branch_bwd", grid=(s // tm,),
        in_specs=[tok, pl.BlockSpec((tm, D), lambda i: (i, 3)), pl.BlockSpec((tm, D), lambda i: (i, 4)),
                  tok, tok, wspec, wspec, wspec, pl.BlockSpec(memory_space=pl.ANY)],
        out_specs=[tok, tok, pl.BlockSpec((tm, 2 * D), lambda i: (i, 0)), tok, tok],
        out_shape=[sd(BF16), sd(BF16), jax.ShapeDtypeStruct((s, 2 * D), BF16), sd(F32), sd(F32)],
        compiler_params=_params(("parallel",)),
    )(do, proj, proj, ba, bb, w_a, w_b, w_out, dep)


def _mix_bwd(dza, dpooled, proj, xr, hr, p, gates, dgates, vecs, w_rg_a, w_rg_x, w_pool):
    s = xr.shape[0]
    tm = min(TM_MIX, s)
    nb = s // tm

    def body(dza_ref, dpooled_ref, xh_ref, x_ref, y_ref, xr_ref, hh_ref, hr_ref, p_ref,
             a_ref, mult_ref, ra_ref, ri_ref, dg_ref, vec_ref, wa_ref, wx_ref, wp_ref,
             dproj_ref, dwa_ref, dwx_ref, dwp_ref, small_ref,
             scan_carry, dxr_carry, q_carry, scan_scr):
        i = pl.program_id(0)
        bi = nb - 1 - i
        first_t = bi == 0

        @pl.when(i == 0)
        def _():
            scan_carry[...] = jnp.zeros_like(scan_carry)
            dxr_carry[...] = jnp.zeros_like(dxr_carry)
            q_carry[...] = jnp.zeros_like(q_carry)
            dwa_ref[...] = jnp.zeros_like(dwa_ref)
            dwx_ref[...] = jnp.zeros_like(dwx_ref)
            dwp_ref[...] = jnp.zeros_like(dwp_ref)
            small_ref[...] = jnp.zeros_like(small_ref)

        row = lax.broadcasted_iota(jnp.int32, (tm, GW), 0)
        is_t0 = jnp.logical_and(first_t, row == 0)
        t_glob = (row + bi * tm + 1).astype(F32)
        colsum = lambda v: jnp.sum(v, axis=0, keepdims=True)
        for g in range(N_GROUPS):
            cs = slice(g * GW, (g + 1) * GW)
            vec = vec_ref[:, cs]
            xr = xr_ref[:, cs]
            hr = hr_ref[:, cs]
            dza = dza_ref[:, cs]
            ga, dga = _gelu_and_grad(y_ref[:, cs])
            dproj_ref[:, D + g * GW:D + (g + 1) * GW] = (dza * hr * dga).astype(BF16)
            dhr = dza * ga
            a = a_ref[:, cs]
            mult = mult_ref[:, cs]
            ra = ra_ref[:, cs].astype(F32)
            ri = ri_ref[:, cs].astype(F32)
            sp = _softplus(vec[V_A_PARAM:V_A_PARAM + 1])
            m = jnp.where(row == tm - 1, 1.0, _shift_up(a, 1))
            gsum = _scan_up(m, dhr, scan_carry[0:1, cs], scan_scr)
            scan_carry[0:1, cs] = a[0:1, :] * gsum[0:1, :]
            hh = jnp.where(first_t, 0.0, hh_ref[:, cs])
            hprev = _shift_down(jnp.concatenate([hh, hr], axis=0), 1)[8:]
            da = gsum * hprev
            dmult = jnp.where(is_t0, 0.0, gsum * xr * ri)
            dlog_a = da * a - dmult * a * a / mult
            dri = gsum * xr * mult
            dxr = gsum * ri * mult
            small_ref[7:8, cs] += colsum((-C_RG) * ra * dlog_a)
            dpa = (((-C_RG) * sp) * dlog_a * ra * (1.0 - ra))
            dpx = dri * ri * (1.0 - ri)
            small_ref[5:6, cs] += colsum(dpa)
            small_ref[6:7, cs] += colsum(dpx)
            dpa = dpa.astype(BF16)
            dpx = dpx.astype(BF16)
            xrb = xr.astype(BF16)
            dwa_ref[g] += _dot_tn(xrb, dpa)
            dwx_ref[g] += _dot_tn(xrb, dpx)
            dxr = dxr + _dot_nt(dpa, wa_ref[g]) + _dot_nt(dpx, wx_ref[g])
            small_ref[4:5, cs] += colsum(dxr)
            xh = jnp.where(first_t, 0.0, xh_ref[:, cs])
            taps = _conv_taps(jnp.concatenate([xh, x_ref[:, cs]], axis=0))
            dxr_ext = jnp.concatenate([dxr, dxr_carry[:, cs]], axis=0)
            dx = vec[V_CONV_W + 3:V_CONV_W + 4] * dxr
            for j in range(4):
                small_ref[j:j + 1, cs] += colsum(dxr * taps[j])
                if j < 3:
                    dx = dx + vec[V_CONV_W + j:V_CONV_W + j + 1] * _shift_up(dxr_ext, 3 - j)[:tm]
            dxr_carry[:, cs] = dxr[0:8, :]
            dproj_ref[:, cs] = dx.astype(BF16)
            pg = p_ref[:, cs]
            dpooled = dpooled_ref[:, cs]
            pb = _dot(pg, wp_ref[g]) + vec[V_B_POOL:V_B_POOL + 1]
            small_ref[9:10, cs] += colsum(dpooled * pb)
            dpb = dpooled * vec[V_POOL_SCALE:V_POOL_SCALE + 1]
            small_ref[8:9, cs] += colsum(dpb)
            dpbb = dpb.astype(BF16)
            dwp_ref[g] += _dot_tn(pg, dpbb)
            dp = _dot_nt(dpbb, wp_ref[g])
            q = dp / jnp.minimum(t_glob, float(POOL_WINDOWS[g]))
            sm = jnp.concatenate([q, q_carry[:, cs]], axis=0)
            k = 1
            while k < POOL_WINDOWS[g]:
                sm = sm + _shift_up(sm, k)
                k *= 2
            q_carry[:, cs] = q[0:HALO_U, :]
            dproj_ref[:, 2 * D + g * GW:2 * D + (g + 1) * GW] = (sm[:tm] - dp).astype(BF16)
        dproj_ref[:, 3 * D:] = dg_ref[...]

    rev = lambda i: nb - 1 - i
    tok = pl.BlockSpec((tm, D), lambda i: (rev(i), 0))
    col = lambda k: pl.BlockSpec((tm, D), lambda i: (rev(i), k))
    halo8 = lambda k: pl.BlockSpec((8, D), lambda i: (jnp.maximum(rev(i) * (tm // 8) - 1, 0), k))
    wspec = pl.BlockSpec((N_GROUPS, GW, GW), lambda i: (0, 0, 0))
    wshape = jax.ShapeDtypeStruct((N_GROUPS, GW, GW), F32)
    return pl.pallas_call(
        body, name="mix_bwd", grid=(nb,),
        in_specs=[tok, tok, halo8(0), col(0), col(1), tok, halo8(0), tok, tok, tok, tok, tok, tok,
                  pl.BlockSpec((tm, 2 * D), lambda i: (rev(i), 0)),
                  pl.BlockSpec((16, D), lambda i: (0, 0)), wspec, wspec, wspec],
        out_specs=[pl.BlockSpec((tm, D_IN), lambda i: (rev(i), 0)), wspec, wspec, wspec,
                   pl.BlockSpec((16, D), lambda i: (0, 0))],
        out_shape=[jax.ShapeDtypeStruct((s, D_IN), BF16), wshape, wshape, wshape,
                   jax.ShapeDtypeStruct((16, D), F32)],
        scratch_shapes=[pltpu.VMEM((8, D), F32), pltpu.VMEM((8, D), F32), pltpu.VMEM((HALO_U, D), F32),
                        pltpu.VMEM((3, tm, LANES), F32)],
        compiler_params=_params(("arbitrary",)),
    )(dza, dpooled, proj, proj, proj, xr, hr, hr, p, *gates, dgates, vecs, w_rg_a, w_rg_x, w_pool)


def _proj_bwd(dproj, x, dx2, modr, vecs, w_in):
    s = x.shape[0]
    tm = min(TM_PROJ, s)

    def body(dp_ref, x_ref, dx2_ref, mod_ref, vec_ref, w_ref, gx_ref, small_ref):
        @pl.when(pl.program_id(0) == 0)
        def _():
            small_ref[...] = jnp.zeros_like(small_ref)

        dh1 = None
        for c in range(D_IN // D):
            cs = slice(c * D, (c + 1) * D)
            part = _dot_nt(dp_ref[:, cs], w_ref[:, cs])
            dh1 = part if dh1 is None else dh1 + part
        xv = x_ref[...]
        r1 = lax.rsqrt(jnp.mean(xv * xv, axis=-1, keepdims=True) + EPS)
        xn1 = xv * r1
        gain = vec_ref[V_G1:V_G1 + 1, :] * (1.0 + mod_ref[M_SC1:M_SC1 + 1, :])
        dxn1 = dh1 * gain
        gx_ref[...] = dx2_ref[...] + r1 * (dxn1 - xn1 * jnp.mean(dxn1 * xn1, axis=-1, keepdims=True))
        small_ref[0:1, :] += jnp.sum(dh1, axis=0, keepdims=True)
        small_ref[1:2, :] += jnp.sum(dh1 * xn1, axis=0, keepdims=True)

    tok = pl.BlockSpec((tm, D), lambda i: (i, 0))
    return pl.pallas_call(
        body, name="proj_bwd", grid=(s // tm,),
        in_specs=[pl.BlockSpec((tm, D_IN), lambda i: (i, 0)), tok, tok,
                  pl.BlockSpec((8, D), lambda i: (0, 0)), pl.BlockSpec((16, D), lambda i: (0, 0)),
                  _resident((D, D_IN))],
        out_specs=[tok, pl.BlockSpec((8, D), lambda i: (0, 0))],
        out_shape=[jax.ShapeDtypeStruct((s, D), F32), jax.ShapeDtypeStruct((8, D), F32)],
        compiler_params=_params(("arbitrary",)),
    )(dproj, x, dx2, modr, vecs, w_in)


def _wgrad(a, b, name, square_a=False, dep=None):
    s, ka = a.shape
    n = b.shape[1]
    tka = ka if ka <= 1024 else ka // 2
    tn = n if n <= 1024 else n // 2
    ts = min(TS_WGRAD, s)
    ns = s // ts
    nc = 512
    deps = [] if dep is None else [dep]

    def body(a_ref, b_ref, *refs):
        out_ref, acc_ref = refs[-2:]
        t = pl.program_id(2)

        @pl.when(t == 0)
        def _():
            acc_ref[...] = jnp.zeros_like(acc_ref)

        av = a_ref[...]
        if square_a:
            af = av.astype(F32)
            av = (af * af).astype(BF16)
        for c in range(tn // nc):
            cs = slice(c * nc, (c + 1) * nc)
            acc_ref[:, cs] += _dot_tn(av, b_ref[:, cs])

        @pl.when(t == ns - 1)
        def _():
            out_ref[...] = acc_ref[...].astype(BF16)

    return pl.pallas_call(
        body, name=name, grid=(ka // tka, n // tn, ns),
        in_specs=[pl.BlockSpec((ts, tka), lambda i, j, t: (t, i)),
                  pl.BlockSpec((ts, tn), lambda i, j, t: (t, j))] + [pl.BlockSpec(memory_space=pl.ANY)] * len(deps),
        out_specs=pl.BlockSpec((tka, tn), lambda i, j, t: (i, j)),
        out_shape=jax.ShapeDtypeStruct((ka, n), BF16),
        scratch_shapes=[pltpu.VMEM((tka, tn), F32)],
        compiler_params=_params(("parallel", "parallel", "arbitrary")),
    )(a, b, *deps)


def _window(ref, kind, idx, size):
    start = pl.multiple_of(idx * size, size)
    if kind == 0:
        return ref.at[pl.ds(start, size)]
    if kind == 1:
        return ref.at[:, pl.ds(start, size)]
    return ref.at[:, :, pl.ds(start, size)]


def _mesh_place():
    x, y, c = lax.axis_index("x"), lax.axis_index("y"), lax.axis_index("c")
    return x, y, c, 4 * x + 2 * y + c


def _peer(x, y, c, q):
    px = 1 - x if q & 4 else x
    py = 1 - y if q & 2 else y
    pc = 1 - c if q & 1 else c
    return (px, py, pc), 4 * px + 2 * py + pc


def _all_gather(shards, kinds, name, dep=None):
    n = len(shards)
    deps = [] if dep is None else [dep]
    full_shapes = []
    for sh, kind in zip(shards, kinds):
        dims = list(sh.shape)
        dims[kind] *= N_DEV
        full_shapes.append(jax.ShapeDtypeStruct(tuple(dims), sh.dtype))

    def body(*refs):
        ins, outs = refs[:n], refs[n + len(deps):2 * n + len(deps)]
        send_sems, recv_sems, local_sems = refs[2 * n + len(deps):]
        x, y, c, me = _mesh_place()
        sends, recvs, locals_ = [], [], []
        for k in range(n):
            size = shards[k].shape[kinds[k]]
            mine = _window(outs[k], kinds[k], me, size)
            lc = pltpu.make_async_copy(ins[k], mine, local_sems.at[k])
            lc.start()
            locals_.append(lc)
            for q in range(1, N_DEV):
                peer, peer_idx = _peer(x, y, c, q)
                cp = pltpu.make_async_remote_copy(
                    src_ref=ins[k], dst_ref=mine, send_sem=send_sems.at[k, q], recv_sem=recv_sems.at[k, q],
                    device_id=peer, device_id_type=MESH)
                cp.start()
                sends.append(cp)
                recvs.append(pltpu.make_async_remote_copy(
                    src_ref=ins[k], dst_ref=_window(outs[k], kinds[k], peer_idx, size),
                    send_sem=send_sems.at[k, q], recv_sem=recv_sems.at[k, q],
                    device_id=peer, device_id_type=MESH))
        for cp in recvs:
            cp.wait_recv()
        for cp in sends:
            cp.wait_send()
        for lc in locals_:
            lc.wait()

    any_spec = pl.BlockSpec(memory_space=pl.ANY)
    return pl.pallas_call(
        body, name=name,
        in_specs=[any_spec] * (n + len(deps)), out_specs=[any_spec] * n, out_shape=full_shapes,
        scratch_shapes=[pltpu.SemaphoreType.DMA((n, N_DEV)), pltpu.SemaphoreType.DMA((n, N_DEV)),
                        pltpu.SemaphoreType.DMA((n,))],
    )(*shards, *deps)


_HBM = pl.BlockSpec(memory_space=pltpu.HBM)
_SEM = pl.BlockSpec(memory_space=pltpu.SEMAPHORE)
_EFFECT = pltpu.SideEffectType.DATAFLOW_SIDE_EFFECTING


N_NEAR = 4


def _near(x, y, c):
    out = [((x, y, 1 - c), 4 * x + 2 * y + 1 - c)]
    for j in (1, 2, 3):
        px = 1 - x if j & 2 else x
        py = 1 - y if j & 1 else y
        out.append(((px, py, c), 4 * px + 2 * py + c))
    return out


def _remote(src, dst, send_sems, recv_sems, slot, device):
    return pltpu.make_async_remote_copy(src_ref=src, dst_ref=dst, send_sem=send_sems.at[slot], recv_sem=recv_sems.at[slot],
                                        device_id=device, device_id_type=MESH)


def _split_call(name, arrays, sems_in, n_new_sems, after, emit):
    na, ns, nn = len(arrays), len(sems_in), len(n_new_sems)

    def body(*refs):
        emit(refs[:na], refs[na:na + ns], refs[na + ns + 1:na + ns + 1 + nn])
        refs[-1][...] = jnp.zeros_like(refs[-1])

    outs = pl.pallas_call(
        body, name=name,
        out_shape=(*[pltpu.SemaphoreType.DMA((m,)) for m in n_new_sems],
                   *[pltpu.HBM(a.shape, a.dtype) for a in arrays], jax.ShapeDtypeStruct((8, 128), F32)),
        in_specs=[_HBM] * na + [_SEM] * ns + [pl.BlockSpec(memory_space=pl.ANY)],
        out_specs=(*[_SEM] * nn, *[_HBM] * na, pl.BlockSpec(memory_space=pltpu.VMEM)),
        input_output_aliases={i: nn + i for i in range(na)},
        compiler_params=pltpu.CompilerParams(has_side_effects=_EFFECT),
    )(*[pltpu.with_memory_space_constraint(a, pltpu.HBM) for a in arrays], *sems_in, after)
    return list(outs[:nn]), list(outs[nn:nn + na]), outs[-1]


class _Gather:
    def __init__(self, shards, kinds, after, name):
        self.n, self.kinds, self.name = len(shards), kinds, name
        self.sizes = [s.shape[k] for s, k in zip(shards, kinds)]
        n = self.n
        lands = []
        for s, k in zip(shards, kinds):
            dims = list(s.shape)
            dims[k] *= N_DEV
            lands.append(lax.empty(tuple(dims), s.dtype))

        def emit(arr, _, new):
            x, y, c, me = _mesh_place()
            for k in range(n):
                pltpu.make_async_copy(arr[k], _window(arr[n + k], kinds[k], me, self.sizes[k]), new[2].at[k]).start()
            for k in range(n):
                mine = _window(arr[n + k], kinds[k], me, self.sizes[k])
                for j, (dev, _) in enumerate(_near(x, y, c)):
                    _remote(arr[k], mine, new[0], new[1], k * N_NEAR + j, dev).start()

        self.sems, self.arrays, self.token = _split_call(name + "_start", [*shards, *lands], [],
                                                         [n * N_NEAR, n * N_NEAR, n], after, emit)

    def forward(self, after):
        n, kinds, sizes = self.n, self.kinds, self.sizes

        def emit(arr, old, new):
            x, y, c, _ = _mesh_place()
            near = _near(x, y, c)
            for k in range(n):
                for j in (1, 2, 3):
                    dev, idx = near[j]
                    landed = _window(arr[n + k], kinds[k], idx, sizes[k])
                    _remote(arr[k], landed, old[0], old[1], k * N_NEAR + j, dev).wait_recv()
                    _remote(landed, landed, new[0], new[1], k * N_NEAR + j, near[0][0]).start()

        new, self.arrays, self.token = _split_call(self.name + "_forward", self.arrays, self.sems, [n * N_NEAR] * 2,
                                                   after, emit)
        self.sems = [*self.sems, *new]

    def finish(self, after):
        n, kinds, sizes = self.n, self.kinds, self.sizes

        def emit(arr, old, _):
            x, y, c, me = _mesh_place()
            near = _near(x, y, c)
            other_core = near[0][0]
            for k in range(n):
                win = lambda idx: _window(arr[n + k], kinds[k], idx, sizes[k])
                pltpu.make_async_copy(arr[k], win(me), old[2].at[k]).wait()
                for j, (dev, idx) in enumerate(near):
                    _remote(arr[k], win(me), old[0], old[1], k * N_NEAR + j, dev).wait_send()
                _remote(arr[k], win(near[0][1]), old[0], old[1], k * N_NEAR, other_core).wait_recv()
                for j in (1, 2, 3):
                    idx = near[j][1]
                    _remote(win(idx), win(idx), old[3], old[4], k * N_NEAR + j, other_core).wait_send()
                    _remote(arr[k], win(idx + 1 - 2 * c), old[3], old[4], k * N_NEAR + j, other_core).wait_recv()

        _, arrays, _ = _split_call(self.name + "_finish", self.arrays, self.sems, [], after, emit)
        return arrays[n:]


class _Scatter:
    def __init__(self, partials, kinds, after, name):
        self.n, self.kinds, self.name, self.partials = len(partials), kinds, name, partials
        self.sizes = [p.shape[k] // N_DEV for p, k in zip(partials, kinds)]
        n, sizes = self.n, self.sizes
        self.slot_shapes = []
        for p, k, size in zip(partials, kinds, sizes):
            dims = list(p.shape)
            dims[k] = size
            self.slot_shapes.append((N_NEAR, *dims))
        slots = [lax.empty(sh, p.dtype) for sh, p in zip(self.slot_shapes, partials)]

        def emit(arr, _, new):
            x, y, c, _ = _mesh_place()
            near = _near(x, y, c)
            for k in range(n):
                for j in range(N_NEAR):
                    owner = near[j][1] if j == 0 else near[j][1] + 1 - 2 * c
                    _remote(_window(arr[k], kinds[k], owner, sizes[k]), arr[n + k].at[j], new[0], new[1],
                            k * N_NEAR + j, near[0][0]).start()

        self.sems, self.arrays, self.token = _split_call(name + "_start", [*partials, *slots], [], [n * N_NEAR] * 2,
                                                         after, emit)

    def combine_and_send(self, own4, after):
        n, kinds, sizes = self.n, self.kinds, self.sizes

        def emit_wait(arr, old, _):
            x, y, c, _ = _mesh_place()
            near = _near(x, y, c)
            for k in range(n):
                for j in range(N_NEAR):
                    owner = near[j][1] if j == 0 else near[j][1] + 1 - 2 * c
                    cp = _remote(_window(arr[k], kinds[k], owner, sizes[k]), arr[n + k].at[j], old[0], old[1],
                                 k * N_NEAR + j, near[0][0])
                    cp.wait_send()
                    cp.wait_recv()

        _, arrays, _ = _split_call(self.name + "_landed", self.arrays, self.sems, [], after, emit_wait)
        chip_sums = _chip_sums(arrays[:n], arrays[n:], kinds, sizes, own4, self.name + "_combine")
        arrivals = [lax.empty((N_NEAR - 1, *sh[1:]), p.dtype) for sh, p in zip(self.slot_shapes, self.partials)]

        def emit_send(arr, _, new):
            x, y, c, _ = _mesh_place()
            near = _near(x, y, c)
            for k in range(n):
                for j in (1, 2, 3):
                    _remote(arr[k].at[j], arr[n + k].at[j - 1], new[0], new[1], k * N_NEAR + j, near[j][0]).start()

        self.sems, self.arrays, self.token = _split_call(self.name + "_send", [*chip_sums, *arrivals], [],
                                                         [n * N_NEAR] * 2, own4, emit_send)

    def finish(self, after):
        n = self.n

        def emit(arr, old, _):
            x, y, c, _ = _mesh_place()
            near = _near(x, y, c)
            for k in range(n):
                for j in (1, 2, 3):
                    cp = _remote(arr[k].at[j], arr[n + k].at[j - 1], old[0], old[1], k * N_NEAR + j, near[j][0])
                    cp.wait_send()
                    cp.wait_recv()

        _, arrays, _ = _split_call(self.name + "_finish", self.arrays, self.sems, [], after, emit)
        return arrays[:n], arrays[n:]


def _chip_sums(partials, slots, kinds, sizes, own4, name):
    n = len(partials)

    def body(own_ref, *refs):
        for k in range(n):
            refs[2 * n + k][...] = (refs[k][...].astype(F32) + refs[n + k][...].astype(F32)).astype(BF16)

    in_specs, slot_specs = [], []
    for p, s, kind, size in zip(partials, slots, kinds, sizes):
        block = list(p.shape)
        block[kind] = size
        nd = len(block)
        in_specs.append(pl.BlockSpec(tuple(block), functools.partial(
            lambda j, own, kind, nd: tuple(own[j] if d == kind else 0 for d in range(nd)), kind=kind, nd=nd)))
        slot_specs.append(pl.BlockSpec((None, *block), functools.partial(
            lambda j, own, nd: (j,) + (0,) * nd, nd=nd)))
    return pl.pallas_call(
        body, name=name,
        grid_spec=pltpu.PrefetchScalarGridSpec(num_scalar_prefetch=1, grid=(N_NEAR,),
                                               in_specs=in_specs + slot_specs, out_specs=slot_specs),
        out_shape=[jax.ShapeDtypeStruct(s.shape, s.dtype) for s in slots],
        compiler_params=_params(("arbitrary",)),
    )(own4, *partials, *slots)


def _after(small, token):
    return small + token[0:1, 0:1].astype(small.dtype)


def _silu(c):
    return c * _sigmoid_tail(c)


def _ada_fwd(c_all, w_ada, b_ada_cols):
    def body(c_ref, w_ref, b_ref, out_ref):
        out_ref[...] = jnp.dot(_silu(c_ref[...]), w_ref[...], preferred_element_type=F32,
                               precision=lax.Precision.HIGHEST) + b_ref[...]

    return pl.pallas_call(
        body, name="ada_fwd", out_shape=jax.ShapeDtypeStruct((N_DEV, w_ada.shape[1]), F32),
    )(c_all, w_ada, b_ada_cols)


def _adam(w, g, m, v):
    m = ADAM_B1 * m + (1.0 - ADAM_B1) * g
    v = ADAM_B2 * v + (1.0 - ADAM_B2) * (g * g)
    m_hat = m / (1.0 - ADAM_B1 ** ADAM_STEP)
    v_hat = v / (1.0 - ADAM_B2 ** ADAM_STEP)
    delta = -ADAM_LR * (m_hat / (jnp.sqrt(v_hat) + ADAM_EPS) + ADAM_WD * w)
    return delta, m, v


def _ada_bwd_adam(c_all, dmod_cols, w, m, v):
    def body(c_ref, d_ref, w_ref, m_ref, v_ref, g_ref, delta_ref, nm_ref, nv_ref):
        g = lax.dot_general(_silu(c_ref[...]), d_ref[...], (((0,), (0,)), ((), ())),
                            preferred_element_type=F32, precision=lax.Precision.HIGHEST)
        g_ref[...] = g
        delta_ref[...], nm_ref[...], nv_ref[...] = _adam(w_ref[...], g, m_ref[...], v_ref[...])

    sd = jax.ShapeDtypeStruct(w.shape, F32)
    return pl.pallas_call(body, name="ada_bwd_adam", out_shape=[sd] * 4,
                          compiler_params=pltpu.CompilerParams(vmem_limit_bytes=V7X_VMEM_LIMIT),
                          )(c_all, dmod_cols, w, m, v)


def _sum_slots_adam(chip_sums, arrivals, w, m, v, name):
    r, cdim = w.shape
    tr = min(r, 256)

    def body(c_ref, a_ref, w_ref, m_ref, v_ref, g_ref, delta_ref, nm_ref, nv_ref):
        g = c_ref[...].astype(F32)
        for j in (1, 2, 3):
            g = g + a_ref[j - 1].astype(F32)
        g_ref[...] = g
        delta_ref[...], nm_ref[...], nv_ref[...] = _adam(w_ref[...], g, m_ref[...], v_ref[...])

    blk = pl.BlockSpec((tr, cdim), lambda i: (i, 0))
    sd = jax.ShapeDtypeStruct((r, cdim), F32)
    return pl.pallas_call(
        body, name=name, grid=(r // tr,),
        in_specs=[pl.BlockSpec((None, tr, cdim), lambda i: (0, i, 0)),
                  pl.BlockSpec((N_NEAR - 1, tr, cdim), lambda i: (0, i, 0)), blk, blk, blk],
        out_specs=[blk] * 4, out_shape=[sd] * 4,
        compiler_params=_params(("parallel",)),
    )(chip_sums, arrivals, w, m, v)


N_SMALL = 40
N_SMALL_PARAMS = 11


def _pack_vecs(conv_w_full, rows):
    def body(cw_ref, *refs):
        out = refs[-1]
        out[...] = jnp.zeros_like(out)
        out[0:4, :] = cw_ref[0:4, :]
        for r, ref in enumerate(refs[:-1]):
            out[4 + r:5 + r, :] = ref[...]

    return pl.pallas_call(body, name="pack_vecs", out_shape=jax.ShapeDtypeStruct((16, D), F32))(conv_w_full, *rows)


def _small_finish(gathered, mod_all, vecs, ws, ms, vs):
    n = N_SMALL_PARAMS

    def body(g_ref, mod_ref, vec_ref, *refs):
        w_refs, m_refs, v_refs = refs[:n], refs[n:2 * n], refs[2 * n:3 * n]
        outs = refs[3 * n:]
        g1 = vec_ref[V_G1:V_G1 + 1, :]
        g2 = vec_ref[V_G2:V_G2 + 1, :]
        zero = jnp.zeros((1, D), F32)
        dg1, dg2, dgf, loss_lanes = zero, zero, zero, zero
        mixer = jnp.zeros((16, D), F32)
        db_ada = jnp.zeros((6, D), F32)
        for b in range(N_DEV):
            gb = g_ref[b]
            mod = mod_ref[b]
            q1 = gb[33:34]
            q2 = gb[9:10]
            dmod = jnp.concatenate([gb[32:33], q1 * g1, gb[10:11], gb[8:9], q2 * g2, gb[1:2]], axis=0)
            outs[4 * n][b] = dmod
            db_ada = db_ada + dmod
            dg1 = dg1 + q1 * (1.0 + mod[M_SC1:M_SC1 + 1])
            dg2 = dg2 + q2 * (1.0 + mod[M_SC2:M_SC2 + 1])
            dgf = dgf + gb[0:1]
            loss_lanes = loss_lanes + gb[2:3]
            mixer = mixer + gb[16:32]
        d_a_param = mixer[7:8] * _sigmoid_tail(vec_ref[V_A_PARAM:V_A_PARAM + 1, :])
        grads = [dg1, dg2, mixer[4:5], mixer[5:6], mixer[6:7], d_a_param, mixer[8:9], mixer[9:10], dgf,
                 db_ada, mixer[0:4]]
        for k in range(n):
            outs[k][...] = grads[k]
            outs[n + k][...], outs[2 * n + k][...], outs[3 * n + k][...] = _adam(
                w_refs[k][...], grads[k], m_refs[k][...], v_refs[k][...])
        outs[4 * n + 1][...] = jnp.broadcast_to(jnp.sum(loss_lanes, axis=1, keepdims=True), (8, 128))

    shapes = [jax.ShapeDtypeStruct(w.shape, F32) for w in ws]
    return pl.pallas_call(
        body, name="small_finish",
        out_shape=shapes * 4 + [jax.ShapeDtypeStruct((N_DEV, 6, D), F32), jax.ShapeDtypeStruct((8, 128), F32)],
    )(gathered, mod_all, vecs, *ws, *ms, *vs)


def _pad_rows(a, rows):
    return jnp.pad(a, ((0, rows - a.shape[0]), (0, 0)))


def kernel(x, c, norm_mix_g, norm_mlp_g, w_ada, b_ada, w_in, conv_w, conv_b, w_rg_a, b_rg_a, w_rg_x, b_rg_x, a_param, w_branch_a, w_pool, b_pool, pool_scale, w_branch_b, w_out, w_up, w_down, final_g, loss_target, m_norm_mix_g, m_norm_mlp_g, m_w_ada, m_b_ada, m_w_in, m_conv_w, m_conv_b, m_w_rg_a, m_b_rg_a, m_w_rg_x, m_b_rg_x, m_a_param, m_w_branch_a, m_w_pool, m_b_pool, m_pool_scale, m_w_branch_b, m_w_out, m_w_up, m_w_down, m_final_g, v_norm_mix_g, v_norm_mlp_g, v_w_ada, v_b_ada, v_w_in, v_conv_w, v_conv_b, v_w_rg_a, v_b_rg_a, v_w_rg_x, v_b_rg_x, v_a_param, v_w_branch_a, v_w_pool, v_b_pool, v_pool_scale, v_w_branch_b, v_w_out, v_w_up, v_w_down, v_final_g):
    me = 4 * lax.axis_index("x") + 2 * lax.axis_index("y") + lax.axis_index("c")
    s = x.shape[1]
    x2d = x.reshape(s, D)
    target = loss_target.reshape(s, D)
    n_ada = w_ada.shape[2]

    sharded = dict(w_in=(w_in[0], 1), w_up=(w_up[0], 1), w_down=(w_down[0], 0), w_branch_a=(w_branch_a[0], 0),
                   w_branch_b=(w_branch_b[0], 0), w_out=(w_out[0], 0), w_rg_a=(w_rg_a[0], 1), w_rg_x=(w_rg_x[0], 1),
                   w_pool=(w_pool[0], 1))
    kind = {k: v[1] for k, v in sharded.items()}
    shard = {k: v[0].astype(BF16) for k, v in sharded.items()}

    conv_w_full, c_rows = _all_gather([_pad_rows(conv_w[0], 8), _pad_rows(c, 8)], [1, 0], "gather_c")
    c_all = c_rows.reshape(N_DEV, 8, D)[:, 0, :]
    b_ada_cols = lax.dynamic_slice(b_ada, (0, me * n_ada), (1, n_ada))
    mod_part = _ada_fwd(c_all, w_ada[0], b_ada_cols)
    mod_parts, = _all_gather([mod_part], [0], "gather_mod")

    first_names = ["w_in", "w_rg_a", "w_rg_x", "w_pool"]
    branch_names = ["w_branch_a", "w_branch_b", "w_out"]
    mlp_names = ["w_up", "w_down"]

    def gather(group, after, name):
        return _Gather([shard[k] for k in group], [kind[k] for k in group], after, name)

    g_first = gather(first_names, mod_parts, "gather_first")
    g_branch = gather(branch_names, g_first.token, "gather_branch")
    g_mlp = gather(mlp_names, g_branch.token, "gather_mlp")

    mod_all = jnp.transpose(mod_parts.reshape(N_DEV, N_DEV, n_ada), (1, 0, 2)).reshape(N_DEV, 6, D)
    mod_all = jnp.pad(mod_all, ((0, 0), (0, 2), (0, 0)))
    modr = lax.dynamic_index_in_dim(mod_all, me, 0, keepdims=False)
    vecs = _pack_vecs(conv_w_full, [conv_b, b_rg_a, b_rg_x, a_param, b_pool, pool_scale,
                                    norm_mix_g, norm_mlp_g, final_g.reshape(1, D)])
    vecs = _after(vecs, g_mlp.token)
    g_first.forward(vecs)
    wg = dict(zip(first_names, g_first.finish(g_first.token)))

    proj, h1 = _proj_fwd(x2d, modr, vecs, wg["w_in"])
    g_branch.forward(h1)
    xr, hr, za, p, pooled, *gates = _mix_fwd(proj, _after(vecs, g_branch.token), wg["w_rg_a"], wg["w_rg_x"], wg["w_pool"])
    g_mlp.forward(za)
    wg.update(zip(branch_names, g_branch.finish(g_mlp.token)))
    ba, bb, merged, o, x2, h2 = _branch_fwd(za, pooled, proj, x2d, modr, vecs,
                                            wg["w_branch_a"], wg["w_branch_b"], wg["w_out"])
    wg.update(zip(mlp_names, g_mlp.finish(h2)))
    ru, dx3, d_dn, small_f = _mlp_fwd(h2, x2, target, modr, vecs, wg["w_up"], wg["w_down"])

    near = _near(lax.axis_index("x"), lax.axis_index("y"), lax.axis_index("c"))
    own4 = jnp.stack([me, near[1][1], near[2][1], near[3][1]]).astype(jnp.int32)

    def scatter(group, partial, after, name):
        return _Scatter([partial[k] for k in group], [kind[k] for k in group], after, name)

    dup, dx2, do, small_m = _mlp_bwd(d_dn, ru, x2, dx3, o, modr, vecs, wg["w_up"], wg["w_down"])
    partial = dict(w_up=_wgrad(h2, dup, "wgrad_up"), w_down=_wgrad(ru, d_dn, "wgrad_down", square_a=True))
    s_mlp = scatter(mlp_names, partial, dx2, "scatter_mlp")

    dba, dbb, dgates, dza, dpooled = _branch_bwd(do, proj, ba, bb, wg["w_branch_a"], wg["w_branch_b"], wg["w_out"],
                                                 dep=s_mlp.token)
    s_mlp.combine_and_send(own4, dza)
    dproj, dw_rg_a, dw_rg_x, dw_pool, small_x = _mix_bwd(dza, dpooled, proj, xr, hr, p, gates, dgates,
                                                         _after(vecs, s_mlp.token),
                                                         wg["w_rg_a"], wg["w_rg_x"], wg["w_pool"])
    partial.update(w_branch_a=_wgrad(za, dba, "wgrad_branch_a"), w_branch_b=_wgrad(pooled, dbb, "wgrad_branch_b"),
                   w_out=_wgrad(merged, do, "wgrad_out"),
                   w_rg_a=dw_rg_a.astype(BF16), w_rg_x=dw_rg_x.astype(BF16), w_pool=dw_pool.astype(BF16))
    mixer_names = ["w_rg_a", "w_rg_x", "w_pool", "w_branch_a", "w_branch_b", "w_out"]
    s_mixer = scatter(mixer_names, partial, s_mlp.token, "scatter_mixer")

    partial["w_in"] = _wgrad(h1, dproj, "wgrad_in", dep=s_mixer.token)
    s_in = scatter(["w_in"], partial, s_mixer.token, "scatter_in")
    s_mixer.combine_and_send(own4, s_in.token)
    s_in.combine_and_send(own4, s_mixer.token)
    grad_x, small_p = _proj_bwd(dproj, x2d, dx2, _after(modr, s_in.token), vecs, wg["w_in"])

    locals_ = dict(w_in=(w_in, m_w_in, v_w_in), w_up=(w_up, m_w_up, v_w_up), w_down=(w_down, m_w_down, v_w_down),
                   w_branch_a=(w_branch_a, m_w_branch_a, v_w_branch_a),
                   w_branch_b=(w_branch_b, m_w_branch_b, v_w_branch_b), w_out=(w_out, m_w_out, v_w_out),
                   w_rg_a=(w_rg_a, m_w_rg_a, v_w_rg_a), w_rg_x=(w_rg_x, m_w_rg_x, v_w_rg_x),
                   w_pool=(w_pool, m_w_pool, v_w_pool))
    res = {}

    def finish(group, exchange, after):
        chip_sums, arrivals = exchange.finish(after)
        for k, cs, ar in zip(group, chip_sums, arrivals):
            w, m, v = locals_[k]
            shape2d = w.reshape(-1, w.shape[-1]).shape
            outs = _sum_slots_adam(cs.reshape(N_NEAR, *shape2d), ar.reshape(N_NEAR - 1, *shape2d),
                                   w.reshape(shape2d), m.reshape(shape2d), v.reshape(shape2d), "adam_" + k)
            res[k] = [t.reshape(w.shape) for t in outs]
        return res[group[-1]][0]

    done = finish(mlp_names, s_mlp, grad_x)
    done = finish(mixer_names, s_mixer, done)
    done = finish(["w_in"], s_in, done)

    small = jnp.concatenate([small_f, small_m, small_x, small_p], axis=0)
    small_all, = _all_gather([small], [0], "gather_small", dep=done)
    small_all = small_all.reshape(N_DEV, N_SMALL, D)

    def embed(cw):
        return lax.dynamic_update_slice(jnp.zeros((4, D), F32), cw[0], (0, me * (D // N_DEV)))

    def smalls(ng, nl, cb, bra, brx, ap, bp, ps, fg, ba_, cw):
        return [ng, nl, cb, bra, brx, ap, bp, ps, fg.reshape(1, D), ba_.reshape(6, D), embed(cw)]

    small_names = ["norm_mix_g", "norm_mlp_g", "conv_b", "b_rg_a", "b_rg_x", "a_param", "b_pool", "pool_scale",
                   "final_g", "b_ada", "conv_w"]
    fin = _small_finish(
        small_all, mod_all, vecs,
        smalls(norm_mix_g, norm_mlp_g, conv_b, b_rg_a, b_rg_x, a_param, b_pool, pool_scale, final_g, b_ada, conv_w),
        smalls(m_norm_mix_g, m_norm_mlp_g, m_conv_b, m_b_rg_a, m_b_rg_x, m_a_param, m_b_pool, m_pool_scale,
               m_final_g, m_b_ada, m_conv_w),
        smalls(v_norm_mix_g, v_norm_mlp_g, v_conv_b, v_b_rg_a, v_b_rg_x, v_a_param, v_b_pool, v_pool_scale,
               v_final_g, v_b_ada, v_conv_w))
    dmod_all, loss_tile = fin[4 * N_SMALL_PARAMS], fin[4 * N_SMALL_PARAMS + 1]
    dmod_cols = lax.dynamic_slice(dmod_all.reshape(N_DEV, 6 * D), (0, me * n_ada), (N_DEV, n_ada))
    res["w_ada"] = [t.reshape(w_ada.shape) for t in _ada_bwd_adam(c_all, dmod_cols, w_ada[0], m_w_ada[0], v_w_ada[0])]

    def final_shape(k, t):
        if k == "final_g":
            return t.reshape(D)
        if k == "b_ada":
            return t.reshape(1, 6 * D)
        if k == "conv_w":
            return lax.dynamic_slice(t, (0, me * (D // N_DEV)), (4, D // N_DEV)).reshape(conv_w.shape)
        return t

    for i, k in enumerate(small_names):
        res[k] = [final_shape(k, fin[which * N_SMALL_PARAMS + i]) for which in range(4)]
    order = ["norm_mix_g", "norm_mlp_g", "w_ada", "b_ada", "w_in", "conv_w", "conv_b", "w_rg_a", "b_rg_a", "w_rg_x",
             "b_rg_x", "a_param", "w_branch_a", "w_pool", "b_pool", "pool_scale", "w_branch_b", "w_out", "w_up",
             "w_down", "final_g"]
    outs = [loss_tile[0, 0], grad_x.reshape(x.shape)]
    for which in range(4):
        for k in order:
            outs.append(res[k][which])
    return tuple(outs)
```

```python
import functools

import jax
import jax.numpy as jnp
from jax import lax
from jax.experimental import pallas as pl
from jax.experimental.pallas import tpu as pltpu

F32 = jnp.float32
BF16 = jnp.bfloat16
MESH = pl.DeviceIdType.MESH

N_DEV = 8
D = 1024
N_GROUPS = 4
GW = D // N_GROUPS
D_IN = 5 * D
D_FF = 4 * D
POOL_WINDOWS = (2, 4, 8, 16)
HALO_X = 8
HALO_U = 16
EPS = 1e-6
C_RG = 8.0
ADAM_LR, ADAM_B1, ADAM_B2, ADAM_EPS, ADAM_WD, ADAM_STEP = 0.001, 0.9, 0.999, 1e-08, 0.01, 10

V7X_VMEM_LIMIT = 56 * 1024 * 1024

V_CONV_W, V_CONV_B, V_B_RG_A, V_B_RG_X, V_A_PARAM, V_B_POOL, V_POOL_SCALE, V_G1, V_G2, V_GF = 0, 4, 5, 6, 7, 8, 9, 10, 11, 12
M_SH1, M_SC1, M_GT1, M_SH2, M_SC2, M_GT2 = 0, 1, 2, 3, 4, 5

TM_PROJ = 512
TM_MIX = 256
TM_BRANCH = 256
TM_MLP = 512
TM_MLP_BWD = 256
TS_WGRAD = 1024


def _params(semantics):
    return pltpu.CompilerParams(dimension_semantics=semantics, vmem_limit_bytes=V7X_VMEM_LIMIT)


def _resident(shape):
    return pl.BlockSpec(shape, lambda *_: (0,) * len(shape), pipeline_mode=pl.Buffered(1))


def _dot(a, b):
    return jnp.dot(a, b, preferred_element_type=F32)


def _dot_nt(a, b):
    return lax.dot_general(a, b, (((1,), (1,)), ((), ())), preferred_element_type=F32)


def _dot_tn(a, b):
    return lax.dot_general(a, b, (((0,), (0,)), ((), ())), preferred_element_type=F32)


def _sigmoid(x):
    return 0.5 * jnp.tanh(0.5 * x) + 0.5


def _sigmoid_tail(x):
    return 1.0 / (1.0 + jnp.exp(-x))


def _gelu_and_grad(x):
    k = 0.7978845608028654
    x2 = x * x
    t = jnp.tanh(k * (x + 0.044715 * x * x2))
    g = 0.5 * x * (1.0 + t)
    dg = 0.5 * (1.0 + t) + 0.5 * x * (1.0 - t * t) * (k * (1.0 + 3.0 * 0.044715 * x2))
    return g, dg


def _softplus(a):
    e = jnp.exp(-jnp.abs(a))
    u = 1.0 + e
    log1p_e = jnp.where(u == 1.0, e, jnp.log(u) * e / jnp.where(u == 1.0, 1.0, u - 1.0))
    return jnp.maximum(a, 0.0) + log1p_e


def _neg_expm1(z):
    series = -(z * (1.0 + z * (0.5 + z * (1.0 / 6.0 + z * (1.0 / 24.0 + z * (1.0 / 120.0))))))
    return jnp.where(z > -0.1, series, 1.0 - jnp.exp(z))


def _shift_down(x, k):
    return pltpu.roll(x, k, 0)


def _shift_up(x, k):
    return pltpu.roll(x, x.shape[0] - k, 0)


def _rglru_gates(xr, w_a, w_x, b_a, b_x, a_param, is_t0):
    xb = xr.astype(BF16)
    ra = _sigmoid(_dot(xb, w_a) + b_a)
    ri = _sigmoid(_dot(xb, w_x) + b_x)
    sp = _softplus(a_param)
    log_a = (-C_RG) * ra * sp
    a = jnp.exp(log_a)
    mult = jnp.where(is_t0, 1.0, jnp.sqrt(_neg_expm1(2.0 * log_a)))
    return ra, ri, sp, a, mult


SUBLANES = 8


LANES = 128


def _scan_strip(a, b, carry, scr, down):
    t = b.shape[0]
    g = t // SUBLANES
    a3 = a.reshape(g, SUBLANES, LANES)
    b3 = b.reshape(g, SUBLANES, LANES)
    sub = lax.broadcasted_iota(jnp.int32, (g, SUBLANES, LANES), 1)
    for k in (1, 2, 4):
        keep = sub >= k if down else sub < SUBLANES - k
        shift = k if down else SUBLANES - k
        b3 = b3 + a3 * jnp.where(keep, pltpu.roll(b3, shift, 1), 0.0)
        a3 = a3 * jnp.where(keep, pltpu.roll(a3, shift, 1), 1.0)
    scr[0] = a3.reshape(t, LANES)
    scr[1] = b3.reshape(t, LANES)
    end_row = SUBLANES - 1 if down else 0
    ag = scr[0, pl.ds(end_row, g, stride=SUBLANES), :]
    bg = scr[1, pl.ds(end_row, g, stride=SUBLANES), :]
    rg = lax.broadcasted_iota(jnp.int32, (g, LANES), 0)
    edge = 0 if down else g - 1
    bg = bg + jnp.where(rg == edge, ag * carry, 0.0)
    k = 1
    while k < g:
        keep = rg >= k if down else rg < g - k
        shift = k if down else g - k
        bg = bg + ag * jnp.where(keep, pltpu.roll(bg, shift, 0), 0.0)
        if 2 * k < g:
            ag = ag * pltpu.roll(ag, shift, 0)
        k *= 2
    entering = jnp.where(rg != edge, pltpu.roll(bg, 1 if down else g - 1, 0), carry)
    for r in range(SUBLANES):
        scr[2, pl.ds(r, g, stride=SUBLANES), :] = entering
    return scr[1] + scr[0] * scr[2], bg[g - 1:g, :]


def _scan_strips(a, b, carry, scr, down):
    outs = [_scan_strip(a[:, c:c + LANES], b[:, c:c + LANES], carry[:, c:c + LANES], scr, down)
            for c in range(0, b.shape[1], LANES)]
    return jnp.concatenate([o[0] for o in outs], axis=1), jnp.concatenate([o[1] for o in outs], axis=1)


def _scan_down(a, b, carry, scr):
    return _scan_strips(a, b, carry, scr, True)


def _scan_up(m, b, carry, scr):
    return _scan_strips(m, b, carry, scr, False)[0]


def _conv_taps(x_ext):
    return [_shift_down(x_ext, 3 - j)[HALO_X:] if j < 3 else x_ext[HALO_X:] for j in range(4)]


def _proj_fwd(x, modr, vecs, w_in):
    s = x.shape[0]
    tm = min(TM_PROJ, s)

    def body(x_ref, mod_ref, vec_ref, w_ref, proj_ref, h1_ref):
        xv = x_ref[...]
        r = lax.rsqrt(jnp.mean(xv * xv, axis=-1, keepdims=True) + EPS)
        gain = vec_ref[V_G1:V_G1 + 1, :] * (1.0 + mod_ref[M_SC1:M_SC1 + 1, :])
        h = (xv * r * gain + mod_ref[M_SH1:M_SH1 + 1, :]).astype(BF16)
        h1_ref[...] = h
        for c in range(D_IN // D):
            proj_ref[:, c * D:(c + 1) * D] = _dot(h, w_ref[:, c * D:(c + 1) * D])

    return pl.pallas_call(
        body, name="proj_fwd", grid=(s // tm,),
        in_specs=[pl.BlockSpec((tm, D), lambda i: (i, 0)),
                  pl.BlockSpec((8, D), lambda i: (0, 0)),
                  pl.BlockSpec((16, D), lambda i: (0, 0)),
                  _resident((D, D_IN))],
        out_specs=[pl.BlockSpec((tm, D_IN), lambda i: (i, 0)),
                   pl.BlockSpec((tm, D), lambda i: (i, 0))],
        out_shape=[jax.ShapeDtypeStruct((s, D_IN), F32), jax.ShapeDtypeStruct((s, D), BF16)],
        compiler_params=_params(("parallel",)),
    )(x, modr, vecs, w_in)


def _mix_fwd(proj, vecs, w_rg_a, w_rg_x, w_pool):
    s = proj.shape[0]
    tm = min(TM_MIX, s)
    nb = s // tm

    def body(xh_ref, x_ref, y_ref, uh_ref, u_ref, vec_ref, wa_ref, wx_ref, wp_ref,
             xr_ref, hr_ref, za_ref, p_ref, pooled_ref, a_ref, mult_ref, ra_ref, ri_ref, carry_ref, scan_scr):
        i = pl.program_id(0)
        first = i == 0

        @pl.when(first)
        def _():
            carry_ref[...] = jnp.zeros_like(carry_ref)

        row = lax.broadcasted_iota(jnp.int32, (tm, GW), 0)
        is_t0 = jnp.logical_and(first, row == 0)
        t_glob = (row + i * tm + 1).astype(F32)
        for g in range(N_GROUPS):
            cs = slice(g * GW, (g + 1) * GW)
            vec = vec_ref[:, cs]
            xh = jnp.where(first, 0.0, xh_ref[:, cs])
            taps = _conv_taps(jnp.concatenate([xh, x_ref[:, cs]], axis=0))
            xr = vec[V_CONV_B:V_CONV_B + 1]
            for j in range(4):
                xr = xr + vec[V_CONV_W + j:V_CONV_W + j + 1] * taps[j]
            xr_ref[:, cs] = xr
            ra, ri, _, a, mult = _rglru_gates(
                xr, wa_ref[g], wx_ref[g], vec[V_B_RG_A:V_B_RG_A + 1], vec[V_B_RG_X:V_B_RG_X + 1],
                vec[V_A_PARAM:V_A_PARAM + 1], is_t0)
            a_ref[:, cs] = a
            mult_ref[:, cs] = mult
            ra_ref[:, cs] = ra.astype(BF16)
            ri_ref[:, cs] = ri.astype(BF16)
            h, last = _scan_down(a, xr * ri * mult, carry_ref[0:1, cs], scan_scr)
            hr_ref[:, cs] = h
            carry_ref[0:1, cs] = last
            ga, _ = _gelu_and_grad(y_ref[:, cs])
            za_ref[:, cs] = (ga * h).astype(BF16)
            uh = jnp.where(first, 0.0, uh_ref[:, cs])
            sm = jnp.concatenate([uh, u_ref[:, cs]], axis=0)
            k = 1
            while k < POOL_WINDOWS[g]:
                sm = sm + _shift_down(sm, k)
                k *= 2
            cnt = jnp.minimum(t_glob, float(POOL_WINDOWS[g]))
            p = (sm[HALO_U:] / cnt - u_ref[:, cs]).astype(BF16)
            p_ref[:, cs] = p
            pb = _dot(p, wp_ref[g]) + vec[V_B_POOL:V_B_POOL + 1]
            pooled_ref[:, cs] = (pb * vec[V_POOL_SCALE:V_POOL_SCALE + 1]).astype(BF16)

    col = lambda k: (lambda i: (i, k))
    wspec = pl.BlockSpec((N_GROUPS, GW, GW), lambda i: (0, 0, 0))
    return pl.pallas_call(
        body, name="mix_fwd", grid=(nb,),
        in_specs=[pl.BlockSpec((HALO_X, D), lambda i: (jnp.maximum(i * (tm // HALO_X) - 1, 0), 0)),
                  pl.BlockSpec((tm, D), col(0)),
                  pl.BlockSpec((tm, D), col(1)),
                  pl.BlockSpec((HALO_U, D), lambda i: (jnp.maximum(i * (tm // HALO_U) - 1, 0), 2)),
                  pl.BlockSpec((tm, D), col(2)),
                  pl.BlockSpec((16, D), lambda i: (0, 0)),
                  wspec, wspec, wspec],
        out_specs=[pl.BlockSpec((tm, D), lambda i: (i, 0))] * 9,
        out_shape=[jax.ShapeDtypeStruct((s, D), F32), jax.ShapeDtypeStruct((s, D), F32),
                   jax.ShapeDtypeStruct((s, D), BF16), jax.ShapeDtypeStruct((s, D), BF16),
                   jax.ShapeDtypeStruct((s, D), BF16),
                   jax.ShapeDtypeStruct((s, D), F32), jax.ShapeDtypeStruct((s, D), F32),
                   jax.ShapeDtypeStruct((s, D), BF16), jax.ShapeDtypeStruct((s, D), BF16)],
        scratch_shapes=[pltpu.VMEM((8, D), F32), pltpu.VMEM((3, tm, LANES), F32)],
        compiler_params=_params(("arbitrary",)),
    )(proj, proj, proj, proj, proj, vecs, w_rg_a, w_rg_x, w_pool)


def _branch_fwd(za, pooled, proj, x, modr, vecs, w_a, w_b, w_out):
    s = x.shape[0]
    tm = min(TM_BRANCH, s)

    def body(za_ref, pooled_ref, ga_ref, gb_ref, x_ref, mod_ref, vec_ref, wa_ref, wb_ref, wo_ref,
             ba_ref, bb_ref, merged_ref, o_ref, x2_ref, h2_ref):
        ba = _dot(za_ref[...], wa_ref[...])
        bb = _dot(pooled_ref[...], wb_ref[...])
        ba_ref[...] = ba.astype(BF16)
        bb_ref[...] = bb.astype(BF16)
        merged = (_sigmoid(ga_ref[...]) * ba + _sigmoid(gb_ref[...]) * bb).astype(BF16)
        merged_ref[...] = merged
        o = _dot(merged, wo_ref[...])
        o_ref[...] = o.astype(BF16)
        x2 = x_ref[...] + mod_ref[M_GT1:M_GT1 + 1, :] * o
        x2_ref[...] = x2
        r = lax.rsqrt(jnp.mean(x2 * x2, axis=-1, keepdims=True) + EPS)
        gain = vec_ref[V_G2:V_G2 + 1, :] * (1.0 + mod_ref[M_SC2:M_SC2 + 1, :])
        h2_ref[...] = (x2 * r * gain + mod_ref[M_SH2:M_SH2 + 1, :]).astype(BF16)

    tok = pl.BlockSpec((tm, D), lambda i: (i, 0))
    wspec = pl.BlockSpec((D, D), lambda i: (0, 0))
    sd = lambda dt: jax.ShapeDtypeStruct((s, D), dt)
    return pl.pallas_call(
        body, name="branch_fwd", grid=(s // tm,),
        in_specs=[tok, tok,
                  pl.BlockSpec((tm, D), lambda i: (i, 3)), pl.BlockSpec((tm, D), lambda i: (i, 4)),
                  tok, pl.BlockSpec((8, D), lambda i: (0, 0)), pl.BlockSpec((16, D), lambda i: (0, 0)),
                  wspec, wspec, wspec],
        out_specs=[tok] * 6,
        out_shape=[sd(BF16), sd(BF16), sd(BF16), sd(BF16), sd(F32), sd(BF16)],
        compiler_params=_params(("parallel",)),
    )(za, pooled, proj, proj, x, modr, vecs, w_a, w_b, w_out)


def _mlp_fwd(h2, x2, target, modr, vecs, w_up, w_down):
    s = x2.shape[0]
    tm = min(TM_MLP, s)

    def body(h2_ref, x2_ref, tgt_ref, mod_ref, vec_ref, wu_ref, wd_ref,
             ru_ref, dx3_ref, ddn_ref, small_ref):
        @pl.when(pl.program_id(0) == 0)
        def _():
            small_ref[...] = jnp.zeros_like(small_ref)

        h2 = h2_ref[...]
        dn = None
        for c in range(D_FF // D):
            cs = slice(c * D, (c + 1) * D)
            ru = jnp.maximum(_dot(h2, wu_ref[:, cs]), 0.0)
            ru_ref[:, cs] = ru.astype(BF16)
            part = _dot((ru * ru).astype(BF16), wd_ref[cs, :])
            dn = part if dn is None else dn + part
        gt2 = mod_ref[M_GT2:M_GT2 + 1, :]
        gf = vec_ref[V_GF:V_GF + 1, :]
        x3 = x2_ref[...] + gt2 * dn
        r3 = lax.rsqrt(jnp.mean(x3 * x3, axis=-1, keepdims=True) + EPS)
        n3 = x3 * r3
        err = n3 * gf - tgt_ref[...]
        dy = err * (1.0 / D)
        dn3 = dy * gf
        dx3 = r3 * (dn3 - n3 * jnp.mean(dn3 * n3, axis=-1, keepdims=True))
        dx3_ref[...] = dx3
        ddn_ref[...] = (dx3 * gt2).astype(BF16)
        small_ref[0:1, :] += jnp.sum(dy * n3, axis=0, keepdims=True)
        small_ref[1:2, :] += jnp.sum(dx3 * dn, axis=0, keepdims=True)
        small_ref[2:3, :] += (0.5 / D) * jnp.sum(err * err, axis=0, keepdims=True)

    tok = pl.BlockSpec((tm, D), lambda i: (i, 0))
    return pl.pallas_call(
        body, name="mlp_fwd", grid=(s // tm,),
        in_specs=[tok, tok, tok,
                  pl.BlockSpec((8, D), lambda i: (0, 0)), pl.BlockSpec((16, D), lambda i: (0, 0)),
                  _resident((D, D_FF)), _resident((D_FF, D))],
        out_specs=[pl.BlockSpec((tm, D_FF), lambda i: (i, 0)), tok, tok,
                   pl.BlockSpec((8, D), lambda i: (0, 0))],
        out_shape=[jax.ShapeDtypeStruct((s, D_FF), BF16), jax.ShapeDtypeStruct((s, D), F32),
                   jax.ShapeDtypeStruct((s, D), BF16), jax.ShapeDtypeStruct((8, D), F32)],
        compiler_params=_params(("arbitrary",)),
    )(h2, x2, target, modr, vecs, w_up, w_down)


def _mlp_bwd(d_dn, ru, x2, dx3, o, modr, vecs, w_up, w_down):
    s = x2.shape[0]
    tm = min(TM_MLP_BWD, s)

    def body(ddn_ref, ru_ref, x2_ref, dx3_ref, o_ref, mod_ref, vec_ref, wu_ref, wd_ref,
             dup_ref, dx2_ref, do_ref, small_ref):
        @pl.when(pl.program_id(0) == 0)
        def _():
            small_ref[...] = jnp.zeros_like(small_ref)

        ddn = ddn_ref[...]
        dh2 = None
        for c in range(D_FF // D):
            cs = slice(c * D, (c + 1) * D)
            dff = _dot_nt(ddn, wd_ref[cs, :])
            dup = (dff * (2.0 * ru_ref[:, cs].astype(F32))).astype(BF16)
            dup_ref[:, cs] = dup
            part = _dot_nt(dup, wu_ref[:, cs])
            dh2 = part if dh2 is None else dh2 + part
        x2 = x2_ref[...]
        r2 = lax.rsqrt(jnp.mean(x2 * x2, axis=-1, keepdims=True) + EPS)
        xn2 = x2 * r2
        gain = vec_ref[V_G2:V_G2 + 1, :] * (1.0 + mod_ref[M_SC2:M_SC2 + 1, :])
        dxn2 = dh2 * gain
        dx2 = dx3_ref[...] + r2 * (dxn2 - xn2 * jnp.mean(dxn2 * xn2, axis=-1, keepdims=True))
        dx2_ref[...] = dx2
        do_ref[...] = (dx2 * mod_ref[M_GT1:M_GT1 + 1, :]).astype(BF16)
        small_ref[0:1, :] += jnp.sum(dh2, axis=0, keepdims=True)
        small_ref[1:2, :] += jnp.sum(dh2 * xn2, axis=0, keepdims=True)
        small_ref[2:3, :] += jnp.sum(dx2 * o_ref[...].astype(F32), axis=0, keepdims=True)

    tok = pl.BlockSpec((tm, D), lambda i: (i, 0))
    wide = pl.BlockSpec((tm, D_FF), lambda i: (i, 0))
    return pl.pallas_call(
        body, name="mlp_bwd", grid=(s // tm,),
        in_specs=[tok, wide, tok, tok, tok,
                  pl.BlockSpec((8, D), lambda i: (0, 0)), pl.BlockSpec((16, D), lambda i: (0, 0)),
                  _resident((D, D_FF)), _resident((D_FF, D))],
        out_specs=[wide, tok, tok, pl.BlockSpec((8, D), lambda i: (0, 0))],
        out_shape=[jax.ShapeDtypeStruct((s, D_FF), BF16), jax.ShapeDtypeStruct((s, D), F32),
                   jax.ShapeDtypeStruct((s, D), BF16), jax.ShapeDtypeStruct((8, D), F32)],
        compiler_params=_params(("arbitrary",)),
    )(d_dn, ru, x2, dx3, o, modr, vecs, w_up, w_down)


def _branch_bwd(do, proj, ba, bb, w_a, w_b, w_out, dep):
    s = do.shape[0]
    tm = min(TM_BRANCH, s)

    def body(do_ref, ga_ref, gb_ref, ba_ref, bb_ref, wa_ref, wb_ref, wo_ref, dep_ref,
             dba_ref, dbb_ref, dg_ref, dza_ref, dpooled_ref):
        dmerged = _dot_nt(do_ref[...], wo_ref[...])
        sa = _sigmoid(ga_ref[...])
        sb = _sigmoid(gb_ref[...])
        dba = (dmerged * sa).astype(BF16)
        dbb = (dmerged * sb).astype(BF16)
        dba_ref[...] = dba
        dbb_ref[...] = dbb
        dg_ref[:, :D] = (dmerged * ba_ref[...].astype(F32) * sa * (1.0 - sa)).astype(BF16)
        dg_ref[:, D:] = (dmerged * bb_ref[...].astype(F32) * sb * (1.0 - sb)).astype(BF16)
        dza_ref[...] = _dot_nt(dba, wa_ref[...])
        dpooled_ref[...] = _dot_nt(dbb, wb_ref[...])

    tok = pl.BlockSpec((tm, D), lambda i: (i, 0))
    wspec = pl.BlockSpec((D, D), lambda i: (0, 0))
    sd = lambda dt: jax.ShapeDtypeStruct((s, D), dt)
    return pl.pallas_call(
        body, name="branch_bwd", grid=(s // tm,),
        in_specs=[tok, pl.BlockSpec((tm, D), lambda i: (i, 3)), pl.BlockSpec((tm, D), lambda i: (i, 4)),
                  tok, tok, wspec, wspec, wspec, pl.BlockSpec(memory_space=pl.ANY)],
        out_specs=[tok, tok, pl.BlockSpec((tm, 2 * D), lambda i: (i, 0)), tok, tok],
        out_shape=[sd(BF16), sd(BF16), jax.ShapeDtypeStruct((s, 2 * D), BF16), sd(F32), sd(F32)],
        compiler_params=_params(("parallel",)),
    )(do, proj, proj, ba, bb, w_a, w_b, w_out, dep)


def _mix_bwd(dza, dpooled, proj, xr, hr, p, gates, dgates, vecs, w_rg_a, w_rg_x, w_pool):
    s = xr.shape[0]
    tm = min(TM_MIX, s)
    nb = s // tm

    def body(dza_ref, dpooled_ref, xh_ref, x_ref, y_ref, xr_ref, hh_ref, hr_ref, p_ref,
             a_ref, mult_ref, ra_ref, ri_ref, dg_ref, vec_ref, wa_ref, wx_ref, wp_ref,
             dproj_ref, dwa_ref, dwx_ref, dwp_ref, small_ref,
             scan_carry, dxr_carry, q_carry, scan_scr):
        i = pl.program_id(0)
        bi = nb - 1 - i
        first_t = bi == 0

        @pl.when(i == 0)
        def _():
            scan_carry[...] = jnp.zeros_like(scan_carry)
            dxr_carry[...] = jnp.zeros_like(dxr_carry)
            q_carry[...] = jnp.zeros_like(q_carry)
            dwa_ref[...] = jnp.zeros_like(dwa_ref)
            dwx_ref[...] = jnp.zeros_like(dwx_ref)
            dwp_ref[...] = jnp.zeros_like(dwp_ref)
            small_ref[...] = jnp.zeros_like(small_ref)

        row = lax.broadcasted_iota(jnp.int32, (tm, GW), 0)
        is_t0 = jnp.logical_and(first_t, row == 0)
        t_glob = (row + bi * tm + 1).astype(F32)
        colsum = lambda v: jnp.sum(v, axis=0, keepdims=True)
        for g in range(N_GROUPS):
            cs = slice(g * GW, (g + 1) * GW)
            vec = vec_ref[:, cs]
            xr = xr_ref[:, cs]
            hr = hr_ref[:, cs]
            dza = dza_ref[:, cs]
            ga, dga = _gelu_and_grad(y_ref[:, cs])
            dproj_ref[:, D + g * GW:D + (g + 1) * GW] = (dza * hr * dga).astype(BF16)
            dhr = dza * ga
            a = a_ref[:, cs]
            mult = mult_ref[:, cs]
            ra = ra_ref[:, cs].astype(F32)
            ri = ri_ref[:, cs].astype(F32)
            sp = _softplus(vec[V_A_PARAM:V_A_PARAM + 1])
            m = jnp.where(row == tm - 1, 1.0, _shift_up(a, 1))
            gsum = _scan_up(m, dhr, scan_carry[0:1, cs], scan_scr)
            scan_carry[0:1, cs] = a[0:1, :] * gsum[0:1, :]
            hh = jnp.where(first_t, 0.0, hh_ref[:, cs])
            hprev = _shift_down(jnp.concatenate([hh, hr], axis=0), 1)[8:]
            da = gsum * hprev
            dmult = jnp.where(is_t0, 0.0, gsum * xr * ri)
            dlog_a = da * a - dmult * a * a / mult
            dri = gsum * xr * mult
            dxr = gsum * ri * mult
            small_ref[7:8, cs] += colsum((-C_RG) * ra * dlog_a)
            dpa = (((-C_RG) * sp) * dlog_a * ra * (1.0 - ra))
            dpx = dri * ri * (1.0 - ri)
            small_ref[5:6, cs] += colsum(dpa)
            small_ref[6:7, cs] += colsum(dpx)
            dpa = dpa.astype(BF16)
            dpx = dpx.astype(BF16)
            xrb = xr.astype(BF16)
            dwa_ref[g] += _dot_tn(xrb, dpa)
            dwx_ref[g] += _dot_tn(xrb, dpx)
            dxr = dxr + _dot_nt(dpa, wa_ref[g]) + _dot_nt(dpx, wx_ref[g])
            small_ref[4:5, cs] += colsum(dxr)
            xh = jnp.where(first_t, 0.0, xh_ref[:, cs])
            taps = _conv_taps(jnp.concatenate([xh, x_ref[:, cs]], axis=0))
            dxr_ext = jnp.concatenate([dxr, dxr_carry[:, cs]], axis=0)
            dx = vec[V_CONV_W + 3:V_CONV_W + 4] * dxr
            for j in range(4):
                small_ref[j:j + 1, cs] += colsum(dxr * taps[j])
                if j < 3:
                    dx = dx + vec[V_CONV_W + j:V_CONV_W + j + 1] * _shift_up(dxr_ext, 3 - j)[:tm]
            dxr_carry[:, cs] = dxr[0:8, :]
            dproj_ref[:, cs] = dx.astype(BF16)
            pg = p_ref[:, cs]
            dpooled = dpooled_ref[:, cs]
            pb = _dot(pg, wp_ref[g]) + vec[V_B_POOL:V_B_POOL + 1]
            small_ref[9:10, cs] += colsum(dpooled * pb)
            dpb = dpooled * vec[V_POOL_SCALE:V_POOL_SCALE + 1]
            small_ref[8:9, cs] += colsum(dpb)
            dpbb = dpb.astype(BF16)
            dwp_ref[g] += _dot_tn(pg, dpbb)
            dp = _dot_nt(dpbb, wp_ref[g])
            q = dp / jnp.minimum(t_glob, float(POOL_WINDOWS[g]))
            sm = jnp.concatenate([q, q_carry[:, cs]], axis=0)
            k = 1
            while k < POOL_WINDOWS[g]:
                sm = sm + _shift_up(sm, k)
                k *= 2
            q_carry[:, cs] = q[0:HALO_U, :]
            dproj_ref[:, 2 * D + g * GW:2 * D + (g + 1) * GW] = (sm[:tm] - dp).astype(BF16)
        dproj_ref[:, 3 * D:] = dg_ref[...]

    rev = lambda i: nb - 1 - i
    tok = pl.BlockSpec((tm, D), lambda i: (rev(i), 0))
    col = lambda k: pl.BlockSpec((tm, D), lambda i: (rev(i), k))
    halo8 = lambda k: pl.BlockSpec((8, D), lambda i: (jnp.maximum(rev(i) * (tm // 8) - 1, 0), k))
    wspec = pl.BlockSpec((N_GROUPS, GW, GW), lambda i: (0, 0, 0))
    wshape = jax.ShapeDtypeStruct((N_GROUPS, GW, GW), F32)
    return pl.pallas_call(
        body, name="mix_bwd", grid=(nb,),
        in_specs=[tok, tok, halo8(0), col(0), col(1), tok, halo8(0), tok, tok, tok, tok, tok, tok,
                  pl.BlockSpec((tm, 2 * D), lambda i: (rev(i), 0)),
                  pl.BlockSpec((16, D), lambda i: (0, 0)), wspec, wspec, wspec],
        out_specs=[pl.BlockSpec((tm, D_IN), lambda i: (rev(i), 0)), wspec, wspec, wspec,
                   pl.BlockSpec((16, D), lambda i: (0, 0))],
        out_shape=[jax.ShapeDtypeStruct((s, D_IN), BF16), wshape, wshape, wshape,
                   jax.ShapeDtypeStruct((16, D), F32)],
        scratch_shapes=[pltpu.VMEM((8, D), F32), pltpu.VMEM((8, D), F32), pltpu.VMEM((HALO_U, D), F32),
                        pltpu.VMEM((3, tm, LANES), F32)],
        compiler_params=_params(("arbitrary",)),
    )(dza, dpooled, proj, proj, proj, xr, hr, hr, p, *gates, dgates, vecs, w_rg_a, w_rg_x, w_pool)


def _proj_bwd(dproj, x, dx2, modr, vecs, w_in):
    s = x.shape[0]
    tm = min(TM_PROJ, s)

    def body(dp_ref, x_ref, dx2_ref, mod_ref, vec_ref, w_ref, gx_ref, small_ref):
        @pl.when(pl.program_id(0) == 0)
        def _():
            small_ref[...] = jnp.zeros_like(small_ref)

        dh1 = None
        for c in range(D_IN // D):
            cs = slice(c * D, (c + 1) * D)
            part = _dot_nt(dp_ref[:, cs], w_ref[:, cs])
            dh1 = part if dh1 is None else dh1 + part
        xv = x_ref[...]
        r1 = lax.rsqrt(jnp.mean(xv * xv, axis=-1, keepdims=True) + EPS)
        xn1 = xv * r1
        gain = vec_ref[V_G1:V_G1 + 1, :] * (1.0 + mod_ref[M_SC1:M_SC1 + 1, :])
        dxn1 = dh1 * gain
        gx_ref[...] = dx2_ref[...] + r1 * (dxn1 - xn1 * jnp.mean(dxn1 * xn1, axis=-1, keepdims=True))
        small_ref[0:1, :] += jnp.sum(dh1, axis=0, keepdims=True)
        small_ref[1:2, :] += jnp.sum(dh1 * xn1, axis=0, keepdims=True)

    tok = pl.BlockSpec((tm, D), lambda i: (i, 0))
    return pl.pallas_call(
        body, name="proj_bwd", grid=(s // tm,),
        in_specs=[pl.BlockSpec((tm, D_IN), lambda i: (i, 0)), tok, tok,
                  pl.BlockSpec((8, D), lambda i: (0, 0)), pl.BlockSpec((16, D), lambda i: (0, 0)),
                  _resident((D, D_IN))],
        out_specs=[tok, pl.BlockSpec((8, D), lambda i: (0, 0))],
        out_shape=[jax.ShapeDtypeStruct((s, D), F32), jax.ShapeDtypeStruct((8, D), F32)],
        compiler_params=_params(("arbitrary",)),
    )(dproj, x, dx2, modr, vecs, w_in)


def _wgrad(a, b, name, square_a=False, dep=None):
    s, ka = a.shape
    n = b.shape[1]
    tka = ka if ka <= 1024 else ka // 2
    tn = n if n <= 1024 else n // 2
    ts = min(TS_WGRAD, s)
    ns = s // ts
    nc = 512
    deps = [] if dep is None else [dep]

    def body(a_ref, b_ref, *refs):
        out_ref, acc_ref = refs[-2:]
        t = pl.program_id(2)

        @pl.when(t == 0)
        def _():
            acc_ref[...] = jnp.zeros_like(acc_ref)

        av = a_ref[...]
        if square_a:
            af = av.astype(F32)
            av = (af * af).astype(BF16)
        for c in range(tn // nc):
            cs = slice(c * nc, (c + 1) * nc)
            acc_ref[:, cs] += _dot_tn(av, b_ref[:, cs])

        @pl.when(t == ns - 1)
        def _():
            out_ref[...] = acc_ref[...].astype(BF16)

    return pl.pallas_call(
        body, name=name, grid=(ka // tka, n // tn, ns),
        in_specs=[pl.BlockSpec((ts, tka), lambda i, j, t: (t, i)),
                  pl.BlockSpec((ts, tn), lambda i, j, t: (t, j))] + [pl.BlockSpec(memory_space=pl.ANY)] * len(deps),
        out_specs=pl.BlockSpec((tka, tn), lambda i, j, t: (i, j)),
        out_shape=jax.ShapeDtypeStruct((ka, n), BF16),
        scratch_shapes=[pltpu.VMEM((tka, tn), F32)],
        compiler_params=_params(("parallel", "parallel", "arbitrary")),
    )(a, b, *deps)


def _window(ref, kind, idx, size):
    start = pl.multiple_of(idx * size, size)
    if kind == 0:
        return ref.at[pl.ds(start, size)]
    if kind == 1:
        return ref.at[:, pl.ds(start, size)]
    return ref.at[:, :, pl.ds(start, size)]


def _mesh_place():
    x, y, c = lax.axis_index("x"), lax.axis_index("y"), lax.axis_index("c")
    return x, y, c, 4 * x + 2 * y + c


def _peer(x, y, c, q):
    px = 1 - x if q & 4 else x
    py = 1 - y if q & 2 else y
    pc = 1 - c if q & 1 else c
    return (px, py, pc), 4 * px + 2 * py + pc


def _all_gather(shards, kinds, name, dep=None):
    n = len(shards)
    deps = [] if dep is None else [dep]
    full_shapes = []
    for sh, kind in zip(shards, kinds):
        dims = list(sh.shape)
        dims[kind] *= N_DEV
        full_shapes.append(jax.ShapeDtypeStruct(tuple(dims), sh.dtype))

    def body(*refs):
        ins, outs = refs[:n], refs[n + len(deps):2 * n + len(deps)]
        send_sems, recv_sems, local_sems = refs[2 * n + len(deps):]
        x, y, c, me = _mesh_place()
        sends, recvs, locals_ = [], [], []
        for k in range(n):
            size = shards[k].shape[kinds[k]]
            mine = _window(outs[k], kinds[k], me, size)
            lc = pltpu.make_async_copy(ins[k], mine, local_sems.at[k])
            lc.start()
            locals_.append(lc)
            for q in range(1, N_DEV):
                peer, peer_idx = _peer(x, y, c, q)
                cp = pltpu.make_async_remote_copy(
                    src_ref=ins[k], dst_ref=mine, send_sem=send_sems.at[k, q], recv_sem=recv_sems.at[k, q],
                    device_id=peer, device_id_type=MESH)
                cp.start()
                sends.append(cp)
                recvs.append(pltpu.make_async_remote_copy(
                    src_ref=ins[k], dst_ref=_window(outs[k], kinds[k], peer_idx, size),
                    send_sem=send_sems.at[k, q], recv_sem=recv_sems.at[k, q],
                    device_id=peer, device_id_type=MESH))
        for cp in recvs:
            cp.wait_recv()
        for cp in sends:
            cp.wait_send()
        for lc in locals_:
            lc.wait()

    any_spec = pl.BlockSpec(memory_space=pl.ANY)
    return pl.pallas_call(
        body, name=name,
        in_specs=[any_spec] * (n + len(deps)), out_specs=[any_spec] * n, out_shape=full_shapes,
        scratch_shapes=[pltpu.SemaphoreType.DMA((n, N_DEV)), pltpu.SemaphoreType.DMA((n, N_DEV)),
                        pltpu.SemaphoreType.DMA((n,))],
    )(*shards, *deps)


_HBM = pl.BlockSpec(memory_space=pltpu.HBM)
_SEM = pl.BlockSpec(memory_space=pltpu.SEMAPHORE)
_EFFECT = pltpu.SideEffectType.DATAFLOW_SIDE_EFFECTING


N_NEAR = 4


def _near(x, y, c):
    out = [((x, y, 1 - c), 4 * x + 2 * y + 1 - c)]
    for j in (1, 2, 3):
        px = 1 - x if j & 2 else x
        py = 1 - y if j & 1 else y
        out.append(((px, py, c), 4 * px + 2 * py + c))
    return out


def _remote(src, dst, send_sems, recv_sems, slot, device):
    return pltpu.make_async_remote_copy(src_ref=src, dst_ref=dst, send_sem=send_sems.at[slot], recv_sem=recv_sems.at[slot],
                                        device_id=device, device_id_type=MESH)


def _split_call(name, arrays, sems_in, n_new_sems, after, emit):
    na, ns, nn = len(arrays), len(sems_in), len(n_new_sems)

    def body(*refs):
        emit(refs[:na], refs[na:na + ns], refs[na + ns + 1:na + ns + 1 + nn])
        refs[-1][...] = jnp.zeros_like(refs[-1])

    outs = pl.pallas_call(
        body, name=name,
        out_shape=(*[pltpu.SemaphoreType.DMA((m,)) for m in n_new_sems],
                   *[pltpu.HBM(a.shape, a.dtype) for a in arrays], jax.ShapeDtypeStruct((8, 128), F32)),
        in_specs=[_HBM] * na + [_SEM] * ns + [pl.BlockSpec(memory_space=pl.ANY)],
        out_specs=(*[_SEM] * nn, *[_HBM] * na, pl.BlockSpec(memory_space=pltpu.VMEM)),
        input_output_aliases={i: nn + i for i in range(na)},
        compiler_params=pltpu.CompilerParams(has_side_effects=_EFFECT),
    )(*[pltpu.with_memory_space_constraint(a, pltpu.HBM) for a in arrays], *sems_in, after)
    return list(outs[:nn]), list(outs[nn:nn + na]), outs[-1]


class _Gather:
    def __init__(self, shards, kinds, after, name):
        self.n, self.kinds, self.name = len(shards), kinds, name
        self.sizes = [s.shape[k] for s, k in zip(shards, kinds)]
        n = self.n
        lands = []
        for s, k in zip(shards, kinds):
            dims = list(s.shape)
            dims[k] *= N_DEV
            lands.append(lax.empty(tuple(dims), s.dtype))

        def emit(arr, _, new):
            x, y, c, me = _mesh_place()
            for k in range(n):
                pltpu.make_async_copy(arr[k], _window(arr[n + k], kinds[k], me, self.sizes[k]), new[2].at[k]).start()
            for k in range(n):
                mine = _window(arr[n + k], kinds[k], me, self.sizes[k])
                for j, (dev, _) in enumerate(_near(x, y, c)):
                    _remote(arr[k], mine, new[0], new[1], k * N_NEAR + j, dev).start()

        self.sems, self.arrays, self.token = _split_call(name + "_start", [*shards, *lands], [],
                                                         [n * N_NEAR, n * N_NEAR, n], after, emit)

    def forward(self, after):
        n, kinds, sizes = self.n, self.kinds, self.sizes

        def emit(arr, old, new):
            x, y, c, _ = _mesh_place()
            near = _near(x, y, c)
            for k in range(n):
                for j in (1, 2, 3):
                    dev, idx = near[j]
                    landed = _window(arr[n + k], kinds[k], idx, sizes[k])
                    _remote(arr[k], landed, old[0], old[1], k * N_NEAR + j, dev).wait_recv()
                    _remote(landed, landed, new[0], new[1], k * N_NEAR + j, near[0][0]).start()

        new, self.arrays, self.token = _split_call(self.name + "_forward", self.arrays, self.sems, [n * N_NEAR] * 2,
                                                   after, emit)
        self.sems = [*self.sems, *new]

    def finish(self, after):
        n, kinds, sizes = self.n, self.kinds, self.sizes

        def emit(arr, old, _):
            x, y, c, me = _mesh_place()
            near = _near(x, y, c)
            other_core = near[0][0]
            for k in range(n):
                win = lambda idx: _window(arr[n + k], kinds[k], idx, sizes[k])
                pltpu.make_async_copy(arr[k], win(me), old[2].at[k]).wait()
                for j, (dev, idx) in enumerate(near):
                    _remote(arr[k], win(me), old[0], old[1], k * N_NEAR + j, dev).wait_send()
                _remote(arr[k], win(near[0][1]), old[0], old[1], k * N_NEAR, other_core).wait_recv()
                for j in (1, 2, 3):
                    idx = near[j][1]
                    _remote(win(idx), win(idx), old[3], old[4], k * N_NEAR + j, other_core).wait_send()
                    _remote(arr[k], win(idx + 1 - 2 * c), old[3], old[4], k * N_NEAR + j, other_core).wait_recv()

        _, arrays, _ = _split_call(self.name + "_finish", self.arrays, self.sems, [], after, emit)
        return arrays[n:]


class _Scatter:
    def __init__(self, partials, kinds, after, name):
        self.n, self.kinds, self.name, self.partials = len(partials), kinds, name, partials
        self.sizes = [p.shape[k] // N_DEV for p, k in zip(partials, kinds)]
        n, sizes = self.n, self.sizes
        self.slot_shapes = []
        for p, k, size in zip(partials, kinds, sizes):
            dims = list(p.shape)
            dims[k] = size
            self.slot_shapes.append((N_NEAR, *dims))
        slots = [lax.empty(sh, p.dtype) for sh, p in zip(self.slot_shapes, partials)]

        def emit(arr, _, new):
            x, y, c, _ = _mesh_place()
            near = _near(x, y, c)
            for k in range(n):
                for j in range(N_NEAR):
                    owner = near[j][1] if j == 0 else near[j][1] + 1 - 2 * c
                    _remote(_window(arr[k], kinds[k], owner, sizes[k]), arr[n + k].at[j], new[0], new[1],
                            k * N_NEAR + j, near[0][0]).start()

        self.sems, self.arrays, self.token = _split_call(name + "_start", [*partials, *slots], [], [n * N_NEAR] * 2,
                                                         after, emit)

    def combine_and_send(self, own4, after):
        n, kinds, sizes = self.n, self.kinds, self.sizes

        def emit_wait(arr, old, _):
            x, y, c, _ = _mesh_place()
            near = _near(x, y, c)
            for k in range(n):
                for j in range(N_NEAR):
                    owner = near[j][1] if j == 0 else near[j][1] + 1 - 2 * c
                    cp = _remote(_window(arr[k], kinds[k], owner, sizes[k]), arr[n + k].at[j], old[0], old[1],
                                 k * N_NEAR + j, near[0][0])
                    cp.wait_send()
                    cp.wait_recv()

        _, arrays, _ = _split_call(self.name + "_landed", self.arrays, self.sems, [], after, emit_wait)
        chip_sums = _chip_sums(arrays[:n], arrays[n:], kinds, sizes, own4, self.name + "_combine")
        arrivals = [lax.empty((N_NEAR - 1, *sh[1:]), p.dtype) for sh, p in zip(self.slot_shapes, self.partials)]

        def emit_send(arr, _, new):
            x, y, c, _ = _mesh_place()
            near = _near(x, y, c)
            for k in range(n):
                for j in (1, 2, 3):
                    _remote(arr[k].at[j], arr[n + k].at[j - 1], new[0], new[1], k * N_NEAR + j, near[j][0]).start()

        self.sems, self.arrays, self.token = _split_call(self.name + "_send", [*chip_sums, *arrivals], [],
                                                         [n * N_NEAR] * 2, own4, emit_send)

    def finish(self, after):
        n = self.n

        def emit(arr, old, _):
            x, y, c, _ = _mesh_place()
            near = _near(x, y, c)
            for k in range(n):
                for j in (1, 2, 3):
                    cp = _remote(arr[k].at[j], arr[n + k].at[j - 1], old[0], old[1], k * N_NEAR + j, near[j][0])
                    cp.wait_send()
                    cp.wait_recv()

        _, arrays, _ = _split_call(self.name + "_finish", self.arrays, self.sems, [], after, emit)
        return arrays[:n], arrays[n:]


def _chip_sums(partials, slots, kinds, sizes, own4, name):
    n = len(partials)

    def body(own_ref, *refs):
        for k in range(n):
            refs[2 * n + k][...] = (refs[k][...].astype(F32) + refs[n + k][...].astype(F32)).astype(BF16)

    in_specs, slot_specs = [], []
    for p, s, kind, size in zip(partials, slots, kinds, sizes):
        block = list(p.shape)
        block[kind] = size
        nd = len(block)
        in_specs.append(pl.BlockSpec(tuple(block), functools.partial(
            lambda j, own, kind, nd: tuple(own[j] if d == kind else 0 for d in range(nd)), kind=kind, nd=nd)))
        slot_specs.append(pl.BlockSpec((None, *block), functools.partial(
            lambda j, own, nd: (j,) + (0,) * nd, nd=nd)))
    return pl.pallas_call(
        body, name=name,
        grid_spec=pltpu.PrefetchScalarGridSpec(num_scalar_prefetch=1, grid=(N_NEAR,),
                                               in_specs=in_specs + slot_specs, out_specs=slot_specs),
        out_shape=[jax.ShapeDtypeStruct(s.shape, s.dtype) for s in slots],
        compiler_params=_params(("arbitrary",)),
    )(own4, *partials, *slots)


def _after(small, token):
    return small + token[0:1, 0:1].astype(small.dtype)


def _silu(c):
    return c * _sigmoid_tail(c)


def _ada_fwd(c_all, w_ada, b_ada_cols):
    def body(c_ref, w_ref, b_ref, out_ref):
        out_ref[...] = jnp.dot(_silu(c_ref[...]), w_ref[...], preferred_element_type=F32,
                               precision=lax.Precision.HIGHEST) + b_ref[...]

    return pl.pallas_call(
        body, name="ada_fwd", out_shape=jax.ShapeDtypeStruct((N_DEV, w_ada.shape[1]), F32),
    )(c_all, w_ada, b_ada_cols)


def _adam(w, g, m, v):
    m = ADAM_B1 * m + (1.0 - ADAM_B1) * g
    v = ADAM_B2 * v + (1.0 - ADAM_B2) * (g * g)
    m_hat = m / (1.0 - ADAM_B1 ** ADAM_STEP)
    v_hat = v / (1.0 - ADAM_B2 ** ADAM_STEP)
    delta = -ADAM_LR * (m_hat / (jnp.sqrt(v_hat) + ADAM_EPS) + ADAM_WD * w)
    return delta, m, v


def _ada_bwd_adam(c_all, dmod_cols, w, m, v):
    def body(c_ref, d_ref, w_ref, m_ref, v_ref, g_ref, delta_ref, nm_ref, nv_ref):
        g = lax.dot_general(_silu(c_ref[...]), d_ref[...], (((0,), (0,)), ((), ())),
                            preferred_element_type=F32, precision=lax.Precision.HIGHEST)
        g_ref[...] = g
        delta_ref[...], nm_ref[...], nv_ref[...] = _adam(w_ref[...], g, m_ref[...], v_ref[...])

    sd = jax.ShapeDtypeStruct(w.shape, F32)
    return pl.pallas_call(body, name="ada_bwd_adam", out_shape=[sd] * 4,
                          compiler_params=pltpu.CompilerParams(vmem_limit_bytes=V7X_VMEM_LIMIT),
                          )(c_all, dmod_cols, w, m, v)


def _adam_group(chip_sums, arrivals, ws, ms, vs, n_tiles, name):
    n = len(ws)

    def body(*refs):
        for k in range(n):
            c_ref, a_ref, w_ref, m_ref, v_ref = (refs[j * n + k] for j in range(5))
            g_ref, delta_ref, nm_ref, nv_ref = (refs[(5 + j) * n + k] for j in range(4))
            g = c_ref[...].astype(F32)
            for j in range(N_NEAR - 1):
                g = g + a_ref[j].astype(F32)
            g_ref[...] = g
            delta_ref[...], nm_ref[...], nv_ref[...] = _adam(w_ref[...], g, m_ref[...], v_ref[...])

    tiles = [(w.shape[0] // n_tiles, w.shape[1]) for w in ws]
    blk = [pl.BlockSpec(t, lambda i: (i, 0)) for t in tiles]
    return pl.pallas_call(
        body, name=name, grid=(n_tiles,),
        in_specs=[pl.BlockSpec((None, *t), lambda i: (0, i, 0)) for t in tiles]
        + [pl.BlockSpec((N_NEAR - 1, *t), lambda i: (0, i, 0)) for t in tiles] + blk * 3,
        out_specs=blk * 4, out_shape=[jax.ShapeDtypeStruct(w.shape, F32) for w in ws] * 4,
        compiler_params=_params(("parallel",)),
    )(*chip_sums, *arrivals, *ws, *ms, *vs)


N_SMALL = 40
N_SMALL_PARAMS = 11


def _pack_vecs(conv_w_full, rows):
    def body(cw_ref, *refs):
        out = refs[-1]
        out[...] = jnp.zeros_like(out)
        out[0:4, :] = cw_ref[0:4, :]
        for r, ref in enumerate(refs[:-1]):
            out[4 + r:5 + r, :] = ref[...]

    return pl.pallas_call(body, name="pack_vecs", out_shape=jax.ShapeDtypeStruct((16, D), F32))(conv_w_full, *rows)


def _small_finish(gathered, mod_all, vecs, ws, ms, vs):
    n = N_SMALL_PARAMS

    def body(g_ref, mod_ref, vec_ref, *refs):
        w_refs, m_refs, v_refs = refs[:n], refs[n:2 * n], refs[2 * n:3 * n]
        outs = refs[3 * n:]
        g1 = vec_ref[V_G1:V_G1 + 1, :]
        g2 = vec_ref[V_G2:V_G2 + 1, :]
        zero = jnp.zeros((1, D), F32)
        dg1, dg2, dgf, loss_lanes = zero, zero, zero, zero
        mixer = jnp.zeros((16, D), F32)
        db_ada = jnp.zeros((6, D), F32)
        for b in range(N_DEV):
            gb = g_ref[b]
            mod = mod_ref[b]
            q1 = gb[33:34]
            q2 = gb[9:10]
            dmod = jnp.concatenate([gb[32:33], q1 * g1, gb[10:11], gb[8:9], q2 * g2, gb[1:2]], axis=0)
            outs[4 * n][b] = dmod
            db_ada = db_ada + dmod
            dg1 = dg1 + q1 * (1.0 + mod[M_SC1:M_SC1 + 1])
            dg2 = dg2 + q2 * (1.0 + mod[M_SC2:M_SC2 + 1])
            dgf = dgf + gb[0:1]
            loss_lanes = loss_lanes + gb[2:3]
            mixer = mixer + gb[16:32]
        d_a_param = mixer[7:8] * _sigmoid_tail(vec_ref[V_A_PARAM:V_A_PARAM + 1, :])
        grads = [dg1, dg2, mixer[4:5], mixer[5:6], mixer[6:7], d_a_param, mixer[8:9], mixer[9:10], dgf,
                 db_ada, mixer[0:4]]
        for k in range(n):
            outs[k][...] = grads[k]
            outs[n + k][...], outs[2 * n + k][...], outs[3 * n + k][...] = _adam(
                w_refs[k][...], grads[k], m_refs[k][...], v_refs[k][...])
        outs[4 * n + 1][...] = jnp.broadcast_to(jnp.sum(loss_lanes, axis=1, keepdims=True), (8, 128))

    shapes = [jax.ShapeDtypeStruct(w.shape, F32) for w in ws]
    return pl.pallas_call(
        body, name="small_finish",
        out_shape=shapes * 4 + [jax.ShapeDtypeStruct((N_DEV, 6, D), F32), jax.ShapeDtypeStruct((8, 128), F32)],
    )(gathered, mod_all, vecs, *ws, *ms, *vs)


def _pad_rows(a, rows):
    return jnp.pad(a, ((0, rows - a.shape[0]), (0, 0)))


def kernel(x, c, norm_mix_g, norm_mlp_g, w_ada, b_ada, w_in, conv_w, conv_b, w_rg_a, b_rg_a, w_rg_x, b_rg_x, a_param, w_branch_a, w_pool, b_pool, pool_scale, w_branch_b, w_out, w_up, w_down, final_g, loss_target, m_norm_mix_g, m_norm_mlp_g, m_w_ada, m_b_ada, m_w_in, m_conv_w, m_conv_b, m_w_rg_a, m_b_rg_a, m_w_rg_x, m_b_rg_x, m_a_param, m_w_branch_a, m_w_pool, m_b_pool, m_pool_scale, m_w_branch_b, m_w_out, m_w_up, m_w_down, m_final_g, v_norm_mix_g, v_norm_mlp_g, v_w_ada, v_b_ada, v_w_in, v_conv_w, v_conv_b, v_w_rg_a, v_b_rg_a, v_w_rg_x, v_b_rg_x, v_a_param, v_w_branch_a, v_w_pool, v_b_pool, v_pool_scale, v_w_branch_b, v_w_out, v_w_up, v_w_down, v_final_g):
    me = 4 * lax.axis_index("x") + 2 * lax.axis_index("y") + lax.axis_index("c")
    s = x.shape[1]
    x2d = x.reshape(s, D)
    target = loss_target.reshape(s, D)
    n_ada = w_ada.shape[2]

    sharded = dict(w_in=(w_in[0], 1), w_up=(w_up[0], 1), w_down=(w_down[0], 0), w_branch_a=(w_branch_a[0], 0),
                   w_branch_b=(w_branch_b[0], 0), w_out=(w_out[0], 0), w_rg_a=(w_rg_a[0], 1), w_rg_x=(w_rg_x[0], 1),
                   w_pool=(w_pool[0], 1))
    kind = {k: v[1] for k, v in sharded.items()}
    shard = {k: v[0].astype(BF16) for k, v in sharded.items()}

    conv_w_full, c_rows = _all_gather([_pad_rows(conv_w[0], 8), _pad_rows(c, 8)], [1, 0], "gather_c")
    c_all = c_rows.reshape(N_DEV, 8, D)[:, 0, :]
    b_ada_cols = lax.dynamic_slice(b_ada, (0, me * n_ada), (1, n_ada))
    mod_part = _ada_fwd(c_all, w_ada[0], b_ada_cols)
    mod_parts, = _all_gather([mod_part], [0], "gather_mod")

    first_names = ["w_in", "w_rg_a", "w_rg_x", "w_pool"]
    branch_names = ["w_branch_a", "w_branch_b", "w_out"]
    mlp_names = ["w_up", "w_down"]

    def gather(group, after, name):
        return _Gather([shard[k] for k in group], [kind[k] for k in group], after, name)

    g_first = gather(first_names, mod_parts, "gather_first")
    g_branch = gather(branch_names, g_first.token, "gather_branch")
    g_mlp = gather(mlp_names, g_branch.token, "gather_mlp")

    mod_all = jnp.transpose(mod_parts.reshape(N_DEV, N_DEV, n_ada), (1, 0, 2)).reshape(N_DEV, 6, D)
    mod_all = jnp.pad(mod_all, ((0, 0), (0, 2), (0, 0)))
    modr = lax.dynamic_index_in_dim(mod_all, me, 0, keepdims=False)
    vecs = _pack_vecs(conv_w_full, [conv_b, b_rg_a, b_rg_x, a_param, b_pool, pool_scale,
                                    norm_mix_g, norm_mlp_g, final_g.reshape(1, D)])
    vecs = _after(vecs, g_mlp.token)
    g_first.forward(vecs)
    wg = dict(zip(first_names, g_first.finish(g_first.token)))

    proj, h1 = _proj_fwd(x2d, modr, vecs, wg["w_in"])
    g_branch.forward(h1)
    xr, hr, za, p, pooled, *gates = _mix_fwd(proj, _after(vecs, g_branch.token), wg["w_rg_a"], wg["w_rg_x"], wg["w_pool"])
    g_mlp.forward(za)
    wg.update(zip(branch_names, g_branch.finish(g_mlp.token)))
    ba, bb, merged, o, x2, h2 = _branch_fwd(za, pooled, proj, x2d, modr, vecs,
                                            wg["w_branch_a"], wg["w_branch_b"], wg["w_out"])
    wg.update(zip(mlp_names, g_mlp.finish(h2)))
    ru, dx3, d_dn, small_f = _mlp_fwd(h2, x2, target, modr, vecs, wg["w_up"], wg["w_down"])

    near = _near(lax.axis_index("x"), lax.axis_index("y"), lax.axis_index("c"))
    own4 = jnp.stack([me, near[1][1], near[2][1], near[3][1]]).astype(jnp.int32)

    def scatter(group, partial, after, name):
        return _Scatter([partial[k] for k in group], [kind[k] for k in group], after, name)

    dup, dx2, do, small_m = _mlp_bwd(d_dn, ru, x2, dx3, o, modr, vecs, wg["w_up"], wg["w_down"])
    partial = dict(w_up=_wgrad(h2, dup, "wgrad_up"), w_down=_wgrad(ru, d_dn, "wgrad_down", square_a=True))
    s_mlp = scatter(mlp_names, partial, dx2, "scatter_mlp")

    dba, dbb, dgates, dza, dpooled = _branch_bwd(do, proj, ba, bb, wg["w_branch_a"], wg["w_branch_b"], wg["w_out"],
                                                 dep=s_mlp.token)
    s_mlp.combine_and_send(own4, dza)
    dproj, dw_rg_a, dw_rg_x, dw_pool, small_x = _mix_bwd(dza, dpooled, proj, xr, hr, p, gates, dgates,
                                                         _after(vecs, s_mlp.token),
                                                         wg["w_rg_a"], wg["w_rg_x"], wg["w_pool"])
    partial.update(w_branch_a=_wgrad(za, dba, "wgrad_branch_a"), w_branch_b=_wgrad(pooled, dbb, "wgrad_branch_b"),
                   w_out=_wgrad(merged, do, "wgrad_out"),
                   w_rg_a=dw_rg_a.astype(BF16), w_rg_x=dw_rg_x.astype(BF16), w_pool=dw_pool.astype(BF16))
    mixer_names = ["w_rg_a", "w_rg_x", "w_pool", "w_branch_a", "w_branch_b", "w_out"]
    s_mixer = scatter(mixer_names, partial, s_mlp.token, "scatter_mixer")

    partial["w_in"] = _wgrad(h1, dproj, "wgrad_in", dep=s_mixer.token)
    s_in = scatter(["w_in"], partial, s_mixer.token, "scatter_in")
    s_mixer.combine_and_send(own4, s_in.token)
    s_in.combine_and_send(own4, s_mixer.token)
    grad_x, small_p = _proj_bwd(dproj, x2d, dx2, _after(modr, s_in.token), vecs, wg["w_in"])

    locals_ = dict(w_in=(w_in, m_w_in, v_w_in), w_up=(w_up, m_w_up, v_w_up), w_down=(w_down, m_w_down, v_w_down),
                   w_branch_a=(w_branch_a, m_w_branch_a, v_w_branch_a),
                   w_branch_b=(w_branch_b, m_w_branch_b, v_w_branch_b), w_out=(w_out, m_w_out, v_w_out),
                   w_rg_a=(w_rg_a, m_w_rg_a, v_w_rg_a), w_rg_x=(w_rg_x, m_w_rg_x, v_w_rg_x),
                   w_pool=(w_pool, m_w_pool, v_w_pool))
    res = {}

    def finish(group, exchange, after, n_tiles, name):
        chip_sums, arrivals = exchange.finish(after)
        flat = lambda t: t.reshape(-1, t.shape[-1])
        shapes = [flat(locals_[k][0]).shape for k in group]
        outs = _adam_group([cs.reshape(N_NEAR, *sh) for cs, sh in zip(chip_sums, shapes)],
                           [ar.reshape(N_NEAR - 1, *sh) for ar, sh in zip(arrivals, shapes)],
                           *[[flat(locals_[k][j]) for k in group] for j in range(3)], n_tiles, name)
        for i, k in enumerate(group):
            res[k] = [outs[j * len(group) + i].reshape(locals_[k][0].shape) for j in range(4)]
        return res[group[-1]][0]

    small = jnp.concatenate([small_f, small_m, small_x, small_p], axis=0)
    g_small = _Gather([small], [0], grad_x, "gather_small")
    done = finish(mlp_names, s_mlp, g_small.token, 4, "adam_mlp")
    g_small.forward(done)
    done = finish(mixer_names, s_mixer, g_small.token, 2, "adam_mixer")
    done = finish(["w_in"], s_in, done, 4, "adam_in")
    small_all, = g_small.finish(done)
    small_all = small_all.reshape(N_DEV, N_SMALL, D)

    def embed(cw):
        return lax.dynamic_update_slice(jnp.zeros((4, D), F32), cw[0], (0, me * (D // N_DEV)))

    def smalls(ng, nl, cb, bra, brx, ap, bp, ps, fg, ba_, cw):
        return [ng, nl, cb, bra, brx, ap, bp, ps, fg.reshape(1, D), ba_.reshape(6, D), embed(cw)]

    small_names = ["norm_mix_g", "norm_mlp_g", "conv_b", "b_rg_a", "b_rg_x", "a_param", "b_pool", "pool_scale",
                   "final_g", "b_ada", "conv_w"]
    fin = _small_finish(
        small_all, mod_all, vecs,
        smalls(norm_mix_g, norm_mlp_g, conv_b, b_rg_a, b_rg_x, a_param, b_pool, pool_scale, final_g, b_ada, conv_w),
        smalls(m_norm_mix_g, m_norm_mlp_g, m_conv_b, m_b_rg_a, m_b_rg_x, m_a_param, m_b_pool, m_pool_scale,
               m_final_g, m_b_ada, m_conv_w),
        smalls(v_norm_mix_g, v_norm_mlp_g, v_conv_b, v_b_rg_a, v_b_rg_x, v_a_param, v_b_pool, v_pool_scale,
               v_final_g, v_b_ada, v_conv_w))
    dmod_all, loss_tile = fin[4 * N_SMALL_PARAMS], fin[4 * N_SMALL_PARAMS + 1]
    dmod_cols = lax.dynamic_slice(dmod_all.reshape(N_DEV, 6 * D), (0, me * n_ada), (N_DEV, n_ada))
    res["w_ada"] = [t.reshape(w_ada.shape) for t in _ada_bwd_adam(c_all, dmod_cols, w_ada[0], m_w_ada[0], v_w_ada[0])]

    def final_shape(k, t):
        if k == "final_g":
            return t.reshape(D)
        if k == "b_ada":
            return t.reshape(1, 6 * D)
        if k == "conv_w":
            return lax.dynamic_slice(t, (0, me * (D // N_DEV)), (4, D // N_DEV)).reshape(conv_w.shape)
        return t

    for i, k in enumerate(small_names):
        res[k] = [final_shape(k, fin[which * N_SMALL_PARAMS + i]) for which in range(4)]
    order = ["norm_mix_g", "norm_mlp_g", "w_ada", "b_ada", "w_in", "conv_w", "conv_b", "w_rg_a", "b_rg_a", "w_rg_x",
             "b_rg_x", "a_param", "w_branch_a", "w_pool", "b_pool", "pool_scale", "w_branch_b", "w_out", "w_up",
             "w_down", "final_g"]
    outs = [loss_tile[0, 0], grad_x.reshape(x.shape)]
    for which in range(4):
        for k in order:
            outs.append(res[k][which])
    return tuple(outs)
```

```python
import functools

import jax
import jax.numpy as jnp
from jax import lax
from jax.experimental import pallas as pl
from jax.experimental.pallas import tpu as pltpu

F32 = jnp.float32
BF16 = jnp.bfloat16
MESH = pl.DeviceIdType.MESH

N_DEV = 8
D = 1024
N_GROUPS = 4
GW = D // N_GROUPS
D_IN = 5 * D
D_FF = 4 * D
POOL_WINDOWS = (2, 4, 8, 16)
HALO_X = 8
HALO_U = 16
EPS = 1e-6
C_RG = 8.0
ADAM_LR, ADAM_B1, ADAM_B2, ADAM_EPS, ADAM_WD, ADAM_STEP = 0.001, 0.9, 0.999, 1e-08, 0.01, 10

V7X_VMEM_LIMIT = 56 * 1024 * 1024

V_CONV_W, V_CONV_B, V_B_RG_A, V_B_RG_X, V_A_PARAM, V_B_POOL, V_POOL_SCALE, V_G1, V_G2, V_GF = 0, 4, 5, 6, 7, 8, 9, 10, 11, 12
M_SH1, M_SC1, M_GT1, M_SH2, M_SC2, M_GT2 = 0, 1, 2, 3, 4, 5

TM_PROJ = 512
TM_MIX = 256
TM_BRANCH = 256
TM_MLP = 512
TM_MLP_BWD = 256
TS_WGRAD = 1024


def _params(semantics):
    return pltpu.CompilerParams(dimension_semantics=semantics, vmem_limit_bytes=V7X_VMEM_LIMIT)


def _resident(shape):
    return pl.BlockSpec(shape, lambda *_: (0,) * len(shape), pipeline_mode=pl.Buffered(1))


def _dot(a, b):
    return jnp.dot(a, b, preferred_element_type=F32)


def _dot_nt(a, b):
    return lax.dot_general(a, b, (((1,), (1,)), ((), ())), preferred_element_type=F32)


def _dot_tn(a, b):
    return lax.dot_general(a, b, (((0,), (0,)), ((), ())), preferred_element_type=F32)


def _sigmoid(x):
    return 0.5 * jnp.tanh(0.5 * x) + 0.5


def _sigmoid_tail(x):
    return 1.0 / (1.0 + jnp.exp(-x))


def _gelu_and_grad(x):
    k = 0.7978845608028654
    x2 = x * x
    t = jnp.tanh(k * (x + 0.044715 * x * x2))
    g = 0.5 * x * (1.0 + t)
    dg = 0.5 * (1.0 + t) + 0.5 * x * (1.0 - t * t) * (k * (1.0 + 3.0 * 0.044715 * x2))
    return g, dg


def _softplus(a):
    e = jnp.exp(-jnp.abs(a))
    u = 1.0 + e
    log1p_e = jnp.where(u == 1.0, e, jnp.log(u) * e / jnp.where(u == 1.0, 1.0, u - 1.0))
    return jnp.maximum(a, 0.0) + log1p_e


def _neg_expm1(z):
    series = -(z * (1.0 + z * (0.5 + z * (1.0 / 6.0 + z * (1.0 / 24.0 + z * (1.0 / 120.0))))))
    return jnp.where(z > -0.1, series, 1.0 - jnp.exp(z))


def _shift_down(x, k):
    return pltpu.roll(x, k, 0)


def _shift_up(x, k):
    return pltpu.roll(x, x.shape[0] - k, 0)


def _rglru_gates(xr, w_a, w_x, b_a, b_x, a_param, is_t0):
    xb = xr.astype(BF16)
    ra = _sigmoid(_dot(xb, w_a) + b_a)
    ri = _sigmoid(_dot(xb, w_x) + b_x)
    sp = _softplus(a_param)
    log_a = (-C_RG) * ra * sp
    a = jnp.exp(log_a)
    mult = jnp.where(is_t0, 1.0, jnp.sqrt(_neg_expm1(2.0 * log_a)))
    return ra, ri, sp, a, mult


SUBLANES = 8


LANES = 128


def _scan_strip(a, b, carry, scr, down):
    t = b.shape[0]
    g = t // SUBLANES
    a3 = a.reshape(g, SUBLANES, LANES)
    b3 = b.reshape(g, SUBLANES, LANES)
    sub = lax.broadcasted_iota(jnp.int32, (g, SUBLANES, LANES), 1)
    for k in (1, 2, 4):
        keep = sub >= k if down else sub < SUBLANES - k
        shift = k if down else SUBLANES - k
        b3 = b3 + a3 * jnp.where(keep, pltpu.roll(b3, shift, 1), 0.0)
        a3 = a3 * jnp.where(keep, pltpu.roll(a3, shift, 1), 1.0)
    scr[0] = a3.reshape(t, LANES)
    scr[1] = b3.reshape(t, LANES)
    end_row = SUBLANES - 1 if down else 0
    ag = scr[0, pl.ds(end_row, g, stride=SUBLANES), :]
    bg = scr[1, pl.ds(end_row, g, stride=SUBLANES), :]
    rg = lax.broadcasted_iota(jnp.int32, (g, LANES), 0)
    edge = 0 if down else g - 1
    bg = bg + jnp.where(rg == edge, ag * carry, 0.0)
    k = 1
    while k < g:
        keep = rg >= k if down else rg < g - k
        shift = k if down else g - k
        bg = bg + ag * jnp.where(keep, pltpu.roll(bg, shift, 0), 0.0)
        if 2 * k < g:
            ag = ag * pltpu.roll(ag, shift, 0)
        k *= 2
    entering = jnp.where(rg != edge, pltpu.roll(bg, 1 if down else g - 1, 0), carry)
    for r in range(SUBLANES):
        scr[2, pl.ds(r, g, stride=SUBLANES), :] = entering
    return scr[1] + scr[0] * scr[2], bg[g - 1:g, :]


def _scan_strips(a, b, carry, scr, down):
    outs = [_scan_strip(a[:, c:c + LANES], b[:, c:c + LANES], carry[:, c:c + LANES], scr, down)
            for c in range(0, b.shape[1], LANES)]
    return jnp.concatenate([o[0] for o in outs], axis=1), jnp.concatenate([o[1] for o in outs], axis=1)


def _scan_down(a, b, carry, scr):
    return _scan_strips(a, b, carry, scr, True)


def _scan_up(m, b, carry, scr):
    return _scan_strips(m, b, carry, scr, False)[0]


def _conv_taps(x_ext):
    return [_shift_down(x_ext, 3 - j)[HALO_X:] if j < 3 else x_ext[HALO_X:] for j in range(4)]


def _proj_mix_fwd(x, modr, vecs, w_in, w_rg_a, w_rg_x, w_pool):
    s = x.shape[0]
    tm = min(TM_MIX, s)

    def body(x_ref, mod_ref, vec_ref, w_ref, wa_ref, wx_ref, wp_ref,
             h1_ref, xrnn_ref, ga_ref, dga_ref, sa_ref, sb_ref, xr_ref, a_ref, mult_ref, b_ref,
             ra_ref, ri_ref, p_ref, pooled_ref, xh_scr, uh_scr):
        i = pl.program_id(0)
        first = i == 0
        xv = x_ref[...]
        r = lax.rsqrt(jnp.mean(xv * xv, axis=-1, keepdims=True) + EPS)
        gain = vec_ref[V_G1:V_G1 + 1, :] * (1.0 + mod_ref[M_SC1:M_SC1 + 1, :])
        h = (xv * r * gain + mod_ref[M_SH1:M_SH1 + 1, :]).astype(BF16)
        h1_ref[...] = h
        x_rnn = _dot(h, w_ref[:, 0:D])
        xrnn_ref[...] = x_rnn
        u = _dot(h, w_ref[:, 2 * D:3 * D])
        ga_ref[...], dga_ref[...] = _gelu_and_grad(_dot(h, w_ref[:, D:2 * D]))
        sa_ref[...] = _sigmoid(_dot(h, w_ref[:, 3 * D:4 * D]))
        sb_ref[...] = _sigmoid(_dot(h, w_ref[:, 4 * D:5 * D]))

        row = lax.broadcasted_iota(jnp.int32, (tm, GW), 0)
        is_t0 = jnp.logical_and(first, row == 0)
        t_glob = (row + i * tm + 1).astype(F32)
        for g in range(N_GROUPS):
            cs = slice(g * GW, (g + 1) * GW)
            vec = vec_ref[:, cs]
            xh = jnp.where(first, 0.0, xh_scr[:, cs])
            taps = _conv_taps(jnp.concatenate([xh, x_rnn[:, cs]], axis=0))
            xr = vec[V_CONV_B:V_CONV_B + 1]
            for j in range(4):
                xr = xr + vec[V_CONV_W + j:V_CONV_W + j + 1] * taps[j]
            xr_ref[:, cs] = xr
            ra, ri, _, a, mult = _rglru_gates(
                xr, wa_ref[g], wx_ref[g], vec[V_B_RG_A:V_B_RG_A + 1], vec[V_B_RG_X:V_B_RG_X + 1],
                vec[V_A_PARAM:V_A_PARAM + 1], is_t0)
            a_ref[:, cs] = a
            mult_ref[:, cs] = mult
            ra_ref[:, cs] = ra.astype(BF16)
            ri_ref[:, cs] = ri.astype(BF16)
            b_ref[:, cs] = xr * ri * mult
            uh = jnp.where(first, 0.0, uh_scr[:, cs])
            ug = u[:, cs]
            sm = jnp.concatenate([uh, ug], axis=0)
            k = 1
            while k < POOL_WINDOWS[g]:
                sm = sm + _shift_down(sm, k)
                k *= 2
            cnt = jnp.minimum(t_glob, float(POOL_WINDOWS[g]))
            p = (sm[HALO_U:] / cnt - ug).astype(BF16)
            p_ref[:, cs] = p
            pb = _dot(p, wp_ref[g]) + vec[V_B_POOL:V_B_POOL + 1]
            pooled_ref[:, cs] = (pb * vec[V_POOL_SCALE:V_POOL_SCALE + 1]).astype(BF16)
        xh_scr[...] = x_rnn[tm - HALO_X:, :]
        uh_scr[...] = u[tm - HALO_U:, :]

    tok = pl.BlockSpec((tm, D), lambda i: (i, 0))
    wspec = pl.BlockSpec((N_GROUPS, GW, GW), lambda i: (0, 0, 0))
    sd = lambda dt: jax.ShapeDtypeStruct((s, D), dt)
    return pl.pallas_call(
        body, name="proj_mix_fwd", grid=(s // tm,),
        in_specs=[tok, pl.BlockSpec((8, D), lambda i: (0, 0)), pl.BlockSpec((16, D), lambda i: (0, 0)),
                  _resident((D, D_IN)), wspec, wspec, wspec],
        out_specs=[tok] * 14,
        out_shape=[sd(BF16)] + [sd(F32)] * 9 + [sd(BF16)] * 4,
        scratch_shapes=[pltpu.VMEM((HALO_X, D), F32), pltpu.VMEM((HALO_U, D), F32)],
        compiler_params=_params(("arbitrary",)),
    )(x, modr, vecs, w_in, w_rg_a, w_rg_x, w_pool)


def _scan_fwd(a, b, ga):
    s = a.shape[0]
    tm = min(TM_MIX, s)

    def body(a_ref, b_ref, ga_ref, hr_ref, za_ref, carry_ref, scan_scr):
        @pl.when(pl.program_id(0) == 0)
        def _():
            carry_ref[...] = jnp.zeros_like(carry_ref)

        for g in range(N_GROUPS):
            cs = slice(g * GW, (g + 1) * GW)
            h, last = _scan_down(a_ref[:, cs], b_ref[:, cs], carry_ref[0:1, cs], scan_scr)
            hr_ref[:, cs] = h
            carry_ref[0:1, cs] = last
            za_ref[:, cs] = (ga_ref[:, cs] * h).astype(BF16)

    tok = pl.BlockSpec((tm, D), lambda i: (i, 0))
    return pl.pallas_call(
        body, name="scan_fwd", grid=(s // tm,),
        in_specs=[tok, tok, tok], out_specs=[tok, tok],
        out_shape=[jax.ShapeDtypeStruct((s, D), F32), jax.ShapeDtypeStruct((s, D), BF16)],
        scratch_shapes=[pltpu.VMEM((8, D), F32), pltpu.VMEM((3, tm, LANES), F32)],
        compiler_params=_params(("arbitrary",)),
    )(a, b, ga)


def _branch_fwd(za, pooled, sa, sb, x, modr, vecs, w_a, w_b, w_out):
    s = x.shape[0]
    tm = min(TM_BRANCH, s)

    def body(za_ref, pooled_ref, sa_ref, sb_ref, x_ref, mod_ref, vec_ref, wa_ref, wb_ref, wo_ref,
             ba_ref, bb_ref, merged_ref, o_ref, x2_ref, h2_ref):
        ba = _dot(za_ref[...], wa_ref[...])
        bb = _dot(pooled_ref[...], wb_ref[...])
        ba_ref[...] = ba.astype(BF16)
        bb_ref[...] = bb.astype(BF16)
        merged = (sa_ref[...] * ba + sb_ref[...] * bb).astype(BF16)
        merged_ref[...] = merged
        o = _dot(merged, wo_ref[...])
        o_ref[...] = o.astype(BF16)
        x2 = x_ref[...] + mod_ref[M_GT1:M_GT1 + 1, :] * o
        x2_ref[...] = x2
        r = lax.rsqrt(jnp.mean(x2 * x2, axis=-1, keepdims=True) + EPS)
        gain = vec_ref[V_G2:V_G2 + 1, :] * (1.0 + mod_ref[M_SC2:M_SC2 + 1, :])
        h2_ref[...] = (x2 * r * gain + mod_ref[M_SH2:M_SH2 + 1, :]).astype(BF16)

    tok = pl.BlockSpec((tm, D), lambda i: (i, 0))
    wspec = pl.BlockSpec((D, D), lambda i: (0, 0))
    sd = lambda dt: jax.ShapeDtypeStruct((s, D), dt)
    return pl.pallas_call(
        body, name="branch_fwd", grid=(s // tm,),
        in_specs=[tok, tok, tok, tok,
                  tok, pl.BlockSpec((8, D), lambda i: (0, 0)), pl.BlockSpec((16, D), lambda i: (0, 0)),
                  wspec, wspec, wspec],
        out_specs=[tok] * 6,
        out_shape=[sd(BF16), sd(BF16), sd(BF16), sd(BF16), sd(F32), sd(BF16)],
        compiler_params=_params(("parallel",)),
    )(za, pooled, sa, sb, x, modr, vecs, w_a, w_b, w_out)


def _mlp_fwd(h2, x2, target, modr, vecs, w_up, w_down):
    s = x2.shape[0]
    tm = min(TM_MLP, s)

    def body(h2_ref, x2_ref, tgt_ref, mod_ref, vec_ref, wu_ref, wd_ref,
             ru_ref, dx3_ref, ddn_ref, small_ref):
        @pl.when(pl.program_id(0) == 0)
        def _():
            small_ref[...] = jnp.zeros_like(small_ref)

        h2 = h2_ref[...]
        dn = None
        for c in range(D_FF // D):
            cs = slice(c * D, (c + 1) * D)
            ru = jnp.maximum(_dot(h2, wu_ref[:, cs]), 0.0)
            ru_ref[:, cs] = ru.astype(BF16)
            part = _dot((ru * ru).astype(BF16), wd_ref[cs, :])
            dn = part if dn is None else dn + part
        gt2 = mod_ref[M_GT2:M_GT2 + 1, :]
        gf = vec_ref[V_GF:V_GF + 1, :]
        x3 = x2_ref[...] + gt2 * dn
        r3 = lax.rsqrt(jnp.mean(x3 * x3, axis=-1, keepdims=True) + EPS)
        n3 = x3 * r3
        err = n3 * gf - tgt_ref[...]
        dy = err * (1.0 / D)
        dn3 = dy * gf
        dx3 = r3 * (dn3 - n3 * jnp.mean(dn3 * n3, axis=-1, keepdims=True))
        dx3_ref[...] = dx3
        ddn_ref[...] = (dx3 * gt2).astype(BF16)
        small_ref[0:1, :] += jnp.sum(dy * n3, axis=0, keepdims=True)
        small_ref[1:2, :] += jnp.sum(dx3 * dn, axis=0, keepdims=True)
        small_ref[2:3, :] += (0.5 / D) * jnp.sum(err * err, axis=0, keepdims=True)

    tok = pl.BlockSpec((tm, D), lambda i: (i, 0))
    return pl.pallas_call(
        body, name="mlp_fwd", grid=(s // tm,),
        in_specs=[tok, tok, tok,
                  pl.BlockSpec((8, D), lambda i: (0, 0)), pl.BlockSpec((16, D), lambda i: (0, 0)),
                  _resident((D, D_FF)), _resident((D_FF, D))],
        out_specs=[pl.BlockSpec((tm, D_FF), lambda i: (i, 0)), tok, tok,
                   pl.BlockSpec((8, D), lambda i: (0, 0))],
        out_shape=[jax.ShapeDtypeStruct((s, D_FF), BF16), jax.ShapeDtypeStruct((s, D), F32),
                   jax.ShapeDtypeStruct((s, D), BF16), jax.ShapeDtypeStruct((8, D), F32)],
        compiler_params=_params(("arbitrary",)),
    )(h2, x2, target, modr, vecs, w_up, w_down)


def _mlp_bwd(d_dn, ru, x2, dx3, o, modr, vecs, w_up, w_down):
    s = x2.shape[0]
    tm = min(TM_MLP_BWD, s)

    def body(ddn_ref, ru_ref, x2_ref, dx3_ref, o_ref, mod_ref, vec_ref, wu_ref, wd_ref,
             dup_ref, dx2_ref, do_ref, small_ref):
        @pl.when(pl.program_id(0) == 0)
        def _():
            small_ref[...] = jnp.zeros_like(small_ref)

        ddn = ddn_ref[...]
        dh2 = None
        for c in range(D_FF // D):
            cs = slice(c * D, (c + 1) * D)
            dff = _dot_nt(ddn, wd_ref[cs, :])
            dup = (dff * (2.0 * ru_ref[:, cs].astype(F32))).astype(BF16)
            dup_ref[:, cs] = dup
            part = _dot_nt(dup, wu_ref[:, cs])
            dh2 = part if dh2 is None else dh2 + part
        x2 = x2_ref[...]
        r2 = lax.rsqrt(jnp.mean(x2 * x2, axis=-1, keepdims=True) + EPS)
        xn2 = x2 * r2
        gain = vec_ref[V_G2:V_G2 + 1, :] * (1.0 + mod_ref[M_SC2:M_SC2 + 1, :])
        dxn2 = dh2 * gain
        dx2 = dx3_ref[...] + r2 * (dxn2 - xn2 * jnp.mean(dxn2 * xn2, axis=-1, keepdims=True))
        dx2_ref[...] = dx2
        do_ref[...] = (dx2 * mod_ref[M_GT1:M_GT1 + 1, :]).astype(BF16)
        small_ref[0:1, :] += jnp.sum(dh2, axis=0, keepdims=True)
        small_ref[1:2, :] += jnp.sum(dh2 * xn2, axis=0, keepdims=True)
        small_ref[2:3, :] += jnp.sum(dx2 * o_ref[...].astype(F32), axis=0, keepdims=True)

    tok = pl.BlockSpec((tm, D), lambda i: (i, 0))
    wide = pl.BlockSpec((tm, D_FF), lambda i: (i, 0))
    return pl.pallas_call(
        body, name="mlp_bwd", grid=(s // tm,),
        in_specs=[tok, wide, tok, tok, tok,
                  pl.BlockSpec((8, D), lambda i: (0, 0)), pl.BlockSpec((16, D), lambda i: (0, 0)),
                  _resident((D, D_FF)), _resident((D_FF, D))],
        out_specs=[wide, tok, tok, pl.BlockSpec((8, D), lambda i: (0, 0))],
        out_shape=[jax.ShapeDtypeStruct((s, D_FF), BF16), jax.ShapeDtypeStruct((s, D), F32),
                   jax.ShapeDtypeStruct((s, D), BF16), jax.ShapeDtypeStruct((8, D), F32)],
        compiler_params=_params(("arbitrary",)),
    )(d_dn, ru, x2, dx3, o, modr, vecs, w_up, w_down)


def _branch_bwd(do, sa, sb, ba, bb, w_a, w_b, w_out, dep):
    s = do.shape[0]
    tm = min(TM_BRANCH, s)

    def body(do_ref, sa_ref, sb_ref, ba_ref, bb_ref, wa_ref, wb_ref, wo_ref, dep_ref,
             dba_ref, dbb_ref, dg_ref, dza_ref, dpooled_ref):
        dmerged = _dot_nt(do_ref[...], wo_ref[...])
        sa = sa_ref[...]
        sb = sb_ref[...]
        dba = (dmerged * sa).astype(BF16)
        dbb = (dmerged * sb).astype(BF16)
        dba_ref[...] = dba
        dbb_ref[...] = dbb
        dg_ref[:, :D] = (dmerged * ba_ref[...].astype(F32) * sa * (1.0 - sa)).astype(BF16)
        dg_ref[:, D:] = (dmerged * bb_ref[...].astype(F32) * sb * (1.0 - sb)).astype(BF16)
        dza_ref[...] = _dot_nt(dba, wa_ref[...])
        dpooled_ref[...] = _dot_nt(dbb, wb_ref[...])

    tok = pl.BlockSpec((tm, D), lambda i: (i, 0))
    wspec = pl.BlockSpec((D, D), lambda i: (0, 0))
    sd = lambda dt: jax.ShapeDtypeStruct((s, D), dt)
    return pl.pallas_call(
        body, name="branch_bwd", grid=(s // tm,),
        in_specs=[tok, tok, tok, tok, tok, wspec, wspec, wspec, pl.BlockSpec(memory_space=pl.ANY)],
        out_specs=[tok, tok, pl.BlockSpec((tm, 2 * D), lambda i: (i, 0)), tok, tok],
        out_shape=[sd(BF16), sd(BF16), jax.ShapeDtypeStruct((s, 2 * D), BF16), sd(F32), sd(F32)],
        compiler_params=_params(("parallel",)),
    )(do, sa, sb, ba, bb, w_a, w_b, w_out, dep)


def _mix_bwd(dza, dpooled, x_rnn, ga, dga, xr, hr, p, gates, dgates, vecs, w_rg_a, w_rg_x, w_pool):
    s = xr.shape[0]
    tm = min(TM_MIX, s)
    nb = s // tm

    def body(dza_ref, dpooled_ref, xh_ref, x_ref, ga_ref, dga_ref, xr_ref, hh_ref, hr_ref, p_ref,
             a_ref, mult_ref, ra_ref, ri_ref, dg_ref, vec_ref, wa_ref, wx_ref, wp_ref,
             dproj_ref, dwa_ref, dwx_ref, dwp_ref, small_ref,
             scan_carry, dxr_carry, q_carry, scan_scr):
        i = pl.program_id(0)
        bi = nb - 1 - i
        first_t = bi == 0

        @pl.when(i == 0)
        def _():
            scan_carry[...] = jnp.zeros_like(scan_carry)
            dxr_carry[...] = jnp.zeros_like(dxr_carry)
            q_carry[...] = jnp.zeros_like(q_carry)
            dwa_ref[...] = jnp.zeros_like(dwa_ref)
            dwx_ref[...] = jnp.zeros_like(dwx_ref)
            dwp_ref[...] = jnp.zeros_like(dwp_ref)
            small_ref[...] = jnp.zeros_like(small_ref)

        row = lax.broadcasted_iota(jnp.int32, (tm, GW), 0)
        is_t0 = jnp.logical_and(first_t, row == 0)
        t_glob = (row + bi * tm + 1).astype(F32)
        colsum = lambda v: jnp.sum(v, axis=0, keepdims=True)
        for g in range(N_GROUPS):
            cs = slice(g * GW, (g + 1) * GW)
            vec = vec_ref[:, cs]
            xr = xr_ref[:, cs]
            hr = hr_ref[:, cs]
            dza = dza_ref[:, cs]
            dproj_ref[:, D + g * GW:D + (g + 1) * GW] = (dza * hr * dga_ref[:, cs]).astype(BF16)
            dhr = dza * ga_ref[:, cs]
            a = a_ref[:, cs]
            mult = mult_ref[:, cs]
            ra = ra_ref[:, cs].astype(F32)
            ri = ri_ref[:, cs].astype(F32)
            sp = _softplus(vec[V_A_PARAM:V_A_PARAM + 1])
            m = jnp.where(row == tm - 1, 1.0, _shift_up(a, 1))
            gsum = _scan_up(m, dhr, scan_carry[0:1, cs], scan_scr)
            scan_carry[0:1, cs] = a[0:1, :] * gsum[0:1, :]
            hh = jnp.where(first_t, 0.0, hh_ref[:, cs])
            hprev = _shift_down(jnp.concatenate([hh, hr], axis=0), 1)[8:]
            da = gsum * hprev
            dmult = jnp.where(is_t0, 0.0, gsum * xr * ri)
            dlog_a = da * a - dmult * a * a / mult
            dri = gsum * xr * mult
            dxr = gsum * ri * mult
            small_ref[7:8, cs] += colsum((-C_RG) * ra * dlog_a)
            dpa = (((-C_RG) * sp) * dlog_a * ra * (1.0 - ra))
            dpx = dri * ri * (1.0 - ri)
            small_ref[5:6, cs] += colsum(dpa)
            small_ref[6:7, cs] += colsum(dpx)
            dpa = dpa.astype(BF16)
            dpx = dpx.astype(BF16)
            xrb = xr.astype(BF16)
            dwa_ref[g] += _dot_tn(xrb, dpa)
            dwx_ref[g] += _dot_tn(xrb, dpx)
            dxr = dxr + _dot_nt(dpa, wa_ref[g]) + _dot_nt(dpx, wx_ref[g])
            small_ref[4:5, cs] += colsum(dxr)
            xh = jnp.where(first_t, 0.0, xh_ref[:, cs])
            taps = _conv_taps(jnp.concatenate([xh, x_ref[:, cs]], axis=0))
            dxr_ext = jnp.concatenate([dxr, dxr_carry[:, cs]], axis=0)
            dx = vec[V_CONV_W + 3:V_CONV_W + 4] * dxr
            for j in range(4):
                small_ref[j:j + 1, cs] += colsum(dxr * taps[j])
                if j < 3:
                    dx = dx + vec[V_CONV_W + j:V_CONV_W + j + 1] * _shift_up(dxr_ext, 3 - j)[:tm]
            dxr_carry[:, cs] = dxr[0:8, :]
            dproj_ref[:, cs] = dx.astype(BF16)
            pg = p_ref[:, cs]
            dpooled = dpooled_ref[:, cs]
            pb = _dot(pg, wp_ref[g]) + vec[V_B_POOL:V_B_POOL + 1]
            small_ref[9:10, cs] += colsum(dpooled * pb)
            dpb = dpooled * vec[V_POOL_SCALE:V_POOL_SCALE + 1]
            small_ref[8:9, cs] += colsum(dpb)
            dpbb = dpb.astype(BF16)
            dwp_ref[g] += _dot_tn(pg, dpbb)
            dp = _dot_nt(dpbb, wp_ref[g])
            q = dp / jnp.minimum(t_glob, float(POOL_WINDOWS[g]))
            sm = jnp.concatenate([q, q_carry[:, cs]], axis=0)
            k = 1
            while k < POOL_WINDOWS[g]:
                sm = sm + _shift_up(sm, k)
                k *= 2
            q_carry[:, cs] = q[0:HALO_U, :]
            dproj_ref[:, 2 * D + g * GW:2 * D + (g + 1) * GW] = (sm[:tm] - dp).astype(BF16)
        dproj_ref[:, 3 * D:] = dg_ref[...]

    rev = lambda i: nb - 1 - i
    tok = pl.BlockSpec((tm, D), lambda i: (rev(i), 0))
    halo8 = lambda k: pl.BlockSpec((8, D), lambda i: (jnp.maximum(rev(i) * (tm // 8) - 1, 0), k))
    wspec = pl.BlockSpec((N_GROUPS, GW, GW), lambda i: (0, 0, 0))
    wshape = jax.ShapeDtypeStruct((N_GROUPS, GW, GW), F32)
    return pl.pallas_call(
        body, name="mix_bwd", grid=(nb,),
        in_specs=[tok, tok, halo8(0), tok, tok, tok, tok, halo8(0), tok, tok, tok, tok, tok, tok,
                  pl.BlockSpec((tm, 2 * D), lambda i: (rev(i), 0)),
                  pl.BlockSpec((16, D), lambda i: (0, 0)), wspec, wspec, wspec],
        out_specs=[pl.BlockSpec((tm, D_IN), lambda i: (rev(i), 0)), wspec, wspec, wspec,
                   pl.BlockSpec((16, D), lambda i: (0, 0))],
        out_shape=[jax.ShapeDtypeStruct((s, D_IN), BF16), wshape, wshape, wshape,
                   jax.ShapeDtypeStruct((16, D), F32)],
        scratch_shapes=[pltpu.VMEM((8, D), F32), pltpu.VMEM((8, D), F32), pltpu.VMEM((HALO_U, D), F32),
                        pltpu.VMEM((3, tm, LANES), F32)],
        compiler_params=_params(("arbitrary",)),
    )(dza, dpooled, x_rnn, x_rnn, ga, dga, xr, hr, hr, p, *gates, dgates, vecs, w_rg_a, w_rg_x, w_pool)


def _proj_bwd(dproj, x, dx2, modr, vecs, w_in):
    s = x.shape[0]
    tm = min(TM_PROJ, s)

    def body(dp_ref, x_ref, dx2_ref, mod_ref, vec_ref, w_ref, gx_ref, small_ref):
        @pl.when(pl.program_id(0) == 0)
        def _():
            small_ref[...] = jnp.zeros_like(small_ref)

        dh1 = None
        for c in range(D_IN // D):
            cs = slice(c * D, (c + 1) * D)
            part = _dot_nt(dp_ref[:, cs], w_ref[:, cs])
            dh1 = part if dh1 is None else dh1 + part
        xv = x_ref[...]
        r1 = lax.rsqrt(jnp.mean(xv * xv, axis=-1, keepdims=True) + EPS)
        xn1 = xv * r1
        gain = vec_ref[V_G1:V_G1 + 1, :] * (1.0 + mod_ref[M_SC1:M_SC1 + 1, :])
        dxn1 = dh1 * gain
        gx_ref[...] = dx2_ref[...] + r1 * (dxn1 - xn1 * jnp.mean(dxn1 * xn1, axis=-1, keepdims=True))
        small_ref[0:1, :] += jnp.sum(dh1, axis=0, keepdims=True)
        small_ref[1:2, :] += jnp.sum(dh1 * xn1, axis=0, keepdims=True)

    tok = pl.BlockSpec((tm, D), lambda i: (i, 0))
    return pl.pallas_call(
        body, name="proj_bwd", grid=(s // tm,),
        in_specs=[pl.BlockSpec((tm, D_IN), lambda i: (i, 0)), tok, tok,
                  pl.BlockSpec((8, D), lambda i: (0, 0)), pl.BlockSpec((16, D), lambda i: (0, 0)),
                  _resident((D, D_IN))],
        out_specs=[tok, pl.BlockSpec((8, D), lambda i: (0, 0))],
        out_shape=[jax.ShapeDtypeStruct((s, D), F32), jax.ShapeDtypeStruct((8, D), F32)],
        compiler_params=_params(("arbitrary",)),
    )(dproj, x, dx2, modr, vecs, w_in)


def _wgrad(a, b, name, square_a=False, dep=None):
    s, ka = a.shape
    n = b.shape[1]
    tka = ka if ka <= 1024 else ka // 2
    tn = n if n <= 1024 else n // 2
    ts = min(TS_WGRAD, s)
    ns = s // ts
    nc = 512
    deps = [] if dep is None else [dep]

    def body(a_ref, b_ref, *refs):
        out_ref, acc_ref = refs[-2:]
        t = pl.program_id(2)

        @pl.when(t == 0)
        def _():
            acc_ref[...] = jnp.zeros_like(acc_ref)

        av = a_ref[...]
        if square_a:
            af = av.astype(F32)
            av = (af * af).astype(BF16)
        for c in range(tn // nc):
            cs = slice(c * nc, (c + 1) * nc)
            acc_ref[:, cs] += _dot_tn(av, b_ref[:, cs])

        @pl.when(t == ns - 1)
        def _():
            out_ref[...] = acc_ref[...].astype(BF16)

    return pl.pallas_call(
        body, name=name, grid=(ka // tka, n // tn, ns),
        in_specs=[pl.BlockSpec((ts, tka), lambda i, j, t: (t, i)),
                  pl.BlockSpec((ts, tn), lambda i, j, t: (t, j))] + [pl.BlockSpec(memory_space=pl.ANY)] * len(deps),
        out_specs=pl.BlockSpec((tka, tn), lambda i, j, t: (i, j)),
        out_shape=jax.ShapeDtypeStruct((ka, n), BF16),
        scratch_shapes=[pltpu.VMEM((tka, tn), F32)],
        compiler_params=_params(("parallel", "parallel", "arbitrary")),
    )(a, b, *deps)


def _window(ref, kind, idx, size):
    start = pl.multiple_of(idx * size, size)
    if kind == 0:
        return ref.at[pl.ds(start, size)]
    if kind == 1:
        return ref.at[:, pl.ds(start, size)]
    return ref.at[:, :, pl.ds(start, size)]


def _mesh_place():
    x, y, c = lax.axis_index("x"), lax.axis_index("y"), lax.axis_index("c")
    return x, y, c, 4 * x + 2 * y + c


def _peer(x, y, c, q):
    px = 1 - x if q & 4 else x
    py = 1 - y if q & 2 else y
    pc = 1 - c if q & 1 else c
    return (px, py, pc), 4 * px + 2 * py + pc


def _all_gather(shards, kinds, name, dep=None):
    n = len(shards)
    deps = [] if dep is None else [dep]
    full_shapes = []
    for sh, kind in zip(shards, kinds):
        dims = list(sh.shape)
        dims[kind] *= N_DEV
        full_shapes.append(jax.ShapeDtypeStruct(tuple(dims), sh.dtype))

    def body(*refs):
        ins, outs = refs[:n], refs[n + len(deps):2 * n + len(deps)]
        send_sems, recv_sems, local_sems = refs[2 * n + len(deps):]
        x, y, c, me = _mesh_place()
        sends, recvs, locals_ = [], [], []
        for k in range(n):
            size = shards[k].shape[kinds[k]]
            mine = _window(outs[k], kinds[k], me, size)
            lc = pltpu.make_async_copy(ins[k], mine, local_sems.at[k])
            lc.start()
            locals_.append(lc)
            for q in range(1, N_DEV):
                peer, peer_idx = _peer(x, y, c, q)
                cp = pltpu.make_async_remote_copy(
                    src_ref=ins[k], dst_ref=mine, send_sem=send_sems.at[k, q], recv_sem=recv_sems.at[k, q],
                    device_id=peer, device_id_type=MESH)
                cp.start()
                sends.append(cp)
                recvs.append(pltpu.make_async_remote_copy(
                    src_ref=ins[k], dst_ref=_window(outs[k], kinds[k], peer_idx, size),
                    send_sem=send_sems.at[k, q], recv_sem=recv_sems.at[k, q],
                    device_id=peer, device_id_type=MESH))
        for cp in recvs:
            cp.wait_recv()
        for cp in sends:
            cp.wait_send()
        for lc in locals_:
            lc.wait()

    any_spec = pl.BlockSpec(memory_space=pl.ANY)
    return pl.pallas_call(
        body, name=name,
        in_specs=[any_spec] * (n + len(deps)), out_specs=[any_spec] * n, out_shape=full_shapes,
        scratch_shapes=[pltpu.SemaphoreType.DMA((n, N_DEV)), pltpu.SemaphoreType.DMA((n, N_DEV)),
                        pltpu.SemaphoreType.DMA((n,))],
    )(*shards, *deps)


_HBM = pl.BlockSpec(memory_space=pltpu.HBM)
_SEM = pl.BlockSpec(memory_space=pltpu.SEMAPHORE)
_EFFECT = pltpu.SideEffectType.DATAFLOW_SIDE_EFFECTING


N_NEAR = 4


def _near(x, y, c):
    out = [((x, y, 1 - c), 4 * x + 2 * y + 1 - c)]
    for j in (1, 2, 3):
        px = 1 - x if j & 2 else x
        py = 1 - y if j & 1 else y
        out.append(((px, py, c), 4 * px + 2 * py + c))
    return out


def _remote(src, dst, send_sems, recv_sems, slot, device):
    return pltpu.make_async_remote_copy(src_ref=src, dst_ref=dst, send_sem=send_sems.at[slot], recv_sem=recv_sems.at[slot],
                                        device_id=device, device_id_type=MESH)


def _split_call(name, arrays, sems_in, n_new_sems, after, emit):
    na, ns, nn = len(arrays), len(sems_in), len(n_new_sems)

    def body(*refs):
        emit(refs[:na], refs[na:na + ns], refs[na + ns + 1:na + ns + 1 + nn])
        refs[-1][...] = jnp.zeros_like(refs[-1])

    outs = pl.pallas_call(
        body, name=name,
        out_shape=(*[pltpu.SemaphoreType.DMA((m,)) for m in n_new_sems],
                   *[pltpu.HBM(a.shape, a.dtype) for a in arrays], jax.ShapeDtypeStruct((8, 128), F32)),
        in_specs=[_HBM] * na + [_SEM] * ns + [pl.BlockSpec(memory_space=pl.ANY)],
        out_specs=(*[_SEM] * nn, *[_HBM] * na, pl.BlockSpec(memory_space=pltpu.VMEM)),
        input_output_aliases={i: nn + i for i in range(na)},
        compiler_params=pltpu.CompilerParams(has_side_effects=_EFFECT),
    )(*[pltpu.with_memory_space_constraint(a, pltpu.HBM) for a in arrays], *sems_in, after)
    return list(outs[:nn]), list(outs[nn:nn + na]), outs[-1]


class _Gather:
    def __init__(self, shards, kinds, after, name):
        self.n, self.kinds, self.name = len(shards), kinds, name
        self.sizes = [s.shape[k] for s, k in zip(shards, kinds)]
        n = self.n
        lands = []
        for s, k in zip(shards, kinds):
            dims = list(s.shape)
            dims[k] *= N_DEV
            lands.append(lax.empty(tuple(dims), s.dtype))

        def emit(arr, _, new):
            x, y, c, me = _mesh_place()
            for k in range(n):
                pltpu.make_async_copy(arr[k], _window(arr[n + k], kinds[k], me, self.sizes[k]), new[2].at[k]).start()
            for k in range(n):
                mine = _window(arr[n + k], kinds[k], me, self.sizes[k])
                for j, (dev, _) in enumerate(_near(x, y, c)):
                    _remote(arr[k], mine, new[0], new[1], k * N_NEAR + j, dev).start()

        self.sems, self.arrays, self.token = _split_call(name + "_start", [*shards, *lands], [],
                                                         [n * N_NEAR, n * N_NEAR, n], after, emit)

    def forward(self, after):
        n, kinds, sizes = self.n, self.kinds, self.sizes

        def emit(arr, old, new):
            x, y, c, _ = _mesh_place()
            near = _near(x, y, c)
            for k in range(n):
                for j in (1, 2, 3):
                    dev, idx = near[j]
                    landed = _window(arr[n + k], kinds[k], idx, sizes[k])
                    _remote(arr[k], landed, old[0], old[1], k * N_NEAR + j, dev).wait_recv()
                    _remote(landed, landed, new[0], new[1], k * N_NEAR + j, near[0][0]).start()

        new, self.arrays, self.token = _split_call(self.name + "_forward", self.arrays, self.sems, [n * N_NEAR] * 2,
                                                   after, emit)
        self.sems = [*self.sems, *new]

    def finish(self, after):
        n, kinds, sizes = self.n, self.kinds, self.sizes

        def emit(arr, old, _):
            x, y, c, me = _mesh_place()
            near = _near(x, y, c)
            other_core = near[0][0]
            for k in range(n):
                win = lambda idx: _window(arr[n + k], kinds[k], idx, sizes[k])
                pltpu.make_async_copy(arr[k], win(me), old[2].at[k]).wait()
                for j, (dev, idx) in enumerate(near):
                    _remote(arr[k], win(me), old[0], old[1], k * N_NEAR + j, dev).wait_send()
                _remote(arr[k], win(near[0][1]), old[0], old[1], k * N_NEAR, other_core).wait_recv()
                for j in (1, 2, 3):
                    idx = near[j][1]
                    _remote(win(idx), win(idx), old[3], old[4], k * N_NEAR + j, other_core).wait_send()
                    _remote(arr[k], win(idx + 1 - 2 * c), old[3], old[4], k * N_NEAR + j, other_core).wait_recv()

        _, arrays, _ = _split_call(self.name + "_finish", self.arrays, self.sems, [], after, emit)
        return arrays[n:]


class _Scatter:
    def __init__(self, partials, kinds, after, name):
        self.n, self.kinds, self.name, self.partials = len(partials), kinds, name, partials
        self.sizes = [p.shape[k] // N_DEV for p, k in zip(partials, kinds)]
        n, sizes = self.n, self.sizes
        self.slot_shapes = []
        for p, k, size in zip(partials, kinds, sizes):
            dims = list(p.shape)
            dims[k] = size
            self.slot_shapes.append((N_NEAR, *dims))
        slots = [lax.empty(sh, p.dtype) for sh, p in zip(self.slot_shapes, partials)]

        def emit(arr, _, new):
            x, y, c, _ = _mesh_place()
            near = _near(x, y, c)
            for k in range(n):
                for j in range(N_NEAR):
                    owner = near[j][1] if j == 0 else near[j][1] + 1 - 2 * c
                    _remote(_window(arr[k], kinds[k], owner, sizes[k]), arr[n + k].at[j], new[0], new[1],
                            k * N_NEAR + j, near[0][0]).start()

        self.sems, self.arrays, self.token = _split_call(name + "_start", [*partials, *slots], [], [n * N_NEAR] * 2,
                                                         after, emit)

    def combine_and_send(self, own4, after):
        n, kinds, sizes = self.n, self.kinds, self.sizes

        def emit_wait(arr, old, _):
            x, y, c, _ = _mesh_place()
            near = _near(x, y, c)
            for k in range(n):
                for j in range(N_NEAR):
                    owner = near[j][1] if j == 0 else near[j][1] + 1 - 2 * c
                    cp = _remote(_window(arr[k], kinds[k], owner, sizes[k]), arr[n + k].at[j], old[0], old[1],
                                 k * N_NEAR + j, near[0][0])
                    cp.wait_send()
                    cp.wait_recv()

        _, arrays, _ = _split_call(self.name + "_landed", self.arrays, self.sems, [], after, emit_wait)
        chip_sums = _chip_sums(arrays[:n], arrays[n:], kinds, sizes, own4, self.name + "_combine")
        arrivals = [lax.empty((N_NEAR - 1, *sh[1:]), p.dtype) for sh, p in zip(self.slot_shapes, self.partials)]

        def emit_send(arr, _, new):
            x, y, c, _ = _mesh_place()
            near = _near(x, y, c)
            for k in range(n):
                for j in (1, 2, 3):
                    _remote(arr[k].at[j], arr[n + k].at[j - 1], new[0], new[1], k * N_NEAR + j, near[j][0]).start()

        self.sems, self.arrays, self.token = _split_call(self.name + "_send", [*chip_sums, *arrivals], [],
                                                         [n * N_NEAR] * 2, own4, emit_send)

    def finish(self, after):
        n = self.n

        def emit(arr, old, _):
            x, y, c, _ = _mesh_place()
            near = _near(x, y, c)
            for k in range(n):
                for j in (1, 2, 3):
                    cp = _remote(arr[k].at[j], arr[n + k].at[j - 1], old[0], old[1], k * N_NEAR + j, near[j][0])
                    cp.wait_send()
                    cp.wait_recv()

        _, arrays, _ = _split_call(self.name + "_finish", self.arrays, self.sems, [], after, emit)
        return arrays[:n], arrays[n:]


def _chip_sums(partials, slots, kinds, sizes, own4, name):
    n = len(partials)

    def body(own_ref, *refs):
        for k in range(n):
            refs[2 * n + k][...] = (refs[k][...].astype(F32) + refs[n + k][...].astype(F32)).astype(BF16)

    in_specs, slot_specs = [], []
    for p, s, kind, size in zip(partials, slots, kinds, sizes):
        block = list(p.shape)
        block[kind] = size
        nd = len(block)
        in_specs.append(pl.BlockSpec(tuple(block), functools.partial(
            lambda j, own, kind, nd: tuple(own[j] if d == kind else 0 for d in range(nd)), kind=kind, nd=nd)))
        slot_specs.append(pl.BlockSpec((None, *block), functools.partial(
            lambda j, own, nd: (j,) + (0,) * nd, nd=nd)))
    return pl.pallas_call(
        body, name=name,
        grid_spec=pltpu.PrefetchScalarGridSpec(num_scalar_prefetch=1, grid=(N_NEAR,),
                                               in_specs=in_specs + slot_specs, out_specs=slot_specs),
        out_shape=[jax.ShapeDtypeStruct(s.shape, s.dtype) for s in slots],
        compiler_params=_params(("arbitrary",)),
    )(own4, *partials, *slots)


def _after(small, token):
    return small + token[0:1, 0:1].astype(small.dtype)


def _silu(c):
    return c * _sigmoid_tail(c)


def _ada_fwd(c_all, w_ada, b_ada_cols):
    def body(c_ref, w_ref, b_ref, out_ref):
        out_ref[...] = jnp.dot(_silu(c_ref[...]), w_ref[...], preferred_element_type=F32,
                               precision=lax.Precision.HIGHEST) + b_ref[...]

    return pl.pallas_call(
        body, name="ada_fwd", out_shape=jax.ShapeDtypeStruct((N_DEV, w_ada.shape[1]), F32),
    )(c_all, w_ada, b_ada_cols)


def _adam(w, g, m, v):
    m = ADAM_B1 * m + (1.0 - ADAM_B1) * g
    v = ADAM_B2 * v + (1.0 - ADAM_B2) * (g * g)
    m_hat = m / (1.0 - ADAM_B1 ** ADAM_STEP)
    v_hat = v / (1.0 - ADAM_B2 ** ADAM_STEP)
    delta = -ADAM_LR * (m_hat / (jnp.sqrt(v_hat) + ADAM_EPS) + ADAM_WD * w)
    return delta, m, v


def _ada_bwd_adam(c_all, dmod_cols, w, m, v):
    def body(c_ref, d_ref, w_ref, m_ref, v_ref, g_ref, delta_ref, nm_ref, nv_ref):
        g = lax.dot_general(_silu(c_ref[...]), d_ref[...], (((0,), (0,)), ((), ())),
                            preferred_element_type=F32, precision=lax.Precision.HIGHEST)
        g_ref[...] = g
        delta_ref[...], nm_ref[...], nv_ref[...] = _adam(w_ref[...], g, m_ref[...], v_ref[...])

    sd = jax.ShapeDtypeStruct(w.shape, F32)
    return pl.pallas_call(body, name="ada_bwd_adam", out_shape=[sd] * 4,
                          compiler_params=pltpu.CompilerParams(vmem_limit_bytes=V7X_VMEM_LIMIT),
                          )(c_all, dmod_cols, w, m, v)


def _adam_group(chip_sums, arrivals, ws, ms, vs, n_tiles, name):
    n = len(ws)

    def body(*refs):
        for k in range(n):
            c_ref, a_ref, w_ref, m_ref, v_ref = (refs[j * n + k] for j in range(5))
            g_ref, delta_ref, nm_ref, nv_ref = (refs[(5 + j) * n + k] for j in range(4))
            g = c_ref[...].astype(F32)
            for j in range(N_NEAR - 1):
                g = g + a_ref[j].astype(F32)
            g_ref[...] = g
            delta_ref[...], nm_ref[...], nv_ref[...] = _adam(w_ref[...], g, m_ref[...], v_ref[...])

    tiles = [(w.shape[0] // n_tiles, w.shape[1]) for w in ws]
    blk = [pl.BlockSpec(t, lambda i: (i, 0)) for t in tiles]
    return pl.pallas_call(
        body, name=name, grid=(n_tiles,),
        in_specs=[pl.BlockSpec((None, *t), lambda i: (0, i, 0)) for t in tiles]
        + [pl.BlockSpec((N_NEAR - 1, *t), lambda i: (0, i, 0)) for t in tiles] + blk * 3,
        out_specs=blk * 4, out_shape=[jax.ShapeDtypeStruct(w.shape, F32) for w in ws] * 4,
        compiler_params=_params(("parallel",)),
    )(*chip_sums, *arrivals, *ws, *ms, *vs)


N_SMALL = 40
N_SMALL_PARAMS = 11


def _pack_vecs(conv_w_full, rows):
    def body(cw_ref, *refs):
        out = refs[-1]
        out[...] = jnp.zeros_like(out)
        out[0:4, :] = cw_ref[0:4, :]
        for r, ref in enumerate(refs[:-1]):
            out[4 + r:5 + r, :] = ref[...]

    return pl.pallas_call(body, name="pack_vecs", out_shape=jax.ShapeDtypeStruct((16, D), F32))(conv_w_full, *rows)


def _small_finish(gathered, mod_all, vecs, ws, ms, vs):
    n = N_SMALL_PARAMS

    def body(g_ref, mod_ref, vec_ref, *refs):
        w_refs, m_refs, v_refs = refs[:n], refs[n:2 * n], refs[2 * n:3 * n]
        outs = refs[3 * n:]
        g1 = vec_ref[V_G1:V_G1 + 1, :]
        g2 = vec_ref[V_G2:V_G2 + 1, :]
        zero = jnp.zeros((1, D), F32)
        dg1, dg2, dgf, loss_lanes = zero, zero, zero, zero
        mixer = jnp.zeros((16, D), F32)
        db_ada = jnp.zeros((6, D), F32)
        for b in range(N_DEV):
            gb = g_ref[b]
            mod = mod_ref[b]
            q1 = gb[33:34]
            q2 = gb[9:10]
            dmod = jnp.concatenate([gb[32:33], q1 * g1, gb[10:11], gb[8:9], q2 * g2, gb[1:2]], axis=0)
            outs[4 * n][b] = dmod
            db_ada = db_ada + dmod
            dg1 = dg1 + q1 * (1.0 + mod[M_SC1:M_SC1 + 1])
            dg2 = dg2 + q2 * (1.0 + mod[M_SC2:M_SC2 + 1])
            dgf = dgf + gb[0:1]
            loss_lanes = loss_lanes + gb[2:3]
            mixer = mixer + gb[16:32]
        d_a_param = mixer[7:8] * _sigmoid_tail(vec_ref[V_A_PARAM:V_A_PARAM + 1, :])
        grads = [dg1, dg2, mixer[4:5], mixer[5:6], mixer[6:7], d_a_param, mixer[8:9], mixer[9:10], dgf,
                 db_ada, mixer[0:4]]
        for k in range(n):
            outs[k][...] = grads[k]
            outs[n + k][...], outs[2 * n + k][...], outs[3 * n + k][...] = _adam(
                w_refs[k][...], grads[k], m_refs[k][...], v_refs[k][...])
        outs[4 * n + 1][...] = jnp.broadcast_to(jnp.sum(loss_lanes, axis=1, keepdims=True), (8, 128))

    shapes = [jax.ShapeDtypeStruct(w.shape, F32) for w in ws]
    return pl.pallas_call(
        body, name="small_finish",
        out_shape=shapes * 4 + [jax.ShapeDtypeStruct((N_DEV, 6, D), F32), jax.ShapeDtypeStruct((8, 128), F32)],
    )(gathered, mod_all, vecs, *ws, *ms, *vs)


def _pad_rows(a, rows):
    return jnp.pad(a, ((0, rows - a.shape[0]), (0, 0)))


def kernel(x, c, norm_mix_g, norm_mlp_g, w_ada, b_ada, w_in, conv_w, conv_b, w_rg_a, b_rg_a, w_rg_x, b_rg_x, a_param, w_branch_a, w_pool, b_pool, pool_scale, w_branch_b, w_out, w_up, w_down, final_g, loss_target, m_norm_mix_g, m_norm_mlp_g, m_w_ada, m_b_ada, m_w_in, m_conv_w, m_conv_b, m_w_rg_a, m_b_rg_a, m_w_rg_x, m_b_rg_x, m_a_param, m_w_branch_a, m_w_pool, m_b_pool, m_pool_scale, m_w_branch_b, m_w_out, m_w_up, m_w_down, m_final_g, v_norm_mix_g, v_norm_mlp_g, v_w_ada, v_b_ada, v_w_in, v_conv_w, v_conv_b, v_w_rg_a, v_b_rg_a, v_w_rg_x, v_b_rg_x, v_a_param, v_w_branch_a, v_w_pool, v_b_pool, v_pool_scale, v_w_branch_b, v_w_out, v_w_up, v_w_down, v_final_g):
    me = 4 * lax.axis_index("x") + 2 * lax.axis_index("y") + lax.axis_index("c")
    s = x.shape[1]
    x2d = x.reshape(s, D)
    target = loss_target.reshape(s, D)
    n_ada = w_ada.shape[2]

    sharded = dict(w_in=(w_in[0], 1), w_up=(w_up[0], 1), w_down=(w_down[0], 0), w_branch_a=(w_branch_a[0], 0),
                   w_branch_b=(w_branch_b[0], 0), w_out=(w_out[0], 0), w_rg_a=(w_rg_a[0], 1), w_rg_x=(w_rg_x[0], 1),
                   w_pool=(w_pool[0], 1))
    kind = {k: v[1] for k, v in sharded.items()}
    shard = {k: v[0].astype(BF16) for k, v in sharded.items()}

    conv_w_full, c_rows = _all_gather([_pad_rows(conv_w[0], 8), _pad_rows(c, 8)], [1, 0], "gather_c")
    c_all = c_rows.reshape(N_DEV, 8, D)[:, 0, :]
    b_ada_cols = lax.dynamic_slice(b_ada, (0, me * n_ada), (1, n_ada))
    mod_part = _ada_fwd(c_all, w_ada[0], b_ada_cols)
    mod_parts, = _all_gather([mod_part], [0], "gather_mod")

    first_names = ["w_in", "w_rg_a", "w_rg_x", "w_pool"]
    branch_names = ["w_branch_a", "w_branch_b", "w_out"]
    mlp_names = ["w_up", "w_down"]

    def gather(group, after, name):
        return _Gather([shard[k] for k in group], [kind[k] for k in group], after, name)

    g_first = gather(first_names, mod_parts, "gather_first")
    g_branch = gather(branch_names, g_first.token, "gather_branch")
    g_mlp = gather(mlp_names, g_branch.token, "gather_mlp")

    mod_all = jnp.transpose(mod_parts.reshape(N_DEV, N_DEV, n_ada), (1, 0, 2)).reshape(N_DEV, 6, D)
    mod_all = jnp.pad(mod_all, ((0, 0), (0, 2), (0, 0)))
    modr = lax.dynamic_index_in_dim(mod_all, me, 0, keepdims=False)
    vecs = _pack_vecs(conv_w_full, [conv_b, b_rg_a, b_rg_x, a_param, b_pool, pool_scale,
                                    norm_mix_g, norm_mlp_g, final_g.reshape(1, D)])
    vecs = _after(vecs, g_mlp.token)
    g_first.forward(vecs)
    wg = dict(zip(first_names, g_first.finish(g_first.token)))

    h1, x_rnn, ga, dga, sa, sb, xr, a, mult, b, ra, ri, p, pooled = _proj_mix_fwd(
        x2d, modr, vecs, wg["w_in"], wg["w_rg_a"], wg["w_rg_x"], wg["w_pool"])
    gates = [a, mult, ra, ri]
    g_branch.forward(h1)
    g_mlp.forward(g_branch.token)
    hr, za = _scan_fwd(a, b, ga)
    wg.update(zip(branch_names, g_branch.finish(g_mlp.token)))
    ba, bb, merged, o, x2, h2 = _branch_fwd(za, pooled, sa, sb, x2d, modr, vecs,
                                            wg["w_branch_a"], wg["w_branch_b"], wg["w_out"])
    wg.update(zip(mlp_names, g_mlp.finish(h2)))
    ru, dx3, d_dn, small_f = _mlp_fwd(h2, x2, target, modr, vecs, wg["w_up"], wg["w_down"])

    near = _near(lax.axis_index("x"), lax.axis_index("y"), lax.axis_index("c"))
    own4 = jnp.stack([me, near[1][1], near[2][1], near[3][1]]).astype(jnp.int32)

    def scatter(group, partial, after, name):
        return _Scatter([partial[k] for k in group], [kind[k] for k in group], after, name)

    dup, dx2, do, small_m = _mlp_bwd(d_dn, ru, x2, dx3, o, modr, vecs, wg["w_up"], wg["w_down"])
    partial = dict(w_up=_wgrad(h2, dup, "wgrad_up"), w_down=_wgrad(ru, d_dn, "wgrad_down", square_a=True))
    s_mlp = scatter(mlp_names, partial, dx2, "scatter_mlp")

    dba, dbb, dgates, dza, dpooled = _branch_bwd(do, sa, sb, ba, bb, wg["w_branch_a"], wg["w_branch_b"], wg["w_out"],
                                                 dep=s_mlp.token)
    s_mlp.combine_and_send(own4, dza)
    dproj, dw_rg_a, dw_rg_x, dw_pool, small_x = _mix_bwd(dza, dpooled, x_rnn, ga, dga, xr, hr, p, gates, dgates,
                                                         _after(vecs, s_mlp.token),
                                                         wg["w_rg_a"], wg["w_rg_x"], wg["w_pool"])
    partial.update(w_branch_a=_wgrad(za, dba, "wgrad_branch_a"), w_branch_b=_wgrad(pooled, dbb, "wgrad_branch_b"),
                   w_out=_wgrad(merged, do, "wgrad_out"),
                   w_rg_a=dw_rg_a.astype(BF16), w_rg_x=dw_rg_x.astype(BF16), w_pool=dw_pool.astype(BF16))
    mixer_names = ["w_rg_a", "w_rg_x", "w_pool", "w_branch_a", "w_branch_b", "w_out"]
    s_mixer = scatter(mixer_names, partial, s_mlp.token, "scatter_mixer")

    partial["w_in"] = _wgrad(h1, dproj, "wgrad_in", dep=s_mixer.token)
    s_in = scatter(["w_in"], partial, s_mixer.token, "scatter_in")
    s_mixer.combine_and_send(own4, s_in.token)
    s_in.combine_and_send(own4, s_mixer.token)
    grad_x, small_p = _proj_bwd(dproj, x2d, dx2, _after(modr, s_in.token), vecs, wg["w_in"])

    locals_ = dict(w_in=(w_in, m_w_in, v_w_in), w_up=(w_up, m_w_up, v_w_up), w_down=(w_down, m_w_down, v_w_down),
                   w_branch_a=(w_branch_a, m_w_branch_a, v_w_branch_a),
                   w_branch_b=(w_branch_b, m_w_branch_b, v_w_branch_b), w_out=(w_out, m_w_out, v_w_out),
                   w_rg_a=(w_rg_a, m_w_rg_a, v_w_rg_a), w_rg_x=(w_rg_x, m_w_rg_x, v_w_rg_x),
                   w_pool=(w_pool, m_w_pool, v_w_pool))
    res = {}

    def finish(group, exchange, after, n_tiles, name):
        chip_sums, arrivals = exchange.finish(after)
        flat = lambda t: t.reshape(-1, t.shape[-1])
        shapes = [flat(locals_[k][0]).shape for k in group]
        outs = _adam_group([cs.reshape(N_NEAR, *sh) for cs, sh in zip(chip_sums, shapes)],
                           [ar.reshape(N_NEAR - 1, *sh) for ar, sh in zip(arrivals, shapes)],
                           *[[flat(locals_[k][j]) for k in group] for j in range(3)], n_tiles, name)
        for i, k in enumerate(group):
            res[k] = [outs[j * len(group) + i].reshape(locals_[k][0].shape) for j in range(4)]
        return res[group[-1]][0]

    small = jnp.concatenate([small_f, small_m, small_x, small_p], axis=0)
    g_small = _Gather([small], [0], grad_x, "gather_small")
    done = finish(mlp_names, s_mlp, g_small.token, 4, "adam_mlp")
    g_small.forward(done)
    done = finish(mixer_names, s_mixer, g_small.token, 2, "adam_mixer")
    done = finish(["w_in"], s_in, done, 4, "adam_in")
    small_all, = g_small.finish(done)
    small_all = small_all.reshape(N_DEV, N_SMALL, D)

    def embed(cw):
        return lax.dynamic_update_slice(jnp.zeros((4, D), F32), cw[0], (0, me * (D // N_DEV)))

    def smalls(ng, nl, cb, bra, brx, ap, bp, ps, fg, ba_, cw):
        return [ng, nl, cb, bra, brx, ap, bp, ps, fg.reshape(1, D), ba_.reshape(6, D), embed(cw)]

    small_names = ["norm_mix_g", "norm_mlp_g", "conv_b", "b_rg_a", "b_rg_x", "a_param", "b_pool", "pool_scale",
                   "final_g", "b_ada", "conv_w"]
    fin = _small_finish(
        small_all, mod_all, vecs,
        smalls(norm_mix_g, norm_mlp_g, conv_b, b_rg_a, b_rg_x, a_param, b_pool, pool_scale, final_g, b_ada, conv_w),
        smalls(m_norm_mix_g, m_norm_mlp_g, m_conv_b, m_b_rg_a, m_b_rg_x, m_a_param, m_b_pool, m_pool_scale,
               m_final_g, m_b_ada, m_conv_w),
        smalls(v_norm_mix_g, v_norm_mlp_g, v_conv_b, v_b_rg_a, v_b_rg_x, v_a_param, v_b_pool, v_pool_scale,
               v_final_g, v_b_ada, v_conv_w))
    dmod_all, loss_tile = fin[4 * N_SMALL_PARAMS], fin[4 * N_SMALL_PARAMS + 1]
    dmod_cols = lax.dynamic_slice(dmod_all.reshape(N_DEV, 6 * D), (0, me * n_ada), (N_DEV, n_ada))
    res["w_ada"] = [t.reshape(w_ada.shape) for t in _ada_bwd_adam(c_all, dmod_cols, w_ada[0], m_w_ada[0], v_w_ada[0])]

    def final_shape(k, t):
        if k == "final_g":
            return t.reshape(D)
        if k == "b_ada":
            return t.reshape(1, 6 * D)
        if k == "conv_w":
            return lax.dynamic_slice(t, (0, me * (D // N_DEV)), (4, D // N_DEV)).reshape(conv_w.shape)
        return t

    for i, k in enumerate(small_names):
        res[k] = [final_shape(k, fin[which * N_SMALL_PARAMS + i]) for which in range(4)]
    order = ["norm_mix_g", "norm_mlp_g", "w_ada", "b_ada", "w_in", "conv_w", "conv_b", "w_rg_a", "b_rg_a", "w_rg_x",
             "b_rg_x", "a_param", "w_branch_a", "w_pool", "b_pool", "pool_scale", "w_branch_b", "w_out", "w_up",
             "w_down", "final_g"]
    outs = [loss_tile[0, 0], grad_x.reshape(x.shape)]
    for which in range(4):
        for k in order:
            outs.append(res[k][which])
    return tuple(outs)
```

```python
import functools

import jax
import jax.numpy as jnp
from jax import lax
from jax.experimental import pallas as pl
from jax.experimental.pallas import tpu as pltpu

F32 = jnp.float32
BF16 = jnp.bfloat16
MESH = pl.DeviceIdType.MESH

N_DEV = 8
D = 1024
N_GROUPS = 4
GW = D // N_GROUPS
D_IN = 5 * D
D_FF = 4 * D
POOL_WINDOWS = (2, 4, 8, 16)
HALO_X = 8
HALO_U = 16
EPS = 1e-6
C_RG = 8.0
ADAM_LR, ADAM_B1, ADAM_B2, ADAM_EPS, ADAM_WD, ADAM_STEP = 0.001, 0.9, 0.999, 1e-08, 0.01, 10

V7X_VMEM_LIMIT = 56 * 1024 * 1024

V_CONV_W, V_CONV_B, V_B_RG_A, V_B_RG_X, V_A_PARAM, V_B_POOL, V_POOL_SCALE, V_G1, V_G2, V_GF = 0, 4, 5, 6, 7, 8, 9, 10, 11, 12
M_SH1, M_SC1, M_GT1, M_SH2, M_SC2, M_GT2 = 0, 1, 2, 3, 4, 5

TM_PROJ = 512
TM_MIX = 256
TM_BRANCH = 256
TM_MLP = 512
TM_MLP_BWD = 256
TS_WGRAD = 1024


def _params(semantics):
    return pltpu.CompilerParams(dimension_semantics=semantics, vmem_limit_bytes=V7X_VMEM_LIMIT)


def _resident(shape):
    return pl.BlockSpec(shape, lambda *_: (0,) * len(shape), pipeline_mode=pl.Buffered(1))


def _dot(a, b):
    return jnp.dot(a, b, preferred_element_type=F32)


def _dot_nt(a, b):
    return lax.dot_general(a, b, (((1,), (1,)), ((), ())), preferred_element_type=F32)


def _dot_tn(a, b):
    return lax.dot_general(a, b, (((0,), (0,)), ((), ())), preferred_element_type=F32)


def _sigmoid(x):
    return 0.5 * jnp.tanh(0.5 * x) + 0.5


def _sigmoid_tail(x):
    return 1.0 / (1.0 + jnp.exp(-x))


def _gelu_and_grad(x):
    k = 0.7978845608028654
    x2 = x * x
    t = jnp.tanh(k * (x + 0.044715 * x * x2))
    g = 0.5 * x * (1.0 + t)
    dg = 0.5 * (1.0 + t) + 0.5 * x * (1.0 - t * t) * (k * (1.0 + 3.0 * 0.044715 * x2))
    return g, dg


def _softplus(a):
    e = jnp.exp(-jnp.abs(a))
    u = 1.0 + e
    log1p_e = jnp.where(u == 1.0, e, jnp.log(u) * e / jnp.where(u == 1.0, 1.0, u - 1.0))
    return jnp.maximum(a, 0.0) + log1p_e


def _neg_expm1(z):
    series = -(z * (1.0 + z * (0.5 + z * (1.0 / 6.0 + z * (1.0 / 24.0 + z * (1.0 / 120.0))))))
    return jnp.where(z > -0.1, series, 1.0 - jnp.exp(z))


def _shift_down(x, k):
    return pltpu.roll(x, k, 0)


def _shift_up(x, k):
    return pltpu.roll(x, x.shape[0] - k, 0)


def _rglru_gates(xr, w_a, w_x, b_a, b_x, a_param, is_t0):
    xb = xr.astype(BF16)
    ra = _sigmoid(_dot(xb, w_a) + b_a)
    ri = _sigmoid(_dot(xb, w_x) + b_x)
    sp = _softplus(a_param)
    log_a = (-C_RG) * ra * sp
    a = jnp.exp(log_a)
    mult = jnp.where(is_t0, 1.0, jnp.sqrt(_neg_expm1(2.0 * log_a)))
    return ra, ri, sp, a, mult


SUBLANES = 8


LANES = 128


def _scan_strip(a, b, carry, scr, down):
    t = b.shape[0]
    g = t // SUBLANES
    a3 = a.reshape(g, SUBLANES, LANES)
    b3 = b.reshape(g, SUBLANES, LANES)
    sub = lax.broadcasted_iota(jnp.int32, (g, SUBLANES, LANES), 1)
    for k in (1, 2, 4):
        keep = sub >= k if down else sub < SUBLANES - k
        shift = k if down else SUBLANES - k
        b3 = b3 + a3 * jnp.where(keep, pltpu.roll(b3, shift, 1), 0.0)
        a3 = a3 * jnp.where(keep, pltpu.roll(a3, shift, 1), 1.0)
    scr[0] = a3.reshape(t, LANES)
    scr[1] = b3.reshape(t, LANES)
    end_row = SUBLANES - 1 if down else 0
    ag = scr[0, pl.ds(end_row, g, stride=SUBLANES), :]
    bg = scr[1, pl.ds(end_row, g, stride=SUBLANES), :]
    rg = lax.broadcasted_iota(jnp.int32, (g, LANES), 0)
    edge = 0 if down else g - 1
    bg = bg + jnp.where(rg == edge, ag * carry, 0.0)
    k = 1
    while k < g:
        keep = rg >= k if down else rg < g - k
        shift = k if down else g - k
        bg = bg + ag * jnp.where(keep, pltpu.roll(bg, shift, 0), 0.0)
        if 2 * k < g:
            ag = ag * pltpu.roll(ag, shift, 0)
        k *= 2
    entering = jnp.where(rg != edge, pltpu.roll(bg, 1 if down else g - 1, 0), carry)
    for r in range(SUBLANES):
        scr[2, pl.ds(r, g, stride=SUBLANES), :] = entering
    return scr[1] + scr[0] * scr[2], bg[g - 1:g, :]


def _scan_strips(a, b, carry, scr, down):
    outs = [_scan_strip(a[:, c:c + LANES], b[:, c:c + LANES], carry[:, c:c + LANES], scr, down)
            for c in range(0, b.shape[1], LANES)]
    return jnp.concatenate([o[0] for o in outs], axis=1), jnp.concatenate([o[1] for o in outs], axis=1)


def _scan_down(a, b, carry, scr):
    return _scan_strips(a, b, carry, scr, True)


def _scan_up(m, b, carry, scr):
    return _scan_strips(m, b, carry, scr, False)[0]


def _conv_taps(x_ext):
    return [_shift_down(x_ext, 3 - j)[HALO_X:] if j < 3 else x_ext[HALO_X:] for j in range(4)]


def _proj_mix_fwd(x, modr, vecs, w_in, w_rg_a, w_rg_x, w_pool):
    s = x.shape[0]
    tm = min(TM_MIX, s)

    def body(x_ref, mod_ref, vec_ref, w_ref, wa_ref, wx_ref, wp_ref,
             h1_ref, xrnn_ref, ga_ref, dga_ref, sa_ref, sb_ref, xr_ref, a_ref, mult_ref, b_ref,
             ra_ref, ri_ref, p_ref, pooled_ref, xh_scr, uh_scr):
        i = pl.program_id(0)
        first = i == 0
        xv = x_ref[...]
        r = lax.rsqrt(jnp.mean(xv * xv, axis=-1, keepdims=True) + EPS)
        gain = vec_ref[V_G1:V_G1 + 1, :] * (1.0 + mod_ref[M_SC1:M_SC1 + 1, :])
        h = (xv * r * gain + mod_ref[M_SH1:M_SH1 + 1, :]).astype(BF16)
        h1_ref[...] = h
        x_rnn = _dot(h, w_ref[:, 0:D])
        xrnn_ref[...] = x_rnn
        u = _dot(h, w_ref[:, 2 * D:3 * D])
        ga_ref[...], dga_ref[...] = _gelu_and_grad(_dot(h, w_ref[:, D:2 * D]))
        sa_ref[...] = _sigmoid(_dot(h, w_ref[:, 3 * D:4 * D]))
        sb_ref[...] = _sigmoid(_dot(h, w_ref[:, 4 * D:5 * D]))

        row = lax.broadcasted_iota(jnp.int32, (tm, GW), 0)
        is_t0 = jnp.logical_and(first, row == 0)
        t_glob = (row + i * tm + 1).astype(F32)
        for g in range(N_GROUPS):
            cs = slice(g * GW, (g + 1) * GW)
            vec = vec_ref[:, cs]
            xh = jnp.where(first, 0.0, xh_scr[:, cs])
            taps = _conv_taps(jnp.concatenate([xh, x_rnn[:, cs]], axis=0))
            xr = vec[V_CONV_B:V_CONV_B + 1]
            for j in range(4):
                xr = xr + vec[V_CONV_W + j:V_CONV_W + j + 1] * taps[j]
            xr_ref[:, cs] = xr
            ra, ri, _, a, mult = _rglru_gates(
                xr, wa_ref[g], wx_ref[g], vec[V_B_RG_A:V_B_RG_A + 1], vec[V_B_RG_X:V_B_RG_X + 1],
                vec[V_A_PARAM:V_A_PARAM + 1], is_t0)
            a_ref[:, cs] = a
            mult_ref[:, cs] = mult
            ra_ref[:, cs] = ra.astype(BF16)
            ri_ref[:, cs] = ri.astype(BF16)
            b_ref[:, cs] = xr * ri * mult
            uh = jnp.where(first, 0.0, uh_scr[:, cs])
            ug = u[:, cs]
            sm = jnp.concatenate([uh, ug], axis=0)
            k = 1
            while k < POOL_WINDOWS[g]:
                sm = sm + _shift_down(sm, k)
                k *= 2
            cnt = jnp.minimum(t_glob, float(POOL_WINDOWS[g]))
            p = (sm[HALO_U:] / cnt - ug).astype(BF16)
            p_ref[:, cs] = p
            pb = _dot(p, wp_ref[g]) + vec[V_B_POOL:V_B_POOL + 1]
            pooled_ref[:, cs] = (pb * vec[V_POOL_SCALE:V_POOL_SCALE + 1]).astype(BF16)
        xh_scr[...] = x_rnn[tm - HALO_X:, :]
        uh_scr[...] = u[tm - HALO_U:, :]

    tok = pl.BlockSpec((tm, D), lambda i: (i, 0))
    wspec = pl.BlockSpec((N_GROUPS, GW, GW), lambda i: (0, 0, 0))
    sd = lambda dt: jax.ShapeDtypeStruct((s, D), dt)
    return pl.pallas_call(
        body, name="proj_mix_fwd", grid=(s // tm,),
        in_specs=[tok, pl.BlockSpec((8, D), lambda i: (0, 0)), pl.BlockSpec((16, D), lambda i: (0, 0)),
                  _resident((D, D_IN)), wspec, wspec, wspec],
        out_specs=[tok] * 14,
        out_shape=[sd(BF16)] + [sd(F32)] * 9 + [sd(BF16)] * 4,
        scratch_shapes=[pltpu.VMEM((HALO_X, D), F32), pltpu.VMEM((HALO_U, D), F32)],
        compiler_params=_params(("arbitrary",)),
    )(x, modr, vecs, w_in, w_rg_a, w_rg_x, w_pool)


def _scan_fwd(a, b, ga):
    s = a.shape[0]
    tm = min(TM_MIX, s)

    def body(a_ref, b_ref, ga_ref, hr_ref, za_ref, carry_ref, scan_scr):
        @pl.when(pl.program_id(0) == 0)
        def _():
            carry_ref[...] = jnp.zeros_like(carry_ref)

        for g in range(N_GROUPS):
            cs = slice(g * GW, (g + 1) * GW)
            h, last = _scan_down(a_ref[:, cs], b_ref[:, cs], carry_ref[0:1, cs], scan_scr)
            hr_ref[:, cs] = h
            carry_ref[0:1, cs] = last
            za_ref[:, cs] = (ga_ref[:, cs] * h).astype(BF16)

    tok = pl.BlockSpec((tm, D), lambda i: (i, 0))
    return pl.pallas_call(
        body, name="scan_fwd", grid=(s // tm,),
        in_specs=[tok, tok, tok], out_specs=[tok, tok],
        out_shape=[jax.ShapeDtypeStruct((s, D), F32), jax.ShapeDtypeStruct((s, D), BF16)],
        scratch_shapes=[pltpu.VMEM((8, D), F32), pltpu.VMEM((3, tm, LANES), F32)],
        compiler_params=_params(("arbitrary",)),
    )(a, b, ga)


def _branch_fwd(za, pooled, sa, sb, x, modr, vecs, w_a, w_b, w_out):
    s = x.shape[0]
    tm = min(TM_BRANCH, s)

    def body(za_ref, pooled_ref, sa_ref, sb_ref, x_ref, mod_ref, vec_ref, wa_ref, wb_ref, wo_ref,
             ba_ref, bb_ref, merged_ref, o_ref, x2_ref, h2_ref):
        ba = _dot(za_ref[...], wa_ref[...])
        bb = _dot(pooled_ref[...], wb_ref[...])
        ba_ref[...] = ba.astype(BF16)
        bb_ref[...] = bb.astype(BF16)
        merged = (sa_ref[...] * ba + sb_ref[...] * bb).astype(BF16)
        merged_ref[...] = merged
        o = _dot(merged, wo_ref[...])
        o_ref[...] = o.astype(BF16)
        x2 = x_ref[...] + mod_ref[M_GT1:M_GT1 + 1, :] * o
        x2_ref[...] = x2
        r = lax.rsqrt(jnp.mean(x2 * x2, axis=-1, keepdims=True) + EPS)
        gain = vec_ref[V_G2:V_G2 + 1, :] * (1.0 + mod_ref[M_SC2:M_SC2 + 1, :])
        h2_ref[...] = (x2 * r * gain + mod_ref[M_SH2:M_SH2 + 1, :]).astype(BF16)

    tok = pl.BlockSpec((tm, D), lambda i: (i, 0))
    wspec = pl.BlockSpec((D, D), lambda i: (0, 0))
    sd = lambda dt: jax.ShapeDtypeStruct((s, D), dt)
    return pl.pallas_call(
        body, name="branch_fwd", grid=(s // tm,),
        in_specs=[tok, tok, tok, tok,
                  tok, pl.BlockSpec((8, D), lambda i: (0, 0)), pl.BlockSpec((16, D), lambda i: (0, 0)),
                  wspec, wspec, wspec],
        out_specs=[tok] * 6,
        out_shape=[sd(BF16), sd(BF16), sd(BF16), sd(BF16), sd(F32), sd(BF16)],
        compiler_params=_params(("parallel",)),
    )(za, pooled, sa, sb, x, modr, vecs, w_a, w_b, w_out)


def _mlp_fwd(h2, x2, target, modr, vecs, w_up, w_down):
    s = x2.shape[0]
    tm = min(TM_MLP, s)

    def body(h2_ref, x2_ref, tgt_ref, mod_ref, vec_ref, wu_ref, wd_ref,
             ru_ref, dx3_ref, ddn_ref, small_ref):
        @pl.when(pl.program_id(0) == 0)
        def _():
            small_ref[...] = jnp.zeros_like(small_ref)

        h2 = h2_ref[...]
        dn = None
        for c in range(D_FF // D):
            cs = slice(c * D, (c + 1) * D)
            ru = jnp.maximum(_dot(h2, wu_ref[:, cs]), 0.0)
            ru_ref[:, cs] = ru.astype(BF16)
            part = _dot((ru * ru).astype(BF16), wd_ref[cs, :])
            dn = part if dn is None else dn + part
        gt2 = mod_ref[M_GT2:M_GT2 + 1, :]
        gf = vec_ref[V_GF:V_GF + 1, :]
        x3 = x2_ref[...] + gt2 * dn
        r3 = lax.rsqrt(jnp.mean(x3 * x3, axis=-1, keepdims=True) + EPS)
        n3 = x3 * r3
        err = n3 * gf - tgt_ref[...]
        dy = err * (1.0 / D)
        dn3 = dy * gf
        dx3 = r3 * (dn3 - n3 * jnp.mean(dn3 * n3, axis=-1, keepdims=True))
        dx3_ref[...] = dx3
        ddn_ref[...] = (dx3 * gt2).astype(BF16)
        small_ref[0:1, :] += jnp.sum(dy * n3, axis=0, keepdims=True)
        small_ref[1:2, :] += jnp.sum(dx3 * dn, axis=0, keepdims=True)
        small_ref[2:3, :] += (0.5 / D) * jnp.sum(err * err, axis=0, keepdims=True)

    tok = pl.BlockSpec((tm, D), lambda i: (i, 0))
    return pl.pallas_call(
        body, name="mlp_fwd", grid=(s // tm,),
        in_specs=[tok, tok, tok,
                  pl.BlockSpec((8, D), lambda i: (0, 0)), pl.BlockSpec((16, D), lambda i: (0, 0)),
                  _resident((D, D_FF)), _resident((D_FF, D))],
        out_specs=[pl.BlockSpec((tm, D_FF), lambda i: (i, 0)), tok, tok,
                   pl.BlockSpec((8, D), lambda i: (0, 0))],
        out_shape=[jax.ShapeDtypeStruct((s, D_FF), BF16), jax.ShapeDtypeStruct((s, D), F32),
                   jax.ShapeDtypeStruct((s, D), BF16), jax.ShapeDtypeStruct((8, D), F32)],
        compiler_params=_params(("arbitrary",)),
    )(h2, x2, target, modr, vecs, w_up, w_down)


def _mlp_bwd(d_dn, ru, x2, dx3, o, modr, vecs, w_up, w_down):
    s = x2.shape[0]
    tm = min(TM_MLP_BWD, s)

    def body(ddn_ref, ru_ref, x2_ref, dx3_ref, o_ref, mod_ref, vec_ref, wu_ref, wd_ref,
             dup_ref, dx2_ref, do_ref, small_ref):
        @pl.when(pl.program_id(0) == 0)
        def _():
            small_ref[...] = jnp.zeros_like(small_ref)

        ddn = ddn_ref[...]
        dh2 = None
        for c in range(D_FF // D):
            cs = slice(c * D, (c + 1) * D)
            dff = _dot_nt(ddn, wd_ref[cs, :])
            dup = (dff * (2.0 * ru_ref[:, cs].astype(F32))).astype(BF16)
            dup_ref[:, cs] = dup
            part = _dot_nt(dup, wu_ref[:, cs])
            dh2 = part if dh2 is None else dh2 + part
        x2 = x2_ref[...]
        r2 = lax.rsqrt(jnp.mean(x2 * x2, axis=-1, keepdims=True) + EPS)
        xn2 = x2 * r2
        gain = vec_ref[V_G2:V_G2 + 1, :] * (1.0 + mod_ref[M_SC2:M_SC2 + 1, :])
        dxn2 = dh2 * gain
        dx2 = dx3_ref[...] + r2 * (dxn2 - xn2 * jnp.mean(dxn2 * xn2, axis=-1, keepdims=True))
        dx2_ref[...] = dx2
        do_ref[...] = (dx2 * mod_ref[M_GT1:M_GT1 + 1, :]).astype(BF16)
        small_ref[0:1, :] += jnp.sum(dh2, axis=0, keepdims=True)
        small_ref[1:2, :] += jnp.sum(dh2 * xn2, axis=0, keepdims=True)
        small_ref[2:3, :] += jnp.sum(dx2 * o_ref[...].astype(F32), axis=0, keepdims=True)

    tok = pl.BlockSpec((tm, D), lambda i: (i, 0))
    wide = pl.BlockSpec((tm, D_FF), lambda i: (i, 0))
    return pl.pallas_call(
        body, name="mlp_bwd", grid=(s // tm,),
        in_specs=[tok, wide, tok, tok, tok,
                  pl.BlockSpec((8, D), lambda i: (0, 0)), pl.BlockSpec((16, D), lambda i: (0, 0)),
                  _resident((D, D_FF)), _resident((D_FF, D))],
        out_specs=[wide, tok, tok, pl.BlockSpec((8, D), lambda i: (0, 0))],
        out_shape=[jax.ShapeDtypeStruct((s, D_FF), BF16), jax.ShapeDtypeStruct((s, D), F32),
                   jax.ShapeDtypeStruct((s, D), BF16), jax.ShapeDtypeStruct((8, D), F32)],
        compiler_params=_params(("arbitrary",)),
    )(d_dn, ru, x2, dx3, o, modr, vecs, w_up, w_down)


def _branch_bwd(do, sa, sb, ba, bb, w_a, w_b, w_out, dep):
    s = do.shape[0]
    tm = min(TM_BRANCH, s)

    def body(do_ref, sa_ref, sb_ref, ba_ref, bb_ref, wa_ref, wb_ref, wo_ref, dep_ref,
             dba_ref, dbb_ref, dg_ref, dza_ref, dpooled_ref):
        dmerged = _dot_nt(do_ref[...], wo_ref[...])
        sa = sa_ref[...]
        sb = sb_ref[...]
        dba = (dmerged * sa).astype(BF16)
        dbb = (dmerged * sb).astype(BF16)
        dba_ref[...] = dba
        dbb_ref[...] = dbb
        dg_ref[:, :D] = (dmerged * ba_ref[...].astype(F32) * sa * (1.0 - sa)).astype(BF16)
        dg_ref[:, D:] = (dmerged * bb_ref[...].astype(F32) * sb * (1.0 - sb)).astype(BF16)
        dza_ref[...] = _dot_nt(dba, wa_ref[...])
        dpooled_ref[...] = _dot_nt(dbb, wb_ref[...])

    tok = pl.BlockSpec((tm, D), lambda i: (i, 0))
    wspec = pl.BlockSpec((D, D), lambda i: (0, 0))
    sd = lambda dt: jax.ShapeDtypeStruct((s, D), dt)
    return pl.pallas_call(
        body, name="branch_bwd", grid=(s // tm,),
        in_specs=[tok, tok, tok, tok, tok, wspec, wspec, wspec, pl.BlockSpec(memory_space=pl.ANY)],
        out_specs=[tok, tok, pl.BlockSpec((tm, 2 * D), lambda i: (i, 0)), tok, tok],
        out_shape=[sd(BF16), sd(BF16), jax.ShapeDtypeStruct((s, 2 * D), BF16), sd(F32), sd(F32)],
        compiler_params=_params(("parallel",)),
    )(do, sa, sb, ba, bb, w_a, w_b, w_out, dep)


def _mix_bwd(dza, dpooled, x_rnn, ga, dga, xr, hr, p, gates, dgates, vecs, w_rg_a, w_rg_x, w_pool):
    s = xr.shape[0]
    tm = min(TM_MIX, s)
    nb = s // tm

    def body(dza_ref, dpooled_ref, xh_ref, x_ref, ga_ref, dga_ref, xr_ref, hh_ref, hr_ref, p_ref,
             a_ref, mult_ref, ra_ref, ri_ref, dg_ref, vec_ref, wa_ref, wx_ref, wp_ref,
             dproj_ref, dwa_ref, dwx_ref, dwp_ref, small_ref,
             scan_carry, dxr_carry, q_carry, scan_scr):
        i = pl.program_id(0)
        bi = nb - 1 - i
        first_t = bi == 0

        @pl.when(i == 0)
        def _():
            scan_carry[...] = jnp.zeros_like(scan_carry)
            dxr_carry[...] = jnp.zeros_like(dxr_carry)
            q_carry[...] = jnp.zeros_like(q_carry)
            dwa_ref[...] = jnp.zeros_like(dwa_ref)
            dwx_ref[...] = jnp.zeros_like(dwx_ref)
            dwp_ref[...] = jnp.zeros_like(dwp_ref)
            small_ref[...] = jnp.zeros_like(small_ref)

        row = lax.broadcasted_iota(jnp.int32, (tm, GW), 0)
        is_t0 = jnp.logical_and(first_t, row == 0)
        t_glob = (row + bi * tm + 1).astype(F32)
        colsum = lambda v: jnp.sum(v, axis=0, keepdims=True)
        for g in range(N_GROUPS):
            cs = slice(g * GW, (g + 1) * GW)
            vec = vec_ref[:, cs]
            xr = xr_ref[:, cs]
            hr = hr_ref[:, cs]
            dza = dza_ref[:, cs]
            dproj_ref[:, D + g * GW:D + (g + 1) * GW] = (dza * hr * dga_ref[:, cs]).astype(BF16)
            dhr = dza * ga_ref[:, cs]
            a = a_ref[:, cs]
            mult = mult_ref[:, cs]
            ra = ra_ref[:, cs].astype(F32)
            ri = ri_ref[:, cs].astype(F32)
            sp = _softplus(vec[V_A_PARAM:V_A_PARAM + 1])
            m = jnp.where(row == tm - 1, 1.0, _shift_up(a, 1))
            gsum = _scan_up(m, dhr, scan_carry[0:1, cs], scan_scr)
            scan_carry[0:1, cs] = a[0:1, :] * gsum[0:1, :]
            hh = jnp.where(first_t, 0.0, hh_ref[:, cs])
            hprev = _shift_down(jnp.concatenate([hh, hr], axis=0), 1)[8:]
            da = gsum * hprev
            dmult = jnp.where(is_t0, 0.0, gsum * xr * ri)
            dlog_a = da * a - dmult * a * a / mult
            dri = gsum * xr * mult
            dxr = gsum * ri * mult
            small_ref[7:8, cs] += colsum((-C_RG) * ra * dlog_a)
            dpa = (((-C_RG) * sp) * dlog_a * ra * (1.0 - ra))
            dpx = dri * ri * (1.0 - ri)
            small_ref[5:6, cs] += colsum(dpa)
            small_ref[6:7, cs] += colsum(dpx)
            dpa = dpa.astype(BF16)
            dpx = dpx.astype(BF16)
            xrb = xr.astype(BF16)
            dwa_ref[g] += _dot_tn(xrb, dpa)
            dwx_ref[g] += _dot_tn(xrb, dpx)
            dxr = dxr + _dot_nt(dpa, wa_ref[g]) + _dot_nt(dpx, wx_ref[g])
            small_ref[4:5, cs] += colsum(dxr)
            xh = jnp.where(first_t, 0.0, xh_ref[:, cs])
            taps = _conv_taps(jnp.concatenate([xh, x_ref[:, cs]], axis=0))
            dxr_ext = jnp.concatenate([dxr, dxr_carry[:, cs]], axis=0)
            dx = vec[V_CONV_W + 3:V_CONV_W + 4] * dxr
            for j in range(4):
                small_ref[j:j + 1, cs] += colsum(dxr * taps[j])
                if j < 3:
                    dx = dx + vec[V_CONV_W + j:V_CONV_W + j + 1] * _shift_up(dxr_ext, 3 - j)[:tm]
            dxr_carry[:, cs] = dxr[0:8, :]
            dproj_ref[:, cs] = dx.astype(BF16)
            pg = p_ref[:, cs]
            dpooled = dpooled_ref[:, cs]
            pb = _dot(pg, wp_ref[g]) + vec[V_B_POOL:V_B_POOL + 1]
            small_ref[9:10, cs] += colsum(dpooled * pb)
            dpb = dpooled * vec[V_POOL_SCALE:V_POOL_SCALE + 1]
            small_ref[8:9, cs] += colsum(dpb)
            dpbb = dpb.astype(BF16)
            dwp_ref[g] += _dot_tn(pg, dpbb)
            dp = _dot_nt(dpbb, wp_ref[g])
            q = dp / jnp.minimum(t_glob, float(POOL_WINDOWS[g]))
            sm = jnp.concatenate([q, q_carry[:, cs]], axis=0)
            k = 1
            while k < POOL_WINDOWS[g]:
                sm = sm + _shift_up(sm, k)
                k *= 2
            q_carry[:, cs] = q[0:HALO_U, :]
            dproj_ref[:, 2 * D + g * GW:2 * D + (g + 1) * GW] = (sm[:tm] - dp).astype(BF16)
        dproj_ref[:, 3 * D:] = dg_ref[...]

    rev = lambda i: nb - 1 - i
    tok = pl.BlockSpec((tm, D), lambda i: (rev(i), 0))
    halo8 = lambda k: pl.BlockSpec((8, D), lambda i: (jnp.maximum(rev(i) * (tm // 8) - 1, 0), k))
    wspec = pl.BlockSpec((N_GROUPS, GW, GW), lambda i: (0, 0, 0))
    wshape = jax.ShapeDtypeStruct((N_GROUPS, GW, GW), F32)
    return pl.pallas_call(
        body, name="mix_bwd", grid=(nb,),
        in_specs=[tok, tok, halo8(0), tok, tok, tok, tok, halo8(0), tok, tok, tok, tok, tok, tok,
                  pl.BlockSpec((tm, 2 * D), lambda i: (rev(i), 0)),
                  pl.BlockSpec((16, D), lambda i: (0, 0)), wspec, wspec, wspec],
        out_specs=[pl.BlockSpec((tm, D_IN), lambda i: (rev(i), 0)), wspec, wspec, wspec,
                   pl.BlockSpec((16, D), lambda i: (0, 0))],
        out_shape=[jax.ShapeDtypeStruct((s, D_IN), BF16), wshape, wshape, wshape,
                   jax.ShapeDtypeStruct((16, D), F32)],
        scratch_shapes=[pltpu.VMEM((8, D), F32), pltpu.VMEM((8, D), F32), pltpu.VMEM((HALO_U, D), F32),
                        pltpu.VMEM((3, tm, LANES), F32)],
        compiler_params=_params(("arbitrary",)),
    )(dza, dpooled, x_rnn, x_rnn, ga, dga, xr, hr, hr, p, *gates, dgates, vecs, w_rg_a, w_rg_x, w_pool)


def _proj_bwd(dproj, x, dx2, modr, vecs, w_in):
    s = x.shape[0]
    tm = min(TM_PROJ, s)

    def body(dp_ref, x_ref, dx2_ref, mod_ref, vec_ref, w_ref, gx_ref, small_ref):
        @pl.when(pl.program_id(0) == 0)
        def _():
            small_ref[...] = jnp.zeros_like(small_ref)

        dh1 = None
        for c in range(D_IN // D):
            cs = slice(c * D, (c + 1) * D)
            part = _dot_nt(dp_ref[:, cs], w_ref[:, cs])
            dh1 = part if dh1 is None else dh1 + part
        xv = x_ref[...]
        r1 = lax.rsqrt(jnp.mean(xv * xv, axis=-1, keepdims=True) + EPS)
        xn1 = xv * r1
        gain = vec_ref[V_G1:V_G1 + 1, :] * (1.0 + mod_ref[M_SC1:M_SC1 + 1, :])
        dxn1 = dh1 * gain
        gx_ref[...] = dx2_ref[...] + r1 * (dxn1 - xn1 * jnp.mean(dxn1 * xn1, axis=-1, keepdims=True))
        small_ref[0:1, :] += jnp.sum(dh1, axis=0, keepdims=True)
        small_ref[1:2, :] += jnp.sum(dh1 * xn1, axis=0, keepdims=True)

    tok = pl.BlockSpec((tm, D), lambda i: (i, 0))
    return pl.pallas_call(
        body, name="proj_bwd", grid=(s // tm,),
        in_specs=[pl.BlockSpec((tm, D_IN), lambda i: (i, 0)), tok, tok,
                  pl.BlockSpec((8, D), lambda i: (0, 0)), pl.BlockSpec((16, D), lambda i: (0, 0)),
                  _resident((D, D_IN))],
        out_specs=[tok, pl.BlockSpec((8, D), lambda i: (0, 0))],
        out_shape=[jax.ShapeDtypeStruct((s, D), F32), jax.ShapeDtypeStruct((8, D), F32)],
        compiler_params=_params(("arbitrary",)),
    )(dproj, x, dx2, modr, vecs, w_in)


def _wgrad(a, b, name, square_a=False, dep=None):
    s, ka = a.shape
    n = b.shape[1]
    tka = ka if ka <= 1024 else ka // 2
    tn = n if n <= 1024 else n // 2
    ts = min(TS_WGRAD, s)
    ns = s // ts
    nc = 512
    deps = [] if dep is None else [dep]

    def body(a_ref, b_ref, *refs):
        out_ref, acc_ref = refs[-2:]
        t = pl.program_id(2)

        @pl.when(t == 0)
        def _():
            acc_ref[...] = jnp.zeros_like(acc_ref)

        av = a_ref[...]
        if square_a:
            af = av.astype(F32)
            av = (af * af).astype(BF16)
        for c in range(tn // nc):
            cs = slice(c * nc, (c + 1) * nc)
            acc_ref[:, cs] += _dot_tn(av, b_ref[:, cs])

        @pl.when(t == ns - 1)
        def _():
            out_ref[...] = acc_ref[...].astype(BF16)

    return pl.pallas_call(
        body, name=name, grid=(ka // tka, n // tn, ns),
        in_specs=[pl.BlockSpec((ts, tka), lambda i, j, t: (t, i)),
                  pl.BlockSpec((ts, tn), lambda i, j, t: (t, j))] + [pl.BlockSpec(memory_space=pl.ANY)] * len(deps),
        out_specs=pl.BlockSpec((tka, tn), lambda i, j, t: (i, j)),
        out_shape=jax.ShapeDtypeStruct((ka, n), BF16),
        scratch_shapes=[pltpu.VMEM((tka, tn), F32)],
        compiler_params=_params(("parallel", "parallel", "arbitrary")),
    )(a, b, *deps)


def _window(ref, kind, idx, size):
    start = pl.multiple_of(idx * size, size)
    if kind == 0:
        return ref.at[pl.ds(start, size)]
    if kind == 1:
        return ref.at[:, pl.ds(start, size)]
    return ref.at[:, :, pl.ds(start, size)]


def _mesh_place():
    x, y, c = lax.axis_index("x"), lax.axis_index("y"), lax.axis_index("c")
    return x, y, c, 4 * x + 2 * y + c


def _peer(x, y, c, q):
    px = 1 - x if q & 4 else x
    py = 1 - y if q & 2 else y
    pc = 1 - c if q & 1 else c
    return (px, py, pc), 4 * px + 2 * py + pc


def _all_gather(shards, kinds, name, dep=None):
    n = len(shards)
    deps = [] if dep is None else [dep]
    full_shapes = []
    for sh, kind in zip(shards, kinds):
        dims = list(sh.shape)
        dims[kind] *= N_DEV
        full_shapes.append(jax.ShapeDtypeStruct(tuple(dims), sh.dtype))

    def body(*refs):
        ins, outs = refs[:n], refs[n + len(deps):2 * n + len(deps)]
        send_sems, recv_sems, local_sems = refs[2 * n + len(deps):]
        x, y, c, me = _mesh_place()
        sends, recvs, locals_ = [], [], []
        for k in range(n):
            size = shards[k].shape[kinds[k]]
            mine = _window(outs[k], kinds[k], me, size)
            lc = pltpu.make_async_copy(ins[k], mine, local_sems.at[k])
            lc.start()
            locals_.append(lc)
            for q in range(1, N_DEV):
                peer, peer_idx = _peer(x, y, c, q)
                cp = pltpu.make_async_remote_copy(
                    src_ref=ins[k], dst_ref=mine, send_sem=send_sems.at[k, q], recv_sem=recv_sems.at[k, q],
                    device_id=peer, device_id_type=MESH)
                cp.start()
                sends.append(cp)
                recvs.append(pltpu.make_async_remote_copy(
                    src_ref=ins[k], dst_ref=_window(outs[k], kinds[k], peer_idx, size),
                    send_sem=send_sems.at[k, q], recv_sem=recv_sems.at[k, q],
                    device_id=peer, device_id_type=MESH))
        for cp in recvs:
            cp.wait_recv()
        for cp in sends:
            cp.wait_send()
        for lc in locals_:
            lc.wait()

    any_spec = pl.BlockSpec(memory_space=pl.ANY)
    return pl.pallas_call(
        body, name=name,
        in_specs=[any_spec] * (n + len(deps)), out_specs=[any_spec] * n, out_shape=full_shapes,
        scratch_shapes=[pltpu.SemaphoreType.DMA((n, N_DEV)), pltpu.SemaphoreType.DMA((n, N_DEV)),
                        pltpu.SemaphoreType.DMA((n,))],
    )(*shards, *deps)


_HBM = pl.BlockSpec(memory_space=pltpu.HBM)
_SEM = pl.BlockSpec(memory_space=pltpu.SEMAPHORE)
_EFFECT = pltpu.SideEffectType.DATAFLOW_SIDE_EFFECTING


N_NEAR = 4


def _near(x, y, c):
    out = [((x, y, 1 - c), 4 * x + 2 * y + 1 - c)]
    for j in (1, 2, 3):
        px = 1 - x if j & 2 else x
        py = 1 - y if j & 1 else y
        out.append(((px, py, c), 4 * px + 2 * py + c))
    return out


def _remote(src, dst, send_sems, recv_sems, slot, device):
    return pltpu.make_async_remote_copy(src_ref=src, dst_ref=dst, send_sem=send_sems.at[slot], recv_sem=recv_sems.at[slot],
                                        device_id=device, device_id_type=MESH)


def _split_call(name, arrays, sems_in, n_new_sems, after, emit):
    na, ns, nn = len(arrays), len(sems_in), len(n_new_sems)

    def body(*refs):
        emit(refs[:na], refs[na:na + ns], refs[na + ns + 1:na + ns + 1 + nn])
        refs[-1][...] = jnp.zeros_like(refs[-1])

    outs = pl.pallas_call(
        body, name=name,
        out_shape=(*[pltpu.SemaphoreType.DMA((m,)) for m in n_new_sems],
                   *[pltpu.HBM(a.shape, a.dtype) for a in arrays], jax.ShapeDtypeStruct((8, 128), F32)),
        in_specs=[_HBM] * na + [_SEM] * ns + [pl.BlockSpec(memory_space=pl.ANY)],
        out_specs=(*[_SEM] * nn, *[_HBM] * na, pl.BlockSpec(memory_space=pltpu.VMEM)),
        input_output_aliases={i: nn + i for i in range(na)},
        compiler_params=pltpu.CompilerParams(has_side_effects=_EFFECT),
    )(*[pltpu.with_memory_space_constraint(a, pltpu.HBM) for a in arrays], *sems_in, after)
    return list(outs[:nn]), list(outs[nn:nn + na]), outs[-1]


class _Gather:
    def __init__(self, shards, kinds, after, name):
        self.n, self.kinds, self.name = len(shards), kinds, name
        self.sizes = [s.shape[k] for s, k in zip(shards, kinds)]
        n = self.n
        lands = []
        for s, k in zip(shards, kinds):
            dims = list(s.shape)
            dims[k] *= N_DEV
            lands.append(lax.empty(tuple(dims), s.dtype))

        def emit(arr, _, new):
            x, y, c, me = _mesh_place()
            for k in range(n):
                pltpu.make_async_copy(arr[k], _window(arr[n + k], kinds[k], me, self.sizes[k]), new[2].at[k]).start()
            for k in range(n):
                mine = _window(arr[n + k], kinds[k], me, self.sizes[k])
                for j, (dev, _) in enumerate(_near(x, y, c)):
                    _remote(arr[k], mine, new[0], new[1], k * N_NEAR + j, dev).start()

        self.sems, self.arrays, self.token = _split_call(name + "_start", [*shards, *lands], [],
                                                         [n * N_NEAR, n * N_NEAR, n], after, emit)

    def forward(self, after):
        n, kinds, sizes = self.n, self.kinds, self.sizes

        def emit(arr, old, new):
            x, y, c, _ = _mesh_place()
            near = _near(x, y, c)
            for k in range(n):
                for j in (1, 2, 3):
                    dev, idx = near[j]
                    landed = _window(arr[n + k], kinds[k], idx, sizes[k])
                    _remote(arr[k], landed, old[0], old[1], k * N_NEAR + j, dev).wait_recv()
                    _remote(landed, landed, new[0], new[1], k * N_NEAR + j, near[0][0]).start()

        new, self.arrays, self.token = _split_call(self.name + "_forward", self.arrays, self.sems, [n * N_NEAR] * 2,
                                                   after, emit)
        self.sems = [*self.sems, *new]

    def finish(self, after):
        n, kinds, sizes = self.n, self.kinds, self.sizes

        def emit(arr, old, _):
            x, y, c, me = _mesh_place()
            near = _near(x, y, c)
            other_core = near[0][0]
            for k in range(n):
                win = lambda idx: _window(arr[n + k], kinds[k], idx, sizes[k])
                pltpu.make_async_copy(arr[k], win(me), old[2].at[k]).wait()
                for j, (dev, idx) in enumerate(near):
                    _remote(arr[k], win(me), old[0], old[1], k * N_NEAR + j, dev).wait_send()
                _remote(arr[k], win(near[0][1]), old[0], old[1], k * N_NEAR, other_core).wait_recv()
                for j in (1, 2, 3):
                    idx = near[j][1]
                    _remote(win(idx), win(idx), old[3], old[4], k * N_NEAR + j, other_core).wait_send()
                    _remote(arr[k], win(idx + 1 - 2 * c), old[3], old[4], k * N_NEAR + j, other_core).wait_recv()

        _, arrays, _ = _split_call(self.name + "_finish", self.arrays, self.sems, [], after, emit)
        return arrays[n:]


class _Scatter:
    def __init__(self, partials, kinds, after, name):
        self.n, self.kinds, self.name, self.partials = len(partials), kinds, name, partials
        self.sizes = [p.shape[k] // N_DEV for p, k in zip(partials, kinds)]
        n, sizes = self.n, self.sizes
        self.slot_shapes = []
        for p, k, size in zip(partials, kinds, sizes):
            dims = list(p.shape)
            dims[k] = size
            self.slot_shapes.append((N_NEAR, *dims))
        slots = [lax.empty(sh, p.dtype) for sh, p in zip(self.slot_shapes, partials)]

        def emit(arr, _, new):
            x, y, c, _ = _mesh_place()
            near = _near(x, y, c)
            for k in range(n):
                for j in range(N_NEAR):
                    owner = near[j][1] if j == 0 else near[j][1] + 1 - 2 * c
                    _remote(_window(arr[k], kinds[k], owner, sizes[k]), arr[n + k].at[j], new[0], new[1],
                            k * N_NEAR + j, near[0][0]).start()

        self.sems, self.arrays, self.token = _split_call(name + "_start", [*partials, *slots], [], [n * N_NEAR] * 2,
                                                         after, emit)

    def combine_and_send(self, own4, after):
        n, kinds, sizes = self.n, self.kinds, self.sizes

        def emit_wait(arr, old, _):
            x, y, c, _ = _mesh_place()
            near = _near(x, y, c)
            for k in range(n):
                for j in range(N_NEAR):
                    owner = near[j][1] if j == 0 else near[j][1] + 1 - 2 * c
                    cp = _remote(_window(arr[k], kinds[k], owner, sizes[k]), arr[n + k].at[j], old[0], old[1],
                                 k * N_NEAR + j, near[0][0])
                    cp.wait_send()
                    cp.wait_recv()

        _, arrays, _ = _split_call(self.name + "_landed", self.arrays, self.sems, [], after, emit_wait)
        chip_sums = _chip_sums(arrays[:n], arrays[n:], kinds, sizes, own4, self.name + "_combine")
        arrivals = [lax.empty((N_NEAR - 1, *sh[1:]), p.dtype) for sh, p in zip(self.slot_shapes, self.partials)]

        def emit_send(arr, _, new):
            x, y, c, _ = _mesh_place()
            near = _near(x, y, c)
            for k in range(n):
                for j in (1, 2, 3):
                    _remote(arr[k].at[j], arr[n + k].at[j - 1], new[0], new[1], k * N_NEAR + j, near[j][0]).start()

        self.sems, self.arrays, self.token = _split_call(self.name + "_send", [*chip_sums, *arrivals], [],
                                                         [n * N_NEAR] * 2, own4, emit_send)

    def finish(self, after):
        n = self.n

        def emit(arr, old, _):
            x, y, c, _ = _mesh_place()
            near = _near(x, y, c)
            for k in range(n):
                for j in (1, 2, 3):
                    cp = _remote(arr[k].at[j], arr[n + k].at[j - 1], old[0], old[1], k * N_NEAR + j, near[j][0])
                    cp.wait_send()
                    cp.wait_recv()

        _, arrays, _ = _split_call(self.name + "_finish", self.arrays, self.sems, [], after, emit)
        return arrays[:n], arrays[n:]


def _chip_sums(partials, slots, kinds, sizes, own4, name):
    n = len(partials)

    def body(own_ref, *refs):
        for k in range(n):
            refs[2 * n + k][...] = (refs[k][...].astype(F32) + refs[n + k][...].astype(F32)).astype(BF16)

    in_specs, slot_specs = [], []
    for p, s, kind, size in zip(partials, slots, kinds, sizes):
        block = list(p.shape)
        block[kind] = size
        nd = len(block)
        in_specs.append(pl.BlockSpec(tuple(block), functools.partial(
            lambda j, own, kind, nd: tuple(own[j] if d == kind else 0 for d in range(nd)), kind=kind, nd=nd)))
        slot_specs.append(pl.BlockSpec((None, *block), functools.partial(
            lambda j, own, nd: (j,) + (0,) * nd, nd=nd)))
    return pl.pallas_call(
        body, name=name,
        grid_spec=pltpu.PrefetchScalarGridSpec(num_scalar_prefetch=1, grid=(N_NEAR,),
                                               in_specs=in_specs + slot_specs, out_specs=slot_specs),
        out_shape=[jax.ShapeDtypeStruct(s.shape, s.dtype) for s in slots],
        compiler_params=_params(("arbitrary",)),
    )(own4, *partials, *slots)


def _after(small, token):
    return small + token[0:1, 0:1].astype(small.dtype)


def _silu(c):
    return c * _sigmoid_tail(c)


def _ada_fwd(c_all, w_ada, b_ada_cols):
    def body(c_ref, w_ref, b_ref, out_ref):
        out_ref[...] = jnp.dot(_silu(c_ref[...]), w_ref[...], preferred_element_type=F32,
                               precision=lax.Precision.HIGHEST) + b_ref[...]

    return pl.pallas_call(
        body, name="ada_fwd", out_shape=jax.ShapeDtypeStruct((N_DEV, w_ada.shape[1]), F32),
    )(c_all, w_ada, b_ada_cols)


def _adam(w, g, m, v):
    m = ADAM_B1 * m + (1.0 - ADAM_B1) * g
    v = ADAM_B2 * v + (1.0 - ADAM_B2) * (g * g)
    m_hat = m / (1.0 - ADAM_B1 ** ADAM_STEP)
    v_hat = v / (1.0 - ADAM_B2 ** ADAM_STEP)
    delta = -ADAM_LR * (m_hat / (jnp.sqrt(v_hat) + ADAM_EPS) + ADAM_WD * w)
    return delta, m, v


def _ada_bwd_adam(c_all, dmod_cols, w, m, v):
    def body(c_ref, d_ref, w_ref, m_ref, v_ref, g_ref, delta_ref, nm_ref, nv_ref):
        g = lax.dot_general(_silu(c_ref[...]), d_ref[...], (((0,), (0,)), ((), ())),
                            preferred_element_type=F32, precision=lax.Precision.HIGHEST)
        g_ref[...] = g
        delta_ref[...], nm_ref[...], nv_ref[...] = _adam(w_ref[...], g, m_ref[...], v_ref[...])

    sd = jax.ShapeDtypeStruct(w.shape, F32)
    return pl.pallas_call(body, name="ada_bwd_adam", out_shape=[sd] * 4,
                          compiler_params=pltpu.CompilerParams(vmem_limit_bytes=V7X_VMEM_LIMIT),
                          )(c_all, dmod_cols, w, m, v)


def _adam_group(chip_sums, arrivals, ws, ms, vs, n_tiles, name):
    n = len(ws)

    def body(*refs):
        for k in range(n):
            c_ref, a_ref, w_ref, m_ref, v_ref = (refs[j * n + k] for j in range(5))
            g_ref, delta_ref, nm_ref, nv_ref = (refs[(5 + j) * n + k] for j in range(4))
            g = c_ref[...].astype(F32)
            for j in range(N_NEAR - 1):
                g = g + a_ref[j].astype(F32)
            g_ref[...] = g
            delta_ref[...], nm_ref[...], nv_ref[...] = _adam(w_ref[...], g, m_ref[...], v_ref[...])

    tiles = [(w.shape[0] // n_tiles, w.shape[1]) for w in ws]
    blk = [pl.BlockSpec(t, lambda i: (i, 0)) for t in tiles]
    return pl.pallas_call(
        body, name=name, grid=(n_tiles,),
        in_specs=[pl.BlockSpec((None, *t), lambda i: (0, i, 0)) for t in tiles]
        + [pl.BlockSpec((N_NEAR - 1, *t), lambda i: (0, i, 0)) for t in tiles] + blk * 3,
        out_specs=blk * 4, out_shape=[jax.ShapeDtypeStruct(w.shape, F32) for w in ws] * 4,
        compiler_params=_params(("parallel",)),
    )(*chip_sums, *arrivals, *ws, *ms, *vs)


N_SMALL = 40
N_SMALL_PARAMS = 11


def _pack_vecs(conv_w_full, rows):
    def body(cw_ref, *refs):
        out = refs[-1]
        out[...] = jnp.zeros_like(out)
        out[0:4, :] = cw_ref[0:4, :]
        for r, ref in enumerate(refs[:-1]):
            out[4 + r:5 + r, :] = ref[...]

    return pl.pallas_call(body, name="pack_vecs", out_shape=jax.ShapeDtypeStruct((16, D), F32))(conv_w_full, *rows)


def _small_finish(gathered, mod_all, vecs, ws, ms, vs):
    n = N_SMALL_PARAMS

    def body(g_ref, mod_ref, vec_ref, *refs):
        w_refs, m_refs, v_refs = refs[:n], refs[n:2 * n], refs[2 * n:3 * n]
        outs = refs[3 * n:]
        g1 = vec_ref[V_G1:V_G1 + 1, :]
        g2 = vec_ref[V_G2:V_G2 + 1, :]
        zero = jnp.zeros((1, D), F32)
        dg1, dg2, dgf, loss_lanes = zero, zero, zero, zero
        mixer = jnp.zeros((16, D), F32)
        db_ada = jnp.zeros((6, D), F32)
        for b in range(N_DEV):
            gb = g_ref[b]
            mod = mod_ref[b]
            q1 = gb[33:34]
            q2 = gb[9:10]
            dmod = jnp.concatenate([gb[32:33], q1 * g1, gb[10:11], gb[8:9], q2 * g2, gb[1:2]], axis=0)
            outs[4 * n][b] = dmod
            db_ada = db_ada + dmod
            dg1 = dg1 + q1 * (1.0 + mod[M_SC1:M_SC1 + 1])
            dg2 = dg2 + q2 * (1.0 + mod[M_SC2:M_SC2 + 1])
            dgf = dgf + gb[0:1]
            loss_lanes = loss_lanes + gb[2:3]
            mixer = mixer + gb[16:32]
        d_a_param = mixer[7:8] * _sigmoid_tail(vec_ref[V_A_PARAM:V_A_PARAM + 1, :])
        grads = [dg1, dg2, mixer[4:5], mixer[5:6], mixer[6:7], d_a_param, mixer[8:9], mixer[9:10], dgf,
                 db_ada, mixer[0:4]]
        for k in range(n):
            outs[k][...] = grads[k]
            outs[n + k][...], outs[2 * n + k][...], outs[3 * n + k][...] = _adam(
                w_refs[k][...], grads[k], m_refs[k][...], v_refs[k][...])
        outs[4 * n + 1][...] = jnp.broadcast_to(jnp.sum(loss_lanes, axis=1, keepdims=True), (8, 128))

    shapes = [jax.ShapeDtypeStruct(w.shape, F32) for w in ws]
    return pl.pallas_call(
        body, name="small_finish",
        out_shape=shapes * 4 + [jax.ShapeDtypeStruct((N_DEV, 6, D), F32), jax.ShapeDtypeStruct((8, 128), F32)],
    )(gathered, mod_all, vecs, *ws, *ms, *vs)


def _pad_rows(a, rows):
    return jnp.pad(a, ((0, rows - a.shape[0]), (0, 0)))


def kernel(x, c, norm_mix_g, norm_mlp_g, w_ada, b_ada, w_in, conv_w, conv_b, w_rg_a, b_rg_a, w_rg_x, b_rg_x, a_param, w_branch_a, w_pool, b_pool, pool_scale, w_branch_b, w_out, w_up, w_down, final_g, loss_target, m_norm_mix_g, m_norm_mlp_g, m_w_ada, m_b_ada, m_w_in, m_conv_w, m_conv_b, m_w_rg_a, m_b_rg_a, m_w_rg_x, m_b_rg_x, m_a_param, m_w_branch_a, m_w_pool, m_b_pool, m_pool_scale, m_w_branch_b, m_w_out, m_w_up, m_w_down, m_final_g, v_norm_mix_g, v_norm_mlp_g, v_w_ada, v_b_ada, v_w_in, v_conv_w, v_conv_b, v_w_rg_a, v_b_rg_a, v_w_rg_x, v_b_rg_x, v_a_param, v_w_branch_a, v_w_pool, v_b_pool, v_pool_scale, v_w_branch_b, v_w_out, v_w_up, v_w_down, v_final_g):
    me = 4 * lax.axis_index("x") + 2 * lax.axis_index("y") + lax.axis_index("c")
    s = x.shape[1]
    x2d = x.reshape(s, D)
    target = loss_target.reshape(s, D)
    n_ada = w_ada.shape[2]

    sharded = dict(w_in=(w_in[0], 1), w_up=(w_up[0], 1), w_down=(w_down[0], 0), w_branch_a=(w_branch_a[0], 0),
                   w_branch_b=(w_branch_b[0], 0), w_out=(w_out[0], 0), w_rg_a=(w_rg_a[0], 1), w_rg_x=(w_rg_x[0], 1),
                   w_pool=(w_pool[0], 1))
    kind = {k: v[1] for k, v in sharded.items()}
    shard = {k: v[0].astype(BF16) for k, v in sharded.items()}

    conv_w_full, c_rows = _all_gather([_pad_rows(conv_w[0], 8), _pad_rows(c, 8)], [1, 0], "gather_c")
    c_all = c_rows.reshape(N_DEV, 8, D)[:, 0, :]
    b_ada_cols = lax.dynamic_slice(b_ada, (0, me * n_ada), (1, n_ada))
    mod_part = _ada_fwd(c_all, w_ada[0], b_ada_cols)
    mod_parts, = _all_gather([mod_part], [0], "gather_mod")

    first_names = ["w_in", "w_rg_a", "w_rg_x", "w_pool"]
    branch_names = ["w_branch_a", "w_branch_b", "w_out"]
    mlp_names = ["w_up", "w_down"]

    def gather(group, after, name):
        return _Gather([shard[k] for k in group], [kind[k] for k in group], after, name)

    g_first = gather(first_names, mod_parts, "gather_first")
    g_branch = gather(branch_names, g_first.token, "gather_branch")
    g_mlp = gather(mlp_names, g_branch.token, "gather_mlp")

    mod_all = jnp.transpose(mod_parts.reshape(N_DEV, N_DEV, n_ada), (1, 0, 2)).reshape(N_DEV, 6, D)
    mod_all = jnp.pad(mod_all, ((0, 0), (0, 2), (0, 0)))
    modr = lax.dynamic_index_in_dim(mod_all, me, 0, keepdims=False)
    vecs = _pack_vecs(conv_w_full, [conv_b, b_rg_a, b_rg_x, a_param, b_pool, pool_scale,
                                    norm_mix_g, norm_mlp_g, final_g.reshape(1, D)])
    vecs = _after(vecs, g_mlp.token)
    g_first.forward(vecs)
    wg = dict(zip(first_names, g_first.finish(g_first.token)))

    h1, x_rnn, ga, dga, sa, sb, xr, a, mult, b, ra, ri, p, pooled = _proj_mix_fwd(
        x2d, modr, vecs, wg["w_in"], wg["w_rg_a"], wg["w_rg_x"], wg["w_pool"])
    gates = [a, mult, ra, ri]
    g_branch.forward(h1)
    hr, za = _scan_fwd(a, b, ga)
    g_mlp.forward(za)
    wg.update(zip(branch_names, g_branch.finish(g_mlp.token)))
    ba, bb, merged, o, x2, h2 = _branch_fwd(za, pooled, sa, sb, x2d, modr, vecs,
                                            wg["w_branch_a"], wg["w_branch_b"], wg["w_out"])
    wg.update(zip(mlp_names, g_mlp.finish(h2)))
    ru, dx3, d_dn, small_f = _mlp_fwd(h2, x2, target, modr, vecs, wg["w_up"], wg["w_down"])

    near = _near(lax.axis_index("x"), lax.axis_index("y"), lax.axis_index("c"))
    own4 = jnp.stack([me, near[1][1], near[2][1], near[3][1]]).astype(jnp.int32)

    def scatter(group, partial, after, name):
        return _Scatter([partial[k] for k in group], [kind[k] for k in group], after, name)

    dup, dx2, do, small_m = _mlp_bwd(d_dn, ru, x2, dx3, o, modr, vecs, wg["w_up"], wg["w_down"])
    partial = dict(w_up=_wgrad(h2, dup, "wgrad_up"), w_down=_wgrad(ru, d_dn, "wgrad_down", square_a=True))
    s_mlp = scatter(mlp_names, partial, dx2, "scatter_mlp")

    dba, dbb, dgates, dza, dpooled = _branch_bwd(do, sa, sb, ba, bb, wg["w_branch_a"], wg["w_branch_b"], wg["w_out"],
                                                 dep=s_mlp.token)
    s_mlp.combine_and_send(own4, dza)
    dproj, dw_rg_a, dw_rg_x, dw_pool, small_x = _mix_bwd(dza, dpooled, x_rnn, ga, dga, xr, hr, p, gates, dgates,
                                                         _after(vecs, s_mlp.token),
                                                         wg["w_rg_a"], wg["w_rg_x"], wg["w_pool"])
    partial.update(w_branch_a=_wgrad(za, dba, "wgrad_branch_a"), w_branch_b=_wgrad(pooled, dbb, "wgrad_branch_b"),
                   w_out=_wgrad(merged, do, "wgrad_out"),
                   w_rg_a=dw_rg_a.astype(BF16), w_rg_x=dw_rg_x.astype(BF16), w_pool=dw_pool.astype(BF16))
    mixer_names = ["w_rg_a", "w_rg_x", "w_pool", "w_branch_a", "w_branch_b", "w_out"]
    s_mixer = scatter(mixer_names, partial, s_mlp.token, "scatter_mixer")

    partial["w_in"] = _wgrad(h1, dproj, "wgrad_in", dep=s_mixer.token)
    s_in = scatter(["w_in"], partial, s_mixer.token, "scatter_in")
    s_mixer.combine_and_send(own4, s_in.token)
    s_in.combine_and_send(own4, s_mixer.token)
    grad_x, small_p = _proj_bwd(dproj, x2d, dx2, _after(modr, s_in.token), vecs, wg["w_in"])

    locals_ = dict(w_in=(w_in, m_w_in, v_w_in), w_up=(w_up, m_w_up, v_w_up), w_down=(w_down, m_w_down, v_w_down),
                   w_branch_a=(w_branch_a, m_w_branch_a, v_w_branch_a),
                   w_branch_b=(w_branch_b, m_w_branch_b, v_w_branch_b), w_out=(w_out, m_w_out, v_w_out),
                   w_rg_a=(w_rg_a, m_w_rg_a, v_w_rg_a), w_rg_x=(w_rg_x, m_w_rg_x, v_w_rg_x),
                   w_pool=(w_pool, m_w_pool, v_w_pool))
    res = {}

    def finish(group, exchange, after, n_tiles, name):
        chip_sums, arrivals = exchange.finish(after)
        flat = lambda t: t.reshape(-1, t.shape[-1])
        shapes = [flat(locals_[k][0]).shape for k in group]
        outs = _adam_group([cs.reshape(N_NEAR, *sh) for cs, sh in zip(chip_sums, shapes)],
                           [ar.reshape(N_NEAR - 1, *sh) for ar, sh in zip(arrivals, shapes)],
                           *[[flat(locals_[k][j]) for k in group] for j in range(3)], n_tiles, name)
        for i, k in enumerate(group):
            res[k] = [outs[j * len(group) + i].reshape(locals_[k][0].shape) for j in range(4)]
        return res[group[-1]][0]

    small = jnp.concatenate([small_f, small_m, small_x, small_p], axis=0)
    g_small = _Gather([small], [0], grad_x, "gather_small")
    done = finish(mlp_names, s_mlp, g_small.token, 4, "adam_mlp")
    g_small.forward(done)
    done = finish(mixer_names, s_mixer, g_small.token, 2, "adam_mixer")
    done = finish(["w_in"], s_in, done, 4, "adam_in")
    small_all, = g_small.finish(done)
    small_all = small_all.reshape(N_DEV, N_SMALL, D)

    def embed(cw):
        return lax.dynamic_update_slice(jnp.zeros((4, D), F32), cw[0], (0, me * (D // N_DEV)))

    def smalls(ng, nl, cb, bra, brx, ap, bp, ps, fg, ba_, cw):
        return [ng, nl, cb, bra, brx, ap, bp, ps, fg.reshape(1, D), ba_.reshape(6, D), embed(cw)]

    small_names = ["norm_mix_g", "norm_mlp_g", "conv_b", "b_rg_a", "b_rg_x", "a_param", "b_pool", "pool_scale",
                   "final_g", "b_ada", "conv_w"]
    fin = _small_finish(
        small_all, mod_all, vecs,
        smalls(norm_mix_g, norm_mlp_g, conv_b, b_rg_a, b_rg_x, a_param, b_pool, pool_scale, final_g, b_ada, conv_w),
        smalls(m_norm_mix_g, m_norm_mlp_g, m_conv_b, m_b_rg_a, m_b_rg_x, m_a_param, m_b_pool, m_pool_scale,
               m_final_g, m_b_ada, m_conv_w),
        smalls(v_norm_mix_g, v_norm_mlp_g, v_conv_b, v_b_rg_a, v_b_rg_x, v_a_param, v_b_pool, v_pool_scale,
               v_final_g, v_b_ada, v_conv_w))
    dmod_all, loss_tile = fin[4 * N_SMALL_PARAMS], fin[4 * N_SMALL_PARAMS + 1]
    dmod_cols = lax.dynamic_slice(dmod_all.reshape(N_DEV, 6 * D), (0, me * n_ada), (N_DEV, n_ada))
    res["w_ada"] = [t.reshape(w_ada.shape) for t in _ada_bwd_adam(c_all, dmod_cols, w_ada[0], m_w_ada[0], v_w_ada[0])]

    def final_shape(k, t):
        if k == "final_g":
            return t.reshape(D)
        if k == "b_ada":
            return t.reshape(1, 6 * D)
        if k == "conv_w":
            return lax.dynamic_slice(t, (0, me * (D // N_DEV)), (4, D // N_DEV)).reshape(conv_w.shape)
        return t

    for i, k in enumerate(small_names):
        res[k] = [final_shape(k, fin[which * N_SMALL_PARAMS + i]) for which in range(4)]
    order = ["norm_mix_g", "norm_mlp_g", "w_ada", "b_ada", "w_in", "conv_w", "conv_b", "w_rg_a", "b_rg_a", "w_rg_x",
             "b_rg_x", "a_param", "w_branch_a", "w_pool", "b_pool", "pool_scale", "w_branch_b", "w_out", "w_up",
             "w_down", "final_g"]
    outs = [loss_tile[0, 0], grad_x.reshape(x.shape)]
    for which in range(4):
        for k in order:
            outs.append(res[k][which])
    return tuple(outs)
```

```python
import functools

import jax
import jax.numpy as jnp
from jax import lax
from jax.experimental import pallas as pl
from jax.experimental.pallas import tpu as pltpu

F32 = jnp.float32
BF16 = jnp.bfloat16
MESH = pl.DeviceIdType.MESH

N_DEV = 8
D = 1024
N_GROUPS = 4
GW = D // N_GROUPS
D_IN = 5 * D
D_FF = 4 * D
POOL_WINDOWS = (2, 4, 8, 16)
HALO_X = 8
HALO_U = 16
EPS = 1e-6
C_RG = 8.0
ADAM_LR, ADAM_B1, ADAM_B2, ADAM_EPS, ADAM_WD, ADAM_STEP = 0.001, 0.9, 0.999, 1e-08, 0.01, 10

V7X_VMEM_LIMIT = 56 * 1024 * 1024

V_CONV_W, V_CONV_B, V_B_RG_A, V_B_RG_X, V_A_PARAM, V_B_POOL, V_POOL_SCALE, V_G1, V_G2, V_GF = 0, 4, 5, 6, 7, 8, 9, 10, 11, 12
M_SH1, M_SC1, M_GT1, M_SH2, M_SC2, M_GT2 = 0, 1, 2, 3, 4, 5

TM_PROJ = 512
TM_MIX = 256
TM_BRANCH = 256
TM_MLP = 512
TM_MLP_BWD = 256
TS_WGRAD = 1024


def _params(semantics):
    return pltpu.CompilerParams(dimension_semantics=semantics, vmem_limit_bytes=V7X_VMEM_LIMIT)


def _resident(shape):
    return pl.BlockSpec(shape, lambda *_: (0,) * len(shape), pipeline_mode=pl.Buffered(1))


def _dot(a, b):
    return jnp.dot(a, b, preferred_element_type=F32)


def _dot_nt(a, b):
    return lax.dot_general(a, b, (((1,), (1,)), ((), ())), preferred_element_type=F32)


def _dot_tn(a, b):
    return lax.dot_general(a, b, (((0,), (0,)), ((), ())), preferred_element_type=F32)


def _sigmoid(x):
    return 0.5 * jnp.tanh(0.5 * x) + 0.5


def _sigmoid_tail(x):
    return 1.0 / (1.0 + jnp.exp(-x))


def _gelu_and_grad(x):
    k = 0.7978845608028654
    x2 = x * x
    t = jnp.tanh(k * (x + 0.044715 * x * x2))
    g = 0.5 * x * (1.0 + t)
    dg = 0.5 * (1.0 + t) + 0.5 * x * (1.0 - t * t) * (k * (1.0 + 3.0 * 0.044715 * x2))
    return g, dg


def _softplus(a):
    e = jnp.exp(-jnp.abs(a))
    u = 1.0 + e
    log1p_e = jnp.where(u == 1.0, e, jnp.log(u) * e / jnp.where(u == 1.0, 1.0, u - 1.0))
    return jnp.maximum(a, 0.0) + log1p_e


def _neg_expm1(z):
    series = -(z * (1.0 + z * (0.5 + z * (1.0 / 6.0 + z * (1.0 / 24.0 + z * (1.0 / 120.0))))))
    return jnp.where(z > -0.1, series, 1.0 - jnp.exp(z))


def _shift_down(x, k):
    return pltpu.roll(x, k, 0)


def _shift_up(x, k):
    return pltpu.roll(x, x.shape[0] - k, 0)


def _rglru_gates(xr, w_a, w_x, b_a, b_x, a_param, is_t0):
    xb = xr.astype(BF16)
    ra = _sigmoid(_dot(xb, w_a) + b_a)
    ri = _sigmoid(_dot(xb, w_x) + b_x)
    sp = _softplus(a_param)
    log_a = (-C_RG) * ra * sp
    a = jnp.exp(log_a)
    mult = jnp.where(is_t0, 1.0, jnp.sqrt(_neg_expm1(2.0 * log_a)))
    return ra, ri, sp, a, mult


SUBLANES = 8


LANES = 128


def _scan_strip(a, b, carry, scr, down):
    t = b.shape[0]
    g = t // SUBLANES
    a3 = a.reshape(g, SUBLANES, LANES)
    b3 = b.reshape(g, SUBLANES, LANES)
    sub = lax.broadcasted_iota(jnp.int32, (g, SUBLANES, LANES), 1)
    for k in (1, 2, 4):
        keep = sub >= k if down else sub < SUBLANES - k
        shift = k if down else SUBLANES - k
        b3 = b3 + a3 * jnp.where(keep, pltpu.roll(b3, shift, 1), 0.0)
        a3 = a3 * jnp.where(keep, pltpu.roll(a3, shift, 1), 1.0)
    scr[0] = a3.reshape(t, LANES)
    scr[1] = b3.reshape(t, LANES)
    end_row = SUBLANES - 1 if down else 0
    ag = scr[0, pl.ds(end_row, g, stride=SUBLANES), :]
    bg = scr[1, pl.ds(end_row, g, stride=SUBLANES), :]
    rg = lax.broadcasted_iota(jnp.int32, (g, LANES), 0)
    edge = 0 if down else g - 1
    bg = bg + jnp.where(rg == edge, ag * carry, 0.0)
    k = 1
    while k < g:
        keep = rg >= k if down else rg < g - k
        shift = k if down else g - k
        bg = bg + ag * jnp.where(keep, pltpu.roll(bg, shift, 0), 0.0)
        if 2 * k < g:
            ag = ag * pltpu.roll(ag, shift, 0)
        k *= 2
    entering = jnp.where(rg != edge, pltpu.roll(bg, 1 if down else g - 1, 0), carry)
    for r in range(SUBLANES):
        scr[2, pl.ds(r, g, stride=SUBLANES), :] = entering
    return scr[1] + scr[0] * scr[2], bg[g - 1:g, :]


def _scan_strips(a, b, carry, scr, down):
    outs = [_scan_strip(a[:, c:c + LANES], b[:, c:c + LANES], carry[:, c:c + LANES], scr, down)
            for c in range(0, b.shape[1], LANES)]
    return jnp.concatenate([o[0] for o in outs], axis=1), jnp.concatenate([o[1] for o in outs], axis=1)


def _scan_down(a, b, carry, scr):
    return _scan_strips(a, b, carry, scr, True)


def _scan_up(m, b, carry, scr):
    return _scan_strips(m, b, carry, scr, False)[0]


def _window_mean(sums, window, first_block, head_t):
    scaled = sums * (1.0 / window)
    head = jnp.where(first_block, sums[:HALO_U] / jnp.minimum(head_t, float(window)), scaled[:HALO_U])
    return jnp.concatenate([head, scaled[HALO_U:]], axis=0)


def _conv_taps(x_ext):
    return [_shift_down(x_ext, 3 - j)[HALO_X:] if j < 3 else x_ext[HALO_X:] for j in range(4)]


def _proj_fwd(x, modr, vecs, w_in):
    s = x.shape[0]
    tm = min(TM_PROJ, s)

    def body(x_ref, mod_ref, vec_ref, w_ref, h1_ref, xrnn_ref, u_ref, ga_ref, dga_ref, sa_ref, sb_ref):
        xv = x_ref[...]
        r = lax.rsqrt(jnp.mean(xv * xv, axis=-1, keepdims=True) + EPS)
        gain = vec_ref[V_G1:V_G1 + 1, :] * (1.0 + mod_ref[M_SC1:M_SC1 + 1, :])
        h = (xv * r * gain + mod_ref[M_SH1:M_SH1 + 1, :]).astype(BF16)
        h1_ref[...] = h
        xrnn_ref[...] = _dot(h, w_ref[:, 0:D])
        ga_ref[...], dga_ref[...] = _gelu_and_grad(_dot(h, w_ref[:, D:2 * D]))
        u_ref[...] = _dot(h, w_ref[:, 2 * D:3 * D])
        sa_ref[...] = _sigmoid(_dot(h, w_ref[:, 3 * D:4 * D]))
        sb_ref[...] = _sigmoid(_dot(h, w_ref[:, 4 * D:5 * D]))

    tok = pl.BlockSpec((tm, D), lambda i: (i, 0))
    sd = lambda dt: jax.ShapeDtypeStruct((s, D), dt)
    return pl.pallas_call(
        body, name="proj_fwd", grid=(s // tm,),
        in_specs=[tok, pl.BlockSpec((8, D), lambda i: (0, 0)), pl.BlockSpec((16, D), lambda i: (0, 0)),
                  _resident((D, D_IN))],
        out_specs=[tok] * 7,
        out_shape=[sd(BF16)] + [sd(F32)] * 6,
        compiler_params=_params(("parallel",)),
    )(x, modr, vecs, w_in)


def _mix_fwd(x_rnn, u_pool, ga, vecs, w_rg_a, w_rg_x, w_pool):
    s = x_rnn.shape[0]
    tm = min(TM_MIX, s)
    nb = s // tm

    def body(xh_ref, x_ref, uh_ref, u_ref, ga_ref, vec_ref, wa_ref, wx_ref, wp_ref,
             xr_ref, hr_ref, za_ref, p_ref, pooled_ref, a_ref, mult_ref, ra_ref, ri_ref, carry_ref, scan_scr):
        i = pl.program_id(0)
        first = i == 0

        @pl.when(first)
        def _():
            carry_ref[...] = jnp.zeros_like(carry_ref)

        row = lax.broadcasted_iota(jnp.int32, (tm, GW), 0)
        is_t0 = jnp.logical_and(first, row == 0)
        head_t = (lax.broadcasted_iota(jnp.int32, (HALO_U, GW), 0) + 1).astype(F32)
        for g in range(N_GROUPS):
            cs = slice(g * GW, (g + 1) * GW)
            vec = vec_ref[:, cs]
            xh = jnp.where(first, 0.0, xh_ref[:, cs])
            taps = _conv_taps(jnp.concatenate([xh, x_ref[:, cs]], axis=0))
            xr = vec[V_CONV_B:V_CONV_B + 1]
            for j in range(4):
                xr = xr + vec[V_CONV_W + j:V_CONV_W + j + 1] * taps[j]
            xr_ref[:, cs] = xr
            ra, ri, _, a, mult = _rglru_gates(
                xr, wa_ref[g], wx_ref[g], vec[V_B_RG_A:V_B_RG_A + 1], vec[V_B_RG_X:V_B_RG_X + 1],
                vec[V_A_PARAM:V_A_PARAM + 1], is_t0)
            a_ref[:, cs] = a
            mult_ref[:, cs] = mult
            ra_ref[:, cs] = ra.astype(BF16)
            ri_ref[:, cs] = ri.astype(BF16)
            h, last = _scan_down(a, xr * ri * mult, carry_ref[0:1, cs], scan_scr)
            hr_ref[:, cs] = h
            carry_ref[0:1, cs] = last
            za_ref[:, cs] = (ga_ref[:, cs] * h).astype(BF16)
            uh = jnp.where(first, 0.0, uh_ref[:, cs])
            sm = jnp.concatenate([uh, u_ref[:, cs]], axis=0)
            k = 1
            while k < POOL_WINDOWS[g]:
                sm = sm + _shift_down(sm, k)
                k *= 2
            mean = _window_mean(sm[HALO_U:], POOL_WINDOWS[g], first, head_t)
            p = (mean - u_ref[:, cs]).astype(BF16)
            p_ref[:, cs] = p
            pb = _dot(p, wp_ref[g]) + vec[V_B_POOL:V_B_POOL + 1]
            pooled_ref[:, cs] = (pb * vec[V_POOL_SCALE:V_POOL_SCALE + 1]).astype(BF16)

    tok = pl.BlockSpec((tm, D), lambda i: (i, 0))
    halo = lambda rows: pl.BlockSpec((rows, D), lambda i: (jnp.maximum(i * (tm // rows) - 1, 0), 0))
    wspec = pl.BlockSpec((N_GROUPS, GW, GW), lambda i: (0, 0, 0))
    sd = lambda dt: jax.ShapeDtypeStruct((s, D), dt)
    return pl.pallas_call(
        body, name="mix_fwd", grid=(nb,),
        in_specs=[halo(HALO_X), tok, halo(HALO_U), tok, tok, pl.BlockSpec((16, D), lambda i: (0, 0)),
                  wspec, wspec, wspec],
        out_specs=[tok] * 9,
        out_shape=[sd(F32), sd(F32), sd(BF16), sd(BF16), sd(BF16), sd(F32), sd(F32), sd(BF16), sd(BF16)],
        scratch_shapes=[pltpu.VMEM((8, D), F32), pltpu.VMEM((3, tm, LANES), F32)],
        compiler_params=_params(("arbitrary",)),
    )(x_rnn, x_rnn, u_pool, u_pool, ga, vecs, w_rg_a, w_rg_x, w_pool)


def _branch_fwd(za, pooled, sa, sb, x, modr, vecs, w_a, w_b, w_out):
    s = x.shape[0]
    tm = min(TM_BRANCH, s)

    def body(za_ref, pooled_ref, sa_ref, sb_ref, x_ref, mod_ref, vec_ref, wa_ref, wb_ref, wo_ref,
             ba_ref, bb_ref, merged_ref, o_ref, x2_ref, h2_ref):
        ba = _dot(za_ref[...], wa_ref[...])
        bb = _dot(pooled_ref[...], wb_ref[...])
        ba_ref[...] = ba.astype(BF16)
        bb_ref[...] = bb.astype(BF16)
        merged = (sa_ref[...] * ba + sb_ref[...] * bb).astype(BF16)
        merged_ref[...] = merged
        o = _dot(merged, wo_ref[...])
        o_ref[...] = o.astype(BF16)
        x2 = x_ref[...] + mod_ref[M_GT1:M_GT1 + 1, :] * o
        x2_ref[...] = x2
        r = lax.rsqrt(jnp.mean(x2 * x2, axis=-1, keepdims=True) + EPS)
        gain = vec_ref[V_G2:V_G2 + 1, :] * (1.0 + mod_ref[M_SC2:M_SC2 + 1, :])
        h2_ref[...] = (x2 * r * gain + mod_ref[M_SH2:M_SH2 + 1, :]).astype(BF16)

    tok = pl.BlockSpec((tm, D), lambda i: (i, 0))
    wspec = pl.BlockSpec((D, D), lambda i: (0, 0))
    sd = lambda dt: jax.ShapeDtypeStruct((s, D), dt)
    return pl.pallas_call(
        body, name="branch_fwd", grid=(s // tm,),
        in_specs=[tok, tok, tok, tok,
                  tok, pl.BlockSpec((8, D), lambda i: (0, 0)), pl.BlockSpec((16, D), lambda i: (0, 0)),
                  wspec, wspec, wspec],
        out_specs=[tok] * 6,
        out_shape=[sd(BF16), sd(BF16), sd(BF16), sd(BF16), sd(F32), sd(BF16)],
        compiler_params=_params(("parallel",)),
    )(za, pooled, sa, sb, x, modr, vecs, w_a, w_b, w_out)


def _mlp_fwd(h2, x2, target, modr, vecs, w_up, w_down):
    s = x2.shape[0]
    tm = min(TM_MLP, s)

    def body(h2_ref, x2_ref, tgt_ref, mod_ref, vec_ref, wu_ref, wd_ref,
             ru_ref, dx3_ref, ddn_ref, small_ref):
        @pl.when(pl.program_id(0) == 0)
        def _():
            small_ref[...] = jnp.zeros_like(small_ref)

        h2 = h2_ref[...]
        dn = None
        for c in range(D_FF // D):
            cs = slice(c * D, (c + 1) * D)
            ru = jnp.maximum(_dot(h2, wu_ref[:, cs]), 0.0)
            ru_ref[:, cs] = ru.astype(BF16)
            part = _dot((ru * ru).astype(BF16), wd_ref[cs, :])
            dn = part if dn is None else dn + part
        gt2 = mod_ref[M_GT2:M_GT2 + 1, :]
        gf = vec_ref[V_GF:V_GF + 1, :]
        x3 = x2_ref[...] + gt2 * dn
        r3 = lax.rsqrt(jnp.mean(x3 * x3, axis=-1, keepdims=True) + EPS)
        n3 = x3 * r3
        err = n3 * gf - tgt_ref[...]
        dy = err * (1.0 / D)
        dn3 = dy * gf
        dx3 = r3 * (dn3 - n3 * jnp.mean(dn3 * n3, axis=-1, keepdims=True))
        dx3_ref[...] = dx3
        ddn_ref[...] = (dx3 * gt2).astype(BF16)
        small_ref[0:1, :] += jnp.sum(dy * n3, axis=0, keepdims=True)
        small_ref[1:2, :] += jnp.sum(dx3 * dn, axis=0, keepdims=True)
        small_ref[2:3, :] += (0.5 / D) * jnp.sum(err * err, axis=0, keepdims=True)

    tok = pl.BlockSpec((tm, D), lambda i: (i, 0))
    return pl.pallas_call(
        body, name="mlp_fwd", grid=(s // tm,),
        in_specs=[tok, tok, tok,
                  pl.BlockSpec((8, D), lambda i: (0, 0)), pl.BlockSpec((16, D), lambda i: (0, 0)),
                  _resident((D, D_FF)), _resident((D_FF, D))],
        out_specs=[pl.BlockSpec((tm, D_FF), lambda i: (i, 0)), tok, tok,
                   pl.BlockSpec((8, D), lambda i: (0, 0))],
        out_shape=[jax.ShapeDtypeStruct((s, D_FF), BF16), jax.ShapeDtypeStruct((s, D), F32),
                   jax.ShapeDtypeStruct((s, D), BF16), jax.ShapeDtypeStruct((8, D), F32)],
        compiler_params=_params(("arbitrary",)),
    )(h2, x2, target, modr, vecs, w_up, w_down)


def _mlp_bwd(d_dn, ru, x2, dx3, o, modr, vecs, w_up, w_down):
    s = x2.shape[0]
    tm = min(TM_MLP_BWD, s)

    def body(ddn_ref, ru_ref, x2_ref, dx3_ref, o_ref, mod_ref, vec_ref, wu_ref, wd_ref,
             dup_ref, dx2_ref, do_ref, small_ref):
        @pl.when(pl.program_id(0) == 0)
        def _():
            small_ref[...] = jnp.zeros_like(small_ref)

        ddn = ddn_ref[...]
        dh2 = None
        for c in range(D_FF // D):
            cs = slice(c * D, (c + 1) * D)
            dff = _dot_nt(ddn, wd_ref[cs, :])
            dup = (dff * (2.0 * ru_ref[:, cs].astype(F32))).astype(BF16)
            dup_ref[:, cs] = dup
            part = _dot_nt(dup, wu_ref[:, cs])
            dh2 = part if dh2 is None else dh2 + part
        x2 = x2_ref[...]
        r2 = lax.rsqrt(jnp.mean(x2 * x2, axis=-1, keepdims=True) + EPS)
        xn2 = x2 * r2
        gain = vec_ref[V_G2:V_G2 + 1, :] * (1.0 + mod_ref[M_SC2:M_SC2 + 1, :])
        dxn2 = dh2 * gain
        dx2 = dx3_ref[...] + r2 * (dxn2 - xn2 * jnp.mean(dxn2 * xn2, axis=-1, keepdims=True))
        dx2_ref[...] = dx2
        do_ref[...] = (dx2 * mod_ref[M_GT1:M_GT1 + 1, :]).astype(BF16)
        small_ref[0:1, :] += jnp.sum(dh2, axis=0, keepdims=True)
        small_ref[1:2, :] += jnp.sum(dh2 * xn2, axis=0, keepdims=True)
        small_ref[2:3, :] += jnp.sum(dx2 * o_ref[...].astype(F32), axis=0, keepdims=True)

    tok = pl.BlockSpec((tm, D), lambda i: (i, 0))
    wide = pl.BlockSpec((tm, D_FF), lambda i: (i, 0))
    return pl.pallas_call(
        body, name="mlp_bwd", grid=(s // tm,),
        in_specs=[tok, wide, tok, tok, tok,
                  pl.BlockSpec((8, D), lambda i: (0, 0)), pl.BlockSpec((16, D), lambda i: (0, 0)),
                  _resident((D, D_FF)), _resident((D_FF, D))],
        out_specs=[wide, tok, tok, pl.BlockSpec((8, D), lambda i: (0, 0))],
        out_shape=[jax.ShapeDtypeStruct((s, D_FF), BF16), jax.ShapeDtypeStruct((s, D), F32),
                   jax.ShapeDtypeStruct((s, D), BF16), jax.ShapeDtypeStruct((8, D), F32)],
        compiler_params=_params(("arbitrary",)),
    )(d_dn, ru, x2, dx3, o, modr, vecs, w_up, w_down)


def _branch_bwd(do, sa, sb, ba, bb, w_a, w_b, w_out, dep):
    s = do.shape[0]
    tm = min(TM_BRANCH, s)

    def body(do_ref, sa_ref, sb_ref, ba_ref, bb_ref, wa_ref, wb_ref, wo_ref, dep_ref,
             dba_ref, dbb_ref, dg_ref, dza_ref, dpooled_ref):
        dmerged = _dot_nt(do_ref[...], wo_ref[...])
        sa = sa_ref[...]
        sb = sb_ref[...]
        dba = (dmerged * sa).astype(BF16)
        dbb = (dmerged * sb).astype(BF16)
        dba_ref[...] = dba
        dbb_ref[...] = dbb
        dg_ref[:, :D] = (dmerged * ba_ref[...].astype(F32) * sa * (1.0 - sa)).astype(BF16)
        dg_ref[:, D:] = (dmerged * bb_ref[...].astype(F32) * sb * (1.0 - sb)).astype(BF16)
        dza_ref[...] = _dot_nt(dba, wa_ref[...])
        dpooled_ref[...] = _dot_nt(dbb, wb_ref[...])

    tok = pl.BlockSpec((tm, D), lambda i: (i, 0))
    wspec = pl.BlockSpec((D, D), lambda i: (0, 0))
    sd = lambda dt: jax.ShapeDtypeStruct((s, D), dt)
    return pl.pallas_call(
        body, name="branch_bwd", grid=(s // tm,),
        in_specs=[tok, tok, tok, tok, tok, wspec, wspec, wspec, pl.BlockSpec(memory_space=pl.ANY)],
        out_specs=[tok, tok, pl.BlockSpec((tm, 2 * D), lambda i: (i, 0)), tok, tok],
        out_shape=[sd(BF16), sd(BF16), jax.ShapeDtypeStruct((s, 2 * D), BF16), sd(F32), sd(F32)],
        compiler_params=_params(("parallel",)),
    )(do, sa, sb, ba, bb, w_a, w_b, w_out, dep)


def _mix_bwd(dza, dpooled, x_rnn, ga, dga, xr, hr, p, gates, dgates, vecs, w_rg_a, w_rg_x, w_pool):
    s = xr.shape[0]
    tm = min(TM_MIX, s)
    nb = s // tm

    def body(dza_ref, dpooled_ref, xh_ref, x_ref, ga_ref, dga_ref, xr_ref, hh_ref, hr_ref, p_ref,
             a_ref, mult_ref, ra_ref, ri_ref, dg_ref, vec_ref, wa_ref, wx_ref, wp_ref,
             dproj_ref, dwa_ref, dwx_ref, dwp_ref, small_ref,
             scan_carry, dxr_carry, q_carry, scan_scr):
        i = pl.program_id(0)
        bi = nb - 1 - i
        first_t = bi == 0

        @pl.when(i == 0)
        def _():
            scan_carry[...] = jnp.zeros_like(scan_carry)
            dxr_carry[...] = jnp.zeros_like(dxr_carry)
            q_carry[...] = jnp.zeros_like(q_carry)
            dwa_ref[...] = jnp.zeros_like(dwa_ref)
            dwx_ref[...] = jnp.zeros_like(dwx_ref)
            dwp_ref[...] = jnp.zeros_like(dwp_ref)
            small_ref[...] = jnp.zeros_like(small_ref)

        row = lax.broadcasted_iota(jnp.int32, (tm, GW), 0)
        is_t0 = jnp.logical_and(first_t, row == 0)
        head_t = (lax.broadcasted_iota(jnp.int32, (HALO_U, GW), 0) + 1).astype(F32)
        colsum = lambda v: jnp.sum(v, axis=0, keepdims=True)
        for g in range(N_GROUPS):
            cs = slice(g * GW, (g + 1) * GW)
            vec = vec_ref[:, cs]
            xr = xr_ref[:, cs]
            hr = hr_ref[:, cs]
            dza = dza_ref[:, cs]
            dproj_ref[:, D + g * GW:D + (g + 1) * GW] = (dza * hr * dga_ref[:, cs]).astype(BF16)
            dhr = dza * ga_ref[:, cs]
            a = a_ref[:, cs]
            mult = mult_ref[:, cs]
            ra = ra_ref[:, cs].astype(F32)
            ri = ri_ref[:, cs].astype(F32)
            sp = _softplus(vec[V_A_PARAM:V_A_PARAM + 1])
            m = jnp.where(row == tm - 1, 1.0, _shift_up(a, 1))
            gsum = _scan_up(m, dhr, scan_carry[0:1, cs], scan_scr)
            scan_carry[0:1, cs] = a[0:1, :] * gsum[0:1, :]
            hh = jnp.where(first_t, 0.0, hh_ref[:, cs])
            hprev = _shift_down(jnp.concatenate([hh, hr], axis=0), 1)[8:]
            da = gsum * hprev
            dmult = jnp.where(is_t0, 0.0, gsum * xr * ri)
            dlog_a = da * a - dmult * a * a / mult
            dri = gsum * xr * mult
            dxr = gsum * ri * mult
            small_ref[7:8, cs] += colsum((-C_RG) * ra * dlog_a)
            dpa = (((-C_RG) * sp) * dlog_a * ra * (1.0 - ra))
            dpx = dri * ri * (1.0 - ri)
            small_ref[5:6, cs] += colsum(dpa)
            small_ref[6:7, cs] += colsum(dpx)
            dpa = dpa.astype(BF16)
            dpx = dpx.astype(BF16)
            xrb = xr.astype(BF16)
            dwa_ref[g] += _dot_tn(xrb, dpa)
            dwx_ref[g] += _dot_tn(xrb, dpx)
            dxr = dxr + _dot_nt(dpa, wa_ref[g]) + _dot_nt(dpx, wx_ref[g])
            small_ref[4:5, cs] += colsum(dxr)
            xh = jnp.where(first_t, 0.0, xh_ref[:, cs])
            taps = _conv_taps(jnp.concatenate([xh, x_ref[:, cs]], axis=0))
            dxr_ext = jnp.concatenate([dxr, dxr_carry[:, cs]], axis=0)
            dx = vec[V_CONV_W + 3:V_CONV_W + 4] * dxr
            for j in range(4):
                small_ref[j:j + 1, cs] += colsum(dxr * taps[j])
                if j < 3:
                    dx = dx + vec[V_CONV_W + j:V_CONV_W + j + 1] * _shift_up(dxr_ext, 3 - j)[:tm]
            dxr_carry[:, cs] = dxr[0:8, :]
            dproj_ref[:, cs] = dx.astype(BF16)
            pg = p_ref[:, cs]
            dpooled = dpooled_ref[:, cs]
            pb = _dot(pg, wp_ref[g]) + vec[V_B_POOL:V_B_POOL + 1]
            small_ref[9:10, cs] += colsum(dpooled * pb)
            dpb = dpooled * vec[V_POOL_SCALE:V_POOL_SCALE + 1]
            small_ref[8:9, cs] += colsum(dpb)
            dpbb = dpb.astype(BF16)
            dwp_ref[g] += _dot_tn(pg, dpbb)
            dp = _dot_nt(dpbb, wp_ref[g])
            q = _window_mean(dp, POOL_WINDOWS[g], first_t, head_t)
            sm = jnp.concatenate([q, q_carry[:, cs]], axis=0)
            k = 1
            while k < POOL_WINDOWS[g]:
                sm = sm + _shift_up(sm, k)
                k *= 2
            q_carry[:, cs] = q[0:HALO_U, :]
            dproj_ref[:, 2 * D + g * GW:2 * D + (g + 1) * GW] = (sm[:tm] - dp).astype(BF16)
        dproj_ref[:, 3 * D:] = dg_ref[...]

    rev = lambda i: nb - 1 - i
    tok = pl.BlockSpec((tm, D), lambda i: (rev(i), 0))
    halo8 = lambda k: pl.BlockSpec((8, D), lambda i: (jnp.maximum(rev(i) * (tm // 8) - 1, 0), k))
    wspec = pl.BlockSpec((N_GROUPS, GW, GW), lambda i: (0, 0, 0))
    wshape = jax.ShapeDtypeStruct((N_GROUPS, GW, GW), F32)
    return pl.pallas_call(
        body, name="mix_bwd", grid=(nb,),
        in_specs=[tok, tok, halo8(0), tok, tok, tok, tok, halo8(0), tok, tok, tok, tok, tok, tok,
                  pl.BlockSpec((tm, 2 * D), lambda i: (rev(i), 0)),
                  pl.BlockSpec((16, D), lambda i: (0, 0)), wspec, wspec, wspec],
        out_specs=[pl.BlockSpec((tm, D_IN), lambda i: (rev(i), 0)), wspec, wspec, wspec,
                   pl.BlockSpec((16, D), lambda i: (0, 0))],
        out_shape=[jax.ShapeDtypeStruct((s, D_IN), BF16), wshape, wshape, wshape,
                   jax.ShapeDtypeStruct((16, D), F32)],
        scratch_shapes=[pltpu.VMEM((8, D), F32), pltpu.VMEM((8, D), F32), pltpu.VMEM((HALO_U, D), F32),
                        pltpu.VMEM((3, tm, LANES), F32)],
        compiler_params=_params(("arbitrary",)),
    )(dza, dpooled, x_rnn, x_rnn, ga, dga, xr, hr, hr, p, *gates, dgates, vecs, w_rg_a, w_rg_x, w_pool)


def _proj_bwd(dproj, x, dx2, modr, vecs, w_in):
    s = x.shape[0]
    tm = min(TM_PROJ, s)

    def body(dp_ref, x_ref, dx2_ref, mod_ref, vec_ref, w_ref, gx_ref, small_ref):
        @pl.when(pl.program_id(0) == 0)
        def _():
            small_ref[...] = jnp.zeros_like(small_ref)

        dh1 = None
        for c in range(D_IN // D):
            cs = slice(c * D, (c + 1) * D)
            part = _dot_nt(dp_ref[:, cs], w_ref[:, cs])
            dh1 = part if dh1 is None else dh1 + part
        xv = x_ref[...]
        r1 = lax.rsqrt(jnp.mean(xv * xv, axis=-1, keepdims=True) + EPS)
        xn1 = xv * r1
        gain = vec_ref[V_G1:V_G1 + 1, :] * (1.0 + mod_ref[M_SC1:M_SC1 + 1, :])
        dxn1 = dh1 * gain
        gx_ref[...] = dx2_ref[...] + r1 * (dxn1 - xn1 * jnp.mean(dxn1 * xn1, axis=-1, keepdims=True))
        small_ref[0:1, :] += jnp.sum(dh1, axis=0, keepdims=True)
        small_ref[1:2, :] += jnp.sum(dh1 * xn1, axis=0, keepdims=True)

    tok = pl.BlockSpec((tm, D), lambda i: (i, 0))
    return pl.pallas_call(
        body, name="proj_bwd", grid=(s // tm,),
        in_specs=[pl.BlockSpec((tm, D_IN), lambda i: (i, 0)), tok, tok,
                  pl.BlockSpec((8, D), lambda i: (0, 0)), pl.BlockSpec((16, D), lambda i: (0, 0)),
                  _resident((D, D_IN))],
        out_specs=[tok, pl.BlockSpec((8, D), lambda i: (0, 0))],
        out_shape=[jax.ShapeDtypeStruct((s, D), F32), jax.ShapeDtypeStruct((8, D), F32)],
        compiler_params=_params(("arbitrary",)),
    )(dproj, x, dx2, modr, vecs, w_in)


def _wgrad(a, b, name, square_a=False, dep=None):
    s, ka = a.shape
    n = b.shape[1]
    tka = ka if ka <= 1024 else ka // 2
    tn = n if n <= 1024 else n // 2
    ts = min(TS_WGRAD, s)
    ns = s // ts
    nc = 512
    deps = [] if dep is None else [dep]

    def body(a_ref, b_ref, *refs):
        out_ref, acc_ref = refs[-2:]
        t = pl.program_id(2)

        @pl.when(t == 0)
        def _():
            acc_ref[...] = jnp.zeros_like(acc_ref)

        av = a_ref[...]
        if square_a:
            af = av.astype(F32)
            av = (af * af).astype(BF16)
        for c in range(tn // nc):
            cs = slice(c * nc, (c + 1) * nc)
            acc_ref[:, cs] += _dot_tn(av, b_ref[:, cs])

        @pl.when(t == ns - 1)
        def _():
            out_ref[...] = acc_ref[...].astype(BF16)

    return pl.pallas_call(
        body, name=name, grid=(ka // tka, n // tn, ns),
        in_specs=[pl.BlockSpec((ts, tka), lambda i, j, t: (t, i)),
                  pl.BlockSpec((ts, tn), lambda i, j, t: (t, j))] + [pl.BlockSpec(memory_space=pl.ANY)] * len(deps),
        out_specs=pl.BlockSpec((tka, tn), lambda i, j, t: (i, j)),
        out_shape=jax.ShapeDtypeStruct((ka, n), BF16),
        scratch_shapes=[pltpu.VMEM((tka, tn), F32)],
        compiler_params=_params(("parallel", "parallel", "arbitrary")),
    )(a, b, *deps)


def _window(ref, kind, idx, size):
    start = pl.multiple_of(idx * size, size)
    if kind == 0:
        return ref.at[pl.ds(start, size)]
    if kind == 1:
        return ref.at[:, pl.ds(start, size)]
    return ref.at[:, :, pl.ds(start, size)]


def _mesh_place():
    x, y, c = lax.axis_index("x"), lax.axis_index("y"), lax.axis_index("c")
    return x, y, c, 4 * x + 2 * y + c


def _peer(x, y, c, q):
    px = 1 - x if q & 4 else x
    py = 1 - y if q & 2 else y
    pc = 1 - c if q & 1 else c
    return (px, py, pc), 4 * px + 2 * py + pc


def _all_gather(shards, kinds, name, dep=None):
    n = len(shards)
    deps = [] if dep is None else [dep]
    full_shapes = []
    for sh, kind in zip(shards, kinds):
        dims = list(sh.shape)
        dims[kind] *= N_DEV
        full_shapes.append(jax.ShapeDtypeStruct(tuple(dims), sh.dtype))

    def body(*refs):
        ins, outs = refs[:n], refs[n + len(deps):2 * n + len(deps)]
        send_sems, recv_sems, local_sems = refs[2 * n + len(deps):]
        x, y, c, me = _mesh_place()
        sends, recvs, locals_ = [], [], []
        for k in range(n):
            size = shards[k].shape[kinds[k]]
            mine = _window(outs[k], kinds[k], me, size)
            lc = pltpu.make_async_copy(ins[k], mine, local_sems.at[k])
            lc.start()
            locals_.append(lc)
            for q in range(1, N_DEV):
                peer, peer_idx = _peer(x, y, c, q)
                cp = pltpu.make_async_remote_copy(
                    src_ref=ins[k], dst_ref=mine, send_sem=send_sems.at[k, q], recv_sem=recv_sems.at[k, q],
                    device_id=peer, device_id_type=MESH)
                cp.start()
                sends.append(cp)
                recvs.append(pltpu.make_async_remote_copy(
                    src_ref=ins[k], dst_ref=_window(outs[k], kinds[k], peer_idx, size),
                    send_sem=send_sems.at[k, q], recv_sem=recv_sems.at[k, q],
                    device_id=peer, device_id_type=MESH))
        for cp in recvs:
            cp.wait_recv()
        for cp in sends:
            cp.wait_send()
        for lc in locals_:
            lc.wait()

    any_spec = pl.BlockSpec(memory_space=pl.ANY)
    return pl.pallas_call(
        body, name=name,
        in_specs=[any_spec] * (n + len(deps)), out_specs=[any_spec] * n, out_shape=full_shapes,
        scratch_shapes=[pltpu.SemaphoreType.DMA((n, N_DEV)), pltpu.SemaphoreType.DMA((n, N_DEV)),
                        pltpu.SemaphoreType.DMA((n,))],
    )(*shards, *deps)


_HBM = pl.BlockSpec(memory_space=pltpu.HBM)
_SEM = pl.BlockSpec(memory_space=pltpu.SEMAPHORE)
_EFFECT = pltpu.SideEffectType.DATAFLOW_SIDE_EFFECTING


N_NEAR = 4


def _near(x, y, c):
    out = [((x, y, 1 - c), 4 * x + 2 * y + 1 - c)]
    for j in (1, 2, 3):
        px = 1 - x if j & 2 else x
        py = 1 - y if j & 1 else y
        out.append(((px, py, c), 4 * px + 2 * py + c))
    return out


def _remote(src, dst, send_sems, recv_sems, slot, device):
    return pltpu.make_async_remote_copy(src_ref=src, dst_ref=dst, send_sem=send_sems.at[slot], recv_sem=recv_sems.at[slot],
                                        device_id=device, device_id_type=MESH)


def _split_call(name, arrays, sems_in, n_new_sems, after, emit):
    na, ns, nn = len(arrays), len(sems_in), len(n_new_sems)

    def body(*refs):
        emit(refs[:na], refs[na:na + ns], refs[na + ns + 1:na + ns + 1 + nn])
        refs[-1][...] = jnp.zeros_like(refs[-1])

    outs = pl.pallas_call(
        body, name=name,
        out_shape=(*[pltpu.SemaphoreType.DMA((m,)) for m in n_new_sems],
                   *[pltpu.HBM(a.shape, a.dtype) for a in arrays], jax.ShapeDtypeStruct((8, 128), F32)),
        in_specs=[_HBM] * na + [_SEM] * ns + [pl.BlockSpec(memory_space=pl.ANY)],
        out_specs=(*[_SEM] * nn, *[_HBM] * na, pl.BlockSpec(memory_space=pltpu.VMEM)),
        input_output_aliases={i: nn + i for i in range(na)},
        compiler_params=pltpu.CompilerParams(has_side_effects=_EFFECT),
    )(*[pltpu.with_memory_space_constraint(a, pltpu.HBM) for a in arrays], *sems_in, after)
    return list(outs[:nn]), list(outs[nn:nn + na]), outs[-1]


class _Gather:
    def __init__(self, shards, kinds, after, name):
        self.n, self.kinds, self.name = len(shards), kinds, name
        self.sizes = [s.shape[k] for s, k in zip(shards, kinds)]
        n = self.n
        lands = []
        for s, k in zip(shards, kinds):
            dims = list(s.shape)
            dims[k] *= N_DEV
            lands.append(lax.empty(tuple(dims), s.dtype))

        def emit(arr, _, new):
            x, y, c, me = _mesh_place()
            for k in range(n):
                pltpu.make_async_copy(arr[k], _window(arr[n + k], kinds[k], me, self.sizes[k]), new[2].at[k]).start()
            for k in range(n):
                mine = _window(arr[n + k], kinds[k], me, self.sizes[k])
                for j, (dev, _) in enumerate(_near(x, y, c)):
                    _remote(arr[k], mine, new[0], new[1], k * N_NEAR + j, dev).start()

        self.sems, self.arrays, self.token = _split_call(name + "_start", [*shards, *lands], [],
                                                         [n * N_NEAR, n * N_NEAR, n], after, emit)

    def forward(self, after):
        n, kinds, sizes = self.n, self.kinds, self.sizes

        def emit(arr, old, new):
            x, y, c, _ = _mesh_place()
            near = _near(x, y, c)
            for k in range(n):
                for j in (1, 2, 3):
                    dev, idx = near[j]
                    landed = _window(arr[n + k], kinds[k], idx, sizes[k])
                    _remote(arr[k], landed, old[0], old[1], k * N_NEAR + j, dev).wait_recv()
                    _remote(landed, landed, new[0], new[1], k * N_NEAR + j, near[0][0]).start()

        new, self.arrays, self.token = _split_call(self.name + "_forward", self.arrays, self.sems, [n * N_NEAR] * 2,
                                                   after, emit)
        self.sems = [*self.sems, *new]

    def finish(self, after):
        n, kinds, sizes = self.n, self.kinds, self.sizes

        def emit(arr, old, _):
            x, y, c, me = _mesh_place()
            near = _near(x, y, c)
            other_core = near[0][0]
            for k in range(n):
                win = lambda idx: _window(arr[n + k], kinds[k], idx, sizes[k])
                pltpu.make_async_copy(arr[k], win(me), old[2].at[k]).wait()
                for j, (dev, idx) in enumerate(near):
                    _remote(arr[k], win(me), old[0], old[1], k * N_NEAR + j, dev).wait_send()
                _remote(arr[k], win(near[0][1]), old[0], old[1], k * N_NEAR, other_core).wait_recv()
                for j in (1, 2, 3):
                    idx = near[j][1]
                    _remote(win(idx), win(idx), old[3], old[4], k * N_NEAR + j, other_core).wait_send()
                    _remote(arr[k], win(idx + 1 - 2 * c), old[3], old[4], k * N_NEAR + j, other_core).wait_recv()

        _, arrays, _ = _split_call(self.name + "_finish", self.arrays, self.sems, [], after, emit)
        return arrays[n:]


class _Scatter:
    def __init__(self, partials, kinds, after, name):
        self.n, self.kinds, self.name, self.partials = len(partials), kinds, name, partials
        self.sizes = [p.shape[k] // N_DEV for p, k in zip(partials, kinds)]
        n, sizes = self.n, self.sizes
        self.slot_shapes = []
        for p, k, size in zip(partials, kinds, sizes):
            dims = list(p.shape)
            dims[k] = size
            self.slot_shapes.append((N_NEAR, *dims))
        slots = [lax.empty(sh, p.dtype) for sh, p in zip(self.slot_shapes, partials)]

        def emit(arr, _, new):
            x, y, c, _ = _mesh_place()
            near = _near(x, y, c)
            for k in range(n):
                for j in range(N_NEAR):
                    owner = near[j][1] if j == 0 else near[j][1] + 1 - 2 * c
                    _remote(_window(arr[k], kinds[k], owner, sizes[k]), arr[n + k].at[j], new[0], new[1],
                            k * N_NEAR + j, near[0][0]).start()

        self.sems, self.arrays, self.token = _split_call(name + "_start", [*partials, *slots], [], [n * N_NEAR] * 2,
                                                         after, emit)

    def combine_and_send(self, own4, after):
        n, kinds, sizes = self.n, self.kinds, self.sizes

        def emit_wait(arr, old, _):
            x, y, c, _ = _mesh_place()
            near = _near(x, y, c)
            for k in range(n):
                for j in range(N_NEAR):
                    owner = near[j][1] if j == 0 else near[j][1] + 1 - 2 * c
                    cp = _remote(_window(arr[k], kinds[k], owner, sizes[k]), arr[n + k].at[j], old[0], old[1],
                                 k * N_NEAR + j, near[0][0])
                    cp.wait_send()
                    cp.wait_recv()

        _, arrays, _ = _split_call(self.name + "_landed", self.arrays, self.sems, [], after, emit_wait)
        chip_sums = _chip_sums(arrays[:n], arrays[n:], kinds, sizes, own4, self.name + "_combine")
        arrivals = [lax.empty((N_NEAR - 1, *sh[1:]), p.dtype) for sh, p in zip(self.slot_shapes, self.partials)]

        def emit_send(arr, _, new):
            x, y, c, _ = _mesh_place()
            near = _near(x, y, c)
            for k in range(n):
                for j in (1, 2, 3):
                    _remote(arr[k].at[j], arr[n + k].at[j - 1], new[0], new[1], k * N_NEAR + j, near[j][0]).start()

        self.sems, self.arrays, self.token = _split_call(self.name + "_send", [*chip_sums, *arrivals], [],
                                                         [n * N_NEAR] * 2, own4, emit_send)

    def finish(self, after):
        n = self.n

        def emit(arr, old, _):
            x, y, c, _ = _mesh_place()
            near = _near(x, y, c)
            for k in range(n):
                for j in (1, 2, 3):
                    cp = _remote(arr[k].at[j], arr[n + k].at[j - 1], old[0], old[1], k * N_NEAR + j, near[j][0])
                    cp.wait_send()
                    cp.wait_recv()

        _, arrays, _ = _split_call(self.name + "_finish", self.arrays, self.sems, [], after, emit)
        return arrays[:n], arrays[n:]


def _chip_sums(partials, slots, kinds, sizes, own4, name):
    n = len(partials)

    def body(own_ref, *refs):
        for k in range(n):
            refs[2 * n + k][...] = (refs[k][...].astype(F32) + refs[n + k][...].astype(F32)).astype(BF16)

    in_specs, slot_specs = [], []
    for p, s, kind, size in zip(partials, slots, kinds, sizes):
        block = list(p.shape)
        block[kind] = size
        nd = len(block)
        in_specs.append(pl.BlockSpec(tuple(block), functools.partial(
            lambda j, own, kind, nd: tuple(own[j] if d == kind else 0 for d in range(nd)), kind=kind, nd=nd)))
        slot_specs.append(pl.BlockSpec((None, *block), functools.partial(
            lambda j, own, nd: (j,) + (0,) * nd, nd=nd)))
    return pl.pallas_call(
        body, name=name,
        grid_spec=pltpu.PrefetchScalarGridSpec(num_scalar_prefetch=1, grid=(N_NEAR,),
                                               in_specs=in_specs + slot_specs, out_specs=slot_specs),
        out_shape=[jax.ShapeDtypeStruct(s.shape, s.dtype) for s in slots],
        compiler_params=_params(("arbitrary",)),
    )(own4, *partials, *slots)


def _after(small, token):
    return small + token[0:1, 0:1].astype(small.dtype)


def _silu(c):
    return c * _sigmoid_tail(c)


def _ada_fwd(c_all, w_ada, b_ada_cols):
    def body(c_ref, w_ref, b_ref, out_ref):
        out_ref[...] = jnp.dot(_silu(c_ref[...]), w_ref[...], preferred_element_type=F32,
                               precision=lax.Precision.HIGHEST) + b_ref[...]

    return pl.pallas_call(
        body, name="ada_fwd", out_shape=jax.ShapeDtypeStruct((N_DEV, w_ada.shape[1]), F32),
    )(c_all, w_ada, b_ada_cols)


def _adam(w, g, m, v):
    m = ADAM_B1 * m + (1.0 - ADAM_B1) * g
    v = ADAM_B2 * v + (1.0 - ADAM_B2) * (g * g)
    m_hat = m / (1.0 - ADAM_B1 ** ADAM_STEP)
    v_hat = v / (1.0 - ADAM_B2 ** ADAM_STEP)
    delta = -ADAM_LR * (m_hat / (jnp.sqrt(v_hat) + ADAM_EPS) + ADAM_WD * w)
    return delta, m, v


def _ada_bwd_adam(c_all, dmod_cols, w, m, v):
    def body(c_ref, d_ref, w_ref, m_ref, v_ref, g_ref, delta_ref, nm_ref, nv_ref):
        g = lax.dot_general(_silu(c_ref[...]), d_ref[...], (((0,), (0,)), ((), ())),
                            preferred_element_type=F32, precision=lax.Precision.HIGHEST)
        g_ref[...] = g
        delta_ref[...], nm_ref[...], nv_ref[...] = _adam(w_ref[...], g, m_ref[...], v_ref[...])

    sd = jax.ShapeDtypeStruct(w.shape, F32)
    return pl.pallas_call(body, name="ada_bwd_adam", out_shape=[sd] * 4,
                          compiler_params=pltpu.CompilerParams(vmem_limit_bytes=V7X_VMEM_LIMIT),
                          )(c_all, dmod_cols, w, m, v)


def _adam_group(chip_sums, arrivals, ws, ms, vs, n_tiles, name):
    n = len(ws)

    def body(*refs):
        for k in range(n):
            c_ref, a_ref, w_ref, m_ref, v_ref = (refs[j * n + k] for j in range(5))
            g_ref, delta_ref, nm_ref, nv_ref = (refs[(5 + j) * n + k] for j in range(4))
            g = c_ref[...].astype(F32)
            for j in range(N_NEAR - 1):
                g = g + a_ref[j].astype(F32)
            g_ref[...] = g
            delta_ref[...], nm_ref[...], nv_ref[...] = _adam(w_ref[...], g, m_ref[...], v_ref[...])

    tiles = [(w.shape[0] // n_tiles, w.shape[1]) for w in ws]
    blk = [pl.BlockSpec(t, lambda i: (i, 0)) for t in tiles]
    return pl.pallas_call(
        body, name=name, grid=(n_tiles,),
        in_specs=[pl.BlockSpec((None, *t), lambda i: (0, i, 0)) for t in tiles]
        + [pl.BlockSpec((N_NEAR - 1, *t), lambda i: (0, i, 0)) for t in tiles] + blk * 3,
        out_specs=blk * 4, out_shape=[jax.ShapeDtypeStruct(w.shape, F32) for w in ws] * 4,
        compiler_params=_params(("parallel",)),
    )(*chip_sums, *arrivals, *ws, *ms, *vs)


N_SMALL = 40
N_SMALL_PARAMS = 11


def _pack_vecs(conv_w_full, rows):
    def body(cw_ref, *refs):
        out = refs[-1]
        out[...] = jnp.zeros_like(out)
        out[0:4, :] = cw_ref[0:4, :]
        for r, ref in enumerate(refs[:-1]):
            out[4 + r:5 + r, :] = ref[...]

    return pl.pallas_call(body, name="pack_vecs", out_shape=jax.ShapeDtypeStruct((16, D), F32))(conv_w_full, *rows)


def _small_finish(gathered, mod_all, vecs, ws, ms, vs):
    n = N_SMALL_PARAMS

    def body(g_ref, mod_ref, vec_ref, *refs):
        w_refs, m_refs, v_refs = refs[:n], refs[n:2 * n], refs[2 * n:3 * n]
        outs = refs[3 * n:]
        g1 = vec_ref[V_G1:V_G1 + 1, :]
        g2 = vec_ref[V_G2:V_G2 + 1, :]
        zero = jnp.zeros((1, D), F32)
        dg1, dg2, dgf, loss_lanes = zero, zero, zero, zero
        mixer = jnp.zeros((16, D), F32)
        db_ada = jnp.zeros((6, D), F32)
        for b in range(N_DEV):
            gb = g_ref[b]
            mod = mod_ref[b]
            q1 = gb[33:34]
            q2 = gb[9:10]
            dmod = jnp.concatenate([gb[32:33], q1 * g1, gb[10:11], gb[8:9], q2 * g2, gb[1:2]], axis=0)
            outs[4 * n][b] = dmod
            db_ada = db_ada + dmod
            dg1 = dg1 + q1 * (1.0 + mod[M_SC1:M_SC1 + 1])
            dg2 = dg2 + q2 * (1.0 + mod[M_SC2:M_SC2 + 1])
            dgf = dgf + gb[0:1]
            loss_lanes = loss_lanes + gb[2:3]
            mixer = mixer + gb[16:32]
        d_a_param = mixer[7:8] * _sigmoid_tail(vec_ref[V_A_PARAM:V_A_PARAM + 1, :])
        grads = [dg1, dg2, mixer[4:5], mixer[5:6], mixer[6:7], d_a_param, mixer[8:9], mixer[9:10], dgf,
                 db_ada, mixer[0:4]]
        for k in range(n):
            outs[k][...] = grads[k]
            outs[n + k][...], outs[2 * n + k][...], outs[3 * n + k][...] = _adam(
                w_refs[k][...], grads[k], m_refs[k][...], v_refs[k][...])
        outs[4 * n + 1][...] = jnp.broadcast_to(jnp.sum(loss_lanes, axis=1, keepdims=True), (8, 128))

    shapes = [jax.ShapeDtypeStruct(w.shape, F32) for w in ws]
    return pl.pallas_call(
        body, name="small_finish",
        out_shape=shapes * 4 + [jax.ShapeDtypeStruct((N_DEV, 6, D), F32), jax.ShapeDtypeStruct((8, 128), F32)],
    )(gathered, mod_all, vecs, *ws, *ms, *vs)


def _pad_rows(a, rows):
    return jnp.pad(a, ((0, rows - a.shape[0]), (0, 0)))


def kernel(x, c, norm_mix_g, norm_mlp_g, w_ada, b_ada, w_in, conv_w, conv_b, w_rg_a, b_rg_a, w_rg_x, b_rg_x, a_param, w_branch_a, w_pool, b_pool, pool_scale, w_branch_b, w_out, w_up, w_down, final_g, loss_target, m_norm_mix_g, m_norm_mlp_g, m_w_ada, m_b_ada, m_w_in, m_conv_w, m_conv_b, m_w_rg_a, m_b_rg_a, m_w_rg_x, m_b_rg_x, m_a_param, m_w_branch_a, m_w_pool, m_b_pool, m_pool_scale, m_w_branch_b, m_w_out, m_w_up, m_w_down, m_final_g, v_norm_mix_g, v_norm_mlp_g, v_w_ada, v_b_ada, v_w_in, v_conv_w, v_conv_b, v_w_rg_a, v_b_rg_a, v_w_rg_x, v_b_rg_x, v_a_param, v_w_branch_a, v_w_pool, v_b_pool, v_pool_scale, v_w_branch_b, v_w_out, v_w_up, v_w_down, v_final_g):
    me = 4 * lax.axis_index("x") + 2 * lax.axis_index("y") + lax.axis_index("c")
    s = x.shape[1]
    x2d = x.reshape(s, D)
    target = loss_target.reshape(s, D)
    n_ada = w_ada.shape[2]

    sharded = dict(w_in=(w_in[0], 1), w_up=(w_up[0], 1), w_down=(w_down[0], 0), w_branch_a=(w_branch_a[0], 0),
                   w_branch_b=(w_branch_b[0], 0), w_out=(w_out[0], 0), w_rg_a=(w_rg_a[0], 1), w_rg_x=(w_rg_x[0], 1),
                   w_pool=(w_pool[0], 1))
    kind = {k: v[1] for k, v in sharded.items()}
    shard = {k: v[0].astype(BF16) for k, v in sharded.items()}

    conv_w_full, c_rows = _all_gather([_pad_rows(conv_w[0], 8), _pad_rows(c, 8)], [1, 0], "gather_c")
    c_all = c_rows.reshape(N_DEV, 8, D)[:, 0, :]
    b_ada_cols = lax.dynamic_slice(b_ada, (0, me * n_ada), (1, n_ada))
    mod_part = _ada_fwd(c_all, w_ada[0], b_ada_cols)
    mod_parts, = _all_gather([mod_part], [0], "gather_mod")

    first_names = ["w_in", "w_rg_a", "w_rg_x", "w_pool"]
    branch_names = ["w_branch_a", "w_branch_b", "w_out"]
    mlp_names = ["w_up", "w_down"]

    def gather(group, after, name):
        return _Gather([shard[k] for k in group], [kind[k] for k in group], after, name)

    g_first = gather(first_names, mod_parts, "gather_first")
    g_branch = gather(branch_names, g_first.token, "gather_branch")
    g_mlp = gather(mlp_names, g_branch.token, "gather_mlp")

    mod_all = jnp.transpose(mod_parts.reshape(N_DEV, N_DEV, n_ada), (1, 0, 2)).reshape(N_DEV, 6, D)
    mod_all = jnp.pad(mod_all, ((0, 0), (0, 2), (0, 0)))
    modr = lax.dynamic_index_in_dim(mod_all, me, 0, keepdims=False)
    vecs = _pack_vecs(conv_w_full, [conv_b, b_rg_a, b_rg_x, a_param, b_pool, pool_scale,
                                    norm_mix_g, norm_mlp_g, final_g.reshape(1, D)])
    vecs = _after(vecs, g_mlp.token)
    g_first.forward(vecs)
    wg = dict(zip(first_names, g_first.finish(g_first.token)))

    h1, x_rnn, u_pool, ga, dga, sa, sb = _proj_fwd(x2d, modr, vecs, wg["w_in"])
    g_branch.forward(h1)
    xr, hr, za, p, pooled, *gates = _mix_fwd(x_rnn, u_pool, ga, _after(vecs, g_branch.token),
                                             wg["w_rg_a"], wg["w_rg_x"], wg["w_pool"])
    g_mlp.forward(za)
    wg.update(zip(branch_names, g_branch.finish(g_mlp.token)))
    ba, bb, merged, o, x2, h2 = _branch_fwd(za, pooled, sa, sb, x2d, modr, vecs,
                                            wg["w_branch_a"], wg["w_branch_b"], wg["w_out"])
    wg.update(zip(mlp_names, g_mlp.finish(h2)))
    ru, dx3, d_dn, small_f = _mlp_fwd(h2, x2, target, modr, vecs, wg["w_up"], wg["w_down"])

    near = _near(lax.axis_index("x"), lax.axis_index("y"), lax.axis_index("c"))
    own4 = jnp.stack([me, near[1][1], near[2][1], near[3][1]]).astype(jnp.int32)

    def scatter(group, partial, after, name):
        return _Scatter([partial[k] for k in group], [kind[k] for k in group], after, name)

    dup, dx2, do, small_m = _mlp_bwd(d_dn, ru, x2, dx3, o, modr, vecs, wg["w_up"], wg["w_down"])
    partial = dict(w_up=_wgrad(h2, dup, "wgrad_up"), w_down=_wgrad(ru, d_dn, "wgrad_down", square_a=True))
    s_mlp = scatter(mlp_names, partial, dx2, "scatter_mlp")

    dba, dbb, dgates, dza, dpooled = _branch_bwd(do, sa, sb, ba, bb, wg["w_branch_a"], wg["w_branch_b"], wg["w_out"],
                                                 dep=s_mlp.token)
    s_mlp.combine_and_send(own4, dza)
    dproj, dw_rg_a, dw_rg_x, dw_pool, small_x = _mix_bwd(dza, dpooled, x_rnn, ga, dga, xr, hr, p, gates, dgates,
                                                         _after(vecs, s_mlp.token),
                                                         wg["w_rg_a"], wg["w_rg_x"], wg["w_pool"])
    partial.update(w_branch_a=_wgrad(za, dba, "wgrad_branch_a"), w_branch_b=_wgrad(pooled, dbb, "wgrad_branch_b"),
                   w_out=_wgrad(merged, do, "wgrad_out"),
                   w_rg_a=dw_rg_a.astype(BF16), w_rg_x=dw_rg_x.astype(BF16), w_pool=dw_pool.astype(BF16))
    mixer_names = ["w_rg_a", "w_rg_x", "w_pool", "w_branch_a", "w_branch_b", "w_out"]
    s_mixer = scatter(mixer_names, partial, s_mlp.token, "scatter_mixer")

    partial["w_in"] = _wgrad(h1, dproj, "wgrad_in", dep=s_mixer.token)
    s_in = scatter(["w_in"], partial, s_mixer.token, "scatter_in")
    s_mixer.combine_and_send(own4, s_in.token)
    s_in.combine_and_send(own4, s_mixer.token)
    grad_x, small_p = _proj_bwd(dproj, x2d, dx2, _after(modr, s_in.token), vecs, wg["w_in"])

    locals_ = dict(w_in=(w_in, m_w_in, v_w_in), w_up=(w_up, m_w_up, v_w_up), w_down=(w_down, m_w_down, v_w_down),
                   w_branch_a=(w_branch_a, m_w_branch_a, v_w_branch_a),
                   w_branch_b=(w_branch_b, m_w_branch_b, v_w_branch_b), w_out=(w_out, m_w_out, v_w_out),
                   w_rg_a=(w_rg_a, m_w_rg_a, v_w_rg_a), w_rg_x=(w_rg_x, m_w_rg_x, v_w_rg_x),
                   w_pool=(w_pool, m_w_pool, v_w_pool))
    res = {}

    def finish(group, exchange, after, n_tiles, name):
        chip_sums, arrivals = exchange.finish(after)
        flat = lambda t: t.reshape(-1, t.shape[-1])
        shapes = [flat(locals_[k][0]).shape for k in group]
        outs = _adam_group([cs.reshape(N_NEAR, *sh) for cs, sh in zip(chip_sums, shapes)],
                           [ar.reshape(N_NEAR - 1, *sh) for ar, sh in zip(arrivals, shapes)],
                           *[[flat(locals_[k][j]) for k in group] for j in range(3)], n_tiles, name)
        for i, k in enumerate(group):
            res[k] = [outs[j * len(group) + i].reshape(locals_[k][0].shape) for j in range(4)]
        return res[group[-1]][0]

    small = jnp.concatenate([small_f, small_m, small_x, small_p], axis=0)
    g_small = _Gather([small], [0], grad_x, "gather_small")
    done = finish(mlp_names, s_mlp, g_small.token, 4, "adam_mlp")
    g_small.forward(done)
    done = finish(mixer_names, s_mixer, g_small.token, 2, "adam_mixer")
    done = finish(["w_in"], s_in, done, 4, "adam_in")
    small_all, = g_small.finish(done)
    small_all = small_all.reshape(N_DEV, N_SMALL, D)

    def embed(cw):
        return lax.dynamic_update_slice(jnp.zeros((4, D), F32), cw[0], (0, me * (D // N_DEV)))

    def smalls(ng, nl, cb, bra, brx, ap, bp, ps, fg, ba_, cw):
        return [ng, nl, cb, bra, brx, ap, bp, ps, fg.reshape(1, D), ba_.reshape(6, D), embed(cw)]

    small_names = ["norm_mix_g", "norm_mlp_g", "conv_b", "b_rg_a", "b_rg_x", "a_param", "b_pool", "pool_scale",
                   "final_g", "b_ada", "conv_w"]
    fin = _small_finish(
        small_all, mod_all, vecs,
        smalls(norm_mix_g, norm_mlp_g, conv_b, b_rg_a, b_rg_x, a_param, b_pool, pool_scale, final_g, b_ada, conv_w),
        smalls(m_norm_mix_g, m_norm_mlp_g, m_conv_b, m_b_rg_a, m_b_rg_x, m_a_param, m_b_pool, m_pool_scale,
               m_final_g, m_b_ada, m_conv_w),
        smalls(v_norm_mix_g, v_norm_mlp_g, v_conv_b, v_b_rg_a, v_b_rg_x, v_a_param, v_b_pool, v_pool_scale,
               v_final_g, v_b_ada, v_conv_w))
    dmod_all, loss_tile = fin[4 * N_SMALL_PARAMS], fin[4 * N_SMALL_PARAMS + 1]
    dmod_cols = lax.dynamic_slice(dmod_all.reshape(N_DEV, 6 * D), (0, me * n_ada), (N_DEV, n_ada))
    res["w_ada"] = [t.reshape(w_ada.shape) for t in _ada_bwd_adam(c_all, dmod_cols, w_ada[0], m_w_ada[0], v_w_ada[0])]

    def final_shape(k, t):
        if k == "final_g":
            return t.reshape(D)
        if k == "b_ada":
            return t.reshape(1, 6 * D)
        if k == "conv_w":
            return lax.dynamic_slice(t, (0, me * (D // N_DEV)), (4, D // N_DEV)).reshape(conv_w.shape)
        return t

    for i, k in enumerate(small_names):
        res[k] = [final_shape(k, fin[which * N_SMALL_PARAMS + i]) for which in range(4)]
    order = ["norm_mix_g", "norm_mlp_g", "w_ada", "b_ada", "w_in", "conv_w", "conv_b", "w_rg_a", "b_rg_a", "w_rg_x",
             "b_rg_x", "a_param", "w_branch_a", "w_pool", "b_pool", "pool_scale", "w_branch_b", "w_out", "w_up",
             "w_down", "final_g"]
    outs = [loss_tile[0, 0], grad_x.reshape(x.shape)]
    for which in range(4):
        for k in order:
            outs.append(res[k][which])
    return tuple(outs)
```

```python
import functools

import jax
import jax.numpy as jnp
from jax import lax
from jax.experimental import pallas as pl
from jax.experimental.pallas import tpu as pltpu

F32 = jnp.float32
BF16 = jnp.bfloat16
MESH = pl.DeviceIdType.MESH

N_DEV = 8
D = 1024
N_GROUPS = 4
GW = D // N_GROUPS
D_IN = 5 * D
D_FF = 4 * D
POOL_WINDOWS = (2, 4, 8, 16)
HALO_X = 8
HALO_U = 16
EPS = 1e-6
C_RG = 8.0
ADAM_LR, ADAM_B1, ADAM_B2, ADAM_EPS, ADAM_WD, ADAM_STEP = 0.001, 0.9, 0.999, 1e-08, 0.01, 10

V7X_VMEM_LIMIT = 56 * 1024 * 1024

V_CONV_W, V_CONV_B, V_B_RG_A, V_B_RG_X, V_A_PARAM, V_B_POOL, V_POOL_SCALE, V_G1, V_G2, V_GF = 0, 4, 5, 6, 7, 8, 9, 10, 11, 12
M_SH1, M_SC1, M_GT1, M_SH2, M_SC2, M_GT2 = 0, 1, 2, 3, 4, 5

TM_PROJ = 512
TM_MIX = 256
TM_BRANCH = 256
TM_MLP = 512
TM_MLP_BWD = 256
TS_WGRAD = 1024


def _params(semantics):
    return pltpu.CompilerParams(dimension_semantics=semantics, vmem_limit_bytes=V7X_VMEM_LIMIT)


def _resident(shape):
    return pl.BlockSpec(shape, lambda *_: (0,) * len(shape), pipeline_mode=pl.Buffered(1))


def _dot(a, b):
    return jnp.dot(a, b, preferred_element_type=F32)


def _dot_nt(a, b):
    return lax.dot_general(a, b, (((1,), (1,)), ((), ())), preferred_element_type=F32)


def _dot_tn(a, b):
    return lax.dot_general(a, b, (((0,), (0,)), ((), ())), preferred_element_type=F32)


def _sigmoid(x):
    return 0.5 * jnp.tanh(0.5 * x) + 0.5


def _sigmoid_tail(x):
    return 1.0 / (1.0 + jnp.exp(-x))


def _gelu_and_grad(x):
    k = 0.7978845608028654
    x2 = x * x
    t = jnp.tanh(k * (x + 0.044715 * x * x2))
    g = 0.5 * x * (1.0 + t)
    dg = 0.5 * (1.0 + t) + 0.5 * x * (1.0 - t * t) * (k * (1.0 + 3.0 * 0.044715 * x2))
    return g, dg


def _softplus(a):
    e = jnp.exp(-jnp.abs(a))
    u = 1.0 + e
    log1p_e = jnp.where(u == 1.0, e, jnp.log(u) * e / jnp.where(u == 1.0, 1.0, u - 1.0))
    return jnp.maximum(a, 0.0) + log1p_e


def _neg_expm1(z):
    series = -(z * (1.0 + z * (0.5 + z * (1.0 / 6.0 + z * (1.0 / 24.0 + z * (1.0 / 120.0))))))
    return jnp.where(z > -0.1, series, 1.0 - jnp.exp(z))


def _shift_down(x, k):
    return pltpu.roll(x, k, 0)


def _shift_up(x, k):
    return pltpu.roll(x, x.shape[0] - k, 0)


def _rglru_gates(xr, w_a, w_x, b_a, b_x, a_param, is_t0):
    xb = xr.astype(BF16)
    ra = _sigmoid(_dot(xb, w_a) + b_a)
    ri = _sigmoid(_dot(xb, w_x) + b_x)
    sp = _softplus(a_param)
    log_a = (-C_RG) * ra * sp
    a = jnp.exp(log_a)
    mult = jnp.where(is_t0, 1.0, jnp.sqrt(_neg_expm1(2.0 * log_a)))
    return ra, ri, sp, a, mult


SUBLANES = 8


LANES = 128


def _scan_strip(a, b, carry, scr, down):
    t = b.shape[0]
    g = t // SUBLANES
    a3 = a.reshape(g, SUBLANES, LANES)
    b3 = b.reshape(g, SUBLANES, LANES)
    sub = lax.broadcasted_iota(jnp.int32, (g, SUBLANES, LANES), 1)
    for k in (1, 2, 4):
        keep = sub >= k if down else sub < SUBLANES - k
        shift = k if down else SUBLANES - k
        b3 = b3 + a3 * jnp.where(keep, pltpu.roll(b3, shift, 1), 0.0)
        a3 = a3 * jnp.where(keep, pltpu.roll(a3, shift, 1), 1.0)
    scr[0] = a3.reshape(t, LANES)
    scr[1] = b3.reshape(t, LANES)
    end_row = SUBLANES - 1 if down else 0
    ag = scr[0, pl.ds(end_row, g, stride=SUBLANES), :]
    bg = scr[1, pl.ds(end_row, g, stride=SUBLANES), :]
    rg = lax.broadcasted_iota(jnp.int32, (g, LANES), 0)
    edge = 0 if down else g - 1
    bg = bg + jnp.where(rg == edge, ag * carry, 0.0)
    k = 1
    while k < g:
        keep = rg >= k if down else rg < g - k
        shift = k if down else g - k
        bg = bg + ag * jnp.where(keep, pltpu.roll(bg, shift, 0), 0.0)
        if 2 * k < g:
            ag = ag * pltpu.roll(ag, shift, 0)
        k *= 2
    entering = jnp.where(rg != edge, pltpu.roll(bg, 1 if down else g - 1, 0), carry)
    for r in range(SUBLANES):
        scr[2, pl.ds(r, g, stride=SUBLANES), :] = entering
    return scr[1] + scr[0] * scr[2], bg[g - 1:g, :]


def _scan_strips(a, b, carry, scr, down):
    outs = [_scan_strip(a[:, c:c + LANES], b[:, c:c + LANES], carry[:, c:c + LANES], scr, down)
            for c in range(0, b.shape[1], LANES)]
    return jnp.concatenate([o[0] for o in outs], axis=1), jnp.concatenate([o[1] for o in outs], axis=1)


def _scan_down(a, b, carry, scr):
    return _scan_strips(a, b, carry, scr, True)


def _scan_up(m, b, carry, scr):
    return _scan_strips(m, b, carry, scr, False)[0]


def _window_mean(sums, window, first_block, head_t):
    scaled = sums * (1.0 / window)
    head = jnp.where(first_block, sums[:HALO_U] / jnp.minimum(head_t, float(window)), scaled[:HALO_U])
    return jnp.concatenate([head, scaled[HALO_U:]], axis=0)


def _conv_taps(x_ext):
    return [_shift_down(x_ext, 3 - j)[HALO_X:] if j < 3 else x_ext[HALO_X:] for j in range(4)]


def _proj_fwd(x, modr, vecs, w_in):
    s = x.shape[0]
    tm = min(TM_PROJ, s)

    def body(x_ref, mod_ref, vec_ref, w_ref, h1_ref, xrnn_ref, u_ref, ga_ref, dga_ref, sa_ref, sb_ref):
        xv = x_ref[...]
        r = lax.rsqrt(jnp.mean(xv * xv, axis=-1, keepdims=True) + EPS)
        gain = vec_ref[V_G1:V_G1 + 1, :] * (1.0 + mod_ref[M_SC1:M_SC1 + 1, :])
        h = (xv * r * gain + mod_ref[M_SH1:M_SH1 + 1, :]).astype(BF16)
        h1_ref[...] = h
        xrnn_ref[...] = _dot(h, w_ref[:, 0:D])
        ga_ref[...], dga_ref[...] = _gelu_and_grad(_dot(h, w_ref[:, D:2 * D]))
        u_ref[...] = _dot(h, w_ref[:, 2 * D:3 * D])
        sa_ref[...] = _sigmoid(_dot(h, w_ref[:, 3 * D:4 * D]))
        sb_ref[...] = _sigmoid(_dot(h, w_ref[:, 4 * D:5 * D]))

    tok = pl.BlockSpec((tm, D), lambda i: (i, 0))
    sd = lambda dt: jax.ShapeDtypeStruct((s, D), dt)
    return pl.pallas_call(
        body, name="proj_fwd", grid=(s // tm,),
        in_specs=[tok, pl.BlockSpec((8, D), lambda i: (0, 0)), pl.BlockSpec((16, D), lambda i: (0, 0)),
                  _resident((D, D_IN))],
        out_specs=[tok] * 7,
        out_shape=[sd(BF16)] + [sd(F32)] * 6,
        compiler_params=_params(("parallel",)),
    )(x, modr, vecs, w_in)


def _mix_fwd(x_rnn, u_pool, ga, vecs, w_rg_a, w_rg_x, w_pool):
    s = x_rnn.shape[0]
    tm = min(TM_MIX, s)
    nb = s // tm

    def body(xh_ref, x_ref, uh_ref, u_ref, ga_ref, vec_ref, wa_ref, wx_ref, wp_ref,
             xr_ref, hr_ref, za_ref, p_ref, pooled_ref, a_ref, mult_ref, ra_ref, ri_ref, carry_ref, scan_scr):
        i = pl.program_id(0)
        first = i == 0

        @pl.when(first)
        def _():
            carry_ref[...] = jnp.zeros_like(carry_ref)

        row = lax.broadcasted_iota(jnp.int32, (tm, GW), 0)
        is_t0 = jnp.logical_and(first, row == 0)
        head_t = (lax.broadcasted_iota(jnp.int32, (HALO_U, GW), 0) + 1).astype(F32)
        for g in range(N_GROUPS):
            cs = slice(g * GW, (g + 1) * GW)
            vec = vec_ref[:, cs]
            xh = jnp.where(first, 0.0, xh_ref[:, cs])
            taps = _conv_taps(jnp.concatenate([xh, x_ref[:, cs]], axis=0))
            xr = vec[V_CONV_B:V_CONV_B + 1]
            for j in range(4):
                xr = xr + vec[V_CONV_W + j:V_CONV_W + j + 1] * taps[j]
            xr_ref[:, cs] = xr
            ra, ri, _, a, mult = _rglru_gates(
                xr, wa_ref[g], wx_ref[g], vec[V_B_RG_A:V_B_RG_A + 1], vec[V_B_RG_X:V_B_RG_X + 1],
                vec[V_A_PARAM:V_A_PARAM + 1], is_t0)
            a_ref[:, cs] = a
            mult_ref[:, cs] = mult
            ra_ref[:, cs] = ra.astype(BF16)
            ri_ref[:, cs] = ri.astype(BF16)
            h, last = _scan_down(a, xr * ri * mult, carry_ref[0:1, cs], scan_scr)
            hr_ref[:, cs] = h
            carry_ref[0:1, cs] = last
            za_ref[:, cs] = (ga_ref[:, cs] * h).astype(BF16)
            uh = jnp.where(first, 0.0, uh_ref[:, cs])
            sm = jnp.concatenate([uh, u_ref[:, cs]], axis=0)
            k = 1
            while k < POOL_WINDOWS[g]:
                sm = sm + _shift_down(sm, k)
                k *= 2
            mean = _window_mean(sm[HALO_U:], POOL_WINDOWS[g], first, head_t)
            p = (mean - u_ref[:, cs]).astype(BF16)
            p_ref[:, cs] = p
            pb = _dot(p, wp_ref[g]) + vec[V_B_POOL:V_B_POOL + 1]
            pooled_ref[:, cs] = (pb * vec[V_POOL_SCALE:V_POOL_SCALE + 1]).astype(BF16)

    tok = pl.BlockSpec((tm, D), lambda i: (i, 0))
    halo = lambda rows: pl.BlockSpec((rows, D), lambda i: (jnp.maximum(i * (tm // rows) - 1, 0), 0))
    wspec = pl.BlockSpec((N_GROUPS, GW, GW), lambda i: (0, 0, 0))
    sd = lambda dt: jax.ShapeDtypeStruct((s, D), dt)
    return pl.pallas_call(
        body, name="mix_fwd", grid=(nb,),
        in_specs=[halo(HALO_X), tok, halo(HALO_U), tok, tok, pl.BlockSpec((16, D), lambda i: (0, 0)),
                  wspec, wspec, wspec],
        out_specs=[tok] * 9,
        out_shape=[sd(F32), sd(F32), sd(BF16), sd(BF16), sd(BF16), sd(F32), sd(F32), sd(BF16), sd(BF16)],
        scratch_shapes=[pltpu.VMEM((8, D), F32), pltpu.VMEM((3, tm, LANES), F32)],
        compiler_params=_params(("arbitrary",)),
    )(x_rnn, x_rnn, u_pool, u_pool, ga, vecs, w_rg_a, w_rg_x, w_pool)


def _branch_fwd(za, pooled, sa, sb, x, modr, vecs, w_a, w_b, w_out):
    s = x.shape[0]
    tm = min(TM_BRANCH, s)

    def body(za_ref, pooled_ref, sa_ref, sb_ref, x_ref, mod_ref, vec_ref, wa_ref, wb_ref, wo_ref,
             ba_ref, bb_ref, merged_ref, o_ref, x2_ref, h2_ref):
        ba = _dot(za_ref[...], wa_ref[...])
        bb = _dot(pooled_ref[...], wb_ref[...])
        ba_ref[...] = ba.astype(BF16)
        bb_ref[...] = bb.astype(BF16)
        merged = (sa_ref[...] * ba + sb_ref[...] * bb).astype(BF16)
        merged_ref[...] = merged
        o = _dot(merged, wo_ref[...])
        o_ref[...] = o.astype(BF16)
        x2 = x_ref[...] + mod_ref[M_GT1:M_GT1 + 1, :] * o
        x2_ref[...] = x2
        r = lax.rsqrt(jnp.mean(x2 * x2, axis=-1, keepdims=True) + EPS)
        gain = vec_ref[V_G2:V_G2 + 1, :] * (1.0 + mod_ref[M_SC2:M_SC2 + 1, :])
        h2_ref[...] = (x2 * r * gain + mod_ref[M_SH2:M_SH2 + 1, :]).astype(BF16)

    tok = pl.BlockSpec((tm, D), lambda i: (i, 0))
    wspec = pl.BlockSpec((D, D), lambda i: (0, 0))
    sd = lambda dt: jax.ShapeDtypeStruct((s, D), dt)
    return pl.pallas_call(
        body, name="branch_fwd", grid=(s // tm,),
        in_specs=[tok, tok, tok, tok,
                  tok, pl.BlockSpec((8, D), lambda i: (0, 0)), pl.BlockSpec((16, D), lambda i: (0, 0)),
                  wspec, wspec, wspec],
        out_specs=[tok] * 6,
        out_shape=[sd(BF16), sd(BF16), sd(BF16), sd(BF16), sd(F32), sd(BF16)],
        compiler_params=_params(("parallel",)),
    )(za, pooled, sa, sb, x, modr, vecs, w_a, w_b, w_out)


def _mlp_fwd(h2, x2, target, modr, vecs, w_up, w_down):
    s = x2.shape[0]
    tm = min(TM_MLP, s)

    def body(h2_ref, x2_ref, tgt_ref, mod_ref, vec_ref, wu_ref, wd_ref,
             ru_ref, dx3_ref, ddn_ref, small_ref):
        @pl.when(pl.program_id(0) == 0)
        def _():
            small_ref[...] = jnp.zeros_like(small_ref)

        h2 = h2_ref[...]
        dn = None
        for c in range(D_FF // D):
            cs = slice(c * D, (c + 1) * D)
            ru = jnp.maximum(_dot(h2, wu_ref[:, cs]), 0.0)
            ru_ref[:, cs] = ru.astype(BF16)
            part = _dot((ru * ru).astype(BF16), wd_ref[cs, :])
            dn = part if dn is None else dn + part
        gt2 = mod_ref[M_GT2:M_GT2 + 1, :]
        gf = vec_ref[V_GF:V_GF + 1, :]
        x3 = x2_ref[...] + gt2 * dn
        r3 = lax.rsqrt(jnp.mean(x3 * x3, axis=-1, keepdims=True) + EPS)
        n3 = x3 * r3
        err = n3 * gf - tgt_ref[...]
        dy = err * (1.0 / D)
        dn3 = dy * gf
        dx3 = r3 * (dn3 - n3 * jnp.mean(dn3 * n3, axis=-1, keepdims=True))
        dx3_ref[...] = dx3
        ddn_ref[...] = (dx3 * gt2).astype(BF16)
        small_ref[0:1, :] += jnp.sum(dy * n3, axis=0, keepdims=True)
        small_ref[1:2, :] += jnp.sum(dx3 * dn, axis=0, keepdims=True)
        small_ref[2:3, :] += (0.5 / D) * jnp.sum(err * err, axis=0, keepdims=True)

    tok = pl.BlockSpec((tm, D), lambda i: (i, 0))
    return pl.pallas_call(
        body, name="mlp_fwd", grid=(s // tm,),
        in_specs=[tok, tok, tok,
                  pl.BlockSpec((8, D), lambda i: (0, 0)), pl.BlockSpec((16, D), lambda i: (0, 0)),
                  _resident((D, D_FF)), _resident((D_FF, D))],
        out_specs=[pl.BlockSpec((tm, D_FF), lambda i: (i, 0)), tok, tok,
                   pl.BlockSpec((8, D), lambda i: (0, 0))],
        out_shape=[jax.ShapeDtypeStruct((s, D_FF), BF16), jax.ShapeDtypeStruct((s, D), F32),
                   jax.ShapeDtypeStruct((s, D), BF16), jax.ShapeDtypeStruct((8, D), F32)],
        compiler_params=_params(("arbitrary",)),
    )(h2, x2, target, modr, vecs, w_up, w_down)


def _mlp_bwd(d_dn, ru, x2, dx3, o, modr, vecs, w_up, w_down):
    s = x2.shape[0]
    tm = min(TM_MLP_BWD, s)

    def body(ddn_ref, ru_ref, x2_ref, dx3_ref, o_ref, mod_ref, vec_ref, wu_ref, wd_ref,
             dup_ref, dx2_ref, do_ref, small_ref):
        @pl.when(pl.program_id(0) == 0)
        def _():
            small_ref[...] = jnp.zeros_like(small_ref)

        ddn = ddn_ref[...]
        dh2 = None
        for c in range(D_FF // D):
            cs = slice(c * D, (c + 1) * D)
            dff = _dot_nt(ddn, wd_ref[cs, :])
            dup = (dff * (2.0 * ru_ref[:, cs].astype(F32))).astype(BF16)
            dup_ref[:, cs] = dup
            part = _dot_nt(dup, wu_ref[:, cs])
            dh2 = part if dh2 is None else dh2 + part
        x2 = x2_ref[...]
        r2 = lax.rsqrt(jnp.mean(x2 * x2, axis=-1, keepdims=True) + EPS)
        xn2 = x2 * r2
        gain = vec_ref[V_G2:V_G2 + 1, :] * (1.0 + mod_ref[M_SC2:M_SC2 + 1, :])
        dxn2 = dh2 * gain
        dx2 = dx3_ref[...] + r2 * (dxn2 - xn2 * jnp.mean(dxn2 * xn2, axis=-1, keepdims=True))
        dx2_ref[...] = dx2
        do_ref[...] = (dx2 * mod_ref[M_GT1:M_GT1 + 1, :]).astype(BF16)
        small_ref[0:1, :] += jnp.sum(dh2, axis=0, keepdims=True)
        small_ref[1:2, :] += jnp.sum(dh2 * xn2, axis=0, keepdims=True)
        small_ref[2:3, :] += jnp.sum(dx2 * o_ref[...].astype(F32), axis=0, keepdims=True)

    tok = pl.BlockSpec((tm, D), lambda i: (i, 0))
    wide = pl.BlockSpec((tm, D_FF), lambda i: (i, 0))
    return pl.pallas_call(
        body, name="mlp_bwd", grid=(s // tm,),
        in_specs=[tok, wide, tok, tok, tok,
                  pl.BlockSpec((8, D), lambda i: (0, 0)), pl.BlockSpec((16, D), lambda i: (0, 0)),
                  _resident((D, D_FF)), _resident((D_FF, D))],
        out_specs=[wide, tok, tok, pl.BlockSpec((8, D), lambda i: (0, 0))],
        out_shape=[jax.ShapeDtypeStruct((s, D_FF), BF16), jax.ShapeDtypeStruct((s, D), F32),
                   jax.ShapeDtypeStruct((s, D), BF16), jax.ShapeDtypeStruct((8, D), F32)],
        compiler_params=_params(("arbitrary",)),
    )(d_dn, ru, x2, dx3, o, modr, vecs, w_up, w_down)


def _branch_bwd(do, sa, sb, ba, bb, w_a, w_b, w_out, dep):
    s = do.shape[0]
    tm = min(TM_BRANCH, s)

    def body(do_ref, sa_ref, sb_ref, ba_ref, bb_ref, wa_ref, wb_ref, wo_ref, dep_ref,
             dba_ref, dbb_ref, dg_ref, dza_ref, dpooled_ref):
        dmerged = _dot_nt(do_ref[...], wo_ref[...])
        sa = sa_ref[...]
        sb = sb_ref[...]
        dba = (dmerged * sa).astype(BF16)
        dbb = (dmerged * sb).astype(BF16)
        dba_ref[...] = dba
        dbb_ref[...] = dbb
        dg_ref[:, :D] = (dmerged * ba_ref[...].astype(F32) * sa * (1.0 - sa)).astype(BF16)
        dg_ref[:, D:] = (dmerged * bb_ref[...].astype(F32) * sb * (1.0 - sb)).astype(BF16)
        dza_ref[...] = _dot_nt(dba, wa_ref[...])
        dpooled_ref[...] = _dot_nt(dbb, wb_ref[...])

    tok = pl.BlockSpec((tm, D), lambda i: (i, 0))
    wspec = pl.BlockSpec((D, D), lambda i: (0, 0))
    sd = lambda dt: jax.ShapeDtypeStruct((s, D), dt)
    return pl.pallas_call(
        body, name="branch_bwd", grid=(s // tm,),
        in_specs=[tok, tok, tok, tok, tok, wspec, wspec, wspec, pl.BlockSpec(memory_space=pl.ANY)],
        out_specs=[tok, tok, pl.BlockSpec((tm, 2 * D), lambda i: (i, 0)), tok, tok],
        out_shape=[sd(BF16), sd(BF16), jax.ShapeDtypeStruct((s, 2 * D), BF16), sd(F32), sd(F32)],
        compiler_params=_params(("parallel",)),
    )(do, sa, sb, ba, bb, w_a, w_b, w_out, dep)


def _mix_bwd(dza, dpooled, x_rnn, ga, dga, xr, hr, p, gates, dgates, vecs, w_rg_a, w_rg_x, w_pool):
    s = xr.shape[0]
    tm = min(TM_MIX, s)
    nb = s // tm

    def body(dza_ref, dpooled_ref, xh_ref, x_ref, ga_ref, dga_ref, xr_ref, hh_ref, hr_ref, p_ref,
             a_ref, mult_ref, ra_ref, ri_ref, dg_ref, vec_ref, wa_ref, wx_ref, wp_ref,
             dproj_ref, dwa_ref, dwx_ref, dwp_ref, small_ref,
             scan_carry, dxr_carry, q_carry, scan_scr, dwa_acc, dwx_acc, dwp_acc):
        i = pl.program_id(0)
        bi = nb - 1 - i
        first_t = bi == 0

        @pl.when(i == 0)
        def _():
            scan_carry[...] = jnp.zeros_like(scan_carry)
            dxr_carry[...] = jnp.zeros_like(dxr_carry)
            q_carry[...] = jnp.zeros_like(q_carry)
            dwa_acc[...] = jnp.zeros_like(dwa_acc)
            dwx_acc[...] = jnp.zeros_like(dwx_acc)
            dwp_acc[...] = jnp.zeros_like(dwp_acc)
            small_ref[...] = jnp.zeros_like(small_ref)

        row = lax.broadcasted_iota(jnp.int32, (tm, GW), 0)
        is_t0 = jnp.logical_and(first_t, row == 0)
        head_t = (lax.broadcasted_iota(jnp.int32, (HALO_U, GW), 0) + 1).astype(F32)
        colsum = lambda v: jnp.sum(v, axis=0, keepdims=True)
        for g in range(N_GROUPS):
            cs = slice(g * GW, (g + 1) * GW)
            vec = vec_ref[:, cs]
            xr = xr_ref[:, cs]
            hr = hr_ref[:, cs]
            dza = dza_ref[:, cs]
            dproj_ref[:, D + g * GW:D + (g + 1) * GW] = (dza * hr * dga_ref[:, cs]).astype(BF16)
            dhr = dza * ga_ref[:, cs]
            a = a_ref[:, cs]
            mult = mult_ref[:, cs]
            ra = ra_ref[:, cs].astype(F32)
            ri = ri_ref[:, cs].astype(F32)
            sp = _softplus(vec[V_A_PARAM:V_A_PARAM + 1])
            m = jnp.where(row == tm - 1, 1.0, _shift_up(a, 1))
            gsum = _scan_up(m, dhr, scan_carry[0:1, cs], scan_scr)
            scan_carry[0:1, cs] = a[0:1, :] * gsum[0:1, :]
            hh = jnp.where(first_t, 0.0, hh_ref[:, cs])
            hprev = _shift_down(jnp.concatenate([hh, hr], axis=0), 1)[8:]
            da = gsum * hprev
            dmult = jnp.where(is_t0, 0.0, gsum * xr * ri)
            dlog_a = da * a - dmult * a * a / mult
            dri = gsum * xr * mult
            dxr = gsum * ri * mult
            small_ref[7:8, cs] += colsum((-C_RG) * ra * dlog_a)
            dpa = (((-C_RG) * sp) * dlog_a * ra * (1.0 - ra))
            dpx = dri * ri * (1.0 - ri)
            small_ref[5:6, cs] += colsum(dpa)
            small_ref[6:7, cs] += colsum(dpx)
            dpa = dpa.astype(BF16)
            dpx = dpx.astype(BF16)
            xrb = xr.astype(BF16)
            dwa_acc[g] += _dot_tn(xrb, dpa)
            dwx_acc[g] += _dot_tn(xrb, dpx)
            dxr = dxr + _dot_nt(dpa, wa_ref[g]) + _dot_nt(dpx, wx_ref[g])
            small_ref[4:5, cs] += colsum(dxr)
            xh = jnp.where(first_t, 0.0, xh_ref[:, cs])
            taps = _conv_taps(jnp.concatenate([xh, x_ref[:, cs]], axis=0))
            dxr_ext = jnp.concatenate([dxr, dxr_carry[:, cs]], axis=0)
            dx = vec[V_CONV_W + 3:V_CONV_W + 4] * dxr
            for j in range(4):
                small_ref[j:j + 1, cs] += colsum(dxr * taps[j])
                if j < 3:
                    dx = dx + vec[V_CONV_W + j:V_CONV_W + j + 1] * _shift_up(dxr_ext, 3 - j)[:tm]
            dxr_carry[:, cs] = dxr[0:8, :]
            dproj_ref[:, cs] = dx.astype(BF16)
            pg = p_ref[:, cs]
            dpooled = dpooled_ref[:, cs]
            pb = _dot(pg, wp_ref[g]) + vec[V_B_POOL:V_B_POOL + 1]
            small_ref[9:10, cs] += colsum(dpooled * pb)
            dpb = dpooled * vec[V_POOL_SCALE:V_POOL_SCALE + 1]
            small_ref[8:9, cs] += colsum(dpb)
            dpbb = dpb.astype(BF16)
            dwp_acc[g] += _dot_tn(pg, dpbb)
            dp = _dot_nt(dpbb, wp_ref[g])
            q = _window_mean(dp, POOL_WINDOWS[g], first_t, head_t)
            sm = jnp.concatenate([q, q_carry[:, cs]], axis=0)
            k = 1
            while k < POOL_WINDOWS[g]:
                sm = sm + _shift_up(sm, k)
                k *= 2
            q_carry[:, cs] = q[0:HALO_U, :]
            dproj_ref[:, 2 * D + g * GW:2 * D + (g + 1) * GW] = (sm[:tm] - dp).astype(BF16)
        dproj_ref[:, 3 * D:] = dg_ref[...]

        @pl.when(i == nb - 1)
        def _():
            dwa_ref[...] = dwa_acc[...].astype(BF16)
            dwx_ref[...] = dwx_acc[...].astype(BF16)
            dwp_ref[...] = dwp_acc[...].astype(BF16)

    rev = lambda i: nb - 1 - i
    tok = pl.BlockSpec((tm, D), lambda i: (rev(i), 0))
    halo8 = lambda k: pl.BlockSpec((8, D), lambda i: (jnp.maximum(rev(i) * (tm // 8) - 1, 0), k))
    wspec = pl.BlockSpec((N_GROUPS, GW, GW), lambda i: (0, 0, 0))
    wshape = jax.ShapeDtypeStruct((N_GROUPS, GW, GW), BF16)
    return pl.pallas_call(
        body, name="mix_bwd", grid=(nb,),
        in_specs=[tok, tok, halo8(0), tok, tok, tok, tok, halo8(0), tok, tok, tok, tok, tok, tok,
                  pl.BlockSpec((tm, 2 * D), lambda i: (rev(i), 0)),
                  pl.BlockSpec((16, D), lambda i: (0, 0)), wspec, wspec, wspec],
        out_specs=[pl.BlockSpec((tm, D_IN), lambda i: (rev(i), 0)), wspec, wspec, wspec,
                   pl.BlockSpec((16, D), lambda i: (0, 0))],
        out_shape=[jax.ShapeDtypeStruct((s, D_IN), BF16), wshape, wshape, wshape,
                   jax.ShapeDtypeStruct((16, D), F32)],
        scratch_shapes=[pltpu.VMEM((8, D), F32), pltpu.VMEM((8, D), F32), pltpu.VMEM((HALO_U, D), F32),
                        pltpu.VMEM((3, tm, LANES), F32)] + [pltpu.VMEM((N_GROUPS, GW, GW), F32)] * 3,
        compiler_params=_params(("arbitrary",)),
    )(dza, dpooled, x_rnn, x_rnn, ga, dga, xr, hr, hr, p, *gates, dgates, vecs, w_rg_a, w_rg_x, w_pool)


def _proj_bwd(dproj, x, dx2, modr, vecs, w_in):
    s = x.shape[0]
    tm = min(TM_PROJ, s)

    def body(dp_ref, x_ref, dx2_ref, mod_ref, vec_ref, w_ref, gx_ref, small_ref):
        @pl.when(pl.program_id(0) == 0)
        def _():
            small_ref[...] = jnp.zeros_like(small_ref)

        dh1 = None
        for c in range(D_IN // D):
            cs = slice(c * D, (c + 1) * D)
            part = _dot_nt(dp_ref[:, cs], w_ref[:, cs])
            dh1 = part if dh1 is None else dh1 + part
        xv = x_ref[...]
        r1 = lax.rsqrt(jnp.mean(xv * xv, axis=-1, keepdims=True) + EPS)
        xn1 = xv * r1
        gain = vec_ref[V_G1:V_G1 + 1, :] * (1.0 + mod_ref[M_SC1:M_SC1 + 1, :])
        dxn1 = dh1 * gain
        gx_ref[...] = dx2_ref[...] + r1 * (dxn1 - xn1 * jnp.mean(dxn1 * xn1, axis=-1, keepdims=True))
        small_ref[0:1, :] += jnp.sum(dh1, axis=0, keepdims=True)
        small_ref[1:2, :] += jnp.sum(dh1 * xn1, axis=0, keepdims=True)

    tok = pl.BlockSpec((tm, D), lambda i: (i, 0))
    return pl.pallas_call(
        body, name="proj_bwd", grid=(s // tm,),
        in_specs=[pl.BlockSpec((tm, D_IN), lambda i: (i, 0)), tok, tok,
                  pl.BlockSpec((8, D), lambda i: (0, 0)), pl.BlockSpec((16, D), lambda i: (0, 0)),
                  _resident((D, D_IN))],
        out_specs=[tok, pl.BlockSpec((8, D), lambda i: (0, 0))],
        out_shape=[jax.ShapeDtypeStruct((s, D), F32), jax.ShapeDtypeStruct((8, D), F32)],
        compiler_params=_params(("arbitrary",)),
    )(dproj, x, dx2, modr, vecs, w_in)


def _wgrad(a, b, name, square_a=False, dep=None):
    s, ka = a.shape
    n = b.shape[1]
    tka = ka if ka <= 1024 else ka // 2
    tn = n if n <= 1024 else n // 2
    ts = min(TS_WGRAD, s)
    ns = s // ts
    nc = 512
    deps = [] if dep is None else [dep]

    def body(a_ref, b_ref, *refs):
        out_ref, acc_ref = refs[-2:]
        t = pl.program_id(2)

        @pl.when(t == 0)
        def _():
            acc_ref[...] = jnp.zeros_like(acc_ref)

        av = a_ref[...]
        if square_a:
            af = av.astype(F32)
            av = (af * af).astype(BF16)
        for c in range(tn // nc):
            cs = slice(c * nc, (c + 1) * nc)
            acc_ref[:, cs] += _dot_tn(av, b_ref[:, cs])

        @pl.when(t == ns - 1)
        def _():
            out_ref[...] = acc_ref[...].astype(BF16)

    return pl.pallas_call(
        body, name=name, grid=(ka // tka, n // tn, ns),
        in_specs=[pl.BlockSpec((ts, tka), lambda i, j, t: (t, i)),
                  pl.BlockSpec((ts, tn), lambda i, j, t: (t, j))] + [pl.BlockSpec(memory_space=pl.ANY)] * len(deps),
        out_specs=pl.BlockSpec((tka, tn), lambda i, j, t: (i, j)),
        out_shape=jax.ShapeDtypeStruct((ka, n), BF16),
        scratch_shapes=[pltpu.VMEM((tka, tn), F32)],
        compiler_params=_params(("parallel", "parallel", "arbitrary")),
    )(a, b, *deps)


def _window(ref, kind, idx, size):
    start = pl.multiple_of(idx * size, size)
    if kind == 0:
        return ref.at[pl.ds(start, size)]
    if kind == 1:
        return ref.at[:, pl.ds(start, size)]
    return ref.at[:, :, pl.ds(start, size)]


def _mesh_place():
    x, y, c = lax.axis_index("x"), lax.axis_index("y"), lax.axis_index("c")
    return x, y, c, 4 * x + 2 * y + c


def _peer(x, y, c, q):
    px = 1 - x if q & 4 else x
    py = 1 - y if q & 2 else y
    pc = 1 - c if q & 1 else c
    return (px, py, pc), 4 * px + 2 * py + pc


def _all_gather(shards, kinds, name, dep=None):
    n = len(shards)
    deps = [] if dep is None else [dep]
    full_shapes = []
    for sh, kind in zip(shards, kinds):
        dims = list(sh.shape)
        dims[kind] *= N_DEV
        full_shapes.append(jax.ShapeDtypeStruct(tuple(dims), sh.dtype))

    def body(*refs):
        ins, outs = refs[:n], refs[n + len(deps):2 * n + len(deps)]
        send_sems, recv_sems, local_sems = refs[2 * n + len(deps):]
        x, y, c, me = _mesh_place()
        sends, recvs, locals_ = [], [], []
        for k in range(n):
            size = shards[k].shape[kinds[k]]
            mine = _window(outs[k], kinds[k], me, size)
            lc = pltpu.make_async_copy(ins[k], mine, local_sems.at[k])
            lc.start()
            locals_.append(lc)
            for q in range(1, N_DEV):
                peer, peer_idx = _peer(x, y, c, q)
                cp = pltpu.make_async_remote_copy(
                    src_ref=ins[k], dst_ref=mine, send_sem=send_sems.at[k, q], recv_sem=recv_sems.at[k, q],
                    device_id=peer, device_id_type=MESH)
                cp.start()
                sends.append(cp)
                recvs.append(pltpu.make_async_remote_copy(
                    src_ref=ins[k], dst_ref=_window(outs[k], kinds[k], peer_idx, size),
                    send_sem=send_sems.at[k, q], recv_sem=recv_sems.at[k, q],
                    device_id=peer, device_id_type=MESH))
        for cp in recvs:
            cp.wait_recv()
        for cp in sends:
            cp.wait_send()
        for lc in locals_:
            lc.wait()

    any_spec = pl.BlockSpec(memory_space=pl.ANY)
    return pl.pallas_call(
        body, name=name,
        in_specs=[any_spec] * (n + len(deps)), out_specs=[any_spec] * n, out_shape=full_shapes,
        scratch_shapes=[pltpu.SemaphoreType.DMA((n, N_DEV)), pltpu.SemaphoreType.DMA((n, N_DEV)),
                        pltpu.SemaphoreType.DMA((n,))],
    )(*shards, *deps)


_HBM = pl.BlockSpec(memory_space=pltpu.HBM)
_SEM = pl.BlockSpec(memory_space=pltpu.SEMAPHORE)
_EFFECT = pltpu.SideEffectType.DATAFLOW_SIDE_EFFECTING


N_NEAR = 4


def _near(x, y, c):
    out = [((x, y, 1 - c), 4 * x + 2 * y + 1 - c)]
    for j in (1, 2, 3):
        px = 1 - x if j & 2 else x
        py = 1 - y if j & 1 else y
        out.append(((px, py, c), 4 * px + 2 * py + c))
    return out


def _remote(src, dst, send_sems, recv_sems, slot, device):
    return pltpu.make_async_remote_copy(src_ref=src, dst_ref=dst, send_sem=send_sems.at[slot], recv_sem=recv_sems.at[slot],
                                        device_id=device, device_id_type=MESH)


def _split_call(name, arrays, sems_in, n_new_sems, after, emit):
    na, ns, nn = len(arrays), len(sems_in), len(n_new_sems)

    def body(*refs):
        emit(refs[:na], refs[na:na + ns], refs[na + ns + 1:na + ns + 1 + nn])
        refs[-1][...] = jnp.zeros_like(refs[-1])

    outs = pl.pallas_call(
        body, name=name,
        out_shape=(*[pltpu.SemaphoreType.DMA((m,)) for m in n_new_sems],
                   *[pltpu.HBM(a.shape, a.dtype) for a in arrays], jax.ShapeDtypeStruct((8, 128), F32)),
        in_specs=[_HBM] * na + [_SEM] * ns + [pl.BlockSpec(memory_space=pl.ANY)],
        out_specs=(*[_SEM] * nn, *[_HBM] * na, pl.BlockSpec(memory_space=pltpu.VMEM)),
        input_output_aliases={i: nn + i for i in range(na)},
        compiler_params=pltpu.CompilerParams(has_side_effects=_EFFECT),
    )(*[pltpu.with_memory_space_constraint(a, pltpu.HBM) for a in arrays], *sems_in, after)
    return list(outs[:nn]), list(outs[nn:nn + na]), outs[-1]


class _Gather:
    def __init__(self, shards, kinds, after, name):
        self.n, self.kinds, self.name = len(shards), kinds, name
        self.sizes = [s.shape[k] for s, k in zip(shards, kinds)]
        n = self.n
        lands = []
        for s, k in zip(shards, kinds):
            dims = list(s.shape)
            dims[k] *= N_DEV
            lands.append(lax.empty(tuple(dims), s.dtype))

        def emit(arr, _, new):
            x, y, c, me = _mesh_place()
            for k in range(n):
                pltpu.make_async_copy(arr[k], _window(arr[n + k], kinds[k], me, self.sizes[k]), new[2].at[k]).start()
            for k in range(n):
                mine = _window(arr[n + k], kinds[k], me, self.sizes[k])
                for j, (dev, _) in enumerate(_near(x, y, c)):
                    _remote(arr[k], mine, new[0], new[1], k * N_NEAR + j, dev).start()

        self.sems, self.arrays, self.token = _split_call(name + "_start", [*shards, *lands], [],
                                                         [n * N_NEAR, n * N_NEAR, n], after, emit)

    def forward(self, after):
        n, kinds, sizes = self.n, self.kinds, self.sizes

        def emit(arr, old, new):
            x, y, c, _ = _mesh_place()
            near = _near(x, y, c)
            for k in range(n):
                for j in (1, 2, 3):
                    dev, idx = near[j]
                    landed = _window(arr[n + k], kinds[k], idx, sizes[k])
                    _remote(arr[k], landed, old[0], old[1], k * N_NEAR + j, dev).wait_recv()
                    _remote(landed, landed, new[0], new[1], k * N_NEAR + j, near[0][0]).start()

        new, self.arrays, self.token = _split_call(self.name + "_forward", self.arrays, self.sems, [n * N_NEAR] * 2,
                                                   after, emit)
        self.sems = [*self.sems, *new]

    def finish(self, after):
        n, kinds, sizes = self.n, self.kinds, self.sizes

        def emit(arr, old, _):
            x, y, c, me = _mesh_place()
            near = _near(x, y, c)
            other_core = near[0][0]
            for k in range(n):
                win = lambda idx: _window(arr[n + k], kinds[k], idx, sizes[k])
                pltpu.make_async_copy(arr[k], win(me), old[2].at[k]).wait()
                for j, (dev, idx) in enumerate(near):
                    _remote(arr[k], win(me), old[0], old[1], k * N_NEAR + j, dev).wait_send()
                _remote(arr[k], win(near[0][1]), old[0], old[1], k * N_NEAR, other_core).wait_recv()
                for j in (1, 2, 3):
                    idx = near[j][1]
                    _remote(win(idx), win(idx), old[3], old[4], k * N_NEAR + j, other_core).wait_send()
                    _remote(arr[k], win(idx + 1 - 2 * c), old[3], old[4], k * N_NEAR + j, other_core).wait_recv()

        _, arrays, _ = _split_call(self.name + "_finish", self.arrays, self.sems, [], after, emit)
        return arrays[n:]


class _Scatter:
    def __init__(self, partials, kinds, after, name):
        self.n, self.kinds, self.name, self.partials = len(partials), kinds, name, partials
        self.sizes = [p.shape[k] // N_DEV for p, k in zip(partials, kinds)]
        n, sizes = self.n, self.sizes
        self.slot_shapes = []
        for p, k, size in zip(partials, kinds, sizes):
            dims = list(p.shape)
            dims[k] = size
            self.slot_shapes.append((N_NEAR, *dims))
        slots = [lax.empty(sh, p.dtype) for sh, p in zip(self.slot_shapes, partials)]

        def emit(arr, _, new):
            x, y, c, _ = _mesh_place()
            near = _near(x, y, c)
            for k in range(n):
                for j in range(N_NEAR):
                    owner = near[j][1] if j == 0 else near[j][1] + 1 - 2 * c
                    _remote(_window(arr[k], kinds[k], owner, sizes[k]), arr[n + k].at[j], new[0], new[1],
                            k * N_NEAR + j, near[0][0]).start()

        self.sems, self.arrays, self.token = _split_call(name + "_start", [*partials, *slots], [], [n * N_NEAR] * 2,
                                                         after, emit)

    def combine_and_send(self, own4, after):
        n, kinds, sizes = self.n, self.kinds, self.sizes

        def emit_wait(arr, old, _):
            x, y, c, _ = _mesh_place()
            near = _near(x, y, c)
            for k in range(n):
                for j in range(N_NEAR):
                    owner = near[j][1] if j == 0 else near[j][1] + 1 - 2 * c
                    cp = _remote(_window(arr[k], kinds[k], owner, sizes[k]), arr[n + k].at[j], old[0], old[1],
                                 k * N_NEAR + j, near[0][0])
                    cp.wait_send()
                    cp.wait_recv()

        _, arrays, _ = _split_call(self.name + "_landed", self.arrays, self.sems, [], after, emit_wait)
        chip_sums = _chip_sums(arrays[:n], arrays[n:], kinds, sizes, own4, self.name + "_combine")
        arrivals = [lax.empty((N_NEAR - 1, *sh[1:]), p.dtype) for sh, p in zip(self.slot_shapes, self.partials)]

        def emit_send(arr, _, new):
            x, y, c, _ = _mesh_place()
            near = _near(x, y, c)
            for k in range(n):
                for j in (1, 2, 3):
                    _remote(arr[k].at[j], arr[n + k].at[j - 1], new[0], new[1], k * N_NEAR + j, near[j][0]).start()

        self.sems, self.arrays, self.token = _split_call(self.name + "_send", [*chip_sums, *arrivals], [],
                                                         [n * N_NEAR] * 2, own4, emit_send)

    def finish(self, after):
        n = self.n

        def emit(arr, old, _):
            x, y, c, _ = _mesh_place()
            near = _near(x, y, c)
            for k in range(n):
                for j in (1, 2, 3):
                    cp = _remote(arr[k].at[j], arr[n + k].at[j - 1], old[0], old[1], k * N_NEAR + j, near[j][0])
                    cp.wait_send()
                    cp.wait_recv()

        _, arrays, _ = _split_call(self.name + "_finish", self.arrays, self.sems, [], after, emit)
        return arrays[:n], arrays[n:]


def _chip_sums(partials, slots, kinds, sizes, own4, name):
    n = len(partials)

    def body(own_ref, *refs):
        for k in range(n):
            refs[2 * n + k][...] = (refs[k][...].astype(F32) + refs[n + k][...].astype(F32)).astype(BF16)

    in_specs, slot_specs = [], []
    for p, s, kind, size in zip(partials, slots, kinds, sizes):
        block = list(p.shape)
        block[kind] = size
        nd = len(block)
        in_specs.append(pl.BlockSpec(tuple(block), functools.partial(
            lambda j, own, kind, nd: tuple(own[j] if d == kind else 0 for d in range(nd)), kind=kind, nd=nd)))
        slot_specs.append(pl.BlockSpec((None, *block), functools.partial(
            lambda j, own, nd: (j,) + (0,) * nd, nd=nd)))
    return pl.pallas_call(
        body, name=name,
        grid_spec=pltpu.PrefetchScalarGridSpec(num_scalar_prefetch=1, grid=(N_NEAR,),
                                               in_specs=in_specs + slot_specs, out_specs=slot_specs),
        out_shape=[jax.ShapeDtypeStruct(s.shape, s.dtype) for s in slots],
        compiler_params=_params(("arbitrary",)),
    )(own4, *partials, *slots)


def _after(small, token):
    return small + token[0:1, 0:1].astype(small.dtype)


def _silu(c):
    return c * _sigmoid_tail(c)


def _ada_fwd(c_all, w_ada, b_ada_cols):
    def body(c_ref, w_ref, b_ref, out_ref):
        out_ref[...] = jnp.dot(_silu(c_ref[...]), w_ref[...], preferred_element_type=F32,
                               precision=lax.Precision.HIGHEST) + b_ref[...]

    return pl.pallas_call(
        body, name="ada_fwd", out_shape=jax.ShapeDtypeStruct((N_DEV, w_ada.shape[1]), F32),
    )(c_all, w_ada, b_ada_cols)


def _adam(w, g, m, v):
    m = ADAM_B1 * m + (1.0 - ADAM_B1) * g
    v = ADAM_B2 * v + (1.0 - ADAM_B2) * (g * g)
    m_hat = m / (1.0 - ADAM_B1 ** ADAM_STEP)
    v_hat = v / (1.0 - ADAM_B2 ** ADAM_STEP)
    delta = -ADAM_LR * (m_hat / (jnp.sqrt(v_hat) + ADAM_EPS) + ADAM_WD * w)
    return delta, m, v


def _ada_bwd_adam(c_all, dmod_cols, w, m, v):
    def body(c_ref, d_ref, w_ref, m_ref, v_ref, g_ref, delta_ref, nm_ref, nv_ref):
        g = lax.dot_general(_silu(c_ref[...]), d_ref[...], (((0,), (0,)), ((), ())),
                            preferred_element_type=F32, precision=lax.Precision.HIGHEST)
        g_ref[...] = g
        delta_ref[...], nm_ref[...], nv_ref[...] = _adam(w_ref[...], g, m_ref[...], v_ref[...])

    sd = jax.ShapeDtypeStruct(w.shape, F32)
    return pl.pallas_call(body, name="ada_bwd_adam", out_shape=[sd] * 4,
                          compiler_params=pltpu.CompilerParams(vmem_limit_bytes=V7X_VMEM_LIMIT),
                          )(c_all, dmod_cols, w, m, v)


def _adam_group(chip_sums, arrivals, ws, ms, vs, n_tiles, name):
    n = len(ws)

    def body(*refs):
        for k in range(n):
            c_ref, a_ref, w_ref, m_ref, v_ref = (refs[j * n + k] for j in range(5))
            g_ref, delta_ref, nm_ref, nv_ref = (refs[(5 + j) * n + k] for j in range(4))
            g = c_ref[...].astype(F32)
            for j in range(N_NEAR - 1):
                g = g + a_ref[j].astype(F32)
            g_ref[...] = g
            delta_ref[...], nm_ref[...], nv_ref[...] = _adam(w_ref[...], g, m_ref[...], v_ref[...])

    tiles = [(w.shape[0] // n_tiles, w.shape[1]) for w in ws]
    blk = [pl.BlockSpec(t, lambda i: (i, 0)) for t in tiles]
    return pl.pallas_call(
        body, name=name, grid=(n_tiles,),
        in_specs=[pl.BlockSpec((None, *t), lambda i: (0, i, 0)) for t in tiles]
        + [pl.BlockSpec((N_NEAR - 1, *t), lambda i: (0, i, 0)) for t in tiles] + blk * 3,
        out_specs=blk * 4, out_shape=[jax.ShapeDtypeStruct(w.shape, F32) for w in ws] * 4,
        compiler_params=_params(("parallel",)),
    )(*chip_sums, *arrivals, *ws, *ms, *vs)


N_SMALL = 40
N_SMALL_PARAMS = 11


def _pack_vecs(conv_w_full, rows):
    def body(cw_ref, *refs):
        out = refs[-1]
        out[...] = jnp.zeros_like(out)
        out[0:4, :] = cw_ref[0:4, :]
        for r, ref in enumerate(refs[:-1]):
            out[4 + r:5 + r, :] = ref[...]

    return pl.pallas_call(body, name="pack_vecs", out_shape=jax.ShapeDtypeStruct((16, D), F32))(conv_w_full, *rows)


def _small_finish(gathered, mod_all, vecs, ws, ms, vs):
    n = N_SMALL_PARAMS

    def body(g_ref, mod_ref, vec_ref, *refs):
        w_refs, m_refs, v_refs = refs[:n], refs[n:2 * n], refs[2 * n:3 * n]
        outs = refs[3 * n:]
        g1 = vec_ref[V_G1:V_G1 + 1, :]
        g2 = vec_ref[V_G2:V_G2 + 1, :]
        zero = jnp.zeros((1, D), F32)
        dg1, dg2, dgf, loss_lanes = zero, zero, zero, zero
        mixer = jnp.zeros((16, D), F32)
        db_ada = jnp.zeros((6, D), F32)
        for b in range(N_DEV):
            gb = g_ref[b]
            mod = mod_ref[b]
            q1 = gb[33:34]
            q2 = gb[9:10]
            dmod = jnp.concatenate([gb[32:33], q1 * g1, gb[10:11], gb[8:9], q2 * g2, gb[1:2]], axis=0)
            outs[4 * n][b] = dmod
            db_ada = db_ada + dmod
            dg1 = dg1 + q1 * (1.0 + mod[M_SC1:M_SC1 + 1])
            dg2 = dg2 + q2 * (1.0 + mod[M_SC2:M_SC2 + 1])
            dgf = dgf + gb[0:1]
            loss_lanes = loss_lanes + gb[2:3]
            mixer = mixer + gb[16:32]
        d_a_param = mixer[7:8] * _sigmoid_tail(vec_ref[V_A_PARAM:V_A_PARAM + 1, :])
        grads = [dg1, dg2, mixer[4:5], mixer[5:6], mixer[6:7], d_a_param, mixer[8:9], mixer[9:10], dgf,
                 db_ada, mixer[0:4]]
        for k in range(n):
            outs[k][...] = grads[k]
            outs[n + k][...], outs[2 * n + k][...], outs[3 * n + k][...] = _adam(
                w_refs[k][...], grads[k], m_refs[k][...], v_refs[k][...])
        outs[4 * n + 1][...] = jnp.broadcast_to(jnp.sum(loss_lanes, axis=1, keepdims=True), (8, 128))

    shapes = [jax.ShapeDtypeStruct(w.shape, F32) for w in ws]
    return pl.pallas_call(
        body, name="small_finish",
        out_shape=shapes * 4 + [jax.ShapeDtypeStruct((N_DEV, 6, D), F32), jax.ShapeDtypeStruct((8, 128), F32)],
    )(gathered, mod_all, vecs, *ws, *ms, *vs)


def _pad_rows(a, rows):
    return jnp.pad(a, ((0, rows - a.shape[0]), (0, 0)))


def kernel(x, c, norm_mix_g, norm_mlp_g, w_ada, b_ada, w_in, conv_w, conv_b, w_rg_a, b_rg_a, w_rg_x, b_rg_x, a_param, w_branch_a, w_pool, b_pool, pool_scale, w_branch_b, w_out, w_up, w_down, final_g, loss_target, m_norm_mix_g, m_norm_mlp_g, m_w_ada, m_b_ada, m_w_in, m_conv_w, m_conv_b, m_w_rg_a, m_b_rg_a, m_w_rg_x, m_b_rg_x, m_a_param, m_w_branch_a, m_w_pool, m_b_pool, m_pool_scale, m_w_branch_b, m_w_out, m_w_up, m_w_down, m_final_g, v_norm_mix_g, v_norm_mlp_g, v_w_ada, v_b_ada, v_w_in, v_conv_w, v_conv_b, v_w_rg_a, v_b_rg_a, v_w_rg_x, v_b_rg_x, v_a_param, v_w_branch_a, v_w_pool, v_b_pool, v_pool_scale, v_w_branch_b, v_w_out, v_w_up, v_w_down, v_final_g):
    me = 4 * lax.axis_index("x") + 2 * lax.axis_index("y") + lax.axis_index("c")
    s = x.shape[1]
    x2d = x.reshape(s, D)
    target = loss_target.reshape(s, D)
    n_ada = w_ada.shape[2]

    sharded = dict(w_in=(w_in[0], 1), w_up=(w_up[0], 1), w_down=(w_down[0], 0), w_branch_a=(w_branch_a[0], 0),
                   w_branch_b=(w_branch_b[0], 0), w_out=(w_out[0], 0), w_rg_a=(w_rg_a[0], 1), w_rg_x=(w_rg_x[0], 1),
                   w_pool=(w_pool[0], 1))
    kind = {k: v[1] for k, v in sharded.items()}
    shard = {k: v[0].astype(BF16) for k, v in sharded.items()}

    first_names = ["w_in", "w_rg_a", "w_rg_x", "w_pool"]
    branch_names = ["w_branch_a", "w_branch_b", "w_out"]
    mlp_names = ["w_up", "w_down"]

    def gather(group, after, name):
        return _Gather([shard[k] for k in group], [kind[k] for k in group], after, name)

    g_first = gather(first_names, c, "gather_first")

    conv_w_full, c_rows = _all_gather([_pad_rows(conv_w[0], 8), _pad_rows(c, 8)], [1, 0], "gather_c",
                                      dep=g_first.token)
    c_all = c_rows.reshape(N_DEV, 8, D)[:, 0, :]
    b_ada_cols = lax.dynamic_slice(b_ada, (0, me * n_ada), (1, n_ada))
    mod_part = _ada_fwd(c_all, w_ada[0], b_ada_cols)
    mod_parts, = _all_gather([mod_part], [0], "gather_mod")
    g_branch = gather(branch_names, mod_parts, "gather_branch")
    g_mlp = gather(mlp_names, g_branch.token, "gather_mlp")

    mod_all = jnp.transpose(mod_parts.reshape(N_DEV, N_DEV, n_ada), (1, 0, 2)).reshape(N_DEV, 6, D)
    mod_all = jnp.pad(mod_all, ((0, 0), (0, 2), (0, 0)))
    modr = lax.dynamic_index_in_dim(mod_all, me, 0, keepdims=False)
    vecs = _pack_vecs(conv_w_full, [conv_b, b_rg_a, b_rg_x, a_param, b_pool, pool_scale,
                                    norm_mix_g, norm_mlp_g, final_g.reshape(1, D)])
    vecs = _after(vecs, g_mlp.token)
    g_first.forward(vecs)
    wg = dict(zip(first_names, g_first.finish(g_first.token)))

    h1, x_rnn, u_pool, ga, dga, sa, sb = _proj_fwd(x2d, modr, vecs, wg["w_in"])
    g_branch.forward(h1)
    xr, hr, za, p, pooled, *gates = _mix_fwd(x_rnn, u_pool, ga, _after(vecs, g_branch.token),
                                             wg["w_rg_a"], wg["w_rg_x"], wg["w_pool"])
    g_mlp.forward(za)
    wg.update(zip(branch_names, g_branch.finish(g_mlp.token)))
    ba, bb, merged, o, x2, h2 = _branch_fwd(za, pooled, sa, sb, x2d, modr, vecs,
                                            wg["w_branch_a"], wg["w_branch_b"], wg["w_out"])
    wg.update(zip(mlp_names, g_mlp.finish(h2)))
    ru, dx3, d_dn, small_f = _mlp_fwd(h2, x2, target, modr, vecs, wg["w_up"], wg["w_down"])

    near = _near(lax.axis_index("x"), lax.axis_index("y"), lax.axis_index("c"))
    own4 = jnp.stack([me, near[1][1], near[2][1], near[3][1]]).astype(jnp.int32)

    def scatter(group, partial, after, name):
        return _Scatter([partial[k] for k in group], [kind[k] for k in group], after, name)

    dup, dx2, do, small_m = _mlp_bwd(d_dn, ru, x2, dx3, o, modr, vecs, wg["w_up"], wg["w_down"])
    partial = dict(w_up=_wgrad(h2, dup, "wgrad_up"), w_down=_wgrad(ru, d_dn, "wgrad_down", square_a=True))
    s_mlp = scatter(mlp_names, partial, dx2, "scatter_mlp")

    dba, dbb, dgates, dza, dpooled = _branch_bwd(do, sa, sb, ba, bb, wg["w_branch_a"], wg["w_branch_b"], wg["w_out"],
                                                 dep=s_mlp.token)
    s_mlp.combine_and_send(own4, dza)
    dproj, dw_rg_a, dw_rg_x, dw_pool, small_x = _mix_bwd(dza, dpooled, x_rnn, ga, dga, xr, hr, p, gates, dgates,
                                                         _after(vecs, s_mlp.token),
                                                         wg["w_rg_a"], wg["w_rg_x"], wg["w_pool"])
    partial.update(w_branch_a=_wgrad(za, dba, "wgrad_branch_a"), w_branch_b=_wgrad(pooled, dbb, "wgrad_branch_b"),
                   w_out=_wgrad(merged, do, "wgrad_out"),
                   w_rg_a=dw_rg_a, w_rg_x=dw_rg_x, w_pool=dw_pool)
    mixer_names = ["w_rg_a", "w_rg_x", "w_pool", "w_branch_a", "w_branch_b", "w_out"]
    s_mixer = scatter(mixer_names, partial, s_mlp.token, "scatter_mixer")

    partial["w_in"] = _wgrad(h1, dproj, "wgrad_in", dep=s_mixer.token)
    s_in = scatter(["w_in"], partial, s_mixer.token, "scatter_in")
    s_mixer.combine_and_send(own4, s_in.token)
    s_in.combine_and_send(own4, s_mixer.token)
    grad_x, small_p = _proj_bwd(dproj, x2d, dx2, _after(modr, s_in.token), vecs, wg["w_in"])

    locals_ = dict(w_in=(w_in, m_w_in, v_w_in), w_up=(w_up, m_w_up, v_w_up), w_down=(w_down, m_w_down, v_w_down),
                   w_branch_a=(w_branch_a, m_w_branch_a, v_w_branch_a),
                   w_branch_b=(w_branch_b, m_w_branch_b, v_w_branch_b), w_out=(w_out, m_w_out, v_w_out),
                   w_rg_a=(w_rg_a, m_w_rg_a, v_w_rg_a), w_rg_x=(w_rg_x, m_w_rg_x, v_w_rg_x),
                   w_pool=(w_pool, m_w_pool, v_w_pool))
    res = {}

    def finish(group, exchange, after, n_tiles, name):
        chip_sums, arrivals = exchange.finish(after)
        flat = lambda t: t.reshape(-1, t.shape[-1])
        shapes = [flat(locals_[k][0]).shape for k in group]
        outs = _adam_group([cs.reshape(N_NEAR, *sh) for cs, sh in zip(chip_sums, shapes)],
                           [ar.reshape(N_NEAR - 1, *sh) for ar, sh in zip(arrivals, shapes)],
                           *[[flat(locals_[k][j]) for k in group] for j in range(3)], n_tiles, name)
        for i, k in enumerate(group):
            res[k] = [outs[j * len(group) + i].reshape(locals_[k][0].shape) for j in range(4)]
        return res[group[-1]][0]

    small = jnp.concatenate([small_f, small_m, small_x, small_p], axis=0)
    g_small = _Gather([small], [0], grad_x, "gather_small")
    done = finish(mlp_names, s_mlp, g_small.token, 4, "adam_mlp")
    g_small.forward(done)
    done = finish(mixer_names, s_mixer, g_small.token, 2, "adam_mixer")
    done = finish(["w_in"], s_in, done, 4, "adam_in")
    small_all, = g_small.finish(done)
    small_all = small_all.reshape(N_DEV, N_SMALL, D)

    def embed(cw):
        return lax.dynamic_update_slice(jnp.zeros((4, D), F32), cw[0], (0, me * (D // N_DEV)))

    def smalls(ng, nl, cb, bra, brx, ap, bp, ps, fg, ba_, cw):
        return [ng, nl, cb, bra, brx, ap, bp, ps, fg.reshape(1, D), ba_.reshape(6, D), embed(cw)]

    small_names = ["norm_mix_g", "norm_mlp_g", "conv_b", "b_rg_a", "b_rg_x", "a_param", "b_pool", "pool_scale",
                   "final_g", "b_ada", "conv_w"]
    fin = _small_finish(
        small_all, mod_all, vecs,
        smalls(norm_mix_g, norm_mlp_g, conv_b, b_rg_a, b_rg_x, a_param, b_pool, pool_scale, final_g, b_ada, conv_w),
        smalls(m_norm_mix_g, m_norm_mlp_g, m_conv_b, m_b_rg_a, m_b_rg_x, m_a_param, m_b_pool, m_pool_scale,
               m_final_g, m_b_ada, m_conv_w),
        smalls(v_norm_mix_g, v_norm_mlp_g, v_conv_b, v_b_rg_a, v_b_rg_x, v_a_param, v_b_pool, v_pool_scale,
               v_final_g, v_b_ada, v_conv_w))
    dmod_all, loss_tile = fin[4 * N_SMALL_PARAMS], fin[4 * N_SMALL_PARAMS + 1]
    dmod_cols = lax.dynamic_slice(dmod_all.reshape(N_DEV, 6 * D), (0, me * n_ada), (N_DEV, n_ada))
    res["w_ada"] = [t.reshape(w_ada.shape) for t in _ada_bwd_adam(c_all, dmod_cols, w_ada[0], m_w_ada[0], v_w_ada[0])]

    def final_shape(k, t):
        if k == "final_g":
            return t.reshape(D)
        if k == "b_ada":
            return t.reshape(1, 6 * D)
        if k == "conv_w":
            return lax.dynamic_slice(t, (0, me * (D // N_DEV)), (4, D // N_DEV)).reshape(conv_w.shape)
        return t

    for i, k in enumerate(small_names):
        res[k] = [final_shape(k, fin[which * N_SMALL_PARAMS + i]) for which in range(4)]
    order = ["norm_mix_g", "norm_mlp_g", "w_ada", "b_ada", "w_in", "conv_w", "conv_b", "w_rg_a", "b_rg_a", "w_rg_x",
             "b_rg_x", "a_param", "w_branch_a", "w_pool", "b_pool", "pool_scale", "w_branch_b", "w_out", "w_up",
             "w_down", "final_g"]
    outs = [loss_tile[0, 0], grad_x.reshape(x.shape)]
    for which in range(4):
        for k in order:
            outs.append(res[k][which])
    return tuple(outs)
```

```python
import functools

import jax
import jax.numpy as jnp
from jax import lax
from jax.experimental import pallas as pl
from jax.experimental.pallas import tpu as pltpu

F32 = jnp.float32
BF16 = jnp.bfloat16
MESH = pl.DeviceIdType.MESH

N_DEV = 8
D = 1024
N_GROUPS = 4
GW = D // N_GROUPS
D_IN = 5 * D
D_FF = 4 * D
POOL_WINDOWS = (2, 4, 8, 16)
HALO_X = 8
HALO_U = 16
EPS = 1e-6
C_RG = 8.0
ADAM_LR, ADAM_B1, ADAM_B2, ADAM_EPS, ADAM_WD, ADAM_STEP = 0.001, 0.9, 0.999, 1e-08, 0.01, 10

V7X_VMEM_LIMIT = 56 * 1024 * 1024

V_CONV_W, V_CONV_B, V_B_RG_A, V_B_RG_X, V_A_PARAM, V_B_POOL, V_POOL_SCALE, V_G1, V_G2, V_GF = 0, 4, 5, 6, 7, 8, 9, 10, 11, 12
M_SH1, M_SC1, M_GT1, M_SH2, M_SC2, M_GT2 = 0, 1, 2, 3, 4, 5

TM_PROJ = 512
TM_MIX = 256
TM_BRANCH = 256
TM_MLP = 512
TM_MLP_BWD = 256
TS_WGRAD = 1024


def _params(semantics):
    return pltpu.CompilerParams(dimension_semantics=semantics, vmem_limit_bytes=V7X_VMEM_LIMIT)


def _resident(shape):
    return pl.BlockSpec(shape, lambda *_: (0,) * len(shape), pipeline_mode=pl.Buffered(1))


def _dot(a, b):
    return jnp.dot(a, b, preferred_element_type=F32)


def _dot_nt(a, b):
    return lax.dot_general(a, b, (((1,), (1,)), ((), ())), preferred_element_type=F32)


def _dot_tn(a, b):
    return lax.dot_general(a, b, (((0,), (0,)), ((), ())), preferred_element_type=F32)


def _sigmoid(x):
    return 0.5 * jnp.tanh(0.5 * x) + 0.5


def _sigmoid_tail(x):
    return 1.0 / (1.0 + jnp.exp(-x))


def _gelu_and_grad(x):
    k = 0.7978845608028654
    x2 = x * x
    t = jnp.tanh(k * (x + 0.044715 * x * x2))
    g = 0.5 * x * (1.0 + t)
    dg = 0.5 * (1.0 + t) + 0.5 * x * (1.0 - t * t) * (k * (1.0 + 3.0 * 0.044715 * x2))
    return g, dg


def _softplus(a):
    e = jnp.exp(-jnp.abs(a))
    u = 1.0 + e
    log1p_e = jnp.where(u == 1.0, e, jnp.log(u) * e / jnp.where(u == 1.0, 1.0, u - 1.0))
    return jnp.maximum(a, 0.0) + log1p_e


def _neg_expm1(z):
    series = -(z * (1.0 + z * (0.5 + z * (1.0 / 6.0 + z * (1.0 / 24.0 + z * (1.0 / 120.0))))))
    return jnp.where(z > -0.1, series, 1.0 - jnp.exp(z))


def _shift_down(x, k):
    return pltpu.roll(x, k, 0)


def _shift_up(x, k):
    return pltpu.roll(x, x.shape[0] - k, 0)


def _rglru_gates(xr, w_a, w_x, b_a, b_x, a_param, is_t0):
    xb = xr.astype(BF16)
    ra = _sigmoid(_dot(xb, w_a) + b_a)
    ri = _sigmoid(_dot(xb, w_x) + b_x)
    sp = _softplus(a_param)
    log_a = (-C_RG) * ra * sp
    a = jnp.exp(log_a)
    mult = jnp.where(is_t0, 1.0, jnp.sqrt(_neg_expm1(2.0 * log_a)))
    return ra, ri, sp, a, mult


SUBLANES = 8


LANES = 128


def _scan_strip(a, b, carry, scr, down):
    t = b.shape[0]
    g = t // SUBLANES
    a3 = a.reshape(g, SUBLANES, LANES)
    b3 = b.reshape(g, SUBLANES, LANES)
    sub = lax.broadcasted_iota(jnp.int32, (g, SUBLANES, LANES), 1)
    for k in (1, 2, 4):
        keep = sub >= k if down else sub < SUBLANES - k
        shift = k if down else SUBLANES - k
        b3 = b3 + a3 * jnp.where(keep, pltpu.roll(b3, shift, 1), 0.0)
        a3 = a3 * jnp.where(keep, pltpu.roll(a3, shift, 1), 1.0)
    scr[0] = a3.reshape(t, LANES)
    scr[1] = b3.reshape(t, LANES)
    end_row = SUBLANES - 1 if down else 0
    ag = scr[0, pl.ds(end_row, g, stride=SUBLANES), :]
    bg = scr[1, pl.ds(end_row, g, stride=SUBLANES), :]
    rg = lax.broadcasted_iota(jnp.int32, (g, LANES), 0)
    edge = 0 if down else g - 1
    bg = bg + jnp.where(rg == edge, ag * carry, 0.0)
    k = 1
    while k < g:
        keep = rg >= k if down else rg < g - k
        shift = k if down else g - k
        bg = bg + ag * jnp.where(keep, pltpu.roll(bg, shift, 0), 0.0)
        if 2 * k < g:
            ag = ag * pltpu.roll(ag, shift, 0)
        k *= 2
    entering = jnp.where(rg != edge, pltpu.roll(bg, 1 if down else g - 1, 0), carry)
    for r in range(SUBLANES):
        scr[2, pl.ds(r, g, stride=SUBLANES), :] = entering
    return scr[1] + scr[0] * scr[2], bg[g - 1:g, :]


def _scan_strips(a, b, carry, scr, down):
    outs = [_scan_strip(a[:, c:c + LANES], b[:, c:c + LANES], carry[:, c:c + LANES], scr, down)
            for c in range(0, b.shape[1], LANES)]
    return jnp.concatenate([o[0] for o in outs], axis=1), jnp.concatenate([o[1] for o in outs], axis=1)


def _scan_down(a, b, carry, scr):
    return _scan_strips(a, b, carry, scr, True)


def _scan_up(m, b, carry, scr):
    return _scan_strips(m, b, carry, scr, False)[0]


def _window_mean(sums, window, first_block, head_t):
    scaled = sums * (1.0 / window)
    head = jnp.where(first_block, sums[:HALO_U] / jnp.minimum(head_t, float(window)), scaled[:HALO_U])
    return jnp.concatenate([head, scaled[HALO_U:]], axis=0)


def _conv_taps(x_ext):
    return [_shift_down(x_ext, 3 - j)[HALO_X:] if j < 3 else x_ext[HALO_X:] for j in range(4)]


def _proj_fwd(x, modr, vecs, w_in):
    s = x.shape[0]
    tm = min(TM_PROJ, s)

    def body(x_ref, mod_ref, vec_ref, w_ref, h1_ref, xrnn_ref, u_ref, ga_ref, dga_ref, sa_ref, sb_ref):
        xv = x_ref[...]
        r = lax.rsqrt(jnp.mean(xv * xv, axis=-1, keepdims=True) + EPS)
        gain = vec_ref[V_G1:V_G1 + 1, :] * (1.0 + mod_ref[M_SC1:M_SC1 + 1, :])
        h = (xv * r * gain + mod_ref[M_SH1:M_SH1 + 1, :]).astype(BF16)
        h1_ref[...] = h
        xrnn_ref[...] = _dot(h, w_ref[:, 0:D])
        ga_ref[...], dga_ref[...] = _gelu_and_grad(_dot(h, w_ref[:, D:2 * D]))
        u_ref[...] = _dot(h, w_ref[:, 2 * D:3 * D])
        sa_ref[...] = _sigmoid(_dot(h, w_ref[:, 3 * D:4 * D]))
        sb_ref[...] = _sigmoid(_dot(h, w_ref[:, 4 * D:5 * D]))

    tok = pl.BlockSpec((tm, D), lambda i: (i, 0))
    sd = lambda dt: jax.ShapeDtypeStruct((s, D), dt)
    return pl.pallas_call(
        body, name="proj_fwd", grid=(s // tm,),
        in_specs=[tok, pl.BlockSpec((8, D), lambda i: (0, 0)), pl.BlockSpec((16, D), lambda i: (0, 0)),
                  _resident((D, D_IN))],
        out_specs=[tok] * 7,
        out_shape=[sd(BF16)] + [sd(F32)] * 6,
        compiler_params=_params(("parallel",)),
    )(x, modr, vecs, w_in)


def _mix_fwd(x_rnn, u_pool, ga, vecs, w_rg_a, w_rg_x, w_pool):
    s = x_rnn.shape[0]
    tm = min(TM_MIX, s)
    nb = s // tm

    def body(xh_ref, x_ref, uh_ref, u_ref, ga_ref, vec_ref, wa_ref, wx_ref, wp_ref,
             xr_ref, hr_ref, za_ref, p_ref, pooled_ref, a_ref, mult_ref, ra_ref, ri_ref, carry_ref, scan_scr):
        i = pl.program_id(0)
        first = i == 0

        @pl.when(first)
        def _():
            carry_ref[...] = jnp.zeros_like(carry_ref)

        row = lax.broadcasted_iota(jnp.int32, (tm, GW), 0)
        is_t0 = jnp.logical_and(first, row == 0)
        head_t = (lax.broadcasted_iota(jnp.int32, (HALO_U, GW), 0) + 1).astype(F32)
        for g in range(N_GROUPS):
            cs = slice(g * GW, (g + 1) * GW)
            vec = vec_ref[:, cs]
            xh = jnp.where(first, 0.0, xh_ref[:, cs])
            taps = _conv_taps(jnp.concatenate([xh, x_ref[:, cs]], axis=0))
            xr = vec[V_CONV_B:V_CONV_B + 1]
            for j in range(4):
                xr = xr + vec[V_CONV_W + j:V_CONV_W + j + 1] * taps[j]
            xr_ref[:, cs] = xr
            ra, ri, _, a, mult = _rglru_gates(
                xr, wa_ref[g], wx_ref[g], vec[V_B_RG_A:V_B_RG_A + 1], vec[V_B_RG_X:V_B_RG_X + 1],
                vec[V_A_PARAM:V_A_PARAM + 1], is_t0)
            a_ref[:, cs] = a
            mult_ref[:, cs] = mult
            ra_ref[:, cs] = ra.astype(BF16)
            ri_ref[:, cs] = ri.astype(BF16)
            h, last = _scan_down(a, xr * ri * mult, carry_ref[0:1, cs], scan_scr)
            hr_ref[:, cs] = h
            carry_ref[0:1, cs] = last
            za_ref[:, cs] = (ga_ref[:, cs] * h).astype(BF16)
            uh = jnp.where(first, 0.0, uh_ref[:, cs])
            sm = jnp.concatenate([uh, u_ref[:, cs]], axis=0)
            k = 1
            while k < POOL_WINDOWS[g]:
                sm = sm + _shift_down(sm, k)
                k *= 2
            mean = _window_mean(sm[HALO_U:], POOL_WINDOWS[g], first, head_t)
            p = (mean - u_ref[:, cs]).astype(BF16)
            p_ref[:, cs] = p
            pb = _dot(p, wp_ref[g]) + vec[V_B_POOL:V_B_POOL + 1]
            pooled_ref[:, cs] = (pb * vec[V_POOL_SCALE:V_POOL_SCALE + 1]).astype(BF16)

    tok = pl.BlockSpec((tm, D), lambda i: (i, 0))
    halo = lambda rows: pl.BlockSpec((rows, D), lambda i: (jnp.maximum(i * (tm // rows) - 1, 0), 0))
    wspec = pl.BlockSpec((N_GROUPS, GW, GW), lambda i: (0, 0, 0))
    sd = lambda dt: jax.ShapeDtypeStruct((s, D), dt)
    return pl.pallas_call(
        body, name="mix_fwd", grid=(nb,),
        in_specs=[halo(HALO_X), tok, halo(HALO_U), tok, tok, pl.BlockSpec((16, D), lambda i: (0, 0)),
                  wspec, wspec, wspec],
        out_specs=[tok] * 9,
        out_shape=[sd(F32), sd(F32), sd(BF16), sd(BF16), sd(BF16), sd(F32), sd(F32), sd(BF16), sd(BF16)],
        scratch_shapes=[pltpu.VMEM((8, D), F32), pltpu.VMEM((3, tm, LANES), F32)],
        compiler_params=_params(("arbitrary",)),
    )(x_rnn, x_rnn, u_pool, u_pool, ga, vecs, w_rg_a, w_rg_x, w_pool)


def _branch_fwd(za, pooled, sa, sb, x, modr, vecs, w_a, w_b, w_out):
    s = x.shape[0]
    tm = min(TM_BRANCH, s)

    def body(za_ref, pooled_ref, sa_ref, sb_ref, x_ref, mod_ref, vec_ref, wa_ref, wb_ref, wo_ref,
             ba_ref, bb_ref, merged_ref, o_ref, x2_ref, h2_ref):
        ba = _dot(za_ref[...], wa_ref[...])
        bb = _dot(pooled_ref[...], wb_ref[...])
        ba_ref[...] = ba.astype(BF16)
        bb_ref[...] = bb.astype(BF16)
        merged = (sa_ref[...] * ba + sb_ref[...] * bb).astype(BF16)
        merged_ref[...] = merged
        o = _dot(merged, wo_ref[...])
        o_ref[...] = o.astype(BF16)
        x2 = x_ref[...] + mod_ref[M_GT1:M_GT1 + 1, :] * o
        x2_ref[...] = x2
        r = lax.rsqrt(jnp.mean(x2 * x2, axis=-1, keepdims=True) + EPS)
        gain = vec_ref[V_G2:V_G2 + 1, :] * (1.0 + mod_ref[M_SC2:M_SC2 + 1, :])
        h2_ref[...] = (x2 * r * gain + mod_ref[M_SH2:M_SH2 + 1, :]).astype(BF16)

    tok = pl.BlockSpec((tm, D), lambda i: (i, 0))
    wspec = pl.BlockSpec((D, D), lambda i: (0, 0))
    sd = lambda dt: jax.ShapeDtypeStruct((s, D), dt)
    return pl.pallas_call(
        body, name="branch_fwd", grid=(s // tm,),
        in_specs=[tok, tok, tok, tok,
                  tok, pl.BlockSpec((8, D), lambda i: (0, 0)), pl.BlockSpec((16, D), lambda i: (0, 0)),
                  wspec, wspec, wspec],
        out_specs=[tok] * 6,
        out_shape=[sd(BF16), sd(BF16), sd(BF16), sd(BF16), sd(F32), sd(BF16)],
        compiler_params=_params(("parallel",)),
    )(za, pooled, sa, sb, x, modr, vecs, w_a, w_b, w_out)


def _mlp_fwd(h2, x2, target, modr, vecs, w_up, w_down):
    s = x2.shape[0]
    tm = min(TM_MLP, s)

    def body(h2_ref, x2_ref, tgt_ref, mod_ref, vec_ref, wu_ref, wd_ref,
             ru_ref, dx3_ref, ddn_ref, small_ref):
        @pl.when(pl.program_id(0) == 0)
        def _():
            small_ref[...] = jnp.zeros_like(small_ref)

        h2 = h2_ref[...]
        dn = None
        for c in range(D_FF // D):
            cs = slice(c * D, (c + 1) * D)
            ru = jnp.maximum(_dot(h2, wu_ref[:, cs]), 0.0)
            ru_ref[:, cs] = ru.astype(BF16)
            part = _dot((ru * ru).astype(BF16), wd_ref[cs, :])
            dn = part if dn is None else dn + part
        gt2 = mod_ref[M_GT2:M_GT2 + 1, :]
        gf = vec_ref[V_GF:V_GF + 1, :]
        x3 = x2_ref[...] + gt2 * dn
        r3 = lax.rsqrt(jnp.mean(x3 * x3, axis=-1, keepdims=True) + EPS)
        n3 = x3 * r3
        err = n3 * gf - tgt_ref[...]
        dy = err * (1.0 / D)
        dn3 = dy * gf
        dx3 = r3 * (dn3 - n3 * jnp.mean(dn3 * n3, axis=-1, keepdims=True))
        dx3_ref[...] = dx3
        ddn_ref[...] = (dx3 * gt2).astype(BF16)
        small_ref[0:1, :] += jnp.sum(dy * n3, axis=0, keepdims=True)
        small_ref[1:2, :] += jnp.sum(dx3 * dn, axis=0, keepdims=True)
        small_ref[2:3, :] += (0.5 / D) * jnp.sum(err * err, axis=0, keepdims=True)

    tok = pl.BlockSpec((tm, D), lambda i: (i, 0))
    return pl.pallas_call(
        body, name="mlp_fwd", grid=(s // tm,),
        in_specs=[tok, tok, tok,
                  pl.BlockSpec((8, D), lambda i: (0, 0)), pl.BlockSpec((16, D), lambda i: (0, 0)),
                  _resident((D, D_FF)), _resident((D_FF, D))],
        out_specs=[pl.BlockSpec((tm, D_FF), lambda i: (i, 0)), tok, tok,
                   pl.BlockSpec((8, D), lambda i: (0, 0))],
        out_shape=[jax.ShapeDtypeStruct((s, D_FF), BF16), jax.ShapeDtypeStruct((s, D), F32),
                   jax.ShapeDtypeStruct((s, D), BF16), jax.ShapeDtypeStruct((8, D), F32)],
        compiler_params=_params(("arbitrary",)),
    )(h2, x2, target, modr, vecs, w_up, w_down)


def _mlp_bwd(d_dn, ru, x2, dx3, o, modr, vecs, w_up, w_down):
    s = x2.shape[0]
    tm = min(TM_MLP_BWD, s)

    def body(ddn_ref, ru_ref, x2_ref, dx3_ref, o_ref, mod_ref, vec_ref, wu_ref, wd_ref,
             dup_ref, dx2_ref, do_ref, small_ref):
        @pl.when(pl.program_id(0) == 0)
        def _():
            small_ref[...] = jnp.zeros_like(small_ref)

        ddn = ddn_ref[...]
        dh2 = None
        for c in range(D_FF // D):
            cs = slice(c * D, (c + 1) * D)
            dff = _dot_nt(ddn, wd_ref[cs, :])
            dup = (dff * (2.0 * ru_ref[:, cs].astype(F32))).astype(BF16)
            dup_ref[:, cs] = dup
            part = _dot_nt(dup, wu_ref[:, cs])
            dh2 = part if dh2 is None else dh2 + part
        x2 = x2_ref[...]
        r2 = lax.rsqrt(jnp.mean(x2 * x2, axis=-1, keepdims=True) + EPS)
        xn2 = x2 * r2
        gain = vec_ref[V_G2:V_G2 + 1, :] * (1.0 + mod_ref[M_SC2:M_SC2 + 1, :])
        dxn2 = dh2 * gain
        dx2 = dx3_ref[...] + r2 * (dxn2 - xn2 * jnp.mean(dxn2 * xn2, axis=-1, keepdims=True))
        dx2_ref[...] = dx2
        do_ref[...] = (dx2 * mod_ref[M_GT1:M_GT1 + 1, :]).astype(BF16)
        small_ref[0:1, :] += jnp.sum(dh2, axis=0, keepdims=True)
        small_ref[1:2, :] += jnp.sum(dh2 * xn2, axis=0, keepdims=True)
        small_ref[2:3, :] += jnp.sum(dx2 * o_ref[...].astype(F32), axis=0, keepdims=True)

    tok = pl.BlockSpec((tm, D), lambda i: (i, 0))
    wide = pl.BlockSpec((tm, D_FF), lambda i: (i, 0))
    return pl.pallas_call(
        body, name="mlp_bwd", grid=(s // tm,),
        in_specs=[tok, wide, tok, tok, tok,
                  pl.BlockSpec((8, D), lambda i: (0, 0)), pl.BlockSpec((16, D), lambda i: (0, 0)),
                  _resident((D, D_FF)), _resident((D_FF, D))],
        out_specs=[wide, tok, tok, pl.BlockSpec((8, D), lambda i: (0, 0))],
        out_shape=[jax.ShapeDtypeStruct((s, D_FF), BF16), jax.ShapeDtypeStruct((s, D), F32),
                   jax.ShapeDtypeStruct((s, D), BF16), jax.ShapeDtypeStruct((8, D), F32)],
        compiler_params=_params(("arbitrary",)),
    )(d_dn, ru, x2, dx3, o, modr, vecs, w_up, w_down)


def _branch_bwd(do, sa, sb, ba, bb, w_a, w_b, w_out, dep):
    s = do.shape[0]
    tm = min(TM_BRANCH, s)

    def body(do_ref, sa_ref, sb_ref, ba_ref, bb_ref, wa_ref, wb_ref, wo_ref, dep_ref,
             dba_ref, dbb_ref, dg_ref, dza_ref, dpooled_ref):
        dmerged = _dot_nt(do_ref[...], wo_ref[...])
        sa = sa_ref[...]
        sb = sb_ref[...]
        dba = (dmerged * sa).astype(BF16)
        dbb = (dmerged * sb).astype(BF16)
        dba_ref[...] = dba
        dbb_ref[...] = dbb
        dg_ref[:, :D] = (dmerged * ba_ref[...].astype(F32) * sa * (1.0 - sa)).astype(BF16)
        dg_ref[:, D:] = (dmerged * bb_ref[...].astype(F32) * sb * (1.0 - sb)).astype(BF16)
        dza_ref[...] = _dot_nt(dba, wa_ref[...])
        dpooled_ref[...] = _dot_nt(dbb, wb_ref[...])

    tok = pl.BlockSpec((tm, D), lambda i: (i, 0))
    wspec = pl.BlockSpec((D, D), lambda i: (0, 0))
    sd = lambda dt: jax.ShapeDtypeStruct((s, D), dt)
    return pl.pallas_call(
        body, name="branch_bwd", grid=(s // tm,),
        in_specs=[tok, tok, tok, tok, tok, wspec, wspec, wspec, pl.BlockSpec(memory_space=pl.ANY)],
        out_specs=[tok, tok, pl.BlockSpec((tm, 2 * D), lambda i: (i, 0)), tok, tok],
        out_shape=[sd(BF16), sd(BF16), jax.ShapeDtypeStruct((s, 2 * D), BF16), sd(F32), sd(F32)],
        compiler_params=_params(("parallel",)),
    )(do, sa, sb, ba, bb, w_a, w_b, w_out, dep)


def _mix_bwd(dza, dpooled, x_rnn, ga, dga, xr, hr, p, gates, dgates, vecs, w_rg_a, w_rg_x, w_pool):
    s = xr.shape[0]
    tm = min(TM_MIX, s)
    nb = s // tm

    def body(dza_ref, dpooled_ref, xh_ref, x_ref, ga_ref, dga_ref, xr_ref, hh_ref, hr_ref, p_ref,
             a_ref, mult_ref, ra_ref, ri_ref, dg_ref, vec_ref, wa_ref, wx_ref, wp_ref,
             dproj_ref, dwa_ref, dwx_ref, dwp_ref, small_ref,
             scan_carry, dxr_carry, q_carry, scan_scr, dwa_acc, dwx_acc, dwp_acc):
        i = pl.program_id(0)
        bi = nb - 1 - i
        first_t = bi == 0

        @pl.when(i == 0)
        def _():
            scan_carry[...] = jnp.zeros_like(scan_carry)
            dxr_carry[...] = jnp.zeros_like(dxr_carry)
            q_carry[...] = jnp.zeros_like(q_carry)
            dwa_acc[...] = jnp.zeros_like(dwa_acc)
            dwx_acc[...] = jnp.zeros_like(dwx_acc)
            dwp_acc[...] = jnp.zeros_like(dwp_acc)
            small_ref[...] = jnp.zeros_like(small_ref)

        row = lax.broadcasted_iota(jnp.int32, (tm, GW), 0)
        is_t0 = jnp.logical_and(first_t, row == 0)
        head_t = (lax.broadcasted_iota(jnp.int32, (HALO_U, GW), 0) + 1).astype(F32)
        colsum = lambda v: jnp.sum(v, axis=0, keepdims=True)
        for g in range(N_GROUPS):
            cs = slice(g * GW, (g + 1) * GW)
            vec = vec_ref[:, cs]
            xr = xr_ref[:, cs]
            hr = hr_ref[:, cs]
            dza = dza_ref[:, cs]
            dproj_ref[:, D + g * GW:D + (g + 1) * GW] = (dza * hr * dga_ref[:, cs]).astype(BF16)
            dhr = dza * ga_ref[:, cs]
            a = a_ref[:, cs]
            mult = mult_ref[:, cs]
            ra = ra_ref[:, cs].astype(F32)
            ri = ri_ref[:, cs].astype(F32)
            sp = _softplus(vec[V_A_PARAM:V_A_PARAM + 1])
            m = jnp.where(row == tm - 1, 1.0, _shift_up(a, 1))
            gsum = _scan_up(m, dhr, scan_carry[0:1, cs], scan_scr)
            scan_carry[0:1, cs] = a[0:1, :] * gsum[0:1, :]
            hh = jnp.where(first_t, 0.0, hh_ref[:, cs])
            hprev = _shift_down(jnp.concatenate([hh, hr], axis=0), 1)[8:]
            da = gsum * hprev
            dmult = jnp.where(is_t0, 0.0, gsum * xr * ri)
            dlog_a = da * a - dmult * a * a / mult
            dri = gsum * xr * mult
            dxr = gsum * ri * mult
            small_ref[7:8, cs] += colsum((-C_RG) * ra * dlog_a)
            dpa = (((-C_RG) * sp) * dlog_a * ra * (1.0 - ra))
            dpx = dri * ri * (1.0 - ri)
            small_ref[5:6, cs] += colsum(dpa)
            small_ref[6:7, cs] += colsum(dpx)
            dpa = dpa.astype(BF16)
            dpx = dpx.astype(BF16)
            xrb = xr.astype(BF16)
            dwa_acc[g] += _dot_tn(xrb, dpa)
            dwx_acc[g] += _dot_tn(xrb, dpx)
            dxr = dxr + _dot_nt(dpa, wa_ref[g]) + _dot_nt(dpx, wx_ref[g])
            small_ref[4:5, cs] += colsum(dxr)
            xh = jnp.where(first_t, 0.0, xh_ref[:, cs])
            taps = _conv_taps(jnp.concatenate([xh, x_ref[:, cs]], axis=0))
            dxr_ext = jnp.concatenate([dxr, dxr_carry[:, cs]], axis=0)
            dx = vec[V_CONV_W + 3:V_CONV_W + 4] * dxr
            for j in range(4):
                small_ref[j:j + 1, cs] += colsum(dxr * taps[j])
                if j < 3:
                    dx = dx + vec[V_CONV_W + j:V_CONV_W + j + 1] * _shift_up(dxr_ext, 3 - j)[:tm]
            dxr_carry[:, cs] = dxr[0:8, :]
            dproj_ref[:, cs] = dx.astype(BF16)
            pg = p_ref[:, cs]
            dpooled = dpooled_ref[:, cs]
            pb = _dot(pg, wp_ref[g]) + vec[V_B_POOL:V_B_POOL + 1]
            small_ref[9:10, cs] += colsum(dpooled * pb)
            dpb = dpooled * vec[V_POOL_SCALE:V_POOL_SCALE + 1]
            small_ref[8:9, cs] += colsum(dpb)
            dpbb = dpb.astype(BF16)
            dwp_acc[g] += _dot_tn(pg, dpbb)
            dp = _dot_nt(dpbb, wp_ref[g])
            q = _window_mean(dp, POOL_WINDOWS[g], first_t, head_t)
            sm = jnp.concatenate([q, q_carry[:, cs]], axis=0)
            k = 1
            while k < POOL_WINDOWS[g]:
                sm = sm + _shift_up(sm, k)
                k *= 2
            q_carry[:, cs] = q[0:HALO_U, :]
            dproj_ref[:, 2 * D + g * GW:2 * D + (g + 1) * GW] = (sm[:tm] - dp).astype(BF16)
        dproj_ref[:, 3 * D:] = dg_ref[...]

        @pl.when(i == nb - 1)
        def _():
            dwa_ref[...] = dwa_acc[...].astype(BF16)
            dwx_ref[...] = dwx_acc[...].astype(BF16)
            dwp_ref[...] = dwp_acc[...].astype(BF16)

    rev = lambda i: nb - 1 - i
    tok = pl.BlockSpec((tm, D), lambda i: (rev(i), 0))
    halo8 = lambda k: pl.BlockSpec((8, D), lambda i: (jnp.maximum(rev(i) * (tm // 8) - 1, 0), k))
    wspec = pl.BlockSpec((N_GROUPS, GW, GW), lambda i: (0, 0, 0))
    wshape = jax.ShapeDtypeStruct((N_GROUPS, GW, GW), BF16)
    return pl.pallas_call(
        body, name="mix_bwd", grid=(nb,),
        in_specs=[tok, tok, halo8(0), tok, tok, tok, tok, halo8(0), tok, tok, tok, tok, tok, tok,
                  pl.BlockSpec((tm, 2 * D), lambda i: (rev(i), 0)),
                  pl.BlockSpec((16, D), lambda i: (0, 0)), wspec, wspec, wspec],
        out_specs=[pl.BlockSpec((tm, D_IN), lambda i: (rev(i), 0)), wspec, wspec, wspec,
                   pl.BlockSpec((16, D), lambda i: (0, 0))],
        out_shape=[jax.ShapeDtypeStruct((s, D_IN), BF16), wshape, wshape, wshape,
                   jax.ShapeDtypeStruct((16, D), F32)],
        scratch_shapes=[pltpu.VMEM((8, D), F32), pltpu.VMEM((8, D), F32), pltpu.VMEM((HALO_U, D), F32),
                        pltpu.VMEM((3, tm, LANES), F32)] + [pltpu.VMEM((N_GROUPS, GW, GW), F32)] * 3,
        compiler_params=_params(("arbitrary",)),
    )(dza, dpooled, x_rnn, x_rnn, ga, dga, xr, hr, hr, p, *gates, dgates, vecs, w_rg_a, w_rg_x, w_pool)


def _proj_bwd(dproj, x, dx2, modr, vecs, w_in):
    s = x.shape[0]
    tm = min(TM_PROJ, s)

    def body(dp_ref, x_ref, dx2_ref, mod_ref, vec_ref, w_ref, gx_ref, small_ref):
        @pl.when(pl.program_id(0) == 0)
        def _():
            small_ref[...] = jnp.zeros_like(small_ref)

        dh1 = None
        for c in range(D_IN // D):
            cs = slice(c * D, (c + 1) * D)
            part = _dot_nt(dp_ref[:, cs], w_ref[:, cs])
            dh1 = part if dh1 is None else dh1 + part
        xv = x_ref[...]
        r1 = lax.rsqrt(jnp.mean(xv * xv, axis=-1, keepdims=True) + EPS)
        xn1 = xv * r1
        gain = vec_ref[V_G1:V_G1 + 1, :] * (1.0 + mod_ref[M_SC1:M_SC1 + 1, :])
        dxn1 = dh1 * gain
        gx_ref[...] = dx2_ref[...] + r1 * (dxn1 - xn1 * jnp.mean(dxn1 * xn1, axis=-1, keepdims=True))
        small_ref[0:1, :] += jnp.sum(dh1, axis=0, keepdims=True)
        small_ref[1:2, :] += jnp.sum(dh1 * xn1, axis=0, keepdims=True)

    tok = pl.BlockSpec((tm, D), lambda i: (i, 0))
    return pl.pallas_call(
        body, name="proj_bwd", grid=(s // tm,),
        in_specs=[pl.BlockSpec((tm, D_IN), lambda i: (i, 0)), tok, tok,
                  pl.BlockSpec((8, D), lambda i: (0, 0)), pl.BlockSpec((16, D), lambda i: (0, 0)),
                  _resident((D, D_IN))],
        out_specs=[tok, pl.BlockSpec((8, D), lambda i: (0, 0))],
        out_shape=[jax.ShapeDtypeStruct((s, D), F32), jax.ShapeDtypeStruct((8, D), F32)],
        compiler_params=_params(("arbitrary",)),
    )(dproj, x, dx2, modr, vecs, w_in)


def _wgrad(a, b, name, square_a=False, dep=None):
    s, ka = a.shape
    n = b.shape[1]
    tka = ka if ka <= 1024 else ka // 2
    tn = n if n <= 1024 else n // 2
    ts = min(TS_WGRAD, s)
    ns = s // ts
    nc = 512
    deps = [] if dep is None else [dep]

    def body(a_ref, b_ref, *refs):
        out_ref, acc_ref = refs[-2:]
        t = pl.program_id(2)

        @pl.when(t == 0)
        def _():
            acc_ref[...] = jnp.zeros_like(acc_ref)

        av = a_ref[...]
        if square_a:
            af = av.astype(F32)
            av = (af * af).astype(BF16)
        for c in range(tn // nc):
            cs = slice(c * nc, (c + 1) * nc)
            acc_ref[:, cs] += _dot_tn(av, b_ref[:, cs])

        @pl.when(t == ns - 1)
        def _():
            out_ref[...] = acc_ref[...].astype(BF16)

    return pl.pallas_call(
        body, name=name, grid=(ka // tka, n // tn, ns),
        in_specs=[pl.BlockSpec((ts, tka), lambda i, j, t: (t, i)),
                  pl.BlockSpec((ts, tn), lambda i, j, t: (t, j))] + [pl.BlockSpec(memory_space=pl.ANY)] * len(deps),
        out_specs=pl.BlockSpec((tka, tn), lambda i, j, t: (i, j)),
        out_shape=jax.ShapeDtypeStruct((ka, n), BF16),
        scratch_shapes=[pltpu.VMEM((tka, tn), F32)],
        compiler_params=_params(("parallel", "parallel", "arbitrary")),
    )(a, b, *deps)


def _window(ref, kind, idx, size):
    start = pl.multiple_of(idx * size, size)
    if kind == 0:
        return ref.at[pl.ds(start, size)]
    if kind == 1:
        return ref.at[:, pl.ds(start, size)]
    return ref.at[:, :, pl.ds(start, size)]


def _mesh_place():
    x, y, c = lax.axis_index("x"), lax.axis_index("y"), lax.axis_index("c")
    return x, y, c, 4 * x + 2 * y + c


def _peer(x, y, c, q):
    px = 1 - x if q & 4 else x
    py = 1 - y if q & 2 else y
    pc = 1 - c if q & 1 else c
    return (px, py, pc), 4 * px + 2 * py + pc


def _all_gather(shards, kinds, name, dep=None):
    n = len(shards)
    deps = [] if dep is None else [dep]
    full_shapes = []
    for sh, kind in zip(shards, kinds):
        dims = list(sh.shape)
        dims[kind] *= N_DEV
        full_shapes.append(jax.ShapeDtypeStruct(tuple(dims), sh.dtype))

    def body(*refs):
        ins, outs = refs[:n], refs[n + len(deps):2 * n + len(deps)]
        send_sems, recv_sems, local_sems = refs[2 * n + len(deps):]
        x, y, c, me = _mesh_place()
        sends, recvs, locals_ = [], [], []
        for k in range(n):
            size = shards[k].shape[kinds[k]]
            mine = _window(outs[k], kinds[k], me, size)
            lc = pltpu.make_async_copy(ins[k], mine, local_sems.at[k])
            lc.start()
            locals_.append(lc)
            for q in range(1, N_DEV):
                peer, peer_idx = _peer(x, y, c, q)
                cp = pltpu.make_async_remote_copy(
                    src_ref=ins[k], dst_ref=mine, send_sem=send_sems.at[k, q], recv_sem=recv_sems.at[k, q],
                    device_id=peer, device_id_type=MESH)
                cp.start()
                sends.append(cp)
                recvs.append(pltpu.make_async_remote_copy(
                    src_ref=ins[k], dst_ref=_window(outs[k], kinds[k], peer_idx, size),
                    send_sem=send_sems.at[k, q], recv_sem=recv_sems.at[k, q],
                    device_id=peer, device_id_type=MESH))
        for cp in recvs:
            cp.wait_recv()
        for cp in sends:
            cp.wait_send()
        for lc in locals_:
            lc.wait()

    any_spec = pl.BlockSpec(memory_space=pl.ANY)
    return pl.pallas_call(
        body, name=name,
        in_specs=[any_spec] * (n + len(deps)), out_specs=[any_spec] * n, out_shape=full_shapes,
        scratch_shapes=[pltpu.SemaphoreType.DMA((n, N_DEV)), pltpu.SemaphoreType.DMA((n, N_DEV)),
                        pltpu.SemaphoreType.DMA((n,))],
    )(*shards, *deps)


_HBM = pl.BlockSpec(memory_space=pltpu.HBM)
_SEM = pl.BlockSpec(memory_space=pltpu.SEMAPHORE)
_EFFECT = pltpu.SideEffectType.DATAFLOW_SIDE_EFFECTING


N_NEAR = 4


def _near(x, y, c):
    out = [((x, y, 1 - c), 4 * x + 2 * y + 1 - c)]
    for j in (1, 2, 3):
        px = 1 - x if j & 2 else x
        py = 1 - y if j & 1 else y
        out.append(((px, py, c), 4 * px + 2 * py + c))
    return out


def _remote(src, dst, send_sems, recv_sems, slot, device):
    return pltpu.make_async_remote_copy(src_ref=src, dst_ref=dst, send_sem=send_sems.at[slot], recv_sem=recv_sems.at[slot],
                                        device_id=device, device_id_type=MESH)


def _split_call(name, arrays, sems_in, n_new_sems, after, emit):
    na, ns, nn = len(arrays), len(sems_in), len(n_new_sems)

    def body(*refs):
        emit(refs[:na], refs[na:na + ns], refs[na + ns + 1:na + ns + 1 + nn])
        refs[-1][...] = jnp.zeros_like(refs[-1])

    outs = pl.pallas_call(
        body, name=name,
        out_shape=(*[pltpu.SemaphoreType.DMA((m,)) for m in n_new_sems],
                   *[pltpu.HBM(a.shape, a.dtype) for a in arrays], jax.ShapeDtypeStruct((8, 128), F32)),
        in_specs=[_HBM] * na + [_SEM] * ns + [pl.BlockSpec(memory_space=pl.ANY)],
        out_specs=(*[_SEM] * nn, *[_HBM] * na, pl.BlockSpec(memory_space=pltpu.VMEM)),
        input_output_aliases={i: nn + i for i in range(na)},
        compiler_params=pltpu.CompilerParams(has_side_effects=_EFFECT),
    )(*[pltpu.with_memory_space_constraint(a, pltpu.HBM) for a in arrays], *sems_in, after)
    return list(outs[:nn]), list(outs[nn:nn + na]), outs[-1]


class _Gather:
    def __init__(self, shards, kinds, after, name):
        self.n, self.kinds, self.name = len(shards), kinds, name
        self.sizes = [s.shape[k] for s, k in zip(shards, kinds)]
        n = self.n
        lands = []
        for s, k in zip(shards, kinds):
            dims = list(s.shape)
            dims[k] *= N_DEV
            lands.append(lax.empty(tuple(dims), s.dtype))

        def emit(arr, _, new):
            x, y, c, me = _mesh_place()
            for k in range(n):
                pltpu.make_async_copy(arr[k], _window(arr[n + k], kinds[k], me, self.sizes[k]), new[2].at[k]).start()
            for k in range(n):
                mine = _window(arr[n + k], kinds[k], me, self.sizes[k])
                for j, (dev, _) in enumerate(_near(x, y, c)):
                    _remote(arr[k], mine, new[0], new[1], k * N_NEAR + j, dev).start()

        self.sems, self.arrays, self.token = _split_call(name + "_start", [*shards, *lands], [],
                                                         [n * N_NEAR, n * N_NEAR, n], after, emit)

    def forward(self, after):
        n, kinds, sizes = self.n, self.kinds, self.sizes

        def emit(arr, old, new):
            x, y, c, _ = _mesh_place()
            near = _near(x, y, c)
            for k in range(n):
                for j in (1, 2, 3):
                    dev, idx = near[j]
                    landed = _window(arr[n + k], kinds[k], idx, sizes[k])
                    _remote(arr[k], landed, old[0], old[1], k * N_NEAR + j, dev).wait_recv()
                    _remote(landed, landed, new[0], new[1], k * N_NEAR + j, near[0][0]).start()

        new, self.arrays, self.token = _split_call(self.name + "_forward", self.arrays, self.sems, [n * N_NEAR] * 2,
                                                   after, emit)
        self.sems = [*self.sems, *new]

    def finish(self, after):
        n, kinds, sizes = self.n, self.kinds, self.sizes

        def emit(arr, old, _):
            x, y, c, me = _mesh_place()
            near = _near(x, y, c)
            other_core = near[0][0]
            for k in range(n):
                win = lambda idx: _window(arr[n + k], kinds[k], idx, sizes[k])
                pltpu.make_async_copy(arr[k], win(me), old[2].at[k]).wait()
                for j, (dev, idx) in enumerate(near):
                    _remote(arr[k], win(me), old[0], old[1], k * N_NEAR + j, dev).wait_send()
                _remote(arr[k], win(near[0][1]), old[0], old[1], k * N_NEAR, other_core).wait_recv()
                for j in (1, 2, 3):
                    idx = near[j][1]
                    _remote(win(idx), win(idx), old[3], old[4], k * N_NEAR + j, other_core).wait_send()
                    _remote(arr[k], win(idx + 1 - 2 * c), old[3], old[4], k * N_NEAR + j, other_core).wait_recv()

        _, arrays, _ = _split_call(self.name + "_finish", self.arrays, self.sems, [], after, emit)
        return arrays[n:]


class _Scatter:
    def __init__(self, partials, kinds, after, name):
        self.n, self.kinds, self.name, self.partials = len(partials), kinds, name, partials
        self.sizes = [p.shape[k] // N_DEV for p, k in zip(partials, kinds)]
        n, sizes = self.n, self.sizes
        self.slot_shapes = []
        for p, k, size in zip(partials, kinds, sizes):
            dims = list(p.shape)
            dims[k] = size
            self.slot_shapes.append((N_NEAR, *dims))
        slots = [lax.empty(sh, p.dtype) for sh, p in zip(self.slot_shapes, partials)]

        def emit(arr, _, new):
            x, y, c, _ = _mesh_place()
            near = _near(x, y, c)
            for k in range(n):
                for j in range(N_NEAR):
                    owner = near[j][1] if j == 0 else near[j][1] + 1 - 2 * c
                    _remote(_window(arr[k], kinds[k], owner, sizes[k]), arr[n + k].at[j], new[0], new[1],
                            k * N_NEAR + j, near[0][0]).start()

        self.sems, self.arrays, self.token = _split_call(name + "_start", [*partials, *slots], [], [n * N_NEAR] * 2,
                                                         after, emit)

    def combine_and_send(self, own4, after):
        n, kinds, sizes = self.n, self.kinds, self.sizes

        def emit_wait(arr, old, _):
            x, y, c, _ = _mesh_place()
            near = _near(x, y, c)
            for k in range(n):
                for j in range(N_NEAR):
                    owner = near[j][1] if j == 0 else near[j][1] + 1 - 2 * c
                    cp = _remote(_window(arr[k], kinds[k], owner, sizes[k]), arr[n + k].at[j], old[0], old[1],
                                 k * N_NEAR + j, near[0][0])
                    cp.wait_send()
                    cp.wait_recv()

        _, arrays, _ = _split_call(self.name + "_landed", self.arrays, self.sems, [], after, emit_wait)
        chip_sums = _chip_sums(arrays[:n], arrays[n:], kinds, sizes, own4, self.name + "_combine")
        arrivals = [lax.empty((N_NEAR - 1, *sh[1:]), p.dtype) for sh, p in zip(self.slot_shapes, self.partials)]

        def emit_send(arr, _, new):
            x, y, c, _ = _mesh_place()
            near = _near(x, y, c)
            for k in range(n):
                for j in (1, 2, 3):
                    _remote(arr[k].at[j], arr[n + k].at[j - 1], new[0], new[1], k * N_NEAR + j, near[j][0]).start()

        self.sems, self.arrays, self.token = _split_call(self.name + "_send", [*chip_sums, *arrivals], [],
                                                         [n * N_NEAR] * 2, own4, emit_send)

    def finish(self, after):
        n = self.n

        def emit(arr, old, _):
            x, y, c, _ = _mesh_place()
            near = _near(x, y, c)
            for k in range(n):
                for j in (1, 2, 3):
                    cp = _remote(arr[k].at[j], arr[n + k].at[j - 1], old[0], old[1], k * N_NEAR + j, near[j][0])
                    cp.wait_send()
                    cp.wait_recv()

        _, arrays, _ = _split_call(self.name + "_finish", self.arrays, self.sems, [], after, emit)
        return arrays[:n], arrays[n:]


def _chip_sums(partials, slots, kinds, sizes, own4, name):
    n = len(partials)

    def body(own_ref, *refs):
        for k in range(n):
            refs[2 * n + k][...] = (refs[k][...].astype(F32) + refs[n + k][...].astype(F32)).astype(BF16)

    in_specs, slot_specs = [], []
    for p, s, kind, size in zip(partials, slots, kinds, sizes):
        block = list(p.shape)
        block[kind] = size
        nd = len(block)
        in_specs.append(pl.BlockSpec(tuple(block), functools.partial(
            lambda j, own, kind, nd: tuple(own[j] if d == kind else 0 for d in range(nd)), kind=kind, nd=nd)))
        slot_specs.append(pl.BlockSpec((None, *block), functools.partial(
            lambda j, own, nd: (j,) + (0,) * nd, nd=nd)))
    return pl.pallas_call(
        body, name=name,
        grid_spec=pltpu.PrefetchScalarGridSpec(num_scalar_prefetch=1, grid=(N_NEAR,),
                                               in_specs=in_specs + slot_specs, out_specs=slot_specs),
        out_shape=[jax.ShapeDtypeStruct(s.shape, s.dtype) for s in slots],
        compiler_params=_params(("arbitrary",)),
    )(own4, *partials, *slots)


def _after(small, token):
    return small + token[0:1, 0:1].astype(small.dtype)


def _silu(c):
    return c * _sigmoid_tail(c)


def _ada_fwd(c_all, w_ada, b_ada_cols):
    def body(c_ref, w_ref, b_ref, out_ref):
        out_ref[...] = jnp.dot(_silu(c_ref[...]), w_ref[...], preferred_element_type=F32,
                               precision=lax.Precision.HIGHEST) + b_ref[...]

    return pl.pallas_call(
        body, name="ada_fwd", out_shape=jax.ShapeDtypeStruct((N_DEV, w_ada.shape[1]), F32),
    )(c_all, w_ada, b_ada_cols)


def _adam(w, g, m, v):
    m = ADAM_B1 * m + (1.0 - ADAM_B1) * g
    v = ADAM_B2 * v + (1.0 - ADAM_B2) * (g * g)
    m_hat = m / (1.0 - ADAM_B1 ** ADAM_STEP)
    v_hat = v / (1.0 - ADAM_B2 ** ADAM_STEP)
    delta = -ADAM_LR * (m_hat / (jnp.sqrt(v_hat) + ADAM_EPS) + ADAM_WD * w)
    return delta, m, v


def _ada_bwd_adam(c_all, dmod_cols, w, m, v):
    def body(c_ref, d_ref, w_ref, m_ref, v_ref, g_ref, delta_ref, nm_ref, nv_ref):
        g = lax.dot_general(_silu(c_ref[...]), d_ref[...], (((0,), (0,)), ((), ())),
                            preferred_element_type=F32, precision=lax.Precision.HIGHEST)
        g_ref[...] = g
        delta_ref[...], nm_ref[...], nv_ref[...] = _adam(w_ref[...], g, m_ref[...], v_ref[...])

    sd = jax.ShapeDtypeStruct(w.shape, F32)
    return pl.pallas_call(body, name="ada_bwd_adam", out_shape=[sd] * 4,
                          compiler_params=pltpu.CompilerParams(vmem_limit_bytes=V7X_VMEM_LIMIT),
                          )(c_all, dmod_cols, w, m, v)


def _adam_group(chip_sums, arrivals, ws, ms, vs, n_tiles, name):
    n = len(ws)

    def body(*refs):
        for k in range(n):
            c_ref, a_ref, w_ref, m_ref, v_ref = (refs[j * n + k] for j in range(5))
            g_ref, delta_ref, nm_ref, nv_ref = (refs[(5 + j) * n + k] for j in range(4))
            g = c_ref[...].astype(F32)
            for j in range(N_NEAR - 1):
                g = g + a_ref[j].astype(F32)
            g_ref[...] = g
            delta_ref[...], nm_ref[...], nv_ref[...] = _adam(w_ref[...], g, m_ref[...], v_ref[...])

    tiles = [(w.shape[0] // n_tiles, w.shape[1]) for w in ws]
    blk = [pl.BlockSpec(t, lambda i: (i, 0)) for t in tiles]
    return pl.pallas_call(
        body, name=name, grid=(n_tiles,),
        in_specs=[pl.BlockSpec((None, *t), lambda i: (0, i, 0)) for t in tiles]
        + [pl.BlockSpec((N_NEAR - 1, *t), lambda i: (0, i, 0)) for t in tiles] + blk * 3,
        out_specs=blk * 4, out_shape=[jax.ShapeDtypeStruct(w.shape, F32) for w in ws] * 4,
        compiler_params=_params(("parallel",)),
    )(*chip_sums, *arrivals, *ws, *ms, *vs)


N_SMALL = 40
N_SMALL_PARAMS = 11


def _pack_vecs(conv_w_full, rows):
    def body(cw_ref, *refs):
        out = refs[-1]
        out[...] = jnp.zeros_like(out)
        out[0:4, :] = cw_ref[0:4, :]
        for r, ref in enumerate(refs[:-1]):
            out[4 + r:5 + r, :] = ref[...]

    return pl.pallas_call(body, name="pack_vecs", out_shape=jax.ShapeDtypeStruct((16, D), F32))(conv_w_full, *rows)


def _small_finish(gathered, mod_all, vecs, ws, ms, vs):
    n = N_SMALL_PARAMS

    def body(g_ref, mod_ref, vec_ref, *refs):
        w_refs, m_refs, v_refs = refs[:n], refs[n:2 * n], refs[2 * n:3 * n]
        outs = refs[3 * n:]
        g1 = vec_ref[V_G1:V_G1 + 1, :]
        g2 = vec_ref[V_G2:V_G2 + 1, :]
        zero = jnp.zeros((1, D), F32)
        dg1, dg2, dgf, loss_lanes = zero, zero, zero, zero
        mixer = jnp.zeros((16, D), F32)
        db_ada = jnp.zeros((6, D), F32)
        for b in range(N_DEV):
            gb = g_ref[b]
            mod = mod_ref[b]
            q1 = gb[33:34]
            q2 = gb[9:10]
            dmod = jnp.concatenate([gb[32:33], q1 * g1, gb[10:11], gb[8:9], q2 * g2, gb[1:2]], axis=0)
            outs[4 * n][b] = dmod
            db_ada = db_ada + dmod
            dg1 = dg1 + q1 * (1.0 + mod[M_SC1:M_SC1 + 1])
            dg2 = dg2 + q2 * (1.0 + mod[M_SC2:M_SC2 + 1])
            dgf = dgf + gb[0:1]
            loss_lanes = loss_lanes + gb[2:3]
            mixer = mixer + gb[16:32]
        d_a_param = mixer[7:8] * _sigmoid_tail(vec_ref[V_A_PARAM:V_A_PARAM + 1, :])
        grads = [dg1, dg2, mixer[4:5], mixer[5:6], mixer[6:7], d_a_param, mixer[8:9], mixer[9:10], dgf,
                 db_ada, mixer[0:4]]
        for k in range(n):
            outs[k][...] = grads[k]
            outs[n + k][...], outs[2 * n + k][...], outs[3 * n + k][...] = _adam(
                w_refs[k][...], grads[k], m_refs[k][...], v_refs[k][...])
        outs[4 * n + 1][...] = jnp.broadcast_to(jnp.sum(loss_lanes, axis=1, keepdims=True), (8, 128))

    shapes = [jax.ShapeDtypeStruct(w.shape, F32) for w in ws]
    return pl.pallas_call(
        body, name="small_finish",
        out_shape=shapes * 4 + [jax.ShapeDtypeStruct((N_DEV, 6, D), F32), jax.ShapeDtypeStruct((8, 128), F32)],
    )(gathered, mod_all, vecs, *ws, *ms, *vs)


def _pad_rows(a, rows):
    return jnp.pad(a, ((0, rows - a.shape[0]), (0, 0)))


def kernel(x, c, norm_mix_g, norm_mlp_g, w_ada, b_ada, w_in, conv_w, conv_b, w_rg_a, b_rg_a, w_rg_x, b_rg_x, a_param, w_branch_a, w_pool, b_pool, pool_scale, w_branch_b, w_out, w_up, w_down, final_g, loss_target, m_norm_mix_g, m_norm_mlp_g, m_w_ada, m_b_ada, m_w_in, m_conv_w, m_conv_b, m_w_rg_a, m_b_rg_a, m_w_rg_x, m_b_rg_x, m_a_param, m_w_branch_a, m_w_pool, m_b_pool, m_pool_scale, m_w_branch_b, m_w_out, m_w_up, m_w_down, m_final_g, v_norm_mix_g, v_norm_mlp_g, v_w_ada, v_b_ada, v_w_in, v_conv_w, v_conv_b, v_w_rg_a, v_b_rg_a, v_w_rg_x, v_b_rg_x, v_a_param, v_w_branch_a, v_w_pool, v_b_pool, v_pool_scale, v_w_branch_b, v_w_out, v_w_up, v_w_down, v_final_g):
    me = 4 * lax.axis_index("x") + 2 * lax.axis_index("y") + lax.axis_index("c")
    s = x.shape[1]
    x2d = x.reshape(s, D)
    target = loss_target.reshape(s, D)
    n_ada = w_ada.shape[2]

    sharded = dict(w_in=(w_in[0], 1), w_up=(w_up[0], 1), w_down=(w_down[0], 0), w_branch_a=(w_branch_a[0], 0),
                   w_branch_b=(w_branch_b[0], 0), w_out=(w_out[0], 0), w_rg_a=(w_rg_a[0], 1), w_rg_x=(w_rg_x[0], 1),
                   w_pool=(w_pool[0], 1))
    kind = {k: v[1] for k, v in sharded.items()}
    shard = {k: v[0].astype(BF16) for k, v in sharded.items()}

    conv_w_full, c_rows = _all_gather([_pad_rows(conv_w[0], 8), _pad_rows(c, 8)], [1, 0], "gather_c")
    c_all = c_rows.reshape(N_DEV, 8, D)[:, 0, :]
    b_ada_cols = lax.dynamic_slice(b_ada, (0, me * n_ada), (1, n_ada))
    mod_part = _ada_fwd(c_all, w_ada[0], b_ada_cols)
    mod_parts, = _all_gather([mod_part], [0], "gather_mod")

    first_names = ["w_in", "w_rg_a", "w_rg_x", "w_pool"]
    later_names = ["w_branch_a", "w_branch_b", "w_out", "w_up", "w_down"]
    mlp_names = ["w_up", "w_down"]

    def gather(group, after, name):
        return _Gather([shard[k] for k in group], [kind[k] for k in group], after, name)

    g_first = gather(first_names, mod_parts, "gather_first")
    g_later = gather(later_names, g_first.token, "gather_later")

    mod_all = jnp.transpose(mod_parts.reshape(N_DEV, N_DEV, n_ada), (1, 0, 2)).reshape(N_DEV, 6, D)
    mod_all = jnp.pad(mod_all, ((0, 0), (0, 2), (0, 0)))
    modr = lax.dynamic_index_in_dim(mod_all, me, 0, keepdims=False)
    vecs = _pack_vecs(conv_w_full, [conv_b, b_rg_a, b_rg_x, a_param, b_pool, pool_scale,
                                    norm_mix_g, norm_mlp_g, final_g.reshape(1, D)])
    vecs = _after(vecs, g_later.token)
    g_first.forward(vecs)
    wg = dict(zip(first_names, g_first.finish(g_first.token)))

    h1, x_rnn, u_pool, ga, dga, sa, sb = _proj_fwd(x2d, modr, vecs, wg["w_in"])
    g_later.forward(h1)
    xr, hr, za, p, pooled, *gates = _mix_fwd(x_rnn, u_pool, ga, _after(vecs, g_later.token),
                                             wg["w_rg_a"], wg["w_rg_x"], wg["w_pool"])
    wg.update(zip(later_names, g_later.finish(za)))
    ba, bb, merged, o, x2, h2 = _branch_fwd(za, pooled, sa, sb, x2d, modr, vecs,
                                            wg["w_branch_a"], wg["w_branch_b"], wg["w_out"])
    ru, dx3, d_dn, small_f = _mlp_fwd(h2, x2, target, modr, vecs, wg["w_up"], wg["w_down"])

    near = _near(lax.axis_index("x"), lax.axis_index("y"), lax.axis_index("c"))
    own4 = jnp.stack([me, near[1][1], near[2][1], near[3][1]]).astype(jnp.int32)

    def scatter(group, partial, after, name):
        return _Scatter([partial[k] for k in group], [kind[k] for k in group], after, name)

    dup, dx2, do, small_m = _mlp_bwd(d_dn, ru, x2, dx3, o, modr, vecs, wg["w_up"], wg["w_down"])
    partial = dict(w_up=_wgrad(h2, dup, "wgrad_up"), w_down=_wgrad(ru, d_dn, "wgrad_down", square_a=True))
    s_mlp = scatter(mlp_names, partial, dx2, "scatter_mlp")

    dba, dbb, dgates, dza, dpooled = _branch_bwd(do, sa, sb, ba, bb, wg["w_branch_a"], wg["w_branch_b"], wg["w_out"],
                                                 dep=s_mlp.token)
    s_mlp.combine_and_send(own4, dza)
    dproj, dw_rg_a, dw_rg_x, dw_pool, small_x = _mix_bwd(dza, dpooled, x_rnn, ga, dga, xr, hr, p, gates, dgates,
                                                         _after(vecs, s_mlp.token),
                                                         wg["w_rg_a"], wg["w_rg_x"], wg["w_pool"])
    partial.update(w_branch_a=_wgrad(za, dba, "wgrad_branch_a"), w_branch_b=_wgrad(pooled, dbb, "wgrad_branch_b"),
                   w_out=_wgrad(merged, do, "wgrad_out"),
                   w_rg_a=dw_rg_a, w_rg_x=dw_rg_x, w_pool=dw_pool)
    mixer_names = ["w_rg_a", "w_rg_x", "w_pool", "w_branch_a", "w_branch_b", "w_out"]
    s_mixer = scatter(mixer_names, partial, s_mlp.token, "scatter_mixer")

    partial["w_in"] = _wgrad(h1, dproj, "wgrad_in", dep=s_mixer.token)
    s_in = scatter(["w_in"], partial, s_mixer.token, "scatter_in")
    s_mixer.combine_and_send(own4, s_in.token)
    s_in.combine_and_send(own4, s_mixer.token)
    grad_x, small_p = _proj_bwd(dproj, x2d, dx2, _after(modr, s_in.token), vecs, wg["w_in"])

    locals_ = dict(w_in=(w_in, m_w_in, v_w_in), w_up=(w_up, m_w_up, v_w_up), w_down=(w_down, m_w_down, v_w_down),
                   w_branch_a=(w_branch_a, m_w_branch_a, v_w_branch_a),
                   w_branch_b=(w_branch_b, m_w_branch_b, v_w_branch_b), w_out=(w_out, m_w_out, v_w_out),
                   w_rg_a=(w_rg_a, m_w_rg_a, v_w_rg_a), w_rg_x=(w_rg_x, m_w_rg_x, v_w_rg_x),
                   w_pool=(w_pool, m_w_pool, v_w_pool))
    res = {}

    def finish(group, exchange, after, n_tiles, name):
        chip_sums, arrivals = exchange.finish(after)
        flat = lambda t: t.reshape(-1, t.shape[-1])
        shapes = [flat(locals_[k][0]).shape for k in group]
        outs = _adam_group([cs.reshape(N_NEAR, *sh) for cs, sh in zip(chip_sums, shapes)],
                           [ar.reshape(N_NEAR - 1, *sh) for ar, sh in zip(arrivals, shapes)],
                           *[[flat(locals_[k][j]) for k in group] for j in range(3)], n_tiles, name)
        for i, k in enumerate(group):
            res[k] = [outs[j * len(group) + i].reshape(locals_[k][0].shape) for j in range(4)]
        return res[group[-1]][0]

    done = finish(mlp_names, s_mlp, grad_x, 4, "adam_mlp")
    done = finish(mixer_names, s_mixer, done, 2, "adam_mixer")
    done = finish(["w_in"], s_in, done, 4, "adam_in")

    small = jnp.concatenate([small_f, small_m, small_x, small_p], axis=0)
    small_all, = _all_gather([small], [0], "gather_small", dep=done)
    small_all = small_all.reshape(N_DEV, N_SMALL, D)

    def embed(cw):
        return lax.dynamic_update_slice(jnp.zeros((4, D), F32), cw[0], (0, me * (D // N_DEV)))

    def smalls(ng, nl, cb, bra, brx, ap, bp, ps, fg, ba_, cw):
        return [ng, nl, cb, bra, brx, ap, bp, ps, fg.reshape(1, D), ba_.reshape(6, D), embed(cw)]

    small_names = ["norm_mix_g", "norm_mlp_g", "conv_b", "b_rg_a", "b_rg_x", "a_param", "b_pool", "pool_scale",
                   "final_g", "b_ada", "conv_w"]
    fin = _small_finish(
        small_all, mod_all, vecs,
        smalls(norm_mix_g, norm_mlp_g, conv_b, b_rg_a, b_rg_x, a_param, b_pool, pool_scale, final_g, b_ada, conv_w),
        smalls(m_norm_mix_g, m_norm_mlp_g, m_conv_b, m_b_rg_a, m_b_rg_x, m_a_param, m_b_pool, m_pool_scale,
               m_final_g, m_b_ada, m_conv_w),
        smalls(v_norm_mix_g, v_norm_mlp_g, v_conv_b, v_b_rg_a, v_b_rg_x, v_a_param, v_b_pool, v_pool_scale,
               v_final_g, v_b_ada, v_conv_w))
    dmod_all, loss_tile = fin[4 * N_SMALL_PARAMS], fin[4 * N_SMALL_PARAMS + 1]
    dmod_cols = lax.dynamic_slice(dmod_all.reshape(N_DEV, 6 * D), (0, me * n_ada), (N_DEV, n_ada))
    res["w_ada"] = [t.reshape(w_ada.shape) for t in _ada_bwd_adam(c_all, dmod_cols, w_ada[0], m_w_ada[0], v_w_ada[0])]

    def final_shape(k, t):
        if k == "final_g":
            return t.reshape(D)
        if k == "b_ada":
            return t.reshape(1, 6 * D)
        if k == "conv_w":
            return lax.dynamic_slice(t, (0, me * (D // N_DEV)), (4, D // N_DEV)).reshape(conv_w.shape)
        return t

    for i, k in enumerate(small_names):
        res[k] = [final_shape(k, fin[which * N_SMALL_PARAMS + i]) for which in range(4)]
    order = ["norm_mix_g", "norm_mlp_g", "w_ada", "b_ada", "w_in", "conv_w", "conv_b", "w_rg_a", "b_rg_a", "w_rg_x",
             "b_rg_x", "a_param", "w_branch_a", "w_pool", "b_pool", "pool_scale", "w_branch_b", "w_out", "w_up",
             "w_down", "final_g"]
    outs = [loss_tile[0, 0], grad_x.reshape(x.shape)]
    for which in range(4):
        for k in order:
            outs.append(res[k][which])
    return tuple(outs)
```

```python
import functools

import jax
import jax.numpy as jnp
from jax import lax
from jax.experimental import pallas as pl
from jax.experimental.pallas import tpu as pltpu

F32 = jnp.float32
BF16 = jnp.bfloat16
MESH = pl.DeviceIdType.MESH

N_DEV = 8
D = 1024
N_GROUPS = 4
GW = D // N_GROUPS
D_IN = 5 * D
D_FF = 4 * D
POOL_WINDOWS = (2, 4, 8, 16)
HALO_X = 8
HALO_U = 16
EPS = 1e-6
C_RG = 8.0
ADAM_LR, ADAM_B1, ADAM_B2, ADAM_EPS, ADAM_WD, ADAM_STEP = 0.001, 0.9, 0.999, 1e-08, 0.01, 10

V7X_VMEM_LIMIT = 56 * 1024 * 1024

V_CONV_W, V_CONV_B, V_B_RG_A, V_B_RG_X, V_A_PARAM, V_B_POOL, V_POOL_SCALE, V_G1, V_G2, V_GF = 0, 4, 5, 6, 7, 8, 9, 10, 11, 12
M_SH1, M_SC1, M_GT1, M_SH2, M_SC2, M_GT2 = 0, 1, 2, 3, 4, 5

TM_PROJ = 512
TM_MIX = 256
TM_BRANCH = 256
TM_MLP = 512
TM_MLP_BWD = 256
TS_WGRAD = 1024


def _params(semantics):
    return pltpu.CompilerParams(dimension_semantics=semantics, vmem_limit_bytes=V7X_VMEM_LIMIT)


def _resident(shape):
    return pl.BlockSpec(shape, lambda *_: (0,) * len(shape), pipeline_mode=pl.Buffered(1))


def _dot(a, b):
    return jnp.dot(a, b, preferred_element_type=F32)


def _dot_nt(a, b):
    return lax.dot_general(a, b, (((1,), (1,)), ((), ())), preferred_element_type=F32)


def _dot_tn(a, b):
    return lax.dot_general(a, b, (((0,), (0,)), ((), ())), preferred_element_type=F32)


def _sigmoid(x):
    return 0.5 * jnp.tanh(0.5 * x) + 0.5


def _sigmoid_tail(x):
    return 1.0 / (1.0 + jnp.exp(-x))


def _gelu_and_grad(x):
    k = 0.7978845608028654
    x2 = x * x
    t = jnp.tanh(k * (x + 0.044715 * x * x2))
    g = 0.5 * x * (1.0 + t)
    dg = 0.5 * (1.0 + t) + 0.5 * x * (1.0 - t * t) * (k * (1.0 + 3.0 * 0.044715 * x2))
    return g, dg


def _softplus(a):
    e = jnp.exp(-jnp.abs(a))
    u = 1.0 + e
    log1p_e = jnp.where(u == 1.0, e, jnp.log(u) * e / jnp.where(u == 1.0, 1.0, u - 1.0))
    return jnp.maximum(a, 0.0) + log1p_e


def _neg_expm1(z):
    series = -(z * (1.0 + z * (0.5 + z * (1.0 / 6.0 + z * (1.0 / 24.0 + z * (1.0 / 120.0))))))
    return jnp.where(z > -0.1, series, 1.0 - jnp.exp(z))


def _shift_down(x, k):
    return pltpu.roll(x, k, 0)


def _shift_up(x, k):
    return pltpu.roll(x, x.shape[0] - k, 0)


def _rglru_gates(xr, w_a, w_x, b_a, b_x, a_param, is_t0):
    xb = xr.astype(BF16)
    ra = _sigmoid(_dot(xb, w_a) + b_a)
    ri = _sigmoid(_dot(xb, w_x) + b_x)
    sp = _softplus(a_param)
    log_a = (-C_RG) * ra * sp
    a = jnp.exp(log_a)
    mult = jnp.where(is_t0, 1.0, jnp.sqrt(_neg_expm1(2.0 * log_a)))
    return ra, ri, sp, a, mult


SUBLANES = 8


LANES = 128


def _scan_strip(a, b, carry, scr, down):
    t = b.shape[0]
    g = t // SUBLANES
    a3 = a.reshape(g, SUBLANES, LANES)
    b3 = b.reshape(g, SUBLANES, LANES)
    sub = lax.broadcasted_iota(jnp.int32, (g, SUBLANES, LANES), 1)
    for k in (1, 2, 4):
        keep = sub >= k if down else sub < SUBLANES - k
        shift = k if down else SUBLANES - k
        b3 = b3 + a3 * jnp.where(keep, pltpu.roll(b3, shift, 1), 0.0)
        a3 = a3 * jnp.where(keep, pltpu.roll(a3, shift, 1), 1.0)
    scr[0] = a3.reshape(t, LANES)
    scr[1] = b3.reshape(t, LANES)
    end_row = SUBLANES - 1 if down else 0
    ag = scr[0, pl.ds(end_row, g, stride=SUBLANES), :]
    bg = scr[1, pl.ds(end_row, g, stride=SUBLANES), :]
    rg = lax.broadcasted_iota(jnp.int32, (g, LANES), 0)
    edge = 0 if down else g - 1
    bg = bg + jnp.where(rg == edge, ag * carry, 0.0)
    k = 1
    while k < g:
        keep = rg >= k if down else rg < g - k
        shift = k if down else g - k
        bg = bg + ag * jnp.where(keep, pltpu.roll(bg, shift, 0), 0.0)
        if 2 * k < g:
            ag = ag * pltpu.roll(ag, shift, 0)
        k *= 2
    entering = jnp.where(rg != edge, pltpu.roll(bg, 1 if down else g - 1, 0), carry)
    for r in range(SUBLANES):
        scr[2, pl.ds(r, g, stride=SUBLANES), :] = entering
    return scr[1] + scr[0] * scr[2], bg[g - 1:g, :]


def _scan_strips(a, b, carry, scr, down):
    outs = [_scan_strip(a[:, c:c + LANES], b[:, c:c + LANES], carry[:, c:c + LANES], scr, down)
            for c in range(0, b.shape[1], LANES)]
    return jnp.concatenate([o[0] for o in outs], axis=1), jnp.concatenate([o[1] for o in outs], axis=1)


def _scan_down(a, b, carry, scr):
    return _scan_strips(a, b, carry, scr, True)


def _scan_up(m, b, carry, scr):
    return _scan_strips(m, b, carry, scr, False)[0]


def _window_mean(sums, window, first_block, head_t):
    scaled = sums * (1.0 / window)
    head = jnp.where(first_block, sums[:HALO_U] / jnp.minimum(head_t, float(window)), scaled[:HALO_U])
    return jnp.concatenate([head, scaled[HALO_U:]], axis=0)


def _conv_taps(x_ext):
    return [_shift_down(x_ext, 3 - j)[HALO_X:] if j < 3 else x_ext[HALO_X:] for j in range(4)]


def _proj_fwd(x, modr, vecs, w_in):
    s = x.shape[0]
    tm = min(TM_PROJ, s)

    def body(x_ref, mod_ref, vec_ref, w_ref, h1_ref, xrnn_ref, u_ref, ga_ref, dga_ref, sa_ref, sb_ref):
        xv = x_ref[...]
        r = lax.rsqrt(jnp.mean(xv * xv, axis=-1, keepdims=True) + EPS)
        gain = vec_ref[V_G1:V_G1 + 1, :] * (1.0 + mod_ref[M_SC1:M_SC1 + 1, :])
        h = (xv * r * gain + mod_ref[M_SH1:M_SH1 + 1, :]).astype(BF16)
        h1_ref[...] = h
        xrnn_ref[...] = _dot(h, w_ref[:, 0:D])
        ga_ref[...], dga_ref[...] = _gelu_and_grad(_dot(h, w_ref[:, D:2 * D]))
        u_ref[...] = _dot(h, w_ref[:, 2 * D:3 * D])
        sa_ref[...] = _sigmoid(_dot(h, w_ref[:, 3 * D:4 * D]))
        sb_ref[...] = _sigmoid(_dot(h, w_ref[:, 4 * D:5 * D]))

    tok = pl.BlockSpec((tm, D), lambda i: (i, 0))
    sd = lambda dt: jax.ShapeDtypeStruct((s, D), dt)
    return pl.pallas_call(
        body, name="proj_fwd", grid=(s // tm,),
        in_specs=[tok, pl.BlockSpec((8, D), lambda i: (0, 0)), pl.BlockSpec((16, D), lambda i: (0, 0)),
                  _resident((D, D_IN))],
        out_specs=[tok] * 7,
        out_shape=[sd(BF16)] + [sd(F32)] * 6,
        compiler_params=_params(("parallel",)),
    )(x, modr, vecs, w_in)


def _mix_fwd(x_rnn, u_pool, ga, vecs, w_rg_a, w_rg_x, w_pool):
    s = x_rnn.shape[0]
    tm = min(TM_MIX, s)
    nb = s // tm

    def body(xh_ref, x_ref, uh_ref, u_ref, ga_ref, vec_ref, wa_ref, wx_ref, wp_ref,
             xr_ref, hr_ref, za_ref, p_ref, pooled_ref, a_ref, mult_ref, ra_ref, ri_ref, carry_ref, scan_scr):
        i = pl.program_id(0)
        first = i == 0

        @pl.when(first)
        def _():
            carry_ref[...] = jnp.zeros_like(carry_ref)

        row = lax.broadcasted_iota(jnp.int32, (tm, GW), 0)
        is_t0 = jnp.logical_and(first, row == 0)
        head_t = (lax.broadcasted_iota(jnp.int32, (HALO_U, GW), 0) + 1).astype(F32)
        for g in range(N_GROUPS):
            cs = slice(g * GW, (g + 1) * GW)
            vec = vec_ref[:, cs]
            xh = jnp.where(first, 0.0, xh_ref[:, cs])
            taps = _conv_taps(jnp.concatenate([xh, x_ref[:, cs]], axis=0))
            xr = vec[V_CONV_B:V_CONV_B + 1]
            for j in range(4):
                xr = xr + vec[V_CONV_W + j:V_CONV_W + j + 1] * taps[j]
            xr_ref[:, cs] = xr
            ra, ri, _, a, mult = _rglru_gates(
                xr, wa_ref[g], wx_ref[g], vec[V_B_RG_A:V_B_RG_A + 1], vec[V_B_RG_X:V_B_RG_X + 1],
                vec[V_A_PARAM:V_A_PARAM + 1], is_t0)
            a_ref[:, cs] = a
            mult_ref[:, cs] = mult
            ra_ref[:, cs] = ra.astype(BF16)
            ri_ref[:, cs] = ri.astype(BF16)
            h, last = _scan_down(a, xr * ri * mult, carry_ref[0:1, cs], scan_scr)
            hr_ref[:, cs] = h
            carry_ref[0:1, cs] = last
            za_ref[:, cs] = (ga_ref[:, cs] * h).astype(BF16)
            uh = jnp.where(first, 0.0, uh_ref[:, cs])
            sm = jnp.concatenate([uh, u_ref[:, cs]], axis=0)
            k = 1
            while k < POOL_WINDOWS[g]:
                sm = sm + _shift_down(sm, k)
                k *= 2
            mean = _window_mean(sm[HALO_U:], POOL_WINDOWS[g], first, head_t)
            p = (mean - u_ref[:, cs]).astype(BF16)
            p_ref[:, cs] = p
            pb = _dot(p, wp_ref[g]) + vec[V_B_POOL:V_B_POOL + 1]
            pooled_ref[:, cs] = (pb * vec[V_POOL_SCALE:V_POOL_SCALE + 1]).astype(BF16)

    tok = pl.BlockSpec((tm, D), lambda i: (i, 0))
    halo = lambda rows: pl.BlockSpec((rows, D), lambda i: (jnp.maximum(i * (tm // rows) - 1, 0), 0))
    wspec = pl.BlockSpec((N_GROUPS, GW, GW), lambda i: (0, 0, 0))
    sd = lambda dt: jax.ShapeDtypeStruct((s, D), dt)
    return pl.pallas_call(
        body, name="mix_fwd", grid=(nb,),
        in_specs=[halo(HALO_X), tok, halo(HALO_U), tok, tok, pl.BlockSpec((16, D), lambda i: (0, 0)),
                  wspec, wspec, wspec],
        out_specs=[tok] * 9,
        out_shape=[sd(F32), sd(F32), sd(BF16), sd(BF16), sd(BF16), sd(F32), sd(F32), sd(BF16), sd(BF16)],
        scratch_shapes=[pltpu.VMEM((8, D), F32), pltpu.VMEM((3, tm, LANES), F32)],
        compiler_params=_params(("arbitrary",)),
    )(x_rnn, x_rnn, u_pool, u_pool, ga, vecs, w_rg_a, w_rg_x, w_pool)


def _branch_fwd(za, pooled, sa, sb, x, modr, vecs, w_a, w_b, w_out):
    s = x.shape[0]
    tm = min(TM_BRANCH, s)

    def body(za_ref, pooled_ref, sa_ref, sb_ref, x_ref, mod_ref, vec_ref, wa_ref, wb_ref, wo_ref,
             ba_ref, bb_ref, merged_ref, o_ref, x2_ref, h2_ref):
        ba = _dot(za_ref[...], wa_ref[...])
        bb = _dot(pooled_ref[...], wb_ref[...])
        ba_ref[...] = ba.astype(BF16)
        bb_ref[...] = bb.astype(BF16)
        merged = (sa_ref[...] * ba + sb_ref[...] * bb).astype(BF16)
        merged_ref[...] = merged
        o = _dot(merged, wo_ref[...])
        o_ref[...] = o.astype(BF16)
        x2 = x_ref[...] + mod_ref[M_GT1:M_GT1 + 1, :] * o
        x2_ref[...] = x2
        r = lax.rsqrt(jnp.mean(x2 * x2, axis=-1, keepdims=True) + EPS)
        gain = vec_ref[V_G2:V_G2 + 1, :] * (1.0 + mod_ref[M_SC2:M_SC2 + 1, :])
        h2_ref[...] = (x2 * r * gain + mod_ref[M_SH2:M_SH2 + 1, :]).astype(BF16)

    tok = pl.BlockSpec((tm, D), lambda i: (i, 0))
    wspec = pl.BlockSpec((D, D), lambda i: (0, 0))
    sd = lambda dt: jax.ShapeDtypeStruct((s, D), dt)
    return pl.pallas_call(
        body, name="branch_fwd", grid=(s // tm,),
        in_specs=[tok, tok, tok, tok,
                  tok, pl.BlockSpec((8, D), lambda i: (0, 0)), pl.BlockSpec((16, D), lambda i: (0, 0)),
                  wspec, wspec, wspec],
        out_specs=[tok] * 6,
        out_shape=[sd(BF16), sd(BF16), sd(BF16), sd(BF16), sd(F32), sd(BF16)],
        compiler_params=_params(("parallel",)),
    )(za, pooled, sa, sb, x, modr, vecs, w_a, w_b, w_out)


def _mlp_fwd(h2, x2, target, modr, vecs, w_up, w_down):
    s = x2.shape[0]
    tm = min(TM_MLP, s)

    def body(h2_ref, x2_ref, tgt_ref, mod_ref, vec_ref, wu_ref, wd_ref,
             ru_ref, dx3_ref, ddn_ref, small_ref):
        @pl.when(pl.program_id(0) == 0)
        def _():
            small_ref[...] = jnp.zeros_like(small_ref)

        h2 = h2_ref[...]
        dn = None
        for c in range(D_FF // D):
            cs = slice(c * D, (c + 1) * D)
            ru = jnp.maximum(_dot(h2, wu_ref[:, cs]), 0.0)
            ru_ref[:, cs] = ru.astype(BF16)
            part = _dot((ru * ru).astype(BF16), wd_ref[cs, :])
            dn = part if dn is None else dn + part
        gt2 = mod_ref[M_GT2:M_GT2 + 1, :]
        gf = vec_ref[V_GF:V_GF + 1, :]
        x3 = x2_ref[...] + gt2 * dn
        r3 = lax.rsqrt(jnp.mean(x3 * x3, axis=-1, keepdims=True) + EPS)
        n3 = x3 * r3
        err = n3 * gf - tgt_ref[...]
        dy = err * (1.0 / D)
        dn3 = dy * gf
        dx3 = r3 * (dn3 - n3 * jnp.mean(dn3 * n3, axis=-1, keepdims=True))
        dx3_ref[...] = dx3
        ddn_ref[...] = (dx3 * gt2).astype(BF16)
        small_ref[0:1, :] += jnp.sum(dy * n3, axis=0, keepdims=True)
        small_ref[1:2, :] += jnp.sum(dx3 * dn, axis=0, keepdims=True)
        small_ref[2:3, :] += (0.5 / D) * jnp.sum(err * err, axis=0, keepdims=True)

    tok = pl.BlockSpec((tm, D), lambda i: (i, 0))
    return pl.pallas_call(
        body, name="mlp_fwd", grid=(s // tm,),
        in_specs=[tok, tok, tok,
                  pl.BlockSpec((8, D), lambda i: (0, 0)), pl.BlockSpec((16, D), lambda i: (0, 0)),
                  _resident((D, D_FF)), _resident((D_FF, D))],
        out_specs=[pl.BlockSpec((tm, D_FF), lambda i: (i, 0)), tok, tok,
                   pl.BlockSpec((8, D), lambda i: (0, 0))],
        out_shape=[jax.ShapeDtypeStruct((s, D_FF), BF16), jax.ShapeDtypeStruct((s, D), F32),
                   jax.ShapeDtypeStruct((s, D), BF16), jax.ShapeDtypeStruct((8, D), F32)],
        compiler_params=_params(("arbitrary",)),
    )(h2, x2, target, modr, vecs, w_up, w_down)


def _mlp_bwd(d_dn, ru, x2, dx3, o, modr, vecs, w_up, w_down):
    s = x2.shape[0]
    tm = min(TM_MLP_BWD, s)

    def body(ddn_ref, ru_ref, x2_ref, dx3_ref, o_ref, mod_ref, vec_ref, wu_ref, wd_ref,
             dup_ref, dx2_ref, do_ref, small_ref):
        @pl.when(pl.program_id(0) == 0)
        def _():
            small_ref[...] = jnp.zeros_like(small_ref)

        ddn = ddn_ref[...]
        dh2 = None
        for c in range(D_FF // D):
            cs = slice(c * D, (c + 1) * D)
            dff = _dot_nt(ddn, wd_ref[cs, :])
            dup = (dff * (2.0 * ru_ref[:, cs].astype(F32))).astype(BF16)
            dup_ref[:, cs] = dup
            part = _dot_nt(dup, wu_ref[:, cs])
            dh2 = part if dh2 is None else dh2 + part
        x2 = x2_ref[...]
        r2 = lax.rsqrt(jnp.mean(x2 * x2, axis=-1, keepdims=True) + EPS)
        xn2 = x2 * r2
        gain = vec_ref[V_G2:V_G2 + 1, :] * (1.0 + mod_ref[M_SC2:M_SC2 + 1, :])
        dxn2 = dh2 * gain
        dx2 = dx3_ref[...] + r2 * (dxn2 - xn2 * jnp.mean(dxn2 * xn2, axis=-1, keepdims=True))
        dx2_ref[...] = dx2
        do_ref[...] = (dx2 * mod_ref[M_GT1:M_GT1 + 1, :]).astype(BF16)
        small_ref[0:1, :] += jnp.sum(dh2, axis=0, keepdims=True)
        small_ref[1:2, :] += jnp.sum(dh2 * xn2, axis=0, keepdims=True)
        small_ref[2:3, :] += jnp.sum(dx2 * o_ref[...].astype(F32), axis=0, keepdims=True)

    tok = pl.BlockSpec((tm, D), lambda i: (i, 0))
    wide = pl.BlockSpec((tm, D_FF), lambda i: (i, 0))
    return pl.pallas_call(
        body, name="mlp_bwd", grid=(s // tm,),
        in_specs=[tok, wide, tok, tok, tok,
                  pl.BlockSpec((8, D), lambda i: (0, 0)), pl.BlockSpec((16, D), lambda i: (0, 0)),
                  _resident((D, D_FF)), _resident((D_FF, D))],
        out_specs=[wide, tok, tok, pl.BlockSpec((8, D), lambda i: (0, 0))],
        out_shape=[jax.ShapeDtypeStruct((s, D_FF), BF16), jax.ShapeDtypeStruct((s, D), F32),
                   jax.ShapeDtypeStruct((s, D), BF16), jax.ShapeDtypeStruct((8, D), F32)],
        compiler_params=_params(("arbitrary",)),
    )(d_dn, ru, x2, dx3, o, modr, vecs, w_up, w_down)


def _branch_bwd(do, sa, sb, ba, bb, w_a, w_b, w_out, dep):
    s = do.shape[0]
    tm = min(TM_BRANCH, s)

    def body(do_ref, sa_ref, sb_ref, ba_ref, bb_ref, wa_ref, wb_ref, wo_ref, dep_ref,
             dba_ref, dbb_ref, dg_ref, dza_ref, dpooled_ref):
        dmerged = _dot_nt(do_ref[...], wo_ref[...])
        sa = sa_ref[...]
        sb = sb_ref[...]
        dba = (dmerged * sa).astype(BF16)
        dbb = (dmerged * sb).astype(BF16)
        dba_ref[...] = dba
        dbb_ref[...] = dbb
        dg_ref[:, :D] = (dmerged * ba_ref[...].astype(F32) * sa * (1.0 - sa)).astype(BF16)
        dg_ref[:, D:] = (dmerged * bb_ref[...].astype(F32) * sb * (1.0 - sb)).astype(BF16)
        dza_ref[...] = _dot_nt(dba, wa_ref[...])
        dpooled_ref[...] = _dot_nt(dbb, wb_ref[...])

    tok = pl.BlockSpec((tm, D), lambda i: (i, 0))
    wspec = pl.BlockSpec((D, D), lambda i: (0, 0))
    sd = lambda dt: jax.ShapeDtypeStruct((s, D), dt)
    return pl.pallas_call(
        body, name="branch_bwd", grid=(s // tm,),
        in_specs=[tok, tok, tok, tok, tok, wspec, wspec, wspec, pl.BlockSpec(memory_space=pl.ANY)],
        out_specs=[tok, tok, pl.BlockSpec((tm, 2 * D), lambda i: (i, 0)), tok, tok],
        out_shape=[sd(BF16), sd(BF16), jax.ShapeDtypeStruct((s, 2 * D), BF16), sd(F32), sd(F32)],
        compiler_params=_params(("parallel",)),
    )(do, sa, sb, ba, bb, w_a, w_b, w_out, dep)


def _mix_bwd(dza, dpooled, x_rnn, ga, dga, xr, hr, p, gates, dgates, vecs, w_rg_a, w_rg_x, w_pool):
    s = xr.shape[0]
    tm = min(TM_MIX, s)
    nb = s // tm

    def body(dza_ref, dpooled_ref, xh_ref, x_ref, ga_ref, dga_ref, xr_ref, hh_ref, hr_ref, p_ref,
             a_ref, mult_ref, ra_ref, ri_ref, dg_ref, vec_ref, wa_ref, wx_ref, wp_ref,
             dproj_ref, dwa_ref, dwx_ref, dwp_ref, small_ref,
             scan_carry, dxr_carry, q_carry, scan_scr, dwa_acc, dwx_acc, dwp_acc):
        i = pl.program_id(0)
        bi = nb - 1 - i
        first_t = bi == 0

        @pl.when(i == 0)
        def _():
            scan_carry[...] = jnp.zeros_like(scan_carry)
            dxr_carry[...] = jnp.zeros_like(dxr_carry)
            q_carry[...] = jnp.zeros_like(q_carry)
            dwa_acc[...] = jnp.zeros_like(dwa_acc)
            dwx_acc[...] = jnp.zeros_like(dwx_acc)
            dwp_acc[...] = jnp.zeros_like(dwp_acc)
            small_ref[...] = jnp.zeros_like(small_ref)

        row = lax.broadcasted_iota(jnp.int32, (tm, GW), 0)
        is_t0 = jnp.logical_and(first_t, row == 0)
        head_t = (lax.broadcasted_iota(jnp.int32, (HALO_U, GW), 0) + 1).astype(F32)
        colsum = lambda v: jnp.sum(v, axis=0, keepdims=True)
        for g in range(N_GROUPS):
            cs = slice(g * GW, (g + 1) * GW)
            vec = vec_ref[:, cs]
            xr = xr_ref[:, cs]
            hr = hr_ref[:, cs]
            dza = dza_ref[:, cs]
            dproj_ref[:, D + g * GW:D + (g + 1) * GW] = (dza * hr * dga_ref[:, cs]).astype(BF16)
            dhr = dza * ga_ref[:, cs]
            a = a_ref[:, cs]
            mult = mult_ref[:, cs]
            ra = ra_ref[:, cs].astype(F32)
            ri = ri_ref[:, cs].astype(F32)
            sp = _softplus(vec[V_A_PARAM:V_A_PARAM + 1])
            m = jnp.where(row == tm - 1, 1.0, _shift_up(a, 1))
            gsum = _scan_up(m, dhr, scan_carry[0:1, cs], scan_scr)
            scan_carry[0:1, cs] = a[0:1, :] * gsum[0:1, :]
            hh = jnp.where(first_t, 0.0, hh_ref[:, cs])
            hprev = _shift_down(jnp.concatenate([hh, hr], axis=0), 1)[8:]
            da = gsum * hprev
            dmult = jnp.where(is_t0, 0.0, gsum * xr * ri)
            dlog_a = da * a - dmult * a * a / mult
            dri = gsum * xr * mult
            dxr = gsum * ri * mult
            small_ref[7:8, cs] += colsum((-C_RG) * ra * dlog_a)
            dpa = (((-C_RG) * sp) * dlog_a * ra * (1.0 - ra))
            dpx = dri * ri * (1.0 - ri)
            small_ref[5:6, cs] += colsum(dpa)
            small_ref[6:7, cs] += colsum(dpx)
            dpa = dpa.astype(BF16)
            dpx = dpx.astype(BF16)
            xrb = xr.astype(BF16)
            dwa_acc[g] += _dot_tn(xrb, dpa)
            dwx_acc[g] += _dot_tn(xrb, dpx)
            dxr = dxr + _dot_nt(dpa, wa_ref[g]) + _dot_nt(dpx, wx_ref[g])
            small_ref[4:5, cs] += colsum(dxr)
            xh = jnp.where(first_t, 0.0, xh_ref[:, cs])
            taps = _conv_taps(jnp.concatenate([xh, x_ref[:, cs]], axis=0))
            dxr_ext = jnp.concatenate([dxr, dxr_carry[:, cs]], axis=0)
            dx = vec[V_CONV_W + 3:V_CONV_W + 4] * dxr
            for j in range(4):
                small_ref[j:j + 1, cs] += colsum(dxr * taps[j])
                if j < 3:
                    dx = dx + vec[V_CONV_W + j:V_CONV_W + j + 1] * _shift_up(dxr_ext, 3 - j)[:tm]
            dxr_carry[:, cs] = dxr[0:8, :]
            dproj_ref[:, cs] = dx.astype(BF16)
            pg = p_ref[:, cs]
            dpooled = dpooled_ref[:, cs]
            pb = _dot(pg, wp_ref[g]) + vec[V_B_POOL:V_B_POOL + 1]
            small_ref[9:10, cs] += colsum(dpooled * pb)
            dpb = dpooled * vec[V_POOL_SCALE:V_POOL_SCALE + 1]
            small_ref[8:9, cs] += colsum(dpb)
            dpbb = dpb.astype(BF16)
            dwp_acc[g] += _dot_tn(pg, dpbb)
            dp = _dot_nt(dpbb, wp_ref[g])
            q = _window_mean(dp, POOL_WINDOWS[g], first_t, head_t)
            sm = jnp.concatenate([q, q_carry[:, cs]], axis=0)
            k = 1
            while k < POOL_WINDOWS[g]:
                sm = sm + _shift_up(sm, k)
                k *= 2
            q_carry[:, cs] = q[0:HALO_U, :]
            dproj_ref[:, 2 * D + g * GW:2 * D + (g + 1) * GW] = (sm[:tm] - dp).astype(BF16)
        dproj_ref[:, 3 * D:] = dg_ref[...]

        @pl.when(i == nb - 1)
        def _():
            dwa_ref[...] = dwa_acc[...].astype(BF16)
            dwx_ref[...] = dwx_acc[...].astype(BF16)
            dwp_ref[...] = dwp_acc[...].astype(BF16)

    rev = lambda i: nb - 1 - i
    tok = pl.BlockSpec((tm, D), lambda i: (rev(i), 0))
    halo8 = lambda k: pl.BlockSpec((8, D), lambda i: (jnp.maximum(rev(i) * (tm // 8) - 1, 0), k))
    wspec = pl.BlockSpec((N_GROUPS, GW, GW), lambda i: (0, 0, 0))
    wshape = jax.ShapeDtypeStruct((N_GROUPS, GW, GW), BF16)
    return pl.pallas_call(
        body, name="mix_bwd", grid=(nb,),
        in_specs=[tok, tok, halo8(0), tok, tok, tok, tok, halo8(0), tok, tok, tok, tok, tok, tok,
                  pl.BlockSpec((tm, 2 * D), lambda i: (rev(i), 0)),
                  pl.BlockSpec((16, D), lambda i: (0, 0)), wspec, wspec, wspec],
        out_specs=[pl.BlockSpec((tm, D_IN), lambda i: (rev(i), 0)), wspec, wspec, wspec,
                   pl.BlockSpec((16, D), lambda i: (0, 0))],
        out_shape=[jax.ShapeDtypeStruct((s, D_IN), BF16), wshape, wshape, wshape,
                   jax.ShapeDtypeStruct((16, D), F32)],
        scratch_shapes=[pltpu.VMEM((8, D), F32), pltpu.VMEM((8, D), F32), pltpu.VMEM((HALO_U, D), F32),
                        pltpu.VMEM((3, tm, LANES), F32)] + [pltpu.VMEM((N_GROUPS, GW, GW), F32)] * 3,
        compiler_params=_params(("arbitrary",)),
    )(dza, dpooled, x_rnn, x_rnn, ga, dga, xr, hr, hr, p, *gates, dgates, vecs, w_rg_a, w_rg_x, w_pool)


def _proj_bwd(dproj, x, dx2, modr, vecs, w_in):
    s = x.shape[0]
    tm = min(TM_PROJ, s)

    def body(dp_ref, x_ref, dx2_ref, mod_ref, vec_ref, w_ref, gx_ref, small_ref):
        @pl.when(pl.program_id(0) == 0)
        def _():
            small_ref[...] = jnp.zeros_like(small_ref)

        dh1 = None
        for c in range(D_IN // D):
            cs = slice(c * D, (c + 1) * D)
            part = _dot_nt(dp_ref[:, cs], w_ref[:, cs])
            dh1 = part if dh1 is None else dh1 + part
        xv = x_ref[...]
        r1 = lax.rsqrt(jnp.mean(xv * xv, axis=-1, keepdims=True) + EPS)
        xn1 = xv * r1
        gain = vec_ref[V_G1:V_G1 + 1, :] * (1.0 + mod_ref[M_SC1:M_SC1 + 1, :])
        dxn1 = dh1 * gain
        gx_ref[...] = dx2_ref[...] + r1 * (dxn1 - xn1 * jnp.mean(dxn1 * xn1, axis=-1, keepdims=True))
        small_ref[0:1, :] += jnp.sum(dh1, axis=0, keepdims=True)
        small_ref[1:2, :] += jnp.sum(dh1 * xn1, axis=0, keepdims=True)

    tok = pl.BlockSpec((tm, D), lambda i: (i, 0))
    return pl.pallas_call(
        body, name="proj_bwd", grid=(s // tm,),
        in_specs=[pl.BlockSpec((tm, D_IN), lambda i: (i, 0)), tok, tok,
                  pl.BlockSpec((8, D), lambda i: (0, 0)), pl.BlockSpec((16, D), lambda i: (0, 0)),
                  _resident((D, D_IN))],
        out_specs=[tok, pl.BlockSpec((8, D), lambda i: (0, 0))],
        out_shape=[jax.ShapeDtypeStruct((s, D), F32), jax.ShapeDtypeStruct((8, D), F32)],
        compiler_params=_params(("arbitrary",)),
    )(dproj, x, dx2, modr, vecs, w_in)


def _wgrad(a, b, name, square_a=False, dep=None):
    s, ka = a.shape
    n = b.shape[1]
    tka = ka if ka <= 1024 else ka // 2
    tn = n if n <= 1024 else n // 2
    ts = min(TS_WGRAD, s)
    ns = s // ts
    nc = 512
    deps = [] if dep is None else [dep]

    def body(a_ref, b_ref, *refs):
        out_ref, acc_ref = refs[-2:]
        t = pl.program_id(2)

        @pl.when(t == 0)
        def _():
            acc_ref[...] = jnp.zeros_like(acc_ref)

        av = a_ref[...]
        if square_a:
            af = av.astype(F32)
            av = (af * af).astype(BF16)
        for c in range(tn // nc):
            cs = slice(c * nc, (c + 1) * nc)
            acc_ref[:, cs] += _dot_tn(av, b_ref[:, cs])

        @pl.when(t == ns - 1)
        def _():
            out_ref[...] = acc_ref[...].astype(BF16)

    return pl.pallas_call(
        body, name=name, grid=(ka // tka, n // tn, ns),
        in_specs=[pl.BlockSpec((ts, tka), lambda i, j, t: (t, i)),
                  pl.BlockSpec((ts, tn), lambda i, j, t: (t, j))] + [pl.BlockSpec(memory_space=pl.ANY)] * len(deps),
        out_specs=pl.BlockSpec((tka, tn), lambda i, j, t: (i, j)),
        out_shape=jax.ShapeDtypeStruct((ka, n), BF16),
        scratch_shapes=[pltpu.VMEM((tka, tn), F32)],
        compiler_params=_params(("parallel", "parallel", "arbitrary")),
    )(a, b, *deps)


def _window(ref, kind, idx, size):
    start = pl.multiple_of(idx * size, size)
    if kind == 0:
        return ref.at[pl.ds(start, size)]
    if kind == 1:
        return ref.at[:, pl.ds(start, size)]
    return ref.at[:, :, pl.ds(start, size)]


def _mesh_place():
    x, y, c = lax.axis_index("x"), lax.axis_index("y"), lax.axis_index("c")
    return x, y, c, 4 * x + 2 * y + c


def _peer(x, y, c, q):
    px = 1 - x if q & 4 else x
    py = 1 - y if q & 2 else y
    pc = 1 - c if q & 1 else c
    return (px, py, pc), 4 * px + 2 * py + pc


def _all_gather(shards, kinds, name, dep=None):
    n = len(shards)
    deps = [] if dep is None else [dep]
    full_shapes = []
    for sh, kind in zip(shards, kinds):
        dims = list(sh.shape)
        dims[kind] *= N_DEV
        full_shapes.append(jax.ShapeDtypeStruct(tuple(dims), sh.dtype))

    def body(*refs):
        ins, outs = refs[:n], refs[n + len(deps):2 * n + len(deps)]
        send_sems, recv_sems, local_sems = refs[2 * n + len(deps):]
        x, y, c, me = _mesh_place()
        sends, recvs, locals_ = [], [], []
        for k in range(n):
            size = shards[k].shape[kinds[k]]
            mine = _window(outs[k], kinds[k], me, size)
            lc = pltpu.make_async_copy(ins[k], mine, local_sems.at[k])
            lc.start()
            locals_.append(lc)
            for q in range(1, N_DEV):
                peer, peer_idx = _peer(x, y, c, q)
                cp = pltpu.make_async_remote_copy(
                    src_ref=ins[k], dst_ref=mine, send_sem=send_sems.at[k, q], recv_sem=recv_sems.at[k, q],
                    device_id=peer, device_id_type=MESH)
                cp.start()
                sends.append(cp)
                recvs.append(pltpu.make_async_remote_copy(
                    src_ref=ins[k], dst_ref=_window(outs[k], kinds[k], peer_idx, size),
                    send_sem=send_sems.at[k, q], recv_sem=recv_sems.at[k, q],
                    device_id=peer, device_id_type=MESH))
        for cp in recvs:
            cp.wait_recv()
        for cp in sends:
            cp.wait_send()
        for lc in locals_:
            lc.wait()

    any_spec = pl.BlockSpec(memory_space=pl.ANY)
    return pl.pallas_call(
        body, name=name,
        in_specs=[any_spec] * (n + len(deps)), out_specs=[any_spec] * n, out_shape=full_shapes,
        scratch_shapes=[pltpu.SemaphoreType.DMA((n, N_DEV)), pltpu.SemaphoreType.DMA((n, N_DEV)),
                        pltpu.SemaphoreType.DMA((n,))],
    )(*shards, *deps)


_HBM = pl.BlockSpec(memory_space=pltpu.HBM)
_SEM = pl.BlockSpec(memory_space=pltpu.SEMAPHORE)
_EFFECT = pltpu.SideEffectType.DATAFLOW_SIDE_EFFECTING


N_NEAR = 4


def _near(x, y, c):
    out = [((x, y, 1 - c), 4 * x + 2 * y + 1 - c)]
    for j in (1, 2, 3):
        px = 1 - x if j & 2 else x
        py = 1 - y if j & 1 else y
        out.append(((px, py, c), 4 * px + 2 * py + c))
    return out


def _remote(src, dst, send_sems, recv_sems, slot, device):
    return pltpu.make_async_remote_copy(src_ref=src, dst_ref=dst, send_sem=send_sems.at[slot], recv_sem=recv_sems.at[slot],
                                        device_id=device, device_id_type=MESH)


def _split_call(name, arrays, sems_in, n_new_sems, after, emit):
    na, ns, nn = len(arrays), len(sems_in), len(n_new_sems)

    def body(*refs):
        emit(refs[:na], refs[na:na + ns], refs[na + ns + 1:na + ns + 1 + nn])
        refs[-1][...] = jnp.zeros_like(refs[-1])

    outs = pl.pallas_call(
        body, name=name,
        out_shape=(*[pltpu.SemaphoreType.DMA((m,)) for m in n_new_sems],
                   *[pltpu.HBM(a.shape, a.dtype) for a in arrays], jax.ShapeDtypeStruct((8, 128), F32)),
        in_specs=[_HBM] * na + [_SEM] * ns + [pl.BlockSpec(memory_space=pl.ANY)],
        out_specs=(*[_SEM] * nn, *[_HBM] * na, pl.BlockSpec(memory_space=pltpu.VMEM)),
        input_output_aliases={i: nn + i for i in range(na)},
        compiler_params=pltpu.CompilerParams(has_side_effects=_EFFECT),
    )(*[pltpu.with_memory_space_constraint(a, pltpu.HBM) for a in arrays], *sems_in, after)
    return list(outs[:nn]), list(outs[nn:nn + na]), outs[-1]


class _Gather:
    def __init__(self, shards, kinds, after, name):
        self.n, self.kinds, self.name = len(shards), kinds, name
        self.sizes = [s.shape[k] for s, k in zip(shards, kinds)]
        n = self.n
        lands = []
        for s, k in zip(shards, kinds):
            dims = list(s.shape)
            dims[k] *= N_DEV
            lands.append(lax.empty(tuple(dims), s.dtype))

        def emit(arr, _, new):
            x, y, c, me = _mesh_place()
            for k in range(n):
                pltpu.make_async_copy(arr[k], _window(arr[n + k], kinds[k], me, self.sizes[k]), new[2].at[k]).start()
            for k in range(n):
                mine = _window(arr[n + k], kinds[k], me, self.sizes[k])
                for j, (dev, _) in enumerate(_near(x, y, c)):
                    _remote(arr[k], mine, new[0], new[1], k * N_NEAR + j, dev).start()

        self.sems, self.arrays, self.token = _split_call(name + "_start", [*shards, *lands], [],
                                                         [n * N_NEAR, n * N_NEAR, n], after, emit)

    def forward(self, after):
        n, kinds, sizes = self.n, self.kinds, self.sizes

        def emit(arr, old, new):
            x, y, c, _ = _mesh_place()
            near = _near(x, y, c)
            for k in range(n):
                for j in (1, 2, 3):
                    dev, idx = near[j]
                    landed = _window(arr[n + k], kinds[k], idx, sizes[k])
                    _remote(arr[k], landed, old[0], old[1], k * N_NEAR + j, dev).wait_recv()
                    _remote(landed, landed, new[0], new[1], k * N_NEAR + j, near[0][0]).start()

        new, self.arrays, self.token = _split_call(self.name + "_forward", self.arrays, self.sems, [n * N_NEAR] * 2,
                                                   after, emit)
        self.sems = [*self.sems, *new]

    def finish(self, after):
        n, kinds, sizes = self.n, self.kinds, self.sizes

        def emit(arr, old, _):
            x, y, c, me = _mesh_place()
            near = _near(x, y, c)
            other_core = near[0][0]
            for k in range(n):
                win = lambda idx: _window(arr[n + k], kinds[k], idx, sizes[k])
                pltpu.make_async_copy(arr[k], win(me), old[2].at[k]).wait()
                for j, (dev, idx) in enumerate(near):
                    _remote(arr[k], win(me), old[0], old[1], k * N_NEAR + j, dev).wait_send()
                _remote(arr[k], win(near[0][1]), old[0], old[1], k * N_NEAR, other_core).wait_recv()
                for j in (1, 2, 3):
                    idx = near[j][1]
                    _remote(win(idx), win(idx), old[3], old[4], k * N_NEAR + j, other_core).wait_send()
                    _remote(arr[k], win(idx + 1 - 2 * c), old[3], old[4], k * N_NEAR + j, other_core).wait_recv()

        _, arrays, _ = _split_call(self.name + "_finish", self.arrays, self.sems, [], after, emit)
        return arrays[n:]


class _Scatter:
    def __init__(self, partials, kinds, after, name):
        self.n, self.kinds, self.name, self.partials = len(partials), kinds, name, partials
        self.sizes = [p.shape[k] // N_DEV for p, k in zip(partials, kinds)]
        n, sizes = self.n, self.sizes
        self.slot_shapes = []
        for p, k, size in zip(partials, kinds, sizes):
            dims = list(p.shape)
            dims[k] = size
            self.slot_shapes.append((N_NEAR, *dims))
        slots = [lax.empty(sh, p.dtype) for sh, p in zip(self.slot_shapes, partials)]

        def emit(arr, _, new):
            x, y, c, _ = _mesh_place()
            near = _near(x, y, c)
            for k in range(n):
                for j in range(N_NEAR):
                    owner = near[j][1] if j == 0 else near[j][1] + 1 - 2 * c
                    _remote(_window(arr[k], kinds[k], owner, sizes[k]), arr[n + k].at[j], new[0], new[1],
                            k * N_NEAR + j, near[0][0]).start()

        self.sems, self.arrays, self.token = _split_call(name + "_start", [*partials, *slots], [], [n * N_NEAR] * 2,
                                                         after, emit)

    def combine_and_send(self, own4, after):
        n, kinds, sizes = self.n, self.kinds, self.sizes

        def emit_wait(arr, old, _):
            x, y, c, _ = _mesh_place()
            near = _near(x, y, c)
            for k in range(n):
                for j in range(N_NEAR):
                    owner = near[j][1] if j == 0 else near[j][1] + 1 - 2 * c
                    cp = _remote(_window(arr[k], kinds[k], owner, sizes[k]), arr[n + k].at[j], old[0], old[1],
                                 k * N_NEAR + j, near[0][0])
                    cp.wait_send()
                    cp.wait_recv()

        _, arrays, _ = _split_call(self.name + "_landed", self.arrays, self.sems, [], after, emit_wait)
        chip_sums = _chip_sums(arrays[:n], arrays[n:], kinds, sizes, own4, self.name + "_combine")
        arrivals = [lax.empty((N_NEAR - 1, *sh[1:]), p.dtype) for sh, p in zip(self.slot_shapes, self.partials)]

        def emit_send(arr, _, new):
            x, y, c, _ = _mesh_place()
            near = _near(x, y, c)
            for k in range(n):
                for j in (1, 2, 3):
                    _remote(arr[k].at[j], arr[n + k].at[j - 1], new[0], new[1], k * N_NEAR + j, near[j][0]).start()

        self.sems, self.arrays, self.token = _split_call(self.name + "_send", [*chip_sums, *arrivals], [],
                                                         [n * N_NEAR] * 2, own4, emit_send)

    def finish(self, after):
        n = self.n

        def emit(arr, old, _):
            x, y, c, _ = _mesh_place()
            near = _near(x, y, c)
            for k in range(n):
                for j in (1, 2, 3):
                    cp = _remote(arr[k].at[j], arr[n + k].at[j - 1], old[0], old[1], k * N_NEAR + j, near[j][0])
                    cp.wait_send()
                    cp.wait_recv()

        _, arrays, _ = _split_call(self.name + "_finish", self.arrays, self.sems, [], after, emit)
        return arrays[:n], arrays[n:]


def _chip_sums(partials, slots, kinds, sizes, own4, name):
    n = len(partials)

    def body(own_ref, *refs):
        for k in range(n):
            refs[2 * n + k][...] = (refs[k][...].astype(F32) + refs[n + k][...].astype(F32)).astype(BF16)

    in_specs, slot_specs = [], []
    for p, s, kind, size in zip(partials, slots, kinds, sizes):
        block = list(p.shape)
        block[kind] = size
        nd = len(block)
        in_specs.append(pl.BlockSpec(tuple(block), functools.partial(
            lambda j, own, kind, nd: tuple(own[j] if d == kind else 0 for d in range(nd)), kind=kind, nd=nd)))
        slot_specs.append(pl.BlockSpec((None, *block), functools.partial(
            lambda j, own, nd: (j,) + (0,) * nd, nd=nd)))
    return pl.pallas_call(
        body, name=name,
        grid_spec=pltpu.PrefetchScalarGridSpec(num_scalar_prefetch=1, grid=(N_NEAR,),
                                               in_specs=in_specs + slot_specs, out_specs=slot_specs),
        out_shape=[jax.ShapeDtypeStruct(s.shape, s.dtype) for s in slots],
        compiler_params=_params(("arbitrary",)),
    )(own4, *partials, *slots)


def _after(small, token):
    return small + token[0:1, 0:1].astype(small.dtype)


def _silu(c):
    return c * _sigmoid_tail(c)


def _ada_fwd(c_all, w_ada, b_ada_cols):
    def body(c_ref, w_ref, b_ref, out_ref):
        out_ref[...] = jnp.dot(_silu(c_ref[...]), w_ref[...], preferred_element_type=F32,
                               precision=lax.Precision.HIGHEST) + b_ref[...]

    return pl.pallas_call(
        body, name="ada_fwd", out_shape=jax.ShapeDtypeStruct((N_DEV, w_ada.shape[1]), F32),
    )(c_all, w_ada, b_ada_cols)


def _adam(w, g, m, v):
    m = ADAM_B1 * m + (1.0 - ADAM_B1) * g
    v = ADAM_B2 * v + (1.0 - ADAM_B2) * (g * g)
    m_hat = m / (1.0 - ADAM_B1 ** ADAM_STEP)
    v_hat = v / (1.0 - ADAM_B2 ** ADAM_STEP)
    delta = -ADAM_LR * (m_hat / (jnp.sqrt(v_hat) + ADAM_EPS) + ADAM_WD * w)
    return delta, m, v


def _ada_bwd_adam(c_all, dmod_cols, w, m, v):
    def body(c_ref, d_ref, w_ref, m_ref, v_ref, g_ref, delta_ref, nm_ref, nv_ref):
        g = lax.dot_general(_silu(c_ref[...]), d_ref[...], (((0,), (0,)), ((), ())),
                            preferred_element_type=F32, precision=lax.Precision.HIGHEST)
        g_ref[...] = g
        delta_ref[...], nm_ref[...], nv_ref[...] = _adam(w_ref[...], g, m_ref[...], v_ref[...])

    sd = jax.ShapeDtypeStruct(w.shape, F32)
    return pl.pallas_call(body, name="ada_bwd_adam", out_shape=[sd] * 4,
                          compiler_params=pltpu.CompilerParams(vmem_limit_bytes=V7X_VMEM_LIMIT),
                          )(c_all, dmod_cols, w, m, v)


def _adam_group(chip_sums, arrivals, ws, ms, vs, n_tiles, name):
    n = len(ws)

    def body(*refs):
        for k in range(n):
            c_ref, a_ref, w_ref, m_ref, v_ref = (refs[j * n + k] for j in range(5))
            g_ref, delta_ref, nm_ref, nv_ref = (refs[(5 + j) * n + k] for j in range(4))
            g = c_ref[...].astype(F32)
            for j in range(N_NEAR - 1):
                g = g + a_ref[j].astype(F32)
            g_ref[...] = g
            delta_ref[...], nm_ref[...], nv_ref[...] = _adam(w_ref[...], g, m_ref[...], v_ref[...])

    tiles = [(w.shape[0] // n_tiles, w.shape[1]) for w in ws]
    blk = [pl.BlockSpec(t, lambda i: (i, 0)) for t in tiles]
    return pl.pallas_call(
        body, name=name, grid=(n_tiles,),
        in_specs=[pl.BlockSpec((None, *t), lambda i: (0, i, 0)) for t in tiles]
        + [pl.BlockSpec((N_NEAR - 1, *t), lambda i: (0, i, 0)) for t in tiles] + blk * 3,
        out_specs=blk * 4, out_shape=[jax.ShapeDtypeStruct(w.shape, F32) for w in ws] * 4,
        compiler_params=_params(("parallel",)),
    )(*chip_sums, *arrivals, *ws, *ms, *vs)


N_SMALL = 40
N_SMALL_PARAMS = 11


def _pack_vecs(conv_w_full, rows):
    def body(cw_ref, *refs):
        out = refs[-1]
        out[...] = jnp.zeros_like(out)
        out[0:4, :] = cw_ref[0:4, :]
        for r, ref in enumerate(refs[:-1]):
            out[4 + r:5 + r, :] = ref[...]

    return pl.pallas_call(body, name="pack_vecs", out_shape=jax.ShapeDtypeStruct((16, D), F32))(conv_w_full, *rows)


def _small_finish(gathered, mod_all, vecs, ws, ms, vs):
    n = N_SMALL_PARAMS

    def body(g_ref, mod_ref, vec_ref, *refs):
        w_refs, m_refs, v_refs = refs[:n], refs[n:2 * n], refs[2 * n:3 * n]
        outs = refs[3 * n:]
        g1 = vec_ref[V_G1:V_G1 + 1, :]
        g2 = vec_ref[V_G2:V_G2 + 1, :]
        zero = jnp.zeros((1, D), F32)
        dg1, dg2, dgf, loss_lanes = zero, zero, zero, zero
        mixer = jnp.zeros((16, D), F32)
        db_ada = jnp.zeros((6, D), F32)
        for b in range(N_DEV):
            gb = g_ref[b]
            mod = mod_ref[b]
            q1 = gb[33:34]
            q2 = gb[9:10]
            dmod = jnp.concatenate([gb[32:33], q1 * g1, gb[10:11], gb[8:9], q2 * g2, gb[1:2]], axis=0)
            outs[4 * n][b] = dmod
            db_ada = db_ada + dmod
            dg1 = dg1 + q1 * (1.0 + mod[M_SC1:M_SC1 + 1])
            dg2 = dg2 + q2 * (1.0 + mod[M_SC2:M_SC2 + 1])
            dgf = dgf + gb[0:1]
            loss_lanes = loss_lanes + gb[2:3]
            mixer = mixer + gb[16:32]
        d_a_param = mixer[7:8] * _sigmoid_tail(vec_ref[V_A_PARAM:V_A_PARAM + 1, :])
        grads = [dg1, dg2, mixer[4:5], mixer[5:6], mixer[6:7], d_a_param, mixer[8:9], mixer[9:10], dgf,
                 db_ada, mixer[0:4]]
        for k in range(n):
            outs[k][...] = grads[k]
            outs[n + k][...], outs[2 * n + k][...], outs[3 * n + k][...] = _adam(
                w_refs[k][...], grads[k], m_refs[k][...], v_refs[k][...])
        outs[4 * n + 1][...] = jnp.broadcast_to(jnp.sum(loss_lanes, axis=1, keepdims=True), (8, 128))

    shapes = [jax.ShapeDtypeStruct(w.shape, F32) for w in ws]
    return pl.pallas_call(
        body, name="small_finish",
        out_shape=shapes * 4 + [jax.ShapeDtypeStruct((N_DEV, 6, D), F32), jax.ShapeDtypeStruct((8, 128), F32)],
    )(gathered, mod_all, vecs, *ws, *ms, *vs)


def _pad_rows(a, rows):
    return jnp.pad(a, ((0, rows - a.shape[0]), (0, 0)))


def kernel(x, c, norm_mix_g, norm_mlp_g, w_ada, b_ada, w_in, conv_w, conv_b, w_rg_a, b_rg_a, w_rg_x, b_rg_x, a_param, w_branch_a, w_pool, b_pool, pool_scale, w_branch_b, w_out, w_up, w_down, final_g, loss_target, m_norm_mix_g, m_norm_mlp_g, m_w_ada, m_b_ada, m_w_in, m_conv_w, m_conv_b, m_w_rg_a, m_b_rg_a, m_w_rg_x, m_b_rg_x, m_a_param, m_w_branch_a, m_w_pool, m_b_pool, m_pool_scale, m_w_branch_b, m_w_out, m_w_up, m_w_down, m_final_g, v_norm_mix_g, v_norm_mlp_g, v_w_ada, v_b_ada, v_w_in, v_conv_w, v_conv_b, v_w_rg_a, v_b_rg_a, v_w_rg_x, v_b_rg_x, v_a_param, v_w_branch_a, v_w_pool, v_b_pool, v_pool_scale, v_w_branch_b, v_w_out, v_w_up, v_w_down, v_final_g):
    me = 4 * lax.axis_index("x") + 2 * lax.axis_index("y") + lax.axis_index("c")
    s = x.shape[1]
    x2d = x.reshape(s, D)
    target = loss_target.reshape(s, D)
    n_ada = w_ada.shape[2]

    sharded = dict(w_in=(w_in[0], 1), w_up=(w_up[0], 1), w_down=(w_down[0], 0), w_branch_a=(w_branch_a[0], 0),
                   w_branch_b=(w_branch_b[0], 0), w_out=(w_out[0], 0), w_rg_a=(w_rg_a[0], 1), w_rg_x=(w_rg_x[0], 1),
                   w_pool=(w_pool[0], 1))
    kind = {k: v[1] for k, v in sharded.items()}
    shard = {k: v[0].astype(BF16) for k, v in sharded.items()}

    conv_w_full, c_rows = _all_gather([_pad_rows(conv_w[0], 8), _pad_rows(c, 8)], [1, 0], "gather_c")
    c_all = c_rows.reshape(N_DEV, 8, D)[:, 0, :]
    b_ada_cols = lax.dynamic_slice(b_ada, (0, me * n_ada), (1, n_ada))
    mod_part = _ada_fwd(c_all, w_ada[0], b_ada_cols)
    mod_parts, = _all_gather([mod_part], [0], "gather_mod")

    first_names = ["w_in", "w_rg_a", "w_rg_x", "w_pool"]
    branch_names = ["w_branch_a", "w_branch_b", "w_out"]
    mlp_names = ["w_up", "w_down"]

    def gather(group, after, name):
        return _Gather([shard[k] for k in group], [kind[k] for k in group], after, name)

    g_first = gather(first_names, mod_parts, "gather_first")
    g_branch = gather(branch_names, g_first.token, "gather_branch")
    g_mlp = gather(mlp_names, g_branch.token, "gather_mlp")

    mod_all = jnp.transpose(mod_parts.reshape(N_DEV, N_DEV, n_ada), (1, 0, 2)).reshape(N_DEV, 6, D)
    mod_all = jnp.pad(mod_all, ((0, 0), (0, 2), (0, 0)))
    modr = lax.dynamic_index_in_dim(mod_all, me, 0, keepdims=False)
    vecs = _pack_vecs(conv_w_full, [conv_b, b_rg_a, b_rg_x, a_param, b_pool, pool_scale,
                                    norm_mix_g, norm_mlp_g, final_g.reshape(1, D)])
    vecs = _after(vecs, g_mlp.token)
    g_first.forward(vecs)
    wg = dict(zip(first_names, g_first.finish(g_first.token)))

    h1, x_rnn, u_pool, ga, dga, sa, sb = _proj_fwd(x2d, modr, vecs, wg["w_in"])
    g_branch.forward(h1)
    xr, hr, za, p, pooled, *gates = _mix_fwd(x_rnn, u_pool, ga, _after(vecs, g_branch.token),
                                             wg["w_rg_a"], wg["w_rg_x"], wg["w_pool"])
    g_mlp.forward(za)
    wg.update(zip(branch_names, g_branch.finish(g_mlp.token)))
    ba, bb, merged, o, x2, h2 = _branch_fwd(za, pooled, sa, sb, x2d, modr, vecs,
                                            wg["w_branch_a"], wg["w_branch_b"], wg["w_out"])
    wg.update(zip(mlp_names, g_mlp.finish(h2)))
    ru, dx3, d_dn, small_f = _mlp_fwd(h2, x2, target, modr, vecs, wg["w_up"], wg["w_down"])

    near = _near(lax.axis_index("x"), lax.axis_index("y"), lax.axis_index("c"))
    own4 = jnp.stack([me, near[1][1], near[2][1], near[3][1]]).astype(jnp.int32)

    def scatter(group, partial, after, name):
        return _Scatter([partial[k] for k in group], [kind[k] for k in group], after, name)

    dup, dx2, do, small_m = _mlp_bwd(d_dn, ru, x2, dx3, o, modr, vecs, wg["w_up"], wg["w_down"])
    partial = dict(w_up=_wgrad(h2, dup, "wgrad_up"), w_down=_wgrad(ru, d_dn, "wgrad_down", square_a=True))
    s_mlp = scatter(mlp_names, partial, dx2, "scatter_mlp")

    dba, dbb, dgates, dza, dpooled = _branch_bwd(do, sa, sb, ba, bb, wg["w_branch_a"], wg["w_branch_b"], wg["w_out"],
                                                 dep=s_mlp.token)
    s_mlp.combine_and_send(own4, dza)
    dproj, dw_rg_a, dw_rg_x, dw_pool, small_x = _mix_bwd(dza, dpooled, x_rnn, ga, dga, xr, hr, p, gates, dgates,
                                                         _after(vecs, s_mlp.token),
                                                         wg["w_rg_a"], wg["w_rg_x"], wg["w_pool"])
    partial.update(w_branch_a=_wgrad(za, dba, "wgrad_branch_a"), w_branch_b=_wgrad(pooled, dbb, "wgrad_branch_b"),
                   w_out=_wgrad(merged, do, "wgrad_out"),
                   w_rg_a=dw_rg_a, w_rg_x=dw_rg_x, w_pool=dw_pool)
    mixer_names = ["w_rg_a", "w_rg_x", "w_pool", "w_branch_a", "w_branch_b", "w_out"]
    s_mixer = scatter(mixer_names, partial, s_mlp.token, "scatter_mixer")

    partial["w_in"] = _wgrad(h1, dproj, "wgrad_in", dep=s_mixer.token)
    s_in = scatter(["w_in"], partial, s_mixer.token, "scatter_in")
    s_mixer.combine_and_send(own4, s_in.token)
    s_in.combine_and_send(own4, s_mixer.token)
    grad_x, small_p = _proj_bwd(dproj, x2d, dx2, _after(modr, s_in.token), vecs, wg["w_in"])

    locals_ = dict(w_in=(w_in, m_w_in, v_w_in), w_up=(w_up, m_w_up, v_w_up), w_down=(w_down, m_w_down, v_w_down),
                   w_branch_a=(w_branch_a, m_w_branch_a, v_w_branch_a),
                   w_branch_b=(w_branch_b, m_w_branch_b, v_w_branch_b), w_out=(w_out, m_w_out, v_w_out),
                   w_rg_a=(w_rg_a, m_w_rg_a, v_w_rg_a), w_rg_x=(w_rg_x, m_w_rg_x, v_w_rg_x),
                   w_pool=(w_pool, m_w_pool, v_w_pool))
    res = {}

    def finish(group, exchange, after, n_tiles, name):
        chip_sums, arrivals = exchange.finish(after)
        flat = lambda t: t.reshape(-1, t.shape[-1])
        shapes = [flat(locals_[k][0]).shape for k in group]
        outs = _adam_group([cs.reshape(N_NEAR, *sh) for cs, sh in zip(chip_sums, shapes)],
                           [ar.reshape(N_NEAR - 1, *sh) for ar, sh in zip(arrivals, shapes)],
                           *[[flat(locals_[k][j]) for k in group] for j in range(3)], n_tiles, name)
        for i, k in enumerate(group):
            res[k] = [outs[j * len(group) + i].reshape(locals_[k][0].shape) for j in range(4)]
        return res[group[-1]][0]

    small = jnp.concatenate([small_f, small_m, small_x, small_p], axis=0)
    g_small = _Gather([small], [0], grad_x, "gather_small")
    done = finish(mlp_names, s_mlp, g_small.token, 4, "adam_mlp")
    done = finish(mixer_names, s_mixer, done, 2, "adam_mixer")
    g_small.forward(done)
    done = finish(["w_in"], s_in, g_small.token, 4, "adam_in")
    small_all, = g_small.finish(done)
    small_all = small_all.reshape(N_DEV, N_SMALL, D)

    def embed(cw):
        return lax.dynamic_update_slice(jnp.zeros((4, D), F32), cw[0], (0, me * (D // N_DEV)))

    def smalls(ng, nl, cb, bra, brx, ap, bp, ps, fg, ba_, cw):
        return [ng, nl, cb, bra, brx, ap, bp, ps, fg.reshape(1, D), ba_.reshape(6, D), embed(cw)]

    small_names = ["norm_mix_g", "norm_mlp_g", "conv_b", "b_rg_a", "b_rg_x", "a_param", "b_pool", "pool_scale",
                   "final_g", "b_ada", "conv_w"]
    fin = _small_finish(
        small_all, mod_all, vecs,
        smalls(norm_mix_g, norm_mlp_g, conv_b, b_rg_a, b_rg_x, a_param, b_pool, pool_scale, final_g, b_ada, conv_w),
        smalls(m_norm_mix_g, m_norm_mlp_g, m_conv_b, m_b_rg_a, m_b_rg_x, m_a_param, m_b_pool, m_pool_scale,
               m_final_g, m_b_ada, m_conv_w),
        smalls(v_norm_mix_g, v_norm_mlp_g, v_conv_b, v_b_rg_a, v_b_rg_x, v_a_param, v_b_pool, v_pool_scale,
               v_final_g, v_b_ada, v_conv_w))
    dmod_all, loss_tile = fin[4 * N_SMALL_PARAMS], fin[4 * N_SMALL_PARAMS + 1]
    dmod_cols = lax.dynamic_slice(dmod_all.reshape(N_DEV, 6 * D), (0, me * n_ada), (N_DEV, n_ada))
    res["w_ada"] = [t.reshape(w_ada.shape) for t in _ada_bwd_adam(c_all, dmod_cols, w_ada[0], m_w_ada[0], v_w_ada[0])]

    def final_shape(k, t):
        if k == "final_g":
            return t.reshape(D)
        if k == "b_ada":
            return t.reshape(1, 6 * D)
        if k == "conv_w":
            return lax.dynamic_slice(t, (0, me * (D // N_DEV)), (4, D // N_DEV)).reshape(conv_w.shape)
        return t

    for i, k in enumerate(small_names):
        res[k] = [final_shape(k, fin[which * N_SMALL_PARAMS + i]) for which in range(4)]
    order = ["norm_mix_g", "norm_mlp_g", "w_ada", "b_ada", "w_in", "conv_w", "conv_b", "w_rg_a", "b_rg_a", "w_rg_x",
             "b_rg_x", "a_param", "w_branch_a", "w_pool", "b_pool", "pool_scale", "w_branch_b", "w_out", "w_up",
             "w_down", "final_g"]
    outs = [loss_tile[0, 0], grad_x.reshape(x.shape)]
    for which in range(4):
        for k in order:
            outs.append(res[k][which])
    return tuple(outs)
```

```python
import functools

import jax
import jax.numpy as jnp
from jax import lax
from jax.experimental import pallas as pl
from jax.experimental.pallas import tpu as pltpu

F32 = jnp.float32
BF16 = jnp.bfloat16
MESH = pl.DeviceIdType.MESH

N_DEV = 8
D = 1024
N_GROUPS = 4
GW = D // N_GROUPS
D_IN = 5 * D
D_FF = 4 * D
POOL_WINDOWS = (2, 4, 8, 16)
HALO_X = 8
HALO_U = 16
EPS = 1e-6
C_RG = 8.0
ADAM_LR, ADAM_B1, ADAM_B2, ADAM_EPS, ADAM_WD, ADAM_STEP = 0.001, 0.9, 0.999, 1e-08, 0.01, 10

V7X_VMEM_LIMIT = 56 * 1024 * 1024

V_CONV_W, V_CONV_B, V_B_RG_A, V_B_RG_X, V_A_PARAM, V_B_POOL, V_POOL_SCALE, V_G1, V_G2, V_GF = 0, 4, 5, 6, 7, 8, 9, 10, 11, 12
M_SH1, M_SC1, M_GT1, M_SH2, M_SC2, M_GT2 = 0, 1, 2, 3, 4, 5

TM_PROJ = 512
TM_MIX = 256
TM_BRANCH = 512
TM_MLP = 512
TM_MLP_BWD = 256
TS_WGRAD = 2048


def _params(semantics):
    return pltpu.CompilerParams(dimension_semantics=semantics, vmem_limit_bytes=V7X_VMEM_LIMIT)


def _resident(shape):
    return pl.BlockSpec(shape, lambda *_: (0,) * len(shape), pipeline_mode=pl.Buffered(1))


def _dot(a, b):
    return jnp.dot(a, b, preferred_element_type=F32)


def _dot_nt(a, b):
    return lax.dot_general(a, b, (((1,), (1,)), ((), ())), preferred_element_type=F32)


def _dot_tn(a, b):
    return lax.dot_general(a, b, (((0,), (0,)), ((), ())), preferred_element_type=F32)


def _sigmoid(x):
    return 0.5 * jnp.tanh(0.5 * x) + 0.5


def _sigmoid_tail(x):
    return 1.0 / (1.0 + jnp.exp(-x))


def _gelu_and_grad(x):
    k = 0.7978845608028654
    x2 = x * x
    t = jnp.tanh(k * (x + 0.044715 * x * x2))
    g = 0.5 * x * (1.0 + t)
    dg = 0.5 * (1.0 + t) + 0.5 * x * (1.0 - t * t) * (k * (1.0 + 3.0 * 0.044715 * x2))
    return g, dg


def _softplus(a):
    e = jnp.exp(-jnp.abs(a))
    u = 1.0 + e
    log1p_e = jnp.where(u == 1.0, e, jnp.log(u) * e / jnp.where(u == 1.0, 1.0, u - 1.0))
    return jnp.maximum(a, 0.0) + log1p_e


def _neg_expm1(z):
    series = -(z * (1.0 + z * (0.5 + z * (1.0 / 6.0 + z * (1.0 / 24.0 + z * (1.0 / 120.0))))))
    return jnp.where(z > -0.1, series, 1.0 - jnp.exp(z))


def _shift_down(x, k):
    return pltpu.roll(x, k, 0)


def _shift_up(x, k):
    return pltpu.roll(x, x.shape[0] - k, 0)


def _rglru_gates(xr, w_a, w_x, b_a, b_x, a_param, is_t0):
    xb = xr.astype(BF16)
    ra = _sigmoid(_dot(xb, w_a) + b_a)
    ri = _sigmoid(_dot(xb, w_x) + b_x)
    sp = _softplus(a_param)
    log_a = (-C_RG) * ra * sp
    a = jnp.exp(log_a)
    mult = jnp.where(is_t0, 1.0, jnp.sqrt(_neg_expm1(2.0 * log_a)))
    return ra, ri, sp, a, mult


SUBLANES = 8


LANES = 128


def _scan_strip(a, b, carry, scr, down):
    t = b.shape[0]
    g = t // SUBLANES
    a3 = a.reshape(g, SUBLANES, LANES)
    b3 = b.reshape(g, SUBLANES, LANES)
    sub = lax.broadcasted_iota(jnp.int32, (g, SUBLANES, LANES), 1)
    for k in (1, 2, 4):
        keep = sub >= k if down else sub < SUBLANES - k
        shift = k if down else SUBLANES - k
        b3 = b3 + a3 * jnp.where(keep, pltpu.roll(b3, shift, 1), 0.0)
        a3 = a3 * jnp.where(keep, pltpu.roll(a3, shift, 1), 1.0)
    scr[0] = a3.reshape(t, LANES)
    scr[1] = b3.reshape(t, LANES)
    end_row = SUBLANES - 1 if down else 0
    ag = scr[0, pl.ds(end_row, g, stride=SUBLANES), :]
    bg = scr[1, pl.ds(end_row, g, stride=SUBLANES), :]
    rg = lax.broadcasted_iota(jnp.int32, (g, LANES), 0)
    edge = 0 if down else g - 1
    bg = bg + jnp.where(rg == edge, ag * carry, 0.0)
    k = 1
    while k < g:
        keep = rg >= k if down else rg < g - k
        shift = k if down else g - k
        bg = bg + ag * jnp.where(keep, pltpu.roll(bg, shift, 0), 0.0)
        if 2 * k < g:
            ag = ag * pltpu.roll(ag, shift, 0)
        k *= 2
    entering = jnp.where(rg != edge, pltpu.roll(bg, 1 if down else g - 1, 0), carry)
    for r in range(SUBLANES):
        scr[2, pl.ds(r, g, stride=SUBLANES), :] = entering
    return scr[1] + scr[0] * scr[2], bg[g - 1:g, :]


def _scan_strips(a, b, carry, scr, down):
    outs = [_scan_strip(a[:, c:c + LANES], b[:, c:c + LANES], carry[:, c:c + LANES], scr, down)
            for c in range(0, b.shape[1], LANES)]
    return jnp.concatenate([o[0] for o in outs], axis=1), jnp.concatenate([o[1] for o in outs], axis=1)


def _scan_down(a, b, carry, scr):
    return _scan_strips(a, b, carry, scr, True)


def _scan_up(m, b, carry, scr):
    return _scan_strips(m, b, carry, scr, False)[0]


def _window_mean(sums, window, first_block, head_t):
    scaled = sums * (1.0 / window)
    head = jnp.where(first_block, sums[:HALO_U] / jnp.minimum(head_t, float(window)), scaled[:HALO_U])
    return jnp.concatenate([head, scaled[HALO_U:]], axis=0)


def _conv_taps(x_ext):
    return [_shift_down(x_ext, 3 - j)[HALO_X:] if j < 3 else x_ext[HALO_X:] for j in range(4)]


def _proj_fwd(x, modr, vecs, w_in):
    s = x.shape[0]
    tm = min(TM_PROJ, s)

    def body(x_ref, mod_ref, vec_ref, w_ref, h1_ref, xrnn_ref, u_ref, ga_ref, dga_ref, sa_ref, sb_ref):
        xv = x_ref[...]
        r = lax.rsqrt(jnp.mean(xv * xv, axis=-1, keepdims=True) + EPS)
        gain = vec_ref[V_G1:V_G1 + 1, :] * (1.0 + mod_ref[M_SC1:M_SC1 + 1, :])
        h = (xv * r * gain + mod_ref[M_SH1:M_SH1 + 1, :]).astype(BF16)
        h1_ref[...] = h
        xrnn_ref[...] = _dot(h, w_ref[:, 0:D])
        ga_ref[...], dga_ref[...] = _gelu_and_grad(_dot(h, w_ref[:, D:2 * D]))
        u_ref[...] = _dot(h, w_ref[:, 2 * D:3 * D])
        sa_ref[...] = _sigmoid(_dot(h, w_ref[:, 3 * D:4 * D]))
        sb_ref[...] = _sigmoid(_dot(h, w_ref[:, 4 * D:5 * D]))

    tok = pl.BlockSpec((tm, D), lambda i: (i, 0))
    sd = lambda dt: jax.ShapeDtypeStruct((s, D), dt)
    return pl.pallas_call(
        body, name="proj_fwd", grid=(s // tm,),
        in_specs=[tok, pl.BlockSpec((8, D), lambda i: (0, 0)), pl.BlockSpec((16, D), lambda i: (0, 0)),
                  _resident((D, D_IN))],
        out_specs=[tok] * 7,
        out_shape=[sd(BF16)] + [sd(F32)] * 6,
        compiler_params=_params(("parallel",)),
    )(x, modr, vecs, w_in)


def _mix_fwd(x_rnn, u_pool, ga, vecs, w_rg_a, w_rg_x, w_pool):
    s = x_rnn.shape[0]
    tm = min(TM_MIX, s)
    nb = s // tm

    def body(xh_ref, x_ref, uh_ref, u_ref, ga_ref, vec_ref, wa_ref, wx_ref, wp_ref,
             xr_ref, hr_ref, za_ref, p_ref, pooled_ref, a_ref, mult_ref, ra_ref, ri_ref, carry_ref, scan_scr):
        i = pl.program_id(0)
        first = i == 0

        @pl.when(first)
        def _():
            carry_ref[...] = jnp.zeros_like(carry_ref)

        row = lax.broadcasted_iota(jnp.int32, (tm, GW), 0)
        is_t0 = jnp.logical_and(first, row == 0)
        head_t = (lax.broadcasted_iota(jnp.int32, (HALO_U, GW), 0) + 1).astype(F32)
        for g in range(N_GROUPS):
            cs = slice(g * GW, (g + 1) * GW)
            vec = vec_ref[:, cs]
            xh = jnp.where(first, 0.0, xh_ref[:, cs])
            taps = _conv_taps(jnp.concatenate([xh, x_ref[:, cs]], axis=0))
            xr = vec[V_CONV_B:V_CONV_B + 1]
            for j in range(4):
                xr = xr + vec[V_CONV_W + j:V_CONV_W + j + 1] * taps[j]
            xr_ref[:, cs] = xr
            ra, ri, _, a, mult = _rglru_gates(
                xr, wa_ref[g], wx_ref[g], vec[V_B_RG_A:V_B_RG_A + 1], vec[V_B_RG_X:V_B_RG_X + 1],
                vec[V_A_PARAM:V_A_PARAM + 1], is_t0)
            a_ref[:, cs] = a
            mult_ref[:, cs] = mult
            ra_ref[:, cs] = ra.astype(BF16)
            ri_ref[:, cs] = ri.astype(BF16)
            h, last = _scan_down(a, xr * ri * mult, carry_ref[0:1, cs], scan_scr)
            hr_ref[:, cs] = h
            carry_ref[0:1, cs] = last
            za_ref[:, cs] = (ga_ref[:, cs] * h).astype(BF16)
            uh = jnp.where(first, 0.0, uh_ref[:, cs])
            sm = jnp.concatenate([uh, u_ref[:, cs]], axis=0)
            k = 1
            while k < POOL_WINDOWS[g]:
                sm = sm + _shift_down(sm, k)
                k *= 2
            mean = _window_mean(sm[HALO_U:], POOL_WINDOWS[g], first, head_t)
            p = (mean - u_ref[:, cs]).astype(BF16)
            p_ref[:, cs] = p
            pb = _dot(p, wp_ref[g]) + vec[V_B_POOL:V_B_POOL + 1]
            pooled_ref[:, cs] = (pb * vec[V_POOL_SCALE:V_POOL_SCALE + 1]).astype(BF16)

    tok = pl.BlockSpec((tm, D), lambda i: (i, 0))
    halo = lambda rows: pl.BlockSpec((rows, D), lambda i: (jnp.maximum(i * (tm // rows) - 1, 0), 0))
    wspec = pl.BlockSpec((N_GROUPS, GW, GW), lambda i: (0, 0, 0))
    sd = lambda dt: jax.ShapeDtypeStruct((s, D), dt)
    return pl.pallas_call(
        body, name="mix_fwd", grid=(nb,),
        in_specs=[halo(HALO_X), tok, halo(HALO_U), tok, tok, pl.BlockSpec((16, D), lambda i: (0, 0)),
                  wspec, wspec, wspec],
        out_specs=[tok] * 9,
        out_shape=[sd(F32), sd(F32), sd(BF16), sd(BF16), sd(BF16), sd(F32), sd(F32), sd(BF16), sd(BF16)],
        scratch_shapes=[pltpu.VMEM((8, D), F32), pltpu.VMEM((3, tm, LANES), F32)],
        compiler_params=_params(("arbitrary",)),
    )(x_rnn, x_rnn, u_pool, u_pool, ga, vecs, w_rg_a, w_rg_x, w_pool)


def _branch_fwd(za, pooled, sa, sb, x, modr, vecs, w_a, w_b, w_out):
    s = x.shape[0]
    tm = min(TM_BRANCH, s)

    def body(za_ref, pooled_ref, sa_ref, sb_ref, x_ref, mod_ref, vec_ref, wa_ref, wb_ref, wo_ref,
             ba_ref, bb_ref, merged_ref, o_ref, x2_ref, h2_ref):
        ba = _dot(za_ref[...], wa_ref[...])
        bb = _dot(pooled_ref[...], wb_ref[...])
        ba_ref[...] = ba.astype(BF16)
        bb_ref[...] = bb.astype(BF16)
        merged = (sa_ref[...] * ba + sb_ref[...] * bb).astype(BF16)
        merged_ref[...] = merged
        o = _dot(merged, wo_ref[...])
        o_ref[...] = o.astype(BF16)
        x2 = x_ref[...] + mod_ref[M_GT1:M_GT1 + 1, :] * o
        x2_ref[...] = x2
        r = lax.rsqrt(jnp.mean(x2 * x2, axis=-1, keepdims=True) + EPS)
        gain = vec_ref[V_G2:V_G2 + 1, :] * (1.0 + mod_ref[M_SC2:M_SC2 + 1, :])
        h2_ref[...] = (x2 * r * gain + mod_ref[M_SH2:M_SH2 + 1, :]).astype(BF16)

    tok = pl.BlockSpec((tm, D), lambda i: (i, 0))
    wspec = pl.BlockSpec((D, D), lambda i: (0, 0))
    sd = lambda dt: jax.ShapeDtypeStruct((s, D), dt)
    return pl.pallas_call(
        body, name="branch_fwd", grid=(s // tm,),
        in_specs=[tok, tok, tok, tok,
                  tok, pl.BlockSpec((8, D), lambda i: (0, 0)), pl.BlockSpec((16, D), lambda i: (0, 0)),
                  wspec, wspec, wspec],
        out_specs=[tok] * 6,
        out_shape=[sd(BF16), sd(BF16), sd(BF16), sd(BF16), sd(F32), sd(BF16)],
        compiler_params=_params(("parallel",)),
    )(za, pooled, sa, sb, x, modr, vecs, w_a, w_b, w_out)


def _mlp_fwd(h2, x2, target, modr, vecs, w_up, w_down):
    s = x2.shape[0]
    tm = min(TM_MLP, s)

    def body(h2_ref, x2_ref, tgt_ref, mod_ref, vec_ref, wu_ref, wd_ref,
             ru_ref, dx3_ref, ddn_ref, small_ref):
        @pl.when(pl.program_id(0) == 0)
        def _():
            small_ref[...] = jnp.zeros_like(small_ref)

        h2 = h2_ref[...]
        dn = None
        for c in range(D_FF // D):
            cs = slice(c * D, (c + 1) * D)
            ru = jnp.maximum(_dot(h2, wu_ref[:, cs]), 0.0)
            ru_ref[:, cs] = ru.astype(BF16)
            part = _dot((ru * ru).astype(BF16), wd_ref[cs, :])
            dn = part if dn is None else dn + part
        gt2 = mod_ref[M_GT2:M_GT2 + 1, :]
        gf = vec_ref[V_GF:V_GF + 1, :]
        x3 = x2_ref[...] + gt2 * dn
        r3 = lax.rsqrt(jnp.mean(x3 * x3, axis=-1, keepdims=True) + EPS)
        n3 = x3 * r3
        err = n3 * gf - tgt_ref[...]
        dy = err * (1.0 / D)
        dn3 = dy * gf
        dx3 = r3 * (dn3 - n3 * jnp.mean(dn3 * n3, axis=-1, keepdims=True))
        dx3_ref[...] = dx3
        ddn_ref[...] = (dx3 * gt2).astype(BF16)
        small_ref[0:1, :] += jnp.sum(dy * n3, axis=0, keepdims=True)
        small_ref[1:2, :] += jnp.sum(dx3 * dn, axis=0, keepdims=True)
        small_ref[2:3, :] += (0.5 / D) * jnp.sum(err * err, axis=0, keepdims=True)

    tok = pl.BlockSpec((tm, D), lambda i: (i, 0))
    return pl.pallas_call(
        body, name="mlp_fwd", grid=(s // tm,),
        in_specs=[tok, tok, tok,
                  pl.BlockSpec((8, D), lambda i: (0, 0)), pl.BlockSpec((16, D), lambda i: (0, 0)),
                  _resident((D, D_FF)), _resident((D_FF, D))],
        out_specs=[pl.BlockSpec((tm, D_FF), lambda i: (i, 0)), tok, tok,
                   pl.BlockSpec((8, D), lambda i: (0, 0))],
        out_shape=[jax.ShapeDtypeStruct((s, D_FF), BF16), jax.ShapeDtypeStruct((s, D), F32),
                   jax.ShapeDtypeStruct((s, D), BF16), jax.ShapeDtypeStruct((8, D), F32)],
        compiler_params=_params(("arbitrary",)),
    )(h2, x2, target, modr, vecs, w_up, w_down)


def _mlp_bwd(d_dn, ru, x2, dx3, o, modr, vecs, w_up, w_down):
    s = x2.shape[0]
    tm = min(TM_MLP_BWD, s)

    def body(ddn_ref, ru_ref, x2_ref, dx3_ref, o_ref, mod_ref, vec_ref, wu_ref, wd_ref,
             dup_ref, dx2_ref, do_ref, small_ref):
        @pl.when(pl.program_id(0) == 0)
        def _():
            small_ref[...] = jnp.zeros_like(small_ref)

        ddn = ddn_ref[...]
        dh2 = None
        for c in range(D_FF // D):
            cs = slice(c * D, (c + 1) * D)
            dff = _dot_nt(ddn, wd_ref[cs, :])
            dup = (dff * (2.0 * ru_ref[:, cs].astype(F32))).astype(BF16)
            dup_ref[:, cs] = dup
            part = _dot_nt(dup, wu_ref[:, cs])
            dh2 = part if dh2 is None else dh2 + part
        x2 = x2_ref[...]
        r2 = lax.rsqrt(jnp.mean(x2 * x2, axis=-1, keepdims=True) + EPS)
        xn2 = x2 * r2
        gain = vec_ref[V_G2:V_G2 + 1, :] * (1.0 + mod_ref[M_SC2:M_SC2 + 1, :])
        dxn2 = dh2 * gain
        dx2 = dx3_ref[...] + r2 * (dxn2 - xn2 * jnp.mean(dxn2 * xn2, axis=-1, keepdims=True))
        dx2_ref[...] = dx2
        do_ref[...] = (dx2 * mod_ref[M_GT1:M_GT1 + 1, :]).astype(BF16)
        small_ref[0:1, :] += jnp.sum(dh2, axis=0, keepdims=True)
        small_ref[1:2, :] += jnp.sum(dh2 * xn2, axis=0, keepdims=True)
        small_ref[2:3, :] += jnp.sum(dx2 * o_ref[...].astype(F32), axis=0, keepdims=True)

    tok = pl.BlockSpec((tm, D), lambda i: (i, 0))
    wide = pl.BlockSpec((tm, D_FF), lambda i: (i, 0))
    return pl.pallas_call(
        body, name="mlp_bwd", grid=(s // tm,),
        in_specs=[tok, wide, tok, tok, tok,
                  pl.BlockSpec((8, D), lambda i: (0, 0)), pl.BlockSpec((16, D), lambda i: (0, 0)),
                  _resident((D, D_FF)), _resident((D_FF, D))],
        out_specs=[wide, tok, tok, pl.BlockSpec((8, D), lambda i: (0, 0))],
        out_shape=[jax.ShapeDtypeStruct((s, D_FF), BF16), jax.ShapeDtypeStruct((s, D), F32),
                   jax.ShapeDtypeStruct((s, D), BF16), jax.ShapeDtypeStruct((8, D), F32)],
        compiler_params=_params(("arbitrary",)),
    )(d_dn, ru, x2, dx3, o, modr, vecs, w_up, w_down)


def _branch_bwd(do, sa, sb, ba, bb, w_a, w_b, w_out, dep):
    s = do.shape[0]
    tm = min(TM_BRANCH, s)

    def body(do_ref, sa_ref, sb_ref, ba_ref, bb_ref, wa_ref, wb_ref, wo_ref, dep_ref,
             dba_ref, dbb_ref, dg_ref, dza_ref, dpooled_ref):
        dmerged = _dot_nt(do_ref[...], wo_ref[...])
        sa = sa_ref[...]
        sb = sb_ref[...]
        dba = (dmerged * sa).astype(BF16)
        dbb = (dmerged * sb).astype(BF16)
        dba_ref[...] = dba
        dbb_ref[...] = dbb
        dg_ref[:, :D] = (dmerged * ba_ref[...].astype(F32) * sa * (1.0 - sa)).astype(BF16)
        dg_ref[:, D:] = (dmerged * bb_ref[...].astype(F32) * sb * (1.0 - sb)).astype(BF16)
        dza_ref[...] = _dot_nt(dba, wa_ref[...])
        dpooled_ref[...] = _dot_nt(dbb, wb_ref[...])

    tok = pl.BlockSpec((tm, D), lambda i: (i, 0))
    wspec = pl.BlockSpec((D, D), lambda i: (0, 0))
    sd = lambda dt: jax.ShapeDtypeStruct((s, D), dt)
    return pl.pallas_call(
        body, name="branch_bwd", grid=(s // tm,),
        in_specs=[tok, tok, tok, tok, tok, wspec, wspec, wspec, pl.BlockSpec(memory_space=pl.ANY)],
        out_specs=[tok, tok, pl.BlockSpec((tm, 2 * D), lambda i: (i, 0)), tok, tok],
        out_shape=[sd(BF16), sd(BF16), jax.ShapeDtypeStruct((s, 2 * D), BF16), sd(F32), sd(F32)],
        compiler_params=_params(("parallel",)),
    )(do, sa, sb, ba, bb, w_a, w_b, w_out, dep)


def _mix_bwd(dza, dpooled, x_rnn, ga, dga, xr, hr, p, gates, dgates, vecs, w_rg_a, w_rg_x, w_pool):
    s = xr.shape[0]
    tm = min(TM_MIX, s)
    nb = s // tm

    def body(dza_ref, dpooled_ref, xh_ref, x_ref, ga_ref, dga_ref, xr_ref, hh_ref, hr_ref, p_ref,
             a_ref, mult_ref, ra_ref, ri_ref, dg_ref, vec_ref, wa_ref, wx_ref, wp_ref,
             dproj_ref, dwa_ref, dwx_ref, dwp_ref, small_ref,
             scan_carry, dxr_carry, q_carry, scan_scr, dwa_acc, dwx_acc, dwp_acc):
        i = pl.program_id(0)
        bi = nb - 1 - i
        first_t = bi == 0

        @pl.when(i == 0)
        def _():
            scan_carry[...] = jnp.zeros_like(scan_carry)
            dxr_carry[...] = jnp.zeros_like(dxr_carry)
            q_carry[...] = jnp.zeros_like(q_carry)
            dwa_acc[...] = jnp.zeros_like(dwa_acc)
            dwx_acc[...] = jnp.zeros_like(dwx_acc)
            dwp_acc[...] = jnp.zeros_like(dwp_acc)
            small_ref[...] = jnp.zeros_like(small_ref)

        row = lax.broadcasted_iota(jnp.int32, (tm, GW), 0)
        is_t0 = jnp.logical_and(first_t, row == 0)
        head_t = (lax.broadcasted_iota(jnp.int32, (HALO_U, GW), 0) + 1).astype(F32)
        colsum = lambda v: jnp.sum(v, axis=0, keepdims=True)
        for g in range(N_GROUPS):
            cs = slice(g * GW, (g + 1) * GW)
            vec = vec_ref[:, cs]
            xr = xr_ref[:, cs]
            hr = hr_ref[:, cs]
            dza = dza_ref[:, cs]
            dproj_ref[:, D + g * GW:D + (g + 1) * GW] = (dza * hr * dga_ref[:, cs]).astype(BF16)
            dhr = dza * ga_ref[:, cs]
            a = a_ref[:, cs]
            mult = mult_ref[:, cs]
            ra = ra_ref[:, cs].astype(F32)
            ri = ri_ref[:, cs].astype(F32)
            sp = _softplus(vec[V_A_PARAM:V_A_PARAM + 1])
            m = jnp.where(row == tm - 1, 1.0, _shift_up(a, 1))
            gsum = _scan_up(m, dhr, scan_carry[0:1, cs], scan_scr)
            scan_carry[0:1, cs] = a[0:1, :] * gsum[0:1, :]
            hh = jnp.where(first_t, 0.0, hh_ref[:, cs])
            hprev = _shift_down(jnp.concatenate([hh, hr], axis=0), 1)[8:]
            da = gsum * hprev
            dmult = jnp.where(is_t0, 0.0, gsum * xr * ri)
            dlog_a = da * a - dmult * a * a / mult
            dri = gsum * xr * mult
            dxr = gsum * ri * mult
            small_ref[7:8, cs] += colsum((-C_RG) * ra * dlog_a)
            dpa = (((-C_RG) * sp) * dlog_a * ra * (1.0 - ra))
            dpx = dri * ri * (1.0 - ri)
            small_ref[5:6, cs] += colsum(dpa)
            small_ref[6:7, cs] += colsum(dpx)
            dpa = dpa.astype(BF16)
            dpx = dpx.astype(BF16)
            xrb = xr.astype(BF16)
            dwa_acc[g] += _dot_tn(xrb, dpa)
            dwx_acc[g] += _dot_tn(xrb, dpx)
            dxr = dxr + _dot_nt(dpa, wa_ref[g]) + _dot_nt(dpx, wx_ref[g])
            small_ref[4:5, cs] += colsum(dxr)
            xh = jnp.where(first_t, 0.0, xh_ref[:, cs])
            taps = _conv_taps(jnp.concatenate([xh, x_ref[:, cs]], axis=0))
            dxr_ext = jnp.concatenate([dxr, dxr_carry[:, cs]], axis=0)
            dx = vec[V_CONV_W + 3:V_CONV_W + 4] * dxr
            for j in range(4):
                small_ref[j:j + 1, cs] += colsum(dxr * taps[j])
                if j < 3:
                    dx = dx + vec[V_CONV_W + j:V_CONV_W + j + 1] * _shift_up(dxr_ext, 3 - j)[:tm]
            dxr_carry[:, cs] = dxr[0:8, :]
            dproj_ref[:, cs] = dx.astype(BF16)
            pg = p_ref[:, cs]
            dpooled = dpooled_ref[:, cs]
            pb = _dot(pg, wp_ref[g]) + vec[V_B_POOL:V_B_POOL + 1]
            small_ref[9:10, cs] += colsum(dpooled * pb)
            dpb = dpooled * vec[V_POOL_SCALE:V_POOL_SCALE + 1]
            small_ref[8:9, cs] += colsum(dpb)
            dpbb = dpb.astype(BF16)
            dwp_acc[g] += _dot_tn(pg, dpbb)
            dp = _dot_nt(dpbb, wp_ref[g])
            q = _window_mean(dp, POOL_WINDOWS[g], first_t, head_t)
            sm = jnp.concatenate([q, q_carry[:, cs]], axis=0)
            k = 1
            while k < POOL_WINDOWS[g]:
                sm = sm + _shift_up(sm, k)
                k *= 2
            q_carry[:, cs] = q[0:HALO_U, :]
            dproj_ref[:, 2 * D + g * GW:2 * D + (g + 1) * GW] = (sm[:tm] - dp).astype(BF16)
        dproj_ref[:, 3 * D:] = dg_ref[...]

        @pl.when(i == nb - 1)
        def _():
            dwa_ref[...] = dwa_acc[...].astype(BF16)
            dwx_ref[...] = dwx_acc[...].astype(BF16)
            dwp_ref[...] = dwp_acc[...].astype(BF16)

    rev = lambda i: nb - 1 - i
    tok = pl.BlockSpec((tm, D), lambda i: (rev(i), 0))
    halo8 = lambda k: pl.BlockSpec((8, D), lambda i: (jnp.maximum(rev(i) * (tm // 8) - 1, 0), k))
    wspec = pl.BlockSpec((N_GROUPS, GW, GW), lambda i: (0, 0, 0))
    wshape = jax.ShapeDtypeStruct((N_GROUPS, GW, GW), BF16)
    return pl.pallas_call(
        body, name="mix_bwd", grid=(nb,),
        in_specs=[tok, tok, halo8(0), tok, tok, tok, tok, halo8(0), tok, tok, tok, tok, tok, tok,
                  pl.BlockSpec((tm, 2 * D), lambda i: (rev(i), 0)),
                  pl.BlockSpec((16, D), lambda i: (0, 0)), wspec, wspec, wspec],
        out_specs=[pl.BlockSpec((tm, D_IN), lambda i: (rev(i), 0)), wspec, wspec, wspec,
                   pl.BlockSpec((16, D), lambda i: (0, 0))],
        out_shape=[jax.ShapeDtypeStruct((s, D_IN), BF16), wshape, wshape, wshape,
                   jax.ShapeDtypeStruct((16, D), F32)],
        scratch_shapes=[pltpu.VMEM((8, D), F32), pltpu.VMEM((8, D), F32), pltpu.VMEM((HALO_U, D), F32),
                        pltpu.VMEM((3, tm, LANES), F32)] + [pltpu.VMEM((N_GROUPS, GW, GW), F32)] * 3,
        compiler_params=_params(("arbitrary",)),
    )(dza, dpooled, x_rnn, x_rnn, ga, dga, xr, hr, hr, p, *gates, dgates, vecs, w_rg_a, w_rg_x, w_pool)


def _proj_bwd(dproj, x, dx2, modr, vecs, w_in):
    s = x.shape[0]
    tm = min(TM_PROJ, s)

    def body(dp_ref, x_ref, dx2_ref, mod_ref, vec_ref, w_ref, gx_ref, small_ref):
        @pl.when(pl.program_id(0) == 0)
        def _():
            small_ref[...] = jnp.zeros_like(small_ref)

        dh1 = None
        for c in range(D_IN // D):
            cs = slice(c * D, (c + 1) * D)
            part = _dot_nt(dp_ref[:, cs], w_ref[:, cs])
            dh1 = part if dh1 is None else dh1 + part
        xv = x_ref[...]
        r1 = lax.rsqrt(jnp.mean(xv * xv, axis=-1, keepdims=True) + EPS)
        xn1 = xv * r1
        gain = vec_ref[V_G1:V_G1 + 1, :] * (1.0 + mod_ref[M_SC1:M_SC1 + 1, :])
        dxn1 = dh1 * gain
        gx_ref[...] = dx2_ref[...] + r1 * (dxn1 - xn1 * jnp.mean(dxn1 * xn1, axis=-1, keepdims=True))
        small_ref[0:1, :] += jnp.sum(dh1, axis=0, keepdims=True)
        small_ref[1:2, :] += jnp.sum(dh1 * xn1, axis=0, keepdims=True)

    tok = pl.BlockSpec((tm, D), lambda i: (i, 0))
    return pl.pallas_call(
        body, name="proj_bwd", grid=(s // tm,),
        in_specs=[pl.BlockSpec((tm, D_IN), lambda i: (i, 0)), tok, tok,
                  pl.BlockSpec((8, D), lambda i: (0, 0)), pl.BlockSpec((16, D), lambda i: (0, 0)),
                  _resident((D, D_IN))],
        out_specs=[tok, pl.BlockSpec((8, D), lambda i: (0, 0))],
        out_shape=[jax.ShapeDtypeStruct((s, D), F32), jax.ShapeDtypeStruct((8, D), F32)],
        compiler_params=_params(("arbitrary",)),
    )(dproj, x, dx2, modr, vecs, w_in)


def _wgrad(a, b, name, square_a=False, dep=None):
    s, ka = a.shape
    n = b.shape[1]
    tka = ka if ka <= 1024 else ka // 2
    tn = n if n <= 1024 else n // 2
    ts = min(TS_WGRAD, s)
    ns = s // ts
    nc = 512
    deps = [] if dep is None else [dep]

    def body(a_ref, b_ref, *refs):
        out_ref, acc_ref = refs[-2:]
        t = pl.program_id(2)

        @pl.when(t == 0)
        def _():
            acc_ref[...] = jnp.zeros_like(acc_ref)

        av = a_ref[...]
        if square_a:
            af = av.astype(F32)
            av = (af * af).astype(BF16)
        for c in range(tn // nc):
            cs = slice(c * nc, (c + 1) * nc)
            acc_ref[:, cs] += _dot_tn(av, b_ref[:, cs])

        @pl.when(t == ns - 1)
        def _():
            out_ref[...] = acc_ref[...].astype(BF16)

    return pl.pallas_call(
        body, name=name, grid=(ka // tka, n // tn, ns),
        in_specs=[pl.BlockSpec((ts, tka), lambda i, j, t: (t, i)),
                  pl.BlockSpec((ts, tn), lambda i, j, t: (t, j))] + [pl.BlockSpec(memory_space=pl.ANY)] * len(deps),
        out_specs=pl.BlockSpec((tka, tn), lambda i, j, t: (i, j)),
        out_shape=jax.ShapeDtypeStruct((ka, n), BF16),
        scratch_shapes=[pltpu.VMEM((tka, tn), F32)],
        compiler_params=_params(("parallel", "parallel", "arbitrary")),
    )(a, b, *deps)


def _window(ref, kind, idx, size):
    start = pl.multiple_of(idx * size, size)
    if kind == 0:
        return ref.at[pl.ds(start, size)]
    if kind == 1:
        return ref.at[:, pl.ds(start, size)]
    return ref.at[:, :, pl.ds(start, size)]


def _mesh_place():
    x, y, c = lax.axis_index("x"), lax.axis_index("y"), lax.axis_index("c")
    return x, y, c, 4 * x + 2 * y + c


def _peer(x, y, c, q):
    px = 1 - x if q & 4 else x
    py = 1 - y if q & 2 else y
    pc = 1 - c if q & 1 else c
    return (px, py, pc), 4 * px + 2 * py + pc


def _all_gather(shards, kinds, name):
    n = len(shards)
    full_shapes = []
    for sh, kind in zip(shards, kinds):
        dims = list(sh.shape)
        dims[kind] *= N_DEV
        full_shapes.append(jax.ShapeDtypeStruct(tuple(dims), sh.dtype))

    def body(*refs):
        ins, outs = refs[:n], refs[n:2 * n]
        send_sems, recv_sems, local_sems = refs[2 * n:]
        x, y, c, me = _mesh_place()
        sends, recvs, locals_ = [], [], []
        for k in range(n):
            size = shards[k].shape[kinds[k]]
            mine = _window(outs[k], kinds[k], me, size)
            lc = pltpu.make_async_copy(ins[k], mine, local_sems.at[k])
            lc.start()
            locals_.append(lc)
            for q in range(1, N_DEV):
                peer, peer_idx = _peer(x, y, c, q)
                cp = pltpu.make_async_remote_copy(
                    src_ref=ins[k], dst_ref=mine, send_sem=send_sems.at[k, q], recv_sem=recv_sems.at[k, q],
                    device_id=peer, device_id_type=MESH)
                cp.start()
                sends.append(cp)
                recvs.append(pltpu.make_async_remote_copy(
                    src_ref=ins[k], dst_ref=_window(outs[k], kinds[k], peer_idx, size),
                    send_sem=send_sems.at[k, q], recv_sem=recv_sems.at[k, q],
                    device_id=peer, device_id_type=MESH))
        for cp in recvs:
            cp.wait_recv()
        for cp in sends:
            cp.wait_send()
        for lc in locals_:
            lc.wait()

    any_spec = pl.BlockSpec(memory_space=pl.ANY)
    return pl.pallas_call(
        body, name=name,
        in_specs=[any_spec] * n, out_specs=[any_spec] * n, out_shape=full_shapes,
        scratch_shapes=[pltpu.SemaphoreType.DMA((n, N_DEV)), pltpu.SemaphoreType.DMA((n, N_DEV)),
                        pltpu.SemaphoreType.DMA((n,))],
    )(*shards)


_HBM = pl.BlockSpec(memory_space=pltpu.HBM)
_SEM = pl.BlockSpec(memory_space=pltpu.SEMAPHORE)
_EFFECT = pltpu.SideEffectType.DATAFLOW_SIDE_EFFECTING


N_NEAR = 4


def _near(x, y, c):
    out = [((x, y, 1 - c), 4 * x + 2 * y + 1 - c)]
    for j in (1, 2, 3):
        px = 1 - x if j & 2 else x
        py = 1 - y if j & 1 else y
        out.append(((px, py, c), 4 * px + 2 * py + c))
    return out


def _remote(src, dst, send_sems, recv_sems, slot, device):
    return pltpu.make_async_remote_copy(src_ref=src, dst_ref=dst, send_sem=send_sems.at[slot], recv_sem=recv_sems.at[slot],
                                        device_id=device, device_id_type=MESH)


def _split_call(name, arrays, sems_in, n_new_sems, after, emit):
    na, ns, nn = len(arrays), len(sems_in), len(n_new_sems)

    def body(*refs):
        emit(refs[:na], refs[na:na + ns], refs[na + ns + 1:na + ns + 1 + nn])
        refs[-1][...] = jnp.zeros_like(refs[-1])

    outs = pl.pallas_call(
        body, name=name,
        out_shape=(*[pltpu.SemaphoreType.DMA((m,)) for m in n_new_sems],
                   *[pltpu.HBM(a.shape, a.dtype) for a in arrays], jax.ShapeDtypeStruct((8, 128), F32)),
        in_specs=[_HBM] * na + [_SEM] * ns + [pl.BlockSpec(memory_space=pl.ANY)],
        out_specs=(*[_SEM] * nn, *[_HBM] * na, pl.BlockSpec(memory_space=pltpu.VMEM)),
        input_output_aliases={i: nn + i for i in range(na)},
        compiler_params=pltpu.CompilerParams(has_side_effects=_EFFECT),
    )(*[pltpu.with_memory_space_constraint(a, pltpu.HBM) for a in arrays], *sems_in, after)
    return list(outs[:nn]), list(outs[nn:nn + na]), outs[-1]


class _Gather:
    def __init__(self, shards, kinds, after, name):
        self.n, self.kinds, self.name = len(shards), kinds, name
        self.sizes = [s.shape[k] for s, k in zip(shards, kinds)]
        n = self.n
        lands = []
        for s, k in zip(shards, kinds):
            dims = list(s.shape)
            dims[k] *= N_DEV
            lands.append(lax.empty(tuple(dims), s.dtype))

        def emit(arr, _, new):
            x, y, c, me = _mesh_place()
            for k in range(n):
                pltpu.make_async_copy(arr[k], _window(arr[n + k], kinds[k], me, self.sizes[k]), new[2].at[k]).start()
            for k in range(n):
                mine = _window(arr[n + k], kinds[k], me, self.sizes[k])
                for j, (dev, _) in enumerate(_near(x, y, c)):
                    _remote(arr[k], mine, new[0], new[1], k * N_NEAR + j, dev).start()

        self.sems, self.arrays, self.token = _split_call(name + "_start", [*shards, *lands], [],
                                                         [n * N_NEAR, n * N_NEAR, n], after, emit)

    def forward(self, after):
        n, kinds, sizes = self.n, self.kinds, self.sizes

        def emit(arr, old, new):
            x, y, c, _ = _mesh_place()
            near = _near(x, y, c)
            for k in range(n):
                for j in (1, 2, 3):
                    dev, idx = near[j]
                    landed = _window(arr[n + k], kinds[k], idx, sizes[k])
                    _remote(arr[k], landed, old[0], old[1], k * N_NEAR + j, dev).wait_recv()
                    _remote(landed, landed, new[0], new[1], k * N_NEAR + j, near[0][0]).start()

        new, self.arrays, self.token = _split_call(self.name + "_forward", self.arrays, self.sems, [n * N_NEAR] * 2,
                                                   after, emit)
        self.sems = [*self.sems, *new]

    def finish(self, after):
        n, kinds, sizes = self.n, self.kinds, self.sizes

        def emit(arr, old, _):
            x, y, c, me = _mesh_place()
            near = _near(x, y, c)
            other_core = near[0][0]
            for k in range(n):
                win = lambda idx: _window(arr[n + k], kinds[k], idx, sizes[k])
                pltpu.make_async_copy(arr[k], win(me), old[2].at[k]).wait()
                for j, (dev, idx) in enumerate(near):
                    _remote(arr[k], win(me), old[0], old[1], k * N_NEAR + j, dev).wait_send()
                _remote(arr[k], win(near[0][1]), old[0], old[1], k * N_NEAR, other_core).wait_recv()
                for j in (1, 2, 3):
                    idx = near[j][1]
                    _remote(win(idx), win(idx), old[3], old[4], k * N_NEAR + j, other_core).wait_send()
                    _remote(arr[k], win(idx + 1 - 2 * c), old[3], old[4], k * N_NEAR + j, other_core).wait_recv()

        _, arrays, _ = _split_call(self.name + "_finish", self.arrays, self.sems, [], after, emit)
        return arrays[n:]


class _Scatter:
    def __init__(self, partials, kinds, after, name):
        self.n, self.kinds, self.name, self.partials = len(partials), kinds, name, partials
        self.sizes = [p.shape[k] // N_DEV for p, k in zip(partials, kinds)]
        n, sizes = self.n, self.sizes
        self.slot_shapes = []
        for p, k, size in zip(partials, kinds, sizes):
            dims = list(p.shape)
            dims[k] = size
            self.slot_shapes.append((N_NEAR, *dims))
        slots = [lax.empty(sh, p.dtype) for sh, p in zip(self.slot_shapes, partials)]

        def emit(arr, _, new):
            x, y, c, _ = _mesh_place()
            near = _near(x, y, c)
            for k in range(n):
                for j in range(N_NEAR):
                    owner = near[j][1] if j == 0 else near[j][1] + 1 - 2 * c
                    _remote(_window(arr[k], kinds[k], owner, sizes[k]), arr[n + k].at[j], new[0], new[1],
                            k * N_NEAR + j, near[0][0]).start()

        self.sems, self.arrays, self.token = _split_call(name + "_start", [*partials, *slots], [], [n * N_NEAR] * 2,
                                                         after, emit)

    def combine_and_send(self, own4, after):
        n, kinds, sizes = self.n, self.kinds, self.sizes

        def emit_wait(arr, old, _):
            x, y, c, _ = _mesh_place()
            near = _near(x, y, c)
            for k in range(n):
                for j in range(N_NEAR):
                    owner = near[j][1] if j == 0 else near[j][1] + 1 - 2 * c
                    cp = _remote(_window(arr[k], kinds[k], owner, sizes[k]), arr[n + k].at[j], old[0], old[1],
                                 k * N_NEAR + j, near[0][0])
                    cp.wait_send()
                    cp.wait_recv()

        _, arrays, _ = _split_call(self.name + "_landed", self.arrays, self.sems, [], after, emit_wait)
        chip_sums = _chip_sums(arrays[:n], arrays[n:], kinds, sizes, own4, self.name + "_combine")
        arrivals = [lax.empty((N_NEAR - 1, *sh[1:]), p.dtype) for sh, p in zip(self.slot_shapes, self.partials)]

        def emit_send(arr, _, new):
            x, y, c, _ = _mesh_place()
            near = _near(x, y, c)
            for k in range(n):
                for j in (1, 2, 3):
                    _remote(arr[k].at[j], arr[n + k].at[j - 1], new[0], new[1], k * N_NEAR + j, near[j][0]).start()

        self.sems, self.arrays, self.token = _split_call(self.name + "_send", [*chip_sums, *arrivals], [],
                                                         [n * N_NEAR] * 2, own4, emit_send)

    def finish(self, after):
        n = self.n

        def emit(arr, old, _):
            x, y, c, _ = _mesh_place()
            near = _near(x, y, c)
            for k in range(n):
                for j in (1, 2, 3):
                    cp = _remote(arr[k].at[j], arr[n + k].at[j - 1], old[0], old[1], k * N_NEAR + j, near[j][0])
                    cp.wait_send()
                    cp.wait_recv()

        _, arrays, _ = _split_call(self.name + "_finish", self.arrays, self.sems, [], after, emit)
        return arrays[:n], arrays[n:]


def _chip_sums(partials, slots, kinds, sizes, own4, name):
    n = len(partials)

    def body(own_ref, *refs):
        for k in range(n):
            refs[2 * n + k][...] = (refs[k][...].astype(F32) + refs[n + k][...].astype(F32)).astype(BF16)

    in_specs, slot_specs = [], []
    for p, s, kind, size in zip(partials, slots, kinds, sizes):
        block = list(p.shape)
        block[kind] = size
        nd = len(block)
        in_specs.append(pl.BlockSpec(tuple(block), functools.partial(
            lambda j, own, kind, nd: tuple(own[j] if d == kind else 0 for d in range(nd)), kind=kind, nd=nd)))
        slot_specs.append(pl.BlockSpec((None, *block), functools.partial(
            lambda j, own, nd: (j,) + (0,) * nd, nd=nd)))
    return pl.pallas_call(
        body, name=name,
        grid_spec=pltpu.PrefetchScalarGridSpec(num_scalar_prefetch=1, grid=(N_NEAR,),
                                               in_specs=in_specs + slot_specs, out_specs=slot_specs),
        out_shape=[jax.ShapeDtypeStruct(s.shape, s.dtype) for s in slots],
        compiler_params=_params(("arbitrary",)),
    )(own4, *partials, *slots)


def _after(small, token):
    return small + token[0:1, 0:1].astype(small.dtype)


def _silu(c):
    return c * _sigmoid_tail(c)


def _ada_fwd(c_all, w_ada, b_ada_cols):
    def body(c_ref, w_ref, b_ref, out_ref):
        out_ref[...] = jnp.dot(_silu(c_ref[...]), w_ref[...], preferred_element_type=F32,
                               precision=lax.Precision.HIGHEST) + b_ref[...]

    return pl.pallas_call(
        body, name="ada_fwd", out_shape=jax.ShapeDtypeStruct((N_DEV, w_ada.shape[1]), F32),
    )(c_all, w_ada, b_ada_cols)


def _adam(w, g, m, v):
    m = ADAM_B1 * m + (1.0 - ADAM_B1) * g
    v = ADAM_B2 * v + (1.0 - ADAM_B2) * (g * g)
    m_hat = m / (1.0 - ADAM_B1 ** ADAM_STEP)
    v_hat = v / (1.0 - ADAM_B2 ** ADAM_STEP)
    delta = -ADAM_LR * (m_hat / (jnp.sqrt(v_hat) + ADAM_EPS) + ADAM_WD * w)
    return delta, m, v


def _ada_bwd_adam(c_all, dmod_cols, w, m, v):
    def body(c_ref, d_ref, w_ref, m_ref, v_ref, g_ref, delta_ref, nm_ref, nv_ref):
        g = lax.dot_general(_silu(c_ref[...]), d_ref[...], (((0,), (0,)), ((), ())),
                            preferred_element_type=F32, precision=lax.Precision.HIGHEST)
        g_ref[...] = g
        delta_ref[...], nm_ref[...], nv_ref[...] = _adam(w_ref[...], g, m_ref[...], v_ref[...])

    sd = jax.ShapeDtypeStruct(w.shape, F32)
    return pl.pallas_call(body, name="ada_bwd_adam", out_shape=[sd] * 4,
                          compiler_params=pltpu.CompilerParams(vmem_limit_bytes=V7X_VMEM_LIMIT),
                          )(c_all, dmod_cols, w, m, v)


def _adam_group(chip_sums, arrivals, ws, ms, vs, n_tiles, name):
    n = len(ws)

    def body(*refs):
        for k in range(n):
            c_ref, a_ref, w_ref, m_ref, v_ref = (refs[j * n + k] for j in range(5))
            g_ref, delta_ref, nm_ref, nv_ref = (refs[(5 + j) * n + k] for j in range(4))
            g = c_ref[...].astype(F32)
            for j in range(N_NEAR - 1):
                g = g + a_ref[j].astype(F32)
            g_ref[...] = g
            delta_ref[...], nm_ref[...], nv_ref[...] = _adam(w_ref[...], g, m_ref[...], v_ref[...])

    tiles = [(w.shape[0] // n_tiles, w.shape[1]) for w in ws]
    blk = [pl.BlockSpec(t, lambda i: (i, 0)) for t in tiles]
    return pl.pallas_call(
        body, name=name, grid=(n_tiles,),
        in_specs=[pl.BlockSpec((None, *t), lambda i: (0, i, 0)) for t in tiles]
        + [pl.BlockSpec((N_NEAR - 1, *t), lambda i: (0, i, 0)) for t in tiles] + blk * 3,
        out_specs=blk * 4, out_shape=[jax.ShapeDtypeStruct(w.shape, F32) for w in ws] * 4,
        compiler_params=_params(("parallel",)),
    )(*chip_sums, *arrivals, *ws, *ms, *vs)


N_SMALL = 40
N_SMALL_PARAMS = 11


def _pack_vecs(conv_w_full, rows):
    def body(cw_ref, *refs):
        out = refs[-1]
        out[...] = jnp.zeros_like(out)
        out[0:4, :] = cw_ref[0:4, :]
        for r, ref in enumerate(refs[:-1]):
            out[4 + r:5 + r, :] = ref[...]

    return pl.pallas_call(body, name="pack_vecs", out_shape=jax.ShapeDtypeStruct((16, D), F32))(conv_w_full, *rows)


def _small_finish(gathered, mod_all, vecs, ws, ms, vs):
    n = N_SMALL_PARAMS

    def body(g_ref, mod_ref, vec_ref, *refs):
        w_refs, m_refs, v_refs = refs[:n], refs[n:2 * n], refs[2 * n:3 * n]
        outs = refs[3 * n:]
        g1 = vec_ref[V_G1:V_G1 + 1, :]
        g2 = vec_ref[V_G2:V_G2 + 1, :]
        zero = jnp.zeros((1, D), F32)
        dg1, dg2, dgf, loss_lanes = zero, zero, zero, zero
        mixer = jnp.zeros((16, D), F32)
        db_ada = jnp.zeros((6, D), F32)
        for b in range(N_DEV):
            gb = g_ref[b]
            mod = mod_ref[b]
            q1 = gb[33:34]
            q2 = gb[9:10]
            dmod = jnp.concatenate([gb[32:33], q1 * g1, gb[10:11], gb[8:9], q2 * g2, gb[1:2]], axis=0)
            outs[4 * n][b] = dmod
            db_ada = db_ada + dmod
            dg1 = dg1 + q1 * (1.0 + mod[M_SC1:M_SC1 + 1])
            dg2 = dg2 + q2 * (1.0 + mod[M_SC2:M_SC2 + 1])
            dgf = dgf + gb[0:1]
            loss_lanes = loss_lanes + gb[2:3]
            mixer = mixer + gb[16:32]
        d_a_param = mixer[7:8] * _sigmoid_tail(vec_ref[V_A_PARAM:V_A_PARAM + 1, :])
        grads = [dg1, dg2, mixer[4:5], mixer[5:6], mixer[6:7], d_a_param, mixer[8:9], mixer[9:10], dgf,
                 db_ada, mixer[0:4]]
        for k in range(n):
            outs[k][...] = grads[k]
            outs[n + k][...], outs[2 * n + k][...], outs[3 * n + k][...] = _adam(
                w_refs[k][...], grads[k], m_refs[k][...], v_refs[k][...])
        outs[4 * n + 1][...] = jnp.broadcast_to(jnp.sum(loss_lanes, axis=1, keepdims=True), (8, 128))

    shapes = [jax.ShapeDtypeStruct(w.shape, F32) for w in ws]
    return pl.pallas_call(
        body, name="small_finish",
        out_shape=shapes * 4 + [jax.ShapeDtypeStruct((N_DEV, 6, D), F32), jax.ShapeDtypeStruct((8, 128), F32)],
    )(gathered, mod_all, vecs, *ws, *ms, *vs)


def _pad_rows(a, rows):
    return jnp.pad(a, ((0, rows - a.shape[0]), (0, 0)))


def kernel(x, c, norm_mix_g, norm_mlp_g, w_ada, b_ada, w_in, conv_w, conv_b, w_rg_a, b_rg_a, w_rg_x, b_rg_x, a_param, w_branch_a, w_pool, b_pool, pool_scale, w_branch_b, w_out, w_up, w_down, final_g, loss_target, m_norm_mix_g, m_norm_mlp_g, m_w_ada, m_b_ada, m_w_in, m_conv_w, m_conv_b, m_w_rg_a, m_b_rg_a, m_w_rg_x, m_b_rg_x, m_a_param, m_w_branch_a, m_w_pool, m_b_pool, m_pool_scale, m_w_branch_b, m_w_out, m_w_up, m_w_down, m_final_g, v_norm_mix_g, v_norm_mlp_g, v_w_ada, v_b_ada, v_w_in, v_conv_w, v_conv_b, v_w_rg_a, v_b_rg_a, v_w_rg_x, v_b_rg_x, v_a_param, v_w_branch_a, v_w_pool, v_b_pool, v_pool_scale, v_w_branch_b, v_w_out, v_w_up, v_w_down, v_final_g):
    me = 4 * lax.axis_index("x") + 2 * lax.axis_index("y") + lax.axis_index("c")
    s = x.shape[1]
    x2d = x.reshape(s, D)
    target = loss_target.reshape(s, D)
    n_ada = w_ada.shape[2]

    sharded = dict(w_in=(w_in[0], 1), w_up=(w_up[0], 1), w_down=(w_down[0], 0), w_branch_a=(w_branch_a[0], 0),
                   w_branch_b=(w_branch_b[0], 0), w_out=(w_out[0], 0), w_rg_a=(w_rg_a[0], 1), w_rg_x=(w_rg_x[0], 1),
                   w_pool=(w_pool[0], 1))
    kind = {k: v[1] for k, v in sharded.items()}
    shard = {k: v[0].astype(BF16) for k, v in sharded.items()}

    conv_w_full, c_rows = _all_gather([_pad_rows(conv_w[0], 8), _pad_rows(c, 8)], [1, 0], "gather_c")
    c_all = c_rows.reshape(N_DEV, 8, D)[:, 0, :]
    b_ada_cols = lax.dynamic_slice(b_ada, (0, me * n_ada), (1, n_ada))
    mod_part = _ada_fwd(c_all, w_ada[0], b_ada_cols)
    mod_parts, = _all_gather([mod_part], [0], "gather_mod")

    first_names = ["w_in", "w_rg_a", "w_rg_x", "w_pool"]
    branch_names = ["w_branch_a", "w_branch_b", "w_out"]
    mlp_names = ["w_up", "w_down"]

    def gather(group, after, name):
        return _Gather([shard[k] for k in group], [kind[k] for k in group], after, name)

    g_first = gather(first_names, mod_parts, "gather_first")
    g_branch = gather(branch_names, g_first.token, "gather_branch")
    g_mlp = gather(mlp_names, g_branch.token, "gather_mlp")

    mod_all = jnp.transpose(mod_parts.reshape(N_DEV, N_DEV, n_ada), (1, 0, 2)).reshape(N_DEV, 6, D)
    mod_all = jnp.pad(mod_all, ((0, 0), (0, 2), (0, 0)))
    modr = lax.dynamic_index_in_dim(mod_all, me, 0, keepdims=False)
    vecs = _pack_vecs(conv_w_full, [conv_b, b_rg_a, b_rg_x, a_param, b_pool, pool_scale,
                                    norm_mix_g, norm_mlp_g, final_g.reshape(1, D)])
    vecs = _after(vecs, g_mlp.token)
    g_first.forward(vecs)
    wg = dict(zip(first_names, g_first.finish(g_first.token)))

    h1, x_rnn, u_pool, ga, dga, sa, sb = _proj_fwd(x2d, modr, vecs, wg["w_in"])
    g_branch.forward(h1)
    xr, hr, za, p, pooled, *gates = _mix_fwd(x_rnn, u_pool, ga, _after(vecs, g_branch.token),
                                             wg["w_rg_a"], wg["w_rg_x"], wg["w_pool"])
    g_mlp.forward(za)
    wg.update(zip(branch_names, g_branch.finish(g_mlp.token)))
    ba, bb, merged, o, x2, h2 = _branch_fwd(za, pooled, sa, sb, x2d, modr, vecs,
                                            wg["w_branch_a"], wg["w_branch_b"], wg["w_out"])
    wg.update(zip(mlp_names, g_mlp.finish(h2)))
    ru, dx3, d_dn, small_f = _mlp_fwd(h2, x2, target, modr, vecs, wg["w_up"], wg["w_down"])

    near = _near(lax.axis_index("x"), lax.axis_index("y"), lax.axis_index("c"))
    own4 = jnp.stack([me, near[1][1], near[2][1], near[3][1]]).astype(jnp.int32)

    def scatter(group, partial, after, name):
        return _Scatter([partial[k] for k in group], [kind[k] for k in group], after, name)

    dup, dx2, do, small_m = _mlp_bwd(d_dn, ru, x2, dx3, o, modr, vecs, wg["w_up"], wg["w_down"])
    partial = dict(w_up=_wgrad(h2, dup, "wgrad_up"), w_down=_wgrad(ru, d_dn, "wgrad_down", square_a=True))
    s_mlp = scatter(mlp_names, partial, dx2, "scatter_mlp")

    dba, dbb, dgates, dza, dpooled = _branch_bwd(do, sa, sb, ba, bb, wg["w_branch_a"], wg["w_branch_b"], wg["w_out"],
                                                 dep=s_mlp.token)
    s_mlp.combine_and_send(own4, dza)
    dproj, dw_rg_a, dw_rg_x, dw_pool, small_x = _mix_bwd(dza, dpooled, x_rnn, ga, dga, xr, hr, p, gates, dgates,
                                                         _after(vecs, s_mlp.token),
                                                         wg["w_rg_a"], wg["w_rg_x"], wg["w_pool"])
    partial.update(w_branch_a=_wgrad(za, dba, "wgrad_branch_a"), w_branch_b=_wgrad(pooled, dbb, "wgrad_branch_b"),
                   w_out=_wgrad(merged, do, "wgrad_out"),
                   w_rg_a=dw_rg_a, w_rg_x=dw_rg_x, w_pool=dw_pool)
    mixer_names = ["w_rg_a", "w_rg_x", "w_pool", "w_branch_a", "w_branch_b", "w_out"]
    s_mixer = scatter(mixer_names, partial, s_mlp.token, "scatter_mixer")

    partial["w_in"] = _wgrad(h1, dproj, "wgrad_in", dep=s_mixer.token)
    s_in = scatter(["w_in"], partial, s_mixer.token, "scatter_in")
    s_mixer.combine_and_send(own4, s_in.token)
    s_in.combine_and_send(own4, s_mixer.token)
    grad_x, small_p = _proj_bwd(dproj, x2d, dx2, _after(modr, s_in.token), vecs, wg["w_in"])

    locals_ = dict(w_in=(w_in, m_w_in, v_w_in), w_up=(w_up, m_w_up, v_w_up), w_down=(w_down, m_w_down, v_w_down),
                   w_branch_a=(w_branch_a, m_w_branch_a, v_w_branch_a),
                   w_branch_b=(w_branch_b, m_w_branch_b, v_w_branch_b), w_out=(w_out, m_w_out, v_w_out),
                   w_rg_a=(w_rg_a, m_w_rg_a, v_w_rg_a), w_rg_x=(w_rg_x, m_w_rg_x, v_w_rg_x),
                   w_pool=(w_pool, m_w_pool, v_w_pool))
    res = {}

    def finish(group, exchange, after, n_tiles, name):
        chip_sums, arrivals = exchange.finish(after)
        flat = lambda t: t.reshape(-1, t.shape[-1])
        shapes = [flat(locals_[k][0]).shape for k in group]
        outs = _adam_group([cs.reshape(N_NEAR, *sh) for cs, sh in zip(chip_sums, shapes)],
                           [ar.reshape(N_NEAR - 1, *sh) for ar, sh in zip(arrivals, shapes)],
                           *[[flat(locals_[k][j]) for k in group] for j in range(3)], n_tiles, name)
        for i, k in enumerate(group):
            res[k] = [outs[j * len(group) + i].reshape(locals_[k][0].shape) for j in range(4)]
        return res[group[-1]][0]

    small = jnp.concatenate([small_f, small_m, small_x, small_p], axis=0)
    g_small = _Gather([small], [0], grad_x, "gather_small")
    done = finish(mlp_names, s_mlp, g_small.token, 4, "adam_mlp")
    done = finish(mixer_names, s_mixer, done, 2, "adam_mixer")
    g_small.forward(done)
    done = finish(["w_in"], s_in, g_small.token, 4, "adam_in")
    small_all, = g_small.finish(done)
    small_all = small_all.reshape(N_DEV, N_SMALL, D)

    def embed(cw):
        return lax.dynamic_update_slice(jnp.zeros((4, D), F32), cw[0], (0, me * (D // N_DEV)))

    def smalls(ng, nl, cb, bra, brx, ap, bp, ps, fg, ba_, cw):
        return [ng, nl, cb, bra, brx, ap, bp, ps, fg.reshape(1, D), ba_.reshape(6, D), embed(cw)]

    small_names = ["norm_mix_g", "norm_mlp_g", "conv_b", "b_rg_a", "b_rg_x", "a_param", "b_pool", "pool_scale",
                   "final_g", "b_ada", "conv_w"]
    fin = _small_finish(
        small_all, mod_all, vecs,
        smalls(norm_mix_g, norm_mlp_g, conv_b, b_rg_a, b_rg_x, a_param, b_pool, pool_scale, final_g, b_ada, conv_w),
        smalls(m_norm_mix_g, m_norm_mlp_g, m_conv_b, m_b_rg_a, m_b_rg_x, m_a_param, m_b_pool, m_pool_scale,
               m_final_g, m_b_ada, m_conv_w),
        smalls(v_norm_mix_g, v_norm_mlp_g, v_conv_b, v_b_rg_a, v_b_rg_x, v_a_param, v_b_pool, v_pool_scale,
               v_final_g, v_b_ada, v_conv_w))
    dmod_all, loss_tile = fin[4 * N_SMALL_PARAMS], fin[4 * N_SMALL_PARAMS + 1]
    dmod_cols = lax.dynamic_slice(dmod_all.reshape(N_DEV, 6 * D), (0, me * n_ada), (N_DEV, n_ada))
    res["w_ada"] = [t.reshape(w_ada.shape) for t in _ada_bwd_adam(c_all, dmod_cols, w_ada[0], m_w_ada[0], v_w_ada[0])]

    def final_shape(k, t):
        if k == "final_g":
            return t.reshape(D)
        if k == "b_ada":
            return t.reshape(1, 6 * D)
        if k == "conv_w":
            return lax.dynamic_slice(t, (0, me * (D // N_DEV)), (4, D // N_DEV)).reshape(conv_w.shape)
        return t

    for i, k in enumerate(small_names):
        res[k] = [final_shape(k, fin[which * N_SMALL_PARAMS + i]) for which in range(4)]
    order = ["norm_mix_g", "norm_mlp_g", "w_ada", "b_ada", "w_in", "conv_w", "conv_b", "w_rg_a", "b_rg_a", "w_rg_x",
             "b_rg_x", "a_param", "w_branch_a", "w_pool", "b_pool", "pool_scale", "w_branch_b", "w_out", "w_up",
             "w_down", "final_g"]
    outs = [loss_tile[0, 0], grad_x.reshape(x.shape)]
    for which in range(4):
        for k in order:
            outs.append(res[k][which])
    return tuple(outs)
```

```python
import functools

import jax
import jax.numpy as jnp
from jax import lax
from jax.experimental import pallas as pl
from jax.experimental.pallas import tpu as pltpu

F32 = jnp.float32
BF16 = jnp.bfloat16
MESH = pl.DeviceIdType.MESH

N_DEV = 8
D = 1024
N_GROUPS = 4
GW = D // N_GROUPS
D_IN = 5 * D
D_FF = 4 * D
POOL_WINDOWS = (2, 4, 8, 16)
HALO_X = 8
HALO_U = 16
EPS = 1e-6
C_RG = 8.0
ADAM_LR, ADAM_B1, ADAM_B2, ADAM_EPS, ADAM_WD, ADAM_STEP = 0.001, 0.9, 0.999, 1e-08, 0.01, 10

V7X_VMEM_LIMIT = 56 * 1024 * 1024

V_CONV_W, V_CONV_B, V_B_RG_A, V_B_RG_X, V_A_PARAM, V_B_POOL, V_POOL_SCALE, V_G1, V_G2, V_GF = 0, 4, 5, 6, 7, 8, 9, 10, 11, 12
M_SH1, M_SC1, M_GT1, M_SH2, M_SC2, M_GT2 = 0, 1, 2, 3, 4, 5

TM_PROJ = 512
TM_MIX = 256
TM_BRANCH = 512
TM_MLP = 512
TM_MLP_BWD = 256
TS_WGRAD = 1024


def _params(semantics):
    return pltpu.CompilerParams(dimension_semantics=semantics, vmem_limit_bytes=V7X_VMEM_LIMIT)


def _resident(shape):
    return pl.BlockSpec(shape, lambda *_: (0,) * len(shape), pipeline_mode=pl.Buffered(1))


def _dot(a, b):
    return jnp.dot(a, b, preferred_element_type=F32)


def _dot_nt(a, b):
    return lax.dot_general(a, b, (((1,), (1,)), ((), ())), preferred_element_type=F32)


def _dot_tn(a, b):
    return lax.dot_general(a, b, (((0,), (0,)), ((), ())), preferred_element_type=F32)


def _sigmoid(x):
    return 0.5 * jnp.tanh(0.5 * x) + 0.5


def _sigmoid_tail(x):
    return 1.0 / (1.0 + jnp.exp(-x))


def _gelu_and_grad(x):
    k = 0.7978845608028654
    x2 = x * x
    t = jnp.tanh(k * (x + 0.044715 * x * x2))
    g = 0.5 * x * (1.0 + t)
    dg = 0.5 * (1.0 + t) + 0.5 * x * (1.0 - t * t) * (k * (1.0 + 3.0 * 0.044715 * x2))
    return g, dg


def _softplus(a):
    e = jnp.exp(-jnp.abs(a))
    u = 1.0 + e
    log1p_e = jnp.where(u == 1.0, e, jnp.log(u) * e / jnp.where(u == 1.0, 1.0, u - 1.0))
    return jnp.maximum(a, 0.0) + log1p_e


def _neg_expm1(z):
    series = -(z * (1.0 + z * (0.5 + z * (1.0 / 6.0 + z * (1.0 / 24.0 + z * (1.0 / 120.0))))))
    return jnp.where(z > -0.1, series, 1.0 - jnp.exp(z))


def _shift_down(x, k):
    return pltpu.roll(x, k, 0)


def _shift_up(x, k):
    return pltpu.roll(x, x.shape[0] - k, 0)


def _rglru_gates(xr, w_a, w_x, b_a, b_x, a_param, is_t0):
    xb = xr.astype(BF16)
    ra = _sigmoid(_dot(xb, w_a) + b_a)
    ri = _sigmoid(_dot(xb, w_x) + b_x)
    sp = _softplus(a_param)
    log_a = (-C_RG) * ra * sp
    a = jnp.exp(log_a)
    mult = jnp.where(is_t0, 1.0, jnp.sqrt(_neg_expm1(2.0 * log_a)))
    return ra, ri, sp, a, mult


SUBLANES = 8


LANES = 128


def _scan_strip(a, b, carry, scr, down):
    t = b.shape[0]
    g = t // SUBLANES
    a3 = a.reshape(g, SUBLANES, LANES)
    b3 = b.reshape(g, SUBLANES, LANES)
    sub = lax.broadcasted_iota(jnp.int32, (g, SUBLANES, LANES), 1)
    for k in (1, 2, 4):
        keep = sub >= k if down else sub < SUBLANES - k
        shift = k if down else SUBLANES - k
        b3 = b3 + a3 * jnp.where(keep, pltpu.roll(b3, shift, 1), 0.0)
        a3 = a3 * jnp.where(keep, pltpu.roll(a3, shift, 1), 1.0)
    scr[0] = a3.reshape(t, LANES)
    scr[1] = b3.reshape(t, LANES)
    end_row = SUBLANES - 1 if down else 0
    ag = scr[0, pl.ds(end_row, g, stride=SUBLANES), :]
    bg = scr[1, pl.ds(end_row, g, stride=SUBLANES), :]
    rg = lax.broadcasted_iota(jnp.int32, (g, LANES), 0)
    edge = 0 if down else g - 1
    bg = bg + jnp.where(rg == edge, ag * carry, 0.0)
    k = 1
    while k < g:
        keep = rg >= k if down else rg < g - k
        shift = k if down else g - k
        bg = bg + ag * jnp.where(keep, pltpu.roll(bg, shift, 0), 0.0)
        if 2 * k < g:
            ag = ag * pltpu.roll(ag, shift, 0)
        k *= 2
    entering = jnp.where(rg != edge, pltpu.roll(bg, 1 if down else g - 1, 0), carry)
    for r in range(SUBLANES):
        scr[2, pl.ds(r, g, stride=SUBLANES), :] = entering
    return scr[1] + scr[0] * scr[2], bg[g - 1:g, :]


def _scan_strips(a, b, carry, scr, down):
    outs = [_scan_strip(a[:, c:c + LANES], b[:, c:c + LANES], carry[:, c:c + LANES], scr, down)
            for c in range(0, b.shape[1], LANES)]
    return jnp.concatenate([o[0] for o in outs], axis=1), jnp.concatenate([o[1] for o in outs], axis=1)


def _scan_down(a, b, carry, scr):
    return _scan_strips(a, b, carry, scr, True)


def _scan_up(m, b, carry, scr):
    return _scan_strips(m, b, carry, scr, False)[0]


def _window_mean(sums, window, first_block, head_t):
    scaled = sums * (1.0 / window)
    head = jnp.where(first_block, sums[:HALO_U] / jnp.minimum(head_t, float(window)), scaled[:HALO_U])
    return jnp.concatenate([head, scaled[HALO_U:]], axis=0)


def _conv_taps(x_ext):
    return [_shift_down(x_ext, 3 - j)[HALO_X:] if j < 3 else x_ext[HALO_X:] for j in range(4)]


def _proj_fwd(x, modr, vecs, w_in):
    s = x.shape[0]
    tm = min(TM_PROJ, s)

    def body(x_ref, mod_ref, vec_ref, w_ref, h1_ref, xrnn_ref, u_ref, ga_ref, dga_ref, sa_ref, sb_ref):
        xv = x_ref[...]
        r = lax.rsqrt(jnp.mean(xv * xv, axis=-1, keepdims=True) + EPS)
        gain = vec_ref[V_G1:V_G1 + 1, :] * (1.0 + mod_ref[M_SC1:M_SC1 + 1, :])
        h = (xv * r * gain + mod_ref[M_SH1:M_SH1 + 1, :]).astype(BF16)
        h1_ref[...] = h
        xrnn_ref[...] = _dot(h, w_ref[:, 0:D])
        ga_ref[...], dga_ref[...] = _gelu_and_grad(_dot(h, w_ref[:, D:2 * D]))
        u_ref[...] = _dot(h, w_ref[:, 2 * D:3 * D])
        sa_ref[...] = _sigmoid(_dot(h, w_ref[:, 3 * D:4 * D]))
        sb_ref[...] = _sigmoid(_dot(h, w_ref[:, 4 * D:5 * D]))

    tok = pl.BlockSpec((tm, D), lambda i: (i, 0))
    sd = lambda dt: jax.ShapeDtypeStruct((s, D), dt)
    return pl.pallas_call(
        body, name="proj_fwd", grid=(s // tm,),
        in_specs=[tok, pl.BlockSpec((8, D), lambda i: (0, 0)), pl.BlockSpec((16, D), lambda i: (0, 0)),
                  _resident((D, D_IN))],
        out_specs=[tok] * 7,
        out_shape=[sd(BF16)] + [sd(F32)] * 6,
        compiler_params=_params(("parallel",)),
    )(x, modr, vecs, w_in)


def _mix_fwd(x_rnn, u_pool, ga, vecs, w_rg_a, w_rg_x, w_pool):
    s = x_rnn.shape[0]
    tm = min(TM_MIX, s)
    nb = s // tm

    def body(xh_ref, x_ref, uh_ref, u_ref, ga_ref, vec_ref, wa_ref, wx_ref, wp_ref,
             xr_ref, hr_ref, za_ref, p_ref, pooled_ref, a_ref, mult_ref, ra_ref, ri_ref, carry_ref, scan_scr):
        i = pl.program_id(0)
        first = i == 0

        @pl.when(first)
        def _():
            carry_ref[...] = jnp.zeros_like(carry_ref)

        row = lax.broadcasted_iota(jnp.int32, (tm, GW), 0)
        is_t0 = jnp.logical_and(first, row == 0)
        head_t = (lax.broadcasted_iota(jnp.int32, (HALO_U, GW), 0) + 1).astype(F32)
        for g in range(N_GROUPS):
            cs = slice(g * GW, (g + 1) * GW)
            vec = vec_ref[:, cs]
            xh = jnp.where(first, 0.0, xh_ref[:, cs])
            taps = _conv_taps(jnp.concatenate([xh, x_ref[:, cs]], axis=0))
            xr = vec[V_CONV_B:V_CONV_B + 1]
            for j in range(4):
                xr = xr + vec[V_CONV_W + j:V_CONV_W + j + 1] * taps[j]
            xr_ref[:, cs] = xr
            ra, ri, _, a, mult = _rglru_gates(
                xr, wa_ref[g], wx_ref[g], vec[V_B_RG_A:V_B_RG_A + 1], vec[V_B_RG_X:V_B_RG_X + 1],
                vec[V_A_PARAM:V_A_PARAM + 1], is_t0)
            a_ref[:, cs] = a
            mult_ref[:, cs] = mult
            ra_ref[:, cs] = ra.astype(BF16)
            ri_ref[:, cs] = ri.astype(BF16)
            h, last = _scan_down(a, xr * ri * mult, carry_ref[0:1, cs], scan_scr)
            hr_ref[:, cs] = h
            carry_ref[0:1, cs] = last
            za_ref[:, cs] = (ga_ref[:, cs] * h).astype(BF16)
            uh = jnp.where(first, 0.0, uh_ref[:, cs])
            sm = jnp.concatenate([uh, u_ref[:, cs]], axis=0)
            k = 1
            while k < POOL_WINDOWS[g]:
                sm = sm + _shift_down(sm, k)
                k *= 2
            mean = _window_mean(sm[HALO_U:], POOL_WINDOWS[g], first, head_t)
            p = (mean - u_ref[:, cs]).astype(BF16)
            p_ref[:, cs] = p
            pb = _dot(p, wp_ref[g]) + vec[V_B_POOL:V_B_POOL + 1]
            pooled_ref[:, cs] = (pb * vec[V_POOL_SCALE:V_POOL_SCALE + 1]).astype(BF16)

    tok = pl.BlockSpec((tm, D), lambda i: (i, 0))
    halo = lambda rows: pl.BlockSpec((rows, D), lambda i: (jnp.maximum(i * (tm // rows) - 1, 0), 0))
    wspec = pl.BlockSpec((N_GROUPS, GW, GW), lambda i: (0, 0, 0))
    sd = lambda dt: jax.ShapeDtypeStruct((s, D), dt)
    return pl.pallas_call(
        body, name="mix_fwd", grid=(nb,),
        in_specs=[halo(HALO_X), tok, halo(HALO_U), tok, tok, pl.BlockSpec((16, D), lambda i: (0, 0)),
                  wspec, wspec, wspec],
        out_specs=[tok] * 9,
        out_shape=[sd(F32), sd(F32), sd(BF16), sd(BF16), sd(BF16), sd(F32), sd(F32), sd(BF16), sd(BF16)],
        scratch_shapes=[pltpu.VMEM((8, D), F32), pltpu.VMEM((3, tm, LANES), F32)],
        compiler_params=_params(("arbitrary",)),
    )(x_rnn, x_rnn, u_pool, u_pool, ga, vecs, w_rg_a, w_rg_x, w_pool)


def _branch_fwd(za, pooled, sa, sb, x, modr, vecs, w_a, w_b, w_out):
    s = x.shape[0]
    tm = min(TM_BRANCH, s)

    def body(za_ref, pooled_ref, sa_ref, sb_ref, x_ref, mod_ref, vec_ref, wa_ref, wb_ref, wo_ref,
             ba_ref, bb_ref, merged_ref, o_ref, x2_ref, h2_ref):
        ba = _dot(za_ref[...], wa_ref[...])
        bb = _dot(pooled_ref[...], wb_ref[...])
        ba_ref[...] = ba.astype(BF16)
        bb_ref[...] = bb.astype(BF16)
        merged = (sa_ref[...] * ba + sb_ref[...] * bb).astype(BF16)
        merged_ref[...] = merged
        o = _dot(merged, wo_ref[...])
        o_ref[...] = o.astype(BF16)
        x2 = x_ref[...] + mod_ref[M_GT1:M_GT1 + 1, :] * o
        x2_ref[...] = x2
        r = lax.rsqrt(jnp.mean(x2 * x2, axis=-1, keepdims=True) + EPS)
        gain = vec_ref[V_G2:V_G2 + 1, :] * (1.0 + mod_ref[M_SC2:M_SC2 + 1, :])
        h2_ref[...] = (x2 * r * gain + mod_ref[M_SH2:M_SH2 + 1, :]).astype(BF16)

    tok = pl.BlockSpec((tm, D), lambda i: (i, 0))
    wspec = pl.BlockSpec((D, D), lambda i: (0, 0))
    sd = lambda dt: jax.ShapeDtypeStruct((s, D), dt)
    return pl.pallas_call(
        body, name="branch_fwd", grid=(s // tm,),
        in_specs=[tok, tok, tok, tok,
                  tok, pl.BlockSpec((8, D), lambda i: (0, 0)), pl.BlockSpec((16, D), lambda i: (0, 0)),
                  wspec, wspec, wspec],
        out_specs=[tok] * 6,
        out_shape=[sd(BF16), sd(BF16), sd(BF16), sd(BF16), sd(F32), sd(BF16)],
        compiler_params=_params(("parallel",)),
    )(za, pooled, sa, sb, x, modr, vecs, w_a, w_b, w_out)


def _mlp_fwd(h2, x2, target, modr, vecs, w_up, w_down):
    s = x2.shape[0]
    tm = min(TM_MLP, s)

    def body(h2_ref, x2_ref, tgt_ref, mod_ref, vec_ref, wu_ref, wd_ref,
             ru_ref, dx3_ref, ddn_ref, small_ref):
        @pl.when(pl.program_id(0) == 0)
        def _():
            small_ref[...] = jnp.zeros_like(small_ref)

        h2 = h2_ref[...]
        dn = None
        for c in range(D_FF // D):
            cs = slice(c * D, (c + 1) * D)
            ru = jnp.maximum(_dot(h2, wu_ref[:, cs]), 0.0)
            ru_ref[:, cs] = ru.astype(BF16)
            part = _dot((ru * ru).astype(BF16), wd_ref[cs, :])
            dn = part if dn is None else dn + part
        gt2 = mod_ref[M_GT2:M_GT2 + 1, :]
        gf = vec_ref[V_GF:V_GF + 1, :]
        x3 = x2_ref[...] + gt2 * dn
        r3 = lax.rsqrt(jnp.mean(x3 * x3, axis=-1, keepdims=True) + EPS)
        n3 = x3 * r3
        err = n3 * gf - tgt_ref[...]
        dy = err * (1.0 / D)
        dn3 = dy * gf
        dx3 = r3 * (dn3 - n3 * jnp.mean(dn3 * n3, axis=-1, keepdims=True))
        dx3_ref[...] = dx3
        ddn_ref[...] = (dx3 * gt2).astype(BF16)
        small_ref[0:1, :] += jnp.sum(dy * n3, axis=0, keepdims=True)
        small_ref[1:2, :] += jnp.sum(dx3 * dn, axis=0, keepdims=True)
        small_ref[2:3, :] += (0.5 / D) * jnp.sum(err * err, axis=0, keepdims=True)

    tok = pl.BlockSpec((tm, D), lambda i: (i, 0))
    return pl.pallas_call(
        body, name="mlp_fwd", grid=(s // tm,),
        in_specs=[tok, tok, tok,
                  pl.BlockSpec((8, D), lambda i: (0, 0)), pl.BlockSpec((16, D), lambda i: (0, 0)),
                  _resident((D, D_FF)), _resident((D_FF, D))],
        out_specs=[pl.BlockSpec((tm, D_FF), lambda i: (i, 0)), tok, tok,
                   pl.BlockSpec((8, D), lambda i: (0, 0))],
        out_shape=[jax.ShapeDtypeStruct((s, D_FF), BF16), jax.ShapeDtypeStruct((s, D), F32),
                   jax.ShapeDtypeStruct((s, D), BF16), jax.ShapeDtypeStruct((8, D), F32)],
        compiler_params=_params(("arbitrary",)),
    )(h2, x2, target, modr, vecs, w_up, w_down)


def _mlp_bwd(d_dn, ru, x2, dx3, o, modr, vecs, w_up, w_down):
    s = x2.shape[0]
    tm = min(TM_MLP_BWD, s)

    def body(ddn_ref, ru_ref, x2_ref, dx3_ref, o_ref, mod_ref, vec_ref, wu_ref, wd_ref,
             dup_ref, dx2_ref, do_ref, small_ref):
        @pl.when(pl.program_id(0) == 0)
        def _():
            small_ref[...] = jnp.zeros_like(small_ref)

        ddn = ddn_ref[...]
        dh2 = None
        for c in range(D_FF // D):
            cs = slice(c * D, (c + 1) * D)
            dff = _dot_nt(ddn, wd_ref[cs, :])
            dup = (dff * (2.0 * ru_ref[:, cs].astype(F32))).astype(BF16)
            dup_ref[:, cs] = dup
            part = _dot_nt(dup, wu_ref[:, cs])
            dh2 = part if dh2 is None else dh2 + part
        x2 = x2_ref[...]
        r2 = lax.rsqrt(jnp.mean(x2 * x2, axis=-1, keepdims=True) + EPS)
        xn2 = x2 * r2
        gain = vec_ref[V_G2:V_G2 + 1, :] * (1.0 + mod_ref[M_SC2:M_SC2 + 1, :])
        dxn2 = dh2 * gain
        dx2 = dx3_ref[...] + r2 * (dxn2 - xn2 * jnp.mean(dxn2 * xn2, axis=-1, keepdims=True))
        dx2_ref[...] = dx2
        do_ref[...] = (dx2 * mod_ref[M_GT1:M_GT1 + 1, :]).astype(BF16)
        small_ref[0:1, :] += jnp.sum(dh2, axis=0, keepdims=True)
        small_ref[1:2, :] += jnp.sum(dh2 * xn2, axis=0, keepdims=True)
        small_ref[2:3, :] += jnp.sum(dx2 * o_ref[...].astype(F32), axis=0, keepdims=True)

    tok = pl.BlockSpec((tm, D), lambda i: (i, 0))
    wide = pl.BlockSpec((tm, D_FF), lambda i: (i, 0))
    return pl.pallas_call(
        body, name="mlp_bwd", grid=(s // tm,),
        in_specs=[tok, wide, tok, tok, tok,
                  pl.BlockSpec((8, D), lambda i: (0, 0)), pl.BlockSpec((16, D), lambda i: (0, 0)),
                  _resident((D, D_FF)), _resident((D_FF, D))],
        out_specs=[wide, tok, tok, pl.BlockSpec((8, D), lambda i: (0, 0))],
        out_shape=[jax.ShapeDtypeStruct((s, D_FF), BF16), jax.ShapeDtypeStruct((s, D), F32),
                   jax.ShapeDtypeStruct((s, D), BF16), jax.ShapeDtypeStruct((8, D), F32)],
        compiler_params=_params(("arbitrary",)),
    )(d_dn, ru, x2, dx3, o, modr, vecs, w_up, w_down)


def _branch_bwd(do, sa, sb, ba, bb, w_a, w_b, w_out, dep):
    s = do.shape[0]
    tm = min(TM_BRANCH, s)

    def body(do_ref, sa_ref, sb_ref, ba_ref, bb_ref, wa_ref, wb_ref, wo_ref, dep_ref,
             dba_ref, dbb_ref, dg_ref, dza_ref, dpooled_ref):
        dmerged = _dot_nt(do_ref[...], wo_ref[...])
        sa = sa_ref[...]
        sb = sb_ref[...]
        dba = (dmerged * sa).astype(BF16)
        dbb = (dmerged * sb).astype(BF16)
        dba_ref[...] = dba
        dbb_ref[...] = dbb
        dg_ref[:, :D] = (dmerged * ba_ref[...].astype(F32) * sa * (1.0 - sa)).astype(BF16)
        dg_ref[:, D:] = (dmerged * bb_ref[...].astype(F32) * sb * (1.0 - sb)).astype(BF16)
        dza_ref[...] = _dot_nt(dba, wa_ref[...])
        dpooled_ref[...] = _dot_nt(dbb, wb_ref[...])

    tok = pl.BlockSpec((tm, D), lambda i: (i, 0))
    wspec = pl.BlockSpec((D, D), lambda i: (0, 0))
    sd = lambda dt: jax.ShapeDtypeStruct((s, D), dt)
    return pl.pallas_call(
        body, name="branch_bwd", grid=(s // tm,),
        in_specs=[tok, tok, tok, tok, tok, wspec, wspec, wspec, pl.BlockSpec(memory_space=pl.ANY)],
        out_specs=[tok, tok, pl.BlockSpec((tm, 2 * D), lambda i: (i, 0)), tok, tok],
        out_shape=[sd(BF16), sd(BF16), jax.ShapeDtypeStruct((s, 2 * D), BF16), sd(F32), sd(F32)],
        compiler_params=_params(("parallel",)),
    )(do, sa, sb, ba, bb, w_a, w_b, w_out, dep)


def _mix_bwd(dza, dpooled, x_rnn, ga, dga, xr, hr, p, gates, dgates, vecs, w_rg_a, w_rg_x, w_pool):
    s = xr.shape[0]
    tm = min(TM_MIX, s)
    nb = s // tm

    def body(dza_ref, dpooled_ref, xh_ref, x_ref, ga_ref, dga_ref, xr_ref, hh_ref, hr_ref, p_ref,
             a_ref, mult_ref, ra_ref, ri_ref, dg_ref, vec_ref, wa_ref, wx_ref, wp_ref,
             dproj_ref, dwa_ref, dwx_ref, dwp_ref, small_ref,
             scan_carry, dxr_carry, q_carry, scan_scr, dwa_acc, dwx_acc, dwp_acc):
        i = pl.program_id(0)
        bi = nb - 1 - i
        first_t = bi == 0

        @pl.when(i == 0)
        def _():
            scan_carry[...] = jnp.zeros_like(scan_carry)
            dxr_carry[...] = jnp.zeros_like(dxr_carry)
            q_carry[...] = jnp.zeros_like(q_carry)
            dwa_acc[...] = jnp.zeros_like(dwa_acc)
            dwx_acc[...] = jnp.zeros_like(dwx_acc)
            dwp_acc[...] = jnp.zeros_like(dwp_acc)
            small_ref[...] = jnp.zeros_like(small_ref)

        row = lax.broadcasted_iota(jnp.int32, (tm, GW), 0)
        is_t0 = jnp.logical_and(first_t, row == 0)
        head_t = (lax.broadcasted_iota(jnp.int32, (HALO_U, GW), 0) + 1).astype(F32)
        colsum = lambda v: jnp.sum(v, axis=0, keepdims=True)
        for g in range(N_GROUPS):
            cs = slice(g * GW, (g + 1) * GW)
            vec = vec_ref[:, cs]
            xr = xr_ref[:, cs]
            hr = hr_ref[:, cs]
            dza = dza_ref[:, cs]
            dproj_ref[:, D + g * GW:D + (g + 1) * GW] = (dza * hr * dga_ref[:, cs]).astype(BF16)
            dhr = dza * ga_ref[:, cs]
            a = a_ref[:, cs]
            mult = mult_ref[:, cs]
            ra = ra_ref[:, cs].astype(F32)
            ri = ri_ref[:, cs].astype(F32)
            sp = _softplus(vec[V_A_PARAM:V_A_PARAM + 1])
            m = jnp.where(row == tm - 1, 1.0, _shift_up(a, 1))
            gsum = _scan_up(m, dhr, scan_carry[0:1, cs], scan_scr)
            scan_carry[0:1, cs] = a[0:1, :] * gsum[0:1, :]
            hh = jnp.where(first_t, 0.0, hh_ref[:, cs])
            hprev = _shift_down(jnp.concatenate([hh, hr], axis=0), 1)[8:]
            da = gsum * hprev
            dmult = jnp.where(is_t0, 0.0, gsum * xr * ri)
            dlog_a = da * a - dmult * a * a / mult
            dri = gsum * xr * mult
            dxr = gsum * ri * mult
            small_ref[7:8, cs] += colsum((-C_RG) * ra * dlog_a)
            dpa = (((-C_RG) * sp) * dlog_a * ra * (1.0 - ra))
            dpx = dri * ri * (1.0 - ri)
            small_ref[5:6, cs] += colsum(dpa)
            small_ref[6:7, cs] += colsum(dpx)
            dpa = dpa.astype(BF16)
            dpx = dpx.astype(BF16)
            xrb = xr.astype(BF16)
            dwa_acc[g] += _dot_tn(xrb, dpa)
            dwx_acc[g] += _dot_tn(xrb, dpx)
            dxr = dxr + _dot_nt(dpa, wa_ref[g]) + _dot_nt(dpx, wx_ref[g])
            small_ref[4:5, cs] += colsum(dxr)
            xh = jnp.where(first_t, 0.0, xh_ref[:, cs])
            taps = _conv_taps(jnp.concatenate([xh, x_ref[:, cs]], axis=0))
            dxr_ext = jnp.concatenate([dxr, dxr_carry[:, cs]], axis=0)
            dx = vec[V_CONV_W + 3:V_CONV_W + 4] * dxr
            for j in range(4):
                small_ref[j:j + 1, cs] += colsum(dxr * taps[j])
                if j < 3:
                    dx = dx + vec[V_CONV_W + j:V_CONV_W + j + 1] * _shift_up(dxr_ext, 3 - j)[:tm]
            dxr_carry[:, cs] = dxr[0:8, :]
            dproj_ref[:, cs] = dx.astype(BF16)
            pg = p_ref[:, cs]
            dpooled = dpooled_ref[:, cs]
            pb = _dot(pg, wp_ref[g]) + vec[V_B_POOL:V_B_POOL + 1]
            small_ref[9:10, cs] += colsum(dpooled * pb)
            dpb = dpooled * vec[V_POOL_SCALE:V_POOL_SCALE + 1]
            small_ref[8:9, cs] += colsum(dpb)
            dpbb = dpb.astype(BF16)
            dwp_acc[g] += _dot_tn(pg, dpbb)
            dp = _dot_nt(dpbb, wp_ref[g])
            q = _window_mean(dp, POOL_WINDOWS[g], first_t, head_t)
            sm = jnp.concatenate([q, q_carry[:, cs]], axis=0)
            k = 1
            while k < POOL_WINDOWS[g]:
                sm = sm + _shift_up(sm, k)
                k *= 2
            q_carry[:, cs] = q[0:HALO_U, :]
            dproj_ref[:, 2 * D + g * GW:2 * D + (g + 1) * GW] = (sm[:tm] - dp).astype(BF16)
        dproj_ref[:, 3 * D:] = dg_ref[...]

        @pl.when(i == nb - 1)
        def _():
            dwa_ref[...] = dwa_acc[...].astype(BF16)
            dwx_ref[...] = dwx_acc[...].astype(BF16)
            dwp_ref[...] = dwp_acc[...].astype(BF16)

    rev = lambda i: nb - 1 - i
    tok = pl.BlockSpec((tm, D), lambda i: (rev(i), 0))
    halo8 = lambda k: pl.BlockSpec((8, D), lambda i: (jnp.maximum(rev(i) * (tm // 8) - 1, 0), k))
    wspec = pl.BlockSpec((N_GROUPS, GW, GW), lambda i: (0, 0, 0))
    wshape = jax.ShapeDtypeStruct((N_GROUPS, GW, GW), BF16)
    return pl.pallas_call(
        body, name="mix_bwd", grid=(nb,),
        in_specs=[tok, tok, halo8(0), tok, tok, tok, tok, halo8(0), tok, tok, tok, tok, tok, tok,
                  pl.BlockSpec((tm, 2 * D), lambda i: (rev(i), 0)),
                  pl.BlockSpec((16, D), lambda i: (0, 0)), wspec, wspec, wspec],
        out_specs=[pl.BlockSpec((tm, D_IN), lambda i: (rev(i), 0)), wspec, wspec, wspec,
                   pl.BlockSpec((16, D), lambda i: (0, 0))],
        out_shape=[jax.ShapeDtypeStruct((s, D_IN), BF16), wshape, wshape, wshape,
                   jax.ShapeDtypeStruct((16, D), F32)],
        scratch_shapes=[pltpu.VMEM((8, D), F32), pltpu.VMEM((8, D), F32), pltpu.VMEM((HALO_U, D), F32),
                        pltpu.VMEM((3, tm, LANES), F32)] + [pltpu.VMEM((N_GROUPS, GW, GW), F32)] * 3,
        compiler_params=_params(("arbitrary",)),
    )(dza, dpooled, x_rnn, x_rnn, ga, dga, xr, hr, hr, p, *gates, dgates, vecs, w_rg_a, w_rg_x, w_pool)


def _proj_bwd(dproj, x, dx2, modr, vecs, w_in):
    s = x.shape[0]
    tm = min(TM_PROJ, s)

    def body(dp_ref, x_ref, dx2_ref, mod_ref, vec_ref, w_ref, gx_ref, small_ref):
        @pl.when(pl.program_id(0) == 0)
        def _():
            small_ref[...] = jnp.zeros_like(small_ref)

        dh1 = None
        for c in range(D_IN // D):
            cs = slice(c * D, (c + 1) * D)
            part = _dot_nt(dp_ref[:, cs], w_ref[:, cs])
            dh1 = part if dh1 is None else dh1 + part
        xv = x_ref[...]
        r1 = lax.rsqrt(jnp.mean(xv * xv, axis=-1, keepdims=True) + EPS)
        xn1 = xv * r1
        gain = vec_ref[V_G1:V_G1 + 1, :] * (1.0 + mod_ref[M_SC1:M_SC1 + 1, :])
        dxn1 = dh1 * gain
        gx_ref[...] = dx2_ref[...] + r1 * (dxn1 - xn1 * jnp.mean(dxn1 * xn1, axis=-1, keepdims=True))
        small_ref[0:1, :] += jnp.sum(dh1, axis=0, keepdims=True)
        small_ref[1:2, :] += jnp.sum(dh1 * xn1, axis=0, keepdims=True)

    tok = pl.BlockSpec((tm, D), lambda i: (i, 0))
    return pl.pallas_call(
        body, name="proj_bwd", grid=(s // tm,),
        in_specs=[pl.BlockSpec((tm, D_IN), lambda i: (i, 0)), tok, tok,
                  pl.BlockSpec((8, D), lambda i: (0, 0)), pl.BlockSpec((16, D), lambda i: (0, 0)),
                  _resident((D, D_IN))],
        out_specs=[tok, pl.BlockSpec((8, D), lambda i: (0, 0))],
        out_shape=[jax.ShapeDtypeStruct((s, D), F32), jax.ShapeDtypeStruct((8, D), F32)],
        compiler_params=_params(("arbitrary",)),
    )(dproj, x, dx2, modr, vecs, w_in)


def _wgrad(a, b, name, square_a=False, dep=None):
    s, ka = a.shape
    n = b.shape[1]
    tka = ka if ka <= 1024 else ka // 2
    tn = n if n <= 1024 else n // 2
    ts = min(TS_WGRAD, s)
    ns = s // ts
    nc = 512
    deps = [] if dep is None else [dep]

    def body(a_ref, b_ref, *refs):
        out_ref, acc_ref = refs[-2:]
        t = pl.program_id(2)

        @pl.when(t == 0)
        def _():
            acc_ref[...] = jnp.zeros_like(acc_ref)

        av = a_ref[...]
        if square_a:
            af = av.astype(F32)
            av = (af * af).astype(BF16)
        for c in range(tn // nc):
            cs = slice(c * nc, (c + 1) * nc)
            acc_ref[:, cs] += _dot_tn(av, b_ref[:, cs])

        @pl.when(t == ns - 1)
        def _():
            out_ref[...] = acc_ref[...].astype(BF16)

    return pl.pallas_call(
        body, name=name, grid=(ka // tka, n // tn, ns),
        in_specs=[pl.BlockSpec((ts, tka), lambda i, j, t: (t, i)),
                  pl.BlockSpec((ts, tn), lambda i, j, t: (t, j))] + [pl.BlockSpec(memory_space=pl.ANY)] * len(deps),
        out_specs=pl.BlockSpec((tka, tn), lambda i, j, t: (i, j)),
        out_shape=jax.ShapeDtypeStruct((ka, n), BF16),
        scratch_shapes=[pltpu.VMEM((tka, tn), F32)],
        compiler_params=_params(("parallel", "parallel", "arbitrary")),
    )(a, b, *deps)


def _window(ref, kind, idx, size):
    start = pl.multiple_of(idx * size, size)
    if kind == 0:
        return ref.at[pl.ds(start, size)]
    if kind == 1:
        return ref.at[:, pl.ds(start, size)]
    return ref.at[:, :, pl.ds(start, size)]


def _mesh_place():
    x, y, c = lax.axis_index("x"), lax.axis_index("y"), lax.axis_index("c")
    return x, y, c, 4 * x + 2 * y + c


def _peer(x, y, c, q):
    px = 1 - x if q & 4 else x
    py = 1 - y if q & 2 else y
    pc = 1 - c if q & 1 else c
    return (px, py, pc), 4 * px + 2 * py + pc


def _all_gather(shards, kinds, name):
    n = len(shards)
    full_shapes = []
    for sh, kind in zip(shards, kinds):
        dims = list(sh.shape)
        dims[kind] *= N_DEV
        full_shapes.append(jax.ShapeDtypeStruct(tuple(dims), sh.dtype))

    def body(*refs):
        ins, outs = refs[:n], refs[n:2 * n]
        send_sems, recv_sems, local_sems = refs[2 * n:]
        x, y, c, me = _mesh_place()
        sends, recvs, locals_ = [], [], []
        for k in range(n):
            size = shards[k].shape[kinds[k]]
            mine = _window(outs[k], kinds[k], me, size)
            lc = pltpu.make_async_copy(ins[k], mine, local_sems.at[k])
            lc.start()
            locals_.append(lc)
            for q in range(1, N_DEV):
                peer, peer_idx = _peer(x, y, c, q)
                cp = pltpu.make_async_remote_copy(
                    src_ref=ins[k], dst_ref=mine, send_sem=send_sems.at[k, q], recv_sem=recv_sems.at[k, q],
                    device_id=peer, device_id_type=MESH)
                cp.start()
                sends.append(cp)
                recvs.append(pltpu.make_async_remote_copy(
                    src_ref=ins[k], dst_ref=_window(outs[k], kinds[k], peer_idx, size),
                    send_sem=send_sems.at[k, q], recv_sem=recv_sems.at[k, q],
                    device_id=peer, device_id_type=MESH))
        for cp in recvs:
            cp.wait_recv()
        for cp in sends:
            cp.wait_send()
        for lc in locals_:
            lc.wait()

    any_spec = pl.BlockSpec(memory_space=pl.ANY)
    return pl.pallas_call(
        body, name=name,
        in_specs=[any_spec] * n, out_specs=[any_spec] * n, out_shape=full_shapes,
        scratch_shapes=[pltpu.SemaphoreType.DMA((n, N_DEV)), pltpu.SemaphoreType.DMA((n, N_DEV)),
                        pltpu.SemaphoreType.DMA((n,))],
    )(*shards)


_HBM = pl.BlockSpec(memory_space=pltpu.HBM)
_SEM = pl.BlockSpec(memory_space=pltpu.SEMAPHORE)
_EFFECT = pltpu.SideEffectType.DATAFLOW_SIDE_EFFECTING


N_NEAR = 4


def _near(x, y, c):
    out = [((x, y, 1 - c), 4 * x + 2 * y + 1 - c)]
    for j in (1, 2, 3):
        px = 1 - x if j & 2 else x
        py = 1 - y if j & 1 else y
        out.append(((px, py, c), 4 * px + 2 * py + c))
    return out


def _remote(src, dst, send_sems, recv_sems, slot, device):
    return pltpu.make_async_remote_copy(src_ref=src, dst_ref=dst, send_sem=send_sems.at[slot], recv_sem=recv_sems.at[slot],
                                        device_id=device, device_id_type=MESH)


def _split_call(name, arrays, sems_in, n_new_sems, after, emit):
    na, ns, nn = len(arrays), len(sems_in), len(n_new_sems)

    def body(*refs):
        emit(refs[:na], refs[na:na + ns], refs[na + ns + 1:na + ns + 1 + nn])
        refs[-1][...] = jnp.zeros_like(refs[-1])

    outs = pl.pallas_call(
        body, name=name,
        out_shape=(*[pltpu.SemaphoreType.DMA((m,)) for m in n_new_sems],
                   *[pltpu.HBM(a.shape, a.dtype) for a in arrays], jax.ShapeDtypeStruct((8, 128), F32)),
        in_specs=[_HBM] * na + [_SEM] * ns + [pl.BlockSpec(memory_space=pl.ANY)],
        out_specs=(*[_SEM] * nn, *[_HBM] * na, pl.BlockSpec(memory_space=pltpu.VMEM)),
        input_output_aliases={i: nn + i for i in range(na)},
        compiler_params=pltpu.CompilerParams(has_side_effects=_EFFECT),
    )(*[pltpu.with_memory_space_constraint(a, pltpu.HBM) for a in arrays], *sems_in, after)
    return list(outs[:nn]), list(outs[nn:nn + na]), outs[-1]


class _Gather:
    def __init__(self, shards, kinds, after, name):
        self.n, self.kinds, self.name = len(shards), kinds, name
        self.sizes = [s.shape[k] for s, k in zip(shards, kinds)]
        n = self.n
        lands = []
        for s, k in zip(shards, kinds):
            dims = list(s.shape)
            dims[k] *= N_DEV
            lands.append(lax.empty(tuple(dims), s.dtype))

        def emit(arr, _, new):
            x, y, c, me = _mesh_place()
            for k in range(n):
                pltpu.make_async_copy(arr[k], _window(arr[n + k], kinds[k], me, self.sizes[k]), new[2].at[k]).start()
            for k in range(n):
                mine = _window(arr[n + k], kinds[k], me, self.sizes[k])
                for j, (dev, _) in enumerate(_near(x, y, c)):
                    _remote(arr[k], mine, new[0], new[1], k * N_NEAR + j, dev).start()

        self.sems, self.arrays, self.token = _split_call(name + "_start", [*shards, *lands], [],
                                                         [n * N_NEAR, n * N_NEAR, n], after, emit)

    def forward(self, after):
        n, kinds, sizes = self.n, self.kinds, self.sizes

        def emit(arr, old, new):
            x, y, c, _ = _mesh_place()
            near = _near(x, y, c)
            for k in range(n):
                for j in (1, 2, 3):
                    dev, idx = near[j]
                    landed = _window(arr[n + k], kinds[k], idx, sizes[k])
                    _remote(arr[k], landed, old[0], old[1], k * N_NEAR + j, dev).wait_recv()
                    _remote(landed, landed, new[0], new[1], k * N_NEAR + j, near[0][0]).start()

        new, self.arrays, self.token = _split_call(self.name + "_forward", self.arrays, self.sems, [n * N_NEAR] * 2,
                                                   after, emit)
        self.sems = [*self.sems, *new]

    def finish(self, after):
        n, kinds, sizes = self.n, self.kinds, self.sizes

        def emit(arr, old, _):
            x, y, c, me = _mesh_place()
            near = _near(x, y, c)
            other_core = near[0][0]
            for k in range(n):
                win = lambda idx: _window(arr[n + k], kinds[k], idx, sizes[k])
                pltpu.make_async_copy(arr[k], win(me), old[2].at[k]).wait()
                for j, (dev, idx) in enumerate(near):
                    _remote(arr[k], win(me), old[0], old[1], k * N_NEAR + j, dev).wait_send()
                _remote(arr[k], win(near[0][1]), old[0], old[1], k * N_NEAR, other_core).wait_recv()
                for j in (1, 2, 3):
                    idx = near[j][1]
                    _remote(win(idx), win(idx), old[3], old[4], k * N_NEAR + j, other_core).wait_send()
                    _remote(arr[k], win(idx + 1 - 2 * c), old[3], old[4], k * N_NEAR + j, other_core).wait_recv()

        _, arrays, _ = _split_call(self.name + "_finish", self.arrays, self.sems, [], after, emit)
        return arrays[n:]


class _Scatter:
    def __init__(self, partials, kinds, after, name):
        self.n, self.kinds, self.name, self.partials = len(partials), kinds, name, partials
        self.sizes = [p.shape[k] // N_DEV for p, k in zip(partials, kinds)]
        n, sizes = self.n, self.sizes
        self.slot_shapes = []
        for p, k, size in zip(partials, kinds, sizes):
            dims = list(p.shape)
            dims[k] = size
            self.slot_shapes.append((N_NEAR, *dims))
        slots = [lax.empty(sh, p.dtype) for sh, p in zip(self.slot_shapes, partials)]

        def emit(arr, _, new):
            x, y, c, _ = _mesh_place()
            near = _near(x, y, c)
            for k in range(n):
                for j in range(N_NEAR):
                    owner = near[j][1] if j == 0 else near[j][1] + 1 - 2 * c
                    _remote(_window(arr[k], kinds[k], owner, sizes[k]), arr[n + k].at[j], new[0], new[1],
                            k * N_NEAR + j, near[0][0]).start()

        self.sems, self.arrays, self.token = _split_call(name + "_start", [*partials, *slots], [], [n * N_NEAR] * 2,
                                                         after, emit)

    def combine_and_send(self, own4, after):
        n, kinds, sizes = self.n, self.kinds, self.sizes

        def emit_wait(arr, old, _):
            x, y, c, _ = _mesh_place()
            near = _near(x, y, c)
            for k in range(n):
                for j in range(N_NEAR):
                    owner = near[j][1] if j == 0 else near[j][1] + 1 - 2 * c
                    cp = _remote(_window(arr[k], kinds[k], owner, sizes[k]), arr[n + k].at[j], old[0], old[1],
                                 k * N_NEAR + j, near[0][0])
                    cp.wait_send()
                    cp.wait_recv()

        _, arrays, _ = _split_call(self.name + "_landed", self.arrays, self.sems, [], after, emit_wait)
        chip_sums = _chip_sums(arrays[:n], arrays[n:], kinds, sizes, own4, self.name + "_combine")
        arrivals = [lax.empty((N_NEAR - 1, *sh[1:]), p.dtype) for sh, p in zip(self.slot_shapes, self.partials)]

        def emit_send(arr, _, new):
            x, y, c, _ = _mesh_place()
            near = _near(x, y, c)
            for k in range(n):
                for j in (1, 2, 3):
                    _remote(arr[k].at[j], arr[n + k].at[j - 1], new[0], new[1], k * N_NEAR + j, near[j][0]).start()

        self.sems, self.arrays, self.token = _split_call(self.name + "_send", [*chip_sums, *arrivals], [],
                                                         [n * N_NEAR] * 2, own4, emit_send)

    def finish(self, after):
        n = self.n

        def emit(arr, old, _):
            x, y, c, _ = _mesh_place()
            near = _near(x, y, c)
            for k in range(n):
                for j in (1, 2, 3):
                    cp = _remote(arr[k].at[j], arr[n + k].at[j - 1], old[0], old[1], k * N_NEAR + j, near[j][0])
                    cp.wait_send()
                    cp.wait_recv()

        _, arrays, _ = _split_call(self.name + "_finish", self.arrays, self.sems, [], after, emit)
        return arrays[:n], arrays[n:]


def _chip_sums(partials, slots, kinds, sizes, own4, name):
    n = len(partials)

    def body(own_ref, *refs):
        for k in range(n):
            refs[2 * n + k][...] = (refs[k][...].astype(F32) + refs[n + k][...].astype(F32)).astype(BF16)

    in_specs, slot_specs = [], []
    for p, s, kind, size in zip(partials, slots, kinds, sizes):
        block = list(p.shape)
        block[kind] = size
        nd = len(block)
        in_specs.append(pl.BlockSpec(tuple(block), functools.partial(
            lambda j, own, kind, nd: tuple(own[j] if d == kind else 0 for d in range(nd)), kind=kind, nd=nd)))
        slot_specs.append(pl.BlockSpec((None, *block), functools.partial(
            lambda j, own, nd: (j,) + (0,) * nd, nd=nd)))
    return pl.pallas_call(
        body, name=name,
        grid_spec=pltpu.PrefetchScalarGridSpec(num_scalar_prefetch=1, grid=(N_NEAR,),
                                               in_specs=in_specs + slot_specs, out_specs=slot_specs),
        out_shape=[jax.ShapeDtypeStruct(s.shape, s.dtype) for s in slots],
        compiler_params=_params(("arbitrary",)),
    )(own4, *partials, *slots)


def _after(small, token):
    return small + token[0:1, 0:1].astype(small.dtype)


def _silu(c):
    return c * _sigmoid_tail(c)


def _ada_fwd(c_all, w_ada, b_ada_cols):
    def body(c_ref, w_ref, b_ref, out_ref):
        out_ref[...] = jnp.dot(_silu(c_ref[...]), w_ref[...], preferred_element_type=F32,
                               precision=lax.Precision.HIGHEST) + b_ref[...]

    return pl.pallas_call(
        body, name="ada_fwd", out_shape=jax.ShapeDtypeStruct((N_DEV, w_ada.shape[1]), F32),
    )(c_all, w_ada, b_ada_cols)


def _adam(w, g, m, v):
    m = ADAM_B1 * m + (1.0 - ADAM_B1) * g
    v = ADAM_B2 * v + (1.0 - ADAM_B2) * (g * g)
    m_hat = m / (1.0 - ADAM_B1 ** ADAM_STEP)
    v_hat = v / (1.0 - ADAM_B2 ** ADAM_STEP)
    delta = -ADAM_LR * (m_hat / (jnp.sqrt(v_hat) + ADAM_EPS) + ADAM_WD * w)
    return delta, m, v


def _ada_bwd_adam(c_all, dmod_cols, w, m, v):
    def body(c_ref, d_ref, w_ref, m_ref, v_ref, g_ref, delta_ref, nm_ref, nv_ref):
        g = lax.dot_general(_silu(c_ref[...]), d_ref[...], (((0,), (0,)), ((), ())),
                            preferred_element_type=F32, precision=lax.Precision.HIGHEST)
        g_ref[...] = g
        delta_ref[...], nm_ref[...], nv_ref[...] = _adam(w_ref[...], g, m_ref[...], v_ref[...])

    sd = jax.ShapeDtypeStruct(w.shape, F32)
    return pl.pallas_call(body, name="ada_bwd_adam", out_shape=[sd] * 4,
                          compiler_params=pltpu.CompilerParams(vmem_limit_bytes=V7X_VMEM_LIMIT),
                          )(c_all, dmod_cols, w, m, v)


def _adam_group(chip_sums, arrivals, ws, ms, vs, n_tiles, name):
    n = len(ws)

    def body(*refs):
        for k in range(n):
            c_ref, a_ref, w_ref, m_ref, v_ref = (refs[j * n + k] for j in range(5))
            g_ref, delta_ref, nm_ref, nv_ref = (refs[(5 + j) * n + k] for j in range(4))
            g = c_ref[...].astype(F32)
            for j in range(N_NEAR - 1):
                g = g + a_ref[j].astype(F32)
            g_ref[...] = g
            delta_ref[...], nm_ref[...], nv_ref[...] = _adam(w_ref[...], g, m_ref[...], v_ref[...])

    tiles = [(w.shape[0] // n_tiles, w.shape[1]) for w in ws]
    blk = [pl.BlockSpec(t, lambda i: (i, 0)) for t in tiles]
    return pl.pallas_call(
        body, name=name, grid=(n_tiles,),
        in_specs=[pl.BlockSpec((None, *t), lambda i: (0, i, 0)) for t in tiles]
        + [pl.BlockSpec((N_NEAR - 1, *t), lambda i: (0, i, 0)) for t in tiles] + blk * 3,
        out_specs=blk * 4, out_shape=[jax.ShapeDtypeStruct(w.shape, F32) for w in ws] * 4,
        compiler_params=_params(("parallel",)),
    )(*chip_sums, *arrivals, *ws, *ms, *vs)


N_SMALL = 40
N_SMALL_PARAMS = 11


def _pack_vecs(conv_w_full, rows):
    def body(cw_ref, *refs):
        out = refs[-1]
        out[...] = jnp.zeros_like(out)
        out[0:4, :] = cw_ref[0:4, :]
        for r, ref in enumerate(refs[:-1]):
            out[4 + r:5 + r, :] = ref[...]

    return pl.pallas_call(body, name="pack_vecs", out_shape=jax.ShapeDtypeStruct((16, D), F32))(conv_w_full, *rows)


def _small_finish(gathered, mod_all, vecs, ws, ms, vs):
    n = N_SMALL_PARAMS

    def body(g_ref, mod_ref, vec_ref, *refs):
        w_refs, m_refs, v_refs = refs[:n], refs[n:2 * n], refs[2 * n:3 * n]
        outs = refs[3 * n:]
        g1 = vec_ref[V_G1:V_G1 + 1, :]
        g2 = vec_ref[V_G2:V_G2 + 1, :]
        zero = jnp.zeros((1, D), F32)
        dg1, dg2, dgf, loss_lanes = zero, zero, zero, zero
        mixer = jnp.zeros((16, D), F32)
        db_ada = jnp.zeros((6, D), F32)
        for b in range(N_DEV):
            gb = g_ref[b]
            mod = mod_ref[b]
            q1 = gb[33:34]
            q2 = gb[9:10]
            dmod = jnp.concatenate([gb[32:33], q1 * g1, gb[10:11], gb[8:9], q2 * g2, gb[1:2]], axis=0)
            outs[4 * n][b] = dmod
            db_ada = db_ada + dmod
            dg1 = dg1 + q1 * (1.0 + mod[M_SC1:M_SC1 + 1])
            dg2 = dg2 + q2 * (1.0 + mod[M_SC2:M_SC2 + 1])
            dgf = dgf + gb[0:1]
            loss_lanes = loss_lanes + gb[2:3]
            mixer = mixer + gb[16:32]
        d_a_param = mixer[7:8] * _sigmoid_tail(vec_ref[V_A_PARAM:V_A_PARAM + 1, :])
        grads = [dg1, dg2, mixer[4:5], mixer[5:6], mixer[6:7], d_a_param, mixer[8:9], mixer[9:10], dgf,
                 db_ada, mixer[0:4]]
        for k in range(n):
            outs[k][...] = grads[k]
            outs[n + k][...], outs[2 * n + k][...], outs[3 * n + k][...] = _adam(
                w_refs[k][...], grads[k], m_refs[k][...], v_refs[k][...])
        outs[4 * n + 1][...] = jnp.broadcast_to(jnp.sum(loss_lanes, axis=1, keepdims=True), (8, 128))

    shapes = [jax.ShapeDtypeStruct(w.shape, F32) for w in ws]
    return pl.pallas_call(
        body, name="small_finish",
        out_shape=shapes * 4 + [jax.ShapeDtypeStruct((N_DEV, 6, D), F32), jax.ShapeDtypeStruct((8, 128), F32)],
    )(gathered, mod_all, vecs, *ws, *ms, *vs)


def _pad_rows(a, rows):
    return jnp.pad(a, ((0, rows - a.shape[0]), (0, 0)))


def kernel(x, c, norm_mix_g, norm_mlp_g, w_ada, b_ada, w_in, conv_w, conv_b, w_rg_a, b_rg_a, w_rg_x, b_rg_x, a_param, w_branch_a, w_pool, b_pool, pool_scale, w_branch_b, w_out, w_up, w_down, final_g, loss_target, m_norm_mix_g, m_norm_mlp_g, m_w_ada, m_b_ada, m_w_in, m_conv_w, m_conv_b, m_w_rg_a, m_b_rg_a, m_w_rg_x, m_b_rg_x, m_a_param, m_w_branch_a, m_w_pool, m_b_pool, m_pool_scale, m_w_branch_b, m_w_out, m_w_up, m_w_down, m_final_g, v_norm_mix_g, v_norm_mlp_g, v_w_ada, v_b_ada, v_w_in, v_conv_w, v_conv_b, v_w_rg_a, v_b_rg_a, v_w_rg_x, v_b_rg_x, v_a_param, v_w_branch_a, v_w_pool, v_b_pool, v_pool_scale, v_w_branch_b, v_w_out, v_w_up, v_w_down, v_final_g):
    me = 4 * lax.axis_index("x") + 2 * lax.axis_index("y") + lax.axis_index("c")
    s = x.shape[1]
    x2d = x.reshape(s, D)
    target = loss_target.reshape(s, D)
    n_ada = w_ada.shape[2]

    sharded = dict(w_in=(w_in[0], 1), w_up=(w_up[0], 1), w_down=(w_down[0], 0), w_branch_a=(w_branch_a[0], 0),
                   w_branch_b=(w_branch_b[0], 0), w_out=(w_out[0], 0), w_rg_a=(w_rg_a[0], 1), w_rg_x=(w_rg_x[0], 1),
                   w_pool=(w_pool[0], 1))
    kind = {k: v[1] for k, v in sharded.items()}
    shard = {k: v[0].astype(BF16) for k, v in sharded.items()}

    first_names = ["w_in", "w_rg_a", "w_rg_x", "w_pool"]
    branch_names = ["w_branch_a", "w_branch_b", "w_out"]
    mlp_names = ["w_up", "w_down"]
    g_first = _Gather([_pad_rows(c, 8), _pad_rows(conv_w[0], 8)] + [shard[k] for k in first_names],
                      [0, 1] + [kind[k] for k in first_names], c, "gather_first")
    g_first.forward(g_first.token)
    c_rows, conv_w_full, *first_full = g_first.finish(g_first.token)
    wg = dict(zip(first_names, first_full))

    c_all = c_rows.reshape(N_DEV, 8, D)[:, 0, :]
    b_ada_cols = lax.dynamic_slice(b_ada, (0, me * n_ada), (1, n_ada))
    mod_part = _ada_fwd(c_all, w_ada[0], b_ada_cols)
    mod_parts, = _all_gather([mod_part], [0], "gather_mod")

    def gather(group, after, name):
        return _Gather([shard[k] for k in group], [kind[k] for k in group], after, name)

    g_branch = gather(branch_names, mod_parts, "gather_branch")
    g_mlp = gather(mlp_names, g_branch.token, "gather_mlp")

    mod_all = jnp.transpose(mod_parts.reshape(N_DEV, N_DEV, n_ada), (1, 0, 2)).reshape(N_DEV, 6, D)
    mod_all = jnp.pad(mod_all, ((0, 0), (0, 2), (0, 0)))
    modr = lax.dynamic_index_in_dim(mod_all, me, 0, keepdims=False)
    vecs = _pack_vecs(conv_w_full, [conv_b, b_rg_a, b_rg_x, a_param, b_pool, pool_scale,
                                    norm_mix_g, norm_mlp_g, final_g.reshape(1, D)])
    vecs = _after(vecs, g_mlp.token)

    h1, x_rnn, u_pool, ga, dga, sa, sb = _proj_fwd(x2d, modr, vecs, wg["w_in"])
    g_branch.forward(h1)
    xr, hr, za, p, pooled, *gates = _mix_fwd(x_rnn, u_pool, ga, _after(vecs, g_branch.token),
                                             wg["w_rg_a"], wg["w_rg_x"], wg["w_pool"])
    g_mlp.forward(za)
    wg.update(zip(branch_names, g_branch.finish(g_mlp.token)))
    ba, bb, merged, o, x2, h2 = _branch_fwd(za, pooled, sa, sb, x2d, modr, vecs,
                                            wg["w_branch_a"], wg["w_branch_b"], wg["w_out"])
    wg.update(zip(mlp_names, g_mlp.finish(h2)))
    ru, dx3, d_dn, small_f = _mlp_fwd(h2, x2, target, modr, vecs, wg["w_up"], wg["w_down"])

    near = _near(lax.axis_index("x"), lax.axis_index("y"), lax.axis_index("c"))
    own4 = jnp.stack([me, near[1][1], near[2][1], near[3][1]]).astype(jnp.int32)

    def scatter(group, partial, after, name):
        return _Scatter([partial[k] for k in group], [kind[k] for k in group], after, name)

    dup, dx2, do, small_m = _mlp_bwd(d_dn, ru, x2, dx3, o, modr, vecs, wg["w_up"], wg["w_down"])
    partial = dict(w_up=_wgrad(h2, dup, "wgrad_up"), w_down=_wgrad(ru, d_dn, "wgrad_down", square_a=True))
    s_mlp = scatter(mlp_names, partial, dx2, "scatter_mlp")

    dba, dbb, dgates, dza, dpooled = _branch_bwd(do, sa, sb, ba, bb, wg["w_branch_a"], wg["w_branch_b"], wg["w_out"],
                                                 dep=s_mlp.token)
    s_mlp.combine_and_send(own4, dza)
    dproj, dw_rg_a, dw_rg_x, dw_pool, small_x = _mix_bwd(dza, dpooled, x_rnn, ga, dga, xr, hr, p, gates, dgates,
                                                         _after(vecs, s_mlp.token),
                                                         wg["w_rg_a"], wg["w_rg_x"], wg["w_pool"])
    partial.update(w_branch_a=_wgrad(za, dba, "wgrad_branch_a"), w_branch_b=_wgrad(pooled, dbb, "wgrad_branch_b"),
                   w_out=_wgrad(merged, do, "wgrad_out"),
                   w_rg_a=dw_rg_a, w_rg_x=dw_rg_x, w_pool=dw_pool)
    mixer_names = ["w_rg_a", "w_rg_x", "w_pool", "w_branch_a", "w_branch_b", "w_out"]
    s_mixer = scatter(mixer_names, partial, s_mlp.token, "scatter_mixer")

    partial["w_in"] = _wgrad(h1, dproj, "wgrad_in", dep=s_mixer.token)
    s_in = scatter(["w_in"], partial, s_mixer.token, "scatter_in")
    s_mixer.combine_and_send(own4, s_in.token)
    s_in.combine_and_send(own4, s_mixer.token)
    grad_x, small_p = _proj_bwd(dproj, x2d, dx2, _after(modr, s_in.token), vecs, wg["w_in"])

    locals_ = dict(w_in=(w_in, m_w_in, v_w_in), w_up=(w_up, m_w_up, v_w_up), w_down=(w_down, m_w_down, v_w_down),
                   w_branch_a=(w_branch_a, m_w_branch_a, v_w_branch_a),
                   w_branch_b=(w_branch_b, m_w_branch_b, v_w_branch_b), w_out=(w_out, m_w_out, v_w_out),
                   w_rg_a=(w_rg_a, m_w_rg_a, v_w_rg_a), w_rg_x=(w_rg_x, m_w_rg_x, v_w_rg_x),
                   w_pool=(w_pool, m_w_pool, v_w_pool))
    res = {}

    def finish(group, exchange, after, n_tiles, name):
        chip_sums, arrivals = exchange.finish(after)
        flat = lambda t: t.reshape(-1, t.shape[-1])
        shapes = [flat(locals_[k][0]).shape for k in group]
        outs = _adam_group([cs.reshape(N_NEAR, *sh) for cs, sh in zip(chip_sums, shapes)],
                           [ar.reshape(N_NEAR - 1, *sh) for ar, sh in zip(arrivals, shapes)],
                           *[[flat(locals_[k][j]) for k in group] for j in range(3)], n_tiles, name)
        for i, k in enumerate(group):
            res[k] = [outs[j * len(group) + i].reshape(locals_[k][0].shape) for j in range(4)]
        return res[group[-1]][0]

    small = jnp.concatenate([small_f, small_m, small_x, small_p], axis=0)
    g_small = _Gather([small], [0], grad_x, "gather_small")
    done = finish(mlp_names, s_mlp, g_small.token, 4, "adam_mlp")
    done = finish(mixer_names, s_mixer, done, 2, "adam_mixer")
    g_small.forward(done)
    done = finish(["w_in"], s_in, g_small.token, 4, "adam_in")
    small_all, = g_small.finish(done)
    small_all = small_all.reshape(N_DEV, N_SMALL, D)

    def embed(cw):
        return lax.dynamic_update_slice(jnp.zeros((4, D), F32), cw[0], (0, me * (D // N_DEV)))

    def smalls(ng, nl, cb, bra, brx, ap, bp, ps, fg, ba_, cw):
        return [ng, nl, cb, bra, brx, ap, bp, ps, fg.reshape(1, D), ba_.reshape(6, D), embed(cw)]

    small_names = ["norm_mix_g", "norm_mlp_g", "conv_b", "b_rg_a", "b_rg_x", "a_param", "b_pool", "pool_scale",
                   "final_g", "b_ada", "conv_w"]
    fin = _small_finish(
        small_all, mod_all, vecs,
        smalls(norm_mix_g, norm_mlp_g, conv_b, b_rg_a, b_rg_x, a_param, b_pool, pool_scale, final_g, b_ada, conv_w),
        smalls(m_norm_mix_g, m_norm_mlp_g, m_conv_b, m_b_rg_a, m_b_rg_x, m_a_param, m_b_pool, m_pool_scale,
               m_final_g, m_b_ada, m_conv_w),
        smalls(v_norm_mix_g, v_norm_mlp_g, v_conv_b, v_b_rg_a, v_b_rg_x, v_a_param, v_b_pool, v_pool_scale,
               v_final_g, v_b_ada, v_conv_w))
    dmod_all, loss_tile = fin[4 * N_SMALL_PARAMS], fin[4 * N_SMALL_PARAMS + 1]
    dmod_cols = lax.dynamic_slice(dmod_all.reshape(N_DEV, 6 * D), (0, me * n_ada), (N_DEV, n_ada))
    res["w_ada"] = [t.reshape(w_ada.shape) for t in _ada_bwd_adam(c_all, dmod_cols, w_ada[0], m_w_ada[0], v_w_ada[0])]

    def final_shape(k, t):
        if k == "final_g":
            return t.reshape(D)
        if k == "b_ada":
            return t.reshape(1, 6 * D)
        if k == "conv_w":
            return lax.dynamic_slice(t, (0, me * (D // N_DEV)), (4, D // N_DEV)).reshape(conv_w.shape)
        return t

    for i, k in enumerate(small_names):
        res[k] = [final_shape(k, fin[which * N_SMALL_PARAMS + i]) for which in range(4)]
    order = ["norm_mix_g", "norm_mlp_g", "w_ada", "b_ada", "w_in", "conv_w", "conv_b", "w_rg_a", "b_rg_a", "w_rg_x",
             "b_rg_x", "a_param", "w_branch_a", "w_pool", "b_pool", "pool_scale", "w_branch_b", "w_out", "w_up",
             "w_down", "final_g"]
    outs = [loss_tile[0, 0], grad_x.reshape(x.shape)]
    for which in range(4):
        for k in order:
            outs.append(res[k][which])
    return tuple(outs)
```

```python
import functools

import jax
import jax.numpy as jnp
from jax import lax
from jax.experimental import pallas as pl
from jax.experimental.pallas import tpu as pltpu

F32 = jnp.float32
BF16 = jnp.bfloat16
MESH = pl.DeviceIdType.MESH

N_DEV = 8
D = 1024
N_GROUPS = 4
GW = D // N_GROUPS
D_IN = 5 * D
D_FF = 4 * D
POOL_WINDOWS = (2, 4, 8, 16)
HALO_X = 8
HALO_U = 16
EPS = 1e-6
C_RG = 8.0
ADAM_LR, ADAM_B1, ADAM_B2, ADAM_EPS, ADAM_WD, ADAM_STEP = 0.001, 0.9, 0.999, 1e-08, 0.01, 10

V7X_VMEM_LIMIT = 56 * 1024 * 1024

V_CONV_W, V_CONV_B, V_B_RG_A, V_B_RG_X, V_A_PARAM, V_B_POOL, V_POOL_SCALE, V_G1, V_G2, V_GF = 0, 4, 5, 6, 7, 8, 9, 10, 11, 12
M_SH1, M_SC1, M_GT1, M_SH2, M_SC2, M_GT2 = 0, 1, 2, 3, 4, 5

TM_PROJ = 512
TM_MIX = 256
TM_BRANCH = 512
TM_MLP = 512
TM_MLP_BWD = 256
TS_WGRAD = 1024


def _params(semantics):
    return pltpu.CompilerParams(dimension_semantics=semantics, vmem_limit_bytes=V7X_VMEM_LIMIT)


def _resident(shape):
    return pl.BlockSpec(shape, lambda *_: (0,) * len(shape), pipeline_mode=pl.Buffered(1))


def _dot(a, b):
    return jnp.dot(a, b, preferred_element_type=F32)


def _dot_nt(a, b):
    return lax.dot_general(a, b, (((1,), (1,)), ((), ())), preferred_element_type=F32)


def _dot_tn(a, b):
    return lax.dot_general(a, b, (((0,), (0,)), ((), ())), preferred_element_type=F32)


def _sigmoid(x):
    return 0.5 * jnp.tanh(0.5 * x) + 0.5


def _sigmoid_tail(x):
    return 1.0 / (1.0 + jnp.exp(-x))


def _gelu_and_grad(x):
    k = 0.7978845608028654
    x2 = x * x
    t = jnp.tanh(k * (x + 0.044715 * x * x2))
    g = 0.5 * x * (1.0 + t)
    dg = 0.5 * (1.0 + t) + 0.5 * x * (1.0 - t * t) * (k * (1.0 + 3.0 * 0.044715 * x2))
    return g, dg


def _softplus(a):
    e = jnp.exp(-jnp.abs(a))
    u = 1.0 + e
    log1p_e = jnp.where(u == 1.0, e, jnp.log(u) * e / jnp.where(u == 1.0, 1.0, u - 1.0))
    return jnp.maximum(a, 0.0) + log1p_e


def _neg_expm1(z):
    series = -(z * (1.0 + z * (0.5 + z * (1.0 / 6.0 + z * (1.0 / 24.0 + z * (1.0 / 120.0))))))
    return jnp.where(z > -0.1, series, 1.0 - jnp.exp(z))


def _shift_down(x, k):
    return pltpu.roll(x, k, 0)


def _shift_up(x, k):
    return pltpu.roll(x, x.shape[0] - k, 0)


def _rglru_gates(xr, w_a, w_x, b_a, b_x, a_param, is_t0):
    xb = xr.astype(BF16)
    ra = _sigmoid(_dot(xb, w_a) + b_a)
    ri = _sigmoid(_dot(xb, w_x) + b_x)
    sp = _softplus(a_param)
    log_a = (-C_RG) * ra * sp
    a = jnp.exp(log_a)
    mult = jnp.where(is_t0, 1.0, jnp.sqrt(_neg_expm1(2.0 * log_a)))
    return ra, ri, sp, a, mult


SUBLANES = 8


LANES = 128


def _scan_strip(a, b, carry, scr, down):
    t = b.shape[0]
    g = t // SUBLANES
    a3 = a.reshape(g, SUBLANES, LANES)
    b3 = b.reshape(g, SUBLANES, LANES)
    sub = lax.broadcasted_iota(jnp.int32, (g, SUBLANES, LANES), 1)
    for k in (1, 2, 4):
        keep = sub >= k if down else sub < SUBLANES - k
        shift = k if down else SUBLANES - k
        b3 = b3 + a3 * jnp.where(keep, pltpu.roll(b3, shift, 1), 0.0)
        a3 = a3 * jnp.where(keep, pltpu.roll(a3, shift, 1), 1.0)
    scr[0] = a3.reshape(t, LANES)
    scr[1] = b3.reshape(t, LANES)
    end_row = SUBLANES - 1 if down else 0
    ag = scr[0, pl.ds(end_row, g, stride=SUBLANES), :]
    bg = scr[1, pl.ds(end_row, g, stride=SUBLANES), :]
    rg = lax.broadcasted_iota(jnp.int32, (g, LANES), 0)
    edge = 0 if down else g - 1
    bg = bg + jnp.where(rg == edge, ag * carry, 0.0)
    k = 1
    while k < g:
        keep = rg >= k if down else rg < g - k
        shift = k if down else g - k
        bg = bg + ag * jnp.where(keep, pltpu.roll(bg, shift, 0), 0.0)
        if 2 * k < g:
            ag = ag * pltpu.roll(ag, shift, 0)
        k *= 2
    entering = jnp.where(rg != edge, pltpu.roll(bg, 1 if down else g - 1, 0), carry)
    for r in range(SUBLANES):
        scr[2, pl.ds(r, g, stride=SUBLANES), :] = entering
    return scr[1] + scr[0] * scr[2], bg[g - 1:g, :]


def _scan_strips(a, b, carry, scr, down):
    outs = [_scan_strip(a[:, c:c + LANES], b[:, c:c + LANES], carry[:, c:c + LANES], scr, down)
            for c in range(0, b.shape[1], LANES)]
    return jnp.concatenate([o[0] for o in outs], axis=1), jnp.concatenate([o[1] for o in outs], axis=1)


def _scan_down(a, b, carry, scr):
    return _scan_strips(a, b, carry, scr, True)


def _scan_up(m, b, carry, scr):
    return _scan_strips(m, b, carry, scr, False)[0]


def _window_mean(sums, window, first_block, head_t):
    scaled = sums * (1.0 / window)
    head = jnp.where(first_block, sums[:HALO_U] / jnp.minimum(head_t, float(window)), scaled[:HALO_U])
    return jnp.concatenate([head, scaled[HALO_U:]], axis=0)


def _conv_taps(x_ext):
    return [_shift_down(x_ext, 3 - j)[HALO_X:] if j < 3 else x_ext[HALO_X:] for j in range(4)]


def _proj_fwd(x, modr, vecs, w_in):
    s = x.shape[0]
    tm = min(TM_PROJ, s)

    def body(x_ref, mod_ref, vec_ref, w_ref, h1_ref, xrnn_ref, u_ref, ga_ref, dga_ref, sa_ref, sb_ref):
        xv = x_ref[...]
        r = lax.rsqrt(jnp.mean(xv * xv, axis=-1, keepdims=True) + EPS)
        gain = vec_ref[V_G1:V_G1 + 1, :] * (1.0 + mod_ref[M_SC1:M_SC1 + 1, :])
        h = (xv * r * gain + mod_ref[M_SH1:M_SH1 + 1, :]).astype(BF16)
        h1_ref[...] = h
        xrnn_ref[...] = _dot(h, w_ref[:, 0:D])
        ga_ref[...], dga_ref[...] = _gelu_and_grad(_dot(h, w_ref[:, D:2 * D]))
        u_ref[...] = _dot(h, w_ref[:, 2 * D:3 * D])
        sa_ref[...] = _sigmoid(_dot(h, w_ref[:, 3 * D:4 * D]))
        sb_ref[...] = _sigmoid(_dot(h, w_ref[:, 4 * D:5 * D]))

    tok = pl.BlockSpec((tm, D), lambda i: (i, 0))
    sd = lambda dt: jax.ShapeDtypeStruct((s, D), dt)
    return pl.pallas_call(
        body, name="proj_fwd", grid=(s // tm,),
        in_specs=[tok, pl.BlockSpec((8, D), lambda i: (0, 0)), pl.BlockSpec((16, D), lambda i: (0, 0)),
                  _resident((D, D_IN))],
        out_specs=[tok] * 7,
        out_shape=[sd(BF16)] + [sd(F32)] * 6,
        compiler_params=_params(("parallel",)),
    )(x, modr, vecs, w_in)


def _mix_fwd(x_rnn, u_pool, ga, vecs, w_rg_a, w_rg_x, w_pool):
    s = x_rnn.shape[0]
    tm = min(TM_MIX, s)
    nb = s // tm

    def body(xh_ref, x_ref, uh_ref, u_ref, ga_ref, vec_ref, wa_ref, wx_ref, wp_ref,
             xr_ref, hr_ref, za_ref, p_ref, pooled_ref, a_ref, mult_ref, ra_ref, ri_ref, carry_ref, scan_scr):
        i = pl.program_id(0)
        first = i == 0

        @pl.when(first)
        def _():
            carry_ref[...] = jnp.zeros_like(carry_ref)

        row = lax.broadcasted_iota(jnp.int32, (tm, GW), 0)
        is_t0 = jnp.logical_and(first, row == 0)
        head_t = (lax.broadcasted_iota(jnp.int32, (HALO_U, GW), 0) + 1).astype(F32)
        for g in range(N_GROUPS):
            cs = slice(g * GW, (g + 1) * GW)
            vec = vec_ref[:, cs]
            xh = jnp.where(first, 0.0, xh_ref[:, cs])
            taps = _conv_taps(jnp.concatenate([xh, x_ref[:, cs]], axis=0))
            xr = vec[V_CONV_B:V_CONV_B + 1]
            for j in range(4):
                xr = xr + vec[V_CONV_W + j:V_CONV_W + j + 1] * taps[j]
            xr_ref[:, cs] = xr
            ra, ri, _, a, mult = _rglru_gates(
                xr, wa_ref[g], wx_ref[g], vec[V_B_RG_A:V_B_RG_A + 1], vec[V_B_RG_X:V_B_RG_X + 1],
                vec[V_A_PARAM:V_A_PARAM + 1], is_t0)
            a_ref[:, cs] = a
            mult_ref[:, cs] = mult
            ra_ref[:, cs] = ra.astype(BF16)
            ri_ref[:, cs] = ri.astype(BF16)
            h, last = _scan_down(a, xr * ri * mult, carry_ref[0:1, cs], scan_scr)
            hr_ref[:, cs] = h
            carry_ref[0:1, cs] = last
            za_ref[:, cs] = (ga_ref[:, cs] * h).astype(BF16)
            uh = jnp.where(first, 0.0, uh_ref[:, cs])
            sm = jnp.concatenate([uh, u_ref[:, cs]], axis=0)
            k = 1
            while k < POOL_WINDOWS[g]:
                sm = sm + _shift_down(sm, k)
                k *= 2
            mean = _window_mean(sm[HALO_U:], POOL_WINDOWS[g], first, head_t)
            p = (mean - u_ref[:, cs]).astype(BF16)
            p_ref[:, cs] = p
            pb = _dot(p, wp_ref[g]) + vec[V_B_POOL:V_B_POOL + 1]
            pooled_ref[:, cs] = (pb * vec[V_POOL_SCALE:V_POOL_SCALE + 1]).astype(BF16)

    tok = pl.BlockSpec((tm, D), lambda i: (i, 0))
    halo = lambda rows: pl.BlockSpec((rows, D), lambda i: (jnp.maximum(i * (tm // rows) - 1, 0), 0))
    wspec = pl.BlockSpec((N_GROUPS, GW, GW), lambda i: (0, 0, 0))
    sd = lambda dt: jax.ShapeDtypeStruct((s, D), dt)
    return pl.pallas_call(
        body, name="mix_fwd", grid=(nb,),
        in_specs=[halo(HALO_X), tok, halo(HALO_U), tok, tok, pl.BlockSpec((16, D), lambda i: (0, 0)),
                  wspec, wspec, wspec],
        out_specs=[tok] * 9,
        out_shape=[sd(F32), sd(F32), sd(BF16), sd(BF16), sd(BF16), sd(F32), sd(F32), sd(BF16), sd(BF16)],
        scratch_shapes=[pltpu.VMEM((8, D), F32), pltpu.VMEM((3, tm, LANES), F32)],
        compiler_params=_params(("arbitrary",)),
    )(x_rnn, x_rnn, u_pool, u_pool, ga, vecs, w_rg_a, w_rg_x, w_pool)


def _branch_fwd(za, pooled, sa, sb, x, modr, vecs, w_a, w_b, w_out):
    s = x.shape[0]
    tm = min(TM_BRANCH, s)

    def body(za_ref, pooled_ref, sa_ref, sb_ref, x_ref, mod_ref, vec_ref, wa_ref, wb_ref, wo_ref,
             ba_ref, bb_ref, merged_ref, o_ref, x2_ref, h2_ref):
        ba = _dot(za_ref[...], wa_ref[...])
        bb = _dot(pooled_ref[...], wb_ref[...])
        ba_ref[...] = ba.astype(BF16)
        bb_ref[...] = bb.astype(BF16)
        merged = (sa_ref[...] * ba + sb_ref[...] * bb).astype(BF16)
        merged_ref[...] = merged
        o = _dot(merged, wo_ref[...])
        o_ref[...] = o.astype(BF16)
        x2 = x_ref[...] + mod_ref[M_GT1:M_GT1 + 1, :] * o
        x2_ref[...] = x2
        r = lax.rsqrt(jnp.mean(x2 * x2, axis=-1, keepdims=True) + EPS)
        gain = vec_ref[V_G2:V_G2 + 1, :] * (1.0 + mod_ref[M_SC2:M_SC2 + 1, :])
        h2_ref[...] = (x2 * r * gain + mod_ref[M_SH2:M_SH2 + 1, :]).astype(BF16)

    tok = pl.BlockSpec((tm, D), lambda i: (i, 0))
    wspec = pl.BlockSpec((D, D), lambda i: (0, 0))
    sd = lambda dt: jax.ShapeDtypeStruct((s, D), dt)
    return pl.pallas_call(
        body, name="branch_fwd", grid=(s // tm,),
        in_specs=[tok, tok, tok, tok,
                  tok, pl.BlockSpec((8, D), lambda i: (0, 0)), pl.BlockSpec((16, D), lambda i: (0, 0)),
                  wspec, wspec, wspec],
        out_specs=[tok] * 6,
        out_shape=[sd(BF16), sd(BF16), sd(BF16), sd(BF16), sd(F32), sd(BF16)],
        compiler_params=_params(("parallel",)),
    )(za, pooled, sa, sb, x, modr, vecs, w_a, w_b, w_out)


def _mlp_fwd(h2, x2, target, modr, vecs, w_up, w_down):
    s = x2.shape[0]
    tm = min(TM_MLP, s)

    def body(h2_ref, x2_ref, tgt_ref, mod_ref, vec_ref, wu_ref, wd_ref,
             ru_ref, dx3_ref, ddn_ref, small_ref):
        @pl.when(pl.program_id(0) == 0)
        def _():
            small_ref[...] = jnp.zeros_like(small_ref)

        h2 = h2_ref[...]
        dn = None
        for c in range(D_FF // D):
            cs = slice(c * D, (c + 1) * D)
            ru = jnp.maximum(_dot(h2, wu_ref[:, cs]), 0.0)
            ru_ref[:, cs] = ru.astype(BF16)
            part = _dot((ru * ru).astype(BF16), wd_ref[cs, :])
            dn = part if dn is None else dn + part
        gt2 = mod_ref[M_GT2:M_GT2 + 1, :]
        gf = vec_ref[V_GF:V_GF + 1, :]
        x3 = x2_ref[...] + gt2 * dn
        r3 = lax.rsqrt(jnp.mean(x3 * x3, axis=-1, keepdims=True) + EPS)
        n3 = x3 * r3
        err = n3 * gf - tgt_ref[...]
        dy = err * (1.0 / D)
        dn3 = dy * gf
        dx3 = r3 * (dn3 - n3 * jnp.mean(dn3 * n3, axis=-1, keepdims=True))
        dx3_ref[...] = dx3
        ddn_ref[...] = (dx3 * gt2).astype(BF16)
        small_ref[0:1, :] += jnp.sum(dy * n3, axis=0, keepdims=True)
        small_ref[1:2, :] += jnp.sum(dx3 * dn, axis=0, keepdims=True)
        small_ref[2:3, :] += (0.5 / D) * jnp.sum(err * err, axis=0, keepdims=True)

    tok = pl.BlockSpec((tm, D), lambda i: (i, 0))
    return pl.pallas_call(
        body, name="mlp_fwd", grid=(s // tm,),
        in_specs=[tok, tok, tok,
                  pl.BlockSpec((8, D), lambda i: (0, 0)), pl.BlockSpec((16, D), lambda i: (0, 0)),
                  _resident((D, D_FF)), _resident((D_FF, D))],
        out_specs=[pl.BlockSpec((tm, D_FF), lambda i: (i, 0)), tok, tok,
                   pl.BlockSpec((8, D), lambda i: (0, 0))],
        out_shape=[jax.ShapeDtypeStruct((s, D_FF), BF16), jax.ShapeDtypeStruct((s, D), F32),
                   jax.ShapeDtypeStruct((s, D), BF16), jax.ShapeDtypeStruct((8, D), F32)],
        compiler_params=_params(("arbitrary",)),
    )(h2, x2, target, modr, vecs, w_up, w_down)


def _mlp_bwd(d_dn, ru, x2, dx3, o, modr, vecs, w_up, w_down):
    s = x2.shape[0]
    tm = min(TM_MLP_BWD, s)

    def body(ddn_ref, ru_ref, x2_ref, dx3_ref, o_ref, mod_ref, vec_ref, wu_ref, wd_ref,
             dup_ref, dx2_ref, do_ref, small_ref):
        @pl.when(pl.program_id(0) == 0)
        def _():
            small_ref[...] = jnp.zeros_like(small_ref)

        ddn = ddn_ref[...]
        dh2 = None
        for c in range(D_FF // D):
            cs = slice(c * D, (c + 1) * D)
            dff = _dot_nt(ddn, wd_ref[cs, :])
            dup = (dff * (2.0 * ru_ref[:, cs].astype(F32))).astype(BF16)
            dup_ref[:, cs] = dup
            part = _dot_nt(dup, wu_ref[:, cs])
            dh2 = part if dh2 is None else dh2 + part
        x2 = x2_ref[...]
        r2 = lax.rsqrt(jnp.mean(x2 * x2, axis=-1, keepdims=True) + EPS)
        xn2 = x2 * r2
        gain = vec_ref[V_G2:V_G2 + 1, :] * (1.0 + mod_ref[M_SC2:M_SC2 + 1, :])
        dxn2 = dh2 * gain
        dx2 = dx3_ref[...] + r2 * (dxn2 - xn2 * jnp.mean(dxn2 * xn2, axis=-1, keepdims=True))
        dx2_ref[...] = dx2
        do_ref[...] = (dx2 * mod_ref[M_GT1:M_GT1 + 1, :]).astype(BF16)
        small_ref[0:1, :] += jnp.sum(dh2, axis=0, keepdims=True)
        small_ref[1:2, :] += jnp.sum(dh2 * xn2, axis=0, keepdims=True)
        small_ref[2:3, :] += jnp.sum(dx2 * o_ref[...].astype(F32), axis=0, keepdims=True)

    tok = pl.BlockSpec((tm, D), lambda i: (i, 0))
    wide = pl.BlockSpec((tm, D_FF), lambda i: (i, 0))
    return pl.pallas_call(
        body, name="mlp_bwd", grid=(s // tm,),
        in_specs=[tok, wide, tok, tok, tok,
                  pl.BlockSpec((8, D), lambda i: (0, 0)), pl.BlockSpec((16, D), lambda i: (0, 0)),
                  _resident((D, D_FF)), _resident((D_FF, D))],
        out_specs=[wide, tok, tok, pl.BlockSpec((8, D), lambda i: (0, 0))],
        out_shape=[jax.ShapeDtypeStruct((s, D_FF), BF16), jax.ShapeDtypeStruct((s, D), F32),
                   jax.ShapeDtypeStruct((s, D), BF16), jax.ShapeDtypeStruct((8, D), F32)],
        compiler_params=_params(("arbitrary",)),
    )(d_dn, ru, x2, dx3, o, modr, vecs, w_up, w_down)


def _branch_bwd(do, sa, sb, ba, bb, w_a, w_b, w_out, dep):
    s = do.shape[0]
    tm = min(TM_BRANCH, s)

    def body(do_ref, sa_ref, sb_ref, ba_ref, bb_ref, wa_ref, wb_ref, wo_ref, dep_ref,
             dba_ref, dbb_ref, dg_ref, dza_ref, dpooled_ref):
        dmerged = _dot_nt(do_ref[...], wo_ref[...])
        sa = sa_ref[...]
        sb = sb_ref[...]
        dba = (dmerged * sa).astype(BF16)
        dbb = (dmerged * sb).astype(BF16)
        dba_ref[...] = dba
        dbb_ref[...] = dbb
        dg_ref[:, :D] = (dmerged * ba_ref[...].astype(F32) * sa * (1.0 - sa)).astype(BF16)
        dg_ref[:, D:] = (dmerged * bb_ref[...].astype(F32) * sb * (1.0 - sb)).astype(BF16)
        dza_ref[...] = _dot_nt(dba, wa_ref[...])
        dpooled_ref[...] = _dot_nt(dbb, wb_ref[...])

    tok = pl.BlockSpec((tm, D), lambda i: (i, 0))
    wspec = pl.BlockSpec((D, D), lambda i: (0, 0))
    sd = lambda dt: jax.ShapeDtypeStruct((s, D), dt)
    return pl.pallas_call(
        body, name="branch_bwd", grid=(s // tm,),
        in_specs=[tok, tok, tok, tok, tok, wspec, wspec, wspec, pl.BlockSpec(memory_space=pl.ANY)],
        out_specs=[tok, tok, pl.BlockSpec((tm, 2 * D), lambda i: (i, 0)), tok, tok],
        out_shape=[sd(BF16), sd(BF16), jax.ShapeDtypeStruct((s, 2 * D), BF16), sd(F32), sd(F32)],
        compiler_params=_params(("parallel",)),
    )(do, sa, sb, ba, bb, w_a, w_b, w_out, dep)


def _mix_bwd(dza, dpooled, x_rnn, ga, dga, xr, hr, p, gates, dgates, vecs, w_rg_a, w_rg_x, w_pool):
    s = xr.shape[0]
    tm = min(TM_MIX, s)
    nb = s // tm

    def body(dza_ref, dpooled_ref, xh_ref, x_ref, ga_ref, dga_ref, xr_ref, hh_ref, hr_ref, p_ref,
             a_ref, mult_ref, ra_ref, ri_ref, dg_ref, vec_ref, wa_ref, wx_ref, wp_ref,
             dproj_ref, dwa_ref, dwx_ref, dwp_ref, small_ref,
             scan_carry, dxr_carry, q_carry, scan_scr, dwa_acc, dwx_acc, dwp_acc):
        i = pl.program_id(0)
        bi = nb - 1 - i
        first_t = bi == 0

        @pl.when(i == 0)
        def _():
            scan_carry[...] = jnp.zeros_like(scan_carry)
            dxr_carry[...] = jnp.zeros_like(dxr_carry)
            q_carry[...] = jnp.zeros_like(q_carry)
            dwa_acc[...] = jnp.zeros_like(dwa_acc)
            dwx_acc[...] = jnp.zeros_like(dwx_acc)
            dwp_acc[...] = jnp.zeros_like(dwp_acc)
            small_ref[...] = jnp.zeros_like(small_ref)

        row = lax.broadcasted_iota(jnp.int32, (tm, GW), 0)
        is_t0 = jnp.logical_and(first_t, row == 0)
        head_t = (lax.broadcasted_iota(jnp.int32, (HALO_U, GW), 0) + 1).astype(F32)
        colsum = lambda v: jnp.sum(v, axis=0, keepdims=True)
        for g in range(N_GROUPS):
            cs = slice(g * GW, (g + 1) * GW)
            vec = vec_ref[:, cs]
            xr = xr_ref[:, cs]
            hr = hr_ref[:, cs]
            dza = dza_ref[:, cs]
            dproj_ref[:, D + g * GW:D + (g + 1) * GW] = (dza * hr * dga_ref[:, cs]).astype(BF16)
            dhr = dza * ga_ref[:, cs]
            a = a_ref[:, cs]
            mult = mult_ref[:, cs]
            ra = ra_ref[:, cs].astype(F32)
            ri = ri_ref[:, cs].astype(F32)
            sp = _softplus(vec[V_A_PARAM:V_A_PARAM + 1])
            m = jnp.where(row == tm - 1, 1.0, _shift_up(a, 1))
            gsum = _scan_up(m, dhr, scan_carry[0:1, cs], scan_scr)
            scan_carry[0:1, cs] = a[0:1, :] * gsum[0:1, :]
            hh = jnp.where(first_t, 0.0, hh_ref[:, cs])
            hprev = _shift_down(jnp.concatenate([hh, hr], axis=0), 1)[8:]
            da = gsum * hprev
            dmult = jnp.where(is_t0, 0.0, gsum * xr * ri)
            dlog_a = da * a - dmult * a * a / mult
            dri = gsum * xr * mult
            dxr = gsum * ri * mult
            small_ref[7:8, cs] += colsum((-C_RG) * ra * dlog_a)
            dpa = (((-C_RG) * sp) * dlog_a * ra * (1.0 - ra))
            dpx = dri * ri * (1.0 - ri)
            small_ref[5:6, cs] += colsum(dpa)
            small_ref[6:7, cs] += colsum(dpx)
            dpa = dpa.astype(BF16)
            dpx = dpx.astype(BF16)
            xrb = xr.astype(BF16)
            dwa_acc[g] += _dot_tn(xrb, dpa)
            dwx_acc[g] += _dot_tn(xrb, dpx)
            dxr = dxr + _dot_nt(dpa, wa_ref[g]) + _dot_nt(dpx, wx_ref[g])
            small_ref[4:5, cs] += colsum(dxr)
            xh = jnp.where(first_t, 0.0, xh_ref[:, cs])
            taps = _conv_taps(jnp.concatenate([xh, x_ref[:, cs]], axis=0))
            dxr_ext = jnp.concatenate([dxr, dxr_carry[:, cs]], axis=0)
            dx = vec[V_CONV_W + 3:V_CONV_W + 4] * dxr
            for j in range(4):
                small_ref[j:j + 1, cs] += colsum(dxr * taps[j])
                if j < 3:
                    dx = dx + vec[V_CONV_W + j:V_CONV_W + j + 1] * _shift_up(dxr_ext, 3 - j)[:tm]
            dxr_carry[:, cs] = dxr[0:8, :]
            dproj_ref[:, cs] = dx.astype(BF16)
            pg = p_ref[:, cs]
            dpooled = dpooled_ref[:, cs]
            pb = _dot(pg, wp_ref[g]) + vec[V_B_POOL:V_B_POOL + 1]
            small_ref[9:10, cs] += colsum(dpooled * pb)
            dpb = dpooled * vec[V_POOL_SCALE:V_POOL_SCALE + 1]
            small_ref[8:9, cs] += colsum(dpb)
            dpbb = dpb.astype(BF16)
            dwp_acc[g] += _dot_tn(pg, dpbb)
            dp = _dot_nt(dpbb, wp_ref[g])
            q = _window_mean(dp, POOL_WINDOWS[g], first_t, head_t)
            sm = jnp.concatenate([q, q_carry[:, cs]], axis=0)
            k = 1
            while k < POOL_WINDOWS[g]:
                sm = sm + _shift_up(sm, k)
                k *= 2
            q_carry[:, cs] = q[0:HALO_U, :]
            dproj_ref[:, 2 * D + g * GW:2 * D + (g + 1) * GW] = (sm[:tm] - dp).astype(BF16)
        dproj_ref[:, 3 * D:] = dg_ref[...]

        @pl.when(i == nb - 1)
        def _():
            dwa_ref[...] = dwa_acc[...].astype(BF16)
            dwx_ref[...] = dwx_acc[...].astype(BF16)
            dwp_ref[...] = dwp_acc[...].astype(BF16)

    rev = lambda i: nb - 1 - i
    tok = pl.BlockSpec((tm, D), lambda i: (rev(i), 0))
    halo8 = lambda k: pl.BlockSpec((8, D), lambda i: (jnp.maximum(rev(i) * (tm // 8) - 1, 0), k))
    wspec = pl.BlockSpec((N_GROUPS, GW, GW), lambda i: (0, 0, 0))
    wshape = jax.ShapeDtypeStruct((N_GROUPS, GW, GW), BF16)
    return pl.pallas_call(
        body, name="mix_bwd", grid=(nb,),
        in_specs=[tok, tok, halo8(0), tok, tok, tok, tok, halo8(0), tok, tok, tok, tok, tok, tok,
                  pl.BlockSpec((tm, 2 * D), lambda i: (rev(i), 0)),
                  pl.BlockSpec((16, D), lambda i: (0, 0)), wspec, wspec, wspec],
        out_specs=[pl.BlockSpec((tm, D_IN), lambda i: (rev(i), 0)), wspec, wspec, wspec,
                   pl.BlockSpec((16, D), lambda i: (0, 0))],
        out_shape=[jax.ShapeDtypeStruct((s, D_IN), BF16), wshape, wshape, wshape,
                   jax.ShapeDtypeStruct((16, D), F32)],
        scratch_shapes=[pltpu.VMEM((8, D), F32), pltpu.VMEM((8, D), F32), pltpu.VMEM((HALO_U, D), F32),
                        pltpu.VMEM((3, tm, LANES), F32)] + [pltpu.VMEM((N_GROUPS, GW, GW), F32)] * 3,
        compiler_params=_params(("arbitrary",)),
    )(dza, dpooled, x_rnn, x_rnn, ga, dga, xr, hr, hr, p, *gates, dgates, vecs, w_rg_a, w_rg_x, w_pool)


def _proj_bwd(dproj, x, dx2, modr, vecs, w_in):
    s = x.shape[0]
    tm = min(TM_PROJ, s)

    def body(dp_ref, x_ref, dx2_ref, mod_ref, vec_ref, w_ref, gx_ref, small_ref):
        @pl.when(pl.program_id(0) == 0)
        def _():
            small_ref[...] = jnp.zeros_like(small_ref)

        dh1 = None
        for c in range(D_IN // D):
            cs = slice(c * D, (c + 1) * D)
            part = _dot_nt(dp_ref[:, cs], w_ref[:, cs])
            dh1 = part if dh1 is None else dh1 + part
        xv = x_ref[...]
        r1 = lax.rsqrt(jnp.mean(xv * xv, axis=-1, keepdims=True) + EPS)
        xn1 = xv * r1
        gain = vec_ref[V_G1:V_G1 + 1, :] * (1.0 + mod_ref[M_SC1:M_SC1 + 1, :])
        dxn1 = dh1 * gain
        gx_ref[...] = dx2_ref[...] + r1 * (dxn1 - xn1 * jnp.mean(dxn1 * xn1, axis=-1, keepdims=True))
        small_ref[0:1, :] += jnp.sum(dh1, axis=0, keepdims=True)
        small_ref[1:2, :] += jnp.sum(dh1 * xn1, axis=0, keepdims=True)

    tok = pl.BlockSpec((tm, D), lambda i: (i, 0))
    return pl.pallas_call(
        body, name="proj_bwd", grid=(s // tm,),
        in_specs=[pl.BlockSpec((tm, D_IN), lambda i: (i, 0)), tok, tok,
                  pl.BlockSpec((8, D), lambda i: (0, 0)), pl.BlockSpec((16, D), lambda i: (0, 0)),
                  _resident((D, D_IN))],
        out_specs=[tok, pl.BlockSpec((8, D), lambda i: (0, 0))],
        out_shape=[jax.ShapeDtypeStruct((s, D), F32), jax.ShapeDtypeStruct((8, D), F32)],
        compiler_params=_params(("arbitrary",)),
    )(dproj, x, dx2, modr, vecs, w_in)


def _wgrad(a, b, name, square_a=False, dep=None):
    s, ka = a.shape
    n = b.shape[1]
    tka = ka if ka <= 1024 else ka // 2
    tn = n if n <= 1024 else n // 2
    ts = min(TS_WGRAD, s)
    ns = s // ts
    nc = 512
    deps = [] if dep is None else [dep]

    def body(a_ref, b_ref, *refs):
        out_ref, acc_ref = refs[-2:]
        t = pl.program_id(2)

        @pl.when(t == 0)
        def _():
            acc_ref[...] = jnp.zeros_like(acc_ref)

        av = a_ref[...]
        if square_a:
            af = av.astype(F32)
            av = (af * af).astype(BF16)
        for c in range(tn // nc):
            cs = slice(c * nc, (c + 1) * nc)
            acc_ref[:, cs] += _dot_tn(av, b_ref[:, cs])

        @pl.when(t == ns - 1)
        def _():
            out_ref[...] = acc_ref[...].astype(BF16)

    return pl.pallas_call(
        body, name=name, grid=(ka // tka, n // tn, ns),
        in_specs=[pl.BlockSpec((ts, tka), lambda i, j, t: (t, i)),
                  pl.BlockSpec((ts, tn), lambda i, j, t: (t, j))] + [pl.BlockSpec(memory_space=pl.ANY)] * len(deps),
        out_specs=pl.BlockSpec((tka, tn), lambda i, j, t: (i, j)),
        out_shape=jax.ShapeDtypeStruct((ka, n), BF16),
        scratch_shapes=[pltpu.VMEM((tka, tn), F32)],
        compiler_params=_params(("parallel", "parallel", "arbitrary")),
    )(a, b, *deps)


def _window(ref, kind, idx, size):
    start = pl.multiple_of(idx * size, size)
    if kind == 0:
        return ref.at[pl.ds(start, size)]
    if kind == 1:
        return ref.at[:, pl.ds(start, size)]
    return ref.at[:, :, pl.ds(start, size)]


def _mesh_place():
    x, y, c = lax.axis_index("x"), lax.axis_index("y"), lax.axis_index("c")
    return x, y, c, 4 * x + 2 * y + c


def _peer(x, y, c, q):
    px = 1 - x if q & 4 else x
    py = 1 - y if q & 2 else y
    pc = 1 - c if q & 1 else c
    return (px, py, pc), 4 * px + 2 * py + pc


def _all_gather(shards, kinds, name):
    n = len(shards)
    full_shapes = []
    for sh, kind in zip(shards, kinds):
        dims = list(sh.shape)
        dims[kind] *= N_DEV
        full_shapes.append(jax.ShapeDtypeStruct(tuple(dims), sh.dtype))

    def body(*refs):
        ins, outs = refs[:n], refs[n:2 * n]
        send_sems, recv_sems, local_sems = refs[2 * n:]
        x, y, c, me = _mesh_place()
        sends, recvs, locals_ = [], [], []
        for k in range(n):
            size = shards[k].shape[kinds[k]]
            mine = _window(outs[k], kinds[k], me, size)
            lc = pltpu.make_async_copy(ins[k], mine, local_sems.at[k])
            lc.start()
            locals_.append(lc)
            for q in range(1, N_DEV):
                peer, peer_idx = _peer(x, y, c, q)
                cp = pltpu.make_async_remote_copy(
                    src_ref=ins[k], dst_ref=mine, send_sem=send_sems.at[k, q], recv_sem=recv_sems.at[k, q],
                    device_id=peer, device_id_type=MESH)
                cp.start()
                sends.append(cp)
                recvs.append(pltpu.make_async_remote_copy(
                    src_ref=ins[k], dst_ref=_window(outs[k], kinds[k], peer_idx, size),
                    send_sem=send_sems.at[k, q], recv_sem=recv_sems.at[k, q],
                    device_id=peer, device_id_type=MESH))
        for cp in recvs:
            cp.wait_recv()
        for cp in sends:
            cp.wait_send()
        for lc in locals_:
            lc.wait()

    any_spec = pl.BlockSpec(memory_space=pl.ANY)
    return pl.pallas_call(
        body, name=name,
        in_specs=[any_spec] * n, out_specs=[any_spec] * n, out_shape=full_shapes,
        scratch_shapes=[pltpu.SemaphoreType.DMA((n, N_DEV)), pltpu.SemaphoreType.DMA((n, N_DEV)),
                        pltpu.SemaphoreType.DMA((n,))],
    )(*shards)


_HBM = pl.BlockSpec(memory_space=pltpu.HBM)
_SEM = pl.BlockSpec(memory_space=pltpu.SEMAPHORE)
_EFFECT = pltpu.SideEffectType.DATAFLOW_SIDE_EFFECTING


N_NEAR = 4


def _near(x, y, c):
    out = [((x, y, 1 - c), 4 * x + 2 * y + 1 - c)]
    for j in (1, 2, 3):
        px = 1 - x if j & 2 else x
        py = 1 - y if j & 1 else y
        out.append(((px, py, c), 4 * px + 2 * py + c))
    return out


def _remote(src, dst, send_sems, recv_sems, slot, device):
    return pltpu.make_async_remote_copy(src_ref=src, dst_ref=dst, send_sem=send_sems.at[slot], recv_sem=recv_sems.at[slot],
                                        device_id=device, device_id_type=MESH)


def _split_call(name, arrays, sems_in, n_new_sems, after, emit):
    na, ns, nn = len(arrays), len(sems_in), len(n_new_sems)

    def body(*refs):
        emit(refs[:na], refs[na:na + ns], refs[na + ns + 1:na + ns + 1 + nn])
        refs[-1][...] = jnp.zeros_like(refs[-1])

    outs = pl.pallas_call(
        body, name=name,
        out_shape=(*[pltpu.SemaphoreType.DMA((m,)) for m in n_new_sems],
                   *[pltpu.HBM(a.shape, a.dtype) for a in arrays], jax.ShapeDtypeStruct((8, 128), F32)),
        in_specs=[_HBM] * na + [_SEM] * ns + [pl.BlockSpec(memory_space=pl.ANY)],
        out_specs=(*[_SEM] * nn, *[_HBM] * na, pl.BlockSpec(memory_space=pltpu.VMEM)),
        input_output_aliases={i: nn + i for i in range(na)},
        compiler_params=pltpu.CompilerParams(has_side_effects=_EFFECT),
    )(*[pltpu.with_memory_space_constraint(a, pltpu.HBM) for a in arrays], *sems_in, after)
    return list(outs[:nn]), list(outs[nn:nn + na]), outs[-1]


class _Gather:
    def __init__(self, shards, kinds, after, name):
        self.n, self.kinds, self.name = len(shards), kinds, name
        self.sizes = [s.shape[k] for s, k in zip(shards, kinds)]
        n = self.n
        lands = []
        for s, k in zip(shards, kinds):
            dims = list(s.shape)
            dims[k] *= N_DEV
            lands.append(lax.empty(tuple(dims), s.dtype))

        def emit(arr, _, new):
            x, y, c, me = _mesh_place()
            for k in range(n):
                pltpu.make_async_copy(arr[k], _window(arr[n + k], kinds[k], me, self.sizes[k]), new[2].at[k]).start()
            for k in range(n):
                mine = _window(arr[n + k], kinds[k], me, self.sizes[k])
                for j, (dev, _) in enumerate(_near(x, y, c)):
                    _remote(arr[k], mine, new[0], new[1], k * N_NEAR + j, dev).start()

        self.sems, self.arrays, self.token = _split_call(name + "_start", [*shards, *lands], [],
                                                         [n * N_NEAR, n * N_NEAR, n], after, emit)

    def forward(self, after):
        n, kinds, sizes = self.n, self.kinds, self.sizes

        def emit(arr, old, new):
            x, y, c, _ = _mesh_place()
            near = _near(x, y, c)
            for k in range(n):
                for j in (1, 2, 3):
                    dev, idx = near[j]
                    landed = _window(arr[n + k], kinds[k], idx, sizes[k])
                    _remote(arr[k], landed, old[0], old[1], k * N_NEAR + j, dev).wait_recv()
                    _remote(landed, landed, new[0], new[1], k * N_NEAR + j, near[0][0]).start()

        new, self.arrays, self.token = _split_call(self.name + "_forward", self.arrays, self.sems, [n * N_NEAR] * 2,
                                                   after, emit)
        self.sems = [*self.sems, *new]

    def finish(self, after):
        n, kinds, sizes = self.n, self.kinds, self.sizes

        def emit(arr, old, _):
            x, y, c, me = _mesh_place()
            near = _near(x, y, c)
            other_core = near[0][0]
            for k in range(n):
                win = lambda idx: _window(arr[n + k], kinds[k], idx, sizes[k])
                pltpu.make_async_copy(arr[k], win(me), old[2].at[k]).wait()
                for j, (dev, idx) in enumerate(near):
                    _remote(arr[k], win(me), old[0], old[1], k * N_NEAR + j, dev).wait_send()
                _remote(arr[k], win(near[0][1]), old[0], old[1], k * N_NEAR, other_core).wait_recv()
                for j in (1, 2, 3):
                    idx = near[j][1]
                    _remote(win(idx), win(idx), old[3], old[4], k * N_NEAR + j, other_core).wait_send()
                    _remote(arr[k], win(idx + 1 - 2 * c), old[3], old[4], k * N_NEAR + j, other_core).wait_recv()

        _, arrays, _ = _split_call(self.name + "_finish", self.arrays, self.sems, [], after, emit)
        return arrays[n:]


class _Scatter:
    def __init__(self, partials, kinds, after, name):
        self.n, self.kinds, self.name, self.partials = len(partials), kinds, name, partials
        self.sizes = [p.shape[k] // N_DEV for p, k in zip(partials, kinds)]
        n, sizes = self.n, self.sizes
        self.slot_shapes = []
        for p, k, size in zip(partials, kinds, sizes):
            dims = list(p.shape)
            dims[k] = size
            self.slot_shapes.append((N_NEAR, *dims))
        slots = [lax.empty(sh, p.dtype) for sh, p in zip(self.slot_shapes, partials)]

        def emit(arr, _, new):
            x, y, c, _ = _mesh_place()
            near = _near(x, y, c)
            for k in range(n):
                for j in range(N_NEAR):
                    owner = near[j][1] if j == 0 else near[j][1] + 1 - 2 * c
                    _remote(_window(arr[k], kinds[k], owner, sizes[k]), arr[n + k].at[j], new[0], new[1],
                            k * N_NEAR + j, near[0][0]).start()

        self.sems, self.arrays, self.token = _split_call(name + "_start", [*partials, *slots], [], [n * N_NEAR] * 2,
                                                         after, emit)

    def combine_and_send(self, own4, after):
        n, kinds, sizes = self.n, self.kinds, self.sizes

        def emit_wait(arr, old, _):
            x, y, c, _ = _mesh_place()
            near = _near(x, y, c)
            for k in range(n):
                for j in range(N_NEAR):
                    owner = near[j][1] if j == 0 else near[j][1] + 1 - 2 * c
                    cp = _remote(_window(arr[k], kinds[k], owner, sizes[k]), arr[n + k].at[j], old[0], old[1],
                                 k * N_NEAR + j, near[0][0])
                    cp.wait_send()
                    cp.wait_recv()

        _, arrays, _ = _split_call(self.name + "_landed", self.arrays, self.sems, [], after, emit_wait)
        chip_sums = _chip_sums(arrays[:n], arrays[n:], kinds, sizes, own4, self.name + "_combine")
        arrivals = [lax.empty((N_NEAR - 1, *sh[1:]), p.dtype) for sh, p in zip(self.slot_shapes, self.partials)]

        def emit_send(arr, _, new):
            x, y, c, _ = _mesh_place()
            near = _near(x, y, c)
            for k in range(n):
                for j in (1, 2, 3):
                    _remote(arr[k].at[j], arr[n + k].at[j - 1], new[0], new[1], k * N_NEAR + j, near[j][0]).start()

        self.sems, self.arrays, self.token = _split_call(self.name + "_send", [*chip_sums, *arrivals], [],
                                                         [n * N_NEAR] * 2, own4, emit_send)

    def finish(self, after):
        n = self.n

        def emit(arr, old, _):
            x, y, c, _ = _mesh_place()
            near = _near(x, y, c)
            for k in range(n):
                for j in (1, 2, 3):
                    cp = _remote(arr[k].at[j], arr[n + k].at[j - 1], old[0], old[1], k * N_NEAR + j, near[j][0])
                    cp.wait_send()
                    cp.wait_recv()

        _, arrays, _ = _split_call(self.name + "_finish", self.arrays, self.sems, [], after, emit)
        return arrays[:n], arrays[n:]


def _chip_sums(partials, slots, kinds, sizes, own4, name):
    n = len(partials)

    def body(own_ref, *refs):
        for k in range(n):
            refs[2 * n + k][...] = (refs[k][...].astype(F32) + refs[n + k][...].astype(F32)).astype(BF16)

    in_specs, slot_specs = [], []
    for p, s, kind, size in zip(partials, slots, kinds, sizes):
        block = list(p.shape)
        block[kind] = size
        nd = len(block)
        in_specs.append(pl.BlockSpec(tuple(block), functools.partial(
            lambda j, own, kind, nd: tuple(own[j] if d == kind else 0 for d in range(nd)), kind=kind, nd=nd)))
        slot_specs.append(pl.BlockSpec((None, *block), functools.partial(
            lambda j, own, nd: (j,) + (0,) * nd, nd=nd)))
    return pl.pallas_call(
        body, name=name,
        grid_spec=pltpu.PrefetchScalarGridSpec(num_scalar_prefetch=1, grid=(N_NEAR,),
                                               in_specs=in_specs + slot_specs, out_specs=slot_specs),
        out_shape=[jax.ShapeDtypeStruct(s.shape, s.dtype) for s in slots],
        compiler_params=_params(("arbitrary",)),
    )(own4, *partials, *slots)


def _after(small, token):
    return small + token[0:1, 0:1].astype(small.dtype)


def _silu(c):
    return c * _sigmoid_tail(c)


def _ada_fwd(c_all, w_ada, b_ada_cols):
    def body(c_ref, w_ref, b_ref, out_ref):
        out_ref[...] = jnp.dot(_silu(c_ref[...]), w_ref[...], preferred_element_type=F32,
                               precision=lax.Precision.HIGHEST) + b_ref[...]

    return pl.pallas_call(
        body, name="ada_fwd", out_shape=jax.ShapeDtypeStruct((N_DEV, w_ada.shape[1]), F32),
    )(c_all, w_ada, b_ada_cols)


def _adam(w, g, m, v):
    m = ADAM_B1 * m + (1.0 - ADAM_B1) * g
    v = ADAM_B2 * v + (1.0 - ADAM_B2) * (g * g)
    m_hat = m / (1.0 - ADAM_B1 ** ADAM_STEP)
    v_hat = v / (1.0 - ADAM_B2 ** ADAM_STEP)
    delta = -ADAM_LR * (m_hat / (jnp.sqrt(v_hat) + ADAM_EPS) + ADAM_WD * w)
    return delta, m, v


def _ada_bwd_adam(c_all, dmod_cols, w, m, v):
    def body(c_ref, d_ref, w_ref, m_ref, v_ref, g_ref, delta_ref, nm_ref, nv_ref):
        g = lax.dot_general(_silu(c_ref[...]), d_ref[...], (((0,), (0,)), ((), ())),
                            preferred_element_type=F32, precision=lax.Precision.HIGHEST)
        g_ref[...] = g
        delta_ref[...], nm_ref[...], nv_ref[...] = _adam(w_ref[...], g, m_ref[...], v_ref[...])

    sd = jax.ShapeDtypeStruct(w.shape, F32)
    return pl.pallas_call(body, name="ada_bwd_adam", out_shape=[sd] * 4,
                          compiler_params=pltpu.CompilerParams(vmem_limit_bytes=V7X_VMEM_LIMIT),
                          )(c_all, dmod_cols, w, m, v)


def _adam_group(chip_sums, arrivals, ws, ms, vs, n_tiles, name):
    n = len(ws)

    def body(*refs):
        for k in range(n):
            c_ref, a_ref, w_ref, m_ref, v_ref = (refs[j * n + k] for j in range(5))
            g_ref, delta_ref, nm_ref, nv_ref = (refs[(5 + j) * n + k] for j in range(4))
            g = c_ref[...].astype(F32)
            for j in range(N_NEAR - 1):
                g = g + a_ref[j].astype(F32)
            g_ref[...] = g
            delta_ref[...], nm_ref[...], nv_ref[...] = _adam(w_ref[...], g, m_ref[...], v_ref[...])

    tiles = [(w.shape[0] // n_tiles, w.shape[1]) for w in ws]
    blk = [pl.BlockSpec(t, lambda i: (i, 0)) for t in tiles]
    return pl.pallas_call(
        body, name=name, grid=(n_tiles,),
        in_specs=[pl.BlockSpec((None, *t), lambda i: (0, i, 0)) for t in tiles]
        + [pl.BlockSpec((N_NEAR - 1, *t), lambda i: (0, i, 0)) for t in tiles] + blk * 3,
        out_specs=blk * 4, out_shape=[jax.ShapeDtypeStruct(w.shape, F32) for w in ws] * 4,
        compiler_params=_params(("parallel",)),
    )(*chip_sums, *arrivals, *ws, *ms, *vs)


N_SMALL = 40
N_SMALL_PARAMS = 11


def _pack_vecs(conv_w_full, rows):
    def body(cw_ref, *refs):
        out = refs[-1]
        out[...] = jnp.zeros_like(out)
        out[0:4, :] = cw_ref[0:4, :]
        for r, ref in enumerate(refs[:-1]):
            out[4 + r:5 + r, :] = ref[...]

    return pl.pallas_call(body, name="pack_vecs", out_shape=jax.ShapeDtypeStruct((16, D), F32))(conv_w_full, *rows)


def _small_finish(gathered, mod_all, vecs, ws, ms, vs):
    n = N_SMALL_PARAMS

    def body(g_ref, mod_ref, vec_ref, *refs):
        w_refs, m_refs, v_refs = refs[:n], refs[n:2 * n], refs[2 * n:3 * n]
        outs = refs[3 * n:]
        g1 = vec_ref[V_G1:V_G1 + 1, :]
        g2 = vec_ref[V_G2:V_G2 + 1, :]
        zero = jnp.zeros((1, D), F32)
        dg1, dg2, dgf, loss_lanes = zero, zero, zero, zero
        mixer = jnp.zeros((16, D), F32)
        db_ada = jnp.zeros((6, D), F32)
        for b in range(N_DEV):
            gb = g_ref[b]
            mod = mod_ref[b]
            q1 = gb[33:34]
            q2 = gb[9:10]
            dmod = jnp.concatenate([gb[32:33], q1 * g1, gb[10:11], gb[8:9], q2 * g2, gb[1:2]], axis=0)
            outs[4 * n][b] = dmod
            db_ada = db_ada + dmod
            dg1 = dg1 + q1 * (1.0 + mod[M_SC1:M_SC1 + 1])
            dg2 = dg2 + q2 * (1.0 + mod[M_SC2:M_SC2 + 1])
            dgf = dgf + gb[0:1]
            loss_lanes = loss_lanes + gb[2:3]
            mixer = mixer + gb[16:32]
        d_a_param = mixer[7:8] * _sigmoid_tail(vec_ref[V_A_PARAM:V_A_PARAM + 1, :])
        grads = [dg1, dg2, mixer[4:5], mixer[5:6], mixer[6:7], d_a_param, mixer[8:9], mixer[9:10], dgf,
                 db_ada, mixer[0:4]]
        for k in range(n):
            outs[k][...] = grads[k]
            outs[n + k][...], outs[2 * n + k][...], outs[3 * n + k][...] = _adam(
                w_refs[k][...], grads[k], m_refs[k][...], v_refs[k][...])
        outs[4 * n + 1][...] = jnp.broadcast_to(jnp.sum(loss_lanes, axis=1, keepdims=True), (8, 128))

    shapes = [jax.ShapeDtypeStruct(w.shape, F32) for w in ws]
    return pl.pallas_call(
        body, name="small_finish",
        out_shape=shapes * 4 + [jax.ShapeDtypeStruct((N_DEV, 6, D), F32), jax.ShapeDtypeStruct((8, 128), F32)],
    )(gathered, mod_all, vecs, *ws, *ms, *vs)


def _pad_rows(a, rows):
    return jnp.pad(a, ((0, rows - a.shape[0]), (0, 0)))


def kernel(x, c, norm_mix_g, norm_mlp_g, w_ada, b_ada, w_in, conv_w, conv_b, w_rg_a, b_rg_a, w_rg_x, b_rg_x, a_param, w_branch_a, w_pool, b_pool, pool_scale, w_branch_b, w_out, w_up, w_down, final_g, loss_target, m_norm_mix_g, m_norm_mlp_g, m_w_ada, m_b_ada, m_w_in, m_conv_w, m_conv_b, m_w_rg_a, m_b_rg_a, m_w_rg_x, m_b_rg_x, m_a_param, m_w_branch_a, m_w_pool, m_b_pool, m_pool_scale, m_w_branch_b, m_w_out, m_w_up, m_w_down, m_final_g, v_norm_mix_g, v_norm_mlp_g, v_w_ada, v_b_ada, v_w_in, v_conv_w, v_conv_b, v_w_rg_a, v_b_rg_a, v_w_rg_x, v_b_rg_x, v_a_param, v_w_branch_a, v_w_pool, v_b_pool, v_pool_scale, v_w_branch_b, v_w_out, v_w_up, v_w_down, v_final_g):
    me = 4 * lax.axis_index("x") + 2 * lax.axis_index("y") + lax.axis_index("c")
    s = x.shape[1]
    x2d = x.reshape(s, D)
    target = loss_target.reshape(s, D)
    n_ada = w_ada.shape[2]

    sharded = dict(w_in=(w_in[0], 1), w_up=(w_up[0], 1), w_down=(w_down[0], 0), w_branch_a=(w_branch_a[0], 0),
                   w_branch_b=(w_branch_b[0], 0), w_out=(w_out[0], 0), w_rg_a=(w_rg_a[0], 1), w_rg_x=(w_rg_x[0], 1),
                   w_pool=(w_pool[0], 1))
    kind = {k: v[1] for k, v in sharded.items()}
    shard = {k: v[0].astype(BF16) for k, v in sharded.items()}

    conv_w_full, c_rows = _all_gather([_pad_rows(conv_w[0], 8), _pad_rows(c, 8)], [1, 0], "gather_c")
    c_all = c_rows.reshape(N_DEV, 8, D)[:, 0, :]
    b_ada_cols = lax.dynamic_slice(b_ada, (0, me * n_ada), (1, n_ada))
    mod_part = _ada_fwd(c_all, w_ada[0], b_ada_cols)
    mod_parts, = _all_gather([mod_part], [0], "gather_mod")

    first_names = ["w_in", "w_rg_a", "w_rg_x", "w_pool"]
    branch_names = ["w_branch_a", "w_branch_b", "w_out"]
    mlp_names = ["w_up", "w_down"]

    def gather(group, after, name):
        return _Gather([shard[k] for k in group], [kind[k] for k in group], after, name)

    g_first = gather(first_names, mod_parts, "gather_first")
    g_branch = gather(branch_names, g_first.token, "gather_branch")
    g_mlp = gather(mlp_names, g_branch.token, "gather_mlp")

    mod_all = jnp.transpose(mod_parts.reshape(N_DEV, N_DEV, n_ada), (1, 0, 2)).reshape(N_DEV, 6, D)
    mod_all = jnp.pad(mod_all, ((0, 0), (0, 2), (0, 0)))
    modr = lax.dynamic_index_in_dim(mod_all, me, 0, keepdims=False)
    vecs = _pack_vecs(conv_w_full, [conv_b, b_rg_a, b_rg_x, a_param, b_pool, pool_scale,
                                    norm_mix_g, norm_mlp_g, final_g.reshape(1, D)])
    vecs = _after(vecs, g_mlp.token)
    g_first.forward(vecs)
    wg = dict(zip(first_names, g_first.finish(g_first.token)))

    h1, x_rnn, u_pool, ga, dga, sa, sb = _proj_fwd(x2d, modr, vecs, wg["w_in"])
    g_branch.forward(h1)
    xr, hr, za, p, pooled, *gates = _mix_fwd(x_rnn, u_pool, ga, _after(vecs, g_branch.token),
                                             wg["w_rg_a"], wg["w_rg_x"], wg["w_pool"])
    g_mlp.forward(za)
    wg.update(zip(branch_names, g_branch.finish(g_mlp.token)))
    ba, bb, merged, o, x2, h2 = _branch_fwd(za, pooled, sa, sb, x2d, modr, vecs,
                                            wg["w_branch_a"], wg["w_branch_b"], wg["w_out"])
    wg.update(zip(mlp_names, g_mlp.finish(h2)))
    ru, dx3, d_dn, small_f = _mlp_fwd(h2, x2, target, modr, vecs, wg["w_up"], wg["w_down"])

    near = _near(lax.axis_index("x"), lax.axis_index("y"), lax.axis_index("c"))
    own4 = jnp.stack([me, near[1][1], near[2][1], near[3][1]]).astype(jnp.int32)

    def scatter(group, partial, after, name):
        return _Scatter([partial[k] for k in group], [kind[k] for k in group], after, name)

    dup, dx2, do, small_m = _mlp_bwd(d_dn, ru, x2, dx3, o, modr, vecs, wg["w_up"], wg["w_down"])
    partial = dict(w_up=_wgrad(h2, dup, "wgrad_up"), w_down=_wgrad(ru, d_dn, "wgrad_down", square_a=True))
    s_mlp = scatter(mlp_names, partial, dx2, "scatter_mlp")

    dba, dbb, dgates, dza, dpooled = _branch_bwd(do, sa, sb, ba, bb, wg["w_branch_a"], wg["w_branch_b"], wg["w_out"],
                                                 dep=s_mlp.token)
    s_mlp.combine_and_send(own4, dza)
    dproj, dw_rg_a, dw_rg_x, dw_pool, small_x = _mix_bwd(dza, dpooled, x_rnn, ga, dga, xr, hr, p, gates, dgates,
                                                         _after(vecs, s_mlp.token),
                                                         wg["w_rg_a"], wg["w_rg_x"], wg["w_pool"])
    partial.update(w_branch_a=_wgrad(za, dba, "wgrad_branch_a"), w_branch_b=_wgrad(pooled, dbb, "wgrad_branch_b"),
                   w_out=_wgrad(merged, do, "wgrad_out"),
                   w_rg_a=dw_rg_a, w_rg_x=dw_rg_x, w_pool=dw_pool)
    mixer_names = ["w_rg_a", "w_rg_x", "w_pool", "w_branch_a", "w_branch_b", "w_out"]
    s_mixer = scatter(mixer_names, partial, s_mlp.token, "scatter_mixer")

    partial["w_in"] = _wgrad(h1, dproj, "wgrad_in", dep=s_mixer.token)
    s_in = scatter(["w_in"], partial, s_mixer.token, "scatter_in")
    s_mixer.combine_and_send(own4, s_in.token)
    s_in.combine_and_send(own4, s_mixer.token)
    grad_x, small_p = _proj_bwd(dproj, x2d, dx2, _after(modr, s_in.token), vecs, wg["w_in"])

    locals_ = dict(w_in=(w_in, m_w_in, v_w_in), w_up=(w_up, m_w_up, v_w_up), w_down=(w_down, m_w_down, v_w_down),
                   w_branch_a=(w_branch_a, m_w_branch_a, v_w_branch_a),
                   w_branch_b=(w_branch_b, m_w_branch_b, v_w_branch_b), w_out=(w_out, m_w_out, v_w_out),
                   w_rg_a=(w_rg_a, m_w_rg_a, v_w_rg_a), w_rg_x=(w_rg_x, m_w_rg_x, v_w_rg_x),
                   w_pool=(w_pool, m_w_pool, v_w_pool))
    res = {}

    def finish(group, exchange, after, n_tiles, name):
        chip_sums, arrivals = exchange.finish(after)
        flat = lambda t: t.reshape(-1, t.shape[-1])
        shapes = [flat(locals_[k][0]).shape for k in group]
        outs = _adam_group([cs.reshape(N_NEAR, *sh) for cs, sh in zip(chip_sums, shapes)],
                           [ar.reshape(N_NEAR - 1, *sh) for ar, sh in zip(arrivals, shapes)],
                           *[[flat(locals_[k][j]) for k in group] for j in range(3)], n_tiles, name)
        for i, k in enumerate(group):
            res[k] = [outs[j * len(group) + i].reshape(locals_[k][0].shape) for j in range(4)]
        return res[group[-1]][0]

    small = jnp.concatenate([small_f, small_m, small_x, small_p], axis=0)
    g_small = _Gather([small], [0], grad_x, "gather_small")
    done = finish(mlp_names, s_mlp, g_small.token, 4, "adam_mlp")
    done = finish(mixer_names, s_mixer, done, 2, "adam_mixer")
    g_small.forward(done)
    done = finish(["w_in"], s_in, g_small.token, 4, "adam_in")
    small_all, = g_small.finish(done)
    small_all = small_all.reshape(N_DEV, N_SMALL, D)

    def embed(cw):
        return lax.dynamic_update_slice(jnp.zeros((4, D), F32), cw[0], (0, me * (D // N_DEV)))

    def smalls(ng, nl, cb, bra, brx, ap, bp, ps, fg, ba_, cw):
        return [ng, nl, cb, bra, brx, ap, bp, ps, fg.reshape(1, D), ba_.reshape(6, D), embed(cw)]

    small_names = ["norm_mix_g", "norm_mlp_g", "conv_b", "b_rg_a", "b_rg_x", "a_param", "b_pool", "pool_scale",
                   "final_g", "b_ada", "conv_w"]
    fin = _small_finish(
        small_all, mod_all, vecs,
        smalls(norm_mix_g, norm_mlp_g, conv_b, b_rg_a, b_rg_x, a_param, b_pool, pool_scale, final_g, b_ada, conv_w),
        smalls(m_norm_mix_g, m_norm_mlp_g, m_conv_b, m_b_rg_a, m_b_rg_x, m_a_param, m_b_pool, m_pool_scale,
               m_final_g, m_b_ada, m_conv_w),
        smalls(v_norm_mix_g, v_norm_mlp_g, v_conv_b, v_b_rg_a, v_b_rg_x, v_a_param, v_b_pool, v_pool_scale,
               v_final_g, v_b_ada, v_conv_w))
    dmod_all, loss_tile = fin[4 * N_SMALL_PARAMS], fin[4 * N_SMALL_PARAMS + 1]
    dmod_cols = lax.dynamic_slice(dmod_all.reshape(N_DEV, 6 * D), (0, me * n_ada), (N_DEV, n_ada))
    res["w_ada"] = [t.reshape(w_ada.shape) for t in _ada_bwd_adam(c_all, dmod_cols, w_ada[0], m_w_ada[0], v_w_ada[0])]

    def final_shape(k, t):
        if k == "final_g":
            return t.reshape(D)
        if k == "b_ada":
            return t.reshape(1, 6 * D)
        if k == "conv_w":
            return lax.dynamic_slice(t, (0, me * (D // N_DEV)), (4, D // N_DEV)).reshape(conv_w.shape)
        return t

    for i, k in enumerate(small_names):
        res[k] = [final_shape(k, fin[which * N_SMALL_PARAMS + i]) for which in range(4)]
    order = ["norm_mix_g", "norm_mlp_g", "w_ada", "b_ada", "w_in", "conv_w", "conv_b", "w_rg_a", "b_rg_a", "w_rg_x",
             "b_rg_x", "a_param", "w_branch_a", "w_pool", "b_pool", "pool_scale", "w_branch_b", "w_out", "w_up",
             "w_down", "final_g"]
    outs = [loss_tile[0, 0], grad_x.reshape(x.shape)]
    for which in range(4):
        for k in order:
            outs.append(res[k][which])
    return tuple(outs)
```

```python
import functools

import jax
import jax.numpy as jnp
from jax import lax
from jax.experimental import pallas as pl
from jax.experimental.pallas import tpu as pltpu

F32 = jnp.float32
BF16 = jnp.bfloat16
MESH = pl.DeviceIdType.MESH

N_DEV = 8
D = 1024
N_GROUPS = 4
GW = D // N_GROUPS
D_IN = 5 * D
D_FF = 4 * D
POOL_WINDOWS = (2, 4, 8, 16)
HALO_X = 8
HALO_U = 16
EPS = 1e-6
C_RG = 8.0
ADAM_LR, ADAM_B1, ADAM_B2, ADAM_EPS, ADAM_WD, ADAM_STEP = 0.001, 0.9, 0.999, 1e-08, 0.01, 10

V7X_VMEM_LIMIT = 56 * 1024 * 1024

V_CONV_W, V_CONV_B, V_B_RG_A, V_B_RG_X, V_A_PARAM, V_B_POOL, V_POOL_SCALE, V_G1, V_G2, V_GF = 0, 4, 5, 6, 7, 8, 9, 10, 11, 12
M_SH1, M_SC1, M_GT1, M_SH2, M_SC2, M_GT2 = 0, 1, 2, 3, 4, 5

TM_PROJ = 512
TM_MIX = 256
TM_BRANCH = 512
TM_MLP = 512
TM_MLP_BWD = 256
TS_WGRAD = 1024


def _params(semantics):
    return pltpu.CompilerParams(dimension_semantics=semantics, vmem_limit_bytes=V7X_VMEM_LIMIT)


def _resident(shape):
    return pl.BlockSpec(shape, lambda *_: (0,) * len(shape), pipeline_mode=pl.Buffered(1))


def _dot(a, b):
    return jnp.dot(a, b, preferred_element_type=F32)


def _dot_nt(a, b):
    return lax.dot_general(a, b, (((1,), (1,)), ((), ())), preferred_element_type=F32)


def _dot_tn(a, b):
    return lax.dot_general(a, b, (((0,), (0,)), ((), ())), preferred_element_type=F32)


def _sigmoid(x):
    return 0.5 * jnp.tanh(0.5 * x) + 0.5


def _sigmoid_tail(x):
    return 1.0 / (1.0 + jnp.exp(-x))


def _gelu_and_grad(x):
    k = 0.7978845608028654
    x2 = x * x
    t = jnp.tanh(k * (x + 0.044715 * x * x2))
    g = 0.5 * x * (1.0 + t)
    dg = 0.5 * (1.0 + t) + 0.5 * x * (1.0 - t * t) * (k * (1.0 + 3.0 * 0.044715 * x2))
    return g, dg


def _softplus(a):
    e = jnp.exp(-jnp.abs(a))
    u = 1.0 + e
    log1p_e = jnp.where(u == 1.0, e, jnp.log(u) * e / jnp.where(u == 1.0, 1.0, u - 1.0))
    return jnp.maximum(a, 0.0) + log1p_e


def _neg_expm1(z):
    series = -(z * (1.0 + z * (0.5 + z * (1.0 / 6.0 + z * (1.0 / 24.0 + z * (1.0 / 120.0))))))
    return jnp.where(z > -0.1, series, 1.0 - jnp.exp(z))


def _shift_down(x, k):
    return pltpu.roll(x, k, 0)


def _shift_up(x, k):
    return pltpu.roll(x, x.shape[0] - k, 0)


def _rglru_gates(xr, w_a, w_x, b_a, b_x, a_param, is_t0):
    xb = xr.astype(BF16)
    ra = _sigmoid(_dot(xb, w_a) + b_a)
    ri = _sigmoid(_dot(xb, w_x) + b_x)
    sp = _softplus(a_param)
    log_a = (-C_RG) * ra * sp
    a = jnp.exp(log_a)
    mult = jnp.where(is_t0, 1.0, jnp.sqrt(_neg_expm1(2.0 * log_a)))
    return ra, ri, sp, a, mult


SUBLANES = 8


LANES = 128


def _scan_strip(a, b, carry, scr, down):
    t = b.shape[0]
    g = t // SUBLANES
    a3 = a.reshape(g, SUBLANES, LANES)
    b3 = b.reshape(g, SUBLANES, LANES)
    sub = lax.broadcasted_iota(jnp.int32, (g, SUBLANES, LANES), 1)
    for k in (1, 2, 4):
        keep = sub >= k if down else sub < SUBLANES - k
        shift = k if down else SUBLANES - k
        b3 = b3 + a3 * jnp.where(keep, pltpu.roll(b3, shift, 1), 0.0)
        a3 = a3 * jnp.where(keep, pltpu.roll(a3, shift, 1), 1.0)
    scr[0] = a3.reshape(t, LANES)
    scr[1] = b3.reshape(t, LANES)
    end_row = SUBLANES - 1 if down else 0
    ag = scr[0, pl.ds(end_row, g, stride=SUBLANES), :]
    bg = scr[1, pl.ds(end_row, g, stride=SUBLANES), :]
    rg = lax.broadcasted_iota(jnp.int32, (g, LANES), 0)
    edge = 0 if down else g - 1
    bg = bg + jnp.where(rg == edge, ag * carry, 0.0)
    k = 1
    while k < g:
        keep = rg >= k if down else rg < g - k
        shift = k if down else g - k
        bg = bg + ag * jnp.where(keep, pltpu.roll(bg, shift, 0), 0.0)
        if 2 * k < g:
            ag = ag * pltpu.roll(ag, shift, 0)
        k *= 2
    entering = jnp.where(rg != edge, pltpu.roll(bg, 1 if down else g - 1, 0), carry)
    for r in range(SUBLANES):
        scr[2, pl.ds(r, g, stride=SUBLANES), :] = entering
    return scr[1] + scr[0] * scr[2], bg[g - 1:g, :]


def _scan_strips(a, b, carry, scr, down):
    outs = [_scan_strip(a[:, c:c + LANES], b[:, c:c + LANES], carry[:, c:c + LANES], scr, down)
            for c in range(0, b.shape[1], LANES)]
    return jnp.concatenate([o[0] for o in outs], axis=1), jnp.concatenate([o[1] for o in outs], axis=1)


def _scan_down(a, b, carry, scr):
    return _scan_strips(a, b, carry, scr, True)


def _scan_up(m, b, carry, scr):
    return _scan_strips(m, b, carry, scr, False)[0]


def _window_mean(sums, window, first_block, head_t):
    scaled = sums * (1.0 / window)
    head = jnp.where(first_block, sums[:HALO_U] / jnp.minimum(head_t, float(window)), scaled[:HALO_U])
    return jnp.concatenate([head, scaled[HALO_U:]], axis=0)


def _conv_taps(x_ext):
    return [_shift_down(x_ext, 3 - j)[HALO_X:] if j < 3 else x_ext[HALO_X:] for j in range(4)]


def _proj_fwd(x, modr, vecs, w_in):
    s = x.shape[0]
    tm = min(TM_PROJ, s)

    def body(x_ref, mod_ref, vec_ref, w_ref, h1_ref, xrnn_ref, u_ref, ga_ref, dga_ref, sa_ref, sb_ref):
        xv = x_ref[...]
        r = lax.rsqrt(jnp.mean(xv * xv, axis=-1, keepdims=True) + EPS)
        gain = vec_ref[V_G1:V_G1 + 1, :] * (1.0 + mod_ref[M_SC1:M_SC1 + 1, :])
        h = (xv * r * gain + mod_ref[M_SH1:M_SH1 + 1, :]).astype(BF16)
        h1_ref[...] = h
        xrnn_ref[...] = _dot(h, w_ref[:, 0:D])
        ga_ref[...], dga_ref[...] = _gelu_and_grad(_dot(h, w_ref[:, D:2 * D]))
        u_ref[...] = _dot(h, w_ref[:, 2 * D:3 * D])
        sa_ref[...] = _sigmoid(_dot(h, w_ref[:, 3 * D:4 * D]))
        sb_ref[...] = _sigmoid(_dot(h, w_ref[:, 4 * D:5 * D]))

    tok = pl.BlockSpec((tm, D), lambda i: (i, 0))
    sd = lambda dt: jax.ShapeDtypeStruct((s, D), dt)
    return pl.pallas_call(
        body, name="proj_fwd", grid=(s // tm,),
        in_specs=[tok, pl.BlockSpec((8, D), lambda i: (0, 0)), pl.BlockSpec((16, D), lambda i: (0, 0)),
                  _resident((D, D_IN))],
        out_specs=[tok] * 7,
        out_shape=[sd(BF16)] + [sd(F32)] * 6,
        compiler_params=_params(("parallel",)),
    )(x, modr, vecs, w_in)


def _mix_fwd(x_rnn, u_pool, ga, vecs, w_rg_a, w_rg_x, w_pool, dep):
    s = x_rnn.shape[0]
    tm = min(TM_MIX, s)
    nb = s // tm

    def body(xh_ref, x_ref, uh_ref, u_ref, ga_ref, vec_ref, wa_ref, wx_ref, wp_ref, dep_ref,
             xr_ref, hr_ref, za_ref, p_ref, pooled_ref, a_ref, mult_ref, ra_ref, ri_ref, carry_ref, scan_scr):
        i = pl.program_id(0)
        first = i == 0

        @pl.when(first)
        def _():
            carry_ref[...] = jnp.zeros_like(carry_ref)

        row = lax.broadcasted_iota(jnp.int32, (tm, GW), 0)
        is_t0 = jnp.logical_and(first, row == 0)
        head_t = (lax.broadcasted_iota(jnp.int32, (HALO_U, GW), 0) + 1).astype(F32)
        for g in range(N_GROUPS):
            cs = slice(g * GW, (g + 1) * GW)
            vec = vec_ref[:, cs]
            xh = jnp.where(first, 0.0, xh_ref[:, cs])
            taps = _conv_taps(jnp.concatenate([xh, x_ref[:, cs]], axis=0))
            xr = vec[V_CONV_B:V_CONV_B + 1]
            for j in range(4):
                xr = xr + vec[V_CONV_W + j:V_CONV_W + j + 1] * taps[j]
            xr_ref[:, cs] = xr
            ra, ri, _, a, mult = _rglru_gates(
                xr, wa_ref[g], wx_ref[g], vec[V_B_RG_A:V_B_RG_A + 1], vec[V_B_RG_X:V_B_RG_X + 1],
                vec[V_A_PARAM:V_A_PARAM + 1], is_t0)
            a_ref[:, cs] = a
            mult_ref[:, cs] = mult
            ra_ref[:, cs] = ra.astype(BF16)
            ri_ref[:, cs] = ri.astype(BF16)
            h, last = _scan_down(a, xr * ri * mult, carry_ref[0:1, cs], scan_scr)
            hr_ref[:, cs] = h
            carry_ref[0:1, cs] = last
            za_ref[:, cs] = (ga_ref[:, cs] * h).astype(BF16)
            uh = jnp.where(first, 0.0, uh_ref[:, cs])
            sm = jnp.concatenate([uh, u_ref[:, cs]], axis=0)
            k = 1
            while k < POOL_WINDOWS[g]:
                sm = sm + _shift_down(sm, k)
                k *= 2
            mean = _window_mean(sm[HALO_U:], POOL_WINDOWS[g], first, head_t)
            p = (mean - u_ref[:, cs]).astype(BF16)
            p_ref[:, cs] = p
            pb = _dot(p, wp_ref[g]) + vec[V_B_POOL:V_B_POOL + 1]
            pooled_ref[:, cs] = (pb * vec[V_POOL_SCALE:V_POOL_SCALE + 1]).astype(BF16)

    tok = pl.BlockSpec((tm, D), lambda i: (i, 0))
    halo = lambda rows: pl.BlockSpec((rows, D), lambda i: (jnp.maximum(i * (tm // rows) - 1, 0), 0))
    wspec = pl.BlockSpec((N_GROUPS, GW, GW), lambda i: (0, 0, 0))
    sd = lambda dt: jax.ShapeDtypeStruct((s, D), dt)
    return pl.pallas_call(
        body, name="mix_fwd", grid=(nb,),
        in_specs=[halo(HALO_X), tok, halo(HALO_U), tok, tok, pl.BlockSpec((16, D), lambda i: (0, 0)),
                  wspec, wspec, wspec, pl.BlockSpec(memory_space=pl.ANY)],
        out_specs=[tok] * 9,
        out_shape=[sd(F32), sd(F32), sd(BF16), sd(BF16), sd(BF16), sd(F32), sd(F32), sd(BF16), sd(BF16)],
        scratch_shapes=[pltpu.VMEM((8, D), F32), pltpu.VMEM((3, tm, LANES), F32)],
        compiler_params=_params(("arbitrary",)),
    )(x_rnn, x_rnn, u_pool, u_pool, ga, vecs, w_rg_a, w_rg_x, w_pool, dep)


def _branch_fwd(za, pooled, sa, sb, x, modr, vecs, w_a, w_b, w_out):
    s = x.shape[0]
    tm = min(TM_BRANCH, s)

    def body(za_ref, pooled_ref, sa_ref, sb_ref, x_ref, mod_ref, vec_ref, wa_ref, wb_ref, wo_ref,
             ba_ref, bb_ref, merged_ref, o_ref, x2_ref, h2_ref):
        ba = _dot(za_ref[...], wa_ref[...])
        bb = _dot(pooled_ref[...], wb_ref[...])
        ba_ref[...] = ba.astype(BF16)
        bb_ref[...] = bb.astype(BF16)
        merged = (sa_ref[...] * ba + sb_ref[...] * bb).astype(BF16)
        merged_ref[...] = merged
        o = _dot(merged, wo_ref[...])
        o_ref[...] = o.astype(BF16)
        x2 = x_ref[...] + mod_ref[M_GT1:M_GT1 + 1, :] * o
        x2_ref[...] = x2
        r = lax.rsqrt(jnp.mean(x2 * x2, axis=-1, keepdims=True) + EPS)
        gain = vec_ref[V_G2:V_G2 + 1, :] * (1.0 + mod_ref[M_SC2:M_SC2 + 1, :])
        h2_ref[...] = (x2 * r * gain + mod_ref[M_SH2:M_SH2 + 1, :]).astype(BF16)

    tok = pl.BlockSpec((tm, D), lambda i: (i, 0))
    wspec = pl.BlockSpec((D, D), lambda i: (0, 0))
    sd = lambda dt: jax.ShapeDtypeStruct((s, D), dt)
    return pl.pallas_call(
        body, name="branch_fwd", grid=(s // tm,),
        in_specs=[tok, tok, tok, tok,
                  tok, pl.BlockSpec((8, D), lambda i: (0, 0)), pl.BlockSpec((16, D), lambda i: (0, 0)),
                  wspec, wspec, wspec],
        out_specs=[tok] * 6,
        out_shape=[sd(BF16), sd(BF16), sd(BF16), sd(BF16), sd(F32), sd(BF16)],
        compiler_params=_params(("parallel",)),
    )(za, pooled, sa, sb, x, modr, vecs, w_a, w_b, w_out)


def _mlp_fwd(h2, x2, target, modr, vecs, w_up, w_down):
    s = x2.shape[0]
    tm = min(TM_MLP, s)

    def body(h2_ref, x2_ref, tgt_ref, mod_ref, vec_ref, wu_ref, wd_ref,
             ru_ref, dx3_ref, ddn_ref, small_ref):
        @pl.when(pl.program_id(0) == 0)
        def _():
            small_ref[...] = jnp.zeros_like(small_ref)

        h2 = h2_ref[...]
        dn = None
        for c in range(D_FF // D):
            cs = slice(c * D, (c + 1) * D)
            ru = jnp.maximum(_dot(h2, wu_ref[:, cs]), 0.0)
            ru_ref[:, cs] = ru.astype(BF16)
            part = _dot((ru * ru).astype(BF16), wd_ref[cs, :])
            dn = part if dn is None else dn + part
        gt2 = mod_ref[M_GT2:M_GT2 + 1, :]
        gf = vec_ref[V_GF:V_GF + 1, :]
        x3 = x2_ref[...] + gt2 * dn
        r3 = lax.rsqrt(jnp.mean(x3 * x3, axis=-1, keepdims=True) + EPS)
        n3 = x3 * r3
        err = n3 * gf - tgt_ref[...]
        dy = err * (1.0 / D)
        dn3 = dy * gf
        dx3 = r3 * (dn3 - n3 * jnp.mean(dn3 * n3, axis=-1, keepdims=True))
        dx3_ref[...] = dx3
        ddn_ref[...] = (dx3 * gt2).astype(BF16)
        small_ref[0:1, :] += jnp.sum(dy * n3, axis=0, keepdims=True)
        small_ref[1:2, :] += jnp.sum(dx3 * dn, axis=0, keepdims=True)
        small_ref[2:3, :] += (0.5 / D) * jnp.sum(err * err, axis=0, keepdims=True)

    tok = pl.BlockSpec((tm, D), lambda i: (i, 0))
    return pl.pallas_call(
        body, name="mlp_fwd", grid=(s // tm,),
        in_specs=[tok, tok, tok,
                  pl.BlockSpec((8, D), lambda i: (0, 0)), pl.BlockSpec((16, D), lambda i: (0, 0)),
                  _resident((D, D_FF)), _resident((D_FF, D))],
        out_specs=[pl.BlockSpec((tm, D_FF), lambda i: (i, 0)), tok, tok,
                   pl.BlockSpec((8, D), lambda i: (0, 0))],
        out_shape=[jax.ShapeDtypeStruct((s, D_FF), BF16), jax.ShapeDtypeStruct((s, D), F32),
                   jax.ShapeDtypeStruct((s, D), BF16), jax.ShapeDtypeStruct((8, D), F32)],
        compiler_params=_params(("arbitrary",)),
    )(h2, x2, target, modr, vecs, w_up, w_down)


def _mlp_bwd(d_dn, ru, x2, dx3, o, modr, vecs, w_up, w_down):
    s = x2.shape[0]
    tm = min(TM_MLP_BWD, s)

    def body(ddn_ref, ru_ref, x2_ref, dx3_ref, o_ref, mod_ref, vec_ref, wu_ref, wd_ref,
             dup_ref, dx2_ref, do_ref, small_ref):
        @pl.when(pl.program_id(0) == 0)
        def _():
            small_ref[...] = jnp.zeros_like(small_ref)

        ddn = ddn_ref[...]
        dh2 = None
        for c in range(D_FF // D):
            cs = slice(c * D, (c + 1) * D)
            dff = _dot_nt(ddn, wd_ref[cs, :])
            dup = (dff * (2.0 * ru_ref[:, cs].astype(F32))).astype(BF16)
            dup_ref[:, cs] = dup
            part = _dot_nt(dup, wu_ref[:, cs])
            dh2 = part if dh2 is None else dh2 + part
        x2 = x2_ref[...]
        r2 = lax.rsqrt(jnp.mean(x2 * x2, axis=-1, keepdims=True) + EPS)
        xn2 = x2 * r2
        gain = vec_ref[V_G2:V_G2 + 1, :] * (1.0 + mod_ref[M_SC2:M_SC2 + 1, :])
        dxn2 = dh2 * gain
        dx2 = dx3_ref[...] + r2 * (dxn2 - xn2 * jnp.mean(dxn2 * xn2, axis=-1, keepdims=True))
        dx2_ref[...] = dx2
        do_ref[...] = (dx2 * mod_ref[M_GT1:M_GT1 + 1, :]).astype(BF16)
        small_ref[0:1, :] += jnp.sum(dh2, axis=0, keepdims=True)
        small_ref[1:2, :] += jnp.sum(dh2 * xn2, axis=0, keepdims=True)
        small_ref[2:3, :] += jnp.sum(dx2 * o_ref[...].astype(F32), axis=0, keepdims=True)

    tok = pl.BlockSpec((tm, D), lambda i: (i, 0))
    wide = pl.BlockSpec((tm, D_FF), lambda i: (i, 0))
    return pl.pallas_call(
        body, name="mlp_bwd", grid=(s // tm,),
        in_specs=[tok, wide, tok, tok, tok,
                  pl.BlockSpec((8, D), lambda i: (0, 0)), pl.BlockSpec((16, D), lambda i: (0, 0)),
                  _resident((D, D_FF)), _resident((D_FF, D))],
        out_specs=[wide, tok, tok, pl.BlockSpec((8, D), lambda i: (0, 0))],
        out_shape=[jax.ShapeDtypeStruct((s, D_FF), BF16), jax.ShapeDtypeStruct((s, D), F32),
                   jax.ShapeDtypeStruct((s, D), BF16), jax.ShapeDtypeStruct((8, D), F32)],
        compiler_params=_params(("arbitrary",)),
    )(d_dn, ru, x2, dx3, o, modr, vecs, w_up, w_down)


def _branch_bwd(do, sa, sb, ba, bb, w_a, w_b, w_out, dep):
    s = do.shape[0]
    tm = min(TM_BRANCH, s)

    def body(do_ref, sa_ref, sb_ref, ba_ref, bb_ref, wa_ref, wb_ref, wo_ref, dep_ref,
             dba_ref, dbb_ref, dg_ref, dza_ref, dpooled_ref):
        dmerged = _dot_nt(do_ref[...], wo_ref[...])
        sa = sa_ref[...]
        sb = sb_ref[...]
        dba = (dmerged * sa).astype(BF16)
        dbb = (dmerged * sb).astype(BF16)
        dba_ref[...] = dba
        dbb_ref[...] = dbb
        dg_ref[:, :D] = (dmerged * ba_ref[...].astype(F32) * sa * (1.0 - sa)).astype(BF16)
        dg_ref[:, D:] = (dmerged * bb_ref[...].astype(F32) * sb * (1.0 - sb)).astype(BF16)
        dza_ref[...] = _dot_nt(dba, wa_ref[...])
        dpooled_ref[...] = _dot_nt(dbb, wb_ref[...])

    tok = pl.BlockSpec((tm, D), lambda i: (i, 0))
    wspec = pl.BlockSpec((D, D), lambda i: (0, 0))
    sd = lambda dt: jax.ShapeDtypeStruct((s, D), dt)
    return pl.pallas_call(
        body, name="branch_bwd", grid=(s // tm,),
        in_specs=[tok, tok, tok, tok, tok, wspec, wspec, wspec, pl.BlockSpec(memory_space=pl.ANY)],
        out_specs=[tok, tok, pl.BlockSpec((tm, 2 * D), lambda i: (i, 0)), tok, tok],
        out_shape=[sd(BF16), sd(BF16), jax.ShapeDtypeStruct((s, 2 * D), BF16), sd(F32), sd(F32)],
        compiler_params=_params(("parallel",)),
    )(do, sa, sb, ba, bb, w_a, w_b, w_out, dep)


def _mix_bwd(dza, dpooled, x_rnn, ga, dga, xr, hr, p, gates, dgates, vecs, w_rg_a, w_rg_x, w_pool, dep):
    s = xr.shape[0]
    tm = min(TM_MIX, s)
    nb = s // tm

    def body(dza_ref, dpooled_ref, xh_ref, x_ref, ga_ref, dga_ref, xr_ref, hh_ref, hr_ref, p_ref,
             a_ref, mult_ref, ra_ref, ri_ref, dg_ref, vec_ref, wa_ref, wx_ref, wp_ref, dep_ref,
             dproj_ref, dwa_ref, dwx_ref, dwp_ref, small_ref,
             scan_carry, dxr_carry, q_carry, scan_scr, dwa_acc, dwx_acc, dwp_acc):
        i = pl.program_id(0)
        bi = nb - 1 - i
        first_t = bi == 0

        @pl.when(i == 0)
        def _():
            scan_carry[...] = jnp.zeros_like(scan_carry)
            dxr_carry[...] = jnp.zeros_like(dxr_carry)
            q_carry[...] = jnp.zeros_like(q_carry)
            dwa_acc[...] = jnp.zeros_like(dwa_acc)
            dwx_acc[...] = jnp.zeros_like(dwx_acc)
            dwp_acc[...] = jnp.zeros_like(dwp_acc)
            small_ref[...] = jnp.zeros_like(small_ref)

        row = lax.broadcasted_iota(jnp.int32, (tm, GW), 0)
        is_t0 = jnp.logical_and(first_t, row == 0)
        head_t = (lax.broadcasted_iota(jnp.int32, (HALO_U, GW), 0) + 1).astype(F32)
        colsum = lambda v: jnp.sum(v, axis=0, keepdims=True)
        for g in range(N_GROUPS):
            cs = slice(g * GW, (g + 1) * GW)
            vec = vec_ref[:, cs]
            xr = xr_ref[:, cs]
            hr = hr_ref[:, cs]
            dza = dza_ref[:, cs]
            dproj_ref[:, D + g * GW:D + (g + 1) * GW] = (dza * hr * dga_ref[:, cs]).astype(BF16)
            dhr = dza * ga_ref[:, cs]
            a = a_ref[:, cs]
            mult = mult_ref[:, cs]
            ra = ra_ref[:, cs].astype(F32)
            ri = ri_ref[:, cs].astype(F32)
            sp = _softplus(vec[V_A_PARAM:V_A_PARAM + 1])
            m = jnp.where(row == tm - 1, 1.0, _shift_up(a, 1))
            gsum = _scan_up(m, dhr, scan_carry[0:1, cs], scan_scr)
            scan_carry[0:1, cs] = a[0:1, :] * gsum[0:1, :]
            hh = jnp.where(first_t, 0.0, hh_ref[:, cs])
            hprev = _shift_down(jnp.concatenate([hh, hr], axis=0), 1)[8:]
            da = gsum * hprev
            dmult = jnp.where(is_t0, 0.0, gsum * xr * ri)
            dlog_a = da * a - dmult * a * a / mult
            dri = gsum * xr * mult
            dxr = gsum * ri * mult
            small_ref[7:8, cs] += colsum((-C_RG) * ra * dlog_a)
            dpa = (((-C_RG) * sp) * dlog_a * ra * (1.0 - ra))
            dpx = dri * ri * (1.0 - ri)
            small_ref[5:6, cs] += colsum(dpa)
            small_ref[6:7, cs] += colsum(dpx)
            dpa = dpa.astype(BF16)
            dpx = dpx.astype(BF16)
            xrb = xr.astype(BF16)
            dwa_acc[g] += _dot_tn(xrb, dpa)
            dwx_acc[g] += _dot_tn(xrb, dpx)
            dxr = dxr + _dot_nt(dpa, wa_ref[g]) + _dot_nt(dpx, wx_ref[g])
            small_ref[4:5, cs] += colsum(dxr)
            xh = jnp.where(first_t, 0.0, xh_ref[:, cs])
            taps = _conv_taps(jnp.concatenate([xh, x_ref[:, cs]], axis=0))
            dxr_ext = jnp.concatenate([dxr, dxr_carry[:, cs]], axis=0)
            dx = vec[V_CONV_W + 3:V_CONV_W + 4] * dxr
            for j in range(4):
                small_ref[j:j + 1, cs] += colsum(dxr * taps[j])
                if j < 3:
                    dx = dx + vec[V_CONV_W + j:V_CONV_W + j + 1] * _shift_up(dxr_ext, 3 - j)[:tm]
            dxr_carry[:, cs] = dxr[0:8, :]
            dproj_ref[:, cs] = dx.astype(BF16)
            pg = p_ref[:, cs]
            dpooled = dpooled_ref[:, cs]
            pb = _dot(pg, wp_ref[g]) + vec[V_B_POOL:V_B_POOL + 1]
            small_ref[9:10, cs] += colsum(dpooled * pb)
            dpb = dpooled * vec[V_POOL_SCALE:V_POOL_SCALE + 1]
            small_ref[8:9, cs] += colsum(dpb)
            dpbb = dpb.astype(BF16)
            dwp_acc[g] += _dot_tn(pg, dpbb)
            dp = _dot_nt(dpbb, wp_ref[g])
            q = _window_mean(dp, POOL_WINDOWS[g], first_t, head_t)
            sm = jnp.concatenate([q, q_carry[:, cs]], axis=0)
            k = 1
            while k < POOL_WINDOWS[g]:
                sm = sm + _shift_up(sm, k)
                k *= 2
            q_carry[:, cs] = q[0:HALO_U, :]
            dproj_ref[:, 2 * D + g * GW:2 * D + (g + 1) * GW] = (sm[:tm] - dp).astype(BF16)
        dproj_ref[:, 3 * D:] = dg_ref[...]

        @pl.when(i == nb - 1)
        def _():
            dwa_ref[...] = dwa_acc[...].astype(BF16)
            dwx_ref[...] = dwx_acc[...].astype(BF16)
            dwp_ref[...] = dwp_acc[...].astype(BF16)

    rev = lambda i: nb - 1 - i
    tok = pl.BlockSpec((tm, D), lambda i: (rev(i), 0))
    halo8 = lambda k: pl.BlockSpec((8, D), lambda i: (jnp.maximum(rev(i) * (tm // 8) - 1, 0), k))
    wspec = pl.BlockSpec((N_GROUPS, GW, GW), lambda i: (0, 0, 0))
    wshape = jax.ShapeDtypeStruct((N_GROUPS, GW, GW), BF16)
    return pl.pallas_call(
        body, name="mix_bwd", grid=(nb,),
        in_specs=[tok, tok, halo8(0), tok, tok, tok, tok, halo8(0), tok, tok, tok, tok, tok, tok,
                  pl.BlockSpec((tm, 2 * D), lambda i: (rev(i), 0)),
                  pl.BlockSpec((16, D), lambda i: (0, 0)), wspec, wspec, wspec, pl.BlockSpec(memory_space=pl.ANY)],
        out_specs=[pl.BlockSpec((tm, D_IN), lambda i: (rev(i), 0)), wspec, wspec, wspec,
                   pl.BlockSpec((16, D), lambda i: (0, 0))],
        out_shape=[jax.ShapeDtypeStruct((s, D_IN), BF16), wshape, wshape, wshape,
                   jax.ShapeDtypeStruct((16, D), F32)],
        scratch_shapes=[pltpu.VMEM((8, D), F32), pltpu.VMEM((8, D), F32), pltpu.VMEM((HALO_U, D), F32),
                        pltpu.VMEM((3, tm, LANES), F32)] + [pltpu.VMEM((N_GROUPS, GW, GW), F32)] * 3,
        compiler_params=_params(("arbitrary",)),
    )(dza, dpooled, x_rnn, x_rnn, ga, dga, xr, hr, hr, p, *gates, dgates, vecs, w_rg_a, w_rg_x, w_pool, dep)


def _proj_bwd(dproj, x, dx2, modr, vecs, w_in, dep):
    s = x.shape[0]
    tm = min(TM_PROJ, s)

    def body(dp_ref, x_ref, dx2_ref, mod_ref, vec_ref, w_ref, dep_ref, gx_ref, small_ref):
        @pl.when(pl.program_id(0) == 0)
        def _():
            small_ref[...] = jnp.zeros_like(small_ref)

        dh1 = None
        for c in range(D_IN // D):
            cs = slice(c * D, (c + 1) * D)
            part = _dot_nt(dp_ref[:, cs], w_ref[:, cs])
            dh1 = part if dh1 is None else dh1 + part
        xv = x_ref[...]
        r1 = lax.rsqrt(jnp.mean(xv * xv, axis=-1, keepdims=True) + EPS)
        xn1 = xv * r1
        gain = vec_ref[V_G1:V_G1 + 1, :] * (1.0 + mod_ref[M_SC1:M_SC1 + 1, :])
        dxn1 = dh1 * gain
        gx_ref[...] = dx2_ref[...] + r1 * (dxn1 - xn1 * jnp.mean(dxn1 * xn1, axis=-1, keepdims=True))
        small_ref[0:1, :] += jnp.sum(dh1, axis=0, keepdims=True)
        small_ref[1:2, :] += jnp.sum(dh1 * xn1, axis=0, keepdims=True)

    tok = pl.BlockSpec((tm, D), lambda i: (i, 0))
    return pl.pallas_call(
        body, name="proj_bwd", grid=(s // tm,),
        in_specs=[pl.BlockSpec((tm, D_IN), lambda i: (i, 0)), tok, tok,
                  pl.BlockSpec((8, D), lambda i: (0, 0)), pl.BlockSpec((16, D), lambda i: (0, 0)),
                  _resident((D, D_IN)), pl.BlockSpec(memory_space=pl.ANY)],
        out_specs=[tok, pl.BlockSpec((8, D), lambda i: (0, 0))],
        out_shape=[jax.ShapeDtypeStruct((s, D), F32), jax.ShapeDtypeStruct((8, D), F32)],
        compiler_params=_params(("arbitrary",)),
    )(dproj, x, dx2, modr, vecs, w_in, dep)


def _wgrad(a, b, name, square_a=False, dep=None):
    s, ka = a.shape
    n = b.shape[1]
    tka = ka if ka <= 1024 else ka // 2
    tn = n if n <= 1024 else n // 2
    ts = min(TS_WGRAD, s)
    ns = s // ts
    nc = 512
    deps = [] if dep is None else [dep]

    def body(a_ref, b_ref, *refs):
        out_ref, acc_ref = refs[-2:]
        t = pl.program_id(2)

        @pl.when(t == 0)
        def _():
            acc_ref[...] = jnp.zeros_like(acc_ref)

        av = a_ref[...]
        if square_a:
            af = av.astype(F32)
            av = (af * af).astype(BF16)
        for c in range(tn // nc):
            cs = slice(c * nc, (c + 1) * nc)
            acc_ref[:, cs] += _dot_tn(av, b_ref[:, cs])

        @pl.when(t == ns - 1)
        def _():
            out_ref[...] = acc_ref[...].astype(BF16)

    return pl.pallas_call(
        body, name=name, grid=(ka // tka, n // tn, ns),
        in_specs=[pl.BlockSpec((ts, tka), lambda i, j, t: (t, i)),
                  pl.BlockSpec((ts, tn), lambda i, j, t: (t, j))] + [pl.BlockSpec(memory_space=pl.ANY)] * len(deps),
        out_specs=pl.BlockSpec((tka, tn), lambda i, j, t: (i, j)),
        out_shape=jax.ShapeDtypeStruct((ka, n), BF16),
        scratch_shapes=[pltpu.VMEM((tka, tn), F32)],
        compiler_params=_params(("parallel", "parallel", "arbitrary")),
    )(a, b, *deps)


def _window(ref, kind, idx, size):
    start = pl.multiple_of(idx * size, size)
    if kind == 0:
        return ref.at[pl.ds(start, size)]
    if kind == 1:
        return ref.at[:, pl.ds(start, size)]
    return ref.at[:, :, pl.ds(start, size)]


def _mesh_place():
    x, y, c = lax.axis_index("x"), lax.axis_index("y"), lax.axis_index("c")
    return x, y, c, 4 * x + 2 * y + c


def _peer(x, y, c, q):
    px = 1 - x if q & 4 else x
    py = 1 - y if q & 2 else y
    pc = 1 - c if q & 1 else c
    return (px, py, pc), 4 * px + 2 * py + pc


def _all_gather(shards, kinds, name):
    n = len(shards)
    full_shapes = []
    for sh, kind in zip(shards, kinds):
        dims = list(sh.shape)
        dims[kind] *= N_DEV
        full_shapes.append(jax.ShapeDtypeStruct(tuple(dims), sh.dtype))

    def body(*refs):
        ins, outs = refs[:n], refs[n:2 * n]
        send_sems, recv_sems, local_sems = refs[2 * n:]
        x, y, c, me = _mesh_place()
        sends, recvs, locals_ = [], [], []
        for k in range(n):
            size = shards[k].shape[kinds[k]]
            mine = _window(outs[k], kinds[k], me, size)
            lc = pltpu.make_async_copy(ins[k], mine, local_sems.at[k])
            lc.start()
            locals_.append(lc)
            for q in range(1, N_DEV):
                peer, peer_idx = _peer(x, y, c, q)
                cp = pltpu.make_async_remote_copy(
                    src_ref=ins[k], dst_ref=mine, send_sem=send_sems.at[k, q], recv_sem=recv_sems.at[k, q],
                    device_id=peer, device_id_type=MESH)
                cp.start()
                sends.append(cp)
                recvs.append(pltpu.make_async_remote_copy(
                    src_ref=ins[k], dst_ref=_window(outs[k], kinds[k], peer_idx, size),
                    send_sem=send_sems.at[k, q], recv_sem=recv_sems.at[k, q],
                    device_id=peer, device_id_type=MESH))
        for cp in recvs:
            cp.wait_recv()
        for cp in sends:
            cp.wait_send()
        for lc in locals_:
            lc.wait()

    any_spec = pl.BlockSpec(memory_space=pl.ANY)
    return pl.pallas_call(
        body, name=name,
        in_specs=[any_spec] * n, out_specs=[any_spec] * n, out_shape=full_shapes,
        scratch_shapes=[pltpu.SemaphoreType.DMA((n, N_DEV)), pltpu.SemaphoreType.DMA((n, N_DEV)),
                        pltpu.SemaphoreType.DMA((n,))],
    )(*shards)


_HBM = pl.BlockSpec(memory_space=pltpu.HBM)
_SEM = pl.BlockSpec(memory_space=pltpu.SEMAPHORE)
_EFFECT = pltpu.SideEffectType.DATAFLOW_SIDE_EFFECTING


N_NEAR = 4


def _near(x, y, c):
    out = [((x, y, 1 - c), 4 * x + 2 * y + 1 - c)]
    for j in (1, 2, 3):
        px = 1 - x if j & 2 else x
        py = 1 - y if j & 1 else y
        out.append(((px, py, c), 4 * px + 2 * py + c))
    return out


def _remote(src, dst, send_sems, recv_sems, slot, device):
    return pltpu.make_async_remote_copy(src_ref=src, dst_ref=dst, send_sem=send_sems.at[slot], recv_sem=recv_sems.at[slot],
                                        device_id=device, device_id_type=MESH)


def _split_call(name, arrays, sems_in, n_new_sems, after, emit):
    na, ns, nn = len(arrays), len(sems_in), len(n_new_sems)

    def body(*refs):
        emit(refs[:na], refs[na:na + ns], refs[na + ns + 1:na + ns + 1 + nn])
        refs[-1][...] = jnp.zeros_like(refs[-1])

    outs = pl.pallas_call(
        body, name=name,
        out_shape=(*[pltpu.SemaphoreType.DMA((m,)) for m in n_new_sems],
                   *[pltpu.HBM(a.shape, a.dtype) for a in arrays], jax.ShapeDtypeStruct((8, 128), F32)),
        in_specs=[_HBM] * na + [_SEM] * ns + [pl.BlockSpec(memory_space=pl.ANY)],
        out_specs=(*[_SEM] * nn, *[_HBM] * na, pl.BlockSpec(memory_space=pltpu.VMEM)),
        input_output_aliases={i: nn + i for i in range(na)},
        compiler_params=pltpu.CompilerParams(has_side_effects=_EFFECT),
    )(*[pltpu.with_memory_space_constraint(a, pltpu.HBM) for a in arrays], *sems_in, after)
    return list(outs[:nn]), list(outs[nn:nn + na]), outs[-1]


class _Gather:
    def __init__(self, shards, kinds, after, name):
        self.n, self.kinds, self.name = len(shards), kinds, name
        self.sizes = [s.shape[k] for s, k in zip(shards, kinds)]
        n = self.n
        lands = []
        for s, k in zip(shards, kinds):
            dims = list(s.shape)
            dims[k] *= N_DEV
            lands.append(lax.empty(tuple(dims), s.dtype))

        def emit(arr, _, new):
            x, y, c, me = _mesh_place()
            for k in range(n):
                pltpu.make_async_copy(arr[k], _window(arr[n + k], kinds[k], me, self.sizes[k]), new[2].at[k]).start()
            for k in range(n):
                mine = _window(arr[n + k], kinds[k], me, self.sizes[k])
                for j, (dev, _) in enumerate(_near(x, y, c)):
                    _remote(arr[k], mine, new[0], new[1], k * N_NEAR + j, dev).start()

        self.sems, self.arrays, self.token = _split_call(name + "_start", [*shards, *lands], [],
                                                         [n * N_NEAR, n * N_NEAR, n], after, emit)

    def forward(self, after):
        n, kinds, sizes = self.n, self.kinds, self.sizes

        def emit(arr, old, new):
            x, y, c, _ = _mesh_place()
            near = _near(x, y, c)
            for k in range(n):
                for j in (1, 2, 3):
                    dev, idx = near[j]
                    landed = _window(arr[n + k], kinds[k], idx, sizes[k])
                    _remote(arr[k], landed, old[0], old[1], k * N_NEAR + j, dev).wait_recv()
                    _remote(landed, landed, new[0], new[1], k * N_NEAR + j, near[0][0]).start()

        new, self.arrays, self.token = _split_call(self.name + "_forward", self.arrays, self.sems, [n * N_NEAR] * 2,
                                                   after, emit)
        self.sems = [*self.sems, *new]

    def finish(self, after):
        n, kinds, sizes = self.n, self.kinds, self.sizes

        def emit(arr, old, _):
            x, y, c, me = _mesh_place()
            near = _near(x, y, c)
            other_core = near[0][0]
            for k in range(n):
                win = lambda idx: _window(arr[n + k], kinds[k], idx, sizes[k])
                pltpu.make_async_copy(arr[k], win(me), old[2].at[k]).wait()
                for j, (dev, idx) in enumerate(near):
                    _remote(arr[k], win(me), old[0], old[1], k * N_NEAR + j, dev).wait_send()
                _remote(arr[k], win(near[0][1]), old[0], old[1], k * N_NEAR, other_core).wait_recv()
                for j in (1, 2, 3):
                    idx = near[j][1]
                    _remote(win(idx), win(idx), old[3], old[4], k * N_NEAR + j, other_core).wait_send()
                    _remote(arr[k], win(idx + 1 - 2 * c), old[3], old[4], k * N_NEAR + j, other_core).wait_recv()

        _, arrays, _ = _split_call(self.name + "_finish", self.arrays, self.sems, [], after, emit)
        return arrays[n:]


class _Scatter:
    def __init__(self, partials, kinds, after, name):
        self.n, self.kinds, self.name, self.partials = len(partials), kinds, name, partials
        self.sizes = [p.shape[k] // N_DEV for p, k in zip(partials, kinds)]
        n, sizes = self.n, self.sizes
        self.slot_shapes = []
        for p, k, size in zip(partials, kinds, sizes):
            dims = list(p.shape)
            dims[k] = size
            self.slot_shapes.append((N_NEAR, *dims))
        slots = [lax.empty(sh, p.dtype) for sh, p in zip(self.slot_shapes, partials)]

        def emit(arr, _, new):
            x, y, c, _ = _mesh_place()
            near = _near(x, y, c)
            for k in range(n):
                for j in range(N_NEAR):
                    owner = near[j][1] if j == 0 else near[j][1] + 1 - 2 * c
                    _remote(_window(arr[k], kinds[k], owner, sizes[k]), arr[n + k].at[j], new[0], new[1],
                            k * N_NEAR + j, near[0][0]).start()

        self.sems, self.arrays, self.token = _split_call(name + "_start", [*partials, *slots], [], [n * N_NEAR] * 2,
                                                         after, emit)

    def combine_and_send(self, own4, after):
        n, kinds, sizes = self.n, self.kinds, self.sizes

        def emit_wait(arr, old, _):
            x, y, c, _ = _mesh_place()
            near = _near(x, y, c)
            for k in range(n):
                for j in range(N_NEAR):
                    owner = near[j][1] if j == 0 else near[j][1] + 1 - 2 * c
                    cp = _remote(_window(arr[k], kinds[k], owner, sizes[k]), arr[n + k].at[j], old[0], old[1],
                                 k * N_NEAR + j, near[0][0])
                    cp.wait_send()
                    cp.wait_recv()

        _, arrays, _ = _split_call(self.name + "_landed", self.arrays, self.sems, [], after, emit_wait)
        chip_sums = _chip_sums(arrays[:n], arrays[n:], kinds, sizes, own4, self.name + "_combine")
        arrivals = [lax.empty((N_NEAR - 1, *sh[1:]), p.dtype) for sh, p in zip(self.slot_shapes, self.partials)]

        def emit_send(arr, _, new):
            x, y, c, _ = _mesh_place()
            near = _near(x, y, c)
            for k in range(n):
                for j in (1, 2, 3):
                    _remote(arr[k].at[j], arr[n + k].at[j - 1], new[0], new[1], k * N_NEAR + j, near[j][0]).start()

        self.sems, self.arrays, self.token = _split_call(self.name + "_send", [*chip_sums, *arrivals], [],
                                                         [n * N_NEAR] * 2, own4, emit_send)

    def finish(self, after):
        n = self.n

        def emit(arr, old, _):
            x, y, c, _ = _mesh_place()
            near = _near(x, y, c)
            for k in range(n):
                for j in (1, 2, 3):
                    cp = _remote(arr[k].at[j], arr[n + k].at[j - 1], old[0], old[1], k * N_NEAR + j, near[j][0])
                    cp.wait_send()
                    cp.wait_recv()

        _, arrays, _ = _split_call(self.name + "_finish", self.arrays, self.sems, [], after, emit)
        return arrays[:n], arrays[n:]


def _chip_sums(partials, slots, kinds, sizes, own4, name):
    n = len(partials)

    def body(own_ref, *refs):
        for k in range(n):
            refs[2 * n + k][...] = (refs[k][...].astype(F32) + refs[n + k][...].astype(F32)).astype(BF16)

    in_specs, slot_specs = [], []
    for p, s, kind, size in zip(partials, slots, kinds, sizes):
        block = list(p.shape)
        block[kind] = size
        nd = len(block)
        in_specs.append(pl.BlockSpec(tuple(block), functools.partial(
            lambda j, own, kind, nd: tuple(own[j] if d == kind else 0 for d in range(nd)), kind=kind, nd=nd)))
        slot_specs.append(pl.BlockSpec((None, *block), functools.partial(
            lambda j, own, nd: (j,) + (0,) * nd, nd=nd)))
    return pl.pallas_call(
        body, name=name,
        grid_spec=pltpu.PrefetchScalarGridSpec(num_scalar_prefetch=1, grid=(N_NEAR,),
                                               in_specs=in_specs + slot_specs, out_specs=slot_specs),
        out_shape=[jax.ShapeDtypeStruct(s.shape, s.dtype) for s in slots],
        compiler_params=_params(("arbitrary",)),
    )(own4, *partials, *slots)


def _to_bf16(arrays):
    def body(*refs):
        for src, dst in zip(refs[:len(arrays)], refs[len(arrays):]):
            dst[...] = src[...].astype(BF16)

    return pl.pallas_call(body, name="to_bf16", out_shape=[jax.ShapeDtypeStruct(a.shape, BF16) for a in arrays],
                          compiler_params=pltpu.CompilerParams(vmem_limit_bytes=V7X_VMEM_LIMIT))(*arrays)


def _silu(c):
    return c * _sigmoid_tail(c)


def _ada_fwd(c_all, w_ada, b_ada_cols):
    def body(c_ref, w_ref, b_ref, out_ref):
        out_ref[...] = jnp.dot(_silu(c_ref[...]), w_ref[...], preferred_element_type=F32,
                               precision=lax.Precision.HIGHEST) + b_ref[...]

    return pl.pallas_call(
        body, name="ada_fwd", out_shape=jax.ShapeDtypeStruct((N_DEV, w_ada.shape[1]), F32),
    )(c_all, w_ada, b_ada_cols)


def _adam(w, g, m, v):
    m = ADAM_B1 * m + (1.0 - ADAM_B1) * g
    v = ADAM_B2 * v + (1.0 - ADAM_B2) * (g * g)
    m_hat = m / (1.0 - ADAM_B1 ** ADAM_STEP)
    v_hat = v / (1.0 - ADAM_B2 ** ADAM_STEP)
    delta = -ADAM_LR * (m_hat / (jnp.sqrt(v_hat) + ADAM_EPS) + ADAM_WD * w)
    return delta, m, v


def _ada_bwd_adam(c_all, dmod_cols, w, m, v):
    def body(c_ref, d_ref, w_ref, m_ref, v_ref, g_ref, delta_ref, nm_ref, nv_ref):
        g = lax.dot_general(_silu(c_ref[...]), d_ref[...], (((0,), (0,)), ((), ())),
                            preferred_element_type=F32, precision=lax.Precision.HIGHEST)
        g_ref[...] = g
        delta_ref[...], nm_ref[...], nv_ref[...] = _adam(w_ref[...], g, m_ref[...], v_ref[...])

    sd = jax.ShapeDtypeStruct(w.shape, F32)
    return pl.pallas_call(body, name="ada_bwd_adam", out_shape=[sd] * 4,
                          compiler_params=pltpu.CompilerParams(vmem_limit_bytes=V7X_VMEM_LIMIT),
                          )(c_all, dmod_cols, w, m, v)


def _adam_group(chip_sums, arrivals, ws, ms, vs, n_tiles, name):
    n = len(ws)

    def body(*refs):
        for k in range(n):
            c_ref, a_ref, w_ref, m_ref, v_ref = (refs[j * n + k] for j in range(5))
            g_ref, delta_ref, nm_ref, nv_ref = (refs[(5 + j) * n + k] for j in range(4))
            g = c_ref[...].astype(F32)
            for j in range(N_NEAR - 1):
                g = g + a_ref[j].astype(F32)
            g_ref[...] = g
            delta_ref[...], nm_ref[...], nv_ref[...] = _adam(w_ref[...], g, m_ref[...], v_ref[...])

    tiles = [(w.shape[0] // n_tiles, w.shape[1]) for w in ws]
    blk = [pl.BlockSpec(t, lambda i: (i, 0)) for t in tiles]
    return pl.pallas_call(
        body, name=name, grid=(n_tiles,),
        in_specs=[pl.BlockSpec((None, *t), lambda i: (0, i, 0)) for t in tiles]
        + [pl.BlockSpec((N_NEAR - 1, *t), lambda i: (0, i, 0)) for t in tiles] + blk * 3,
        out_specs=blk * 4, out_shape=[jax.ShapeDtypeStruct(w.shape, F32) for w in ws] * 4,
        compiler_params=_params(("parallel",)),
    )(*chip_sums, *arrivals, *ws, *ms, *vs)


N_SMALL = 40
N_SMALL_PARAMS = 11


def _pack_vecs(conv_w_full, rows):
    def body(cw_ref, *refs):
        out = refs[-1]
        out[...] = jnp.zeros_like(out)
        out[0:4, :] = cw_ref[0:4, :]
        for r, ref in enumerate(refs[:-1]):
            out[4 + r:5 + r, :] = ref[...]

    return pl.pallas_call(body, name="pack_vecs", out_shape=jax.ShapeDtypeStruct((16, D), F32))(conv_w_full, *rows)


def _small_finish(gathered, mod_all, vecs, ws, ms, vs):
    n = N_SMALL_PARAMS

    def body(g_ref, mod_ref, vec_ref, *refs):
        w_refs, m_refs, v_refs = refs[:n], refs[n:2 * n], refs[2 * n:3 * n]
        outs = refs[3 * n:]
        g1 = vec_ref[V_G1:V_G1 + 1, :]
        g2 = vec_ref[V_G2:V_G2 + 1, :]
        zero = jnp.zeros((1, D), F32)
        dg1, dg2, dgf, loss_lanes = zero, zero, zero, zero
        mixer = jnp.zeros((16, D), F32)
        db_ada = jnp.zeros((6, D), F32)
        for b in range(N_DEV):
            gb = g_ref[b]
            mod = mod_ref[b]
            q1 = gb[33:34]
            q2 = gb[9:10]
            dmod = jnp.concatenate([gb[32:33], q1 * g1, gb[10:11], gb[8:9], q2 * g2, gb[1:2]], axis=0)
            outs[4 * n][b] = dmod
            db_ada = db_ada + dmod
            dg1 = dg1 + q1 * (1.0 + mod[M_SC1:M_SC1 + 1])
            dg2 = dg2 + q2 * (1.0 + mod[M_SC2:M_SC2 + 1])
            dgf = dgf + gb[0:1]
            loss_lanes = loss_lanes + gb[2:3]
            mixer = mixer + gb[16:32]
        d_a_param = mixer[7:8] * _sigmoid_tail(vec_ref[V_A_PARAM:V_A_PARAM + 1, :])
        grads = [dg1, dg2, mixer[4:5], mixer[5:6], mixer[6:7], d_a_param, mixer[8:9], mixer[9:10], dgf,
                 db_ada, mixer[0:4]]
        for k in range(n):
            outs[k][...] = grads[k]
            outs[n + k][...], outs[2 * n + k][...], outs[3 * n + k][...] = _adam(
                w_refs[k][...], grads[k], m_refs[k][...], v_refs[k][...])
        outs[4 * n + 1][...] = jnp.broadcast_to(jnp.sum(loss_lanes, axis=1, keepdims=True), (8, 128))

    shapes = [jax.ShapeDtypeStruct(w.shape, F32) for w in ws]
    return pl.pallas_call(
        body, name="small_finish",
        out_shape=shapes * 4 + [jax.ShapeDtypeStruct((N_DEV, 6, D), F32), jax.ShapeDtypeStruct((8, 128), F32)],
    )(gathered, mod_all, vecs, *ws, *ms, *vs)


def _pad_rows(a, rows):
    return jnp.pad(a, ((0, rows - a.shape[0]), (0, 0)))


def kernel(x, c, norm_mix_g, norm_mlp_g, w_ada, b_ada, w_in, conv_w, conv_b, w_rg_a, b_rg_a, w_rg_x, b_rg_x, a_param, w_branch_a, w_pool, b_pool, pool_scale, w_branch_b, w_out, w_up, w_down, final_g, loss_target, m_norm_mix_g, m_norm_mlp_g, m_w_ada, m_b_ada, m_w_in, m_conv_w, m_conv_b, m_w_rg_a, m_b_rg_a, m_w_rg_x, m_b_rg_x, m_a_param, m_w_branch_a, m_w_pool, m_b_pool, m_pool_scale, m_w_branch_b, m_w_out, m_w_up, m_w_down, m_final_g, v_norm_mix_g, v_norm_mlp_g, v_w_ada, v_b_ada, v_w_in, v_conv_w, v_conv_b, v_w_rg_a, v_b_rg_a, v_w_rg_x, v_b_rg_x, v_a_param, v_w_branch_a, v_w_pool, v_b_pool, v_pool_scale, v_w_branch_b, v_w_out, v_w_up, v_w_down, v_final_g):
    me = 4 * lax.axis_index("x") + 2 * lax.axis_index("y") + lax.axis_index("c")
    s = x.shape[1]
    x2d = x.reshape(s, D)
    target = loss_target.reshape(s, D)
    n_ada = w_ada.shape[2]

    sharded = dict(w_in=(w_in[0], 1), w_up=(w_up[0], 1), w_down=(w_down[0], 0), w_branch_a=(w_branch_a[0], 0),
                   w_branch_b=(w_branch_b[0], 0), w_out=(w_out[0], 0), w_rg_a=(w_rg_a[0], 1), w_rg_x=(w_rg_x[0], 1),
                   w_pool=(w_pool[0], 1))
    kind = {k: v[1] for k, v in sharded.items()}
    shard = dict(zip(sharded, _to_bf16([v[0] for v in sharded.values()])))

    conv_w_full, c_rows = _all_gather([_pad_rows(conv_w[0], 8), _pad_rows(c, 8)], [1, 0], "gather_c")
    c_all = c_rows.reshape(N_DEV, 8, D)[:, 0, :]
    b_ada_cols = lax.dynamic_slice(b_ada, (0, me * n_ada), (1, n_ada))
    mod_part = _ada_fwd(c_all, w_ada[0], b_ada_cols)
    mod_parts, = _all_gather([mod_part], [0], "gather_mod")

    first_names = ["w_in", "w_rg_a", "w_rg_x", "w_pool"]
    branch_names = ["w_branch_a", "w_branch_b", "w_out"]
    mlp_names = ["w_up", "w_down"]

    def gather(group, after, name):
        return _Gather([shard[k] for k in group], [kind[k] for k in group], after, name)

    g_first = gather(first_names, mod_parts, "gather_first")
    g_branch = gather(branch_names, g_first.token, "gather_branch")
    g_mlp = gather(mlp_names, g_branch.token, "gather_mlp")

    mod_all = jnp.transpose(mod_parts.reshape(N_DEV, N_DEV, n_ada), (1, 0, 2)).reshape(N_DEV, 6, D)
    mod_all = jnp.pad(mod_all, ((0, 0), (0, 2), (0, 0)))
    modr = lax.dynamic_index_in_dim(mod_all, me, 0, keepdims=False)
    vecs = _pack_vecs(conv_w_full, [conv_b, b_rg_a, b_rg_x, a_param, b_pool, pool_scale,
                                    norm_mix_g, norm_mlp_g, final_g.reshape(1, D)])
    g_first.forward(g_mlp.token)
    wg = dict(zip(first_names, g_first.finish(g_first.token)))

    h1, x_rnn, u_pool, ga, dga, sa, sb = _proj_fwd(x2d, modr, vecs, wg["w_in"])
    g_branch.forward(h1)
    xr, hr, za, p, pooled, *gates = _mix_fwd(x_rnn, u_pool, ga, vecs, wg["w_rg_a"], wg["w_rg_x"], wg["w_pool"],
                                             dep=g_branch.token)
    g_mlp.forward(za)
    wg.update(zip(branch_names, g_branch.finish(g_mlp.token)))
    ba, bb, merged, o, x2, h2 = _branch_fwd(za, pooled, sa, sb, x2d, modr, vecs,
                                            wg["w_branch_a"], wg["w_branch_b"], wg["w_out"])
    wg.update(zip(mlp_names, g_mlp.finish(h2)))
    ru, dx3, d_dn, small_f = _mlp_fwd(h2, x2, target, modr, vecs, wg["w_up"], wg["w_down"])

    near = _near(lax.axis_index("x"), lax.axis_index("y"), lax.axis_index("c"))
    own4 = jnp.stack([me, near[1][1], near[2][1], near[3][1]]).astype(jnp.int32)

    def scatter(group, partial, after, name):
        return _Scatter([partial[k] for k in group], [kind[k] for k in group], after, name)

    dup, dx2, do, small_m = _mlp_bwd(d_dn, ru, x2, dx3, o, modr, vecs, wg["w_up"], wg["w_down"])
    partial = dict(w_up=_wgrad(h2, dup, "wgrad_up"), w_down=_wgrad(ru, d_dn, "wgrad_down", square_a=True))
    s_mlp = scatter(mlp_names, partial, dx2, "scatter_mlp")

    dba, dbb, dgates, dza, dpooled = _branch_bwd(do, sa, sb, ba, bb, wg["w_branch_a"], wg["w_branch_b"], wg["w_out"],
                                                 dep=s_mlp.token)
    s_mlp.combine_and_send(own4, dza)
    dproj, dw_rg_a, dw_rg_x, dw_pool, small_x = _mix_bwd(dza, dpooled, x_rnn, ga, dga, xr, hr, p, gates, dgates,
                                                         vecs, wg["w_rg_a"], wg["w_rg_x"], wg["w_pool"],
                                                         dep=s_mlp.token)
    partial.update(w_branch_a=_wgrad(za, dba, "wgrad_branch_a"), w_branch_b=_wgrad(pooled, dbb, "wgrad_branch_b"),
                   w_out=_wgrad(merged, do, "wgrad_out"),
                   w_rg_a=dw_rg_a, w_rg_x=dw_rg_x, w_pool=dw_pool)
    mixer_names = ["w_rg_a", "w_rg_x", "w_pool", "w_branch_a", "w_branch_b", "w_out"]
    s_mixer = scatter(mixer_names, partial, s_mlp.token, "scatter_mixer")

    partial["w_in"] = _wgrad(h1, dproj, "wgrad_in", dep=s_mixer.token)
    s_in = scatter(["w_in"], partial, s_mixer.token, "scatter_in")
    s_mixer.combine_and_send(own4, s_in.token)
    s_in.combine_and_send(own4, s_mixer.token)
    grad_x, small_p = _proj_bwd(dproj, x2d, dx2, modr, vecs, wg["w_in"], dep=s_in.token)

    locals_ = dict(w_in=(w_in, m_w_in, v_w_in), w_up=(w_up, m_w_up, v_w_up), w_down=(w_down, m_w_down, v_w_down),
                   w_branch_a=(w_branch_a, m_w_branch_a, v_w_branch_a),
                   w_branch_b=(w_branch_b, m_w_branch_b, v_w_branch_b), w_out=(w_out, m_w_out, v_w_out),
                   w_rg_a=(w_rg_a, m_w_rg_a, v_w_rg_a), w_rg_x=(w_rg_x, m_w_rg_x, v_w_rg_x),
                   w_pool=(w_pool, m_w_pool, v_w_pool))
    res = {}

    def finish(group, exchange, after, n_tiles, name):
        chip_sums, arrivals = exchange.finish(after)
        flat = lambda t: t.reshape(-1, t.shape[-1])
        shapes = [flat(locals_[k][0]).shape for k in group]
        outs = _adam_group([cs.reshape(N_NEAR, *sh) for cs, sh in zip(chip_sums, shapes)],
                           [ar.reshape(N_NEAR - 1, *sh) for ar, sh in zip(arrivals, shapes)],
                           *[[flat(locals_[k][j]) for k in group] for j in range(3)], n_tiles, name)
        for i, k in enumerate(group):
            res[k] = [outs[j * len(group) + i].reshape(locals_[k][0].shape) for j in range(4)]
        return res[group[-1]][0]

    small = jnp.concatenate([small_f, small_m, small_x, small_p], axis=0)
    g_small = _Gather([small], [0], grad_x, "gather_small")
    done = finish(mlp_names, s_mlp, g_small.token, 4, "adam_mlp")
    done = finish(mixer_names, s_mixer, done, 2, "adam_mixer")
    g_small.forward(done)
    done = finish(["w_in"], s_in, g_small.token, 4, "adam_in")
    small_all, = g_small.finish(done)
    small_all = small_all.reshape(N_DEV, N_SMALL, D)

    def embed(cw):
        return lax.dynamic_update_slice(jnp.zeros((4, D), F32), cw[0], (0, me * (D // N_DEV)))

    def smalls(ng, nl, cb, bra, brx, ap, bp, ps, fg, ba_, cw):
        return [ng, nl, cb, bra, brx, ap, bp, ps, fg.reshape(1, D), ba_.reshape(6, D), embed(cw)]

    small_names = ["norm_mix_g", "norm_mlp_g", "conv_b", "b_rg_a", "b_rg_x", "a_param", "b_pool", "pool_scale",
                   "final_g", "b_ada", "conv_w"]
    fin = _small_finish(
        small_all, mod_all, vecs,
        smalls(norm_mix_g, norm_mlp_g, conv_b, b_rg_a, b_rg_x, a_param, b_pool, pool_scale, final_g, b_ada, conv_w),
        smalls(m_norm_mix_g, m_norm_mlp_g, m_conv_b, m_b_rg_a, m_b_rg_x, m_a_param, m_b_pool, m_pool_scale,
               m_final_g, m_b_ada, m_conv_w),
        smalls(v_norm_mix_g, v_norm_mlp_g, v_conv_b, v_b_rg_a, v_b_rg_x, v_a_param, v_b_pool, v_pool_scale,
               v_final_g, v_b_ada, v_conv_w))
    dmod_all, loss_tile = fin[4 * N_SMALL_PARAMS], fin[4 * N_SMALL_PARAMS + 1]
    dmod_cols = lax.dynamic_slice(dmod_all.reshape(N_DEV, 6 * D), (0, me * n_ada), (N_DEV, n_ada))
    res["w_ada"] = [t.reshape(w_ada.shape) for t in _ada_bwd_adam(c_all, dmod_cols, w_ada[0], m_w_ada[0], v_w_ada[0])]

    def final_shape(k, t):
        if k == "final_g":
            return t.reshape(D)
        if k == "b_ada":
            return t.reshape(1, 6 * D)
        if k == "conv_w":
            return lax.dynamic_slice(t, (0, me * (D // N_DEV)), (4, D // N_DEV)).reshape(conv_w.shape)
        return t

    for i, k in enumerate(small_names):
        res[k] = [final_shape(k, fin[which * N_SMALL_PARAMS + i]) for which in range(4)]
    order = ["norm_mix_g", "norm_mlp_g", "w_ada", "b_ada", "w_in", "conv_w", "conv_b", "w_rg_a", "b_rg_a", "w_rg_x",
             "b_rg_x", "a_param", "w_branch_a", "w_pool", "b_pool", "pool_scale", "w_branch_b", "w_out", "w_up",
             "w_down", "final_g"]
    outs = [loss_tile[0, 0], grad_x.reshape(x.shape)]
    for which in range(4):
        for k in order:
            outs.append(res[k][which])
    return tuple(outs)
```

```python
import functools

import jax
import jax.numpy as jnp
from jax import lax
from jax.experimental import pallas as pl
from jax.experimental.pallas import tpu as pltpu

F32 = jnp.float32
BF16 = jnp.bfloat16
MESH = pl.DeviceIdType.MESH

N_DEV = 8
D = 1024
N_GROUPS = 4
GW = D // N_GROUPS
D_IN = 5 * D
D_FF = 4 * D
POOL_WINDOWS = (2, 4, 8, 16)
HALO_X = 8
HALO_U = 16
EPS = 1e-6
C_RG = 8.0
ADAM_LR, ADAM_B1, ADAM_B2, ADAM_EPS, ADAM_WD, ADAM_STEP = 0.001, 0.9, 0.999, 1e-08, 0.01, 10

V7X_VMEM_LIMIT = 56 * 1024 * 1024

V_CONV_W, V_CONV_B, V_B_RG_A, V_B_RG_X, V_A_PARAM, V_B_POOL, V_POOL_SCALE, V_G1, V_G2, V_GF = 0, 4, 5, 6, 7, 8, 9, 10, 11, 12
M_SH1, M_SC1, M_GT1, M_SH2, M_SC2, M_GT2 = 0, 1, 2, 3, 4, 5

TM_PROJ = 512
TM_MIX = 256
TM_BRANCH = 512
TM_MLP = 512
TM_MLP_BWD = 256
TS_WGRAD = 1024


def _params(semantics):
    return pltpu.CompilerParams(dimension_semantics=semantics, vmem_limit_bytes=V7X_VMEM_LIMIT)


def _resident(shape):
    return pl.BlockSpec(shape, lambda *_: (0,) * len(shape), pipeline_mode=pl.Buffered(1))


def _dot(a, b):
    return jnp.dot(a, b, preferred_element_type=F32)


def _dot_nt(a, b):
    return lax.dot_general(a, b, (((1,), (1,)), ((), ())), preferred_element_type=F32)


def _dot_tn(a, b):
    return lax.dot_general(a, b, (((0,), (0,)), ((), ())), preferred_element_type=F32)


def _sigmoid(x):
    return 0.5 * jnp.tanh(0.5 * x) + 0.5


def _sigmoid_tail(x):
    return 1.0 / (1.0 + jnp.exp(-x))


def _gelu_and_grad(x):
    k = 0.7978845608028654
    x2 = x * x
    t = jnp.tanh(k * (x + 0.044715 * x * x2))
    g = 0.5 * x * (1.0 + t)
    dg = 0.5 * (1.0 + t) + 0.5 * x * (1.0 - t * t) * (k * (1.0 + 3.0 * 0.044715 * x2))
    return g, dg


def _softplus(a):
    e = jnp.exp(-jnp.abs(a))
    u = 1.0 + e
    log1p_e = jnp.where(u == 1.0, e, jnp.log(u) * e / jnp.where(u == 1.0, 1.0, u - 1.0))
    return jnp.maximum(a, 0.0) + log1p_e


def _neg_expm1(z):
    series = -(z * (1.0 + z * (0.5 + z * (1.0 / 6.0 + z * (1.0 / 24.0 + z * (1.0 / 120.0))))))
    return jnp.where(z > -0.1, series, 1.0 - jnp.exp(z))


def _shift_down(x, k):
    return pltpu.roll(x, k, 0)


def _shift_up(x, k):
    return pltpu.roll(x, x.shape[0] - k, 0)


def _rglru_gates(xr, w_a, w_x, b_a, b_x, a_param, is_t0):
    xb = xr.astype(BF16)
    ra = _sigmoid(_dot(xb, w_a) + b_a)
    ri = _sigmoid(_dot(xb, w_x) + b_x)
    sp = _softplus(a_param)
    log_a = (-C_RG) * ra * sp
    a = jnp.exp(log_a)
    mult = jnp.where(is_t0, 1.0, jnp.sqrt(_neg_expm1(2.0 * log_a)))
    return ra, ri, sp, a, mult


SUBLANES = 8


LANES = 128


def _scan_strip(a, b, carry, scr, down):
    t = b.shape[0]
    g = t // SUBLANES
    a3 = a.reshape(g, SUBLANES, LANES)
    b3 = b.reshape(g, SUBLANES, LANES)
    sub = lax.broadcasted_iota(jnp.int32, (g, SUBLANES, LANES), 1)
    for k in (1, 2, 4):
        keep = sub >= k if down else sub < SUBLANES - k
        shift = k if down else SUBLANES - k
        b3 = b3 + a3 * jnp.where(keep, pltpu.roll(b3, shift, 1), 0.0)
        a3 = a3 * jnp.where(keep, pltpu.roll(a3, shift, 1), 1.0)
    scr[0] = a3.reshape(t, LANES)
    scr[1] = b3.reshape(t, LANES)
    end_row = SUBLANES - 1 if down else 0
    ag = scr[0, pl.ds(end_row, g, stride=SUBLANES), :]
    bg = scr[1, pl.ds(end_row, g, stride=SUBLANES), :]
    rg = lax.broadcasted_iota(jnp.int32, (g, LANES), 0)
    edge = 0 if down else g - 1
    bg = bg + jnp.where(rg == edge, ag * carry, 0.0)
    k = 1
    while k < g:
        keep = rg >= k if down else rg < g - k
        shift = k if down else g - k
        bg = bg + ag * jnp.where(keep, pltpu.roll(bg, shift, 0), 0.0)
        if 2 * k < g:
            ag = ag * pltpu.roll(ag, shift, 0)
        k *= 2
    entering = jnp.where(rg != edge, pltpu.roll(bg, 1 if down else g - 1, 0), carry)
    for r in range(SUBLANES):
        scr[2, pl.ds(r, g, stride=SUBLANES), :] = entering
    return scr[1] + scr[0] * scr[2], bg[g - 1:g, :]


def _scan_strips(a, b, carry, scr, down):
    outs = [_scan_strip(a[:, c:c + LANES], b[:, c:c + LANES], carry[:, c:c + LANES], scr, down)
            for c in range(0, b.shape[1], LANES)]
    return jnp.concatenate([o[0] for o in outs], axis=1), jnp.concatenate([o[1] for o in outs], axis=1)


def _scan_down(a, b, carry, scr):
    return _scan_strips(a, b, carry, scr, True)


def _scan_up(m, b, carry, scr):
    return _scan_strips(m, b, carry, scr, False)[0]


def _window_mean(sums, window, first_block, head_t):
    scaled = sums * (1.0 / window)
    head = jnp.where(first_block, sums[:HALO_U] / jnp.minimum(head_t, float(window)), scaled[:HALO_U])
    return jnp.concatenate([head, scaled[HALO_U:]], axis=0)


def _conv_taps(x_ext):
    return [_shift_down(x_ext, 3 - j)[HALO_X:] if j < 3 else x_ext[HALO_X:] for j in range(4)]


def _proj_fwd(x, modr, vecs, w_in):
    s = x.shape[0]
    tm = min(TM_PROJ, s)

    def body(x_ref, mod_ref, vec_ref, w_ref, h1_ref, xrnn_ref, u_ref, ga_ref, dga_ref, sa_ref, sb_ref):
        xv = x_ref[...]
        r = lax.rsqrt(jnp.mean(xv * xv, axis=-1, keepdims=True) + EPS)
        gain = vec_ref[V_G1:V_G1 + 1, :] * (1.0 + mod_ref[M_SC1:M_SC1 + 1, :])
        h = (xv * r * gain + mod_ref[M_SH1:M_SH1 + 1, :]).astype(BF16)
        h1_ref[...] = h
        xrnn_ref[...] = _dot(h, w_ref[:, 0:D])
        ga_ref[...], dga_ref[...] = _gelu_and_grad(_dot(h, w_ref[:, D:2 * D]))
        u_ref[...] = _dot(h, w_ref[:, 2 * D:3 * D])
        sa_ref[...] = _sigmoid(_dot(h, w_ref[:, 3 * D:4 * D]))
        sb_ref[...] = _sigmoid(_dot(h, w_ref[:, 4 * D:5 * D]))

    tok = pl.BlockSpec((tm, D), lambda i: (i, 0))
    sd = lambda dt: jax.ShapeDtypeStruct((s, D), dt)
    return pl.pallas_call(
        body, name="proj_fwd", grid=(s // tm,),
        in_specs=[tok, pl.BlockSpec((8, D), lambda i: (0, 0)), pl.BlockSpec((16, D), lambda i: (0, 0)),
                  _resident((D, D_IN))],
        out_specs=[tok] * 7,
        out_shape=[sd(BF16)] + [sd(F32)] * 6,
        compiler_params=_params(("parallel",)),
    )(x, modr, vecs, w_in)


def _mix_fwd(x_rnn, u_pool, ga, vecs, w_rg_a, w_rg_x, w_pool, dep):
    s = x_rnn.shape[0]
    tm = min(TM_MIX, s)
    nb = s // tm

    def body(xh_ref, x_ref, uh_ref, u_ref, ga_ref, vec_ref, wa_ref, wx_ref, wp_ref, dep_ref,
             xr_ref, hr_ref, za_ref, p_ref, pooled_ref, a_ref, mult_ref, ra_ref, ri_ref, carry_ref, scan_scr):
        i = pl.program_id(0)
        first = i == 0

        @pl.when(first)
        def _():
            carry_ref[...] = jnp.zeros_like(carry_ref)

        row = lax.broadcasted_iota(jnp.int32, (tm, GW), 0)
        is_t0 = jnp.logical_and(first, row == 0)
        head_t = (lax.broadcasted_iota(jnp.int32, (HALO_U, GW), 0) + 1).astype(F32)
        for g in range(N_GROUPS):
            cs = slice(g * GW, (g + 1) * GW)
            vec = vec_ref[:, cs]
            xh = jnp.where(first, 0.0, xh_ref[:, cs])
            taps = _conv_taps(jnp.concatenate([xh, x_ref[:, cs]], axis=0))
            xr = vec[V_CONV_B:V_CONV_B + 1]
            for j in range(4):
                xr = xr + vec[V_CONV_W + j:V_CONV_W + j + 1] * taps[j]
            xr_ref[:, cs] = xr
            ra, ri, _, a, mult = _rglru_gates(
                xr, wa_ref[g], wx_ref[g], vec[V_B_RG_A:V_B_RG_A + 1], vec[V_B_RG_X:V_B_RG_X + 1],
                vec[V_A_PARAM:V_A_PARAM + 1], is_t0)
            a_ref[:, cs] = a
            mult_ref[:, cs] = mult
            ra_ref[:, cs] = ra.astype(BF16)
            ri_ref[:, cs] = ri.astype(BF16)
            h, last = _scan_down(a, xr * ri * mult, carry_ref[0:1, cs], scan_scr)
            hr_ref[:, cs] = h
            carry_ref[0:1, cs] = last
            za_ref[:, cs] = (ga_ref[:, cs] * h).astype(BF16)
            uh = jnp.where(first, 0.0, uh_ref[:, cs])
            sm = jnp.concatenate([uh, u_ref[:, cs]], axis=0)
            k = 1
            while k < POOL_WINDOWS[g]:
                sm = sm + _shift_down(sm, k)
                k *= 2
            mean = _window_mean(sm[HALO_U:], POOL_WINDOWS[g], first, head_t)
            p = (mean - u_ref[:, cs]).astype(BF16)
            p_ref[:, cs] = p
            pb = _dot(p, wp_ref[g]) + vec[V_B_POOL:V_B_POOL + 1]
            pooled_ref[:, cs] = (pb * vec[V_POOL_SCALE:V_POOL_SCALE + 1]).astype(BF16)

    tok = pl.BlockSpec((tm, D), lambda i: (i, 0))
    halo = lambda rows: pl.BlockSpec((rows, D), lambda i: (jnp.maximum(i * (tm // rows) - 1, 0), 0))
    wspec = pl.BlockSpec((N_GROUPS, GW, GW), lambda i: (0, 0, 0))
    sd = lambda dt: jax.ShapeDtypeStruct((s, D), dt)
    return pl.pallas_call(
        body, name="mix_fwd", grid=(nb,),
        in_specs=[halo(HALO_X), tok, halo(HALO_U), tok, tok, pl.BlockSpec((16, D), lambda i: (0, 0)),
                  wspec, wspec, wspec, pl.BlockSpec(memory_space=pl.ANY)],
        out_specs=[tok] * 9,
        out_shape=[sd(F32), sd(F32), sd(BF16), sd(BF16), sd(BF16), sd(F32), sd(F32), sd(BF16), sd(BF16)],
        scratch_shapes=[pltpu.VMEM((8, D), F32), pltpu.VMEM((3, tm, LANES), F32)],
        compiler_params=_params(("arbitrary",)),
    )(x_rnn, x_rnn, u_pool, u_pool, ga, vecs, w_rg_a, w_rg_x, w_pool, dep)


def _branch_fwd(za, pooled, sa, sb, x, modr, vecs, w_a, w_b, w_out):
    s = x.shape[0]
    tm = min(TM_BRANCH, s)

    def body(za_ref, pooled_ref, sa_ref, sb_ref, x_ref, mod_ref, vec_ref, wa_ref, wb_ref, wo_ref,
             ba_ref, bb_ref, merged_ref, o_ref, x2_ref, h2_ref):
        ba = _dot(za_ref[...], wa_ref[...])
        bb = _dot(pooled_ref[...], wb_ref[...])
        ba_ref[...] = ba.astype(BF16)
        bb_ref[...] = bb.astype(BF16)
        merged = (sa_ref[...] * ba + sb_ref[...] * bb).astype(BF16)
        merged_ref[...] = merged
        o = _dot(merged, wo_ref[...])
        o_ref[...] = o.astype(BF16)
        x2 = x_ref[...] + mod_ref[M_GT1:M_GT1 + 1, :] * o
        x2_ref[...] = x2
        r = lax.rsqrt(jnp.mean(x2 * x2, axis=-1, keepdims=True) + EPS)
        gain = vec_ref[V_G2:V_G2 + 1, :] * (1.0 + mod_ref[M_SC2:M_SC2 + 1, :])
        h2_ref[...] = (x2 * r * gain + mod_ref[M_SH2:M_SH2 + 1, :]).astype(BF16)

    tok = pl.BlockSpec((tm, D), lambda i: (i, 0))
    wspec = pl.BlockSpec((D, D), lambda i: (0, 0))
    sd = lambda dt: jax.ShapeDtypeStruct((s, D), dt)
    return pl.pallas_call(
        body, name="branch_fwd", grid=(s // tm,),
        in_specs=[tok, tok, tok, tok,
                  tok, pl.BlockSpec((8, D), lambda i: (0, 0)), pl.BlockSpec((16, D), lambda i: (0, 0)),
                  wspec, wspec, wspec],
        out_specs=[tok] * 6,
        out_shape=[sd(BF16), sd(BF16), sd(BF16), sd(BF16), sd(F32), sd(BF16)],
        compiler_params=_params(("parallel",)),
    )(za, pooled, sa, sb, x, modr, vecs, w_a, w_b, w_out)


def _mlp_fwd(h2, x2, target, modr, vecs, w_up, w_down):
    s = x2.shape[0]
    tm = min(TM_MLP, s)

    def body(h2_ref, x2_ref, tgt_ref, mod_ref, vec_ref, wu_ref, wd_ref,
             ru_ref, dx3_ref, ddn_ref, small_ref):
        @pl.when(pl.program_id(0) == 0)
        def _():
            small_ref[...] = jnp.zeros_like(small_ref)

        h2 = h2_ref[...]
        dn = None
        for c in range(D_FF // D):
            cs = slice(c * D, (c + 1) * D)
            ru = jnp.maximum(_dot(h2, wu_ref[:, cs]), 0.0)
            ru_ref[:, cs] = ru.astype(BF16)
            part = _dot((ru * ru).astype(BF16), wd_ref[cs, :])
            dn = part if dn is None else dn + part
        gt2 = mod_ref[M_GT2:M_GT2 + 1, :]
        gf = vec_ref[V_GF:V_GF + 1, :]
        x3 = x2_ref[...] + gt2 * dn
        r3 = lax.rsqrt(jnp.mean(x3 * x3, axis=-1, keepdims=True) + EPS)
        n3 = x3 * r3
        err = n3 * gf - tgt_ref[...]
        dy = err * (1.0 / D)
        dn3 = dy * gf
        dx3 = r3 * (dn3 - n3 * jnp.mean(dn3 * n3, axis=-1, keepdims=True))
        dx3_ref[...] = dx3
        ddn_ref[...] = (dx3 * gt2).astype(BF16)
        small_ref[0:1, :] += jnp.sum(dy * n3, axis=0, keepdims=True)
        small_ref[1:2, :] += jnp.sum(dx3 * dn, axis=0, keepdims=True)
        small_ref[2:3, :] += (0.5 / D) * jnp.sum(err * err, axis=0, keepdims=True)

    tok = pl.BlockSpec((tm, D), lambda i: (i, 0))
    return pl.pallas_call(
        body, name="mlp_fwd", grid=(s // tm,),
        in_specs=[tok, tok, tok,
                  pl.BlockSpec((8, D), lambda i: (0, 0)), pl.BlockSpec((16, D), lambda i: (0, 0)),
                  _resident((D, D_FF)), _resident((D_FF, D))],
        out_specs=[pl.BlockSpec((tm, D_FF), lambda i: (i, 0)), tok, tok,
                   pl.BlockSpec((8, D), lambda i: (0, 0))],
        out_shape=[jax.ShapeDtypeStruct((s, D_FF), BF16), jax.ShapeDtypeStruct((s, D), F32),
                   jax.ShapeDtypeStruct((s, D), BF16), jax.ShapeDtypeStruct((8, D), F32)],
        compiler_params=_params(("arbitrary",)),
    )(h2, x2, target, modr, vecs, w_up, w_down)


def _mlp_bwd(d_dn, ru, x2, dx3, o, modr, vecs, w_up, w_down):
    s = x2.shape[0]
    tm = min(TM_MLP_BWD, s)

    def body(ddn_ref, ru_ref, x2_ref, dx3_ref, o_ref, mod_ref, vec_ref, wu_ref, wd_ref,
             dup_ref, dx2_ref, do_ref, small_ref):
        @pl.when(pl.program_id(0) == 0)
        def _():
            small_ref[...] = jnp.zeros_like(small_ref)

        ddn = ddn_ref[...]
        dh2 = None
        for c in range(D_FF // D):
            cs = slice(c * D, (c + 1) * D)
            dff = _dot_nt(ddn, wd_ref[cs, :])
            dup = (dff * (2.0 * ru_ref[:, cs].astype(F32))).astype(BF16)
            dup_ref[:, cs] = dup
            part = _dot_nt(dup, wu_ref[:, cs])
            dh2 = part if dh2 is None else dh2 + part
        x2 = x2_ref[...]
        r2 = lax.rsqrt(jnp.mean(x2 * x2, axis=-1, keepdims=True) + EPS)
        xn2 = x2 * r2
        gain = vec_ref[V_G2:V_G2 + 1, :] * (1.0 + mod_ref[M_SC2:M_SC2 + 1, :])
        dxn2 = dh2 * gain
        dx2 = dx3_ref[...] + r2 * (dxn2 - xn2 * jnp.mean(dxn2 * xn2, axis=-1, keepdims=True))
        dx2_ref[...] = dx2
        do_ref[...] = (dx2 * mod_ref[M_GT1:M_GT1 + 1, :]).astype(BF16)
        small_ref[0:1, :] += jnp.sum(dh2, axis=0, keepdims=True)
        small_ref[1:2, :] += jnp.sum(dh2 * xn2, axis=0, keepdims=True)
        small_ref[2:3, :] += jnp.sum(dx2 * o_ref[...].astype(F32), axis=0, keepdims=True)

    tok = pl.BlockSpec((tm, D), lambda i: (i, 0))
    wide = pl.BlockSpec((tm, D_FF), lambda i: (i, 0))
    return pl.pallas_call(
        body, name="mlp_bwd", grid=(s // tm,),
        in_specs=[tok, wide, tok, tok, tok,
                  pl.BlockSpec((8, D), lambda i: (0, 0)), pl.BlockSpec((16, D), lambda i: (0, 0)),
                  _resident((D, D_FF)), _resident((D_FF, D))],
        out_specs=[wide, tok, tok, pl.BlockSpec((8, D), lambda i: (0, 0))],
        out_shape=[jax.ShapeDtypeStruct((s, D_FF), BF16), jax.ShapeDtypeStruct((s, D), F32),
                   jax.ShapeDtypeStruct((s, D), BF16), jax.ShapeDtypeStruct((8, D), F32)],
        compiler_params=_params(("arbitrary",)),
    )(d_dn, ru, x2, dx3, o, modr, vecs, w_up, w_down)


def _branch_bwd(do, sa, sb, ba, bb, w_a, w_b, w_out, dep):
    s = do.shape[0]
    tm = min(TM_BRANCH, s)

    def body(do_ref, sa_ref, sb_ref, ba_ref, bb_ref, wa_ref, wb_ref, wo_ref, dep_ref,
             dba_ref, dbb_ref, dg_ref, dza_ref, dpooled_ref):
        dmerged = _dot_nt(do_ref[...], wo_ref[...])
        sa = sa_ref[...]
        sb = sb_ref[...]
        dba = (dmerged * sa).astype(BF16)
        dbb = (dmerged * sb).astype(BF16)
        dba_ref[...] = dba
        dbb_ref[...] = dbb
        dg_ref[:, :D] = (dmerged * ba_ref[...].astype(F32) * sa * (1.0 - sa)).astype(BF16)
        dg_ref[:, D:] = (dmerged * bb_ref[...].astype(F32) * sb * (1.0 - sb)).astype(BF16)
        dza_ref[...] = _dot_nt(dba, wa_ref[...])
        dpooled_ref[...] = _dot_nt(dbb, wb_ref[...])

    tok = pl.BlockSpec((tm, D), lambda i: (i, 0))
    wspec = pl.BlockSpec((D, D), lambda i: (0, 0))
    sd = lambda dt: jax.ShapeDtypeStruct((s, D), dt)
    return pl.pallas_call(
        body, name="branch_bwd", grid=(s // tm,),
        in_specs=[tok, tok, tok, tok, tok, wspec, wspec, wspec, pl.BlockSpec(memory_space=pl.ANY)],
        out_specs=[tok, tok, pl.BlockSpec((tm, 2 * D), lambda i: (i, 0)), tok, tok],
        out_shape=[sd(BF16), sd(BF16), jax.ShapeDtypeStruct((s, 2 * D), BF16), sd(F32), sd(F32)],
        compiler_params=_params(("parallel",)),
    )(do, sa, sb, ba, bb, w_a, w_b, w_out, dep)


def _mix_bwd(dza, dpooled, x_rnn, ga, dga, xr, hr, p, gates, dgates, vecs, w_rg_a, w_rg_x, w_pool, dep):
    s = xr.shape[0]
    tm = min(TM_MIX, s)
    nb = s // tm

    def body(dza_ref, dpooled_ref, xh_ref, x_ref, ga_ref, dga_ref, xr_ref, hh_ref, hr_ref, p_ref,
             a_ref, mult_ref, ra_ref, ri_ref, dg_ref, vec_ref, wa_ref, wx_ref, wp_ref, dep_ref,
             dproj_ref, dwa_ref, dwx_ref, dwp_ref, small_ref,
             scan_carry, dxr_carry, q_carry, scan_scr, dwa_acc, dwx_acc, dwp_acc):
        i = pl.program_id(0)
        bi = nb - 1 - i
        first_t = bi == 0

        @pl.when(i == 0)
        def _():
            scan_carry[...] = jnp.zeros_like(scan_carry)
            dxr_carry[...] = jnp.zeros_like(dxr_carry)
            q_carry[...] = jnp.zeros_like(q_carry)
            dwa_acc[...] = jnp.zeros_like(dwa_acc)
            dwx_acc[...] = jnp.zeros_like(dwx_acc)
            dwp_acc[...] = jnp.zeros_like(dwp_acc)
            small_ref[...] = jnp.zeros_like(small_ref)

        row = lax.broadcasted_iota(jnp.int32, (tm, GW), 0)
        is_t0 = jnp.logical_and(first_t, row == 0)
        head_t = (lax.broadcasted_iota(jnp.int32, (HALO_U, GW), 0) + 1).astype(F32)
        colsum = lambda v: jnp.sum(v, axis=0, keepdims=True)
        for g in range(N_GROUPS):
            cs = slice(g * GW, (g + 1) * GW)
            vec = vec_ref[:, cs]
            xr = xr_ref[:, cs]
            hr = hr_ref[:, cs]
            dza = dza_ref[:, cs]
            dproj_ref[:, D + g * GW:D + (g + 1) * GW] = (dza * hr * dga_ref[:, cs]).astype(BF16)
            dhr = dza * ga_ref[:, cs]
            a = a_ref[:, cs]
            mult = mult_ref[:, cs]
            ra = ra_ref[:, cs].astype(F32)
            ri = ri_ref[:, cs].astype(F32)
            sp = _softplus(vec[V_A_PARAM:V_A_PARAM + 1])
            m = jnp.where(row == tm - 1, 1.0, _shift_up(a, 1))
            gsum = _scan_up(m, dhr, scan_carry[0:1, cs], scan_scr)
            scan_carry[0:1, cs] = a[0:1, :] * gsum[0:1, :]
            hh = jnp.where(first_t, 0.0, hh_ref[:, cs])
            hprev = _shift_down(jnp.concatenate([hh, hr], axis=0), 1)[8:]
            da = gsum * hprev
            dmult = jnp.where(is_t0, 0.0, gsum * xr * ri)
            dlog_a = da * a - dmult * a * a / mult
            dri = gsum * xr * mult
            dxr = gsum * ri * mult
            small_ref[7:8, cs] += colsum((-C_RG) * ra * dlog_a)
            dpa = (((-C_RG) * sp) * dlog_a * ra * (1.0 - ra))
            dpx = dri * ri * (1.0 - ri)
            small_ref[5:6, cs] += colsum(dpa)
            small_ref[6:7, cs] += colsum(dpx)
            dpa = dpa.astype(BF16)
            dpx = dpx.astype(BF16)
            xrb = xr.astype(BF16)
            dwa_acc[g] += _dot_tn(xrb, dpa)
            dwx_acc[g] += _dot_tn(xrb, dpx)
            dxr = dxr + _dot_nt(dpa, wa_ref[g]) + _dot_nt(dpx, wx_ref[g])
            small_ref[4:5, cs] += colsum(dxr)
            xh = jnp.where(first_t, 0.0, xh_ref[:, cs])
            taps = _conv_taps(jnp.concatenate([xh, x_ref[:, cs]], axis=0))
            dxr_ext = jnp.concatenate([dxr, dxr_carry[:, cs]], axis=0)
            dx = vec[V_CONV_W + 3:V_CONV_W + 4] * dxr
            for j in range(4):
                small_ref[j:j + 1, cs] += colsum(dxr * taps[j])
                if j < 3:
                    dx = dx + vec[V_CONV_W + j:V_CONV_W + j + 1] * _shift_up(dxr_ext, 3 - j)[:tm]
            dxr_carry[:, cs] = dxr[0:8, :]
            dproj_ref[:, cs] = dx.astype(BF16)
            pg = p_ref[:, cs]
            dpooled = dpooled_ref[:, cs]
            pb = _dot(pg, wp_ref[g]) + vec[V_B_POOL:V_B_POOL + 1]
            small_ref[9:10, cs] += colsum(dpooled * pb)
            dpb = dpooled * vec[V_POOL_SCALE:V_POOL_SCALE + 1]
            small_ref[8:9, cs] += colsum(dpb)
            dpbb = dpb.astype(BF16)
            dwp_acc[g] += _dot_tn(pg, dpbb)
            dp = _dot_nt(dpbb, wp_ref[g])
            q = _window_mean(dp, POOL_WINDOWS[g], first_t, head_t)
            sm = jnp.concatenate([q, q_carry[:, cs]], axis=0)
            k = 1
            while k < POOL_WINDOWS[g]:
                sm = sm + _shift_up(sm, k)
                k *= 2
            q_carry[:, cs] = q[0:HALO_U, :]
            dproj_ref[:, 2 * D + g * GW:2 * D + (g + 1) * GW] = (sm[:tm] - dp).astype(BF16)
        dproj_ref[:, 3 * D:] = dg_ref[...]

        @pl.when(i == nb - 1)
        def _():
            dwa_ref[...] = dwa_acc[...].astype(BF16)
            dwx_ref[...] = dwx_acc[...].astype(BF16)
            dwp_ref[...] = dwp_acc[...].astype(BF16)

    rev = lambda i: nb - 1 - i
    tok = pl.BlockSpec((tm, D), lambda i: (rev(i), 0))
    halo8 = lambda k: pl.BlockSpec((8, D), lambda i: (jnp.maximum(rev(i) * (tm // 8) - 1, 0), k))
    wspec = pl.BlockSpec((N_GROUPS, GW, GW), lambda i: (0, 0, 0))
    wshape = jax.ShapeDtypeStruct((N_GROUPS, GW, GW), BF16)
    return pl.pallas_call(
        body, name="mix_bwd", grid=(nb,),
        in_specs=[tok, tok, halo8(0), tok, tok, tok, tok, halo8(0), tok, tok, tok, tok, tok, tok,
                  pl.BlockSpec((tm, 2 * D), lambda i: (rev(i), 0)),
                  pl.BlockSpec((16, D), lambda i: (0, 0)), wspec, wspec, wspec, pl.BlockSpec(memory_space=pl.ANY)],
        out_specs=[pl.BlockSpec((tm, D_IN), lambda i: (rev(i), 0)), wspec, wspec, wspec,
                   pl.BlockSpec((16, D), lambda i: (0, 0))],
        out_shape=[jax.ShapeDtypeStruct((s, D_IN), BF16), wshape, wshape, wshape,
                   jax.ShapeDtypeStruct((16, D), F32)],
        scratch_shapes=[pltpu.VMEM((8, D), F32), pltpu.VMEM((8, D), F32), pltpu.VMEM((HALO_U, D), F32),
                        pltpu.VMEM((3, tm, LANES), F32)] + [pltpu.VMEM((N_GROUPS, GW, GW), F32)] * 3,
        compiler_params=_params(("arbitrary",)),
    )(dza, dpooled, x_rnn, x_rnn, ga, dga, xr, hr, hr, p, *gates, dgates, vecs, w_rg_a, w_rg_x, w_pool, dep)


def _proj_bwd(dproj, x, dx2, modr, vecs, w_in, dep):
    s = x.shape[0]
    tm = min(TM_PROJ, s)

    def body(dp_ref, x_ref, dx2_ref, mod_ref, vec_ref, w_ref, dep_ref, gx_ref, small_ref):
        @pl.when(pl.program_id(0) == 0)
        def _():
            small_ref[...] = jnp.zeros_like(small_ref)

        dh1 = None
        for c in range(D_IN // D):
            cs = slice(c * D, (c + 1) * D)
            part = _dot_nt(dp_ref[:, cs], w_ref[:, cs])
            dh1 = part if dh1 is None else dh1 + part
        xv = x_ref[...]
        r1 = lax.rsqrt(jnp.mean(xv * xv, axis=-1, keepdims=True) + EPS)
        xn1 = xv * r1
        gain = vec_ref[V_G1:V_G1 + 1, :] * (1.0 + mod_ref[M_SC1:M_SC1 + 1, :])
        dxn1 = dh1 * gain
        gx_ref[...] = dx2_ref[...] + r1 * (dxn1 - xn1 * jnp.mean(dxn1 * xn1, axis=-1, keepdims=True))
        small_ref[0:1, :] += jnp.sum(dh1, axis=0, keepdims=True)
        small_ref[1:2, :] += jnp.sum(dh1 * xn1, axis=0, keepdims=True)

    tok = pl.BlockSpec((tm, D), lambda i: (i, 0))
    return pl.pallas_call(
        body, name="proj_bwd", grid=(s // tm,),
        in_specs=[pl.BlockSpec((tm, D_IN), lambda i: (i, 0)), tok, tok,
                  pl.BlockSpec((8, D), lambda i: (0, 0)), pl.BlockSpec((16, D), lambda i: (0, 0)),
                  _resident((D, D_IN)), pl.BlockSpec(memory_space=pl.ANY)],
        out_specs=[tok, pl.BlockSpec((8, D), lambda i: (0, 0))],
        out_shape=[jax.ShapeDtypeStruct((s, D), F32), jax.ShapeDtypeStruct((8, D), F32)],
        compiler_params=_params(("arbitrary",)),
    )(dproj, x, dx2, modr, vecs, w_in, dep)


def _wgrad(a, b, name, square_a=False, dep=None):
    s, ka = a.shape
    n = b.shape[1]
    tka = ka if ka <= 1024 else ka // 2
    tn = n if n <= 1024 else n // 2
    ts = min(TS_WGRAD, s)
    ns = s // ts
    nc = 512
    deps = [] if dep is None else [dep]

    def body(a_ref, b_ref, *refs):
        out_ref, acc_ref = refs[-2:]
        t = pl.program_id(2)

        @pl.when(t == 0)
        def _():
            acc_ref[...] = jnp.zeros_like(acc_ref)

        av = a_ref[...]
        if square_a:
            af = av.astype(F32)
            av = (af * af).astype(BF16)
        for c in range(tn // nc):
            cs = slice(c * nc, (c + 1) * nc)
            acc_ref[:, cs] += _dot_tn(av, b_ref[:, cs])

        @pl.when(t == ns - 1)
        def _():
            out_ref[...] = acc_ref[...].astype(BF16)

    return pl.pallas_call(
        body, name=name, grid=(ka // tka, n // tn, ns),
        in_specs=[pl.BlockSpec((ts, tka), lambda i, j, t: (t, i)),
                  pl.BlockSpec((ts, tn), lambda i, j, t: (t, j))] + [pl.BlockSpec(memory_space=pl.ANY)] * len(deps),
        out_specs=pl.BlockSpec((tka, tn), lambda i, j, t: (i, j)),
        out_shape=jax.ShapeDtypeStruct((ka, n), BF16),
        scratch_shapes=[pltpu.VMEM((tka, tn), F32)],
        compiler_params=_params(("parallel", "parallel", "arbitrary")),
    )(a, b, *deps)


def _window(ref, kind, idx, size):
    start = pl.multiple_of(idx * size, size)
    if kind == 0:
        return ref.at[pl.ds(start, size)]
    if kind == 1:
        return ref.at[:, pl.ds(start, size)]
    return ref.at[:, :, pl.ds(start, size)]


def _mesh_place():
    x, y, c = lax.axis_index("x"), lax.axis_index("y"), lax.axis_index("c")
    return x, y, c, 4 * x + 2 * y + c


def _peer(x, y, c, q):
    px = 1 - x if q & 4 else x
    py = 1 - y if q & 2 else y
    pc = 1 - c if q & 1 else c
    return (px, py, pc), 4 * px + 2 * py + pc


def _gather_to_all(srcs, dsts, kinds, send_sems, recv_sems, local_sems, first):
    x, y, c, me = _mesh_place()
    sends, recvs, locals_ = [], [], []
    for k, (src, dst, kind) in enumerate(zip(srcs, dsts, kinds)):
        size = src.shape[kind]
        mine = _window(dst, kind, me, size)
        lc = pltpu.make_async_copy(src, mine, local_sems.at[first + k])
        lc.start()
        locals_.append(lc)
        for q in range(1, N_DEV):
            peer, peer_idx = _peer(x, y, c, q)
            sems = dict(send_sem=send_sems.at[first + k, q], recv_sem=recv_sems.at[first + k, q],
                        device_id=peer, device_id_type=MESH)
            cp = pltpu.make_async_remote_copy(src_ref=src, dst_ref=mine, **sems)
            cp.start()
            sends.append(cp)
            recvs.append(pltpu.make_async_remote_copy(src_ref=src, dst_ref=_window(dst, kind, peer_idx, size), **sems))
    for cp in recvs:
        cp.wait_recv()
    for cp in sends:
        cp.wait_send()
    for lc in locals_:
        lc.wait()


_HBM = pl.BlockSpec(memory_space=pltpu.HBM)
_SEM = pl.BlockSpec(memory_space=pltpu.SEMAPHORE)
_EFFECT = pltpu.SideEffectType.DATAFLOW_SIDE_EFFECTING


N_NEAR = 4


def _near(x, y, c):
    out = [((x, y, 1 - c), 4 * x + 2 * y + 1 - c)]
    for j in (1, 2, 3):
        px = 1 - x if j & 2 else x
        py = 1 - y if j & 1 else y
        out.append(((px, py, c), 4 * px + 2 * py + c))
    return out


def _remote(src, dst, send_sems, recv_sems, slot, device):
    return pltpu.make_async_remote_copy(src_ref=src, dst_ref=dst, send_sem=send_sems.at[slot], recv_sem=recv_sems.at[slot],
                                        device_id=device, device_id_type=MESH)


def _split_call(name, arrays, sems_in, n_new_sems, after, emit):
    na, ns, nn = len(arrays), len(sems_in), len(n_new_sems)

    def body(*refs):
        emit(refs[:na], refs[na:na + ns], refs[na + ns + 1:na + ns + 1 + nn])
        refs[-1][...] = jnp.zeros_like(refs[-1])

    outs = pl.pallas_call(
        body, name=name,
        out_shape=(*[pltpu.SemaphoreType.DMA((m,)) for m in n_new_sems],
                   *[pltpu.HBM(a.shape, a.dtype) for a in arrays], jax.ShapeDtypeStruct((8, 128), F32)),
        in_specs=[_HBM] * na + [_SEM] * ns + [pl.BlockSpec(memory_space=pl.ANY)],
        out_specs=(*[_SEM] * nn, *[_HBM] * na, pl.BlockSpec(memory_space=pltpu.VMEM)),
        input_output_aliases={i: nn + i for i in range(na)},
        compiler_params=pltpu.CompilerParams(has_side_effects=_EFFECT),
    )(*[pltpu.with_memory_space_constraint(a, pltpu.HBM) for a in arrays], *sems_in, after)
    return list(outs[:nn]), list(outs[nn:nn + na]), outs[-1]


class _Gather:
    def __init__(self, shards, kinds, after, name):
        self.n, self.kinds, self.name = len(shards), kinds, name
        self.sizes = [s.shape[k] for s, k in zip(shards, kinds)]
        n = self.n
        lands = []
        for s, k in zip(shards, kinds):
            dims = list(s.shape)
            dims[k] *= N_DEV
            lands.append(lax.empty(tuple(dims), s.dtype))

        def emit(arr, _, new):
            x, y, c, me = _mesh_place()
            for k in range(n):
                pltpu.make_async_copy(arr[k], _window(arr[n + k], kinds[k], me, self.sizes[k]), new[2].at[k]).start()
            for k in range(n):
                mine = _window(arr[n + k], kinds[k], me, self.sizes[k])
                for j, (dev, _) in enumerate(_near(x, y, c)):
                    _remote(arr[k], mine, new[0], new[1], k * N_NEAR + j, dev).start()

        self.sems, self.arrays, self.token = _split_call(name + "_start", [*shards, *lands], [],
                                                         [n * N_NEAR, n * N_NEAR, n], after, emit)

    def forward(self, after):
        n, kinds, sizes = self.n, self.kinds, self.sizes

        def emit(arr, old, new):
            x, y, c, _ = _mesh_place()
            near = _near(x, y, c)
            for k in range(n):
                for j in (1, 2, 3):
                    dev, idx = near[j]
                    landed = _window(arr[n + k], kinds[k], idx, sizes[k])
                    _remote(arr[k], landed, old[0], old[1], k * N_NEAR + j, dev).wait_recv()
                    _remote(landed, landed, new[0], new[1], k * N_NEAR + j, near[0][0]).start()

        new, self.arrays, self.token = _split_call(self.name + "_forward", self.arrays, self.sems, [n * N_NEAR] * 2,
                                                   after, emit)
        self.sems = [*self.sems, *new]

    def finish(self, after):
        n, kinds, sizes = self.n, self.kinds, self.sizes

        def emit(arr, old, _):
            x, y, c, me = _mesh_place()
            near = _near(x, y, c)
            other_core = near[0][0]
            for k in range(n):
                win = lambda idx: _window(arr[n + k], kinds[k], idx, sizes[k])
                pltpu.make_async_copy(arr[k], win(me), old[2].at[k]).wait()
                for j, (dev, idx) in enumerate(near):
                    _remote(arr[k], win(me), old[0], old[1], k * N_NEAR + j, dev).wait_send()
                _remote(arr[k], win(near[0][1]), old[0], old[1], k * N_NEAR, other_core).wait_recv()
                for j in (1, 2, 3):
                    idx = near[j][1]
                    _remote(win(idx), win(idx), old[3], old[4], k * N_NEAR + j, other_core).wait_send()
                    _remote(arr[k], win(idx + 1 - 2 * c), old[3], old[4], k * N_NEAR + j, other_core).wait_recv()

        _, arrays, _ = _split_call(self.name + "_finish", self.arrays, self.sems, [], after, emit)
        return arrays[n:]


class _Scatter:
    def __init__(self, partials, kinds, after, name):
        self.n, self.kinds, self.name, self.partials = len(partials), kinds, name, partials
        self.sizes = [p.shape[k] // N_DEV for p, k in zip(partials, kinds)]
        n, sizes = self.n, self.sizes
        self.slot_shapes = []
        for p, k, size in zip(partials, kinds, sizes):
            dims = list(p.shape)
            dims[k] = size
            self.slot_shapes.append((N_NEAR, *dims))
        slots = [lax.empty(sh, p.dtype) for sh, p in zip(self.slot_shapes, partials)]

        def emit(arr, _, new):
            x, y, c, _ = _mesh_place()
            near = _near(x, y, c)
            for k in range(n):
                for j in range(N_NEAR):
                    owner = near[j][1] if j == 0 else near[j][1] + 1 - 2 * c
                    _remote(_window(arr[k], kinds[k], owner, sizes[k]), arr[n + k].at[j], new[0], new[1],
                            k * N_NEAR + j, near[0][0]).start()

        self.sems, self.arrays, self.token = _split_call(name + "_start", [*partials, *slots], [], [n * N_NEAR] * 2,
                                                         after, emit)

    def combine_and_send(self, own4, after):
        n, kinds, sizes = self.n, self.kinds, self.sizes

        def emit_wait(arr, old, _):
            x, y, c, _ = _mesh_place()
            near = _near(x, y, c)
            for k in range(n):
                for j in range(N_NEAR):
                    owner = near[j][1] if j == 0 else near[j][1] + 1 - 2 * c
                    cp = _remote(_window(arr[k], kinds[k], owner, sizes[k]), arr[n + k].at[j], old[0], old[1],
                                 k * N_NEAR + j, near[0][0])
                    cp.wait_send()
                    cp.wait_recv()

        _, arrays, _ = _split_call(self.name + "_landed", self.arrays, self.sems, [], after, emit_wait)
        chip_sums = _chip_sums(arrays[:n], arrays[n:], kinds, sizes, own4, self.name + "_combine")
        arrivals = [lax.empty((N_NEAR - 1, *sh[1:]), p.dtype) for sh, p in zip(self.slot_shapes, self.partials)]

        def emit_send(arr, _, new):
            x, y, c, _ = _mesh_place()
            near = _near(x, y, c)
            for k in range(n):
                for j in (1, 2, 3):
                    _remote(arr[k].at[j], arr[n + k].at[j - 1], new[0], new[1], k * N_NEAR + j, near[j][0]).start()

        self.sems, self.arrays, self.token = _split_call(self.name + "_send", [*chip_sums, *arrivals], [],
                                                         [n * N_NEAR] * 2, own4, emit_send)

    def finish(self, after):
        n = self.n

        def emit(arr, old, _):
            x, y, c, _ = _mesh_place()
            near = _near(x, y, c)
            for k in range(n):
                for j in (1, 2, 3):
                    cp = _remote(arr[k].at[j], arr[n + k].at[j - 1], old[0], old[1], k * N_NEAR + j, near[j][0])
                    cp.wait_send()
                    cp.wait_recv()

        _, arrays, _ = _split_call(self.name + "_finish", self.arrays, self.sems, [], after, emit)
        return arrays[:n], arrays[n:]


def _chip_sums(partials, slots, kinds, sizes, own4, name):
    n = len(partials)

    def body(own_ref, *refs):
        for k in range(n):
            refs[2 * n + k][...] = (refs[k][...].astype(F32) + refs[n + k][...].astype(F32)).astype(BF16)

    in_specs, slot_specs = [], []
    for p, s, kind, size in zip(partials, slots, kinds, sizes):
        block = list(p.shape)
        block[kind] = size
        nd = len(block)
        in_specs.append(pl.BlockSpec(tuple(block), functools.partial(
            lambda j, own, kind, nd: tuple(own[j] if d == kind else 0 for d in range(nd)), kind=kind, nd=nd)))
        slot_specs.append(pl.BlockSpec((None, *block), functools.partial(
            lambda j, own, nd: (j,) + (0,) * nd, nd=nd)))
    return pl.pallas_call(
        body, name=name,
        grid_spec=pltpu.PrefetchScalarGridSpec(num_scalar_prefetch=1, grid=(N_NEAR,),
                                               in_specs=in_specs + slot_specs, out_specs=slot_specs),
        out_shape=[jax.ShapeDtypeStruct(s.shape, s.dtype) for s in slots],
        compiler_params=_params(("arbitrary",)),
    )(own4, *partials, *slots)


def _to_bf16(arrays):
    def body(*refs):
        for src, dst in zip(refs[:len(arrays)], refs[len(arrays):]):
            dst[...] = src[...].astype(BF16)

    return pl.pallas_call(body, name="to_bf16", out_shape=[jax.ShapeDtypeStruct(a.shape, BF16) for a in arrays],
                          compiler_params=pltpu.CompilerParams(vmem_limit_bytes=V7X_VMEM_LIMIT))(*arrays)


def _silu(c):
    return c * _sigmoid_tail(c)


def _conditioning(c_rows8, conv_w_rows8, w_ada, b_ada_cols):
    n_ada = w_ada.shape[1]

    def body(c_ref, cw_ref, w_ref, b_ref, c_all_ref, cw_all_ref, mod_ref, part_ref, send_sems, recv_sems, local_sems):
        _gather_to_all([c_ref, cw_ref], [c_all_ref, cw_all_ref], [0, 1], send_sems, recv_sems, local_sems, 0)
        pick = (lax.broadcasted_iota(jnp.int32, (N_DEV, N_DEV * 8), 1)
                == 8 * lax.broadcasted_iota(jnp.int32, (N_DEV, N_DEV * 8), 0)).astype(F32)
        c_all = jnp.dot(pick, c_all_ref[...], preferred_element_type=F32, precision=lax.Precision.HIGHEST)
        part_ref[...] = jnp.dot(_silu(c_all), w_ref[...], preferred_element_type=F32,
                                precision=lax.Precision.HIGHEST) + b_ref[...]
        _gather_to_all([part_ref], [mod_ref], [0], send_sems, recv_sems, local_sems, 2)

    vmem = pl.BlockSpec(memory_space=pltpu.VMEM)
    return pl.pallas_call(
        body, name="conditioning",
        in_specs=[vmem] * 4, out_specs=[vmem] * 3,
        out_shape=[jax.ShapeDtypeStruct((N_DEV * 8, D), F32), jax.ShapeDtypeStruct((8, D), F32),
                   jax.ShapeDtypeStruct((N_DEV * N_DEV, n_ada), F32)],
        scratch_shapes=[pltpu.VMEM((N_DEV, n_ada), F32), pltpu.SemaphoreType.DMA((3, N_DEV)),
                        pltpu.SemaphoreType.DMA((3, N_DEV)), pltpu.SemaphoreType.DMA((3,))],
        compiler_params=pltpu.CompilerParams(vmem_limit_bytes=V7X_VMEM_LIMIT),
    )(c_rows8, conv_w_rows8, w_ada, b_ada_cols)


def _adam(w, g, m, v):
    m = ADAM_B1 * m + (1.0 - ADAM_B1) * g
    v = ADAM_B2 * v + (1.0 - ADAM_B2) * (g * g)
    m_hat = m / (1.0 - ADAM_B1 ** ADAM_STEP)
    v_hat = v / (1.0 - ADAM_B2 ** ADAM_STEP)
    delta = -ADAM_LR * (m_hat / (jnp.sqrt(v_hat) + ADAM_EPS) + ADAM_WD * w)
    return delta, m, v


def _ada_bwd_adam(c_all, dmod_cols, w, m, v):
    def body(c_ref, d_ref, w_ref, m_ref, v_ref, g_ref, delta_ref, nm_ref, nv_ref):
        g = lax.dot_general(_silu(c_ref[...]), d_ref[...], (((0,), (0,)), ((), ())),
                            preferred_element_type=F32, precision=lax.Precision.HIGHEST)
        g_ref[...] = g
        delta_ref[...], nm_ref[...], nv_ref[...] = _adam(w_ref[...], g, m_ref[...], v_ref[...])

    sd = jax.ShapeDtypeStruct(w.shape, F32)
    return pl.pallas_call(body, name="ada_bwd_adam", out_shape=[sd] * 4,
                          compiler_params=pltpu.CompilerParams(vmem_limit_bytes=V7X_VMEM_LIMIT),
                          )(c_all, dmod_cols, w, m, v)


def _adam_group(chip_sums, arrivals, ws, ms, vs, n_tiles, name):
    n = len(ws)

    def body(*refs):
        for k in range(n):
            c_ref, a_ref, w_ref, m_ref, v_ref = (refs[j * n + k] for j in range(5))
            g_ref, delta_ref, nm_ref, nv_ref = (refs[(5 + j) * n + k] for j in range(4))
            g = c_ref[...].astype(F32)
            for j in range(N_NEAR - 1):
                g = g + a_ref[j].astype(F32)
            g_ref[...] = g
            delta_ref[...], nm_ref[...], nv_ref[...] = _adam(w_ref[...], g, m_ref[...], v_ref[...])

    tiles = [(w.shape[0] // n_tiles, w.shape[1]) for w in ws]
    blk = [pl.BlockSpec(t, lambda i: (i, 0)) for t in tiles]
    return pl.pallas_call(
        body, name=name, grid=(n_tiles,),
        in_specs=[pl.BlockSpec((None, *t), lambda i: (0, i, 0)) for t in tiles]
        + [pl.BlockSpec((N_NEAR - 1, *t), lambda i: (0, i, 0)) for t in tiles] + blk * 3,
        out_specs=blk * 4, out_shape=[jax.ShapeDtypeStruct(w.shape, F32) for w in ws] * 4,
        compiler_params=_params(("parallel",)),
    )(*chip_sums, *arrivals, *ws, *ms, *vs)


N_SMALL = 40
N_SMALL_PARAMS = 11


def _pack_vecs(conv_w_full, rows):
    def body(cw_ref, *refs):
        out = refs[-1]
        out[...] = jnp.zeros_like(out)
        out[0:4, :] = cw_ref[0:4, :]
        for r, ref in enumerate(refs[:-1]):
            out[4 + r:5 + r, :] = ref[...]

    return pl.pallas_call(body, name="pack_vecs", out_shape=jax.ShapeDtypeStruct((16, D), F32))(conv_w_full, *rows)


def _small_finish(gathered, mod_all, vecs, ws, ms, vs):
    n = N_SMALL_PARAMS

    def body(g_ref, mod_ref, vec_ref, *refs):
        w_refs, m_refs, v_refs = refs[:n], refs[n:2 * n], refs[2 * n:3 * n]
        outs = refs[3 * n:]
        g1 = vec_ref[V_G1:V_G1 + 1, :]
        g2 = vec_ref[V_G2:V_G2 + 1, :]
        zero = jnp.zeros((1, D), F32)
        dg1, dg2, dgf, loss_lanes = zero, zero, zero, zero
        mixer = jnp.zeros((16, D), F32)
        db_ada = jnp.zeros((6, D), F32)
        for b in range(N_DEV):
            gb = g_ref[b]
            mod = mod_ref[b]
            q1 = gb[33:34]
            q2 = gb[9:10]
            dmod = jnp.concatenate([gb[32:33], q1 * g1, gb[10:11], gb[8:9], q2 * g2, gb[1:2]], axis=0)
            outs[4 * n][b] = dmod
            db_ada = db_ada + dmod
            dg1 = dg1 + q1 * (1.0 + mod[M_SC1:M_SC1 + 1])
            dg2 = dg2 + q2 * (1.0 + mod[M_SC2:M_SC2 + 1])
            dgf = dgf + gb[0:1]
            loss_lanes = loss_lanes + gb[2:3]
            mixer = mixer + gb[16:32]
        d_a_param = mixer[7:8] * _sigmoid_tail(vec_ref[V_A_PARAM:V_A_PARAM + 1, :])
        grads = [dg1, dg2, mixer[4:5], mixer[5:6], mixer[6:7], d_a_param, mixer[8:9], mixer[9:10], dgf,
                 db_ada, mixer[0:4]]
        for k in range(n):
            outs[k][...] = grads[k]
            outs[n + k][...], outs[2 * n + k][...], outs[3 * n + k][...] = _adam(
                w_refs[k][...], grads[k], m_refs[k][...], v_refs[k][...])
        outs[4 * n + 1][...] = jnp.broadcast_to(jnp.sum(loss_lanes, axis=1, keepdims=True), (8, 128))

    shapes = [jax.ShapeDtypeStruct(w.shape, F32) for w in ws]
    return pl.pallas_call(
        body, name="small_finish",
        out_shape=shapes * 4 + [jax.ShapeDtypeStruct((N_DEV, 6, D), F32), jax.ShapeDtypeStruct((8, 128), F32)],
    )(gathered, mod_all, vecs, *ws, *ms, *vs)


def _pad_rows(a, rows):
    return jnp.pad(a, ((0, rows - a.shape[0]), (0, 0)))


def kernel(x, c, norm_mix_g, norm_mlp_g, w_ada, b_ada, w_in, conv_w, conv_b, w_rg_a, b_rg_a, w_rg_x, b_rg_x, a_param, w_branch_a, w_pool, b_pool, pool_scale, w_branch_b, w_out, w_up, w_down, final_g, loss_target, m_norm_mix_g, m_norm_mlp_g, m_w_ada, m_b_ada, m_w_in, m_conv_w, m_conv_b, m_w_rg_a, m_b_rg_a, m_w_rg_x, m_b_rg_x, m_a_param, m_w_branch_a, m_w_pool, m_b_pool, m_pool_scale, m_w_branch_b, m_w_out, m_w_up, m_w_down, m_final_g, v_norm_mix_g, v_norm_mlp_g, v_w_ada, v_b_ada, v_w_in, v_conv_w, v_conv_b, v_w_rg_a, v_b_rg_a, v_w_rg_x, v_b_rg_x, v_a_param, v_w_branch_a, v_w_pool, v_b_pool, v_pool_scale, v_w_branch_b, v_w_out, v_w_up, v_w_down, v_final_g):
    me = 4 * lax.axis_index("x") + 2 * lax.axis_index("y") + lax.axis_index("c")
    s = x.shape[1]
    x2d = x.reshape(s, D)
    target = loss_target.reshape(s, D)
    n_ada = w_ada.shape[2]

    sharded = dict(w_in=(w_in[0], 1), w_up=(w_up[0], 1), w_down=(w_down[0], 0), w_branch_a=(w_branch_a[0], 0),
                   w_branch_b=(w_branch_b[0], 0), w_out=(w_out[0], 0), w_rg_a=(w_rg_a[0], 1), w_rg_x=(w_rg_x[0], 1),
                   w_pool=(w_pool[0], 1))
    kind = {k: v[1] for k, v in sharded.items()}
    shard = dict(zip(sharded, _to_bf16([v[0] for v in sharded.values()])))

    b_ada_cols = lax.dynamic_slice(b_ada, (0, me * n_ada), (1, n_ada))
    c_rows, conv_w_full, mod_parts = _conditioning(_pad_rows(c, 8), _pad_rows(conv_w[0], 8), w_ada[0], b_ada_cols)
    c_all = c_rows.reshape(N_DEV, 8, D)[:, 0, :]

    first_names = ["w_in", "w_rg_a", "w_rg_x", "w_pool"]
    branch_names = ["w_branch_a", "w_branch_b", "w_out"]
    mlp_names = ["w_up", "w_down"]

    def gather(group, after, name):
        return _Gather([shard[k] for k in group], [kind[k] for k in group], after, name)

    g_first = gather(first_names, mod_parts, "gather_first")
    g_branch = gather(branch_names, g_first.token, "gather_branch")
    g_mlp = gather(mlp_names, g_branch.token, "gather_mlp")

    mod_all = jnp.transpose(mod_parts.reshape(N_DEV, N_DEV, n_ada), (1, 0, 2)).reshape(N_DEV, 6, D)
    mod_all = jnp.pad(mod_all, ((0, 0), (0, 2), (0, 0)))
    modr = lax.dynamic_index_in_dim(mod_all, me, 0, keepdims=False)
    vecs = _pack_vecs(conv_w_full, [conv_b, b_rg_a, b_rg_x, a_param, b_pool, pool_scale,
                                    norm_mix_g, norm_mlp_g, final_g.reshape(1, D)])
    g_first.forward(g_mlp.token)
    wg = dict(zip(first_names, g_first.finish(g_first.token)))

    h1, x_rnn, u_pool, ga, dga, sa, sb = _proj_fwd(x2d, modr, vecs, wg["w_in"])
    g_branch.forward(h1)
    xr, hr, za, p, pooled, *gates = _mix_fwd(x_rnn, u_pool, ga, vecs, wg["w_rg_a"], wg["w_rg_x"], wg["w_pool"],
                                             dep=g_branch.token)
    g_mlp.forward(za)
    wg.update(zip(branch_names, g_branch.finish(g_mlp.token)))
    ba, bb, merged, o, x2, h2 = _branch_fwd(za, pooled, sa, sb, x2d, modr, vecs,
                                            wg["w_branch_a"], wg["w_branch_b"], wg["w_out"])
    wg.update(zip(mlp_names, g_mlp.finish(h2)))
    ru, dx3, d_dn, small_f = _mlp_fwd(h2, x2, target, modr, vecs, wg["w_up"], wg["w_down"])

    near = _near(lax.axis_index("x"), lax.axis_index("y"), lax.axis_index("c"))
    own4 = jnp.stack([me, near[1][1], near[2][1], near[3][1]]).astype(jnp.int32)

    def scatter(group, partial, after, name):
        return _Scatter([partial[k] for k in group], [kind[k] for k in group], after, name)

    dup, dx2, do, small_m = _mlp_bwd(d_dn, ru, x2, dx3, o, modr, vecs, wg["w_up"], wg["w_down"])
    partial = dict(w_up=_wgrad(h2, dup, "wgrad_up"), w_down=_wgrad(ru, d_dn, "wgrad_down", square_a=True))
    s_mlp = scatter(mlp_names, partial, dx2, "scatter_mlp")

    dba, dbb, dgates, dza, dpooled = _branch_bwd(do, sa, sb, ba, bb, wg["w_branch_a"], wg["w_branch_b"], wg["w_out"],
                                                 dep=s_mlp.token)
    s_mlp.combine_and_send(own4, dza)
    dproj, dw_rg_a, dw_rg_x, dw_pool, small_x = _mix_bwd(dza, dpooled, x_rnn, ga, dga, xr, hr, p, gates, dgates,
                                                         vecs, wg["w_rg_a"], wg["w_rg_x"], wg["w_pool"],
                                                         dep=s_mlp.token)
    partial.update(w_branch_a=_wgrad(za, dba, "wgrad_branch_a"), w_branch_b=_wgrad(pooled, dbb, "wgrad_branch_b"),
                   w_out=_wgrad(merged, do, "wgrad_out"),
                   w_rg_a=dw_rg_a, w_rg_x=dw_rg_x, w_pool=dw_pool)
    mixer_names = ["w_rg_a", "w_rg_x", "w_pool", "w_branch_a", "w_branch_b", "w_out"]
    s_mixer = scatter(mixer_names, partial, s_mlp.token, "scatter_mixer")

    partial["w_in"] = _wgrad(h1, dproj, "wgrad_in", dep=s_mixer.token)
    s_in = scatter(["w_in"], partial, s_mixer.token, "scatter_in")
    s_mixer.combine_and_send(own4, s_in.token)
    s_in.combine_and_send(own4, s_mixer.token)
    grad_x, small_p = _proj_bwd(dproj, x2d, dx2, modr, vecs, wg["w_in"], dep=s_in.token)

    locals_ = dict(w_in=(w_in, m_w_in, v_w_in), w_up=(w_up, m_w_up, v_w_up), w_down=(w_down, m_w_down, v_w_down),
                   w_branch_a=(w_branch_a, m_w_branch_a, v_w_branch_a),
                   w_branch_b=(w_branch_b, m_w_branch_b, v_w_branch_b), w_out=(w_out, m_w_out, v_w_out),
                   w_rg_a=(w_rg_a, m_w_rg_a, v_w_rg_a), w_rg_x=(w_rg_x, m_w_rg_x, v_w_rg_x),
                   w_pool=(w_pool, m_w_pool, v_w_pool))
    res = {}

    def finish(group, exchange, after, n_tiles, name):
        chip_sums, arrivals = exchange.finish(after)
        flat = lambda t: t.reshape(-1, t.shape[-1])
        shapes = [flat(locals_[k][0]).shape for k in group]
        outs = _adam_group([cs.reshape(N_NEAR, *sh) for cs, sh in zip(chip_sums, shapes)],
                           [ar.reshape(N_NEAR - 1, *sh) for ar, sh in zip(arrivals, shapes)],
                           *[[flat(locals_[k][j]) for k in group] for j in range(3)], n_tiles, name)
        for i, k in enumerate(group):
            res[k] = [outs[j * len(group) + i].reshape(locals_[k][0].shape) for j in range(4)]
        return res[group[-1]][0]

    small = jnp.concatenate([small_f, small_m, small_x, small_p], axis=0)
    g_small = _Gather([small], [0], grad_x, "gather_small")
    done = finish(mlp_names, s_mlp, g_small.token, 4, "adam_mlp")
    done = finish(mixer_names, s_mixer, done, 2, "adam_mixer")
    g_small.forward(done)
    done = finish(["w_in"], s_in, g_small.token, 4, "adam_in")
    small_all, = g_small.finish(done)
    small_all = small_all.reshape(N_DEV, N_SMALL, D)

    def embed(cw):
        return lax.dynamic_update_slice(jnp.zeros((4, D), F32), cw[0], (0, me * (D // N_DEV)))

    def smalls(ng, nl, cb, bra, brx, ap, bp, ps, fg, ba_, cw):
        return [ng, nl, cb, bra, brx, ap, bp, ps, fg.reshape(1, D), ba_.reshape(6, D), embed(cw)]

    small_names = ["norm_mix_g", "norm_mlp_g", "conv_b", "b_rg_a", "b_rg_x", "a_param", "b_pool", "pool_scale",
                   "final_g", "b_ada", "conv_w"]
    fin = _small_finish(
        small_all, mod_all, vecs,
        smalls(norm_mix_g, norm_mlp_g, conv_b, b_rg_a, b_rg_x, a_param, b_pool, pool_scale, final_g, b_ada, conv_w),
        smalls(m_norm_mix_g, m_norm_mlp_g, m_conv_b, m_b_rg_a, m_b_rg_x, m_a_param, m_b_pool, m_pool_scale,
               m_final_g, m_b_ada, m_conv_w),
        smalls(v_norm_mix_g, v_norm_mlp_g, v_conv_b, v_b_rg_a, v_b_rg_x, v_a_param, v_b_pool, v_pool_scale,
               v_final_g, v_b_ada, v_conv_w))
    dmod_all, loss_tile = fin[4 * N_SMALL_PARAMS], fin[4 * N_SMALL_PARAMS + 1]
    dmod_cols = lax.dynamic_slice(dmod_all.reshape(N_DEV, 6 * D), (0, me * n_ada), (N_DEV, n_ada))
    res["w_ada"] = [t.reshape(w_ada.shape) for t in _ada_bwd_adam(c_all, dmod_cols, w_ada[0], m_w_ada[0], v_w_ada[0])]

    def final_shape(k, t):
        if k == "final_g":
            return t.reshape(D)
        if k == "b_ada":
            return t.reshape(1, 6 * D)
        if k == "conv_w":
            return lax.dynamic_slice(t, (0, me * (D // N_DEV)), (4, D // N_DEV)).reshape(conv_w.shape)
        return t

    for i, k in enumerate(small_names):
        res[k] = [final_shape(k, fin[which * N_SMALL_PARAMS + i]) for which in range(4)]
    order = ["norm_mix_g", "norm_mlp_g", "w_ada", "b_ada", "w_in", "conv_w", "conv_b", "w_rg_a", "b_rg_a", "w_rg_x",
             "b_rg_x", "a_param", "w_branch_a", "w_pool", "b_pool", "pool_scale", "w_branch_b", "w_out", "w_up",
             "w_down", "final_g"]
    outs = [loss_tile[0, 0], grad_x.reshape(x.shape)]
    for which in range(4):
        for k in order:
            outs.append(res[k][which])
    return tuple(outs)
```

```python
import functools

import jax
import jax.numpy as jnp
from jax import lax
from jax.experimental import pallas as pl
from jax.experimental.pallas import tpu as pltpu

F32 = jnp.float32
BF16 = jnp.bfloat16
MESH = pl.DeviceIdType.MESH

N_DEV = 8
D = 1024
N_GROUPS = 4
GW = D // N_GROUPS
D_IN = 5 * D
D_FF = 4 * D
POOL_WINDOWS = (2, 4, 8, 16)
HALO_X = 8
HALO_U = 16
EPS = 1e-6
C_RG = 8.0
ADAM_LR, ADAM_B1, ADAM_B2, ADAM_EPS, ADAM_WD, ADAM_STEP = 0.001, 0.9, 0.999, 1e-08, 0.01, 10

V7X_VMEM_LIMIT = 56 * 1024 * 1024

V_CONV_W, V_CONV_B, V_B_RG_A, V_B_RG_X, V_A_PARAM, V_B_POOL, V_POOL_SCALE, V_G1, V_G2, V_GF = 0, 4, 5, 6, 7, 8, 9, 10, 11, 12
M_SH1, M_SC1, M_GT1, M_SH2, M_SC2, M_GT2 = 0, 1, 2, 3, 4, 5

TM_PROJ = 512
TM_MIX = 256
TM_BRANCH = 512
TM_MLP = 512
TM_MLP_BWD = 256
TS_WGRAD = 1024


def _params(semantics):
    return pltpu.CompilerParams(dimension_semantics=semantics, vmem_limit_bytes=V7X_VMEM_LIMIT)


def _resident(shape):
    return pl.BlockSpec(shape, lambda *_: (0,) * len(shape), pipeline_mode=pl.Buffered(1))


def _dot(a, b):
    return jnp.dot(a, b, preferred_element_type=F32)


def _dot_nt(a, b):
    return lax.dot_general(a, b, (((1,), (1,)), ((), ())), preferred_element_type=F32)


def _dot_tn(a, b):
    return lax.dot_general(a, b, (((0,), (0,)), ((), ())), preferred_element_type=F32)


def _sigmoid(x):
    return 0.5 * jnp.tanh(0.5 * x) + 0.5


def _sigmoid_tail(x):
    return 1.0 / (1.0 + jnp.exp(-x))


def _gelu_and_grad(x):
    k = 0.7978845608028654
    x2 = x * x
    t = jnp.tanh(k * (x + 0.044715 * x * x2))
    g = 0.5 * x * (1.0 + t)
    dg = 0.5 * (1.0 + t) + 0.5 * x * (1.0 - t * t) * (k * (1.0 + 3.0 * 0.044715 * x2))
    return g, dg


def _softplus(a):
    e = jnp.exp(-jnp.abs(a))
    u = 1.0 + e
    log1p_e = jnp.where(u == 1.0, e, jnp.log(u) * e / jnp.where(u == 1.0, 1.0, u - 1.0))
    return jnp.maximum(a, 0.0) + log1p_e


def _neg_expm1(z):
    series = -(z * (1.0 + z * (0.5 + z * (1.0 / 6.0 + z * (1.0 / 24.0 + z * (1.0 / 120.0))))))
    return jnp.where(z > -0.1, series, 1.0 - jnp.exp(z))


def _shift_down(x, k):
    return pltpu.roll(x, k, 0)


def _shift_up(x, k):
    return pltpu.roll(x, x.shape[0] - k, 0)


def _rglru_gates(xr, w_a, w_x, b_a, b_x, a_param, is_t0):
    xb = xr.astype(BF16)
    ra = _sigmoid(_dot(xb, w_a) + b_a)
    ri = _sigmoid(_dot(xb, w_x) + b_x)
    sp = _softplus(a_param)
    log_a = (-C_RG) * ra * sp
    a = jnp.exp(log_a)
    mult = jnp.where(is_t0, 1.0, jnp.sqrt(_neg_expm1(2.0 * log_a)))
    return ra, ri, sp, a, mult


SUBLANES = 8


LANES = 128


def _scan_strip(a, b, carry, scr, down):
    t = b.shape[0]
    g = t // SUBLANES
    a3 = a.reshape(g, SUBLANES, LANES)
    b3 = b.reshape(g, SUBLANES, LANES)
    sub = lax.broadcasted_iota(jnp.int32, (g, SUBLANES, LANES), 1)
    for k in (1, 2, 4):
        keep = sub >= k if down else sub < SUBLANES - k
        shift = k if down else SUBLANES - k
        b3 = b3 + a3 * jnp.where(keep, pltpu.roll(b3, shift, 1), 0.0)
        a3 = a3 * jnp.where(keep, pltpu.roll(a3, shift, 1), 1.0)
    scr[0] = a3.reshape(t, LANES)
    scr[1] = b3.reshape(t, LANES)
    end_row = SUBLANES - 1 if down else 0
    ag = scr[0, pl.ds(end_row, g, stride=SUBLANES), :]
    bg = scr[1, pl.ds(end_row, g, stride=SUBLANES), :]
    rg = lax.broadcasted_iota(jnp.int32, (g, LANES), 0)
    edge = 0 if down else g - 1
    bg = bg + jnp.where(rg == edge, ag * carry, 0.0)
    k = 1
    while k < g:
        keep = rg >= k if down else rg < g - k
        shift = k if down else g - k
        bg = bg + ag * jnp.where(keep, pltpu.roll(bg, shift, 0), 0.0)
        if 2 * k < g:
            ag = ag * pltpu.roll(ag, shift, 0)
        k *= 2
    entering = jnp.where(rg != edge, pltpu.roll(bg, 1 if down else g - 1, 0), carry)
    for r in range(SUBLANES):
        scr[2, pl.ds(r, g, stride=SUBLANES), :] = entering
    return scr[1] + scr[0] * scr[2], bg[g - 1:g, :]


def _scan_strips(a, b, carry, scr, down):
    outs = [_scan_strip(a[:, c:c + LANES], b[:, c:c + LANES], carry[:, c:c + LANES], scr, down)
            for c in range(0, b.shape[1], LANES)]
    return jnp.concatenate([o[0] for o in outs], axis=1), jnp.concatenate([o[1] for o in outs], axis=1)


def _scan_down(a, b, carry, scr):
    return _scan_strips(a, b, carry, scr, True)


def _scan_up(m, b, carry, scr):
    return _scan_strips(m, b, carry, scr, False)[0]


def _window_mean(sums, window, first_block, head_t):
    scaled = sums * (1.0 / window)
    head = jnp.where(first_block, sums[:HALO_U] / jnp.minimum(head_t, float(window)), scaled[:HALO_U])
    return jnp.concatenate([head, scaled[HALO_U:]], axis=0)


def _conv_taps(x_ext):
    return [_shift_down(x_ext, 3 - j)[HALO_X:] if j < 3 else x_ext[HALO_X:] for j in range(4)]


def _proj_fwd(x, modr, vecs, w_in, col_blocks, proj_so_far, name):
    s = x.shape[0]
    tm = min(TM_PROJ, s)
    cb = D_IN // 4

    def body(cols_ref, x_ref, mod_ref, vec_ref, w_ref, *refs):
        proj_ref, h1_ref = refs[-2:]
        xv = x_ref[...]
        r = lax.rsqrt(jnp.mean(xv * xv, axis=-1, keepdims=True) + EPS)
        gain = vec_ref[V_G1:V_G1 + 1, :] * (1.0 + mod_ref[M_SC1:M_SC1 + 1, :])
        h = (xv * r * gain + mod_ref[M_SH1:M_SH1 + 1, :]).astype(BF16)
        proj_ref[...] = _dot(h, w_ref[...])

        @pl.when(pl.program_id(0) == 0)
        def _():
            h1_ref[...] = h

    earlier = [] if proj_so_far is None else [proj_so_far]
    return pl.pallas_call(
        body, name=name,
        grid_spec=pltpu.PrefetchScalarGridSpec(
            num_scalar_prefetch=1, grid=(2, s // tm),
            in_specs=[pl.BlockSpec((tm, D), lambda h, i, cols: (i, 0)),
                      pl.BlockSpec((8, D), lambda h, i, cols: (0, 0)),
                      pl.BlockSpec((16, D), lambda h, i, cols: (0, 0)),
                      pl.BlockSpec((D, cb), lambda h, i, cols: (0, cols[h]))]
            + [pl.BlockSpec(memory_space=pl.ANY)] * len(earlier),
            out_specs=[pl.BlockSpec((tm, cb), lambda h, i, cols: (i, cols[h])),
                       pl.BlockSpec((tm, D), lambda h, i, cols: (jnp.where(h == 0, i, s // tm - 1), 0))]),
        out_shape=[jax.ShapeDtypeStruct((s, D_IN), F32), jax.ShapeDtypeStruct((s, D), BF16)],
        input_output_aliases={5: 0} if earlier else {},
        compiler_params=_params(("arbitrary", "arbitrary")),
    )(col_blocks, x, modr, vecs, w_in, *earlier)


def _mix_fwd(proj, vecs, w_rg_a, w_rg_x, w_pool, dep):
    s = proj.shape[0]
    tm = min(TM_MIX, s)
    nb = s // tm

    def body(xh_ref, x_ref, y_ref, uh_ref, u_ref, vec_ref, wa_ref, wx_ref, wp_ref, dep_ref,
             xr_ref, hr_ref, za_ref, p_ref, pooled_ref, a_ref, mult_ref, ra_ref, ri_ref, carry_ref, scan_scr):
        i = pl.program_id(0)
        first = i == 0

        @pl.when(first)
        def _():
            carry_ref[...] = jnp.zeros_like(carry_ref)

        row = lax.broadcasted_iota(jnp.int32, (tm, GW), 0)
        is_t0 = jnp.logical_and(first, row == 0)
        head_t = (lax.broadcasted_iota(jnp.int32, (HALO_U, GW), 0) + 1).astype(F32)
        for g in range(N_GROUPS):
            cs = slice(g * GW, (g + 1) * GW)
            vec = vec_ref[:, cs]
            xh = jnp.where(first, 0.0, xh_ref[:, cs])
            taps = _conv_taps(jnp.concatenate([xh, x_ref[:, cs]], axis=0))
            xr = vec[V_CONV_B:V_CONV_B + 1]
            for j in range(4):
                xr = xr + vec[V_CONV_W + j:V_CONV_W + j + 1] * taps[j]
            xr_ref[:, cs] = xr
            ra, ri, _, a, mult = _rglru_gates(
                xr, wa_ref[g], wx_ref[g], vec[V_B_RG_A:V_B_RG_A + 1], vec[V_B_RG_X:V_B_RG_X + 1],
                vec[V_A_PARAM:V_A_PARAM + 1], is_t0)
            a_ref[:, cs] = a
            mult_ref[:, cs] = mult
            ra_ref[:, cs] = ra.astype(BF16)
            ri_ref[:, cs] = ri.astype(BF16)
            h, last = _scan_down(a, xr * ri * mult, carry_ref[0:1, cs], scan_scr)
            hr_ref[:, cs] = h
            carry_ref[0:1, cs] = last
            za_ref[:, cs] = (_gelu_and_grad(y_ref[:, cs])[0] * h).astype(BF16)
            uh = jnp.where(first, 0.0, uh_ref[:, cs])
            sm = jnp.concatenate([uh, u_ref[:, cs]], axis=0)
            k = 1
            while k < POOL_WINDOWS[g]:
                sm = sm + _shift_down(sm, k)
                k *= 2
            mean = _window_mean(sm[HALO_U:], POOL_WINDOWS[g], first, head_t)
            p = (mean - u_ref[:, cs]).astype(BF16)
            p_ref[:, cs] = p
            pb = _dot(p, wp_ref[g]) + vec[V_B_POOL:V_B_POOL + 1]
            pooled_ref[:, cs] = (pb * vec[V_POOL_SCALE:V_POOL_SCALE + 1]).astype(BF16)

    tok = pl.BlockSpec((tm, D), lambda i: (i, 0))
    col = lambda k: pl.BlockSpec((tm, D), lambda i: (i, k))
    halo = lambda rows, k: pl.BlockSpec((rows, D), lambda i: (jnp.maximum(i * (tm // rows) - 1, 0), k))
    wspec = pl.BlockSpec((N_GROUPS, GW, GW), lambda i: (0, 0, 0))
    sd = lambda dt: jax.ShapeDtypeStruct((s, D), dt)
    return pl.pallas_call(
        body, name="mix_fwd", grid=(nb,),
        in_specs=[halo(HALO_X, 0), col(0), col(1), halo(HALO_U, 2), col(2), pl.BlockSpec((16, D), lambda i: (0, 0)),
                  wspec, wspec, wspec, pl.BlockSpec(memory_space=pl.ANY)],
        out_specs=[tok] * 9,
        out_shape=[sd(F32), sd(F32), sd(BF16), sd(BF16), sd(BF16), sd(F32), sd(F32), sd(BF16), sd(BF16)],
        scratch_shapes=[pltpu.VMEM((8, D), F32), pltpu.VMEM((3, tm, LANES), F32)],
        compiler_params=_params(("arbitrary",)),
    )(proj, proj, proj, proj, proj, vecs, w_rg_a, w_rg_x, w_pool, dep)


def _branch_fwd(za, pooled, proj, x, modr, vecs, w_a, w_b, w_out):
    s = x.shape[0]
    tm = min(TM_BRANCH, s)

    def body(za_ref, pooled_ref, ga_ref, gb_ref, x_ref, mod_ref, vec_ref, wa_ref, wb_ref, wo_ref,
             ba_ref, bb_ref, merged_ref, o_ref, x2_ref, h2_ref):
        ba = _dot(za_ref[...], wa_ref[...])
        bb = _dot(pooled_ref[...], wb_ref[...])
        ba_ref[...] = ba.astype(BF16)
        bb_ref[...] = bb.astype(BF16)
        merged = (_sigmoid(ga_ref[...]) * ba + _sigmoid(gb_ref[...]) * bb).astype(BF16)
        merged_ref[...] = merged
        o = _dot(merged, wo_ref[...])
        o_ref[...] = o.astype(BF16)
        x2 = x_ref[...] + mod_ref[M_GT1:M_GT1 + 1, :] * o
        x2_ref[...] = x2
        r = lax.rsqrt(jnp.mean(x2 * x2, axis=-1, keepdims=True) + EPS)
        gain = vec_ref[V_G2:V_G2 + 1, :] * (1.0 + mod_ref[M_SC2:M_SC2 + 1, :])
        h2_ref[...] = (x2 * r * gain + mod_ref[M_SH2:M_SH2 + 1, :]).astype(BF16)

    tok = pl.BlockSpec((tm, D), lambda i: (i, 0))
    wspec = pl.BlockSpec((D, D), lambda i: (0, 0))
    sd = lambda dt: jax.ShapeDtypeStruct((s, D), dt)
    return pl.pallas_call(
        body, name="branch_fwd", grid=(s // tm,),
        in_specs=[tok, tok, pl.BlockSpec((tm, D), lambda i: (i, 3)), pl.BlockSpec((tm, D), lambda i: (i, 4)),
                  tok, pl.BlockSpec((8, D), lambda i: (0, 0)), pl.BlockSpec((16, D), lambda i: (0, 0)),
                  wspec, wspec, wspec],
        out_specs=[tok] * 6,
        out_shape=[sd(BF16), sd(BF16), sd(BF16), sd(BF16), sd(F32), sd(BF16)],
        compiler_params=_params(("parallel",)),
    )(za, pooled, proj, proj, x, modr, vecs, w_a, w_b, w_out)


def _mlp_fwd(h2, x2, target, modr, vecs, w_up, w_down):
    s = x2.shape[0]
    tm = min(TM_MLP, s)

    def body(h2_ref, x2_ref, tgt_ref, mod_ref, vec_ref, wu_ref, wd_ref,
             ru_ref, dx3_ref, ddn_ref, small_ref):
        @pl.when(pl.program_id(0) == 0)
        def _():
            small_ref[...] = jnp.zeros_like(small_ref)

        h2 = h2_ref[...]
        dn = None
        for c in range(D_FF // D):
            cs = slice(c * D, (c + 1) * D)
            ru = jnp.maximum(_dot(h2, wu_ref[:, cs]), 0.0)
            ru_ref[:, cs] = ru.astype(BF16)
            part = _dot((ru * ru).astype(BF16), wd_ref[cs, :])
            dn = part if dn is None else dn + part
        gt2 = mod_ref[M_GT2:M_GT2 + 1, :]
        gf = vec_ref[V_GF:V_GF + 1, :]
        x3 = x2_ref[...] + gt2 * dn
        r3 = lax.rsqrt(jnp.mean(x3 * x3, axis=-1, keepdims=True) + EPS)
        n3 = x3 * r3
        err = n3 * gf - tgt_ref[...]
        dy = err * (1.0 / D)
        dn3 = dy * gf
        dx3 = r3 * (dn3 - n3 * jnp.mean(dn3 * n3, axis=-1, keepdims=True))
        dx3_ref[...] = dx3
        ddn_ref[...] = (dx3 * gt2).astype(BF16)
        small_ref[0:1, :] += jnp.sum(dy * n3, axis=0, keepdims=True)
        small_ref[1:2, :] += jnp.sum(dx3 * dn, axis=0, keepdims=True)
        small_ref[2:3, :] += (0.5 / D) * jnp.sum(err * err, axis=0, keepdims=True)

    tok = pl.BlockSpec((tm, D), lambda i: (i, 0))
    return pl.pallas_call(
        body, name="mlp_fwd", grid=(s // tm,),
        in_specs=[tok, tok, tok,
                  pl.BlockSpec((8, D), lambda i: (0, 0)), pl.BlockSpec((16, D), lambda i: (0, 0)),
                  _resident((D, D_FF)), _resident((D_FF, D))],
        out_specs=[pl.BlockSpec((tm, D_FF), lambda i: (i, 0)), tok, tok,
                   pl.BlockSpec((8, D), lambda i: (0, 0))],
        out_shape=[jax.ShapeDtypeStruct((s, D_FF), BF16), jax.ShapeDtypeStruct((s, D), F32),
                   jax.ShapeDtypeStruct((s, D), BF16), jax.ShapeDtypeStruct((8, D), F32)],
        compiler_params=_params(("arbitrary",)),
    )(h2, x2, target, modr, vecs, w_up, w_down)


def _mlp_bwd(d_dn, ru, x2, dx3, o, modr, vecs, w_up, w_down):
    s = x2.shape[0]
    tm = min(TM_MLP_BWD, s)

    def body(ddn_ref, ru_ref, x2_ref, dx3_ref, o_ref, mod_ref, vec_ref, wu_ref, wd_ref,
             dup_ref, dx2_ref, do_ref, small_ref):
        @pl.when(pl.program_id(0) == 0)
        def _():
            small_ref[...] = jnp.zeros_like(small_ref)

        ddn = ddn_ref[...]
        dh2 = None
        for c in range(D_FF // D):
            cs = slice(c * D, (c + 1) * D)
            dff = _dot_nt(ddn, wd_ref[cs, :])
            dup = (dff * (2.0 * ru_ref[:, cs].astype(F32))).astype(BF16)
            dup_ref[:, cs] = dup
            part = _dot_nt(dup, wu_ref[:, cs])
            dh2 = part if dh2 is None else dh2 + part
        x2 = x2_ref[...]
        r2 = lax.rsqrt(jnp.mean(x2 * x2, axis=-1, keepdims=True) + EPS)
        xn2 = x2 * r2
        gain = vec_ref[V_G2:V_G2 + 1, :] * (1.0 + mod_ref[M_SC2:M_SC2 + 1, :])
        dxn2 = dh2 * gain
        dx2 = dx3_ref[...] + r2 * (dxn2 - xn2 * jnp.mean(dxn2 * xn2, axis=-1, keepdims=True))
        dx2_ref[...] = dx2
        do_ref[...] = (dx2 * mod_ref[M_GT1:M_GT1 + 1, :]).astype(BF16)
        small_ref[0:1, :] += jnp.sum(dh2, axis=0, keepdims=True)
        small_ref[1:2, :] += jnp.sum(dh2 * xn2, axis=0, keepdims=True)
        small_ref[2:3, :] += jnp.sum(dx2 * o_ref[...].astype(F32), axis=0, keepdims=True)

    tok = pl.BlockSpec((tm, D), lambda i: (i, 0))
    wide = pl.BlockSpec((tm, D_FF), lambda i: (i, 0))
    return pl.pallas_call(
        body, name="mlp_bwd", grid=(s // tm,),
        in_specs=[tok, wide, tok, tok, tok,
                  pl.BlockSpec((8, D), lambda i: (0, 0)), pl.BlockSpec((16, D), lambda i: (0, 0)),
                  _resident((D, D_FF)), _resident((D_FF, D))],
        out_specs=[wide, tok, tok, pl.BlockSpec((8, D), lambda i: (0, 0))],
        out_shape=[jax.ShapeDtypeStruct((s, D_FF), BF16), jax.ShapeDtypeStruct((s, D), F32),
                   jax.ShapeDtypeStruct((s, D), BF16), jax.ShapeDtypeStruct((8, D), F32)],
        compiler_params=_params(("arbitrary",)),
    )(d_dn, ru, x2, dx3, o, modr, vecs, w_up, w_down)


def _branch_bwd(do, proj, ba, bb, w_a, w_b, w_out, dep):
    s = do.shape[0]
    tm = min(TM_BRANCH, s)

    def body(do_ref, ga_ref, gb_ref, ba_ref, bb_ref, wa_ref, wb_ref, wo_ref, dep_ref,
             dba_ref, dbb_ref, dg_ref, dza_ref, dpooled_ref):
        dmerged = _dot_nt(do_ref[...], wo_ref[...])
        sa = _sigmoid(ga_ref[...])
        sb = _sigmoid(gb_ref[...])
        dba = (dmerged * sa).astype(BF16)
        dbb = (dmerged * sb).astype(BF16)
        dba_ref[...] = dba
        dbb_ref[...] = dbb
        dg_ref[:, :D] = (dmerged * ba_ref[...].astype(F32) * sa * (1.0 - sa)).astype(BF16)
        dg_ref[:, D:] = (dmerged * bb_ref[...].astype(F32) * sb * (1.0 - sb)).astype(BF16)
        dza_ref[...] = _dot_nt(dba, wa_ref[...])
        dpooled_ref[...] = _dot_nt(dbb, wb_ref[...])

    tok = pl.BlockSpec((tm, D), lambda i: (i, 0))
    wspec = pl.BlockSpec((D, D), lambda i: (0, 0))
    sd = lambda dt: jax.ShapeDtypeStruct((s, D), dt)
    return pl.pallas_call(
        body, name="branch_bwd", grid=(s // tm,),
        in_specs=[tok, pl.BlockSpec((tm, D), lambda i: (i, 3)), pl.BlockSpec((tm, D), lambda i: (i, 4)),
                  tok, tok, wspec, wspec, wspec, pl.BlockSpec(memory_space=pl.ANY)],
        out_specs=[tok, tok, pl.BlockSpec((tm, 2 * D), lambda i: (i, 0)), tok, tok],
        out_shape=[sd(BF16), sd(BF16), jax.ShapeDtypeStruct((s, 2 * D), BF16), sd(F32), sd(F32)],
        compiler_params=_params(("parallel",)),
    )(do, proj, proj, ba, bb, w_a, w_b, w_out, dep)


def _mix_bwd(dza, dpooled, proj, xr, hr, p, gates, dgates, vecs, w_rg_a, w_rg_x, w_pool, dep):
    s = xr.shape[0]
    tm = min(TM_MIX, s)
    nb = s // tm

    def body(dza_ref, dpooled_ref, xh_ref, x_ref, y_ref, xr_ref, hh_ref, hr_ref, p_ref,
             a_ref, mult_ref, ra_ref, ri_ref, dg_ref, vec_ref, wa_ref, wx_ref, wp_ref, dep_ref,
             dproj_ref, dwa_ref, dwx_ref, dwp_ref, small_ref,
             scan_carry, dxr_carry, q_carry, scan_scr, dwa_acc, dwx_acc, dwp_acc):
        i = pl.program_id(0)
        bi = nb - 1 - i
        first_t = bi == 0

        @pl.when(i == 0)
        def _():
            scan_carry[...] = jnp.zeros_like(scan_carry)
            dxr_carry[...] = jnp.zeros_like(dxr_carry)
            q_carry[...] = jnp.zeros_like(q_carry)
            dwa_acc[...] = jnp.zeros_like(dwa_acc)
            dwx_acc[...] = jnp.zeros_like(dwx_acc)
            dwp_acc[...] = jnp.zeros_like(dwp_acc)
            small_ref[...] = jnp.zeros_like(small_ref)

        row = lax.broadcasted_iota(jnp.int32, (tm, GW), 0)
        is_t0 = jnp.logical_and(first_t, row == 0)
        head_t = (lax.broadcasted_iota(jnp.int32, (HALO_U, GW), 0) + 1).astype(F32)
        colsum = lambda v: jnp.sum(v, axis=0, keepdims=True)
        for g in range(N_GROUPS):
            cs = slice(g * GW, (g + 1) * GW)
            vec = vec_ref[:, cs]
            xr = xr_ref[:, cs]
            hr = hr_ref[:, cs]
            dza = dza_ref[:, cs]
            ga, dga = _gelu_and_grad(y_ref[:, cs])
            dproj_ref[:, D + g * GW:D + (g + 1) * GW] = (dza * hr * dga).astype(BF16)
            dhr = dza * ga
            a = a_ref[:, cs]
            mult = mult_ref[:, cs]
            ra = ra_ref[:, cs].astype(F32)
            ri = ri_ref[:, cs].astype(F32)
            sp = _softplus(vec[V_A_PARAM:V_A_PARAM + 1])
            m = jnp.where(row == tm - 1, 1.0, _shift_up(a, 1))
            gsum = _scan_up(m, dhr, scan_carry[0:1, cs], scan_scr)
            scan_carry[0:1, cs] = a[0:1, :] * gsum[0:1, :]
            hh = jnp.where(first_t, 0.0, hh_ref[:, cs])
            hprev = _shift_down(jnp.concatenate([hh, hr], axis=0), 1)[8:]
            da = gsum * hprev
            dmult = jnp.where(is_t0, 0.0, gsum * xr * ri)
            dlog_a = da * a - dmult * a * a / mult
            dri = gsum * xr * mult
            dxr = gsum * ri * mult
            small_ref[7:8, cs] += colsum((-C_RG) * ra * dlog_a)
            dpa = (((-C_RG) * sp) * dlog_a * ra * (1.0 - ra))
            dpx = dri * ri * (1.0 - ri)
            small_ref[5:6, cs] += colsum(dpa)
            small_ref[6:7, cs] += colsum(dpx)
            dpa = dpa.astype(BF16)
            dpx = dpx.astype(BF16)
            xrb = xr.astype(BF16)
            dwa_acc[g] += _dot_tn(xrb, dpa)
            dwx_acc[g] += _dot_tn(xrb, dpx)
            dxr = dxr + _dot_nt(dpa, wa_ref[g]) + _dot_nt(dpx, wx_ref[g])
            small_ref[4:5, cs] += colsum(dxr)
            xh = jnp.where(first_t, 0.0, xh_ref[:, cs])
            taps = _conv_taps(jnp.concatenate([xh, x_ref[:, cs]], axis=0))
            dxr_ext = jnp.concatenate([dxr, dxr_carry[:, cs]], axis=0)
            dx = vec[V_CONV_W + 3:V_CONV_W + 4] * dxr
            for j in range(4):
                small_ref[j:j + 1, cs] += colsum(dxr * taps[j])
                if j < 3:
                    dx = dx + vec[V_CONV_W + j:V_CONV_W + j + 1] * _shift_up(dxr_ext, 3 - j)[:tm]
            dxr_carry[:, cs] = dxr[0:8, :]
            dproj_ref[:, cs] = dx.astype(BF16)
            pg = p_ref[:, cs]
            dpooled = dpooled_ref[:, cs]
            pb = _dot(pg, wp_ref[g]) + vec[V_B_POOL:V_B_POOL + 1]
            small_ref[9:10, cs] += colsum(dpooled * pb)
            dpb = dpooled * vec[V_POOL_SCALE:V_POOL_SCALE + 1]
            small_ref[8:9, cs] += colsum(dpb)
            dpbb = dpb.astype(BF16)
            dwp_acc[g] += _dot_tn(pg, dpbb)
            dp = _dot_nt(dpbb, wp_ref[g])
            q = _window_mean(dp, POOL_WINDOWS[g], first_t, head_t)
            sm = jnp.concatenate([q, q_carry[:, cs]], axis=0)
            k = 1
            while k < POOL_WINDOWS[g]:
                sm = sm + _shift_up(sm, k)
                k *= 2
            q_carry[:, cs] = q[0:HALO_U, :]
            dproj_ref[:, 2 * D + g * GW:2 * D + (g + 1) * GW] = (sm[:tm] - dp).astype(BF16)
        dproj_ref[:, 3 * D:] = dg_ref[...]

        @pl.when(i == nb - 1)
        def _():
            dwa_ref[...] = dwa_acc[...].astype(BF16)
            dwx_ref[...] = dwx_acc[...].astype(BF16)
            dwp_ref[...] = dwp_acc[...].astype(BF16)

    rev = lambda i: nb - 1 - i
    tok = pl.BlockSpec((tm, D), lambda i: (rev(i), 0))
    col = lambda k: pl.BlockSpec((tm, D), lambda i: (rev(i), k))
    halo8 = lambda k: pl.BlockSpec((8, D), lambda i: (jnp.maximum(rev(i) * (tm // 8) - 1, 0), k))
    wspec = pl.BlockSpec((N_GROUPS, GW, GW), lambda i: (0, 0, 0))
    wshape = jax.ShapeDtypeStruct((N_GROUPS, GW, GW), BF16)
    return pl.pallas_call(
        body, name="mix_bwd", grid=(nb,),
        in_specs=[tok, tok, halo8(0), col(0), col(1), tok, halo8(0), tok, tok, tok, tok, tok, tok,
                  pl.BlockSpec((tm, 2 * D), lambda i: (rev(i), 0)),
                  pl.BlockSpec((16, D), lambda i: (0, 0)), wspec, wspec, wspec, pl.BlockSpec(memory_space=pl.ANY)],
        out_specs=[pl.BlockSpec((tm, D_IN), lambda i: (rev(i), 0)), wspec, wspec, wspec,
                   pl.BlockSpec((16, D), lambda i: (0, 0))],
        out_shape=[jax.ShapeDtypeStruct((s, D_IN), BF16), wshape, wshape, wshape,
                   jax.ShapeDtypeStruct((16, D), F32)],
        scratch_shapes=[pltpu.VMEM((8, D), F32), pltpu.VMEM((8, D), F32), pltpu.VMEM((HALO_U, D), F32),
                        pltpu.VMEM((3, tm, LANES), F32)] + [pltpu.VMEM((N_GROUPS, GW, GW), F32)] * 3,
        compiler_params=_params(("arbitrary",)),
    )(dza, dpooled, proj, proj, proj, xr, hr, hr, p, *gates, dgates, vecs, w_rg_a, w_rg_x, w_pool, dep)


def _proj_bwd(dproj, x, dx2, modr, vecs, w_in, dep):
    s = x.shape[0]
    tm = min(TM_PROJ, s)

    def body(dp_ref, x_ref, dx2_ref, mod_ref, vec_ref, w_ref, dep_ref, gx_ref, small_ref):
        @pl.when(pl.program_id(0) == 0)
        def _():
            small_ref[...] = jnp.zeros_like(small_ref)

        dh1 = None
        for c in range(D_IN // D):
            cs = slice(c * D, (c + 1) * D)
            part = _dot_nt(dp_ref[:, cs], w_ref[:, cs])
            dh1 = part if dh1 is None else dh1 + part
        xv = x_ref[...]
        r1 = lax.rsqrt(jnp.mean(xv * xv, axis=-1, keepdims=True) + EPS)
        xn1 = xv * r1
        gain = vec_ref[V_G1:V_G1 + 1, :] * (1.0 + mod_ref[M_SC1:M_SC1 + 1, :])
        dxn1 = dh1 * gain
        gx_ref[...] = dx2_ref[...] + r1 * (dxn1 - xn1 * jnp.mean(dxn1 * xn1, axis=-1, keepdims=True))
        small_ref[0:1, :] += jnp.sum(dh1, axis=0, keepdims=True)
        small_ref[1:2, :] += jnp.sum(dh1 * xn1, axis=0, keepdims=True)

    tok = pl.BlockSpec((tm, D), lambda i: (i, 0))
    return pl.pallas_call(
        body, name="proj_bwd", grid=(s // tm,),
        in_specs=[pl.BlockSpec((tm, D_IN), lambda i: (i, 0)), tok, tok,
                  pl.BlockSpec((8, D), lambda i: (0, 0)), pl.BlockSpec((16, D), lambda i: (0, 0)),
                  _resident((D, D_IN)), pl.BlockSpec(memory_space=pl.ANY)],
        out_specs=[tok, pl.BlockSpec((8, D), lambda i: (0, 0))],
        out_shape=[jax.ShapeDtypeStruct((s, D), F32), jax.ShapeDtypeStruct((8, D), F32)],
        compiler_params=_params(("arbitrary",)),
    )(dproj, x, dx2, modr, vecs, w_in, dep)


def _wgrad(a, b, name, square_a=False, dep=None):
    s, ka = a.shape
    n = b.shape[1]
    tka = ka if ka <= 1024 else ka // 2
    tn = n if n <= 1024 else n // 2
    ts = min(TS_WGRAD, s)
    ns = s // ts
    nc = 512
    deps = [] if dep is None else [dep]

    def body(a_ref, b_ref, *refs):
        out_ref, acc_ref = refs[-2:]
        t = pl.program_id(2)

        @pl.when(t == 0)
        def _():
            acc_ref[...] = jnp.zeros_like(acc_ref)

        av = a_ref[...]
        if square_a:
            af = av.astype(F32)
            av = (af * af).astype(BF16)
        for c in range(tn // nc):
            cs = slice(c * nc, (c + 1) * nc)
            acc_ref[:, cs] += _dot_tn(av, b_ref[:, cs])

        @pl.when(t == ns - 1)
        def _():
            out_ref[...] = acc_ref[...].astype(BF16)

    return pl.pallas_call(
        body, name=name, grid=(ka // tka, n // tn, ns),
        in_specs=[pl.BlockSpec((ts, tka), lambda i, j, t: (t, i)),
                  pl.BlockSpec((ts, tn), lambda i, j, t: (t, j))] + [pl.BlockSpec(memory_space=pl.ANY)] * len(deps),
        out_specs=pl.BlockSpec((tka, tn), lambda i, j, t: (i, j)),
        out_shape=jax.ShapeDtypeStruct((ka, n), BF16),
        scratch_shapes=[pltpu.VMEM((tka, tn), F32)],
        compiler_params=_params(("parallel", "parallel", "arbitrary")),
    )(a, b, *deps)


def _window(ref, kind, idx, size):
    start = pl.multiple_of(idx * size, size)
    if kind == 0:
        return ref.at[pl.ds(start, size)]
    if kind == 1:
        return ref.at[:, pl.ds(start, size)]
    return ref.at[:, :, pl.ds(start, size)]


def _mesh_place():
    x, y, c = lax.axis_index("x"), lax.axis_index("y"), lax.axis_index("c")
    return x, y, c, 4 * x + 2 * y + c


def _peer(x, y, c, q):
    px = 1 - x if q & 4 else x
    py = 1 - y if q & 2 else y
    pc = 1 - c if q & 1 else c
    return (px, py, pc), 4 * px + 2 * py + pc


def _gather_to_all(srcs, dsts, kinds, send_sems, recv_sems, local_sems, first):
    x, y, c, me = _mesh_place()
    sends, recvs, locals_ = [], [], []
    for k, (src, dst, kind) in enumerate(zip(srcs, dsts, kinds)):
        size = src.shape[kind]
        mine = _window(dst, kind, me, size)
        lc = pltpu.make_async_copy(src, mine, local_sems.at[first + k])
        lc.start()
        locals_.append(lc)
        for q in range(1, N_DEV):
            peer, peer_idx = _peer(x, y, c, q)
            sems = dict(send_sem=send_sems.at[first + k, q], recv_sem=recv_sems.at[first + k, q],
                        device_id=peer, device_id_type=MESH)
            cp = pltpu.make_async_remote_copy(src_ref=src, dst_ref=mine, **sems)
            cp.start()
            sends.append(cp)
            recvs.append(pltpu.make_async_remote_copy(src_ref=src, dst_ref=_window(dst, kind, peer_idx, size), **sems))
    for cp in recvs:
        cp.wait_recv()
    for cp in sends:
        cp.wait_send()
    for lc in locals_:
        lc.wait()


_HBM = pl.BlockSpec(memory_space=pltpu.HBM)
_SEM = pl.BlockSpec(memory_space=pltpu.SEMAPHORE)
_EFFECT = pltpu.SideEffectType.DATAFLOW_SIDE_EFFECTING


N_NEAR = 4


def _near(x, y, c):
    out = [((x, y, 1 - c), 4 * x + 2 * y + 1 - c)]
    for j in (1, 2, 3):
        px = 1 - x if j & 2 else x
        py = 1 - y if j & 1 else y
        out.append(((px, py, c), 4 * px + 2 * py + c))
    return out


def _remote(src, dst, send_sems, recv_sems, slot, device):
    return pltpu.make_async_remote_copy(src_ref=src, dst_ref=dst, send_sem=send_sems.at[slot], recv_sem=recv_sems.at[slot],
                                        device_id=device, device_id_type=MESH)


def _split_call(name, arrays, sems_in, n_new_sems, after, emit):
    na, ns, nn = len(arrays), len(sems_in), len(n_new_sems)

    def body(*refs):
        emit(refs[:na], refs[na:na + ns], refs[na + ns + 1:na + ns + 1 + nn])
        refs[-1][...] = jnp.zeros_like(refs[-1])

    outs = pl.pallas_call(
        body, name=name,
        out_shape=(*[pltpu.SemaphoreType.DMA((m,)) for m in n_new_sems],
                   *[pltpu.HBM(a.shape, a.dtype) for a in arrays], jax.ShapeDtypeStruct((8, 128), F32)),
        in_specs=[_HBM] * na + [_SEM] * ns + [pl.BlockSpec(memory_space=pl.ANY)],
        out_specs=(*[_SEM] * nn, *[_HBM] * na, pl.BlockSpec(memory_space=pltpu.VMEM)),
        input_output_aliases={i: nn + i for i in range(na)},
        compiler_params=pltpu.CompilerParams(has_side_effects=_EFFECT),
    )(*[pltpu.with_memory_space_constraint(a, pltpu.HBM) for a in arrays], *sems_in, after)
    return list(outs[:nn]), list(outs[nn:nn + na]), outs[-1]


class _Gather:
    def __init__(self, shards, kinds, after, name):
        self.n, self.kinds, self.name = len(shards), kinds, name
        self.sizes = [s.shape[k] for s, k in zip(shards, kinds)]
        n = self.n
        lands = []
        for s, k in zip(shards, kinds):
            dims = list(s.shape)
            dims[k] *= N_DEV
            lands.append(lax.empty(tuple(dims), s.dtype))

        def emit(arr, _, new):
            x, y, c, me = _mesh_place()
            for k in range(n):
                pltpu.make_async_copy(arr[k], _window(arr[n + k], kinds[k], me, self.sizes[k]), new[2].at[k]).start()
            for k in range(n):
                mine = _window(arr[n + k], kinds[k], me, self.sizes[k])
                for j, (dev, _) in enumerate(_near(x, y, c)):
                    _remote(arr[k], mine, new[0], new[1], k * N_NEAR + j, dev).start()

        self.sems, self.arrays, self.token = _split_call(name + "_start", [*shards, *lands], [],
                                                         [n * N_NEAR, n * N_NEAR, n], after, emit)
        self.passed_on = {}

    def forward(self, after, chips=(1, 2, 3), tag=""):
        n, kinds, sizes = self.n, self.kinds, self.sizes

        def emit(arr, old, new):
            x, y, c, _ = _mesh_place()
            near = _near(x, y, c)
            for k in range(n):
                for j in chips:
                    dev, idx = near[j]
                    landed = _window(arr[n + k], kinds[k], idx, sizes[k])
                    _remote(arr[k], landed, old[0], old[1], k * N_NEAR + j, dev).wait_recv()
                    _remote(landed, landed, new[0], new[1], k * N_NEAR + j, near[0][0]).start()

        self.passed_on[chips], self.arrays, self.token = _split_call(
            self.name + "_forward" + tag, self.arrays, self.sems, [n * N_NEAR] * 2, after, emit)

    def finish(self, after, chips=(1, 2, 3), own_chip=True, tag=""):
        n, kinds, sizes = self.n, self.kinds, self.sizes

        def emit(arr, old, _):
            x, y, c, me = _mesh_place()
            near = _near(x, y, c)
            other_core = near[0][0]
            for k in range(n):
                win = lambda idx: _window(arr[n + k], kinds[k], idx, sizes[k])
                if own_chip:
                    pltpu.make_async_copy(arr[k], win(me), old[2].at[k]).wait()
                    _remote(arr[k], win(me), old[0], old[1], k * N_NEAR, other_core).wait_send()
                    _remote(arr[k], win(near[0][1]), old[0], old[1], k * N_NEAR, other_core).wait_recv()
                for j in chips:
                    _remote(arr[k], win(me), old[0], old[1], k * N_NEAR + j, near[j][0]).wait_send()
                for j in chips:
                    idx = near[j][1]
                    _remote(win(idx), win(idx), old[3], old[4], k * N_NEAR + j, other_core).wait_send()
                    _remote(arr[k], win(idx + 1 - 2 * c), old[3], old[4], k * N_NEAR + j, other_core).wait_recv()

        _, self.arrays, _ = _split_call(self.name + "_finish" + tag, self.arrays, [*self.sems, *self.passed_on[chips]],
                                        [], after, emit)
        return self.arrays[n:]


class _Scatter:
    def __init__(self, partials, kinds, after, name):
        self.n, self.kinds, self.name, self.partials = len(partials), kinds, name, partials
        self.sizes = [p.shape[k] // N_DEV for p, k in zip(partials, kinds)]
        n, sizes = self.n, self.sizes
        self.slot_shapes = []
        for p, k, size in zip(partials, kinds, sizes):
            dims = list(p.shape)
            dims[k] = size
            self.slot_shapes.append((N_NEAR, *dims))
        slots = [lax.empty(sh, p.dtype) for sh, p in zip(self.slot_shapes, partials)]

        def emit(arr, _, new):
            x, y, c, _ = _mesh_place()
            near = _near(x, y, c)
            for k in range(n):
                for j in range(N_NEAR):
                    owner = near[j][1] if j == 0 else near[j][1] + 1 - 2 * c
                    _remote(_window(arr[k], kinds[k], owner, sizes[k]), arr[n + k].at[j], new[0], new[1],
                            k * N_NEAR + j, near[0][0]).start()

        self.sems, self.arrays, self.token = _split_call(name + "_start", [*partials, *slots], [], [n * N_NEAR] * 2,
                                                         after, emit)

    def combine_and_send(self, own4, after):
        n, kinds, sizes = self.n, self.kinds, self.sizes

        def emit_wait(arr, old, _):
            x, y, c, _ = _mesh_place()
            near = _near(x, y, c)
            for k in range(n):
                for j in range(N_NEAR):
                    owner = near[j][1] if j == 0 else near[j][1] + 1 - 2 * c
                    cp = _remote(_window(arr[k], kinds[k], owner, sizes[k]), arr[n + k].at[j], old[0], old[1],
                                 k * N_NEAR + j, near[0][0])
                    cp.wait_send()
                    cp.wait_recv()

        _, arrays, _ = _split_call(self.name + "_landed", self.arrays, self.sems, [], after, emit_wait)
        chip_sums = _chip_sums(arrays[:n], arrays[n:], kinds, sizes, own4, self.name + "_combine")
        arrivals = [lax.empty((N_NEAR - 1, *sh[1:]), p.dtype) for sh, p in zip(self.slot_shapes, self.partials)]

        def emit_send(arr, _, new):
            x, y, c, _ = _mesh_place()
            near = _near(x, y, c)
            for k in range(n):
                for j in (1, 2, 3):
                    _remote(arr[k].at[j], arr[n + k].at[j - 1], new[0], new[1], k * N_NEAR + j, near[j][0]).start()

        self.sems, self.arrays, self.token = _split_call(self.name + "_send", [*chip_sums, *arrivals], [],
                                                         [n * N_NEAR] * 2, own4, emit_send)

    def finish(self, after):
        n = self.n

        def emit(arr, old, _):
            x, y, c, _ = _mesh_place()
            near = _near(x, y, c)
            for k in range(n):
                for j in (1, 2, 3):
                    cp = _remote(arr[k].at[j], arr[n + k].at[j - 1], old[0], old[1], k * N_NEAR + j, near[j][0])
                    cp.wait_send()
                    cp.wait_recv()

        _, arrays, _ = _split_call(self.name + "_finish", self.arrays, self.sems, [], after, emit)
        return arrays[:n], arrays[n:]


def _chip_sums(partials, slots, kinds, sizes, own4, name):
    n = len(partials)

    def body(own_ref, *refs):
        for k in range(n):
            refs[2 * n + k][...] = (refs[k][...].astype(F32) + refs[n + k][...].astype(F32)).astype(BF16)

    in_specs, slot_specs = [], []
    for p, s, kind, size in zip(partials, slots, kinds, sizes):
        block = list(p.shape)
        block[kind] = size
        nd = len(block)
        in_specs.append(pl.BlockSpec(tuple(block), functools.partial(
            lambda j, own, kind, nd: tuple(own[j] if d == kind else 0 for d in range(nd)), kind=kind, nd=nd)))
        slot_specs.append(pl.BlockSpec((None, *block), functools.partial(
            lambda j, own, nd: (j,) + (0,) * nd, nd=nd)))
    return pl.pallas_call(
        body, name=name,
        grid_spec=pltpu.PrefetchScalarGridSpec(num_scalar_prefetch=1, grid=(N_NEAR,),
                                               in_specs=in_specs + slot_specs, out_specs=slot_specs),
        out_shape=[jax.ShapeDtypeStruct(s.shape, s.dtype) for s in slots],
        compiler_params=_params(("arbitrary",)),
    )(own4, *partials, *slots)


def _to_bf16(arrays):
    def body(*refs):
        for src, dst in zip(refs[:len(arrays)], refs[len(arrays):]):
            dst[...] = src[...].astype(BF16)

    return pl.pallas_call(body, name="to_bf16", out_shape=[jax.ShapeDtypeStruct(a.shape, BF16) for a in arrays],
                          compiler_params=pltpu.CompilerParams(vmem_limit_bytes=V7X_VMEM_LIMIT))(*arrays)


def _silu(c):
    return c * _sigmoid_tail(c)


def _conditioning(c_rows8, conv_w_rows8, w_ada, b_ada_cols):
    n_ada = w_ada.shape[1]

    def body(c_ref, cw_ref, w_ref, b_ref, c_all_ref, cw_all_ref, mod_ref, part_ref, send_sems, recv_sems, local_sems):
        _gather_to_all([c_ref, cw_ref], [c_all_ref, cw_all_ref], [0, 1], send_sems, recv_sems, local_sems, 0)
        pick = (lax.broadcasted_iota(jnp.int32, (N_DEV, N_DEV * 8), 1)
                == 8 * lax.broadcasted_iota(jnp.int32, (N_DEV, N_DEV * 8), 0)).astype(F32)
        c_all = jnp.dot(pick, c_all_ref[...], preferred_element_type=F32, precision=lax.Precision.HIGHEST)
        part_ref[...] = jnp.dot(_silu(c_all), w_ref[...], preferred_element_type=F32,
                                precision=lax.Precision.HIGHEST) + b_ref[...]
        _gather_to_all([part_ref], [mod_ref], [0], send_sems, recv_sems, local_sems, 2)

    vmem = pl.BlockSpec(memory_space=pltpu.VMEM)
    return pl.pallas_call(
        body, name="conditioning",
        in_specs=[vmem] * 4, out_specs=[vmem] * 3,
        out_shape=[jax.ShapeDtypeStruct((N_DEV * 8, D), F32), jax.ShapeDtypeStruct((8, D), F32),
                   jax.ShapeDtypeStruct((N_DEV * N_DEV, n_ada), F32)],
        scratch_shapes=[pltpu.VMEM((N_DEV, n_ada), F32), pltpu.SemaphoreType.DMA((3, N_DEV)),
                        pltpu.SemaphoreType.DMA((3, N_DEV)), pltpu.SemaphoreType.DMA((3,))],
        compiler_params=pltpu.CompilerParams(vmem_limit_bytes=V7X_VMEM_LIMIT),
    )(c_rows8, conv_w_rows8, w_ada, b_ada_cols)


def _adam(w, g, m, v):
    m = ADAM_B1 * m + (1.0 - ADAM_B1) * g
    v = ADAM_B2 * v + (1.0 - ADAM_B2) * (g * g)
    m_hat = m / (1.0 - ADAM_B1 ** ADAM_STEP)
    v_hat = v / (1.0 - ADAM_B2 ** ADAM_STEP)
    delta = -ADAM_LR * (m_hat / (jnp.sqrt(v_hat) + ADAM_EPS) + ADAM_WD * w)
    return delta, m, v


def _ada_bwd_adam(c_all, dmod_cols, w, m, v):
    def body(c_ref, d_ref, w_ref, m_ref, v_ref, g_ref, delta_ref, nm_ref, nv_ref):
        g = lax.dot_general(_silu(c_ref[...]), d_ref[...], (((0,), (0,)), ((), ())),
                            preferred_element_type=F32, precision=lax.Precision.HIGHEST)
        g_ref[...] = g
        delta_ref[...], nm_ref[...], nv_ref[...] = _adam(w_ref[...], g, m_ref[...], v_ref[...])

    sd = jax.ShapeDtypeStruct(w.shape, F32)
    return pl.pallas_call(body, name="ada_bwd_adam", out_shape=[sd] * 4,
                          compiler_params=pltpu.CompilerParams(vmem_limit_bytes=V7X_VMEM_LIMIT),
                          )(c_all, dmod_cols, w, m, v)


def _adam_group(chip_sums, arrivals, ws, ms, vs, n_tiles, name):
    n = len(ws)

    def body(*refs):
        for k in range(n):
            c_ref, a_ref, w_ref, m_ref, v_ref = (refs[j * n + k] for j in range(5))
            g_ref, delta_ref, nm_ref, nv_ref = (refs[(5 + j) * n + k] for j in range(4))
            g = c_ref[...].astype(F32)
            for j in range(N_NEAR - 1):
                g = g + a_ref[j].astype(F32)
            g_ref[...] = g
            delta_ref[...], nm_ref[...], nv_ref[...] = _adam(w_ref[...], g, m_ref[...], v_ref[...])

    tiles = [(w.shape[0] // n_tiles, w.shape[1]) for w in ws]
    blk = [pl.BlockSpec(t, lambda i: (i, 0)) for t in tiles]
    return pl.pallas_call(
        body, name=name, grid=(n_tiles,),
        in_specs=[pl.BlockSpec((None, *t), lambda i: (0, i, 0)) for t in tiles]
        + [pl.BlockSpec((N_NEAR - 1, *t), lambda i: (0, i, 0)) for t in tiles] + blk * 3,
        out_specs=blk * 4, out_shape=[jax.ShapeDtypeStruct(w.shape, F32) for w in ws] * 4,
        compiler_params=_params(("parallel",)),
    )(*chip_sums, *arrivals, *ws, *ms, *vs)


N_SMALL = 40
N_SMALL_PARAMS = 11


def _pack_vecs(conv_w_full, rows):
    def body(cw_ref, *refs):
        out = refs[-1]
        out[...] = jnp.zeros_like(out)
        out[0:4, :] = cw_ref[0:4, :]
        for r, ref in enumerate(refs[:-1]):
            out[4 + r:5 + r, :] = ref[...]

    return pl.pallas_call(body, name="pack_vecs", out_shape=jax.ShapeDtypeStruct((16, D), F32))(conv_w_full, *rows)


def _small_finish(gathered, mod_all, vecs, ws, ms, vs):
    n = N_SMALL_PARAMS

    def body(g_ref, mod_ref, vec_ref, *refs):
        w_refs, m_refs, v_refs = refs[:n], refs[n:2 * n], refs[2 * n:3 * n]
        outs = refs[3 * n:]
        g1 = vec_ref[V_G1:V_G1 + 1, :]
        g2 = vec_ref[V_G2:V_G2 + 1, :]
        zero = jnp.zeros((1, D), F32)
        dg1, dg2, dgf, loss_lanes = zero, zero, zero, zero
        mixer = jnp.zeros((16, D), F32)
        db_ada = jnp.zeros((6, D), F32)
        for b in range(N_DEV):
            gb = g_ref[b]
            mod = mod_ref[b]
            q1 = gb[33:34]
            q2 = gb[9:10]
            dmod = jnp.concatenate([gb[32:33], q1 * g1, gb[10:11], gb[8:9], q2 * g2, gb[1:2]], axis=0)
            outs[4 * n][b] = dmod
            db_ada = db_ada + dmod
            dg1 = dg1 + q1 * (1.0 + mod[M_SC1:M_SC1 + 1])
            dg2 = dg2 + q2 * (1.0 + mod[M_SC2:M_SC2 + 1])
            dgf = dgf + gb[0:1]
            loss_lanes = loss_lanes + gb[2:3]
            mixer = mixer + gb[16:32]
        d_a_param = mixer[7:8] * _sigmoid_tail(vec_ref[V_A_PARAM:V_A_PARAM + 1, :])
        grads = [dg1, dg2, mixer[4:5], mixer[5:6], mixer[6:7], d_a_param, mixer[8:9], mixer[9:10], dgf,
                 db_ada, mixer[0:4]]
        for k in range(n):
            outs[k][...] = grads[k]
            outs[n + k][...], outs[2 * n + k][...], outs[3 * n + k][...] = _adam(
                w_refs[k][...], grads[k], m_refs[k][...], v_refs[k][...])
        outs[4 * n + 1][...] = jnp.broadcast_to(jnp.sum(loss_lanes, axis=1, keepdims=True), (8, 128))

    shapes = [jax.ShapeDtypeStruct(w.shape, F32) for w in ws]
    return pl.pallas_call(
        body, name="small_finish",
        out_shape=shapes * 4 + [jax.ShapeDtypeStruct((N_DEV, 6, D), F32), jax.ShapeDtypeStruct((8, 128), F32)],
    )(gathered, mod_all, vecs, *ws, *ms, *vs)


def _pad_rows(a, rows):
    return jnp.pad(a, ((0, rows - a.shape[0]), (0, 0)))


def kernel(x, c, norm_mix_g, norm_mlp_g, w_ada, b_ada, w_in, conv_w, conv_b, w_rg_a, b_rg_a, w_rg_x, b_rg_x, a_param, w_branch_a, w_pool, b_pool, pool_scale, w_branch_b, w_out, w_up, w_down, final_g, loss_target, m_norm_mix_g, m_norm_mlp_g, m_w_ada, m_b_ada, m_w_in, m_conv_w, m_conv_b, m_w_rg_a, m_b_rg_a, m_w_rg_x, m_b_rg_x, m_a_param, m_w_branch_a, m_w_pool, m_b_pool, m_pool_scale, m_w_branch_b, m_w_out, m_w_up, m_w_down, m_final_g, v_norm_mix_g, v_norm_mlp_g, v_w_ada, v_b_ada, v_w_in, v_conv_w, v_conv_b, v_w_rg_a, v_b_rg_a, v_w_rg_x, v_b_rg_x, v_a_param, v_w_branch_a, v_w_pool, v_b_pool, v_pool_scale, v_w_branch_b, v_w_out, v_w_up, v_w_down, v_final_g):
    me = 4 * lax.axis_index("x") + 2 * lax.axis_index("y") + lax.axis_index("c")
    s = x.shape[1]
    x2d = x.reshape(s, D)
    target = loss_target.reshape(s, D)
    n_ada = w_ada.shape[2]

    sharded = dict(w_in=(w_in[0], 1), w_up=(w_up[0], 1), w_down=(w_down[0], 0), w_branch_a=(w_branch_a[0], 0),
                   w_branch_b=(w_branch_b[0], 0), w_out=(w_out[0], 0), w_rg_a=(w_rg_a[0], 1), w_rg_x=(w_rg_x[0], 1),
                   w_pool=(w_pool[0], 1))
    kind = {k: v[1] for k, v in sharded.items()}
    shard = dict(zip(sharded, _to_bf16([v[0] for v in sharded.values()])))

    b_ada_cols = lax.dynamic_slice(b_ada, (0, me * n_ada), (1, n_ada))
    c_rows, conv_w_full, mod_parts = _conditioning(_pad_rows(c, 8), _pad_rows(conv_w[0], 8), w_ada[0], b_ada_cols)
    c_all = c_rows.reshape(N_DEV, 8, D)[:, 0, :]

    first_names = ["w_in", "w_rg_a", "w_rg_x", "w_pool"]
    branch_names = ["w_branch_a", "w_branch_b", "w_out"]
    mlp_names = ["w_up", "w_down"]

    def gather(group, after, name):
        return _Gather([shard[k] for k in group], [kind[k] for k in group], after, name)

    g_first = gather(first_names, mod_parts, "gather_first")
    g_branch = gather(branch_names, g_first.token, "gather_branch")
    g_mlp = gather(mlp_names, g_branch.token, "gather_mlp")

    mod_all = jnp.transpose(mod_parts.reshape(N_DEV, N_DEV, n_ada), (1, 0, 2)).reshape(N_DEV, 6, D)
    mod_all = jnp.pad(mod_all, ((0, 0), (0, 2), (0, 0)))
    modr = lax.dynamic_index_in_dim(mod_all, me, 0, keepdims=False)
    vecs = _pack_vecs(conv_w_full, [conv_b, b_rg_a, b_rg_x, a_param, b_pool, pool_scale,
                                    norm_mix_g, norm_mlp_g, final_g.reshape(1, D)])
    y_pos = lax.axis_index("y")
    across_x, across_y = (2,), (1, 3)
    g_first.forward(g_mlp.token, across_x, "_x")
    w_in_half = g_first.finish(g_first.token, across_x, True, "_x")[0]
    proj, h1 = _proj_fwd(x2d, modr, vecs, w_in_half, jnp.stack([y_pos, 2 + y_pos]).astype(jnp.int32), None,
                         "proj_fwd_x")
    g_first.forward(h1, across_y, "_y")
    wg = dict(zip(first_names, g_first.finish(g_first.token, across_y, False, "_y")))
    proj, _ = _proj_fwd(x2d, modr, vecs, wg["w_in"], jnp.stack([1 - y_pos, 3 - y_pos]).astype(jnp.int32), proj,
                        "proj_fwd_y")
    g_branch.forward(proj)
    xr, hr, za, p, pooled, *gates = _mix_fwd(proj, vecs, wg["w_rg_a"], wg["w_rg_x"], wg["w_pool"],
                                             dep=g_branch.token)
    g_mlp.forward(za)
    wg.update(zip(branch_names, g_branch.finish(g_mlp.token)))
    ba, bb, merged, o, x2, h2 = _branch_fwd(za, pooled, proj, x2d, modr, vecs,
                                            wg["w_branch_a"], wg["w_branch_b"], wg["w_out"])
    wg.update(zip(mlp_names, g_mlp.finish(h2)))
    ru, dx3, d_dn, small_f = _mlp_fwd(h2, x2, target, modr, vecs, wg["w_up"], wg["w_down"])

    near = _near(lax.axis_index("x"), lax.axis_index("y"), lax.axis_index("c"))
    own4 = jnp.stack([me, near[1][1], near[2][1], near[3][1]]).astype(jnp.int32)

    def scatter(group, partial, after, name):
        return _Scatter([partial[k] for k in group], [kind[k] for k in group], after, name)

    dup, dx2, do, small_m = _mlp_bwd(d_dn, ru, x2, dx3, o, modr, vecs, wg["w_up"], wg["w_down"])
    partial = dict(w_up=_wgrad(h2, dup, "wgrad_up"), w_down=_wgrad(ru, d_dn, "wgrad_down", square_a=True))
    s_mlp = scatter(mlp_names, partial, dx2, "scatter_mlp")

    dba, dbb, dgates, dza, dpooled = _branch_bwd(do, proj, ba, bb, wg["w_branch_a"], wg["w_branch_b"], wg["w_out"],
                                                 dep=s_mlp.token)
    s_mlp.combine_and_send(own4, dza)
    dproj, dw_rg_a, dw_rg_x, dw_pool, small_x = _mix_bwd(dza, dpooled, proj, xr, hr, p, gates, dgates,
                                                         vecs, wg["w_rg_a"], wg["w_rg_x"], wg["w_pool"],
                                                         dep=s_mlp.token)
    partial.update(w_branch_a=_wgrad(za, dba, "wgrad_branch_a"), w_branch_b=_wgrad(pooled, dbb, "wgrad_branch_b"),
                   w_out=_wgrad(merged, do, "wgrad_out"),
                   w_rg_a=dw_rg_a, w_rg_x=dw_rg_x, w_pool=dw_pool)
    mixer_names = ["w_rg_a", "w_rg_x", "w_pool", "w_branch_a", "w_branch_b", "w_out"]
    s_mixer = scatter(mixer_names, partial, s_mlp.token, "scatter_mixer")

    partial["w_in"] = _wgrad(h1, dproj, "wgrad_in", dep=s_mixer.token)
    s_in = scatter(["w_in"], partial, s_mixer.token, "scatter_in")
    s_mixer.combine_and_send(own4, s_in.token)
    s_in.combine_and_send(own4, s_mixer.token)
    grad_x, small_p = _proj_bwd(dproj, x2d, dx2, modr, vecs, wg["w_in"], dep=s_in.token)

    locals_ = dict(w_in=(w_in, m_w_in, v_w_in), w_up=(w_up, m_w_up, v_w_up), w_down=(w_down, m_w_down, v_w_down),
                   w_branch_a=(w_branch_a, m_w_branch_a, v_w_branch_a),
                   w_branch_b=(w_branch_b, m_w_branch_b, v_w_branch_b), w_out=(w_out, m_w_out, v_w_out),
                   w_rg_a=(w_rg_a, m_w_rg_a, v_w_rg_a), w_rg_x=(w_rg_x, m_w_rg_x, v_w_rg_x),
                   w_pool=(w_pool, m_w_pool, v_w_pool))
    res = {}

    def finish(group, exchange, after, n_tiles, name):
        chip_sums, arrivals = exchange.finish(after)
        flat = lambda t: t.reshape(-1, t.shape[-1])
        shapes = [flat(locals_[k][0]).shape for k in group]
        outs = _adam_group([cs.reshape(N_NEAR, *sh) for cs, sh in zip(chip_sums, shapes)],
                           [ar.reshape(N_NEAR - 1, *sh) for ar, sh in zip(arrivals, shapes)],
                           *[[flat(locals_[k][j]) for k in group] for j in range(3)], n_tiles, name)
        for i, k in enumerate(group):
            res[k] = [outs[j * len(group) + i].reshape(locals_[k][0].shape) for j in range(4)]
        return res[group[-1]][0]

    small = jnp.concatenate([small_f, small_m, small_x, small_p], axis=0)
    g_small = _Gather([small], [0], grad_x, "gather_small")
    done = finish(mlp_names, s_mlp, g_small.token, 4, "adam_mlp")
    done = finish(mixer_names, s_mixer, done, 2, "adam_mixer")
    g_small.forward(done)
    done = finish(["w_in"], s_in, g_small.token, 4, "adam_in")
    small_all, = g_small.finish(done)
    small_all = small_all.reshape(N_DEV, N_SMALL, D)

    def embed(cw):
        return lax.dynamic_update_slice(jnp.zeros((4, D), F32), cw[0], (0, me * (D // N_DEV)))

    def smalls(ng, nl, cb, bra, brx, ap, bp, ps, fg, ba_, cw):
        return [ng, nl, cb, bra, brx, ap, bp, ps, fg.reshape(1, D), ba_.reshape(6, D), embed(cw)]

    small_names = ["norm_mix_g", "norm_mlp_g", "conv_b", "b_rg_a", "b_rg_x", "a_param", "b_pool", "pool_scale",
                   "final_g", "b_ada", "conv_w"]
    fin = _small_finish(
        small_all, mod_all, vecs,
        smalls(norm_mix_g, norm_mlp_g, conv_b, b_rg_a, b_rg_x, a_param, b_pool, pool_scale, final_g, b_ada, conv_w),
        smalls(m_norm_mix_g, m_norm_mlp_g, m_conv_b, m_b_rg_a, m_b_rg_x, m_a_param, m_b_pool, m_pool_scale,
               m_final_g, m_b_ada, m_conv_w),
        smalls(v_norm_mix_g, v_norm_mlp_g, v_conv_b, v_b_rg_a, v_b_rg_x, v_a_param, v_b_pool, v_pool_scale,
               v_final_g, v_b_ada, v_conv_w))
    dmod_all, loss_tile = fin[4 * N_SMALL_PARAMS], fin[4 * N_SMALL_PARAMS + 1]
    dmod_cols = lax.dynamic_slice(dmod_all.reshape(N_DEV, 6 * D), (0, me * n_ada), (N_DEV, n_ada))
    res["w_ada"] = [t.reshape(w_ada.shape) for t in _ada_bwd_adam(c_all, dmod_cols, w_ada[0], m_w_ada[0], v_w_ada[0])]

    def final_shape(k, t):
        if k == "final_g":
            return t.reshape(D)
        if k == "b_ada":
            return t.reshape(1, 6 * D)
        if k == "conv_w":
            return lax.dynamic_slice(t, (0, me * (D // N_DEV)), (4, D // N_DEV)).reshape(conv_w.shape)
        return t

    for i, k in enumerate(small_names):
        res[k] = [final_shape(k, fin[which * N_SMALL_PARAMS + i]) for which in range(4)]
    order = ["norm_mix_g", "norm_mlp_g", "w_ada", "b_ada", "w_in", "conv_w", "conv_b", "w_rg_a", "b_rg_a", "w_rg_x",
             "b_rg_x", "a_param", "w_branch_a", "w_pool", "b_pool", "pool_scale", "w_branch_b", "w_out", "w_up",
             "w_down", "final_g"]
    outs = [loss_tile[0, 0], grad_x.reshape(x.shape)]
    for which in range(4):
        for k in order:
            outs.append(res[k][which])
    return tuple(outs)
```

```python
import functools

import jax
import jax.numpy as jnp
from jax import lax
from jax.experimental import pallas as pl
from jax.experimental.pallas import tpu as pltpu

F32 = jnp.float32
BF16 = jnp.bfloat16
MESH = pl.DeviceIdType.MESH

N_DEV = 8
D = 1024
N_GROUPS = 4
GW = D // N_GROUPS
D_IN = 5 * D
D_FF = 4 * D
POOL_WINDOWS = (2, 4, 8, 16)
HALO_X = 8
HALO_U = 16
EPS = 1e-6
C_RG = 8.0
ADAM_LR, ADAM_B1, ADAM_B2, ADAM_EPS, ADAM_WD, ADAM_STEP = 0.001, 0.9, 0.999, 1e-08, 0.01, 10

V7X_VMEM_LIMIT = 56 * 1024 * 1024

V_CONV_W, V_CONV_B, V_B_RG_A, V_B_RG_X, V_A_PARAM, V_B_POOL, V_POOL_SCALE, V_G1, V_G2, V_GF = 0, 4, 5, 6, 7, 8, 9, 10, 11, 12
M_SH1, M_SC1, M_GT1, M_SH2, M_SC2, M_GT2 = 0, 1, 2, 3, 4, 5

TM_PROJ = 512
TM_MIX = 256
TM_BRANCH = 512
TM_MLP = 512
TM_MLP_BWD = 256
TS_WGRAD = 1024


def _params(semantics):
    return pltpu.CompilerParams(dimension_semantics=semantics, vmem_limit_bytes=V7X_VMEM_LIMIT)


def _resident(shape):
    return pl.BlockSpec(shape, lambda *_: (0,) * len(shape), pipeline_mode=pl.Buffered(1))


def _dot(a, b):
    return jnp.dot(a, b, preferred_element_type=F32)


def _dot_nt(a, b):
    return lax.dot_general(a, b, (((1,), (1,)), ((), ())), preferred_element_type=F32)


def _dot_tn(a, b):
    return lax.dot_general(a, b, (((0,), (0,)), ((), ())), preferred_element_type=F32)


def _sigmoid(x):
    return 0.5 * jnp.tanh(0.5 * x) + 0.5


def _sigmoid_tail(x):
    return 1.0 / (1.0 + jnp.exp(-x))


def _gelu_and_grad(x):
    k = 0.7978845608028654
    x2 = x * x
    t = jnp.tanh(k * (x + 0.044715 * x * x2))
    g = 0.5 * x * (1.0 + t)
    dg = 0.5 * (1.0 + t) + 0.5 * x * (1.0 - t * t) * (k * (1.0 + 3.0 * 0.044715 * x2))
    return g, dg


def _softplus(a):
    e = jnp.exp(-jnp.abs(a))
    u = 1.0 + e
    log1p_e = jnp.where(u == 1.0, e, jnp.log(u) * e / jnp.where(u == 1.0, 1.0, u - 1.0))
    return jnp.maximum(a, 0.0) + log1p_e


def _neg_expm1(z):
    series = -(z * (1.0 + z * (0.5 + z * (1.0 / 6.0 + z * (1.0 / 24.0 + z * (1.0 / 120.0))))))
    return jnp.where(z > -0.1, series, 1.0 - jnp.exp(z))


def _shift_down(x, k):
    return pltpu.roll(x, k, 0)


def _shift_up(x, k):
    return pltpu.roll(x, x.shape[0] - k, 0)


def _rglru_gates(xr, w_a, w_x, b_a, b_x, a_param, is_t0):
    xb = xr.astype(BF16)
    ra = _sigmoid(_dot(xb, w_a) + b_a)
    ri = _sigmoid(_dot(xb, w_x) + b_x)
    sp = _softplus(a_param)
    log_a = (-C_RG) * ra * sp
    a = jnp.exp(log_a)
    mult = jnp.where(is_t0, 1.0, jnp.sqrt(_neg_expm1(2.0 * log_a)))
    return ra, ri, sp, a, mult


SUBLANES = 8


LANES = 128


def _scan_strip(a, b, carry, scr, down):
    t = b.shape[0]
    g = t // SUBLANES
    a3 = a.reshape(g, SUBLANES, LANES)
    b3 = b.reshape(g, SUBLANES, LANES)
    sub = lax.broadcasted_iota(jnp.int32, (g, SUBLANES, LANES), 1)
    for k in (1, 2, 4):
        keep = sub >= k if down else sub < SUBLANES - k
        shift = k if down else SUBLANES - k
        b3 = b3 + a3 * jnp.where(keep, pltpu.roll(b3, shift, 1), 0.0)
        a3 = a3 * jnp.where(keep, pltpu.roll(a3, shift, 1), 1.0)
    scr[0] = a3.reshape(t, LANES)
    scr[1] = b3.reshape(t, LANES)
    end_row = SUBLANES - 1 if down else 0
    ag = scr[0, pl.ds(end_row, g, stride=SUBLANES), :]
    bg = scr[1, pl.ds(end_row, g, stride=SUBLANES), :]
    rg = lax.broadcasted_iota(jnp.int32, (g, LANES), 0)
    edge = 0 if down else g - 1
    bg = bg + jnp.where(rg == edge, ag * carry, 0.0)
    k = 1
    while k < g:
        keep = rg >= k if down else rg < g - k
        shift = k if down else g - k
        bg = bg + ag * jnp.where(keep, pltpu.roll(bg, shift, 0), 0.0)
        if 2 * k < g:
            ag = ag * pltpu.roll(ag, shift, 0)
        k *= 2
    entering = jnp.where(rg != edge, pltpu.roll(bg, 1 if down else g - 1, 0), carry)
    for r in range(SUBLANES):
        scr[2, pl.ds(r, g, stride=SUBLANES), :] = entering
    return scr[1] + scr[0] * scr[2], bg[g - 1:g, :]


def _scan_strips(a, b, carry, scr, down):
    outs = [_scan_strip(a[:, c:c + LANES], b[:, c:c + LANES], carry[:, c:c + LANES], scr, down)
            for c in range(0, b.shape[1], LANES)]
    return jnp.concatenate([o[0] for o in outs], axis=1), jnp.concatenate([o[1] for o in outs], axis=1)


def _scan_down(a, b, carry, scr):
    return _scan_strips(a, b, carry, scr, True)


def _scan_up(m, b, carry, scr):
    return _scan_strips(m, b, carry, scr, False)[0]


def _window_mean(sums, window, first_block, head_t):
    scaled = sums * (1.0 / window)
    head = jnp.where(first_block, sums[:HALO_U] / jnp.minimum(head_t, float(window)), scaled[:HALO_U])
    return jnp.concatenate([head, scaled[HALO_U:]], axis=0)


def _conv_taps(x_ext):
    return [_shift_down(x_ext, 3 - j)[HALO_X:] if j < 3 else x_ext[HALO_X:] for j in range(4)]


def _proj_fwd(x, modr, vecs, w_in):
    s = x.shape[0]
    tm = min(TM_PROJ, s)

    def body(x_ref, mod_ref, vec_ref, w_ref, h1_ref, xrnn_ref, u_ref, ga_ref, dga_ref, sa_ref, sb_ref):
        xv = x_ref[...]
        r = lax.rsqrt(jnp.mean(xv * xv, axis=-1, keepdims=True) + EPS)
        gain = vec_ref[V_G1:V_G1 + 1, :] * (1.0 + mod_ref[M_SC1:M_SC1 + 1, :])
        h = (xv * r * gain + mod_ref[M_SH1:M_SH1 + 1, :]).astype(BF16)
        h1_ref[...] = h
        xrnn_ref[...] = _dot(h, w_ref[:, 0:D])
        ga_ref[...], dga_ref[...] = _gelu_and_grad(_dot(h, w_ref[:, D:2 * D]))
        u_ref[...] = _dot(h, w_ref[:, 2 * D:3 * D])
        sa_ref[...] = _sigmoid(_dot(h, w_ref[:, 3 * D:4 * D]))
        sb_ref[...] = _sigmoid(_dot(h, w_ref[:, 4 * D:5 * D]))

    tok = pl.BlockSpec((tm, D), lambda i: (i, 0))
    sd = lambda dt: jax.ShapeDtypeStruct((s, D), dt)
    return pl.pallas_call(
        body, name="proj_fwd", grid=(s // tm,),
        in_specs=[tok, pl.BlockSpec((8, D), lambda i: (0, 0)), pl.BlockSpec((16, D), lambda i: (0, 0)),
                  _resident((D, D_IN))],
        out_specs=[tok] * 7,
        out_shape=[sd(BF16)] + [sd(F32)] * 6,
        compiler_params=_params(("parallel",)),
    )(x, modr, vecs, w_in)


def _mix_fwd(x_rnn, u_pool, ga, vecs, w_rg_a, w_rg_x, w_pool, dep):
    s = x_rnn.shape[0]
    tm = min(TM_MIX, s)
    nb = s // tm

    def body(xh_ref, x_ref, uh_ref, u_ref, ga_ref, vec_ref, wa_ref, wx_ref, wp_ref, dep_ref,
             xr_ref, hr_ref, za_ref, p_ref, pooled_ref, a_ref, mult_ref, ra_ref, ri_ref, carry_ref, scan_scr):
        i = pl.program_id(0)
        first = i == 0

        @pl.when(first)
        def _():
            carry_ref[...] = jnp.zeros_like(carry_ref)

        row = lax.broadcasted_iota(jnp.int32, (tm, GW), 0)
        is_t0 = jnp.logical_and(first, row == 0)
        head_t = (lax.broadcasted_iota(jnp.int32, (HALO_U, GW), 0) + 1).astype(F32)
        for g in range(N_GROUPS):
            cs = slice(g * GW, (g + 1) * GW)
            vec = vec_ref[:, cs]
            xh = jnp.where(first, 0.0, xh_ref[:, cs])
            taps = _conv_taps(jnp.concatenate([xh, x_ref[:, cs]], axis=0))
            xr = vec[V_CONV_B:V_CONV_B + 1]
            for j in range(4):
                xr = xr + vec[V_CONV_W + j:V_CONV_W + j + 1] * taps[j]
            xr_ref[:, cs] = xr
            ra, ri, _, a, mult = _rglru_gates(
                xr, wa_ref[g], wx_ref[g], vec[V_B_RG_A:V_B_RG_A + 1], vec[V_B_RG_X:V_B_RG_X + 1],
                vec[V_A_PARAM:V_A_PARAM + 1], is_t0)
            a_ref[:, cs] = a
            mult_ref[:, cs] = mult
            ra_ref[:, cs] = ra.astype(BF16)
            ri_ref[:, cs] = ri.astype(BF16)
            h, last = _scan_down(a, xr * ri * mult, carry_ref[0:1, cs], scan_scr)
            hr_ref[:, cs] = h
            carry_ref[0:1, cs] = last
            za_ref[:, cs] = (ga_ref[:, cs] * h).astype(BF16)
            uh = jnp.where(first, 0.0, uh_ref[:, cs])
            sm = jnp.concatenate([uh, u_ref[:, cs]], axis=0)
            k = 1
            while k < POOL_WINDOWS[g]:
                sm = sm + _shift_down(sm, k)
                k *= 2
            mean = _window_mean(sm[HALO_U:], POOL_WINDOWS[g], first, head_t)
            p = (mean - u_ref[:, cs]).astype(BF16)
            p_ref[:, cs] = p
            pb = _dot(p, wp_ref[g]) + vec[V_B_POOL:V_B_POOL + 1]
            pooled_ref[:, cs] = (pb * vec[V_POOL_SCALE:V_POOL_SCALE + 1]).astype(BF16)

    tok = pl.BlockSpec((tm, D), lambda i: (i, 0))
    halo = lambda rows: pl.BlockSpec((rows, D), lambda i: (jnp.maximum(i * (tm // rows) - 1, 0), 0))
    wspec = pl.BlockSpec((N_GROUPS, GW, GW), lambda i: (0, 0, 0))
    sd = lambda dt: jax.ShapeDtypeStruct((s, D), dt)
    return pl.pallas_call(
        body, name="mix_fwd", grid=(nb,),
        in_specs=[halo(HALO_X), tok, halo(HALO_U), tok, tok, pl.BlockSpec((16, D), lambda i: (0, 0)),
                  wspec, wspec, wspec, pl.BlockSpec(memory_space=pl.ANY)],
        out_specs=[tok] * 9,
        out_shape=[sd(F32), sd(F32), sd(BF16), sd(BF16), sd(BF16), sd(F32), sd(F32), sd(BF16), sd(BF16)],
        scratch_shapes=[pltpu.VMEM((8, D), F32), pltpu.VMEM((3, tm, LANES), F32)],
        compiler_params=_params(("arbitrary",)),
    )(x_rnn, x_rnn, u_pool, u_pool, ga, vecs, w_rg_a, w_rg_x, w_pool, dep)


def _branch_fwd(za, pooled, sa, sb, x, modr, vecs, w_a, w_b, w_out):
    s = x.shape[0]
    tm = min(TM_BRANCH, s)

    def body(za_ref, pooled_ref, sa_ref, sb_ref, x_ref, mod_ref, vec_ref, wa_ref, wb_ref, wo_ref,
             ba_ref, bb_ref, merged_ref, o_ref, x2_ref, h2_ref):
        ba = _dot(za_ref[...], wa_ref[...])
        bb = _dot(pooled_ref[...], wb_ref[...])
        ba_ref[...] = ba.astype(BF16)
        bb_ref[...] = bb.astype(BF16)
        merged = (sa_ref[...] * ba + sb_ref[...] * bb).astype(BF16)
        merged_ref[...] = merged
        o = _dot(merged, wo_ref[...])
        o_ref[...] = o.astype(BF16)
        x2 = x_ref[...] + mod_ref[M_GT1:M_GT1 + 1, :] * o
        x2_ref[...] = x2
        r = lax.rsqrt(jnp.mean(x2 * x2, axis=-1, keepdims=True) + EPS)
        gain = vec_ref[V_G2:V_G2 + 1, :] * (1.0 + mod_ref[M_SC2:M_SC2 + 1, :])
        h2_ref[...] = (x2 * r * gain + mod_ref[M_SH2:M_SH2 + 1, :]).astype(BF16)

    tok = pl.BlockSpec((tm, D), lambda i: (i, 0))
    wspec = pl.BlockSpec((D, D), lambda i: (0, 0))
    sd = lambda dt: jax.ShapeDtypeStruct((s, D), dt)
    return pl.pallas_call(
        body, name="branch_fwd", grid=(s // tm,),
        in_specs=[tok, tok, tok, tok,
                  tok, pl.BlockSpec((8, D), lambda i: (0, 0)), pl.BlockSpec((16, D), lambda i: (0, 0)),
                  wspec, wspec, wspec],
        out_specs=[tok] * 6,
        out_shape=[sd(BF16), sd(BF16), sd(BF16), sd(BF16), sd(F32), sd(BF16)],
        compiler_params=_params(("parallel",)),
    )(za, pooled, sa, sb, x, modr, vecs, w_a, w_b, w_out)


def _mlp_fwd(h2, x2, target, modr, vecs, w_up, w_down):
    s = x2.shape[0]
    tm = min(TM_MLP, s)

    def body(h2_ref, x2_ref, tgt_ref, mod_ref, vec_ref, wu_ref, wd_ref,
             ru_ref, dx3_ref, ddn_ref, small_ref):
        @pl.when(pl.program_id(0) == 0)
        def _():
            small_ref[...] = jnp.zeros_like(small_ref)

        h2 = h2_ref[...]
        dn = None
        for c in range(D_FF // D):
            cs = slice(c * D, (c + 1) * D)
            ru = jnp.maximum(_dot(h2, wu_ref[:, cs]), 0.0)
            ru_ref[:, cs] = ru.astype(BF16)
            part = _dot((ru * ru).astype(BF16), wd_ref[cs, :])
            dn = part if dn is None else dn + part
        gt2 = mod_ref[M_GT2:M_GT2 + 1, :]
        gf = vec_ref[V_GF:V_GF + 1, :]
        x3 = x2_ref[...] + gt2 * dn
        r3 = lax.rsqrt(jnp.mean(x3 * x3, axis=-1, keepdims=True) + EPS)
        n3 = x3 * r3
        err = n3 * gf - tgt_ref[...]
        dy = err * (1.0 / D)
        dn3 = dy * gf
        dx3 = r3 * (dn3 - n3 * jnp.mean(dn3 * n3, axis=-1, keepdims=True))
        dx3_ref[...] = dx3
        ddn_ref[...] = (dx3 * gt2).astype(BF16)
        small_ref[0:1, :] += jnp.sum(dy * n3, axis=0, keepdims=True)
        small_ref[1:2, :] += jnp.sum(dx3 * dn, axis=0, keepdims=True)
        small_ref[2:3, :] += (0.5 / D) * jnp.sum(err * err, axis=0, keepdims=True)

    tok = pl.BlockSpec((tm, D), lambda i: (i, 0))
    return pl.pallas_call(
        body, name="mlp_fwd", grid=(s // tm,),
        in_specs=[tok, tok, tok,
                  pl.BlockSpec((8, D), lambda i: (0, 0)), pl.BlockSpec((16, D), lambda i: (0, 0)),
                  _resident((D, D_FF)), _resident((D_FF, D))],
        out_specs=[pl.BlockSpec((tm, D_FF), lambda i: (i, 0)), tok, tok,
                   pl.BlockSpec((8, D), lambda i: (0, 0))],
        out_shape=[jax.ShapeDtypeStruct((s, D_FF), BF16), jax.ShapeDtypeStruct((s, D), F32),
                   jax.ShapeDtypeStruct((s, D), BF16), jax.ShapeDtypeStruct((8, D), F32)],
        compiler_params=_params(("arbitrary",)),
    )(h2, x2, target, modr, vecs, w_up, w_down)


def _mlp_bwd(d_dn, ru, x2, dx3, o, modr, vecs, w_up, w_down):
    s = x2.shape[0]
    tm = min(TM_MLP_BWD, s)

    def body(ddn_ref, ru_ref, x2_ref, dx3_ref, o_ref, mod_ref, vec_ref, wu_ref, wd_ref,
             dup_ref, dx2_ref, do_ref, small_ref):
        @pl.when(pl.program_id(0) == 0)
        def _():
            small_ref[...] = jnp.zeros_like(small_ref)

        ddn = ddn_ref[...]
        dh2 = None
        for c in range(D_FF // D):
            cs = slice(c * D, (c + 1) * D)
            dff = _dot_nt(ddn, wd_ref[cs, :])
            dup = (dff * (2.0 * ru_ref[:, cs].astype(F32))).astype(BF16)
            dup_ref[:, cs] = dup
            part = _dot_nt(dup, wu_ref[:, cs])
            dh2 = part if dh2 is None else dh2 + part
        x2 = x2_ref[...]
        r2 = lax.rsqrt(jnp.mean(x2 * x2, axis=-1, keepdims=True) + EPS)
        xn2 = x2 * r2
        gain = vec_ref[V_G2:V_G2 + 1, :] * (1.0 + mod_ref[M_SC2:M_SC2 + 1, :])
        dxn2 = dh2 * gain
        dx2 = dx3_ref[...] + r2 * (dxn2 - xn2 * jnp.mean(dxn2 * xn2, axis=-1, keepdims=True))
        dx2_ref[...] = dx2
        do_ref[...] = (dx2 * mod_ref[M_GT1:M_GT1 + 1, :]).astype(BF16)
        small_ref[0:1, :] += jnp.sum(dh2, axis=0, keepdims=True)
        small_ref[1:2, :] += jnp.sum(dh2 * xn2, axis=0, keepdims=True)
        small_ref[2:3, :] += jnp.sum(dx2 * o_ref[...].astype(F32), axis=0, keepdims=True)

    tok = pl.BlockSpec((tm, D), lambda i: (i, 0))
    wide = pl.BlockSpec((tm, D_FF), lambda i: (i, 0))
    return pl.pallas_call(
        body, name="mlp_bwd", grid=(s // tm,),
        in_specs=[tok, wide, tok, tok, tok,
                  pl.BlockSpec((8, D), lambda i: (0, 0)), pl.BlockSpec((16, D), lambda i: (0, 0)),
                  _resident((D, D_FF)), _resident((D_FF, D))],
        out_specs=[wide, tok, tok, pl.BlockSpec((8, D), lambda i: (0, 0))],
        out_shape=[jax.ShapeDtypeStruct((s, D_FF), BF16), jax.ShapeDtypeStruct((s, D), F32),
                   jax.ShapeDtypeStruct((s, D), BF16), jax.ShapeDtypeStruct((8, D), F32)],
        compiler_params=_params(("arbitrary",)),
    )(d_dn, ru, x2, dx3, o, modr, vecs, w_up, w_down)


def _branch_bwd(do, sa, sb, ba, bb, w_a, w_b, w_out, dep):
    s = do.shape[0]
    tm = min(TM_BRANCH, s)

    def body(do_ref, sa_ref, sb_ref, ba_ref, bb_ref, wa_ref, wb_ref, wo_ref, dep_ref,
             dba_ref, dbb_ref, dg_ref, dza_ref, dpooled_ref):
        dmerged = _dot_nt(do_ref[...], wo_ref[...])
        sa = sa_ref[...]
        sb = sb_ref[...]
        dba = (dmerged * sa).astype(BF16)
        dbb = (dmerged * sb).astype(BF16)
        dba_ref[...] = dba
        dbb_ref[...] = dbb
        dg_ref[:, :D] = (dmerged * ba_ref[...].astype(F32) * sa * (1.0 - sa)).astype(BF16)
        dg_ref[:, D:] = (dmerged * bb_ref[...].astype(F32) * sb * (1.0 - sb)).astype(BF16)
        dza_ref[...] = _dot_nt(dba, wa_ref[...])
        dpooled_ref[...] = _dot_nt(dbb, wb_ref[...])

    tok = pl.BlockSpec((tm, D), lambda i: (i, 0))
    wspec = pl.BlockSpec((D, D), lambda i: (0, 0))
    sd = lambda dt: jax.ShapeDtypeStruct((s, D), dt)
    return pl.pallas_call(
        body, name="branch_bwd", grid=(s // tm,),
        in_specs=[tok, tok, tok, tok, tok, wspec, wspec, wspec, pl.BlockSpec(memory_space=pl.ANY)],
        out_specs=[tok, tok, pl.BlockSpec((tm, 2 * D), lambda i: (i, 0)), tok, tok],
        out_shape=[sd(BF16), sd(BF16), jax.ShapeDtypeStruct((s, 2 * D), BF16), sd(F32), sd(F32)],
        compiler_params=_params(("parallel",)),
    )(do, sa, sb, ba, bb, w_a, w_b, w_out, dep)


def _mix_bwd(dza, dpooled, x_rnn, ga, dga, xr, hr, p, gates, dgates, vecs, w_rg_a, w_rg_x, w_pool, dep):
    s = xr.shape[0]
    tm = min(TM_MIX, s)
    nb = s // tm

    def body(dza_ref, dpooled_ref, xh_ref, x_ref, ga_ref, dga_ref, xr_ref, hh_ref, hr_ref, p_ref,
             a_ref, mult_ref, ra_ref, ri_ref, dg_ref, vec_ref, wa_ref, wx_ref, wp_ref, dep_ref,
             dproj_ref, dwa_ref, dwx_ref, dwp_ref, small_ref,
             scan_carry, dxr_carry, q_carry, scan_scr, dwa_acc, dwx_acc, dwp_acc):
        i = pl.program_id(0)
        bi = nb - 1 - i
        first_t = bi == 0

        @pl.when(i == 0)
        def _():
            scan_carry[...] = jnp.zeros_like(scan_carry)
            dxr_carry[...] = jnp.zeros_like(dxr_carry)
            q_carry[...] = jnp.zeros_like(q_carry)
            dwa_acc[...] = jnp.zeros_like(dwa_acc)
            dwx_acc[...] = jnp.zeros_like(dwx_acc)
            dwp_acc[...] = jnp.zeros_like(dwp_acc)
            small_ref[...] = jnp.zeros_like(small_ref)

        row = lax.broadcasted_iota(jnp.int32, (tm, GW), 0)
        is_t0 = jnp.logical_and(first_t, row == 0)
        head_t = (lax.broadcasted_iota(jnp.int32, (HALO_U, GW), 0) + 1).astype(F32)
        colsum = lambda v: jnp.sum(v, axis=0, keepdims=True)
        for g in range(N_GROUPS):
            cs = slice(g * GW, (g + 1) * GW)
            vec = vec_ref[:, cs]
            xr = xr_ref[:, cs]
            hr = hr_ref[:, cs]
            dza = dza_ref[:, cs]
            dproj_ref[:, D + g * GW:D + (g + 1) * GW] = (dza * hr * dga_ref[:, cs]).astype(BF16)
            dhr = dza * ga_ref[:, cs]
            a = a_ref[:, cs]
            mult = mult_ref[:, cs]
            ra = ra_ref[:, cs].astype(F32)
            ri = ri_ref[:, cs].astype(F32)
            sp = _softplus(vec[V_A_PARAM:V_A_PARAM + 1])
            m = jnp.where(row == tm - 1, 1.0, _shift_up(a, 1))
            gsum = _scan_up(m, dhr, scan_carry[0:1, cs], scan_scr)
            scan_carry[0:1, cs] = a[0:1, :] * gsum[0:1, :]
            hh = jnp.where(first_t, 0.0, hh_ref[:, cs])
            hprev = _shift_down(jnp.concatenate([hh, hr], axis=0), 1)[8:]
            da = gsum * hprev
            dmult = jnp.where(is_t0, 0.0, gsum * xr * ri)
            dlog_a = da * a - dmult * a * a / mult
            dri = gsum * xr * mult
            dxr = gsum * ri * mult
            small_ref[7:8, cs] += colsum((-C_RG) * ra * dlog_a)
            dpa = (((-C_RG) * sp) * dlog_a * ra * (1.0 - ra))
            dpx = dri * ri * (1.0 - ri)
            small_ref[5:6, cs] += colsum(dpa)
            small_ref[6:7, cs] += colsum(dpx)
            dpa = dpa.astype(BF16)
            dpx = dpx.astype(BF16)
            xrb = xr.astype(BF16)
            dwa_acc[g] += _dot_tn(xrb, dpa)
            dwx_acc[g] += _dot_tn(xrb, dpx)
            dxr = dxr + _dot_nt(dpa, wa_ref[g]) + _dot_nt(dpx, wx_ref[g])
            small_ref[4:5, cs] += colsum(dxr)
            xh = jnp.where(first_t, 0.0, xh_ref[:, cs])
            taps = _conv_taps(jnp.concatenate([xh, x_ref[:, cs]], axis=0))
            dxr_ext = jnp.concatenate([dxr, dxr_carry[:, cs]], axis=0)
            dx = vec[V_CONV_W + 3:V_CONV_W + 4] * dxr
            for j in range(4):
                small_ref[j:j + 1, cs] += colsum(dxr * taps[j])
                if j < 3:
                    dx = dx + vec[V_CONV_W + j:V_CONV_W + j + 1] * _shift_up(dxr_ext, 3 - j)[:tm]
            dxr_carry[:, cs] = dxr[0:8, :]
            dproj_ref[:, cs] = dx.astype(BF16)
            pg = p_ref[:, cs]
            dpooled = dpooled_ref[:, cs]
            pb = _dot(pg, wp_ref[g]) + vec[V_B_POOL:V_B_POOL + 1]
            small_ref[9:10, cs] += colsum(dpooled * pb)
            dpb = dpooled * vec[V_POOL_SCALE:V_POOL_SCALE + 1]
            small_ref[8:9, cs] += colsum(dpb)
            dpbb = dpb.astype(BF16)
            dwp_acc[g] += _dot_tn(pg, dpbb)
            dp = _dot_nt(dpbb, wp_ref[g])
            q = _window_mean(dp, POOL_WINDOWS[g], first_t, head_t)
            sm = jnp.concatenate([q, q_carry[:, cs]], axis=0)
            k = 1
            while k < POOL_WINDOWS[g]:
                sm = sm + _shift_up(sm, k)
                k *= 2
            q_carry[:, cs] = q[0:HALO_U, :]
            dproj_ref[:, 2 * D + g * GW:2 * D + (g + 1) * GW] = (sm[:tm] - dp).astype(BF16)
        dproj_ref[:, 3 * D:] = dg_ref[...]

        @pl.when(i == nb - 1)
        def _():
            dwa_ref[...] = dwa_acc[...].astype(BF16)
            dwx_ref[...] = dwx_acc[...].astype(BF16)
            dwp_ref[...] = dwp_acc[...].astype(BF16)

    rev = lambda i: nb - 1 - i
    tok = pl.BlockSpec((tm, D), lambda i: (rev(i), 0))
    halo8 = lambda k: pl.BlockSpec((8, D), lambda i: (jnp.maximum(rev(i) * (tm // 8) - 1, 0), k))
    wspec = pl.BlockSpec((N_GROUPS, GW, GW), lambda i: (0, 0, 0))
    wshape = jax.ShapeDtypeStruct((N_GROUPS, GW, GW), BF16)
    return pl.pallas_call(
        body, name="mix_bwd", grid=(nb,),
        in_specs=[tok, tok, halo8(0), tok, tok, tok, tok, halo8(0), tok, tok, tok, tok, tok, tok,
                  pl.BlockSpec((tm, 2 * D), lambda i: (rev(i), 0)),
                  pl.BlockSpec((16, D), lambda i: (0, 0)), wspec, wspec, wspec, pl.BlockSpec(memory_space=pl.ANY)],
        out_specs=[pl.BlockSpec((tm, D_IN), lambda i: (rev(i), 0)), wspec, wspec, wspec,
                   pl.BlockSpec((16, D), lambda i: (0, 0))],
        out_shape=[jax.ShapeDtypeStruct((s, D_IN), BF16), wshape, wshape, wshape,
                   jax.ShapeDtypeStruct((16, D), F32)],
        scratch_shapes=[pltpu.VMEM((8, D), F32), pltpu.VMEM((8, D), F32), pltpu.VMEM((HALO_U, D), F32),
                        pltpu.VMEM((3, tm, LANES), F32)] + [pltpu.VMEM((N_GROUPS, GW, GW), F32)] * 3,
        compiler_params=_params(("arbitrary",)),
    )(dza, dpooled, x_rnn, x_rnn, ga, dga, xr, hr, hr, p, *gates, dgates, vecs, w_rg_a, w_rg_x, w_pool, dep)


def _proj_bwd(dproj, x, dx2, modr, vecs, w_in, dep):
    s = x.shape[0]
    tm = min(TM_PROJ, s)

    def body(dp_ref, x_ref, dx2_ref, mod_ref, vec_ref, w_ref, dep_ref, gx_ref, small_ref):
        @pl.when(pl.program_id(0) == 0)
        def _():
            small_ref[...] = jnp.zeros_like(small_ref)

        dh1 = None
        for c in range(D_IN // D):
            cs = slice(c * D, (c + 1) * D)
            part = _dot_nt(dp_ref[:, cs], w_ref[:, cs])
            dh1 = part if dh1 is None else dh1 + part
        xv = x_ref[...]
        r1 = lax.rsqrt(jnp.mean(xv * xv, axis=-1, keepdims=True) + EPS)
        xn1 = xv * r1
        gain = vec_ref[V_G1:V_G1 + 1, :] * (1.0 + mod_ref[M_SC1:M_SC1 + 1, :])
        dxn1 = dh1 * gain
        gx_ref[...] = dx2_ref[...] + r1 * (dxn1 - xn1 * jnp.mean(dxn1 * xn1, axis=-1, keepdims=True))
        small_ref[0:1, :] += jnp.sum(dh1, axis=0, keepdims=True)
        small_ref[1:2, :] += jnp.sum(dh1 * xn1, axis=0, keepdims=True)

    tok = pl.BlockSpec((tm, D), lambda i: (i, 0))
    return pl.pallas_call(
        body, name="proj_bwd", grid=(s // tm,),
        in_specs=[pl.BlockSpec((tm, D_IN), lambda i: (i, 0)), tok, tok,
                  pl.BlockSpec((8, D), lambda i: (0, 0)), pl.BlockSpec((16, D), lambda i: (0, 0)),
                  _resident((D, D_IN)), pl.BlockSpec(memory_space=pl.ANY)],
        out_specs=[tok, pl.BlockSpec((8, D), lambda i: (0, 0))],
        out_shape=[jax.ShapeDtypeStruct((s, D), F32), jax.ShapeDtypeStruct((8, D), F32)],
        compiler_params=_params(("arbitrary",)),
    )(dproj, x, dx2, modr, vecs, w_in, dep)


def _wgrad(a, b, name, square_a=False, dep=None):
    s, ka = a.shape
    n = b.shape[1]
    tka = ka if ka <= 1024 else ka // 2
    tn = n if n <= 1024 else n // 2
    ts = min(TS_WGRAD, s)
    ns = s // ts
    nc = 512
    deps = [] if dep is None else [dep]

    def body(a_ref, b_ref, *refs):
        out_ref, acc_ref = refs[-2:]
        t = pl.program_id(2)

        @pl.when(t == 0)
        def _():
            acc_ref[...] = jnp.zeros_like(acc_ref)

        av = a_ref[...]
        if square_a:
            af = av.astype(F32)
            av = (af * af).astype(BF16)
        for c in range(tn // nc):
            cs = slice(c * nc, (c + 1) * nc)
            acc_ref[:, cs] += _dot_tn(av, b_ref[:, cs])

        @pl.when(t == ns - 1)
        def _():
            out_ref[...] = acc_ref[...].astype(BF16)

    return pl.pallas_call(
        body, name=name, grid=(ka // tka, n // tn, ns),
        in_specs=[pl.BlockSpec((ts, tka), lambda i, j, t: (t, i)),
                  pl.BlockSpec((ts, tn), lambda i, j, t: (t, j))] + [pl.BlockSpec(memory_space=pl.ANY)] * len(deps),
        out_specs=pl.BlockSpec((tka, tn), lambda i, j, t: (i, j)),
        out_shape=jax.ShapeDtypeStruct((ka, n), BF16),
        scratch_shapes=[pltpu.VMEM((tka, tn), F32)],
        compiler_params=_params(("parallel", "parallel", "arbitrary")),
    )(a, b, *deps)


def _window(ref, kind, idx, size):
    start = pl.multiple_of(idx * size, size)
    if kind == 0:
        return ref.at[pl.ds(start, size)]
    if kind == 1:
        return ref.at[:, pl.ds(start, size)]
    return ref.at[:, :, pl.ds(start, size)]


def _mesh_place():
    x, y, c = lax.axis_index("x"), lax.axis_index("y"), lax.axis_index("c")
    return x, y, c, 4 * x + 2 * y + c


def _peer(x, y, c, q):
    px = 1 - x if q & 4 else x
    py = 1 - y if q & 2 else y
    pc = 1 - c if q & 1 else c
    return (px, py, pc), 4 * px + 2 * py + pc


def _gather_to_all(srcs, dsts, kinds, send_sems, recv_sems, local_sems, first):
    x, y, c, me = _mesh_place()
    sends, recvs, locals_ = [], [], []
    for k, (src, dst, kind) in enumerate(zip(srcs, dsts, kinds)):
        size = src.shape[kind]
        mine = _window(dst, kind, me, size)
        lc = pltpu.make_async_copy(src, mine, local_sems.at[first + k])
        lc.start()
        locals_.append(lc)
        for q in range(1, N_DEV):
            peer, peer_idx = _peer(x, y, c, q)
            sems = dict(send_sem=send_sems.at[first + k, q], recv_sem=recv_sems.at[first + k, q],
                        device_id=peer, device_id_type=MESH)
            cp = pltpu.make_async_remote_copy(src_ref=src, dst_ref=mine, **sems)
            cp.start()
            sends.append(cp)
            recvs.append(pltpu.make_async_remote_copy(src_ref=src, dst_ref=_window(dst, kind, peer_idx, size), **sems))
    for cp in recvs:
        cp.wait_recv()
    for cp in sends:
        cp.wait_send()
    for lc in locals_:
        lc.wait()


_HBM = pl.BlockSpec(memory_space=pltpu.HBM)
_SEM = pl.BlockSpec(memory_space=pltpu.SEMAPHORE)
_EFFECT = pltpu.SideEffectType.DATAFLOW_SIDE_EFFECTING


N_NEAR = 4


def _near(x, y, c):
    out = [((x, y, 1 - c), 4 * x + 2 * y + 1 - c)]
    for j in (1, 2, 3):
        px = 1 - x if j & 2 else x
        py = 1 - y if j & 1 else y
        out.append(((px, py, c), 4 * px + 2 * py + c))
    return out


def _remote(src, dst, send_sems, recv_sems, slot, device):
    return pltpu.make_async_remote_copy(src_ref=src, dst_ref=dst, send_sem=send_sems.at[slot], recv_sem=recv_sems.at[slot],
                                        device_id=device, device_id_type=MESH)


def _split_call(name, arrays, sems_in, n_new_sems, after, emit):
    na, ns, nn = len(arrays), len(sems_in), len(n_new_sems)

    def body(*refs):
        emit(refs[:na], refs[na:na + ns], refs[na + ns + 1:na + ns + 1 + nn])
        refs[-1][...] = jnp.zeros_like(refs[-1])

    outs = pl.pallas_call(
        body, name=name,
        out_shape=(*[pltpu.SemaphoreType.DMA((m,)) for m in n_new_sems],
                   *[pltpu.HBM(a.shape, a.dtype) for a in arrays], jax.ShapeDtypeStruct((8, 128), F32)),
        in_specs=[_HBM] * na + [_SEM] * ns + [pl.BlockSpec(memory_space=pl.ANY)],
        out_specs=(*[_SEM] * nn, *[_HBM] * na, pl.BlockSpec(memory_space=pltpu.VMEM)),
        input_output_aliases={i: nn + i for i in range(na)},
        compiler_params=pltpu.CompilerParams(has_side_effects=_EFFECT),
    )(*[pltpu.with_memory_space_constraint(a, pltpu.HBM) for a in arrays], *sems_in, after)
    return list(outs[:nn]), list(outs[nn:nn + na]), outs[-1]


class _Gather:
    def __init__(self, shards, kinds, after, name):
        self.n, self.kinds, self.name = len(shards), kinds, name
        self.sizes = [s.shape[k] for s, k in zip(shards, kinds)]
        n = self.n
        lands = []
        for s, k in zip(shards, kinds):
            dims = list(s.shape)
            dims[k] *= N_DEV
            lands.append(lax.empty(tuple(dims), s.dtype))

        def emit(arr, _, new):
            x, y, c, me = _mesh_place()
            for k in range(n):
                pltpu.make_async_copy(arr[k], _window(arr[n + k], kinds[k], me, self.sizes[k]), new[2].at[k]).start()
            for k in range(n):
                mine = _window(arr[n + k], kinds[k], me, self.sizes[k])
                for j, (dev, _) in enumerate(_near(x, y, c)):
                    _remote(arr[k], mine, new[0], new[1], k * N_NEAR + j, dev).start()

        self.sems, self.arrays, self.token = _split_call(name + "_start", [*shards, *lands], [],
                                                         [n * N_NEAR, n * N_NEAR, n], after, emit)

    def forward(self, after):
        n, kinds, sizes = self.n, self.kinds, self.sizes

        def emit(arr, old, new):
            x, y, c, _ = _mesh_place()
            near = _near(x, y, c)
            for k in range(n):
                for j in (1, 2, 3):
                    dev, idx = near[j]
                    landed = _window(arr[n + k], kinds[k], idx, sizes[k])
                    _remote(arr[k], landed, old[0], old[1], k * N_NEAR + j, dev).wait_recv()
                    _remote(landed, landed, new[0], new[1], k * N_NEAR + j, near[0][0]).start()

        new, self.arrays, self.token = _split_call(self.name + "_forward", self.arrays, self.sems, [n * N_NEAR] * 2,
                                                   after, emit)
        self.sems = [*self.sems, *new]

    def finish(self, after):
        n, kinds, sizes = self.n, self.kinds, self.sizes

        def emit(arr, old, _):
            x, y, c, me = _mesh_place()
            near = _near(x, y, c)
            other_core = near[0][0]
            for k in range(n):
                win = lambda idx: _window(arr[n + k], kinds[k], idx, sizes[k])
                pltpu.make_async_copy(arr[k], win(me), old[2].at[k]).wait()
                for j, (dev, idx) in enumerate(near):
                    _remote(arr[k], win(me), old[0], old[1], k * N_NEAR + j, dev).wait_send()
                _remote(arr[k], win(near[0][1]), old[0], old[1], k * N_NEAR, other_core).wait_recv()
                for j in (1, 2, 3):
                    idx = near[j][1]
                    _remote(win(idx), win(idx), old[3], old[4], k * N_NEAR + j, other_core).wait_send()
                    _remote(arr[k], win(idx + 1 - 2 * c), old[3], old[4], k * N_NEAR + j, other_core).wait_recv()

        _, arrays, _ = _split_call(self.name + "_finish", self.arrays, self.sems, [], after, emit)
        return arrays[n:]


class _Scatter:
    def __init__(self, partials, kinds, after, name):
        self.n, self.kinds, self.name, self.partials = len(partials), kinds, name, partials
        self.sizes = [p.shape[k] // N_DEV for p, k in zip(partials, kinds)]
        n, sizes = self.n, self.sizes
        self.slot_shapes = []
        for p, k, size in zip(partials, kinds, sizes):
            dims = list(p.shape)
            dims[k] = size
            self.slot_shapes.append((N_NEAR, *dims))
        slots = [lax.empty(sh, p.dtype) for sh, p in zip(self.slot_shapes, partials)]

        def emit(arr, _, new):
            x, y, c, _ = _mesh_place()
            near = _near(x, y, c)
            for k in range(n):
                for j in range(N_NEAR):
                    owner = near[j][1] if j == 0 else near[j][1] + 1 - 2 * c
                    _remote(_window(arr[k], kinds[k], owner, sizes[k]), arr[n + k].at[j], new[0], new[1],
                            k * N_NEAR + j, near[0][0]).start()

        self.sems, self.arrays, self.token = _split_call(name + "_start", [*partials, *slots], [], [n * N_NEAR] * 2,
                                                         after, emit)

    def combine_and_send(self, own4, after):
        n, kinds, sizes = self.n, self.kinds, self.sizes

        def emit_wait(arr, old, _):
            x, y, c, _ = _mesh_place()
            near = _near(x, y, c)
            for k in range(n):
                for j in range(N_NEAR):
                    owner = near[j][1] if j == 0 else near[j][1] + 1 - 2 * c
                    cp = _remote(_window(arr[k], kinds[k], owner, sizes[k]), arr[n + k].at[j], old[0], old[1],
                                 k * N_NEAR + j, near[0][0])
                    cp.wait_send()
                    cp.wait_recv()

        _, arrays, _ = _split_call(self.name + "_landed", self.arrays, self.sems, [], after, emit_wait)
        chip_sums = _chip_sums(arrays[:n], arrays[n:], kinds, sizes, own4, self.name + "_combine")
        arrivals = [lax.empty((N_NEAR - 1, *sh[1:]), p.dtype) for sh, p in zip(self.slot_shapes, self.partials)]

        def emit_send(arr, _, new):
            x, y, c, _ = _mesh_place()
            near = _near(x, y, c)
            for k in range(n):
                for j in (1, 2, 3):
                    _remote(arr[k].at[j], arr[n + k].at[j - 1], new[0], new[1], k * N_NEAR + j, near[j][0]).start()

        self.sems, self.arrays, self.token = _split_call(self.name + "_send", [*chip_sums, *arrivals], [],
                                                         [n * N_NEAR] * 2, own4, emit_send)

    def finish(self, after):
        n = self.n

        def emit(arr, old, _):
            x, y, c, _ = _mesh_place()
            near = _near(x, y, c)
            for k in range(n):
                for j in (1, 2, 3):
                    cp = _remote(arr[k].at[j], arr[n + k].at[j - 1], old[0], old[1], k * N_NEAR + j, near[j][0])
                    cp.wait_send()
                    cp.wait_recv()

        _, arrays, _ = _split_call(self.name + "_finish", self.arrays, self.sems, [], after, emit)
        return arrays[:n], arrays[n:]


def _chip_sums(partials, slots, kinds, sizes, own4, name):
    n = len(partials)

    def body(own_ref, *refs):
        for k in range(n):
            refs[2 * n + k][...] = (refs[k][...].astype(F32) + refs[n + k][...].astype(F32)).astype(BF16)

    in_specs, slot_specs = [], []
    for p, s, kind, size in zip(partials, slots, kinds, sizes):
        block = list(p.shape)
        block[kind] = size
        nd = len(block)
        in_specs.append(pl.BlockSpec(tuple(block), functools.partial(
            lambda j, own, kind, nd: tuple(own[j] if d == kind else 0 for d in range(nd)), kind=kind, nd=nd)))
        slot_specs.append(pl.BlockSpec((None, *block), functools.partial(
            lambda j, own, nd: (j,) + (0,) * nd, nd=nd)))
    return pl.pallas_call(
        body, name=name,
        grid_spec=pltpu.PrefetchScalarGridSpec(num_scalar_prefetch=1, grid=(N_NEAR,),
                                               in_specs=in_specs + slot_specs, out_specs=slot_specs),
        out_shape=[jax.ShapeDtypeStruct(s.shape, s.dtype) for s in slots],
        compiler_params=_params(("arbitrary",)),
    )(own4, *partials, *slots)


def _to_bf16(arrays):
    def body(*refs):
        for src, dst in zip(refs[:len(arrays)], refs[len(arrays):]):
            dst[...] = src[...].astype(BF16)

    return pl.pallas_call(body, name="to_bf16", out_shape=[jax.ShapeDtypeStruct(a.shape, BF16) for a in arrays],
                          compiler_params=pltpu.CompilerParams(vmem_limit_bytes=V7X_VMEM_LIMIT))(*arrays)


def _silu(c):
    return c * _sigmoid_tail(c)


def _conditioning(c_rows8, conv_w_rows8, w_ada, b_ada_cols):
    n_ada = w_ada.shape[1]

    def body(c_ref, cw_ref, w_ref, b_ref, c_all_ref, cw_all_ref, mod_ref, part_ref, send_sems, recv_sems, local_sems):
        _gather_to_all([c_ref, cw_ref], [c_all_ref, cw_all_ref], [0, 1], send_sems, recv_sems, local_sems, 0)
        pick = (lax.broadcasted_iota(jnp.int32, (N_DEV, N_DEV * 8), 1)
                == 8 * lax.broadcasted_iota(jnp.int32, (N_DEV, N_DEV * 8), 0)).astype(F32)
        c_all = jnp.dot(pick, c_all_ref[...], preferred_element_type=F32, precision=lax.Precision.HIGHEST)
        part_ref[...] = jnp.dot(_silu(c_all), w_ref[...], preferred_element_type=F32,
                                precision=lax.Precision.HIGHEST) + b_ref[...]
        _gather_to_all([part_ref], [mod_ref], [0], send_sems, recv_sems, local_sems, 2)

    vmem = pl.BlockSpec(memory_space=pltpu.VMEM)
    return pl.pallas_call(
        body, name="conditioning",
        in_specs=[vmem] * 4, out_specs=[vmem] * 3,
        out_shape=[jax.ShapeDtypeStruct((N_DEV * 8, D), F32), jax.ShapeDtypeStruct((8, D), F32),
                   jax.ShapeDtypeStruct((N_DEV * N_DEV, n_ada), F32)],
        scratch_shapes=[pltpu.VMEM((N_DEV, n_ada), F32), pltpu.SemaphoreType.DMA((3, N_DEV)),
                        pltpu.SemaphoreType.DMA((3, N_DEV)), pltpu.SemaphoreType.DMA((3,))],
        compiler_params=pltpu.CompilerParams(vmem_limit_bytes=V7X_VMEM_LIMIT),
    )(c_rows8, conv_w_rows8, w_ada, b_ada_cols)


def _adam(w, g, m, v):
    m = ADAM_B1 * m + (1.0 - ADAM_B1) * g
    v = ADAM_B2 * v + (1.0 - ADAM_B2) * (g * g)
    m_hat = m / (1.0 - ADAM_B1 ** ADAM_STEP)
    v_hat = v / (1.0 - ADAM_B2 ** ADAM_STEP)
    delta = -ADAM_LR * (m_hat / (jnp.sqrt(v_hat) + ADAM_EPS) + ADAM_WD * w)
    return delta, m, v


def _ada_bwd_adam(c_all, dmod_cols, w, m, v):
    def body(c_ref, d_ref, w_ref, m_ref, v_ref, g_ref, delta_ref, nm_ref, nv_ref):
        g = lax.dot_general(_silu(c_ref[...]), d_ref[...], (((0,), (0,)), ((), ())),
                            preferred_element_type=F32, precision=lax.Precision.HIGHEST)
        g_ref[...] = g
        delta_ref[...], nm_ref[...], nv_ref[...] = _adam(w_ref[...], g, m_ref[...], v_ref[...])

    tr = 256
    blk = pl.BlockSpec((tr, w.shape[1]), lambda i: (i, 0))
    sd = jax.ShapeDtypeStruct(w.shape, F32)
    return pl.pallas_call(
        body, name="ada_bwd_adam", grid=(w.shape[0] // tr,),
        in_specs=[pl.BlockSpec((N_DEV, tr), lambda i: (0, i)), pl.BlockSpec(dmod_cols.shape, lambda i: (0, 0)),
                  blk, blk, blk],
        out_specs=[blk] * 4, out_shape=[sd] * 4,
        compiler_params=_params(("parallel",)),
    )(c_all, dmod_cols, w, m, v)


def _adam_group(chip_sums, arrivals, ws, ms, vs, n_tiles, name):
    n = len(ws)

    def body(*refs):
        for k in range(n):
            c_ref, a_ref, w_ref, m_ref, v_ref = (refs[j * n + k] for j in range(5))
            g_ref, delta_ref, nm_ref, nv_ref = (refs[(5 + j) * n + k] for j in range(4))
            g = c_ref[...].astype(F32)
            for j in range(N_NEAR - 1):
                g = g + a_ref[j].astype(F32)
            g_ref[...] = g
            delta_ref[...], nm_ref[...], nv_ref[...] = _adam(w_ref[...], g, m_ref[...], v_ref[...])

    tiles = [(w.shape[0] // n_tiles, w.shape[1]) for w in ws]
    blk = [pl.BlockSpec(t, lambda i: (i, 0)) for t in tiles]
    return pl.pallas_call(
        body, name=name, grid=(n_tiles,),
        in_specs=[pl.BlockSpec((None, *t), lambda i: (0, i, 0)) for t in tiles]
        + [pl.BlockSpec((N_NEAR - 1, *t), lambda i: (0, i, 0)) for t in tiles] + blk * 3,
        out_specs=blk * 4, out_shape=[jax.ShapeDtypeStruct(w.shape, F32) for w in ws] * 4,
        compiler_params=_params(("parallel",)),
    )(*chip_sums, *arrivals, *ws, *ms, *vs)


N_SMALL = 40
N_SMALL_PARAMS = 11


def _pack_vecs(conv_w_full, rows):
    def body(cw_ref, *refs):
        out = refs[-1]
        out[...] = jnp.zeros_like(out)
        out[0:4, :] = cw_ref[0:4, :]
        for r, ref in enumerate(refs[:-1]):
            out[4 + r:5 + r, :] = ref[...]

    return pl.pallas_call(body, name="pack_vecs", out_shape=jax.ShapeDtypeStruct((16, D), F32))(conv_w_full, *rows)


def _small_finish(gathered, mod_all, vecs, ws, ms, vs):
    n = N_SMALL_PARAMS

    def body(g_ref, mod_ref, vec_ref, *refs):
        w_refs, m_refs, v_refs = refs[:n], refs[n:2 * n], refs[2 * n:3 * n]
        outs = refs[3 * n:]
        g1 = vec_ref[V_G1:V_G1 + 1, :]
        g2 = vec_ref[V_G2:V_G2 + 1, :]
        zero = jnp.zeros((1, D), F32)
        dg1, dg2, dgf, loss_lanes = zero, zero, zero, zero
        mixer = jnp.zeros((16, D), F32)
        db_ada = jnp.zeros((6, D), F32)
        for b in range(N_DEV):
            gb = g_ref[b]
            mod = mod_ref[b]
            q1 = gb[33:34]
            q2 = gb[9:10]
            dmod = jnp.concatenate([gb[32:33], q1 * g1, gb[10:11], gb[8:9], q2 * g2, gb[1:2]], axis=0)
            outs[4 * n][b] = dmod
            db_ada = db_ada + dmod
            dg1 = dg1 + q1 * (1.0 + mod[M_SC1:M_SC1 + 1])
            dg2 = dg2 + q2 * (1.0 + mod[M_SC2:M_SC2 + 1])
            dgf = dgf + gb[0:1]
            loss_lanes = loss_lanes + gb[2:3]
            mixer = mixer + gb[16:32]
        d_a_param = mixer[7:8] * _sigmoid_tail(vec_ref[V_A_PARAM:V_A_PARAM + 1, :])
        grads = [dg1, dg2, mixer[4:5], mixer[5:6], mixer[6:7], d_a_param, mixer[8:9], mixer[9:10], dgf,
                 db_ada, mixer[0:4]]
        for k in range(n):
            outs[k][...] = grads[k]
            outs[n + k][...], outs[2 * n + k][...], outs[3 * n + k][...] = _adam(
                w_refs[k][...], grads[k], m_refs[k][...], v_refs[k][...])
        outs[4 * n + 1][...] = jnp.broadcast_to(jnp.sum(loss_lanes, axis=1, keepdims=True), (8, 128))

    shapes = [jax.ShapeDtypeStruct(w.shape, F32) for w in ws]
    return pl.pallas_call(
        body, name="small_finish",
        out_shape=shapes * 4 + [jax.ShapeDtypeStruct((N_DEV, 6, D), F32), jax.ShapeDtypeStruct((8, 128), F32)],
    )(gathered, mod_all, vecs, *ws, *ms, *vs)


def _pad_rows(a, rows):
    return jnp.pad(a, ((0, rows - a.shape[0]), (0, 0)))


def kernel(x, c, norm_mix_g, norm_mlp_g, w_ada, b_ada, w_in, conv_w, conv_b, w_rg_a, b_rg_a, w_rg_x, b_rg_x, a_param, w_branch_a, w_pool, b_pool, pool_scale, w_branch_b, w_out, w_up, w_down, final_g, loss_target, m_norm_mix_g, m_norm_mlp_g, m_w_ada, m_b_ada, m_w_in, m_conv_w, m_conv_b, m_w_rg_a, m_b_rg_a, m_w_rg_x, m_b_rg_x, m_a_param, m_w_branch_a, m_w_pool, m_b_pool, m_pool_scale, m_w_branch_b, m_w_out, m_w_up, m_w_down, m_final_g, v_norm_mix_g, v_norm_mlp_g, v_w_ada, v_b_ada, v_w_in, v_conv_w, v_conv_b, v_w_rg_a, v_b_rg_a, v_w_rg_x, v_b_rg_x, v_a_param, v_w_branch_a, v_w_pool, v_b_pool, v_pool_scale, v_w_branch_b, v_w_out, v_w_up, v_w_down, v_final_g):
    me = 4 * lax.axis_index("x") + 2 * lax.axis_index("y") + lax.axis_index("c")
    s = x.shape[1]
    x2d = x.reshape(s, D)
    target = loss_target.reshape(s, D)
    n_ada = w_ada.shape[2]

    sharded = dict(w_in=(w_in[0], 1), w_up=(w_up[0], 1), w_down=(w_down[0], 0), w_branch_a=(w_branch_a[0], 0),
                   w_branch_b=(w_branch_b[0], 0), w_out=(w_out[0], 0), w_rg_a=(w_rg_a[0], 1), w_rg_x=(w_rg_x[0], 1),
                   w_pool=(w_pool[0], 1))
    kind = {k: v[1] for k, v in sharded.items()}
    shard = dict(zip(sharded, _to_bf16([v[0] for v in sharded.values()])))

    b_ada_cols = lax.dynamic_slice(b_ada, (0, me * n_ada), (1, n_ada))
    c_rows, conv_w_full, mod_parts = _conditioning(_pad_rows(c, 8), _pad_rows(conv_w[0], 8), w_ada[0], b_ada_cols)
    c_all = c_rows.reshape(N_DEV, 8, D)[:, 0, :]

    first_names = ["w_in", "w_rg_a", "w_rg_x", "w_pool"]
    branch_names = ["w_branch_a", "w_branch_b", "w_out"]
    mlp_names = ["w_up", "w_down"]

    def gather(group, after, name):
        return _Gather([shard[k] for k in group], [kind[k] for k in group], after, name)

    g_first = gather(first_names, mod_parts, "gather_first")
    g_branch = gather(branch_names, g_first.token, "gather_branch")
    g_mlp = gather(mlp_names, g_branch.token, "gather_mlp")

    mod_all = jnp.transpose(mod_parts.reshape(N_DEV, N_DEV, n_ada), (1, 0, 2)).reshape(N_DEV, 6, D)
    mod_all = jnp.pad(mod_all, ((0, 0), (0, 2), (0, 0)))
    modr = lax.dynamic_index_in_dim(mod_all, me, 0, keepdims=False)
    vecs = _pack_vecs(conv_w_full, [conv_b, b_rg_a, b_rg_x, a_param, b_pool, pool_scale,
                                    norm_mix_g, norm_mlp_g, final_g.reshape(1, D)])
    g_first.forward(g_mlp.token)
    wg = dict(zip(first_names, g_first.finish(g_first.token)))

    h1, x_rnn, u_pool, ga, dga, sa, sb = _proj_fwd(x2d, modr, vecs, wg["w_in"])
    g_branch.forward(h1)
    xr, hr, za, p, pooled, *gates = _mix_fwd(x_rnn, u_pool, ga, vecs, wg["w_rg_a"], wg["w_rg_x"], wg["w_pool"],
                                             dep=g_branch.token)
    g_mlp.forward(za)
    wg.update(zip(branch_names, g_branch.finish(g_mlp.token)))
    ba, bb, merged, o, x2, h2 = _branch_fwd(za, pooled, sa, sb, x2d, modr, vecs,
                                            wg["w_branch_a"], wg["w_branch_b"], wg["w_out"])
    wg.update(zip(mlp_names, g_mlp.finish(h2)))
    ru, dx3, d_dn, small_f = _mlp_fwd(h2, x2, target, modr, vecs, wg["w_up"], wg["w_down"])

    near = _near(lax.axis_index("x"), lax.axis_index("y"), lax.axis_index("c"))
    own4 = jnp.stack([me, near[1][1], near[2][1], near[3][1]]).astype(jnp.int32)

    def scatter(group, partial, after, name):
        return _Scatter([partial[k] for k in group], [kind[k] for k in group], after, name)

    dup, dx2, do, small_m = _mlp_bwd(d_dn, ru, x2, dx3, o, modr, vecs, wg["w_up"], wg["w_down"])
    partial = dict(w_up=_wgrad(h2, dup, "wgrad_up"), w_down=_wgrad(ru, d_dn, "wgrad_down", square_a=True))
    s_mlp = scatter(mlp_names, partial, dx2, "scatter_mlp")

    dba, dbb, dgates, dza, dpooled = _branch_bwd(do, sa, sb, ba, bb, wg["w_branch_a"], wg["w_branch_b"], wg["w_out"],
                                                 dep=s_mlp.token)
    s_mlp.combine_and_send(own4, dza)
    dproj, dw_rg_a, dw_rg_x, dw_pool, small_x = _mix_bwd(dza, dpooled, x_rnn, ga, dga, xr, hr, p, gates, dgates,
                                                         vecs, wg["w_rg_a"], wg["w_rg_x"], wg["w_pool"],
                                                         dep=s_mlp.token)
    partial.update(w_branch_a=_wgrad(za, dba, "wgrad_branch_a"), w_branch_b=_wgrad(pooled, dbb, "wgrad_branch_b"),
                   w_out=_wgrad(merged, do, "wgrad_out"),
                   w_rg_a=dw_rg_a, w_rg_x=dw_rg_x, w_pool=dw_pool)
    mixer_names = ["w_rg_a", "w_rg_x", "w_pool", "w_branch_a", "w_branch_b", "w_out"]
    s_mixer = scatter(mixer_names, partial, s_mlp.token, "scatter_mixer")

    partial["w_in"] = _wgrad(h1, dproj, "wgrad_in", dep=s_mixer.token)
    s_in = scatter(["w_in"], partial, s_mixer.token, "scatter_in")
    s_mixer.combine_and_send(own4, s_in.token)
    s_in.combine_and_send(own4, s_mixer.token)
    grad_x, small_p = _proj_bwd(dproj, x2d, dx2, modr, vecs, wg["w_in"], dep=s_in.token)

    locals_ = dict(w_in=(w_in, m_w_in, v_w_in), w_up=(w_up, m_w_up, v_w_up), w_down=(w_down, m_w_down, v_w_down),
                   w_branch_a=(w_branch_a, m_w_branch_a, v_w_branch_a),
                   w_branch_b=(w_branch_b, m_w_branch_b, v_w_branch_b), w_out=(w_out, m_w_out, v_w_out),
                   w_rg_a=(w_rg_a, m_w_rg_a, v_w_rg_a), w_rg_x=(w_rg_x, m_w_rg_x, v_w_rg_x),
                   w_pool=(w_pool, m_w_pool, v_w_pool))
    res = {}

    def finish(group, exchange, after, n_tiles, name):
        chip_sums, arrivals = exchange.finish(after)
        flat = lambda t: t.reshape(-1, t.shape[-1])
        shapes = [flat(locals_[k][0]).shape for k in group]
        outs = _adam_group([cs.reshape(N_NEAR, *sh) for cs, sh in zip(chip_sums, shapes)],
                           [ar.reshape(N_NEAR - 1, *sh) for ar, sh in zip(arrivals, shapes)],
                           *[[flat(locals_[k][j]) for k in group] for j in range(3)], n_tiles, name)
        for i, k in enumerate(group):
            res[k] = [outs[j * len(group) + i].reshape(locals_[k][0].shape) for j in range(4)]
        return res[group[-1]][0]

    small = jnp.concatenate([small_f, small_m, small_x, small_p], axis=0)
    g_small = _Gather([small], [0], grad_x, "gather_small")
    done = finish(mlp_names, s_mlp, g_small.token, 4, "adam_mlp")
    done = finish(mixer_names, s_mixer, done, 2, "adam_mixer")
    g_small.forward(done)
    done = finish(["w_in"], s_in, g_small.token, 4, "adam_in")
    small_all, = g_small.finish(done)
    small_all = small_all.reshape(N_DEV, N_SMALL, D)

    def embed(cw):
        return lax.dynamic_update_slice(jnp.zeros((4, D), F32), cw[0], (0, me * (D // N_DEV)))

    def smalls(ng, nl, cb, bra, brx, ap, bp, ps, fg, ba_, cw):
        return [ng, nl, cb, bra, brx, ap, bp, ps, fg.reshape(1, D), ba_.reshape(6, D), embed(cw)]

    small_names = ["norm_mix_g", "norm_mlp_g", "conv_b", "b_rg_a", "b_rg_x", "a_param", "b_pool", "pool_scale",
                   "final_g", "b_ada", "conv_w"]
    fin = _small_finish(
        small_all, mod_all, vecs,
        smalls(norm_mix_g, norm_mlp_g, conv_b, b_rg_a, b_rg_x, a_param, b_pool, pool_scale, final_g, b_ada, conv_w),
        smalls(m_norm_mix_g, m_norm_mlp_g, m_conv_b, m_b_rg_a, m_b_rg_x, m_a_param, m_b_pool, m_pool_scale,
               m_final_g, m_b_ada, m_conv_w),
        smalls(v_norm_mix_g, v_norm_mlp_g, v_conv_b, v_b_rg_a, v_b_rg_x, v_a_param, v_b_pool, v_pool_scale,
               v_final_g, v_b_ada, v_conv_w))
    dmod_all, loss_tile = fin[4 * N_SMALL_PARAMS], fin[4 * N_SMALL_PARAMS + 1]
    dmod_cols = lax.dynamic_slice(dmod_all.reshape(N_DEV, 6 * D), (0, me * n_ada), (N_DEV, n_ada))
    res["w_ada"] = [t.reshape(w_ada.shape) for t in _ada_bwd_adam(c_all, dmod_cols, w_ada[0], m_w_ada[0], v_w_ada[0])]

    def final_shape(k, t):
        if k == "final_g":
            return t.reshape(D)
        if k == "b_ada":
            return t.reshape(1, 6 * D)
        if k == "conv_w":
            return lax.dynamic_slice(t, (0, me * (D // N_DEV)), (4, D // N_DEV)).reshape(conv_w.shape)
        return t

    for i, k in enumerate(small_names):
        res[k] = [final_shape(k, fin[which * N_SMALL_PARAMS + i]) for which in range(4)]
    order = ["norm_mix_g", "norm_mlp_g", "w_ada", "b_ada", "w_in", "conv_w", "conv_b", "w_rg_a", "b_rg_a", "w_rg_x",
             "b_rg_x", "a_param", "w_branch_a", "w_pool", "b_pool", "pool_scale", "w_branch_b", "w_out", "w_up",
             "w_down", "final_g"]
    outs = [loss_tile[0, 0], grad_x.reshape(x.shape)]
    for which in range(4):
        for k in order:
            outs.append(res[k][which])
    return tuple(outs)
```

```python
import functools

import jax
import jax.numpy as jnp
from jax import lax
from jax.experimental import pallas as pl
from jax.experimental.pallas import tpu as pltpu

F32 = jnp.float32
BF16 = jnp.bfloat16
MESH = pl.DeviceIdType.MESH

N_DEV = 8
D = 1024
N_GROUPS = 4
GW = D // N_GROUPS
D_IN = 5 * D
D_FF = 4 * D
POOL_WINDOWS = (2, 4, 8, 16)
HALO_X = 8
HALO_U = 16
EPS = 1e-6
C_RG = 8.0
ADAM_LR, ADAM_B1, ADAM_B2, ADAM_EPS, ADAM_WD, ADAM_STEP = 0.001, 0.9, 0.999, 1e-08, 0.01, 10

V7X_VMEM_LIMIT = 56 * 1024 * 1024

V_CONV_W, V_CONV_B, V_B_RG_A, V_B_RG_X, V_A_PARAM, V_B_POOL, V_POOL_SCALE, V_G1, V_G2, V_GF = 0, 4, 5, 6, 7, 8, 9, 10, 11, 12
M_SH1, M_SC1, M_GT1, M_SH2, M_SC2, M_GT2 = 0, 1, 2, 3, 4, 5

TM_PROJ = 512
TM_MIX = 256
TM_BRANCH = 512
TM_MLP = 512
TM_MLP_BWD = 256
TS_WGRAD = 1024


def _params(semantics):
    return pltpu.CompilerParams(dimension_semantics=semantics, vmem_limit_bytes=V7X_VMEM_LIMIT)


def _resident(shape):
    return pl.BlockSpec(shape, lambda *_: (0,) * len(shape), pipeline_mode=pl.Buffered(1))


def _dot(a, b):
    return jnp.dot(a, b, preferred_element_type=F32)


def _dot_nt(a, b):
    return lax.dot_general(a, b, (((1,), (1,)), ((), ())), preferred_element_type=F32)


def _dot_tn(a, b):
    return lax.dot_general(a, b, (((0,), (0,)), ((), ())), preferred_element_type=F32)


def _sigmoid(x):
    return 0.5 * jnp.tanh(0.5 * x) + 0.5


def _sigmoid_tail(x):
    return 1.0 / (1.0 + jnp.exp(-x))


def _gelu_and_grad(x):
    k = 0.7978845608028654
    x2 = x * x
    t = jnp.tanh(k * (x + 0.044715 * x * x2))
    g = 0.5 * x * (1.0 + t)
    dg = 0.5 * (1.0 + t) + 0.5 * x * (1.0 - t * t) * (k * (1.0 + 3.0 * 0.044715 * x2))
    return g, dg


def _softplus(a):
    e = jnp.exp(-jnp.abs(a))
    u = 1.0 + e
    log1p_e = jnp.where(u == 1.0, e, jnp.log(u) * e / jnp.where(u == 1.0, 1.0, u - 1.0))
    return jnp.maximum(a, 0.0) + log1p_e


def _neg_expm1(z):
    series = -(z * (1.0 + z * (0.5 + z * (1.0 / 6.0 + z * (1.0 / 24.0 + z * (1.0 / 120.0))))))
    return jnp.where(z > -0.1, series, 1.0 - jnp.exp(z))


def _shift_down(x, k):
    return pltpu.roll(x, k, 0)


def _shift_up(x, k):
    return pltpu.roll(x, x.shape[0] - k, 0)


def _rglru_gates(xr, w_a, w_x, b_a, b_x, a_param, is_t0):
    xb = xr.astype(BF16)
    ra = _sigmoid(_dot(xb, w_a) + b_a)
    ri = _sigmoid(_dot(xb, w_x) + b_x)
    sp = _softplus(a_param)
    log_a = (-C_RG) * ra * sp
    a = jnp.exp(log_a)
    mult = jnp.where(is_t0, 1.0, jnp.sqrt(_neg_expm1(2.0 * log_a)))
    return ra, ri, sp, a, mult


SUBLANES = 8


LANES = 128


def _scan_strip(a, b, carry, scr, down):
    t = b.shape[0]
    g = t // SUBLANES
    a3 = a.reshape(g, SUBLANES, LANES)
    b3 = b.reshape(g, SUBLANES, LANES)
    sub = lax.broadcasted_iota(jnp.int32, (g, SUBLANES, LANES), 1)
    for k in (1, 2, 4):
        keep = sub >= k if down else sub < SUBLANES - k
        shift = k if down else SUBLANES - k
        b3 = b3 + a3 * jnp.where(keep, pltpu.roll(b3, shift, 1), 0.0)
        a3 = a3 * jnp.where(keep, pltpu.roll(a3, shift, 1), 1.0)
    scr[0] = a3.reshape(t, LANES)
    scr[1] = b3.reshape(t, LANES)
    end_row = SUBLANES - 1 if down else 0
    ag = scr[0, pl.ds(end_row, g, stride=SUBLANES), :]
    bg = scr[1, pl.ds(end_row, g, stride=SUBLANES), :]
    rg = lax.broadcasted_iota(jnp.int32, (g, LANES), 0)
    edge = 0 if down else g - 1
    bg = bg + jnp.where(rg == edge, ag * carry, 0.0)
    k = 1
    while k < g:
        keep = rg >= k if down else rg < g - k
        shift = k if down else g - k
        bg = bg + ag * jnp.where(keep, pltpu.roll(bg, shift, 0), 0.0)
        if 2 * k < g:
            ag = ag * pltpu.roll(ag, shift, 0)
        k *= 2
    entering = jnp.where(rg != edge, pltpu.roll(bg, 1 if down else g - 1, 0), carry)
    for r in range(SUBLANES):
        scr[2, pl.ds(r, g, stride=SUBLANES), :] = entering
    return scr[1] + scr[0] * scr[2], bg[g - 1:g, :]


def _scan_strips(a, b, carry, scr, down):
    outs = [_scan_strip(a[:, c:c + LANES], b[:, c:c + LANES], carry[:, c:c + LANES], scr, down)
            for c in range(0, b.shape[1], LANES)]
    return jnp.concatenate([o[0] for o in outs], axis=1), jnp.concatenate([o[1] for o in outs], axis=1)


def _scan_down(a, b, carry, scr):
    return _scan_strips(a, b, carry, scr, True)


def _scan_up(m, b, carry, scr):
    return _scan_strips(m, b, carry, scr, False)[0]


def _window_mean(sums, window, first_block, head_t):
    scaled = sums * (1.0 / window)
    head = jnp.where(first_block, sums[:HALO_U] / jnp.minimum(head_t, float(window)), scaled[:HALO_U])
    return jnp.concatenate([head, scaled[HALO_U:]], axis=0)


def _conv_taps(x_ext):
    return [_shift_down(x_ext, 3 - j)[HALO_X:] if j < 3 else x_ext[HALO_X:] for j in range(4)]


def _proj_fwd(x, modr, vecs, w_in):
    s = x.shape[0]
    tm = min(TM_PROJ, s)

    def body(x_ref, mod_ref, vec_ref, w_ref, h1_ref, xrnn_ref, u_ref, ga_ref, dga_ref, sa_ref, sb_ref):
        xv = x_ref[...]
        r = lax.rsqrt(jnp.mean(xv * xv, axis=-1, keepdims=True) + EPS)
        gain = vec_ref[V_G1:V_G1 + 1, :] * (1.0 + mod_ref[M_SC1:M_SC1 + 1, :])
        h = (xv * r * gain + mod_ref[M_SH1:M_SH1 + 1, :]).astype(BF16)
        h1_ref[...] = h
        xrnn_ref[...] = _dot(h, w_ref[:, 0:D])
        ga_ref[...], dga_ref[...] = _gelu_and_grad(_dot(h, w_ref[:, D:2 * D]))
        u_ref[...] = _dot(h, w_ref[:, 2 * D:3 * D])
        sa_ref[...] = _sigmoid(_dot(h, w_ref[:, 3 * D:4 * D]))
        sb_ref[...] = _sigmoid(_dot(h, w_ref[:, 4 * D:5 * D]))

    tok = pl.BlockSpec((tm, D), lambda i: (i, 0))
    sd = lambda dt: jax.ShapeDtypeStruct((s, D), dt)
    return pl.pallas_call(
        body, name="proj_fwd", grid=(s // tm,),
        in_specs=[tok, pl.BlockSpec((8, D), lambda i: (0, 0)), pl.BlockSpec((16, D), lambda i: (0, 0)),
                  _resident((D, D_IN))],
        out_specs=[tok] * 7,
        out_shape=[sd(BF16)] + [sd(F32)] * 6,
        compiler_params=_params(("parallel",)),
    )(x, modr, vecs, w_in)


def _mix_fwd(x_rnn, u_pool, ga, vecs, w_rg_a, w_rg_x, w_pool, dep):
    s = x_rnn.shape[0]
    tm = min(TM_MIX, s)
    nb = s // tm

    def body(xh_ref, x_ref, uh_ref, u_ref, ga_ref, vec_ref, wa_ref, wx_ref, wp_ref, dep_ref,
             xr_ref, hr_ref, za_ref, p_ref, pooled_ref, a_ref, mult_ref, ra_ref, ri_ref, carry_ref, scan_scr):
        i = pl.program_id(0)
        first = i == 0

        @pl.when(first)
        def _():
            carry_ref[...] = jnp.zeros_like(carry_ref)

        row = lax.broadcasted_iota(jnp.int32, (tm, GW), 0)
        is_t0 = jnp.logical_and(first, row == 0)
        head_t = (lax.broadcasted_iota(jnp.int32, (HALO_U, GW), 0) + 1).astype(F32)
        for g in range(N_GROUPS):
            cs = slice(g * GW, (g + 1) * GW)
            vec = vec_ref[:, cs]
            xh = jnp.where(first, 0.0, xh_ref[:, cs])
            taps = _conv_taps(jnp.concatenate([xh, x_ref[:, cs]], axis=0))
            xr = vec[V_CONV_B:V_CONV_B + 1]
            for j in range(4):
                xr = xr + vec[V_CONV_W + j:V_CONV_W + j + 1] * taps[j]
            xr_ref[:, cs] = xr
            ra, ri, _, a, mult = _rglru_gates(
                xr, wa_ref[g], wx_ref[g], vec[V_B_RG_A:V_B_RG_A + 1], vec[V_B_RG_X:V_B_RG_X + 1],
                vec[V_A_PARAM:V_A_PARAM + 1], is_t0)
            a_ref[:, cs] = a
            mult_ref[:, cs] = mult
            ra_ref[:, cs] = ra.astype(BF16)
            ri_ref[:, cs] = ri.astype(BF16)
            h, last = _scan_down(a, xr * ri * mult, carry_ref[0:1, cs], scan_scr)
            hr_ref[:, cs] = h
            carry_ref[0:1, cs] = last
            za_ref[:, cs] = (ga_ref[:, cs] * h).astype(BF16)
            uh = jnp.where(first, 0.0, uh_ref[:, cs])
            sm = jnp.concatenate([uh, u_ref[:, cs]], axis=0)
            k = 1
            while k < POOL_WINDOWS[g]:
                sm = sm + _shift_down(sm, k)
                k *= 2
            mean = _window_mean(sm[HALO_U:], POOL_WINDOWS[g], first, head_t)
            p = (mean - u_ref[:, cs]).astype(BF16)
            p_ref[:, cs] = p
            pb = _dot(p, wp_ref[g]) + vec[V_B_POOL:V_B_POOL + 1]
            pooled_ref[:, cs] = (pb * vec[V_POOL_SCALE:V_POOL_SCALE + 1]).astype(BF16)

    tok = pl.BlockSpec((tm, D), lambda i: (i, 0))
    halo = lambda rows: pl.BlockSpec((rows, D), lambda i: (jnp.maximum(i * (tm // rows) - 1, 0), 0))
    wspec = pl.BlockSpec((N_GROUPS, GW, GW), lambda i: (0, 0, 0))
    sd = lambda dt: jax.ShapeDtypeStruct((s, D), dt)
    return pl.pallas_call(
        body, name="mix_fwd", grid=(nb,),
        in_specs=[halo(HALO_X), tok, halo(HALO_U), tok, tok, pl.BlockSpec((16, D), lambda i: (0, 0)),
                  wspec, wspec, wspec, pl.BlockSpec(memory_space=pl.ANY)],
        out_specs=[tok] * 9,
        out_shape=[sd(F32), sd(F32), sd(BF16), sd(BF16), sd(BF16), sd(F32), sd(F32), sd(BF16), sd(BF16)],
        scratch_shapes=[pltpu.VMEM((8, D), F32), pltpu.VMEM((3, tm, LANES), F32)],
        compiler_params=_params(("arbitrary",)),
    )(x_rnn, x_rnn, u_pool, u_pool, ga, vecs, w_rg_a, w_rg_x, w_pool, dep)


def _branch_fwd(za, pooled, sa, sb, x, modr, vecs, w_a, w_b, w_out):
    s = x.shape[0]
    tm = min(TM_BRANCH, s)

    def body(za_ref, pooled_ref, sa_ref, sb_ref, x_ref, mod_ref, vec_ref, wa_ref, wb_ref, wo_ref,
             ba_ref, bb_ref, merged_ref, o_ref, x2_ref, h2_ref):
        ba = _dot(za_ref[...], wa_ref[...])
        bb = _dot(pooled_ref[...], wb_ref[...])
        ba_ref[...] = ba.astype(BF16)
        bb_ref[...] = bb.astype(BF16)
        merged = (sa_ref[...] * ba + sb_ref[...] * bb).astype(BF16)
        merged_ref[...] = merged
        o = _dot(merged, wo_ref[...])
        o_ref[...] = o.astype(BF16)
        x2 = x_ref[...] + mod_ref[M_GT1:M_GT1 + 1, :] * o
        x2_ref[...] = x2
        r = lax.rsqrt(jnp.mean(x2 * x2, axis=-1, keepdims=True) + EPS)
        gain = vec_ref[V_G2:V_G2 + 1, :] * (1.0 + mod_ref[M_SC2:M_SC2 + 1, :])
        h2_ref[...] = (x2 * r * gain + mod_ref[M_SH2:M_SH2 + 1, :]).astype(BF16)

    tok = pl.BlockSpec((tm, D), lambda i: (i, 0))
    wspec = pl.BlockSpec((D, D), lambda i: (0, 0))
    sd = lambda dt: jax.ShapeDtypeStruct((s, D), dt)
    return pl.pallas_call(
        body, name="branch_fwd", grid=(s // tm,),
        in_specs=[tok, tok, tok, tok,
                  tok, pl.BlockSpec((8, D), lambda i: (0, 0)), pl.BlockSpec((16, D), lambda i: (0, 0)),
                  wspec, wspec, wspec],
        out_specs=[tok] * 6,
        out_shape=[sd(BF16), sd(BF16), sd(BF16), sd(BF16), sd(F32), sd(BF16)],
        compiler_params=_params(("parallel",)),
    )(za, pooled, sa, sb, x, modr, vecs, w_a, w_b, w_out)


def _mlp_fwd(h2, x2, target, modr, vecs, w_up, w_down):
    s = x2.shape[0]
    tm = min(TM_MLP, s)

    def body(h2_ref, x2_ref, tgt_ref, mod_ref, vec_ref, wu_ref, wd_ref,
             ru_ref, dx3_ref, ddn_ref, small_ref):
        @pl.when(pl.program_id(0) == 0)
        def _():
            small_ref[...] = jnp.zeros_like(small_ref)

        h2 = h2_ref[...]
        dn = None
        for c in range(D_FF // D):
            cs = slice(c * D, (c + 1) * D)
            ru = jnp.maximum(_dot(h2, wu_ref[:, cs]), 0.0)
            ru_ref[:, cs] = ru.astype(BF16)
            part = _dot((ru * ru).astype(BF16), wd_ref[cs, :])
            dn = part if dn is None else dn + part
        gt2 = mod_ref[M_GT2:M_GT2 + 1, :]
        gf = vec_ref[V_GF:V_GF + 1, :]
        x3 = x2_ref[...] + gt2 * dn
        r3 = lax.rsqrt(jnp.mean(x3 * x3, axis=-1, keepdims=True) + EPS)
        n3 = x3 * r3
        err = n3 * gf - tgt_ref[...]
        dy = err * (1.0 / D)
        dn3 = dy * gf
        dx3 = r3 * (dn3 - n3 * jnp.mean(dn3 * n3, axis=-1, keepdims=True))
        dx3_ref[...] = dx3
        ddn_ref[...] = (dx3 * gt2).astype(BF16)
        small_ref[0:1, :] += jnp.sum(dy * n3, axis=0, keepdims=True)
        small_ref[1:2, :] += jnp.sum(dx3 * dn, axis=0, keepdims=True)
        small_ref[2:3, :] += (0.5 / D) * jnp.sum(err * err, axis=0, keepdims=True)

    tok = pl.BlockSpec((tm, D), lambda i: (i, 0))
    return pl.pallas_call(
        body, name="mlp_fwd", grid=(s // tm,),
        in_specs=[tok, tok, tok,
                  pl.BlockSpec((8, D), lambda i: (0, 0)), pl.BlockSpec((16, D), lambda i: (0, 0)),
                  _resident((D, D_FF)), _resident((D_FF, D))],
        out_specs=[pl.BlockSpec((tm, D_FF), lambda i: (i, 0)), tok, tok,
                   pl.BlockSpec((8, D), lambda i: (0, 0))],
        out_shape=[jax.ShapeDtypeStruct((s, D_FF), BF16), jax.ShapeDtypeStruct((s, D), F32),
                   jax.ShapeDtypeStruct((s, D), BF16), jax.ShapeDtypeStruct((8, D), F32)],
        compiler_params=_params(("arbitrary",)),
    )(h2, x2, target, modr, vecs, w_up, w_down)


def _mlp_bwd(d_dn, ru, x2, dx3, o, modr, vecs, w_up, w_down):
    s = x2.shape[0]
    tm = min(TM_MLP_BWD, s)

    def body(ddn_ref, ru_ref, x2_ref, dx3_ref, o_ref, mod_ref, vec_ref, wu_ref, wd_ref,
             dup_ref, dx2_ref, do_ref, small_ref):
        @pl.when(pl.program_id(0) == 0)
        def _():
            small_ref[...] = jnp.zeros_like(small_ref)

        ddn = ddn_ref[...]
        dh2 = None
        for c in range(D_FF // D):
            cs = slice(c * D, (c + 1) * D)
            dff = _dot_nt(ddn, wd_ref[cs, :])
            dup = (dff * (2.0 * ru_ref[:, cs].astype(F32))).astype(BF16)
            dup_ref[:, cs] = dup
            part = _dot_nt(dup, wu_ref[:, cs])
            dh2 = part if dh2 is None else dh2 + part
        x2 = x2_ref[...]
        r2 = lax.rsqrt(jnp.mean(x2 * x2, axis=-1, keepdims=True) + EPS)
        xn2 = x2 * r2
        gain = vec_ref[V_G2:V_G2 + 1, :] * (1.0 + mod_ref[M_SC2:M_SC2 + 1, :])
        dxn2 = dh2 * gain
        dx2 = dx3_ref[...] + r2 * (dxn2 - xn2 * jnp.mean(dxn2 * xn2, axis=-1, keepdims=True))
        dx2_ref[...] = dx2
        do_ref[...] = (dx2 * mod_ref[M_GT1:M_GT1 + 1, :]).astype(BF16)
        small_ref[0:1, :] += jnp.sum(dh2, axis=0, keepdims=True)
        small_ref[1:2, :] += jnp.sum(dh2 * xn2, axis=0, keepdims=True)
        small_ref[2:3, :] += jnp.sum(dx2 * o_ref[...].astype(F32), axis=0, keepdims=True)

    tok = pl.BlockSpec((tm, D), lambda i: (i, 0))
    wide = pl.BlockSpec((tm, D_FF), lambda i: (i, 0))
    return pl.pallas_call(
        body, name="mlp_bwd", grid=(s // tm,),
        in_specs=[tok, wide, tok, tok, tok,
                  pl.BlockSpec((8, D), lambda i: (0, 0)), pl.BlockSpec((16, D), lambda i: (0, 0)),
                  _resident((D, D_FF)), _resident((D_FF, D))],
        out_specs=[wide, tok, tok, pl.BlockSpec((8, D), lambda i: (0, 0))],
        out_shape=[jax.ShapeDtypeStruct((s, D_FF), BF16), jax.ShapeDtypeStruct((s, D), F32),
                   jax.ShapeDtypeStruct((s, D), BF16), jax.ShapeDtypeStruct((8, D), F32)],
        compiler_params=_params(("arbitrary",)),
    )(d_dn, ru, x2, dx3, o, modr, vecs, w_up, w_down)


def _branch_bwd(do, sa, sb, ba, bb, w_a, w_b, w_out, dep):
    s = do.shape[0]
    tm = min(TM_BRANCH, s)

    def body(do_ref, sa_ref, sb_ref, ba_ref, bb_ref, wa_ref, wb_ref, wo_ref, dep_ref,
             dba_ref, dbb_ref, dg_ref, dza_ref, dpooled_ref):
        dmerged = _dot_nt(do_ref[...], wo_ref[...])
        sa = sa_ref[...]
        sb = sb_ref[...]
        dba = (dmerged * sa).astype(BF16)
        dbb = (dmerged * sb).astype(BF16)
        dba_ref[...] = dba
        dbb_ref[...] = dbb
        dg_ref[:, :D] = (dmerged * ba_ref[...].astype(F32) * sa * (1.0 - sa)).astype(BF16)
        dg_ref[:, D:] = (dmerged * bb_ref[...].astype(F32) * sb * (1.0 - sb)).astype(BF16)
        dza_ref[...] = _dot_nt(dba, wa_ref[...])
        dpooled_ref[...] = _dot_nt(dbb, wb_ref[...])

    tok = pl.BlockSpec((tm, D), lambda i: (i, 0))
    wspec = pl.BlockSpec((D, D), lambda i: (0, 0))
    sd = lambda dt: jax.ShapeDtypeStruct((s, D), dt)
    return pl.pallas_call(
        body, name="branch_bwd", grid=(s // tm,),
        in_specs=[tok, tok, tok, tok, tok, wspec, wspec, wspec, pl.BlockSpec(memory_space=pl.ANY)],
        out_specs=[tok, tok, pl.BlockSpec((tm, 2 * D), lambda i: (i, 0)), tok, tok],
        out_shape=[sd(BF16), sd(BF16), jax.ShapeDtypeStruct((s, 2 * D), BF16), sd(F32), sd(F32)],
        compiler_params=_params(("parallel",)),
    )(do, sa, sb, ba, bb, w_a, w_b, w_out, dep)


def _mix_bwd(dza, dpooled, x_rnn, ga, dga, xr, hr, p, gates, dgates, vecs, w_rg_a, w_rg_x, w_pool, dep):
    s = xr.shape[0]
    tm = min(TM_MIX, s)
    nb = s // tm

    def body(dza_ref, dpooled_ref, xh_ref, x_ref, ga_ref, dga_ref, xr_ref, hh_ref, hr_ref, p_ref,
             a_ref, mult_ref, ra_ref, ri_ref, dg_ref, vec_ref, wa_ref, wx_ref, wp_ref, dep_ref,
             dproj_ref, dwa_ref, dwx_ref, dwp_ref, small_ref,
             scan_carry, dxr_carry, q_carry, scan_scr, dwa_acc, dwx_acc, dwp_acc):
        i = pl.program_id(0)
        bi = nb - 1 - i
        first_t = bi == 0

        @pl.when(i == 0)
        def _():
            scan_carry[...] = jnp.zeros_like(scan_carry)
            dxr_carry[...] = jnp.zeros_like(dxr_carry)
            q_carry[...] = jnp.zeros_like(q_carry)
            dwa_acc[...] = jnp.zeros_like(dwa_acc)
            dwx_acc[...] = jnp.zeros_like(dwx_acc)
            dwp_acc[...] = jnp.zeros_like(dwp_acc)
            small_ref[...] = jnp.zeros_like(small_ref)

        row = lax.broadcasted_iota(jnp.int32, (tm, GW), 0)
        is_t0 = jnp.logical_and(first_t, row == 0)
        head_t = (lax.broadcasted_iota(jnp.int32, (HALO_U, GW), 0) + 1).astype(F32)
        colsum = lambda v: jnp.sum(v, axis=0, keepdims=True)
        for g in range(N_GROUPS):
            cs = slice(g * GW, (g + 1) * GW)
            vec = vec_ref[:, cs]
            xr = xr_ref[:, cs]
            hr = hr_ref[:, cs]
            dza = dza_ref[:, cs]
            dproj_ref[:, D + g * GW:D + (g + 1) * GW] = (dza * hr * dga_ref[:, cs]).astype(BF16)
            dhr = dza * ga_ref[:, cs]
            a = a_ref[:, cs]
            mult = mult_ref[:, cs]
            ra = ra_ref[:, cs].astype(F32)
            ri = ri_ref[:, cs].astype(F32)
            sp = _softplus(vec[V_A_PARAM:V_A_PARAM + 1])
            m = jnp.where(row == tm - 1, 1.0, _shift_up(a, 1))
            gsum = _scan_up(m, dhr, scan_carry[0:1, cs], scan_scr)
            scan_carry[0:1, cs] = a[0:1, :] * gsum[0:1, :]
            hh = jnp.where(first_t, 0.0, hh_ref[:, cs])
            hprev = _shift_down(jnp.concatenate([hh, hr], axis=0), 1)[8:]
            da = gsum * hprev
            dmult = jnp.where(is_t0, 0.0, gsum * xr * ri)
            dlog_a = da * a - dmult * a * a / mult
            dri = gsum * xr * mult
            dxr = gsum * ri * mult
            small_ref[7:8, cs] += colsum((-C_RG) * ra * dlog_a)
            dpa = (((-C_RG) * sp) * dlog_a * ra * (1.0 - ra))
            dpx = dri * ri * (1.0 - ri)
            small_ref[5:6, cs] += colsum(dpa)
            small_ref[6:7, cs] += colsum(dpx)
            dpa = dpa.astype(BF16)
            dpx = dpx.astype(BF16)
            xrb = xr.astype(BF16)
            dwa_acc[g] += _dot_tn(xrb, dpa)
            dwx_acc[g] += _dot_tn(xrb, dpx)
            dxr = dxr + _dot_nt(dpa, wa_ref[g]) + _dot_nt(dpx, wx_ref[g])
            small_ref[4:5, cs] += colsum(dxr)
            xh = jnp.where(first_t, 0.0, xh_ref[:, cs])
            taps = _conv_taps(jnp.concatenate([xh, x_ref[:, cs]], axis=0))
            dxr_ext = jnp.concatenate([dxr, dxr_carry[:, cs]], axis=0)
            dx = vec[V_CONV_W + 3:V_CONV_W + 4] * dxr
            for j in range(4):
                small_ref[j:j + 1, cs] += colsum(dxr * taps[j])
                if j < 3:
                    dx = dx + vec[V_CONV_W + j:V_CONV_W + j + 1] * _shift_up(dxr_ext, 3 - j)[:tm]
            dxr_carry[:, cs] = dxr[0:8, :]
            dproj_ref[:, cs] = dx.astype(BF16)
            pg = p_ref[:, cs]
            dpooled = dpooled_ref[:, cs]
            pb = _dot(pg, wp_ref[g]) + vec[V_B_POOL:V_B_POOL + 1]
            small_ref[9:10, cs] += colsum(dpooled * pb)
            dpb = dpooled * vec[V_POOL_SCALE:V_POOL_SCALE + 1]
            small_ref[8:9, cs] += colsum(dpb)
            dpbb = dpb.astype(BF16)
            dwp_acc[g] += _dot_tn(pg, dpbb)
            dp = _dot_nt(dpbb, wp_ref[g])
            q = _window_mean(dp, POOL_WINDOWS[g], first_t, head_t)
            sm = jnp.concatenate([q, q_carry[:, cs]], axis=0)
            k = 1
            while k < POOL_WINDOWS[g]:
                sm = sm + _shift_up(sm, k)
                k *= 2
            q_carry[:, cs] = q[0:HALO_U, :]
            dproj_ref[:, 2 * D + g * GW:2 * D + (g + 1) * GW] = (sm[:tm] - dp).astype(BF16)
        dproj_ref[:, 3 * D:] = dg_ref[...]

        @pl.when(i == nb - 1)
        def _():
            dwa_ref[...] = dwa_acc[...].astype(BF16)
            dwx_ref[...] = dwx_acc[...].astype(BF16)
            dwp_ref[...] = dwp_acc[...].astype(BF16)

    rev = lambda i: nb - 1 - i
    tok = pl.BlockSpec((tm, D), lambda i: (rev(i), 0))
    halo8 = lambda k: pl.BlockSpec((8, D), lambda i: (jnp.maximum(rev(i) * (tm // 8) - 1, 0), k))
    wspec = pl.BlockSpec((N_GROUPS, GW, GW), lambda i: (0, 0, 0))
    wshape = jax.ShapeDtypeStruct((N_GROUPS, GW, GW), BF16)
    return pl.pallas_call(
        body, name="mix_bwd", grid=(nb,),
        in_specs=[tok, tok, halo8(0), tok, tok, tok, tok, halo8(0), tok, tok, tok, tok, tok, tok,
                  pl.BlockSpec((tm, 2 * D), lambda i: (rev(i), 0)),
                  pl.BlockSpec((16, D), lambda i: (0, 0)), wspec, wspec, wspec, pl.BlockSpec(memory_space=pl.ANY)],
        out_specs=[pl.BlockSpec((tm, D_IN), lambda i: (rev(i), 0)), wspec, wspec, wspec,
                   pl.BlockSpec((16, D), lambda i: (0, 0))],
        out_shape=[jax.ShapeDtypeStruct((s, D_IN), BF16), wshape, wshape, wshape,
                   jax.ShapeDtypeStruct((16, D), F32)],
        scratch_shapes=[pltpu.VMEM((8, D), F32), pltpu.VMEM((8, D), F32), pltpu.VMEM((HALO_U, D), F32),
                        pltpu.VMEM((3, tm, LANES), F32)] + [pltpu.VMEM((N_GROUPS, GW, GW), F32)] * 3,
        compiler_params=_params(("arbitrary",)),
    )(dza, dpooled, x_rnn, x_rnn, ga, dga, xr, hr, hr, p, *gates, dgates, vecs, w_rg_a, w_rg_x, w_pool, dep)


def _proj_bwd(dproj, x, dx2, modr, vecs, w_in, dep):
    s = x.shape[0]
    tm = min(TM_PROJ, s)

    def body(dp_ref, x_ref, dx2_ref, mod_ref, vec_ref, w_ref, dep_ref, gx_ref, small_ref):
        @pl.when(pl.program_id(0) == 0)
        def _():
            small_ref[...] = jnp.zeros_like(small_ref)

        dh1 = None
        for c in range(D_IN // D):
            cs = slice(c * D, (c + 1) * D)
            part = _dot_nt(dp_ref[:, cs], w_ref[:, cs])
            dh1 = part if dh1 is None else dh1 + part
        xv = x_ref[...]
        r1 = lax.rsqrt(jnp.mean(xv * xv, axis=-1, keepdims=True) + EPS)
        xn1 = xv * r1
        gain = vec_ref[V_G1:V_G1 + 1, :] * (1.0 + mod_ref[M_SC1:M_SC1 + 1, :])
        dxn1 = dh1 * gain
        gx_ref[...] = dx2_ref[...] + r1 * (dxn1 - xn1 * jnp.mean(dxn1 * xn1, axis=-1, keepdims=True))
        small_ref[0:1, :] += jnp.sum(dh1, axis=0, keepdims=True)
        small_ref[1:2, :] += jnp.sum(dh1 * xn1, axis=0, keepdims=True)

    tok = pl.BlockSpec((tm, D), lambda i: (i, 0))
    return pl.pallas_call(
        body, name="proj_bwd", grid=(s // tm,),
        in_specs=[pl.BlockSpec((tm, D_IN), lambda i: (i, 0)), tok, tok,
                  pl.BlockSpec((8, D), lambda i: (0, 0)), pl.BlockSpec((16, D), lambda i: (0, 0)),
                  _resident((D, D_IN)), pl.BlockSpec(memory_space=pl.ANY)],
        out_specs=[tok, pl.BlockSpec((8, D), lambda i: (0, 0))],
        out_shape=[jax.ShapeDtypeStruct((s, D), F32), jax.ShapeDtypeStruct((8, D), F32)],
        compiler_params=_params(("arbitrary",)),
    )(dproj, x, dx2, modr, vecs, w_in, dep)


def _wgrad(a, b, name, square_a=False, dep=None):
    s, ka = a.shape
    n = b.shape[1]
    tka = ka if ka <= 1024 else ka // 2
    tn = n if n <= 1024 else n // 2
    ts = min(TS_WGRAD, s)
    ns = s // ts
    nc = 512
    deps = [] if dep is None else [dep]

    def body(a_ref, b_ref, *refs):
        out_ref, acc_ref = refs[-2:]
        t = pl.program_id(2)

        @pl.when(t == 0)
        def _():
            acc_ref[...] = jnp.zeros_like(acc_ref)

        av = a_ref[...]
        if square_a:
            af = av.astype(F32)
            av = (af * af).astype(BF16)
        for c in range(tn // nc):
            cs = slice(c * nc, (c + 1) * nc)
            acc_ref[:, cs] += _dot_tn(av, b_ref[:, cs])

        @pl.when(t == ns - 1)
        def _():
            out_ref[...] = acc_ref[...].astype(BF16)

    return pl.pallas_call(
        body, name=name, grid=(ka // tka, n // tn, ns),
        in_specs=[pl.BlockSpec((ts, tka), lambda i, j, t: (t, i)),
                  pl.BlockSpec((ts, tn), lambda i, j, t: (t, j))] + [pl.BlockSpec(memory_space=pl.ANY)] * len(deps),
        out_specs=pl.BlockSpec((tka, tn), lambda i, j, t: (i, j)),
        out_shape=jax.ShapeDtypeStruct((ka, n), BF16),
        scratch_shapes=[pltpu.VMEM((tka, tn), F32)],
        compiler_params=_params(("parallel", "parallel", "arbitrary")),
    )(a, b, *deps)


def _window(ref, kind, idx, size):
    start = pl.multiple_of(idx * size, size)
    if kind == 0:
        return ref.at[pl.ds(start, size)]
    if kind == 1:
        return ref.at[:, pl.ds(start, size)]
    return ref.at[:, :, pl.ds(start, size)]


def _mesh_place():
    x, y, c = lax.axis_index("x"), lax.axis_index("y"), lax.axis_index("c")
    return x, y, c, 4 * x + 2 * y + c


def _peer(x, y, c, q):
    px = 1 - x if q & 4 else x
    py = 1 - y if q & 2 else y
    pc = 1 - c if q & 1 else c
    return (px, py, pc), 4 * px + 2 * py + pc


_HBM = pl.BlockSpec(memory_space=pltpu.HBM)
_SEM = pl.BlockSpec(memory_space=pltpu.SEMAPHORE)
_EFFECT = pltpu.SideEffectType.DATAFLOW_SIDE_EFFECTING


N_NEAR = 4


def _near(x, y, c):
    out = [((x, y, 1 - c), 4 * x + 2 * y + 1 - c)]
    for j in (1, 2, 3):
        px = 1 - x if j & 2 else x
        py = 1 - y if j & 1 else y
        out.append(((px, py, c), 4 * px + 2 * py + c))
    return out


def _remote(src, dst, send_sems, recv_sems, slot, device):
    return pltpu.make_async_remote_copy(src_ref=src, dst_ref=dst, send_sem=send_sems.at[slot], recv_sem=recv_sems.at[slot],
                                        device_id=device, device_id_type=MESH)


def _split_call(name, arrays, sems_in, n_new_sems, after, emit):
    na, ns, nn = len(arrays), len(sems_in), len(n_new_sems)

    def body(*refs):
        emit(refs[:na], refs[na:na + ns], refs[na + ns + 1:na + ns + 1 + nn])
        refs[-1][...] = jnp.zeros_like(refs[-1])

    outs = pl.pallas_call(
        body, name=name,
        out_shape=(*[pltpu.SemaphoreType.DMA((m,)) for m in n_new_sems],
                   *[pltpu.HBM(a.shape, a.dtype) for a in arrays], jax.ShapeDtypeStruct((8, 128), F32)),
        in_specs=[_HBM] * na + [_SEM] * ns + [pl.BlockSpec(memory_space=pl.ANY)],
        out_specs=(*[_SEM] * nn, *[_HBM] * na, pl.BlockSpec(memory_space=pltpu.VMEM)),
        input_output_aliases={i: nn + i for i in range(na)},
        compiler_params=pltpu.CompilerParams(has_side_effects=_EFFECT),
    )(*[pltpu.with_memory_space_constraint(a, pltpu.HBM) for a in arrays], *sems_in, after)
    return list(outs[:nn]), list(outs[nn:nn + na]), outs[-1]


class _Gather:
    def __init__(self, shards, kinds, after, name):
        self.n, self.kinds, self.name = len(shards), kinds, name
        self.sizes = [s.shape[k] for s, k in zip(shards, kinds)]
        n = self.n
        lands = []
        for s, k in zip(shards, kinds):
            dims = list(s.shape)
            dims[k] *= N_DEV
            lands.append(lax.empty(tuple(dims), s.dtype))

        def emit(arr, _, new):
            x, y, c, me = _mesh_place()
            for k in range(n):
                pltpu.make_async_copy(arr[k], _window(arr[n + k], kinds[k], me, self.sizes[k]), new[2].at[k]).start()
            for k in range(n):
                mine = _window(arr[n + k], kinds[k], me, self.sizes[k])
                for j, (dev, _) in enumerate(_near(x, y, c)):
                    _remote(arr[k], mine, new[0], new[1], k * N_NEAR + j, dev).start()

        self.sems, self.arrays, self.token = _split_call(name + "_start", [*shards, *lands], [],
                                                         [n * N_NEAR, n * N_NEAR, n], after, emit)

    def forward(self, after):
        n, kinds, sizes = self.n, self.kinds, self.sizes

        def emit(arr, old, new):
            x, y, c, _ = _mesh_place()
            near = _near(x, y, c)
            for k in range(n):
                for j in (1, 2, 3):
                    dev, idx = near[j]
                    landed = _window(arr[n + k], kinds[k], idx, sizes[k])
                    _remote(arr[k], landed, old[0], old[1], k * N_NEAR + j, dev).wait_recv()
                    _remote(landed, landed, new[0], new[1], k * N_NEAR + j, near[0][0]).start()

        new, self.arrays, self.token = _split_call(self.name + "_forward", self.arrays, self.sems, [n * N_NEAR] * 2,
                                                   after, emit)
        self.sems = [*self.sems, *new]

    def finish(self, after):
        n, kinds, sizes = self.n, self.kinds, self.sizes

        def emit(arr, old, _):
            x, y, c, me = _mesh_place()
            near = _near(x, y, c)
            other_core = near[0][0]
            for k in range(n):
                win = lambda idx: _window(arr[n + k], kinds[k], idx, sizes[k])
                pltpu.make_async_copy(arr[k], win(me), old[2].at[k]).wait()
                for j, (dev, idx) in enumerate(near):
                    _remote(arr[k], win(me), old[0], old[1], k * N_NEAR + j, dev).wait_send()
                _remote(arr[k], win(near[0][1]), old[0], old[1], k * N_NEAR, other_core).wait_recv()
                for j in (1, 2, 3):
                    idx = near[j][1]
                    _remote(win(idx), win(idx), old[3], old[4], k * N_NEAR + j, other_core).wait_send()
                    _remote(arr[k], win(idx + 1 - 2 * c), old[3], old[4], k * N_NEAR + j, other_core).wait_recv()

        _, arrays, _ = _split_call(self.name + "_finish", self.arrays, self.sems, [], after, emit)
        return arrays[n:]


class _Spread:
    def __init__(self, shards, kinds, after, name):
        self.n, self.kinds, self.name = len(shards), kinds, name
        self.sizes = [s.shape[k] for s, k in zip(shards, kinds)]
        n = self.n
        lands = []
        for s, k in zip(shards, kinds):
            dims = list(s.shape)
            dims[k] *= N_DEV
            lands.append(lax.empty(tuple(dims), s.dtype))

        def emit(arr, _, new):
            x, y, c, me = _mesh_place()
            for k in range(n):
                mine = _window(arr[n + k], kinds[k], me, self.sizes[k])
                pltpu.make_async_copy(arr[k], mine, new[2].at[k]).start()
                for q in range(1, N_DEV):
                    _remote(arr[k], mine, new[0], new[1], k * N_DEV + q, _peer(x, y, c, q)[0]).start()

        self.sems, self.arrays, self.token = _split_call(name + "_start", [*shards, *lands], [],
                                                         [n * N_DEV, n * N_DEV, n], after, emit)

    def finish(self, after):
        n, kinds, sizes = self.n, self.kinds, self.sizes

        def emit(arr, old, _):
            x, y, c, me = _mesh_place()
            for k in range(n):
                win = lambda idx: _window(arr[n + k], kinds[k], idx, sizes[k])
                pltpu.make_async_copy(arr[k], win(me), old[2].at[k]).wait()
                for q in range(1, N_DEV):
                    peer, peer_idx = _peer(x, y, c, q)
                    _remote(arr[k], win(me), old[0], old[1], k * N_DEV + q, peer).wait_send()
                    _remote(arr[k], win(peer_idx), old[0], old[1], k * N_DEV + q, peer).wait_recv()

        _, arrays, _ = _split_call(self.name + "_finish", self.arrays, self.sems, [], after, emit)
        return arrays[n:]


class _Scatter:
    def __init__(self, partials, kinds, after, name):
        self.n, self.kinds, self.name, self.partials = len(partials), kinds, name, partials
        self.sizes = [p.shape[k] // N_DEV for p, k in zip(partials, kinds)]
        n, sizes = self.n, self.sizes
        self.slot_shapes = []
        for p, k, size in zip(partials, kinds, sizes):
            dims = list(p.shape)
            dims[k] = size
            self.slot_shapes.append((N_NEAR, *dims))
        slots = [lax.empty(sh, p.dtype) for sh, p in zip(self.slot_shapes, partials)]

        def emit(arr, _, new):
            x, y, c, _ = _mesh_place()
            near = _near(x, y, c)
            for k in range(n):
                for j in range(N_NEAR):
                    owner = near[j][1] if j == 0 else near[j][1] + 1 - 2 * c
                    _remote(_window(arr[k], kinds[k], owner, sizes[k]), arr[n + k].at[j], new[0], new[1],
                            k * N_NEAR + j, near[0][0]).start()

        self.sems, self.arrays, self.token = _split_call(name + "_start", [*partials, *slots], [], [n * N_NEAR] * 2,
                                                         after, emit)

    def combine_and_send(self, own4, after):
        n, kinds, sizes = self.n, self.kinds, self.sizes

        def emit_wait(arr, old, _):
            x, y, c, _ = _mesh_place()
            near = _near(x, y, c)
            for k in range(n):
                for j in range(N_NEAR):
                    owner = near[j][1] if j == 0 else near[j][1] + 1 - 2 * c
                    cp = _remote(_window(arr[k], kinds[k], owner, sizes[k]), arr[n + k].at[j], old[0], old[1],
                                 k * N_NEAR + j, near[0][0])
                    cp.wait_send()
                    cp.wait_recv()

        _, arrays, _ = _split_call(self.name + "_landed", self.arrays, self.sems, [], after, emit_wait)
        chip_sums = _chip_sums(arrays[:n], arrays[n:], kinds, sizes, own4, self.name + "_combine")
        arrivals = [lax.empty((N_NEAR - 1, *sh[1:]), p.dtype) for sh, p in zip(self.slot_shapes, self.partials)]

        def emit_send(arr, _, new):
            x, y, c, _ = _mesh_place()
            near = _near(x, y, c)
            for k in range(n):
                for j in (1, 2, 3):
                    _remote(arr[k].at[j], arr[n + k].at[j - 1], new[0], new[1], k * N_NEAR + j, near[j][0]).start()

        self.sems, self.arrays, self.token = _split_call(self.name + "_send", [*chip_sums, *arrivals], [],
                                                         [n * N_NEAR] * 2, own4, emit_send)

    def finish(self, after):
        n = self.n

        def emit(arr, old, _):
            x, y, c, _ = _mesh_place()
            near = _near(x, y, c)
            for k in range(n):
                for j in (1, 2, 3):
                    cp = _remote(arr[k].at[j], arr[n + k].at[j - 1], old[0], old[1], k * N_NEAR + j, near[j][0])
                    cp.wait_send()
                    cp.wait_recv()

        _, arrays, _ = _split_call(self.name + "_finish", self.arrays, self.sems, [], after, emit)
        return arrays[:n], arrays[n:]


def _chip_sums(partials, slots, kinds, sizes, own4, name):
    n = len(partials)

    def body(own_ref, *refs):
        for k in range(n):
            refs[2 * n + k][...] = (refs[k][...].astype(F32) + refs[n + k][...].astype(F32)).astype(BF16)

    in_specs, slot_specs = [], []
    for p, s, kind, size in zip(partials, slots, kinds, sizes):
        block = list(p.shape)
        block[kind] = size
        nd = len(block)
        in_specs.append(pl.BlockSpec(tuple(block), functools.partial(
            lambda j, own, kind, nd: tuple(own[j] if d == kind else 0 for d in range(nd)), kind=kind, nd=nd)))
        slot_specs.append(pl.BlockSpec((None, *block), functools.partial(
            lambda j, own, nd: (j,) + (0,) * nd, nd=nd)))
    return pl.pallas_call(
        body, name=name,
        grid_spec=pltpu.PrefetchScalarGridSpec(num_scalar_prefetch=1, grid=(N_NEAR,),
                                               in_specs=in_specs + slot_specs, out_specs=slot_specs),
        out_shape=[jax.ShapeDtypeStruct(s.shape, s.dtype) for s in slots],
        compiler_params=_params(("arbitrary",)),
    )(own4, *partials, *slots)


def _to_bf16(arrays):
    def body(*refs):
        for src, dst in zip(refs[:len(arrays)], refs[len(arrays):]):
            dst[...] = src[...].astype(BF16)

    return pl.pallas_call(body, name="to_bf16", out_shape=[jax.ShapeDtypeStruct(a.shape, BF16) for a in arrays],
                          compiler_params=pltpu.CompilerParams(vmem_limit_bytes=V7X_VMEM_LIMIT))(*arrays)


def _silu(c):
    return c * _sigmoid_tail(c)


def _ada_fwd(c_all, w_ada, b_ada_cols, dep):
    def body(c_ref, w_ref, b_ref, dep_ref, out_ref):
        out_ref[...] = jnp.dot(_silu(c_ref[...]), w_ref[...], preferred_element_type=F32,
                               precision=lax.Precision.HIGHEST) + b_ref[...]

    vmem = pl.BlockSpec(memory_space=pltpu.VMEM)
    return pl.pallas_call(
        body, name="ada_fwd", in_specs=[vmem, vmem, vmem, pl.BlockSpec(memory_space=pl.ANY)], out_specs=vmem,
        out_shape=jax.ShapeDtypeStruct((N_DEV, w_ada.shape[1]), F32),
    )(c_all, w_ada, b_ada_cols, dep)


def _adam(w, g, m, v):
    m = ADAM_B1 * m + (1.0 - ADAM_B1) * g
    v = ADAM_B2 * v + (1.0 - ADAM_B2) * (g * g)
    m_hat = m / (1.0 - ADAM_B1 ** ADAM_STEP)
    v_hat = v / (1.0 - ADAM_B2 ** ADAM_STEP)
    delta = -ADAM_LR * (m_hat / (jnp.sqrt(v_hat) + ADAM_EPS) + ADAM_WD * w)
    return delta, m, v


def _ada_bwd_adam(c_all, dmod_cols, w, m, v):
    def body(c_ref, d_ref, w_ref, m_ref, v_ref, g_ref, delta_ref, nm_ref, nv_ref):
        g = lax.dot_general(_silu(c_ref[...]), d_ref[...], (((0,), (0,)), ((), ())),
                            preferred_element_type=F32, precision=lax.Precision.HIGHEST)
        g_ref[...] = g
        delta_ref[...], nm_ref[...], nv_ref[...] = _adam(w_ref[...], g, m_ref[...], v_ref[...])

    sd = jax.ShapeDtypeStruct(w.shape, F32)
    return pl.pallas_call(body, name="ada_bwd_adam", out_shape=[sd] * 4,
                          compiler_params=pltpu.CompilerParams(vmem_limit_bytes=V7X_VMEM_LIMIT),
                          )(c_all, dmod_cols, w, m, v)


def _adam_group(chip_sums, arrivals, ws, ms, vs, n_tiles, name):
    n = len(ws)

    def body(*refs):
        for k in range(n):
            c_ref, a_ref, w_ref, m_ref, v_ref = (refs[j * n + k] for j in range(5))
            g_ref, delta_ref, nm_ref, nv_ref = (refs[(5 + j) * n + k] for j in range(4))
            g = c_ref[...].astype(F32)
            for j in range(N_NEAR - 1):
                g = g + a_ref[j].astype(F32)
            g_ref[...] = g
            delta_ref[...], nm_ref[...], nv_ref[...] = _adam(w_ref[...], g, m_ref[...], v_ref[...])

    tiles = [(w.shape[0] // n_tiles, w.shape[1]) for w in ws]
    blk = [pl.BlockSpec(t, lambda i: (i, 0)) for t in tiles]
    return pl.pallas_call(
        body, name=name, grid=(n_tiles,),
        in_specs=[pl.BlockSpec((None, *t), lambda i: (0, i, 0)) for t in tiles]
        + [pl.BlockSpec((N_NEAR - 1, *t), lambda i: (0, i, 0)) for t in tiles] + blk * 3,
        out_specs=blk * 4, out_shape=[jax.ShapeDtypeStruct(w.shape, F32) for w in ws] * 4,
        compiler_params=_params(("parallel",)),
    )(*chip_sums, *arrivals, *ws, *ms, *vs)


N_SMALL = 40
N_SMALL_PARAMS = 11


def _pack_vecs(conv_w_full, rows):
    def body(cw_ref, *refs):
        out = refs[-1]
        out[...] = jnp.zeros_like(out)
        out[0:4, :] = cw_ref[0:4, :]
        for r, ref in enumerate(refs[:-1]):
            out[4 + r:5 + r, :] = ref[...]

    return pl.pallas_call(body, name="pack_vecs", out_shape=jax.ShapeDtypeStruct((16, D), F32))(conv_w_full, *rows)


def _small_finish(gathered, mod_all, vecs, ws, ms, vs):
    n = N_SMALL_PARAMS

    def body(g_ref, mod_ref, vec_ref, *refs):
        w_refs, m_refs, v_refs = refs[:n], refs[n:2 * n], refs[2 * n:3 * n]
        outs = refs[3 * n:]
        g1 = vec_ref[V_G1:V_G1 + 1, :]
        g2 = vec_ref[V_G2:V_G2 + 1, :]
        zero = jnp.zeros((1, D), F32)
        dg1, dg2, dgf, loss_lanes = zero, zero, zero, zero
        mixer = jnp.zeros((16, D), F32)
        db_ada = jnp.zeros((6, D), F32)
        for b in range(N_DEV):
            gb = g_ref[b]
            mod = mod_ref[b]
            q1 = gb[33:34]
            q2 = gb[9:10]
            dmod = jnp.concatenate([gb[32:33], q1 * g1, gb[10:11], gb[8:9], q2 * g2, gb[1:2]], axis=0)
            outs[4 * n][b] = dmod
            db_ada = db_ada + dmod
            dg1 = dg1 + q1 * (1.0 + mod[M_SC1:M_SC1 + 1])
            dg2 = dg2 + q2 * (1.0 + mod[M_SC2:M_SC2 + 1])
            dgf = dgf + gb[0:1]
            loss_lanes = loss_lanes + gb[2:3]
            mixer = mixer + gb[16:32]
        d_a_param = mixer[7:8] * _sigmoid_tail(vec_ref[V_A_PARAM:V_A_PARAM + 1, :])
        grads = [dg1, dg2, mixer[4:5], mixer[5:6], mixer[6:7], d_a_param, mixer[8:9], mixer[9:10], dgf,
                 db_ada, mixer[0:4]]
        for k in range(n):
            outs[k][...] = grads[k]
            outs[n + k][...], outs[2 * n + k][...], outs[3 * n + k][...] = _adam(
                w_refs[k][...], grads[k], m_refs[k][...], v_refs[k][...])
        outs[4 * n + 1][...] = jnp.broadcast_to(jnp.sum(loss_lanes, axis=1, keepdims=True), (8, 128))

    shapes = [jax.ShapeDtypeStruct(w.shape, F32) for w in ws]
    return pl.pallas_call(
        body, name="small_finish",
        out_shape=shapes * 4 + [jax.ShapeDtypeStruct((N_DEV, 6, D), F32), jax.ShapeDtypeStruct((8, 128), F32)],
    )(gathered, mod_all, vecs, *ws, *ms, *vs)


def _pad_rows(a, rows):
    return jnp.pad(a, ((0, rows - a.shape[0]), (0, 0)))


def kernel(x, c, norm_mix_g, norm_mlp_g, w_ada, b_ada, w_in, conv_w, conv_b, w_rg_a, b_rg_a, w_rg_x, b_rg_x, a_param, w_branch_a, w_pool, b_pool, pool_scale, w_branch_b, w_out, w_up, w_down, final_g, loss_target, m_norm_mix_g, m_norm_mlp_g, m_w_ada, m_b_ada, m_w_in, m_conv_w, m_conv_b, m_w_rg_a, m_b_rg_a, m_w_rg_x, m_b_rg_x, m_a_param, m_w_branch_a, m_w_pool, m_b_pool, m_pool_scale, m_w_branch_b, m_w_out, m_w_up, m_w_down, m_final_g, v_norm_mix_g, v_norm_mlp_g, v_w_ada, v_b_ada, v_w_in, v_conv_w, v_conv_b, v_w_rg_a, v_b_rg_a, v_w_rg_x, v_b_rg_x, v_a_param, v_w_branch_a, v_w_pool, v_b_pool, v_pool_scale, v_w_branch_b, v_w_out, v_w_up, v_w_down, v_final_g):
    me = 4 * lax.axis_index("x") + 2 * lax.axis_index("y") + lax.axis_index("c")
    s = x.shape[1]
    x2d = x.reshape(s, D)
    target = loss_target.reshape(s, D)
    n_ada = w_ada.shape[2]

    b_ada_cols = lax.dynamic_slice(b_ada, (0, me * n_ada), (1, n_ada))
    spread_c = _Spread([_pad_rows(c, 8), _pad_rows(conv_w[0], 8)], [0, 1], c, "spread_c")

    sharded = dict(w_in=(w_in[0], 1), w_up=(w_up[0], 1), w_down=(w_down[0], 0), w_branch_a=(w_branch_a[0], 0),
                   w_branch_b=(w_branch_b[0], 0), w_out=(w_out[0], 0), w_rg_a=(w_rg_a[0], 1), w_rg_x=(w_rg_x[0], 1),
                   w_pool=(w_pool[0], 1))
    kind = {k: v[1] for k, v in sharded.items()}
    shard = dict(zip(sharded, _to_bf16([v[0] for v in sharded.values()])))

    first_names = ["w_in", "w_rg_a", "w_rg_x", "w_pool"]
    branch_names = ["w_branch_a", "w_branch_b", "w_out"]
    mlp_names = ["w_up", "w_down"]

    def gather(group, after, name):
        return _Gather([shard[k] for k in group], [kind[k] for k in group], after, name)

    g_first = gather(first_names, spread_c.token, "gather_first")

    c_rows, conv_w_full = spread_c.finish(g_first.token)
    c_all = c_rows.reshape(N_DEV, 8, D)[:, 0, :]
    mod_part = _ada_fwd(c_all, w_ada[0], b_ada_cols, g_first.token)
    spread_mod = _Spread([mod_part], [0], g_first.token, "spread_mod")
    g_branch = gather(branch_names, spread_mod.token, "gather_branch")
    g_mlp = gather(mlp_names, g_branch.token, "gather_mlp")

    vecs = _pack_vecs(conv_w_full, [conv_b, b_rg_a, b_rg_x, a_param, b_pool, pool_scale,
                                    norm_mix_g, norm_mlp_g, final_g.reshape(1, D)])
    g_first.forward(g_mlp.token)
    wg = dict(zip(first_names, g_first.finish(g_first.token)))
    mod_parts, = spread_mod.finish(g_first.token)
    mod_all = jnp.transpose(mod_parts.reshape(N_DEV, N_DEV, n_ada), (1, 0, 2)).reshape(N_DEV, 6, D)
    mod_all = jnp.pad(mod_all, ((0, 0), (0, 2), (0, 0)))
    modr = lax.dynamic_index_in_dim(mod_all, me, 0, keepdims=False)

    h1, x_rnn, u_pool, ga, dga, sa, sb = _proj_fwd(x2d, modr, vecs, wg["w_in"])
    g_branch.forward(h1)
    xr, hr, za, p, pooled, *gates = _mix_fwd(x_rnn, u_pool, ga, vecs, wg["w_rg_a"], wg["w_rg_x"], wg["w_pool"],
                                             dep=g_branch.token)
    g_mlp.forward(za)
    wg.update(zip(branch_names, g_branch.finish(g_mlp.token)))
    ba, bb, merged, o, x2, h2 = _branch_fwd(za, pooled, sa, sb, x2d, modr, vecs,
                                            wg["w_branch_a"], wg["w_branch_b"], wg["w_out"])
    wg.update(zip(mlp_names, g_mlp.finish(h2)))
    ru, dx3, d_dn, small_f = _mlp_fwd(h2, x2, target, modr, vecs, wg["w_up"], wg["w_down"])

    near = _near(lax.axis_index("x"), lax.axis_index("y"), lax.axis_index("c"))
    own4 = jnp.stack([me, near[1][1], near[2][1], near[3][1]]).astype(jnp.int32)

    def scatter(group, partial, after, name):
        return _Scatter([partial[k] for k in group], [kind[k] for k in group], after, name)

    dup, dx2, do, small_m = _mlp_bwd(d_dn, ru, x2, dx3, o, modr, vecs, wg["w_up"], wg["w_down"])
    partial = dict(w_up=_wgrad(h2, dup, "wgrad_up"), w_down=_wgrad(ru, d_dn, "wgrad_down", square_a=True))
    s_mlp = scatter(mlp_names, partial, dx2, "scatter_mlp")

    dba, dbb, dgates, dza, dpooled = _branch_bwd(do, sa, sb, ba, bb, wg["w_branch_a"], wg["w_branch_b"], wg["w_out"],
                                                 dep=s_mlp.token)
    s_mlp.combine_and_send(own4, dza)
    dproj, dw_rg_a, dw_rg_x, dw_pool, small_x = _mix_bwd(dza, dpooled, x_rnn, ga, dga, xr, hr, p, gates, dgates,
                                                         vecs, wg["w_rg_a"], wg["w_rg_x"], wg["w_pool"],
                                                         dep=s_mlp.token)
    partial.update(w_branch_a=_wgrad(za, dba, "wgrad_branch_a"), w_branch_b=_wgrad(pooled, dbb, "wgrad_branch_b"),
                   w_out=_wgrad(merged, do, "wgrad_out"),
                   w_rg_a=dw_rg_a, w_rg_x=dw_rg_x, w_pool=dw_pool)
    mixer_names = ["w_rg_a", "w_rg_x", "w_pool", "w_branch_a", "w_branch_b", "w_out"]
    s_mixer = scatter(mixer_names, partial, s_mlp.token, "scatter_mixer")

    partial["w_in"] = _wgrad(h1, dproj, "wgrad_in", dep=s_mixer.token)
    s_in = scatter(["w_in"], partial, s_mixer.token, "scatter_in")
    s_mixer.combine_and_send(own4, s_in.token)
    s_in.combine_and_send(own4, s_mixer.token)
    grad_x, small_p = _proj_bwd(dproj, x2d, dx2, modr, vecs, wg["w_in"], dep=s_in.token)

    locals_ = dict(w_in=(w_in, m_w_in, v_w_in), w_up=(w_up, m_w_up, v_w_up), w_down=(w_down, m_w_down, v_w_down),
                   w_branch_a=(w_branch_a, m_w_branch_a, v_w_branch_a),
                   w_branch_b=(w_branch_b, m_w_branch_b, v_w_branch_b), w_out=(w_out, m_w_out, v_w_out),
                   w_rg_a=(w_rg_a, m_w_rg_a, v_w_rg_a), w_rg_x=(w_rg_x, m_w_rg_x, v_w_rg_x),
                   w_pool=(w_pool, m_w_pool, v_w_pool))
    res = {}

    def finish(group, exchange, after, n_tiles, name):
        chip_sums, arrivals = exchange.finish(after)
        flat = lambda t: t.reshape(-1, t.shape[-1])
        shapes = [flat(locals_[k][0]).shape for k in group]
        outs = _adam_group([cs.reshape(N_NEAR, *sh) for cs, sh in zip(chip_sums, shapes)],
                           [ar.reshape(N_NEAR - 1, *sh) for ar, sh in zip(arrivals, shapes)],
                           *[[flat(locals_[k][j]) for k in group] for j in range(3)], n_tiles, name)
        for i, k in enumerate(group):
            res[k] = [outs[j * len(group) + i].reshape(locals_[k][0].shape) for j in range(4)]
        return res[group[-1]][0]

    small = jnp.concatenate([small_f, small_m, small_x, small_p], axis=0)
    g_small = _Gather([small], [0], grad_x, "gather_small")
    done = finish(mlp_names, s_mlp, g_small.token, 4, "adam_mlp")
    done = finish(mixer_names, s_mixer, done, 2, "adam_mixer")
    g_small.forward(done)
    done = finish(["w_in"], s_in, g_small.token, 4, "adam_in")
    small_all, = g_small.finish(done)
    small_all = small_all.reshape(N_DEV, N_SMALL, D)

    def embed(cw):
        return lax.dynamic_update_slice(jnp.zeros((4, D), F32), cw[0], (0, me * (D // N_DEV)))

    def smalls(ng, nl, cb, bra, brx, ap, bp, ps, fg, ba_, cw):
        return [ng, nl, cb, bra, brx, ap, bp, ps, fg.reshape(1, D), ba_.reshape(6, D), embed(cw)]

    small_names = ["norm_mix_g", "norm_mlp_g", "conv_b", "b_rg_a", "b_rg_x", "a_param", "b_pool", "pool_scale",
                   "final_g", "b_ada", "conv_w"]
    fin = _small_finish(
        small_all, mod_all, vecs,
        smalls(norm_mix_g, norm_mlp_g, conv_b, b_rg_a, b_rg_x, a_param, b_pool, pool_scale, final_g, b_ada, conv_w),
        smalls(m_norm_mix_g, m_norm_mlp_g, m_conv_b, m_b_rg_a, m_b_rg_x, m_a_param, m_b_pool, m_pool_scale,
               m_final_g, m_b_ada, m_conv_w),
        smalls(v_norm_mix_g, v_norm_mlp_g, v_conv_b, v_b_rg_a, v_b_rg_x, v_a_param, v_b_pool, v_pool_scale,
               v_final_g, v_b_ada, v_conv_w))
    dmod_all, loss_tile = fin[4 * N_SMALL_PARAMS], fin[4 * N_SMALL_PARAMS + 1]
    dmod_cols = lax.dynamic_slice(dmod_all.reshape(N_DEV, 6 * D), (0, me * n_ada), (N_DEV, n_ada))
    res["w_ada"] = [t.reshape(w_ada.shape) for t in _ada_bwd_adam(c_all, dmod_cols, w_ada[0], m_w_ada[0], v_w_ada[0])]

    def final_shape(k, t):
        if k == "final_g":
            return t.reshape(D)
        if k == "b_ada":
            return t.reshape(1, 6 * D)
        if k == "conv_w":
            return lax.dynamic_slice(t, (0, me * (D // N_DEV)), (4, D // N_DEV)).reshape(conv_w.shape)
        return t

    for i, k in enumerate(small_names):
        res[k] = [final_shape(k, fin[which * N_SMALL_PARAMS + i]) for which in range(4)]
    order = ["norm_mix_g", "norm_mlp_g", "w_ada", "b_ada", "w_in", "conv_w", "conv_b", "w_rg_a", "b_rg_a", "w_rg_x",
             "b_rg_x", "a_param", "w_branch_a", "w_pool", "b_pool", "pool_scale", "w_branch_b", "w_out", "w_up",
             "w_down", "final_g"]
    outs = [loss_tile[0, 0], grad_x.reshape(x.shape)]
    for which in range(4):
        for k in order:
            outs.append(res[k][which])
    return tuple(outs)
```

```python
import functools

import jax
import jax.numpy as jnp
from jax import lax
from jax.experimental import pallas as pl
from jax.experimental.pallas import tpu as pltpu

F32 = jnp.float32
BF16 = jnp.bfloat16
MESH = pl.DeviceIdType.MESH

N_DEV = 8
D = 1024
N_GROUPS = 4
GW = D // N_GROUPS
D_IN = 5 * D
D_FF = 4 * D
POOL_WINDOWS = (2, 4, 8, 16)
HALO_X = 8
HALO_U = 16
EPS = 1e-6
C_RG = 8.0
ADAM_LR, ADAM_B1, ADAM_B2, ADAM_EPS, ADAM_WD, ADAM_STEP = 0.001, 0.9, 0.999, 1e-08, 0.01, 10

V7X_VMEM_LIMIT = 56 * 1024 * 1024

V_CONV_W, V_CONV_B, V_B_RG_A, V_B_RG_X, V_A_PARAM, V_B_POOL, V_POOL_SCALE, V_G1, V_G2, V_GF = 0, 4, 5, 6, 7, 8, 9, 10, 11, 12
M_SH1, M_SC1, M_GT1, M_SH2, M_SC2, M_GT2 = 0, 1, 2, 3, 4, 5

TM_PROJ = 512
TM_MIX = 256
TM_BRANCH = 512
TM_MLP = 512
TM_MLP_BWD = 256
TS_WGRAD = 1024


def _params(semantics):
    return pltpu.CompilerParams(dimension_semantics=semantics, vmem_limit_bytes=V7X_VMEM_LIMIT)


def _resident(shape):
    return pl.BlockSpec(shape, lambda *_: (0,) * len(shape), pipeline_mode=pl.Buffered(1))


def _dot(a, b):
    return jnp.dot(a, b, preferred_element_type=F32)


def _dot_nt(a, b):
    return lax.dot_general(a, b, (((1,), (1,)), ((), ())), preferred_element_type=F32)


def _dot_tn(a, b):
    return lax.dot_general(a, b, (((0,), (0,)), ((), ())), preferred_element_type=F32)


def _sigmoid(x):
    return 0.5 * jnp.tanh(0.5 * x) + 0.5


def _sigmoid_tail(x):
    return 1.0 / (1.0 + jnp.exp(-x))


def _gelu_and_grad(x):
    k = 0.7978845608028654
    x2 = x * x
    t = jnp.tanh(k * (x + 0.044715 * x * x2))
    g = 0.5 * x * (1.0 + t)
    dg = 0.5 * (1.0 + t) + 0.5 * x * (1.0 - t * t) * (k * (1.0 + 3.0 * 0.044715 * x2))
    return g, dg


def _softplus(a):
    e = jnp.exp(-jnp.abs(a))
    u = 1.0 + e
    log1p_e = jnp.where(u == 1.0, e, jnp.log(u) * e / jnp.where(u == 1.0, 1.0, u - 1.0))
    return jnp.maximum(a, 0.0) + log1p_e


def _neg_expm1(z):
    series = -(z * (1.0 + z * (0.5 + z * (1.0 / 6.0 + z * (1.0 / 24.0 + z * (1.0 / 120.0))))))
    return jnp.where(z > -0.1, series, 1.0 - jnp.exp(z))


def _shift_down(x, k):
    return pltpu.roll(x, k, 0)


def _shift_up(x, k):
    return pltpu.roll(x, x.shape[0] - k, 0)


def _rglru_gates(xr, w_a, w_x, b_a, b_x, a_param, is_t0):
    xb = xr.astype(BF16)
    ra = _sigmoid(_dot(xb, w_a) + b_a)
    ri = _sigmoid(_dot(xb, w_x) + b_x)
    sp = _softplus(a_param)
    log_a = (-C_RG) * ra * sp
    a = jnp.exp(log_a)
    mult = jnp.where(is_t0, 1.0, jnp.sqrt(_neg_expm1(2.0 * log_a)))
    return ra, ri, sp, a, mult


SUBLANES = 8


LANES = 128


def _scan_strip(a, b, carry, scr, down):
    t = b.shape[0]
    g = t // SUBLANES
    a3 = a.reshape(g, SUBLANES, LANES)
    b3 = b.reshape(g, SUBLANES, LANES)
    sub = lax.broadcasted_iota(jnp.int32, (g, SUBLANES, LANES), 1)
    for k in (1, 2, 4):
        keep = sub >= k if down else sub < SUBLANES - k
        shift = k if down else SUBLANES - k
        b3 = b3 + a3 * jnp.where(keep, pltpu.roll(b3, shift, 1), 0.0)
        a3 = a3 * jnp.where(keep, pltpu.roll(a3, shift, 1), 1.0)
    scr[0] = a3.reshape(t, LANES)
    scr[1] = b3.reshape(t, LANES)
    end_row = SUBLANES - 1 if down else 0
    ag = scr[0, pl.ds(end_row, g, stride=SUBLANES), :]
    bg = scr[1, pl.ds(end_row, g, stride=SUBLANES), :]
    rg = lax.broadcasted_iota(jnp.int32, (g, LANES), 0)
    edge = 0 if down else g - 1
    bg = bg + jnp.where(rg == edge, ag * carry, 0.0)
    k = 1
    while k < g:
        keep = rg >= k if down else rg < g - k
        shift = k if down else g - k
        bg = bg + ag * jnp.where(keep, pltpu.roll(bg, shift, 0), 0.0)
        if 2 * k < g:
            ag = ag * pltpu.roll(ag, shift, 0)
        k *= 2
    entering = jnp.where(rg != edge, pltpu.roll(bg, 1 if down else g - 1, 0), carry)
    for r in range(SUBLANES):
        scr[2, pl.ds(r, g, stride=SUBLANES), :] = entering
    return scr[1] + scr[0] * scr[2], bg[g - 1:g, :]


def _scan_strips(a, b, carry, scr, down):
    outs = [_scan_strip(a[:, c:c + LANES], b[:, c:c + LANES], carry[:, c:c + LANES], scr, down)
            for c in range(0, b.shape[1], LANES)]
    return jnp.concatenate([o[0] for o in outs], axis=1), jnp.concatenate([o[1] for o in outs], axis=1)


def _scan_down(a, b, carry, scr):
    return _scan_strips(a, b, carry, scr, True)


def _scan_up(m, b, carry, scr):
    return _scan_strips(m, b, carry, scr, False)[0]


def _window_mean(sums, window, first_block, head_t):
    scaled = sums * (1.0 / window)
    head = jnp.where(first_block, sums[:HALO_U] / jnp.minimum(head_t, float(window)), scaled[:HALO_U])
    return jnp.concatenate([head, scaled[HALO_U:]], axis=0)


def _conv_taps(x_ext):
    return [_shift_down(x_ext, 3 - j)[HALO_X:] if j < 3 else x_ext[HALO_X:] for j in range(4)]


def _proj_fwd(x, modr, vecs, w_in):
    s = x.shape[0]
    tm = min(TM_PROJ, s)

    def body(x_ref, mod_ref, vec_ref, w_ref, h1_ref, xrnn_ref, u_ref, ga_ref, dga_ref, sa_ref, sb_ref):
        xv = x_ref[...]
        r = lax.rsqrt(jnp.mean(xv * xv, axis=-1, keepdims=True) + EPS)
        gain = vec_ref[V_G1:V_G1 + 1, :] * (1.0 + mod_ref[M_SC1:M_SC1 + 1, :])
        h = (xv * r * gain + mod_ref[M_SH1:M_SH1 + 1, :]).astype(BF16)
        h1_ref[...] = h
        xrnn_ref[...] = _dot(h, w_ref[:, 0:D])
        ga_ref[...], dga_ref[...] = _gelu_and_grad(_dot(h, w_ref[:, D:2 * D]))
        u_ref[...] = _dot(h, w_ref[:, 2 * D:3 * D])
        sa_ref[...] = _sigmoid(_dot(h, w_ref[:, 3 * D:4 * D]))
        sb_ref[...] = _sigmoid(_dot(h, w_ref[:, 4 * D:5 * D]))

    tok = pl.BlockSpec((tm, D), lambda i: (i, 0))
    sd = lambda dt: jax.ShapeDtypeStruct((s, D), dt)
    return pl.pallas_call(
        body, name="proj_fwd", grid=(s // tm,),
        in_specs=[tok, pl.BlockSpec((8, D), lambda i: (0, 0)), pl.BlockSpec((16, D), lambda i: (0, 0)),
                  _resident((D, D_IN))],
        out_specs=[tok] * 7,
        out_shape=[sd(BF16)] + [sd(F32)] * 6,
        compiler_params=_params(("parallel",)),
    )(x, modr, vecs, w_in)


def _mix_fwd(x_rnn, u_pool, ga, vecs, w_rg_a, w_rg_x, w_pool, dep):
    s = x_rnn.shape[0]
    tm = min(TM_MIX, s)
    nb = s // tm

    def body(xh_ref, x_ref, uh_ref, u_ref, ga_ref, vec_ref, wa_ref, wx_ref, wp_ref, dep_ref,
             xr_ref, hr_ref, za_ref, p_ref, pooled_ref, a_ref, mult_ref, ra_ref, ri_ref, carry_ref, scan_scr):
        i = pl.program_id(0)
        first = i == 0

        @pl.when(first)
        def _():
            carry_ref[...] = jnp.zeros_like(carry_ref)

        row = lax.broadcasted_iota(jnp.int32, (tm, GW), 0)
        is_t0 = jnp.logical_and(first, row == 0)
        head_t = (lax.broadcasted_iota(jnp.int32, (HALO_U, GW), 0) + 1).astype(F32)
        for g in range(N_GROUPS):
            cs = slice(g * GW, (g + 1) * GW)
            vec = vec_ref[:, cs]
            xh = jnp.where(first, 0.0, xh_ref[:, cs])
            taps = _conv_taps(jnp.concatenate([xh, x_ref[:, cs]], axis=0))
            xr = vec[V_CONV_B:V_CONV_B + 1]
            for j in range(4):
                xr = xr + vec[V_CONV_W + j:V_CONV_W + j + 1] * taps[j]
            xr_ref[:, cs] = xr
            ra, ri, _, a, mult = _rglru_gates(
                xr, wa_ref[g], wx_ref[g], vec[V_B_RG_A:V_B_RG_A + 1], vec[V_B_RG_X:V_B_RG_X + 1],
                vec[V_A_PARAM:V_A_PARAM + 1], is_t0)
            a_ref[:, cs] = a
            mult_ref[:, cs] = mult
            ra_ref[:, cs] = ra.astype(BF16)
            ri_ref[:, cs] = ri.astype(BF16)
            h, last = _scan_down(a, xr * ri * mult, carry_ref[0:1, cs], scan_scr)
            hr_ref[:, cs] = h
            carry_ref[0:1, cs] = last
            za_ref[:, cs] = (ga_ref[:, cs] * h).astype(BF16)
            uh = jnp.where(first, 0.0, uh_ref[:, cs])
            sm = jnp.concatenate([uh, u_ref[:, cs]], axis=0)
            k = 1
            while k < POOL_WINDOWS[g]:
                sm = sm + _shift_down(sm, k)
                k *= 2
            mean = _window_mean(sm[HALO_U:], POOL_WINDOWS[g], first, head_t)
            p = (mean - u_ref[:, cs]).astype(BF16)
            p_ref[:, cs] = p
            pb = _dot(p, wp_ref[g]) + vec[V_B_POOL:V_B_POOL + 1]
            pooled_ref[:, cs] = (pb * vec[V_POOL_SCALE:V_POOL_SCALE + 1]).astype(BF16)

    tok = pl.BlockSpec((tm, D), lambda i: (i, 0))
    halo = lambda rows: pl.BlockSpec((rows, D), lambda i: (jnp.maximum(i * (tm // rows) - 1, 0), 0))
    wspec = pl.BlockSpec((N_GROUPS, GW, GW), lambda i: (0, 0, 0))
    sd = lambda dt: jax.ShapeDtypeStruct((s, D), dt)
    return pl.pallas_call(
        body, name="mix_fwd", grid=(nb,),
        in_specs=[halo(HALO_X), tok, halo(HALO_U), tok, tok, pl.BlockSpec((16, D), lambda i: (0, 0)),
                  wspec, wspec, wspec, pl.BlockSpec(memory_space=pl.ANY)],
        out_specs=[tok] * 9,
        out_shape=[sd(F32), sd(F32), sd(BF16), sd(BF16), sd(BF16), sd(F32), sd(F32), sd(BF16), sd(BF16)],
        scratch_shapes=[pltpu.VMEM((8, D), F32), pltpu.VMEM((3, tm, LANES), F32)],
        compiler_params=_params(("arbitrary",)),
    )(x_rnn, x_rnn, u_pool, u_pool, ga, vecs, w_rg_a, w_rg_x, w_pool, dep)


def _branch_fwd(za, pooled, sa, sb, x, modr, vecs, w_a, w_b, w_out):
    s = x.shape[0]
    tm = min(TM_BRANCH, s)

    def body(za_ref, pooled_ref, sa_ref, sb_ref, x_ref, mod_ref, vec_ref, wa_ref, wb_ref, wo_ref,
             ba_ref, bb_ref, merged_ref, o_ref, x2_ref, h2_ref):
        ba = _dot(za_ref[...], wa_ref[...])
        bb = _dot(pooled_ref[...], wb_ref[...])
        ba_ref[...] = ba.astype(BF16)
        bb_ref[...] = bb.astype(BF16)
        merged = (sa_ref[...] * ba + sb_ref[...] * bb).astype(BF16)
        merged_ref[...] = merged
        o = _dot(merged, wo_ref[...])
        o_ref[...] = o.astype(BF16)
        x2 = x_ref[...] + mod_ref[M_GT1:M_GT1 + 1, :] * o
        x2_ref[...] = x2
        r = lax.rsqrt(jnp.mean(x2 * x2, axis=-1, keepdims=True) + EPS)
        gain = vec_ref[V_G2:V_G2 + 1, :] * (1.0 + mod_ref[M_SC2:M_SC2 + 1, :])
        h2_ref[...] = (x2 * r * gain + mod_ref[M_SH2:M_SH2 + 1, :]).astype(BF16)

    tok = pl.BlockSpec((tm, D), lambda i: (i, 0))
    wspec = pl.BlockSpec((D, D), lambda i: (0, 0))
    sd = lambda dt: jax.ShapeDtypeStruct((s, D), dt)
    return pl.pallas_call(
        body, name="branch_fwd", grid=(s // tm,),
        in_specs=[tok, tok, tok, tok,
                  tok, pl.BlockSpec((8, D), lambda i: (0, 0)), pl.BlockSpec((16, D), lambda i: (0, 0)),
                  wspec, wspec, wspec],
        out_specs=[tok] * 6,
        out_shape=[sd(BF16), sd(BF16), sd(BF16), sd(BF16), sd(F32), sd(BF16)],
        compiler_params=_params(("parallel",)),
    )(za, pooled, sa, sb, x, modr, vecs, w_a, w_b, w_out)


def _mlp_fwd(h2, x2, target, modr, vecs, w_up, w_down):
    s = x2.shape[0]
    tm = min(TM_MLP, s)

    def body(h2_ref, x2_ref, tgt_ref, mod_ref, vec_ref, wu_ref, wd_ref,
             ru_ref, dx3_ref, ddn_ref, small_ref):
        @pl.when(pl.program_id(0) == 0)
        def _():
            small_ref[...] = jnp.zeros_like(small_ref)

        h2 = h2_ref[...]
        dn = None
        for c in range(D_FF // D):
            cs = slice(c * D, (c + 1) * D)
            ru = jnp.maximum(_dot(h2, wu_ref[:, cs]), 0.0)
            ru_ref[:, cs] = ru.astype(BF16)
            part = _dot((ru * ru).astype(BF16), wd_ref[cs, :])
            dn = part if dn is None else dn + part
        gt2 = mod_ref[M_GT2:M_GT2 + 1, :]
        gf = vec_ref[V_GF:V_GF + 1, :]
        x3 = x2_ref[...] + gt2 * dn
        r3 = lax.rsqrt(jnp.mean(x3 * x3, axis=-1, keepdims=True) + EPS)
        n3 = x3 * r3
        err = n3 * gf - tgt_ref[...]
        dy = err * (1.0 / D)
        dn3 = dy * gf
        dx3 = r3 * (dn3 - n3 * jnp.mean(dn3 * n3, axis=-1, keepdims=True))
        dx3_ref[...] = dx3
        ddn_ref[...] = (dx3 * gt2).astype(BF16)
        small_ref[0:1, :] += jnp.sum(dy * n3, axis=0, keepdims=True)
        small_ref[1:2, :] += jnp.sum(dx3 * dn, axis=0, keepdims=True)
        small_ref[2:3, :] += (0.5 / D) * jnp.sum(err * err, axis=0, keepdims=True)

    tok = pl.BlockSpec((tm, D), lambda i: (i, 0))
    return pl.pallas_call(
        body, name="mlp_fwd", grid=(s // tm,),
        in_specs=[tok, tok, tok,
                  pl.BlockSpec((8, D), lambda i: (0, 0)), pl.BlockSpec((16, D), lambda i: (0, 0)),
                  _resident((D, D_FF)), _resident((D_FF, D))],
        out_specs=[pl.BlockSpec((tm, D_FF), lambda i: (i, 0)), tok, tok,
                   pl.BlockSpec((8, D), lambda i: (0, 0))],
        out_shape=[jax.ShapeDtypeStruct((s, D_FF), BF16), jax.ShapeDtypeStruct((s, D), F32),
                   jax.ShapeDtypeStruct((s, D), BF16), jax.ShapeDtypeStruct((8, D), F32)],
        compiler_params=_params(("arbitrary",)),
    )(h2, x2, target, modr, vecs, w_up, w_down)


def _mlp_bwd(d_dn, ru, x2, dx3, o, modr, vecs, w_up, w_down):
    s = x2.shape[0]
    tm = min(TM_MLP_BWD, s)

    def body(ddn_ref, ru_ref, x2_ref, dx3_ref, o_ref, mod_ref, vec_ref, wu_ref, wd_ref,
             dup_ref, dx2_ref, do_ref, small_ref):
        @pl.when(pl.program_id(0) == 0)
        def _():
            small_ref[...] = jnp.zeros_like(small_ref)

        ddn = ddn_ref[...]
        dh2 = None
        for c in range(D_FF // D):
            cs = slice(c * D, (c + 1) * D)
            dff = _dot_nt(ddn, wd_ref[cs, :])
            dup = (dff * (2.0 * ru_ref[:, cs].astype(F32))).astype(BF16)
            dup_ref[:, cs] = dup
            part = _dot_nt(dup, wu_ref[:, cs])
            dh2 = part if dh2 is None else dh2 + part
        x2 = x2_ref[...]
        r2 = lax.rsqrt(jnp.mean(x2 * x2, axis=-1, keepdims=True) + EPS)
        xn2 = x2 * r2
        gain = vec_ref[V_G2:V_G2 + 1, :] * (1.0 + mod_ref[M_SC2:M_SC2 + 1, :])
        dxn2 = dh2 * gain
        dx2 = dx3_ref[...] + r2 * (dxn2 - xn2 * jnp.mean(dxn2 * xn2, axis=-1, keepdims=True))
        dx2_ref[...] = dx2
        do_ref[...] = (dx2 * mod_ref[M_GT1:M_GT1 + 1, :]).astype(BF16)
        small_ref[0:1, :] += jnp.sum(dh2, axis=0, keepdims=True)
        small_ref[1:2, :] += jnp.sum(dh2 * xn2, axis=0, keepdims=True)
        small_ref[2:3, :] += jnp.sum(dx2 * o_ref[...].astype(F32), axis=0, keepdims=True)

    tok = pl.BlockSpec((tm, D), lambda i: (i, 0))
    wide = pl.BlockSpec((tm, D_FF), lambda i: (i, 0))
    return pl.pallas_call(
        body, name="mlp_bwd", grid=(s // tm,),
        in_specs=[tok, wide, tok, tok, tok,
                  pl.BlockSpec((8, D), lambda i: (0, 0)), pl.BlockSpec((16, D), lambda i: (0, 0)),
                  _resident((D, D_FF)), _resident((D_FF, D))],
        out_specs=[wide, tok, tok, pl.BlockSpec((8, D), lambda i: (0, 0))],
        out_shape=[jax.ShapeDtypeStruct((s, D_FF), BF16), jax.ShapeDtypeStruct((s, D), F32),
                   jax.ShapeDtypeStruct((s, D), BF16), jax.ShapeDtypeStruct((8, D), F32)],
        compiler_params=_params(("arbitrary",)),
    )(d_dn, ru, x2, dx3, o, modr, vecs, w_up, w_down)


def _branch_bwd(do, sa, sb, ba, bb, w_a, w_b, w_out, dep):
    s = do.shape[0]
    tm = min(TM_BRANCH, s)

    def body(do_ref, sa_ref, sb_ref, ba_ref, bb_ref, wa_ref, wb_ref, wo_ref, dep_ref,
             dba_ref, dbb_ref, dg_ref, dza_ref, dpooled_ref):
        dmerged = _dot_nt(do_ref[...], wo_ref[...])
        sa = sa_ref[...]
        sb = sb_ref[...]
        dba = (dmerged * sa).astype(BF16)
        dbb = (dmerged * sb).astype(BF16)
        dba_ref[...] = dba
        dbb_ref[...] = dbb
        dg_ref[:, :D] = (dmerged * ba_ref[...].astype(F32) * sa * (1.0 - sa)).astype(BF16)
        dg_ref[:, D:] = (dmerged * bb_ref[...].astype(F32) * sb * (1.0 - sb)).astype(BF16)
        dza_ref[...] = _dot_nt(dba, wa_ref[...])
        dpooled_ref[...] = _dot_nt(dbb, wb_ref[...])

    tok = pl.BlockSpec((tm, D), lambda i: (i, 0))
    wspec = pl.BlockSpec((D, D), lambda i: (0, 0))
    sd = lambda dt: jax.ShapeDtypeStruct((s, D), dt)
    return pl.pallas_call(
        body, name="branch_bwd", grid=(s // tm,),
        in_specs=[tok, tok, tok, tok, tok, wspec, wspec, wspec, pl.BlockSpec(memory_space=pl.ANY)],
        out_specs=[tok, tok, pl.BlockSpec((tm, 2 * D), lambda i: (i, 0)), tok, tok],
        out_shape=[sd(BF16), sd(BF16), jax.ShapeDtypeStruct((s, 2 * D), BF16), sd(F32), sd(F32)],
        compiler_params=_params(("parallel",)),
    )(do, sa, sb, ba, bb, w_a, w_b, w_out, dep)


def _mix_bwd(dza, dpooled, x_rnn, ga, dga, xr, hr, p, gates, dgates, vecs, w_rg_a, w_rg_x, w_pool, dep):
    s = xr.shape[0]
    tm = min(TM_MIX, s)
    nb = s // tm

    def body(dza_ref, dpooled_ref, xh_ref, x_ref, ga_ref, dga_ref, xr_ref, hh_ref, hr_ref, p_ref,
             a_ref, mult_ref, ra_ref, ri_ref, dg_ref, vec_ref, wa_ref, wx_ref, wp_ref, dep_ref,
             dproj_ref, dwa_ref, dwx_ref, dwp_ref, small_ref,
             scan_carry, dxr_carry, q_carry, scan_scr, dwa_acc, dwx_acc, dwp_acc):
        i = pl.program_id(0)
        bi = nb - 1 - i
        first_t = bi == 0

        @pl.when(i == 0)
        def _():
            scan_carry[...] = jnp.zeros_like(scan_carry)
            dxr_carry[...] = jnp.zeros_like(dxr_carry)
            q_carry[...] = jnp.zeros_like(q_carry)
            dwa_acc[...] = jnp.zeros_like(dwa_acc)
            dwx_acc[...] = jnp.zeros_like(dwx_acc)
            dwp_acc[...] = jnp.zeros_like(dwp_acc)
            small_ref[...] = jnp.zeros_like(small_ref)

        row = lax.broadcasted_iota(jnp.int32, (tm, GW), 0)
        is_t0 = jnp.logical_and(first_t, row == 0)
        head_t = (lax.broadcasted_iota(jnp.int32, (HALO_U, GW), 0) + 1).astype(F32)
        colsum = lambda v: jnp.sum(v, axis=0, keepdims=True)
        for g in range(N_GROUPS):
            cs = slice(g * GW, (g + 1) * GW)
            vec = vec_ref[:, cs]
            xr = xr_ref[:, cs]
            hr = hr_ref[:, cs]
            dza = dza_ref[:, cs]
            dproj_ref[:, D + g * GW:D + (g + 1) * GW] = (dza * hr * dga_ref[:, cs]).astype(BF16)
            dhr = dza * ga_ref[:, cs]
            a = a_ref[:, cs]
            mult = mult_ref[:, cs]
            ra = ra_ref[:, cs].astype(F32)
            ri = ri_ref[:, cs].astype(F32)
            sp = _softplus(vec[V_A_PARAM:V_A_PARAM + 1])
            m = jnp.where(row == tm - 1, 1.0, _shift_up(a, 1))
            gsum = _scan_up(m, dhr, scan_carry[0:1, cs], scan_scr)
            scan_carry[0:1, cs] = a[0:1, :] * gsum[0:1, :]
            hh = jnp.where(first_t, 0.0, hh_ref[:, cs])
            hprev = _shift_down(jnp.concatenate([hh, hr], axis=0), 1)[8:]
            da = gsum * hprev
            dmult = jnp.where(is_t0, 0.0, gsum * xr * ri)
            dlog_a = da * a - dmult * a * a / mult
            dri = gsum * xr * mult
            dxr = gsum * ri * mult
            small_ref[7:8, cs] += colsum((-C_RG) * ra * dlog_a)
            dpa = (((-C_RG) * sp) * dlog_a * ra * (1.0 - ra))
            dpx = dri * ri * (1.0 - ri)
            small_ref[5:6, cs] += colsum(dpa)
            small_ref[6:7, cs] += colsum(dpx)
            dpa = dpa.astype(BF16)
            dpx = dpx.astype(BF16)
            xrb = xr.astype(BF16)
            dwa_acc[g] += _dot_tn(xrb, dpa)
            dwx_acc[g] += _dot_tn(xrb, dpx)
            dxr = dxr + _dot_nt(dpa, wa_ref[g]) + _dot_nt(dpx, wx_ref[g])
            small_ref[4:5, cs] += colsum(dxr)
            xh = jnp.where(first_t, 0.0, xh_ref[:, cs])
            taps = _conv_taps(jnp.concatenate([xh, x_ref[:, cs]], axis=0))
            dxr_ext = jnp.concatenate([dxr, dxr_carry[:, cs]], axis=0)
            dx = vec[V_CONV_W + 3:V_CONV_W + 4] * dxr
            for j in range(4):
                small_ref[j:j + 1, cs] += colsum(dxr * taps[j])
                if j < 3:
                    dx = dx + vec[V_CONV_W + j:V_CONV_W + j + 1] * _shift_up(dxr_ext, 3 - j)[:tm]
            dxr_carry[:, cs] = dxr[0:8, :]
            dproj_ref[:, cs] = dx.astype(BF16)
            pg = p_ref[:, cs]
            dpooled = dpooled_ref[:, cs]
            pb = _dot(pg, wp_ref[g]) + vec[V_B_POOL:V_B_POOL + 1]
            small_ref[9:10, cs] += colsum(dpooled * pb)
            dpb = dpooled * vec[V_POOL_SCALE:V_POOL_SCALE + 1]
            small_ref[8:9, cs] += colsum(dpb)
            dpbb = dpb.astype(BF16)
            dwp_acc[g] += _dot_tn(pg, dpbb)
            dp = _dot_nt(dpbb, wp_ref[g])
            q = _window_mean(dp, POOL_WINDOWS[g], first_t, head_t)
            sm = jnp.concatenate([q, q_carry[:, cs]], axis=0)
            k = 1
            while k < POOL_WINDOWS[g]:
                sm = sm + _shift_up(sm, k)
                k *= 2
            q_carry[:, cs] = q[0:HALO_U, :]
            dproj_ref[:, 2 * D + g * GW:2 * D + (g + 1) * GW] = (sm[:tm] - dp).astype(BF16)
        dproj_ref[:, 3 * D:] = dg_ref[...]

        @pl.when(i == nb - 1)
        def _():
            dwa_ref[...] = dwa_acc[...].astype(BF16)
            dwx_ref[...] = dwx_acc[...].astype(BF16)
            dwp_ref[...] = dwp_acc[...].astype(BF16)

    rev = lambda i: nb - 1 - i
    tok = pl.BlockSpec((tm, D), lambda i: (rev(i), 0))
    halo8 = lambda k: pl.BlockSpec((8, D), lambda i: (jnp.maximum(rev(i) * (tm // 8) - 1, 0), k))
    wspec = pl.BlockSpec((N_GROUPS, GW, GW), lambda i: (0, 0, 0))
    wshape = jax.ShapeDtypeStruct((N_GROUPS, GW, GW), BF16)
    return pl.pallas_call(
        body, name="mix_bwd", grid=(nb,),
        in_specs=[tok, tok, halo8(0), tok, tok, tok, tok, halo8(0), tok, tok, tok, tok, tok, tok,
                  pl.BlockSpec((tm, 2 * D), lambda i: (rev(i), 0)),
                  pl.BlockSpec((16, D), lambda i: (0, 0)), wspec, wspec, wspec, pl.BlockSpec(memory_space=pl.ANY)],
        out_specs=[pl.BlockSpec((tm, D_IN), lambda i: (rev(i), 0)), wspec, wspec, wspec,
                   pl.BlockSpec((16, D), lambda i: (0, 0))],
        out_shape=[jax.ShapeDtypeStruct((s, D_IN), BF16), wshape, wshape, wshape,
                   jax.ShapeDtypeStruct((16, D), F32)],
        scratch_shapes=[pltpu.VMEM((8, D), F32), pltpu.VMEM((8, D), F32), pltpu.VMEM((HALO_U, D), F32),
                        pltpu.VMEM((3, tm, LANES), F32)] + [pltpu.VMEM((N_GROUPS, GW, GW), F32)] * 3,
        compiler_params=_params(("arbitrary",)),
    )(dza, dpooled, x_rnn, x_rnn, ga, dga, xr, hr, hr, p, *gates, dgates, vecs, w_rg_a, w_rg_x, w_pool, dep)


def _proj_bwd(dproj, x, dx2, modr, vecs, w_in, dep):
    s = x.shape[0]
    tm = min(TM_PROJ, s)

    def body(dp_ref, x_ref, dx2_ref, mod_ref, vec_ref, w_ref, dep_ref, gx_ref, small_ref):
        @pl.when(pl.program_id(0) == 0)
        def _():
            small_ref[...] = jnp.zeros_like(small_ref)

        dh1 = None
        for c in range(D_IN // D):
            cs = slice(c * D, (c + 1) * D)
            part = _dot_nt(dp_ref[:, cs], w_ref[:, cs])
            dh1 = part if dh1 is None else dh1 + part
        xv = x_ref[...]
        r1 = lax.rsqrt(jnp.mean(xv * xv, axis=-1, keepdims=True) + EPS)
        xn1 = xv * r1
        gain = vec_ref[V_G1:V_G1 + 1, :] * (1.0 + mod_ref[M_SC1:M_SC1 + 1, :])
        dxn1 = dh1 * gain
        gx_ref[...] = dx2_ref[...] + r1 * (dxn1 - xn1 * jnp.mean(dxn1 * xn1, axis=-1, keepdims=True))
        small_ref[0:1, :] += jnp.sum(dh1, axis=0, keepdims=True)
        small_ref[1:2, :] += jnp.sum(dh1 * xn1, axis=0, keepdims=True)

    tok = pl.BlockSpec((tm, D), lambda i: (i, 0))
    return pl.pallas_call(
        body, name="proj_bwd", grid=(s // tm,),
        in_specs=[pl.BlockSpec((tm, D_IN), lambda i: (i, 0)), tok, tok,
                  pl.BlockSpec((8, D), lambda i: (0, 0)), pl.BlockSpec((16, D), lambda i: (0, 0)),
                  _resident((D, D_IN)), pl.BlockSpec(memory_space=pl.ANY)],
        out_specs=[tok, pl.BlockSpec((8, D), lambda i: (0, 0))],
        out_shape=[jax.ShapeDtypeStruct((s, D), F32), jax.ShapeDtypeStruct((8, D), F32)],
        compiler_params=_params(("arbitrary",)),
    )(dproj, x, dx2, modr, vecs, w_in, dep)


def _wgrad(a, b, name, square_a=False, dep=None):
    s, ka = a.shape
    n = b.shape[1]
    tka = ka if ka <= 1024 else ka // 2
    tn = n if n <= 1024 else n // 2
    ts = min(TS_WGRAD, s)
    ns = s // ts
    nc = 512
    deps = [] if dep is None else [dep]

    def body(a_ref, b_ref, *refs):
        out_ref, acc_ref = refs[-2:]
        t = pl.program_id(2)

        @pl.when(t == 0)
        def _():
            acc_ref[...] = jnp.zeros_like(acc_ref)

        av = a_ref[...]
        if square_a:
            af = av.astype(F32)
            av = (af * af).astype(BF16)
        for c in range(tn // nc):
            cs = slice(c * nc, (c + 1) * nc)
            acc_ref[:, cs] += _dot_tn(av, b_ref[:, cs])

        @pl.when(t == ns - 1)
        def _():
            out_ref[...] = acc_ref[...].astype(BF16)

    return pl.pallas_call(
        body, name=name, grid=(ka // tka, n // tn, ns),
        in_specs=[pl.BlockSpec((ts, tka), lambda i, j, t: (t, i)),
                  pl.BlockSpec((ts, tn), lambda i, j, t: (t, j))] + [pl.BlockSpec(memory_space=pl.ANY)] * len(deps),
        out_specs=pl.BlockSpec((tka, tn), lambda i, j, t: (i, j)),
        out_shape=jax.ShapeDtypeStruct((ka, n), BF16),
        scratch_shapes=[pltpu.VMEM((tka, tn), F32)],
        compiler_params=_params(("parallel", "parallel", "arbitrary")),
    )(a, b, *deps)


def _window(ref, kind, idx, size):
    start = pl.multiple_of(idx * size, size)
    if kind == 0:
        return ref.at[pl.ds(start, size)]
    if kind == 1:
        return ref.at[:, pl.ds(start, size)]
    return ref.at[:, :, pl.ds(start, size)]


def _mesh_place():
    x, y, c = lax.axis_index("x"), lax.axis_index("y"), lax.axis_index("c")
    return x, y, c, 4 * x + 2 * y + c


def _peer(x, y, c, q):
    px = 1 - x if q & 4 else x
    py = 1 - y if q & 2 else y
    pc = 1 - c if q & 1 else c
    return (px, py, pc), 4 * px + 2 * py + pc


_HBM = pl.BlockSpec(memory_space=pltpu.HBM)
_SEM = pl.BlockSpec(memory_space=pltpu.SEMAPHORE)
_EFFECT = pltpu.SideEffectType.DATAFLOW_SIDE_EFFECTING


N_NEAR = 4


def _near(x, y, c):
    out = [((x, y, 1 - c), 4 * x + 2 * y + 1 - c)]
    for j in (1, 2, 3):
        px = 1 - x if j & 2 else x
        py = 1 - y if j & 1 else y
        out.append(((px, py, c), 4 * px + 2 * py + c))
    return out


def _remote(src, dst, send_sems, recv_sems, slot, device):
    return pltpu.make_async_remote_copy(src_ref=src, dst_ref=dst, send_sem=send_sems.at[slot], recv_sem=recv_sems.at[slot],
                                        device_id=device, device_id_type=MESH)


def _split_call(name, arrays, sems_in, n_new_sems, after, emit):
    na, ns, nn = len(arrays), len(sems_in), len(n_new_sems)

    def body(*refs):
        emit(refs[:na], refs[na:na + ns], refs[na + ns + 1:na + ns + 1 + nn])
        refs[-1][...] = jnp.zeros_like(refs[-1])

    outs = pl.pallas_call(
        body, name=name,
        out_shape=(*[pltpu.SemaphoreType.DMA((m,)) for m in n_new_sems],
                   *[pltpu.HBM(a.shape, a.dtype) for a in arrays], jax.ShapeDtypeStruct((8, 128), F32)),
        in_specs=[_HBM] * na + [_SEM] * ns + [pl.BlockSpec(memory_space=pl.ANY)],
        out_specs=(*[_SEM] * nn, *[_HBM] * na, pl.BlockSpec(memory_space=pltpu.VMEM)),
        input_output_aliases={i: nn + i for i in range(na)},
        compiler_params=pltpu.CompilerParams(has_side_effects=_EFFECT),
    )(*[pltpu.with_memory_space_constraint(a, pltpu.HBM) for a in arrays], *sems_in, after)
    return list(outs[:nn]), list(outs[nn:nn + na]), outs[-1]


class _Gather:
    def __init__(self, shards, kinds, after, name):
        self.n, self.kinds, self.name = len(shards), kinds, name
        self.sizes = [s.shape[k] for s, k in zip(shards, kinds)]
        n = self.n
        lands = []
        for s, k in zip(shards, kinds):
            dims = list(s.shape)
            dims[k] *= N_DEV
            lands.append(lax.empty(tuple(dims), s.dtype))

        def emit(arr, _, new):
            x, y, c, me = _mesh_place()
            for k in range(n):
                pltpu.make_async_copy(arr[k], _window(arr[n + k], kinds[k], me, self.sizes[k]), new[2].at[k]).start()
            for k in range(n):
                mine = _window(arr[n + k], kinds[k], me, self.sizes[k])
                for j, (dev, _) in enumerate(_near(x, y, c)):
                    _remote(arr[k], mine, new[0], new[1], k * N_NEAR + j, dev).start()

        self.sems, self.arrays, self.token = _split_call(name + "_start", [*shards, *lands], [],
                                                         [n * N_NEAR, n * N_NEAR, n], after, emit)

    def forward(self, after):
        n, kinds, sizes = self.n, self.kinds, self.sizes

        def emit(arr, old, new):
            x, y, c, _ = _mesh_place()
            near = _near(x, y, c)
            for k in range(n):
                for j in (1, 2, 3):
                    dev, idx = near[j]
                    landed = _window(arr[n + k], kinds[k], idx, sizes[k])
                    _remote(arr[k], landed, old[0], old[1], k * N_NEAR + j, dev).wait_recv()
                    _remote(landed, landed, new[0], new[1], k * N_NEAR + j, near[0][0]).start()

        new, self.arrays, self.token = _split_call(self.name + "_forward", self.arrays, self.sems, [n * N_NEAR] * 2,
                                                   after, emit)
        self.sems = [*self.sems, *new]

    def finish(self, after):
        n, kinds, sizes = self.n, self.kinds, self.sizes

        def emit(arr, old, _):
            x, y, c, me = _mesh_place()
            near = _near(x, y, c)
            other_core = near[0][0]
            for k in range(n):
                win = lambda idx: _window(arr[n + k], kinds[k], idx, sizes[k])
                pltpu.make_async_copy(arr[k], win(me), old[2].at[k]).wait()
                for j, (dev, idx) in enumerate(near):
                    _remote(arr[k], win(me), old[0], old[1], k * N_NEAR + j, dev).wait_send()
                _remote(arr[k], win(near[0][1]), old[0], old[1], k * N_NEAR, other_core).wait_recv()
                for j in (1, 2, 3):
                    idx = near[j][1]
                    _remote(win(idx), win(idx), old[3], old[4], k * N_NEAR + j, other_core).wait_send()
                    _remote(arr[k], win(idx + 1 - 2 * c), old[3], old[4], k * N_NEAR + j, other_core).wait_recv()

        _, arrays, _ = _split_call(self.name + "_finish", self.arrays, self.sems, [], after, emit)
        return arrays[n:]


class _Spread:
    def __init__(self, shards, kinds, after, name):
        self.n, self.kinds, self.name = len(shards), kinds, name
        self.sizes = [s.shape[k] for s, k in zip(shards, kinds)]
        n = self.n
        lands = []
        for s, k in zip(shards, kinds):
            dims = list(s.shape)
            dims[k] *= N_DEV
            lands.append(lax.empty(tuple(dims), s.dtype))

        def emit(arr, _, new):
            x, y, c, me = _mesh_place()
            for k in range(n):
                mine = _window(arr[n + k], kinds[k], me, self.sizes[k])
                pltpu.make_async_copy(arr[k], mine, new[2].at[k]).start()
                for q in range(1, N_DEV):
                    _remote(arr[k], mine, new[0], new[1], k * N_DEV + q, _peer(x, y, c, q)[0]).start()

        self.sems, self.arrays, self.token = _split_call(name + "_start", [*shards, *lands], [],
                                                         [n * N_DEV, n * N_DEV, n], after, emit)

    def finish(self, after):
        n, kinds, sizes = self.n, self.kinds, self.sizes

        def emit(arr, old, _):
            x, y, c, me = _mesh_place()
            for k in range(n):
                win = lambda idx: _window(arr[n + k], kinds[k], idx, sizes[k])
                pltpu.make_async_copy(arr[k], win(me), old[2].at[k]).wait()
                for q in range(1, N_DEV):
                    peer, peer_idx = _peer(x, y, c, q)
                    _remote(arr[k], win(me), old[0], old[1], k * N_DEV + q, peer).wait_send()
                    _remote(arr[k], win(peer_idx), old[0], old[1], k * N_DEV + q, peer).wait_recv()

        _, arrays, _ = _split_call(self.name + "_finish", self.arrays, self.sems, [], after, emit)
        return arrays[n:]


class _Scatter:
    def __init__(self, partials, kinds, after, name):
        self.n, self.kinds, self.name, self.partials = len(partials), kinds, name, partials
        self.sizes = [p.shape[k] // N_DEV for p, k in zip(partials, kinds)]
        n, sizes = self.n, self.sizes
        self.slot_shapes = []
        for p, k, size in zip(partials, kinds, sizes):
            dims = list(p.shape)
            dims[k] = size
            self.slot_shapes.append((N_NEAR, *dims))
        slots = [lax.empty(sh, p.dtype) for sh, p in zip(self.slot_shapes, partials)]

        def emit(arr, _, new):
            x, y, c, _ = _mesh_place()
            near = _near(x, y, c)
            for k in range(n):
                for j in range(N_NEAR):
                    owner = near[j][1] if j == 0 else near[j][1] + 1 - 2 * c
                    _remote(_window(arr[k], kinds[k], owner, sizes[k]), arr[n + k].at[j], new[0], new[1],
                            k * N_NEAR + j, near[0][0]).start()

        self.sems, self.arrays, self.token = _split_call(name + "_start", [*partials, *slots], [], [n * N_NEAR] * 2,
                                                         after, emit)

    def combine_and_send(self, own4, after):
        n, kinds, sizes = self.n, self.kinds, self.sizes

        def emit_wait(arr, old, _):
            x, y, c, _ = _mesh_place()
            near = _near(x, y, c)
            for k in range(n):
                for j in range(N_NEAR):
                    owner = near[j][1] if j == 0 else near[j][1] + 1 - 2 * c
                    cp = _remote(_window(arr[k], kinds[k], owner, sizes[k]), arr[n + k].at[j], old[0], old[1],
                                 k * N_NEAR + j, near[0][0])
                    cp.wait_send()
                    cp.wait_recv()

        _, arrays, _ = _split_call(self.name + "_landed", self.arrays, self.sems, [], after, emit_wait)
        chip_sums = _chip_sums(arrays[:n], arrays[n:], kinds, sizes, own4, self.name + "_combine")
        arrivals = [lax.empty((N_NEAR - 1, *sh[1:]), p.dtype) for sh, p in zip(self.slot_shapes, self.partials)]

        def emit_send(arr, _, new):
            x, y, c, _ = _mesh_place()
            near = _near(x, y, c)
            for k in range(n):
                for j in (1, 2, 3):
                    _remote(arr[k].at[j], arr[n + k].at[j - 1], new[0], new[1], k * N_NEAR + j, near[j][0]).start()

        self.sems, self.arrays, self.token = _split_call(self.name + "_send", [*chip_sums, *arrivals], [],
                                                         [n * N_NEAR] * 2, own4, emit_send)

    def finish(self, after):
        n = self.n

        def emit(arr, old, _):
            x, y, c, _ = _mesh_place()
            near = _near(x, y, c)
            for k in range(n):
                for j in (1, 2, 3):
                    cp = _remote(arr[k].at[j], arr[n + k].at[j - 1], old[0], old[1], k * N_NEAR + j, near[j][0])
                    cp.wait_send()
                    cp.wait_recv()

        _, arrays, _ = _split_call(self.name + "_finish", self.arrays, self.sems, [], after, emit)
        return arrays[:n], arrays[n:]


def _chip_sums(partials, slots, kinds, sizes, own4, name):
    n = len(partials)

    def body(own_ref, *refs):
        for k in range(n):
            refs[2 * n + k][...] = (refs[k][...].astype(F32) + refs[n + k][...].astype(F32)).astype(BF16)

    in_specs, slot_specs = [], []
    for p, s, kind, size in zip(partials, slots, kinds, sizes):
        block = list(p.shape)
        block[kind] = size
        nd = len(block)
        in_specs.append(pl.BlockSpec(tuple(block), functools.partial(
            lambda j, own, kind, nd: tuple(own[j] if d == kind else 0 for d in range(nd)), kind=kind, nd=nd)))
        slot_specs.append(pl.BlockSpec((None, *block), functools.partial(
            lambda j, own, nd: (j,) + (0,) * nd, nd=nd)))
    return pl.pallas_call(
        body, name=name,
        grid_spec=pltpu.PrefetchScalarGridSpec(num_scalar_prefetch=1, grid=(N_NEAR,),
                                               in_specs=in_specs + slot_specs, out_specs=slot_specs),
        out_shape=[jax.ShapeDtypeStruct(s.shape, s.dtype) for s in slots],
        compiler_params=_params(("arbitrary",)),
    )(own4, *partials, *slots)


def _to_bf16(arrays, name, dep=None):
    n = len(arrays)
    deps = [] if dep is None else [dep]

    def body(*refs):
        for src, dst in zip(refs[:n], refs[n + len(deps):]):
            dst[...] = src[...].astype(BF16)

    vmem = pl.BlockSpec(memory_space=pltpu.VMEM)
    return pl.pallas_call(body, name=name, out_shape=[jax.ShapeDtypeStruct(a.shape, BF16) for a in arrays],
                          in_specs=[vmem] * n + [pl.BlockSpec(memory_space=pl.ANY)] * len(deps), out_specs=[vmem] * n,
                          compiler_params=pltpu.CompilerParams(vmem_limit_bytes=V7X_VMEM_LIMIT))(*arrays, *deps)


def _silu(c):
    return c * _sigmoid_tail(c)


def _ada_fwd(c_all, w_ada, b_ada_cols, dep):
    def body(c_ref, w_ref, b_ref, dep_ref, out_ref):
        out_ref[...] = jnp.dot(_silu(c_ref[...]), w_ref[...], preferred_element_type=F32,
                               precision=lax.Precision.HIGHEST) + b_ref[...]

    vmem = pl.BlockSpec(memory_space=pltpu.VMEM)
    return pl.pallas_call(
        body, name="ada_fwd", in_specs=[vmem, vmem, vmem, pl.BlockSpec(memory_space=pl.ANY)], out_specs=vmem,
        out_shape=jax.ShapeDtypeStruct((N_DEV, w_ada.shape[1]), F32),
    )(c_all, w_ada, b_ada_cols, dep)


def _adam(w, g, m, v):
    m = ADAM_B1 * m + (1.0 - ADAM_B1) * g
    v = ADAM_B2 * v + (1.0 - ADAM_B2) * (g * g)
    m_hat = m / (1.0 - ADAM_B1 ** ADAM_STEP)
    v_hat = v / (1.0 - ADAM_B2 ** ADAM_STEP)
    delta = -ADAM_LR * (m_hat / (jnp.sqrt(v_hat) + ADAM_EPS) + ADAM_WD * w)
    return delta, m, v


def _ada_bwd_adam(c_all, dmod_cols, w, m, v):
    def body(c_ref, d_ref, w_ref, m_ref, v_ref, g_ref, delta_ref, nm_ref, nv_ref):
        g = lax.dot_general(_silu(c_ref[...]), d_ref[...], (((0,), (0,)), ((), ())),
                            preferred_element_type=F32, precision=lax.Precision.HIGHEST)
        g_ref[...] = g
        delta_ref[...], nm_ref[...], nv_ref[...] = _adam(w_ref[...], g, m_ref[...], v_ref[...])

    sd = jax.ShapeDtypeStruct(w.shape, F32)
    return pl.pallas_call(body, name="ada_bwd_adam", out_shape=[sd] * 4,
                          compiler_params=pltpu.CompilerParams(vmem_limit_bytes=V7X_VMEM_LIMIT),
                          )(c_all, dmod_cols, w, m, v)


def _adam_group(chip_sums, arrivals, ws, ms, vs, n_tiles, name):
    n = len(ws)

    def body(*refs):
        for k in range(n):
            c_ref, a_ref, w_ref, m_ref, v_ref = (refs[j * n + k] for j in range(5))
            g_ref, delta_ref, nm_ref, nv_ref = (refs[(5 + j) * n + k] for j in range(4))
            g = c_ref[...].astype(F32)
            for j in range(N_NEAR - 1):
                g = g + a_ref[j].astype(F32)
            g_ref[...] = g
            delta_ref[...], nm_ref[...], nv_ref[...] = _adam(w_ref[...], g, m_ref[...], v_ref[...])

    tiles = [(w.shape[0] // n_tiles, w.shape[1]) for w in ws]
    blk = [pl.BlockSpec(t, lambda i: (i, 0)) for t in tiles]
    return pl.pallas_call(
        body, name=name, grid=(n_tiles,),
        in_specs=[pl.BlockSpec((None, *t), lambda i: (0, i, 0)) for t in tiles]
        + [pl.BlockSpec((N_NEAR - 1, *t), lambda i: (0, i, 0)) for t in tiles] + blk * 3,
        out_specs=blk * 4, out_shape=[jax.ShapeDtypeStruct(w.shape, F32) for w in ws] * 4,
        compiler_params=_params(("parallel",)),
    )(*chip_sums, *arrivals, *ws, *ms, *vs)


N_SMALL = 40
N_SMALL_PARAMS = 11


def _pack_vecs(conv_w_full, rows):
    def body(cw_ref, *refs):
        out = refs[-1]
        out[...] = jnp.zeros_like(out)
        out[0:4, :] = cw_ref[0:4, :]
        for r, ref in enumerate(refs[:-1]):
            out[4 + r:5 + r, :] = ref[...]

    return pl.pallas_call(body, name="pack_vecs", out_shape=jax.ShapeDtypeStruct((16, D), F32))(conv_w_full, *rows)


def _small_finish(gathered, mod_all, vecs, ws, ms, vs):
    n = N_SMALL_PARAMS

    def body(g_ref, mod_ref, vec_ref, *refs):
        w_refs, m_refs, v_refs = refs[:n], refs[n:2 * n], refs[2 * n:3 * n]
        outs = refs[3 * n:]
        g1 = vec_ref[V_G1:V_G1 + 1, :]
        g2 = vec_ref[V_G2:V_G2 + 1, :]
        zero = jnp.zeros((1, D), F32)
        dg1, dg2, dgf, loss_lanes = zero, zero, zero, zero
        mixer = jnp.zeros((16, D), F32)
        db_ada = jnp.zeros((6, D), F32)
        for b in range(N_DEV):
            gb = g_ref[b]
            mod = mod_ref[b]
            q1 = gb[33:34]
            q2 = gb[9:10]
            dmod = jnp.concatenate([gb[32:33], q1 * g1, gb[10:11], gb[8:9], q2 * g2, gb[1:2]], axis=0)
            outs[4 * n][b] = dmod
            db_ada = db_ada + dmod
            dg1 = dg1 + q1 * (1.0 + mod[M_SC1:M_SC1 + 1])
            dg2 = dg2 + q2 * (1.0 + mod[M_SC2:M_SC2 + 1])
            dgf = dgf + gb[0:1]
            loss_lanes = loss_lanes + gb[2:3]
            mixer = mixer + gb[16:32]
        d_a_param = mixer[7:8] * _sigmoid_tail(vec_ref[V_A_PARAM:V_A_PARAM + 1, :])
        grads = [dg1, dg2, mixer[4:5], mixer[5:6], mixer[6:7], d_a_param, mixer[8:9], mixer[9:10], dgf,
                 db_ada, mixer[0:4]]
        for k in range(n):
            outs[k][...] = grads[k]
            outs[n + k][...], outs[2 * n + k][...], outs[3 * n + k][...] = _adam(
                w_refs[k][...], grads[k], m_refs[k][...], v_refs[k][...])
        outs[4 * n + 1][...] = jnp.broadcast_to(jnp.sum(loss_lanes, axis=1, keepdims=True), (8, 128))

    shapes = [jax.ShapeDtypeStruct(w.shape, F32) for w in ws]
    return pl.pallas_call(
        body, name="small_finish",
        out_shape=shapes * 4 + [jax.ShapeDtypeStruct((N_DEV, 6, D), F32), jax.ShapeDtypeStruct((8, 128), F32)],
    )(gathered, mod_all, vecs, *ws, *ms, *vs)


def _pad_rows(a, rows):
    return jnp.pad(a, ((0, rows - a.shape[0]), (0, 0)))


def kernel(x, c, norm_mix_g, norm_mlp_g, w_ada, b_ada, w_in, conv_w, conv_b, w_rg_a, b_rg_a, w_rg_x, b_rg_x, a_param, w_branch_a, w_pool, b_pool, pool_scale, w_branch_b, w_out, w_up, w_down, final_g, loss_target, m_norm_mix_g, m_norm_mlp_g, m_w_ada, m_b_ada, m_w_in, m_conv_w, m_conv_b, m_w_rg_a, m_b_rg_a, m_w_rg_x, m_b_rg_x, m_a_param, m_w_branch_a, m_w_pool, m_b_pool, m_pool_scale, m_w_branch_b, m_w_out, m_w_up, m_w_down, m_final_g, v_norm_mix_g, v_norm_mlp_g, v_w_ada, v_b_ada, v_w_in, v_conv_w, v_conv_b, v_w_rg_a, v_b_rg_a, v_w_rg_x, v_b_rg_x, v_a_param, v_w_branch_a, v_w_pool, v_b_pool, v_pool_scale, v_w_branch_b, v_w_out, v_w_up, v_w_down, v_final_g):
    me = 4 * lax.axis_index("x") + 2 * lax.axis_index("y") + lax.axis_index("c")
    s = x.shape[1]
    x2d = x.reshape(s, D)
    target = loss_target.reshape(s, D)
    n_ada = w_ada.shape[2]

    b_ada_cols = lax.dynamic_slice(b_ada, (0, me * n_ada), (1, n_ada))
    spread_c = _Spread([_pad_rows(c, 8), _pad_rows(conv_w[0], 8)], [0, 1], c, "spread_c")

    sharded = dict(w_in=(w_in[0], 1), w_up=(w_up[0], 1), w_down=(w_down[0], 0), w_branch_a=(w_branch_a[0], 0),
                   w_branch_b=(w_branch_b[0], 0), w_out=(w_out[0], 0), w_rg_a=(w_rg_a[0], 1), w_rg_x=(w_rg_x[0], 1),
                   w_pool=(w_pool[0], 1))
    kind = {k: v[1] for k, v in sharded.items()}
    first_names = ["w_in"]
    later_names = [k for k in sharded if k not in first_names]
    mix_names = ["w_rg_a", "w_rg_x", "w_pool"]
    branch_names = ["w_branch_a", "w_branch_b", "w_out"]
    mlp_names = ["w_up", "w_down"]

    def gather(group, after, name):
        return _Gather([shard[k] for k in group], [kind[k] for k in group], after, name)

    shard = dict(zip(first_names, _to_bf16([sharded[k][0] for k in first_names], "to_bf16_first")))
    g_first = gather(first_names, spread_c.token, "gather_first")
    shard.update(zip(later_names, _to_bf16([sharded[k][0] for k in later_names], "to_bf16_later", dep=g_first.token)))

    c_rows, conv_w_full = spread_c.finish(g_first.token)
    c_all = c_rows.reshape(N_DEV, 8, D)[:, 0, :]
    mod_part = _ada_fwd(c_all, w_ada[0], b_ada_cols, g_first.token)
    spread_mod = _Spread([mod_part], [0], g_first.token, "spread_mod")
    spread_mix = _Spread([shard[k] for k in mix_names], [kind[k] for k in mix_names], spread_mod.token, "spread_mix")
    g_branch = gather(branch_names, spread_mix.token, "gather_branch")
    g_mlp = gather(mlp_names, g_branch.token, "gather_mlp")

    vecs = _pack_vecs(conv_w_full, [conv_b, b_rg_a, b_rg_x, a_param, b_pool, pool_scale,
                                    norm_mix_g, norm_mlp_g, final_g.reshape(1, D)])
    g_first.forward(g_mlp.token)
    wg = dict(zip(first_names, g_first.finish(g_first.token)))
    mod_parts, = spread_mod.finish(g_first.token)
    mod_all = jnp.transpose(mod_parts.reshape(N_DEV, N_DEV, n_ada), (1, 0, 2)).reshape(N_DEV, 6, D)
    mod_all = jnp.pad(mod_all, ((0, 0), (0, 2), (0, 0)))
    modr = lax.dynamic_index_in_dim(mod_all, me, 0, keepdims=False)

    h1, x_rnn, u_pool, ga, dga, sa, sb = _proj_fwd(x2d, modr, vecs, wg["w_in"])
    g_branch.forward(h1)
    wg.update(zip(mix_names, spread_mix.finish(g_branch.token)))
    xr, hr, za, p, pooled, *gates = _mix_fwd(x_rnn, u_pool, ga, vecs, wg["w_rg_a"], wg["w_rg_x"], wg["w_pool"],
                                             dep=g_branch.token)
    g_mlp.forward(za)
    wg.update(zip(branch_names, g_branch.finish(g_mlp.token)))
    ba, bb, merged, o, x2, h2 = _branch_fwd(za, pooled, sa, sb, x2d, modr, vecs,
                                            wg["w_branch_a"], wg["w_branch_b"], wg["w_out"])
    wg.update(zip(mlp_names, g_mlp.finish(h2)))
    ru, dx3, d_dn, small_f = _mlp_fwd(h2, x2, target, modr, vecs, wg["w_up"], wg["w_down"])

    near = _near(lax.axis_index("x"), lax.axis_index("y"), lax.axis_index("c"))
    own4 = jnp.stack([me, near[1][1], near[2][1], near[3][1]]).astype(jnp.int32)

    def scatter(group, partial, after, name):
        return _Scatter([partial[k] for k in group], [kind[k] for k in group], after, name)

    dup, dx2, do, small_m = _mlp_bwd(d_dn, ru, x2, dx3, o, modr, vecs, wg["w_up"], wg["w_down"])
    partial = dict(w_up=_wgrad(h2, dup, "wgrad_up"), w_down=_wgrad(ru, d_dn, "wgrad_down", square_a=True))
    s_mlp = scatter(mlp_names, partial, dx2, "scatter_mlp")

    dba, dbb, dgates, dza, dpooled = _branch_bwd(do, sa, sb, ba, bb, wg["w_branch_a"], wg["w_branch_b"], wg["w_out"],
                                                 dep=s_mlp.token)
    s_mlp.combine_and_send(own4, dza)
    dproj, dw_rg_a, dw_rg_x, dw_pool, small_x = _mix_bwd(dza, dpooled, x_rnn, ga, dga, xr, hr, p, gates, dgates,
                                                         vecs, wg["w_rg_a"], wg["w_rg_x"], wg["w_pool"],
                                                         dep=s_mlp.token)
    partial.update(w_branch_a=_wgrad(za, dba, "wgrad_branch_a"), w_branch_b=_wgrad(pooled, dbb, "wgrad_branch_b"),
                   w_out=_wgrad(merged, do, "wgrad_out"),
                   w_rg_a=dw_rg_a, w_rg_x=dw_rg_x, w_pool=dw_pool)
    mixer_names = ["w_rg_a", "w_rg_x", "w_pool", "w_branch_a", "w_branch_b", "w_out"]
    s_mixer = scatter(mixer_names, partial, s_mlp.token, "scatter_mixer")

    partial["w_in"] = _wgrad(h1, dproj, "wgrad_in", dep=s_mixer.token)
    s_in = scatter(["w_in"], partial, s_mixer.token, "scatter_in")
    s_mixer.combine_and_send(own4, s_in.token)
    s_in.combine_and_send(own4, s_mixer.token)
    grad_x, small_p = _proj_bwd(dproj, x2d, dx2, modr, vecs, wg["w_in"], dep=s_in.token)

    locals_ = dict(w_in=(w_in, m_w_in, v_w_in), w_up=(w_up, m_w_up, v_w_up), w_down=(w_down, m_w_down, v_w_down),
                   w_branch_a=(w_branch_a, m_w_branch_a, v_w_branch_a),
                   w_branch_b=(w_branch_b, m_w_branch_b, v_w_branch_b), w_out=(w_out, m_w_out, v_w_out),
                   w_rg_a=(w_rg_a, m_w_rg_a, v_w_rg_a), w_rg_x=(w_rg_x, m_w_rg_x, v_w_rg_x),
                   w_pool=(w_pool, m_w_pool, v_w_pool))
    res = {}

    def finish(group, exchange, after, n_tiles, name):
        chip_sums, arrivals = exchange.finish(after)
        flat = lambda t: t.reshape(-1, t.shape[-1])
        shapes = [flat(locals_[k][0]).shape for k in group]
        outs = _adam_group([cs.reshape(N_NEAR, *sh) for cs, sh in zip(chip_sums, shapes)],
                           [ar.reshape(N_NEAR - 1, *sh) for ar, sh in zip(arrivals, shapes)],
                           *[[flat(locals_[k][j]) for k in group] for j in range(3)], n_tiles, name)
        for i, k in enumerate(group):
            res[k] = [outs[j * len(group) + i].reshape(locals_[k][0].shape) for j in range(4)]
        return res[group[-1]][0]

    small = jnp.concatenate([small_f, small_m, small_x, small_p], axis=0)
    g_small = _Gather([small], [0], grad_x, "gather_small")
    done = finish(mlp_names, s_mlp, g_small.token, 4, "adam_mlp")
    done = finish(mixer_names, s_mixer, done, 2, "adam_mixer")
    g_small.forward(done)
    done = finish(["w_in"], s_in, g_small.token, 4, "adam_in")
    small_all, = g_small.finish(done)
    small_all = small_all.reshape(N_DEV, N_SMALL, D)

    def embed(cw):
        return lax.dynamic_update_slice(jnp.zeros((4, D), F32), cw[0], (0, me * (D // N_DEV)))

    def smalls(ng, nl, cb, bra, brx, ap, bp, ps, fg, ba_, cw):
        return [ng, nl, cb, bra, brx, ap, bp, ps, fg.reshape(1, D), ba_.reshape(6, D), embed(cw)]

    small_names = ["norm_mix_g", "norm_mlp_g", "conv_b", "b_rg_a", "b_rg_x", "a_param", "b_pool", "pool_scale",
                   "final_g", "b_ada", "conv_w"]
    fin = _small_finish(
        small_all, mod_all, vecs,
        smalls(norm_mix_g, norm_mlp_g, conv_b, b_rg_a, b_rg_x, a_param, b_pool, pool_scale, final_g, b_ada, conv_w),
        smalls(m_norm_mix_g, m_norm_mlp_g, m_conv_b, m_b_rg_a, m_b_rg_x, m_a_param, m_b_pool, m_pool_scale,
               m_final_g, m_b_ada, m_conv_w),
        smalls(v_norm_mix_g, v_norm_mlp_g, v_conv_b, v_b_rg_a, v_b_rg_x, v_a_param, v_b_pool, v_pool_scale,
               v_final_g, v_b_ada, v_conv_w))
    dmod_all, loss_tile = fin[4 * N_SMALL_PARAMS], fin[4 * N_SMALL_PARAMS + 1]
    dmod_cols = lax.dynamic_slice(dmod_all.reshape(N_DEV, 6 * D), (0, me * n_ada), (N_DEV, n_ada))
    res["w_ada"] = [t.reshape(w_ada.shape) for t in _ada_bwd_adam(c_all, dmod_cols, w_ada[0], m_w_ada[0], v_w_ada[0])]

    def final_shape(k, t):
        if k == "final_g":
            return t.reshape(D)
        if k == "b_ada":
            return t.reshape(1, 6 * D)
        if k == "conv_w":
            return lax.dynamic_slice(t, (0, me * (D // N_DEV)), (4, D // N_DEV)).reshape(conv_w.shape)
        return t

    for i, k in enumerate(small_names):
        res[k] = [final_shape(k, fin[which * N_SMALL_PARAMS + i]) for which in range(4)]
    order = ["norm_mix_g", "norm_mlp_g", "w_ada", "b_ada", "w_in", "conv_w", "conv_b", "w_rg_a", "b_rg_a", "w_rg_x",
             "b_rg_x", "a_param", "w_branch_a", "w_pool", "b_pool", "pool_scale", "w_branch_b", "w_out", "w_up",
             "w_down", "final_g"]
    outs = [loss_tile[0, 0], grad_x.reshape(x.shape)]
    for which in range(4):
        for k in order:
            outs.append(res[k][which])
    return tuple(outs)
```

```python
import functools

import jax
import jax.numpy as jnp
from jax import lax
from jax.experimental import pallas as pl
from jax.experimental.pallas import tpu as pltpu

F32 = jnp.float32
BF16 = jnp.bfloat16
MESH = pl.DeviceIdType.MESH

N_DEV = 8
D = 1024
N_GROUPS = 4
GW = D // N_GROUPS
D_IN = 5 * D
D_FF = 4 * D
POOL_WINDOWS = (2, 4, 8, 16)
HALO_X = 8
HALO_U = 16
EPS = 1e-6
C_RG = 8.0
ADAM_LR, ADAM_B1, ADAM_B2, ADAM_EPS, ADAM_WD, ADAM_STEP = 0.001, 0.9, 0.999, 1e-08, 0.01, 10

V7X_VMEM_LIMIT = 56 * 1024 * 1024

V_CONV_W, V_CONV_B, V_B_RG_A, V_B_RG_X, V_A_PARAM, V_B_POOL, V_POOL_SCALE, V_G1, V_G2, V_GF = 0, 4, 5, 6, 7, 8, 9, 10, 11, 12
M_SH1, M_SC1, M_GT1, M_SH2, M_SC2, M_GT2 = 0, 1, 2, 3, 4, 5

TM_PROJ = 512
TM_MIX = 256
TM_BRANCH = 512
TM_MLP = 512
TM_MLP_BWD = 256
TS_WGRAD = 1024


def _params(semantics):
    return pltpu.CompilerParams(dimension_semantics=semantics, vmem_limit_bytes=V7X_VMEM_LIMIT)


def _resident(shape):
    return pl.BlockSpec(shape, lambda *_: (0,) * len(shape), pipeline_mode=pl.Buffered(1))


def _dot(a, b):
    return jnp.dot(a, b, preferred_element_type=F32)


def _dot_nt(a, b):
    return lax.dot_general(a, b, (((1,), (1,)), ((), ())), preferred_element_type=F32)


def _dot_tn(a, b):
    return lax.dot_general(a, b, (((0,), (0,)), ((), ())), preferred_element_type=F32)


def _sigmoid(x):
    return 0.5 * jnp.tanh(0.5 * x) + 0.5


def _sigmoid_tail(x):
    return 1.0 / (1.0 + jnp.exp(-x))


def _gelu_and_grad(x):
    k = 0.7978845608028654
    x2 = x * x
    t = jnp.tanh(k * (x + 0.044715 * x * x2))
    g = 0.5 * x * (1.0 + t)
    dg = 0.5 * (1.0 + t) + 0.5 * x * (1.0 - t * t) * (k * (1.0 + 3.0 * 0.044715 * x2))
    return g, dg


def _softplus(a):
    e = jnp.exp(-jnp.abs(a))
    u = 1.0 + e
    log1p_e = jnp.where(u == 1.0, e, jnp.log(u) * e / jnp.where(u == 1.0, 1.0, u - 1.0))
    return jnp.maximum(a, 0.0) + log1p_e


def _neg_expm1(z):
    series = -(z * (1.0 + z * (0.5 + z * (1.0 / 6.0 + z * (1.0 / 24.0 + z * (1.0 / 120.0))))))
    return jnp.where(z > -0.1, series, 1.0 - jnp.exp(z))


def _shift_down(x, k):
    return pltpu.roll(x, k, 0)


def _shift_up(x, k):
    return pltpu.roll(x, x.shape[0] - k, 0)


def _rglru_gates(xr, w_a, w_x, b_a, b_x, a_param, is_t0):
    xb = xr.astype(BF16)
    ra = _sigmoid(_dot(xb, w_a) + b_a)
    ri = _sigmoid(_dot(xb, w_x) + b_x)
    sp = _softplus(a_param)
    log_a = (-C_RG) * ra * sp
    a = jnp.exp(log_a)
    mult = jnp.where(is_t0, 1.0, jnp.sqrt(_neg_expm1(2.0 * log_a)))
    return ra, ri, sp, a, mult


SUBLANES = 8


LANES = 128


def _scan_strip(a, b, carry, scr, down):
    t = b.shape[0]
    g = t // SUBLANES
    a3 = a.reshape(g, SUBLANES, LANES)
    b3 = b.reshape(g, SUBLANES, LANES)
    sub = lax.broadcasted_iota(jnp.int32, (g, SUBLANES, LANES), 1)
    for k in (1, 2, 4):
        keep = sub >= k if down else sub < SUBLANES - k
        shift = k if down else SUBLANES - k
        b3 = b3 + a3 * jnp.where(keep, pltpu.roll(b3, shift, 1), 0.0)
        a3 = a3 * jnp.where(keep, pltpu.roll(a3, shift, 1), 1.0)
    scr[0] = a3.reshape(t, LANES)
    scr[1] = b3.reshape(t, LANES)
    end_row = SUBLANES - 1 if down else 0
    ag = scr[0, pl.ds(end_row, g, stride=SUBLANES), :]
    bg = scr[1, pl.ds(end_row, g, stride=SUBLANES), :]
    rg = lax.broadcasted_iota(jnp.int32, (g, LANES), 0)
    edge = 0 if down else g - 1
    bg = bg + jnp.where(rg == edge, ag * carry, 0.0)
    k = 1
    while k < g:
        keep = rg >= k if down else rg < g - k
        shift = k if down else g - k
        bg = bg + ag * jnp.where(keep, pltpu.roll(bg, shift, 0), 0.0)
        if 2 * k < g:
            ag = ag * pltpu.roll(ag, shift, 0)
        k *= 2
    entering = jnp.where(rg != edge, pltpu.roll(bg, 1 if down else g - 1, 0), carry)
    for r in range(SUBLANES):
        scr[2, pl.ds(r, g, stride=SUBLANES), :] = entering
    return scr[1] + scr[0] * scr[2], bg[g - 1:g, :]


def _scan_strips(a, b, carry, scr, down):
    outs = [_scan_strip(a[:, c:c + LANES], b[:, c:c + LANES], carry[:, c:c + LANES], scr, down)
            for c in range(0, b.shape[1], LANES)]
    return jnp.concatenate([o[0] for o in outs], axis=1), jnp.concatenate([o[1] for o in outs], axis=1)


def _scan_down(a, b, carry, scr):
    return _scan_strips(a, b, carry, scr, True)


def _scan_up(m, b, carry, scr):
    return _scan_strips(m, b, carry, scr, False)[0]


def _window_mean(sums, window, first_block, head_t):
    scaled = sums * (1.0 / window)
    head = jnp.where(first_block, sums[:HALO_U] / jnp.minimum(head_t, float(window)), scaled[:HALO_U])
    return jnp.concatenate([head, scaled[HALO_U:]], axis=0)


def _conv_taps(x_ext):
    return [_shift_down(x_ext, 3 - j)[HALO_X:] if j < 3 else x_ext[HALO_X:] for j in range(4)]


def _proj_fwd(x, modr, vecs, w_in):
    s = x.shape[0]
    tm = min(TM_PROJ, s)

    def body(x_ref, mod_ref, vec_ref, w_ref, h1_ref, xrnn_ref, u_ref, ga_ref, dga_ref, sa_ref, sb_ref):
        xv = x_ref[...]
        r = lax.rsqrt(jnp.mean(xv * xv, axis=-1, keepdims=True) + EPS)
        gain = vec_ref[V_G1:V_G1 + 1, :] * (1.0 + mod_ref[M_SC1:M_SC1 + 1, :])
        h = (xv * r * gain + mod_ref[M_SH1:M_SH1 + 1, :]).astype(BF16)
        h1_ref[...] = h
        xrnn_ref[...] = _dot(h, w_ref[:, 0:D])
        ga_ref[...], dga_ref[...] = _gelu_and_grad(_dot(h, w_ref[:, D:2 * D]))
        u_ref[...] = _dot(h, w_ref[:, 2 * D:3 * D])
        sa_ref[...] = _sigmoid(_dot(h, w_ref[:, 3 * D:4 * D]))
        sb_ref[...] = _sigmoid(_dot(h, w_ref[:, 4 * D:5 * D]))

    tok = pl.BlockSpec((tm, D), lambda i: (i, 0))
    sd = lambda dt: jax.ShapeDtypeStruct((s, D), dt)
    return pl.pallas_call(
        body, name="proj_fwd", grid=(s // tm,),
        in_specs=[tok, pl.BlockSpec((8, D), lambda i: (0, 0)), pl.BlockSpec((16, D), lambda i: (0, 0)),
                  _resident((D, D_IN))],
        out_specs=[tok] * 7,
        out_shape=[sd(BF16)] + [sd(F32)] * 6,
        compiler_params=_params(("parallel",)),
    )(x, modr, vecs, w_in)


def _mix_fwd(x_rnn, u_pool, ga, vecs, w_rg_a, w_rg_x, w_pool, dep):
    s = x_rnn.shape[0]
    tm = min(TM_MIX, s)
    nb = s // tm

    def body(xh_ref, x_ref, uh_ref, u_ref, ga_ref, vec_ref, wa_ref, wx_ref, wp_ref, dep_ref,
             xr_ref, hr_ref, za_ref, p_ref, pooled_ref, a_ref, mult_ref, ra_ref, ri_ref, carry_ref, scan_scr):
        i = pl.program_id(0)
        first = i == 0

        @pl.when(first)
        def _():
            carry_ref[...] = jnp.zeros_like(carry_ref)

        row = lax.broadcasted_iota(jnp.int32, (tm, GW), 0)
        is_t0 = jnp.logical_and(first, row == 0)
        head_t = (lax.broadcasted_iota(jnp.int32, (HALO_U, GW), 0) + 1).astype(F32)
        for g in range(N_GROUPS):
            cs = slice(g * GW, (g + 1) * GW)
            vec = vec_ref[:, cs]
            xh = jnp.where(first, 0.0, xh_ref[:, cs])
            taps = _conv_taps(jnp.concatenate([xh, x_ref[:, cs]], axis=0))
            xr = vec[V_CONV_B:V_CONV_B + 1]
            for j in range(4):
                xr = xr + vec[V_CONV_W + j:V_CONV_W + j + 1] * taps[j]
            xr_ref[:, cs] = xr
            ra, ri, _, a, mult = _rglru_gates(
                xr, wa_ref[g], wx_ref[g], vec[V_B_RG_A:V_B_RG_A + 1], vec[V_B_RG_X:V_B_RG_X + 1],
                vec[V_A_PARAM:V_A_PARAM + 1], is_t0)
            a_ref[:, cs] = a
            mult_ref[:, cs] = mult
            ra_ref[:, cs] = ra.astype(BF16)
            ri_ref[:, cs] = ri.astype(BF16)
            h, last = _scan_down(a, xr * ri * mult, carry_ref[0:1, cs], scan_scr)
            hr_ref[:, cs] = h
            carry_ref[0:1, cs] = last
            za_ref[:, cs] = (ga_ref[:, cs] * h).astype(BF16)
            uh = jnp.where(first, 0.0, uh_ref[:, cs])
            sm = jnp.concatenate([uh, u_ref[:, cs]], axis=0)
            k = 1
            while k < POOL_WINDOWS[g]:
                sm = sm + _shift_down(sm, k)
                k *= 2
            mean = _window_mean(sm[HALO_U:], POOL_WINDOWS[g], first, head_t)
            p = (mean - u_ref[:, cs]).astype(BF16)
            p_ref[:, cs] = p
            pb = _dot(p, wp_ref[g]) + vec[V_B_POOL:V_B_POOL + 1]
            pooled_ref[:, cs] = (pb * vec[V_POOL_SCALE:V_POOL_SCALE + 1]).astype(BF16)

    tok = pl.BlockSpec((tm, D), lambda i: (i, 0))
    halo = lambda rows: pl.BlockSpec((rows, D), lambda i: (jnp.maximum(i * (tm // rows) - 1, 0), 0))
    wspec = pl.BlockSpec((N_GROUPS, GW, GW), lambda i: (0, 0, 0))
    sd = lambda dt: jax.ShapeDtypeStruct((s, D), dt)
    return pl.pallas_call(
        body, name="mix_fwd", grid=(nb,),
        in_specs=[halo(HALO_X), tok, halo(HALO_U), tok, tok, pl.BlockSpec((16, D), lambda i: (0, 0)),
                  wspec, wspec, wspec, pl.BlockSpec(memory_space=pl.ANY)],
        out_specs=[tok] * 9,
        out_shape=[sd(F32), sd(F32), sd(BF16), sd(BF16), sd(BF16), sd(F32), sd(F32), sd(BF16), sd(BF16)],
        scratch_shapes=[pltpu.VMEM((8, D), F32), pltpu.VMEM((3, tm, LANES), F32)],
        compiler_params=_params(("arbitrary",)),
    )(x_rnn, x_rnn, u_pool, u_pool, ga, vecs, w_rg_a, w_rg_x, w_pool, dep)


def _branch_fwd(za, pooled, sa, sb, x, modr, vecs, w_a, w_b, w_out):
    s = x.shape[0]
    tm = min(TM_BRANCH, s)

    def body(za_ref, pooled_ref, sa_ref, sb_ref, x_ref, mod_ref, vec_ref, wa_ref, wb_ref, wo_ref,
             ba_ref, bb_ref, merged_ref, o_ref, x2_ref, h2_ref):
        ba = _dot(za_ref[...], wa_ref[...])
        bb = _dot(pooled_ref[...], wb_ref[...])
        ba_ref[...] = ba.astype(BF16)
        bb_ref[...] = bb.astype(BF16)
        merged = (sa_ref[...] * ba + sb_ref[...] * bb).astype(BF16)
        merged_ref[...] = merged
        o = _dot(merged, wo_ref[...])
        o_ref[...] = o.astype(BF16)
        x2 = x_ref[...] + mod_ref[M_GT1:M_GT1 + 1, :] * o
        x2_ref[...] = x2
        r = lax.rsqrt(jnp.mean(x2 * x2, axis=-1, keepdims=True) + EPS)
        gain = vec_ref[V_G2:V_G2 + 1, :] * (1.0 + mod_ref[M_SC2:M_SC2 + 1, :])
        h2_ref[...] = (x2 * r * gain + mod_ref[M_SH2:M_SH2 + 1, :]).astype(BF16)

    tok = pl.BlockSpec((tm, D), lambda i: (i, 0))
    wspec = pl.BlockSpec((D, D), lambda i: (0, 0))
    sd = lambda dt: jax.ShapeDtypeStruct((s, D), dt)
    return pl.pallas_call(
        body, name="branch_fwd", grid=(s // tm,),
        in_specs=[tok, tok, tok, tok,
                  tok, pl.BlockSpec((8, D), lambda i: (0, 0)), pl.BlockSpec((16, D), lambda i: (0, 0)),
                  wspec, wspec, wspec],
        out_specs=[tok] * 6,
        out_shape=[sd(BF16), sd(BF16), sd(BF16), sd(BF16), sd(F32), sd(BF16)],
        compiler_params=_params(("parallel",)),
    )(za, pooled, sa, sb, x, modr, vecs, w_a, w_b, w_out)


def _mlp_fwd(h2, x2, target, modr, vecs, w_up, w_down):
    s = x2.shape[0]
    tm = min(TM_MLP, s)

    def body(h2_ref, x2_ref, tgt_ref, mod_ref, vec_ref, wu_ref, wd_ref,
             ru_ref, dx3_ref, ddn_ref, small_ref):
        @pl.when(pl.program_id(0) == 0)
        def _():
            small_ref[...] = jnp.zeros_like(small_ref)

        h2 = h2_ref[...]
        dn = None
        for c in range(D_FF // D):
            cs = slice(c * D, (c + 1) * D)
            ru = jnp.maximum(_dot(h2, wu_ref[:, cs]), 0.0)
            ru_ref[:, cs] = ru.astype(BF16)
            part = _dot((ru * ru).astype(BF16), wd_ref[cs, :])
            dn = part if dn is None else dn + part
        gt2 = mod_ref[M_GT2:M_GT2 + 1, :]
        gf = vec_ref[V_GF:V_GF + 1, :]
        x3 = x2_ref[...] + gt2 * dn
        r3 = lax.rsqrt(jnp.mean(x3 * x3, axis=-1, keepdims=True) + EPS)
        n3 = x3 * r3
        err = n3 * gf - tgt_ref[...]
        dy = err * (1.0 / D)
        dn3 = dy * gf
        dx3 = r3 * (dn3 - n3 * jnp.mean(dn3 * n3, axis=-1, keepdims=True))
        dx3_ref[...] = dx3
        ddn_ref[...] = (dx3 * gt2).astype(BF16)
        small_ref[0:1, :] += jnp.sum(dy * n3, axis=0, keepdims=True)
        small_ref[1:2, :] += jnp.sum(dx3 * dn, axis=0, keepdims=True)
        small_ref[2:3, :] += (0.5 / D) * jnp.sum(err * err, axis=0, keepdims=True)

    tok = pl.BlockSpec((tm, D), lambda i: (i, 0))
    return pl.pallas_call(
        body, name="mlp_fwd", grid=(s // tm,),
        in_specs=[tok, tok, tok,
                  pl.BlockSpec((8, D), lambda i: (0, 0)), pl.BlockSpec((16, D), lambda i: (0, 0)),
                  _resident((D, D_FF)), _resident((D_FF, D))],
        out_specs=[pl.BlockSpec((tm, D_FF), lambda i: (i, 0)), tok, tok,
                   pl.BlockSpec((8, D), lambda i: (0, 0))],
        out_shape=[jax.ShapeDtypeStruct((s, D_FF), BF16), jax.ShapeDtypeStruct((s, D), F32),
                   jax.ShapeDtypeStruct((s, D), BF16), jax.ShapeDtypeStruct((8, D), F32)],
        compiler_params=_params(("arbitrary",)),
    )(h2, x2, target, modr, vecs, w_up, w_down)


def _mlp_bwd(d_dn, ru, x2, dx3, o, modr, vecs, w_up, w_down):
    s = x2.shape[0]
    tm = min(TM_MLP_BWD, s)

    def body(ddn_ref, ru_ref, x2_ref, dx3_ref, o_ref, mod_ref, vec_ref, wu_ref, wd_ref,
             dup_ref, dx2_ref, do_ref, small_ref):
        @pl.when(pl.program_id(0) == 0)
        def _():
            small_ref[...] = jnp.zeros_like(small_ref)

        ddn = ddn_ref[...]
        dh2 = None
        for c in range(D_FF // D):
            cs = slice(c * D, (c + 1) * D)
            dff = _dot_nt(ddn, wd_ref[cs, :])
            dup = (dff * (2.0 * ru_ref[:, cs].astype(F32))).astype(BF16)
            dup_ref[:, cs] = dup
            part = _dot_nt(dup, wu_ref[:, cs])
            dh2 = part if dh2 is None else dh2 + part
        x2 = x2_ref[...]
        r2 = lax.rsqrt(jnp.mean(x2 * x2, axis=-1, keepdims=True) + EPS)
        xn2 = x2 * r2
        gain = vec_ref[V_G2:V_G2 + 1, :] * (1.0 + mod_ref[M_SC2:M_SC2 + 1, :])
        dxn2 = dh2 * gain
        dx2 = dx3_ref[...] + r2 * (dxn2 - xn2 * jnp.mean(dxn2 * xn2, axis=-1, keepdims=True))
        dx2_ref[...] = dx2
        do_ref[...] = (dx2 * mod_ref[M_GT1:M_GT1 + 1, :]).astype(BF16)
        small_ref[0:1, :] += jnp.sum(dh2, axis=0, keepdims=True)
        small_ref[1:2, :] += jnp.sum(dh2 * xn2, axis=0, keepdims=True)
        small_ref[2:3, :] += jnp.sum(dx2 * o_ref[...].astype(F32), axis=0, keepdims=True)

    tok = pl.BlockSpec((tm, D), lambda i: (i, 0))
    wide = pl.BlockSpec((tm, D_FF), lambda i: (i, 0))
    return pl.pallas_call(
        body, name="mlp_bwd", grid=(s // tm,),
        in_specs=[tok, wide, tok, tok, tok,
                  pl.BlockSpec((8, D), lambda i: (0, 0)), pl.BlockSpec((16, D), lambda i: (0, 0)),
                  _resident((D, D_FF)), _resident((D_FF, D))],
        out_specs=[wide, tok, tok, pl.BlockSpec((8, D), lambda i: (0, 0))],
        out_shape=[jax.ShapeDtypeStruct((s, D_FF), BF16), jax.ShapeDtypeStruct((s, D), F32),
                   jax.ShapeDtypeStruct((s, D), BF16), jax.ShapeDtypeStruct((8, D), F32)],
        compiler_params=_params(("arbitrary",)),
    )(d_dn, ru, x2, dx3, o, modr, vecs, w_up, w_down)


def _branch_bwd(do, sa, sb, ba, bb, w_a, w_b, w_out, dep):
    s = do.shape[0]
    tm = min(TM_BRANCH, s)

    def body(do_ref, sa_ref, sb_ref, ba_ref, bb_ref, wa_ref, wb_ref, wo_ref, dep_ref,
             dba_ref, dbb_ref, dg_ref, dza_ref, dpooled_ref):
        dmerged = _dot_nt(do_ref[...], wo_ref[...])
        sa = sa_ref[...]
        sb = sb_ref[...]
        dba = (dmerged * sa).astype(BF16)
        dbb = (dmerged * sb).astype(BF16)
        dba_ref[...] = dba
        dbb_ref[...] = dbb
        dg_ref[:, :D] = (dmerged * ba_ref[...].astype(F32) * sa * (1.0 - sa)).astype(BF16)
        dg_ref[:, D:] = (dmerged * bb_ref[...].astype(F32) * sb * (1.0 - sb)).astype(BF16)
        dza_ref[...] = _dot_nt(dba, wa_ref[...])
        dpooled_ref[...] = _dot_nt(dbb, wb_ref[...])

    tok = pl.BlockSpec((tm, D), lambda i: (i, 0))
    wspec = pl.BlockSpec((D, D), lambda i: (0, 0))
    sd = lambda dt: jax.ShapeDtypeStruct((s, D), dt)
    return pl.pallas_call(
        body, name="branch_bwd", grid=(s // tm,),
        in_specs=[tok, tok, tok, tok, tok, wspec, wspec, wspec, pl.BlockSpec(memory_space=pl.ANY)],
        out_specs=[tok, tok, pl.BlockSpec((tm, 2 * D), lambda i: (i, 0)), tok, tok],
        out_shape=[sd(BF16), sd(BF16), jax.ShapeDtypeStruct((s, 2 * D), BF16), sd(F32), sd(F32)],
        compiler_params=_params(("parallel",)),
    )(do, sa, sb, ba, bb, w_a, w_b, w_out, dep)


def _mix_bwd(dza, dpooled, x_rnn, ga, dga, xr, hr, p, gates, dgates, vecs, w_rg_a, w_rg_x, w_pool, dep):
    s = xr.shape[0]
    tm = min(TM_MIX, s)
    nb = s // tm

    def body(dza_ref, dpooled_ref, xh_ref, x_ref, ga_ref, dga_ref, xr_ref, hh_ref, hr_ref, p_ref,
             a_ref, mult_ref, ra_ref, ri_ref, dg_ref, vec_ref, wa_ref, wx_ref, wp_ref, dep_ref,
             dproj_ref, dwa_ref, dwx_ref, dwp_ref, small_ref,
             scan_carry, dxr_carry, q_carry, scan_scr, dwa_acc, dwx_acc, dwp_acc):
        i = pl.program_id(0)
        bi = nb - 1 - i
        first_t = bi == 0

        @pl.when(i == 0)
        def _():
            scan_carry[...] = jnp.zeros_like(scan_carry)
            dxr_carry[...] = jnp.zeros_like(dxr_carry)
            q_carry[...] = jnp.zeros_like(q_carry)
            dwa_acc[...] = jnp.zeros_like(dwa_acc)
            dwx_acc[...] = jnp.zeros_like(dwx_acc)
            dwp_acc[...] = jnp.zeros_like(dwp_acc)
            small_ref[...] = jnp.zeros_like(small_ref)

        row = lax.broadcasted_iota(jnp.int32, (tm, GW), 0)
        is_t0 = jnp.logical_and(first_t, row == 0)
        head_t = (lax.broadcasted_iota(jnp.int32, (HALO_U, GW), 0) + 1).astype(F32)
        colsum = lambda v: jnp.sum(v, axis=0, keepdims=True)
        for g in range(N_GROUPS):
            cs = slice(g * GW, (g + 1) * GW)
            vec = vec_ref[:, cs]
            xr = xr_ref[:, cs]
            hr = hr_ref[:, cs]
            dza = dza_ref[:, cs]
            dproj_ref[:, D + g * GW:D + (g + 1) * GW] = (dza * hr * dga_ref[:, cs]).astype(BF16)
            dhr = dza * ga_ref[:, cs]
            a = a_ref[:, cs]
            mult = mult_ref[:, cs]
            ra = ra_ref[:, cs].astype(F32)
            ri = ri_ref[:, cs].astype(F32)
            sp = _softplus(vec[V_A_PARAM:V_A_PARAM + 1])
            m = jnp.where(row == tm - 1, 1.0, _shift_up(a, 1))
            gsum = _scan_up(m, dhr, scan_carry[0:1, cs], scan_scr)
            scan_carry[0:1, cs] = a[0:1, :] * gsum[0:1, :]
            hh = jnp.where(first_t, 0.0, hh_ref[:, cs])
            hprev = _shift_down(jnp.concatenate([hh, hr], axis=0), 1)[8:]
            da = gsum * hprev
            dmult = jnp.where(is_t0, 0.0, gsum * xr * ri)
            dlog_a = da * a - dmult * a * a / mult
            dri = gsum * xr * mult
            dxr = gsum * ri * mult
            small_ref[7:8, cs] += colsum((-C_RG) * ra * dlog_a)
            dpa = (((-C_RG) * sp) * dlog_a * ra * (1.0 - ra))
            dpx = dri * ri * (1.0 - ri)
            small_ref[5:6, cs] += colsum(dpa)
            small_ref[6:7, cs] += colsum(dpx)
            dpa = dpa.astype(BF16)
            dpx = dpx.astype(BF16)
            xrb = xr.astype(BF16)
            dwa_acc[g] += _dot_tn(xrb, dpa)
            dwx_acc[g] += _dot_tn(xrb, dpx)
            dxr = dxr + _dot_nt(dpa, wa_ref[g]) + _dot_nt(dpx, wx_ref[g])
            small_ref[4:5, cs] += colsum(dxr)
            xh = jnp.where(first_t, 0.0, xh_ref[:, cs])
            taps = _conv_taps(jnp.concatenate([xh, x_ref[:, cs]], axis=0))
            dxr_ext = jnp.concatenate([dxr, dxr_carry[:, cs]], axis=0)
            dx = vec[V_CONV_W + 3:V_CONV_W + 4] * dxr
            for j in range(4):
                small_ref[j:j + 1, cs] += colsum(dxr * taps[j])
                if j < 3:
                    dx = dx + vec[V_CONV_W + j:V_CONV_W + j + 1] * _shift_up(dxr_ext, 3 - j)[:tm]
            dxr_carry[:, cs] = dxr[0:8, :]
            dproj_ref[:, cs] = dx.astype(BF16)
            pg = p_ref[:, cs]
            dpooled = dpooled_ref[:, cs]
            pb = _dot(pg, wp_ref[g]) + vec[V_B_POOL:V_B_POOL + 1]
            small_ref[9:10, cs] += colsum(dpooled * pb)
            dpb = dpooled * vec[V_POOL_SCALE:V_POOL_SCALE + 1]
            small_ref[8:9, cs] += colsum(dpb)
            dpbb = dpb.astype(BF16)
            dwp_acc[g] += _dot_tn(pg, dpbb)
            dp = _dot_nt(dpbb, wp_ref[g])
            q = _window_mean(dp, POOL_WINDOWS[g], first_t, head_t)
            sm = jnp.concatenate([q, q_carry[:, cs]], axis=0)
            k = 1
            while k < POOL_WINDOWS[g]:
                sm = sm + _shift_up(sm, k)
                k *= 2
            q_carry[:, cs] = q[0:HALO_U, :]
            dproj_ref[:, 2 * D + g * GW:2 * D + (g + 1) * GW] = (sm[:tm] - dp).astype(BF16)
        dproj_ref[:, 3 * D:] = dg_ref[...]

        @pl.when(i == nb - 1)
        def _():
            dwa_ref[...] = dwa_acc[...].astype(BF16)
            dwx_ref[...] = dwx_acc[...].astype(BF16)
            dwp_ref[...] = dwp_acc[...].astype(BF16)

    rev = lambda i: nb - 1 - i
    tok = pl.BlockSpec((tm, D), lambda i: (rev(i), 0))
    halo8 = lambda k: pl.BlockSpec((8, D), lambda i: (jnp.maximum(rev(i) * (tm // 8) - 1, 0), k))
    wspec = pl.BlockSpec((N_GROUPS, GW, GW), lambda i: (0, 0, 0))
    wshape = jax.ShapeDtypeStruct((N_GROUPS, GW, GW), BF16)
    return pl.pallas_call(
        body, name="mix_bwd", grid=(nb,),
        in_specs=[tok, tok, halo8(0), tok, tok, tok, tok, halo8(0), tok, tok, tok, tok, tok, tok,
                  pl.BlockSpec((tm, 2 * D), lambda i: (rev(i), 0)),
                  pl.BlockSpec((16, D), lambda i: (0, 0)), wspec, wspec, wspec, pl.BlockSpec(memory_space=pl.ANY)],
        out_specs=[pl.BlockSpec((tm, D_IN), lambda i: (rev(i), 0)), wspec, wspec, wspec,
                   pl.BlockSpec((16, D), lambda i: (0, 0))],
        out_shape=[jax.ShapeDtypeStruct((s, D_IN), BF16), wshape, wshape, wshape,
                   jax.ShapeDtypeStruct((16, D), F32)],
        scratch_shapes=[pltpu.VMEM((8, D), F32), pltpu.VMEM((8, D), F32), pltpu.VMEM((HALO_U, D), F32),
                        pltpu.VMEM((3, tm, LANES), F32)] + [pltpu.VMEM((N_GROUPS, GW, GW), F32)] * 3,
        compiler_params=_params(("arbitrary",)),
    )(dza, dpooled, x_rnn, x_rnn, ga, dga, xr, hr, hr, p, *gates, dgates, vecs, w_rg_a, w_rg_x, w_pool, dep)


def _proj_bwd(dproj, x, dx2, modr, vecs, w_in, dep):
    s = x.shape[0]
    tm = min(TM_PROJ, s)

    def body(dp_ref, x_ref, dx2_ref, mod_ref, vec_ref, w_ref, dep_ref, gx_ref, small_ref):
        @pl.when(pl.program_id(0) == 0)
        def _():
            small_ref[...] = jnp.zeros_like(small_ref)

        dh1 = None
        for c in range(D_IN // D):
            cs = slice(c * D, (c + 1) * D)
            part = _dot_nt(dp_ref[:, cs], w_ref[:, cs])
            dh1 = part if dh1 is None else dh1 + part
        xv = x_ref[...]
        r1 = lax.rsqrt(jnp.mean(xv * xv, axis=-1, keepdims=True) + EPS)
        xn1 = xv * r1
        gain = vec_ref[V_G1:V_G1 + 1, :] * (1.0 + mod_ref[M_SC1:M_SC1 + 1, :])
        dxn1 = dh1 * gain
        gx_ref[...] = dx2_ref[...] + r1 * (dxn1 - xn1 * jnp.mean(dxn1 * xn1, axis=-1, keepdims=True))
        small_ref[0:1, :] += jnp.sum(dh1, axis=0, keepdims=True)
        small_ref[1:2, :] += jnp.sum(dh1 * xn1, axis=0, keepdims=True)

    tok = pl.BlockSpec((tm, D), lambda i: (i, 0))
    return pl.pallas_call(
        body, name="proj_bwd", grid=(s // tm,),
        in_specs=[pl.BlockSpec((tm, D_IN), lambda i: (i, 0)), tok, tok,
                  pl.BlockSpec((8, D), lambda i: (0, 0)), pl.BlockSpec((16, D), lambda i: (0, 0)),
                  _resident((D, D_IN)), pl.BlockSpec(memory_space=pl.ANY)],
        out_specs=[tok, pl.BlockSpec((8, D), lambda i: (0, 0))],
        out_shape=[jax.ShapeDtypeStruct((s, D), F32), jax.ShapeDtypeStruct((8, D), F32)],
        compiler_params=_params(("arbitrary",)),
    )(dproj, x, dx2, modr, vecs, w_in, dep)


def _wgrad(a, b, name, square_a=False, dep=None):
    s, ka = a.shape
    n = b.shape[1]
    tka = ka if ka <= 1024 else ka // 2
    tn = n if n <= 1024 else n // 2
    ts = min(TS_WGRAD, s)
    ns = s // ts
    nc = 512
    deps = [] if dep is None else [dep]

    def body(a_ref, b_ref, *refs):
        out_ref, acc_ref = refs[-2:]
        t = pl.program_id(2)

        @pl.when(t == 0)
        def _():
            acc_ref[...] = jnp.zeros_like(acc_ref)

        av = a_ref[...]
        if square_a:
            af = av.astype(F32)
            av = (af * af).astype(BF16)
        for c in range(tn // nc):
            cs = slice(c * nc, (c + 1) * nc)
            acc_ref[:, cs] += _dot_tn(av, b_ref[:, cs])

        @pl.when(t == ns - 1)
        def _():
            out_ref[...] = acc_ref[...].astype(BF16)

    return pl.pallas_call(
        body, name=name, grid=(ka // tka, n // tn, ns),
        in_specs=[pl.BlockSpec((ts, tka), lambda i, j, t: (t, i)),
                  pl.BlockSpec((ts, tn), lambda i, j, t: (t, j))] + [pl.BlockSpec(memory_space=pl.ANY)] * len(deps),
        out_specs=pl.BlockSpec((tka, tn), lambda i, j, t: (i, j)),
        out_shape=jax.ShapeDtypeStruct((ka, n), BF16),
        scratch_shapes=[pltpu.VMEM((tka, tn), F32)],
        compiler_params=_params(("parallel", "parallel", "arbitrary")),
    )(a, b, *deps)


def _window(ref, kind, idx, size):
    start = pl.multiple_of(idx * size, size)
    if kind == 0:
        return ref.at[pl.ds(start, size)]
    if kind == 1:
        return ref.at[:, pl.ds(start, size)]
    return ref.at[:, :, pl.ds(start, size)]


def _mesh_place():
    x, y, c = lax.axis_index("x"), lax.axis_index("y"), lax.axis_index("c")
    return x, y, c, 4 * x + 2 * y + c


def _peer(x, y, c, q):
    px = 1 - x if q & 4 else x
    py = 1 - y if q & 2 else y
    pc = 1 - c if q & 1 else c
    return (px, py, pc), 4 * px + 2 * py + pc


_HBM = pl.BlockSpec(memory_space=pltpu.HBM)
_SEM = pl.BlockSpec(memory_space=pltpu.SEMAPHORE)
_EFFECT = pltpu.SideEffectType.DATAFLOW_SIDE_EFFECTING


N_NEAR = 4


def _near(x, y, c):
    out = [((x, y, 1 - c), 4 * x + 2 * y + 1 - c)]
    for j in (1, 2, 3):
        px = 1 - x if j & 2 else x
        py = 1 - y if j & 1 else y
        out.append(((px, py, c), 4 * px + 2 * py + c))
    return out


def _remote(src, dst, send_sems, recv_sems, slot, device):
    return pltpu.make_async_remote_copy(src_ref=src, dst_ref=dst, send_sem=send_sems.at[slot], recv_sem=recv_sems.at[slot],
                                        device_id=device, device_id_type=MESH)


def _split_call(name, arrays, sems_in, n_new_sems, after, emit):
    na, ns, nn = len(arrays), len(sems_in), len(n_new_sems)

    def body(*refs):
        emit(refs[:na], refs[na:na + ns], refs[na + ns + 1:na + ns + 1 + nn])
        refs[-1][...] = jnp.zeros_like(refs[-1])

    outs = pl.pallas_call(
        body, name=name,
        out_shape=(*[pltpu.SemaphoreType.DMA((m,)) for m in n_new_sems],
                   *[pltpu.HBM(a.shape, a.dtype) for a in arrays], jax.ShapeDtypeStruct((8, 128), F32)),
        in_specs=[_HBM] * na + [_SEM] * ns + [pl.BlockSpec(memory_space=pl.ANY)],
        out_specs=(*[_SEM] * nn, *[_HBM] * na, pl.BlockSpec(memory_space=pltpu.VMEM)),
        input_output_aliases={i: nn + i for i in range(na)},
        compiler_params=pltpu.CompilerParams(has_side_effects=_EFFECT),
    )(*[pltpu.with_memory_space_constraint(a, pltpu.HBM) for a in arrays], *sems_in, after)
    return list(outs[:nn]), list(outs[nn:nn + na]), outs[-1]


class _Gather:
    def __init__(self, shards, kinds, after, name):
        self.n, self.kinds, self.name = len(shards), kinds, name
        self.sizes = [s.shape[k] for s, k in zip(shards, kinds)]
        n = self.n
        lands = []
        for s, k in zip(shards, kinds):
            dims = list(s.shape)
            dims[k] *= N_DEV
            lands.append(lax.empty(tuple(dims), s.dtype))

        def emit(arr, _, new):
            x, y, c, me = _mesh_place()
            for k in range(n):
                pltpu.make_async_copy(arr[k], _window(arr[n + k], kinds[k], me, self.sizes[k]), new[2].at[k]).start()
            for k in range(n):
                mine = _window(arr[n + k], kinds[k], me, self.sizes[k])
                for j, (dev, _) in enumerate(_near(x, y, c)):
                    _remote(arr[k], mine, new[0], new[1], k * N_NEAR + j, dev).start()

        self.sems, self.arrays, self.token = _split_call(name + "_start", [*shards, *lands], [],
                                                         [n * N_NEAR, n * N_NEAR, n], after, emit)

    def forward(self, after):
        n, kinds, sizes = self.n, self.kinds, self.sizes

        def emit(arr, old, new):
            x, y, c, _ = _mesh_place()
            near = _near(x, y, c)
            for k in range(n):
                for j in (1, 2, 3):
                    dev, idx = near[j]
                    landed = _window(arr[n + k], kinds[k], idx, sizes[k])
                    _remote(arr[k], landed, old[0], old[1], k * N_NEAR + j, dev).wait_recv()
                    _remote(landed, landed, new[0], new[1], k * N_NEAR + j, near[0][0]).start()

        new, self.arrays, self.token = _split_call(self.name + "_forward", self.arrays, self.sems, [n * N_NEAR] * 2,
                                                   after, emit)
        self.sems = [*self.sems, *new]

    def finish(self, after):
        n, kinds, sizes = self.n, self.kinds, self.sizes

        def emit(arr, old, _):
            x, y, c, me = _mesh_place()
            near = _near(x, y, c)
            other_core = near[0][0]
            for k in range(n):
                win = lambda idx: _window(arr[n + k], kinds[k], idx, sizes[k])
                pltpu.make_async_copy(arr[k], win(me), old[2].at[k]).wait()
                for j, (dev, idx) in enumerate(near):
                    _remote(arr[k], win(me), old[0], old[1], k * N_NEAR + j, dev).wait_send()
                _remote(arr[k], win(near[0][1]), old[0], old[1], k * N_NEAR, other_core).wait_recv()
                for j in (1, 2, 3):
                    idx = near[j][1]
                    _remote(win(idx), win(idx), old[3], old[4], k * N_NEAR + j, other_core).wait_send()
                    _remote(arr[k], win(idx + 1 - 2 * c), old[3], old[4], k * N_NEAR + j, other_core).wait_recv()

        _, arrays, _ = _split_call(self.name + "_finish", self.arrays, self.sems, [], after, emit)
        return arrays[n:]


class _Spread:
    def __init__(self, shards, kinds, after, name):
        self.n, self.kinds, self.name = len(shards), kinds, name
        self.sizes = [s.shape[k] for s, k in zip(shards, kinds)]
        n = self.n
        lands = []
        for s, k in zip(shards, kinds):
            dims = list(s.shape)
            dims[k] *= N_DEV
            lands.append(lax.empty(tuple(dims), s.dtype))

        def emit(arr, _, new):
            x, y, c, me = _mesh_place()
            for k in range(n):
                mine = _window(arr[n + k], kinds[k], me, self.sizes[k])
                pltpu.make_async_copy(arr[k], mine, new[2].at[k]).start()
                for q in range(1, N_DEV):
                    _remote(arr[k], mine, new[0], new[1], k * N_DEV + q, _peer(x, y, c, q)[0]).start()

        self.sems, self.arrays, self.token = _split_call(name + "_start", [*shards, *lands], [],
                                                         [n * N_DEV, n * N_DEV, n], after, emit)

    def finish(self, after):
        n, kinds, sizes = self.n, self.kinds, self.sizes

        def emit(arr, old, _):
            x, y, c, me = _mesh_place()
            for k in range(n):
                win = lambda idx: _window(arr[n + k], kinds[k], idx, sizes[k])
                pltpu.make_async_copy(arr[k], win(me), old[2].at[k]).wait()
                for q in range(1, N_DEV):
                    peer, peer_idx = _peer(x, y, c, q)
                    _remote(arr[k], win(me), old[0], old[1], k * N_DEV + q, peer).wait_send()
                    _remote(arr[k], win(peer_idx), old[0], old[1], k * N_DEV + q, peer).wait_recv()

        _, arrays, _ = _split_call(self.name + "_finish", self.arrays, self.sems, [], after, emit)
        return arrays[n:]


class _Scatter:
    def __init__(self, partials, kinds, after, name):
        self.n, self.kinds, self.name, self.partials = len(partials), kinds, name, partials
        self.sizes = [p.shape[k] // N_DEV for p, k in zip(partials, kinds)]
        n, sizes = self.n, self.sizes
        self.slot_shapes = []
        for p, k, size in zip(partials, kinds, sizes):
            dims = list(p.shape)
            dims[k] = size
            self.slot_shapes.append((N_NEAR, *dims))
        slots = [lax.empty(sh, p.dtype) for sh, p in zip(self.slot_shapes, partials)]

        def emit(arr, _, new):
            x, y, c, _ = _mesh_place()
            near = _near(x, y, c)
            for k in range(n):
                for j in range(N_NEAR):
                    owner = near[j][1] if j == 0 else near[j][1] + 1 - 2 * c
                    _remote(_window(arr[k], kinds[k], owner, sizes[k]), arr[n + k].at[j], new[0], new[1],
                            k * N_NEAR + j, near[0][0]).start()

        self.sems, self.arrays, self.token = _split_call(name + "_start", [*partials, *slots], [], [n * N_NEAR] * 2,
                                                         after, emit)

    def combine_and_send(self, own4, after):
        n, kinds, sizes = self.n, self.kinds, self.sizes

        def emit_wait(arr, old, _):
            x, y, c, _ = _mesh_place()
            near = _near(x, y, c)
            for k in range(n):
                for j in range(N_NEAR):
                    owner = near[j][1] if j == 0 else near[j][1] + 1 - 2 * c
                    cp = _remote(_window(arr[k], kinds[k], owner, sizes[k]), arr[n + k].at[j], old[0], old[1],
                                 k * N_NEAR + j, near[0][0])
                    cp.wait_send()
                    cp.wait_recv()

        _, arrays, _ = _split_call(self.name + "_landed", self.arrays, self.sems, [], after, emit_wait)
        chip_sums = _chip_sums(arrays[:n], arrays[n:], kinds, sizes, own4, self.name + "_combine")
        arrivals = [lax.empty((N_NEAR - 1, *sh[1:]), p.dtype) for sh, p in zip(self.slot_shapes, self.partials)]

        def emit_send(arr, _, new):
            x, y, c, _ = _mesh_place()
            near = _near(x, y, c)
            for k in range(n):
                for j in (1, 2, 3):
                    _remote(arr[k].at[j], arr[n + k].at[j - 1], new[0], new[1], k * N_NEAR + j, near[j][0]).start()

        self.sems, self.arrays, self.token = _split_call(self.name + "_send", [*chip_sums, *arrivals], [],
                                                         [n * N_NEAR] * 2, own4, emit_send)

    def finish(self, after):
        n = self.n

        def emit(arr, old, _):
            x, y, c, _ = _mesh_place()
            near = _near(x, y, c)
            for k in range(n):
                for j in (1, 2, 3):
                    cp = _remote(arr[k].at[j], arr[n + k].at[j - 1], old[0], old[1], k * N_NEAR + j, near[j][0])
                    cp.wait_send()
                    cp.wait_recv()

        _, arrays, _ = _split_call(self.name + "_finish", self.arrays, self.sems, [], after, emit)
        return arrays[:n], arrays[n:]


def _chip_sums(partials, slots, kinds, sizes, own4, name):
    n = len(partials)

    def body(own_ref, *refs):
        for k in range(n):
            refs[2 * n + k][...] = (refs[k][...].astype(F32) + refs[n + k][...].astype(F32)).astype(BF16)

    in_specs, slot_specs = [], []
    for p, s, kind, size in zip(partials, slots, kinds, sizes):
        block = list(p.shape)
        block[kind] = size
        nd = len(block)
        in_specs.append(pl.BlockSpec(tuple(block), functools.partial(
            lambda j, own, kind, nd: tuple(own[j] if d == kind else 0 for d in range(nd)), kind=kind, nd=nd)))
        slot_specs.append(pl.BlockSpec((None, *block), functools.partial(
            lambda j, own, nd: (j,) + (0,) * nd, nd=nd)))
    return pl.pallas_call(
        body, name=name,
        grid_spec=pltpu.PrefetchScalarGridSpec(num_scalar_prefetch=1, grid=(N_NEAR,),
                                               in_specs=in_specs + slot_specs, out_specs=slot_specs),
        out_shape=[jax.ShapeDtypeStruct(s.shape, s.dtype) for s in slots],
        compiler_params=_params(("arbitrary",)),
    )(own4, *partials, *slots)


def _to_bf16(arrays, name, dep=None):
    n = len(arrays)
    deps = [] if dep is None else [dep]

    def body(*refs):
        for src, dst in zip(refs[:n], refs[n + len(deps):]):
            dst[...] = src[...].astype(BF16)

    vmem = pl.BlockSpec(memory_space=pltpu.VMEM)
    return pl.pallas_call(body, name=name, out_shape=[jax.ShapeDtypeStruct(a.shape, BF16) for a in arrays],
                          in_specs=[vmem] * n + [pl.BlockSpec(memory_space=pl.ANY)] * len(deps), out_specs=[vmem] * n,
                          compiler_params=pltpu.CompilerParams(vmem_limit_bytes=V7X_VMEM_LIMIT))(*arrays, *deps)


def _silu(c):
    return c * _sigmoid_tail(c)


def _ada_fwd(c_all, w_ada, b_ada_cols, dep):
    def body(c_ref, w_ref, b_ref, dep_ref, out_ref):
        out_ref[...] = jnp.dot(_silu(c_ref[...]), w_ref[...], preferred_element_type=F32,
                               precision=lax.Precision.HIGHEST) + b_ref[...]

    vmem = pl.BlockSpec(memory_space=pltpu.VMEM)
    return pl.pallas_call(
        body, name="ada_fwd", in_specs=[vmem, vmem, vmem, pl.BlockSpec(memory_space=pl.ANY)], out_specs=vmem,
        out_shape=jax.ShapeDtypeStruct((N_DEV, w_ada.shape[1]), F32),
    )(c_all, w_ada, b_ada_cols, dep)


def _adam(w, g, m, v):
    m = ADAM_B1 * m + (1.0 - ADAM_B1) * g
    v = ADAM_B2 * v + (1.0 - ADAM_B2) * (g * g)
    m_hat = m / (1.0 - ADAM_B1 ** ADAM_STEP)
    v_hat = v / (1.0 - ADAM_B2 ** ADAM_STEP)
    delta = -ADAM_LR * (m_hat / (jnp.sqrt(v_hat) + ADAM_EPS) + ADAM_WD * w)
    return delta, m, v


def _ada_bwd_adam(c_all, dmod_cols, w, m, v):
    def body(c_ref, d_ref, w_ref, m_ref, v_ref, g_ref, delta_ref, nm_ref, nv_ref):
        g = lax.dot_general(_silu(c_ref[...]), d_ref[...], (((0,), (0,)), ((), ())),
                            preferred_element_type=F32, precision=lax.Precision.HIGHEST)
        g_ref[...] = g
        delta_ref[...], nm_ref[...], nv_ref[...] = _adam(w_ref[...], g, m_ref[...], v_ref[...])

    sd = jax.ShapeDtypeStruct(w.shape, F32)
    return pl.pallas_call(body, name="ada_bwd_adam", out_shape=[sd] * 4,
                          compiler_params=pltpu.CompilerParams(vmem_limit_bytes=V7X_VMEM_LIMIT),
                          )(c_all, dmod_cols, w, m, v)


def _adam_group(chip_sums, arrivals, ws, ms, vs, n_tiles, name):
    n = len(ws)

    def body(*refs):
        for k in range(n):
            c_ref, a_ref, w_ref, m_ref, v_ref = (refs[j * n + k] for j in range(5))
            g_ref, delta_ref, nm_ref, nv_ref = (refs[(5 + j) * n + k] for j in range(4))
            g = c_ref[...].astype(F32)
            for j in range(N_NEAR - 1):
                g = g + a_ref[j].astype(F32)
            g_ref[...] = g
            delta_ref[...], nm_ref[...], nv_ref[...] = _adam(w_ref[...], g, m_ref[...], v_ref[...])

    tiles = [(w.shape[0] // n_tiles, w.shape[1]) for w in ws]
    blk = [pl.BlockSpec(t, lambda i: (i, 0)) for t in tiles]
    return pl.pallas_call(
        body, name=name, grid=(n_tiles,),
        in_specs=[pl.BlockSpec((None, *t), lambda i: (0, i, 0)) for t in tiles]
        + [pl.BlockSpec((N_NEAR - 1, *t), lambda i: (0, i, 0)) for t in tiles] + blk * 3,
        out_specs=blk * 4, out_shape=[jax.ShapeDtypeStruct(w.shape, F32) for w in ws] * 4,
        compiler_params=_params(("parallel",)),
    )(*chip_sums, *arrivals, *ws, *ms, *vs)


N_SMALL = 40
N_SMALL_PARAMS = 11


def _pack_vecs(conv_w_full, rows):
    def body(cw_ref, *refs):
        out = refs[-1]
        out[...] = jnp.zeros_like(out)
        out[0:4, :] = cw_ref[0:4, :]
        for r, ref in enumerate(refs[:-1]):
            out[4 + r:5 + r, :] = ref[...]

    return pl.pallas_call(body, name="pack_vecs", out_shape=jax.ShapeDtypeStruct((16, D), F32))(conv_w_full, *rows)


def _small_finish(gathered, mod_all, vecs, ws, ms, vs):
    n = N_SMALL_PARAMS

    def body(g_ref, mod_ref, vec_ref, *refs):
        w_refs, m_refs, v_refs = refs[:n], refs[n:2 * n], refs[2 * n:3 * n]
        outs = refs[3 * n:]
        g1 = vec_ref[V_G1:V_G1 + 1, :]
        g2 = vec_ref[V_G2:V_G2 + 1, :]
        zero = jnp.zeros((1, D), F32)
        dg1, dg2, dgf, loss_lanes = zero, zero, zero, zero
        mixer = jnp.zeros((16, D), F32)
        db_ada = jnp.zeros((6, D), F32)
        for b in range(N_DEV):
            gb = g_ref[b]
            mod = mod_ref[b]
            q1 = gb[33:34]
            q2 = gb[9:10]
            dmod = jnp.concatenate([gb[32:33], q1 * g1, gb[10:11], gb[8:9], q2 * g2, gb[1:2]], axis=0)
            outs[4 * n][b] = dmod
            db_ada = db_ada + dmod
            dg1 = dg1 + q1 * (1.0 + mod[M_SC1:M_SC1 + 1])
            dg2 = dg2 + q2 * (1.0 + mod[M_SC2:M_SC2 + 1])
            dgf = dgf + gb[0:1]
            loss_lanes = loss_lanes + gb[2:3]
            mixer = mixer + gb[16:32]
        d_a_param = mixer[7:8] * _sigmoid_tail(vec_ref[V_A_PARAM:V_A_PARAM + 1, :])
        grads = [dg1, dg2, mixer[4:5], mixer[5:6], mixer[6:7], d_a_param, mixer[8:9], mixer[9:10], dgf,
                 db_ada, mixer[0:4]]
        def load(ref, rows):
            if ref.shape[0] == rows:
                return ref[...]
            return jnp.concatenate([ref[:, j * D:(j + 1) * D] for j in range(rows)], axis=0)

        def store(ref, val):
            if ref.shape == val.shape:
                ref[...] = val
            else:
                for j in range(val.shape[0]):
                    ref[:, j * D:(j + 1) * D] = val[j:j + 1]

        for k in range(n):
            rows = grads[k].shape[0]
            results = (grads[k], *_adam(load(w_refs[k], rows), grads[k], load(m_refs[k], rows), load(v_refs[k], rows)))
            for which, val in enumerate(results):
                store(outs[which * n + k], val)
        outs[4 * n + 1][...] = jnp.broadcast_to(jnp.sum(loss_lanes, axis=1, keepdims=True), (8, 128))

    shapes = [jax.ShapeDtypeStruct(w.shape, F32) for w in ws]
    return pl.pallas_call(
        body, name="small_finish",
        out_shape=shapes * 4 + [jax.ShapeDtypeStruct((N_DEV, 6, D), F32), jax.ShapeDtypeStruct((8, 128), F32)],
    )(gathered, mod_all, vecs, *ws, *ms, *vs)


def _pad_rows(a, rows):
    return jnp.pad(a, ((0, rows - a.shape[0]), (0, 0)))


def kernel(x, c, norm_mix_g, norm_mlp_g, w_ada, b_ada, w_in, conv_w, conv_b, w_rg_a, b_rg_a, w_rg_x, b_rg_x, a_param, w_branch_a, w_pool, b_pool, pool_scale, w_branch_b, w_out, w_up, w_down, final_g, loss_target, m_norm_mix_g, m_norm_mlp_g, m_w_ada, m_b_ada, m_w_in, m_conv_w, m_conv_b, m_w_rg_a, m_b_rg_a, m_w_rg_x, m_b_rg_x, m_a_param, m_w_branch_a, m_w_pool, m_b_pool, m_pool_scale, m_w_branch_b, m_w_out, m_w_up, m_w_down, m_final_g, v_norm_mix_g, v_norm_mlp_g, v_w_ada, v_b_ada, v_w_in, v_conv_w, v_conv_b, v_w_rg_a, v_b_rg_a, v_w_rg_x, v_b_rg_x, v_a_param, v_w_branch_a, v_w_pool, v_b_pool, v_pool_scale, v_w_branch_b, v_w_out, v_w_up, v_w_down, v_final_g):
    me = 4 * lax.axis_index("x") + 2 * lax.axis_index("y") + lax.axis_index("c")
    s = x.shape[1]
    x2d = x.reshape(s, D)
    target = loss_target.reshape(s, D)
    n_ada = w_ada.shape[2]

    b_ada_cols = lax.dynamic_slice(b_ada, (0, me * n_ada), (1, n_ada))
    spread_c = _Spread([_pad_rows(c, 8), _pad_rows(conv_w[0], 8)], [0, 1], c, "spread_c")

    sharded = dict(w_in=(w_in[0], 1), w_up=(w_up[0], 1), w_down=(w_down[0], 0), w_branch_a=(w_branch_a[0], 0),
                   w_branch_b=(w_branch_b[0], 0), w_out=(w_out[0], 0), w_rg_a=(w_rg_a[0], 1), w_rg_x=(w_rg_x[0], 1),
                   w_pool=(w_pool[0], 1))
    kind = {k: v[1] for k, v in sharded.items()}
    first_names = ["w_in"]
    later_names = [k for k in sharded if k not in first_names]
    mix_names = ["w_rg_a", "w_rg_x", "w_pool"]
    branch_names = ["w_branch_a", "w_branch_b", "w_out"]
    mlp_names = ["w_up", "w_down"]

    def gather(group, after, name):
        return _Gather([shard[k] for k in group], [kind[k] for k in group], after, name)

    shard = dict(zip(first_names, _to_bf16([sharded[k][0] for k in first_names], "to_bf16_first")))
    g_first = gather(first_names, spread_c.token, "gather_first")
    shard.update(zip(later_names, _to_bf16([sharded[k][0] for k in later_names], "to_bf16_later", dep=g_first.token)))

    c_rows, conv_w_full = spread_c.finish(g_first.token)
    c_all = c_rows.reshape(N_DEV, 8, D)[:, 0, :]
    mod_part = _ada_fwd(c_all, w_ada[0], b_ada_cols, g_first.token)
    spread_mod = _Spread([mod_part], [0], g_first.token, "spread_mod")
    spread_mix = _Spread([shard[k] for k in mix_names], [kind[k] for k in mix_names], spread_mod.token, "spread_mix")
    g_branch = gather(branch_names, spread_mix.token, "gather_branch")
    g_mlp = gather(mlp_names, g_branch.token, "gather_mlp")

    vecs = _pack_vecs(conv_w_full, [conv_b, b_rg_a, b_rg_x, a_param, b_pool, pool_scale,
                                    norm_mix_g, norm_mlp_g, final_g.reshape(1, D)])
    g_first.forward(g_mlp.token)
    wg = dict(zip(first_names, g_first.finish(g_first.token)))
    mod_parts, = spread_mod.finish(g_first.token)
    mod_all = jnp.transpose(mod_parts.reshape(N_DEV, N_DEV, n_ada), (1, 0, 2)).reshape(N_DEV, 6, D)
    mod_all = jnp.pad(mod_all, ((0, 0), (0, 2), (0, 0)))
    modr = lax.dynamic_index_in_dim(mod_all, me, 0, keepdims=False)

    h1, x_rnn, u_pool, ga, dga, sa, sb = _proj_fwd(x2d, modr, vecs, wg["w_in"])
    g_branch.forward(h1)
    wg.update(zip(mix_names, spread_mix.finish(g_branch.token)))
    xr, hr, za, p, pooled, *gates = _mix_fwd(x_rnn, u_pool, ga, vecs, wg["w_rg_a"], wg["w_rg_x"], wg["w_pool"],
                                             dep=g_branch.token)
    g_mlp.forward(za)
    wg.update(zip(branch_names, g_branch.finish(g_mlp.token)))
    ba, bb, merged, o, x2, h2 = _branch_fwd(za, pooled, sa, sb, x2d, modr, vecs,
                                            wg["w_branch_a"], wg["w_branch_b"], wg["w_out"])
    wg.update(zip(mlp_names, g_mlp.finish(h2)))
    ru, dx3, d_dn, small_f = _mlp_fwd(h2, x2, target, modr, vecs, wg["w_up"], wg["w_down"])

    near = _near(lax.axis_index("x"), lax.axis_index("y"), lax.axis_index("c"))
    own4 = jnp.stack([me, near[1][1], near[2][1], near[3][1]]).astype(jnp.int32)

    def scatter(group, partial, after, name):
        return _Scatter([partial[k] for k in group], [kind[k] for k in group], after, name)

    dup, dx2, do, small_m = _mlp_bwd(d_dn, ru, x2, dx3, o, modr, vecs, wg["w_up"], wg["w_down"])
    partial = dict(w_up=_wgrad(h2, dup, "wgrad_up"), w_down=_wgrad(ru, d_dn, "wgrad_down", square_a=True))
    s_mlp = scatter(mlp_names, partial, dx2, "scatter_mlp")

    dba, dbb, dgates, dza, dpooled = _branch_bwd(do, sa, sb, ba, bb, wg["w_branch_a"], wg["w_branch_b"], wg["w_out"],
                                                 dep=s_mlp.token)
    s_mlp.combine_and_send(own4, dza)
    dproj, dw_rg_a, dw_rg_x, dw_pool, small_x = _mix_bwd(dza, dpooled, x_rnn, ga, dga, xr, hr, p, gates, dgates,
                                                         vecs, wg["w_rg_a"], wg["w_rg_x"], wg["w_pool"],
                                                         dep=s_mlp.token)
    partial.update(w_branch_a=_wgrad(za, dba, "wgrad_branch_a"), w_branch_b=_wgrad(pooled, dbb, "wgrad_branch_b"),
                   w_out=_wgrad(merged, do, "wgrad_out"),
                   w_rg_a=dw_rg_a, w_rg_x=dw_rg_x, w_pool=dw_pool)
    mixer_names = ["w_rg_a", "w_rg_x", "w_pool", "w_branch_a", "w_branch_b", "w_out"]
    s_mixer = scatter(mixer_names, partial, s_mlp.token, "scatter_mixer")

    partial["w_in"] = _wgrad(h1, dproj, "wgrad_in", dep=s_mixer.token)
    s_in = scatter(["w_in"], partial, s_mixer.token, "scatter_in")
    s_mixer.combine_and_send(own4, s_in.token)
    s_in.combine_and_send(own4, s_mixer.token)
    grad_x, small_p = _proj_bwd(dproj, x2d, dx2, modr, vecs, wg["w_in"], dep=s_in.token)

    locals_ = dict(w_in=(w_in, m_w_in, v_w_in), w_up=(w_up, m_w_up, v_w_up), w_down=(w_down, m_w_down, v_w_down),
                   w_branch_a=(w_branch_a, m_w_branch_a, v_w_branch_a),
                   w_branch_b=(w_branch_b, m_w_branch_b, v_w_branch_b), w_out=(w_out, m_w_out, v_w_out),
                   w_rg_a=(w_rg_a, m_w_rg_a, v_w_rg_a), w_rg_x=(w_rg_x, m_w_rg_x, v_w_rg_x),
                   w_pool=(w_pool, m_w_pool, v_w_pool))
    res = {}

    def finish(group, exchange, after, n_tiles, name):
        chip_sums, arrivals = exchange.finish(after)
        flat = lambda t: t.reshape(-1, t.shape[-1])
        shapes = [flat(locals_[k][0]).shape for k in group]
        outs = _adam_group([cs.reshape(N_NEAR, *sh) for cs, sh in zip(chip_sums, shapes)],
                           [ar.reshape(N_NEAR - 1, *sh) for ar, sh in zip(arrivals, shapes)],
                           *[[flat(locals_[k][j]) for k in group] for j in range(3)], n_tiles, name)
        for i, k in enumerate(group):
            res[k] = [outs[j * len(group) + i].reshape(locals_[k][0].shape) for j in range(4)]
        return res[group[-1]][0]

    small = jnp.concatenate([small_f, small_m, small_x, small_p], axis=0)
    g_small = _Gather([small], [0], grad_x, "gather_small")
    done = finish(mlp_names, s_mlp, g_small.token, 4, "adam_mlp")
    done = finish(mixer_names, s_mixer, done, 2, "adam_mixer")
    g_small.forward(done)
    done = finish(["w_in"], s_in, g_small.token, 4, "adam_in")
    small_all, = g_small.finish(done)
    small_all = small_all.reshape(N_DEV, N_SMALL, D)

    def embed(cw):
        return lax.dynamic_update_slice(jnp.zeros((4, D), F32), cw[0], (0, me * (D // N_DEV)))

    def smalls(ng, nl, cb, bra, brx, ap, bp, ps, fg, ba_, cw):
        return [ng, nl, cb, bra, brx, ap, bp, ps, fg.reshape(1, D), ba_, embed(cw)]

    small_names = ["norm_mix_g", "norm_mlp_g", "conv_b", "b_rg_a", "b_rg_x", "a_param", "b_pool", "pool_scale",
                   "final_g", "b_ada", "conv_w"]
    fin = _small_finish(
        small_all, mod_all, vecs,
        smalls(norm_mix_g, norm_mlp_g, conv_b, b_rg_a, b_rg_x, a_param, b_pool, pool_scale, final_g, b_ada, conv_w),
        smalls(m_norm_mix_g, m_norm_mlp_g, m_conv_b, m_b_rg_a, m_b_rg_x, m_a_param, m_b_pool, m_pool_scale,
               m_final_g, m_b_ada, m_conv_w),
        smalls(v_norm_mix_g, v_norm_mlp_g, v_conv_b, v_b_rg_a, v_b_rg_x, v_a_param, v_b_pool, v_pool_scale,
               v_final_g, v_b_ada, v_conv_w))
    dmod_all, loss_tile = fin[4 * N_SMALL_PARAMS], fin[4 * N_SMALL_PARAMS + 1]
    dmod_cols = lax.dynamic_slice(dmod_all.reshape(N_DEV, 6 * D), (0, me * n_ada), (N_DEV, n_ada))
    res["w_ada"] = [t.reshape(w_ada.shape) for t in _ada_bwd_adam(c_all, dmod_cols, w_ada[0], m_w_ada[0], v_w_ada[0])]

    def final_shape(k, t):
        if k == "final_g":
            return t.reshape(D)
        if k == "conv_w":
            return lax.dynamic_slice(t, (0, me * (D // N_DEV)), (4, D // N_DEV)).reshape(conv_w.shape)
        return t

    for i, k in enumerate(small_names):
        res[k] = [final_shape(k, fin[which * N_SMALL_PARAMS + i]) for which in range(4)]
    order = ["norm_mix_g", "norm_mlp_g", "w_ada", "b_ada", "w_in", "conv_w", "conv_b", "w_rg_a", "b_rg_a", "w_rg_x",
             "b_rg_x", "a_param", "w_branch_a", "w_pool", "b_pool", "pool_scale", "w_branch_b", "w_out", "w_up",
             "w_down", "final_g"]
    outs = [loss_tile[0, 0], grad_x.reshape(x.shape)]
    for which in range(4):
        for k in order:
            outs.append(res[k][which])
    return tuple(outs)
```

```python
import functools

import jax
import jax.numpy as jnp
from jax import lax
from jax.experimental import pallas as pl
from jax.experimental.pallas import tpu as pltpu

F32 = jnp.float32
BF16 = jnp.bfloat16
MESH = pl.DeviceIdType.MESH

N_DEV = 8
D = 1024
N_GROUPS = 4
GW = D // N_GROUPS
D_IN = 5 * D
D_FF = 4 * D
POOL_WINDOWS = (2, 4, 8, 16)
HALO_X = 8
HALO_U = 16
EPS = 1e-6
C_RG = 8.0
ADAM_LR, ADAM_B1, ADAM_B2, ADAM_EPS, ADAM_WD, ADAM_STEP = 0.001, 0.9, 0.999, 1e-08, 0.01, 10

V7X_VMEM_LIMIT = 56 * 1024 * 1024

V_CONV_W, V_CONV_B, V_B_RG_A, V_B_RG_X, V_A_PARAM, V_B_POOL, V_POOL_SCALE, V_G1, V_G2, V_GF = 0, 4, 5, 6, 7, 8, 9, 10, 11, 12
M_SH1, M_SC1, M_GT1, M_SH2, M_SC2, M_GT2 = 0, 1, 2, 3, 4, 5

TM_PROJ = 512
TM_MIX = 256
TM_BRANCH = 512
TM_MLP = 512
TM_MLP_BWD = 256
TS_WGRAD = 1024


def _params(semantics):
    return pltpu.CompilerParams(dimension_semantics=semantics, vmem_limit_bytes=V7X_VMEM_LIMIT)


def _resident(shape):
    return pl.BlockSpec(shape, lambda *_: (0,) * len(shape), pipeline_mode=pl.Buffered(1))


def _dot(a, b):
    return jnp.dot(a, b, preferred_element_type=F32)


def _dot_nt(a, b):
    return lax.dot_general(a, b, (((1,), (1,)), ((), ())), preferred_element_type=F32)


def _dot_tn(a, b):
    return lax.dot_general(a, b, (((0,), (0,)), ((), ())), preferred_element_type=F32)


def _sigmoid(x):
    return 0.5 * jnp.tanh(0.5 * x) + 0.5


def _sigmoid_tail(x):
    return 1.0 / (1.0 + jnp.exp(-x))


def _gelu_and_grad(x):
    k = 0.7978845608028654
    x2 = x * x
    t = jnp.tanh(k * (x + 0.044715 * x * x2))
    g = 0.5 * x * (1.0 + t)
    dg = 0.5 * (1.0 + t) + 0.5 * x * (1.0 - t * t) * (k * (1.0 + 3.0 * 0.044715 * x2))
    return g, dg


def _softplus(a):
    e = jnp.exp(-jnp.abs(a))
    u = 1.0 + e
    log1p_e = jnp.where(u == 1.0, e, jnp.log(u) * e / jnp.where(u == 1.0, 1.0, u - 1.0))
    return jnp.maximum(a, 0.0) + log1p_e


def _neg_expm1(z):
    series = -(z * (1.0 + z * (0.5 + z * (1.0 / 6.0 + z * (1.0 / 24.0 + z * (1.0 / 120.0))))))
    return jnp.where(z > -0.1, series, 1.0 - jnp.exp(z))


def _shift_down(x, k):
    return pltpu.roll(x, k, 0)


def _shift_up(x, k):
    return pltpu.roll(x, x.shape[0] - k, 0)


def _rglru_gates(xr, w_a, w_x, b_a, b_x, a_param, is_t0):
    xb = xr.astype(BF16)
    ra = _sigmoid(_dot(xb, w_a) + b_a)
    ri = _sigmoid(_dot(xb, w_x) + b_x)
    sp = _softplus(a_param)
    log_a = (-C_RG) * ra * sp
    a = jnp.exp(log_a)
    mult = jnp.where(is_t0, 1.0, jnp.sqrt(_neg_expm1(2.0 * log_a)))
    return ra, ri, sp, a, mult


SUBLANES = 8


LANES = 128


def _scan_strip(a, b, carry, scr, down):
    t = b.shape[0]
    g = t // SUBLANES
    a3 = a.reshape(g, SUBLANES, LANES)
    b3 = b.reshape(g, SUBLANES, LANES)
    sub = lax.broadcasted_iota(jnp.int32, (g, SUBLANES, LANES), 1)
    for k in (1, 2, 4):
        keep = sub >= k if down else sub < SUBLANES - k
        shift = k if down else SUBLANES - k
        b3 = b3 + a3 * jnp.where(keep, pltpu.roll(b3, shift, 1), 0.0)
        a3 = a3 * jnp.where(keep, pltpu.roll(a3, shift, 1), 1.0)
    scr[0] = a3.reshape(t, LANES)
    scr[1] = b3.reshape(t, LANES)
    end_row = SUBLANES - 1 if down else 0
    ag = scr[0, pl.ds(end_row, g, stride=SUBLANES), :]
    bg = scr[1, pl.ds(end_row, g, stride=SUBLANES), :]
    rg = lax.broadcasted_iota(jnp.int32, (g, LANES), 0)
    edge = 0 if down else g - 1
    bg = bg + jnp.where(rg == edge, ag * carry, 0.0)
    k = 1
    while k < g:
        keep = rg >= k if down else rg < g - k
        shift = k if down else g - k
        bg = bg + ag * jnp.where(keep, pltpu.roll(bg, shift, 0), 0.0)
        if 2 * k < g:
            ag = ag * pltpu.roll(ag, shift, 0)
        k *= 2
    entering = jnp.where(rg != edge, pltpu.roll(bg, 1 if down else g - 1, 0), carry)
    for r in range(SUBLANES):
        scr[2, pl.ds(r, g, stride=SUBLANES), :] = entering
    return scr[1] + scr[0] * scr[2], bg[g - 1:g, :]


def _scan_strips(a, b, carry, scr, down):
    outs = [_scan_strip(a[:, c:c + LANES], b[:, c:c + LANES], carry[:, c:c + LANES], scr, down)
            for c in range(0, b.shape[1], LANES)]
    return jnp.concatenate([o[0] for o in outs], axis=1), jnp.concatenate([o[1] for o in outs], axis=1)


def _scan_down(a, b, carry, scr):
    return _scan_strips(a, b, carry, scr, True)


def _scan_up(m, b, carry, scr):
    return _scan_strips(m, b, carry, scr, False)[0]


def _window_mean(sums, window, first_block, head_t):
    scaled = sums * (1.0 / window)
    head = jnp.where(first_block, sums[:HALO_U] / jnp.minimum(head_t, float(window)), scaled[:HALO_U])
    return jnp.concatenate([head, scaled[HALO_U:]], axis=0)


def _conv_taps(x_ext):
    return [_shift_down(x_ext, 3 - j)[HALO_X:] if j < 3 else x_ext[HALO_X:] for j in range(4)]


def _proj_fwd(x, modr, vecs, w_in):
    s = x.shape[0]
    tm = min(TM_PROJ, s)

    def body(x_ref, mod_ref, vec_ref, w_ref, h1_ref, xrnn_ref, u_ref, ga_ref, dga_ref, sa_ref, sb_ref):
        xv = x_ref[...]
        r = lax.rsqrt(jnp.mean(xv * xv, axis=-1, keepdims=True) + EPS)
        gain = vec_ref[V_G1:V_G1 + 1, :] * (1.0 + mod_ref[M_SC1:M_SC1 + 1, :])
        h = (xv * r * gain + mod_ref[M_SH1:M_SH1 + 1, :]).astype(BF16)
        h1_ref[...] = h
        xrnn_ref[...] = _dot(h, w_ref[:, 0:D])
        ga_ref[...], dga_ref[...] = _gelu_and_grad(_dot(h, w_ref[:, D:2 * D]))
        u_ref[...] = _dot(h, w_ref[:, 2 * D:3 * D])
        sa_ref[...] = _sigmoid(_dot(h, w_ref[:, 3 * D:4 * D]))
        sb_ref[...] = _sigmoid(_dot(h, w_ref[:, 4 * D:5 * D]))

    tok = pl.BlockSpec((tm, D), lambda i: (i, 0))
    sd = lambda dt: jax.ShapeDtypeStruct((s, D), dt)
    return pl.pallas_call(
        body, name="proj_fwd", grid=(s // tm,),
        in_specs=[tok, pl.BlockSpec((8, D), lambda i: (0, 0)), pl.BlockSpec((16, D), lambda i: (0, 0)),
                  _resident((D, D_IN))],
        out_specs=[tok] * 7,
        out_shape=[sd(BF16)] + [sd(F32)] * 6,
        compiler_params=_params(("parallel",)),
    )(x, modr, vecs, w_in)


def _mix_fwd(x_rnn, u_pool, ga, vecs, w_rg_a, w_rg_x, w_pool, dep):
    s = x_rnn.shape[0]
    tm = min(TM_MIX, s)
    nb = s // tm

    def body(xh_ref, x_ref, uh_ref, u_ref, ga_ref, vec_ref, wa_ref, wx_ref, wp_ref, dep_ref,
             xr_ref, hr_ref, za_ref, p_ref, pooled_ref, a_ref, mult_ref, ra_ref, ri_ref, carry_ref, scan_scr):
        i = pl.program_id(0)
        first = i == 0

        @pl.when(first)
        def _():
            carry_ref[...] = jnp.zeros_like(carry_ref)

        row = lax.broadcasted_iota(jnp.int32, (tm, GW), 0)
        is_t0 = jnp.logical_and(first, row == 0)
        head_t = (lax.broadcasted_iota(jnp.int32, (HALO_U, GW), 0) + 1).astype(F32)
        for g in range(N_GROUPS):
            cs = slice(g * GW, (g + 1) * GW)
            vec = vec_ref[:, cs]
            xh = jnp.where(first, 0.0, xh_ref[:, cs])
            taps = _conv_taps(jnp.concatenate([xh, x_ref[:, cs]], axis=0))
            xr = vec[V_CONV_B:V_CONV_B + 1]
            for j in range(4):
                xr = xr + vec[V_CONV_W + j:V_CONV_W + j + 1] * taps[j]
            xr_ref[:, cs] = xr
            ra, ri, _, a, mult = _rglru_gates(
                xr, wa_ref[g], wx_ref[g], vec[V_B_RG_A:V_B_RG_A + 1], vec[V_B_RG_X:V_B_RG_X + 1],
                vec[V_A_PARAM:V_A_PARAM + 1], is_t0)
            a_ref[:, cs] = a
            mult_ref[:, cs] = mult
            ra_ref[:, cs] = ra.astype(BF16)
            ri_ref[:, cs] = ri.astype(BF16)
            h, last = _scan_down(a, xr * ri * mult, carry_ref[0:1, cs], scan_scr)
            hr_ref[:, cs] = h
            carry_ref[0:1, cs] = last
            za_ref[:, cs] = (ga_ref[:, cs] * h).astype(BF16)
            uh = jnp.where(first, 0.0, uh_ref[:, cs])
            sm = jnp.concatenate([uh, u_ref[:, cs]], axis=0)
            k = 1
            while k < POOL_WINDOWS[g]:
                sm = sm + _shift_down(sm, k)
                k *= 2
            mean = _window_mean(sm[HALO_U:], POOL_WINDOWS[g], first, head_t)
            p = (mean - u_ref[:, cs]).astype(BF16)
            p_ref[:, cs] = p
            pb = _dot(p, wp_ref[g]) + vec[V_B_POOL:V_B_POOL + 1]
            pooled_ref[:, cs] = (pb * vec[V_POOL_SCALE:V_POOL_SCALE + 1]).astype(BF16)

    tok = pl.BlockSpec((tm, D), lambda i: (i, 0))
    halo = lambda rows: pl.BlockSpec((rows, D), lambda i: (jnp.maximum(i * (tm // rows) - 1, 0), 0))
    wspec = pl.BlockSpec((N_GROUPS, GW, GW), lambda i: (0, 0, 0))
    sd = lambda dt: jax.ShapeDtypeStruct((s, D), dt)
    return pl.pallas_call(
        body, name="mix_fwd", grid=(nb,),
        in_specs=[halo(HALO_X), tok, halo(HALO_U), tok, tok, pl.BlockSpec((16, D), lambda i: (0, 0)),
                  wspec, wspec, wspec, pl.BlockSpec(memory_space=pl.ANY)],
        out_specs=[tok] * 9,
        out_shape=[sd(F32), sd(F32), sd(BF16), sd(BF16), sd(BF16), sd(F32), sd(F32), sd(BF16), sd(BF16)],
        scratch_shapes=[pltpu.VMEM((8, D), F32), pltpu.VMEM((3, tm, LANES), F32)],
        compiler_params=_params(("arbitrary",)),
    )(x_rnn, x_rnn, u_pool, u_pool, ga, vecs, w_rg_a, w_rg_x, w_pool, dep)


def _branch_fwd(za, pooled, sa, sb, x, modr, vecs, w_a, w_b, w_out):
    s = x.shape[0]
    tm = min(TM_BRANCH, s)

    def body(za_ref, pooled_ref, sa_ref, sb_ref, x_ref, mod_ref, vec_ref, wa_ref, wb_ref, wo_ref,
             ba_ref, bb_ref, merged_ref, o_ref, x2_ref, h2_ref):
        ba = _dot(za_ref[...], wa_ref[...])
        bb = _dot(pooled_ref[...], wb_ref[...])
        ba_ref[...] = ba.astype(BF16)
        bb_ref[...] = bb.astype(BF16)
        merged = (sa_ref[...] * ba + sb_ref[...] * bb).astype(BF16)
        merged_ref[...] = merged
        o = _dot(merged, wo_ref[...])
        o_ref[...] = o.astype(BF16)
        x2 = x_ref[...] + mod_ref[M_GT1:M_GT1 + 1, :] * o
        x2_ref[...] = x2
        r = lax.rsqrt(jnp.mean(x2 * x2, axis=-1, keepdims=True) + EPS)
        gain = vec_ref[V_G2:V_G2 + 1, :] * (1.0 + mod_ref[M_SC2:M_SC2 + 1, :])
        h2_ref[...] = (x2 * r * gain + mod_ref[M_SH2:M_SH2 + 1, :]).astype(BF16)

    tok = pl.BlockSpec((tm, D), lambda i: (i, 0))
    wspec = pl.BlockSpec((D, D), lambda i: (0, 0))
    sd = lambda dt: jax.ShapeDtypeStruct((s, D), dt)
    return pl.pallas_call(
        body, name="branch_fwd", grid=(s // tm,),
        in_specs=[tok, tok, tok, tok,
                  tok, pl.BlockSpec((8, D), lambda i: (0, 0)), pl.BlockSpec((16, D), lambda i: (0, 0)),
                  wspec, wspec, wspec],
        out_specs=[tok] * 6,
        out_shape=[sd(BF16), sd(BF16), sd(BF16), sd(BF16), sd(F32), sd(BF16)],
        compiler_params=_params(("parallel",)),
    )(za, pooled, sa, sb, x, modr, vecs, w_a, w_b, w_out)


def _mlp_fwd(h2, x2, target, modr, vecs, w_up, w_down):
    s = x2.shape[0]
    tm = min(TM_MLP, s)

    def body(h2_ref, x2_ref, tgt_ref, mod_ref, vec_ref, wu_ref, wd_ref,
             ru_ref, dx3_ref, ddn_ref, small_ref):
        @pl.when(pl.program_id(0) == 0)
        def _():
            small_ref[...] = jnp.zeros_like(small_ref)

        h2 = h2_ref[...]
        dn = None
        for c in range(D_FF // D):
            cs = slice(c * D, (c + 1) * D)
            ru = jnp.maximum(_dot(h2, wu_ref[:, cs]), 0.0)
            ru_ref[:, cs] = ru.astype(BF16)
            part = _dot((ru * ru).astype(BF16), wd_ref[cs, :])
            dn = part if dn is None else dn + part
        gt2 = mod_ref[M_GT2:M_GT2 + 1, :]
        gf = vec_ref[V_GF:V_GF + 1, :]
        x3 = x2_ref[...] + gt2 * dn
        r3 = lax.rsqrt(jnp.mean(x3 * x3, axis=-1, keepdims=True) + EPS)
        n3 = x3 * r3
        err = n3 * gf - tgt_ref[...]
        dy = err * (1.0 / D)
        dn3 = dy * gf
        dx3 = r3 * (dn3 - n3 * jnp.mean(dn3 * n3, axis=-1, keepdims=True))
        dx3_ref[...] = dx3
        ddn_ref[...] = (dx3 * gt2).astype(BF16)
        small_ref[0:1, :] += jnp.sum(dy * n3, axis=0, keepdims=True)
        small_ref[1:2, :] += jnp.sum(dx3 * dn, axis=0, keepdims=True)
        small_ref[2:3, :] += (0.5 / D) * jnp.sum(err * err, axis=0, keepdims=True)

    tok = pl.BlockSpec((tm, D), lambda i: (i, 0))
    return pl.pallas_call(
        body, name="mlp_fwd", grid=(s // tm,),
        in_specs=[tok, tok, tok,
                  pl.BlockSpec((8, D), lambda i: (0, 0)), pl.BlockSpec((16, D), lambda i: (0, 0)),
                  _resident((D, D_FF)), _resident((D_FF, D))],
        out_specs=[pl.BlockSpec((tm, D_FF), lambda i: (i, 0)), tok, tok,
                   pl.BlockSpec((8, D), lambda i: (0, 0))],
        out_shape=[jax.ShapeDtypeStruct((s, D_FF), BF16), jax.ShapeDtypeStruct((s, D), F32),
                   jax.ShapeDtypeStruct((s, D), BF16), jax.ShapeDtypeStruct((8, D), F32)],
        compiler_params=_params(("arbitrary",)),
    )(h2, x2, target, modr, vecs, w_up, w_down)


def _mlp_bwd(d_dn, ru, x2, dx3, o, modr, vecs, w_up, w_down):
    s = x2.shape[0]
    tm = min(TM_MLP_BWD, s)

    def body(ddn_ref, ru_ref, x2_ref, dx3_ref, o_ref, mod_ref, vec_ref, wu_ref, wd_ref,
             dup_ref, dx2_ref, do_ref, small_ref):
        @pl.when(pl.program_id(0) == 0)
        def _():
            small_ref[...] = jnp.zeros_like(small_ref)

        ddn = ddn_ref[...]
        dh2 = None
        for c in range(D_FF // D):
            cs = slice(c * D, (c + 1) * D)
            dff = _dot_nt(ddn, wd_ref[cs, :])
            dup = (dff * (2.0 * ru_ref[:, cs].astype(F32))).astype(BF16)
            dup_ref[:, cs] = dup
            part = _dot_nt(dup, wu_ref[:, cs])
            dh2 = part if dh2 is None else dh2 + part
        x2 = x2_ref[...]
        r2 = lax.rsqrt(jnp.mean(x2 * x2, axis=-1, keepdims=True) + EPS)
        xn2 = x2 * r2
        gain = vec_ref[V_G2:V_G2 + 1, :] * (1.0 + mod_ref[M_SC2:M_SC2 + 1, :])
        dxn2 = dh2 * gain
        dx2 = dx3_ref[...] + r2 * (dxn2 - xn2 * jnp.mean(dxn2 * xn2, axis=-1, keepdims=True))
        dx2_ref[...] = dx2
        do_ref[...] = (dx2 * mod_ref[M_GT1:M_GT1 + 1, :]).astype(BF16)
        small_ref[0:1, :] += jnp.sum(dh2, axis=0, keepdims=True)
        small_ref[1:2, :] += jnp.sum(dh2 * xn2, axis=0, keepdims=True)
        small_ref[2:3, :] += jnp.sum(dx2 * o_ref[...].astype(F32), axis=0, keepdims=True)

    tok = pl.BlockSpec((tm, D), lambda i: (i, 0))
    wide = pl.BlockSpec((tm, D_FF), lambda i: (i, 0))
    return pl.pallas_call(
        body, name="mlp_bwd", grid=(s // tm,),
        in_specs=[tok, wide, tok, tok, tok,
                  pl.BlockSpec((8, D), lambda i: (0, 0)), pl.BlockSpec((16, D), lambda i: (0, 0)),
                  _resident((D, D_FF)), _resident((D_FF, D))],
        out_specs=[wide, tok, tok, pl.BlockSpec((8, D), lambda i: (0, 0))],
        out_shape=[jax.ShapeDtypeStruct((s, D_FF), BF16), jax.ShapeDtypeStruct((s, D), F32),
                   jax.ShapeDtypeStruct((s, D), BF16), jax.ShapeDtypeStruct((8, D), F32)],
        compiler_params=_params(("arbitrary",)),
    )(d_dn, ru, x2, dx3, o, modr, vecs, w_up, w_down)


def _branch_bwd(do, sa, sb, ba, bb, w_a, w_b, w_out, dep):
    s = do.shape[0]
    tm = min(TM_BRANCH, s)

    def body(do_ref, sa_ref, sb_ref, ba_ref, bb_ref, wa_ref, wb_ref, wo_ref, dep_ref,
             dba_ref, dbb_ref, dg_ref, dza_ref, dpooled_ref):
        dmerged = _dot_nt(do_ref[...], wo_ref[...])
        sa = sa_ref[...]
        sb = sb_ref[...]
        dba = (dmerged * sa).astype(BF16)
        dbb = (dmerged * sb).astype(BF16)
        dba_ref[...] = dba
        dbb_ref[...] = dbb
        dg_ref[:, :D] = (dmerged * ba_ref[...].astype(F32) * sa * (1.0 - sa)).astype(BF16)
        dg_ref[:, D:] = (dmerged * bb_ref[...].astype(F32) * sb * (1.0 - sb)).astype(BF16)
        dza_ref[...] = _dot_nt(dba, wa_ref[...])
        dpooled_ref[...] = _dot_nt(dbb, wb_ref[...])

    tok = pl.BlockSpec((tm, D), lambda i: (i, 0))
    wspec = pl.BlockSpec((D, D), lambda i: (0, 0))
    sd = lambda dt: jax.ShapeDtypeStruct((s, D), dt)
    return pl.pallas_call(
        body, name="branch_bwd", grid=(s // tm,),
        in_specs=[tok, tok, tok, tok, tok, wspec, wspec, wspec, pl.BlockSpec(memory_space=pl.ANY)],
        out_specs=[tok, tok, pl.BlockSpec((tm, 2 * D), lambda i: (i, 0)), tok, tok],
        out_shape=[sd(BF16), sd(BF16), jax.ShapeDtypeStruct((s, 2 * D), BF16), sd(F32), sd(F32)],
        compiler_params=_params(("parallel",)),
    )(do, sa, sb, ba, bb, w_a, w_b, w_out, dep)


def _mix_bwd(dza, dpooled, x_rnn, ga, dga, xr, hr, p, gates, dgates, vecs, w_rg_a, w_rg_x, w_pool, dep):
    s = xr.shape[0]
    tm = min(TM_MIX, s)
    nb = s // tm

    def body(dza_ref, dpooled_ref, xh_ref, x_ref, ga_ref, dga_ref, xr_ref, hh_ref, hr_ref, p_ref,
             a_ref, mult_ref, ra_ref, ri_ref, dg_ref, vec_ref, wa_ref, wx_ref, wp_ref, dep_ref,
             dproj_ref, dwa_ref, dwx_ref, dwp_ref, small_ref,
             scan_carry, dxr_carry, q_carry, scan_scr, dwa_acc, dwx_acc, dwp_acc):
        i = pl.program_id(0)
        bi = nb - 1 - i
        first_t = bi == 0

        @pl.when(i == 0)
        def _():
            scan_carry[...] = jnp.zeros_like(scan_carry)
            dxr_carry[...] = jnp.zeros_like(dxr_carry)
            q_carry[...] = jnp.zeros_like(q_carry)
            dwa_acc[...] = jnp.zeros_like(dwa_acc)
            dwx_acc[...] = jnp.zeros_like(dwx_acc)
            dwp_acc[...] = jnp.zeros_like(dwp_acc)
            small_ref[...] = jnp.zeros_like(small_ref)

        row = lax.broadcasted_iota(jnp.int32, (tm, GW), 0)
        is_t0 = jnp.logical_and(first_t, row == 0)
        head_t = (lax.broadcasted_iota(jnp.int32, (HALO_U, GW), 0) + 1).astype(F32)
        colsum = lambda v: jnp.sum(v, axis=0, keepdims=True)
        for g in range(N_GROUPS):
            cs = slice(g * GW, (g + 1) * GW)
            vec = vec_ref[:, cs]
            xr = xr_ref[:, cs]
            hr = hr_ref[:, cs]
            dza = dza_ref[:, cs]
            dproj_ref[:, D + g * GW:D + (g + 1) * GW] = (dza * hr * dga_ref[:, cs]).astype(BF16)
            dhr = dza * ga_ref[:, cs]
            a = a_ref[:, cs]
            mult = mult_ref[:, cs]
            ra = ra_ref[:, cs].astype(F32)
            ri = ri_ref[:, cs].astype(F32)
            sp = _softplus(vec[V_A_PARAM:V_A_PARAM + 1])
            m = jnp.where(row == tm - 1, 1.0, _shift_up(a, 1))
            gsum = _scan_up(m, dhr, scan_carry[0:1, cs], scan_scr)
            scan_carry[0:1, cs] = a[0:1, :] * gsum[0:1, :]
            hh = jnp.where(first_t, 0.0, hh_ref[:, cs])
            hprev = _shift_down(jnp.concatenate([hh, hr], axis=0), 1)[8:]
            da = gsum * hprev
            dmult = jnp.where(is_t0, 0.0, gsum * xr * ri)
            dlog_a = da * a - dmult * a * a / mult
            dri = gsum * xr * mult
            dxr = gsum * ri * mult
            small_ref[7:8, cs] += colsum((-C_RG) * ra * dlog_a)
            dpa = (((-C_RG) * sp) * dlog_a * ra * (1.0 - ra))
            dpx = dri * ri * (1.0 - ri)
            small_ref[5:6, cs] += colsum(dpa)
            small_ref[6:7, cs] += colsum(dpx)
            dpa = dpa.astype(BF16)
            dpx = dpx.astype(BF16)
            xrb = xr.astype(BF16)
            dwa_acc[g] += _dot_tn(xrb, dpa)
            dwx_acc[g] += _dot_tn(xrb, dpx)
            dxr = dxr + _dot_nt(dpa, wa_ref[g]) + _dot_nt(dpx, wx_ref[g])
            small_ref[4:5, cs] += colsum(dxr)
            xh = jnp.where(first_t, 0.0, xh_ref[:, cs])
            taps = _conv_taps(jnp.concatenate([xh, x_ref[:, cs]], axis=0))
            dxr_ext = jnp.concatenate([dxr, dxr_carry[:, cs]], axis=0)
            dx = vec[V_CONV_W + 3:V_CONV_W + 4] * dxr
            for j in range(4):
                small_ref[j:j + 1, cs] += colsum(dxr * taps[j])
                if j < 3:
                    dx = dx + vec[V_CONV_W + j:V_CONV_W + j + 1] * _shift_up(dxr_ext, 3 - j)[:tm]
            dxr_carry[:, cs] = dxr[0:8, :]
            dproj_ref[:, cs] = dx.astype(BF16)
            pg = p_ref[:, cs]
            dpooled = dpooled_ref[:, cs]
            pb = _dot(pg, wp_ref[g]) + vec[V_B_POOL:V_B_POOL + 1]
            small_ref[9:10, cs] += colsum(dpooled * pb)
            dpb = dpooled * vec[V_POOL_SCALE:V_POOL_SCALE + 1]
            small_ref[8:9, cs] += colsum(dpb)
            dpbb = dpb.astype(BF16)
            dwp_acc[g] += _dot_tn(pg, dpbb)
            dp = _dot_nt(dpbb, wp_ref[g])
            q = _window_mean(dp, POOL_WINDOWS[g], first_t, head_t)
            sm = jnp.concatenate([q, q_carry[:, cs]], axis=0)
            k = 1
            while k < POOL_WINDOWS[g]:
                sm = sm + _shift_up(sm, k)
                k *= 2
            q_carry[:, cs] = q[0:HALO_U, :]
            dproj_ref[:, 2 * D + g * GW:2 * D + (g + 1) * GW] = (sm[:tm] - dp).astype(BF16)
        dproj_ref[:, 3 * D:] = dg_ref[...]

        @pl.when(i == nb - 1)
        def _():
            dwa_ref[...] = dwa_acc[...].astype(BF16)
            dwx_ref[...] = dwx_acc[...].astype(BF16)
            dwp_ref[...] = dwp_acc[...].astype(BF16)

    rev = lambda i: nb - 1 - i
    tok = pl.BlockSpec((tm, D), lambda i: (rev(i), 0))
    halo8 = lambda k: pl.BlockSpec((8, D), lambda i: (jnp.maximum(rev(i) * (tm // 8) - 1, 0), k))
    wspec = pl.BlockSpec((N_GROUPS, GW, GW), lambda i: (0, 0, 0))
    wshape = jax.ShapeDtypeStruct((N_GROUPS, GW, GW), BF16)
    return pl.pallas_call(
        body, name="mix_bwd", grid=(nb,),
        in_specs=[tok, tok, halo8(0), tok, tok, tok, tok, halo8(0), tok, tok, tok, tok, tok, tok,
                  pl.BlockSpec((tm, 2 * D), lambda i: (rev(i), 0)),
                  pl.BlockSpec((16, D), lambda i: (0, 0)), wspec, wspec, wspec, pl.BlockSpec(memory_space=pl.ANY)],
        out_specs=[pl.BlockSpec((tm, D_IN), lambda i: (rev(i), 0)), wspec, wspec, wspec,
                   pl.BlockSpec((16, D), lambda i: (0, 0))],
        out_shape=[jax.ShapeDtypeStruct((s, D_IN), BF16), wshape, wshape, wshape,
                   jax.ShapeDtypeStruct((16, D), F32)],
        scratch_shapes=[pltpu.VMEM((8, D), F32), pltpu.VMEM((8, D), F32), pltpu.VMEM((HALO_U, D), F32),
                        pltpu.VMEM((3, tm, LANES), F32)] + [pltpu.VMEM((N_GROUPS, GW, GW), F32)] * 3,
        compiler_params=_params(("arbitrary",)),
    )(dza, dpooled, x_rnn, x_rnn, ga, dga, xr, hr, hr, p, *gates, dgates, vecs, w_rg_a, w_rg_x, w_pool, dep)


def _proj_bwd(dproj, x, dx2, modr, vecs, w_in, dep):
    s = x.shape[0]
    tm = min(TM_PROJ, s)

    def body(dp_ref, x_ref, dx2_ref, mod_ref, vec_ref, w_ref, dep_ref, gx_ref, small_ref):
        @pl.when(pl.program_id(0) == 0)
        def _():
            small_ref[...] = jnp.zeros_like(small_ref)

        dh1 = None
        for c in range(D_IN // D):
            cs = slice(c * D, (c + 1) * D)
            part = _dot_nt(dp_ref[:, cs], w_ref[:, cs])
            dh1 = part if dh1 is None else dh1 + part
        xv = x_ref[...]
        r1 = lax.rsqrt(jnp.mean(xv * xv, axis=-1, keepdims=True) + EPS)
        xn1 = xv * r1
        gain = vec_ref[V_G1:V_G1 + 1, :] * (1.0 + mod_ref[M_SC1:M_SC1 + 1, :])
        dxn1 = dh1 * gain
        gx_ref[...] = dx2_ref[...] + r1 * (dxn1 - xn1 * jnp.mean(dxn1 * xn1, axis=-1, keepdims=True))
        small_ref[0:1, :] += jnp.sum(dh1, axis=0, keepdims=True)
        small_ref[1:2, :] += jnp.sum(dh1 * xn1, axis=0, keepdims=True)

    tok = pl.BlockSpec((tm, D), lambda i: (i, 0))
    return pl.pallas_call(
        body, name="proj_bwd", grid=(s // tm,),
        in_specs=[pl.BlockSpec((tm, D_IN), lambda i: (i, 0)), tok, tok,
                  pl.BlockSpec((8, D), lambda i: (0, 0)), pl.BlockSpec((16, D), lambda i: (0, 0)),
                  _resident((D, D_IN)), pl.BlockSpec(memory_space=pl.ANY)],
        out_specs=[tok, pl.BlockSpec((8, D), lambda i: (0, 0))],
        out_shape=[jax.ShapeDtypeStruct((s, D), F32), jax.ShapeDtypeStruct((8, D), F32)],
        compiler_params=_params(("arbitrary",)),
    )(dproj, x, dx2, modr, vecs, w_in, dep)


def _wgrad(a, b, name, square_a=False, dep=None):
    s, ka = a.shape
    n = b.shape[1]
    tka = ka if ka <= 1024 else ka // 2
    tn = n if n <= 1024 else n // 2
    ts = min(TS_WGRAD, s)
    ns = s // ts
    nc = 512
    deps = [] if dep is None else [dep]

    def body(a_ref, b_ref, *refs):
        out_ref, acc_ref = refs[-2:]
        t = pl.program_id(2)

        @pl.when(t == 0)
        def _():
            acc_ref[...] = jnp.zeros_like(acc_ref)

        av = a_ref[...]
        if square_a:
            af = av.astype(F32)
            av = (af * af).astype(BF16)
        for c in range(tn // nc):
            cs = slice(c * nc, (c + 1) * nc)
            acc_ref[:, cs] += _dot_tn(av, b_ref[:, cs])

        @pl.when(t == ns - 1)
        def _():
            out_ref[...] = acc_ref[...].astype(BF16)

    return pl.pallas_call(
        body, name=name, grid=(ka // tka, n // tn, ns),
        in_specs=[pl.BlockSpec((ts, tka), lambda i, j, t: (t, i)),
                  pl.BlockSpec((ts, tn), lambda i, j, t: (t, j))] + [pl.BlockSpec(memory_space=pl.ANY)] * len(deps),
        out_specs=pl.BlockSpec((tka, tn), lambda i, j, t: (i, j)),
        out_shape=jax.ShapeDtypeStruct((ka, n), BF16),
        scratch_shapes=[pltpu.VMEM((tka, tn), F32)],
        compiler_params=_params(("parallel", "parallel", "arbitrary")),
    )(a, b, *deps)


def _window(ref, kind, idx, size):
    start = pl.multiple_of(idx * size, size)
    if kind == 0:
        return ref.at[pl.ds(start, size)]
    if kind == 1:
        return ref.at[:, pl.ds(start, size)]
    return ref.at[:, :, pl.ds(start, size)]


def _mesh_place():
    x, y, c = lax.axis_index("x"), lax.axis_index("y"), lax.axis_index("c")
    return x, y, c, 4 * x + 2 * y + c


def _peer(x, y, c, q):
    px = 1 - x if q & 4 else x
    py = 1 - y if q & 2 else y
    pc = 1 - c if q & 1 else c
    return (px, py, pc), 4 * px + 2 * py + pc


_HBM = pl.BlockSpec(memory_space=pltpu.HBM)
_SEM = pl.BlockSpec(memory_space=pltpu.SEMAPHORE)
_EFFECT = pltpu.SideEffectType.DATAFLOW_SIDE_EFFECTING


N_NEAR = 4


def _near(x, y, c):
    out = [((x, y, 1 - c), 4 * x + 2 * y + 1 - c)]
    for j in (1, 2, 3):
        px = 1 - x if j & 2 else x
        py = 1 - y if j & 1 else y
        out.append(((px, py, c), 4 * px + 2 * py + c))
    return out


def _remote(src, dst, send_sems, recv_sems, slot, device):
    return pltpu.make_async_remote_copy(src_ref=src, dst_ref=dst, send_sem=send_sems.at[slot], recv_sem=recv_sems.at[slot],
                                        device_id=device, device_id_type=MESH)


def _split_call(name, arrays, sems_in, n_new_sems, after, emit):
    na, ns, nn = len(arrays), len(sems_in), len(n_new_sems)

    def body(*refs):
        emit(refs[:na], refs[na:na + ns], refs[na + ns + 1:na + ns + 1 + nn])
        refs[-1][...] = jnp.zeros_like(refs[-1])

    outs = pl.pallas_call(
        body, name=name,
        out_shape=(*[pltpu.SemaphoreType.DMA((m,)) for m in n_new_sems],
                   *[pltpu.HBM(a.shape, a.dtype) for a in arrays], jax.ShapeDtypeStruct((8, 128), F32)),
        in_specs=[_HBM] * na + [_SEM] * ns + [pl.BlockSpec(memory_space=pl.ANY)],
        out_specs=(*[_SEM] * nn, *[_HBM] * na, pl.BlockSpec(memory_space=pltpu.VMEM)),
        input_output_aliases={i: nn + i for i in range(na)},
        compiler_params=pltpu.CompilerParams(has_side_effects=_EFFECT),
    )(*[pltpu.with_memory_space_constraint(a, pltpu.HBM) for a in arrays], *sems_in, after)
    return list(outs[:nn]), list(outs[nn:nn + na]), outs[-1]


class _Gather:
    def __init__(self, shards, kinds, after, name):
        self.n, self.kinds, self.name = len(shards), kinds, name
        self.sizes = [s.shape[k] for s, k in zip(shards, kinds)]
        n = self.n
        lands = []
        for s, k in zip(shards, kinds):
            dims = list(s.shape)
            dims[k] *= N_DEV
            lands.append(lax.empty(tuple(dims), s.dtype))

        def emit(arr, _, new):
            x, y, c, me = _mesh_place()
            for k in range(n):
                pltpu.make_async_copy(arr[k], _window(arr[n + k], kinds[k], me, self.sizes[k]), new[2].at[k]).start()
            for k in range(n):
                mine = _window(arr[n + k], kinds[k], me, self.sizes[k])
                for j, (dev, _) in enumerate(_near(x, y, c)):
                    _remote(arr[k], mine, new[0], new[1], k * N_NEAR + j, dev).start()

        self.sems, self.arrays, self.token = _split_call(name + "_start", [*shards, *lands], [],
                                                         [n * N_NEAR, n * N_NEAR, n], after, emit)

    def forward(self, after):
        n, kinds, sizes = self.n, self.kinds, self.sizes

        def emit(arr, old, new):
            x, y, c, _ = _mesh_place()
            near = _near(x, y, c)
            for k in range(n):
                for j in (1, 2, 3):
                    dev, idx = near[j]
                    landed = _window(arr[n + k], kinds[k], idx, sizes[k])
                    _remote(arr[k], landed, old[0], old[1], k * N_NEAR + j, dev).wait_recv()
                    _remote(landed, landed, new[0], new[1], k * N_NEAR + j, near[0][0]).start()

        new, self.arrays, self.token = _split_call(self.name + "_forward", self.arrays, self.sems, [n * N_NEAR] * 2,
                                                   after, emit)
        self.sems = [*self.sems, *new]

    def finish(self, after):
        n, kinds, sizes = self.n, self.kinds, self.sizes

        def emit(arr, old, _):
            x, y, c, me = _mesh_place()
            near = _near(x, y, c)
            other_core = near[0][0]
            for k in range(n):
                win = lambda idx: _window(arr[n + k], kinds[k], idx, sizes[k])
                pltpu.make_async_copy(arr[k], win(me), old[2].at[k]).wait()
                for j, (dev, idx) in enumerate(near):
                    _remote(arr[k], win(me), old[0], old[1], k * N_NEAR + j, dev).wait_send()
                _remote(arr[k], win(near[0][1]), old[0], old[1], k * N_NEAR, other_core).wait_recv()
                for j in (1, 2, 3):
                    idx = near[j][1]
                    _remote(win(idx), win(idx), old[3], old[4], k * N_NEAR + j, other_core).wait_send()
                    _remote(arr[k], win(idx + 1 - 2 * c), old[3], old[4], k * N_NEAR + j, other_core).wait_recv()

        _, arrays, _ = _split_call(self.name + "_finish", self.arrays, self.sems, [], after, emit)
        return arrays[n:]


class _Spread:
    def __init__(self, shards, kinds, after, name):
        self.n, self.kinds, self.name = len(shards), kinds, name
        self.sizes = [s.shape[k] for s, k in zip(shards, kinds)]
        n = self.n
        lands = []
        for s, k in zip(shards, kinds):
            dims = list(s.shape)
            dims[k] *= N_DEV
            lands.append(lax.empty(tuple(dims), s.dtype))

        def emit(arr, _, new):
            x, y, c, me = _mesh_place()
            for k in range(n):
                mine = _window(arr[n + k], kinds[k], me, self.sizes[k])
                pltpu.make_async_copy(arr[k], mine, new[2].at[k]).start()
                for q in range(1, N_DEV):
                    _remote(arr[k], mine, new[0], new[1], k * N_DEV + q, _peer(x, y, c, q)[0]).start()

        self.sems, self.arrays, self.token = _split_call(name + "_start", [*shards, *lands], [],
                                                         [n * N_DEV, n * N_DEV, n], after, emit)

    def finish(self, after):
        n, kinds, sizes = self.n, self.kinds, self.sizes

        def emit(arr, old, _):
            x, y, c, me = _mesh_place()
            for k in range(n):
                win = lambda idx: _window(arr[n + k], kinds[k], idx, sizes[k])
                pltpu.make_async_copy(arr[k], win(me), old[2].at[k]).wait()
                for q in range(1, N_DEV):
                    peer, peer_idx = _peer(x, y, c, q)
                    _remote(arr[k], win(me), old[0], old[1], k * N_DEV + q, peer).wait_send()
                    _remote(arr[k], win(peer_idx), old[0], old[1], k * N_DEV + q, peer).wait_recv()

        _, arrays, _ = _split_call(self.name + "_finish", self.arrays, self.sems, [], after, emit)
        return arrays[n:]


class _Scatter:
    def __init__(self, partials, kinds, after, name):
        self.n, self.kinds, self.name, self.partials = len(partials), kinds, name, partials
        self.sizes = [p.shape[k] // N_DEV for p, k in zip(partials, kinds)]
        n, sizes = self.n, self.sizes
        self.slot_shapes = []
        for p, k, size in zip(partials, kinds, sizes):
            dims = list(p.shape)
            dims[k] = size
            self.slot_shapes.append((N_NEAR, *dims))
        slots = [lax.empty(sh, p.dtype) for sh, p in zip(self.slot_shapes, partials)]

        def emit(arr, _, new):
            x, y, c, _ = _mesh_place()
            near = _near(x, y, c)
            for k in range(n):
                for j in range(N_NEAR):
                    owner = near[j][1] if j == 0 else near[j][1] + 1 - 2 * c
                    _remote(_window(arr[k], kinds[k], owner, sizes[k]), arr[n + k].at[j], new[0], new[1],
                            k * N_NEAR + j, near[0][0]).start()

        self.sems, self.arrays, self.token = _split_call(name + "_start", [*partials, *slots], [], [n * N_NEAR] * 2,
                                                         after, emit)

    def combine_and_send(self, own4, after):
        n, kinds, sizes = self.n, self.kinds, self.sizes

        def emit_wait(arr, old, _):
            x, y, c, _ = _mesh_place()
            near = _near(x, y, c)
            for k in range(n):
                for j in range(N_NEAR):
                    owner = near[j][1] if j == 0 else near[j][1] + 1 - 2 * c
                    cp = _remote(_window(arr[k], kinds[k], owner, sizes[k]), arr[n + k].at[j], old[0], old[1],
                                 k * N_NEAR + j, near[0][0])
                    cp.wait_send()
                    cp.wait_recv()

        _, arrays, _ = _split_call(self.name + "_landed", self.arrays, self.sems, [], after, emit_wait)
        chip_sums = _chip_sums(arrays[:n], arrays[n:], kinds, sizes, own4, self.name + "_combine")
        arrivals = [lax.empty((N_NEAR - 1, *sh[1:]), p.dtype) for sh, p in zip(self.slot_shapes, self.partials)]

        def emit_send(arr, _, new):
            x, y, c, _ = _mesh_place()
            near = _near(x, y, c)
            for k in range(n):
                for j in (1, 2, 3):
                    _remote(arr[k].at[j], arr[n + k].at[j - 1], new[0], new[1], k * N_NEAR + j, near[j][0]).start()

        self.sems, self.arrays, self.token = _split_call(self.name + "_send", [*chip_sums, *arrivals], [],
                                                         [n * N_NEAR] * 2, own4, emit_send)

    def finish(self, after):
        n = self.n

        def emit(arr, old, _):
            x, y, c, _ = _mesh_place()
            near = _near(x, y, c)
            for k in range(n):
                for j in (1, 2, 3):
                    cp = _remote(arr[k].at[j], arr[n + k].at[j - 1], old[0], old[1], k * N_NEAR + j, near[j][0])
                    cp.wait_send()
                    cp.wait_recv()

        _, arrays, _ = _split_call(self.name + "_finish", self.arrays, self.sems, [], after, emit)
        return arrays[:n], arrays[n:]


def _chip_sums(partials, slots, kinds, sizes, own4, name):
    n = len(partials)

    def body(own_ref, *refs):
        for k in range(n):
            refs[2 * n + k][...] = (refs[k][...].astype(F32) + refs[n + k][...].astype(F32)).astype(BF16)

    in_specs, slot_specs = [], []
    for p, s, kind, size in zip(partials, slots, kinds, sizes):
        block = list(p.shape)
        block[kind] = size
        nd = len(block)
        in_specs.append(pl.BlockSpec(tuple(block), functools.partial(
            lambda j, own, kind, nd: tuple(own[j] if d == kind else 0 for d in range(nd)), kind=kind, nd=nd)))
        slot_specs.append(pl.BlockSpec((None, *block), functools.partial(
            lambda j, own, nd: (j,) + (0,) * nd, nd=nd)))
    return pl.pallas_call(
        body, name=name,
        grid_spec=pltpu.PrefetchScalarGridSpec(num_scalar_prefetch=1, grid=(N_NEAR,),
                                               in_specs=in_specs + slot_specs, out_specs=slot_specs),
        out_shape=[jax.ShapeDtypeStruct(s.shape, s.dtype) for s in slots],
        compiler_params=_params(("arbitrary",)),
    )(own4, *partials, *slots)


def _to_bf16(arrays, name, dep=None):
    n = len(arrays)
    deps = [] if dep is None else [dep]

    def body(*refs):
        for src, dst in zip(refs[:n], refs[n + len(deps):]):
            dst[...] = src[...].astype(BF16)

    vmem = pl.BlockSpec(memory_space=pltpu.VMEM)
    return pl.pallas_call(body, name=name, out_shape=[jax.ShapeDtypeStruct(a.shape, BF16) for a in arrays],
                          in_specs=[vmem] * n + [pl.BlockSpec(memory_space=pl.ANY)] * len(deps), out_specs=[vmem] * n,
                          compiler_params=pltpu.CompilerParams(vmem_limit_bytes=V7X_VMEM_LIMIT))(*arrays, *deps)


def _silu(c):
    return c * _sigmoid_tail(c)


def _ada_fwd(c_all, w_ada, b_ada_cols, dep):
    def body(c_ref, w_ref, b_ref, dep_ref, out_ref):
        out_ref[...] = jnp.dot(_silu(c_ref[...]), w_ref[...], preferred_element_type=F32,
                               precision=lax.Precision.HIGHEST) + b_ref[...]

    vmem = pl.BlockSpec(memory_space=pltpu.VMEM)
    return pl.pallas_call(
        body, name="ada_fwd", in_specs=[vmem, vmem, vmem, pl.BlockSpec(memory_space=pl.ANY)], out_specs=vmem,
        out_shape=jax.ShapeDtypeStruct((N_DEV, w_ada.shape[1]), F32),
    )(c_all, w_ada, b_ada_cols, dep)


def _adam(w, g, m, v):
    m = ADAM_B1 * m + (1.0 - ADAM_B1) * g
    v = ADAM_B2 * v + (1.0 - ADAM_B2) * (g * g)
    m_hat = m / (1.0 - ADAM_B1 ** ADAM_STEP)
    v_hat = v / (1.0 - ADAM_B2 ** ADAM_STEP)
    delta = -ADAM_LR * (m_hat / (jnp.sqrt(v_hat) + ADAM_EPS) + ADAM_WD * w)
    return delta, m, v


def _ada_bwd_adam(c_all, dmod_cols, w, m, v):
    def body(c_ref, d_ref, w_ref, m_ref, v_ref, g_ref, delta_ref, nm_ref, nv_ref):
        g = lax.dot_general(_silu(c_ref[...]), d_ref[...], (((0,), (0,)), ((), ())),
                            preferred_element_type=F32, precision=lax.Precision.HIGHEST)
        g_ref[...] = g
        delta_ref[...], nm_ref[...], nv_ref[...] = _adam(w_ref[...], g, m_ref[...], v_ref[...])

    sd = jax.ShapeDtypeStruct(w.shape, F32)
    return pl.pallas_call(body, name="ada_bwd_adam", out_shape=[sd] * 4,
                          compiler_params=pltpu.CompilerParams(vmem_limit_bytes=V7X_VMEM_LIMIT),
                          )(c_all, dmod_cols, w, m, v)


def _adam_group(chip_sums, arrivals, ws, ms, vs, n_tiles, name):
    n = len(ws)

    def body(*refs):
        for k in range(n):
            c_ref, a_ref, w_ref, m_ref, v_ref = (refs[j * n + k] for j in range(5))
            g_ref, delta_ref, nm_ref, nv_ref = (refs[(5 + j) * n + k] for j in range(4))
            g = c_ref[...].astype(F32)
            for j in range(N_NEAR - 1):
                g = g + a_ref[j].astype(F32)
            g_ref[...] = g
            delta_ref[...], nm_ref[...], nv_ref[...] = _adam(w_ref[...], g, m_ref[...], v_ref[...])

    tiles = [(w.shape[0] // n_tiles, w.shape[1]) for w in ws]
    blk = [pl.BlockSpec(t, lambda i: (i, 0)) for t in tiles]
    return pl.pallas_call(
        body, name=name, grid=(n_tiles,),
        in_specs=[pl.BlockSpec((None, *t), lambda i: (0, i, 0)) for t in tiles]
        + [pl.BlockSpec((N_NEAR - 1, *t), lambda i: (0, i, 0)) for t in tiles] + blk * 3,
        out_specs=blk * 4, out_shape=[jax.ShapeDtypeStruct(w.shape, F32) for w in ws] * 4,
        compiler_params=_params(("parallel",)),
    )(*chip_sums, *arrivals, *ws, *ms, *vs)


N_SMALL = 40
N_SMALL_PARAMS = 11


def _pack_vecs(conv_w_full, rows):
    def body(cw_ref, *refs):
        out = refs[-1]
        out[...] = jnp.zeros_like(out)
        out[0:4, :] = cw_ref[0:4, :]
        for r, ref in enumerate(refs[:-1]):
            out[4 + r:5 + r, :] = ref[...]

    return pl.pallas_call(body, name="pack_vecs", out_shape=jax.ShapeDtypeStruct((16, D), F32))(conv_w_full, *rows)


def _small_finish(gathered, conv_cols, mod_all, vecs, ws, ms, vs):
    n = N_SMALL_PARAMS

    def body(g_ref, conv_ref, mod_ref, vec_ref, *refs):
        w_refs, m_refs, v_refs = refs[:n], refs[n:2 * n], refs[2 * n:3 * n]
        outs = refs[3 * n:]
        g1 = vec_ref[V_G1:V_G1 + 1, :]
        g2 = vec_ref[V_G2:V_G2 + 1, :]
        zero = jnp.zeros((1, D), F32)
        dg1, dg2, dgf, loss_lanes = zero, zero, zero, zero
        mixer = jnp.zeros((16, D), F32)
        db_ada = jnp.zeros((6, D), F32)
        d_conv_w = jnp.zeros(conv_ref.shape[1:], F32)
        for b in range(N_DEV):
            gb = g_ref[b]
            mod = mod_ref[b]
            q1 = gb[33:34]
            q2 = gb[9:10]
            dmod = jnp.concatenate([gb[32:33], q1 * g1, gb[10:11], gb[8:9], q2 * g2, gb[1:2]], axis=0)
            outs[4 * n][b] = dmod
            db_ada = db_ada + dmod
            dg1 = dg1 + q1 * (1.0 + mod[M_SC1:M_SC1 + 1])
            dg2 = dg2 + q2 * (1.0 + mod[M_SC2:M_SC2 + 1])
            dgf = dgf + gb[0:1]
            loss_lanes = loss_lanes + gb[2:3]
            mixer = mixer + gb[16:32]
            d_conv_w = d_conv_w + conv_ref[b]
        d_a_param = mixer[7:8] * _sigmoid_tail(vec_ref[V_A_PARAM:V_A_PARAM + 1, :])
        grads = [dg1, dg2, mixer[4:5], mixer[5:6], mixer[6:7], d_a_param, mixer[8:9], mixer[9:10], dgf,
                 db_ada, d_conv_w]

        def load(ref, rows):
            if ref.shape[0] == rows:
                return ref[...]
            return jnp.concatenate([ref[:, j * D:(j + 1) * D] for j in range(rows)], axis=0)

        def store(ref, val):
            if ref.shape == val.shape:
                ref[...] = val
            else:
                for j in range(val.shape[0]):
                    ref[:, j * D:(j + 1) * D] = val[j:j + 1]

        for k in range(n):
            rows = grads[k].shape[0]
            results = (grads[k], *_adam(load(w_refs[k], rows), grads[k], load(m_refs[k], rows), load(v_refs[k], rows)))
            for which, val in enumerate(results):
                store(outs[which * n + k], val)
        outs[4 * n + 1][...] = jnp.broadcast_to(jnp.sum(loss_lanes, axis=1, keepdims=True), (8, 128))

    shapes = [jax.ShapeDtypeStruct(w.shape, F32) for w in ws]
    return pl.pallas_call(
        body, name="small_finish",
        out_shape=shapes * 4 + [jax.ShapeDtypeStruct((N_DEV, 6, D), F32), jax.ShapeDtypeStruct((8, 128), F32)],
    )(gathered, conv_cols, mod_all, vecs, *ws, *ms, *vs)


def _pad_rows(a, rows):
    return jnp.pad(a, ((0, rows - a.shape[0]), (0, 0)))


def kernel(x, c, norm_mix_g, norm_mlp_g, w_ada, b_ada, w_in, conv_w, conv_b, w_rg_a, b_rg_a, w_rg_x, b_rg_x, a_param, w_branch_a, w_pool, b_pool, pool_scale, w_branch_b, w_out, w_up, w_down, final_g, loss_target, m_norm_mix_g, m_norm_mlp_g, m_w_ada, m_b_ada, m_w_in, m_conv_w, m_conv_b, m_w_rg_a, m_b_rg_a, m_w_rg_x, m_b_rg_x, m_a_param, m_w_branch_a, m_w_pool, m_b_pool, m_pool_scale, m_w_branch_b, m_w_out, m_w_up, m_w_down, m_final_g, v_norm_mix_g, v_norm_mlp_g, v_w_ada, v_b_ada, v_w_in, v_conv_w, v_conv_b, v_w_rg_a, v_b_rg_a, v_w_rg_x, v_b_rg_x, v_a_param, v_w_branch_a, v_w_pool, v_b_pool, v_pool_scale, v_w_branch_b, v_w_out, v_w_up, v_w_down, v_final_g):
    me = 4 * lax.axis_index("x") + 2 * lax.axis_index("y") + lax.axis_index("c")
    s = x.shape[1]
    x2d = x.reshape(s, D)
    target = loss_target.reshape(s, D)
    n_ada = w_ada.shape[2]

    b_ada_cols = lax.dynamic_slice(b_ada, (0, me * n_ada), (1, n_ada))
    spread_c = _Spread([_pad_rows(c, 8), _pad_rows(conv_w[0], 8)], [0, 1], c, "spread_c")

    sharded = dict(w_in=(w_in[0], 1), w_up=(w_up[0], 1), w_down=(w_down[0], 0), w_branch_a=(w_branch_a[0], 0),
                   w_branch_b=(w_branch_b[0], 0), w_out=(w_out[0], 0), w_rg_a=(w_rg_a[0], 1), w_rg_x=(w_rg_x[0], 1),
                   w_pool=(w_pool[0], 1))
    kind = {k: v[1] for k, v in sharded.items()}
    first_names = ["w_in"]
    later_names = [k for k in sharded if k not in first_names]
    mix_names = ["w_rg_a", "w_rg_x", "w_pool"]
    branch_names = ["w_branch_a", "w_branch_b", "w_out"]
    mlp_names = ["w_up", "w_down"]

    def gather(group, after, name):
        return _Gather([shard[k] for k in group], [kind[k] for k in group], after, name)

    shard = dict(zip(first_names, _to_bf16([sharded[k][0] for k in first_names], "to_bf16_first")))
    g_first = gather(first_names, spread_c.token, "gather_first")
    shard.update(zip(later_names, _to_bf16([sharded[k][0] for k in later_names], "to_bf16_later", dep=g_first.token)))

    c_rows, conv_w_full = spread_c.finish(g_first.token)
    c_all = c_rows.reshape(N_DEV, 8, D)[:, 0, :]
    mod_part = _ada_fwd(c_all, w_ada[0], b_ada_cols, g_first.token)
    spread_mod = _Spread([mod_part], [0], g_first.token, "spread_mod")
    spread_mix = _Spread([shard[k] for k in mix_names], [kind[k] for k in mix_names], spread_mod.token, "spread_mix")
    g_branch = gather(branch_names, spread_mix.token, "gather_branch")
    g_mlp = gather(mlp_names, g_branch.token, "gather_mlp")

    vecs = _pack_vecs(conv_w_full, [conv_b, b_rg_a, b_rg_x, a_param, b_pool, pool_scale,
                                    norm_mix_g, norm_mlp_g, final_g.reshape(1, D)])
    g_first.forward(g_mlp.token)
    wg = dict(zip(first_names, g_first.finish(g_first.token)))
    mod_parts, = spread_mod.finish(g_first.token)
    mod_all = jnp.transpose(mod_parts.reshape(N_DEV, N_DEV, n_ada), (1, 0, 2)).reshape(N_DEV, 6, D)
    modr = _pad_rows(lax.dynamic_index_in_dim(mod_all, me, 0, keepdims=False), 8)

    h1, x_rnn, u_pool, ga, dga, sa, sb = _proj_fwd(x2d, modr, vecs, wg["w_in"])
    g_branch.forward(h1)
    wg.update(zip(mix_names, spread_mix.finish(g_branch.token)))
    xr, hr, za, p, pooled, *gates = _mix_fwd(x_rnn, u_pool, ga, vecs, wg["w_rg_a"], wg["w_rg_x"], wg["w_pool"],
                                             dep=g_branch.token)
    g_mlp.forward(za)
    wg.update(zip(branch_names, g_branch.finish(g_mlp.token)))
    ba, bb, merged, o, x2, h2 = _branch_fwd(za, pooled, sa, sb, x2d, modr, vecs,
                                            wg["w_branch_a"], wg["w_branch_b"], wg["w_out"])
    wg.update(zip(mlp_names, g_mlp.finish(h2)))
    ru, dx3, d_dn, small_f = _mlp_fwd(h2, x2, target, modr, vecs, wg["w_up"], wg["w_down"])

    near = _near(lax.axis_index("x"), lax.axis_index("y"), lax.axis_index("c"))
    own4 = jnp.stack([me, near[1][1], near[2][1], near[3][1]]).astype(jnp.int32)

    def scatter(group, partial, after, name):
        return _Scatter([partial[k] for k in group], [kind[k] for k in group], after, name)

    dup, dx2, do, small_m = _mlp_bwd(d_dn, ru, x2, dx3, o, modr, vecs, wg["w_up"], wg["w_down"])
    partial = dict(w_up=_wgrad(h2, dup, "wgrad_up"), w_down=_wgrad(ru, d_dn, "wgrad_down", square_a=True))
    s_mlp = scatter(mlp_names, partial, dx2, "scatter_mlp")

    dba, dbb, dgates, dza, dpooled = _branch_bwd(do, sa, sb, ba, bb, wg["w_branch_a"], wg["w_branch_b"], wg["w_out"],
                                                 dep=s_mlp.token)
    s_mlp.combine_and_send(own4, dza)
    dproj, dw_rg_a, dw_rg_x, dw_pool, small_x = _mix_bwd(dza, dpooled, x_rnn, ga, dga, xr, hr, p, gates, dgates,
                                                         vecs, wg["w_rg_a"], wg["w_rg_x"], wg["w_pool"],
                                                         dep=s_mlp.token)
    partial.update(w_branch_a=_wgrad(za, dba, "wgrad_branch_a"), w_branch_b=_wgrad(pooled, dbb, "wgrad_branch_b"),
                   w_out=_wgrad(merged, do, "wgrad_out"),
                   w_rg_a=dw_rg_a, w_rg_x=dw_rg_x, w_pool=dw_pool)
    mixer_names = ["w_rg_a", "w_rg_x", "w_pool", "w_branch_a", "w_branch_b", "w_out"]
    s_mixer = scatter(mixer_names, partial, s_mlp.token, "scatter_mixer")

    partial["w_in"] = _wgrad(h1, dproj, "wgrad_in", dep=s_mixer.token)
    s_in = scatter(["w_in"], partial, s_mixer.token, "scatter_in")
    s_mixer.combine_and_send(own4, s_in.token)
    s_in.combine_and_send(own4, s_mixer.token)
    grad_x, small_p = _proj_bwd(dproj, x2d, dx2, modr, vecs, wg["w_in"], dep=s_in.token)

    locals_ = dict(w_in=(w_in, m_w_in, v_w_in), w_up=(w_up, m_w_up, v_w_up), w_down=(w_down, m_w_down, v_w_down),
                   w_branch_a=(w_branch_a, m_w_branch_a, v_w_branch_a),
                   w_branch_b=(w_branch_b, m_w_branch_b, v_w_branch_b), w_out=(w_out, m_w_out, v_w_out),
                   w_rg_a=(w_rg_a, m_w_rg_a, v_w_rg_a), w_rg_x=(w_rg_x, m_w_rg_x, v_w_rg_x),
                   w_pool=(w_pool, m_w_pool, v_w_pool))
    res = {}

    def finish(group, exchange, after, n_tiles, name):
        chip_sums, arrivals = exchange.finish(after)
        flat = lambda t: t.reshape(-1, t.shape[-1])
        shapes = [flat(locals_[k][0]).shape for k in group]
        outs = _adam_group([cs.reshape(N_NEAR, *sh) for cs, sh in zip(chip_sums, shapes)],
                           [ar.reshape(N_NEAR - 1, *sh) for ar, sh in zip(arrivals, shapes)],
                           *[[flat(locals_[k][j]) for k in group] for j in range(3)], n_tiles, name)
        for i, k in enumerate(group):
            res[k] = [outs[j * len(group) + i].reshape(locals_[k][0].shape) for j in range(4)]
        return res[group[-1]][0]

    small = jnp.concatenate([small_f, small_m, small_x, small_p], axis=0)
    g_small = _Gather([small], [0], grad_x, "gather_small")
    done = finish(mlp_names, s_mlp, g_small.token, 4, "adam_mlp")
    done = finish(mixer_names, s_mixer, done, 2, "adam_mixer")
    g_small.forward(done)
    done = finish(["w_in"], s_in, g_small.token, 4, "adam_in")
    small_all, = g_small.finish(done)
    small_all = small_all.reshape(N_DEV, N_SMALL, D)

    conv_cols = lax.dynamic_slice(small_all, (0, 16, me * (D // N_DEV)), (N_DEV, 4, D // N_DEV))

    def smalls(ng, nl, cb, bra, brx, ap, bp, ps, fg, ba_, cw):
        return [ng, nl, cb, bra, brx, ap, bp, ps, fg.reshape(1, D), ba_, cw[0]]

    small_names = ["norm_mix_g", "norm_mlp_g", "conv_b", "b_rg_a", "b_rg_x", "a_param", "b_pool", "pool_scale",
                   "final_g", "b_ada", "conv_w"]
    fin = _small_finish(
        small_all, conv_cols, mod_all, vecs,
        smalls(norm_mix_g, norm_mlp_g, conv_b, b_rg_a, b_rg_x, a_param, b_pool, pool_scale, final_g, b_ada, conv_w),
        smalls(m_norm_mix_g, m_norm_mlp_g, m_conv_b, m_b_rg_a, m_b_rg_x, m_a_param, m_b_pool, m_pool_scale,
               m_final_g, m_b_ada, m_conv_w),
        smalls(v_norm_mix_g, v_norm_mlp_g, v_conv_b, v_b_rg_a, v_b_rg_x, v_a_param, v_b_pool, v_pool_scale,
               v_final_g, v_b_ada, v_conv_w))
    dmod_all, loss_tile = fin[4 * N_SMALL_PARAMS], fin[4 * N_SMALL_PARAMS + 1]
    dmod_cols = lax.dynamic_slice(dmod_all.reshape(N_DEV, 6 * D), (0, me * n_ada), (N_DEV, n_ada))
    res["w_ada"] = [t.reshape(w_ada.shape) for t in _ada_bwd_adam(c_all, dmod_cols, w_ada[0], m_w_ada[0], v_w_ada[0])]

    def final_shape(k, t):
        if k == "final_g":
            return t.reshape(D)
        if k == "conv_w":
            return t.reshape(conv_w.shape)
        return t

    for i, k in enumerate(small_names):
        res[k] = [final_shape(k, fin[which * N_SMALL_PARAMS + i]) for which in range(4)]
    order = ["norm_mix_g", "norm_mlp_g", "w_ada", "b_ada", "w_in", "conv_w", "conv_b", "w_rg_a", "b_rg_a", "w_rg_x",
             "b_rg_x", "a_param", "w_branch_a", "w_pool", "b_pool", "pool_scale", "w_branch_b", "w_out", "w_up",
             "w_down", "final_g"]
    outs = [loss_tile[0, 0], grad_x.reshape(x.shape)]
    for which in range(4):
        for k in order:
            outs.append(res[k][which])
    return tuple(outs)
```

```python
import functools

import jax
import jax.numpy as jnp
from jax import lax
from jax.experimental import pallas as pl
from jax.experimental.pallas import tpu as pltpu

F32 = jnp.float32
BF16 = jnp.bfloat16
MESH = pl.DeviceIdType.MESH

N_DEV = 8
D = 1024
N_GROUPS = 4
GW = D // N_GROUPS
D_IN = 5 * D
D_FF = 4 * D
POOL_WINDOWS = (2, 4, 8, 16)
HALO_X = 8
HALO_U = 16
EPS = 1e-6
C_RG = 8.0
ADAM_LR, ADAM_B1, ADAM_B2, ADAM_EPS, ADAM_WD, ADAM_STEP = 0.001, 0.9, 0.999, 1e-08, 0.01, 10

V7X_VMEM_LIMIT = 56 * 1024 * 1024

V_CONV_W, V_CONV_B, V_B_RG_A, V_B_RG_X, V_A_PARAM, V_B_POOL, V_POOL_SCALE, V_G1, V_G2, V_GF = 0, 4, 5, 6, 7, 8, 9, 10, 11, 12
M_SH1, M_SC1, M_GT1, M_SH2, M_SC2, M_GT2 = 0, 1, 2, 3, 4, 5

TM_PROJ = 512
TM_MIX = 256
TM_BRANCH = 512
TM_MLP = 512
TM_MLP_BWD = 256
TS_WGRAD = 1024


def _params(semantics):
    return pltpu.CompilerParams(dimension_semantics=semantics, vmem_limit_bytes=V7X_VMEM_LIMIT)


def _resident(shape):
    return pl.BlockSpec(shape, lambda *_: (0,) * len(shape), pipeline_mode=pl.Buffered(1))


def _dot(a, b):
    return jnp.dot(a, b, preferred_element_type=F32)


def _dot_nt(a, b):
    return lax.dot_general(a, b, (((1,), (1,)), ((), ())), preferred_element_type=F32)


def _dot_tn(a, b):
    return lax.dot_general(a, b, (((0,), (0,)), ((), ())), preferred_element_type=F32)


def _sigmoid(x):
    return 0.5 * jnp.tanh(0.5 * x) + 0.5


def _sigmoid_tail(x):
    return 1.0 / (1.0 + jnp.exp(-x))


def _gelu_and_grad(x):
    k = 0.7978845608028654
    x2 = x * x
    t = jnp.tanh(k * (x + 0.044715 * x * x2))
    g = 0.5 * x * (1.0 + t)
    dg = 0.5 * (1.0 + t) + 0.5 * x * (1.0 - t * t) * (k * (1.0 + 3.0 * 0.044715 * x2))
    return g, dg


def _softplus(a):
    e = jnp.exp(-jnp.abs(a))
    u = 1.0 + e
    log1p_e = jnp.where(u == 1.0, e, jnp.log(u) * e / jnp.where(u == 1.0, 1.0, u - 1.0))
    return jnp.maximum(a, 0.0) + log1p_e


def _neg_expm1(z):
    series = -(z * (1.0 + z * (0.5 + z * (1.0 / 6.0 + z * (1.0 / 24.0 + z * (1.0 / 120.0))))))
    return jnp.where(z > -0.1, series, 1.0 - jnp.exp(z))


def _shift_down(x, k):
    return pltpu.roll(x, k, 0)


def _shift_up(x, k):
    return pltpu.roll(x, x.shape[0] - k, 0)


def _rglru_gates(xr, w_a, w_x, b_a, b_x, a_param, is_t0):
    xb = xr.astype(BF16)
    ra = _sigmoid(_dot(xb, w_a) + b_a)
    ri = _sigmoid(_dot(xb, w_x) + b_x)
    sp = _softplus(a_param)
    log_a = (-C_RG) * ra * sp
    a = jnp.exp(log_a)
    mult = jnp.where(is_t0, 1.0, jnp.sqrt(_neg_expm1(2.0 * log_a)))
    return ra, ri, sp, a, mult


SUBLANES = 8


LANES = 128


def _scan_strip(a, b, carry, scr, down):
    t = b.shape[0]
    g = t // SUBLANES
    a3 = a.reshape(g, SUBLANES, LANES)
    b3 = b.reshape(g, SUBLANES, LANES)
    sub = lax.broadcasted_iota(jnp.int32, (g, SUBLANES, LANES), 1)
    for k in (1, 2, 4):
        keep = sub >= k if down else sub < SUBLANES - k
        shift = k if down else SUBLANES - k
        b3 = b3 + a3 * jnp.where(keep, pltpu.roll(b3, shift, 1), 0.0)
        a3 = a3 * jnp.where(keep, pltpu.roll(a3, shift, 1), 1.0)
    scr[0] = a3.reshape(t, LANES)
    scr[1] = b3.reshape(t, LANES)
    end_row = SUBLANES - 1 if down else 0
    ag = scr[0, pl.ds(end_row, g, stride=SUBLANES), :]
    bg = scr[1, pl.ds(end_row, g, stride=SUBLANES), :]
    rg = lax.broadcasted_iota(jnp.int32, (g, LANES), 0)
    edge = 0 if down else g - 1
    bg = bg + jnp.where(rg == edge, ag * carry, 0.0)
    k = 1
    while k < g:
        keep = rg >= k if down else rg < g - k
        shift = k if down else g - k
        bg = bg + ag * jnp.where(keep, pltpu.roll(bg, shift, 0), 0.0)
        if 2 * k < g:
            ag = ag * pltpu.roll(ag, shift, 0)
        k *= 2
    entering = jnp.where(rg != edge, pltpu.roll(bg, 1 if down else g - 1, 0), carry)
    for r in range(SUBLANES):
        scr[2, pl.ds(r, g, stride=SUBLANES), :] = entering
    return scr[1] + scr[0] * scr[2], bg[g - 1:g, :]


def _scan_strips(a, b, carry, scr, down):
    outs = [_scan_strip(a[:, c:c + LANES], b[:, c:c + LANES], carry[:, c:c + LANES], scr, down)
            for c in range(0, b.shape[1], LANES)]
    return jnp.concatenate([o[0] for o in outs], axis=1), jnp.concatenate([o[1] for o in outs], axis=1)


def _scan_down(a, b, carry, scr):
    return _scan_strips(a, b, carry, scr, True)


def _scan_up(m, b, carry, scr):
    return _scan_strips(m, b, carry, scr, False)[0]


def _window_mean(sums, window, first_block, head_t):
    scaled = sums * (1.0 / window)
    head = jnp.where(first_block, sums[:HALO_U] / jnp.minimum(head_t, float(window)), scaled[:HALO_U])
    return jnp.concatenate([head, scaled[HALO_U:]], axis=0)


def _conv_taps(x_ext):
    return [_shift_down(x_ext, 3 - j)[HALO_X:] if j < 3 else x_ext[HALO_X:] for j in range(4)]


def _proj_fwd(x, modr, vecs, w_in):
    s = x.shape[0]
    tm = min(TM_PROJ, s)

    def body(x_ref, mod_ref, vec_ref, w_ref, h1_ref, xrnn_ref, u_ref, ga_ref, dga_ref, sa_ref, sb_ref):
        xv = x_ref[...]
        r = lax.rsqrt(jnp.mean(xv * xv, axis=-1, keepdims=True) + EPS)
        gain = vec_ref[V_G1:V_G1 + 1, :] * (1.0 + mod_ref[M_SC1:M_SC1 + 1, :])
        h = (xv * r * gain + mod_ref[M_SH1:M_SH1 + 1, :]).astype(BF16)
        h1_ref[...] = h
        xrnn_ref[...] = _dot(h, w_ref[:, 0:D])
        ga_ref[...], dga_ref[...] = _gelu_and_grad(_dot(h, w_ref[:, D:2 * D]))
        u_ref[...] = _dot(h, w_ref[:, 2 * D:3 * D])
        sa_ref[...] = _sigmoid(_dot(h, w_ref[:, 3 * D:4 * D]))
        sb_ref[...] = _sigmoid(_dot(h, w_ref[:, 4 * D:5 * D]))

    tok = pl.BlockSpec((tm, D), lambda i: (i, 0))
    sd = lambda dt: jax.ShapeDtypeStruct((s, D), dt)
    return pl.pallas_call(
        body, name="proj_fwd", grid=(s // tm,),
        in_specs=[tok, pl.BlockSpec((8, D), lambda i: (0, 0)), pl.BlockSpec((16, D), lambda i: (0, 0)),
                  _resident((D, D_IN))],
        out_specs=[tok] * 7,
        out_shape=[sd(BF16)] + [sd(F32)] * 6,
        compiler_params=_params(("parallel",)),
    )(x, modr, vecs, w_in)


def _mix_fwd(x_rnn, u_pool, ga, vecs, w_rg_a, w_rg_x, w_pool, dep):
    s = x_rnn.shape[0]
    tm = min(TM_MIX, s)
    nb = s // tm

    def body(xh_ref, x_ref, uh_ref, u_ref, ga_ref, vec_ref, wa_ref, wx_ref, wp_ref, dep_ref,
             xr_ref, hr_ref, za_ref, p_ref, pooled_ref, a_ref, mult_ref, ra_ref, ri_ref, carry_ref, scan_scr):
        i = pl.program_id(0)
        first = i == 0

        @pl.when(first)
        def _():
            carry_ref[...] = jnp.zeros_like(carry_ref)

        row = lax.broadcasted_iota(jnp.int32, (tm, GW), 0)
        is_t0 = jnp.logical_and(first, row == 0)
        head_t = (lax.broadcasted_iota(jnp.int32, (HALO_U, GW), 0) + 1).astype(F32)
        for g in range(N_GROUPS):
            cs = slice(g * GW, (g + 1) * GW)
            vec = vec_ref[:, cs]
            xh = jnp.where(first, 0.0, xh_ref[:, cs])
            taps = _conv_taps(jnp.concatenate([xh, x_ref[:, cs]], axis=0))
            xr = vec[V_CONV_B:V_CONV_B + 1]
            for j in range(4):
                xr = xr + vec[V_CONV_W + j:V_CONV_W + j + 1] * taps[j]
            xr_ref[:, cs] = xr
            ra, ri, _, a, mult = _rglru_gates(
                xr, wa_ref[g], wx_ref[g], vec[V_B_RG_A:V_B_RG_A + 1], vec[V_B_RG_X:V_B_RG_X + 1],
                vec[V_A_PARAM:V_A_PARAM + 1], is_t0)
            a_ref[:, cs] = a
            mult_ref[:, cs] = mult
            ra_ref[:, cs] = ra.astype(BF16)
            ri_ref[:, cs] = ri.astype(BF16)
            h, last = _scan_down(a, xr * ri * mult, carry_ref[0:1, cs], scan_scr)
            hr_ref[:, cs] = h
            carry_ref[0:1, cs] = last
            za_ref[:, cs] = (ga_ref[:, cs] * h).astype(BF16)
            uh = jnp.where(first, 0.0, uh_ref[:, cs])
            sm = jnp.concatenate([uh, u_ref[:, cs]], axis=0)
            k = 1
            while k < POOL_WINDOWS[g]:
                sm = sm + _shift_down(sm, k)
                k *= 2
            mean = _window_mean(sm[HALO_U:], POOL_WINDOWS[g], first, head_t)
            p = (mean - u_ref[:, cs]).astype(BF16)
            p_ref[:, cs] = p
            pb = _dot(p, wp_ref[g]) + vec[V_B_POOL:V_B_POOL + 1]
            pooled_ref[:, cs] = (pb * vec[V_POOL_SCALE:V_POOL_SCALE + 1]).astype(BF16)

    tok = pl.BlockSpec((tm, D), lambda i: (i, 0))
    halo = lambda rows: pl.BlockSpec((rows, D), lambda i: (jnp.maximum(i * (tm // rows) - 1, 0), 0))
    wspec = pl.BlockSpec((N_GROUPS, GW, GW), lambda i: (0, 0, 0))
    sd = lambda dt: jax.ShapeDtypeStruct((s, D), dt)
    return pl.pallas_call(
        body, name="mix_fwd", grid=(nb,),
        in_specs=[halo(HALO_X), tok, halo(HALO_U), tok, tok, pl.BlockSpec((16, D), lambda i: (0, 0)),
                  wspec, wspec, wspec, pl.BlockSpec(memory_space=pl.ANY)],
        out_specs=[tok] * 9,
        out_shape=[sd(F32), sd(F32), sd(BF16), sd(BF16), sd(BF16), sd(F32), sd(F32), sd(BF16), sd(BF16)],
        scratch_shapes=[pltpu.VMEM((8, D), F32), pltpu.VMEM((3, tm, LANES), F32)],
        compiler_params=_params(("arbitrary",)),
    )(x_rnn, x_rnn, u_pool, u_pool, ga, vecs, w_rg_a, w_rg_x, w_pool, dep)


def _branch_fwd(za, pooled, sa, sb, x, modr, vecs, w_a, w_b, w_out):
    s = x.shape[0]
    tm = min(TM_BRANCH, s)

    def body(za_ref, pooled_ref, sa_ref, sb_ref, x_ref, mod_ref, vec_ref, wa_ref, wb_ref, wo_ref,
             ba_ref, bb_ref, merged_ref, o_ref, x2_ref, h2_ref):
        ba = _dot(za_ref[...], wa_ref[...])
        bb = _dot(pooled_ref[...], wb_ref[...])
        ba_ref[...] = ba.astype(BF16)
        bb_ref[...] = bb.astype(BF16)
        merged = (sa_ref[...] * ba + sb_ref[...] * bb).astype(BF16)
        merged_ref[...] = merged
        o = _dot(merged, wo_ref[...])
        o_ref[...] = o.astype(BF16)
        x2 = x_ref[...] + mod_ref[M_GT1:M_GT1 + 1, :] * o
        x2_ref[...] = x2
        r = lax.rsqrt(jnp.mean(x2 * x2, axis=-1, keepdims=True) + EPS)
        gain = vec_ref[V_G2:V_G2 + 1, :] * (1.0 + mod_ref[M_SC2:M_SC2 + 1, :])
        h2_ref[...] = (x2 * r * gain + mod_ref[M_SH2:M_SH2 + 1, :]).astype(BF16)

    tok = pl.BlockSpec((tm, D), lambda i: (i, 0))
    wspec = pl.BlockSpec((D, D), lambda i: (0, 0))
    sd = lambda dt: jax.ShapeDtypeStruct((s, D), dt)
    return pl.pallas_call(
        body, name="branch_fwd", grid=(s // tm,),
        in_specs=[tok, tok, tok, tok,
                  tok, pl.BlockSpec((8, D), lambda i: (0, 0)), pl.BlockSpec((16, D), lambda i: (0, 0)),
                  wspec, wspec, wspec],
        out_specs=[tok] * 6,
        out_shape=[sd(BF16), sd(BF16), sd(BF16), sd(BF16), sd(F32), sd(BF16)],
        compiler_params=_params(("parallel",)),
    )(za, pooled, sa, sb, x, modr, vecs, w_a, w_b, w_out)


def _mlp_fwd(h2, x2, target, modr, vecs, w_up, w_down):
    s = x2.shape[0]
    tm = min(TM_MLP, s)

    def body(h2_ref, x2_ref, tgt_ref, mod_ref, vec_ref, wu_ref, wd_ref,
             ru_ref, dx3_ref, ddn_ref, small_ref):
        @pl.when(pl.program_id(0) == 0)
        def _():
            small_ref[...] = jnp.zeros_like(small_ref)

        h2 = h2_ref[...]
        dn = None
        for c in range(D_FF // D):
            cs = slice(c * D, (c + 1) * D)
            ru = jnp.maximum(_dot(h2, wu_ref[:, cs]), 0.0)
            ru_ref[:, cs] = ru.astype(BF16)
            part = _dot((ru * ru).astype(BF16), wd_ref[cs, :])
            dn = part if dn is None else dn + part
        gt2 = mod_ref[M_GT2:M_GT2 + 1, :]
        gf = vec_ref[V_GF:V_GF + 1, :]
        x3 = x2_ref[...] + gt2 * dn
        r3 = lax.rsqrt(jnp.mean(x3 * x3, axis=-1, keepdims=True) + EPS)
        n3 = x3 * r3
        err = n3 * gf - tgt_ref[...]
        dy = err * (1.0 / D)
        dn3 = dy * gf
        dx3 = r3 * (dn3 - n3 * jnp.mean(dn3 * n3, axis=-1, keepdims=True))
        dx3_ref[...] = dx3
        ddn_ref[...] = (dx3 * gt2).astype(BF16)
        small_ref[0:1, :] += jnp.sum(dy * n3, axis=0, keepdims=True)
        small_ref[1:2, :] += jnp.sum(dx3 * dn, axis=0, keepdims=True)
        small_ref[2:3, :] += (0.5 / D) * jnp.sum(err * err, axis=0, keepdims=True)

    tok = pl.BlockSpec((tm, D), lambda i: (i, 0))
    return pl.pallas_call(
        body, name="mlp_fwd", grid=(s // tm,),
        in_specs=[tok, tok, tok,
                  pl.BlockSpec((8, D), lambda i: (0, 0)), pl.BlockSpec((16, D), lambda i: (0, 0)),
                  _resident((D, D_FF)), _resident((D_FF, D))],
        out_specs=[pl.BlockSpec((tm, D_FF), lambda i: (i, 0)), tok, tok,
                   pl.BlockSpec((8, D), lambda i: (0, 0))],
        out_shape=[jax.ShapeDtypeStruct((s, D_FF), BF16), jax.ShapeDtypeStruct((s, D), F32),
                   jax.ShapeDtypeStruct((s, D), BF16), jax.ShapeDtypeStruct((8, D), F32)],
        compiler_params=_params(("arbitrary",)),
    )(h2, x2, target, modr, vecs, w_up, w_down)


def _mlp_bwd(d_dn, ru, x2, dx3, o, modr, vecs, w_up, w_down):
    s = x2.shape[0]
    tm = min(TM_MLP_BWD, s)

    def body(ddn_ref, ru_ref, x2_ref, dx3_ref, o_ref, mod_ref, vec_ref, wu_ref, wd_ref,
             dup_ref, dx2_ref, do_ref, small_ref):
        @pl.when(pl.program_id(0) == 0)
        def _():
            small_ref[...] = jnp.zeros_like(small_ref)

        ddn = ddn_ref[...]
        dh2 = None
        for c in range(D_FF // D):
            cs = slice(c * D, (c + 1) * D)
            dff = _dot_nt(ddn, wd_ref[cs, :])
            dup = (dff * (2.0 * ru_ref[:, cs].astype(F32))).astype(BF16)
            dup_ref[:, cs] = dup
            part = _dot_nt(dup, wu_ref[:, cs])
            dh2 = part if dh2 is None else dh2 + part
        x2 = x2_ref[...]
        r2 = lax.rsqrt(jnp.mean(x2 * x2, axis=-1, keepdims=True) + EPS)
        xn2 = x2 * r2
        gain = vec_ref[V_G2:V_G2 + 1, :] * (1.0 + mod_ref[M_SC2:M_SC2 + 1, :])
        dxn2 = dh2 * gain
        dx2 = dx3_ref[...] + r2 * (dxn2 - xn2 * jnp.mean(dxn2 * xn2, axis=-1, keepdims=True))
        dx2_ref[...] = dx2
        do_ref[...] = (dx2 * mod_ref[M_GT1:M_GT1 + 1, :]).astype(BF16)
        small_ref[0:1, :] += jnp.sum(dh2, axis=0, keepdims=True)
        small_ref[1:2, :] += jnp.sum(dh2 * xn2, axis=0, keepdims=True)
        small_ref[2:3, :] += jnp.sum(dx2 * o_ref[...].astype(F32), axis=0, keepdims=True)

    tok = pl.BlockSpec((tm, D), lambda i: (i, 0))
    wide = pl.BlockSpec((tm, D_FF), lambda i: (i, 0))
    return pl.pallas_call(
        body, name="mlp_bwd", grid=(s // tm,),
        in_specs=[tok, wide, tok, tok, tok,
                  pl.BlockSpec((8, D), lambda i: (0, 0)), pl.BlockSpec((16, D), lambda i: (0, 0)),
                  _resident((D, D_FF)), _resident((D_FF, D))],
        out_specs=[wide, tok, tok, pl.BlockSpec((8, D), lambda i: (0, 0))],
        out_shape=[jax.ShapeDtypeStruct((s, D_FF), BF16), jax.ShapeDtypeStruct((s, D), F32),
                   jax.ShapeDtypeStruct((s, D), BF16), jax.ShapeDtypeStruct((8, D), F32)],
        compiler_params=_params(("arbitrary",)),
    )(d_dn, ru, x2, dx3, o, modr, vecs, w_up, w_down)


def _branch_bwd(do, sa, sb, ba, bb, w_a, w_b, w_out, dep):
    s = do.shape[0]
    tm = min(TM_BRANCH, s)

    def body(do_ref, sa_ref, sb_ref, ba_ref, bb_ref, wa_ref, wb_ref, wo_ref, dep_ref,
             dba_ref, dbb_ref, dg_ref, dza_ref, dpooled_ref):
        dmerged = _dot_nt(do_ref[...], wo_ref[...])
        sa = sa_ref[...]
        sb = sb_ref[...]
        dba = (dmerged * sa).astype(BF16)
        dbb = (dmerged * sb).astype(BF16)
        dba_ref[...] = dba
        dbb_ref[...] = dbb
        dg_ref[:, :D] = (dmerged * ba_ref[...].astype(F32) * sa * (1.0 - sa)).astype(BF16)
        dg_ref[:, D:] = (dmerged * bb_ref[...].astype(F32) * sb * (1.0 - sb)).astype(BF16)
        dza_ref[...] = _dot_nt(dba, wa_ref[...])
        dpooled_ref[...] = _dot_nt(dbb, wb_ref[...])

    tok = pl.BlockSpec((tm, D), lambda i: (i, 0))
    wspec = pl.BlockSpec((D, D), lambda i: (0, 0))
    sd = lambda dt: jax.ShapeDtypeStruct((s, D), dt)
    return pl.pallas_call(
        body, name="branch_bwd", grid=(s // tm,),
        in_specs=[tok, tok, tok, tok, tok, wspec, wspec, wspec, pl.BlockSpec(memory_space=pl.ANY)],
        out_specs=[tok, tok, pl.BlockSpec((tm, 2 * D), lambda i: (i, 0)), tok, tok],
        out_shape=[sd(BF16), sd(BF16), jax.ShapeDtypeStruct((s, 2 * D), BF16), sd(F32), sd(F32)],
        compiler_params=_params(("parallel",)),
    )(do, sa, sb, ba, bb, w_a, w_b, w_out, dep)


def _mix_bwd(dza, dpooled, x_rnn, ga, dga, xr, hr, p, gates, dgates, vecs, w_rg_a, w_rg_x, w_pool, dep):
    s = xr.shape[0]
    tm = min(TM_MIX, s)
    nb = s // tm

    def body(dza_ref, dpooled_ref, xh_ref, x_ref, ga_ref, dga_ref, xr_ref, hh_ref, hr_ref, p_ref,
             a_ref, mult_ref, ra_ref, ri_ref, dg_ref, vec_ref, wa_ref, wx_ref, wp_ref, dep_ref,
             dproj_ref, dwa_ref, dwx_ref, dwp_ref, small_ref,
             scan_carry, dxr_carry, q_carry, scan_scr, dwa_acc, dwx_acc, dwp_acc):
        i = pl.program_id(0)
        bi = nb - 1 - i
        first_t = bi == 0

        @pl.when(i == 0)
        def _():
            scan_carry[...] = jnp.zeros_like(scan_carry)
            dxr_carry[...] = jnp.zeros_like(dxr_carry)
            q_carry[...] = jnp.zeros_like(q_carry)
            dwa_acc[...] = jnp.zeros_like(dwa_acc)
            dwx_acc[...] = jnp.zeros_like(dwx_acc)
            dwp_acc[...] = jnp.zeros_like(dwp_acc)
            small_ref[...] = jnp.zeros_like(small_ref)

        row = lax.broadcasted_iota(jnp.int32, (tm, GW), 0)
        is_t0 = jnp.logical_and(first_t, row == 0)
        head_t = (lax.broadcasted_iota(jnp.int32, (HALO_U, GW), 0) + 1).astype(F32)
        colsum = lambda v: jnp.sum(v, axis=0, keepdims=True)
        for g in range(N_GROUPS):
            cs = slice(g * GW, (g + 1) * GW)
            vec = vec_ref[:, cs]
            xr = xr_ref[:, cs]
            hr = hr_ref[:, cs]
            dza = dza_ref[:, cs]
            dproj_ref[:, D + g * GW:D + (g + 1) * GW] = (dza * hr * dga_ref[:, cs]).astype(BF16)
            dhr = dza * ga_ref[:, cs]
            a = a_ref[:, cs]
            mult = mult_ref[:, cs]
            ra = ra_ref[:, cs].astype(F32)
            ri = ri_ref[:, cs].astype(F32)
            sp = _softplus(vec[V_A_PARAM:V_A_PARAM + 1])
            m = jnp.where(row == tm - 1, 1.0, _shift_up(a, 1))
            gsum = _scan_up(m, dhr, scan_carry[0:1, cs], scan_scr)
            scan_carry[0:1, cs] = a[0:1, :] * gsum[0:1, :]
            hh = jnp.where(first_t, 0.0, hh_ref[:, cs])
            hprev = _shift_down(jnp.concatenate([hh, hr], axis=0), 1)[8:]
            da = gsum * hprev
            dmult = jnp.where(is_t0, 0.0, gsum * xr * ri)
            dlog_a = da * a - dmult * a * a / mult
            dri = gsum * xr * mult
            dxr = gsum * ri * mult
            small_ref[7:8, cs] += colsum((-C_RG) * ra * dlog_a)
            dpa = (((-C_RG) * sp) * dlog_a * ra * (1.0 - ra))
            dpx = dri * ri * (1.0 - ri)
            small_ref[5:6, cs] += colsum(dpa)
            small_ref[6:7, cs] += colsum(dpx)
            dpa = dpa.astype(BF16)
            dpx = dpx.astype(BF16)
            xrb = xr.astype(BF16)
            dwa_acc[g] += _dot_tn(xrb, dpa)
            dwx_acc[g] += _dot_tn(xrb, dpx)
            dxr = dxr + _dot_nt(dpa, wa_ref[g]) + _dot_nt(dpx, wx_ref[g])
            small_ref[4:5, cs] += colsum(dxr)
            xh = jnp.where(first_t, 0.0, xh_ref[:, cs])
            taps = _conv_taps(jnp.concatenate([xh, x_ref[:, cs]], axis=0))
            dxr_ext = jnp.concatenate([dxr, dxr_carry[:, cs]], axis=0)
            dx = vec[V_CONV_W + 3:V_CONV_W + 4] * dxr
            for j in range(4):
                small_ref[j:j + 1, cs] += colsum(dxr * taps[j])
                if j < 3:
                    dx = dx + vec[V_CONV_W + j:V_CONV_W + j + 1] * _shift_up(dxr_ext, 3 - j)[:tm]
            dxr_carry[:, cs] = dxr[0:8, :]
            dproj_ref[:, cs] = dx.astype(BF16)
            pg = p_ref[:, cs]
            dpooled = dpooled_ref[:, cs]
            pb = _dot(pg, wp_ref[g]) + vec[V_B_POOL:V_B_POOL + 1]
            small_ref[9:10, cs] += colsum(dpooled * pb)
            dpb = dpooled * vec[V_POOL_SCALE:V_POOL_SCALE + 1]
            small_ref[8:9, cs] += colsum(dpb)
            dpbb = dpb.astype(BF16)
            dwp_acc[g] += _dot_tn(pg, dpbb)
            dp = _dot_nt(dpbb, wp_ref[g])
            q = _window_mean(dp, POOL_WINDOWS[g], first_t, head_t)
            sm = jnp.concatenate([q, q_carry[:, cs]], axis=0)
            k = 1
            while k < POOL_WINDOWS[g]:
                sm = sm + _shift_up(sm, k)
                k *= 2
            q_carry[:, cs] = q[0:HALO_U, :]
            dproj_ref[:, 2 * D + g * GW:2 * D + (g + 1) * GW] = (sm[:tm] - dp).astype(BF16)
        dproj_ref[:, 3 * D:] = dg_ref[...]

        @pl.when(i == nb - 1)
        def _():
            dwa_ref[...] = dwa_acc[...].astype(BF16)
            dwx_ref[...] = dwx_acc[...].astype(BF16)
            dwp_ref[...] = dwp_acc[...].astype(BF16)

    rev = lambda i: nb - 1 - i
    tok = pl.BlockSpec((tm, D), lambda i: (rev(i), 0))
    halo8 = lambda k: pl.BlockSpec((8, D), lambda i: (jnp.maximum(rev(i) * (tm // 8) - 1, 0), k))
    wspec = pl.BlockSpec((N_GROUPS, GW, GW), lambda i: (0, 0, 0))
    wshape = jax.ShapeDtypeStruct((N_GROUPS, GW, GW), BF16)
    return pl.pallas_call(
        body, name="mix_bwd", grid=(nb,),
        in_specs=[tok, tok, halo8(0), tok, tok, tok, tok, halo8(0), tok, tok, tok, tok, tok, tok,
                  pl.BlockSpec((tm, 2 * D), lambda i: (rev(i), 0)),
                  pl.BlockSpec((16, D), lambda i: (0, 0)), wspec, wspec, wspec, pl.BlockSpec(memory_space=pl.ANY)],
        out_specs=[pl.BlockSpec((tm, D_IN), lambda i: (rev(i), 0)), wspec, wspec, wspec,
                   pl.BlockSpec((16, D), lambda i: (0, 0))],
        out_shape=[jax.ShapeDtypeStruct((s, D_IN), BF16), wshape, wshape, wshape,
                   jax.ShapeDtypeStruct((16, D), F32)],
        scratch_shapes=[pltpu.VMEM((8, D), F32), pltpu.VMEM((8, D), F32), pltpu.VMEM((HALO_U, D), F32),
                        pltpu.VMEM((3, tm, LANES), F32)] + [pltpu.VMEM((N_GROUPS, GW, GW), F32)] * 3,
        compiler_params=_params(("arbitrary",)),
    )(dza, dpooled, x_rnn, x_rnn, ga, dga, xr, hr, hr, p, *gates, dgates, vecs, w_rg_a, w_rg_x, w_pool, dep)


def _proj_bwd(dproj, x, dx2, modr, vecs, w_in, dep):
    s = x.shape[0]
    tm = min(TM_PROJ, s)

    def body(dp_ref, x_ref, dx2_ref, mod_ref, vec_ref, w_ref, dep_ref, gx_ref, small_ref):
        @pl.when(pl.program_id(0) == 0)
        def _():
            small_ref[...] = jnp.zeros_like(small_ref)

        dh1 = None
        for c in range(D_IN // D):
            cs = slice(c * D, (c + 1) * D)
            part = _dot_nt(dp_ref[:, cs], w_ref[:, cs])
            dh1 = part if dh1 is None else dh1 + part
        xv = x_ref[...]
        r1 = lax.rsqrt(jnp.mean(xv * xv, axis=-1, keepdims=True) + EPS)
        xn1 = xv * r1
        gain = vec_ref[V_G1:V_G1 + 1, :] * (1.0 + mod_ref[M_SC1:M_SC1 + 1, :])
        dxn1 = dh1 * gain
        gx_ref[...] = dx2_ref[...] + r1 * (dxn1 - xn1 * jnp.mean(dxn1 * xn1, axis=-1, keepdims=True))
        small_ref[0:1, :] += jnp.sum(dh1, axis=0, keepdims=True)
        small_ref[1:2, :] += jnp.sum(dh1 * xn1, axis=0, keepdims=True)

    tok = pl.BlockSpec((tm, D), lambda i: (i, 0))
    return pl.pallas_call(
        body, name="proj_bwd", grid=(s // tm,),
        in_specs=[pl.BlockSpec((tm, D_IN), lambda i: (i, 0)), tok, tok,
                  pl.BlockSpec((8, D), lambda i: (0, 0)), pl.BlockSpec((16, D), lambda i: (0, 0)),
                  _resident((D, D_IN)), pl.BlockSpec(memory_space=pl.ANY)],
        out_specs=[tok, pl.BlockSpec((8, D), lambda i: (0, 0))],
        out_shape=[jax.ShapeDtypeStruct((s, D), F32), jax.ShapeDtypeStruct((8, D), F32)],
        compiler_params=_params(("arbitrary",)),
    )(dproj, x, dx2, modr, vecs, w_in, dep)


def _wgrad(a, b, name, square_a=False, dep=None):
    s, ka = a.shape
    n = b.shape[1]
    tka = ka if ka <= 1024 else ka // 2
    tn = n if n <= 1024 else n // 2
    ts = min(TS_WGRAD, s)
    ns = s // ts
    nc = 512
    deps = [] if dep is None else [dep]

    def body(a_ref, b_ref, *refs):
        out_ref, acc_ref = refs[-2:]
        t = pl.program_id(2)

        @pl.when(t == 0)
        def _():
            acc_ref[...] = jnp.zeros_like(acc_ref)

        av = a_ref[...]
        if square_a:
            af = av.astype(F32)
            av = (af * af).astype(BF16)
        for c in range(tn // nc):
            cs = slice(c * nc, (c + 1) * nc)
            acc_ref[:, cs] += _dot_tn(av, b_ref[:, cs])

        @pl.when(t == ns - 1)
        def _():
            out_ref[...] = acc_ref[...].astype(BF16)

    return pl.pallas_call(
        body, name=name, grid=(ka // tka, n // tn, ns),
        in_specs=[pl.BlockSpec((ts, tka), lambda i, j, t: (t, i)),
                  pl.BlockSpec((ts, tn), lambda i, j, t: (t, j))] + [pl.BlockSpec(memory_space=pl.ANY)] * len(deps),
        out_specs=pl.BlockSpec((tka, tn), lambda i, j, t: (i, j)),
        out_shape=jax.ShapeDtypeStruct((ka, n), BF16),
        scratch_shapes=[pltpu.VMEM((tka, tn), F32)],
        compiler_params=_params(("parallel", "parallel", "arbitrary")),
    )(a, b, *deps)


def _window(ref, kind, idx, size):
    start = pl.multiple_of(idx * size, size)
    if kind == 0:
        return ref.at[pl.ds(start, size)]
    if kind == 1:
        return ref.at[:, pl.ds(start, size)]
    return ref.at[:, :, pl.ds(start, size)]


def _mesh_place():
    x, y, c = lax.axis_index("x"), lax.axis_index("y"), lax.axis_index("c")
    return x, y, c, 4 * x + 2 * y + c


def _peer(x, y, c, q):
    px = 1 - x if q & 4 else x
    py = 1 - y if q & 2 else y
    pc = 1 - c if q & 1 else c
    return (px, py, pc), 4 * px + 2 * py + pc


_HBM = pl.BlockSpec(memory_space=pltpu.HBM)
_SEM = pl.BlockSpec(memory_space=pltpu.SEMAPHORE)
_EFFECT = pltpu.SideEffectType.DATAFLOW_SIDE_EFFECTING


N_NEAR = 4


def _near(x, y, c):
    out = [((x, y, 1 - c), 4 * x + 2 * y + 1 - c)]
    for j in (1, 2, 3):
        px = 1 - x if j & 2 else x
        py = 1 - y if j & 1 else y
        out.append(((px, py, c), 4 * px + 2 * py + c))
    return out


def _remote(src, dst, send_sems, recv_sems, slot, device):
    return pltpu.make_async_remote_copy(src_ref=src, dst_ref=dst, send_sem=send_sems.at[slot], recv_sem=recv_sems.at[slot],
                                        device_id=device, device_id_type=MESH)


def _split_call(name, arrays, sems_in, n_new_sems, after, emit):
    na, ns, nn = len(arrays), len(sems_in), len(n_new_sems)

    def body(*refs):
        emit(refs[:na], refs[na:na + ns], refs[na + ns + 1:na + ns + 1 + nn])
        refs[-1][...] = jnp.zeros_like(refs[-1])

    outs = pl.pallas_call(
        body, name=name,
        out_shape=(*[pltpu.SemaphoreType.DMA((m,)) for m in n_new_sems],
                   *[pltpu.HBM(a.shape, a.dtype) for a in arrays], jax.ShapeDtypeStruct((8, 128), F32)),
        in_specs=[_HBM] * na + [_SEM] * ns + [pl.BlockSpec(memory_space=pl.ANY)],
        out_specs=(*[_SEM] * nn, *[_HBM] * na, pl.BlockSpec(memory_space=pltpu.VMEM)),
        input_output_aliases={i: nn + i for i in range(na)},
        compiler_params=pltpu.CompilerParams(has_side_effects=_EFFECT),
    )(*[pltpu.with_memory_space_constraint(a, pltpu.HBM) for a in arrays], *sems_in, after)
    return list(outs[:nn]), list(outs[nn:nn + na]), outs[-1]


class _Gather:
    def __init__(self, shards, kinds, after, name):
        self.n, self.kinds, self.name = len(shards), kinds, name
        self.sizes = [s.shape[k] for s, k in zip(shards, kinds)]
        n = self.n
        lands = []
        for s, k in zip(shards, kinds):
            dims = list(s.shape)
            dims[k] *= N_DEV
            lands.append(lax.empty(tuple(dims), s.dtype))

        def emit(arr, _, new):
            x, y, c, me = _mesh_place()
            for k in range(n):
                pltpu.make_async_copy(arr[k], _window(arr[n + k], kinds[k], me, self.sizes[k]), new[2].at[k]).start()
            for k in range(n):
                mine = _window(arr[n + k], kinds[k], me, self.sizes[k])
                for j, (dev, _) in enumerate(_near(x, y, c)):
                    _remote(arr[k], mine, new[0], new[1], k * N_NEAR + j, dev).start()

        self.sems, self.arrays, self.token = _split_call(name + "_start", [*shards, *lands], [],
                                                         [n * N_NEAR, n * N_NEAR, n], after, emit)

    def forward(self, after):
        n, kinds, sizes = self.n, self.kinds, self.sizes

        def emit(arr, old, new):
            x, y, c, _ = _mesh_place()
            near = _near(x, y, c)
            for k in range(n):
                for j in (1, 2, 3):
                    dev, idx = near[j]
                    landed = _window(arr[n + k], kinds[k], idx, sizes[k])
                    _remote(arr[k], landed, old[0], old[1], k * N_NEAR + j, dev).wait_recv()
                    _remote(landed, landed, new[0], new[1], k * N_NEAR + j, near[0][0]).start()

        new, self.arrays, self.token = _split_call(self.name + "_forward", self.arrays, self.sems, [n * N_NEAR] * 2,
                                                   after, emit)
        self.sems = [*self.sems, *new]

    def finish(self, after):
        n, kinds, sizes = self.n, self.kinds, self.sizes

        def emit(arr, old, _):
            x, y, c, me = _mesh_place()
            near = _near(x, y, c)
            other_core = near[0][0]
            for k in range(n):
                win = lambda idx: _window(arr[n + k], kinds[k], idx, sizes[k])
                pltpu.make_async_copy(arr[k], win(me), old[2].at[k]).wait()
                for j, (dev, idx) in enumerate(near):
                    _remote(arr[k], win(me), old[0], old[1], k * N_NEAR + j, dev).wait_send()
                _remote(arr[k], win(near[0][1]), old[0], old[1], k * N_NEAR, other_core).wait_recv()
                for j in (1, 2, 3):
                    idx = near[j][1]
                    _remote(win(idx), win(idx), old[3], old[4], k * N_NEAR + j, other_core).wait_send()
                    _remote(arr[k], win(idx + 1 - 2 * c), old[3], old[4], k * N_NEAR + j, other_core).wait_recv()

        _, arrays, _ = _split_call(self.name + "_finish", self.arrays, self.sems, [], after, emit)
        return arrays[n:]


class _Spread:
    def __init__(self, shards, kinds, after, name):
        self.n, self.kinds, self.name = len(shards), kinds, name
        self.sizes = [s.shape[k] for s, k in zip(shards, kinds)]
        n = self.n
        lands = []
        for s, k in zip(shards, kinds):
            dims = list(s.shape)
            dims[k] *= N_DEV
            lands.append(lax.empty(tuple(dims), s.dtype))

        def emit(arr, _, new):
            x, y, c, me = _mesh_place()
            for k in range(n):
                mine = _window(arr[n + k], kinds[k], me, self.sizes[k])
                pltpu.make_async_copy(arr[k], mine, new[2].at[k]).start()
                for q in range(1, N_DEV):
                    _remote(arr[k], mine, new[0], new[1], k * N_DEV + q, _peer(x, y, c, q)[0]).start()

        self.sems, self.arrays, self.token = _split_call(name + "_start", [*shards, *lands], [],
                                                         [n * N_DEV, n * N_DEV, n], after, emit)

    def finish(self, after):
        n, kinds, sizes = self.n, self.kinds, self.sizes

        def emit(arr, old, _):
            x, y, c, me = _mesh_place()
            for k in range(n):
                win = lambda idx: _window(arr[n + k], kinds[k], idx, sizes[k])
                pltpu.make_async_copy(arr[k], win(me), old[2].at[k]).wait()
                for q in range(1, N_DEV):
                    peer, peer_idx = _peer(x, y, c, q)
                    _remote(arr[k], win(me), old[0], old[1], k * N_DEV + q, peer).wait_send()
                    _remote(arr[k], win(peer_idx), old[0], old[1], k * N_DEV + q, peer).wait_recv()

        _, arrays, _ = _split_call(self.name + "_finish", self.arrays, self.sems, [], after, emit)
        return arrays[n:]


class _Scatter:
    def __init__(self, partials, kinds, after, name):
        self.n, self.kinds, self.name, self.partials = len(partials), kinds, name, partials
        self.sizes = [p.shape[k] // N_DEV for p, k in zip(partials, kinds)]
        n, sizes = self.n, self.sizes
        self.slot_shapes = []
        for p, k, size in zip(partials, kinds, sizes):
            dims = list(p.shape)
            dims[k] = size
            self.slot_shapes.append((N_NEAR, *dims))
        slots = [lax.empty(sh, p.dtype) for sh, p in zip(self.slot_shapes, partials)]

        def emit(arr, _, new):
            x, y, c, _ = _mesh_place()
            near = _near(x, y, c)
            for k in range(n):
                for j in range(N_NEAR):
                    owner = near[j][1] if j == 0 else near[j][1] + 1 - 2 * c
                    _remote(_window(arr[k], kinds[k], owner, sizes[k]), arr[n + k].at[j], new[0], new[1],
                            k * N_NEAR + j, near[0][0]).start()

        self.sems, self.arrays, self.token = _split_call(name + "_start", [*partials, *slots], [], [n * N_NEAR] * 2,
                                                         after, emit)

    def combine_and_send(self, own4, after):
        n, kinds, sizes = self.n, self.kinds, self.sizes

        def emit_wait(arr, old, _):
            x, y, c, _ = _mesh_place()
            near = _near(x, y, c)
            for k in range(n):
                for j in range(N_NEAR):
                    owner = near[j][1] if j == 0 else near[j][1] + 1 - 2 * c
                    cp = _remote(_window(arr[k], kinds[k], owner, sizes[k]), arr[n + k].at[j], old[0], old[1],
                                 k * N_NEAR + j, near[0][0])
                    cp.wait_send()
                    cp.wait_recv()

        _, arrays, _ = _split_call(self.name + "_landed", self.arrays, self.sems, [], after, emit_wait)
        chip_sums = _chip_sums(arrays[:n], arrays[n:], kinds, sizes, own4, self.name + "_combine")
        arrivals = [lax.empty((N_NEAR - 1, *sh[1:]), p.dtype) for sh, p in zip(self.slot_shapes, self.partials)]

        def emit_send(arr, _, new):
            x, y, c, _ = _mesh_place()
            near = _near(x, y, c)
            for k in range(n):
                for j in (1, 2, 3):
                    _remote(arr[k].at[j], arr[n + k].at[j - 1], new[0], new[1], k * N_NEAR + j, near[j][0]).start()

        self.sems, self.arrays, self.token = _split_call(self.name + "_send", [*chip_sums, *arrivals], [],
                                                         [n * N_NEAR] * 2, own4, emit_send)

    def finish(self, after):
        n = self.n

        def emit(arr, old, _):
            x, y, c, _ = _mesh_place()
            near = _near(x, y, c)
            for k in range(n):
                for j in (1, 2, 3):
                    cp = _remote(arr[k].at[j], arr[n + k].at[j - 1], old[0], old[1], k * N_NEAR + j, near[j][0])
                    cp.wait_send()
                    cp.wait_recv()

        _, arrays, _ = _split_call(self.name + "_finish", self.arrays, self.sems, [], after, emit)
        return arrays[:n], arrays[n:]


def _chip_sums(partials, slots, kinds, sizes, own4, name):
    n = len(partials)

    def body(own_ref, *refs):
        for k in range(n):
            refs[2 * n + k][...] = (refs[k][...].astype(F32) + refs[n + k][...].astype(F32)).astype(BF16)

    in_specs, slot_specs = [], []
    for p, s, kind, size in zip(partials, slots, kinds, sizes):
        block = list(p.shape)
        block[kind] = size
        nd = len(block)
        in_specs.append(pl.BlockSpec(tuple(block), functools.partial(
            lambda j, own, kind, nd: tuple(own[j] if d == kind else 0 for d in range(nd)), kind=kind, nd=nd)))
        slot_specs.append(pl.BlockSpec((None, *block), functools.partial(
            lambda j, own, nd: (j,) + (0,) * nd, nd=nd)))
    return pl.pallas_call(
        body, name=name,
        grid_spec=pltpu.PrefetchScalarGridSpec(num_scalar_prefetch=1, grid=(N_NEAR,),
                                               in_specs=in_specs + slot_specs, out_specs=slot_specs),
        out_shape=[jax.ShapeDtypeStruct(s.shape, s.dtype) for s in slots],
        compiler_params=_params(("arbitrary",)),
    )(own4, *partials, *slots)


def _to_bf16(arrays, name, dep=None):
    n = len(arrays)
    deps = [] if dep is None else [dep]

    def body(*refs):
        for src, dst in zip(refs[:n], refs[n + len(deps):]):
            dst[...] = src[...].astype(BF16)

    vmem = pl.BlockSpec(memory_space=pltpu.VMEM)
    return pl.pallas_call(body, name=name, out_shape=[jax.ShapeDtypeStruct(a.shape, BF16) for a in arrays],
                          in_specs=[vmem] * n + [pl.BlockSpec(memory_space=pl.ANY)] * len(deps), out_specs=[vmem] * n,
                          compiler_params=pltpu.CompilerParams(vmem_limit_bytes=V7X_VMEM_LIMIT))(*arrays, *deps)


def _silu(c):
    return c * _sigmoid_tail(c)


def _ada_fwd(c_all, w_ada, b_ada_cols, dep):
    def body(c_ref, w_ref, b_ref, dep_ref, out_ref):
        out_ref[...] = jnp.dot(_silu(c_ref[...]), w_ref[...], preferred_element_type=F32,
                               precision=lax.Precision.HIGHEST) + b_ref[...]

    vmem = pl.BlockSpec(memory_space=pltpu.VMEM)
    return pl.pallas_call(
        body, name="ada_fwd", in_specs=[vmem, vmem, vmem, pl.BlockSpec(memory_space=pl.ANY)], out_specs=vmem,
        out_shape=jax.ShapeDtypeStruct((N_DEV, w_ada.shape[1]), F32),
    )(c_all, w_ada, b_ada_cols, dep)


def _adam(w, g, m, v):
    m = ADAM_B1 * m + (1.0 - ADAM_B1) * g
    v = ADAM_B2 * v + (1.0 - ADAM_B2) * (g * g)
    m_hat = m / (1.0 - ADAM_B1 ** ADAM_STEP)
    v_hat = v / (1.0 - ADAM_B2 ** ADAM_STEP)
    delta = -ADAM_LR * (m_hat / (jnp.sqrt(v_hat) + ADAM_EPS) + ADAM_WD * w)
    return delta, m, v


def _ada_bwd_adam(c_all, dmod_cols, w, m, v):
    def body(c_ref, d_ref, w_ref, m_ref, v_ref, g_ref, delta_ref, nm_ref, nv_ref):
        g = lax.dot_general(_silu(c_ref[...]), d_ref[...], (((0,), (0,)), ((), ())),
                            preferred_element_type=F32, precision=lax.Precision.HIGHEST)
        g_ref[...] = g
        delta_ref[...], nm_ref[...], nv_ref[...] = _adam(w_ref[...], g, m_ref[...], v_ref[...])

    sd = jax.ShapeDtypeStruct(w.shape, F32)
    return pl.pallas_call(body, name="ada_bwd_adam", out_shape=[sd] * 4,
                          compiler_params=pltpu.CompilerParams(vmem_limit_bytes=V7X_VMEM_LIMIT),
                          )(c_all, dmod_cols, w, m, v)


def _adam_group(chip_sums, arrivals, ws, ms, vs, n_tiles, name):
    n = len(ws)

    def body(*refs):
        for k in range(n):
            c_ref, a_ref, w_ref, m_ref, v_ref = (refs[j * n + k] for j in range(5))
            g_ref, delta_ref, nm_ref, nv_ref = (refs[(5 + j) * n + k] for j in range(4))
            g = c_ref[...].astype(F32)
            for j in range(N_NEAR - 1):
                g = g + a_ref[j].astype(F32)
            g_ref[...] = g
            delta_ref[...], nm_ref[...], nv_ref[...] = _adam(w_ref[...], g, m_ref[...], v_ref[...])

    tiles = [(w.shape[0] // n_tiles, w.shape[1]) for w in ws]
    blk = [pl.BlockSpec(t, lambda i: (i, 0)) for t in tiles]
    return pl.pallas_call(
        body, name=name, grid=(n_tiles,),
        in_specs=[pl.BlockSpec((None, *t), lambda i: (0, i, 0)) for t in tiles]
        + [pl.BlockSpec((N_NEAR - 1, *t), lambda i: (0, i, 0)) for t in tiles] + blk * 3,
        out_specs=blk * 4, out_shape=[jax.ShapeDtypeStruct(w.shape, F32) for w in ws] * 4,
        compiler_params=_params(("parallel",)),
    )(*chip_sums, *arrivals, *ws, *ms, *vs)


N_SMALL = 40
N_SMALL_PARAMS = 11


def _pack_vecs(conv_w_full, rows):
    def body(cw_ref, *refs):
        out = refs[-1]
        out[...] = jnp.zeros_like(out)
        out[0:4, :] = cw_ref[0:4, :]
        for r, ref in enumerate(refs[:-1]):
            out[4 + r:5 + r, :] = ref[...]

    return pl.pallas_call(body, name="pack_vecs", out_shape=jax.ShapeDtypeStruct((16, D), F32))(conv_w_full, *rows)


def _small_finish(gathered, conv_cols, mod_all, vecs, ws, ms, vs):
    n = N_SMALL_PARAMS

    def body(g_ref, conv_ref, mod_ref, vec_ref, *refs):
        w_refs, m_refs, v_refs = refs[:n], refs[n:2 * n], refs[2 * n:3 * n]
        outs = refs[3 * n:]
        g1 = vec_ref[V_G1:V_G1 + 1, :]
        g2 = vec_ref[V_G2:V_G2 + 1, :]
        zero = jnp.zeros((1, D), F32)
        dg1, dg2, dgf, loss_lanes = zero, zero, zero, zero
        mixer = jnp.zeros((16, D), F32)
        db_ada = jnp.zeros((6, D), F32)
        d_conv_w = jnp.zeros(conv_ref.shape[1:], F32)
        for b in range(N_DEV):
            gb = g_ref[b]
            mod = mod_ref[b]
            q1 = gb[33:34]
            q2 = gb[9:10]
            dmod = jnp.concatenate([gb[32:33], q1 * g1, gb[10:11], gb[8:9], q2 * g2, gb[1:2]], axis=0)
            outs[4 * n][b] = dmod
            db_ada = db_ada + dmod
            dg1 = dg1 + q1 * (1.0 + mod[M_SC1:M_SC1 + 1])
            dg2 = dg2 + q2 * (1.0 + mod[M_SC2:M_SC2 + 1])
            dgf = dgf + gb[0:1]
            loss_lanes = loss_lanes + gb[2:3]
            mixer = mixer + gb[16:32]
            d_conv_w = d_conv_w + conv_ref[b]
        d_a_param = mixer[7:8] * _sigmoid_tail(vec_ref[V_A_PARAM:V_A_PARAM + 1, :])
        grads = [dg1, dg2, mixer[4:5], mixer[5:6], mixer[6:7], d_a_param, mixer[8:9], mixer[9:10], dgf,
                 db_ada, d_conv_w]

        def load(ref, rows):
            if ref.shape[0] == rows:
                return ref[...]
            return jnp.concatenate([ref[:, j * D:(j + 1) * D] for j in range(rows)], axis=0)

        def store(ref, val):
            if ref.shape == val.shape:
                ref[...] = val
            else:
                for j in range(val.shape[0]):
                    ref[:, j * D:(j + 1) * D] = val[j:j + 1]

        for k in range(n):
            rows = grads[k].shape[0]
            results = (grads[k], *_adam(load(w_refs[k], rows), grads[k], load(m_refs[k], rows), load(v_refs[k], rows)))
            for which, val in enumerate(results):
                store(outs[which * n + k], val)
        outs[4 * n + 1][...] = jnp.broadcast_to(jnp.sum(loss_lanes, axis=1, keepdims=True), (8, 128))

    shapes = [jax.ShapeDtypeStruct(w.shape, F32) for w in ws]
    return pl.pallas_call(
        body, name="small_finish",
        out_shape=shapes * 4 + [jax.ShapeDtypeStruct((N_DEV, 6, D), F32), jax.ShapeDtypeStruct((8, 128), F32)],
    )(gathered, conv_cols, mod_all, vecs, *ws, *ms, *vs)


def _pad_rows(a, rows):
    return jnp.pad(a, ((0, rows - a.shape[0]), (0, 0)))


def kernel(x, c, norm_mix_g, norm_mlp_g, w_ada, b_ada, w_in, conv_w, conv_b, w_rg_a, b_rg_a, w_rg_x, b_rg_x, a_param, w_branch_a, w_pool, b_pool, pool_scale, w_branch_b, w_out, w_up, w_down, final_g, loss_target, m_norm_mix_g, m_norm_mlp_g, m_w_ada, m_b_ada, m_w_in, m_conv_w, m_conv_b, m_w_rg_a, m_b_rg_a, m_w_rg_x, m_b_rg_x, m_a_param, m_w_branch_a, m_w_pool, m_b_pool, m_pool_scale, m_w_branch_b, m_w_out, m_w_up, m_w_down, m_final_g, v_norm_mix_g, v_norm_mlp_g, v_w_ada, v_b_ada, v_w_in, v_conv_w, v_conv_b, v_w_rg_a, v_b_rg_a, v_w_rg_x, v_b_rg_x, v_a_param, v_w_branch_a, v_w_pool, v_b_pool, v_pool_scale, v_w_branch_b, v_w_out, v_w_up, v_w_down, v_final_g):
    me = 4 * lax.axis_index("x") + 2 * lax.axis_index("y") + lax.axis_index("c")
    s = x.shape[1]
    x2d = x.reshape(s, D)
    target = loss_target.reshape(s, D)
    n_ada = w_ada.shape[2]

    b_ada_cols = lax.dynamic_slice(b_ada, (0, me * n_ada), (1, n_ada))
    spread_c = _Spread([_pad_rows(c, 8), _pad_rows(conv_w[0], 8)], [0, 1], c, "spread_c")

    sharded = dict(w_in=(w_in[0], 1), w_up=(w_up[0], 1), w_down=(w_down[0], 0), w_branch_a=(w_branch_a[0], 0),
                   w_branch_b=(w_branch_b[0], 0), w_out=(w_out[0], 0), w_rg_a=(w_rg_a[0], 1), w_rg_x=(w_rg_x[0], 1),
                   w_pool=(w_pool[0], 1))
    kind = {k: v[1] for k, v in sharded.items()}
    first_names = ["w_in"]
    later_names = [k for k in sharded if k not in first_names]
    mix_names = ["w_rg_a", "w_rg_x", "w_pool"]
    branch_names = ["w_branch_a", "w_branch_b", "w_out"]
    mlp_names = ["w_up", "w_down"]

    def gather(group, after, name):
        return _Gather([shard[k] for k in group], [kind[k] for k in group], after, name)

    shard = dict(zip(first_names, _to_bf16([sharded[k][0] for k in first_names], "to_bf16_first")))
    g_first = gather(first_names, spread_c.token, "gather_first")
    shard.update(zip(later_names, _to_bf16([sharded[k][0] for k in later_names], "to_bf16_later", dep=g_first.token)))

    c_rows, conv_w_full = spread_c.finish(g_first.token)
    c_all = c_rows.reshape(N_DEV, 8, D)[:, 0, :]
    mod_part = _ada_fwd(c_all, w_ada[0], b_ada_cols, g_first.token)
    vecs = _pack_vecs(conv_w_full, [conv_b, b_rg_a, b_rg_x, a_param, b_pool, pool_scale,
                                    norm_mix_g, norm_mlp_g, final_g.reshape(1, D)])
    spread_mod = _Spread([mod_part], [0], vecs, "spread_mod")
    spread_mix = _Spread([shard[k] for k in mix_names], [kind[k] for k in mix_names], spread_mod.token, "spread_mix")
    g_branch = gather(branch_names, spread_mix.token, "gather_branch")
    g_mlp = gather(mlp_names, g_branch.token, "gather_mlp")
    g_first.forward(g_mlp.token)
    wg = dict(zip(first_names, g_first.finish(g_first.token)))
    mod_parts, = spread_mod.finish(g_first.token)
    mod_all = jnp.transpose(mod_parts.reshape(N_DEV, N_DEV, n_ada), (1, 0, 2)).reshape(N_DEV, 6, D)
    modr = _pad_rows(lax.dynamic_index_in_dim(mod_all, me, 0, keepdims=False), 8)

    h1, x_rnn, u_pool, ga, dga, sa, sb = _proj_fwd(x2d, modr, vecs, wg["w_in"])
    g_branch.forward(h1)
    wg.update(zip(mix_names, spread_mix.finish(g_branch.token)))
    xr, hr, za, p, pooled, *gates = _mix_fwd(x_rnn, u_pool, ga, vecs, wg["w_rg_a"], wg["w_rg_x"], wg["w_pool"],
                                             dep=g_branch.token)
    g_mlp.forward(za)
    wg.update(zip(branch_names, g_branch.finish(g_mlp.token)))
    ba, bb, merged, o, x2, h2 = _branch_fwd(za, pooled, sa, sb, x2d, modr, vecs,
                                            wg["w_branch_a"], wg["w_branch_b"], wg["w_out"])
    wg.update(zip(mlp_names, g_mlp.finish(h2)))
    ru, dx3, d_dn, small_f = _mlp_fwd(h2, x2, target, modr, vecs, wg["w_up"], wg["w_down"])

    near = _near(lax.axis_index("x"), lax.axis_index("y"), lax.axis_index("c"))
    own4 = jnp.stack([me, near[1][1], near[2][1], near[3][1]]).astype(jnp.int32)

    def scatter(group, partial, after, name):
        return _Scatter([partial[k] for k in group], [kind[k] for k in group], after, name)

    dup, dx2, do, small_m = _mlp_bwd(d_dn, ru, x2, dx3, o, modr, vecs, wg["w_up"], wg["w_down"])
    partial = dict(w_up=_wgrad(h2, dup, "wgrad_up"), w_down=_wgrad(ru, d_dn, "wgrad_down", square_a=True))
    s_mlp = scatter(mlp_names, partial, dx2, "scatter_mlp")

    dba, dbb, dgates, dza, dpooled = _branch_bwd(do, sa, sb, ba, bb, wg["w_branch_a"], wg["w_branch_b"], wg["w_out"],
                                                 dep=s_mlp.token)
    s_mlp.combine_and_send(own4, dza)
    dproj, dw_rg_a, dw_rg_x, dw_pool, small_x = _mix_bwd(dza, dpooled, x_rnn, ga, dga, xr, hr, p, gates, dgates,
                                                         vecs, wg["w_rg_a"], wg["w_rg_x"], wg["w_pool"],
                                                         dep=s_mlp.token)
    partial.update(w_branch_a=_wgrad(za, dba, "wgrad_branch_a"), w_branch_b=_wgrad(pooled, dbb, "wgrad_branch_b"),
                   w_out=_wgrad(merged, do, "wgrad_out"),
                   w_rg_a=dw_rg_a, w_rg_x=dw_rg_x, w_pool=dw_pool)
    mixer_names = ["w_rg_a", "w_rg_x", "w_pool", "w_branch_a", "w_branch_b", "w_out"]
    s_mixer = scatter(mixer_names, partial, s_mlp.token, "scatter_mixer")

    partial["w_in"] = _wgrad(h1, dproj, "wgrad_in", dep=s_mixer.token)
    s_in = scatter(["w_in"], partial, s_mixer.token, "scatter_in")
    s_mixer.combine_and_send(own4, s_in.token)
    s_in.combine_and_send(own4, s_mixer.token)
    grad_x, small_p = _proj_bwd(dproj, x2d, dx2, modr, vecs, wg["w_in"], dep=s_in.token)

    locals_ = dict(w_in=(w_in, m_w_in, v_w_in), w_up=(w_up, m_w_up, v_w_up), w_down=(w_down, m_w_down, v_w_down),
                   w_branch_a=(w_branch_a, m_w_branch_a, v_w_branch_a),
                   w_branch_b=(w_branch_b, m_w_branch_b, v_w_branch_b), w_out=(w_out, m_w_out, v_w_out),
                   w_rg_a=(w_rg_a, m_w_rg_a, v_w_rg_a), w_rg_x=(w_rg_x, m_w_rg_x, v_w_rg_x),
                   w_pool=(w_pool, m_w_pool, v_w_pool))
    res = {}

    def finish(group, exchange, after, n_tiles, name):
        chip_sums, arrivals = exchange.finish(after)
        flat = lambda t: t.reshape(-1, t.shape[-1])
        shapes = [flat(locals_[k][0]).shape for k in group]
        outs = _adam_group([cs.reshape(N_NEAR, *sh) for cs, sh in zip(chip_sums, shapes)],
                           [ar.reshape(N_NEAR - 1, *sh) for ar, sh in zip(arrivals, shapes)],
                           *[[flat(locals_[k][j]) for k in group] for j in range(3)], n_tiles, name)
        for i, k in enumerate(group):
            res[k] = [outs[j * len(group) + i].reshape(locals_[k][0].shape) for j in range(4)]
        return res[group[-1]][0]

    small = jnp.concatenate([small_f, small_m, small_x, small_p], axis=0)
    g_small = _Spread([small], [0], grad_x, "spread_small")
    done = finish(mlp_names, s_mlp, g_small.token, 4, "adam_mlp")
    done = finish(mixer_names, s_mixer, done, 2, "adam_mixer")
    done = finish(["w_in"], s_in, done, 4, "adam_in")
    small_all, = g_small.finish(done)
    small_all = small_all.reshape(N_DEV, N_SMALL, D)

    conv_cols = lax.dynamic_slice(small_all, (0, 16, me * (D // N_DEV)), (N_DEV, 4, D // N_DEV))

    def smalls(ng, nl, cb, bra, brx, ap, bp, ps, fg, ba_, cw):
        return [ng, nl, cb, bra, brx, ap, bp, ps, fg.reshape(1, D), ba_, cw[0]]

    small_names = ["norm_mix_g", "norm_mlp_g", "conv_b", "b_rg_a", "b_rg_x", "a_param", "b_pool", "pool_scale",
                   "final_g", "b_ada", "conv_w"]
    fin = _small_finish(
        small_all, conv_cols, mod_all, vecs,
        smalls(norm_mix_g, norm_mlp_g, conv_b, b_rg_a, b_rg_x, a_param, b_pool, pool_scale, final_g, b_ada, conv_w),
        smalls(m_norm_mix_g, m_norm_mlp_g, m_conv_b, m_b_rg_a, m_b_rg_x, m_a_param, m_b_pool, m_pool_scale,
               m_final_g, m_b_ada, m_conv_w),
        smalls(v_norm_mix_g, v_norm_mlp_g, v_conv_b, v_b_rg_a, v_b_rg_x, v_a_param, v_b_pool, v_pool_scale,
               v_final_g, v_b_ada, v_conv_w))
    dmod_all, loss_tile = fin[4 * N_SMALL_PARAMS], fin[4 * N_SMALL_PARAMS + 1]
    dmod_cols = lax.dynamic_slice(dmod_all.reshape(N_DEV, 6 * D), (0, me * n_ada), (N_DEV, n_ada))
    res["w_ada"] = [t.reshape(w_ada.shape) for t in _ada_bwd_adam(c_all, dmod_cols, w_ada[0], m_w_ada[0], v_w_ada[0])]

    def final_shape(k, t):
        if k == "final_g":
            return t.reshape(D)
        if k == "conv_w":
            return t.reshape(conv_w.shape)
        return t

    for i, k in enumerate(small_names):
        res[k] = [final_shape(k, fin[which * N_SMALL_PARAMS + i]) for which in range(4)]
    order = ["norm_mix_g", "norm_mlp_g", "w_ada", "b_ada", "w_in", "conv_w", "conv_b", "w_rg_a", "b_rg_a", "w_rg_x",
             "b_rg_x", "a_param", "w_branch_a", "w_pool", "b_pool", "pool_scale", "w_branch_b", "w_out", "w_up",
             "w_down", "final_g"]
    outs = [loss_tile[0, 0], grad_x.reshape(x.shape)]
    for which in range(4):
        for k in order:
            outs.append(res[k][which])
    return tuple(outs)
```

```python
import functools

import jax
import jax.numpy as jnp
from jax import lax
from jax.experimental import pallas as pl
from jax.experimental.pallas import tpu as pltpu

F32 = jnp.float32
BF16 = jnp.bfloat16
MESH = pl.DeviceIdType.MESH

N_DEV = 8
D = 1024
N_GROUPS = 4
GW = D // N_GROUPS
D_IN = 5 * D
D_FF = 4 * D
POOL_WINDOWS = (2, 4, 8, 16)
HALO_X = 8
HALO_U = 16
EPS = 1e-6
C_RG = 8.0
ADAM_LR, ADAM_B1, ADAM_B2, ADAM_EPS, ADAM_WD, ADAM_STEP = 0.001, 0.9, 0.999, 1e-08, 0.01, 10

V7X_VMEM_LIMIT = 56 * 1024 * 1024

V_CONV_W, V_CONV_B, V_B_RG_A, V_B_RG_X, V_A_PARAM, V_B_POOL, V_POOL_SCALE, V_G1, V_G2, V_GF = 0, 4, 5, 6, 7, 8, 9, 10, 11, 12
M_SH1, M_SC1, M_GT1, M_SH2, M_SC2, M_GT2 = 0, 1, 2, 3, 4, 5

TM_PROJ = 512
TM_MIX = 256
TM_BRANCH = 512
TM_MLP = 512
TM_MLP_BWD = 256
TS_WGRAD = 1024


def _params(semantics):
    return pltpu.CompilerParams(dimension_semantics=semantics, vmem_limit_bytes=V7X_VMEM_LIMIT)


def _resident(shape):
    return pl.BlockSpec(shape, lambda *_: (0,) * len(shape), pipeline_mode=pl.Buffered(1))


def _dot(a, b):
    return jnp.dot(a, b, preferred_element_type=F32)


def _dot_nt(a, b):
    return lax.dot_general(a, b, (((1,), (1,)), ((), ())), preferred_element_type=F32)


def _dot_tn(a, b):
    return lax.dot_general(a, b, (((0,), (0,)), ((), ())), preferred_element_type=F32)


def _sigmoid(x):
    return 0.5 * jnp.tanh(0.5 * x) + 0.5


def _sigmoid_tail(x):
    return 1.0 / (1.0 + jnp.exp(-x))


def _gelu_and_grad(x):
    k = 0.7978845608028654
    x2 = x * x
    t = jnp.tanh(k * (x + 0.044715 * x * x2))
    g = 0.5 * x * (1.0 + t)
    dg = 0.5 * (1.0 + t) + 0.5 * x * (1.0 - t * t) * (k * (1.0 + 3.0 * 0.044715 * x2))
    return g, dg


def _softplus(a):
    e = jnp.exp(-jnp.abs(a))
    u = 1.0 + e
    log1p_e = jnp.where(u == 1.0, e, jnp.log(u) * e / jnp.where(u == 1.0, 1.0, u - 1.0))
    return jnp.maximum(a, 0.0) + log1p_e


def _neg_expm1(z):
    series = -(z * (1.0 + z * (0.5 + z * (1.0 / 6.0 + z * (1.0 / 24.0 + z * (1.0 / 120.0))))))
    return jnp.where(z > -0.1, series, 1.0 - jnp.exp(z))


def _shift_down(x, k):
    return pltpu.roll(x, k, 0)


def _shift_up(x, k):
    return pltpu.roll(x, x.shape[0] - k, 0)


def _rglru_gates(xr, w_a, w_x, b_a, b_x, a_param, is_t0):
    xb = xr.astype(BF16)
    ra = _sigmoid(_dot(xb, w_a) + b_a)
    ri = _sigmoid(_dot(xb, w_x) + b_x)
    sp = _softplus(a_param)
    log_a = (-C_RG) * ra * sp
    a = jnp.exp(log_a)
    mult = jnp.where(is_t0, 1.0, jnp.sqrt(_neg_expm1(2.0 * log_a)))
    return ra, ri, sp, a, mult


SUBLANES = 8


LANES = 128


def _scan_strip(a, b, carry, scr, down):
    t = b.shape[0]
    g = t // SUBLANES
    a3 = a.reshape(g, SUBLANES, LANES)
    b3 = b.reshape(g, SUBLANES, LANES)
    sub = lax.broadcasted_iota(jnp.int32, (g, SUBLANES, LANES), 1)
    for k in (1, 2, 4):
        keep = sub >= k if down else sub < SUBLANES - k
        shift = k if down else SUBLANES - k
        b3 = b3 + a3 * jnp.where(keep, pltpu.roll(b3, shift, 1), 0.0)
        a3 = a3 * jnp.where(keep, pltpu.roll(a3, shift, 1), 1.0)
    scr[0] = a3.reshape(t, LANES)
    scr[1] = b3.reshape(t, LANES)
    end_row = SUBLANES - 1 if down else 0
    ag = scr[0, pl.ds(end_row, g, stride=SUBLANES), :]
    bg = scr[1, pl.ds(end_row, g, stride=SUBLANES), :]
    rg = lax.broadcasted_iota(jnp.int32, (g, LANES), 0)
    edge = 0 if down else g - 1
    bg = bg + jnp.where(rg == edge, ag * carry, 0.0)
    k = 1
    while k < g:
        keep = rg >= k if down else rg < g - k
        shift = k if down else g - k
        bg = bg + ag * jnp.where(keep, pltpu.roll(bg, shift, 0), 0.0)
        if 2 * k < g:
            ag = ag * pltpu.roll(ag, shift, 0)
        k *= 2
    entering = jnp.where(rg != edge, pltpu.roll(bg, 1 if down else g - 1, 0), carry)
    for r in range(SUBLANES):
        scr[2, pl.ds(r, g, stride=SUBLANES), :] = entering
    return scr[1] + scr[0] * scr[2], bg[g - 1:g, :]


def _scan_strips(a, b, carry, scr, down):
    outs = [_scan_strip(a[:, c:c + LANES], b[:, c:c + LANES], carry[:, c:c + LANES], scr, down)
            for c in range(0, b.shape[1], LANES)]
    return jnp.concatenate([o[0] for o in outs], axis=1), jnp.concatenate([o[1] for o in outs], axis=1)


def _scan_down(a, b, carry, scr):
    return _scan_strips(a, b, carry, scr, True)


def _scan_up(m, b, carry, scr):
    return _scan_strips(m, b, carry, scr, False)[0]


def _window_mean(sums, window, first_block, head_t):
    scaled = sums * (1.0 / window)
    head = jnp.where(first_block, sums[:HALO_U] / jnp.minimum(head_t, float(window)), scaled[:HALO_U])
    return jnp.concatenate([head, scaled[HALO_U:]], axis=0)


def _conv_taps(x_ext):
    return [_shift_down(x_ext, 3 - j)[HALO_X:] if j < 3 else x_ext[HALO_X:] for j in range(4)]


def _proj_fwd(x, modr, vecs, w_in):
    s = x.shape[0]
    tm = min(TM_PROJ, s)

    def body(x_ref, mod_ref, vec_ref, w_ref, h1_ref, xrnn_ref, u_ref, ga_ref, dga_ref, sa_ref, sb_ref):
        xv = x_ref[...]
        r = lax.rsqrt(jnp.mean(xv * xv, axis=-1, keepdims=True) + EPS)
        gain = vec_ref[V_G1:V_G1 + 1, :] * (1.0 + mod_ref[M_SC1:M_SC1 + 1, :])
        h = (xv * r * gain + mod_ref[M_SH1:M_SH1 + 1, :]).astype(BF16)
        h1_ref[...] = h
        xrnn_ref[...] = _dot(h, w_ref[:, 0:D])
        ga_ref[...], dga_ref[...] = _gelu_and_grad(_dot(h, w_ref[:, D:2 * D]))
        u_ref[...] = _dot(h, w_ref[:, 2 * D:3 * D])
        sa_ref[...] = _sigmoid(_dot(h, w_ref[:, 3 * D:4 * D]))
        sb_ref[...] = _sigmoid(_dot(h, w_ref[:, 4 * D:5 * D]))

    tok = pl.BlockSpec((tm, D), lambda i: (i, 0))
    sd = lambda dt: jax.ShapeDtypeStruct((s, D), dt)
    return pl.pallas_call(
        body, name="proj_fwd", grid=(s // tm,),
        in_specs=[tok, pl.BlockSpec((8, D), lambda i: (0, 0)), pl.BlockSpec((16, D), lambda i: (0, 0)),
                  _resident((D, D_IN))],
        out_specs=[tok] * 7,
        out_shape=[sd(BF16)] + [sd(F32)] * 6,
        compiler_params=_params(("parallel",)),
    )(x, modr, vecs, w_in)


def _mix_fwd(x_rnn, u_pool, ga, vecs, w_rg_a, w_rg_x, w_pool, dep):
    s = x_rnn.shape[0]
    tm = min(TM_MIX, s)
    nb = s // tm

    def body(xh_ref, x_ref, uh_ref, u_ref, ga_ref, vec_ref, wa_ref, wx_ref, wp_ref, dep_ref,
             xr_ref, hr_ref, za_ref, p_ref, pooled_ref, a_ref, mult_ref, ra_ref, ri_ref, carry_ref, scan_scr):
        i = pl.program_id(0)
        first = i == 0

        @pl.when(first)
        def _():
            carry_ref[...] = jnp.zeros_like(carry_ref)

        row = lax.broadcasted_iota(jnp.int32, (tm, GW), 0)
        is_t0 = jnp.logical_and(first, row == 0)
        head_t = (lax.broadcasted_iota(jnp.int32, (HALO_U, GW), 0) + 1).astype(F32)
        for g in range(N_GROUPS):
            cs = slice(g * GW, (g + 1) * GW)
            vec = vec_ref[:, cs]
            xh = jnp.where(first, 0.0, xh_ref[:, cs])
            taps = _conv_taps(jnp.concatenate([xh, x_ref[:, cs]], axis=0))
            xr = vec[V_CONV_B:V_CONV_B + 1]
            for j in range(4):
                xr = xr + vec[V_CONV_W + j:V_CONV_W + j + 1] * taps[j]
            xr_ref[:, cs] = xr
            ra, ri, _, a, mult = _rglru_gates(
                xr, wa_ref[g], wx_ref[g], vec[V_B_RG_A:V_B_RG_A + 1], vec[V_B_RG_X:V_B_RG_X + 1],
                vec[V_A_PARAM:V_A_PARAM + 1], is_t0)
            a_ref[:, cs] = a
            mult_ref[:, cs] = mult
            ra_ref[:, cs] = ra.astype(BF16)
            ri_ref[:, cs] = ri.astype(BF16)
            h, last = _scan_down(a, xr * ri * mult, carry_ref[0:1, cs], scan_scr)
            hr_ref[:, cs] = h
            carry_ref[0:1, cs] = last
            za_ref[:, cs] = (ga_ref[:, cs] * h).astype(BF16)
            uh = jnp.where(first, 0.0, uh_ref[:, cs])
            sm = jnp.concatenate([uh, u_ref[:, cs]], axis=0)
            k = 1
            while k < POOL_WINDOWS[g]:
                sm = sm + _shift_down(sm, k)
                k *= 2
            mean = _window_mean(sm[HALO_U:], POOL_WINDOWS[g], first, head_t)
            p = (mean - u_ref[:, cs]).astype(BF16)
            p_ref[:, cs] = p
            pb = _dot(p, wp_ref[g]) + vec[V_B_POOL:V_B_POOL + 1]
            pooled_ref[:, cs] = (pb * vec[V_POOL_SCALE:V_POOL_SCALE + 1]).astype(BF16)

    tok = pl.BlockSpec((tm, D), lambda i: (i, 0))
    halo = lambda rows: pl.BlockSpec((rows, D), lambda i: (jnp.maximum(i * (tm // rows) - 1, 0), 0))
    wspec = pl.BlockSpec((N_GROUPS, GW, GW), lambda i: (0, 0, 0))
    sd = lambda dt: jax.ShapeDtypeStruct((s, D), dt)
    return pl.pallas_call(
        body, name="mix_fwd", grid=(nb,),
        in_specs=[halo(HALO_X), tok, halo(HALO_U), tok, tok, pl.BlockSpec((16, D), lambda i: (0, 0)),
                  wspec, wspec, wspec, pl.BlockSpec(memory_space=pl.ANY)],
        out_specs=[tok] * 9,
        out_shape=[sd(F32), sd(F32), sd(BF16), sd(BF16), sd(BF16), sd(F32), sd(F32), sd(BF16), sd(BF16)],
        scratch_shapes=[pltpu.VMEM((8, D), F32), pltpu.VMEM((3, tm, LANES), F32)],
        compiler_params=_params(("arbitrary",)),
    )(x_rnn, x_rnn, u_pool, u_pool, ga, vecs, w_rg_a, w_rg_x, w_pool, dep)


def _branch_fwd(za, pooled, sa, sb, x, modr, vecs, w_a, w_b, w_out):
    s = x.shape[0]
    tm = min(TM_BRANCH, s)

    def body(za_ref, pooled_ref, sa_ref, sb_ref, x_ref, mod_ref, vec_ref, wa_ref, wb_ref, wo_ref,
             ba_ref, bb_ref, merged_ref, o_ref, x2_ref, h2_ref):
        ba = _dot(za_ref[...], wa_ref[...])
        bb = _dot(pooled_ref[...], wb_ref[...])
        ba_ref[...] = ba.astype(BF16)
        bb_ref[...] = bb.astype(BF16)
        merged = (sa_ref[...] * ba + sb_ref[...] * bb).astype(BF16)
        merged_ref[...] = merged
        o = _dot(merged, wo_ref[...])
        o_ref[...] = o.astype(BF16)
        x2 = x_ref[...] + mod_ref[M_GT1:M_GT1 + 1, :] * o
        x2_ref[...] = x2
        r = lax.rsqrt(jnp.mean(x2 * x2, axis=-1, keepdims=True) + EPS)
        gain = vec_ref[V_G2:V_G2 + 1, :] * (1.0 + mod_ref[M_SC2:M_SC2 + 1, :])
        h2_ref[...] = (x2 * r * gain + mod_ref[M_SH2:M_SH2 + 1, :]).astype(BF16)

    tok = pl.BlockSpec((tm, D), lambda i: (i, 0))
    wspec = pl.BlockSpec((D, D), lambda i: (0, 0))
    sd = lambda dt: jax.ShapeDtypeStruct((s, D), dt)
    return pl.pallas_call(
        body, name="branch_fwd", grid=(s // tm,),
        in_specs=[tok, tok, tok, tok,
                  tok, pl.BlockSpec((8, D), lambda i: (0, 0)), pl.BlockSpec((16, D), lambda i: (0, 0)),
                  wspec, wspec, wspec],
        out_specs=[tok] * 6,
        out_shape=[sd(BF16), sd(BF16), sd(BF16), sd(BF16), sd(F32), sd(BF16)],
        compiler_params=_params(("parallel",)),
    )(za, pooled, sa, sb, x, modr, vecs, w_a, w_b, w_out)


def _mlp_fwd(h2, x2, target, modr, vecs, w_up, w_down):
    s = x2.shape[0]
    tm = min(TM_MLP, s)

    def body(h2_ref, x2_ref, tgt_ref, mod_ref, vec_ref, wu_ref, wd_ref,
             ru_ref, dx3_ref, ddn_ref, small_ref):
        @pl.when(pl.program_id(0) == 0)
        def _():
            small_ref[...] = jnp.zeros_like(small_ref)

        h2 = h2_ref[...]
        dn = None
        for c in range(D_FF // D):
            cs = slice(c * D, (c + 1) * D)
            ru = jnp.maximum(_dot(h2, wu_ref[:, cs]), 0.0)
            ru_ref[:, cs] = ru.astype(BF16)
            part = _dot((ru * ru).astype(BF16), wd_ref[cs, :])
            dn = part if dn is None else dn + part
        gt2 = mod_ref[M_GT2:M_GT2 + 1, :]
        gf = vec_ref[V_GF:V_GF + 1, :]
        x3 = x2_ref[...] + gt2 * dn
        r3 = lax.rsqrt(jnp.mean(x3 * x3, axis=-1, keepdims=True) + EPS)
        n3 = x3 * r3
        err = n3 * gf - tgt_ref[...]
        dy = err * (1.0 / D)
        dn3 = dy * gf
        dx3 = r3 * (dn3 - n3 * jnp.mean(dn3 * n3, axis=-1, keepdims=True))
        dx3_ref[...] = dx3
        ddn_ref[...] = (dx3 * gt2).astype(BF16)
        small_ref[0:1, :] += jnp.sum(dy * n3, axis=0, keepdims=True)
        small_ref[1:2, :] += jnp.sum(dx3 * dn, axis=0, keepdims=True)
        small_ref[2:3, :] += (0.5 / D) * jnp.sum(err * err, axis=0, keepdims=True)

    tok = pl.BlockSpec((tm, D), lambda i: (i, 0))
    return pl.pallas_call(
        body, name="mlp_fwd", grid=(s // tm,),
        in_specs=[tok, tok, tok,
                  pl.BlockSpec((8, D), lambda i: (0, 0)), pl.BlockSpec((16, D), lambda i: (0, 0)),
                  _resident((D, D_FF)), _resident((D_FF, D))],
        out_specs=[pl.BlockSpec((tm, D_FF), lambda i: (i, 0)), tok, tok,
                   pl.BlockSpec((8, D), lambda i: (0, 0))],
        out_shape=[jax.ShapeDtypeStruct((s, D_FF), BF16), jax.ShapeDtypeStruct((s, D), F32),
                   jax.ShapeDtypeStruct((s, D), BF16), jax.ShapeDtypeStruct((8, D), F32)],
        compiler_params=_params(("arbitrary",)),
    )(h2, x2, target, modr, vecs, w_up, w_down)


def _mlp_bwd(d_dn, ru, x2, dx3, o, modr, vecs, w_up, w_down):
    s = x2.shape[0]
    tm = min(TM_MLP_BWD, s)

    def body(ddn_ref, ru_ref, x2_ref, dx3_ref, o_ref, mod_ref, vec_ref, wu_ref, wd_ref,
             dup_ref, dx2_ref, do_ref, small_ref):
        @pl.when(pl.program_id(0) == 0)
        def _():
            small_ref[...] = jnp.zeros_like(small_ref)

        ddn = ddn_ref[...]
        dh2 = None
        for c in range(D_FF // D):
            cs = slice(c * D, (c + 1) * D)
            dff = _dot_nt(ddn, wd_ref[cs, :])
            dup = (dff * (2.0 * ru_ref[:, cs].astype(F32))).astype(BF16)
            dup_ref[:, cs] = dup
            part = _dot_nt(dup, wu_ref[:, cs])
            dh2 = part if dh2 is None else dh2 + part
        x2 = x2_ref[...]
        r2 = lax.rsqrt(jnp.mean(x2 * x2, axis=-1, keepdims=True) + EPS)
        xn2 = x2 * r2
        gain = vec_ref[V_G2:V_G2 + 1, :] * (1.0 + mod_ref[M_SC2:M_SC2 + 1, :])
        dxn2 = dh2 * gain
        dx2 = dx3_ref[...] + r2 * (dxn2 - xn2 * jnp.mean(dxn2 * xn2, axis=-1, keepdims=True))
        dx2_ref[...] = dx2
        do_ref[...] = (dx2 * mod_ref[M_GT1:M_GT1 + 1, :]).astype(BF16)
        small_ref[0:1, :] += jnp.sum(dh2, axis=0, keepdims=True)
        small_ref[1:2, :] += jnp.sum(dh2 * xn2, axis=0, keepdims=True)
        small_ref[2:3, :] += jnp.sum(dx2 * o_ref[...].astype(F32), axis=0, keepdims=True)

    tok = pl.BlockSpec((tm, D), lambda i: (i, 0))
    wide = pl.BlockSpec((tm, D_FF), lambda i: (i, 0))
    return pl.pallas_call(
        body, name="mlp_bwd", grid=(s // tm,),
        in_specs=[tok, wide, tok, tok, tok,
                  pl.BlockSpec((8, D), lambda i: (0, 0)), pl.BlockSpec((16, D), lambda i: (0, 0)),
                  _resident((D, D_FF)), _resident((D_FF, D))],
        out_specs=[wide, tok, tok, pl.BlockSpec((8, D), lambda i: (0, 0))],
        out_shape=[jax.ShapeDtypeStruct((s, D_FF), BF16), jax.ShapeDtypeStruct((s, D), F32),
                   jax.ShapeDtypeStruct((s, D), BF16), jax.ShapeDtypeStruct((8, D), F32)],
        compiler_params=_params(("arbitrary",)),
    )(d_dn, ru, x2, dx3, o, modr, vecs, w_up, w_down)


def _branch_bwd(do, sa, sb, ba, bb, w_a, w_b, w_out, dep):
    s = do.shape[0]
    tm = min(TM_BRANCH, s)

    def body(do_ref, sa_ref, sb_ref, ba_ref, bb_ref, wa_ref, wb_ref, wo_ref, dep_ref,
             dba_ref, dbb_ref, dg_ref, dza_ref, dpooled_ref):
        dmerged = _dot_nt(do_ref[...], wo_ref[...])
        sa = sa_ref[...]
        sb = sb_ref[...]
        dba = (dmerged * sa).astype(BF16)
        dbb = (dmerged * sb).astype(BF16)
        dba_ref[...] = dba
        dbb_ref[...] = dbb
        dg_ref[:, :D] = (dmerged * ba_ref[...].astype(F32) * sa * (1.0 - sa)).astype(BF16)
        dg_ref[:, D:] = (dmerged * bb_ref[...].astype(F32) * sb * (1.0 - sb)).astype(BF16)
        dza_ref[...] = _dot_nt(dba, wa_ref[...])
        dpooled_ref[...] = _dot_nt(dbb, wb_ref[...])

    tok = pl.BlockSpec((tm, D), lambda i: (i, 0))
    wspec = pl.BlockSpec((D, D), lambda i: (0, 0))
    sd = lambda dt: jax.ShapeDtypeStruct((s, D), dt)
    return pl.pallas_call(
        body, name="branch_bwd", grid=(s // tm,),
        in_specs=[tok, tok, tok, tok, tok, wspec, wspec, wspec, pl.BlockSpec(memory_space=pl.ANY)],
        out_specs=[tok, tok, pl.BlockSpec((tm, 2 * D), lambda i: (i, 0)), tok, tok],
        out_shape=[sd(BF16), sd(BF16), jax.ShapeDtypeStruct((s, 2 * D), BF16), sd(F32), sd(F32)],
        compiler_params=_params(("parallel",)),
    )(do, sa, sb, ba, bb, w_a, w_b, w_out, dep)


def _mix_bwd(dza, dpooled, x_rnn, ga, dga, xr, hr, p, gates, dgates, vecs, w_rg_a, w_rg_x, w_pool, dep):
    s = xr.shape[0]
    tm = min(TM_MIX, s)
    nb = s // tm

    def body(dza_ref, dpooled_ref, xh_ref, x_ref, ga_ref, dga_ref, xr_ref, hh_ref, hr_ref, p_ref,
             a_ref, mult_ref, ra_ref, ri_ref, dg_ref, vec_ref, wa_ref, wx_ref, wp_ref, dep_ref,
             dproj_ref, dwa_ref, dwx_ref, dwp_ref, small_ref,
             scan_carry, dxr_carry, q_carry, scan_scr, dwa_acc, dwx_acc, dwp_acc):
        i = pl.program_id(0)
        bi = nb - 1 - i
        first_t = bi == 0

        @pl.when(i == 0)
        def _():
            scan_carry[...] = jnp.zeros_like(scan_carry)
            dxr_carry[...] = jnp.zeros_like(dxr_carry)
            q_carry[...] = jnp.zeros_like(q_carry)
            dwa_acc[...] = jnp.zeros_like(dwa_acc)
            dwx_acc[...] = jnp.zeros_like(dwx_acc)
            dwp_acc[...] = jnp.zeros_like(dwp_acc)
            small_ref[...] = jnp.zeros_like(small_ref)

        row = lax.broadcasted_iota(jnp.int32, (tm, GW), 0)
        is_t0 = jnp.logical_and(first_t, row == 0)
        head_t = (lax.broadcasted_iota(jnp.int32, (HALO_U, GW), 0) + 1).astype(F32)
        colsum = lambda v: jnp.sum(v, axis=0, keepdims=True)
        for g in range(N_GROUPS):
            cs = slice(g * GW, (g + 1) * GW)
            vec = vec_ref[:, cs]
            xr = xr_ref[:, cs]
            hr = hr_ref[:, cs]
            dza = dza_ref[:, cs]
            dproj_ref[:, D + g * GW:D + (g + 1) * GW] = (dza * hr * dga_ref[:, cs]).astype(BF16)
            dhr = dza * ga_ref[:, cs]
            a = a_ref[:, cs]
            mult = mult_ref[:, cs]
            ra = ra_ref[:, cs].astype(F32)
            ri = ri_ref[:, cs].astype(F32)
            sp = _softplus(vec[V_A_PARAM:V_A_PARAM + 1])
            m = jnp.where(row == tm - 1, 1.0, _shift_up(a, 1))
            gsum = _scan_up(m, dhr, scan_carry[0:1, cs], scan_scr)
            scan_carry[0:1, cs] = a[0:1, :] * gsum[0:1, :]
            hh = jnp.where(first_t, 0.0, hh_ref[:, cs])
            hprev = _shift_down(jnp.concatenate([hh, hr], axis=0), 1)[8:]
            da = gsum * hprev
            dmult = jnp.where(is_t0, 0.0, gsum * xr * ri)
            dlog_a = da * a - dmult * a * a / mult
            dri = gsum * xr * mult
            dxr = gsum * ri * mult
            small_ref[7:8, cs] += colsum((-C_RG) * ra * dlog_a)
            dpa = (((-C_RG) * sp) * dlog_a * ra * (1.0 - ra))
            dpx = dri * ri * (1.0 - ri)
            small_ref[5:6, cs] += colsum(dpa)
            small_ref[6:7, cs] += colsum(dpx)
            dpa = dpa.astype(BF16)
            dpx = dpx.astype(BF16)
            xrb = xr.astype(BF16)
            dwa_acc[g] += _dot_tn(xrb, dpa)
            dwx_acc[g] += _dot_tn(xrb, dpx)
            dxr = dxr + _dot_nt(dpa, wa_ref[g]) + _dot_nt(dpx, wx_ref[g])
            small_ref[4:5, cs] += colsum(dxr)
            xh = jnp.where(first_t, 0.0, xh_ref[:, cs])
            taps = _conv_taps(jnp.concatenate([xh, x_ref[:, cs]], axis=0))
            dxr_ext = jnp.concatenate([dxr, dxr_carry[:, cs]], axis=0)
            dx = vec[V_CONV_W + 3:V_CONV_W + 4] * dxr
            for j in range(4):
                small_ref[j:j + 1, cs] += colsum(dxr * taps[j])
                if j < 3:
                    dx = dx + vec[V_CONV_W + j:V_CONV_W + j + 1] * _shift_up(dxr_ext, 3 - j)[:tm]
            dxr_carry[:, cs] = dxr[0:8, :]
            dproj_ref[:, cs] = dx.astype(BF16)
            pg = p_ref[:, cs]
            dpooled = dpooled_ref[:, cs]
            pb = _dot(pg, wp_ref[g]) + vec[V_B_POOL:V_B_POOL + 1]
            small_ref[9:10, cs] += colsum(dpooled * pb)
            dpb = dpooled * vec[V_POOL_SCALE:V_POOL_SCALE + 1]
            small_ref[8:9, cs] += colsum(dpb)
            dpbb = dpb.astype(BF16)
            dwp_acc[g] += _dot_tn(pg, dpbb)
            dp = _dot_nt(dpbb, wp_ref[g])
            q = _window_mean(dp, POOL_WINDOWS[g], first_t, head_t)
            sm = jnp.concatenate([q, q_carry[:, cs]], axis=0)
            k = 1
            while k < POOL_WINDOWS[g]:
                sm = sm + _shift_up(sm, k)
                k *= 2
            q_carry[:, cs] = q[0:HALO_U, :]
            dproj_ref[:, 2 * D + g * GW:2 * D + (g + 1) * GW] = (sm[:tm] - dp).astype(BF16)
        dproj_ref[:, 3 * D:] = dg_ref[...]

        @pl.when(i == nb - 1)
        def _():
            dwa_ref[...] = dwa_acc[...].astype(BF16)
            dwx_ref[...] = dwx_acc[...].astype(BF16)
            dwp_ref[...] = dwp_acc[...].astype(BF16)

    rev = lambda i: nb - 1 - i
    tok = pl.BlockSpec((tm, D), lambda i: (rev(i), 0))
    halo8 = lambda k: pl.BlockSpec((8, D), lambda i: (jnp.maximum(rev(i) * (tm // 8) - 1, 0), k))
    wspec = pl.BlockSpec((N_GROUPS, GW, GW), lambda i: (0, 0, 0))
    wshape = jax.ShapeDtypeStruct((N_GROUPS, GW, GW), BF16)
    return pl.pallas_call(
        body, name="mix_bwd", grid=(nb,),
        in_specs=[tok, tok, halo8(0), tok, tok, tok, tok, halo8(0), tok, tok, tok, tok, tok, tok,
                  pl.BlockSpec((tm, 2 * D), lambda i: (rev(i), 0)),
                  pl.BlockSpec((16, D), lambda i: (0, 0)), wspec, wspec, wspec, pl.BlockSpec(memory_space=pl.ANY)],
        out_specs=[pl.BlockSpec((tm, D_IN), lambda i: (rev(i), 0)), wspec, wspec, wspec,
                   pl.BlockSpec((16, D), lambda i: (0, 0))],
        out_shape=[jax.ShapeDtypeStruct((s, D_IN), BF16), wshape, wshape, wshape,
                   jax.ShapeDtypeStruct((16, D), F32)],
        scratch_shapes=[pltpu.VMEM((8, D), F32), pltpu.VMEM((8, D), F32), pltpu.VMEM((HALO_U, D), F32),
                        pltpu.VMEM((3, tm, LANES), F32)] + [pltpu.VMEM((N_GROUPS, GW, GW), F32)] * 3,
        compiler_params=_params(("arbitrary",)),
    )(dza, dpooled, x_rnn, x_rnn, ga, dga, xr, hr, hr, p, *gates, dgates, vecs, w_rg_a, w_rg_x, w_pool, dep)


def _proj_bwd(dproj, x, dx2, modr, vecs, w_in, dep):
    s = x.shape[0]
    tm = min(TM_PROJ, s)

    def body(dp_ref, x_ref, dx2_ref, mod_ref, vec_ref, w_ref, dep_ref, gx_ref, small_ref):
        @pl.when(pl.program_id(0) == 0)
        def _():
            small_ref[...] = jnp.zeros_like(small_ref)

        dh1 = None
        for c in range(D_IN // D):
            cs = slice(c * D, (c + 1) * D)
            part = _dot_nt(dp_ref[:, cs], w_ref[:, cs])
            dh1 = part if dh1 is None else dh1 + part
        xv = x_ref[...]
        r1 = lax.rsqrt(jnp.mean(xv * xv, axis=-1, keepdims=True) + EPS)
        xn1 = xv * r1
        gain = vec_ref[V_G1:V_G1 + 1, :] * (1.0 + mod_ref[M_SC1:M_SC1 + 1, :])
        dxn1 = dh1 * gain
        gx_ref[...] = dx2_ref[...] + r1 * (dxn1 - xn1 * jnp.mean(dxn1 * xn1, axis=-1, keepdims=True))
        small_ref[0:1, :] += jnp.sum(dh1, axis=0, keepdims=True)
        small_ref[1:2, :] += jnp.sum(dh1 * xn1, axis=0, keepdims=True)

    tok = pl.BlockSpec((tm, D), lambda i: (i, 0))
    return pl.pallas_call(
        body, name="proj_bwd", grid=(s // tm,),
        in_specs=[pl.BlockSpec((tm, D_IN), lambda i: (i, 0)), tok, tok,
                  pl.BlockSpec((8, D), lambda i: (0, 0)), pl.BlockSpec((16, D), lambda i: (0, 0)),
                  _resident((D, D_IN)), pl.BlockSpec(memory_space=pl.ANY)],
        out_specs=[tok, pl.BlockSpec((8, D), lambda i: (0, 0))],
        out_shape=[jax.ShapeDtypeStruct((s, D), F32), jax.ShapeDtypeStruct((8, D), F32)],
        compiler_params=_params(("arbitrary",)),
    )(dproj, x, dx2, modr, vecs, w_in, dep)


def _wgrad(a, b, name, square_a=False, dep=None):
    s, ka = a.shape
    n = b.shape[1]
    tka = ka if ka <= 1024 else ka // 2
    tn = n if n <= 1024 else n // 2
    ts = min(TS_WGRAD, s)
    ns = s // ts
    nc = 512
    deps = [] if dep is None else [dep]

    def body(a_ref, b_ref, *refs):
        out_ref, acc_ref = refs[-2:]
        t = pl.program_id(2)

        @pl.when(t == 0)
        def _():
            acc_ref[...] = jnp.zeros_like(acc_ref)

        av = a_ref[...]
        if square_a:
            af = av.astype(F32)
            av = (af * af).astype(BF16)
        for c in range(tn // nc):
            cs = slice(c * nc, (c + 1) * nc)
            acc_ref[:, cs] += _dot_tn(av, b_ref[:, cs])

        @pl.when(t == ns - 1)
        def _():
            out_ref[...] = acc_ref[...].astype(BF16)

    return pl.pallas_call(
        body, name=name, grid=(ka // tka, n // tn, ns),
        in_specs=[pl.BlockSpec((ts, tka), lambda i, j, t: (t, i)),
                  pl.BlockSpec((ts, tn), lambda i, j, t: (t, j))] + [pl.BlockSpec(memory_space=pl.ANY)] * len(deps),
        out_specs=pl.BlockSpec((tka, tn), lambda i, j, t: (i, j)),
        out_shape=jax.ShapeDtypeStruct((ka, n), BF16),
        scratch_shapes=[pltpu.VMEM((tka, tn), F32)],
        compiler_params=_params(("parallel", "parallel", "arbitrary")),
    )(a, b, *deps)


def _window(ref, kind, idx, size):
    start = pl.multiple_of(idx * size, size)
    if kind == 0:
        return ref.at[pl.ds(start, size)]
    if kind == 1:
        return ref.at[:, pl.ds(start, size)]
    return ref.at[:, :, pl.ds(start, size)]


def _mesh_place():
    x, y, c = lax.axis_index("x"), lax.axis_index("y"), lax.axis_index("c")
    return x, y, c, 4 * x + 2 * y + c


def _peer(x, y, c, q):
    px = 1 - x if q & 4 else x
    py = 1 - y if q & 2 else y
    pc = 1 - c if q & 1 else c
    return (px, py, pc), 4 * px + 2 * py + pc


_HBM = pl.BlockSpec(memory_space=pltpu.HBM)
_SEM = pl.BlockSpec(memory_space=pltpu.SEMAPHORE)
_EFFECT = pltpu.SideEffectType.DATAFLOW_SIDE_EFFECTING


N_NEAR = 4


def _near(x, y, c):
    out = [((x, y, 1 - c), 4 * x + 2 * y + 1 - c)]
    for j in (1, 2, 3):
        px = 1 - x if j & 2 else x
        py = 1 - y if j & 1 else y
        out.append(((px, py, c), 4 * px + 2 * py + c))
    return out


def _remote(src, dst, send_sems, recv_sems, slot, device):
    return pltpu.make_async_remote_copy(src_ref=src, dst_ref=dst, send_sem=send_sems.at[slot], recv_sem=recv_sems.at[slot],
                                        device_id=device, device_id_type=MESH)


def _split_call(name, arrays, sems_in, n_new_sems, after, emit):
    na, ns, nn = len(arrays), len(sems_in), len(n_new_sems)

    def body(*refs):
        emit(refs[:na], refs[na:na + ns], refs[na + ns + 1:na + ns + 1 + nn])
        refs[-1][...] = jnp.zeros_like(refs[-1])

    outs = pl.pallas_call(
        body, name=name,
        out_shape=(*[pltpu.SemaphoreType.DMA((m,)) for m in n_new_sems],
                   *[pltpu.HBM(a.shape, a.dtype) for a in arrays], jax.ShapeDtypeStruct((8, 128), F32)),
        in_specs=[_HBM] * na + [_SEM] * ns + [pl.BlockSpec(memory_space=pl.ANY)],
        out_specs=(*[_SEM] * nn, *[_HBM] * na, pl.BlockSpec(memory_space=pltpu.VMEM)),
        input_output_aliases={i: nn + i for i in range(na)},
        compiler_params=pltpu.CompilerParams(has_side_effects=_EFFECT),
    )(*[pltpu.with_memory_space_constraint(a, pltpu.HBM) for a in arrays], *sems_in, after)
    return list(outs[:nn]), list(outs[nn:nn + na]), outs[-1]


def _together(name, steps, after):
    parts = [(ex.arrays, ex.sems, ex.new_sems[step], getattr(ex, "emit_" + step)) for ex, step in steps]

    def emit(arr, old, new):
        ia = io = ib = 0
        for arrays, sems, new_sems, emit_one in parts:
            emit_one(arr[ia:ia + len(arrays)], old[io:io + len(sems)], new[ib:ib + len(new_sems)])
            ia, io, ib = ia + len(arrays), io + len(sems), ib + len(new_sems)

    new, arrays, token = _split_call(name, [a for p in parts for a in p[0]], [s for p in parts for s in p[1]],
                                     [m for p in parts for m in p[2]], after, emit)
    out = []
    ia = ib = 0
    for (ex, _), (arrs, sems, new_sems, _) in zip(steps, parts):
        ex.arrays, ex.sems, ex.token = arrays[ia:ia + len(arrs)], [*sems, *new[ib:ib + len(new_sems)]], token
        ia, ib = ia + len(arrs), ib + len(new_sems)
        out.append(ex.arrays[ex.n:])
    return out


class _AllGather:
    def __init__(self, shards, kinds, name):
        self.n, self.kinds, self.name = len(shards), kinds, name
        self.sizes = [s.shape[k] for s, k in zip(shards, kinds)]
        lands = []
        for s, k in zip(shards, kinds):
            dims = list(s.shape)
            dims[k] *= N_DEV
            lands.append(lax.empty(tuple(dims), s.dtype))
        self.arrays, self.sems = [*shards, *lands], []

    def window(self, arr, k, idx):
        return _window(arr[self.n + k], self.kinds[k], idx, self.sizes[k])

    def start(self, after):
        _together(self.name + "_start", [(self, "start")], after)

    def forward(self, after):
        _together(self.name + "_forward", [(self, "forward")], after)

    def finish(self, after):
        return _together(self.name + "_finish", [(self, "finish")], after)[0]


class _Gather(_AllGather):
    def __init__(self, shards, kinds, name):
        super().__init__(shards, kinds, name)
        n = self.n
        self.new_sems = dict(start=[n * N_NEAR, n * N_NEAR, n], forward=[n * N_NEAR] * 2, finish=[])

    def emit_start(self, arr, _, new):
        x, y, c, me = _mesh_place()
        for k in range(self.n):
            pltpu.make_async_copy(arr[k], self.window(arr, k, me), new[2].at[k]).start()
        for k in range(self.n):
            for j, (dev, _) in enumerate(_near(x, y, c)):
                _remote(arr[k], self.window(arr, k, me), new[0], new[1], k * N_NEAR + j, dev).start()

    def emit_forward(self, arr, old, new):
        x, y, c, _ = _mesh_place()
        near = _near(x, y, c)
        for k in range(self.n):
            for j in (1, 2, 3):
                dev, idx = near[j]
                landed = self.window(arr, k, idx)
                _remote(arr[k], landed, old[0], old[1], k * N_NEAR + j, dev).wait_recv()
                _remote(landed, landed, new[0], new[1], k * N_NEAR + j, near[0][0]).start()

    def emit_finish(self, arr, old, _):
        x, y, c, me = _mesh_place()
        near = _near(x, y, c)
        other_core = near[0][0]
        for k in range(self.n):
            win = lambda idx: self.window(arr, k, idx)
            pltpu.make_async_copy(arr[k], win(me), old[2].at[k]).wait()
            for j, (dev, idx) in enumerate(near):
                _remote(arr[k], win(me), old[0], old[1], k * N_NEAR + j, dev).wait_send()
            _remote(arr[k], win(near[0][1]), old[0], old[1], k * N_NEAR, other_core).wait_recv()
            for j in (1, 2, 3):
                idx = near[j][1]
                _remote(win(idx), win(idx), old[3], old[4], k * N_NEAR + j, other_core).wait_send()
                _remote(arr[k], win(idx + 1 - 2 * c), old[3], old[4], k * N_NEAR + j, other_core).wait_recv()


class _Spread(_AllGather):
    def __init__(self, shards, kinds, name):
        super().__init__(shards, kinds, name)
        n = self.n
        self.new_sems = dict(start=[n * N_DEV, n * N_DEV, n], finish=[])

    def emit_start(self, arr, _, new):
        x, y, c, me = _mesh_place()
        for k in range(self.n):
            mine = self.window(arr, k, me)
            pltpu.make_async_copy(arr[k], mine, new[2].at[k]).start()
            for q in range(1, N_DEV):
                _remote(arr[k], mine, new[0], new[1], k * N_DEV + q, _peer(x, y, c, q)[0]).start()

    def emit_finish(self, arr, old, _):
        x, y, c, me = _mesh_place()
        for k in range(self.n):
            win = lambda idx: self.window(arr, k, idx)
            pltpu.make_async_copy(arr[k], win(me), old[2].at[k]).wait()
            for q in range(1, N_DEV):
                peer, peer_idx = _peer(x, y, c, q)
                _remote(arr[k], win(me), old[0], old[1], k * N_DEV + q, peer).wait_send()
                _remote(arr[k], win(peer_idx), old[0], old[1], k * N_DEV + q, peer).wait_recv()


class _Scatter:
    def __init__(self, partials, kinds, after, name):
        self.n, self.kinds, self.name, self.partials = len(partials), kinds, name, partials
        self.sizes = [p.shape[k] // N_DEV for p, k in zip(partials, kinds)]
        n, sizes = self.n, self.sizes
        self.slot_shapes = []
        for p, k, size in zip(partials, kinds, sizes):
            dims = list(p.shape)
            dims[k] = size
            self.slot_shapes.append((N_NEAR, *dims))
        slots = [lax.empty(sh, p.dtype) for sh, p in zip(self.slot_shapes, partials)]

        def emit(arr, _, new):
            x, y, c, _ = _mesh_place()
            near = _near(x, y, c)
            for k in range(n):
                for j in range(N_NEAR):
                    owner = near[j][1] if j == 0 else near[j][1] + 1 - 2 * c
                    _remote(_window(arr[k], kinds[k], owner, sizes[k]), arr[n + k].at[j], new[0], new[1],
                            k * N_NEAR + j, near[0][0]).start()

        self.sems, self.arrays, self.token = _split_call(name + "_start", [*partials, *slots], [], [n * N_NEAR] * 2,
                                                         after, emit)

    def combine_and_send(self, own4, after):
        n, kinds, sizes = self.n, self.kinds, self.sizes

        def emit_wait(arr, old, _):
            x, y, c, _ = _mesh_place()
            near = _near(x, y, c)
            for k in range(n):
                for j in range(N_NEAR):
                    owner = near[j][1] if j == 0 else near[j][1] + 1 - 2 * c
                    cp = _remote(_window(arr[k], kinds[k], owner, sizes[k]), arr[n + k].at[j], old[0], old[1],
                                 k * N_NEAR + j, near[0][0])
                    cp.wait_send()
                    cp.wait_recv()

        _, arrays, _ = _split_call(self.name + "_landed", self.arrays, self.sems, [], after, emit_wait)
        chip_sums = _chip_sums(arrays[:n], arrays[n:], kinds, sizes, own4, self.name + "_combine")
        arrivals = [lax.empty((N_NEAR - 1, *sh[1:]), p.dtype) for sh, p in zip(self.slot_shapes, self.partials)]

        def emit_send(arr, _, new):
            x, y, c, _ = _mesh_place()
            near = _near(x, y, c)
            for k in range(n):
                for j in (1, 2, 3):
                    _remote(arr[k].at[j], arr[n + k].at[j - 1], new[0], new[1], k * N_NEAR + j, near[j][0]).start()

        self.sems, self.arrays, self.token = _split_call(self.name + "_send", [*chip_sums, *arrivals], [],
                                                         [n * N_NEAR] * 2, own4, emit_send)

    def finish(self, after):
        n = self.n

        def emit(arr, old, _):
            x, y, c, _ = _mesh_place()
            near = _near(x, y, c)
            for k in range(n):
                for j in (1, 2, 3):
                    cp = _remote(arr[k].at[j], arr[n + k].at[j - 1], old[0], old[1], k * N_NEAR + j, near[j][0])
                    cp.wait_send()
                    cp.wait_recv()

        _, arrays, _ = _split_call(self.name + "_finish", self.arrays, self.sems, [], after, emit)
        return arrays[:n], arrays[n:]


def _chip_sums(partials, slots, kinds, sizes, own4, name):
    n = len(partials)

    def body(own_ref, *refs):
        for k in range(n):
            refs[2 * n + k][...] = (refs[k][...].astype(F32) + refs[n + k][...].astype(F32)).astype(BF16)

    in_specs, slot_specs = [], []
    for p, s, kind, size in zip(partials, slots, kinds, sizes):
        block = list(p.shape)
        block[kind] = size
        nd = len(block)
        in_specs.append(pl.BlockSpec(tuple(block), functools.partial(
            lambda j, own, kind, nd: tuple(own[j] if d == kind else 0 for d in range(nd)), kind=kind, nd=nd)))
        slot_specs.append(pl.BlockSpec((None, *block), functools.partial(
            lambda j, own, nd: (j,) + (0,) * nd, nd=nd)))
    return pl.pallas_call(
        body, name=name,
        grid_spec=pltpu.PrefetchScalarGridSpec(num_scalar_prefetch=1, grid=(N_NEAR,),
                                               in_specs=in_specs + slot_specs, out_specs=slot_specs),
        out_shape=[jax.ShapeDtypeStruct(s.shape, s.dtype) for s in slots],
        compiler_params=_params(("arbitrary",)),
    )(own4, *partials, *slots)


def _to_bf16(arrays, name, dep=None):
    n = len(arrays)
    deps = [] if dep is None else [dep]

    def body(*refs):
        for src, dst in zip(refs[:n], refs[n + len(deps):]):
            dst[...] = src[...].astype(BF16)

    vmem = pl.BlockSpec(memory_space=pltpu.VMEM)
    return pl.pallas_call(body, name=name, out_shape=[jax.ShapeDtypeStruct(a.shape, BF16) for a in arrays],
                          in_specs=[vmem] * n + [pl.BlockSpec(memory_space=pl.ANY)] * len(deps), out_specs=[vmem] * n,
                          compiler_params=pltpu.CompilerParams(vmem_limit_bytes=V7X_VMEM_LIMIT))(*arrays, *deps)


def _silu(c):
    return c * _sigmoid_tail(c)


def _ada_fwd(c_all, w_ada, b_ada_cols, dep):
    def body(c_ref, w_ref, b_ref, dep_ref, out_ref):
        out_ref[...] = jnp.dot(_silu(c_ref[...]), w_ref[...], preferred_element_type=F32,
                               precision=lax.Precision.HIGHEST) + b_ref[...]

    vmem = pl.BlockSpec(memory_space=pltpu.VMEM)
    return pl.pallas_call(
        body, name="ada_fwd", in_specs=[vmem, vmem, vmem, pl.BlockSpec(memory_space=pl.ANY)], out_specs=vmem,
        out_shape=jax.ShapeDtypeStruct((N_DEV, w_ada.shape[1]), F32),
    )(c_all, w_ada, b_ada_cols, dep)


def _adam(w, g, m, v):
    m = ADAM_B1 * m + (1.0 - ADAM_B1) * g
    v = ADAM_B2 * v + (1.0 - ADAM_B2) * (g * g)
    m_hat = m / (1.0 - ADAM_B1 ** ADAM_STEP)
    v_hat = v / (1.0 - ADAM_B2 ** ADAM_STEP)
    delta = -ADAM_LR * (m_hat / (jnp.sqrt(v_hat) + ADAM_EPS) + ADAM_WD * w)
    return delta, m, v


def _ada_bwd_adam(c_all, dmod_cols, w, m, v):
    def body(c_ref, d_ref, w_ref, m_ref, v_ref, g_ref, delta_ref, nm_ref, nv_ref):
        g = lax.dot_general(_silu(c_ref[...]), d_ref[...], (((0,), (0,)), ((), ())),
                            preferred_element_type=F32, precision=lax.Precision.HIGHEST)
        g_ref[...] = g
        delta_ref[...], nm_ref[...], nv_ref[...] = _adam(w_ref[...], g, m_ref[...], v_ref[...])

    sd = jax.ShapeDtypeStruct(w.shape, F32)
    return pl.pallas_call(body, name="ada_bwd_adam", out_shape=[sd] * 4,
                          compiler_params=pltpu.CompilerParams(vmem_limit_bytes=V7X_VMEM_LIMIT),
                          )(c_all, dmod_cols, w, m, v)


def _adam_group(chip_sums, arrivals, ws, ms, vs, n_tiles, name):
    n = len(ws)

    def body(*refs):
        for k in range(n):
            c_ref, a_ref, w_ref, m_ref, v_ref = (refs[j * n + k] for j in range(5))
            g_ref, delta_ref, nm_ref, nv_ref = (refs[(5 + j) * n + k] for j in range(4))
            g = c_ref[...].astype(F32)
            for j in range(N_NEAR - 1):
                g = g + a_ref[j].astype(F32)
            g_ref[...] = g
            delta_ref[...], nm_ref[...], nv_ref[...] = _adam(w_ref[...], g, m_ref[...], v_ref[...])

    tiles = [(w.shape[0] // n_tiles, w.shape[1]) for w in ws]
    blk = [pl.BlockSpec(t, lambda i: (i, 0)) for t in tiles]
    return pl.pallas_call(
        body, name=name, grid=(n_tiles,),
        in_specs=[pl.BlockSpec((None, *t), lambda i: (0, i, 0)) for t in tiles]
        + [pl.BlockSpec((N_NEAR - 1, *t), lambda i: (0, i, 0)) for t in tiles] + blk * 3,
        out_specs=blk * 4, out_shape=[jax.ShapeDtypeStruct(w.shape, F32) for w in ws] * 4,
        compiler_params=_params(("parallel",)),
    )(*chip_sums, *arrivals, *ws, *ms, *vs)


N_SMALL = 40
N_SMALL_PARAMS = 11


def _pack_vecs(conv_w_full, rows):
    def body(cw_ref, *refs):
        out = refs[-1]
        out[...] = jnp.zeros_like(out)
        out[0:4, :] = cw_ref[0:4, :]
        for r, ref in enumerate(refs[:-1]):
            out[4 + r:5 + r, :] = ref[...]

    return pl.pallas_call(body, name="pack_vecs", out_shape=jax.ShapeDtypeStruct((16, D), F32))(conv_w_full, *rows)


def _small_finish(gathered, conv_cols, mod_all, vecs, ws, ms, vs):
    n = N_SMALL_PARAMS

    def body(g_ref, conv_ref, mod_ref, vec_ref, *refs):
        w_refs, m_refs, v_refs = refs[:n], refs[n:2 * n], refs[2 * n:3 * n]
        outs = refs[3 * n:]
        g1 = vec_ref[V_G1:V_G1 + 1, :]
        g2 = vec_ref[V_G2:V_G2 + 1, :]
        zero = jnp.zeros((1, D), F32)
        dg1, dg2, dgf, loss_lanes = zero, zero, zero, zero
        mixer = jnp.zeros((16, D), F32)
        db_ada = jnp.zeros((6, D), F32)
        d_conv_w = jnp.zeros(conv_ref.shape[1:], F32)
        for b in range(N_DEV):
            gb = g_ref[b]
            mod = mod_ref[b]
            q1 = gb[33:34]
            q2 = gb[9:10]
            dmod = jnp.concatenate([gb[32:33], q1 * g1, gb[10:11], gb[8:9], q2 * g2, gb[1:2]], axis=0)
            outs[4 * n][b] = dmod
            db_ada = db_ada + dmod
            dg1 = dg1 + q1 * (1.0 + mod[M_SC1:M_SC1 + 1])
            dg2 = dg2 + q2 * (1.0 + mod[M_SC2:M_SC2 + 1])
            dgf = dgf + gb[0:1]
            loss_lanes = loss_lanes + gb[2:3]
            mixer = mixer + gb[16:32]
            d_conv_w = d_conv_w + conv_ref[b]
        d_a_param = mixer[7:8] * _sigmoid_tail(vec_ref[V_A_PARAM:V_A_PARAM + 1, :])
        grads = [dg1, dg2, mixer[4:5], mixer[5:6], mixer[6:7], d_a_param, mixer[8:9], mixer[9:10], dgf,
                 db_ada, d_conv_w]

        def load(ref, rows):
            if ref.shape[0] == rows:
                return ref[...]
            return jnp.concatenate([ref[:, j * D:(j + 1) * D] for j in range(rows)], axis=0)

        def store(ref, val):
            if ref.shape == val.shape:
                ref[...] = val
            else:
                for j in range(val.shape[0]):
                    ref[:, j * D:(j + 1) * D] = val[j:j + 1]

        for k in range(n):
            rows = grads[k].shape[0]
            results = (grads[k], *_adam(load(w_refs[k], rows), grads[k], load(m_refs[k], rows), load(v_refs[k], rows)))
            for which, val in enumerate(results):
                store(outs[which * n + k], val)
        outs[4 * n + 1][...] = jnp.broadcast_to(jnp.sum(loss_lanes, axis=1, keepdims=True), (8, 128))

    shapes = [jax.ShapeDtypeStruct(w.shape, F32) for w in ws]
    return pl.pallas_call(
        body, name="small_finish",
        out_shape=shapes * 4 + [jax.ShapeDtypeStruct((N_DEV, 6, D), F32), jax.ShapeDtypeStruct((8, 128), F32)],
    )(gathered, conv_cols, mod_all, vecs, *ws, *ms, *vs)


def _pad_rows(a, rows):
    return jnp.pad(a, ((0, rows - a.shape[0]), (0, 0)))


def kernel(x, c, norm_mix_g, norm_mlp_g, w_ada, b_ada, w_in, conv_w, conv_b, w_rg_a, b_rg_a, w_rg_x, b_rg_x, a_param, w_branch_a, w_pool, b_pool, pool_scale, w_branch_b, w_out, w_up, w_down, final_g, loss_target, m_norm_mix_g, m_norm_mlp_g, m_w_ada, m_b_ada, m_w_in, m_conv_w, m_conv_b, m_w_rg_a, m_b_rg_a, m_w_rg_x, m_b_rg_x, m_a_param, m_w_branch_a, m_w_pool, m_b_pool, m_pool_scale, m_w_branch_b, m_w_out, m_w_up, m_w_down, m_final_g, v_norm_mix_g, v_norm_mlp_g, v_w_ada, v_b_ada, v_w_in, v_conv_w, v_conv_b, v_w_rg_a, v_b_rg_a, v_w_rg_x, v_b_rg_x, v_a_param, v_w_branch_a, v_w_pool, v_b_pool, v_pool_scale, v_w_branch_b, v_w_out, v_w_up, v_w_down, v_final_g):
    me = 4 * lax.axis_index("x") + 2 * lax.axis_index("y") + lax.axis_index("c")
    s = x.shape[1]
    x2d = x.reshape(s, D)
    target = loss_target.reshape(s, D)
    n_ada = w_ada.shape[2]

    b_ada_cols = lax.dynamic_slice(b_ada, (0, me * n_ada), (1, n_ada))

    sharded = dict(w_in=(w_in[0], 1), w_up=(w_up[0], 1), w_down=(w_down[0], 0), w_branch_a=(w_branch_a[0], 0),
                   w_branch_b=(w_branch_b[0], 0), w_out=(w_out[0], 0), w_rg_a=(w_rg_a[0], 1), w_rg_x=(w_rg_x[0], 1),
                   w_pool=(w_pool[0], 1))
    kind = {k: v[1] for k, v in sharded.items()}
    first_names = ["w_in"]
    later_names = [k for k in sharded if k not in first_names]
    mix_names = ["w_rg_a", "w_rg_x", "w_pool"]
    branch_names = ["w_branch_a", "w_branch_b", "w_out"]
    mlp_names = ["w_up", "w_down"]

    def gather(group, after, name):
        exchange = _Gather([shard[k] for k in group], [kind[k] for k in group], name)
        exchange.start(after)
        return exchange

    shard = dict(zip(first_names, _to_bf16([sharded[k][0] for k in first_names], "to_bf16_first")))
    spread_c = _Spread([_pad_rows(c, 8), _pad_rows(conv_w[0], 8)], [0, 1], "spread_c")
    g_first = _Gather([shard[k] for k in first_names], [kind[k] for k in first_names], "gather_first")
    _together("first_start", [(spread_c, "start"), (g_first, "start")], c)
    shard.update(zip(later_names, _to_bf16([sharded[k][0] for k in later_names], "to_bf16_later", dep=g_first.token)))

    c_rows, conv_w_full = spread_c.finish(g_first.token)
    c_all = c_rows.reshape(N_DEV, 8, D)[:, 0, :]
    mod_part = _ada_fwd(c_all, w_ada[0], b_ada_cols, g_first.token)
    vecs = _pack_vecs(conv_w_full, [conv_b, b_rg_a, b_rg_x, a_param, b_pool, pool_scale,
                                    norm_mix_g, norm_mlp_g, final_g.reshape(1, D)])
    spread_mod = _Spread([mod_part], [0], "spread_mod")
    spread_mod.start(vecs)
    g_mix = gather(mix_names, spread_mod.token, "gather_mix")
    g_branch = gather(branch_names, g_mix.token, "gather_branch")
    g_mlp = gather(mlp_names, g_branch.token, "gather_mlp")
    g_first.forward(g_mlp.token)
    wg = dict(zip(first_names, g_first.finish(g_first.token)))
    mod_parts, = spread_mod.finish(g_first.token)
    mod_all = jnp.transpose(mod_parts.reshape(N_DEV, N_DEV, n_ada), (1, 0, 2)).reshape(N_DEV, 6, D)
    modr = _pad_rows(lax.dynamic_index_in_dim(mod_all, me, 0, keepdims=False), 8)

    h1, x_rnn, u_pool, ga, dga, sa, sb = _proj_fwd(x2d, modr, vecs, wg["w_in"])
    _together("mixer_forward", [(g_mix, "forward"), (g_branch, "forward")], h1)
    wg.update(zip(mix_names, g_mix.finish(g_branch.token)))
    xr, hr, za, p, pooled, *gates = _mix_fwd(x_rnn, u_pool, ga, vecs, wg["w_rg_a"], wg["w_rg_x"], wg["w_pool"],
                                             dep=g_branch.token)
    g_mlp.forward(za)
    wg.update(zip(branch_names, g_branch.finish(g_mlp.token)))
    ba, bb, merged, o, x2, h2 = _branch_fwd(za, pooled, sa, sb, x2d, modr, vecs,
                                            wg["w_branch_a"], wg["w_branch_b"], wg["w_out"])
    wg.update(zip(mlp_names, g_mlp.finish(h2)))
    ru, dx3, d_dn, small_f = _mlp_fwd(h2, x2, target, modr, vecs, wg["w_up"], wg["w_down"])

    near = _near(lax.axis_index("x"), lax.axis_index("y"), lax.axis_index("c"))
    own4 = jnp.stack([me, near[1][1], near[2][1], near[3][1]]).astype(jnp.int32)

    def scatter(group, partial, after, name):
        return _Scatter([partial[k] for k in group], [kind[k] for k in group], after, name)

    dup, dx2, do, small_m = _mlp_bwd(d_dn, ru, x2, dx3, o, modr, vecs, wg["w_up"], wg["w_down"])
    partial = dict(w_up=_wgrad(h2, dup, "wgrad_up"), w_down=_wgrad(ru, d_dn, "wgrad_down", square_a=True))
    s_mlp = scatter(mlp_names, partial, dx2, "scatter_mlp")

    dba, dbb, dgates, dza, dpooled = _branch_bwd(do, sa, sb, ba, bb, wg["w_branch_a"], wg["w_branch_b"], wg["w_out"],
                                                 dep=s_mlp.token)
    s_mlp.combine_and_send(own4, dza)
    dproj, dw_rg_a, dw_rg_x, dw_pool, small_x = _mix_bwd(dza, dpooled, x_rnn, ga, dga, xr, hr, p, gates, dgates,
                                                         vecs, wg["w_rg_a"], wg["w_rg_x"], wg["w_pool"],
                                                         dep=s_mlp.token)
    partial.update(w_branch_a=_wgrad(za, dba, "wgrad_branch_a"), w_branch_b=_wgrad(pooled, dbb, "wgrad_branch_b"),
                   w_out=_wgrad(merged, do, "wgrad_out"),
                   w_rg_a=dw_rg_a, w_rg_x=dw_rg_x, w_pool=dw_pool)
    mixer_names = ["w_rg_a", "w_rg_x", "w_pool", "w_branch_a", "w_branch_b", "w_out"]
    s_mixer = scatter(mixer_names, partial, s_mlp.token, "scatter_mixer")

    partial["w_in"] = _wgrad(h1, dproj, "wgrad_in", dep=s_mixer.token)
    s_in = scatter(["w_in"], partial, s_mixer.token, "scatter_in")
    s_mixer.combine_and_send(own4, s_in.token)
    s_in.combine_and_send(own4, s_mixer.token)
    grad_x, small_p = _proj_bwd(dproj, x2d, dx2, modr, vecs, wg["w_in"], dep=s_in.token)

    locals_ = dict(w_in=(w_in, m_w_in, v_w_in), w_up=(w_up, m_w_up, v_w_up), w_down=(w_down, m_w_down, v_w_down),
                   w_branch_a=(w_branch_a, m_w_branch_a, v_w_branch_a),
                   w_branch_b=(w_branch_b, m_w_branch_b, v_w_branch_b), w_out=(w_out, m_w_out, v_w_out),
                   w_rg_a=(w_rg_a, m_w_rg_a, v_w_rg_a), w_rg_x=(w_rg_x, m_w_rg_x, v_w_rg_x),
                   w_pool=(w_pool, m_w_pool, v_w_pool))
    res = {}

    def finish(group, exchange, after, n_tiles, name):
        chip_sums, arrivals = exchange.finish(after)
        flat = lambda t: t.reshape(-1, t.shape[-1])
        shapes = [flat(locals_[k][0]).shape for k in group]
        outs = _adam_group([cs.reshape(N_NEAR, *sh) for cs, sh in zip(chip_sums, shapes)],
                           [ar.reshape(N_NEAR - 1, *sh) for ar, sh in zip(arrivals, shapes)],
                           *[[flat(locals_[k][j]) for k in group] for j in range(3)], n_tiles, name)
        for i, k in enumerate(group):
            res[k] = [outs[j * len(group) + i].reshape(locals_[k][0].shape) for j in range(4)]
        return res[group[-1]][0]

    small = jnp.concatenate([small_f, small_m, small_x, small_p], axis=0)
    g_small = _Spread([small], [0], "spread_small")
    g_small.start(grad_x)
    done = finish(mlp_names, s_mlp, g_small.token, 4, "adam_mlp")
    done = finish(mixer_names, s_mixer, done, 2, "adam_mixer")
    done = finish(["w_in"], s_in, done, 4, "adam_in")
    small_all, = g_small.finish(done)
    small_all = small_all.reshape(N_DEV, N_SMALL, D)

    conv_cols = lax.dynamic_slice(small_all, (0, 16, me * (D // N_DEV)), (N_DEV, 4, D // N_DEV))

    def smalls(ng, nl, cb, bra, brx, ap, bp, ps, fg, ba_, cw):
        return [ng, nl, cb, bra, brx, ap, bp, ps, fg.reshape(1, D), ba_, cw[0]]

    small_names = ["norm_mix_g", "norm_mlp_g", "conv_b", "b_rg_a", "b_rg_x", "a_param", "b_pool", "pool_scale",
                   "final_g", "b_ada", "conv_w"]
    fin = _small_finish(
        small_all, conv_cols, mod_all, vecs,
        smalls(norm_mix_g, norm_mlp_g, conv_b, b_rg_a, b_rg_x, a_param, b_pool, pool_scale, final_g, b_ada, conv_w),
        smalls(m_norm_mix_g, m_norm_mlp_g, m_conv_b, m_b_rg_a, m_b_rg_x, m_a_param, m_b_pool, m_pool_scale,
               m_final_g, m_b_ada, m_conv_w),
        smalls(v_norm_mix_g, v_norm_mlp_g, v_conv_b, v_b_rg_a, v_b_rg_x, v_a_param, v_b_pool, v_pool_scale,
               v_final_g, v_b_ada, v_conv_w))
    dmod_all, loss_tile = fin[4 * N_SMALL_PARAMS], fin[4 * N_SMALL_PARAMS + 1]
    dmod_cols = lax.dynamic_slice(dmod_all.reshape(N_DEV, 6 * D), (0, me * n_ada), (N_DEV, n_ada))
    res["w_ada"] = [t.reshape(w_ada.shape) for t in _ada_bwd_adam(c_all, dmod_cols, w_ada[0], m_w_ada[0], v_w_ada[0])]

    def final_shape(k, t):
        if k == "final_g":
            return t.reshape(D)
        if k == "conv_w":
            return t.reshape(conv_w.shape)
        return t

    for i, k in enumerate(small_names):
        res[k] = [final_shape(k, fin[which * N_SMALL_PARAMS + i]) for which in range(4)]
    order = ["norm_mix_g", "norm_mlp_g", "w_ada", "b_ada", "w_in", "conv_w", "conv_b", "w_rg_a", "b_rg_a", "w_rg_x",
             "b_rg_x", "a_param", "w_branch_a", "w_pool", "b_pool", "pool_scale", "w_branch_b", "w_out", "w_up",
             "w_down", "final_g"]
    outs = [loss_tile[0, 0], grad_x.reshape(x.shape)]
    for which in range(4):
        for k in order:
            outs.append(res[k][which])
    return tuple(outs)
```

```python
import functools

import jax
import jax.numpy as jnp
from jax import lax
from jax.experimental import pallas as pl
from jax.experimental.pallas import tpu as pltpu

F32 = jnp.float32
BF16 = jnp.bfloat16
MESH = pl.DeviceIdType.MESH

N_DEV = 8
D = 1024
N_GROUPS = 4
GW = D // N_GROUPS
D_IN = 5 * D
D_FF = 4 * D
POOL_WINDOWS = (2, 4, 8, 16)
HALO_X = 8
HALO_U = 16
EPS = 1e-6
C_RG = 8.0
ADAM_LR, ADAM_B1, ADAM_B2, ADAM_EPS, ADAM_WD, ADAM_STEP = 0.001, 0.9, 0.999, 1e-08, 0.01, 10

V7X_VMEM_LIMIT = 56 * 1024 * 1024

V_CONV_W, V_CONV_B, V_B_RG_A, V_B_RG_X, V_A_PARAM, V_B_POOL, V_POOL_SCALE, V_G1, V_G2, V_GF = 0, 4, 5, 6, 7, 8, 9, 10, 11, 12
M_SH1, M_SC1, M_GT1, M_SH2, M_SC2, M_GT2 = 0, 1, 2, 3, 4, 5

TM_PROJ = 512
TM_MIX = 256
TM_BRANCH = 512
TM_MLP = 512
TM_MLP_BWD = 256
TS_WGRAD = 1024


def _params(semantics):
    return pltpu.CompilerParams(dimension_semantics=semantics, vmem_limit_bytes=V7X_VMEM_LIMIT)


def _resident(shape):
    return pl.BlockSpec(shape, lambda *_: (0,) * len(shape), pipeline_mode=pl.Buffered(1))


def _dot(a, b):
    return jnp.dot(a, b, preferred_element_type=F32)


def _dot_nt(a, b):
    return lax.dot_general(a, b, (((1,), (1,)), ((), ())), preferred_element_type=F32)


def _dot_tn(a, b):
    return lax.dot_general(a, b, (((0,), (0,)), ((), ())), preferred_element_type=F32)


def _sigmoid(x):
    return 0.5 * jnp.tanh(0.5 * x) + 0.5


def _sigmoid_tail(x):
    return 1.0 / (1.0 + jnp.exp(-x))


def _gelu_and_grad(x):
    k = 0.7978845608028654
    x2 = x * x
    t = jnp.tanh(k * (x + 0.044715 * x * x2))
    g = 0.5 * x * (1.0 + t)
    dg = 0.5 * (1.0 + t) + 0.5 * x * (1.0 - t * t) * (k * (1.0 + 3.0 * 0.044715 * x2))
    return g, dg


def _softplus(a):
    e = jnp.exp(-jnp.abs(a))
    u = 1.0 + e
    log1p_e = jnp.where(u == 1.0, e, jnp.log(u) * e / jnp.where(u == 1.0, 1.0, u - 1.0))
    return jnp.maximum(a, 0.0) + log1p_e


def _neg_expm1(z):
    series = -(z * (1.0 + z * (0.5 + z * (1.0 / 6.0 + z * (1.0 / 24.0 + z * (1.0 / 120.0))))))
    return jnp.where(z > -0.1, series, 1.0 - jnp.exp(z))


def _shift_down(x, k):
    return pltpu.roll(x, k, 0)


def _shift_up(x, k):
    return pltpu.roll(x, x.shape[0] - k, 0)


def _rglru_gates(xr, w_a, w_x, b_a, b_x, a_param, is_t0):
    xb = xr.astype(BF16)
    ra = _sigmoid(_dot(xb, w_a) + b_a)
    ri = _sigmoid(_dot(xb, w_x) + b_x)
    sp = _softplus(a_param)
    log_a = (-C_RG) * ra * sp
    a = jnp.exp(log_a)
    mult = jnp.where(is_t0, 1.0, jnp.sqrt(_neg_expm1(2.0 * log_a)))
    return ra, ri, sp, a, mult


SUBLANES = 8


LANES = 128


def _scan_strip(a, b, carry, scr, down):
    t = b.shape[0]
    g = t // SUBLANES
    a3 = a.reshape(g, SUBLANES, LANES)
    b3 = b.reshape(g, SUBLANES, LANES)
    sub = lax.broadcasted_iota(jnp.int32, (g, SUBLANES, LANES), 1)
    for k in (1, 2, 4):
        keep = sub >= k if down else sub < SUBLANES - k
        shift = k if down else SUBLANES - k
        b3 = b3 + a3 * jnp.where(keep, pltpu.roll(b3, shift, 1), 0.0)
        a3 = a3 * jnp.where(keep, pltpu.roll(a3, shift, 1), 1.0)
    scr[0] = a3.reshape(t, LANES)
    scr[1] = b3.reshape(t, LANES)
    end_row = SUBLANES - 1 if down else 0
    ag = scr[0, pl.ds(end_row, g, stride=SUBLANES), :]
    bg = scr[1, pl.ds(end_row, g, stride=SUBLANES), :]
    rg = lax.broadcasted_iota(jnp.int32, (g, LANES), 0)
    edge = 0 if down else g - 1
    bg = bg + jnp.where(rg == edge, ag * carry, 0.0)
    k = 1
    while k < g:
        keep = rg >= k if down else rg < g - k
        shift = k if down else g - k
        bg = bg + ag * jnp.where(keep, pltpu.roll(bg, shift, 0), 0.0)
        if 2 * k < g:
            ag = ag * pltpu.roll(ag, shift, 0)
        k *= 2
    entering = jnp.where(rg != edge, pltpu.roll(bg, 1 if down else g - 1, 0), carry)
    for r in range(SUBLANES):
        scr[2, pl.ds(r, g, stride=SUBLANES), :] = entering
    return scr[1] + scr[0] * scr[2], bg[g - 1:g, :]


def _scan_strips(a, b, carry, scr, down):
    outs = [_scan_strip(a[:, c:c + LANES], b[:, c:c + LANES], carry[:, c:c + LANES], scr, down)
            for c in range(0, b.shape[1], LANES)]
    return jnp.concatenate([o[0] for o in outs], axis=1), jnp.concatenate([o[1] for o in outs], axis=1)


def _scan_down(a, b, carry, scr):
    return _scan_strips(a, b, carry, scr, True)


def _scan_up(m, b, carry, scr):
    return _scan_strips(m, b, carry, scr, False)[0]


def _window_mean(sums, window, first_block, head_t):
    scaled = sums * (1.0 / window)
    head = jnp.where(first_block, sums[:HALO_U] / jnp.minimum(head_t, float(window)), scaled[:HALO_U])
    return jnp.concatenate([head, scaled[HALO_U:]], axis=0)


def _conv_taps(x_ext):
    return [_shift_down(x_ext, 3 - j)[HALO_X:] if j < 3 else x_ext[HALO_X:] for j in range(4)]


def _proj_fwd(x, modr, vecs, w_in):
    s = x.shape[0]
    tm = min(TM_PROJ, s)

    def body(x_ref, mod_ref, vec_ref, w_ref, h1_ref, xrnn_ref, u_ref, ga_ref, dga_ref, sa_ref, sb_ref):
        xv = x_ref[...]
        r = lax.rsqrt(jnp.mean(xv * xv, axis=-1, keepdims=True) + EPS)
        gain = vec_ref[V_G1:V_G1 + 1, :] * (1.0 + mod_ref[M_SC1:M_SC1 + 1, :])
        h = (xv * r * gain + mod_ref[M_SH1:M_SH1 + 1, :]).astype(BF16)
        h1_ref[...] = h
        xrnn_ref[...] = _dot(h, w_ref[:, 0:D])
        ga_ref[...], dga_ref[...] = _gelu_and_grad(_dot(h, w_ref[:, D:2 * D]))
        u_ref[...] = _dot(h, w_ref[:, 2 * D:3 * D])
        sa_ref[...] = _sigmoid(_dot(h, w_ref[:, 3 * D:4 * D]))
        sb_ref[...] = _sigmoid(_dot(h, w_ref[:, 4 * D:5 * D]))

    tok = pl.BlockSpec((tm, D), lambda i: (i, 0))
    sd = lambda dt: jax.ShapeDtypeStruct((s, D), dt)
    return pl.pallas_call(
        body, name="proj_fwd", grid=(s // tm,),
        in_specs=[tok, pl.BlockSpec((8, D), lambda i: (0, 0)), pl.BlockSpec((16, D), lambda i: (0, 0)),
                  _resident((D, D_IN))],
        out_specs=[tok] * 7,
        out_shape=[sd(BF16)] + [sd(F32)] * 6,
        compiler_params=_params(("parallel",)),
    )(x, modr, vecs, w_in)


def _mix_fwd(x_rnn, u_pool, ga, vecs, w_rg_a, w_rg_x, w_pool, dep):
    s = x_rnn.shape[0]
    tm = min(TM_MIX, s)
    nb = s // tm

    def body(xh_ref, x_ref, uh_ref, u_ref, ga_ref, vec_ref, wa_ref, wx_ref, wp_ref, dep_ref,
             xr_ref, hr_ref, za_ref, p_ref, pooled_ref, a_ref, mult_ref, ra_ref, ri_ref, carry_ref, scan_scr):
        i = pl.program_id(0)
        first = i == 0

        @pl.when(first)
        def _():
            carry_ref[...] = jnp.zeros_like(carry_ref)

        row = lax.broadcasted_iota(jnp.int32, (tm, GW), 0)
        is_t0 = jnp.logical_and(first, row == 0)
        head_t = (lax.broadcasted_iota(jnp.int32, (HALO_U, GW), 0) + 1).astype(F32)
        for g in range(N_GROUPS):
            cs = slice(g * GW, (g + 1) * GW)
            vec = vec_ref[:, cs]
            xh = jnp.where(first, 0.0, xh_ref[:, cs])
            taps = _conv_taps(jnp.concatenate([xh, x_ref[:, cs]], axis=0))
            xr = vec[V_CONV_B:V_CONV_B + 1]
            for j in range(4):
                xr = xr + vec[V_CONV_W + j:V_CONV_W + j + 1] * taps[j]
            xr_ref[:, cs] = xr
            ra, ri, _, a, mult = _rglru_gates(
                xr, wa_ref[g], wx_ref[g], vec[V_B_RG_A:V_B_RG_A + 1], vec[V_B_RG_X:V_B_RG_X + 1],
                vec[V_A_PARAM:V_A_PARAM + 1], is_t0)
            a_ref[:, cs] = a
            mult_ref[:, cs] = mult
            ra_ref[:, cs] = ra.astype(BF16)
            ri_ref[:, cs] = ri.astype(BF16)
            h, last = _scan_down(a, xr * ri * mult, carry_ref[0:1, cs], scan_scr)
            hr_ref[:, cs] = h
            carry_ref[0:1, cs] = last
            za_ref[:, cs] = (ga_ref[:, cs] * h).astype(BF16)
            uh = jnp.where(first, 0.0, uh_ref[:, cs])
            sm = jnp.concatenate([uh, u_ref[:, cs]], axis=0)
            k = 1
            while k < POOL_WINDOWS[g]:
                sm = sm + _shift_down(sm, k)
                k *= 2
            mean = _window_mean(sm[HALO_U:], POOL_WINDOWS[g], first, head_t)
            p = (mean - u_ref[:, cs]).astype(BF16)
            p_ref[:, cs] = p
            pb = _dot(p, wp_ref[g]) + vec[V_B_POOL:V_B_POOL + 1]
            pooled_ref[:, cs] = (pb * vec[V_POOL_SCALE:V_POOL_SCALE + 1]).astype(BF16)

    tok = pl.BlockSpec((tm, D), lambda i: (i, 0))
    halo = lambda rows: pl.BlockSpec((rows, D), lambda i: (jnp.maximum(i * (tm // rows) - 1, 0), 0))
    wspec = pl.BlockSpec((N_GROUPS, GW, GW), lambda i: (0, 0, 0))
    sd = lambda dt: jax.ShapeDtypeStruct((s, D), dt)
    return pl.pallas_call(
        body, name="mix_fwd", grid=(nb,),
        in_specs=[halo(HALO_X), tok, halo(HALO_U), tok, tok, pl.BlockSpec((16, D), lambda i: (0, 0)),
                  wspec, wspec, wspec, pl.BlockSpec(memory_space=pl.ANY)],
        out_specs=[tok] * 9,
        out_shape=[sd(F32), sd(F32), sd(BF16), sd(BF16), sd(BF16), sd(F32), sd(F32), sd(BF16), sd(BF16)],
        scratch_shapes=[pltpu.VMEM((8, D), F32), pltpu.VMEM((3, tm, LANES), F32)],
        compiler_params=_params(("arbitrary",)),
    )(x_rnn, x_rnn, u_pool, u_pool, ga, vecs, w_rg_a, w_rg_x, w_pool, dep)


def _branch_fwd(za, pooled, sa, sb, x, modr, vecs, w_a, w_b, w_out):
    s = x.shape[0]
    tm = min(TM_BRANCH, s)

    def body(za_ref, pooled_ref, sa_ref, sb_ref, x_ref, mod_ref, vec_ref, wa_ref, wb_ref, wo_ref,
             ba_ref, bb_ref, merged_ref, o_ref, x2_ref, h2_ref):
        ba = _dot(za_ref[...], wa_ref[...])
        bb = _dot(pooled_ref[...], wb_ref[...])
        ba_ref[...] = ba.astype(BF16)
        bb_ref[...] = bb.astype(BF16)
        merged = (sa_ref[...] * ba + sb_ref[...] * bb).astype(BF16)
        merged_ref[...] = merged
        o = _dot(merged, wo_ref[...])
        o_ref[...] = o.astype(BF16)
        x2 = x_ref[...] + mod_ref[M_GT1:M_GT1 + 1, :] * o
        x2_ref[...] = x2
        r = lax.rsqrt(jnp.mean(x2 * x2, axis=-1, keepdims=True) + EPS)
        gain = vec_ref[V_G2:V_G2 + 1, :] * (1.0 + mod_ref[M_SC2:M_SC2 + 1, :])
        h2_ref[...] = (x2 * r * gain + mod_ref[M_SH2:M_SH2 + 1, :]).astype(BF16)

    tok = pl.BlockSpec((tm, D), lambda i: (i, 0))
    wspec = pl.BlockSpec((D, D), lambda i: (0, 0))
    sd = lambda dt: jax.ShapeDtypeStruct((s, D), dt)
    return pl.pallas_call(
        body, name="branch_fwd", grid=(s // tm,),
        in_specs=[tok, tok, tok, tok,
                  tok, pl.BlockSpec((8, D), lambda i: (0, 0)), pl.BlockSpec((16, D), lambda i: (0, 0)),
                  wspec, wspec, wspec],
        out_specs=[tok] * 6,
        out_shape=[sd(BF16), sd(BF16), sd(BF16), sd(BF16), sd(F32), sd(BF16)],
        compiler_params=_params(("parallel",)),
    )(za, pooled, sa, sb, x, modr, vecs, w_a, w_b, w_out)


def _mlp_fwd(h2, x2, target, modr, vecs, w_up, w_down):
    s = x2.shape[0]
    tm = min(TM_MLP, s)

    def body(h2_ref, x2_ref, tgt_ref, mod_ref, vec_ref, wu_ref, wd_ref,
             ru_ref, dx3_ref, ddn_ref, small_ref):
        @pl.when(pl.program_id(0) == 0)
        def _():
            small_ref[...] = jnp.zeros_like(small_ref)

        h2 = h2_ref[...]
        dn = None
        for c in range(D_FF // D):
            cs = slice(c * D, (c + 1) * D)
            ru = jnp.maximum(_dot(h2, wu_ref[:, cs]), 0.0)
            ru_ref[:, cs] = ru.astype(BF16)
            part = _dot((ru * ru).astype(BF16), wd_ref[cs, :])
            dn = part if dn is None else dn + part
        gt2 = mod_ref[M_GT2:M_GT2 + 1, :]
        gf = vec_ref[V_GF:V_GF + 1, :]
        x3 = x2_ref[...] + gt2 * dn
        r3 = lax.rsqrt(jnp.mean(x3 * x3, axis=-1, keepdims=True) + EPS)
        n3 = x3 * r3
        err = n3 * gf - tgt_ref[...]
        dy = err * (1.0 / D)
        dn3 = dy * gf
        dx3 = r3 * (dn3 - n3 * jnp.mean(dn3 * n3, axis=-1, keepdims=True))
        dx3_ref[...] = dx3
        ddn_ref[...] = (dx3 * gt2).astype(BF16)
        small_ref[0:1, :] += jnp.sum(dy * n3, axis=0, keepdims=True)
        small_ref[1:2, :] += jnp.sum(dx3 * dn, axis=0, keepdims=True)
        small_ref[2:3, :] += (0.5 / D) * jnp.sum(err * err, axis=0, keepdims=True)

    tok = pl.BlockSpec((tm, D), lambda i: (i, 0))
    return pl.pallas_call(
        body, name="mlp_fwd", grid=(s // tm,),
        in_specs=[tok, tok, tok,
                  pl.BlockSpec((8, D), lambda i: (0, 0)), pl.BlockSpec((16, D), lambda i: (0, 0)),
                  _resident((D, D_FF)), _resident((D_FF, D))],
        out_specs=[pl.BlockSpec((tm, D_FF), lambda i: (i, 0)), tok, tok,
                   pl.BlockSpec((8, D), lambda i: (0, 0))],
        out_shape=[jax.ShapeDtypeStruct((s, D_FF), BF16), jax.ShapeDtypeStruct((s, D), F32),
                   jax.ShapeDtypeStruct((s, D), BF16), jax.ShapeDtypeStruct((8, D), F32)],
        compiler_params=_params(("arbitrary",)),
    )(h2, x2, target, modr, vecs, w_up, w_down)


def _mlp_bwd(d_dn, ru, x2, dx3, o, modr, vecs, w_up, w_down):
    s = x2.shape[0]
    tm = min(TM_MLP_BWD, s)

    def body(ddn_ref, ru_ref, x2_ref, dx3_ref, o_ref, mod_ref, vec_ref, wu_ref, wd_ref,
             dup_ref, dx2_ref, do_ref, small_ref):
        @pl.when(pl.program_id(0) == 0)
        def _():
            small_ref[...] = jnp.zeros_like(small_ref)

        ddn = ddn_ref[...]
        dh2 = None
        for c in range(D_FF // D):
            cs = slice(c * D, (c + 1) * D)
            dff = _dot_nt(ddn, wd_ref[cs, :])
            dup = (dff * (2.0 * ru_ref[:, cs].astype(F32))).astype(BF16)
            dup_ref[:, cs] = dup
            part = _dot_nt(dup, wu_ref[:, cs])
            dh2 = part if dh2 is None else dh2 + part
        x2 = x2_ref[...]
        r2 = lax.rsqrt(jnp.mean(x2 * x2, axis=-1, keepdims=True) + EPS)
        xn2 = x2 * r2
        gain = vec_ref[V_G2:V_G2 + 1, :] * (1.0 + mod_ref[M_SC2:M_SC2 + 1, :])
        dxn2 = dh2 * gain
        dx2 = dx3_ref[...] + r2 * (dxn2 - xn2 * jnp.mean(dxn2 * xn2, axis=-1, keepdims=True))
        dx2_ref[...] = dx2
        do_ref[...] = (dx2 * mod_ref[M_GT1:M_GT1 + 1, :]).astype(BF16)
        small_ref[0:1, :] += jnp.sum(dh2, axis=0, keepdims=True)
        small_ref[1:2, :] += jnp.sum(dh2 * xn2, axis=0, keepdims=True)
        small_ref[2:3, :] += jnp.sum(dx2 * o_ref[...].astype(F32), axis=0, keepdims=True)

    tok = pl.BlockSpec((tm, D), lambda i: (i, 0))
    wide = pl.BlockSpec((tm, D_FF), lambda i: (i, 0))
    return pl.pallas_call(
        body, name="mlp_bwd", grid=(s // tm,),
        in_specs=[tok, wide, tok, tok, tok,
                  pl.BlockSpec((8, D), lambda i: (0, 0)), pl.BlockSpec((16, D), lambda i: (0, 0)),
                  _resident((D, D_FF)), _resident((D_FF, D))],
        out_specs=[wide, tok, tok, pl.BlockSpec((8, D), lambda i: (0, 0))],
        out_shape=[jax.ShapeDtypeStruct((s, D_FF), BF16), jax.ShapeDtypeStruct((s, D), F32),
                   jax.ShapeDtypeStruct((s, D), BF16), jax.ShapeDtypeStruct((8, D), F32)],
        compiler_params=_params(("arbitrary",)),
    )(d_dn, ru, x2, dx3, o, modr, vecs, w_up, w_down)


def _branch_bwd(do, sa, sb, ba, bb, w_a, w_b, w_out, dep):
    s = do.shape[0]
    tm = min(TM_BRANCH, s)

    def body(do_ref, sa_ref, sb_ref, ba_ref, bb_ref, wa_ref, wb_ref, wo_ref, dep_ref,
             dba_ref, dbb_ref, dg_ref, dza_ref, dpooled_ref):
        dmerged = _dot_nt(do_ref[...], wo_ref[...])
        sa = sa_ref[...]
        sb = sb_ref[...]
        dba = (dmerged * sa).astype(BF16)
        dbb = (dmerged * sb).astype(BF16)
        dba_ref[...] = dba
        dbb_ref[...] = dbb
        dg_ref[:, :D] = (dmerged * ba_ref[...].astype(F32) * sa * (1.0 - sa)).astype(BF16)
        dg_ref[:, D:] = (dmerged * bb_ref[...].astype(F32) * sb * (1.0 - sb)).astype(BF16)
        dza_ref[...] = _dot_nt(dba, wa_ref[...])
        dpooled_ref[...] = _dot_nt(dbb, wb_ref[...])

    tok = pl.BlockSpec((tm, D), lambda i: (i, 0))
    wspec = pl.BlockSpec((D, D), lambda i: (0, 0))
    sd = lambda dt: jax.ShapeDtypeStruct((s, D), dt)
    return pl.pallas_call(
        body, name="branch_bwd", grid=(s // tm,),
        in_specs=[tok, tok, tok, tok, tok, wspec, wspec, wspec, pl.BlockSpec(memory_space=pl.ANY)],
        out_specs=[tok, tok, pl.BlockSpec((tm, 2 * D), lambda i: (i, 0)), tok, tok],
        out_shape=[sd(BF16), sd(BF16), jax.ShapeDtypeStruct((s, 2 * D), BF16), sd(F32), sd(F32)],
        compiler_params=_params(("parallel",)),
    )(do, sa, sb, ba, bb, w_a, w_b, w_out, dep)


def _mix_bwd(dza, dpooled, x_rnn, ga, dga, xr, hr, p, gates, dgates, vecs, w_rg_a, w_rg_x, w_pool, dep):
    s = xr.shape[0]
    tm = min(TM_MIX, s)
    nb = s // tm

    def body(dza_ref, dpooled_ref, xh_ref, x_ref, ga_ref, dga_ref, xr_ref, hh_ref, hr_ref, p_ref,
             a_ref, mult_ref, ra_ref, ri_ref, dg_ref, vec_ref, wa_ref, wx_ref, wp_ref, dep_ref,
             dproj_ref, dwa_ref, dwx_ref, dwp_ref, small_ref,
             scan_carry, dxr_carry, q_carry, scan_scr, dwa_acc, dwx_acc, dwp_acc):
        i = pl.program_id(0)
        bi = nb - 1 - i
        first_t = bi == 0

        @pl.when(i == 0)
        def _():
            scan_carry[...] = jnp.zeros_like(scan_carry)
            dxr_carry[...] = jnp.zeros_like(dxr_carry)
            q_carry[...] = jnp.zeros_like(q_carry)
            dwa_acc[...] = jnp.zeros_like(dwa_acc)
            dwx_acc[...] = jnp.zeros_like(dwx_acc)
            dwp_acc[...] = jnp.zeros_like(dwp_acc)
            small_ref[...] = jnp.zeros_like(small_ref)

        row = lax.broadcasted_iota(jnp.int32, (tm, GW), 0)
        is_t0 = jnp.logical_and(first_t, row == 0)
        head_t = (lax.broadcasted_iota(jnp.int32, (HALO_U, GW), 0) + 1).astype(F32)
        colsum = lambda v: jnp.sum(v, axis=0, keepdims=True)
        for g in range(N_GROUPS):
            cs = slice(g * GW, (g + 1) * GW)
            vec = vec_ref[:, cs]
            xr = xr_ref[:, cs]
            hr = hr_ref[:, cs]
            dza = dza_ref[:, cs]
            dproj_ref[:, D + g * GW:D + (g + 1) * GW] = (dza * hr * dga_ref[:, cs]).astype(BF16)
            dhr = dza * ga_ref[:, cs]
            a = a_ref[:, cs]
            mult = mult_ref[:, cs]
            ra = ra_ref[:, cs].astype(F32)
            ri = ri_ref[:, cs].astype(F32)
            sp = _softplus(vec[V_A_PARAM:V_A_PARAM + 1])
            m = jnp.where(row == tm - 1, 1.0, _shift_up(a, 1))
            gsum = _scan_up(m, dhr, scan_carry[0:1, cs], scan_scr)
            scan_carry[0:1, cs] = a[0:1, :] * gsum[0:1, :]
            hh = jnp.where(first_t, 0.0, hh_ref[:, cs])
            hprev = _shift_down(jnp.concatenate([hh, hr], axis=0), 1)[8:]
            da = gsum * hprev
            dmult = jnp.where(is_t0, 0.0, gsum * xr * ri)
            dlog_a = da * a - dmult * a * a / mult
            dri = gsum * xr * mult
            dxr = gsum * ri * mult
            small_ref[7:8, cs] += colsum((-C_RG) * ra * dlog_a)
            dpa = (((-C_RG) * sp) * dlog_a * ra * (1.0 - ra))
            dpx = dri * ri * (1.0 - ri)
            small_ref[5:6, cs] += colsum(dpa)
            small_ref[6:7, cs] += colsum(dpx)
            dpa = dpa.astype(BF16)
            dpx = dpx.astype(BF16)
            xrb = xr.astype(BF16)
            dwa_acc[g] += _dot_tn(xrb, dpa)
            dwx_acc[g] += _dot_tn(xrb, dpx)
            dxr = dxr + _dot_nt(dpa, wa_ref[g]) + _dot_nt(dpx, wx_ref[g])
            small_ref[4:5, cs] += colsum(dxr)
            xh = jnp.where(first_t, 0.0, xh_ref[:, cs])
            taps = _conv_taps(jnp.concatenate([xh, x_ref[:, cs]], axis=0))
            dxr_ext = jnp.concatenate([dxr, dxr_carry[:, cs]], axis=0)
            dx = vec[V_CONV_W + 3:V_CONV_W + 4] * dxr
            for j in range(4):
                small_ref[j:j + 1, cs] += colsum(dxr * taps[j])
                if j < 3:
                    dx = dx + vec[V_CONV_W + j:V_CONV_W + j + 1] * _shift_up(dxr_ext, 3 - j)[:tm]
            dxr_carry[:, cs] = dxr[0:8, :]
            dproj_ref[:, cs] = dx.astype(BF16)
            pg = p_ref[:, cs]
            dpooled = dpooled_ref[:, cs]
            pb = _dot(pg, wp_ref[g]) + vec[V_B_POOL:V_B_POOL + 1]
            small_ref[9:10, cs] += colsum(dpooled * pb)
            dpb = dpooled * vec[V_POOL_SCALE:V_POOL_SCALE + 1]
            small_ref[8:9, cs] += colsum(dpb)
            dpbb = dpb.astype(BF16)
            dwp_acc[g] += _dot_tn(pg, dpbb)
            dp = _dot_nt(dpbb, wp_ref[g])
            q = _window_mean(dp, POOL_WINDOWS[g], first_t, head_t)
            sm = jnp.concatenate([q, q_carry[:, cs]], axis=0)
            k = 1
            while k < POOL_WINDOWS[g]:
                sm = sm + _shift_up(sm, k)
                k *= 2
            q_carry[:, cs] = q[0:HALO_U, :]
            dproj_ref[:, 2 * D + g * GW:2 * D + (g + 1) * GW] = (sm[:tm] - dp).astype(BF16)
        dproj_ref[:, 3 * D:] = dg_ref[...]

        @pl.when(i == nb - 1)
        def _():
            dwa_ref[...] = dwa_acc[...].astype(BF16)
            dwx_ref[...] = dwx_acc[...].astype(BF16)
            dwp_ref[...] = dwp_acc[...].astype(BF16)

    rev = lambda i: nb - 1 - i
    tok = pl.BlockSpec((tm, D), lambda i: (rev(i), 0))
    halo8 = lambda k: pl.BlockSpec((8, D), lambda i: (jnp.maximum(rev(i) * (tm // 8) - 1, 0), k))
    wspec = pl.BlockSpec((N_GROUPS, GW, GW), lambda i: (0, 0, 0))
    wshape = jax.ShapeDtypeStruct((N_GROUPS, GW, GW), BF16)
    return pl.pallas_call(
        body, name="mix_bwd", grid=(nb,),
        in_specs=[tok, tok, halo8(0), tok, tok, tok, tok, halo8(0), tok, tok, tok, tok, tok, tok,
                  pl.BlockSpec((tm, 2 * D), lambda i: (rev(i), 0)),
                  pl.BlockSpec((16, D), lambda i: (0, 0)), wspec, wspec, wspec, pl.BlockSpec(memory_space=pl.ANY)],
        out_specs=[pl.BlockSpec((tm, D_IN), lambda i: (rev(i), 0)), wspec, wspec, wspec,
                   pl.BlockSpec((16, D), lambda i: (0, 0))],
        out_shape=[jax.ShapeDtypeStruct((s, D_IN), BF16), wshape, wshape, wshape,
                   jax.ShapeDtypeStruct((16, D), F32)],
        scratch_shapes=[pltpu.VMEM((8, D), F32), pltpu.VMEM((8, D), F32), pltpu.VMEM((HALO_U, D), F32),
                        pltpu.VMEM((3, tm, LANES), F32)] + [pltpu.VMEM((N_GROUPS, GW, GW), F32)] * 3,
        compiler_params=_params(("arbitrary",)),
    )(dza, dpooled, x_rnn, x_rnn, ga, dga, xr, hr, hr, p, *gates, dgates, vecs, w_rg_a, w_rg_x, w_pool, dep)


def _proj_bwd(dproj, x, dx2, modr, vecs, w_in, dep):
    s = x.shape[0]
    tm = min(TM_PROJ, s)

    def body(dp_ref, x_ref, dx2_ref, mod_ref, vec_ref, w_ref, dep_ref, gx_ref, small_ref):
        @pl.when(pl.program_id(0) == 0)
        def _():
            small_ref[...] = jnp.zeros_like(small_ref)

        dh1 = None
        for c in range(D_IN // D):
            cs = slice(c * D, (c + 1) * D)
            part = _dot_nt(dp_ref[:, cs], w_ref[:, cs])
            dh1 = part if dh1 is None else dh1 + part
        xv = x_ref[...]
        r1 = lax.rsqrt(jnp.mean(xv * xv, axis=-1, keepdims=True) + EPS)
        xn1 = xv * r1
        gain = vec_ref[V_G1:V_G1 + 1, :] * (1.0 + mod_ref[M_SC1:M_SC1 + 1, :])
        dxn1 = dh1 * gain
        gx_ref[...] = dx2_ref[...] + r1 * (dxn1 - xn1 * jnp.mean(dxn1 * xn1, axis=-1, keepdims=True))
        small_ref[0:1, :] += jnp.sum(dh1, axis=0, keepdims=True)
        small_ref[1:2, :] += jnp.sum(dh1 * xn1, axis=0, keepdims=True)

    tok = pl.BlockSpec((tm, D), lambda i: (i, 0))
    return pl.pallas_call(
        body, name="proj_bwd", grid=(s // tm,),
        in_specs=[pl.BlockSpec((tm, D_IN), lambda i: (i, 0)), tok, tok,
                  pl.BlockSpec((8, D), lambda i: (0, 0)), pl.BlockSpec((16, D), lambda i: (0, 0)),
                  _resident((D, D_IN)), pl.BlockSpec(memory_space=pl.ANY)],
        out_specs=[tok, pl.BlockSpec((8, D), lambda i: (0, 0))],
        out_shape=[jax.ShapeDtypeStruct((s, D), F32), jax.ShapeDtypeStruct((8, D), F32)],
        compiler_params=_params(("arbitrary",)),
    )(dproj, x, dx2, modr, vecs, w_in, dep)


def _wgrad(a, b, name, square_a=False, dep=None):
    s, ka = a.shape
    n = b.shape[1]
    tka = ka if ka <= 1024 else ka // 2
    tn = n if n <= 1024 else n // 2
    ts = min(TS_WGRAD, s)
    ns = s // ts
    nc = 512
    deps = [] if dep is None else [dep]

    def body(a_ref, b_ref, *refs):
        out_ref, acc_ref = refs[-2:]
        t = pl.program_id(2)

        @pl.when(t == 0)
        def _():
            acc_ref[...] = jnp.zeros_like(acc_ref)

        av = a_ref[...]
        if square_a:
            af = av.astype(F32)
            av = (af * af).astype(BF16)
        for c in range(tn // nc):
            cs = slice(c * nc, (c + 1) * nc)
            acc_ref[:, cs] += _dot_tn(av, b_ref[:, cs])

        @pl.when(t == ns - 1)
        def _():
            out_ref[...] = acc_ref[...].astype(BF16)

    return pl.pallas_call(
        body, name=name, grid=(ka // tka, n // tn, ns),
        in_specs=[pl.BlockSpec((ts, tka), lambda i, j, t: (t, i)),
                  pl.BlockSpec((ts, tn), lambda i, j, t: (t, j))] + [pl.BlockSpec(memory_space=pl.ANY)] * len(deps),
        out_specs=pl.BlockSpec((tka, tn), lambda i, j, t: (i, j)),
        out_shape=jax.ShapeDtypeStruct((ka, n), BF16),
        scratch_shapes=[pltpu.VMEM((tka, tn), F32)],
        compiler_params=_params(("parallel", "parallel", "arbitrary")),
    )(a, b, *deps)


def _window(ref, kind, idx, size):
    start = pl.multiple_of(idx * size, size)
    if kind == 0:
        return ref.at[pl.ds(start, size)]
    if kind == 1:
        return ref.at[:, pl.ds(start, size)]
    return ref.at[:, :, pl.ds(start, size)]


def _mesh_place():
    x, y, c = lax.axis_index("x"), lax.axis_index("y"), lax.axis_index("c")
    return x, y, c, 4 * x + 2 * y + c


def _peer(x, y, c, q):
    px = 1 - x if q & 4 else x
    py = 1 - y if q & 2 else y
    pc = 1 - c if q & 1 else c
    return (px, py, pc), 4 * px + 2 * py + pc


_HBM = pl.BlockSpec(memory_space=pltpu.HBM)
_SEM = pl.BlockSpec(memory_space=pltpu.SEMAPHORE)
_EFFECT = pltpu.SideEffectType.DATAFLOW_SIDE_EFFECTING


N_NEAR = 4


def _near(x, y, c):
    out = [((x, y, 1 - c), 4 * x + 2 * y + 1 - c)]
    for j in (1, 2, 3):
        px = 1 - x if j & 2 else x
        py = 1 - y if j & 1 else y
        out.append(((px, py, c), 4 * px + 2 * py + c))
    return out


def _remote(src, dst, send_sems, recv_sems, slot, device):
    return pltpu.make_async_remote_copy(src_ref=src, dst_ref=dst, send_sem=send_sems.at[slot], recv_sem=recv_sems.at[slot],
                                        device_id=device, device_id_type=MESH)


def _split_call(name, arrays, sems_in, n_new_sems, after, emit):
    na, ns, nn = len(arrays), len(sems_in), len(n_new_sems)

    def body(*refs):
        emit(refs[:na], refs[na:na + ns], refs[na + ns + 1:na + ns + 1 + nn])
        refs[-1][...] = jnp.zeros_like(refs[-1])

    outs = pl.pallas_call(
        body, name=name,
        out_shape=(*[pltpu.SemaphoreType.DMA((m,)) for m in n_new_sems],
                   *[pltpu.HBM(a.shape, a.dtype) for a in arrays], jax.ShapeDtypeStruct((8, 128), F32)),
        in_specs=[_HBM] * na + [_SEM] * ns + [pl.BlockSpec(memory_space=pl.ANY)],
        out_specs=(*[_SEM] * nn, *[_HBM] * na, pl.BlockSpec(memory_space=pltpu.VMEM)),
        input_output_aliases={i: nn + i for i in range(na)},
        compiler_params=pltpu.CompilerParams(has_side_effects=_EFFECT),
    )(*[pltpu.with_memory_space_constraint(a, pltpu.HBM) for a in arrays], *sems_in, after)
    return list(outs[:nn]), list(outs[nn:nn + na]), outs[-1]


def _together(name, steps, after):
    parts = [(ex.arrays, ex.sems, ex.new_sems[step], getattr(ex, "emit_" + step)) for ex, step in steps]

    def emit(arr, old, new):
        ia = io = ib = 0
        for arrays, sems, new_sems, emit_one in parts:
            emit_one(arr[ia:ia + len(arrays)], old[io:io + len(sems)], new[ib:ib + len(new_sems)])
            ia, io, ib = ia + len(arrays), io + len(sems), ib + len(new_sems)

    new, arrays, token = _split_call(name, [a for p in parts for a in p[0]], [s for p in parts for s in p[1]],
                                     [m for p in parts for m in p[2]], after, emit)
    out = []
    ia = ib = 0
    for (ex, _), (arrs, sems, new_sems, _) in zip(steps, parts):
        ex.arrays, ex.sems, ex.token = arrays[ia:ia + len(arrs)], [*sems, *new[ib:ib + len(new_sems)]], token
        ia, ib = ia + len(arrs), ib + len(new_sems)
        out.append(ex.arrays[ex.n:])
    return out


class _AllGather:
    def __init__(self, shards, kinds, name):
        self.n, self.kinds, self.name = len(shards), kinds, name
        self.sizes = [s.shape[k] for s, k in zip(shards, kinds)]
        lands = []
        for s, k in zip(shards, kinds):
            dims = list(s.shape)
            dims[k] *= N_DEV
            lands.append(lax.empty(tuple(dims), s.dtype))
        self.arrays, self.sems = [*shards, *lands], []

    def window(self, arr, k, idx):
        return _window(arr[self.n + k], self.kinds[k], idx, self.sizes[k])

    def start(self, after):
        _together(self.name + "_start", [(self, "start")], after)

    def forward(self, after):
        _together(self.name + "_forward", [(self, "forward")], after)

    def finish(self, after):
        return _together(self.name + "_finish", [(self, "finish")], after)[0]


class _Gather(_AllGather):
    def __init__(self, shards, kinds, name):
        super().__init__(shards, kinds, name)
        n = self.n
        self.new_sems = dict(start=[n * N_NEAR, n * N_NEAR, n], forward=[n * N_NEAR] * 2, finish=[])

    def emit_start(self, arr, _, new):
        x, y, c, me = _mesh_place()
        for k in range(self.n):
            pltpu.make_async_copy(arr[k], self.window(arr, k, me), new[2].at[k]).start()
        for k in range(self.n):
            for j, (dev, _) in enumerate(_near(x, y, c)):
                _remote(arr[k], self.window(arr, k, me), new[0], new[1], k * N_NEAR + j, dev).start()

    def emit_forward(self, arr, old, new):
        x, y, c, _ = _mesh_place()
        near = _near(x, y, c)
        for k in range(self.n):
            for j in (1, 2, 3):
                dev, idx = near[j]
                landed = self.window(arr, k, idx)
                _remote(arr[k], landed, old[0], old[1], k * N_NEAR + j, dev).wait_recv()
                _remote(landed, landed, new[0], new[1], k * N_NEAR + j, near[0][0]).start()

    def emit_finish(self, arr, old, _):
        x, y, c, me = _mesh_place()
        near = _near(x, y, c)
        other_core = near[0][0]
        for k in range(self.n):
            win = lambda idx: self.window(arr, k, idx)
            pltpu.make_async_copy(arr[k], win(me), old[2].at[k]).wait()
            for j, (dev, idx) in enumerate(near):
                _remote(arr[k], win(me), old[0], old[1], k * N_NEAR + j, dev).wait_send()
            _remote(arr[k], win(near[0][1]), old[0], old[1], k * N_NEAR, other_core).wait_recv()
            for j in (1, 2, 3):
                idx = near[j][1]
                _remote(win(idx), win(idx), old[3], old[4], k * N_NEAR + j, other_core).wait_send()
                _remote(arr[k], win(idx + 1 - 2 * c), old[3], old[4], k * N_NEAR + j, other_core).wait_recv()


class _Spread(_AllGather):
    def __init__(self, shards, kinds, name):
        super().__init__(shards, kinds, name)
        n = self.n
        self.new_sems = dict(start=[n * N_DEV, n * N_DEV, n], finish=[])

    def emit_start(self, arr, _, new):
        x, y, c, me = _mesh_place()
        for k in range(self.n):
            mine = self.window(arr, k, me)
            pltpu.make_async_copy(arr[k], mine, new[2].at[k]).start()
            for q in range(1, N_DEV):
                _remote(arr[k], mine, new[0], new[1], k * N_DEV + q, _peer(x, y, c, q)[0]).start()

    def emit_finish(self, arr, old, _):
        x, y, c, me = _mesh_place()
        for k in range(self.n):
            win = lambda idx: self.window(arr, k, idx)
            pltpu.make_async_copy(arr[k], win(me), old[2].at[k]).wait()
            for q in range(1, N_DEV):
                peer, peer_idx = _peer(x, y, c, q)
                _remote(arr[k], win(me), old[0], old[1], k * N_DEV + q, peer).wait_send()
                _remote(arr[k], win(peer_idx), old[0], old[1], k * N_DEV + q, peer).wait_recv()


class _Scatter:
    def __init__(self, partials, kinds, after, name):
        self.n, self.kinds, self.name, self.partials = len(partials), kinds, name, partials
        self.sizes = [p.shape[k] // N_DEV for p, k in zip(partials, kinds)]
        n, sizes = self.n, self.sizes
        self.slot_shapes = []
        for p, k, size in zip(partials, kinds, sizes):
            dims = list(p.shape)
            dims[k] = size
            self.slot_shapes.append((N_NEAR, *dims))
        slots = [lax.empty(sh, p.dtype) for sh, p in zip(self.slot_shapes, partials)]

        def emit(arr, _, new):
            x, y, c, _ = _mesh_place()
            near = _near(x, y, c)
            for k in range(n):
                for j in range(N_NEAR):
                    owner = near[j][1] if j == 0 else near[j][1] + 1 - 2 * c
                    _remote(_window(arr[k], kinds[k], owner, sizes[k]), arr[n + k].at[j], new[0], new[1],
                            k * N_NEAR + j, near[0][0]).start()

        self.sems, self.arrays, self.token = _split_call(name + "_start", [*partials, *slots], [], [n * N_NEAR] * 2,
                                                         after, emit)

    def combine_and_send(self, own4, after):
        n, kinds, sizes = self.n, self.kinds, self.sizes

        def emit_wait(arr, old, _):
            x, y, c, _ = _mesh_place()
            near = _near(x, y, c)
            for k in range(n):
                for j in range(N_NEAR):
                    owner = near[j][1] if j == 0 else near[j][1] + 1 - 2 * c
                    cp = _remote(_window(arr[k], kinds[k], owner, sizes[k]), arr[n + k].at[j], old[0], old[1],
                                 k * N_NEAR + j, near[0][0])
                    cp.wait_send()
                    cp.wait_recv()

        _, arrays, _ = _split_call(self.name + "_landed", self.arrays, self.sems, [], after, emit_wait)
        chip_sums = _chip_sums(arrays[:n], arrays[n:], kinds, sizes, own4, self.name + "_combine")
        arrivals = [lax.empty((N_NEAR - 1, *sh[1:]), p.dtype) for sh, p in zip(self.slot_shapes, self.partials)]

        def emit_send(arr, _, new):
            x, y, c, _ = _mesh_place()
            near = _near(x, y, c)
            for k in range(n):
                for j in (1, 2, 3):
                    _remote(arr[k].at[j], arr[n + k].at[j - 1], new[0], new[1], k * N_NEAR + j, near[j][0]).start()

        self.sems, self.arrays, self.token = _split_call(self.name + "_send", [*chip_sums, *arrivals], [],
                                                         [n * N_NEAR] * 2, own4, emit_send)

    def finish(self, after):
        n = self.n

        def emit(arr, old, _):
            x, y, c, _ = _mesh_place()
            near = _near(x, y, c)
            for k in range(n):
                for j in (1, 2, 3):
                    cp = _remote(arr[k].at[j], arr[n + k].at[j - 1], old[0], old[1], k * N_NEAR + j, near[j][0])
                    cp.wait_send()
                    cp.wait_recv()

        _, arrays, _ = _split_call(self.name + "_finish", self.arrays, self.sems, [], after, emit)
        return arrays[:n], arrays[n:]


def _chip_sums(partials, slots, kinds, sizes, own4, name):
    n = len(partials)

    def body(own_ref, *refs):
        for k in range(n):
            refs[2 * n + k][...] = (refs[k][...].astype(F32) + refs[n + k][...].astype(F32)).astype(BF16)

    in_specs, slot_specs = [], []
    for p, s, kind, size in zip(partials, slots, kinds, sizes):
        block = list(p.shape)
        block[kind] = size
        nd = len(block)
        in_specs.append(pl.BlockSpec(tuple(block), functools.partial(
            lambda j, own, kind, nd: tuple(own[j] if d == kind else 0 for d in range(nd)), kind=kind, nd=nd)))
        slot_specs.append(pl.BlockSpec((None, *block), functools.partial(
            lambda j, own, nd: (j,) + (0,) * nd, nd=nd)))
    return pl.pallas_call(
        body, name=name,
        grid_spec=pltpu.PrefetchScalarGridSpec(num_scalar_prefetch=1, grid=(N_NEAR,),
                                               in_specs=in_specs + slot_specs, out_specs=slot_specs),
        out_shape=[jax.ShapeDtypeStruct(s.shape, s.dtype) for s in slots],
        compiler_params=_params(("arbitrary",)),
    )(own4, *partials, *slots)


def _to_bf16(arrays, name, dep=None):
    n = len(arrays)
    deps = [] if dep is None else [dep]

    def body(*refs):
        for src, dst in zip(refs[:n], refs[n + len(deps):]):
            dst[...] = src[...].astype(BF16)

    vmem = pl.BlockSpec(memory_space=pltpu.VMEM)
    return pl.pallas_call(body, name=name, out_shape=[jax.ShapeDtypeStruct(a.shape, BF16) for a in arrays],
                          in_specs=[vmem] * n + [pl.BlockSpec(memory_space=pl.ANY)] * len(deps), out_specs=[vmem] * n,
                          compiler_params=pltpu.CompilerParams(vmem_limit_bytes=V7X_VMEM_LIMIT))(*arrays, *deps)


def _silu(c):
    return c * _sigmoid_tail(c)


def _ada_fwd(c_all, w_ada, b_ada_cols, dep):
    def body(c_ref, w_ref, b_ref, dep_ref, out_ref):
        out_ref[...] = jnp.dot(_silu(c_ref[...]), w_ref[...], preferred_element_type=F32,
                               precision=lax.Precision.HIGHEST) + b_ref[...]

    vmem = pl.BlockSpec(memory_space=pltpu.VMEM)
    return pl.pallas_call(
        body, name="ada_fwd", in_specs=[vmem, vmem, vmem, pl.BlockSpec(memory_space=pl.ANY)], out_specs=vmem,
        out_shape=jax.ShapeDtypeStruct((N_DEV, w_ada.shape[1]), F32),
    )(c_all, w_ada, b_ada_cols, dep)


def _adam(w, g, m, v):
    m = ADAM_B1 * m + (1.0 - ADAM_B1) * g
    v = ADAM_B2 * v + (1.0 - ADAM_B2) * (g * g)
    m_hat = m / (1.0 - ADAM_B1 ** ADAM_STEP)
    v_hat = v / (1.0 - ADAM_B2 ** ADAM_STEP)
    delta = -ADAM_LR * (m_hat / (jnp.sqrt(v_hat) + ADAM_EPS) + ADAM_WD * w)
    return delta, m, v


def _ada_bwd_adam(c_all, dmod_cols, w, m, v):
    def body(c_ref, d_ref, w_ref, m_ref, v_ref, g_ref, delta_ref, nm_ref, nv_ref):
        g = lax.dot_general(_silu(c_ref[...]), d_ref[...], (((0,), (0,)), ((), ())),
                            preferred_element_type=F32, precision=lax.Precision.HIGHEST)
        g_ref[...] = g
        delta_ref[...], nm_ref[...], nv_ref[...] = _adam(w_ref[...], g, m_ref[...], v_ref[...])

    sd = jax.ShapeDtypeStruct(w.shape, F32)
    return pl.pallas_call(body, name="ada_bwd_adam", out_shape=[sd] * 4,
                          compiler_params=pltpu.CompilerParams(vmem_limit_bytes=V7X_VMEM_LIMIT),
                          )(c_all, dmod_cols, w, m, v)


def _adam_group(chip_sums, arrivals, ws, ms, vs, n_tiles, name):
    n = len(ws)

    def body(*refs):
        for k in range(n):
            c_ref, a_ref, w_ref, m_ref, v_ref = (refs[j * n + k] for j in range(5))
            g_ref, delta_ref, nm_ref, nv_ref = (refs[(5 + j) * n + k] for j in range(4))
            g = c_ref[...].astype(F32)
            for j in range(N_NEAR - 1):
                g = g + a_ref[j].astype(F32)
            g_ref[...] = g
            delta_ref[...], nm_ref[...], nv_ref[...] = _adam(w_ref[...], g, m_ref[...], v_ref[...])

    tiles = [(w.shape[0] // n_tiles, w.shape[1]) for w in ws]
    blk = [pl.BlockSpec(t, lambda i: (i, 0)) for t in tiles]
    return pl.pallas_call(
        body, name=name, grid=(n_tiles,),
        in_specs=[pl.BlockSpec((None, *t), lambda i: (0, i, 0)) for t in tiles]
        + [pl.BlockSpec((N_NEAR - 1, *t), lambda i: (0, i, 0)) for t in tiles] + blk * 3,
        out_specs=blk * 4, out_shape=[jax.ShapeDtypeStruct(w.shape, F32) for w in ws] * 4,
        compiler_params=_params(("parallel",)),
    )(*chip_sums, *arrivals, *ws, *ms, *vs)


N_SMALL = 40
N_SMALL_PARAMS = 11


def _pack_vecs(conv_w_full, rows):
    def body(cw_ref, *refs):
        out = refs[-1]
        out[...] = jnp.zeros_like(out)
        out[0:4, :] = cw_ref[0:4, :]
        for r, ref in enumerate(refs[:-1]):
            out[4 + r:5 + r, :] = ref[...]

    return pl.pallas_call(body, name="pack_vecs", out_shape=jax.ShapeDtypeStruct((16, D), F32))(conv_w_full, *rows)


def _small_finish(gathered, conv_cols, mod_all, vecs, ws, ms, vs):
    n = N_SMALL_PARAMS

    def body(g_ref, conv_ref, mod_ref, vec_ref, *refs):
        w_refs, m_refs, v_refs = refs[:n], refs[n:2 * n], refs[2 * n:3 * n]
        outs = refs[3 * n:]
        g1 = vec_ref[V_G1:V_G1 + 1, :]
        g2 = vec_ref[V_G2:V_G2 + 1, :]
        zero = jnp.zeros((1, D), F32)
        dg1, dg2, dgf, loss_lanes = zero, zero, zero, zero
        mixer = jnp.zeros((16, D), F32)
        db_ada = jnp.zeros((6, D), F32)
        d_conv_w = jnp.zeros(conv_ref.shape[1:], F32)
        for b in range(N_DEV):
            gb = g_ref[b]
            mod = mod_ref[b]
            q1 = gb[33:34]
            q2 = gb[9:10]
            dmod = jnp.concatenate([gb[32:33], q1 * g1, gb[10:11], gb[8:9], q2 * g2, gb[1:2]], axis=0)
            outs[4 * n][b] = dmod
            db_ada = db_ada + dmod
            dg1 = dg1 + q1 * (1.0 + mod[M_SC1:M_SC1 + 1])
            dg2 = dg2 + q2 * (1.0 + mod[M_SC2:M_SC2 + 1])
            dgf = dgf + gb[0:1]
            loss_lanes = loss_lanes + gb[2:3]
            mixer = mixer + gb[16:32]
            d_conv_w = d_conv_w + conv_ref[b]
        d_a_param = mixer[7:8] * _sigmoid_tail(vec_ref[V_A_PARAM:V_A_PARAM + 1, :])
        grads = [dg1, dg2, mixer[4:5], mixer[5:6], mixer[6:7], d_a_param, mixer[8:9], mixer[9:10], dgf,
                 db_ada, d_conv_w]

        def load(ref, rows):
            if ref.shape[0] == rows:
                return ref[...]
            return jnp.concatenate([ref[:, j * D:(j + 1) * D] for j in range(rows)], axis=0)

        def store(ref, val):
            if ref.shape == val.shape:
                ref[...] = val
            else:
                for j in range(val.shape[0]):
                    ref[:, j * D:(j + 1) * D] = val[j:j + 1]

        for k in range(n):
            rows = grads[k].shape[0]
            results = (grads[k], *_adam(load(w_refs[k], rows), grads[k], load(m_refs[k], rows), load(v_refs[k], rows)))
            for which, val in enumerate(results):
                store(outs[which * n + k], val)
        outs[4 * n + 1][...] = jnp.broadcast_to(jnp.sum(loss_lanes, axis=1, keepdims=True), (8, 128))

    shapes = [jax.ShapeDtypeStruct(w.shape, F32) for w in ws]
    return pl.pallas_call(
        body, name="small_finish",
        out_shape=shapes * 4 + [jax.ShapeDtypeStruct((N_DEV, 6, D), F32), jax.ShapeDtypeStruct((8, 128), F32)],
    )(gathered, conv_cols, mod_all, vecs, *ws, *ms, *vs)


def _pad_rows(a, rows):
    return jnp.pad(a, ((0, rows - a.shape[0]), (0, 0)))


def kernel(x, c, norm_mix_g, norm_mlp_g, w_ada, b_ada, w_in, conv_w, conv_b, w_rg_a, b_rg_a, w_rg_x, b_rg_x, a_param, w_branch_a, w_pool, b_pool, pool_scale, w_branch_b, w_out, w_up, w_down, final_g, loss_target, m_norm_mix_g, m_norm_mlp_g, m_w_ada, m_b_ada, m_w_in, m_conv_w, m_conv_b, m_w_rg_a, m_b_rg_a, m_w_rg_x, m_b_rg_x, m_a_param, m_w_branch_a, m_w_pool, m_b_pool, m_pool_scale, m_w_branch_b, m_w_out, m_w_up, m_w_down, m_final_g, v_norm_mix_g, v_norm_mlp_g, v_w_ada, v_b_ada, v_w_in, v_conv_w, v_conv_b, v_w_rg_a, v_b_rg_a, v_w_rg_x, v_b_rg_x, v_a_param, v_w_branch_a, v_w_pool, v_b_pool, v_pool_scale, v_w_branch_b, v_w_out, v_w_up, v_w_down, v_final_g):
    me = 4 * lax.axis_index("x") + 2 * lax.axis_index("y") + lax.axis_index("c")
    s = x.shape[1]
    x2d = x.reshape(s, D)
    target = loss_target.reshape(s, D)
    n_ada = w_ada.shape[2]

    b_ada_cols = lax.dynamic_slice(b_ada, (0, me * n_ada), (1, n_ada))

    sharded = dict(w_in=(w_in[0], 1), w_up=(w_up[0], 1), w_down=(w_down[0], 0), w_branch_a=(w_branch_a[0], 0),
                   w_branch_b=(w_branch_b[0], 0), w_out=(w_out[0], 0), w_rg_a=(w_rg_a[0], 1), w_rg_x=(w_rg_x[0], 1),
                   w_pool=(w_pool[0], 1))
    kind = {k: v[1] for k, v in sharded.items()}
    first_names = ["w_in"]
    later_names = [k for k in sharded if k not in first_names]
    mix_names = ["w_rg_a", "w_rg_x", "w_pool"]
    branch_names = ["w_branch_a", "w_branch_b", "w_out"]
    mlp_names = ["w_up", "w_down"]

    def gather(group, after, name):
        exchange = _Gather([shard[k] for k in group], [kind[k] for k in group], name)
        exchange.start(after)
        return exchange

    shard = dict(zip(first_names, _to_bf16([sharded[k][0] for k in first_names], "to_bf16_first")))
    spread_c = _Spread([_pad_rows(c, 8), _pad_rows(conv_w[0], 8)], [0, 1], "spread_c")
    g_first = _Gather([shard[k] for k in first_names], [kind[k] for k in first_names], "gather_first")
    _together("first_start", [(spread_c, "start"), (g_first, "start")], c)
    shard.update(zip(later_names, _to_bf16([sharded[k][0] for k in later_names], "to_bf16_later", dep=g_first.token)))

    c_rows, conv_w_full = spread_c.finish(g_first.token)
    c_all = c_rows.reshape(N_DEV, 8, D)[:, 0, :]
    mod_part = _ada_fwd(c_all, w_ada[0], b_ada_cols, g_first.token)
    vecs = _pack_vecs(conv_w_full, [conv_b, b_rg_a, b_rg_x, a_param, b_pool, pool_scale,
                                    norm_mix_g, norm_mlp_g, final_g.reshape(1, D)])
    spread_mod = _Spread([mod_part], [0], "spread_mod")
    spread_mod.start(vecs)
    g_mix = gather(mix_names, spread_mod.token, "gather_mix")
    g_branch = gather(branch_names, g_mix.token, "gather_branch")
    g_mlp = gather(mlp_names, g_branch.token, "gather_mlp")
    g_first.forward(g_mlp.token)
    wg = dict(zip(first_names, g_first.finish(g_first.token)))
    mod_parts, = spread_mod.finish(g_first.token)
    mod_all = jnp.transpose(mod_parts.reshape(N_DEV, N_DEV, n_ada), (1, 0, 2)).reshape(N_DEV, 6, D)
    modr = _pad_rows(lax.dynamic_index_in_dim(mod_all, me, 0, keepdims=False), 8)

    h1, x_rnn, u_pool, ga, dga, sa, sb = _proj_fwd(x2d, modr, vecs, wg["w_in"])
    g_mix.forward(h1)
    wg.update(zip(mix_names, g_mix.finish(g_mix.token)))
    xr, hr, za, p, pooled, *gates = _mix_fwd(x_rnn, u_pool, ga, vecs, wg["w_rg_a"], wg["w_rg_x"], wg["w_pool"],
                                             dep=g_mix.token)
    _together("late_forward", [(g_branch, "forward"), (g_mlp, "forward")], za)
    wg.update(zip(branch_names, g_branch.finish(g_mlp.token)))
    ba, bb, merged, o, x2, h2 = _branch_fwd(za, pooled, sa, sb, x2d, modr, vecs,
                                            wg["w_branch_a"], wg["w_branch_b"], wg["w_out"])
    wg.update(zip(mlp_names, g_mlp.finish(h2)))
    ru, dx3, d_dn, small_f = _mlp_fwd(h2, x2, target, modr, vecs, wg["w_up"], wg["w_down"])

    near = _near(lax.axis_index("x"), lax.axis_index("y"), lax.axis_index("c"))
    own4 = jnp.stack([me, near[1][1], near[2][1], near[3][1]]).astype(jnp.int32)

    def scatter(group, partial, after, name):
        return _Scatter([partial[k] for k in group], [kind[k] for k in group], after, name)

    dup, dx2, do, small_m = _mlp_bwd(d_dn, ru, x2, dx3, o, modr, vecs, wg["w_up"], wg["w_down"])
    partial = dict(w_up=_wgrad(h2, dup, "wgrad_up"), w_down=_wgrad(ru, d_dn, "wgrad_down", square_a=True))
    s_mlp = scatter(mlp_names, partial, dx2, "scatter_mlp")

    dba, dbb, dgates, dza, dpooled = _branch_bwd(do, sa, sb, ba, bb, wg["w_branch_a"], wg["w_branch_b"], wg["w_out"],
                                                 dep=s_mlp.token)
    s_mlp.combine_and_send(own4, dza)
    dproj, dw_rg_a, dw_rg_x, dw_pool, small_x = _mix_bwd(dza, dpooled, x_rnn, ga, dga, xr, hr, p, gates, dgates,
                                                         vecs, wg["w_rg_a"], wg["w_rg_x"], wg["w_pool"],
                                                         dep=s_mlp.token)
    partial.update(w_branch_a=_wgrad(za, dba, "wgrad_branch_a"), w_branch_b=_wgrad(pooled, dbb, "wgrad_branch_b"),
                   w_out=_wgrad(merged, do, "wgrad_out"),
                   w_rg_a=dw_rg_a, w_rg_x=dw_rg_x, w_pool=dw_pool)
    mixer_names = ["w_rg_a", "w_rg_x", "w_pool", "w_branch_a", "w_branch_b", "w_out"]
    s_mixer = scatter(mixer_names, partial, s_mlp.token, "scatter_mixer")

    partial["w_in"] = _wgrad(h1, dproj, "wgrad_in", dep=s_mixer.token)
    s_in = scatter(["w_in"], partial, s_mixer.token, "scatter_in")
    s_mixer.combine_and_send(own4, s_in.token)
    s_in.combine_and_send(own4, s_mixer.token)
    grad_x, small_p = _proj_bwd(dproj, x2d, dx2, modr, vecs, wg["w_in"], dep=s_in.token)

    locals_ = dict(w_in=(w_in, m_w_in, v_w_in), w_up=(w_up, m_w_up, v_w_up), w_down=(w_down, m_w_down, v_w_down),
                   w_branch_a=(w_branch_a, m_w_branch_a, v_w_branch_a),
                   w_branch_b=(w_branch_b, m_w_branch_b, v_w_branch_b), w_out=(w_out, m_w_out, v_w_out),
                   w_rg_a=(w_rg_a, m_w_rg_a, v_w_rg_a), w_rg_x=(w_rg_x, m_w_rg_x, v_w_rg_x),
                   w_pool=(w_pool, m_w_pool, v_w_pool))
    res = {}

    def finish(group, exchange, after, n_tiles, name):
        chip_sums, arrivals = exchange.finish(after)
        flat = lambda t: t.reshape(-1, t.shape[-1])
        shapes = [flat(locals_[k][0]).shape for k in group]
        outs = _adam_group([cs.reshape(N_NEAR, *sh) for cs, sh in zip(chip_sums, shapes)],
                           [ar.reshape(N_NEAR - 1, *sh) for ar, sh in zip(arrivals, shapes)],
                           *[[flat(locals_[k][j]) for k in group] for j in range(3)], n_tiles, name)
        for i, k in enumerate(group):
            res[k] = [outs[j * len(group) + i].reshape(locals_[k][0].shape) for j in range(4)]
        return res[group[-1]][0]

    small = jnp.concatenate([small_f, small_m, small_x, small_p], axis=0)
    g_small = _Spread([small], [0], "spread_small")
    g_small.start(grad_x)
    done = finish(mlp_names, s_mlp, g_small.token, 4, "adam_mlp")
    done = finish(mixer_names, s_mixer, done, 2, "adam_mixer")
    done = finish(["w_in"], s_in, done, 4, "adam_in")
    small_all, = g_small.finish(done)
    small_all = small_all.reshape(N_DEV, N_SMALL, D)

    conv_cols = lax.dynamic_slice(small_all, (0, 16, me * (D // N_DEV)), (N_DEV, 4, D // N_DEV))

    def smalls(ng, nl, cb, bra, brx, ap, bp, ps, fg, ba_, cw):
        return [ng, nl, cb, bra, brx, ap, bp, ps, fg.reshape(1, D), ba_, cw[0]]

    small_names = ["norm_mix_g", "norm_mlp_g", "conv_b", "b_rg_a", "b_rg_x", "a_param", "b_pool", "pool_scale",
                   "final_g", "b_ada", "conv_w"]
    fin = _small_finish(
        small_all, conv_cols, mod_all, vecs,
        smalls(norm_mix_g, norm_mlp_g, conv_b, b_rg_a, b_rg_x, a_param, b_pool, pool_scale, final_g, b_ada, conv_w),
        smalls(m_norm_mix_g, m_norm_mlp_g, m_conv_b, m_b_rg_a, m_b_rg_x, m_a_param, m_b_pool, m_pool_scale,
               m_final_g, m_b_ada, m_conv_w),
        smalls(v_norm_mix_g, v_norm_mlp_g, v_conv_b, v_b_rg_a, v_b_rg_x, v_a_param, v_b_pool, v_pool_scale,
               v_final_g, v_b_ada, v_conv_w))
    dmod_all, loss_tile = fin[4 * N_SMALL_PARAMS], fin[4 * N_SMALL_PARAMS + 1]
    dmod_cols = lax.dynamic_slice(dmod_all.reshape(N_DEV, 6 * D), (0, me * n_ada), (N_DEV, n_ada))
    res["w_ada"] = [t.reshape(w_ada.shape) for t in _ada_bwd_adam(c_all, dmod_cols, w_ada[0], m_w_ada[0], v_w_ada[0])]

    def final_shape(k, t):
        if k == "final_g":
            return t.reshape(D)
        if k == "conv_w":
            return t.reshape(conv_w.shape)
        return t

    for i, k in enumerate(small_names):
        res[k] = [final_shape(k, fin[which * N_SMALL_PARAMS + i]) for which in range(4)]
    order = ["norm_mix_g", "norm_mlp_g", "w_ada", "b_ada", "w_in", "conv_w", "conv_b", "w_rg_a", "b_rg_a", "w_rg_x",
             "b_rg_x", "a_param", "w_branch_a", "w_pool", "b_pool", "pool_scale", "w_branch_b", "w_out", "w_up",
             "w_down", "final_g"]
    outs = [loss_tile[0, 0], grad_x.reshape(x.shape)]
    for which in range(4):
        for k in order:
            outs.append(res[k][which])
    return tuple(outs)
```

```python
import functools

import jax
import jax.numpy as jnp
from jax import lax
from jax.experimental import pallas as pl
from jax.experimental.pallas import tpu as pltpu

F32 = jnp.float32
BF16 = jnp.bfloat16
MESH = pl.DeviceIdType.MESH

N_DEV = 8
D = 1024
N_GROUPS = 4
GW = D // N_GROUPS
D_IN = 5 * D
D_FF = 4 * D
POOL_WINDOWS = (2, 4, 8, 16)
HALO_X = 8
HALO_U = 16
EPS = 1e-6
C_RG = 8.0
ADAM_LR, ADAM_B1, ADAM_B2, ADAM_EPS, ADAM_WD, ADAM_STEP = 0.001, 0.9, 0.999, 1e-08, 0.01, 10

V7X_VMEM_LIMIT = 56 * 1024 * 1024

V_CONV_W, V_CONV_B, V_B_RG_A, V_B_RG_X, V_A_PARAM, V_B_POOL, V_POOL_SCALE, V_G1, V_G2, V_GF = 0, 4, 5, 6, 7, 8, 9, 10, 11, 12
M_SH1, M_SC1, M_GT1, M_SH2, M_SC2, M_GT2 = 0, 1, 2, 3, 4, 5

TM_PROJ = 512
TM_MIX = 256
TM_BRANCH = 512
TM_MLP = 512
TM_MLP_BWD = 256
TS_WGRAD = 1024


def _params(semantics):
    return pltpu.CompilerParams(dimension_semantics=semantics, vmem_limit_bytes=V7X_VMEM_LIMIT)


def _resident(shape):
    return pl.BlockSpec(shape, lambda *_: (0,) * len(shape), pipeline_mode=pl.Buffered(1))


def _dot(a, b):
    return jnp.dot(a, b, preferred_element_type=F32)


def _dot_nt(a, b):
    return lax.dot_general(a, b, (((1,), (1,)), ((), ())), preferred_element_type=F32)


def _dot_tn(a, b):
    return lax.dot_general(a, b, (((0,), (0,)), ((), ())), preferred_element_type=F32)


def _sigmoid(x):
    return 0.5 * jnp.tanh(0.5 * x) + 0.5


def _sigmoid_tail(x):
    return 1.0 / (1.0 + jnp.exp(-x))


def _gelu_and_grad(x):
    k = 0.7978845608028654
    x2 = x * x
    t = jnp.tanh(k * (x + 0.044715 * x * x2))
    g = 0.5 * x * (1.0 + t)
    dg = 0.5 * (1.0 + t) + 0.5 * x * (1.0 - t * t) * (k * (1.0 + 3.0 * 0.044715 * x2))
    return g, dg


def _softplus(a):
    e = jnp.exp(-jnp.abs(a))
    u = 1.0 + e
    log1p_e = jnp.where(u == 1.0, e, jnp.log(u) * e / jnp.where(u == 1.0, 1.0, u - 1.0))
    return jnp.maximum(a, 0.0) + log1p_e


def _neg_expm1(z):
    series = -(z * (1.0 + z * (0.5 + z * (1.0 / 6.0 + z * (1.0 / 24.0 + z * (1.0 / 120.0))))))
    return jnp.where(z > -0.1, series, 1.0 - jnp.exp(z))


def _shift_down(x, k):
    return pltpu.roll(x, k, 0)


def _shift_up(x, k):
    return pltpu.roll(x, x.shape[0] - k, 0)


def _rglru_gates(xr, w_a, w_x, b_a, b_x, a_param, is_t0):
    xb = xr.astype(BF16)
    ra = _sigmoid(_dot(xb, w_a) + b_a)
    ri = _sigmoid(_dot(xb, w_x) + b_x)
    sp = _softplus(a_param)
    log_a = (-C_RG) * ra * sp
    a = jnp.exp(log_a)
    mult = jnp.where(is_t0, 1.0, jnp.sqrt(_neg_expm1(2.0 * log_a)))
    return ra, ri, sp, a, mult


SUBLANES = 8


LANES = 128


def _scan_strip(a, b, carry, scr, down):
    t = b.shape[0]
    g = t // SUBLANES
    a3 = a.reshape(g, SUBLANES, LANES)
    b3 = b.reshape(g, SUBLANES, LANES)
    sub = lax.broadcasted_iota(jnp.int32, (g, SUBLANES, LANES), 1)
    for k in (1, 2, 4):
        keep = sub >= k if down else sub < SUBLANES - k
        shift = k if down else SUBLANES - k
        b3 = b3 + a3 * jnp.where(keep, pltpu.roll(b3, shift, 1), 0.0)
        a3 = a3 * jnp.where(keep, pltpu.roll(a3, shift, 1), 1.0)
    scr[0] = a3.reshape(t, LANES)
    scr[1] = b3.reshape(t, LANES)
    end_row = SUBLANES - 1 if down else 0
    ag = scr[0, pl.ds(end_row, g, stride=SUBLANES), :]
    bg = scr[1, pl.ds(end_row, g, stride=SUBLANES), :]
    rg = lax.broadcasted_iota(jnp.int32, (g, LANES), 0)
    edge = 0 if down else g - 1
    bg = bg + jnp.where(rg == edge, ag * carry, 0.0)
    k = 1
    while k < g:
        keep = rg >= k if down else rg < g - k
        shift = k if down else g - k
        bg = bg + ag * jnp.where(keep, pltpu.roll(bg, shift, 0), 0.0)
        if 2 * k < g:
            ag = ag * pltpu.roll(ag, shift, 0)
        k *= 2
    entering = jnp.where(rg != edge, pltpu.roll(bg, 1 if down else g - 1, 0), carry)
    for r in range(SUBLANES):
        scr[2, pl.ds(r, g, stride=SUBLANES), :] = entering
    return scr[1] + scr[0] * scr[2], bg[g - 1:g, :]


def _scan_strips(a, b, carry, scr, down):
    outs = [_scan_strip(a[:, c:c + LANES], b[:, c:c + LANES], carry[:, c:c + LANES], scr, down)
            for c in range(0, b.shape[1], LANES)]
    return jnp.concatenate([o[0] for o in outs], axis=1), jnp.concatenate([o[1] for o in outs], axis=1)


def _scan_down(a, b, carry, scr):
    return _scan_strips(a, b, carry, scr, True)


def _scan_up(m, b, carry, scr):
    return _scan_strips(m, b, carry, scr, False)[0]


def _window_mean(sums, window, first_block, head_t):
    scaled = sums * (1.0 / window)
    head = jnp.where(first_block, sums[:HALO_U] / jnp.minimum(head_t, float(window)), scaled[:HALO_U])
    return jnp.concatenate([head, scaled[HALO_U:]], axis=0)


def _conv_taps(x_ext):
    return [_shift_down(x_ext, 3 - j)[HALO_X:] if j < 3 else x_ext[HALO_X:] for j in range(4)]


def _proj_fwd(x, modr, vecs, w_in):
    s = x.shape[0]
    tm = min(TM_PROJ, s)

    def body(x_ref, mod_ref, vec_ref, w_ref, h1_ref, xrnn_ref, u_ref, ga_ref, dga_ref, sa_ref, sb_ref):
        xv = x_ref[...]
        r = lax.rsqrt(jnp.mean(xv * xv, axis=-1, keepdims=True) + EPS)
        gain = vec_ref[V_G1:V_G1 + 1, :] * (1.0 + mod_ref[M_SC1:M_SC1 + 1, :])
        h = (xv * r * gain + mod_ref[M_SH1:M_SH1 + 1, :]).astype(BF16)
        h1_ref[...] = h
        xrnn_ref[...] = _dot(h, w_ref[:, 0:D])
        ga_ref[...], dga_ref[...] = _gelu_and_grad(_dot(h, w_ref[:, D:2 * D]))
        u_ref[...] = _dot(h, w_ref[:, 2 * D:3 * D])
        sa_ref[...] = _sigmoid(_dot(h, w_ref[:, 3 * D:4 * D]))
        sb_ref[...] = _sigmoid(_dot(h, w_ref[:, 4 * D:5 * D]))

    tok = pl.BlockSpec((tm, D), lambda i: (i, 0))
    sd = lambda dt: jax.ShapeDtypeStruct((s, D), dt)
    return pl.pallas_call(
        body, name="proj_fwd", grid=(s // tm,),
        in_specs=[tok, pl.BlockSpec((8, D), lambda i: (0, 0)), pl.BlockSpec((16, D), lambda i: (0, 0)),
                  _resident((D, D_IN))],
        out_specs=[tok] * 7,
        out_shape=[sd(BF16)] + [sd(F32)] * 6,
        compiler_params=_params(("parallel",)),
    )(x, modr, vecs, w_in)


def _mix_fwd(x_rnn, u_pool, ga, vecs, w_rg_a, w_rg_x, w_pool, dep):
    s = x_rnn.shape[0]
    tm = min(TM_MIX, s)
    nb = s // tm

    def body(xh_ref, x_ref, uh_ref, u_ref, ga_ref, vec_ref, wa_ref, wx_ref, wp_ref, dep_ref,
             xr_ref, hr_ref, za_ref, p_ref, pooled_ref, a_ref, mult_ref, ra_ref, ri_ref, carry_ref, scan_scr):
        i = pl.program_id(0)
        first = i == 0

        @pl.when(first)
        def _():
            carry_ref[...] = jnp.zeros_like(carry_ref)

        row = lax.broadcasted_iota(jnp.int32, (tm, GW), 0)
        is_t0 = jnp.logical_and(first, row == 0)
        head_t = (lax.broadcasted_iota(jnp.int32, (HALO_U, GW), 0) + 1).astype(F32)
        for g in range(N_GROUPS):
            cs = slice(g * GW, (g + 1) * GW)
            vec = vec_ref[:, cs]
            xh = jnp.where(first, 0.0, xh_ref[:, cs])
            taps = _conv_taps(jnp.concatenate([xh, x_ref[:, cs]], axis=0))
            xr = vec[V_CONV_B:V_CONV_B + 1]
            for j in range(4):
                xr = xr + vec[V_CONV_W + j:V_CONV_W + j + 1] * taps[j]
            xr_ref[:, cs] = xr
            ra, ri, _, a, mult = _rglru_gates(
                xr, wa_ref[g], wx_ref[g], vec[V_B_RG_A:V_B_RG_A + 1], vec[V_B_RG_X:V_B_RG_X + 1],
                vec[V_A_PARAM:V_A_PARAM + 1], is_t0)
            a_ref[:, cs] = a
            mult_ref[:, cs] = mult
            ra_ref[:, cs] = ra.astype(BF16)
            ri_ref[:, cs] = ri.astype(BF16)
            h, last = _scan_down(a, xr * ri * mult, carry_ref[0:1, cs], scan_scr)
            hr_ref[:, cs] = h
            carry_ref[0:1, cs] = last
            za_ref[:, cs] = (ga_ref[:, cs] * h).astype(BF16)
            uh = jnp.where(first, 0.0, uh_ref[:, cs])
            sm = jnp.concatenate([uh, u_ref[:, cs]], axis=0)
            k = 1
            while k < POOL_WINDOWS[g]:
                sm = sm + _shift_down(sm, k)
                k *= 2
            mean = _window_mean(sm[HALO_U:], POOL_WINDOWS[g], first, head_t)
            p = (mean - u_ref[:, cs]).astype(BF16)
            p_ref[:, cs] = p
            pb = _dot(p, wp_ref[g]) + vec[V_B_POOL:V_B_POOL + 1]
            pooled_ref[:, cs] = (pb * vec[V_POOL_SCALE:V_POOL_SCALE + 1]).astype(BF16)

    tok = pl.BlockSpec((tm, D), lambda i: (i, 0))
    halo = lambda rows: pl.BlockSpec((rows, D), lambda i: (jnp.maximum(i * (tm // rows) - 1, 0), 0))
    wspec = pl.BlockSpec((N_GROUPS, GW, GW), lambda i: (0, 0, 0))
    sd = lambda dt: jax.ShapeDtypeStruct((s, D), dt)
    return pl.pallas_call(
        body, name="mix_fwd", grid=(nb,),
        in_specs=[halo(HALO_X), tok, halo(HALO_U), tok, tok, pl.BlockSpec((16, D), lambda i: (0, 0)),
                  wspec, wspec, wspec, pl.BlockSpec(memory_space=pl.ANY)],
        out_specs=[tok] * 9,
        out_shape=[sd(F32), sd(F32), sd(BF16), sd(BF16), sd(BF16), sd(F32), sd(F32), sd(BF16), sd(BF16)],
        scratch_shapes=[pltpu.VMEM((8, D), F32), pltpu.VMEM((3, tm, LANES), F32)],
        compiler_params=_params(("arbitrary",)),
    )(x_rnn, x_rnn, u_pool, u_pool, ga, vecs, w_rg_a, w_rg_x, w_pool, dep)


def _branch_fwd(za, pooled, sa, sb, x, modr, vecs, w_a, w_b, w_out):
    s = x.shape[0]
    tm = min(TM_BRANCH, s)

    def body(za_ref, pooled_ref, sa_ref, sb_ref, x_ref, mod_ref, vec_ref, wa_ref, wb_ref, wo_ref,
             ba_ref, bb_ref, merged_ref, o_ref, x2_ref, h2_ref):
        ba = _dot(za_ref[...], wa_ref[...])
        bb = _dot(pooled_ref[...], wb_ref[...])
        ba_ref[...] = ba.astype(BF16)
        bb_ref[...] = bb.astype(BF16)
        merged = (sa_ref[...] * ba + sb_ref[...] * bb).astype(BF16)
        merged_ref[...] = merged
        o = _dot(merged, wo_ref[...])
        o_ref[...] = o.astype(BF16)
        x2 = x_ref[...] + mod_ref[M_GT1:M_GT1 + 1, :] * o
        x2_ref[...] = x2
        r = lax.rsqrt(jnp.mean(x2 * x2, axis=-1, keepdims=True) + EPS)
        gain = vec_ref[V_G2:V_G2 + 1, :] * (1.0 + mod_ref[M_SC2:M_SC2 + 1, :])
        h2_ref[...] = (x2 * r * gain + mod_ref[M_SH2:M_SH2 + 1, :]).astype(BF16)

    tok = pl.BlockSpec((tm, D), lambda i: (i, 0))
    wspec = pl.BlockSpec((D, D), lambda i: (0, 0))
    sd = lambda dt: jax.ShapeDtypeStruct((s, D), dt)
    return pl.pallas_call(
        body, name="branch_fwd", grid=(s // tm,),
        in_specs=[tok, tok, tok, tok,
                  tok, pl.BlockSpec((8, D), lambda i: (0, 0)), pl.BlockSpec((16, D), lambda i: (0, 0)),
                  wspec, wspec, wspec],
        out_specs=[tok] * 6,
        out_shape=[sd(BF16), sd(BF16), sd(BF16), sd(BF16), sd(F32), sd(BF16)],
        compiler_params=_params(("parallel",)),
    )(za, pooled, sa, sb, x, modr, vecs, w_a, w_b, w_out)


def _mlp_fwd(h2, x2, target, modr, vecs, w_up, w_down):
    s = x2.shape[0]
    tm = min(TM_MLP, s)

    def body(h2_ref, x2_ref, tgt_ref, mod_ref, vec_ref, wu_ref, wd_ref,
             ru_ref, dx3_ref, ddn_ref, small_ref):
        @pl.when(pl.program_id(0) == 0)
        def _():
            small_ref[...] = jnp.zeros_like(small_ref)

        h2 = h2_ref[...]
        dn = None
        for c in range(D_FF // D):
            cs = slice(c * D, (c + 1) * D)
            ru = jnp.maximum(_dot(h2, wu_ref[:, cs]), 0.0)
            ru_ref[:, cs] = ru.astype(BF16)
            part = _dot((ru * ru).astype(BF16), wd_ref[cs, :])
            dn = part if dn is None else dn + part
        gt2 = mod_ref[M_GT2:M_GT2 + 1, :]
        gf = vec_ref[V_GF:V_GF + 1, :]
        x3 = x2_ref[...] + gt2 * dn
        r3 = lax.rsqrt(jnp.mean(x3 * x3, axis=-1, keepdims=True) + EPS)
        n3 = x3 * r3
        err = n3 * gf - tgt_ref[...]
        dy = err * (1.0 / D)
        dn3 = dy * gf
        dx3 = r3 * (dn3 - n3 * jnp.mean(dn3 * n3, axis=-1, keepdims=True))
        dx3_ref[...] = dx3
        ddn_ref[...] = (dx3 * gt2).astype(BF16)
        small_ref[0:1, :] += jnp.sum(dy * n3, axis=0, keepdims=True)
        small_ref[1:2, :] += jnp.sum(dx3 * dn, axis=0, keepdims=True)
        small_ref[2:3, :] += (0.5 / D) * jnp.sum(err * err, axis=0, keepdims=True)

    tok = pl.BlockSpec((tm, D), lambda i: (i, 0))
    return pl.pallas_call(
        body, name="mlp_fwd", grid=(s // tm,),
        in_specs=[tok, tok, tok,
                  pl.BlockSpec((8, D), lambda i: (0, 0)), pl.BlockSpec((16, D), lambda i: (0, 0)),
                  _resident((D, D_FF)), _resident((D_FF, D))],
        out_specs=[pl.BlockSpec((tm, D_FF), lambda i: (i, 0)), tok, tok,
                   pl.BlockSpec((8, D), lambda i: (0, 0))],
        out_shape=[jax.ShapeDtypeStruct((s, D_FF), BF16), jax.ShapeDtypeStruct((s, D), F32),
                   jax.ShapeDtypeStruct((s, D), BF16), jax.ShapeDtypeStruct((8, D), F32)],
        compiler_params=_params(("arbitrary",)),
    )(h2, x2, target, modr, vecs, w_up, w_down)


def _mlp_bwd(d_dn, ru, x2, dx3, o, modr, vecs, w_up, w_down):
    s = x2.shape[0]
    tm = min(TM_MLP_BWD, s)

    def body(ddn_ref, ru_ref, x2_ref, dx3_ref, o_ref, mod_ref, vec_ref, wu_ref, wd_ref,
             dup_ref, dx2_ref, do_ref, small_ref):
        @pl.when(pl.program_id(0) == 0)
        def _():
            small_ref[...] = jnp.zeros_like(small_ref)

        ddn = ddn_ref[...]
        dh2 = None
        for c in range(D_FF // D):
            cs = slice(c * D, (c + 1) * D)
            dff = _dot_nt(ddn, wd_ref[cs, :])
            dup = (dff * (2.0 * ru_ref[:, cs].astype(F32))).astype(BF16)
            dup_ref[:, cs] = dup
            part = _dot_nt(dup, wu_ref[:, cs])
            dh2 = part if dh2 is None else dh2 + part
        x2 = x2_ref[...]
        r2 = lax.rsqrt(jnp.mean(x2 * x2, axis=-1, keepdims=True) + EPS)
        xn2 = x2 * r2
        gain = vec_ref[V_G2:V_G2 + 1, :] * (1.0 + mod_ref[M_SC2:M_SC2 + 1, :])
        dxn2 = dh2 * gain
        dx2 = dx3_ref[...] + r2 * (dxn2 - xn2 * jnp.mean(dxn2 * xn2, axis=-1, keepdims=True))
        dx2_ref[...] = dx2
        do_ref[...] = (dx2 * mod_ref[M_GT1:M_GT1 + 1, :]).astype(BF16)
        small_ref[0:1, :] += jnp.sum(dh2, axis=0, keepdims=True)
        small_ref[1:2, :] += jnp.sum(dh2 * xn2, axis=0, keepdims=True)
        small_ref[2:3, :] += jnp.sum(dx2 * o_ref[...].astype(F32), axis=0, keepdims=True)

    tok = pl.BlockSpec((tm, D), lambda i: (i, 0))
    wide = pl.BlockSpec((tm, D_FF), lambda i: (i, 0))
    return pl.pallas_call(
        body, name="mlp_bwd", grid=(s // tm,),
        in_specs=[tok, wide, tok, tok, tok,
                  pl.BlockSpec((8, D), lambda i: (0, 0)), pl.BlockSpec((16, D), lambda i: (0, 0)),
                  _resident((D, D_FF)), _resident((D_FF, D))],
        out_specs=[wide, tok, tok, pl.BlockSpec((8, D), lambda i: (0, 0))],
        out_shape=[jax.ShapeDtypeStruct((s, D_FF), BF16), jax.ShapeDtypeStruct((s, D), F32),
                   jax.ShapeDtypeStruct((s, D), BF16), jax.ShapeDtypeStruct((8, D), F32)],
        compiler_params=_params(("arbitrary",)),
    )(d_dn, ru, x2, dx3, o, modr, vecs, w_up, w_down)


def _branch_bwd(do, sa, sb, ba, bb, w_a, w_b, w_out, dep):
    s = do.shape[0]
    tm = min(TM_BRANCH, s)

    def body(do_ref, sa_ref, sb_ref, ba_ref, bb_ref, wa_ref, wb_ref, wo_ref, dep_ref,
             dba_ref, dbb_ref, dg_ref, dza_ref, dpooled_ref):
        dmerged = _dot_nt(do_ref[...], wo_ref[...])
        sa = sa_ref[...]
        sb = sb_ref[...]
        dba = (dmerged * sa).astype(BF16)
        dbb = (dmerged * sb).astype(BF16)
        dba_ref[...] = dba
        dbb_ref[...] = dbb
        dg_ref[:, :D] = (dmerged * ba_ref[...].astype(F32) * sa * (1.0 - sa)).astype(BF16)
        dg_ref[:, D:] = (dmerged * bb_ref[...].astype(F32) * sb * (1.0 - sb)).astype(BF16)
        dza_ref[...] = _dot_nt(dba, wa_ref[...])
        dpooled_ref[...] = _dot_nt(dbb, wb_ref[...])

    tok = pl.BlockSpec((tm, D), lambda i: (i, 0))
    wspec = pl.BlockSpec((D, D), lambda i: (0, 0))
    sd = lambda dt: jax.ShapeDtypeStruct((s, D), dt)
    return pl.pallas_call(
        body, name="branch_bwd", grid=(s // tm,),
        in_specs=[tok, tok, tok, tok, tok, wspec, wspec, wspec, pl.BlockSpec(memory_space=pl.ANY)],
        out_specs=[tok, tok, pl.BlockSpec((tm, 2 * D), lambda i: (i, 0)), tok, tok],
        out_shape=[sd(BF16), sd(BF16), jax.ShapeDtypeStruct((s, 2 * D), BF16), sd(F32), sd(F32)],
        compiler_params=_params(("parallel",)),
    )(do, sa, sb, ba, bb, w_a, w_b, w_out, dep)


def _mix_bwd(dza, dpooled, x_rnn, ga, dga, xr, hr, p, gates, dgates, vecs, w_rg_a, w_rg_x, w_pool, dep):
    s = xr.shape[0]
    tm = min(TM_MIX, s)
    nb = s // tm

    def body(dza_ref, dpooled_ref, xh_ref, x_ref, ga_ref, dga_ref, xr_ref, hh_ref, hr_ref, p_ref,
             a_ref, mult_ref, ra_ref, ri_ref, dg_ref, vec_ref, wa_ref, wx_ref, wp_ref, dep_ref,
             dproj_ref, dwa_ref, dwx_ref, dwp_ref, small_ref,
             scan_carry, dxr_carry, q_carry, scan_scr, dwa_acc, dwx_acc, dwp_acc):
        i = pl.program_id(0)
        bi = nb - 1 - i
        first_t = bi == 0

        @pl.when(i == 0)
        def _():
            scan_carry[...] = jnp.zeros_like(scan_carry)
            dxr_carry[...] = jnp.zeros_like(dxr_carry)
            q_carry[...] = jnp.zeros_like(q_carry)
            dwa_acc[...] = jnp.zeros_like(dwa_acc)
            dwx_acc[...] = jnp.zeros_like(dwx_acc)
            dwp_acc[...] = jnp.zeros_like(dwp_acc)
            small_ref[...] = jnp.zeros_like(small_ref)

        row = lax.broadcasted_iota(jnp.int32, (tm, GW), 0)
        is_t0 = jnp.logical_and(first_t, row == 0)
        head_t = (lax.broadcasted_iota(jnp.int32, (HALO_U, GW), 0) + 1).astype(F32)
        colsum = lambda v: jnp.sum(v, axis=0, keepdims=True)
        for g in range(N_GROUPS):
            cs = slice(g * GW, (g + 1) * GW)
            vec = vec_ref[:, cs]
            xr = xr_ref[:, cs]
            hr = hr_ref[:, cs]
            dza = dza_ref[:, cs]
            dproj_ref[:, D + g * GW:D + (g + 1) * GW] = (dza * hr * dga_ref[:, cs]).astype(BF16)
            dhr = dza * ga_ref[:, cs]
            a = a_ref[:, cs]
            mult = mult_ref[:, cs]
            ra = ra_ref[:, cs].astype(F32)
            ri = ri_ref[:, cs].astype(F32)
            sp = _softplus(vec[V_A_PARAM:V_A_PARAM + 1])
            m = jnp.where(row == tm - 1, 1.0, _shift_up(a, 1))
            gsum = _scan_up(m, dhr, scan_carry[0:1, cs], scan_scr)
            scan_carry[0:1, cs] = a[0:1, :] * gsum[0:1, :]
            hh = jnp.where(first_t, 0.0, hh_ref[:, cs])
            hprev = _shift_down(jnp.concatenate([hh, hr], axis=0), 1)[8:]
            da = gsum * hprev
            dmult = jnp.where(is_t0, 0.0, gsum * xr * ri)
            dlog_a = da * a - dmult * a * a / mult
            dri = gsum * xr * mult
            dxr = gsum * ri * mult
            small_ref[7:8, cs] += colsum((-C_RG) * ra * dlog_a)
            dpa = (((-C_RG) * sp) * dlog_a * ra * (1.0 - ra))
            dpx = dri * ri * (1.0 - ri)
            small_ref[5:6, cs] += colsum(dpa)
            small_ref[6:7, cs] += colsum(dpx)
            dpa = dpa.astype(BF16)
            dpx = dpx.astype(BF16)
            xrb = xr.astype(BF16)
            dwa_acc[g] += _dot_tn(xrb, dpa)
            dwx_acc[g] += _dot_tn(xrb, dpx)
            dxr = dxr + _dot_nt(dpa, wa_ref[g]) + _dot_nt(dpx, wx_ref[g])
            small_ref[4:5, cs] += colsum(dxr)
            xh = jnp.where(first_t, 0.0, xh_ref[:, cs])
            taps = _conv_taps(jnp.concatenate([xh, x_ref[:, cs]], axis=0))
            dxr_ext = jnp.concatenate([dxr, dxr_carry[:, cs]], axis=0)
            dx = vec[V_CONV_W + 3:V_CONV_W + 4] * dxr
            for j in range(4):
                small_ref[j:j + 1, cs] += colsum(dxr * taps[j])
                if j < 3:
                    dx = dx + vec[V_CONV_W + j:V_CONV_W + j + 1] * _shift_up(dxr_ext, 3 - j)[:tm]
            dxr_carry[:, cs] = dxr[0:8, :]
            dproj_ref[:, cs] = dx.astype(BF16)
            pg = p_ref[:, cs]
            dpooled = dpooled_ref[:, cs]
            pb = _dot(pg, wp_ref[g]) + vec[V_B_POOL:V_B_POOL + 1]
            small_ref[9:10, cs] += colsum(dpooled * pb)
            dpb = dpooled * vec[V_POOL_SCALE:V_POOL_SCALE + 1]
            small_ref[8:9, cs] += colsum(dpb)
            dpbb = dpb.astype(BF16)
            dwp_acc[g] += _dot_tn(pg, dpbb)
            dp = _dot_nt(dpbb, wp_ref[g])
            q = _window_mean(dp, POOL_WINDOWS[g], first_t, head_t)
            sm = jnp.concatenate([q, q_carry[:, cs]], axis=0)
            k = 1
            while k < POOL_WINDOWS[g]:
                sm = sm + _shift_up(sm, k)
                k *= 2
            q_carry[:, cs] = q[0:HALO_U, :]
            dproj_ref[:, 2 * D + g * GW:2 * D + (g + 1) * GW] = (sm[:tm] - dp).astype(BF16)
        dproj_ref[:, 3 * D:] = dg_ref[...]

        @pl.when(i == nb - 1)
        def _():
            dwa_ref[...] = dwa_acc[...].astype(BF16)
            dwx_ref[...] = dwx_acc[...].astype(BF16)
            dwp_ref[...] = dwp_acc[...].astype(BF16)

    rev = lambda i: nb - 1 - i
    tok = pl.BlockSpec((tm, D), lambda i: (rev(i), 0))
    halo8 = lambda k: pl.BlockSpec((8, D), lambda i: (jnp.maximum(rev(i) * (tm // 8) - 1, 0), k))
    wspec = pl.BlockSpec((N_GROUPS, GW, GW), lambda i: (0, 0, 0))
    wshape = jax.ShapeDtypeStruct((N_GROUPS, GW, GW), BF16)
    return pl.pallas_call(
        body, name="mix_bwd", grid=(nb,),
        in_specs=[tok, tok, halo8(0), tok, tok, tok, tok, halo8(0), tok, tok, tok, tok, tok, tok,
                  pl.BlockSpec((tm, 2 * D), lambda i: (rev(i), 0)),
                  pl.BlockSpec((16, D), lambda i: (0, 0)), wspec, wspec, wspec, pl.BlockSpec(memory_space=pl.ANY)],
        out_specs=[pl.BlockSpec((tm, D_IN), lambda i: (rev(i), 0)), wspec, wspec, wspec,
                   pl.BlockSpec((16, D), lambda i: (0, 0))],
        out_shape=[jax.ShapeDtypeStruct((s, D_IN), BF16), wshape, wshape, wshape,
                   jax.ShapeDtypeStruct((16, D), F32)],
        scratch_shapes=[pltpu.VMEM((8, D), F32), pltpu.VMEM((8, D), F32), pltpu.VMEM((HALO_U, D), F32),
                        pltpu.VMEM((3, tm, LANES), F32)] + [pltpu.VMEM((N_GROUPS, GW, GW), F32)] * 3,
        compiler_params=_params(("arbitrary",)),
    )(dza, dpooled, x_rnn, x_rnn, ga, dga, xr, hr, hr, p, *gates, dgates, vecs, w_rg_a, w_rg_x, w_pool, dep)


def _proj_bwd(dproj, x, dx2, modr, vecs, w_in, dep):
    s = x.shape[0]
    tm = min(TM_PROJ, s)

    def body(dp_ref, x_ref, dx2_ref, mod_ref, vec_ref, w_ref, dep_ref, gx_ref, small_ref):
        @pl.when(pl.program_id(0) == 0)
        def _():
            small_ref[...] = jnp.zeros_like(small_ref)

        dh1 = None
        for c in range(D_IN // D):
            cs = slice(c * D, (c + 1) * D)
            part = _dot_nt(dp_ref[:, cs], w_ref[:, cs])
            dh1 = part if dh1 is None else dh1 + part
        xv = x_ref[...]
        r1 = lax.rsqrt(jnp.mean(xv * xv, axis=-1, keepdims=True) + EPS)
        xn1 = xv * r1
        gain = vec_ref[V_G1:V_G1 + 1, :] * (1.0 + mod_ref[M_SC1:M_SC1 + 1, :])
        dxn1 = dh1 * gain
        gx_ref[...] = dx2_ref[...] + r1 * (dxn1 - xn1 * jnp.mean(dxn1 * xn1, axis=-1, keepdims=True))
        small_ref[0:1, :] += jnp.sum(dh1, axis=0, keepdims=True)
        small_ref[1:2, :] += jnp.sum(dh1 * xn1, axis=0, keepdims=True)

    tok = pl.BlockSpec((tm, D), lambda i: (i, 0))
    return pl.pallas_call(
        body, name="proj_bwd", grid=(s // tm,),
        in_specs=[pl.BlockSpec((tm, D_IN), lambda i: (i, 0)), tok, tok,
                  pl.BlockSpec((8, D), lambda i: (0, 0)), pl.BlockSpec((16, D), lambda i: (0, 0)),
                  _resident((D, D_IN)), pl.BlockSpec(memory_space=pl.ANY)],
        out_specs=[tok, pl.BlockSpec((8, D), lambda i: (0, 0))],
        out_shape=[jax.ShapeDtypeStruct((s, D), F32), jax.ShapeDtypeStruct((8, D), F32)],
        compiler_params=_params(("arbitrary",)),
    )(dproj, x, dx2, modr, vecs, w_in, dep)


def _wgrad(a, b, name, square_a=False, dep=None):
    s, ka = a.shape
    n = b.shape[1]
    tka = ka if ka <= 1024 else ka // 2
    tn = n if n <= 1024 else n // 2
    ts = min(TS_WGRAD, s)
    ns = s // ts
    nc = 512
    deps = [] if dep is None else [dep]

    def body(a_ref, b_ref, *refs):
        out_ref, acc_ref = refs[-2:]
        t = pl.program_id(2)

        @pl.when(t == 0)
        def _():
            acc_ref[...] = jnp.zeros_like(acc_ref)

        av = a_ref[...]
        if square_a:
            af = av.astype(F32)
            av = (af * af).astype(BF16)
        for c in range(tn // nc):
            cs = slice(c * nc, (c + 1) * nc)
            acc_ref[:, cs] += _dot_tn(av, b_ref[:, cs])

        @pl.when(t == ns - 1)
        def _():
            out_ref[...] = acc_ref[...].astype(BF16)

    return pl.pallas_call(
        body, name=name, grid=(ka // tka, n // tn, ns),
        in_specs=[pl.BlockSpec((ts, tka), lambda i, j, t: (t, i)),
                  pl.BlockSpec((ts, tn), lambda i, j, t: (t, j))] + [pl.BlockSpec(memory_space=pl.ANY)] * len(deps),
        out_specs=pl.BlockSpec((tka, tn), lambda i, j, t: (i, j)),
        out_shape=jax.ShapeDtypeStruct((ka, n), BF16),
        scratch_shapes=[pltpu.VMEM((tka, tn), F32)],
        compiler_params=_params(("parallel", "parallel", "arbitrary")),
    )(a, b, *deps)


def _window(ref, kind, idx, size):
    start = pl.multiple_of(idx * size, size)
    if kind == 0:
        return ref.at[pl.ds(start, size)]
    if kind == 1:
        return ref.at[:, pl.ds(start, size)]
    return ref.at[:, :, pl.ds(start, size)]


def _mesh_place():
    x, y, c = lax.axis_index("x"), lax.axis_index("y"), lax.axis_index("c")
    return x, y, c, 4 * x + 2 * y + c


def _peer(x, y, c, q):
    px = 1 - x if q & 4 else x
    py = 1 - y if q & 2 else y
    pc = 1 - c if q & 1 else c
    return (px, py, pc), 4 * px + 2 * py + pc


_HBM = pl.BlockSpec(memory_space=pltpu.HBM)
_SEM = pl.BlockSpec(memory_space=pltpu.SEMAPHORE)
_EFFECT = pltpu.SideEffectType.DATAFLOW_SIDE_EFFECTING


N_NEAR = 4


def _near(x, y, c):
    out = [((x, y, 1 - c), 4 * x + 2 * y + 1 - c)]
    for j in (1, 2, 3):
        px = 1 - x if j & 2 else x
        py = 1 - y if j & 1 else y
        out.append(((px, py, c), 4 * px + 2 * py + c))
    return out


def _remote(src, dst, send_sems, recv_sems, slot, device):
    return pltpu.make_async_remote_copy(src_ref=src, dst_ref=dst, send_sem=send_sems.at[slot], recv_sem=recv_sems.at[slot],
                                        device_id=device, device_id_type=MESH)


def _split_call(name, arrays, sems_in, n_new_sems, after, emit):
    na, ns, nn = len(arrays), len(sems_in), len(n_new_sems)

    def body(*refs):
        emit(refs[:na], refs[na:na + ns], refs[na + ns + 1:na + ns + 1 + nn])
        refs[-1][...] = jnp.zeros_like(refs[-1])

    outs = pl.pallas_call(
        body, name=name,
        out_shape=(*[pltpu.SemaphoreType.DMA((m,)) for m in n_new_sems],
                   *[pltpu.HBM(a.shape, a.dtype) for a in arrays], jax.ShapeDtypeStruct((8, 128), F32)),
        in_specs=[_HBM] * na + [_SEM] * ns + [pl.BlockSpec(memory_space=pl.ANY)],
        out_specs=(*[_SEM] * nn, *[_HBM] * na, pl.BlockSpec(memory_space=pltpu.VMEM)),
        input_output_aliases={i: nn + i for i in range(na)},
        compiler_params=pltpu.CompilerParams(has_side_effects=_EFFECT),
    )(*[pltpu.with_memory_space_constraint(a, pltpu.HBM) for a in arrays], *sems_in, after)
    return list(outs[:nn]), list(outs[nn:nn + na]), outs[-1]


def _together(name, steps, after):
    parts = [(ex.arrays, ex.sems, ex.new_sems[step], getattr(ex, "emit_" + step)) for ex, step in steps]

    def emit(arr, old, new):
        ia = io = ib = 0
        for arrays, sems, new_sems, emit_one in parts:
            emit_one(arr[ia:ia + len(arrays)], old[io:io + len(sems)], new[ib:ib + len(new_sems)])
            ia, io, ib = ia + len(arrays), io + len(sems), ib + len(new_sems)

    new, arrays, token = _split_call(name, [a for p in parts for a in p[0]], [s for p in parts for s in p[1]],
                                     [m for p in parts for m in p[2]], after, emit)
    out = []
    ia = ib = 0
    for (ex, _), (arrs, sems, new_sems, _) in zip(steps, parts):
        ex.arrays, ex.sems, ex.token = arrays[ia:ia + len(arrs)], [*sems, *new[ib:ib + len(new_sems)]], token
        ia, ib = ia + len(arrs), ib + len(new_sems)
        out.append(ex.arrays[ex.n:])
    return out


class _AllGather:
    def __init__(self, shards, kinds, name):
        self.n, self.kinds, self.name = len(shards), kinds, name
        self.sizes = [s.shape[k] for s, k in zip(shards, kinds)]
        lands = []
        for s, k in zip(shards, kinds):
            dims = list(s.shape)
            dims[k] *= N_DEV
            lands.append(lax.empty(tuple(dims), s.dtype))
        self.arrays, self.sems = [*shards, *lands], []

    def window(self, arr, k, idx):
        return _window(arr[self.n + k], self.kinds[k], idx, self.sizes[k])

    def start(self, after):
        _together(self.name + "_start", [(self, "start")], after)

    def forward(self, after):
        _together(self.name + "_forward", [(self, "forward")], after)

    def finish(self, after):
        return _together(self.name + "_finish", [(self, "finish")], after)[0]


class _Gather(_AllGather):
    def __init__(self, shards, kinds, name):
        super().__init__(shards, kinds, name)
        n = self.n
        self.new_sems = dict(start=[n * N_NEAR, n * N_NEAR, n], forward=[n * N_NEAR] * 2, finish=[])

    def emit_start(self, arr, _, new):
        x, y, c, me = _mesh_place()
        for k in range(self.n):
            pltpu.make_async_copy(arr[k], self.window(arr, k, me), new[2].at[k]).start()
        for k in range(self.n):
            for j, (dev, _) in enumerate(_near(x, y, c)):
                _remote(arr[k], self.window(arr, k, me), new[0], new[1], k * N_NEAR + j, dev).start()

    def emit_forward(self, arr, old, new):
        x, y, c, _ = _mesh_place()
        near = _near(x, y, c)
        for k in range(self.n):
            for j in (1, 2, 3):
                dev, idx = near[j]
                landed = self.window(arr, k, idx)
                _remote(arr[k], landed, old[0], old[1], k * N_NEAR + j, dev).wait_recv()
                _remote(landed, landed, new[0], new[1], k * N_NEAR + j, near[0][0]).start()

    def emit_finish(self, arr, old, _):
        x, y, c, me = _mesh_place()
        near = _near(x, y, c)
        other_core = near[0][0]
        for k in range(self.n):
            win = lambda idx: self.window(arr, k, idx)
            pltpu.make_async_copy(arr[k], win(me), old[2].at[k]).wait()
            for j, (dev, idx) in enumerate(near):
                _remote(arr[k], win(me), old[0], old[1], k * N_NEAR + j, dev).wait_send()
            _remote(arr[k], win(near[0][1]), old[0], old[1], k * N_NEAR, other_core).wait_recv()
            for j in (1, 2, 3):
                idx = near[j][1]
                _remote(win(idx), win(idx), old[3], old[4], k * N_NEAR + j, other_core).wait_send()
                _remote(arr[k], win(idx + 1 - 2 * c), old[3], old[4], k * N_NEAR + j, other_core).wait_recv()


class _Spread(_AllGather):
    def __init__(self, shards, kinds, name):
        super().__init__(shards, kinds, name)
        n = self.n
        self.new_sems = dict(start=[n * N_DEV, n * N_DEV, n], finish=[])

    def emit_start(self, arr, _, new):
        x, y, c, me = _mesh_place()
        for k in range(self.n):
            mine = self.window(arr, k, me)
            pltpu.make_async_copy(arr[k], mine, new[2].at[k]).start()
            for q in range(1, N_DEV):
                _remote(arr[k], mine, new[0], new[1], k * N_DEV + q, _peer(x, y, c, q)[0]).start()

    def emit_finish(self, arr, old, _):
        x, y, c, me = _mesh_place()
        for k in range(self.n):
            win = lambda idx: self.window(arr, k, idx)
            pltpu.make_async_copy(arr[k], win(me), old[2].at[k]).wait()
            for q in range(1, N_DEV):
                peer, peer_idx = _peer(x, y, c, q)
                _remote(arr[k], win(me), old[0], old[1], k * N_DEV + q, peer).wait_send()
                _remote(arr[k], win(peer_idx), old[0], old[1], k * N_DEV + q, peer).wait_recv()


class _Scatter:
    def __init__(self, partials, kinds, after, name):
        self.n, self.kinds, self.name, self.partials = len(partials), kinds, name, partials
        self.sizes = [p.shape[k] // N_DEV for p, k in zip(partials, kinds)]
        n, sizes = self.n, self.sizes
        self.slot_shapes = []
        for p, k, size in zip(partials, kinds, sizes):
            dims = list(p.shape)
            dims[k] = size
            self.slot_shapes.append((N_NEAR, *dims))
        slots = [lax.empty(sh, p.dtype) for sh, p in zip(self.slot_shapes, partials)]

        def emit(arr, _, new):
            x, y, c, _ = _mesh_place()
            near = _near(x, y, c)
            for k in range(n):
                for j in range(N_NEAR):
                    owner = near[j][1] if j == 0 else near[j][1] + 1 - 2 * c
                    _remote(_window(arr[k], kinds[k], owner, sizes[k]), arr[n + k].at[j], new[0], new[1],
                            k * N_NEAR + j, near[0][0]).start()

        self.sems, self.arrays, self.token = _split_call(name + "_start", [*partials, *slots], [], [n * N_NEAR] * 2,
                                                         after, emit)

    def combine_and_send(self, own4, after):
        n, kinds, sizes = self.n, self.kinds, self.sizes

        def emit_wait(arr, old, _):
            x, y, c, _ = _mesh_place()
            near = _near(x, y, c)
            for k in range(n):
                for j in range(N_NEAR):
                    owner = near[j][1] if j == 0 else near[j][1] + 1 - 2 * c
                    cp = _remote(_window(arr[k], kinds[k], owner, sizes[k]), arr[n + k].at[j], old[0], old[1],
                                 k * N_NEAR + j, near[0][0])
                    cp.wait_send()
                    cp.wait_recv()

        _, arrays, _ = _split_call(self.name + "_landed", self.arrays, self.sems, [], after, emit_wait)
        chip_sums = _chip_sums(arrays[:n], arrays[n:], kinds, sizes, own4, self.name + "_combine")
        arrivals = [lax.empty((N_NEAR - 1, *sh[1:]), p.dtype) for sh, p in zip(self.slot_shapes, self.partials)]

        def emit_send(arr, _, new):
            x, y, c, _ = _mesh_place()
            near = _near(x, y, c)
            for k in range(n):
                for j in (1, 2, 3):
                    _remote(arr[k].at[j], arr[n + k].at[j - 1], new[0], new[1], k * N_NEAR + j, near[j][0]).start()

        self.sems, self.arrays, self.token = _split_call(self.name + "_send", [*chip_sums, *arrivals], [],
                                                         [n * N_NEAR] * 2, own4, emit_send)

    def finish(self, after):
        n = self.n

        def emit(arr, old, _):
            x, y, c, _ = _mesh_place()
            near = _near(x, y, c)
            for k in range(n):
                for j in (1, 2, 3):
                    cp = _remote(arr[k].at[j], arr[n + k].at[j - 1], old[0], old[1], k * N_NEAR + j, near[j][0])
                    cp.wait_send()
                    cp.wait_recv()

        _, arrays, _ = _split_call(self.name + "_finish", self.arrays, self.sems, [], after, emit)
        return arrays[:n], arrays[n:]


def _chip_sums(partials, slots, kinds, sizes, own4, name):
    n = len(partials)

    def body(own_ref, *refs):
        for k in range(n):
            refs[2 * n + k][...] = (refs[k][...].astype(F32) + refs[n + k][...].astype(F32)).astype(BF16)

    in_specs, slot_specs = [], []
    for p, s, kind, size in zip(partials, slots, kinds, sizes):
        block = list(p.shape)
        block[kind] = size
        nd = len(block)
        in_specs.append(pl.BlockSpec(tuple(block), functools.partial(
            lambda j, own, kind, nd: tuple(own[j] if d == kind else 0 for d in range(nd)), kind=kind, nd=nd)))
        slot_specs.append(pl.BlockSpec((None, *block), functools.partial(
            lambda j, own, nd: (j,) + (0,) * nd, nd=nd)))
    return pl.pallas_call(
        body, name=name,
        grid_spec=pltpu.PrefetchScalarGridSpec(num_scalar_prefetch=1, grid=(N_NEAR,),
                                               in_specs=in_specs + slot_specs, out_specs=slot_specs),
        out_shape=[jax.ShapeDtypeStruct(s.shape, s.dtype) for s in slots],
        compiler_params=_params(("arbitrary",)),
    )(own4, *partials, *slots)


def _to_bf16(arrays, name, dep=None):
    n = len(arrays)
    deps = [] if dep is None else [dep]

    def body(*refs):
        for src, dst in zip(refs[:n], refs[n + len(deps):]):
            dst[...] = src[...].astype(BF16)

    vmem = pl.BlockSpec(memory_space=pltpu.VMEM)
    return pl.pallas_call(body, name=name, out_shape=[jax.ShapeDtypeStruct(a.shape, BF16) for a in arrays],
                          in_specs=[vmem] * n + [pl.BlockSpec(memory_space=pl.ANY)] * len(deps), out_specs=[vmem] * n,
                          compiler_params=pltpu.CompilerParams(vmem_limit_bytes=V7X_VMEM_LIMIT))(*arrays, *deps)


def _silu(c):
    return c * _sigmoid_tail(c)


def _ada_fwd(c_all, w_ada, b_ada_cols, dep):
    def body(c_ref, w_ref, b_ref, dep_ref, out_ref):
        out_ref[...] = jnp.dot(_silu(c_ref[...]), w_ref[...], preferred_element_type=F32,
                               precision=lax.Precision.HIGHEST) + b_ref[...]

    vmem = pl.BlockSpec(memory_space=pltpu.VMEM)
    return pl.pallas_call(
        body, name="ada_fwd", in_specs=[vmem, vmem, vmem, pl.BlockSpec(memory_space=pl.ANY)], out_specs=vmem,
        out_shape=jax.ShapeDtypeStruct((N_DEV, w_ada.shape[1]), F32),
    )(c_all, w_ada, b_ada_cols, dep)


def _adam(w, g, m, v):
    m = ADAM_B1 * m + (1.0 - ADAM_B1) * g
    v = ADAM_B2 * v + (1.0 - ADAM_B2) * (g * g)
    m_hat = m / (1.0 - ADAM_B1 ** ADAM_STEP)
    v_hat = v / (1.0 - ADAM_B2 ** ADAM_STEP)
    delta = -ADAM_LR * (m_hat / (jnp.sqrt(v_hat) + ADAM_EPS) + ADAM_WD * w)
    return delta, m, v


def _ada_bwd_adam(c_all, dmod_cols, w, m, v):
    def body(c_ref, d_ref, w_ref, m_ref, v_ref, g_ref, delta_ref, nm_ref, nv_ref):
        g = lax.dot_general(_silu(c_ref[...]), d_ref[...], (((0,), (0,)), ((), ())),
                            preferred_element_type=F32, precision=lax.Precision.HIGHEST)
        g_ref[...] = g
        delta_ref[...], nm_ref[...], nv_ref[...] = _adam(w_ref[...], g, m_ref[...], v_ref[...])

    sd = jax.ShapeDtypeStruct(w.shape, F32)
    return pl.pallas_call(body, name="ada_bwd_adam", out_shape=[sd] * 4,
                          compiler_params=pltpu.CompilerParams(vmem_limit_bytes=V7X_VMEM_LIMIT),
                          )(c_all, dmod_cols, w, m, v)


def _adam_group(chip_sums, arrivals, ws, ms, vs, n_tiles, name):
    n = len(ws)

    def body(*refs):
        for k in range(n):
            c_ref, a_ref, w_ref, m_ref, v_ref = (refs[j * n + k] for j in range(5))
            g_ref, delta_ref, nm_ref, nv_ref = (refs[(5 + j) * n + k] for j in range(4))
            g = c_ref[...].astype(F32)
            for j in range(N_NEAR - 1):
                g = g + a_ref[j].astype(F32)
            g_ref[...] = g
            delta_ref[...], nm_ref[...], nv_ref[...] = _adam(w_ref[...], g, m_ref[...], v_ref[...])

    tiles = [(w.shape[0] // n_tiles, w.shape[1]) for w in ws]
    blk = [pl.BlockSpec(t, lambda i: (i, 0)) for t in tiles]
    return pl.pallas_call(
        body, name=name, grid=(n_tiles,),
        in_specs=[pl.BlockSpec((None, *t), lambda i: (0, i, 0)) for t in tiles]
        + [pl.BlockSpec((N_NEAR - 1, *t), lambda i: (0, i, 0)) for t in tiles] + blk * 3,
        out_specs=blk * 4, out_shape=[jax.ShapeDtypeStruct(w.shape, F32) for w in ws] * 4,
        compiler_params=_params(("parallel",)),
    )(*chip_sums, *arrivals, *ws, *ms, *vs)


N_SMALL = 40
N_SMALL_PARAMS = 11


def _pack_vecs(conv_w_full, rows):
    def body(cw_ref, *refs):
        out = refs[-1]
        out[...] = jnp.zeros_like(out)
        out[0:4, :] = cw_ref[0:4, :]
        for r, ref in enumerate(refs[:-1]):
            out[4 + r:5 + r, :] = ref[...]

    return pl.pallas_call(body, name="pack_vecs", out_shape=jax.ShapeDtypeStruct((16, D), F32))(conv_w_full, *rows)


def _small_finish(gathered, conv_cols, mod_all, vecs, ws, ms, vs):
    n = N_SMALL_PARAMS

    def body(g_ref, conv_ref, mod_ref, vec_ref, *refs):
        w_refs, m_refs, v_refs = refs[:n], refs[n:2 * n], refs[2 * n:3 * n]
        outs = refs[3 * n:]
        g1 = vec_ref[V_G1:V_G1 + 1, :]
        g2 = vec_ref[V_G2:V_G2 + 1, :]
        zero = jnp.zeros((1, D), F32)
        dg1, dg2, dgf, loss_lanes = zero, zero, zero, zero
        mixer = jnp.zeros((16, D), F32)
        db_ada = jnp.zeros((6, D), F32)
        d_conv_w = jnp.zeros(conv_ref.shape[1:], F32)
        for b in range(N_DEV):
            gb = g_ref[b]
            mod = mod_ref[b]
            q1 = gb[33:34]
            q2 = gb[9:10]
            dmod = jnp.concatenate([gb[32:33], q1 * g1, gb[10:11], gb[8:9], q2 * g2, gb[1:2]], axis=0)
            outs[4 * n][b] = dmod
            db_ada = db_ada + dmod
            dg1 = dg1 + q1 * (1.0 + mod[M_SC1:M_SC1 + 1])
            dg2 = dg2 + q2 * (1.0 + mod[M_SC2:M_SC2 + 1])
            dgf = dgf + gb[0:1]
            loss_lanes = loss_lanes + gb[2:3]
            mixer = mixer + gb[16:32]
            d_conv_w = d_conv_w + conv_ref[b]
        d_a_param = mixer[7:8] * _sigmoid_tail(vec_ref[V_A_PARAM:V_A_PARAM + 1, :])
        grads = [dg1, dg2, mixer[4:5], mixer[5:6], mixer[6:7], d_a_param, mixer[8:9], mixer[9:10], dgf,
                 db_ada, d_conv_w]

        def load(ref, rows):
            if ref.shape[0] == rows:
                return ref[...]
            return jnp.concatenate([ref[:, j * D:(j + 1) * D] for j in range(rows)], axis=0)

        def store(ref, val):
            if ref.shape == val.shape:
                ref[...] = val
            else:
                for j in range(val.shape[0]):
                    ref[:, j * D:(j + 1) * D] = val[j:j + 1]

        for k in range(n):
            rows = grads[k].shape[0]
            results = (grads[k], *_adam(load(w_refs[k], rows), grads[k], load(m_refs[k], rows), load(v_refs[k], rows)))
            for which, val in enumerate(results):
                store(outs[which * n + k], val)
        outs[4 * n + 1][...] = jnp.broadcast_to(jnp.sum(loss_lanes, axis=1, keepdims=True), (8, 128))

    shapes = [jax.ShapeDtypeStruct(w.shape, F32) for w in ws]
    return pl.pallas_call(
        body, name="small_finish",
        out_shape=shapes * 4 + [jax.ShapeDtypeStruct((N_DEV, 6, D), F32), jax.ShapeDtypeStruct((8, 128), F32)],
    )(gathered, conv_cols, mod_all, vecs, *ws, *ms, *vs)


def _pad_rows(a, rows):
    return jnp.pad(a, ((0, rows - a.shape[0]), (0, 0)))


def kernel(x, c, norm_mix_g, norm_mlp_g, w_ada, b_ada, w_in, conv_w, conv_b, w_rg_a, b_rg_a, w_rg_x, b_rg_x, a_param, w_branch_a, w_pool, b_pool, pool_scale, w_branch_b, w_out, w_up, w_down, final_g, loss_target, m_norm_mix_g, m_norm_mlp_g, m_w_ada, m_b_ada, m_w_in, m_conv_w, m_conv_b, m_w_rg_a, m_b_rg_a, m_w_rg_x, m_b_rg_x, m_a_param, m_w_branch_a, m_w_pool, m_b_pool, m_pool_scale, m_w_branch_b, m_w_out, m_w_up, m_w_down, m_final_g, v_norm_mix_g, v_norm_mlp_g, v_w_ada, v_b_ada, v_w_in, v_conv_w, v_conv_b, v_w_rg_a, v_b_rg_a, v_w_rg_x, v_b_rg_x, v_a_param, v_w_branch_a, v_w_pool, v_b_pool, v_pool_scale, v_w_branch_b, v_w_out, v_w_up, v_w_down, v_final_g):
    me = 4 * lax.axis_index("x") + 2 * lax.axis_index("y") + lax.axis_index("c")
    s = x.shape[1]
    x2d = x.reshape(s, D)
    target = loss_target.reshape(s, D)
    n_ada = w_ada.shape[2]

    b_ada_cols = lax.dynamic_slice(b_ada, (0, me * n_ada), (1, n_ada))

    sharded = dict(w_in=(w_in[0], 1), w_up=(w_up[0], 1), w_down=(w_down[0], 0), w_branch_a=(w_branch_a[0], 0),
                   w_branch_b=(w_branch_b[0], 0), w_out=(w_out[0], 0), w_rg_a=(w_rg_a[0], 1), w_rg_x=(w_rg_x[0], 1),
                   w_pool=(w_pool[0], 1))
    kind = {k: v[1] for k, v in sharded.items()}
    first_names = ["w_in"]
    later_names = [k for k in sharded if k not in first_names]
    mix_names = ["w_rg_a", "w_rg_x", "w_pool"]
    branch_names = ["w_branch_a", "w_branch_b", "w_out"]
    mlp_names = ["w_up", "w_down"]

    def gather(group, after, name):
        exchange = _Gather([shard[k] for k in group], [kind[k] for k in group], name)
        exchange.start(after)
        return exchange

    shard = dict(zip(first_names, _to_bf16([sharded[k][0] for k in first_names], "to_bf16_first")))
    spread_c = _Spread([c, conv_w[0]], [0, 1], "spread_c")
    g_first = _Gather([shard[k] for k in first_names], [kind[k] for k in first_names], "gather_first")
    _together("first_start", [(spread_c, "start"), (g_first, "start")], c)
    shard.update(zip(later_names, _to_bf16([sharded[k][0] for k in later_names], "to_bf16_later", dep=g_first.token)))

    c_all, conv_w_full = spread_c.finish(g_first.token)
    mod_part = _ada_fwd(c_all, w_ada[0], b_ada_cols, g_first.token)
    vecs = _pack_vecs(conv_w_full, [conv_b, b_rg_a, b_rg_x, a_param, b_pool, pool_scale,
                                    norm_mix_g, norm_mlp_g, final_g.reshape(1, D)])
    spread_mod = _Spread([mod_part], [0], "spread_mod")
    spread_mod.start(vecs)
    g_mix = gather(mix_names, spread_mod.token, "gather_mix")
    g_branch = gather(branch_names, g_mix.token, "gather_branch")
    g_mlp = gather(mlp_names, g_branch.token, "gather_mlp")
    g_first.forward(g_mlp.token)
    wg = dict(zip(first_names, g_first.finish(g_first.token)))
    mod_parts, = spread_mod.finish(g_first.token)
    mod_all = jnp.transpose(mod_parts.reshape(N_DEV, N_DEV, n_ada), (1, 0, 2)).reshape(N_DEV, 6, D)
    modr = _pad_rows(lax.dynamic_index_in_dim(mod_all, me, 0, keepdims=False), 8)

    h1, x_rnn, u_pool, ga, dga, sa, sb = _proj_fwd(x2d, modr, vecs, wg["w_in"])
    _together("mixer_forward", [(g_mix, "forward"), (g_branch, "forward")], h1)
    wg.update(zip(mix_names, g_mix.finish(g_branch.token)))
    xr, hr, za, p, pooled, *gates = _mix_fwd(x_rnn, u_pool, ga, vecs, wg["w_rg_a"], wg["w_rg_x"], wg["w_pool"],
                                             dep=g_branch.token)
    g_mlp.forward(za)
    wg.update(zip(branch_names, g_branch.finish(g_mlp.token)))
    ba, bb, merged, o, x2, h2 = _branch_fwd(za, pooled, sa, sb, x2d, modr, vecs,
                                            wg["w_branch_a"], wg["w_branch_b"], wg["w_out"])
    wg.update(zip(mlp_names, g_mlp.finish(h2)))
    ru, dx3, d_dn, small_f = _mlp_fwd(h2, x2, target, modr, vecs, wg["w_up"], wg["w_down"])

    near = _near(lax.axis_index("x"), lax.axis_index("y"), lax.axis_index("c"))
    own4 = jnp.stack([me, near[1][1], near[2][1], near[3][1]]).astype(jnp.int32)

    def scatter(group, partial, after, name):
        return _Scatter([partial[k] for k in group], [kind[k] for k in group], after, name)

    dup, dx2, do, small_m = _mlp_bwd(d_dn, ru, x2, dx3, o, modr, vecs, wg["w_up"], wg["w_down"])
    partial = dict(w_up=_wgrad(h2, dup, "wgrad_up"), w_down=_wgrad(ru, d_dn, "wgrad_down", square_a=True))
    s_mlp = scatter(mlp_names, partial, dx2, "scatter_mlp")

    dba, dbb, dgates, dza, dpooled = _branch_bwd(do, sa, sb, ba, bb, wg["w_branch_a"], wg["w_branch_b"], wg["w_out"],
                                                 dep=s_mlp.token)
    s_mlp.combine_and_send(own4, dza)
    dproj, dw_rg_a, dw_rg_x, dw_pool, small_x = _mix_bwd(dza, dpooled, x_rnn, ga, dga, xr, hr, p, gates, dgates,
                                                         vecs, wg["w_rg_a"], wg["w_rg_x"], wg["w_pool"],
                                                         dep=s_mlp.token)
    partial.update(w_branch_a=_wgrad(za, dba, "wgrad_branch_a"), w_branch_b=_wgrad(pooled, dbb, "wgrad_branch_b"),
                   w_out=_wgrad(merged, do, "wgrad_out"),
                   w_rg_a=dw_rg_a, w_rg_x=dw_rg_x, w_pool=dw_pool)
    mixer_names = ["w_rg_a", "w_rg_x", "w_pool", "w_branch_a", "w_branch_b", "w_out"]
    s_mixer = scatter(mixer_names, partial, s_mlp.token, "scatter_mixer")

    partial["w_in"] = _wgrad(h1, dproj, "wgrad_in", dep=s_mixer.token)
    s_in = scatter(["w_in"], partial, s_mixer.token, "scatter_in")
    s_mixer.combine_and_send(own4, s_in.token)
    s_in.combine_and_send(own4, s_mixer.token)
    grad_x, small_p = _proj_bwd(dproj, x2d, dx2, modr, vecs, wg["w_in"], dep=s_in.token)

    locals_ = dict(w_in=(w_in, m_w_in, v_w_in), w_up=(w_up, m_w_up, v_w_up), w_down=(w_down, m_w_down, v_w_down),
                   w_branch_a=(w_branch_a, m_w_branch_a, v_w_branch_a),
                   w_branch_b=(w_branch_b, m_w_branch_b, v_w_branch_b), w_out=(w_out, m_w_out, v_w_out),
                   w_rg_a=(w_rg_a, m_w_rg_a, v_w_rg_a), w_rg_x=(w_rg_x, m_w_rg_x, v_w_rg_x),
                   w_pool=(w_pool, m_w_pool, v_w_pool))
    res = {}

    def finish(group, exchange, after, n_tiles, name):
        chip_sums, arrivals = exchange.finish(after)
        flat = lambda t: t.reshape(-1, t.shape[-1])
        shapes = [flat(locals_[k][0]).shape for k in group]
        outs = _adam_group([cs.reshape(N_NEAR, *sh) for cs, sh in zip(chip_sums, shapes)],
                           [ar.reshape(N_NEAR - 1, *sh) for ar, sh in zip(arrivals, shapes)],
                           *[[flat(locals_[k][j]) for k in group] for j in range(3)], n_tiles, name)
        for i, k in enumerate(group):
            res[k] = [outs[j * len(group) + i].reshape(locals_[k][0].shape) for j in range(4)]
        return res[group[-1]][0]

    small = jnp.concatenate([small_f, small_m, small_x, small_p], axis=0)
    g_small = _Spread([small], [0], "spread_small")
    g_small.start(grad_x)
    done = finish(mlp_names, s_mlp, g_small.token, 4, "adam_mlp")
    done = finish(mixer_names, s_mixer, done, 2, "adam_mixer")
    done = finish(["w_in"], s_in, done, 4, "adam_in")
    small_all, = g_small.finish(done)
    small_all = small_all.reshape(N_DEV, N_SMALL, D)

    conv_cols = lax.dynamic_slice(small_all, (0, 16, me * (D // N_DEV)), (N_DEV, 4, D // N_DEV))

    def smalls(ng, nl, cb, bra, brx, ap, bp, ps, fg, ba_, cw):
        return [ng, nl, cb, bra, brx, ap, bp, ps, fg.reshape(1, D), ba_, cw[0]]

    small_names = ["norm_mix_g", "norm_mlp_g", "conv_b", "b_rg_a", "b_rg_x", "a_param", "b_pool", "pool_scale",
                   "final_g", "b_ada", "conv_w"]
    fin = _small_finish(
        small_all, conv_cols, mod_all, vecs,
        smalls(norm_mix_g, norm_mlp_g, conv_b, b_rg_a, b_rg_x, a_param, b_pool, pool_scale, final_g, b_ada, conv_w),
        smalls(m_norm_mix_g, m_norm_mlp_g, m_conv_b, m_b_rg_a, m_b_rg_x, m_a_param, m_b_pool, m_pool_scale,
               m_final_g, m_b_ada, m_conv_w),
        smalls(v_norm_mix_g, v_norm_mlp_g, v_conv_b, v_b_rg_a, v_b_rg_x, v_a_param, v_b_pool, v_pool_scale,
               v_final_g, v_b_ada, v_conv_w))
    dmod_all, loss_tile = fin[4 * N_SMALL_PARAMS], fin[4 * N_SMALL_PARAMS + 1]
    dmod_cols = lax.dynamic_slice(dmod_all.reshape(N_DEV, 6 * D), (0, me * n_ada), (N_DEV, n_ada))
    res["w_ada"] = [t.reshape(w_ada.shape) for t in _ada_bwd_adam(c_all, dmod_cols, w_ada[0], m_w_ada[0], v_w_ada[0])]

    def final_shape(k, t):
        if k == "final_g":
            return t.reshape(D)
        if k == "conv_w":
            return t.reshape(conv_w.shape)
        return t

    for i, k in enumerate(small_names):
        res[k] = [final_shape(k, fin[which * N_SMALL_PARAMS + i]) for which in range(4)]
    order = ["norm_mix_g", "norm_mlp_g", "w_ada", "b_ada", "w_in", "conv_w", "conv_b", "w_rg_a", "b_rg_a", "w_rg_x",
             "b_rg_x", "a_param", "w_branch_a", "w_pool", "b_pool", "pool_scale", "w_branch_b", "w_out", "w_up",
             "w_down", "final_g"]
    outs = [loss_tile[0, 0], grad_x.reshape(x.shape)]
    for which in range(4):
        for k in order:
            outs.append(res[k][which])
    return tuple(outs)
```

```python
import functools

import jax
import jax.numpy as jnp
from jax import lax
from jax.experimental import pallas as pl
from jax.experimental.pallas import tpu as pltpu

F32 = jnp.float32
BF16 = jnp.bfloat16
MESH = pl.DeviceIdType.MESH

N_DEV = 8
D = 1024
N_GROUPS = 4
GW = D // N_GROUPS
D_IN = 5 * D
D_FF = 4 * D
POOL_WINDOWS = (2, 4, 8, 16)
HALO_X = 8
HALO_U = 16
EPS = 1e-6
C_RG = 8.0
ADAM_LR, ADAM_B1, ADAM_B2, ADAM_EPS, ADAM_WD, ADAM_STEP = 0.001, 0.9, 0.999, 1e-08, 0.01, 10

V7X_VMEM_LIMIT = 56 * 1024 * 1024

V_CONV_W, V_CONV_B, V_B_RG_A, V_B_RG_X, V_A_PARAM, V_B_POOL, V_POOL_SCALE, V_G1, V_G2, V_GF = 0, 4, 5, 6, 7, 8, 9, 10, 11, 12
M_SH1, M_SC1, M_GT1, M_SH2, M_SC2, M_GT2 = 0, 1, 2, 3, 4, 5

TM_PROJ = 512
TM_MIX = 256
TM_BRANCH = 512
TM_MLP = 512
TM_MLP_BWD = 256
TS_WGRAD = 1024


def _params(semantics):
    return pltpu.CompilerParams(dimension_semantics=semantics, vmem_limit_bytes=V7X_VMEM_LIMIT)


def _resident(shape):
    return pl.BlockSpec(shape, lambda *_: (0,) * len(shape), pipeline_mode=pl.Buffered(1))


def _dot(a, b):
    return jnp.dot(a, b, preferred_element_type=F32)


def _dot_nt(a, b):
    return lax.dot_general(a, b, (((1,), (1,)), ((), ())), preferred_element_type=F32)


def _dot_tn(a, b):
    return lax.dot_general(a, b, (((0,), (0,)), ((), ())), preferred_element_type=F32)


def _sigmoid(x):
    return 0.5 * jnp.tanh(0.5 * x) + 0.5


def _sigmoid_tail(x):
    return 1.0 / (1.0 + jnp.exp(-x))


def _gelu_and_grad(x):
    k = 0.7978845608028654
    x2 = x * x
    t = jnp.tanh(k * (x + 0.044715 * x * x2))
    g = 0.5 * x * (1.0 + t)
    dg = 0.5 * (1.0 + t) + 0.5 * x * (1.0 - t * t) * (k * (1.0 + 3.0 * 0.044715 * x2))
    return g, dg


def _softplus(a):
    e = jnp.exp(-jnp.abs(a))
    u = 1.0 + e
    log1p_e = jnp.where(u == 1.0, e, jnp.log(u) * e / jnp.where(u == 1.0, 1.0, u - 1.0))
    return jnp.maximum(a, 0.0) + log1p_e


def _neg_expm1(z):
    series = -(z * (1.0 + z * (0.5 + z * (1.0 / 6.0 + z * (1.0 / 24.0 + z * (1.0 / 120.0))))))
    return jnp.where(z > -0.1, series, 1.0 - jnp.exp(z))


def _shift_down(x, k):
    return pltpu.roll(x, k, 0)


def _shift_up(x, k):
    return pltpu.roll(x, x.shape[0] - k, 0)


def _rglru_gates(xr, w_a, w_x, b_a, b_x, a_param, is_t0):
    xb = xr.astype(BF16)
    ra = _sigmoid(_dot(xb, w_a) + b_a)
    ri = _sigmoid(_dot(xb, w_x) + b_x)
    sp = _softplus(a_param)
    log_a = (-C_RG) * ra * sp
    a = jnp.exp(log_a)
    mult = jnp.where(is_t0, 1.0, jnp.sqrt(_neg_expm1(2.0 * log_a)))
    return ra, ri, sp, a, mult


SUBLANES = 8


LANES = 128


def _scan_strip(a, b, carry, scr, down):
    t = b.shape[0]
    g = t // SUBLANES
    a3 = a.reshape(g, SUBLANES, LANES)
    b3 = b.reshape(g, SUBLANES, LANES)
    sub = lax.broadcasted_iota(jnp.int32, (g, SUBLANES, LANES), 1)
    for k in (1, 2, 4):
        keep = sub >= k if down else sub < SUBLANES - k
        shift = k if down else SUBLANES - k
        b3 = b3 + a3 * jnp.where(keep, pltpu.roll(b3, shift, 1), 0.0)
        a3 = a3 * jnp.where(keep, pltpu.roll(a3, shift, 1), 1.0)
    scr[0] = a3.reshape(t, LANES)
    scr[1] = b3.reshape(t, LANES)
    end_row = SUBLANES - 1 if down else 0
    ag = scr[0, pl.ds(end_row, g, stride=SUBLANES), :]
    bg = scr[1, pl.ds(end_row, g, stride=SUBLANES), :]
    rg = lax.broadcasted_iota(jnp.int32, (g, LANES), 0)
    edge = 0 if down else g - 1
    bg = bg + jnp.where(rg == edge, ag * carry, 0.0)
    k = 1
    while k < g:
        keep = rg >= k if down else rg < g - k
        shift = k if down else g - k
        bg = bg + ag * jnp.where(keep, pltpu.roll(bg, shift, 0), 0.0)
        if 2 * k < g:
            ag = ag * pltpu.roll(ag, shift, 0)
        k *= 2
    entering = jnp.where(rg != edge, pltpu.roll(bg, 1 if down else g - 1, 0), carry)
    for r in range(SUBLANES):
        scr[2, pl.ds(r, g, stride=SUBLANES), :] = entering
    return scr[1] + scr[0] * scr[2], bg[g - 1:g, :]


def _scan_strips(a, b, carry, scr, down):
    outs = [_scan_strip(a[:, c:c + LANES], b[:, c:c + LANES], carry[:, c:c + LANES], scr, down)
            for c in range(0, b.shape[1], LANES)]
    return jnp.concatenate([o[0] for o in outs], axis=1), jnp.concatenate([o[1] for o in outs], axis=1)


def _scan_down(a, b, carry, scr):
    return _scan_strips(a, b, carry, scr, True)


def _scan_up(m, b, carry, scr):
    return _scan_strips(m, b, carry, scr, False)[0]


def _window_mean(sums, window, first_block, head_t):
    scaled = sums * (1.0 / window)
    head = jnp.where(first_block, sums[:HALO_U] / jnp.minimum(head_t, float(window)), scaled[:HALO_U])
    return jnp.concatenate([head, scaled[HALO_U:]], axis=0)


def _conv_taps(x_ext):
    return [_shift_down(x_ext, 3 - j)[HALO_X:] if j < 3 else x_ext[HALO_X:] for j in range(4)]


def _proj_fwd(x, modr, vecs, w_in):
    s = x.shape[0]
    tm = min(TM_PROJ, s)

    def body(x_ref, mod_ref, vec_ref, w_ref, h1_ref, xrnn_ref, u_ref, ga_ref, dga_ref, sa_ref, sb_ref):
        xv = x_ref[...]
        r = lax.rsqrt(jnp.mean(xv * xv, axis=-1, keepdims=True) + EPS)
        gain = vec_ref[V_G1:V_G1 + 1, :] * (1.0 + mod_ref[M_SC1:M_SC1 + 1, :])
        h = (xv * r * gain + mod_ref[M_SH1:M_SH1 + 1, :]).astype(BF16)
        h1_ref[...] = h
        xrnn_ref[...] = _dot(h, w_ref[:, 0:D])
        ga_ref[...], dga_ref[...] = _gelu_and_grad(_dot(h, w_ref[:, D:2 * D]))
        u_ref[...] = _dot(h, w_ref[:, 2 * D:3 * D])
        sa_ref[...] = _sigmoid(_dot(h, w_ref[:, 3 * D:4 * D]))
        sb_ref[...] = _sigmoid(_dot(h, w_ref[:, 4 * D:5 * D]))

    tok = pl.BlockSpec((tm, D), lambda i: (i, 0))
    sd = lambda dt: jax.ShapeDtypeStruct((s, D), dt)
    return pl.pallas_call(
        body, name="proj_fwd", grid=(s // tm,),
        in_specs=[tok, pl.BlockSpec((8, D), lambda i: (0, 0)), pl.BlockSpec((16, D), lambda i: (0, 0)),
                  _resident((D, D_IN))],
        out_specs=[tok] * 7,
        out_shape=[sd(BF16)] + [sd(F32)] * 6,
        compiler_params=_params(("parallel",)),
    )(x, modr, vecs, w_in)


def _mix_fwd(x_rnn, u_pool, ga, vecs, w_rg_a, w_rg_x, w_pool, dep):
    s = x_rnn.shape[0]
    tm = min(TM_MIX, s)
    nb = s // tm

    def body(xh_ref, x_ref, uh_ref, u_ref, ga_ref, vec_ref, wa_ref, wx_ref, wp_ref, dep_ref,
             xr_ref, hr_ref, za_ref, p_ref, pooled_ref, a_ref, mult_ref, ra_ref, ri_ref, carry_ref, scan_scr):
        i = pl.program_id(0)
        first = i == 0

        @pl.when(first)
        def _():
            carry_ref[...] = jnp.zeros_like(carry_ref)

        row = lax.broadcasted_iota(jnp.int32, (tm, GW), 0)
        is_t0 = jnp.logical_and(first, row == 0)
        head_t = (lax.broadcasted_iota(jnp.int32, (HALO_U, GW), 0) + 1).astype(F32)
        for g in range(N_GROUPS):
            cs = slice(g * GW, (g + 1) * GW)
            vec = vec_ref[:, cs]
            xh = jnp.where(first, 0.0, xh_ref[:, cs])
            taps = _conv_taps(jnp.concatenate([xh, x_ref[:, cs]], axis=0))
            xr = vec[V_CONV_B:V_CONV_B + 1]
            for j in range(4):
                xr = xr + vec[V_CONV_W + j:V_CONV_W + j + 1] * taps[j]
            xr_ref[:, cs] = xr
            ra, ri, _, a, mult = _rglru_gates(
                xr, wa_ref[g], wx_ref[g], vec[V_B_RG_A:V_B_RG_A + 1], vec[V_B_RG_X:V_B_RG_X + 1],
                vec[V_A_PARAM:V_A_PARAM + 1], is_t0)
            a_ref[:, cs] = a
            mult_ref[:, cs] = mult
            ra_ref[:, cs] = ra.astype(BF16)
            ri_ref[:, cs] = ri.astype(BF16)
            h, last = _scan_down(a, xr * ri * mult, carry_ref[0:1, cs], scan_scr)
            hr_ref[:, cs] = h
            carry_ref[0:1, cs] = last
            za_ref[:, cs] = (ga_ref[:, cs] * h).astype(BF16)
            uh = jnp.where(first, 0.0, uh_ref[:, cs])
            sm = jnp.concatenate([uh, u_ref[:, cs]], axis=0)
            k = 1
            while k < POOL_WINDOWS[g]:
                sm = sm + _shift_down(sm, k)
                k *= 2
            mean = _window_mean(sm[HALO_U:], POOL_WINDOWS[g], first, head_t)
            p = (mean - u_ref[:, cs]).astype(BF16)
            p_ref[:, cs] = p
            pb = _dot(p, wp_ref[g]) + vec[V_B_POOL:V_B_POOL + 1]
            pooled_ref[:, cs] = (pb * vec[V_POOL_SCALE:V_POOL_SCALE + 1]).astype(BF16)

    tok = pl.BlockSpec((tm, D), lambda i: (i, 0))
    halo = lambda rows: pl.BlockSpec((rows, D), lambda i: (jnp.maximum(i * (tm // rows) - 1, 0), 0))
    wspec = pl.BlockSpec((N_GROUPS, GW, GW), lambda i: (0, 0, 0))
    sd = lambda dt: jax.ShapeDtypeStruct((s, D), dt)
    return pl.pallas_call(
        body, name="mix_fwd", grid=(nb,),
        in_specs=[halo(HALO_X), tok, halo(HALO_U), tok, tok, pl.BlockSpec((16, D), lambda i: (0, 0)),
                  wspec, wspec, wspec, pl.BlockSpec(memory_space=pl.ANY)],
        out_specs=[tok] * 9,
        out_shape=[sd(F32), sd(F32), sd(BF16), sd(BF16), sd(BF16), sd(F32), sd(F32), sd(BF16), sd(BF16)],
        scratch_shapes=[pltpu.VMEM((8, D), F32), pltpu.VMEM((3, tm, LANES), F32)],
        compiler_params=_params(("arbitrary",)),
    )(x_rnn, x_rnn, u_pool, u_pool, ga, vecs, w_rg_a, w_rg_x, w_pool, dep)


def _branch_fwd(za, pooled, sa, sb, x, modr, vecs, w_a, w_b, w_out):
    s = x.shape[0]
    tm = min(TM_BRANCH, s)

    def body(za_ref, pooled_ref, sa_ref, sb_ref, x_ref, mod_ref, vec_ref, wa_ref, wb_ref, wo_ref,
             ba_ref, bb_ref, merged_ref, o_ref, x2_ref, h2_ref):
        ba = _dot(za_ref[...], wa_ref[...])
        bb = _dot(pooled_ref[...], wb_ref[...])
        ba_ref[...] = ba.astype(BF16)
        bb_ref[...] = bb.astype(BF16)
        merged = (sa_ref[...] * ba + sb_ref[...] * bb).astype(BF16)
        merged_ref[...] = merged
        o = _dot(merged, wo_ref[...])
        o_ref[...] = o.astype(BF16)
        x2 = x_ref[...] + mod_ref[M_GT1:M_GT1 + 1, :] * o
        x2_ref[...] = x2
        r = lax.rsqrt(jnp.mean(x2 * x2, axis=-1, keepdims=True) + EPS)
        gain = vec_ref[V_G2:V_G2 + 1, :] * (1.0 + mod_ref[M_SC2:M_SC2 + 1, :])
        h2_ref[...] = (x2 * r * gain + mod_ref[M_SH2:M_SH2 + 1, :]).astype(BF16)

    tok = pl.BlockSpec((tm, D), lambda i: (i, 0))
    wspec = pl.BlockSpec((D, D), lambda i: (0, 0))
    sd = lambda dt: jax.ShapeDtypeStruct((s, D), dt)
    return pl.pallas_call(
        body, name="branch_fwd", grid=(s // tm,),
        in_specs=[tok, tok, tok, tok,
                  tok, pl.BlockSpec((8, D), lambda i: (0, 0)), pl.BlockSpec((16, D), lambda i: (0, 0)),
                  wspec, wspec, wspec],
        out_specs=[tok] * 6,
        out_shape=[sd(BF16), sd(BF16), sd(BF16), sd(BF16), sd(F32), sd(BF16)],
        compiler_params=_params(("parallel",)),
    )(za, pooled, sa, sb, x, modr, vecs, w_a, w_b, w_out)


def _mlp_fwd(h2, x2, target, modr, vecs, w_up, w_down):
    s = x2.shape[0]
    tm = min(TM_MLP, s)

    def body(h2_ref, x2_ref, tgt_ref, mod_ref, vec_ref, wu_ref, wd_ref,
             ru_ref, dx3_ref, ddn_ref, small_ref):
        @pl.when(pl.program_id(0) == 0)
        def _():
            small_ref[...] = jnp.zeros_like(small_ref)

        h2 = h2_ref[...]
        dn = None
        for c in range(D_FF // D):
            cs = slice(c * D, (c + 1) * D)
            ru = jnp.maximum(_dot(h2, wu_ref[:, cs]), 0.0)
            ru_ref[:, cs] = ru.astype(BF16)
            part = _dot((ru * ru).astype(BF16), wd_ref[cs, :])
            dn = part if dn is None else dn + part
        gt2 = mod_ref[M_GT2:M_GT2 + 1, :]
        gf = vec_ref[V_GF:V_GF + 1, :]
        x3 = x2_ref[...] + gt2 * dn
        r3 = lax.rsqrt(jnp.mean(x3 * x3, axis=-1, keepdims=True) + EPS)
        n3 = x3 * r3
        err = n3 * gf - tgt_ref[...]
        dy = err * (1.0 / D)
        dn3 = dy * gf
        dx3 = r3 * (dn3 - n3 * jnp.mean(dn3 * n3, axis=-1, keepdims=True))
        dx3_ref[...] = dx3
        ddn_ref[...] = (dx3 * gt2).astype(BF16)
        small_ref[0:1, :] += jnp.sum(dy * n3, axis=0, keepdims=True)
        small_ref[1:2, :] += jnp.sum(dx3 * dn, axis=0, keepdims=True)
        small_ref[2:3, :] += (0.5 / D) * jnp.sum(err * err, axis=0, keepdims=True)

    tok = pl.BlockSpec((tm, D), lambda i: (i, 0))
    return pl.pallas_call(
        body, name="mlp_fwd", grid=(s // tm,),
        in_specs=[tok, tok, tok,
                  pl.BlockSpec((8, D), lambda i: (0, 0)), pl.BlockSpec((16, D), lambda i: (0, 0)),
                  _resident((D, D_FF)), _resident((D_FF, D))],
        out_specs=[pl.BlockSpec((tm, D_FF), lambda i: (i, 0)), tok, tok,
                   pl.BlockSpec((8, D), lambda i: (0, 0))],
        out_shape=[jax.ShapeDtypeStruct((s, D_FF), BF16), jax.ShapeDtypeStruct((s, D), F32),
                   jax.ShapeDtypeStruct((s, D), BF16), jax.ShapeDtypeStruct((8, D), F32)],
        compiler_params=_params(("arbitrary",)),
    )(h2, x2, target, modr, vecs, w_up, w_down)


def _mlp_bwd(d_dn, ru, x2, dx3, o, modr, vecs, w_up, w_down):
    s = x2.shape[0]
    tm = min(TM_MLP_BWD, s)

    def body(ddn_ref, ru_ref, x2_ref, dx3_ref, o_ref, mod_ref, vec_ref, wu_ref, wd_ref,
             dup_ref, dx2_ref, do_ref, small_ref):
        @pl.when(pl.program_id(0) == 0)
        def _():
            small_ref[...] = jnp.zeros_like(small_ref)

        ddn = ddn_ref[...]
        dh2 = None
        for c in range(D_FF // D):
            cs = slice(c * D, (c + 1) * D)
            dff = _dot_nt(ddn, wd_ref[cs, :])
            dup = (dff * (2.0 * ru_ref[:, cs].astype(F32))).astype(BF16)
            dup_ref[:, cs] = dup
            part = _dot_nt(dup, wu_ref[:, cs])
            dh2 = part if dh2 is None else dh2 + part
        x2 = x2_ref[...]
        r2 = lax.rsqrt(jnp.mean(x2 * x2, axis=-1, keepdims=True) + EPS)
        xn2 = x2 * r2
        gain = vec_ref[V_G2:V_G2 + 1, :] * (1.0 + mod_ref[M_SC2:M_SC2 + 1, :])
        dxn2 = dh2 * gain
        dx2 = dx3_ref[...] + r2 * (dxn2 - xn2 * jnp.mean(dxn2 * xn2, axis=-1, keepdims=True))
        dx2_ref[...] = dx2
        do_ref[...] = (dx2 * mod_ref[M_GT1:M_GT1 + 1, :]).astype(BF16)
        small_ref[0:1, :] += jnp.sum(dh2, axis=0, keepdims=True)
        small_ref[1:2, :] += jnp.sum(dh2 * xn2, axis=0, keepdims=True)
        small_ref[2:3, :] += jnp.sum(dx2 * o_ref[...].astype(F32), axis=0, keepdims=True)

    tok = pl.BlockSpec((tm, D), lambda i: (i, 0))
    wide = pl.BlockSpec((tm, D_FF), lambda i: (i, 0))
    return pl.pallas_call(
        body, name="mlp_bwd", grid=(s // tm,),
        in_specs=[tok, wide, tok, tok, tok,
                  pl.BlockSpec((8, D), lambda i: (0, 0)), pl.BlockSpec((16, D), lambda i: (0, 0)),
                  _resident((D, D_FF)), _resident((D_FF, D))],
        out_specs=[wide, tok, tok, pl.BlockSpec((8, D), lambda i: (0, 0))],
        out_shape=[jax.ShapeDtypeStruct((s, D_FF), BF16), jax.ShapeDtypeStruct((s, D), F32),
                   jax.ShapeDtypeStruct((s, D), BF16), jax.ShapeDtypeStruct((8, D), F32)],
        compiler_params=_params(("arbitrary",)),
    )(d_dn, ru, x2, dx3, o, modr, vecs, w_up, w_down)


def _branch_bwd(do, sa, sb, ba, bb, w_a, w_b, w_out, dep):
    s = do.shape[0]
    tm = min(TM_BRANCH, s)

    def body(do_ref, sa_ref, sb_ref, ba_ref, bb_ref, wa_ref, wb_ref, wo_ref, dep_ref,
             dba_ref, dbb_ref, dg_ref, dza_ref, dpooled_ref):
        dmerged = _dot_nt(do_ref[...], wo_ref[...])
        sa = sa_ref[...]
        sb = sb_ref[...]
        dba = (dmerged * sa).astype(BF16)
        dbb = (dmerged * sb).astype(BF16)
        dba_ref[...] = dba
        dbb_ref[...] = dbb
        dg_ref[:, :D] = (dmerged * ba_ref[...].astype(F32) * sa * (1.0 - sa)).astype(BF16)
        dg_ref[:, D:] = (dmerged * bb_ref[...].astype(F32) * sb * (1.0 - sb)).astype(BF16)
        dza_ref[...] = _dot_nt(dba, wa_ref[...])
        dpooled_ref[...] = _dot_nt(dbb, wb_ref[...])

    tok = pl.BlockSpec((tm, D), lambda i: (i, 0))
    wspec = pl.BlockSpec((D, D), lambda i: (0, 0))
    sd = lambda dt: jax.ShapeDtypeStruct((s, D), dt)
    return pl.pallas_call(
        body, name="branch_bwd", grid=(s // tm,),
        in_specs=[tok, tok, tok, tok, tok, wspec, wspec, wspec, pl.BlockSpec(memory_space=pl.ANY)],
        out_specs=[tok, tok, pl.BlockSpec((tm, 2 * D), lambda i: (i, 0)), tok, tok],
        out_shape=[sd(BF16), sd(BF16), jax.ShapeDtypeStruct((s, 2 * D), BF16), sd(F32), sd(F32)],
        compiler_params=_params(("parallel",)),
    )(do, sa, sb, ba, bb, w_a, w_b, w_out, dep)


def _mix_bwd(dza, dpooled, x_rnn, ga, dga, xr, hr, p, gates, dgates, vecs, w_rg_a, w_rg_x, w_pool, dep):
    s = xr.shape[0]
    tm = min(TM_MIX, s)
    nb = s // tm

    def body(dza_ref, dpooled_ref, xh_ref, x_ref, ga_ref, dga_ref, xr_ref, hh_ref, hr_ref, p_ref,
             a_ref, mult_ref, ra_ref, ri_ref, dg_ref, vec_ref, wa_ref, wx_ref, wp_ref, dep_ref,
             dproj_ref, dwa_ref, dwx_ref, dwp_ref, small_ref,
             scan_carry, dxr_carry, q_carry, scan_scr, dwa_acc, dwx_acc, dwp_acc):
        i = pl.program_id(0)
        bi = nb - 1 - i
        first_t = bi == 0

        @pl.when(i == 0)
        def _():
            scan_carry[...] = jnp.zeros_like(scan_carry)
            dxr_carry[...] = jnp.zeros_like(dxr_carry)
            q_carry[...] = jnp.zeros_like(q_carry)
            dwa_acc[...] = jnp.zeros_like(dwa_acc)
            dwx_acc[...] = jnp.zeros_like(dwx_acc)
            dwp_acc[...] = jnp.zeros_like(dwp_acc)
            small_ref[...] = jnp.zeros_like(small_ref)

        row = lax.broadcasted_iota(jnp.int32, (tm, GW), 0)
        is_t0 = jnp.logical_and(first_t, row == 0)
        head_t = (lax.broadcasted_iota(jnp.int32, (HALO_U, GW), 0) + 1).astype(F32)
        colsum = lambda v: jnp.sum(v, axis=0, keepdims=True)
        for g in range(N_GROUPS):
            cs = slice(g * GW, (g + 1) * GW)
            vec = vec_ref[:, cs]
            xr = xr_ref[:, cs]
            hr = hr_ref[:, cs]
            dza = dza_ref[:, cs]
            dproj_ref[:, D + g * GW:D + (g + 1) * GW] = (dza * hr * dga_ref[:, cs]).astype(BF16)
            dhr = dza * ga_ref[:, cs]
            a = a_ref[:, cs]
            mult = mult_ref[:, cs]
            ra = ra_ref[:, cs].astype(F32)
            ri = ri_ref[:, cs].astype(F32)
            sp = _softplus(vec[V_A_PARAM:V_A_PARAM + 1])
            m = jnp.where(row == tm - 1, 1.0, _shift_up(a, 1))
            gsum = _scan_up(m, dhr, scan_carry[0:1, cs], scan_scr)
            scan_carry[0:1, cs] = a[0:1, :] * gsum[0:1, :]
            hh = jnp.where(first_t, 0.0, hh_ref[:, cs])
            hprev = _shift_down(jnp.concatenate([hh, hr], axis=0), 1)[8:]
            da = gsum * hprev
            dmult = jnp.where(is_t0, 0.0, gsum * xr * ri)
            dlog_a = da * a - dmult * a * a / mult
            dri = gsum * xr * mult
            dxr = gsum * ri * mult
            small_ref[7:8, cs] += colsum((-C_RG) * ra * dlog_a)
            dpa = (((-C_RG) * sp) * dlog_a * ra * (1.0 - ra))
            dpx = dri * ri * (1.0 - ri)
            small_ref[5:6, cs] += colsum(dpa)
            small_ref[6:7, cs] += colsum(dpx)
            dpa = dpa.astype(BF16)
            dpx = dpx.astype(BF16)
            xrb = xr.astype(BF16)
            dwa_acc[g] += _dot_tn(xrb, dpa)
            dwx_acc[g] += _dot_tn(xrb, dpx)
            dxr = dxr + _dot_nt(dpa, wa_ref[g]) + _dot_nt(dpx, wx_ref[g])
            small_ref[4:5, cs] += colsum(dxr)
            xh = jnp.where(first_t, 0.0, xh_ref[:, cs])
            taps = _conv_taps(jnp.concatenate([xh, x_ref[:, cs]], axis=0))
            dxr_ext = jnp.concatenate([dxr, dxr_carry[:, cs]], axis=0)
            dx = vec[V_CONV_W + 3:V_CONV_W + 4] * dxr
            for j in range(4):
                small_ref[j:j + 1, cs] += colsum(dxr * taps[j])
                if j < 3:
                    dx = dx + vec[V_CONV_W + j:V_CONV_W + j + 1] * _shift_up(dxr_ext, 3 - j)[:tm]
            dxr_carry[:, cs] = dxr[0:8, :]
            dproj_ref[:, cs] = dx.astype(BF16)
            pg = p_ref[:, cs]
            dpooled = dpooled_ref[:, cs]
            pb = _dot(pg, wp_ref[g]) + vec[V_B_POOL:V_B_POOL + 1]
            small_ref[9:10, cs] += colsum(dpooled * pb)
            dpb = dpooled * vec[V_POOL_SCALE:V_POOL_SCALE + 1]
            small_ref[8:9, cs] += colsum(dpb)
            dpbb = dpb.astype(BF16)
            dwp_acc[g] += _dot_tn(pg, dpbb)
            dp = _dot_nt(dpbb, wp_ref[g])
            q = _window_mean(dp, POOL_WINDOWS[g], first_t, head_t)
            sm = jnp.concatenate([q, q_carry[:, cs]], axis=0)
            k = 1
            while k < POOL_WINDOWS[g]:
                sm = sm + _shift_up(sm, k)
                k *= 2
            q_carry[:, cs] = q[0:HALO_U, :]
            dproj_ref[:, 2 * D + g * GW:2 * D + (g + 1) * GW] = (sm[:tm] - dp).astype(BF16)
        dproj_ref[:, 3 * D:] = dg_ref[...]

        @pl.when(i == nb - 1)
        def _():
            dwa_ref[...] = dwa_acc[...].astype(BF16)
            dwx_ref[...] = dwx_acc[...].astype(BF16)
            dwp_ref[...] = dwp_acc[...].astype(BF16)

    rev = lambda i: nb - 1 - i
    tok = pl.BlockSpec((tm, D), lambda i: (rev(i), 0))
    halo8 = lambda k: pl.BlockSpec((8, D), lambda i: (jnp.maximum(rev(i) * (tm // 8) - 1, 0), k))
    wspec = pl.BlockSpec((N_GROUPS, GW, GW), lambda i: (0, 0, 0))
    wshape = jax.ShapeDtypeStruct((N_GROUPS, GW, GW), BF16)
    return pl.pallas_call(
        body, name="mix_bwd", grid=(nb,),
        in_specs=[tok, tok, halo8(0), tok, tok, tok, tok, halo8(0), tok, tok, tok, tok, tok, tok,
                  pl.BlockSpec((tm, 2 * D), lambda i: (rev(i), 0)),
                  pl.BlockSpec((16, D), lambda i: (0, 0)), wspec, wspec, wspec, pl.BlockSpec(memory_space=pl.ANY)],
        out_specs=[pl.BlockSpec((tm, D_IN), lambda i: (rev(i), 0)), wspec, wspec, wspec,
                   pl.BlockSpec((16, D), lambda i: (0, 0))],
        out_shape=[jax.ShapeDtypeStruct((s, D_IN), BF16), wshape, wshape, wshape,
                   jax.ShapeDtypeStruct((16, D), F32)],
        scratch_shapes=[pltpu.VMEM((8, D), F32), pltpu.VMEM((8, D), F32), pltpu.VMEM((HALO_U, D), F32),
                        pltpu.VMEM((3, tm, LANES), F32)] + [pltpu.VMEM((N_GROUPS, GW, GW), F32)] * 3,
        compiler_params=_params(("arbitrary",)),
    )(dza, dpooled, x_rnn, x_rnn, ga, dga, xr, hr, hr, p, *gates, dgates, vecs, w_rg_a, w_rg_x, w_pool, dep)


def _proj_bwd(dproj, x, dx2, modr, vecs, w_in, dep):
    s = x.shape[0]
    tm = min(TM_PROJ, s)

    def body(dp_ref, x_ref, dx2_ref, mod_ref, vec_ref, w_ref, dep_ref, gx_ref, small_ref):
        @pl.when(pl.program_id(0) == 0)
        def _():
            small_ref[...] = jnp.zeros_like(small_ref)

        dh1 = None
        for c in range(D_IN // D):
            cs = slice(c * D, (c + 1) * D)
            part = _dot_nt(dp_ref[:, cs], w_ref[:, cs])
            dh1 = part if dh1 is None else dh1 + part
        xv = x_ref[...]
        r1 = lax.rsqrt(jnp.mean(xv * xv, axis=-1, keepdims=True) + EPS)
        xn1 = xv * r1
        gain = vec_ref[V_G1:V_G1 + 1, :] * (1.0 + mod_ref[M_SC1:M_SC1 + 1, :])
        dxn1 = dh1 * gain
        gx_ref[...] = dx2_ref[...] + r1 * (dxn1 - xn1 * jnp.mean(dxn1 * xn1, axis=-1, keepdims=True))
        small_ref[0:1, :] += jnp.sum(dh1, axis=0, keepdims=True)
        small_ref[1:2, :] += jnp.sum(dh1 * xn1, axis=0, keepdims=True)

    tok = pl.BlockSpec((tm, D), lambda i: (i, 0))
    return pl.pallas_call(
        body, name="proj_bwd", grid=(s // tm,),
        in_specs=[pl.BlockSpec((tm, D_IN), lambda i: (i, 0)), tok, tok,
                  pl.BlockSpec((8, D), lambda i: (0, 0)), pl.BlockSpec((16, D), lambda i: (0, 0)),
                  _resident((D, D_IN)), pl.BlockSpec(memory_space=pl.ANY)],
        out_specs=[tok, pl.BlockSpec((8, D), lambda i: (0, 0))],
        out_shape=[jax.ShapeDtypeStruct((s, D), F32), jax.ShapeDtypeStruct((8, D), F32)],
        compiler_params=_params(("arbitrary",)),
    )(dproj, x, dx2, modr, vecs, w_in, dep)


def _wgrad(a, b, name, square_a=False, dep=None):
    s, ka = a.shape
    n = b.shape[1]
    tka = ka if ka <= 1024 else ka // 2
    tn = n if n <= 1024 else n // 2
    ts = min(TS_WGRAD, s)
    ns = s // ts
    nc = 512
    deps = [] if dep is None else [dep]

    def body(a_ref, b_ref, *refs):
        out_ref, acc_ref = refs[-2:]
        t = pl.program_id(2)

        @pl.when(t == 0)
        def _():
            acc_ref[...] = jnp.zeros_like(acc_ref)

        av = a_ref[...]
        if square_a:
            af = av.astype(F32)
            av = (af * af).astype(BF16)
        for c in range(tn // nc):
            cs = slice(c * nc, (c + 1) * nc)
            acc_ref[:, cs] += _dot_tn(av, b_ref[:, cs])

        @pl.when(t == ns - 1)
        def _():
            out_ref[...] = acc_ref[...].astype(BF16)

    return pl.pallas_call(
        body, name=name, grid=(ka // tka, n // tn, ns),
        in_specs=[pl.BlockSpec((ts, tka), lambda i, j, t: (t, i)),
                  pl.BlockSpec((ts, tn), lambda i, j, t: (t, j))] + [pl.BlockSpec(memory_space=pl.ANY)] * len(deps),
        out_specs=pl.BlockSpec((tka, tn), lambda i, j, t: (i, j)),
        out_shape=jax.ShapeDtypeStruct((ka, n), BF16),
        scratch_shapes=[pltpu.VMEM((tka, tn), F32)],
        compiler_params=_params(("parallel", "parallel", "arbitrary")),
    )(a, b, *deps)


def _window(ref, kind, idx, size):
    start = pl.multiple_of(idx * size, size)
    if kind == 0:
        return ref.at[pl.ds(start, size)]
    if kind == 1:
        return ref.at[:, pl.ds(start, size)]
    return ref.at[:, :, pl.ds(start, size)]


def _mesh_place():
    x, y, c = lax.axis_index("x"), lax.axis_index("y"), lax.axis_index("c")
    return x, y, c, 4 * x + 2 * y + c


def _peer(x, y, c, q):
    px = 1 - x if q & 4 else x
    py = 1 - y if q & 2 else y
    pc = 1 - c if q & 1 else c
    return (px, py, pc), 4 * px + 2 * py + pc


_HBM = pl.BlockSpec(memory_space=pltpu.HBM)
_SEM = pl.BlockSpec(memory_space=pltpu.SEMAPHORE)
_EFFECT = pltpu.SideEffectType.DATAFLOW_SIDE_EFFECTING


N_NEAR = 4


def _near(x, y, c):
    out = [((x, y, 1 - c), 4 * x + 2 * y + 1 - c)]
    for j in (1, 2, 3):
        px = 1 - x if j & 2 else x
        py = 1 - y if j & 1 else y
        out.append(((px, py, c), 4 * px + 2 * py + c))
    return out


def _remote(src, dst, send_sems, recv_sems, slot, device):
    return pltpu.make_async_remote_copy(src_ref=src, dst_ref=dst, send_sem=send_sems.at[slot], recv_sem=recv_sems.at[slot],
                                        device_id=device, device_id_type=MESH)


def _split_call(name, arrays, sems_in, n_new_sems, after, emit):
    na, ns, nn = len(arrays), len(sems_in), len(n_new_sems)

    def body(*refs):
        emit(refs[:na], refs[na:na + ns], refs[na + ns + 1:na + ns + 1 + nn])
        refs[-1][...] = jnp.zeros_like(refs[-1])

    outs = pl.pallas_call(
        body, name=name,
        out_shape=(*[pltpu.SemaphoreType.DMA((m,)) for m in n_new_sems],
                   *[pltpu.HBM(a.shape, a.dtype) for a in arrays], jax.ShapeDtypeStruct((8, 128), F32)),
        in_specs=[_HBM] * na + [_SEM] * ns + [pl.BlockSpec(memory_space=pl.ANY)],
        out_specs=(*[_SEM] * nn, *[_HBM] * na, pl.BlockSpec(memory_space=pltpu.VMEM)),
        input_output_aliases={i: nn + i for i in range(na)},
        compiler_params=pltpu.CompilerParams(has_side_effects=_EFFECT),
    )(*[pltpu.with_memory_space_constraint(a, pltpu.HBM) for a in arrays], *sems_in, after)
    return list(outs[:nn]), list(outs[nn:nn + na]), outs[-1]


def _together(name, steps, after):
    parts = [(ex.arrays, ex.sems, ex.new_sems[step], getattr(ex, "emit_" + step)) for ex, step in steps]

    def emit(arr, old, new):
        ia = io = ib = 0
        for arrays, sems, new_sems, emit_one in parts:
            emit_one(arr[ia:ia + len(arrays)], old[io:io + len(sems)], new[ib:ib + len(new_sems)])
            ia, io, ib = ia + len(arrays), io + len(sems), ib + len(new_sems)

    new, arrays, token = _split_call(name, [a for p in parts for a in p[0]], [s for p in parts for s in p[1]],
                                     [m for p in parts for m in p[2]], after, emit)
    out = []
    ia = ib = 0
    for (ex, _), (arrs, sems, new_sems, _) in zip(steps, parts):
        ex.arrays, ex.sems, ex.token = arrays[ia:ia + len(arrs)], [*sems, *new[ib:ib + len(new_sems)]], token
        ia, ib = ia + len(arrs), ib + len(new_sems)
        out.append(ex.arrays[ex.n:])
    return out


class _AllGather:
    def __init__(self, shards, kinds, name):
        self.n, self.kinds, self.name = len(shards), kinds, name
        self.sizes = [s.shape[k] for s, k in zip(shards, kinds)]
        lands = []
        for s, k in zip(shards, kinds):
            dims = list(s.shape)
            dims[k] *= N_DEV
            lands.append(lax.empty(tuple(dims), s.dtype))
        self.arrays, self.sems = [*shards, *lands], []

    def window(self, arr, k, idx):
        return _window(arr[self.n + k], self.kinds[k], idx, self.sizes[k])

    def start(self, after):
        _together(self.name + "_start", [(self, "start")], after)

    def forward(self, after):
        _together(self.name + "_forward", [(self, "forward")], after)

    def finish(self, after):
        return _together(self.name + "_finish", [(self, "finish")], after)[0]


class _Gather(_AllGather):
    def __init__(self, shards, kinds, name):
        super().__init__(shards, kinds, name)
        n = self.n
        self.new_sems = dict(start=[n * N_NEAR, n * N_NEAR, n], forward=[n * N_NEAR] * 2, finish=[])

    def emit_start(self, arr, _, new):
        x, y, c, me = _mesh_place()
        for k in range(self.n):
            pltpu.make_async_copy(arr[k], self.window(arr, k, me), new[2].at[k]).start()
        for k in range(self.n):
            for j, (dev, _) in enumerate(_near(x, y, c)):
                _remote(arr[k], self.window(arr, k, me), new[0], new[1], k * N_NEAR + j, dev).start()

    def emit_forward(self, arr, old, new):
        x, y, c, _ = _mesh_place()
        near = _near(x, y, c)
        for k in range(self.n):
            for j in (1, 2, 3):
                dev, idx = near[j]
                landed = self.window(arr, k, idx)
                _remote(arr[k], landed, old[0], old[1], k * N_NEAR + j, dev).wait_recv()
                _remote(landed, landed, new[0], new[1], k * N_NEAR + j, near[0][0]).start()

    def emit_finish(self, arr, old, _):
        x, y, c, me = _mesh_place()
        near = _near(x, y, c)
        other_core = near[0][0]
        for k in range(self.n):
            win = lambda idx: self.window(arr, k, idx)
            pltpu.make_async_copy(arr[k], win(me), old[2].at[k]).wait()
            for j, (dev, idx) in enumerate(near):
                _remote(arr[k], win(me), old[0], old[1], k * N_NEAR + j, dev).wait_send()
            _remote(arr[k], win(near[0][1]), old[0], old[1], k * N_NEAR, other_core).wait_recv()
            for j in (1, 2, 3):
                idx = near[j][1]
                _remote(win(idx), win(idx), old[3], old[4], k * N_NEAR + j, other_core).wait_send()
                _remote(arr[k], win(idx + 1 - 2 * c), old[3], old[4], k * N_NEAR + j, other_core).wait_recv()


class _Spread(_AllGather):
    def __init__(self, shards, kinds, name):
        super().__init__(shards, kinds, name)
        n = self.n
        self.new_sems = dict(start=[n * N_DEV, n * N_DEV, n], finish=[])

    def emit_start(self, arr, _, new):
        x, y, c, me = _mesh_place()
        for k in range(self.n):
            mine = self.window(arr, k, me)
            pltpu.make_async_copy(arr[k], mine, new[2].at[k]).start()
            for q in range(1, N_DEV):
                _remote(arr[k], mine, new[0], new[1], k * N_DEV + q, _peer(x, y, c, q)[0]).start()

    def emit_finish(self, arr, old, _):
        x, y, c, me = _mesh_place()
        for k in range(self.n):
            win = lambda idx: self.window(arr, k, idx)
            pltpu.make_async_copy(arr[k], win(me), old[2].at[k]).wait()
            for q in range(1, N_DEV):
                peer, peer_idx = _peer(x, y, c, q)
                _remote(arr[k], win(me), old[0], old[1], k * N_DEV + q, peer).wait_send()
                _remote(arr[k], win(peer_idx), old[0], old[1], k * N_DEV + q, peer).wait_recv()


class _Scatter:
    def __init__(self, partials, kinds, after, name):
        self.n, self.kinds, self.name, self.partials = len(partials), kinds, name, partials
        self.sizes = [p.shape[k] // N_DEV for p, k in zip(partials, kinds)]
        n, sizes = self.n, self.sizes
        self.slot_shapes = []
        for p, k, size in zip(partials, kinds, sizes):
            dims = list(p.shape)
            dims[k] = size
            self.slot_shapes.append((N_NEAR, *dims))
        slots = [lax.empty(sh, p.dtype) for sh, p in zip(self.slot_shapes, partials)]

        def emit(arr, _, new):
            x, y, c, _ = _mesh_place()
            near = _near(x, y, c)
            for k in range(n):
                for j in range(N_NEAR):
                    owner = near[j][1] if j == 0 else near[j][1] + 1 - 2 * c
                    _remote(_window(arr[k], kinds[k], owner, sizes[k]), arr[n + k].at[j], new[0], new[1],
                            k * N_NEAR + j, near[0][0]).start()

        self.sems, self.arrays, self.token = _split_call(name + "_start", [*partials, *slots], [], [n * N_NEAR] * 2,
                                                         after, emit)

    def combine_and_send(self, own4, after):
        n, kinds, sizes = self.n, self.kinds, self.sizes

        def emit_wait(arr, old, _):
            x, y, c, _ = _mesh_place()
            near = _near(x, y, c)
            for k in range(n):
                for j in range(N_NEAR):
                    owner = near[j][1] if j == 0 else near[j][1] + 1 - 2 * c
                    cp = _remote(_window(arr[k], kinds[k], owner, sizes[k]), arr[n + k].at[j], old[0], old[1],
                                 k * N_NEAR + j, near[0][0])
                    cp.wait_send()
                    cp.wait_recv()

        _, arrays, _ = _split_call(self.name + "_landed", self.arrays, self.sems, [], after, emit_wait)
        chip_sums = _chip_sums(arrays[:n], arrays[n:], kinds, sizes, own4, self.name + "_combine")
        arrivals = [lax.empty((N_NEAR - 1, *sh[1:]), p.dtype) for sh, p in zip(self.slot_shapes, self.partials)]

        def emit_send(arr, _, new):
            x, y, c, _ = _mesh_place()
            near = _near(x, y, c)
            for k in range(n):
                for j in (1, 2, 3):
                    _remote(arr[k].at[j], arr[n + k].at[j - 1], new[0], new[1], k * N_NEAR + j, near[j][0]).start()

        self.sems, self.arrays, self.token = _split_call(self.name + "_send", [*chip_sums, *arrivals], [],
                                                         [n * N_NEAR] * 2, own4, emit_send)

    def finish(self, after):
        n = self.n

        def emit(arr, old, _):
            x, y, c, _ = _mesh_place()
            near = _near(x, y, c)
            for k in range(n):
                for j in (1, 2, 3):
                    cp = _remote(arr[k].at[j], arr[n + k].at[j - 1], old[0], old[1], k * N_NEAR + j, near[j][0])
                    cp.wait_send()
                    cp.wait_recv()

        _, arrays, _ = _split_call(self.name + "_finish", self.arrays, self.sems, [], after, emit)
        return arrays[:n], arrays[n:]


def _chip_sums(partials, slots, kinds, sizes, own4, name):
    n = len(partials)

    def body(own_ref, *refs):
        for k in range(n):
            refs[2 * n + k][...] = (refs[k][...].astype(F32) + refs[n + k][...].astype(F32)).astype(BF16)

    in_specs, slot_specs = [], []
    for p, s, kind, size in zip(partials, slots, kinds, sizes):
        block = list(p.shape)
        block[kind] = size
        nd = len(block)
        in_specs.append(pl.BlockSpec(tuple(block), functools.partial(
            lambda j, own, kind, nd: tuple(own[j] if d == kind else 0 for d in range(nd)), kind=kind, nd=nd)))
        slot_specs.append(pl.BlockSpec((None, *block), functools.partial(
            lambda j, own, nd: (j,) + (0,) * nd, nd=nd)))
    return pl.pallas_call(
        body, name=name,
        grid_spec=pltpu.PrefetchScalarGridSpec(num_scalar_prefetch=1, grid=(N_NEAR,),
                                               in_specs=in_specs + slot_specs, out_specs=slot_specs),
        out_shape=[jax.ShapeDtypeStruct(s.shape, s.dtype) for s in slots],
        compiler_params=_params(("arbitrary",)),
    )(own4, *partials, *slots)


def _to_bf16(arrays, name, dep=None):
    n = len(arrays)
    deps = [] if dep is None else [dep]

    def body(*refs):
        for src, dst in zip(refs[:n], refs[n + len(deps):]):
            dst[...] = src[...].astype(BF16)

    vmem = pl.BlockSpec(memory_space=pltpu.VMEM)
    return pl.pallas_call(body, name=name, out_shape=[jax.ShapeDtypeStruct(a.shape, BF16) for a in arrays],
                          in_specs=[vmem] * n + [pl.BlockSpec(memory_space=pl.ANY)] * len(deps), out_specs=[vmem] * n,
                          compiler_params=pltpu.CompilerParams(vmem_limit_bytes=V7X_VMEM_LIMIT))(*arrays, *deps)


def _silu(c):
    return c * _sigmoid_tail(c)


def _ada_fwd(c_all, w_ada, b_ada_cols, dep):
    def body(c_ref, w_ref, b_ref, dep_ref, out_ref):
        out_ref[...] = jnp.dot(_silu(c_ref[...]), w_ref[...], preferred_element_type=F32,
                               precision=lax.Precision.HIGHEST) + b_ref[...]

    vmem = pl.BlockSpec(memory_space=pltpu.VMEM)
    return pl.pallas_call(
        body, name="ada_fwd", in_specs=[vmem, vmem, vmem, pl.BlockSpec(memory_space=pl.ANY)], out_specs=vmem,
        out_shape=jax.ShapeDtypeStruct((N_DEV, w_ada.shape[1]), F32),
    )(c_all, w_ada, b_ada_cols, dep)


def _adam(w, g, m, v):
    m = ADAM_B1 * m + (1.0 - ADAM_B1) * g
    v = ADAM_B2 * v + (1.0 - ADAM_B2) * (g * g)
    m_hat = m / (1.0 - ADAM_B1 ** ADAM_STEP)
    v_hat = v / (1.0 - ADAM_B2 ** ADAM_STEP)
    delta = -ADAM_LR * (m_hat / (jnp.sqrt(v_hat) + ADAM_EPS) + ADAM_WD * w)
    return delta, m, v


def _ada_bwd_adam(c_all, dmod_cols, w, m, v):
    def body(c_ref, d_ref, w_ref, m_ref, v_ref, g_ref, delta_ref, nm_ref, nv_ref):
        g = lax.dot_general(_silu(c_ref[...]), d_ref[...], (((0,), (0,)), ((), ())),
                            preferred_element_type=F32, precision=lax.Precision.HIGHEST)
        g_ref[...] = g
        delta_ref[...], nm_ref[...], nv_ref[...] = _adam(w_ref[...], g, m_ref[...], v_ref[...])

    sd = jax.ShapeDtypeStruct(w.shape, F32)
    return pl.pallas_call(body, name="ada_bwd_adam", out_shape=[sd] * 4,
                          compiler_params=pltpu.CompilerParams(vmem_limit_bytes=V7X_VMEM_LIMIT),
                          )(c_all, dmod_cols, w, m, v)


def _adam_group(chip_sums, arrivals, ws, ms, vs, n_tiles, name):
    n = len(ws)

    def body(*refs):
        for k in range(n):
            c_ref, a_ref, w_ref, m_ref, v_ref = (refs[j * n + k] for j in range(5))
            g_ref, delta_ref, nm_ref, nv_ref = (refs[(5 + j) * n + k] for j in range(4))
            g = c_ref[...].astype(F32)
            for j in range(N_NEAR - 1):
                g = g + a_ref[j].astype(F32)
            g_ref[...] = g
            delta_ref[...], nm_ref[...], nv_ref[...] = _adam(w_ref[...], g, m_ref[...], v_ref[...])

    tiles = [(w.shape[0] // n_tiles, w.shape[1]) for w in ws]
    blk = [pl.BlockSpec(t, lambda i: (i, 0)) for t in tiles]
    return pl.pallas_call(
        body, name=name, grid=(n_tiles,),
        in_specs=[pl.BlockSpec((None, *t), lambda i: (0, i, 0)) for t in tiles]
        + [pl.BlockSpec((N_NEAR - 1, *t), lambda i: (0, i, 0)) for t in tiles] + blk * 3,
        out_specs=blk * 4, out_shape=[jax.ShapeDtypeStruct(w.shape, F32) for w in ws] * 4,
        compiler_params=_params(("parallel",)),
    )(*chip_sums, *arrivals, *ws, *ms, *vs)


N_SMALL = 40
N_SMALL_PARAMS = 11


def _pack_vecs(conv_w_full, rows):
    def body(cw_ref, *refs):
        out = refs[-1]
        out[...] = jnp.zeros_like(out)
        out[0:4, :] = cw_ref[0:4, :]
        for r, ref in enumerate(refs[:-1]):
            out[4 + r:5 + r, :] = ref[...]

    return pl.pallas_call(body, name="pack_vecs", out_shape=jax.ShapeDtypeStruct((16, D), F32))(conv_w_full, *rows)


def _small_finish(gathered, conv_cols, mod_all, vecs, ws, ms, vs):
    n = N_SMALL_PARAMS

    def body(g_ref, conv_ref, mod_ref, vec_ref, *refs):
        w_refs, m_refs, v_refs = refs[:n], refs[n:2 * n], refs[2 * n:3 * n]
        outs = refs[3 * n:]
        g1 = vec_ref[V_G1:V_G1 + 1, :]
        g2 = vec_ref[V_G2:V_G2 + 1, :]
        zero = jnp.zeros((1, D), F32)
        dg1, dg2, dgf, loss_lanes = zero, zero, zero, zero
        mixer = jnp.zeros((16, D), F32)
        db_ada = jnp.zeros((6, D), F32)
        d_conv_w = jnp.zeros(conv_ref.shape[1:], F32)
        for b in range(N_DEV):
            gb = g_ref[b]
            mod = mod_ref[b]
            q1 = gb[33:34]
            q2 = gb[9:10]
            dmod = jnp.concatenate([gb[32:33], q1 * g1, gb[10:11], gb[8:9], q2 * g2, gb[1:2]], axis=0)
            outs[4 * n][b] = dmod
            db_ada = db_ada + dmod
            dg1 = dg1 + q1 * (1.0 + mod[M_SC1:M_SC1 + 1])
            dg2 = dg2 + q2 * (1.0 + mod[M_SC2:M_SC2 + 1])
            dgf = dgf + gb[0:1]
            loss_lanes = loss_lanes + gb[2:3]
            mixer = mixer + gb[16:32]
            d_conv_w = d_conv_w + conv_ref[b]
        d_a_param = mixer[7:8] * _sigmoid_tail(vec_ref[V_A_PARAM:V_A_PARAM + 1, :])
        grads = [dg1, dg2, mixer[4:5], mixer[5:6], mixer[6:7], d_a_param, mixer[8:9], mixer[9:10], dgf,
                 db_ada, d_conv_w]

        def load(ref, rows):
            if ref.shape[0] == rows:
                return ref[...]
            return jnp.concatenate([ref[:, j * D:(j + 1) * D] for j in range(rows)], axis=0)

        def store(ref, val):
            if ref.shape == val.shape:
                ref[...] = val
            else:
                for j in range(val.shape[0]):
                    ref[:, j * D:(j + 1) * D] = val[j:j + 1]

        for k in range(n):
            rows = grads[k].shape[0]
            results = (grads[k], *_adam(load(w_refs[k], rows), grads[k], load(m_refs[k], rows), load(v_refs[k], rows)))
            for which, val in enumerate(results):
                store(outs[which * n + k], val)
        outs[4 * n + 1][...] = jnp.broadcast_to(jnp.sum(loss_lanes, axis=1, keepdims=True), (8, 128))

    shapes = [jax.ShapeDtypeStruct(w.shape, F32) for w in ws]
    return pl.pallas_call(
        body, name="small_finish",
        out_shape=shapes * 4 + [jax.ShapeDtypeStruct((N_DEV, 6, D), F32), jax.ShapeDtypeStruct((8, 128), F32)],
    )(gathered, conv_cols, mod_all, vecs, *ws, *ms, *vs)


def _pad_rows(a, rows):
    return jnp.pad(a, ((0, rows - a.shape[0]), (0, 0)))


def kernel(x, c, norm_mix_g, norm_mlp_g, w_ada, b_ada, w_in, conv_w, conv_b, w_rg_a, b_rg_a, w_rg_x, b_rg_x, a_param, w_branch_a, w_pool, b_pool, pool_scale, w_branch_b, w_out, w_up, w_down, final_g, loss_target, m_norm_mix_g, m_norm_mlp_g, m_w_ada, m_b_ada, m_w_in, m_conv_w, m_conv_b, m_w_rg_a, m_b_rg_a, m_w_rg_x, m_b_rg_x, m_a_param, m_w_branch_a, m_w_pool, m_b_pool, m_pool_scale, m_w_branch_b, m_w_out, m_w_up, m_w_down, m_final_g, v_norm_mix_g, v_norm_mlp_g, v_w_ada, v_b_ada, v_w_in, v_conv_w, v_conv_b, v_w_rg_a, v_b_rg_a, v_w_rg_x, v_b_rg_x, v_a_param, v_w_branch_a, v_w_pool, v_b_pool, v_pool_scale, v_w_branch_b, v_w_out, v_w_up, v_w_down, v_final_g):
    me = 4 * lax.axis_index("x") + 2 * lax.axis_index("y") + lax.axis_index("c")
    s = x.shape[1]
    x2d = x.reshape(s, D)
    target = loss_target.reshape(s, D)
    n_ada = w_ada.shape[2]

    b_ada_cols = lax.dynamic_slice(b_ada, (0, me * n_ada), (1, n_ada))

    sharded = dict(w_in=(w_in[0], 1), w_up=(w_up[0], 1), w_down=(w_down[0], 0), w_branch_a=(w_branch_a[0], 0),
                   w_branch_b=(w_branch_b[0], 0), w_out=(w_out[0], 0), w_rg_a=(w_rg_a[0], 1), w_rg_x=(w_rg_x[0], 1),
                   w_pool=(w_pool[0], 1))
    kind = {k: v[1] for k, v in sharded.items()}
    first_names = ["w_in"]
    later_names = [k for k in sharded if k not in first_names]
    mix_names = ["w_rg_a", "w_rg_x", "w_pool"]
    branch_names = ["w_branch_a", "w_branch_b", "w_out"]
    mlp_names = ["w_up", "w_down"]

    def gather(group, after, name):
        exchange = _Gather([shard[k] for k in group], [kind[k] for k in group], name)
        exchange.start(after)
        return exchange

    shard = dict(zip(first_names, _to_bf16([sharded[k][0] for k in first_names], "to_bf16_first")))
    spread_c = _Spread([c, conv_w[0]], [0, 1], "spread_c")
    g_first = _Gather([shard[k] for k in first_names], [kind[k] for k in first_names], "gather_first")
    _together("first_start", [(spread_c, "start"), (g_first, "start")], c)
    shard.update(zip(later_names, _to_bf16([sharded[k][0] for k in later_names], "to_bf16_later", dep=g_first.token)))

    c_all, conv_w_full = spread_c.finish(g_first.token)
    mod_part = _ada_fwd(c_all, w_ada[0], b_ada_cols, g_first.token)
    vecs = _pack_vecs(conv_w_full, [conv_b, b_rg_a, b_rg_x, a_param, b_pool, pool_scale,
                                    norm_mix_g, norm_mlp_g, final_g.reshape(1, D)])
    spread_mod = _Spread([mod_part], [0], "spread_mod")
    spread_mod.start(vecs)
    g_mix = gather(mix_names, spread_mod.token, "gather_mix")
    g_branch = _Gather([shard[k] for k in branch_names], [kind[k] for k in branch_names], "gather_branch")
    g_mlp = _Gather([shard[k] for k in mlp_names], [kind[k] for k in mlp_names], "gather_mlp")
    _together("first_forward", [(g_first, "forward"), (g_branch, "start"), (g_mlp, "start")], g_mix.token)
    wg = dict(zip(first_names, g_first.finish(g_first.token)))
    mod_parts, = spread_mod.finish(g_first.token)
    mod_all = jnp.transpose(mod_parts.reshape(N_DEV, N_DEV, n_ada), (1, 0, 2)).reshape(N_DEV, 6, D)
    modr = _pad_rows(lax.dynamic_index_in_dim(mod_all, me, 0, keepdims=False), 8)

    h1, x_rnn, u_pool, ga, dga, sa, sb = _proj_fwd(x2d, modr, vecs, wg["w_in"])
    _together("mixer_forward", [(g_mix, "forward"), (g_branch, "forward")], h1)
    wg.update(zip(mix_names, g_mix.finish(g_branch.token)))
    xr, hr, za, p, pooled, *gates = _mix_fwd(x_rnn, u_pool, ga, vecs, wg["w_rg_a"], wg["w_rg_x"], wg["w_pool"],
                                             dep=g_branch.token)
    g_mlp.forward(za)
    wg.update(zip(branch_names, g_branch.finish(g_mlp.token)))
    ba, bb, merged, o, x2, h2 = _branch_fwd(za, pooled, sa, sb, x2d, modr, vecs,
                                            wg["w_branch_a"], wg["w_branch_b"], wg["w_out"])
    wg.update(zip(mlp_names, g_mlp.finish(h2)))
    ru, dx3, d_dn, small_f = _mlp_fwd(h2, x2, target, modr, vecs, wg["w_up"], wg["w_down"])

    near = _near(lax.axis_index("x"), lax.axis_index("y"), lax.axis_index("c"))
    own4 = jnp.stack([me, near[1][1], near[2][1], near[3][1]]).astype(jnp.int32)

    def scatter(group, partial, after, name):
        return _Scatter([partial[k] for k in group], [kind[k] for k in group], after, name)

    dup, dx2, do, small_m = _mlp_bwd(d_dn, ru, x2, dx3, o, modr, vecs, wg["w_up"], wg["w_down"])
    partial = dict(w_up=_wgrad(h2, dup, "wgrad_up"), w_down=_wgrad(ru, d_dn, "wgrad_down", square_a=True))
    s_mlp = scatter(mlp_names, partial, dx2, "scatter_mlp")

    dba, dbb, dgates, dza, dpooled = _branch_bwd(do, sa, sb, ba, bb, wg["w_branch_a"], wg["w_branch_b"], wg["w_out"],
                                                 dep=s_mlp.token)
    s_mlp.combine_and_send(own4, dza)
    dproj, dw_rg_a, dw_rg_x, dw_pool, small_x = _mix_bwd(dza, dpooled, x_rnn, ga, dga, xr, hr, p, gates, dgates,
                                                         vecs, wg["w_rg_a"], wg["w_rg_x"], wg["w_pool"],
                                                         dep=s_mlp.token)
    partial.update(w_branch_a=_wgrad(za, dba, "wgrad_branch_a"), w_branch_b=_wgrad(pooled, dbb, "wgrad_branch_b"),
                   w_out=_wgrad(merged, do, "wgrad_out"),
                   w_rg_a=dw_rg_a, w_rg_x=dw_rg_x, w_pool=dw_pool)
    mixer_names = ["w_rg_a", "w_rg_x", "w_pool", "w_branch_a", "w_branch_b", "w_out"]
    s_mixer = scatter(mixer_names, partial, s_mlp.token, "scatter_mixer")

    partial["w_in"] = _wgrad(h1, dproj, "wgrad_in", dep=s_mixer.token)
    s_in = scatter(["w_in"], partial, s_mixer.token, "scatter_in")
    s_mixer.combine_and_send(own4, s_in.token)
    s_in.combine_and_send(own4, s_mixer.token)
    grad_x, small_p = _proj_bwd(dproj, x2d, dx2, modr, vecs, wg["w_in"], dep=s_in.token)

    locals_ = dict(w_in=(w_in, m_w_in, v_w_in), w_up=(w_up, m_w_up, v_w_up), w_down=(w_down, m_w_down, v_w_down),
                   w_branch_a=(w_branch_a, m_w_branch_a, v_w_branch_a),
                   w_branch_b=(w_branch_b, m_w_branch_b, v_w_branch_b), w_out=(w_out, m_w_out, v_w_out),
                   w_rg_a=(w_rg_a, m_w_rg_a, v_w_rg_a), w_rg_x=(w_rg_x, m_w_rg_x, v_w_rg_x),
                   w_pool=(w_pool, m_w_pool, v_w_pool))
    res = {}

    def finish(group, exchange, after, n_tiles, name):
        chip_sums, arrivals = exchange.finish(after)
        flat = lambda t: t.reshape(-1, t.shape[-1])
        shapes = [flat(locals_[k][0]).shape for k in group]
        outs = _adam_group([cs.reshape(N_NEAR, *sh) for cs, sh in zip(chip_sums, shapes)],
                           [ar.reshape(N_NEAR - 1, *sh) for ar, sh in zip(arrivals, shapes)],
                           *[[flat(locals_[k][j]) for k in group] for j in range(3)], n_tiles, name)
        for i, k in enumerate(group):
            res[k] = [outs[j * len(group) + i].reshape(locals_[k][0].shape) for j in range(4)]
        return res[group[-1]][0]

    small = jnp.concatenate([small_f, small_m, small_x, small_p], axis=0)
    g_small = _Spread([small], [0], "spread_small")
    g_small.start(grad_x)
    done = finish(mlp_names, s_mlp, g_small.token, 4, "adam_mlp")
    done = finish(mixer_names, s_mixer, done, 2, "adam_mixer")
    done = finish(["w_in"], s_in, done, 4, "adam_in")
    small_all, = g_small.finish(done)
    small_all = small_all.reshape(N_DEV, N_SMALL, D)

    conv_cols = lax.dynamic_slice(small_all, (0, 16, me * (D // N_DEV)), (N_DEV, 4, D // N_DEV))

    def smalls(ng, nl, cb, bra, brx, ap, bp, ps, fg, ba_, cw):
        return [ng, nl, cb, bra, brx, ap, bp, ps, fg.reshape(1, D), ba_, cw[0]]

    small_names = ["norm_mix_g", "norm_mlp_g", "conv_b", "b_rg_a", "b_rg_x", "a_param", "b_pool", "pool_scale",
                   "final_g", "b_ada", "conv_w"]
    fin = _small_finish(
        small_all, conv_cols, mod_all, vecs,
        smalls(norm_mix_g, norm_mlp_g, conv_b, b_rg_a, b_rg_x, a_param, b_pool, pool_scale, final_g, b_ada, conv_w),
        smalls(m_norm_mix_g, m_norm_mlp_g, m_conv_b, m_b_rg_a, m_b_rg_x, m_a_param, m_b_pool, m_pool_scale,
               m_final_g, m_b_ada, m_conv_w),
        smalls(v_norm_mix_g, v_norm_mlp_g, v_conv_b, v_b_rg_a, v_b_rg_x, v_a_param, v_b_pool, v_pool_scale,
               v_final_g, v_b_ada, v_conv_w))
    dmod_all, loss_tile = fin[4 * N_SMALL_PARAMS], fin[4 * N_SMALL_PARAMS + 1]
    dmod_cols = lax.dynamic_slice(dmod_all.reshape(N_DEV, 6 * D), (0, me * n_ada), (N_DEV, n_ada))
    res["w_ada"] = [t.reshape(w_ada.shape) for t in _ada_bwd_adam(c_all, dmod_cols, w_ada[0], m_w_ada[0], v_w_ada[0])]

    def final_shape(k, t):
        if k == "final_g":
            return t.reshape(D)
        if k == "conv_w":
            return t.reshape(conv_w.shape)
        return t

    for i, k in enumerate(small_names):
        res[k] = [final_shape(k, fin[which * N_SMALL_PARAMS + i]) for which in range(4)]
    order = ["norm_mix_g", "norm_mlp_g", "w_ada", "b_ada", "w_in", "conv_w", "conv_b", "w_rg_a", "b_rg_a", "w_rg_x",
             "b_rg_x", "a_param", "w_branch_a", "w_pool", "b_pool", "pool_scale", "w_branch_b", "w_out", "w_up",
             "w_down", "final_g"]
    outs = [loss_tile[0, 0], grad_x.reshape(x.shape)]
    for which in range(4):
        for k in order:
            outs.append(res[k][which])
    return tuple(outs)
```

```python
import functools

import jax
import jax.numpy as jnp
from jax import lax
from jax.experimental import pallas as pl
from jax.experimental.pallas import tpu as pltpu

F32 = jnp.float32
BF16 = jnp.bfloat16
MESH = pl.DeviceIdType.MESH

N_DEV = 8
D = 1024
N_GROUPS = 4
GW = D // N_GROUPS
D_IN = 5 * D
D_FF = 4 * D
POOL_WINDOWS = (2, 4, 8, 16)
HALO_X = 8
HALO_U = 16
EPS = 1e-6
C_RG = 8.0
ADAM_LR, ADAM_B1, ADAM_B2, ADAM_EPS, ADAM_WD, ADAM_STEP = 0.001, 0.9, 0.999, 1e-08, 0.01, 10

V7X_VMEM_LIMIT = 56 * 1024 * 1024

V_CONV_W, V_CONV_B, V_B_RG_A, V_B_RG_X, V_A_PARAM, V_B_POOL, V_POOL_SCALE, V_G1, V_G2, V_GF = 0, 4, 5, 6, 7, 8, 9, 10, 11, 12
M_SH1, M_SC1, M_GT1, M_SH2, M_SC2, M_GT2 = 0, 1, 2, 3, 4, 5

TM_PROJ = 512
TM_MIX = 256
TM_BRANCH = 512
TM_MLP = 512
TM_MLP_BWD = 256
TS_WGRAD = 1024
TS_WGRAD_SQUARE = 512


def _params(semantics):
    return pltpu.CompilerParams(dimension_semantics=semantics, vmem_limit_bytes=V7X_VMEM_LIMIT)


def _resident(shape):
    return pl.BlockSpec(shape, lambda *_: (0,) * len(shape), pipeline_mode=pl.Buffered(1))


def _dot(a, b):
    return jnp.dot(a, b, preferred_element_type=F32)


def _dot_nt(a, b):
    return lax.dot_general(a, b, (((1,), (1,)), ((), ())), preferred_element_type=F32)


def _dot_tn(a, b):
    return lax.dot_general(a, b, (((0,), (0,)), ((), ())), preferred_element_type=F32)


def _sigmoid(x):
    return 0.5 * jnp.tanh(0.5 * x) + 0.5


def _sigmoid_tail(x):
    return 1.0 / (1.0 + jnp.exp(-x))


def _gelu_and_grad(x):
    k = 0.7978845608028654
    x2 = x * x
    t = jnp.tanh(k * (x + 0.044715 * x * x2))
    g = 0.5 * x * (1.0 + t)
    dg = 0.5 * (1.0 + t) + 0.5 * x * (1.0 - t * t) * (k * (1.0 + 3.0 * 0.044715 * x2))
    return g, dg


def _softplus(a):
    e = jnp.exp(-jnp.abs(a))
    u = 1.0 + e
    log1p_e = jnp.where(u == 1.0, e, jnp.log(u) * e / jnp.where(u == 1.0, 1.0, u - 1.0))
    return jnp.maximum(a, 0.0) + log1p_e


def _neg_expm1(z):
    series = -(z * (1.0 + z * (0.5 + z * (1.0 / 6.0 + z * (1.0 / 24.0 + z * (1.0 / 120.0))))))
    return jnp.where(z > -0.1, series, 1.0 - jnp.exp(z))


def _shift_down(x, k):
    return pltpu.roll(x, k, 0)


def _shift_up(x, k):
    return pltpu.roll(x, x.shape[0] - k, 0)


def _rglru_gates(xr, w_a, w_x, b_a, b_x, a_param, is_t0):
    xb = xr.astype(BF16)
    ra = _sigmoid(_dot(xb, w_a) + b_a)
    ri = _sigmoid(_dot(xb, w_x) + b_x)
    sp = _softplus(a_param)
    log_a = (-C_RG) * ra * sp
    a = jnp.exp(log_a)
    mult = jnp.where(is_t0, 1.0, jnp.sqrt(_neg_expm1(2.0 * log_a)))
    return ra, ri, sp, a, mult


SUBLANES = 8


LANES = 128


def _scan_strip(a, b, carry, scr, down):
    t = b.shape[0]
    g = t // SUBLANES
    a3 = a.reshape(g, SUBLANES, LANES)
    b3 = b.reshape(g, SUBLANES, LANES)
    sub = lax.broadcasted_iota(jnp.int32, (g, SUBLANES, LANES), 1)
    for k in (1, 2, 4):
        keep = sub >= k if down else sub < SUBLANES - k
        shift = k if down else SUBLANES - k
        b3 = b3 + a3 * jnp.where(keep, pltpu.roll(b3, shift, 1), 0.0)
        a3 = a3 * jnp.where(keep, pltpu.roll(a3, shift, 1), 1.0)
    scr[0] = a3.reshape(t, LANES)
    scr[1] = b3.reshape(t, LANES)
    end_row = SUBLANES - 1 if down else 0
    ag = scr[0, pl.ds(end_row, g, stride=SUBLANES), :]
    bg = scr[1, pl.ds(end_row, g, stride=SUBLANES), :]
    rg = lax.broadcasted_iota(jnp.int32, (g, LANES), 0)
    edge = 0 if down else g - 1
    bg = bg + jnp.where(rg == edge, ag * carry, 0.0)
    k = 1
    while k < g:
        keep = rg >= k if down else rg < g - k
        shift = k if down else g - k
        bg = bg + ag * jnp.where(keep, pltpu.roll(bg, shift, 0), 0.0)
        if 2 * k < g:
            ag = ag * pltpu.roll(ag, shift, 0)
        k *= 2
    entering = jnp.where(rg != edge, pltpu.roll(bg, 1 if down else g - 1, 0), carry)
    for r in range(SUBLANES):
        scr[2, pl.ds(r, g, stride=SUBLANES), :] = entering
    return scr[1] + scr[0] * scr[2], bg[g - 1:g, :]


def _scan_strips(a, b, carry, scr, down):
    outs = [_scan_strip(a[:, c:c + LANES], b[:, c:c + LANES], carry[:, c:c + LANES], scr, down)
            for c in range(0, b.shape[1], LANES)]
    return jnp.concatenate([o[0] for o in outs], axis=1), jnp.concatenate([o[1] for o in outs], axis=1)


def _scan_down(a, b, carry, scr):
    return _scan_strips(a, b, carry, scr, True)


def _scan_up(m, b, carry, scr):
    return _scan_strips(m, b, carry, scr, False)[0]


def _window_mean(sums, window, first_block, head_t):
    scaled = sums * (1.0 / window)
    head = jnp.where(first_block, sums[:HALO_U] / jnp.minimum(head_t, float(window)), scaled[:HALO_U])
    return jnp.concatenate([head, scaled[HALO_U:]], axis=0)


def _conv_taps(x_ext):
    return [_shift_down(x_ext, 3 - j)[HALO_X:] if j < 3 else x_ext[HALO_X:] for j in range(4)]


def _proj_fwd(x, modr, vecs, w_in):
    s = x.shape[0]
    tm = min(TM_PROJ, s)

    def body(x_ref, mod_ref, vec_ref, w_ref, h1_ref, xrnn_ref, u_ref, ga_ref, dga_ref, sa_ref, sb_ref):
        xv = x_ref[...]
        r = lax.rsqrt(jnp.mean(xv * xv, axis=-1, keepdims=True) + EPS)
        gain = vec_ref[V_G1:V_G1 + 1, :] * (1.0 + mod_ref[M_SC1:M_SC1 + 1, :])
        h = (xv * r * gain + mod_ref[M_SH1:M_SH1 + 1, :]).astype(BF16)
        h1_ref[...] = h
        xrnn_ref[...] = _dot(h, w_ref[:, 0:D])
        ga_ref[...], dga_ref[...] = _gelu_and_grad(_dot(h, w_ref[:, D:2 * D]))
        u_ref[...] = _dot(h, w_ref[:, 2 * D:3 * D])
        sa_ref[...] = _sigmoid(_dot(h, w_ref[:, 3 * D:4 * D]))
        sb_ref[...] = _sigmoid(_dot(h, w_ref[:, 4 * D:5 * D]))

    tok = pl.BlockSpec((tm, D), lambda i: (i, 0))
    sd = lambda dt: jax.ShapeDtypeStruct((s, D), dt)
    return pl.pallas_call(
        body, name="proj_fwd", grid=(s // tm,),
        in_specs=[tok, pl.BlockSpec((8, D), lambda i: (0, 0)), pl.BlockSpec((16, D), lambda i: (0, 0)),
                  _resident((D, D_IN))],
        out_specs=[tok] * 7,
        out_shape=[sd(BF16)] + [sd(F32)] * 6,
        compiler_params=_params(("parallel",)),
    )(x, modr, vecs, w_in)


def _mix_fwd(x_rnn, u_pool, ga, vecs, w_rg_a, w_rg_x, w_pool, dep):
    s = x_rnn.shape[0]
    tm = min(TM_MIX, s)
    nb = s // tm

    def body(xh_ref, x_ref, uh_ref, u_ref, ga_ref, vec_ref, wa_ref, wx_ref, wp_ref, dep_ref,
             xr_ref, hr_ref, za_ref, p_ref, pooled_ref, a_ref, mult_ref, ra_ref, ri_ref, carry_ref, scan_scr):
        i = pl.program_id(0)
        first = i == 0

        @pl.when(first)
        def _():
            carry_ref[...] = jnp.zeros_like(carry_ref)

        row = lax.broadcasted_iota(jnp.int32, (tm, GW), 0)
        is_t0 = jnp.logical_and(first, row == 0)
        head_t = (lax.broadcasted_iota(jnp.int32, (HALO_U, GW), 0) + 1).astype(F32)
        for g in range(N_GROUPS):
            cs = slice(g * GW, (g + 1) * GW)
            vec = vec_ref[:, cs]
            xh = jnp.where(first, 0.0, xh_ref[:, cs])
            taps = _conv_taps(jnp.concatenate([xh, x_ref[:, cs]], axis=0))
            xr = vec[V_CONV_B:V_CONV_B + 1]
            for j in range(4):
                xr = xr + vec[V_CONV_W + j:V_CONV_W + j + 1] * taps[j]
            xr_ref[:, cs] = xr
            ra, ri, _, a, mult = _rglru_gates(
                xr, wa_ref[g], wx_ref[g], vec[V_B_RG_A:V_B_RG_A + 1], vec[V_B_RG_X:V_B_RG_X + 1],
                vec[V_A_PARAM:V_A_PARAM + 1], is_t0)
            a_ref[:, cs] = a
            mult_ref[:, cs] = mult
            ra_ref[:, cs] = ra.astype(BF16)
            ri_ref[:, cs] = ri.astype(BF16)
            h, last = _scan_down(a, xr * ri * mult, carry_ref[0:1, cs], scan_scr)
            hr_ref[:, cs] = h
            carry_ref[0:1, cs] = last
            za_ref[:, cs] = (ga_ref[:, cs] * h).astype(BF16)
            uh = jnp.where(first, 0.0, uh_ref[:, cs])
            sm = jnp.concatenate([uh, u_ref[:, cs]], axis=0)
            k = 1
            while k < POOL_WINDOWS[g]:
                sm = sm + _shift_down(sm, k)
                k *= 2
            mean = _window_mean(sm[HALO_U:], POOL_WINDOWS[g], first, head_t)
            p = (mean - u_ref[:, cs]).astype(BF16)
            p_ref[:, cs] = p
            pb = _dot(p, wp_ref[g]) + vec[V_B_POOL:V_B_POOL + 1]
            pooled_ref[:, cs] = (pb * vec[V_POOL_SCALE:V_POOL_SCALE + 1]).astype(BF16)

    tok = pl.BlockSpec((tm, D), lambda i: (i, 0))
    halo = lambda rows: pl.BlockSpec((rows, D), lambda i: (jnp.maximum(i * (tm // rows) - 1, 0), 0))
    wspec = pl.BlockSpec((N_GROUPS, GW, GW), lambda i: (0, 0, 0))
    sd = lambda dt: jax.ShapeDtypeStruct((s, D), dt)
    return pl.pallas_call(
        body, name="mix_fwd", grid=(nb,),
        in_specs=[halo(HALO_X), tok, halo(HALO_U), tok, tok, pl.BlockSpec((16, D), lambda i: (0, 0)),
                  wspec, wspec, wspec, pl.BlockSpec(memory_space=pl.ANY)],
        out_specs=[tok] * 9,
        out_shape=[sd(F32), sd(F32), sd(BF16), sd(BF16), sd(BF16), sd(F32), sd(F32), sd(BF16), sd(BF16)],
        scratch_shapes=[pltpu.VMEM((8, D), F32), pltpu.VMEM((3, tm, LANES), F32)],
        compiler_params=_params(("arbitrary",)),
    )(x_rnn, x_rnn, u_pool, u_pool, ga, vecs, w_rg_a, w_rg_x, w_pool, dep)


def _branch_fwd(za, pooled, sa, sb, x, modr, vecs, w_a, w_b, w_out):
    s = x.shape[0]
    tm = min(TM_BRANCH, s)

    def body(za_ref, pooled_ref, sa_ref, sb_ref, x_ref, mod_ref, vec_ref, wa_ref, wb_ref, wo_ref,
             ba_ref, bb_ref, merged_ref, o_ref, x2_ref, h2_ref):
        ba = _dot(za_ref[...], wa_ref[...])
        bb = _dot(pooled_ref[...], wb_ref[...])
        ba_ref[...] = ba.astype(BF16)
        bb_ref[...] = bb.astype(BF16)
        merged = (sa_ref[...] * ba + sb_ref[...] * bb).astype(BF16)
        merged_ref[...] = merged
        o = _dot(merged, wo_ref[...])
        o_ref[...] = o.astype(BF16)
        x2 = x_ref[...] + mod_ref[M_GT1:M_GT1 + 1, :] * o
        x2_ref[...] = x2
        r = lax.rsqrt(jnp.mean(x2 * x2, axis=-1, keepdims=True) + EPS)
        gain = vec_ref[V_G2:V_G2 + 1, :] * (1.0 + mod_ref[M_SC2:M_SC2 + 1, :])
        h2_ref[...] = (x2 * r * gain + mod_ref[M_SH2:M_SH2 + 1, :]).astype(BF16)

    tok = pl.BlockSpec((tm, D), lambda i: (i, 0))
    wspec = pl.BlockSpec((D, D), lambda i: (0, 0))
    sd = lambda dt: jax.ShapeDtypeStruct((s, D), dt)
    return pl.pallas_call(
        body, name="branch_fwd", grid=(s // tm,),
        in_specs=[tok, tok, tok, tok,
                  tok, pl.BlockSpec((8, D), lambda i: (0, 0)), pl.BlockSpec((16, D), lambda i: (0, 0)),
                  wspec, wspec, wspec],
        out_specs=[tok] * 6,
        out_shape=[sd(BF16), sd(BF16), sd(BF16), sd(BF16), sd(F32), sd(BF16)],
        compiler_params=_params(("parallel",)),
    )(za, pooled, sa, sb, x, modr, vecs, w_a, w_b, w_out)


def _mlp_fwd(h2, x2, target, modr, vecs, w_up, w_down):
    s = x2.shape[0]
    tm = min(TM_MLP, s)

    def body(h2_ref, x2_ref, tgt_ref, mod_ref, vec_ref, wu_ref, wd_ref,
             ru_ref, dx3_ref, ddn_ref, small_ref):
        @pl.when(pl.program_id(0) == 0)
        def _():
            small_ref[...] = jnp.zeros_like(small_ref)

        h2 = h2_ref[...]
        dn = None
        for c in range(D_FF // D):
            cs = slice(c * D, (c + 1) * D)
            ru = jnp.maximum(_dot(h2, wu_ref[:, cs]), 0.0)
            ru_ref[:, cs] = ru.astype(BF16)
            part = _dot((ru * ru).astype(BF16), wd_ref[cs, :])
            dn = part if dn is None else dn + part
        gt2 = mod_ref[M_GT2:M_GT2 + 1, :]
        gf = vec_ref[V_GF:V_GF + 1, :]
        x3 = x2_ref[...] + gt2 * dn
        r3 = lax.rsqrt(jnp.mean(x3 * x3, axis=-1, keepdims=True) + EPS)
        n3 = x3 * r3
        err = n3 * gf - tgt_ref[...]
        dy = err * (1.0 / D)
        dn3 = dy * gf
        dx3 = r3 * (dn3 - n3 * jnp.mean(dn3 * n3, axis=-1, keepdims=True))
        dx3_ref[...] = dx3
        ddn_ref[...] = (dx3 * gt2).astype(BF16)
        small_ref[0:1, :] += jnp.sum(dy * n3, axis=0, keepdims=True)
        small_ref[1:2, :] += jnp.sum(dx3 * dn, axis=0, keepdims=True)
        small_ref[2:3, :] += (0.5 / D) * jnp.sum(err * err, axis=0, keepdims=True)

    tok = pl.BlockSpec((tm, D), lambda i: (i, 0))
    return pl.pallas_call(
        body, name="mlp_fwd", grid=(s // tm,),
        in_specs=[tok, tok, tok,
                  pl.BlockSpec((8, D), lambda i: (0, 0)), pl.BlockSpec((16, D), lambda i: (0, 0)),
                  _resident((D, D_FF)), _resident((D_FF, D))],
        out_specs=[pl.BlockSpec((tm, D_FF), lambda i: (i, 0)), tok, tok,
                   pl.BlockSpec((8, D), lambda i: (0, 0))],
        out_shape=[jax.ShapeDtypeStruct((s, D_FF), BF16), jax.ShapeDtypeStruct((s, D), F32),
                   jax.ShapeDtypeStruct((s, D), BF16), jax.ShapeDtypeStruct((8, D), F32)],
        compiler_params=_params(("arbitrary",)),
    )(h2, x2, target, modr, vecs, w_up, w_down)


def _mlp_bwd(d_dn, ru, x2, dx3, o, modr, vecs, w_up, w_down):
    s = x2.shape[0]
    tm = min(TM_MLP_BWD, s)

    def body(ddn_ref, ru_ref, x2_ref, dx3_ref, o_ref, mod_ref, vec_ref, wu_ref, wd_ref,
             dup_ref, dx2_ref, do_ref, small_ref):
        @pl.when(pl.program_id(0) == 0)
        def _():
            small_ref[...] = jnp.zeros_like(small_ref)

        ddn = ddn_ref[...]
        dh2 = None
        for c in range(D_FF // D):
            cs = slice(c * D, (c + 1) * D)
            dff = _dot_nt(ddn, wd_ref[cs, :])
            dup = (dff * (2.0 * ru_ref[:, cs].astype(F32))).astype(BF16)
            dup_ref[:, cs] = dup
            part = _dot_nt(dup, wu_ref[:, cs])
            dh2 = part if dh2 is None else dh2 + part
        x2 = x2_ref[...]
        r2 = lax.rsqrt(jnp.mean(x2 * x2, axis=-1, keepdims=True) + EPS)
        xn2 = x2 * r2
        gain = vec_ref[V_G2:V_G2 + 1, :] * (1.0 + mod_ref[M_SC2:M_SC2 + 1, :])
        dxn2 = dh2 * gain
        dx2 = dx3_ref[...] + r2 * (dxn2 - xn2 * jnp.mean(dxn2 * xn2, axis=-1, keepdims=True))
        dx2_ref[...] = dx2
        do_ref[...] = (dx2 * mod_ref[M_GT1:M_GT1 + 1, :]).astype(BF16)
        small_ref[0:1, :] += jnp.sum(dh2, axis=0, keepdims=True)
        small_ref[1:2, :] += jnp.sum(dh2 * xn2, axis=0, keepdims=True)
        small_ref[2:3, :] += jnp.sum(dx2 * o_ref[...].astype(F32), axis=0, keepdims=True)

    tok = pl.BlockSpec((tm, D), lambda i: (i, 0))
    wide = pl.BlockSpec((tm, D_FF), lambda i: (i, 0))
    return pl.pallas_call(
        body, name="mlp_bwd", grid=(s // tm,),
        in_specs=[tok, wide, tok, tok, tok,
                  pl.BlockSpec((8, D), lambda i: (0, 0)), pl.BlockSpec((16, D), lambda i: (0, 0)),
                  _resident((D, D_FF)), _resident((D_FF, D))],
        out_specs=[wide, tok, tok, pl.BlockSpec((8, D), lambda i: (0, 0))],
        out_shape=[jax.ShapeDtypeStruct((s, D_FF), BF16), jax.ShapeDtypeStruct((s, D), F32),
                   jax.ShapeDtypeStruct((s, D), BF16), jax.ShapeDtypeStruct((8, D), F32)],
        compiler_params=_params(("arbitrary",)),
    )(d_dn, ru, x2, dx3, o, modr, vecs, w_up, w_down)


def _branch_bwd(do, sa, sb, ba, bb, w_a, w_b, w_out, dep):
    s = do.shape[0]
    tm = min(TM_BRANCH, s)

    def body(do_ref, sa_ref, sb_ref, ba_ref, bb_ref, wa_ref, wb_ref, wo_ref, dep_ref,
             dba_ref, dbb_ref, dg_ref, dza_ref, dpooled_ref):
        dmerged = _dot_nt(do_ref[...], wo_ref[...])
        sa = sa_ref[...]
        sb = sb_ref[...]
        dba = (dmerged * sa).astype(BF16)
        dbb = (dmerged * sb).astype(BF16)
        dba_ref[...] = dba
        dbb_ref[...] = dbb
        dg_ref[:, :D] = (dmerged * ba_ref[...].astype(F32) * sa * (1.0 - sa)).astype(BF16)
        dg_ref[:, D:] = (dmerged * bb_ref[...].astype(F32) * sb * (1.0 - sb)).astype(BF16)
        dza_ref[...] = _dot_nt(dba, wa_ref[...])
        dpooled_ref[...] = _dot_nt(dbb, wb_ref[...])

    tok = pl.BlockSpec((tm, D), lambda i: (i, 0))
    wspec = pl.BlockSpec((D, D), lambda i: (0, 0))
    sd = lambda dt: jax.ShapeDtypeStruct((s, D), dt)
    return pl.pallas_call(
        body, name="branch_bwd", grid=(s // tm,),
        in_specs=[tok, tok, tok, tok, tok, wspec, wspec, wspec, pl.BlockSpec(memory_space=pl.ANY)],
        out_specs=[tok, tok, pl.BlockSpec((tm, 2 * D), lambda i: (i, 0)), tok, tok],
        out_shape=[sd(BF16), sd(BF16), jax.ShapeDtypeStruct((s, 2 * D), BF16), sd(F32), sd(F32)],
        compiler_params=_params(("parallel",)),
    )(do, sa, sb, ba, bb, w_a, w_b, w_out, dep)


def _mix_bwd(dza, dpooled, x_rnn, ga, dga, xr, hr, p, gates, dgates, vecs, w_rg_a, w_rg_x, w_pool, dep):
    s = xr.shape[0]
    tm = min(TM_MIX, s)
    nb = s // tm

    def body(dza_ref, dpooled_ref, xh_ref, x_ref, ga_ref, dga_ref, xr_ref, hh_ref, hr_ref, p_ref,
             a_ref, mult_ref, ra_ref, ri_ref, dg_ref, vec_ref, wa_ref, wx_ref, wp_ref, dep_ref,
             dproj_ref, dwa_ref, dwx_ref, dwp_ref, small_ref,
             scan_carry, dxr_carry, q_carry, scan_scr, dwa_acc, dwx_acc, dwp_acc):
        i = pl.program_id(0)
        bi = nb - 1 - i
        first_t = bi == 0

        @pl.when(i == 0)
        def _():
            scan_carry[...] = jnp.zeros_like(scan_carry)
            dxr_carry[...] = jnp.zeros_like(dxr_carry)
            q_carry[...] = jnp.zeros_like(q_carry)
            dwa_acc[...] = jnp.zeros_like(dwa_acc)
            dwx_acc[...] = jnp.zeros_like(dwx_acc)
            dwp_acc[...] = jnp.zeros_like(dwp_acc)
            small_ref[...] = jnp.zeros_like(small_ref)

        row = lax.broadcasted_iota(jnp.int32, (tm, GW), 0)
        is_t0 = jnp.logical_and(first_t, row == 0)
        head_t = (lax.broadcasted_iota(jnp.int32, (HALO_U, GW), 0) + 1).astype(F32)
        colsum = lambda v: jnp.sum(v, axis=0, keepdims=True)
        for g in range(N_GROUPS):
            cs = slice(g * GW, (g + 1) * GW)
            vec = vec_ref[:, cs]
            xr = xr_ref[:, cs]
            hr = hr_ref[:, cs]
            dza = dza_ref[:, cs]
            dproj_ref[:, D + g * GW:D + (g + 1) * GW] = (dza * hr * dga_ref[:, cs]).astype(BF16)
            dhr = dza * ga_ref[:, cs]
            a = a_ref[:, cs]
            mult = mult_ref[:, cs]
            ra = ra_ref[:, cs].astype(F32)
            ri = ri_ref[:, cs].astype(F32)
            sp = _softplus(vec[V_A_PARAM:V_A_PARAM + 1])
            m = jnp.where(row == tm - 1, 1.0, _shift_up(a, 1))
            gsum = _scan_up(m, dhr, scan_carry[0:1, cs], scan_scr)
            scan_carry[0:1, cs] = a[0:1, :] * gsum[0:1, :]
            hh = jnp.where(first_t, 0.0, hh_ref[:, cs])
            hprev = _shift_down(jnp.concatenate([hh, hr], axis=0), 1)[8:]
            da = gsum * hprev
            dmult = jnp.where(is_t0, 0.0, gsum * xr * ri)
            dlog_a = da * a - dmult * a * a / mult
            dri = gsum * xr * mult
            dxr = gsum * ri * mult
            small_ref[7:8, cs] += colsum((-C_RG) * ra * dlog_a)
            dpa = (((-C_RG) * sp) * dlog_a * ra * (1.0 - ra))
            dpx = dri * ri * (1.0 - ri)
            small_ref[5:6, cs] += colsum(dpa)
            small_ref[6:7, cs] += colsum(dpx)
            dpa = dpa.astype(BF16)
            dpx = dpx.astype(BF16)
            xrb = xr.astype(BF16)
            dwa_acc[g] += _dot_tn(xrb, dpa)
            dwx_acc[g] += _dot_tn(xrb, dpx)
            dxr = dxr + _dot_nt(dpa, wa_ref[g]) + _dot_nt(dpx, wx_ref[g])
            small_ref[4:5, cs] += colsum(dxr)
            xh = jnp.where(first_t, 0.0, xh_ref[:, cs])
            taps = _conv_taps(jnp.concatenate([xh, x_ref[:, cs]], axis=0))
            dxr_ext = jnp.concatenate([dxr, dxr_carry[:, cs]], axis=0)
            dx = vec[V_CONV_W + 3:V_CONV_W + 4] * dxr
            for j in range(4):
                small_ref[j:j + 1, cs] += colsum(dxr * taps[j])
                if j < 3:
                    dx = dx + vec[V_CONV_W + j:V_CONV_W + j + 1] * _shift_up(dxr_ext, 3 - j)[:tm]
            dxr_carry[:, cs] = dxr[0:8, :]
            dproj_ref[:, cs] = dx.astype(BF16)
            pg = p_ref[:, cs]
            dpooled = dpooled_ref[:, cs]
            pb = _dot(pg, wp_ref[g]) + vec[V_B_POOL:V_B_POOL + 1]
            small_ref[9:10, cs] += colsum(dpooled * pb)
            dpb = dpooled * vec[V_POOL_SCALE:V_POOL_SCALE + 1]
            small_ref[8:9, cs] += colsum(dpb)
            dpbb = dpb.astype(BF16)
            dwp_acc[g] += _dot_tn(pg, dpbb)
            dp = _dot_nt(dpbb, wp_ref[g])
            q = _window_mean(dp, POOL_WINDOWS[g], first_t, head_t)
            sm = jnp.concatenate([q, q_carry[:, cs]], axis=0)
            k = 1
            while k < POOL_WINDOWS[g]:
                sm = sm + _shift_up(sm, k)
                k *= 2
            q_carry[:, cs] = q[0:HALO_U, :]
            dproj_ref[:, 2 * D + g * GW:2 * D + (g + 1) * GW] = (sm[:tm] - dp).astype(BF16)
        dproj_ref[:, 3 * D:] = dg_ref[...]

        @pl.when(i == nb - 1)
        def _():
            dwa_ref[...] = dwa_acc[...].astype(BF16)
            dwx_ref[...] = dwx_acc[...].astype(BF16)
            dwp_ref[...] = dwp_acc[...].astype(BF16)

    rev = lambda i: nb - 1 - i
    tok = pl.BlockSpec((tm, D), lambda i: (rev(i), 0))
    halo8 = lambda k: pl.BlockSpec((8, D), lambda i: (jnp.maximum(rev(i) * (tm // 8) - 1, 0), k))
    wspec = pl.BlockSpec((N_GROUPS, GW, GW), lambda i: (0, 0, 0))
    wshape = jax.ShapeDtypeStruct((N_GROUPS, GW, GW), BF16)
    return pl.pallas_call(
        body, name="mix_bwd", grid=(nb,),
        in_specs=[tok, tok, halo8(0), tok, tok, tok, tok, halo8(0), tok, tok, tok, tok, tok, tok,
                  pl.BlockSpec((tm, 2 * D), lambda i: (rev(i), 0)),
                  pl.BlockSpec((16, D), lambda i: (0, 0)), wspec, wspec, wspec, pl.BlockSpec(memory_space=pl.ANY)],
        out_specs=[pl.BlockSpec((tm, D_IN), lambda i: (rev(i), 0)), wspec, wspec, wspec,
                   pl.BlockSpec((16, D), lambda i: (0, 0))],
        out_shape=[jax.ShapeDtypeStruct((s, D_IN), BF16), wshape, wshape, wshape,
                   jax.ShapeDtypeStruct((16, D), F32)],
        scratch_shapes=[pltpu.VMEM((8, D), F32), pltpu.VMEM((8, D), F32), pltpu.VMEM((HALO_U, D), F32),
                        pltpu.VMEM((3, tm, LANES), F32)] + [pltpu.VMEM((N_GROUPS, GW, GW), F32)] * 3,
        compiler_params=_params(("arbitrary",)),
    )(dza, dpooled, x_rnn, x_rnn, ga, dga, xr, hr, hr, p, *gates, dgates, vecs, w_rg_a, w_rg_x, w_pool, dep)


def _proj_bwd(dproj, x, dx2, modr, vecs, w_in, dep):
    s = x.shape[0]
    tm = min(TM_PROJ, s)

    def body(dp_ref, x_ref, dx2_ref, mod_ref, vec_ref, w_ref, dep_ref, gx_ref, small_ref):
        @pl.when(pl.program_id(0) == 0)
        def _():
            small_ref[...] = jnp.zeros_like(small_ref)

        dh1 = None
        for c in range(D_IN // D):
            cs = slice(c * D, (c + 1) * D)
            part = _dot_nt(dp_ref[:, cs], w_ref[:, cs])
            dh1 = part if dh1 is None else dh1 + part
        xv = x_ref[...]
        r1 = lax.rsqrt(jnp.mean(xv * xv, axis=-1, keepdims=True) + EPS)
        xn1 = xv * r1
        gain = vec_ref[V_G1:V_G1 + 1, :] * (1.0 + mod_ref[M_SC1:M_SC1 + 1, :])
        dxn1 = dh1 * gain
        gx_ref[...] = dx2_ref[...] + r1 * (dxn1 - xn1 * jnp.mean(dxn1 * xn1, axis=-1, keepdims=True))
        small_ref[0:1, :] += jnp.sum(dh1, axis=0, keepdims=True)
        small_ref[1:2, :] += jnp.sum(dh1 * xn1, axis=0, keepdims=True)

    tok = pl.BlockSpec((tm, D), lambda i: (i, 0))
    return pl.pallas_call(
        body, name="proj_bwd", grid=(s // tm,),
        in_specs=[pl.BlockSpec((tm, D_IN), lambda i: (i, 0)), tok, tok,
                  pl.BlockSpec((8, D), lambda i: (0, 0)), pl.BlockSpec((16, D), lambda i: (0, 0)),
                  _resident((D, D_IN)), pl.BlockSpec(memory_space=pl.ANY)],
        out_specs=[tok, pl.BlockSpec((8, D), lambda i: (0, 0))],
        out_shape=[jax.ShapeDtypeStruct((s, D), F32), jax.ShapeDtypeStruct((8, D), F32)],
        compiler_params=_params(("arbitrary",)),
    )(dproj, x, dx2, modr, vecs, w_in, dep)


def _wgrad(a, b, name, square_a=False, dep=None):
    s, ka = a.shape
    n = b.shape[1]
    tka = ka if ka <= 1024 else ka // 2
    tn = n if n <= 1024 else n // 2
    ts = min(TS_WGRAD if ka * n > D * D else TS_WGRAD_SQUARE, s)
    ns = s // ts
    nc = 512
    deps = [] if dep is None else [dep]

    def body(a_ref, b_ref, *refs):
        out_ref, acc_ref = refs[-2:]
        t = pl.program_id(2)

        @pl.when(t == 0)
        def _():
            acc_ref[...] = jnp.zeros_like(acc_ref)

        av = a_ref[...]
        if square_a:
            af = av.astype(F32)
            av = (af * af).astype(BF16)
        for c in range(tn // nc):
            cs = slice(c * nc, (c + 1) * nc)
            acc_ref[:, cs] += _dot_tn(av, b_ref[:, cs])

        @pl.when(t == ns - 1)
        def _():
            out_ref[...] = acc_ref[...].astype(BF16)

    return pl.pallas_call(
        body, name=name, grid=(ka // tka, n // tn, ns),
        in_specs=[pl.BlockSpec((ts, tka), lambda i, j, t: (t, i)),
                  pl.BlockSpec((ts, tn), lambda i, j, t: (t, j))] + [pl.BlockSpec(memory_space=pl.ANY)] * len(deps),
        out_specs=pl.BlockSpec((tka, tn), lambda i, j, t: (i, j)),
        out_shape=jax.ShapeDtypeStruct((ka, n), BF16),
        scratch_shapes=[pltpu.VMEM((tka, tn), F32)],
        compiler_params=_params(("parallel", "parallel", "arbitrary")),
    )(a, b, *deps)


def _window(ref, kind, idx, size):
    start = pl.multiple_of(idx * size, size)
    if kind == 0:
        return ref.at[pl.ds(start, size)]
    if kind == 1:
        return ref.at[:, pl.ds(start, size)]
    return ref.at[:, :, pl.ds(start, size)]


def _mesh_place():
    x, y, c = lax.axis_index("x"), lax.axis_index("y"), lax.axis_index("c")
    return x, y, c, 4 * x + 2 * y + c


def _peer(x, y, c, q):
    px = 1 - x if q & 4 else x
    py = 1 - y if q & 2 else y
    pc = 1 - c if q & 1 else c
    return (px, py, pc), 4 * px + 2 * py + pc


_HBM = pl.BlockSpec(memory_space=pltpu.HBM)
_SEM = pl.BlockSpec(memory_space=pltpu.SEMAPHORE)
_EFFECT = pltpu.SideEffectType.DATAFLOW_SIDE_EFFECTING


N_NEAR = 4


def _near(x, y, c):
    out = [((x, y, 1 - c), 4 * x + 2 * y + 1 - c)]
    for j in (1, 2, 3):
        px = 1 - x if j & 2 else x
        py = 1 - y if j & 1 else y
        out.append(((px, py, c), 4 * px + 2 * py + c))
    return out


def _remote(src, dst, send_sems, recv_sems, slot, device):
    return pltpu.make_async_remote_copy(src_ref=src, dst_ref=dst, send_sem=send_sems.at[slot], recv_sem=recv_sems.at[slot],
                                        device_id=device, device_id_type=MESH)


def _split_call(name, arrays, sems_in, n_new_sems, after, emit):
    na, ns, nn = len(arrays), len(sems_in), len(n_new_sems)

    def body(*refs):
        emit(refs[:na], refs[na:na + ns], refs[na + ns + 1:na + ns + 1 + nn])
        refs[-1][...] = jnp.zeros_like(refs[-1])

    outs = pl.pallas_call(
        body, name=name,
        out_shape=(*[pltpu.SemaphoreType.DMA((m,)) for m in n_new_sems],
                   *[pltpu.HBM(a.shape, a.dtype) for a in arrays], jax.ShapeDtypeStruct((8, 128), F32)),
        in_specs=[_HBM] * na + [_SEM] * ns + [pl.BlockSpec(memory_space=pl.ANY)],
        out_specs=(*[_SEM] * nn, *[_HBM] * na, pl.BlockSpec(memory_space=pltpu.VMEM)),
        input_output_aliases={i: nn + i for i in range(na)},
        compiler_params=pltpu.CompilerParams(has_side_effects=_EFFECT),
    )(*[pltpu.with_memory_space_constraint(a, pltpu.HBM) for a in arrays], *sems_in, after)
    return list(outs[:nn]), list(outs[nn:nn + na]), outs[-1]


def _together(name, steps, after):
    parts = [(ex.arrays, ex.sems, ex.new_sems[step], getattr(ex, "emit_" + step)) for ex, step in steps]

    def emit(arr, old, new):
        ia = io = ib = 0
        for arrays, sems, new_sems, emit_one in parts:
            emit_one(arr[ia:ia + len(arrays)], old[io:io + len(sems)], new[ib:ib + len(new_sems)])
            ia, io, ib = ia + len(arrays), io + len(sems), ib + len(new_sems)

    new, arrays, token = _split_call(name, [a for p in parts for a in p[0]], [s for p in parts for s in p[1]],
                                     [m for p in parts for m in p[2]], after, emit)
    out = []
    ia = ib = 0
    for (ex, _), (arrs, sems, new_sems, _) in zip(steps, parts):
        ex.arrays, ex.sems, ex.token = arrays[ia:ia + len(arrs)], [*sems, *new[ib:ib + len(new_sems)]], token
        ia, ib = ia + len(arrs), ib + len(new_sems)
        out.append(ex.arrays[ex.n:])
    return out


class _AllGather:
    def __init__(self, shards, kinds, name):
        self.n, self.kinds, self.name = len(shards), kinds, name
        self.sizes = [s.shape[k] for s, k in zip(shards, kinds)]
        lands = []
        for s, k in zip(shards, kinds):
            dims = list(s.shape)
            dims[k] *= N_DEV
            lands.append(lax.empty(tuple(dims), s.dtype))
        self.arrays, self.sems = [*shards, *lands], []

    def window(self, arr, k, idx):
        return _window(arr[self.n + k], self.kinds[k], idx, self.sizes[k])

    def start(self, after):
        _together(self.name + "_start", [(self, "start")], after)

    def forward(self, after):
        _together(self.name + "_forward", [(self, "forward")], after)

    def finish(self, after):
        return _together(self.name + "_finish", [(self, "finish")], after)[0]


class _Gather(_AllGather):
    def __init__(self, shards, kinds, name):
        super().__init__(shards, kinds, name)
        n = self.n
        self.new_sems = dict(start=[n * N_NEAR, n * N_NEAR, n], forward=[n * N_NEAR] * 2, finish=[])

    def emit_start(self, arr, _, new):
        x, y, c, me = _mesh_place()
        for k in range(self.n):
            pltpu.make_async_copy(arr[k], self.window(arr, k, me), new[2].at[k]).start()
        for k in range(self.n):
            for j, (dev, _) in enumerate(_near(x, y, c)):
                _remote(arr[k], self.window(arr, k, me), new[0], new[1], k * N_NEAR + j, dev).start()

    def emit_forward(self, arr, old, new):
        x, y, c, _ = _mesh_place()
        near = _near(x, y, c)
        for k in range(self.n):
            for j in (1, 2, 3):
                dev, idx = near[j]
                landed = self.window(arr, k, idx)
                _remote(arr[k], landed, old[0], old[1], k * N_NEAR + j, dev).wait_recv()
                _remote(landed, landed, new[0], new[1], k * N_NEAR + j, near[0][0]).start()

    def emit_finish(self, arr, old, _):
        x, y, c, me = _mesh_place()
        near = _near(x, y, c)
        other_core = near[0][0]
        for k in range(self.n):
            win = lambda idx: self.window(arr, k, idx)
            pltpu.make_async_copy(arr[k], win(me), old[2].at[k]).wait()
            for j, (dev, idx) in enumerate(near):
                _remote(arr[k], win(me), old[0], old[1], k * N_NEAR + j, dev).wait_send()
            _remote(arr[k], win(near[0][1]), old[0], old[1], k * N_NEAR, other_core).wait_recv()
            for j in (1, 2, 3):
                idx = near[j][1]
                _remote(win(idx), win(idx), old[3], old[4], k * N_NEAR + j, other_core).wait_send()
                _remote(arr[k], win(idx + 1 - 2 * c), old[3], old[4], k * N_NEAR + j, other_core).wait_recv()


class _Spread(_AllGather):
    def __init__(self, shards, kinds, name):
        super().__init__(shards, kinds, name)
        n = self.n
        self.new_sems = dict(start=[n * N_DEV, n * N_DEV, n], finish=[])

    def emit_start(self, arr, _, new):
        x, y, c, me = _mesh_place()
        for k in range(self.n):
            mine = self.window(arr, k, me)
            pltpu.make_async_copy(arr[k], mine, new[2].at[k]).start()
            for q in range(1, N_DEV):
                _remote(arr[k], mine, new[0], new[1], k * N_DEV + q, _peer(x, y, c, q)[0]).start()

    def emit_finish(self, arr, old, _):
        x, y, c, me = _mesh_place()
        for k in range(self.n):
            win = lambda idx: self.window(arr, k, idx)
            pltpu.make_async_copy(arr[k], win(me), old[2].at[k]).wait()
            for q in range(1, N_DEV):
                peer, peer_idx = _peer(x, y, c, q)
                _remote(arr[k], win(me), old[0], old[1], k * N_DEV + q, peer).wait_send()
                _remote(arr[k], win(peer_idx), old[0], old[1], k * N_DEV + q, peer).wait_recv()


class _Scatter:
    def __init__(self, partials, kinds, after, name):
        self.n, self.kinds, self.name, self.partials = len(partials), kinds, name, partials
        self.sizes = [p.shape[k] // N_DEV for p, k in zip(partials, kinds)]
        n, sizes = self.n, self.sizes
        self.slot_shapes = []
        for p, k, size in zip(partials, kinds, sizes):
            dims = list(p.shape)
            dims[k] = size
            self.slot_shapes.append((N_NEAR, *dims))
        slots = [lax.empty(sh, p.dtype) for sh, p in zip(self.slot_shapes, partials)]

        def emit(arr, _, new):
            x, y, c, _ = _mesh_place()
            near = _near(x, y, c)
            for k in range(n):
                for j in range(N_NEAR):
                    owner = near[j][1] if j == 0 else near[j][1] + 1 - 2 * c
                    _remote(_window(arr[k], kinds[k], owner, sizes[k]), arr[n + k].at[j], new[0], new[1],
                            k * N_NEAR + j, near[0][0]).start()

        self.sems, self.arrays, self.token = _split_call(name + "_start", [*partials, *slots], [], [n * N_NEAR] * 2,
                                                         after, emit)

    def combine_and_send(self, own4, after):
        n, kinds, sizes = self.n, self.kinds, self.sizes

        def emit_wait(arr, old, _):
            x, y, c, _ = _mesh_place()
            near = _near(x, y, c)
            for k in range(n):
                for j in range(N_NEAR):
                    owner = near[j][1] if j == 0 else near[j][1] + 1 - 2 * c
                    cp = _remote(_window(arr[k], kinds[k], owner, sizes[k]), arr[n + k].at[j], old[0], old[1],
                                 k * N_NEAR + j, near[0][0])
                    cp.wait_send()
                    cp.wait_recv()

        _, arrays, _ = _split_call(self.name + "_landed", self.arrays, self.sems, [], after, emit_wait)
        chip_sums = _chip_sums(arrays[:n], arrays[n:], kinds, sizes, own4, self.name + "_combine")
        arrivals = [lax.empty((N_NEAR - 1, *sh[1:]), p.dtype) for sh, p in zip(self.slot_shapes, self.partials)]

        def emit_send(arr, _, new):
            x, y, c, _ = _mesh_place()
            near = _near(x, y, c)
            for k in range(n):
                for j in (1, 2, 3):
                    _remote(arr[k].at[j], arr[n + k].at[j - 1], new[0], new[1], k * N_NEAR + j, near[j][0]).start()

        self.sems, self.arrays, self.token = _split_call(self.name + "_send", [*chip_sums, *arrivals], [],
                                                         [n * N_NEAR] * 2, own4, emit_send)

    def finish(self, after):
        n = self.n

        def emit(arr, old, _):
            x, y, c, _ = _mesh_place()
            near = _near(x, y, c)
            for k in range(n):
                for j in (1, 2, 3):
                    cp = _remote(arr[k].at[j], arr[n + k].at[j - 1], old[0], old[1], k * N_NEAR + j, near[j][0])
                    cp.wait_send()
                    cp.wait_recv()

        _, arrays, _ = _split_call(self.name + "_finish", self.arrays, self.sems, [], after, emit)
        return arrays[:n], arrays[n:]


def _chip_sums(partials, slots, kinds, sizes, own4, name):
    n = len(partials)

    def body(own_ref, *refs):
        for k in range(n):
            refs[2 * n + k][...] = (refs[k][...].astype(F32) + refs[n + k][...].astype(F32)).astype(BF16)

    in_specs, slot_specs = [], []
    for p, s, kind, size in zip(partials, slots, kinds, sizes):
        block = list(p.shape)
        block[kind] = size
        nd = len(block)
        in_specs.append(pl.BlockSpec(tuple(block), functools.partial(
            lambda j, own, kind, nd: tuple(own[j] if d == kind else 0 for d in range(nd)), kind=kind, nd=nd)))
        slot_specs.append(pl.BlockSpec((None, *block), functools.partial(
            lambda j, own, nd: (j,) + (0,) * nd, nd=nd)))
    return pl.pallas_call(
        body, name=name,
        grid_spec=pltpu.PrefetchScalarGridSpec(num_scalar_prefetch=1, grid=(N_NEAR,),
                                               in_specs=in_specs + slot_specs, out_specs=slot_specs),
        out_shape=[jax.ShapeDtypeStruct(s.shape, s.dtype) for s in slots],
        compiler_params=_params(("arbitrary",)),
    )(own4, *partials, *slots)


def _to_bf16(arrays, name, dep=None):
    n = len(arrays)
    deps = [] if dep is None else [dep]

    def body(*refs):
        for src, dst in zip(refs[:n], refs[n + len(deps):]):
            dst[...] = src[...].astype(BF16)

    vmem = pl.BlockSpec(memory_space=pltpu.VMEM)
    return pl.pallas_call(body, name=name, out_shape=[jax.ShapeDtypeStruct(a.shape, BF16) for a in arrays],
                          in_specs=[vmem] * n + [pl.BlockSpec(memory_space=pl.ANY)] * len(deps), out_specs=[vmem] * n,
                          compiler_params=pltpu.CompilerParams(vmem_limit_bytes=V7X_VMEM_LIMIT))(*arrays, *deps)


def _silu(c):
    return c * _sigmoid_tail(c)


def _ada_fwd(c_all, w_ada, b_ada_cols, dep):
    def body(c_ref, w_ref, b_ref, dep_ref, out_ref):
        out_ref[...] = jnp.dot(_silu(c_ref[...]), w_ref[...], preferred_element_type=F32,
                               precision=lax.Precision.HIGHEST) + b_ref[...]

    vmem = pl.BlockSpec(memory_space=pltpu.VMEM)
    return pl.pallas_call(
        body, name="ada_fwd", in_specs=[vmem, vmem, vmem, pl.BlockSpec(memory_space=pl.ANY)], out_specs=vmem,
        out_shape=jax.ShapeDtypeStruct((N_DEV, w_ada.shape[1]), F32),
    )(c_all, w_ada, b_ada_cols, dep)


def _adam(w, g, m, v):
    m = ADAM_B1 * m + (1.0 - ADAM_B1) * g
    v = ADAM_B2 * v + (1.0 - ADAM_B2) * (g * g)
    m_hat = m / (1.0 - ADAM_B1 ** ADAM_STEP)
    v_hat = v / (1.0 - ADAM_B2 ** ADAM_STEP)
    delta = -ADAM_LR * (m_hat / (jnp.sqrt(v_hat) + ADAM_EPS) + ADAM_WD * w)
    return delta, m, v


def _ada_bwd_adam(c_all, dmod_cols, w, m, v):
    def body(c_ref, d_ref, w_ref, m_ref, v_ref, g_ref, delta_ref, nm_ref, nv_ref):
        g = lax.dot_general(_silu(c_ref[...]), d_ref[...], (((0,), (0,)), ((), ())),
                            preferred_element_type=F32, precision=lax.Precision.HIGHEST)
        g_ref[...] = g
        delta_ref[...], nm_ref[...], nv_ref[...] = _adam(w_ref[...], g, m_ref[...], v_ref[...])

    sd = jax.ShapeDtypeStruct(w.shape, F32)
    return pl.pallas_call(body, name="ada_bwd_adam", out_shape=[sd] * 4,
                          compiler_params=pltpu.CompilerParams(vmem_limit_bytes=V7X_VMEM_LIMIT),
                          )(c_all, dmod_cols, w, m, v)


def _adam_group(chip_sums, arrivals, ws, ms, vs, n_tiles, name):
    n = len(ws)

    def body(*refs):
        for k in range(n):
            c_ref, a_ref, w_ref, m_ref, v_ref = (refs[j * n + k] for j in range(5))
            g_ref, delta_ref, nm_ref, nv_ref = (refs[(5 + j) * n + k] for j in range(4))
            g = c_ref[...].astype(F32)
            for j in range(N_NEAR - 1):
                g = g + a_ref[j].astype(F32)
            g_ref[...] = g
            delta_ref[...], nm_ref[...], nv_ref[...] = _adam(w_ref[...], g, m_ref[...], v_ref[...])

    tiles = [(w.shape[0] // n_tiles, w.shape[1]) for w in ws]
    blk = [pl.BlockSpec(t, lambda i: (i, 0)) for t in tiles]
    return pl.pallas_call(
        body, name=name, grid=(n_tiles,),
        in_specs=[pl.BlockSpec((None, *t), lambda i: (0, i, 0)) for t in tiles]
        + [pl.BlockSpec((N_NEAR - 1, *t), lambda i: (0, i, 0)) for t in tiles] + blk * 3,
        out_specs=blk * 4, out_shape=[jax.ShapeDtypeStruct(w.shape, F32) for w in ws] * 4,
        compiler_params=_params(("parallel",)),
    )(*chip_sums, *arrivals, *ws, *ms, *vs)


N_SMALL = 40
N_SMALL_PARAMS = 11


def _pack_vecs(conv_w_full, rows):
    def body(cw_ref, *refs):
        out = refs[-1]
        out[...] = jnp.zeros_like(out)
        out[0:4, :] = cw_ref[0:4, :]
        for r, ref in enumerate(refs[:-1]):
            out[4 + r:5 + r, :] = ref[...]

    return pl.pallas_call(body, name="pack_vecs", out_shape=jax.ShapeDtypeStruct((16, D), F32))(conv_w_full, *rows)


def _small_finish(gathered, conv_cols, mod_all, vecs, ws, ms, vs):
    n = N_SMALL_PARAMS

    def body(g_ref, conv_ref, mod_ref, vec_ref, *refs):
        w_refs, m_refs, v_refs = refs[:n], refs[n:2 * n], refs[2 * n:3 * n]
        outs = refs[3 * n:]
        g1 = vec_ref[V_G1:V_G1 + 1, :]
        g2 = vec_ref[V_G2:V_G2 + 1, :]
        zero = jnp.zeros((1, D), F32)
        dg1, dg2, dgf, loss_lanes = zero, zero, zero, zero
        mixer = jnp.zeros((16, D), F32)
        db_ada = jnp.zeros((6, D), F32)
        d_conv_w = jnp.zeros(conv_ref.shape[1:], F32)
        for b in range(N_DEV):
            gb = g_ref[b]
            mod = mod_ref[b]
            q1 = gb[33:34]
            q2 = gb[9:10]
            dmod = jnp.concatenate([gb[32:33], q1 * g1, gb[10:11], gb[8:9], q2 * g2, gb[1:2]], axis=0)
            outs[4 * n][b] = dmod
            db_ada = db_ada + dmod
            dg1 = dg1 + q1 * (1.0 + mod[M_SC1:M_SC1 + 1])
            dg2 = dg2 + q2 * (1.0 + mod[M_SC2:M_SC2 + 1])
            dgf = dgf + gb[0:1]
            loss_lanes = loss_lanes + gb[2:3]
            mixer = mixer + gb[16:32]
            d_conv_w = d_conv_w + conv_ref[b]
        d_a_param = mixer[7:8] * _sigmoid_tail(vec_ref[V_A_PARAM:V_A_PARAM + 1, :])
        grads = [dg1, dg2, mixer[4:5], mixer[5:6], mixer[6:7], d_a_param, mixer[8:9], mixer[9:10], dgf,
                 db_ada, d_conv_w]

        def load(ref, rows):
            if ref.shape[0] == rows:
                return ref[...]
            return jnp.concatenate([ref[:, j * D:(j + 1) * D] for j in range(rows)], axis=0)

        def store(ref, val):
            if ref.shape == val.shape:
                ref[...] = val
            else:
                for j in range(val.shape[0]):
                    ref[:, j * D:(j + 1) * D] = val[j:j + 1]

        for k in range(n):
            rows = grads[k].shape[0]
            results = (grads[k], *_adam(load(w_refs[k], rows), grads[k], load(m_refs[k], rows), load(v_refs[k], rows)))
            for which, val in enumerate(results):
                store(outs[which * n + k], val)
        outs[4 * n + 1][...] = jnp.broadcast_to(jnp.sum(loss_lanes, axis=1, keepdims=True), (8, 128))

    shapes = [jax.ShapeDtypeStruct(w.shape, F32) for w in ws]
    return pl.pallas_call(
        body, name="small_finish",
        out_shape=shapes * 4 + [jax.ShapeDtypeStruct((N_DEV, 6, D), F32), jax.ShapeDtypeStruct((8, 128), F32)],
    )(gathered, conv_cols, mod_all, vecs, *ws, *ms, *vs)


def _pad_rows(a, rows):
    return jnp.pad(a, ((0, rows - a.shape[0]), (0, 0)))


def kernel(x, c, norm_mix_g, norm_mlp_g, w_ada, b_ada, w_in, conv_w, conv_b, w_rg_a, b_rg_a, w_rg_x, b_rg_x, a_param, w_branch_a, w_pool, b_pool, pool_scale, w_branch_b, w_out, w_up, w_down, final_g, loss_target, m_norm_mix_g, m_norm_mlp_g, m_w_ada, m_b_ada, m_w_in, m_conv_w, m_conv_b, m_w_rg_a, m_b_rg_a, m_w_rg_x, m_b_rg_x, m_a_param, m_w_branch_a, m_w_pool, m_b_pool, m_pool_scale, m_w_branch_b, m_w_out, m_w_up, m_w_down, m_final_g, v_norm_mix_g, v_norm_mlp_g, v_w_ada, v_b_ada, v_w_in, v_conv_w, v_conv_b, v_w_rg_a, v_b_rg_a, v_w_rg_x, v_b_rg_x, v_a_param, v_w_branch_a, v_w_pool, v_b_pool, v_pool_scale, v_w_branch_b, v_w_out, v_w_up, v_w_down, v_final_g):
    me = 4 * lax.axis_index("x") + 2 * lax.axis_index("y") + lax.axis_index("c")
    s = x.shape[1]
    x2d = x.reshape(s, D)
    target = loss_target.reshape(s, D)
    n_ada = w_ada.shape[2]

    b_ada_cols = lax.dynamic_slice(b_ada, (0, me * n_ada), (1, n_ada))

    sharded = dict(w_in=(w_in[0], 1), w_up=(w_up[0], 1), w_down=(w_down[0], 0), w_branch_a=(w_branch_a[0], 0),
                   w_branch_b=(w_branch_b[0], 0), w_out=(w_out[0], 0), w_rg_a=(w_rg_a[0], 1), w_rg_x=(w_rg_x[0], 1),
                   w_pool=(w_pool[0], 1))
    kind = {k: v[1] for k, v in sharded.items()}
    first_names = ["w_in"]
    later_names = [k for k in sharded if k not in first_names]
    mix_names = ["w_rg_a", "w_rg_x", "w_pool"]
    branch_names = ["w_branch_a", "w_branch_b", "w_out"]
    mlp_names = ["w_up", "w_down"]

    def gather(group, after, name):
        exchange = _Gather([shard[k] for k in group], [kind[k] for k in group], name)
        exchange.start(after)
        return exchange

    shard = dict(zip(first_names, _to_bf16([sharded[k][0] for k in first_names], "to_bf16_first")))
    spread_c = _Spread([c, conv_w[0]], [0, 1], "spread_c")
    g_first = _Gather([shard[k] for k in first_names], [kind[k] for k in first_names], "gather_first")
    _together("first_start", [(spread_c, "start"), (g_first, "start")], c)
    shard.update(zip(later_names, _to_bf16([sharded[k][0] for k in later_names], "to_bf16_later", dep=g_first.token)))

    c_all, conv_w_full = spread_c.finish(g_first.token)
    mod_part = _ada_fwd(c_all, w_ada[0], b_ada_cols, g_first.token)
    vecs = _pack_vecs(conv_w_full, [conv_b, b_rg_a, b_rg_x, a_param, b_pool, pool_scale,
                                    norm_mix_g, norm_mlp_g, final_g.reshape(1, D)])
    spread_mod = _Spread([mod_part], [0], "spread_mod")
    spread_mod.start(vecs)
    g_mix = gather(mix_names, spread_mod.token, "gather_mix")
    g_branch = gather(branch_names, g_mix.token, "gather_branch")
    g_mlp = gather(mlp_names, g_branch.token, "gather_mlp")
    g_first.forward(g_mlp.token)
    wg = dict(zip(first_names, g_first.finish(g_first.token)))
    mod_parts, = spread_mod.finish(g_first.token)
    mod_all = jnp.transpose(mod_parts.reshape(N_DEV, N_DEV, n_ada), (1, 0, 2)).reshape(N_DEV, 6, D)
    modr = _pad_rows(lax.dynamic_index_in_dim(mod_all, me, 0, keepdims=False), 8)

    h1, x_rnn, u_pool, ga, dga, sa, sb = _proj_fwd(x2d, modr, vecs, wg["w_in"])
    _together("mixer_forward", [(g_mix, "forward"), (g_branch, "forward")], h1)
    wg.update(zip(mix_names, g_mix.finish(g_branch.token)))
    xr, hr, za, p, pooled, *gates = _mix_fwd(x_rnn, u_pool, ga, vecs, wg["w_rg_a"], wg["w_rg_x"], wg["w_pool"],
                                             dep=g_branch.token)
    g_mlp.forward(za)
    wg.update(zip(branch_names, g_branch.finish(g_mlp.token)))
    ba, bb, merged, o, x2, h2 = _branch_fwd(za, pooled, sa, sb, x2d, modr, vecs,
                                            wg["w_branch_a"], wg["w_branch_b"], wg["w_out"])
    wg.update(zip(mlp_names, g_mlp.finish(h2)))
    ru, dx3, d_dn, small_f = _mlp_fwd(h2, x2, target, modr, vecs, wg["w_up"], wg["w_down"])

    near = _near(lax.axis_index("x"), lax.axis_index("y"), lax.axis_index("c"))
    own4 = jnp.stack([me, near[1][1], near[2][1], near[3][1]]).astype(jnp.int32)

    def scatter(group, partial, after, name):
        return _Scatter([partial[k] for k in group], [kind[k] for k in group], after, name)

    dup, dx2, do, small_m = _mlp_bwd(d_dn, ru, x2, dx3, o, modr, vecs, wg["w_up"], wg["w_down"])
    partial = dict(w_up=_wgrad(h2, dup, "wgrad_up"), w_down=_wgrad(ru, d_dn, "wgrad_down", square_a=True))
    s_mlp = scatter(mlp_names, partial, dx2, "scatter_mlp")

    dba, dbb, dgates, dza, dpooled = _branch_bwd(do, sa, sb, ba, bb, wg["w_branch_a"], wg["w_branch_b"], wg["w_out"],
                                                 dep=s_mlp.token)
    s_mlp.combine_and_send(own4, dza)
    dproj, dw_rg_a, dw_rg_x, dw_pool, small_x = _mix_bwd(dza, dpooled, x_rnn, ga, dga, xr, hr, p, gates, dgates,
                                                         vecs, wg["w_rg_a"], wg["w_rg_x"], wg["w_pool"],
                                                         dep=s_mlp.token)
    partial.update(w_branch_a=_wgrad(za, dba, "wgrad_branch_a"), w_branch_b=_wgrad(pooled, dbb, "wgrad_branch_b"),
                   w_out=_wgrad(merged, do, "wgrad_out"),
                   w_rg_a=dw_rg_a, w_rg_x=dw_rg_x, w_pool=dw_pool)
    mixer_names = ["w_rg_a", "w_rg_x", "w_pool", "w_branch_a", "w_branch_b", "w_out"]
    s_mixer = scatter(mixer_names, partial, s_mlp.token, "scatter_mixer")

    partial["w_in"] = _wgrad(h1, dproj, "wgrad_in", dep=s_mixer.token)
    s_in = scatter(["w_in"], partial, s_mixer.token, "scatter_in")
    s_mixer.combine_and_send(own4, s_in.token)
    s_in.combine_and_send(own4, s_mixer.token)
    grad_x, small_p = _proj_bwd(dproj, x2d, dx2, modr, vecs, wg["w_in"], dep=s_in.token)

    locals_ = dict(w_in=(w_in, m_w_in, v_w_in), w_up=(w_up, m_w_up, v_w_up), w_down=(w_down, m_w_down, v_w_down),
                   w_branch_a=(w_branch_a, m_w_branch_a, v_w_branch_a),
                   w_branch_b=(w_branch_b, m_w_branch_b, v_w_branch_b), w_out=(w_out, m_w_out, v_w_out),
                   w_rg_a=(w_rg_a, m_w_rg_a, v_w_rg_a), w_rg_x=(w_rg_x, m_w_rg_x, v_w_rg_x),
                   w_pool=(w_pool, m_w_pool, v_w_pool))
    res = {}

    def finish(group, exchange, after, n_tiles, name):
        chip_sums, arrivals = exchange.finish(after)
        flat = lambda t: t.reshape(-1, t.shape[-1])
        shapes = [flat(locals_[k][0]).shape for k in group]
        outs = _adam_group([cs.reshape(N_NEAR, *sh) for cs, sh in zip(chip_sums, shapes)],
                           [ar.reshape(N_NEAR - 1, *sh) for ar, sh in zip(arrivals, shapes)],
                           *[[flat(locals_[k][j]) for k in group] for j in range(3)], n_tiles, name)
        for i, k in enumerate(group):
            res[k] = [outs[j * len(group) + i].reshape(locals_[k][0].shape) for j in range(4)]
        return res[group[-1]][0]

    small = jnp.concatenate([small_f, small_m, small_x, small_p], axis=0)
    g_small = _Spread([small], [0], "spread_small")
    g_small.start(grad_x)
    done = finish(mlp_names, s_mlp, g_small.token, 4, "adam_mlp")
    done = finish(mixer_names, s_mixer, done, 2, "adam_mixer")
    done = finish(["w_in"], s_in, done, 4, "adam_in")
    small_all, = g_small.finish(done)
    small_all = small_all.reshape(N_DEV, N_SMALL, D)

    conv_cols = lax.dynamic_slice(small_all, (0, 16, me * (D // N_DEV)), (N_DEV, 4, D // N_DEV))

    def smalls(ng, nl, cb, bra, brx, ap, bp, ps, fg, ba_, cw):
        return [ng, nl, cb, bra, brx, ap, bp, ps, fg.reshape(1, D), ba_, cw[0]]

    small_names = ["norm_mix_g", "norm_mlp_g", "conv_b", "b_rg_a", "b_rg_x", "a_param", "b_pool", "pool_scale",
                   "final_g", "b_ada", "conv_w"]
    fin = _small_finish(
        small_all, conv_cols, mod_all, vecs,
        smalls(norm_mix_g, norm_mlp_g, conv_b, b_rg_a, b_rg_x, a_param, b_pool, pool_scale, final_g, b_ada, conv_w),
        smalls(m_norm_mix_g, m_norm_mlp_g, m_conv_b, m_b_rg_a, m_b_rg_x, m_a_param, m_b_pool, m_pool_scale,
               m_final_g, m_b_ada, m_conv_w),
        smalls(v_norm_mix_g, v_norm_mlp_g, v_conv_b, v_b_rg_a, v_b_rg_x, v_a_param, v_b_pool, v_pool_scale,
               v_final_g, v_b_ada, v_conv_w))
    dmod_all, loss_tile = fin[4 * N_SMALL_PARAMS], fin[4 * N_SMALL_PARAMS + 1]
    dmod_cols = lax.dynamic_slice(dmod_all.reshape(N_DEV, 6 * D), (0, me * n_ada), (N_DEV, n_ada))
    res["w_ada"] = [t.reshape(w_ada.shape) for t in _ada_bwd_adam(c_all, dmod_cols, w_ada[0], m_w_ada[0], v_w_ada[0])]

    def final_shape(k, t):
        if k == "final_g":
            return t.reshape(D)
        if k == "conv_w":
            return t.reshape(conv_w.shape)
        return t

    for i, k in enumerate(small_names):
        res[k] = [final_shape(k, fin[which * N_SMALL_PARAMS + i]) for which in range(4)]
    order = ["norm_mix_g", "norm_mlp_g", "w_ada", "b_ada", "w_in", "conv_w", "conv_b", "w_rg_a", "b_rg_a", "w_rg_x",
             "b_rg_x", "a_param", "w_branch_a", "w_pool", "b_pool", "pool_scale", "w_branch_b", "w_out", "w_up",
             "w_down", "final_g"]
    outs = [loss_tile[0, 0], grad_x.reshape(x.shape)]
    for which in range(4):
        for k in order:
            outs.append(res[k][which])
    return tuple(outs)
```

```python
import functools

import jax
import jax.numpy as jnp
from jax import lax
from jax.experimental import pallas as pl
from jax.experimental.pallas import tpu as pltpu

F32 = jnp.float32
BF16 = jnp.bfloat16
MESH = pl.DeviceIdType.MESH

N_DEV = 8
D = 1024
N_GROUPS = 4
GW = D // N_GROUPS
D_IN = 5 * D
D_FF = 4 * D
POOL_WINDOWS = (2, 4, 8, 16)
HALO_X = 8
HALO_U = 16
EPS = 1e-6
C_RG = 8.0
ADAM_LR, ADAM_B1, ADAM_B2, ADAM_EPS, ADAM_WD, ADAM_STEP = 0.001, 0.9, 0.999, 1e-08, 0.01, 10

V7X_VMEM_LIMIT = 56 * 1024 * 1024

V_CONV_W, V_CONV_B, V_B_RG_A, V_B_RG_X, V_A_PARAM, V_B_POOL, V_POOL_SCALE, V_G1, V_G2, V_GF = 0, 4, 5, 6, 7, 8, 9, 10, 11, 12
M_SH1, M_SC1, M_GT1, M_SH2, M_SC2, M_GT2 = 0, 1, 2, 3, 4, 5

TM_PROJ = 512
TM_MIX = 256
TM_BRANCH = 512
TM_MLP = 512
TM_MLP_BWD = 256
TS_WGRAD = 1024


def _params(semantics):
    return pltpu.CompilerParams(dimension_semantics=semantics, vmem_limit_bytes=V7X_VMEM_LIMIT)


def _resident(shape):
    return pl.BlockSpec(shape, lambda *_: (0,) * len(shape), pipeline_mode=pl.Buffered(1))


def _dot(a, b):
    return jnp.dot(a, b, preferred_element_type=F32)


def _dot_nt(a, b):
    return lax.dot_general(a, b, (((1,), (1,)), ((), ())), preferred_element_type=F32)


def _dot_tn(a, b):
    return lax.dot_general(a, b, (((0,), (0,)), ((), ())), preferred_element_type=F32)


def _sigmoid(x):
    return 0.5 * jnp.tanh(0.5 * x) + 0.5


def _sigmoid_tail(x):
    return 1.0 / (1.0 + jnp.exp(-x))


def _gelu_and_grad(x):
    k = 0.7978845608028654
    x2 = x * x
    t = jnp.tanh(k * (x + 0.044715 * x * x2))
    g = 0.5 * x * (1.0 + t)
    dg = 0.5 * (1.0 + t) + 0.5 * x * (1.0 - t * t) * (k * (1.0 + 3.0 * 0.044715 * x2))
    return g, dg


def _softplus(a):
    e = jnp.exp(-jnp.abs(a))
    u = 1.0 + e
    log1p_e = jnp.where(u == 1.0, e, jnp.log(u) * e / jnp.where(u == 1.0, 1.0, u - 1.0))
    return jnp.maximum(a, 0.0) + log1p_e


def _neg_expm1(z):
    series = -(z * (1.0 + z * (0.5 + z * (1.0 / 6.0 + z * (1.0 / 24.0 + z * (1.0 / 120.0))))))
    return jnp.where(z > -0.1, series, 1.0 - jnp.exp(z))


def _shift_down(x, k):
    return pltpu.roll(x, k, 0)


def _shift_up(x, k):
    return pltpu.roll(x, x.shape[0] - k, 0)


def _rglru_gates(xr, w_a, w_x, b_a, b_x, a_param, is_t0):
    xb = xr.astype(BF16)
    ra = _sigmoid(_dot(xb, w_a) + b_a)
    ri = _sigmoid(_dot(xb, w_x) + b_x)
    sp = _softplus(a_param)
    log_a = (-C_RG) * ra * sp
    a = jnp.exp(log_a)
    mult = jnp.where(is_t0, 1.0, jnp.sqrt(_neg_expm1(2.0 * log_a)))
    return ra, ri, sp, a, mult


SUBLANES = 8


LANES = 128


def _scan_strip(a, b, carry, scr, down):
    t = b.shape[0]
    g = t // SUBLANES
    a3 = a.reshape(g, SUBLANES, LANES)
    b3 = b.reshape(g, SUBLANES, LANES)
    sub = lax.broadcasted_iota(jnp.int32, (g, SUBLANES, LANES), 1)
    for k in (1, 2, 4):
        keep = sub >= k if down else sub < SUBLANES - k
        shift = k if down else SUBLANES - k
        b3 = b3 + a3 * jnp.where(keep, pltpu.roll(b3, shift, 1), 0.0)
        a3 = a3 * jnp.where(keep, pltpu.roll(a3, shift, 1), 1.0)
    scr[0] = a3.reshape(t, LANES)
    scr[1] = b3.reshape(t, LANES)
    end_row = SUBLANES - 1 if down else 0
    ag = scr[0, pl.ds(end_row, g, stride=SUBLANES), :]
    bg = scr[1, pl.ds(end_row, g, stride=SUBLANES), :]
    rg = lax.broadcasted_iota(jnp.int32, (g, LANES), 0)
    edge = 0 if down else g - 1
    bg = bg + jnp.where(rg == edge, ag * carry, 0.0)
    k = 1
    while k < g:
        keep = rg >= k if down else rg < g - k
        shift = k if down else g - k
        bg = bg + ag * jnp.where(keep, pltpu.roll(bg, shift, 0), 0.0)
        if 2 * k < g:
            ag = ag * pltpu.roll(ag, shift, 0)
        k *= 2
    entering = jnp.where(rg != edge, pltpu.roll(bg, 1 if down else g - 1, 0), carry)
    for r in range(SUBLANES):
        scr[2, pl.ds(r, g, stride=SUBLANES), :] = entering
    return scr[1] + scr[0] * scr[2], bg[g - 1:g, :]


def _scan_strips(a, b, carry, scr, down):
    outs = [_scan_strip(a[:, c:c + LANES], b[:, c:c + LANES], carry[:, c:c + LANES], scr, down)
            for c in range(0, b.shape[1], LANES)]
    return jnp.concatenate([o[0] for o in outs], axis=1), jnp.concatenate([o[1] for o in outs], axis=1)


def _scan_down(a, b, carry, scr):
    return _scan_strips(a, b, carry, scr, True)


def _scan_up(m, b, carry, scr):
    return _scan_strips(m, b, carry, scr, False)[0]


def _window_mean(sums, window, first_block, head_t):
    scaled = sums * (1.0 / window)
    head = jnp.where(first_block, sums[:HALO_U] / jnp.minimum(head_t, float(window)), scaled[:HALO_U])
    return jnp.concatenate([head, scaled[HALO_U:]], axis=0)


def _conv_taps(x_ext):
    return [_shift_down(x_ext, 3 - j)[HALO_X:] if j < 3 else x_ext[HALO_X:] for j in range(4)]


def _proj_fwd(x, modr, vecs, w_in):
    s = x.shape[0]
    tm = min(TM_PROJ, s)

    def body(x_ref, mod_ref, vec_ref, w_ref, h1_ref, xrnn_ref, u_ref, ga_ref, dga_ref, sa_ref, sb_ref):
        xv = x_ref[...]
        r = lax.rsqrt(jnp.mean(xv * xv, axis=-1, keepdims=True) + EPS)
        gain = vec_ref[V_G1:V_G1 + 1, :] * (1.0 + mod_ref[M_SC1:M_SC1 + 1, :])
        h = (xv * r * gain + mod_ref[M_SH1:M_SH1 + 1, :]).astype(BF16)
        h1_ref[...] = h
        xrnn_ref[...] = _dot(h, w_ref[:, 0:D])
        ga_ref[...], dga_ref[...] = _gelu_and_grad(_dot(h, w_ref[:, D:2 * D]))
        u_ref[...] = _dot(h, w_ref[:, 2 * D:3 * D])
        sa_ref[...] = _sigmoid(_dot(h, w_ref[:, 3 * D:4 * D]))
        sb_ref[...] = _sigmoid(_dot(h, w_ref[:, 4 * D:5 * D]))

    tok = pl.BlockSpec((tm, D), lambda i: (i, 0))
    sd = lambda dt: jax.ShapeDtypeStruct((s, D), dt)
    return pl.pallas_call(
        body, name="proj_fwd", grid=(s // tm,),
        in_specs=[tok, pl.BlockSpec((8, D), lambda i: (0, 0)), pl.BlockSpec((16, D), lambda i: (0, 0)),
                  _resident((D, D_IN))],
        out_specs=[tok] * 7,
        out_shape=[sd(BF16)] + [sd(F32)] * 6,
        compiler_params=_params(("parallel",)),
    )(x, modr, vecs, w_in)


def _mix_fwd(x_rnn, u_pool, ga, vecs, w_rg_a, w_rg_x, w_pool, dep):
    s = x_rnn.shape[0]
    tm = min(TM_MIX, s)
    nb = s // tm

    def body(xh_ref, x_ref, uh_ref, u_ref, ga_ref, vec_ref, wa_ref, wx_ref, wp_ref, dep_ref,
             xr_ref, hr_ref, za_ref, p_ref, pooled_ref, a_ref, mult_ref, ra_ref, ri_ref, carry_ref, scan_scr):
        i = pl.program_id(0)
        first = i == 0

        @pl.when(first)
        def _():
            carry_ref[...] = jnp.zeros_like(carry_ref)

        row = lax.broadcasted_iota(jnp.int32, (tm, GW), 0)
        is_t0 = jnp.logical_and(first, row == 0)
        head_t = (lax.broadcasted_iota(jnp.int32, (HALO_U, GW), 0) + 1).astype(F32)
        for g in range(N_GROUPS):
            cs = slice(g * GW, (g + 1) * GW)
            vec = vec_ref[:, cs]
            xh = jnp.where(first, 0.0, xh_ref[:, cs])
            taps = _conv_taps(jnp.concatenate([xh, x_ref[:, cs]], axis=0))
            xr = vec[V_CONV_B:V_CONV_B + 1]
            for j in range(4):
                xr = xr + vec[V_CONV_W + j:V_CONV_W + j + 1] * taps[j]
            xr_ref[:, cs] = xr
            ra, ri, _, a, mult = _rglru_gates(
                xr, wa_ref[g], wx_ref[g], vec[V_B_RG_A:V_B_RG_A + 1], vec[V_B_RG_X:V_B_RG_X + 1],
                vec[V_A_PARAM:V_A_PARAM + 1], is_t0)
            a_ref[:, cs] = a
            mult_ref[:, cs] = mult
            ra_ref[:, cs] = ra.astype(BF16)
            ri_ref[:, cs] = ri.astype(BF16)
            h, last = _scan_down(a, xr * ri * mult, carry_ref[0:1, cs], scan_scr)
            hr_ref[:, cs] = h
            carry_ref[0:1, cs] = last
            za_ref[:, cs] = (ga_ref[:, cs] * h).astype(BF16)
            uh = jnp.where(first, 0.0, uh_ref[:, cs])
            sm = jnp.concatenate([uh, u_ref[:, cs]], axis=0)
            k = 1
            while k < POOL_WINDOWS[g]:
                sm = sm + _shift_down(sm, k)
                k *= 2
            mean = _window_mean(sm[HALO_U:], POOL_WINDOWS[g], first, head_t)
            p = (mean - u_ref[:, cs]).astype(BF16)
            p_ref[:, cs] = p
            pb = _dot(p, wp_ref[g]) + vec[V_B_POOL:V_B_POOL + 1]
            pooled_ref[:, cs] = (pb * vec[V_POOL_SCALE:V_POOL_SCALE + 1]).astype(BF16)

    tok = pl.BlockSpec((tm, D), lambda i: (i, 0))
    halo = lambda rows: pl.BlockSpec((rows, D), lambda i: (jnp.maximum(i * (tm // rows) - 1, 0), 0))
    wspec = pl.BlockSpec((N_GROUPS, GW, GW), lambda i: (0, 0, 0))
    sd = lambda dt: jax.ShapeDtypeStruct((s, D), dt)
    return pl.pallas_call(
        body, name="mix_fwd", grid=(nb,),
        in_specs=[halo(HALO_X), tok, halo(HALO_U), tok, tok, pl.BlockSpec((16, D), lambda i: (0, 0)),
                  wspec, wspec, wspec, pl.BlockSpec(memory_space=pl.ANY)],
        out_specs=[tok] * 9,
        out_shape=[sd(F32), sd(F32), sd(BF16), sd(BF16), sd(BF16), sd(F32), sd(F32), sd(BF16), sd(BF16)],
        scratch_shapes=[pltpu.VMEM((8, D), F32), pltpu.VMEM((3, tm, LANES), F32)],
        compiler_params=_params(("arbitrary",)),
    )(x_rnn, x_rnn, u_pool, u_pool, ga, vecs, w_rg_a, w_rg_x, w_pool, dep)


def _branch_fwd(za, pooled, sa, sb, x, modr, vecs, w_a, w_b, w_out):
    s = x.shape[0]
    tm = min(TM_BRANCH, s)

    def body(za_ref, pooled_ref, sa_ref, sb_ref, x_ref, mod_ref, vec_ref, wa_ref, wb_ref, wo_ref,
             ba_ref, bb_ref, merged_ref, o_ref, x2_ref, h2_ref):
        ba = _dot(za_ref[...], wa_ref[...])
        bb = _dot(pooled_ref[...], wb_ref[...])
        ba_ref[...] = ba.astype(BF16)
        bb_ref[...] = bb.astype(BF16)
        merged = (sa_ref[...] * ba + sb_ref[...] * bb).astype(BF16)
        merged_ref[...] = merged
        o = _dot(merged, wo_ref[...])
        o_ref[...] = o.astype(BF16)
        x2 = x_ref[...] + mod_ref[M_GT1:M_GT1 + 1, :] * o
        x2_ref[...] = x2
        r = lax.rsqrt(jnp.mean(x2 * x2, axis=-1, keepdims=True) + EPS)
        gain = vec_ref[V_G2:V_G2 + 1, :] * (1.0 + mod_ref[M_SC2:M_SC2 + 1, :])
        h2_ref[...] = (x2 * r * gain + mod_ref[M_SH2:M_SH2 + 1, :]).astype(BF16)

    tok = pl.BlockSpec((tm, D), lambda i: (i, 0))
    wspec = pl.BlockSpec((D, D), lambda i: (0, 0))
    sd = lambda dt: jax.ShapeDtypeStruct((s, D), dt)
    return pl.pallas_call(
        body, name="branch_fwd", grid=(s // tm,),
        in_specs=[tok, tok, tok, tok,
                  tok, pl.BlockSpec((8, D), lambda i: (0, 0)), pl.BlockSpec((16, D), lambda i: (0, 0)),
                  wspec, wspec, wspec],
        out_specs=[tok] * 6,
        out_shape=[sd(BF16), sd(BF16), sd(BF16), sd(BF16), sd(F32), sd(BF16)],
        compiler_params=_params(("parallel",)),
    )(za, pooled, sa, sb, x, modr, vecs, w_a, w_b, w_out)


def _mlp_fwd(h2, x2, target, modr, vecs, w_up, w_down):
    s = x2.shape[0]
    tm = min(TM_MLP, s)

    def body(h2_ref, x2_ref, tgt_ref, mod_ref, vec_ref, wu_ref, wd_ref,
             ru_ref, dx3_ref, ddn_ref, small_ref):
        @pl.when(pl.program_id(0) == 0)
        def _():
            small_ref[...] = jnp.zeros_like(small_ref)

        h2 = h2_ref[...]
        dn = None
        for c in range(D_FF // D):
            cs = slice(c * D, (c + 1) * D)
            ru = jnp.maximum(_dot(h2, wu_ref[:, cs]), 0.0)
            ru_ref[:, cs] = ru.astype(BF16)
            part = _dot((ru * ru).astype(BF16), wd_ref[cs, :])
            dn = part if dn is None else dn + part
        gt2 = mod_ref[M_GT2:M_GT2 + 1, :]
        gf = vec_ref[V_GF:V_GF + 1, :]
        x3 = x2_ref[...] + gt2 * dn
        r3 = lax.rsqrt(jnp.mean(x3 * x3, axis=-1, keepdims=True) + EPS)
        n3 = x3 * r3
        err = n3 * gf - tgt_ref[...]
        dy = err * (1.0 / D)
        dn3 = dy * gf
        dx3 = r3 * (dn3 - n3 * jnp.mean(dn3 * n3, axis=-1, keepdims=True))
        dx3_ref[...] = dx3
        ddn_ref[...] = (dx3 * gt2).astype(BF16)
        small_ref[0:1, :] += jnp.sum(dy * n3, axis=0, keepdims=True)
        small_ref[1:2, :] += jnp.sum(dx3 * dn, axis=0, keepdims=True)
        small_ref[2:3, :] += (0.5 / D) * jnp.sum(err * err, axis=0, keepdims=True)

    tok = pl.BlockSpec((tm, D), lambda i: (i, 0))
    return pl.pallas_call(
        body, name="mlp_fwd", grid=(s // tm,),
        in_specs=[tok, tok, tok,
                  pl.BlockSpec((8, D), lambda i: (0, 0)), pl.BlockSpec((16, D), lambda i: (0, 0)),
                  _resident((D, D_FF)), _resident((D_FF, D))],
        out_specs=[pl.BlockSpec((tm, D_FF), lambda i: (i, 0)), tok, tok,
                   pl.BlockSpec((8, D), lambda i: (0, 0))],
        out_shape=[jax.ShapeDtypeStruct((s, D_FF), BF16), jax.ShapeDtypeStruct((s, D), F32),
                   jax.ShapeDtypeStruct((s, D), BF16), jax.ShapeDtypeStruct((8, D), F32)],
        compiler_params=_params(("arbitrary",)),
    )(h2, x2, target, modr, vecs, w_up, w_down)


def _mlp_bwd(d_dn, ru, x2, dx3, o, modr, vecs, w_up, w_down):
    s = x2.shape[0]
    tm = min(TM_MLP_BWD, s)

    def body(ddn_ref, ru_ref, x2_ref, dx3_ref, o_ref, mod_ref, vec_ref, wu_ref, wd_ref,
             dup_ref, dx2_ref, do_ref, small_ref):
        @pl.when(pl.program_id(0) == 0)
        def _():
            small_ref[...] = jnp.zeros_like(small_ref)

        ddn = ddn_ref[...]
        dh2 = None
        for c in range(D_FF // D):
            cs = slice(c * D, (c + 1) * D)
            dff = _dot_nt(ddn, wd_ref[cs, :])
            dup = (dff * (2.0 * ru_ref[:, cs].astype(F32))).astype(BF16)
            dup_ref[:, cs] = dup
            part = _dot_nt(dup, wu_ref[:, cs])
            dh2 = part if dh2 is None else dh2 + part
        x2 = x2_ref[...]
        r2 = lax.rsqrt(jnp.mean(x2 * x2, axis=-1, keepdims=True) + EPS)
        xn2 = x2 * r2
        gain = vec_ref[V_G2:V_G2 + 1, :] * (1.0 + mod_ref[M_SC2:M_SC2 + 1, :])
        dxn2 = dh2 * gain
        dx2 = dx3_ref[...] + r2 * (dxn2 - xn2 * jnp.mean(dxn2 * xn2, axis=-1, keepdims=True))
        dx2_ref[...] = dx2
        do_ref[...] = (dx2 * mod_ref[M_GT1:M_GT1 + 1, :]).astype(BF16)
        small_ref[0:1, :] += jnp.sum(dh2, axis=0, keepdims=True)
        small_ref[1:2, :] += jnp.sum(dh2 * xn2, axis=0, keepdims=True)
        small_ref[2:3, :] += jnp.sum(dx2 * o_ref[...].astype(F32), axis=0, keepdims=True)

    tok = pl.BlockSpec((tm, D), lambda i: (i, 0))
    wide = pl.BlockSpec((tm, D_FF), lambda i: (i, 0))
    return pl.pallas_call(
        body, name="mlp_bwd", grid=(s // tm,),
        in_specs=[tok, wide, tok, tok, tok,
                  pl.BlockSpec((8, D), lambda i: (0, 0)), pl.BlockSpec((16, D), lambda i: (0, 0)),
                  _resident((D, D_FF)), _resident((D_FF, D))],
        out_specs=[wide, tok, tok, pl.BlockSpec((8, D), lambda i: (0, 0))],
        out_shape=[jax.ShapeDtypeStruct((s, D_FF), BF16), jax.ShapeDtypeStruct((s, D), F32),
                   jax.ShapeDtypeStruct((s, D), BF16), jax.ShapeDtypeStruct((8, D), F32)],
        compiler_params=_params(("arbitrary",)),
    )(d_dn, ru, x2, dx3, o, modr, vecs, w_up, w_down)


def _branch_bwd(do, sa, sb, ba, bb, w_a, w_b, w_out, dep):
    s = do.shape[0]
    tm = min(TM_BRANCH, s)

    def body(do_ref, sa_ref, sb_ref, ba_ref, bb_ref, wa_ref, wb_ref, wo_ref, dep_ref,
             dba_ref, dbb_ref, dg_ref, dza_ref, dpooled_ref):
        dmerged = _dot_nt(do_ref[...], wo_ref[...])
        sa = sa_ref[...]
        sb = sb_ref[...]
        dba = (dmerged * sa).astype(BF16)
        dbb = (dmerged * sb).astype(BF16)
        dba_ref[...] = dba
        dbb_ref[...] = dbb
        dg_ref[:, :D] = (dmerged * ba_ref[...].astype(F32) * sa * (1.0 - sa)).astype(BF16)
        dg_ref[:, D:] = (dmerged * bb_ref[...].astype(F32) * sb * (1.0 - sb)).astype(BF16)
        dza_ref[...] = _dot_nt(dba, wa_ref[...])
        dpooled_ref[...] = _dot_nt(dbb, wb_ref[...])

    tok = pl.BlockSpec((tm, D), lambda i: (i, 0))
    wspec = pl.BlockSpec((D, D), lambda i: (0, 0))
    sd = lambda dt: jax.ShapeDtypeStruct((s, D), dt)
    return pl.pallas_call(
        body, name="branch_bwd", grid=(s // tm,),
        in_specs=[tok, tok, tok, tok, tok, wspec, wspec, wspec, pl.BlockSpec(memory_space=pl.ANY)],
        out_specs=[tok, tok, pl.BlockSpec((tm, 2 * D), lambda i: (i, 0)), tok, tok],
        out_shape=[sd(BF16), sd(BF16), jax.ShapeDtypeStruct((s, 2 * D), BF16), sd(F32), sd(F32)],
        compiler_params=_params(("parallel",)),
    )(do, sa, sb, ba, bb, w_a, w_b, w_out, dep)


def _mix_bwd(dza, dpooled, x_rnn, ga, dga, xr, hr, p, gates, dgates, vecs, w_rg_a, w_rg_x, w_pool, dep):
    s = xr.shape[0]
    tm = min(TM_MIX, s)
    nb = s // tm

    def body(dza_ref, dpooled_ref, xh_ref, x_ref, ga_ref, dga_ref, xr_ref, hh_ref, hr_ref, p_ref,
             a_ref, mult_ref, ra_ref, ri_ref, dg_ref, vec_ref, wa_ref, wx_ref, wp_ref, dep_ref,
             dproj_ref, dwa_ref, dwx_ref, dwp_ref, small_ref,
             scan_carry, dxr_carry, q_carry, scan_scr, dwa_acc, dwx_acc, dwp_acc):
        i = pl.program_id(0)
        bi = nb - 1 - i
        first_t = bi == 0

        @pl.when(i == 0)
        def _():
            scan_carry[...] = jnp.zeros_like(scan_carry)
            dxr_carry[...] = jnp.zeros_like(dxr_carry)
            q_carry[...] = jnp.zeros_like(q_carry)
            dwa_acc[...] = jnp.zeros_like(dwa_acc)
            dwx_acc[...] = jnp.zeros_like(dwx_acc)
            dwp_acc[...] = jnp.zeros_like(dwp_acc)
            small_ref[...] = jnp.zeros_like(small_ref)

        row = lax.broadcasted_iota(jnp.int32, (tm, GW), 0)
        is_t0 = jnp.logical_and(first_t, row == 0)
        head_t = (lax.broadcasted_iota(jnp.int32, (HALO_U, GW), 0) + 1).astype(F32)
        colsum = lambda v: jnp.sum(v, axis=0, keepdims=True)
        for g in range(N_GROUPS):
            cs = slice(g * GW, (g + 1) * GW)
            vec = vec_ref[:, cs]
            xr = xr_ref[:, cs]
            hr = hr_ref[:, cs]
            dza = dza_ref[:, cs]
            dproj_ref[:, D + g * GW:D + (g + 1) * GW] = (dza * hr * dga_ref[:, cs]).astype(BF16)
            dhr = dza * ga_ref[:, cs]
            a = a_ref[:, cs]
            mult = mult_ref[:, cs]
            ra = ra_ref[:, cs].astype(F32)
            ri = ri_ref[:, cs].astype(F32)
            sp = _softplus(vec[V_A_PARAM:V_A_PARAM + 1])
            m = jnp.where(row == tm - 1, 1.0, _shift_up(a, 1))
            gsum = _scan_up(m, dhr, scan_carry[0:1, cs], scan_scr)
            scan_carry[0:1, cs] = a[0:1, :] * gsum[0:1, :]
            hh = jnp.where(first_t, 0.0, hh_ref[:, cs])
            hprev = _shift_down(jnp.concatenate([hh, hr], axis=0), 1)[8:]
            da = gsum * hprev
            dmult = jnp.where(is_t0, 0.0, gsum * xr * ri)
            dlog_a = da * a - dmult * a * a / mult
            dri = gsum * xr * mult
            dxr = gsum * ri * mult
            small_ref[7:8, cs] += colsum((-C_RG) * ra * dlog_a)
            dpa = (((-C_RG) * sp) * dlog_a * ra * (1.0 - ra))
            dpx = dri * ri * (1.0 - ri)
            small_ref[5:6, cs] += colsum(dpa)
            small_ref[6:7, cs] += colsum(dpx)
            dpa = dpa.astype(BF16)
            dpx = dpx.astype(BF16)
            xrb = xr.astype(BF16)
            dwa_acc[g] += _dot_tn(xrb, dpa)
            dwx_acc[g] += _dot_tn(xrb, dpx)
            dxr = dxr + _dot_nt(dpa, wa_ref[g]) + _dot_nt(dpx, wx_ref[g])
            small_ref[4:5, cs] += colsum(dxr)
            xh = jnp.where(first_t, 0.0, xh_ref[:, cs])
            taps = _conv_taps(jnp.concatenate([xh, x_ref[:, cs]], axis=0))
            dxr_ext = jnp.concatenate([dxr, dxr_carry[:, cs]], axis=0)
            dx = vec[V_CONV_W + 3:V_CONV_W + 4] * dxr
            for j in range(4):
                small_ref[j:j + 1, cs] += colsum(dxr * taps[j])
                if j < 3:
                    dx = dx + vec[V_CONV_W + j:V_CONV_W + j + 1] * _shift_up(dxr_ext, 3 - j)[:tm]
            dxr_carry[:, cs] = dxr[0:8, :]
            dproj_ref[:, cs] = dx.astype(BF16)
            pg = p_ref[:, cs]
            dpooled = dpooled_ref[:, cs]
            pb = _dot(pg, wp_ref[g]) + vec[V_B_POOL:V_B_POOL + 1]
            small_ref[9:10, cs] += colsum(dpooled * pb)
            dpb = dpooled * vec[V_POOL_SCALE:V_POOL_SCALE + 1]
            small_ref[8:9, cs] += colsum(dpb)
            dpbb = dpb.astype(BF16)
            dwp_acc[g] += _dot_tn(pg, dpbb)
            dp = _dot_nt(dpbb, wp_ref[g])
            q = _window_mean(dp, POOL_WINDOWS[g], first_t, head_t)
            sm = jnp.concatenate([q, q_carry[:, cs]], axis=0)
            k = 1
            while k < POOL_WINDOWS[g]:
                sm = sm + _shift_up(sm, k)
                k *= 2
            q_carry[:, cs] = q[0:HALO_U, :]
            dproj_ref[:, 2 * D + g * GW:2 * D + (g + 1) * GW] = (sm[:tm] - dp).astype(BF16)
        dproj_ref[:, 3 * D:] = dg_ref[...]

        @pl.when(i == nb - 1)
        def _():
            dwa_ref[...] = dwa_acc[...].astype(BF16)
            dwx_ref[...] = dwx_acc[...].astype(BF16)
            dwp_ref[...] = dwp_acc[...].astype(BF16)

    rev = lambda i: nb - 1 - i
    tok = pl.BlockSpec((tm, D), lambda i: (rev(i), 0))
    halo8 = lambda k: pl.BlockSpec((8, D), lambda i: (jnp.maximum(rev(i) * (tm // 8) - 1, 0), k))
    wspec = pl.BlockSpec((N_GROUPS, GW, GW), lambda i: (0, 0, 0))
    wshape = jax.ShapeDtypeStruct((N_GROUPS, GW, GW), BF16)
    return pl.pallas_call(
        body, name="mix_bwd", grid=(nb,),
        in_specs=[tok, tok, halo8(0), tok, tok, tok, tok, halo8(0), tok, tok, tok, tok, tok, tok,
                  pl.BlockSpec((tm, 2 * D), lambda i: (rev(i), 0)),
                  pl.BlockSpec((16, D), lambda i: (0, 0)), wspec, wspec, wspec, pl.BlockSpec(memory_space=pl.ANY)],
        out_specs=[pl.BlockSpec((tm, D_IN), lambda i: (rev(i), 0)), wspec, wspec, wspec,
                   pl.BlockSpec((16, D), lambda i: (0, 0))],
        out_shape=[jax.ShapeDtypeStruct((s, D_IN), BF16), wshape, wshape, wshape,
                   jax.ShapeDtypeStruct((16, D), F32)],
        scratch_shapes=[pltpu.VMEM((8, D), F32), pltpu.VMEM((8, D), F32), pltpu.VMEM((HALO_U, D), F32),
                        pltpu.VMEM((3, tm, LANES), F32)] + [pltpu.VMEM((N_GROUPS, GW, GW), F32)] * 3,
        compiler_params=_params(("arbitrary",)),
    )(dza, dpooled, x_rnn, x_rnn, ga, dga, xr, hr, hr, p, *gates, dgates, vecs, w_rg_a, w_rg_x, w_pool, dep)


def _proj_bwd(dproj, x, dx2, modr, vecs, w_in, dep):
    s = x.shape[0]
    tm = min(TM_PROJ, s)

    def body(dp_ref, x_ref, dx2_ref, mod_ref, vec_ref, w_ref, dep_ref, gx_ref, small_ref):
        @pl.when(pl.program_id(0) == 0)
        def _():
            small_ref[...] = jnp.zeros_like(small_ref)

        dh1 = None
        for c in range(D_IN // D):
            cs = slice(c * D, (c + 1) * D)
            part = _dot_nt(dp_ref[:, cs], w_ref[:, cs])
            dh1 = part if dh1 is None else dh1 + part
        xv = x_ref[...]
        r1 = lax.rsqrt(jnp.mean(xv * xv, axis=-1, keepdims=True) + EPS)
        xn1 = xv * r1
        gain = vec_ref[V_G1:V_G1 + 1, :] * (1.0 + mod_ref[M_SC1:M_SC1 + 1, :])
        dxn1 = dh1 * gain
        gx_ref[...] = dx2_ref[...] + r1 * (dxn1 - xn1 * jnp.mean(dxn1 * xn1, axis=-1, keepdims=True))
        small_ref[0:1, :] += jnp.sum(dh1, axis=0, keepdims=True)
        small_ref[1:2, :] += jnp.sum(dh1 * xn1, axis=0, keepdims=True)

    tok = pl.BlockSpec((tm, D), lambda i: (i, 0))
    return pl.pallas_call(
        body, name="proj_bwd", grid=(s // tm,),
        in_specs=[pl.BlockSpec((tm, D_IN), lambda i: (i, 0)), tok, tok,
                  pl.BlockSpec((8, D), lambda i: (0, 0)), pl.BlockSpec((16, D), lambda i: (0, 0)),
                  _resident((D, D_IN)), pl.BlockSpec(memory_space=pl.ANY)],
        out_specs=[tok, pl.BlockSpec((8, D), lambda i: (0, 0))],
        out_shape=[jax.ShapeDtypeStruct((s, D), F32), jax.ShapeDtypeStruct((8, D), F32)],
        compiler_params=_params(("arbitrary",)),
    )(dproj, x, dx2, modr, vecs, w_in, dep)


def _wgrad(a, b, name, square_a=False, dep=None):
    s, ka = a.shape
    n = b.shape[1]
    tka = ka if ka <= 1024 else ka // 2
    tn = n if n <= 1024 else n // 2
    ts = min(TS_WGRAD, s)
    ns = s // ts
    nc = 512
    deps = [] if dep is None else [dep]

    def body(a_ref, b_ref, *refs):
        out_ref, acc_ref = refs[-2:]
        t = pl.program_id(2)

        @pl.when(t == 0)
        def _():
            acc_ref[...] = jnp.zeros_like(acc_ref)

        av = a_ref[...]
        if square_a:
            af = av.astype(F32)
            av = (af * af).astype(BF16)
        for c in range(tn // nc):
            cs = slice(c * nc, (c + 1) * nc)
            acc_ref[:, cs] += _dot_tn(av, b_ref[:, cs])

        @pl.when(t == ns - 1)
        def _():
            out_ref[...] = acc_ref[...].astype(BF16)

    return pl.pallas_call(
        body, name=name, grid=(ka // tka, n // tn, ns),
        in_specs=[pl.BlockSpec((ts, tka), lambda i, j, t: (t, i)),
                  pl.BlockSpec((ts, tn), lambda i, j, t: (t, j))] + [pl.BlockSpec(memory_space=pl.ANY)] * len(deps),
        out_specs=pl.BlockSpec((tka, tn), lambda i, j, t: (i, j)),
        out_shape=jax.ShapeDtypeStruct((ka, n), BF16),
        scratch_shapes=[pltpu.VMEM((tka, tn), F32)],
        compiler_params=_params(("parallel", "parallel", "arbitrary")),
    )(a, b, *deps)


def _wgrad_square(pairs, name):
    n_pairs = len(pairs)
    s = pairs[0][0].shape[0]
    ts = min(TS_WGRAD, s)
    ns = s // ts
    nc = 512

    def body(*refs):
        in_refs, out_refs, acc_ref = refs[:2 * n_pairs], refs[2 * n_pairs:3 * n_pairs], refs[-1]
        g = pl.program_id(0)
        for k in range(n_pairs):
            @pl.when(g == k * ns)
            def _():
                acc_ref[...] = jnp.zeros_like(acc_ref)

            @pl.when(g // ns == k)
            def _(k=k):
                av = in_refs[2 * k][...]
                for c in range(D // nc):
                    cs = slice(c * nc, (c + 1) * nc)
                    acc_ref[:, cs] += _dot_tn(av, in_refs[2 * k + 1][:, cs])

            @pl.when(g == k * ns + ns - 1)
            def _(k=k):
                out_refs[k][...] = acc_ref[...].astype(BF16)

    def token_block(k):
        return pl.BlockSpec((ts, D), lambda g: (jnp.clip(g - k * ns, 0, ns - 1), 0))

    return pl.pallas_call(
        body, name=name, grid=(n_pairs * ns,),
        in_specs=[token_block(k) for k in range(n_pairs) for _ in range(2)],
        out_specs=[pl.BlockSpec((D, D), lambda g: (0, 0))] * n_pairs,
        out_shape=[jax.ShapeDtypeStruct((D, D), BF16)] * n_pairs,
        scratch_shapes=[pltpu.VMEM((D, D), F32)],
        compiler_params=_params(("arbitrary",)),
    )(*[x for pair in pairs for x in pair])


def _window(ref, kind, idx, size):
    start = pl.multiple_of(idx * size, size)
    if kind == 0:
        return ref.at[pl.ds(start, size)]
    if kind == 1:
        return ref.at[:, pl.ds(start, size)]
    return ref.at[:, :, pl.ds(start, size)]


def _mesh_place():
    x, y, c = lax.axis_index("x"), lax.axis_index("y"), lax.axis_index("c")
    return x, y, c, 4 * x + 2 * y + c


def _peer(x, y, c, q):
    px = 1 - x if q & 4 else x
    py = 1 - y if q & 2 else y
    pc = 1 - c if q & 1 else c
    return (px, py, pc), 4 * px + 2 * py + pc


_HBM = pl.BlockSpec(memory_space=pltpu.HBM)
_SEM = pl.BlockSpec(memory_space=pltpu.SEMAPHORE)
_EFFECT = pltpu.SideEffectType.DATAFLOW_SIDE_EFFECTING


N_NEAR = 4


def _near(x, y, c):
    out = [((x, y, 1 - c), 4 * x + 2 * y + 1 - c)]
    for j in (1, 2, 3):
        px = 1 - x if j & 2 else x
        py = 1 - y if j & 1 else y
        out.append(((px, py, c), 4 * px + 2 * py + c))
    return out


def _remote(src, dst, send_sems, recv_sems, slot, device):
    return pltpu.make_async_remote_copy(src_ref=src, dst_ref=dst, send_sem=send_sems.at[slot], recv_sem=recv_sems.at[slot],
                                        device_id=device, device_id_type=MESH)


def _split_call(name, arrays, sems_in, n_new_sems, after, emit):
    na, ns, nn = len(arrays), len(sems_in), len(n_new_sems)

    def body(*refs):
        emit(refs[:na], refs[na:na + ns], refs[na + ns + 1:na + ns + 1 + nn])
        refs[-1][...] = jnp.zeros_like(refs[-1])

    outs = pl.pallas_call(
        body, name=name,
        out_shape=(*[pltpu.SemaphoreType.DMA((m,)) for m in n_new_sems],
                   *[pltpu.HBM(a.shape, a.dtype) for a in arrays], jax.ShapeDtypeStruct((8, 128), F32)),
        in_specs=[_HBM] * na + [_SEM] * ns + [pl.BlockSpec(memory_space=pl.ANY)],
        out_specs=(*[_SEM] * nn, *[_HBM] * na, pl.BlockSpec(memory_space=pltpu.VMEM)),
        input_output_aliases={i: nn + i for i in range(na)},
        compiler_params=pltpu.CompilerParams(has_side_effects=_EFFECT),
    )(*[pltpu.with_memory_space_constraint(a, pltpu.HBM) for a in arrays], *sems_in, after)
    return list(outs[:nn]), list(outs[nn:nn + na]), outs[-1]


def _together(name, steps, after):
    parts = [(ex.arrays, ex.sems, ex.new_sems[step], getattr(ex, "emit_" + step)) for ex, step in steps]

    def emit(arr, old, new):
        ia = io = ib = 0
        for arrays, sems, new_sems, emit_one in parts:
            emit_one(arr[ia:ia + len(arrays)], old[io:io + len(sems)], new[ib:ib + len(new_sems)])
            ia, io, ib = ia + len(arrays), io + len(sems), ib + len(new_sems)

    new, arrays, token = _split_call(name, [a for p in parts for a in p[0]], [s for p in parts for s in p[1]],
                                     [m for p in parts for m in p[2]], after, emit)
    out = []
    ia = ib = 0
    for (ex, _), (arrs, sems, new_sems, _) in zip(steps, parts):
        ex.arrays, ex.sems, ex.token = arrays[ia:ia + len(arrs)], [*sems, *new[ib:ib + len(new_sems)]], token
        ia, ib = ia + len(arrs), ib + len(new_sems)
        out.append(ex.arrays[ex.n:])
    return out


class _AllGather:
    def __init__(self, shards, kinds, name):
        self.n, self.kinds, self.name = len(shards), kinds, name
        self.sizes = [s.shape[k] for s, k in zip(shards, kinds)]
        lands = []
        for s, k in zip(shards, kinds):
            dims = list(s.shape)
            dims[k] *= N_DEV
            lands.append(lax.empty(tuple(dims), s.dtype))
        self.arrays, self.sems = [*shards, *lands], []

    def window(self, arr, k, idx):
        return _window(arr[self.n + k], self.kinds[k], idx, self.sizes[k])

    def start(self, after):
        _together(self.name + "_start", [(self, "start")], after)

    def forward(self, after):
        _together(self.name + "_forward", [(self, "forward")], after)

    def finish(self, after):
        return _together(self.name + "_finish", [(self, "finish")], after)[0]


class _Gather(_AllGather):
    def __init__(self, shards, kinds, name):
        super().__init__(shards, kinds, name)
        n = self.n
        self.new_sems = dict(start=[n * N_NEAR, n * N_NEAR, n], forward=[n * N_NEAR] * 2, finish=[])

    def emit_start(self, arr, _, new):
        x, y, c, me = _mesh_place()
        for k in range(self.n):
            pltpu.make_async_copy(arr[k], self.window(arr, k, me), new[2].at[k]).start()
        for k in range(self.n):
            for j, (dev, _) in enumerate(_near(x, y, c)):
                _remote(arr[k], self.window(arr, k, me), new[0], new[1], k * N_NEAR + j, dev).start()

    def emit_forward(self, arr, old, new):
        x, y, c, _ = _mesh_place()
        near = _near(x, y, c)
        for k in range(self.n):
            for j in (1, 2, 3):
                dev, idx = near[j]
                landed = self.window(arr, k, idx)
                _remote(arr[k], landed, old[0], old[1], k * N_NEAR + j, dev).wait_recv()
                _remote(landed, landed, new[0], new[1], k * N_NEAR + j, near[0][0]).start()

    def emit_finish(self, arr, old, _):
        x, y, c, me = _mesh_place()
        near = _near(x, y, c)
        other_core = near[0][0]
        for k in range(self.n):
            win = lambda idx: self.window(arr, k, idx)
            pltpu.make_async_copy(arr[k], win(me), old[2].at[k]).wait()
            for j, (dev, idx) in enumerate(near):
                _remote(arr[k], win(me), old[0], old[1], k * N_NEAR + j, dev).wait_send()
            _remote(arr[k], win(near[0][1]), old[0], old[1], k * N_NEAR, other_core).wait_recv()
            for j in (1, 2, 3):
                idx = near[j][1]
                _remote(win(idx), win(idx), old[3], old[4], k * N_NEAR + j, other_core).wait_send()
                _remote(arr[k], win(idx + 1 - 2 * c), old[3], old[4], k * N_NEAR + j, other_core).wait_recv()


class _Spread(_AllGather):
    def __init__(self, shards, kinds, name):
        super().__init__(shards, kinds, name)
        n = self.n
        self.new_sems = dict(start=[n * N_DEV, n * N_DEV, n], finish=[])

    def emit_start(self, arr, _, new):
        x, y, c, me = _mesh_place()
        for k in range(self.n):
            mine = self.window(arr, k, me)
            pltpu.make_async_copy(arr[k], mine, new[2].at[k]).start()
            for q in range(1, N_DEV):
                _remote(arr[k], mine, new[0], new[1], k * N_DEV + q, _peer(x, y, c, q)[0]).start()

    def emit_finish(self, arr, old, _):
        x, y, c, me = _mesh_place()
        for k in range(self.n):
            win = lambda idx: self.window(arr, k, idx)
            pltpu.make_async_copy(arr[k], win(me), old[2].at[k]).wait()
            for q in range(1, N_DEV):
                peer, peer_idx = _peer(x, y, c, q)
                _remote(arr[k], win(me), old[0], old[1], k * N_DEV + q, peer).wait_send()
                _remote(arr[k], win(peer_idx), old[0], old[1], k * N_DEV + q, peer).wait_recv()


class _Scatter:
    def __init__(self, partials, kinds, after, name):
        self.n, self.kinds, self.name, self.partials = len(partials), kinds, name, partials
        self.sizes = [p.shape[k] // N_DEV for p, k in zip(partials, kinds)]
        n, sizes = self.n, self.sizes
        self.slot_shapes = []
        for p, k, size in zip(partials, kinds, sizes):
            dims = list(p.shape)
            dims[k] = size
            self.slot_shapes.append((N_NEAR, *dims))
        slots = [lax.empty(sh, p.dtype) for sh, p in zip(self.slot_shapes, partials)]

        def emit(arr, _, new):
            x, y, c, _ = _mesh_place()
            near = _near(x, y, c)
            for k in range(n):
                for j in range(N_NEAR):
                    owner = near[j][1] if j == 0 else near[j][1] + 1 - 2 * c
                    _remote(_window(arr[k], kinds[k], owner, sizes[k]), arr[n + k].at[j], new[0], new[1],
                            k * N_NEAR + j, near[0][0]).start()

        self.sems, self.arrays, self.token = _split_call(name + "_start", [*partials, *slots], [], [n * N_NEAR] * 2,
                                                         after, emit)

    def combine_and_send(self, own4, after):
        n, kinds, sizes = self.n, self.kinds, self.sizes

        def emit_wait(arr, old, _):
            x, y, c, _ = _mesh_place()
            near = _near(x, y, c)
            for k in range(n):
                for j in range(N_NEAR):
                    owner = near[j][1] if j == 0 else near[j][1] + 1 - 2 * c
                    cp = _remote(_window(arr[k], kinds[k], owner, sizes[k]), arr[n + k].at[j], old[0], old[1],
                                 k * N_NEAR + j, near[0][0])
                    cp.wait_send()
                    cp.wait_recv()

        _, arrays, _ = _split_call(self.name + "_landed", self.arrays, self.sems, [], after, emit_wait)
        chip_sums = _chip_sums(arrays[:n], arrays[n:], kinds, sizes, own4, self.name + "_combine")
        arrivals = [lax.empty((N_NEAR - 1, *sh[1:]), p.dtype) for sh, p in zip(self.slot_shapes, self.partials)]

        def emit_send(arr, _, new):
            x, y, c, _ = _mesh_place()
            near = _near(x, y, c)
            for k in range(n):
                for j in (1, 2, 3):
                    _remote(arr[k].at[j], arr[n + k].at[j - 1], new[0], new[1], k * N_NEAR + j, near[j][0]).start()

        self.sems, self.arrays, self.token = _split_call(self.name + "_send", [*chip_sums, *arrivals], [],
                                                         [n * N_NEAR] * 2, own4, emit_send)

    def finish(self, after):
        n = self.n

        def emit(arr, old, _):
            x, y, c, _ = _mesh_place()
            near = _near(x, y, c)
            for k in range(n):
                for j in (1, 2, 3):
                    cp = _remote(arr[k].at[j], arr[n + k].at[j - 1], old[0], old[1], k * N_NEAR + j, near[j][0])
                    cp.wait_send()
                    cp.wait_recv()

        _, arrays, _ = _split_call(self.name + "_finish", self.arrays, self.sems, [], after, emit)
        return arrays[:n], arrays[n:]


def _chip_sums(partials, slots, kinds, sizes, own4, name):
    n = len(partials)

    def body(own_ref, *refs):
        for k in range(n):
            refs[2 * n + k][...] = (refs[k][...].astype(F32) + refs[n + k][...].astype(F32)).astype(BF16)

    in_specs, slot_specs = [], []
    for p, s, kind, size in zip(partials, slots, kinds, sizes):
        block = list(p.shape)
        block[kind] = size
        nd = len(block)
        in_specs.append(pl.BlockSpec(tuple(block), functools.partial(
            lambda j, own, kind, nd: tuple(own[j] if d == kind else 0 for d in range(nd)), kind=kind, nd=nd)))
        slot_specs.append(pl.BlockSpec((None, *block), functools.partial(
            lambda j, own, nd: (j,) + (0,) * nd, nd=nd)))
    return pl.pallas_call(
        body, name=name,
        grid_spec=pltpu.PrefetchScalarGridSpec(num_scalar_prefetch=1, grid=(N_NEAR,),
                                               in_specs=in_specs + slot_specs, out_specs=slot_specs),
        out_shape=[jax.ShapeDtypeStruct(s.shape, s.dtype) for s in slots],
        compiler_params=_params(("arbitrary",)),
    )(own4, *partials, *slots)


def _to_bf16(arrays, name, dep=None):
    n = len(arrays)
    deps = [] if dep is None else [dep]

    def body(*refs):
        for src, dst in zip(refs[:n], refs[n + len(deps):]):
            dst[...] = src[...].astype(BF16)

    vmem = pl.BlockSpec(memory_space=pltpu.VMEM)
    return pl.pallas_call(body, name=name, out_shape=[jax.ShapeDtypeStruct(a.shape, BF16) for a in arrays],
                          in_specs=[vmem] * n + [pl.BlockSpec(memory_space=pl.ANY)] * len(deps), out_specs=[vmem] * n,
                          compiler_params=pltpu.CompilerParams(vmem_limit_bytes=V7X_VMEM_LIMIT))(*arrays, *deps)


def _silu(c):
    return c * _sigmoid_tail(c)


def _ada_fwd(c_all, w_ada, b_ada_cols, dep):
    def body(c_ref, w_ref, b_ref, dep_ref, out_ref):
        out_ref[...] = jnp.dot(_silu(c_ref[...]), w_ref[...], preferred_element_type=F32,
                               precision=lax.Precision.HIGHEST) + b_ref[...]

    vmem = pl.BlockSpec(memory_space=pltpu.VMEM)
    return pl.pallas_call(
        body, name="ada_fwd", in_specs=[vmem, vmem, vmem, pl.BlockSpec(memory_space=pl.ANY)], out_specs=vmem,
        out_shape=jax.ShapeDtypeStruct((N_DEV, w_ada.shape[1]), F32),
    )(c_all, w_ada, b_ada_cols, dep)


def _adam(w, g, m, v):
    m = ADAM_B1 * m + (1.0 - ADAM_B1) * g
    v = ADAM_B2 * v + (1.0 - ADAM_B2) * (g * g)
    m_hat = m / (1.0 - ADAM_B1 ** ADAM_STEP)
    v_hat = v / (1.0 - ADAM_B2 ** ADAM_STEP)
    delta = -ADAM_LR * (m_hat / (jnp.sqrt(v_hat) + ADAM_EPS) + ADAM_WD * w)
    return delta, m, v


def _ada_bwd_adam(c_all, dmod_cols, w, m, v):
    def body(c_ref, d_ref, w_ref, m_ref, v_ref, g_ref, delta_ref, nm_ref, nv_ref):
        g = lax.dot_general(_silu(c_ref[...]), d_ref[...], (((0,), (0,)), ((), ())),
                            preferred_element_type=F32, precision=lax.Precision.HIGHEST)
        g_ref[...] = g
        delta_ref[...], nm_ref[...], nv_ref[...] = _adam(w_ref[...], g, m_ref[...], v_ref[...])

    sd = jax.ShapeDtypeStruct(w.shape, F32)
    return pl.pallas_call(body, name="ada_bwd_adam", out_shape=[sd] * 4,
                          compiler_params=pltpu.CompilerParams(vmem_limit_bytes=V7X_VMEM_LIMIT),
                          )(c_all, dmod_cols, w, m, v)


def _adam_group(chip_sums, arrivals, ws, ms, vs, n_tiles, name):
    n = len(ws)

    def body(*refs):
        for k in range(n):
            c_ref, a_ref, w_ref, m_ref, v_ref = (refs[j * n + k] for j in range(5))
            g_ref, delta_ref, nm_ref, nv_ref = (refs[(5 + j) * n + k] for j in range(4))
            g = c_ref[...].astype(F32)
            for j in range(N_NEAR - 1):
                g = g + a_ref[j].astype(F32)
            g_ref[...] = g
            delta_ref[...], nm_ref[...], nv_ref[...] = _adam(w_ref[...], g, m_ref[...], v_ref[...])

    tiles = [(w.shape[0] // n_tiles, w.shape[1]) for w in ws]
    blk = [pl.BlockSpec(t, lambda i: (i, 0)) for t in tiles]
    return pl.pallas_call(
        body, name=name, grid=(n_tiles,),
        in_specs=[pl.BlockSpec((None, *t), lambda i: (0, i, 0)) for t in tiles]
        + [pl.BlockSpec((N_NEAR - 1, *t), lambda i: (0, i, 0)) for t in tiles] + blk * 3,
        out_specs=blk * 4, out_shape=[jax.ShapeDtypeStruct(w.shape, F32) for w in ws] * 4,
        compiler_params=_params(("parallel",)),
    )(*chip_sums, *arrivals, *ws, *ms, *vs)


N_SMALL = 40
SMALL_MIXER_ROW = 16
N_SMALL_PARAMS = 11


def _pack_vecs(conv_w_full, rows):
    def body(cw_ref, *refs):
        out = refs[-1]
        out[...] = jnp.zeros_like(out)
        out[0:4, :] = cw_ref[0:4, :]
        for r, ref in enumerate(refs[:-1]):
            out[4 + r:5 + r, :] = ref[...]

    return pl.pallas_call(body, name="pack_vecs", out_shape=jax.ShapeDtypeStruct((16, D), F32))(conv_w_full, *rows)


def _small_finish(gathered, conv_cols, mod_all, vecs, ws, ms, vs):
    n = N_SMALL_PARAMS

    def body(g_ref, conv_ref, mod_ref, vec_ref, *refs):
        w_refs, m_refs, v_refs = refs[:n], refs[n:2 * n], refs[2 * n:3 * n]
        outs = refs[3 * n:]
        g1 = vec_ref[V_G1:V_G1 + 1, :]
        g2 = vec_ref[V_G2:V_G2 + 1, :]
        zero = jnp.zeros((1, D), F32)
        dg1, dg2, dgf, loss_lanes = zero, zero, zero, zero
        mixer = jnp.zeros((16, D), F32)
        db_ada = jnp.zeros((6, D), F32)
        d_conv_w = jnp.zeros(conv_ref.shape[1:], F32)
        for b in range(N_DEV):
            gb = g_ref[b]
            mod = mod_ref[b]
            q1 = gb[33:34]
            q2 = gb[9:10]
            dmod = jnp.concatenate([gb[32:33], q1 * g1, gb[10:11], gb[8:9], q2 * g2, gb[1:2]], axis=0)
            outs[4 * n][b] = dmod
            db_ada = db_ada + dmod
            dg1 = dg1 + q1 * (1.0 + mod[M_SC1:M_SC1 + 1])
            dg2 = dg2 + q2 * (1.0 + mod[M_SC2:M_SC2 + 1])
            dgf = dgf + gb[0:1]
            loss_lanes = loss_lanes + gb[2:3]
            mixer = mixer + gb[SMALL_MIXER_ROW:SMALL_MIXER_ROW + 16]
            d_conv_w = d_conv_w + conv_ref[b]
        d_a_param = mixer[7:8] * _sigmoid_tail(vec_ref[V_A_PARAM:V_A_PARAM + 1, :])
        grads = [dg1, dg2, mixer[4:5], mixer[5:6], mixer[6:7], d_a_param, mixer[8:9], mixer[9:10], dgf,
                 db_ada, d_conv_w]

        def load(ref, rows):
            if ref.shape[0] == rows:
                return ref[...]
            return jnp.concatenate([ref[:, j * D:(j + 1) * D] for j in range(rows)], axis=0)

        def store(ref, val):
            if ref.shape == val.shape:
                ref[...] = val
            else:
                for j in range(val.shape[0]):
                    ref[:, j * D:(j + 1) * D] = val[j:j + 1]

        for k in range(n):
            rows = grads[k].shape[0]
            results = (grads[k], *_adam(load(w_refs[k], rows), grads[k], load(m_refs[k], rows), load(v_refs[k], rows)))
            for which, val in enumerate(results):
                store(outs[which * n + k], val)
        outs[4 * n + 1][...] = jnp.broadcast_to(jnp.sum(loss_lanes, axis=1, keepdims=True), (8, 128))

    shapes = [jax.ShapeDtypeStruct(w.shape, F32) for w in ws]
    return pl.pallas_call(
        body, name="small_finish",
        out_shape=shapes * 4 + [jax.ShapeDtypeStruct((N_DEV, 6, D), F32), jax.ShapeDtypeStruct((8, 128), F32)],
    )(gathered, conv_cols, mod_all, vecs, *ws, *ms, *vs)


def _pad_rows(a, rows):
    return jnp.pad(a, ((0, rows - a.shape[0]), (0, 0)))


def kernel(x, c, norm_mix_g, norm_mlp_g, w_ada, b_ada, w_in, conv_w, conv_b, w_rg_a, b_rg_a, w_rg_x, b_rg_x, a_param, w_branch_a, w_pool, b_pool, pool_scale, w_branch_b, w_out, w_up, w_down, final_g, loss_target, m_norm_mix_g, m_norm_mlp_g, m_w_ada, m_b_ada, m_w_in, m_conv_w, m_conv_b, m_w_rg_a, m_b_rg_a, m_w_rg_x, m_b_rg_x, m_a_param, m_w_branch_a, m_w_pool, m_b_pool, m_pool_scale, m_w_branch_b, m_w_out, m_w_up, m_w_down, m_final_g, v_norm_mix_g, v_norm_mlp_g, v_w_ada, v_b_ada, v_w_in, v_conv_w, v_conv_b, v_w_rg_a, v_b_rg_a, v_w_rg_x, v_b_rg_x, v_a_param, v_w_branch_a, v_w_pool, v_b_pool, v_pool_scale, v_w_branch_b, v_w_out, v_w_up, v_w_down, v_final_g):
    me = 4 * lax.axis_index("x") + 2 * lax.axis_index("y") + lax.axis_index("c")
    s = x.shape[1]
    x2d = x.reshape(s, D)
    target = loss_target.reshape(s, D)
    n_ada = w_ada.shape[2]

    b_ada_cols = lax.dynamic_slice(b_ada, (0, me * n_ada), (1, n_ada))

    sharded = dict(w_in=(w_in[0], 1), w_up=(w_up[0], 1), w_down=(w_down[0], 0), w_branch_a=(w_branch_a[0], 0),
                   w_branch_b=(w_branch_b[0], 0), w_out=(w_out[0], 0), w_rg_a=(w_rg_a[0], 1), w_rg_x=(w_rg_x[0], 1),
                   w_pool=(w_pool[0], 1))
    kind = {k: v[1] for k, v in sharded.items()}
    first_names = ["w_in"]
    later_names = [k for k in sharded if k not in first_names]
    mix_names = ["w_rg_a", "w_rg_x", "w_pool"]
    branch_names = ["w_branch_a", "w_branch_b", "w_out"]
    mlp_names = ["w_up", "w_down"]

    def gather(group, after, name):
        exchange = _Gather([shard[k] for k in group], [kind[k] for k in group], name)
        exchange.start(after)
        return exchange

    shard = dict(zip(first_names, _to_bf16([sharded[k][0] for k in first_names], "to_bf16_first")))
    spread_c = _Spread([c, conv_w[0]], [0, 1], "spread_c")
    g_first = _Gather([shard[k] for k in first_names], [kind[k] for k in first_names], "gather_first")
    _together("first_start", [(spread_c, "start"), (g_first, "start")], c)
    shard.update(zip(later_names, _to_bf16([sharded[k][0] for k in later_names], "to_bf16_later", dep=g_first.token)))

    c_all, conv_w_full = spread_c.finish(g_first.token)
    mod_part = _ada_fwd(c_all, w_ada[0], b_ada_cols, g_first.token)
    vecs = _pack_vecs(conv_w_full, [conv_b, b_rg_a, b_rg_x, a_param, b_pool, pool_scale,
                                    norm_mix_g, norm_mlp_g, final_g.reshape(1, D)])
    spread_mod = _Spread([mod_part], [0], "spread_mod")
    spread_mod.start(vecs)
    g_mix = gather(mix_names, spread_mod.token, "gather_mix")
    g_branch = gather(branch_names, g_mix.token, "gather_branch")
    g_mlp = gather(mlp_names, g_branch.token, "gather_mlp")
    g_first.forward(g_mlp.token)
    wg = dict(zip(first_names, g_first.finish(g_first.token)))
    mod_parts, = spread_mod.finish(g_first.token)
    mod_all = jnp.transpose(mod_parts.reshape(N_DEV, N_DEV, n_ada), (1, 0, 2)).reshape(N_DEV, 6, D)
    modr = _pad_rows(lax.dynamic_index_in_dim(mod_all, me, 0, keepdims=False), 8)

    h1, x_rnn, u_pool, ga, dga, sa, sb = _proj_fwd(x2d, modr, vecs, wg["w_in"])
    _together("mixer_forward", [(g_mix, "forward"), (g_branch, "forward")], h1)
    wg.update(zip(mix_names, g_mix.finish(g_branch.token)))
    xr, hr, za, p, pooled, *gates = _mix_fwd(x_rnn, u_pool, ga, vecs, wg["w_rg_a"], wg["w_rg_x"], wg["w_pool"],
                                             dep=g_branch.token)
    g_mlp.forward(za)
    wg.update(zip(branch_names, g_branch.finish(g_mlp.token)))
    ba, bb, merged, o, x2, h2 = _branch_fwd(za, pooled, sa, sb, x2d, modr, vecs,
                                            wg["w_branch_a"], wg["w_branch_b"], wg["w_out"])
    wg.update(zip(mlp_names, g_mlp.finish(h2)))
    ru, dx3, d_dn, small_f = _mlp_fwd(h2, x2, target, modr, vecs, wg["w_up"], wg["w_down"])

    near = _near(lax.axis_index("x"), lax.axis_index("y"), lax.axis_index("c"))
    own4 = jnp.stack([me, near[1][1], near[2][1], near[3][1]]).astype(jnp.int32)

    def scatter(group, partial, after, name):
        return _Scatter([partial[k] for k in group], [kind[k] for k in group], after, name)

    dup, dx2, do, small_m = _mlp_bwd(d_dn, ru, x2, dx3, o, modr, vecs, wg["w_up"], wg["w_down"])
    partial = dict(w_up=_wgrad(h2, dup, "wgrad_up"), w_down=_wgrad(ru, d_dn, "wgrad_down", square_a=True))
    s_mlp = scatter(mlp_names, partial, dx2, "scatter_mlp")

    dba, dbb, dgates, dza, dpooled = _branch_bwd(do, sa, sb, ba, bb, wg["w_branch_a"], wg["w_branch_b"], wg["w_out"],
                                                 dep=s_mlp.token)
    s_mlp.combine_and_send(own4, dza)
    dproj, dw_rg_a, dw_rg_x, dw_pool, small_x = _mix_bwd(dza, dpooled, x_rnn, ga, dga, xr, hr, p, gates, dgates,
                                                         vecs, wg["w_rg_a"], wg["w_rg_x"], wg["w_pool"],
                                                         dep=s_mlp.token)
    partial.update(zip(["w_out", "w_branch_b", "w_branch_a"],
                       _wgrad_square([(merged, do), (pooled, dbb), (za, dba)], "wgrad_square")),
                   w_rg_a=dw_rg_a, w_rg_x=dw_rg_x, w_pool=dw_pool)
    mixer_names = ["w_rg_a", "w_rg_x", "w_pool", "w_branch_a", "w_branch_b", "w_out"]
    s_mixer = scatter(mixer_names, partial, s_mlp.token, "scatter_mixer")

    partial["w_in"] = _wgrad(h1, dproj, "wgrad_in", dep=s_mixer.token)
    s_in = scatter(["w_in"], partial, s_mixer.token, "scatter_in")
    s_mixer.combine_and_send(own4, s_in.token)
    s_in.combine_and_send(own4, s_mixer.token)
    grad_x, small_p = _proj_bwd(dproj, x2d, dx2, modr, vecs, wg["w_in"], dep=s_in.token)

    locals_ = dict(w_in=(w_in, m_w_in, v_w_in), w_up=(w_up, m_w_up, v_w_up), w_down=(w_down, m_w_down, v_w_down),
                   w_branch_a=(w_branch_a, m_w_branch_a, v_w_branch_a),
                   w_branch_b=(w_branch_b, m_w_branch_b, v_w_branch_b), w_out=(w_out, m_w_out, v_w_out),
                   w_rg_a=(w_rg_a, m_w_rg_a, v_w_rg_a), w_rg_x=(w_rg_x, m_w_rg_x, v_w_rg_x),
                   w_pool=(w_pool, m_w_pool, v_w_pool))
    res = {}

    def finish(group, exchange, after, n_tiles, name):
        chip_sums, arrivals = exchange.finish(after)
        flat = lambda t: t.reshape(-1, t.shape[-1])
        shapes = [flat(locals_[k][0]).shape for k in group]
        outs = _adam_group([cs.reshape(N_NEAR, *sh) for cs, sh in zip(chip_sums, shapes)],
                           [ar.reshape(N_NEAR - 1, *sh) for ar, sh in zip(arrivals, shapes)],
                           *[[flat(locals_[k][j]) for k in group] for j in range(3)], n_tiles, name)
        for i, k in enumerate(group):
            res[k] = [outs[j * len(group) + i].reshape(locals_[k][0].shape) for j in range(4)]
        return res[group[-1]][0]

    small = jnp.concatenate([small_f, small_m, small_x, small_p], axis=0)
    g_small = _Spread([small], [0], "spread_small")
    g_small.start(grad_x)
    done = finish(mlp_names, s_mlp, g_small.token, 4, "adam_mlp")
    done = finish(mixer_names, s_mixer, done, 2, "adam_mixer")
    done = finish(["w_in"], s_in, done, 4, "adam_in")
    small_all, = g_small.finish(done)
    small_all = small_all.reshape(N_DEV, N_SMALL, D)

    conv_cols = lax.dynamic_slice(small_all, (0, SMALL_MIXER_ROW, me * (D // N_DEV)), (N_DEV, 4, D // N_DEV))

    def smalls(ng, nl, cb, bra, brx, ap, bp, ps, fg, ba_, cw):
        return [ng, nl, cb, bra, brx, ap, bp, ps, fg.reshape(1, D), ba_, cw[0]]

    small_names = ["norm_mix_g", "norm_mlp_g", "conv_b", "b_rg_a", "b_rg_x", "a_param", "b_pool", "pool_scale",
                   "final_g", "b_ada", "conv_w"]
    fin = _small_finish(
        small_all, conv_cols, mod_all, vecs,
        smalls(norm_mix_g, norm_mlp_g, conv_b, b_rg_a, b_rg_x, a_param, b_pool, pool_scale, final_g, b_ada, conv_w),
        smalls(m_norm_mix_g, m_norm_mlp_g, m_conv_b, m_b_rg_a, m_b_rg_x, m_a_param, m_b_pool, m_pool_scale,
               m_final_g, m_b_ada, m_conv_w),
        smalls(v_norm_mix_g, v_norm_mlp_g, v_conv_b, v_b_rg_a, v_b_rg_x, v_a_param, v_b_pool, v_pool_scale,
               v_final_g, v_b_ada, v_conv_w))
    dmod_all, loss_tile = fin[4 * N_SMALL_PARAMS], fin[4 * N_SMALL_PARAMS + 1]
    dmod_cols = lax.dynamic_slice(dmod_all.reshape(N_DEV, 6 * D), (0, me * n_ada), (N_DEV, n_ada))
    res["w_ada"] = [t.reshape(w_ada.shape) for t in _ada_bwd_adam(c_all, dmod_cols, w_ada[0], m_w_ada[0], v_w_ada[0])]

    def final_shape(k, t):
        if k == "final_g":
            return t.reshape(D)
        if k == "conv_w":
            return t.reshape(conv_w.shape)
        return t

    for i, k in enumerate(small_names):
        res[k] = [final_shape(k, fin[which * N_SMALL_PARAMS + i]) for which in range(4)]
    order = ["norm_mix_g", "norm_mlp_g", "w_ada", "b_ada", "w_in", "conv_w", "conv_b", "w_rg_a", "b_rg_a", "w_rg_x",
             "b_rg_x", "a_param", "w_branch_a", "w_pool", "b_pool", "pool_scale", "w_branch_b", "w_out", "w_up",
             "w_down", "final_g"]
    outs = [loss_tile[0, 0], grad_x.reshape(x.shape)]
    for which in range(4):
        for k in order:
            outs.append(res[k][which])
    return tuple(outs)
```

```python
import functools

import jax
import jax.numpy as jnp
from jax import lax
from jax.experimental import pallas as pl
from jax.experimental.pallas import tpu as pltpu

F32 = jnp.float32
BF16 = jnp.bfloat16
MESH = pl.DeviceIdType.MESH

N_DEV = 8
D = 1024
N_GROUPS = 4
GW = D // N_GROUPS
D_IN = 5 * D
D_FF = 4 * D
POOL_WINDOWS = (2, 4, 8, 16)
HALO_X = 8
HALO_U = 16
EPS = 1e-6
C_RG = 8.0
ADAM_LR, ADAM_B1, ADAM_B2, ADAM_EPS, ADAM_WD, ADAM_STEP = 0.001, 0.9, 0.999, 1e-08, 0.01, 10

V7X_VMEM_LIMIT = 56 * 1024 * 1024

V_CONV_W, V_CONV_B, V_B_RG_A, V_B_RG_X, V_A_PARAM, V_B_POOL, V_POOL_SCALE, V_G1, V_G2, V_GF = 0, 4, 5, 6, 7, 8, 9, 10, 11, 12
M_SH1, M_SC1, M_GT1, M_SH2, M_SC2, M_GT2 = 0, 1, 2, 3, 4, 5

TM_PROJ = 512
TM_MIX = 256
TM_BRANCH = 512
TM_MLP = 512
TM_MLP_BWD = 256
TS_WGRAD = 1024


def _params(semantics):
    return pltpu.CompilerParams(dimension_semantics=semantics, vmem_limit_bytes=V7X_VMEM_LIMIT)


def _resident(shape):
    return pl.BlockSpec(shape, lambda *_: (0,) * len(shape), pipeline_mode=pl.Buffered(1))


def _dot(a, b):
    return jnp.dot(a, b, preferred_element_type=F32)


def _dot_nt(a, b):
    return lax.dot_general(a, b, (((1,), (1,)), ((), ())), preferred_element_type=F32)


def _dot_tn(a, b):
    return lax.dot_general(a, b, (((0,), (0,)), ((), ())), preferred_element_type=F32)


def _sigmoid(x):
    return 0.5 * jnp.tanh(0.5 * x) + 0.5


def _sigmoid_tail(x):
    return 1.0 / (1.0 + jnp.exp(-x))


def _gelu_and_grad(x):
    k = 0.7978845608028654
    x2 = x * x
    t = jnp.tanh(k * (x + 0.044715 * x * x2))
    g = 0.5 * x * (1.0 + t)
    dg = 0.5 * (1.0 + t) + 0.5 * x * (1.0 - t * t) * (k * (1.0 + 3.0 * 0.044715 * x2))
    return g, dg


def _softplus(a):
    e = jnp.exp(-jnp.abs(a))
    u = 1.0 + e
    log1p_e = jnp.where(u == 1.0, e, jnp.log(u) * e / jnp.where(u == 1.0, 1.0, u - 1.0))
    return jnp.maximum(a, 0.0) + log1p_e


def _neg_expm1(z):
    series = -(z * (1.0 + z * (0.5 + z * (1.0 / 6.0 + z * (1.0 / 24.0 + z * (1.0 / 120.0))))))
    return jnp.where(z > -0.1, series, 1.0 - jnp.exp(z))


def _shift_down(x, k):
    return pltpu.roll(x, k, 0)


def _shift_up(x, k):
    return pltpu.roll(x, x.shape[0] - k, 0)


def _rglru_gates(xr, w_a, w_x, b_a, b_x, a_param, is_t0):
    xb = xr.astype(BF16)
    ra = _sigmoid(_dot(xb, w_a) + b_a)
    ri = _sigmoid(_dot(xb, w_x) + b_x)
    sp = _softplus(a_param)
    log_a = (-C_RG) * ra * sp
    a = jnp.exp(log_a)
    mult = jnp.where(is_t0, 1.0, jnp.sqrt(_neg_expm1(2.0 * log_a)))
    return ra, ri, sp, a, mult


SUBLANES = 8


LANES = 128


def _scan_strip(a, b, carry, scr, down):
    t = b.shape[0]
    g = t // SUBLANES
    a3 = a.reshape(g, SUBLANES, LANES)
    b3 = b.reshape(g, SUBLANES, LANES)
    sub = lax.broadcasted_iota(jnp.int32, (g, SUBLANES, LANES), 1)
    for k in (1, 2, 4):
        keep = sub >= k if down else sub < SUBLANES - k
        shift = k if down else SUBLANES - k
        b3 = b3 + a3 * jnp.where(keep, pltpu.roll(b3, shift, 1), 0.0)
        a3 = a3 * jnp.where(keep, pltpu.roll(a3, shift, 1), 1.0)
    scr[0] = a3.reshape(t, LANES)
    scr[1] = b3.reshape(t, LANES)
    end_row = SUBLANES - 1 if down else 0
    ag = scr[0, pl.ds(end_row, g, stride=SUBLANES), :]
    bg = scr[1, pl.ds(end_row, g, stride=SUBLANES), :]
    rg = lax.broadcasted_iota(jnp.int32, (g, LANES), 0)
    edge = 0 if down else g - 1
    bg = bg + jnp.where(rg == edge, ag * carry, 0.0)
    k = 1
    while k < g:
        keep = rg >= k if down else rg < g - k
        shift = k if down else g - k
        bg = bg + ag * jnp.where(keep, pltpu.roll(bg, shift, 0), 0.0)
        if 2 * k < g:
            ag = ag * pltpu.roll(ag, shift, 0)
        k *= 2
    entering = jnp.where(rg != edge, pltpu.roll(bg, 1 if down else g - 1, 0), carry)
    for r in range(SUBLANES):
        scr[2, pl.ds(r, g, stride=SUBLANES), :] = entering
    return scr[1] + scr[0] * scr[2], bg[g - 1:g, :]


def _scan_strips(a, b, carry, scr, down):
    outs = [_scan_strip(a[:, c:c + LANES], b[:, c:c + LANES], carry[:, c:c + LANES], scr, down)
            for c in range(0, b.shape[1], LANES)]
    return jnp.concatenate([o[0] for o in outs], axis=1), jnp.concatenate([o[1] for o in outs], axis=1)


def _scan_down(a, b, carry, scr):
    return _scan_strips(a, b, carry, scr, True)


def _scan_up(m, b, carry, scr):
    return _scan_strips(m, b, carry, scr, False)[0]


def _window_mean(sums, window, first_block, head_t):
    scaled = sums * (1.0 / window)
    head = jnp.where(first_block, sums[:HALO_U] / jnp.minimum(head_t, float(window)), scaled[:HALO_U])
    return jnp.concatenate([head, scaled[HALO_U:]], axis=0)


def _conv_taps(x_ext):
    return [_shift_down(x_ext, 3 - j)[HALO_X:] if j < 3 else x_ext[HALO_X:] for j in range(4)]


def _proj_fwd(x, modr, vecs, w_in):
    s = x.shape[0]
    tm = min(TM_PROJ, s)

    def body(x_ref, mod_ref, vec_ref, w_ref, h1_ref, xrnn_ref, u_ref, ga_ref, dga_ref, sa_ref, sb_ref):
        xv = x_ref[...]
        r = lax.rsqrt(jnp.mean(xv * xv, axis=-1, keepdims=True) + EPS)
        gain = vec_ref[V_G1:V_G1 + 1, :] * (1.0 + mod_ref[M_SC1:M_SC1 + 1, :])
        h = (xv * r * gain + mod_ref[M_SH1:M_SH1 + 1, :]).astype(BF16)
        h1_ref[...] = h
        xrnn_ref[...] = _dot(h, w_ref[:, 0:D])
        ga_ref[...], dga_ref[...] = _gelu_and_grad(_dot(h, w_ref[:, D:2 * D]))
        u_ref[...] = _dot(h, w_ref[:, 2 * D:3 * D])
        sa_ref[...] = _sigmoid(_dot(h, w_ref[:, 3 * D:4 * D]))
        sb_ref[...] = _sigmoid(_dot(h, w_ref[:, 4 * D:5 * D]))

    tok = pl.BlockSpec((tm, D), lambda i: (i, 0))
    sd = lambda dt: jax.ShapeDtypeStruct((s, D), dt)
    return pl.pallas_call(
        body, name="proj_fwd", grid=(s // tm,),
        in_specs=[tok, pl.BlockSpec((8, D), lambda i: (0, 0)), pl.BlockSpec((16, D), lambda i: (0, 0)),
                  _resident((D, D_IN))],
        out_specs=[tok] * 7,
        out_shape=[sd(BF16)] + [sd(F32)] * 6,
        compiler_params=_params(("parallel",)),
    )(x, modr, vecs, w_in)


def _mix_fwd(x_rnn, u_pool, ga, vecs, w_rg_a, w_rg_x, w_pool, dep):
    s = x_rnn.shape[0]
    tm = min(TM_MIX, s)
    nb = s // tm

    def body(xh_ref, x_ref, uh_ref, u_ref, ga_ref, vec_ref, wa_ref, wx_ref, wp_ref, dep_ref,
             xr_ref, hr_ref, za_ref, p_ref, pooled_ref, a_ref, mult_ref, ra_ref, ri_ref, carry_ref, scan_scr):
        i = pl.program_id(0)
        first = i == 0

        @pl.when(first)
        def _():
            carry_ref[...] = jnp.zeros_like(carry_ref)

        row = lax.broadcasted_iota(jnp.int32, (tm, GW), 0)
        is_t0 = jnp.logical_and(first, row == 0)
        head_t = (lax.broadcasted_iota(jnp.int32, (HALO_U, GW), 0) + 1).astype(F32)
        for g in range(N_GROUPS):
            cs = slice(g * GW, (g + 1) * GW)
            vec = vec_ref[:, cs]
            xh = jnp.where(first, 0.0, xh_ref[:, cs])
            taps = _conv_taps(jnp.concatenate([xh, x_ref[:, cs]], axis=0))
            xr = vec[V_CONV_B:V_CONV_B + 1]
            for j in range(4):
                xr = xr + vec[V_CONV_W + j:V_CONV_W + j + 1] * taps[j]
            xr_ref[:, cs] = xr
            ra, ri, _, a, mult = _rglru_gates(
                xr, wa_ref[g], wx_ref[g], vec[V_B_RG_A:V_B_RG_A + 1], vec[V_B_RG_X:V_B_RG_X + 1],
                vec[V_A_PARAM:V_A_PARAM + 1], is_t0)
            a_ref[:, cs] = a
            mult_ref[:, cs] = mult
            ra_ref[:, cs] = ra.astype(BF16)
            ri_ref[:, cs] = ri.astype(BF16)
            h, last = _scan_down(a, xr * ri * mult, carry_ref[0:1, cs], scan_scr)
            hr_ref[:, cs] = h
            carry_ref[0:1, cs] = last
            za_ref[:, cs] = (ga_ref[:, cs] * h).astype(BF16)
            uh = jnp.where(first, 0.0, uh_ref[:, cs])
            sm = jnp.concatenate([uh, u_ref[:, cs]], axis=0)
            k = 1
            while k < POOL_WINDOWS[g]:
                sm = sm + _shift_down(sm, k)
                k *= 2
            mean = _window_mean(sm[HALO_U:], POOL_WINDOWS[g], first, head_t)
            p = (mean - u_ref[:, cs]).astype(BF16)
            p_ref[:, cs] = p
            pb = _dot(p, wp_ref[g]) + vec[V_B_POOL:V_B_POOL + 1]
            pooled_ref[:, cs] = (pb * vec[V_POOL_SCALE:V_POOL_SCALE + 1]).astype(BF16)

    tok = pl.BlockSpec((tm, D), lambda i: (i, 0))
    halo = lambda rows: pl.BlockSpec((rows, D), lambda i: (jnp.maximum(i * (tm // rows) - 1, 0), 0))
    wspec = pl.BlockSpec((N_GROUPS, GW, GW), lambda i: (0, 0, 0))
    sd = lambda dt: jax.ShapeDtypeStruct((s, D), dt)
    return pl.pallas_call(
        body, name="mix_fwd", grid=(nb,),
        in_specs=[halo(HALO_X), tok, halo(HALO_U), tok, tok, pl.BlockSpec((16, D), lambda i: (0, 0)),
                  wspec, wspec, wspec, pl.BlockSpec(memory_space=pl.ANY)],
        out_specs=[tok] * 9,
        out_shape=[sd(F32), sd(F32), sd(BF16), sd(BF16), sd(BF16), sd(F32), sd(F32), sd(BF16), sd(BF16)],
        scratch_shapes=[pltpu.VMEM((8, D), F32), pltpu.VMEM((3, tm, LANES), F32)],
        compiler_params=_params(("arbitrary",)),
    )(x_rnn, x_rnn, u_pool, u_pool, ga, vecs, w_rg_a, w_rg_x, w_pool, dep)


def _branch_fwd(za, pooled, sa, sb, x, modr, vecs, w_a, w_b, w_out):
    s = x.shape[0]
    tm = min(TM_BRANCH, s)

    def body(za_ref, pooled_ref, sa_ref, sb_ref, x_ref, mod_ref, vec_ref, wa_ref, wb_ref, wo_ref,
             ba_ref, bb_ref, merged_ref, o_ref, x2_ref, h2_ref):
        ba = _dot(za_ref[...], wa_ref[...])
        bb = _dot(pooled_ref[...], wb_ref[...])
        ba_ref[...] = ba.astype(BF16)
        bb_ref[...] = bb.astype(BF16)
        merged = (sa_ref[...] * ba + sb_ref[...] * bb).astype(BF16)
        merged_ref[...] = merged
        o = _dot(merged, wo_ref[...])
        o_ref[...] = o.astype(BF16)
        x2 = x_ref[...] + mod_ref[M_GT1:M_GT1 + 1, :] * o
        x2_ref[...] = x2
        r = lax.rsqrt(jnp.mean(x2 * x2, axis=-1, keepdims=True) + EPS)
        gain = vec_ref[V_G2:V_G2 + 1, :] * (1.0 + mod_ref[M_SC2:M_SC2 + 1, :])
        h2_ref[...] = (x2 * r * gain + mod_ref[M_SH2:M_SH2 + 1, :]).astype(BF16)

    tok = pl.BlockSpec((tm, D), lambda i: (i, 0))
    wspec = pl.BlockSpec((D, D), lambda i: (0, 0))
    sd = lambda dt: jax.ShapeDtypeStruct((s, D), dt)
    return pl.pallas_call(
        body, name="branch_fwd", grid=(s // tm,),
        in_specs=[tok, tok, tok, tok,
                  tok, pl.BlockSpec((8, D), lambda i: (0, 0)), pl.BlockSpec((16, D), lambda i: (0, 0)),
                  wspec, wspec, wspec],
        out_specs=[tok] * 6,
        out_shape=[sd(BF16), sd(BF16), sd(BF16), sd(BF16), sd(F32), sd(BF16)],
        compiler_params=_params(("parallel",)),
    )(za, pooled, sa, sb, x, modr, vecs, w_a, w_b, w_out)


def _mlp_fwd(h2, x2, target, modr, vecs, w_up, w_down):
    s = x2.shape[0]
    tm = min(TM_MLP, s)

    def body(h2_ref, x2_ref, tgt_ref, mod_ref, vec_ref, wu_ref, wd_ref,
             ru_ref, dx3_ref, ddn_ref, small_ref):
        @pl.when(pl.program_id(0) == 0)
        def _():
            small_ref[...] = jnp.zeros_like(small_ref)

        h2 = h2_ref[...]
        dn = None
        for c in range(D_FF // D):
            cs = slice(c * D, (c + 1) * D)
            ru = jnp.maximum(_dot(h2, wu_ref[:, cs]), 0.0)
            ru_ref[:, cs] = ru.astype(BF16)
            part = _dot((ru * ru).astype(BF16), wd_ref[cs, :])
            dn = part if dn is None else dn + part
        gt2 = mod_ref[M_GT2:M_GT2 + 1, :]
        gf = vec_ref[V_GF:V_GF + 1, :]
        x3 = x2_ref[...] + gt2 * dn
        r3 = lax.rsqrt(jnp.mean(x3 * x3, axis=-1, keepdims=True) + EPS)
        n3 = x3 * r3
        err = n3 * gf - tgt_ref[...]
        dy = err * (1.0 / D)
        dn3 = dy * gf
        dx3 = r3 * (dn3 - n3 * jnp.mean(dn3 * n3, axis=-1, keepdims=True))
        dx3_ref[...] = dx3
        ddn_ref[...] = (dx3 * gt2).astype(BF16)
        small_ref[0:1, :] += jnp.sum(dy * n3, axis=0, keepdims=True)
        small_ref[1:2, :] += jnp.sum(dx3 * dn, axis=0, keepdims=True)
        small_ref[2:3, :] += (0.5 / D) * jnp.sum(err * err, axis=0, keepdims=True)

    tok = pl.BlockSpec((tm, D), lambda i: (i, 0))
    return pl.pallas_call(
        body, name="mlp_fwd", grid=(s // tm,),
        in_specs=[tok, tok, tok,
                  pl.BlockSpec((8, D), lambda i: (0, 0)), pl.BlockSpec((16, D), lambda i: (0, 0)),
                  _resident((D, D_FF)), _resident((D_FF, D))],
        out_specs=[pl.BlockSpec((tm, D_FF), lambda i: (i, 0)), tok, tok,
                   pl.BlockSpec((8, D), lambda i: (0, 0))],
        out_shape=[jax.ShapeDtypeStruct((s, D_FF), BF16), jax.ShapeDtypeStruct((s, D), F32),
                   jax.ShapeDtypeStruct((s, D), BF16), jax.ShapeDtypeStruct((8, D), F32)],
        compiler_params=_params(("arbitrary",)),
    )(h2, x2, target, modr, vecs, w_up, w_down)


def _mlp_bwd(d_dn, ru, x2, dx3, o, modr, vecs, w_up, w_down):
    s = x2.shape[0]
    tm = min(TM_MLP_BWD, s)

    def body(ddn_ref, ru_ref, x2_ref, dx3_ref, o_ref, mod_ref, vec_ref, wu_ref, wd_ref,
             dup_ref, dx2_ref, do_ref, small_ref):
        @pl.when(pl.program_id(0) == 0)
        def _():
            small_ref[...] = jnp.zeros_like(small_ref)

        ddn = ddn_ref[...]
        dh2 = None
        for c in range(D_FF // D):
            cs = slice(c * D, (c + 1) * D)
            dff = _dot_nt(ddn, wd_ref[cs, :])
            dup = (dff * (2.0 * ru_ref[:, cs].astype(F32))).astype(BF16)
            dup_ref[:, cs] = dup
            part = _dot_nt(dup, wu_ref[:, cs])
            dh2 = part if dh2 is None else dh2 + part
        x2 = x2_ref[...]
        r2 = lax.rsqrt(jnp.mean(x2 * x2, axis=-1, keepdims=True) + EPS)
        xn2 = x2 * r2
        gain = vec_ref[V_G2:V_G2 + 1, :] * (1.0 + mod_ref[M_SC2:M_SC2 + 1, :])
        dxn2 = dh2 * gain
        dx2 = dx3_ref[...] + r2 * (dxn2 - xn2 * jnp.mean(dxn2 * xn2, axis=-1, keepdims=True))
        dx2_ref[...] = dx2
        do_ref[...] = (dx2 * mod_ref[M_GT1:M_GT1 + 1, :]).astype(BF16)
        small_ref[0:1, :] += jnp.sum(dh2, axis=0, keepdims=True)
        small_ref[1:2, :] += jnp.sum(dh2 * xn2, axis=0, keepdims=True)
        small_ref[2:3, :] += jnp.sum(dx2 * o_ref[...].astype(F32), axis=0, keepdims=True)

    tok = pl.BlockSpec((tm, D), lambda i: (i, 0))
    wide = pl.BlockSpec((tm, D_FF), lambda i: (i, 0))
    return pl.pallas_call(
        body, name="mlp_bwd", grid=(s // tm,),
        in_specs=[tok, wide, tok, tok, tok,
                  pl.BlockSpec((8, D), lambda i: (0, 0)), pl.BlockSpec((16, D), lambda i: (0, 0)),
                  _resident((D, D_FF)), _resident((D_FF, D))],
        out_specs=[wide, tok, tok, pl.BlockSpec((8, D), lambda i: (0, 0))],
        out_shape=[jax.ShapeDtypeStruct((s, D_FF), BF16), jax.ShapeDtypeStruct((s, D), F32),
                   jax.ShapeDtypeStruct((s, D), BF16), jax.ShapeDtypeStruct((8, D), F32)],
        compiler_params=_params(("arbitrary",)),
    )(d_dn, ru, x2, dx3, o, modr, vecs, w_up, w_down)


def _branch_bwd(do, sa, sb, ba, bb, w_a, w_b, w_out, dep):
    s = do.shape[0]
    tm = min(TM_BRANCH, s)

    def body(do_ref, sa_ref, sb_ref, ba_ref, bb_ref, wa_ref, wb_ref, wo_ref, dep_ref,
             dba_ref, dbb_ref, dg_ref, dza_ref, dpooled_ref):
        dmerged = _dot_nt(do_ref[...], wo_ref[...])
        sa = sa_ref[...]
        sb = sb_ref[...]
        dba = (dmerged * sa).astype(BF16)
        dbb = (dmerged * sb).astype(BF16)
        dba_ref[...] = dba
        dbb_ref[...] = dbb
        dg_ref[:, :D] = (dmerged * ba_ref[...].astype(F32) * sa * (1.0 - sa)).astype(BF16)
        dg_ref[:, D:] = (dmerged * bb_ref[...].astype(F32) * sb * (1.0 - sb)).astype(BF16)
        dza_ref[...] = _dot_nt(dba, wa_ref[...])
        dpooled_ref[...] = _dot_nt(dbb, wb_ref[...])

    tok = pl.BlockSpec((tm, D), lambda i: (i, 0))
    wspec = pl.BlockSpec((D, D), lambda i: (0, 0))
    sd = lambda dt: jax.ShapeDtypeStruct((s, D), dt)
    return pl.pallas_call(
        body, name="branch_bwd", grid=(s // tm,),
        in_specs=[tok, tok, tok, tok, tok, wspec, wspec, wspec, pl.BlockSpec(memory_space=pl.ANY)],
        out_specs=[tok, tok, pl.BlockSpec((tm, 2 * D), lambda i: (i, 0)), tok, tok],
        out_shape=[sd(BF16), sd(BF16), jax.ShapeDtypeStruct((s, 2 * D), BF16), sd(F32), sd(F32)],
        compiler_params=_params(("parallel",)),
    )(do, sa, sb, ba, bb, w_a, w_b, w_out, dep)


def _mix_bwd(dza, dpooled, x_rnn, ga, dga, xr, hr, p, gates, dgates, vecs, w_rg_a, w_rg_x, w_pool, dep):
    s = xr.shape[0]
    tm = min(TM_MIX, s)
    nb = s // tm

    def body(dza_ref, dpooled_ref, xh_ref, x_ref, ga_ref, dga_ref, xr_ref, hh_ref, hr_ref, p_ref,
             a_ref, mult_ref, ra_ref, ri_ref, dg_ref, vec_ref, wa_ref, wx_ref, wp_ref, dep_ref,
             dproj_ref, dwa_ref, dwx_ref, dwp_ref, small_ref,
             scan_carry, dxr_carry, q_carry, scan_scr, dwa_acc, dwx_acc, dwp_acc):
        i = pl.program_id(0)
        bi = nb - 1 - i
        first_t = bi == 0

        @pl.when(i == 0)
        def _():
            scan_carry[...] = jnp.zeros_like(scan_carry)
            dxr_carry[...] = jnp.zeros_like(dxr_carry)
            q_carry[...] = jnp.zeros_like(q_carry)
            dwa_acc[...] = jnp.zeros_like(dwa_acc)
            dwx_acc[...] = jnp.zeros_like(dwx_acc)
            dwp_acc[...] = jnp.zeros_like(dwp_acc)
            small_ref[...] = jnp.zeros_like(small_ref)

        row = lax.broadcasted_iota(jnp.int32, (tm, GW), 0)
        is_t0 = jnp.logical_and(first_t, row == 0)
        head_t = (lax.broadcasted_iota(jnp.int32, (HALO_U, GW), 0) + 1).astype(F32)
        colsum = lambda v: jnp.sum(v, axis=0, keepdims=True)
        for g in range(N_GROUPS):
            cs = slice(g * GW, (g + 1) * GW)
            vec = vec_ref[:, cs]
            xr = xr_ref[:, cs]
            hr = hr_ref[:, cs]
            dza = dza_ref[:, cs]
            dproj_ref[:, D + g * GW:D + (g + 1) * GW] = (dza * hr * dga_ref[:, cs]).astype(BF16)
            dhr = dza * ga_ref[:, cs]
            a = a_ref[:, cs]
            mult = mult_ref[:, cs]
            ra = ra_ref[:, cs].astype(F32)
            ri = ri_ref[:, cs].astype(F32)
            sp = _softplus(vec[V_A_PARAM:V_A_PARAM + 1])
            m = jnp.where(row == tm - 1, 1.0, _shift_up(a, 1))
            gsum = _scan_up(m, dhr, scan_carry[0:1, cs], scan_scr)
            scan_carry[0:1, cs] = a[0:1, :] * gsum[0:1, :]
            hh = jnp.where(first_t, 0.0, hh_ref[:, cs])
            hprev = _shift_down(jnp.concatenate([hh, hr], axis=0), 1)[8:]
            da = gsum * hprev
            dmult = jnp.where(is_t0, 0.0, gsum * xr * ri)
            dlog_a = da * a - dmult * a * a / mult
            dri = gsum * xr * mult
            dxr = gsum * ri * mult
            small_ref[7:8, cs] += colsum((-C_RG) * ra * dlog_a)
            dpa = (((-C_RG) * sp) * dlog_a * ra * (1.0 - ra))
            dpx = dri * ri * (1.0 - ri)
            small_ref[5:6, cs] += colsum(dpa)
            small_ref[6:7, cs] += colsum(dpx)
            dpa = dpa.astype(BF16)
            dpx = dpx.astype(BF16)
            xrb = xr.astype(BF16)
            dwa_acc[g] += _dot_tn(xrb, dpa)
            dwx_acc[g] += _dot_tn(xrb, dpx)
            dxr = dxr + _dot_nt(dpa, wa_ref[g]) + _dot_nt(dpx, wx_ref[g])
            small_ref[4:5, cs] += colsum(dxr)
            xh = jnp.where(first_t, 0.0, xh_ref[:, cs])
            taps = _conv_taps(jnp.concatenate([xh, x_ref[:, cs]], axis=0))
            dxr_ext = jnp.concatenate([dxr, dxr_carry[:, cs]], axis=0)
            dx = vec[V_CONV_W + 3:V_CONV_W + 4] * dxr
            for j in range(4):
                small_ref[j:j + 1, cs] += colsum(dxr * taps[j])
                if j < 3:
                    dx = dx + vec[V_CONV_W + j:V_CONV_W + j + 1] * _shift_up(dxr_ext, 3 - j)[:tm]
            dxr_carry[:, cs] = dxr[0:8, :]
            dproj_ref[:, cs] = dx.astype(BF16)
            pg = p_ref[:, cs]
            dpooled = dpooled_ref[:, cs]
            pb = _dot(pg, wp_ref[g]) + vec[V_B_POOL:V_B_POOL + 1]
            small_ref[9:10, cs] += colsum(dpooled * pb)
            dpb = dpooled * vec[V_POOL_SCALE:V_POOL_SCALE + 1]
            small_ref[8:9, cs] += colsum(dpb)
            dpbb = dpb.astype(BF16)
            dwp_acc[g] += _dot_tn(pg, dpbb)
            dp = _dot_nt(dpbb, wp_ref[g])
            q = _window_mean(dp, POOL_WINDOWS[g], first_t, head_t)
            sm = jnp.concatenate([q, q_carry[:, cs]], axis=0)
            k = 1
            while k < POOL_WINDOWS[g]:
                sm = sm + _shift_up(sm, k)
                k *= 2
            q_carry[:, cs] = q[0:HALO_U, :]
            dproj_ref[:, 2 * D + g * GW:2 * D + (g + 1) * GW] = (sm[:tm] - dp).astype(BF16)
        dproj_ref[:, 3 * D:] = dg_ref[...]

        @pl.when(i == nb - 1)
        def _():
            dwa_ref[...] = dwa_acc[...].astype(BF16)
            dwx_ref[...] = dwx_acc[...].astype(BF16)
            dwp_ref[...] = dwp_acc[...].astype(BF16)

    rev = lambda i: nb - 1 - i
    tok = pl.BlockSpec((tm, D), lambda i: (rev(i), 0))
    halo8 = lambda k: pl.BlockSpec((8, D), lambda i: (jnp.maximum(rev(i) * (tm // 8) - 1, 0), k))
    wspec = pl.BlockSpec((N_GROUPS, GW, GW), lambda i: (0, 0, 0))
    wshape = jax.ShapeDtypeStruct((N_GROUPS, GW, GW), BF16)
    return pl.pallas_call(
        body, name="mix_bwd", grid=(nb,),
        in_specs=[tok, tok, halo8(0), tok, tok, tok, tok, halo8(0), tok, tok, tok, tok, tok, tok,
                  pl.BlockSpec((tm, 2 * D), lambda i: (rev(i), 0)),
                  pl.BlockSpec((16, D), lambda i: (0, 0)), wspec, wspec, wspec, pl.BlockSpec(memory_space=pl.ANY)],
        out_specs=[pl.BlockSpec((tm, D_IN), lambda i: (rev(i), 0)), wspec, wspec, wspec,
                   pl.BlockSpec((16, D), lambda i: (0, 0))],
        out_shape=[jax.ShapeDtypeStruct((s, D_IN), BF16), wshape, wshape, wshape,
                   jax.ShapeDtypeStruct((16, D), F32)],
        scratch_shapes=[pltpu.VMEM((8, D), F32), pltpu.VMEM((8, D), F32), pltpu.VMEM((HALO_U, D), F32),
                        pltpu.VMEM((3, tm, LANES), F32)] + [pltpu.VMEM((N_GROUPS, GW, GW), F32)] * 3,
        compiler_params=_params(("arbitrary",)),
    )(dza, dpooled, x_rnn, x_rnn, ga, dga, xr, hr, hr, p, *gates, dgates, vecs, w_rg_a, w_rg_x, w_pool, dep)


def _proj_bwd(dproj, x, dx2, modr, vecs, w_in, dep):
    s = x.shape[0]
    tm = min(TM_PROJ, s)

    def body(dp_ref, x_ref, dx2_ref, mod_ref, vec_ref, w_ref, dep_ref, gx_ref, small_ref):
        @pl.when(pl.program_id(0) == 0)
        def _():
            small_ref[...] = jnp.zeros_like(small_ref)

        dh1 = None
        for c in range(D_IN // D):
            cs = slice(c * D, (c + 1) * D)
            part = _dot_nt(dp_ref[:, cs], w_ref[:, cs])
            dh1 = part if dh1 is None else dh1 + part
        xv = x_ref[...]
        r1 = lax.rsqrt(jnp.mean(xv * xv, axis=-1, keepdims=True) + EPS)
        xn1 = xv * r1
        gain = vec_ref[V_G1:V_G1 + 1, :] * (1.0 + mod_ref[M_SC1:M_SC1 + 1, :])
        dxn1 = dh1 * gain
        gx_ref[...] = dx2_ref[...] + r1 * (dxn1 - xn1 * jnp.mean(dxn1 * xn1, axis=-1, keepdims=True))
        small_ref[0:1, :] += jnp.sum(dh1, axis=0, keepdims=True)
        small_ref[1:2, :] += jnp.sum(dh1 * xn1, axis=0, keepdims=True)

    tok = pl.BlockSpec((tm, D), lambda i: (i, 0))
    return pl.pallas_call(
        body, name="proj_bwd", grid=(s // tm,),
        in_specs=[pl.BlockSpec((tm, D_IN), lambda i: (i, 0)), tok, tok,
                  pl.BlockSpec((8, D), lambda i: (0, 0)), pl.BlockSpec((16, D), lambda i: (0, 0)),
                  _resident((D, D_IN)), pl.BlockSpec(memory_space=pl.ANY)],
        out_specs=[tok, pl.BlockSpec((8, D), lambda i: (0, 0))],
        out_shape=[jax.ShapeDtypeStruct((s, D), F32), jax.ShapeDtypeStruct((8, D), F32)],
        compiler_params=_params(("arbitrary",)),
    )(dproj, x, dx2, modr, vecs, w_in, dep)


def _wgrad(a, b, name, square_a=False, dep=None):
    s, ka = a.shape
    n = b.shape[1]
    tka = ka if ka <= 1024 else ka // 2
    tn = n if n <= 1024 else n // 2
    ts = min(TS_WGRAD, s)
    ns = s // ts
    nc = 512
    deps = [] if dep is None else [dep]

    def body(a_ref, b_ref, *refs):
        out_ref, acc_ref = refs[-2:]
        t = pl.program_id(2)

        @pl.when(t == 0)
        def _():
            acc_ref[...] = jnp.zeros_like(acc_ref)

        av = a_ref[...]
        if square_a:
            af = av.astype(F32)
            av = (af * af).astype(BF16)
        for c in range(tn // nc):
            cs = slice(c * nc, (c + 1) * nc)
            acc_ref[:, cs] += _dot_tn(av, b_ref[:, cs])

        @pl.when(t == ns - 1)
        def _():
            out_ref[...] = acc_ref[...].astype(BF16)

    return pl.pallas_call(
        body, name=name, grid=(ka // tka, n // tn, ns),
        in_specs=[pl.BlockSpec((ts, tka), lambda i, j, t: (t, i)),
                  pl.BlockSpec((ts, tn), lambda i, j, t: (t, j))] + [pl.BlockSpec(memory_space=pl.ANY)] * len(deps),
        out_specs=pl.BlockSpec((tka, tn), lambda i, j, t: (i, j)),
        out_shape=jax.ShapeDtypeStruct((ka, n), BF16),
        scratch_shapes=[pltpu.VMEM((tka, tn), F32)],
        compiler_params=_params(("parallel", "parallel", "arbitrary")),
    )(a, b, *deps)


def _wgrad_square(pairs, name):
    n_pairs = len(pairs)
    s = pairs[0][0].shape[0]
    ts = min(TS_WGRAD, s)
    ns = s // ts
    nc = 512

    def body(*refs):
        in_refs, out_refs, acc_ref = refs[:2 * n_pairs], refs[2 * n_pairs:3 * n_pairs], refs[-1]
        g = pl.program_id(0)
        for k in range(n_pairs):
            @pl.when(g == k * ns)
            def _():
                acc_ref[...] = jnp.zeros_like(acc_ref)

            @pl.when(g // ns == k)
            def _(k=k):
                av = in_refs[2 * k][...]
                for c in range(D // nc):
                    cs = slice(c * nc, (c + 1) * nc)
                    acc_ref[:, cs] += _dot_tn(av, in_refs[2 * k + 1][:, cs])

            @pl.when(g == k * ns + ns - 1)
            def _(k=k):
                out_refs[k][...] = acc_ref[...].astype(BF16)

    def token_block(k):
        return pl.BlockSpec((ts, D), lambda g: (jnp.clip(g - k * ns, 0, ns - 1), 0))

    return pl.pallas_call(
        body, name=name, grid=(n_pairs * ns,),
        in_specs=[token_block(k) for k in range(n_pairs) for _ in range(2)],
        out_specs=[pl.BlockSpec((D, D), lambda g: (0, 0))] * n_pairs,
        out_shape=[jax.ShapeDtypeStruct((D, D), BF16)] * n_pairs,
        scratch_shapes=[pltpu.VMEM((D, D), F32)],
        compiler_params=_params(("arbitrary",)),
    )(*[x for pair in pairs for x in pair])


def _window(ref, kind, idx, size):
    start = pl.multiple_of(idx * size, size)
    if kind == 0:
        return ref.at[pl.ds(start, size)]
    if kind == 1:
        return ref.at[:, pl.ds(start, size)]
    return ref.at[:, :, pl.ds(start, size)]


def _mesh_place():
    x, y, c = lax.axis_index("x"), lax.axis_index("y"), lax.axis_index("c")
    return x, y, c, 4 * x + 2 * y + c


def _peer(x, y, c, q):
    px = 1 - x if q & 4 else x
    py = 1 - y if q & 2 else y
    pc = 1 - c if q & 1 else c
    return (px, py, pc), 4 * px + 2 * py + pc


_HBM = pl.BlockSpec(memory_space=pltpu.HBM)
_SEM = pl.BlockSpec(memory_space=pltpu.SEMAPHORE)
_EFFECT = pltpu.SideEffectType.DATAFLOW_SIDE_EFFECTING


N_NEAR = 4


def _near(x, y, c):
    out = [((x, y, 1 - c), 4 * x + 2 * y + 1 - c)]
    for j in (1, 2, 3):
        px = 1 - x if j & 2 else x
        py = 1 - y if j & 1 else y
        out.append(((px, py, c), 4 * px + 2 * py + c))
    return out


def _remote(src, dst, send_sems, recv_sems, slot, device):
    return pltpu.make_async_remote_copy(src_ref=src, dst_ref=dst, send_sem=send_sems.at[slot], recv_sem=recv_sems.at[slot],
                                        device_id=device, device_id_type=MESH)


def _split_call(name, arrays, sems_in, n_new_sems, after, emit):
    na, ns, nn = len(arrays), len(sems_in), len(n_new_sems)

    def body(*refs):
        emit(refs[:na], refs[na:na + ns], refs[na + ns + 1:na + ns + 1 + nn])
        refs[-1][...] = jnp.zeros_like(refs[-1])

    outs = pl.pallas_call(
        body, name=name,
        out_shape=(*[pltpu.SemaphoreType.DMA((m,)) for m in n_new_sems],
                   *[pltpu.HBM(a.shape, a.dtype) for a in arrays], jax.ShapeDtypeStruct((8, 128), F32)),
        in_specs=[_HBM] * na + [_SEM] * ns + [pl.BlockSpec(memory_space=pl.ANY)],
        out_specs=(*[_SEM] * nn, *[_HBM] * na, pl.BlockSpec(memory_space=pltpu.VMEM)),
        input_output_aliases={i: nn + i for i in range(na)},
        compiler_params=pltpu.CompilerParams(has_side_effects=_EFFECT),
    )(*[pltpu.with_memory_space_constraint(a, pltpu.HBM) for a in arrays], *sems_in, after)
    return list(outs[:nn]), list(outs[nn:nn + na]), outs[-1]


def _together(name, steps, after):
    parts = [(ex.arrays, ex.sems, ex.new_sems[step], getattr(ex, "emit_" + step)) for ex, step in steps]

    def emit(arr, old, new):
        ia = io = ib = 0
        for arrays, sems, new_sems, emit_one in parts:
            emit_one(arr[ia:ia + len(arrays)], old[io:io + len(sems)], new[ib:ib + len(new_sems)])
            ia, io, ib = ia + len(arrays), io + len(sems), ib + len(new_sems)

    new, arrays, token = _split_call(name, [a for p in parts for a in p[0]], [s for p in parts for s in p[1]],
                                     [m for p in parts for m in p[2]], after, emit)
    out = []
    ia = ib = 0
    for (ex, _), (arrs, sems, new_sems, _) in zip(steps, parts):
        ex.arrays, ex.sems, ex.token = arrays[ia:ia + len(arrs)], [*sems, *new[ib:ib + len(new_sems)]], token
        ia, ib = ia + len(arrs), ib + len(new_sems)
        out.append(ex.arrays[ex.n:])
    return out


class _AllGather:
    def __init__(self, shards, kinds, name):
        self.n, self.kinds, self.name = len(shards), kinds, name
        self.sizes = [s.shape[k] for s, k in zip(shards, kinds)]
        lands = []
        for s, k in zip(shards, kinds):
            dims = list(s.shape)
            dims[k] *= N_DEV
            lands.append(lax.empty(tuple(dims), s.dtype))
        self.arrays, self.sems = [*shards, *lands], []

    def window(self, arr, k, idx):
        return _window(arr[self.n + k], self.kinds[k], idx, self.sizes[k])

    def start(self, after):
        _together(self.name + "_start", [(self, "start")], after)

    def forward(self, after):
        _together(self.name + "_forward", [(self, "forward")], after)

    def finish(self, after):
        return _together(self.name + "_finish", [(self, "finish")], after)[0]


class _Gather(_AllGather):
    def __init__(self, shards, kinds, name):
        super().__init__(shards, kinds, name)
        n = self.n
        self.new_sems = dict(start=[n * N_NEAR, n * N_NEAR, n], forward=[n * N_NEAR] * 2, finish=[])

    def emit_start(self, arr, _, new):
        x, y, c, me = _mesh_place()
        for k in range(self.n):
            pltpu.make_async_copy(arr[k], self.window(arr, k, me), new[2].at[k]).start()
        for k in range(self.n):
            for j, (dev, _) in enumerate(_near(x, y, c)):
                _remote(arr[k], self.window(arr, k, me), new[0], new[1], k * N_NEAR + j, dev).start()

    def emit_forward(self, arr, old, new):
        x, y, c, _ = _mesh_place()
        near = _near(x, y, c)
        for k in range(self.n):
            for j in (1, 2, 3):
                dev, idx = near[j]
                landed = self.window(arr, k, idx)
                _remote(arr[k], landed, old[0], old[1], k * N_NEAR + j, dev).wait_recv()
                _remote(landed, landed, new[0], new[1], k * N_NEAR + j, near[0][0]).start()

    def emit_finish(self, arr, old, _):
        x, y, c, me = _mesh_place()
        near = _near(x, y, c)
        other_core = near[0][0]
        for k in range(self.n):
            win = lambda idx: self.window(arr, k, idx)
            pltpu.make_async_copy(arr[k], win(me), old[2].at[k]).wait()
            for j, (dev, idx) in enumerate(near):
                _remote(arr[k], win(me), old[0], old[1], k * N_NEAR + j, dev).wait_send()
            _remote(arr[k], win(near[0][1]), old[0], old[1], k * N_NEAR, other_core).wait_recv()
            for j in (1, 2, 3):
                idx = near[j][1]
                _remote(win(idx), win(idx), old[3], old[4], k * N_NEAR + j, other_core).wait_send()
                _remote(arr[k], win(idx + 1 - 2 * c), old[3], old[4], k * N_NEAR + j, other_core).wait_recv()


class _Spread(_AllGather):
    def __init__(self, shards, kinds, name):
        super().__init__(shards, kinds, name)
        n = self.n
        self.new_sems = dict(start=[n * N_DEV, n * N_DEV, n], finish=[])

    def emit_start(self, arr, _, new):
        x, y, c, me = _mesh_place()
        for k in range(self.n):
            mine = self.window(arr, k, me)
            pltpu.make_async_copy(arr[k], mine, new[2].at[k]).start()
            for q in range(1, N_DEV):
                _remote(arr[k], mine, new[0], new[1], k * N_DEV + q, _peer(x, y, c, q)[0]).start()

    def emit_finish(self, arr, old, _):
        x, y, c, me = _mesh_place()
        for k in range(self.n):
            win = lambda idx: self.window(arr, k, idx)
            pltpu.make_async_copy(arr[k], win(me), old[2].at[k]).wait()
            for q in range(1, N_DEV):
                peer, peer_idx = _peer(x, y, c, q)
                _remote(arr[k], win(me), old[0], old[1], k * N_DEV + q, peer).wait_send()
                _remote(arr[k], win(peer_idx), old[0], old[1], k * N_DEV + q, peer).wait_recv()


class _Scatter:
    def __init__(self, partials, kinds, after, name):
        self.n, self.kinds, self.name, self.partials = len(partials), kinds, name, partials
        self.sizes = [p.shape[k] // N_DEV for p, k in zip(partials, kinds)]
        n, sizes = self.n, self.sizes
        self.slot_shapes = []
        for p, k, size in zip(partials, kinds, sizes):
            dims = list(p.shape)
            dims[k] = size
            self.slot_shapes.append((N_NEAR, *dims))
        slots = [lax.empty(sh, p.dtype) for sh, p in zip(self.slot_shapes, partials)]

        def emit(arr, _, new):
            x, y, c, _ = _mesh_place()
            near = _near(x, y, c)
            for k in range(n):
                for j in range(N_NEAR):
                    owner = near[j][1] if j == 0 else near[j][1] + 1 - 2 * c
                    _remote(_window(arr[k], kinds[k], owner, sizes[k]), arr[n + k].at[j], new[0], new[1],
                            k * N_NEAR + j, near[0][0]).start()

        self.sems, self.arrays, self.token = _split_call(name + "_start", [*partials, *slots], [], [n * N_NEAR] * 2,
                                                         after, emit)

    def combine_and_send(self, own4, after):
        n, kinds, sizes = self.n, self.kinds, self.sizes

        def emit_wait(arr, old, _):
            x, y, c, _ = _mesh_place()
            near = _near(x, y, c)
            for k in range(n):
                for j in range(N_NEAR):
                    owner = near[j][1] if j == 0 else near[j][1] + 1 - 2 * c
                    cp = _remote(_window(arr[k], kinds[k], owner, sizes[k]), arr[n + k].at[j], old[0], old[1],
                                 k * N_NEAR + j, near[0][0])
                    cp.wait_send()
                    cp.wait_recv()

        _, arrays, _ = _split_call(self.name + "_landed", self.arrays, self.sems, [], after, emit_wait)
        chip_sums = _chip_sums(arrays[:n], arrays[n:], kinds, sizes, own4, self.name + "_combine")
        arrivals = [lax.empty((N_NEAR - 1, *sh[1:]), p.dtype) for sh, p in zip(self.slot_shapes, self.partials)]

        def emit_send(arr, _, new):
            x, y, c, _ = _mesh_place()
            near = _near(x, y, c)
            for k in range(n):
                for j in (1, 2, 3):
                    _remote(arr[k].at[j], arr[n + k].at[j - 1], new[0], new[1], k * N_NEAR + j, near[j][0]).start()

        self.sems, self.arrays, self.token = _split_call(self.name + "_send", [*chip_sums, *arrivals], [],
                                                         [n * N_NEAR] * 2, own4, emit_send)

    def finish(self, after):
        n = self.n

        def emit(arr, old, _):
            x, y, c, _ = _mesh_place()
            near = _near(x, y, c)
            for k in range(n):
                for j in (1, 2, 3):
                    cp = _remote(arr[k].at[j], arr[n + k].at[j - 1], old[0], old[1], k * N_NEAR + j, near[j][0])
                    cp.wait_send()
                    cp.wait_recv()

        _, arrays, _ = _split_call(self.name + "_finish", self.arrays, self.sems, [], after, emit)
        return arrays[:n], arrays[n:]


def _chip_sums(partials, slots, kinds, sizes, own4, name):
    n = len(partials)

    def body(own_ref, *refs):
        for k in range(n):
            refs[2 * n + k][...] = (refs[k][...].astype(F32) + refs[n + k][...].astype(F32)).astype(BF16)

    in_specs, slot_specs = [], []
    for p, s, kind, size in zip(partials, slots, kinds, sizes):
        block = list(p.shape)
        block[kind] = size
        nd = len(block)
        in_specs.append(pl.BlockSpec(tuple(block), functools.partial(
            lambda j, own, kind, nd: tuple(own[j] if d == kind else 0 for d in range(nd)), kind=kind, nd=nd)))
        slot_specs.append(pl.BlockSpec((None, *block), functools.partial(
            lambda j, own, nd: (j,) + (0,) * nd, nd=nd)))
    return pl.pallas_call(
        body, name=name,
        grid_spec=pltpu.PrefetchScalarGridSpec(num_scalar_prefetch=1, grid=(N_NEAR,),
                                               in_specs=in_specs + slot_specs, out_specs=slot_specs),
        out_shape=[jax.ShapeDtypeStruct(s.shape, s.dtype) for s in slots],
        compiler_params=_params(("arbitrary",)),
    )(own4, *partials, *slots)


def _to_bf16(arrays, name, dep=None):
    n = len(arrays)
    deps = [] if dep is None else [dep]

    def body(*refs):
        for src, dst in zip(refs[:n], refs[n + len(deps):]):
            dst[...] = src[...].astype(BF16)

    vmem = pl.BlockSpec(memory_space=pltpu.VMEM)
    return pl.pallas_call(body, name=name, out_shape=[jax.ShapeDtypeStruct(a.shape, BF16) for a in arrays],
                          in_specs=[vmem] * n + [pl.BlockSpec(memory_space=pl.ANY)] * len(deps), out_specs=[vmem] * n,
                          compiler_params=pltpu.CompilerParams(vmem_limit_bytes=V7X_VMEM_LIMIT))(*arrays, *deps)


def _silu(c):
    return c * _sigmoid_tail(c)


def _ada_fwd(c_all, w_ada, b_ada_cols, dep):
    def body(c_ref, w_ref, b_ref, dep_ref, out_ref):
        out_ref[...] = jnp.dot(_silu(c_ref[...]), w_ref[...], preferred_element_type=F32,
                               precision=lax.Precision.HIGHEST) + b_ref[...]

    vmem = pl.BlockSpec(memory_space=pltpu.VMEM)
    return pl.pallas_call(
        body, name="ada_fwd", in_specs=[vmem, vmem, vmem, pl.BlockSpec(memory_space=pl.ANY)], out_specs=vmem,
        out_shape=jax.ShapeDtypeStruct((N_DEV, w_ada.shape[1]), F32),
    )(c_all, w_ada, b_ada_cols, dep)


def _adam(w, g, m, v):
    m = ADAM_B1 * m + (1.0 - ADAM_B1) * g
    v = ADAM_B2 * v + (1.0 - ADAM_B2) * (g * g)
    m_hat = m / (1.0 - ADAM_B1 ** ADAM_STEP)
    v_hat = v / (1.0 - ADAM_B2 ** ADAM_STEP)
    delta = -ADAM_LR * (m_hat / (jnp.sqrt(v_hat) + ADAM_EPS) + ADAM_WD * w)
    return delta, m, v


def _ada_bwd_adam(c_all, dmod_cols, w, m, v):
    rows, cols = w.shape
    n_chunks, n_buf = 8, 3
    cr = rows // n_chunks

    def body(c_ref, d_ref, w_hbm, m_hbm, v_hbm, g_hbm, delta_hbm, nm_hbm, nv_hbm, in_buf, out_buf, in_sem, out_sem):
        def fetch(i):
            return [pltpu.make_async_copy(src.at[pl.ds(i * cr, cr)], in_buf.at[i % n_buf, j], in_sem.at[i % n_buf, j])
                    for j, src in enumerate((w_hbm, m_hbm, v_hbm))]

        def put(i):
            return [pltpu.make_async_copy(out_buf.at[i % n_buf, j], dst.at[pl.ds(i * cr, cr)], out_sem.at[i % n_buf, j])
                    for j, dst in enumerate((g_hbm, delta_hbm, nm_hbm, nv_hbm))]

        for i in range(n_buf):
            for cp in fetch(i):
                cp.start()
        sc = _silu(c_ref[...])
        for i in range(n_chunks):
            s = i % n_buf
            for cp in fetch(i):
                cp.wait()
            if i >= n_buf:
                for cp in put(i - n_buf):
                    cp.wait()
            g = lax.dot_general(sc[:, i * cr:(i + 1) * cr], d_ref[...], (((0,), (0,)), ((), ())),
                                preferred_element_type=F32, precision=lax.Precision.HIGHEST)
            delta, nm, nv = _adam(in_buf[s, 0], g, in_buf[s, 1], in_buf[s, 2])
            for j, val in enumerate((g, delta, nm, nv)):
                out_buf[s, j] = val
            for cp in put(i):
                cp.start()
            if i + n_buf < n_chunks:
                for cp in fetch(i + n_buf):
                    cp.start()
        for i in range(n_chunks - n_buf, n_chunks):
            for cp in put(i):
                cp.wait()

    sd = jax.ShapeDtypeStruct(w.shape, F32)
    vmem, hbm = pl.BlockSpec(memory_space=pltpu.VMEM), pl.BlockSpec(memory_space=pl.ANY)
    return pl.pallas_call(
        body, name="ada_bwd_adam", out_shape=[sd] * 4,
        in_specs=[vmem, vmem, hbm, hbm, hbm], out_specs=[hbm] * 4,
        scratch_shapes=[pltpu.VMEM((n_buf, 3, cr, cols), F32), pltpu.VMEM((n_buf, 4, cr, cols), F32),
                        pltpu.SemaphoreType.DMA((n_buf, 3)), pltpu.SemaphoreType.DMA((n_buf, 4))],
        compiler_params=pltpu.CompilerParams(vmem_limit_bytes=V7X_VMEM_LIMIT),
    )(c_all, dmod_cols, w, m, v)


def _adam_group(chip_sums, arrivals, ws, ms, vs, n_tiles, name):
    n = len(ws)

    def body(*refs):
        for k in range(n):
            c_ref, a_ref, w_ref, m_ref, v_ref = (refs[j * n + k] for j in range(5))
            g_ref, delta_ref, nm_ref, nv_ref = (refs[(5 + j) * n + k] for j in range(4))
            g = c_ref[...].astype(F32)
            for j in range(N_NEAR - 1):
                g = g + a_ref[j].astype(F32)
            g_ref[...] = g
            delta_ref[...], nm_ref[...], nv_ref[...] = _adam(w_ref[...], g, m_ref[...], v_ref[...])

    tiles = [(w.shape[0] // n_tiles, w.shape[1]) for w in ws]
    blk = [pl.BlockSpec(t, lambda i: (i, 0)) for t in tiles]
    return pl.pallas_call(
        body, name=name, grid=(n_tiles,),
        in_specs=[pl.BlockSpec((None, *t), lambda i: (0, i, 0)) for t in tiles]
        + [pl.BlockSpec((N_NEAR - 1, *t), lambda i: (0, i, 0)) for t in tiles] + blk * 3,
        out_specs=blk * 4, out_shape=[jax.ShapeDtypeStruct(w.shape, F32) for w in ws] * 4,
        compiler_params=_params(("parallel",)),
    )(*chip_sums, *arrivals, *ws, *ms, *vs)


N_SMALL = 40
SMALL_MIXER_ROW = 16
N_SMALL_PARAMS = 11


def _pack_vecs(conv_w_full, rows):
    def body(cw_ref, *refs):
        out = refs[-1]
        out[...] = jnp.zeros_like(out)
        out[0:4, :] = cw_ref[0:4, :]
        for r, ref in enumerate(refs[:-1]):
            out[4 + r:5 + r, :] = ref[...]

    return pl.pallas_call(body, name="pack_vecs", out_shape=jax.ShapeDtypeStruct((16, D), F32))(conv_w_full, *rows)


def _small_finish(gathered, conv_cols, mod_all, vecs, ws, ms, vs):
    n = N_SMALL_PARAMS

    def body(g_ref, conv_ref, mod_ref, vec_ref, *refs):
        w_refs, m_refs, v_refs = refs[:n], refs[n:2 * n], refs[2 * n:3 * n]
        outs = refs[3 * n:]
        g1 = vec_ref[V_G1:V_G1 + 1, :]
        g2 = vec_ref[V_G2:V_G2 + 1, :]
        zero = jnp.zeros((1, D), F32)
        dg1, dg2, dgf, loss_lanes = zero, zero, zero, zero
        mixer = jnp.zeros((16, D), F32)
        db_ada = jnp.zeros((6, D), F32)
        d_conv_w = jnp.zeros(conv_ref.shape[1:], F32)
        for b in range(N_DEV):
            gb = g_ref[b]
            mod = mod_ref[b]
            q1 = gb[33:34]
            q2 = gb[9:10]
            dmod = jnp.concatenate([gb[32:33], q1 * g1, gb[10:11], gb[8:9], q2 * g2, gb[1:2]], axis=0)
            outs[4 * n][b] = dmod
            db_ada = db_ada + dmod
            dg1 = dg1 + q1 * (1.0 + mod[M_SC1:M_SC1 + 1])
            dg2 = dg2 + q2 * (1.0 + mod[M_SC2:M_SC2 + 1])
            dgf = dgf + gb[0:1]
            loss_lanes = loss_lanes + gb[2:3]
            mixer = mixer + gb[SMALL_MIXER_ROW:SMALL_MIXER_ROW + 16]
            d_conv_w = d_conv_w + conv_ref[b]
        d_a_param = mixer[7:8] * _sigmoid_tail(vec_ref[V_A_PARAM:V_A_PARAM + 1, :])
        grads = [dg1, dg2, mixer[4:5], mixer[5:6], mixer[6:7], d_a_param, mixer[8:9], mixer[9:10], dgf,
                 db_ada, d_conv_w]

        def load(ref, rows):
            if ref.shape[0] == rows:
                return ref[...]
            return jnp.concatenate([ref[:, j * D:(j + 1) * D] for j in range(rows)], axis=0)

        def store(ref, val):
            if ref.shape == val.shape:
                ref[...] = val
            else:
                for j in range(val.shape[0]):
                    ref[:, j * D:(j + 1) * D] = val[j:j + 1]

        for k in range(n):
            rows = grads[k].shape[0]
            results = (grads[k], *_adam(load(w_refs[k], rows), grads[k], load(m_refs[k], rows), load(v_refs[k], rows)))
            for which, val in enumerate(results):
                store(outs[which * n + k], val)
        outs[4 * n + 1][...] = jnp.broadcast_to(jnp.sum(loss_lanes, axis=1, keepdims=True), (8, 128))

    shapes = [jax.ShapeDtypeStruct(w.shape, F32) for w in ws]
    return pl.pallas_call(
        body, name="small_finish",
        out_shape=shapes * 4 + [jax.ShapeDtypeStruct((N_DEV, 6, D), F32), jax.ShapeDtypeStruct((8, 128), F32)],
    )(gathered, conv_cols, mod_all, vecs, *ws, *ms, *vs)


def _pad_rows(a, rows):
    return jnp.pad(a, ((0, rows - a.shape[0]), (0, 0)))


def kernel(x, c, norm_mix_g, norm_mlp_g, w_ada, b_ada, w_in, conv_w, conv_b, w_rg_a, b_rg_a, w_rg_x, b_rg_x, a_param, w_branch_a, w_pool, b_pool, pool_scale, w_branch_b, w_out, w_up, w_down, final_g, loss_target, m_norm_mix_g, m_norm_mlp_g, m_w_ada, m_b_ada, m_w_in, m_conv_w, m_conv_b, m_w_rg_a, m_b_rg_a, m_w_rg_x, m_b_rg_x, m_a_param, m_w_branch_a, m_w_pool, m_b_pool, m_pool_scale, m_w_branch_b, m_w_out, m_w_up, m_w_down, m_final_g, v_norm_mix_g, v_norm_mlp_g, v_w_ada, v_b_ada, v_w_in, v_conv_w, v_conv_b, v_w_rg_a, v_b_rg_a, v_w_rg_x, v_b_rg_x, v_a_param, v_w_branch_a, v_w_pool, v_b_pool, v_pool_scale, v_w_branch_b, v_w_out, v_w_up, v_w_down, v_final_g):
    me = 4 * lax.axis_index("x") + 2 * lax.axis_index("y") + lax.axis_index("c")
    s = x.shape[1]
    x2d = x.reshape(s, D)
    target = loss_target.reshape(s, D)
    n_ada = w_ada.shape[2]

    b_ada_cols = lax.dynamic_slice(b_ada, (0, me * n_ada), (1, n_ada))

    sharded = dict(w_in=(w_in[0], 1), w_up=(w_up[0], 1), w_down=(w_down[0], 0), w_branch_a=(w_branch_a[0], 0),
                   w_branch_b=(w_branch_b[0], 0), w_out=(w_out[0], 0), w_rg_a=(w_rg_a[0], 1), w_rg_x=(w_rg_x[0], 1),
                   w_pool=(w_pool[0], 1))
    kind = {k: v[1] for k, v in sharded.items()}
    first_names = ["w_in"]
    later_names = [k for k in sharded if k not in first_names]
    mix_names = ["w_rg_a", "w_rg_x", "w_pool"]
    branch_names = ["w_branch_a", "w_branch_b", "w_out"]
    mlp_names = ["w_up", "w_down"]

    def gather(group, after, name):
        exchange = _Gather([shard[k] for k in group], [kind[k] for k in group], name)
        exchange.start(after)
        return exchange

    shard = dict(zip(first_names, _to_bf16([sharded[k][0] for k in first_names], "to_bf16_first")))
    spread_c = _Spread([c, conv_w[0]], [0, 1], "spread_c")
    g_first = _Gather([shard[k] for k in first_names], [kind[k] for k in first_names], "gather_first")
    _together("first_start", [(spread_c, "start"), (g_first, "start")], c)
    shard.update(zip(later_names, _to_bf16([sharded[k][0] for k in later_names], "to_bf16_later", dep=g_first.token)))

    c_all, conv_w_full = spread_c.finish(g_first.token)
    mod_part = _ada_fwd(c_all, w_ada[0], b_ada_cols, g_first.token)
    vecs = _pack_vecs(conv_w_full, [conv_b, b_rg_a, b_rg_x, a_param, b_pool, pool_scale,
                                    norm_mix_g, norm_mlp_g, final_g.reshape(1, D)])
    spread_mod = _Spread([mod_part], [0], "spread_mod")
    spread_mod.start(vecs)
    g_mix = gather(mix_names, spread_mod.token, "gather_mix")
    g_branch = gather(branch_names, g_mix.token, "gather_branch")
    g_mlp = gather(mlp_names, g_branch.token, "gather_mlp")
    g_first.forward(g_mlp.token)
    wg = dict(zip(first_names, g_first.finish(g_first.token)))
    mod_parts, = spread_mod.finish(g_first.token)
    mod_all = jnp.transpose(mod_parts.reshape(N_DEV, N_DEV, n_ada), (1, 0, 2)).reshape(N_DEV, 6, D)
    modr = _pad_rows(lax.dynamic_index_in_dim(mod_all, me, 0, keepdims=False), 8)

    h1, x_rnn, u_pool, ga, dga, sa, sb = _proj_fwd(x2d, modr, vecs, wg["w_in"])
    _together("mixer_forward", [(g_mix, "forward"), (g_branch, "forward")], h1)
    wg.update(zip(mix_names, g_mix.finish(g_branch.token)))
    xr, hr, za, p, pooled, *gates = _mix_fwd(x_rnn, u_pool, ga, vecs, wg["w_rg_a"], wg["w_rg_x"], wg["w_pool"],
                                             dep=g_branch.token)
    g_mlp.forward(za)
    wg.update(zip(branch_names, g_branch.finish(g_mlp.token)))
    ba, bb, merged, o, x2, h2 = _branch_fwd(za, pooled, sa, sb, x2d, modr, vecs,
                                            wg["w_branch_a"], wg["w_branch_b"], wg["w_out"])
    wg.update(zip(mlp_names, g_mlp.finish(h2)))
    ru, dx3, d_dn, small_f = _mlp_fwd(h2, x2, target, modr, vecs, wg["w_up"], wg["w_down"])

    near = _near(lax.axis_index("x"), lax.axis_index("y"), lax.axis_index("c"))
    own4 = jnp.stack([me, near[1][1], near[2][1], near[3][1]]).astype(jnp.int32)

    def scatter(group, partial, after, name):
        return _Scatter([partial[k] for k in group], [kind[k] for k in group], after, name)

    dup, dx2, do, small_m = _mlp_bwd(d_dn, ru, x2, dx3, o, modr, vecs, wg["w_up"], wg["w_down"])
    partial = dict(w_up=_wgrad(h2, dup, "wgrad_up"), w_down=_wgrad(ru, d_dn, "wgrad_down", square_a=True))
    s_mlp = scatter(mlp_names, partial, dx2, "scatter_mlp")

    dba, dbb, dgates, dza, dpooled = _branch_bwd(do, sa, sb, ba, bb, wg["w_branch_a"], wg["w_branch_b"], wg["w_out"],
                                                 dep=s_mlp.token)
    s_mlp.combine_and_send(own4, dza)
    dproj, dw_rg_a, dw_rg_x, dw_pool, small_x = _mix_bwd(dza, dpooled, x_rnn, ga, dga, xr, hr, p, gates, dgates,
                                                         vecs, wg["w_rg_a"], wg["w_rg_x"], wg["w_pool"],
                                                         dep=s_mlp.token)
    partial.update(zip(["w_out", "w_branch_b", "w_branch_a"],
                       _wgrad_square([(merged, do), (pooled, dbb), (za, dba)], "wgrad_square")),
                   w_rg_a=dw_rg_a, w_rg_x=dw_rg_x, w_pool=dw_pool)
    mixer_names = ["w_rg_a", "w_rg_x", "w_pool", "w_branch_a", "w_branch_b", "w_out"]
    s_mixer = scatter(mixer_names, partial, s_mlp.token, "scatter_mixer")

    partial["w_in"] = _wgrad(h1, dproj, "wgrad_in", dep=s_mixer.token)
    s_in = scatter(["w_in"], partial, s_mixer.token, "scatter_in")
    s_mixer.combine_and_send(own4, s_in.token)
    s_in.combine_and_send(own4, s_mixer.token)
    grad_x, small_p = _proj_bwd(dproj, x2d, dx2, modr, vecs, wg["w_in"], dep=s_in.token)

    locals_ = dict(w_in=(w_in, m_w_in, v_w_in), w_up=(w_up, m_w_up, v_w_up), w_down=(w_down, m_w_down, v_w_down),
                   w_branch_a=(w_branch_a, m_w_branch_a, v_w_branch_a),
                   w_branch_b=(w_branch_b, m_w_branch_b, v_w_branch_b), w_out=(w_out, m_w_out, v_w_out),
                   w_rg_a=(w_rg_a, m_w_rg_a, v_w_rg_a), w_rg_x=(w_rg_x, m_w_rg_x, v_w_rg_x),
                   w_pool=(w_pool, m_w_pool, v_w_pool))
    res = {}

    def finish(group, exchange, after, n_tiles, name):
        chip_sums, arrivals = exchange.finish(after)
        flat = lambda t: t.reshape(-1, t.shape[-1])
        shapes = [flat(locals_[k][0]).shape for k in group]
        outs = _adam_group([cs.reshape(N_NEAR, *sh) for cs, sh in zip(chip_sums, shapes)],
                           [ar.reshape(N_NEAR - 1, *sh) for ar, sh in zip(arrivals, shapes)],
                           *[[flat(locals_[k][j]) for k in group] for j in range(3)], n_tiles, name)
        for i, k in enumerate(group):
            res[k] = [outs[j * len(group) + i].reshape(locals_[k][0].shape) for j in range(4)]
        return res[group[-1]][0]

    small = jnp.concatenate([small_f, small_m, small_x, small_p], axis=0)
    g_small = _Spread([small], [0], "spread_small")
    g_small.start(grad_x)
    done = finish(mlp_names, s_mlp, g_small.token, 4, "adam_mlp")
    done = finish(mixer_names, s_mixer, done, 2, "adam_mixer")
    done = finish(["w_in"], s_in, done, 4, "adam_in")
    small_all, = g_small.finish(done)
    small_all = small_all.reshape(N_DEV, N_SMALL, D)

    conv_cols = lax.dynamic_slice(small_all, (0, SMALL_MIXER_ROW, me * (D // N_DEV)), (N_DEV, 4, D // N_DEV))

    def smalls(ng, nl, cb, bra, brx, ap, bp, ps, fg, ba_, cw):
        return [ng, nl, cb, bra, brx, ap, bp, ps, fg.reshape(1, D), ba_, cw[0]]

    small_names = ["norm_mix_g", "norm_mlp_g", "conv_b", "b_rg_a", "b_rg_x", "a_param", "b_pool", "pool_scale",
                   "final_g", "b_ada", "conv_w"]
    fin = _small_finish(
        small_all, conv_cols, mod_all, vecs,
        smalls(norm_mix_g, norm_mlp_g, conv_b, b_rg_a, b_rg_x, a_param, b_pool, pool_scale, final_g, b_ada, conv_w),
        smalls(m_norm_mix_g, m_norm_mlp_g, m_conv_b, m_b_rg_a, m_b_rg_x, m_a_param, m_b_pool, m_pool_scale,
               m_final_g, m_b_ada, m_conv_w),
        smalls(v_norm_mix_g, v_norm_mlp_g, v_conv_b, v_b_rg_a, v_b_rg_x, v_a_param, v_b_pool, v_pool_scale,
               v_final_g, v_b_ada, v_conv_w))
    dmod_all, loss_tile = fin[4 * N_SMALL_PARAMS], fin[4 * N_SMALL_PARAMS + 1]
    dmod_cols = lax.dynamic_slice(dmod_all.reshape(N_DEV, 6 * D), (0, me * n_ada), (N_DEV, n_ada))
    res["w_ada"] = [t.reshape(w_ada.shape) for t in _ada_bwd_adam(c_all, dmod_cols, w_ada[0], m_w_ada[0], v_w_ada[0])]

    def final_shape(k, t):
        if k == "final_g":
            return t.reshape(D)
        if k == "conv_w":
            return t.reshape(conv_w.shape)
        return t

    for i, k in enumerate(small_names):
        res[k] = [final_shape(k, fin[which * N_SMALL_PARAMS + i]) for which in range(4)]
    order = ["norm_mix_g", "norm_mlp_g", "w_ada", "b_ada", "w_in", "conv_w", "conv_b", "w_rg_a", "b_rg_a", "w_rg_x",
             "b_rg_x", "a_param", "w_branch_a", "w_pool", "b_pool", "pool_scale", "w_branch_b", "w_out", "w_up",
             "w_down", "final_g"]
    outs = [loss_tile[0, 0], grad_x.reshape(x.shape)]
    for which in range(4):
        for k in order:
            outs.append(res[k][which])
    return tuple(outs)
```
